```python
import jax
import jax.numpy as jnp
from jax import lax
import numpy as np

D_MODEL = 2048
BATCH = 8
SEQ = 2048
DEPTH = 1

N_META = 16
HEAD_DIM = 128
GDN_HEADS = 8
FOX_HEADS = 8
GDN_WIDTH = GDN_HEADS * HEAD_DIM
FOX_WIDTH = FOX_HEADS * HEAD_DIM
MIX_WIDTH = GDN_WIDTH + FOX_WIDTH
CONV_WIDTH = 4
CHUNK = 64
Q_BLOCK = 128
EPS = 1e-6

IN_SPLITS = (GDN_WIDTH, GDN_WIDTH, GDN_WIDTH, GDN_WIDTH, GDN_HEADS, GDN_HEADS,
             FOX_WIDTH, FOX_WIDTH, FOX_WIDTH, FOX_WIDTH, FOX_HEADS)
IN_WIDTH = sum(IN_SPLITS)
SPLIT_POINTS = tuple(int(s) for s in np.cumsum(IN_SPLITS)[:-1])

kernel_name = 'hymba_gdn_fox_sandwich_meta'


def rmsnorm(x, w):
    xf = x.astype(jnp.float32)
    y = xf * lax.rsqrt(jnp.mean(xf * xf, axis=-1, keepdims=True) + EPS)
    return y * w.astype(jnp.float32)


def l2norm(x):
    return x * lax.rsqrt(jnp.sum(x * x, axis=-1, keepdims=True) + EPS)


def causal_depthwise_conv(x, w):
    L = x.shape[1]
    xp = jnp.pad(x, ((0, 0), (CONV_WIDTH - 1, 0), (0, 0)))
    y = xp[:, 0:L, :] * w[:, 0]
    for j in range(1, CONV_WIDTH):
        y = y + xp[:, j:j + L, :] * w[:, j]
    return y


def gdn_chunk_prep(q, k, v, beta, g):
    C = q.shape[-2]
    G = jnp.cumsum(g, axis=-1)
    causal = jnp.tril(jnp.ones((C, C), dtype=bool))
    strict = jnp.tril(jnp.ones((C, C), dtype=bool), -1)
    diff = G[..., :, None] - G[..., None, :]
    D = jnp.where(causal, jnp.exp(jnp.where(causal, diff, 0.0)), 0.0)
    kk = jnp.einsum('bhncd,bhnsd->bhncs', k, k)
    n_mat = jnp.where(strict, beta[..., :, None] * kk * D, 0.0)
    eye = jnp.eye(C, dtype=q.dtype)
    T = lax.linalg.triangular_solve(eye + n_mat, jnp.broadcast_to(eye, n_mat.shape),
                                    left_side=True, lower=True, unit_diagonal=True)
    U = jnp.einsum('bhncs,bhnsd->bhncd', T, beta[..., None] * v)
    W = jnp.einsum('bhncs,bhnsd->bhncd', T, (beta * jnp.exp(G))[..., None] * k)
    a_qk = jnp.where(causal, jnp.einsum('bhncd,bhnsd->bhncs', q, k) * D, 0.0)
    q_dec = q * jnp.exp(G)[..., None]
    k_dec = k * jnp.exp(G[..., -1:] - G)[..., None]
    chunk_decay = jnp.exp(G[..., -1])
    return (q_dec, k_dec, U, W, a_qk, chunk_decay)


def gdn_chunk_step(S, xs):
    q_dec, k_dec, U, W, a_qk, decay = xs
    v_new = U - jnp.einsum('bhcd,bhde->bhce', W, S)
    o = jnp.einsum('bhcd,bhde->bhce', q_dec, S) + jnp.einsum('bhcs,bhse->bhce', a_qk, v_new)
    S = S * decay[..., None, None] + jnp.einsum('bhcd,bhce->bhde', k_dec, v_new)
    return S, o


def gated_delta_rule(q, k, v, beta, g):
    B, H, L, d = q.shape
    n_chunks = (L - N_META) // CHUNK

    def split(t):
        meta = t[:, :, :N_META][:, :, None]
        real = t[:, :, N_META:].reshape((B, H, n_chunks, CHUNK) + t.shape[3:])
        return meta, real

    parts = [split(t) for t in (q, k, v, beta, g)]
    meta_in = gdn_chunk_prep(*[p[0] for p in parts])
    real_in = gdn_chunk_prep(*[p[1] for p in parts])
    S0 = jnp.zeros((B, H, d, d), jnp.float32)
    S, o_meta = gdn_chunk_step(S0, tuple(t[:, :, 0] for t in meta_in))
    real_xs = tuple(jnp.moveaxis(t, 2, 0) for t in real_in)
    _, o_real = lax.scan(gdn_chunk_step, S, real_xs)
    o_real = jnp.moveaxis(o_real, 0, 2).reshape(B, H, L - N_META, d)
    return jnp.concatenate([o_meta, o_real], axis=2)


def forgetting_attention(q, k, v, logf):
    L = q.shape[2]
    scale = HEAD_DIM ** -0.5
    c = jnp.cumsum(logf, axis=-1)
    starts = [0] + [N_META + i * Q_BLOCK for i in range((L - N_META) // Q_BLOCK)]
    ends = [N_META] + [N_META + (i + 1) * Q_BLOCK for i in range((L - N_META) // Q_BLOCK)]
    outs = []
    for s0, s1 in zip(starts, ends):
        qb = q[:, :, s0:s1]
        kb = k[:, :, :s1]
        vb = v[:, :, :s1]
        logits = (jnp.einsum('bhqd,bhkd->bhqk', qb, kb) * scale
                  + (c[:, :, s0:s1, None] - c[:, :, None, :s1]))
        qpos = s0 + jnp.arange(s1 - s0)
        kpos = jnp.arange(s1)
        mask = kpos[None, :] <= qpos[:, None]
        logits = jnp.where(mask, logits, -jnp.inf)
        p = jax.nn.softmax(logits, axis=-1)
        outs.append(jnp.einsum('bhqk,bhkd->bhqd', p, vb))
    return jnp.concatenate(outs, axis=2)


def hybrid_layer(h, pre_w, w_in, conv_w, a_log, dt_bias, gdn_norm_w,
                 fox_q_norm_w, fox_k_norm_w, fox_f_bias, w_out, post_w):
    B, L, _ = h.shape
    f32 = jnp.float32
    xn = rmsnorm(h, pre_w).astype(h.dtype)
    proj = xn @ w_in
    gq, gk, gv, gz, gb, ga, fq, fk, fv, fg, ff = jnp.split(proj, SPLIT_POINTS, axis=-1)

    def heads(t, n):
        return t.reshape(B, L, n, HEAD_DIM).transpose(0, 2, 1, 3)

    qkv = jax.nn.silu(causal_depthwise_conv(jnp.concatenate([gq, gk, gv], axis=-1), conv_w))
    gq, gk, gv = jnp.split(qkv.astype(f32), 3, axis=-1)
    gq = l2norm(heads(gq, GDN_HEADS)) * (HEAD_DIM ** -0.5)
    gk = l2norm(heads(gk, GDN_HEADS))
    gv = heads(gv, GDN_HEADS)
    beta = jax.nn.sigmoid(gb.astype(f32)).transpose(0, 2, 1)
    g = (-jnp.exp(a_log.astype(f32))
         * jax.nn.softplus(ga.astype(f32) + dt_bias.astype(f32))).transpose(0, 2, 1)
    o_gdn = gated_delta_rule(gq, gk, gv, beta, g)
    o_gdn = rmsnorm(o_gdn, gdn_norm_w) * jax.nn.silu(heads(gz.astype(f32), GDN_HEADS))

    fq = rmsnorm(heads(fq, FOX_HEADS), fox_q_norm_w)
    fk = rmsnorm(heads(fk, FOX_HEADS), fox_k_norm_w)
    fv = heads(fv.astype(f32), FOX_HEADS)
    logf = jax.nn.log_sigmoid(ff.astype(f32) + fox_f_bias.astype(f32)).transpose(0, 2, 1)
    o_fox = forgetting_attention(fq, fk, fv, logf) * jax.nn.silu(heads(fg.astype(f32), FOX_HEADS))

    merged = jnp.concatenate([o_gdn, o_fox], axis=1)
    merged = merged.transpose(0, 2, 1, 3).reshape(B, L, MIX_WIDTH).astype(h.dtype)
    out = merged @ w_out
    return h + rmsnorm(out, post_w).astype(h.dtype)


def _fwd_setup_inputs(seed: int = 0) -> dict:
    key = jax.random.key(seed)
    ks = jax.random.split(key, 14)
    f32 = jnp.float32
    x = jax.random.normal(ks[0], (BATCH, SEQ, D_MODEL), f32)
    meta_tokens = jax.random.normal(ks[1], (N_META, D_MODEL), f32)
    pre_norm_w = 1.0 + 0.01 * jax.random.normal(ks[2], (DEPTH, D_MODEL), f32)
    w_in = jax.random.normal(ks[3], (DEPTH, D_MODEL, IN_WIDTH), f32) * (D_MODEL ** -0.5)
    conv_w = jax.random.normal(ks[4], (DEPTH, 3 * GDN_WIDTH, CONV_WIDTH), f32) * (CONV_WIDTH ** -0.5)
    a_log = jnp.log(jax.random.uniform(ks[5], (DEPTH, GDN_HEADS), f32, 1.0, 16.0))
    dt = jnp.exp(jax.random.uniform(ks[6], (DEPTH, GDN_HEADS), f32,
                                    float(np.log(1e-3)), float(np.log(1e-1))))
    dt_bias = dt + jnp.log(-jnp.expm1(-dt))
    gdn_norm_w = 1.0 + 0.01 * jax.random.normal(ks[7], (DEPTH, HEAD_DIM), f32)
    fox_q_norm_w = 1.0 + 0.01 * jax.random.normal(ks[8], (DEPTH, HEAD_DIM), f32)
    fox_k_norm_w = 1.0 + 0.01 * jax.random.normal(ks[9], (DEPTH, HEAD_DIM), f32)
    fox_f_bias = jax.random.uniform(ks[10], (DEPTH, FOX_HEADS), f32, 1.0, 4.0)
    w_out = jax.random.normal(ks[11], (DEPTH, MIX_WIDTH, D_MODEL), f32) * (MIX_WIDTH ** -0.5)
    post_norm_w = 1.0 + 0.01 * jax.random.normal(ks[12], (DEPTH, D_MODEL), f32)
    return {'x': x, 'meta_tokens': meta_tokens, 'pre_norm_w': pre_norm_w, 'w_in': w_in,
            'conv_w': conv_w, 'a_log': a_log, 'dt_bias': dt_bias, 'gdn_norm_w': gdn_norm_w,
            'fox_q_norm_w': fox_q_norm_w, 'fox_k_norm_w': fox_k_norm_w, 'fox_f_bias': fox_f_bias,
            'w_out': w_out, 'post_norm_w': post_norm_w}


def _fwd_reference(x, meta_tokens, pre_norm_w, w_in, conv_w, a_log, dt_bias, gdn_norm_w,
              fox_q_norm_w, fox_k_norm_w, fox_f_bias, w_out, post_norm_w):
    B = x.shape[0]
    meta = jnp.broadcast_to(meta_tokens.astype(x.dtype)[None], (B, N_META, x.shape[-1]))
    h = jnp.concatenate([meta, x], axis=1)
    for l in range(DEPTH):
        h = hybrid_layer(h, pre_norm_w[l], w_in[l], conv_w[l], a_log[l], dt_bias[l],
                         gdn_norm_w[l], fox_q_norm_w[l], fox_k_norm_w[l], fox_f_bias[l],
                         w_out[l], post_norm_w[l])
    return h[:, N_META:]


import jax as _jax
import jax.numpy as _jnp

TWIN_FORMAT = 'train_step'
FWD_PARAMS = ['x', 'meta_tokens', 'pre_norm_w', 'w_in', 'conv_w', 'a_log', 'dt_bias', 'gdn_norm_w', 'fox_q_norm_w', 'fox_k_norm_w', 'fox_f_bias', 'w_out', 'post_norm_w']
TWIN_WEIGHTS = ['meta_tokens', 'pre_norm_w', 'w_in', 'conv_w', 'a_log', 'dt_bias', 'gdn_norm_w', 'fox_q_norm_w', 'fox_k_norm_w', 'fox_f_bias', 'w_out', 'post_norm_w']
TWIN_DIFF_INPUT = 'x'
TWIN_INPUTS = ['x', 'meta_tokens', 'pre_norm_w', 'w_in', 'conv_w', 'a_log', 'dt_bias', 'gdn_norm_w', 'fox_q_norm_w', 'fox_k_norm_w', 'fox_f_bias', 'w_out', 'post_norm_w', 'loss_target', 'm_meta_tokens', 'm_pre_norm_w', 'm_w_in', 'm_conv_w', 'm_a_log', 'm_dt_bias', 'm_gdn_norm_w', 'm_fox_q_norm_w', 'm_fox_k_norm_w', 'm_fox_f_bias', 'm_w_out', 'm_post_norm_w', 'v_meta_tokens', 'v_pre_norm_w', 'v_w_in', 'v_conv_w', 'v_a_log', 'v_dt_bias', 'v_gdn_norm_w', 'v_fox_q_norm_w', 'v_fox_k_norm_w', 'v_fox_f_bias', 'v_w_out', 'v_post_norm_w']
TWIN_OUTPUTS = ['loss', 'grad_x', 'grad_meta_tokens', 'grad_pre_norm_w', 'grad_w_in', 'grad_conv_w', 'grad_a_log', 'grad_dt_bias', 'grad_gdn_norm_w', 'grad_fox_q_norm_w', 'grad_fox_k_norm_w', 'grad_fox_f_bias', 'grad_w_out', 'grad_post_norm_w', 'delta_meta_tokens', 'delta_pre_norm_w', 'delta_w_in', 'delta_conv_w', 'delta_a_log', 'delta_dt_bias', 'delta_gdn_norm_w', 'delta_fox_q_norm_w', 'delta_fox_k_norm_w', 'delta_fox_f_bias', 'delta_w_out', 'delta_post_norm_w', 'new_m_meta_tokens', 'new_m_pre_norm_w', 'new_m_w_in', 'new_m_conv_w', 'new_m_a_log', 'new_m_dt_bias', 'new_m_gdn_norm_w', 'new_m_fox_q_norm_w', 'new_m_fox_k_norm_w', 'new_m_fox_f_bias', 'new_m_w_out', 'new_m_post_norm_w', 'new_v_meta_tokens', 'new_v_pre_norm_w', 'new_v_w_in', 'new_v_conv_w', 'new_v_a_log', 'new_v_dt_bias', 'new_v_gdn_norm_w', 'new_v_fox_q_norm_w', 'new_v_fox_k_norm_w', 'new_v_fox_f_bias', 'new_v_w_out', 'new_v_post_norm_w']
TWIN_LEAF_KINDS = {'loss': 'loss', 'grad_x': 'grad_x', 'grad_meta_tokens': 'grad_w', 'grad_pre_norm_w': 'grad_w', 'grad_w_in': 'grad_w', 'grad_conv_w': 'grad_w', 'grad_a_log': 'grad_w', 'grad_dt_bias': 'grad_w', 'grad_gdn_norm_w': 'grad_w', 'grad_fox_q_norm_w': 'grad_w', 'grad_fox_k_norm_w': 'grad_w', 'grad_fox_f_bias': 'grad_w', 'grad_w_out': 'grad_w', 'grad_post_norm_w': 'grad_w', 'delta_meta_tokens': 'delta_w', 'delta_pre_norm_w': 'delta_w', 'delta_w_in': 'delta_w', 'delta_conv_w': 'delta_w', 'delta_a_log': 'delta_w', 'delta_dt_bias': 'delta_w', 'delta_gdn_norm_w': 'delta_w', 'delta_fox_q_norm_w': 'delta_w', 'delta_fox_k_norm_w': 'delta_w', 'delta_fox_f_bias': 'delta_w', 'delta_w_out': 'delta_w', 'delta_post_norm_w': 'delta_w', 'new_m_meta_tokens': 'new_m', 'new_m_pre_norm_w': 'new_m', 'new_m_w_in': 'new_m', 'new_m_conv_w': 'new_m', 'new_m_a_log': 'new_m', 'new_m_dt_bias': 'new_m', 'new_m_gdn_norm_w': 'new_m', 'new_m_fox_q_norm_w': 'new_m', 'new_m_fox_k_norm_w': 'new_m', 'new_m_fox_f_bias': 'new_m', 'new_m_w_out': 'new_m', 'new_m_post_norm_w': 'new_m', 'new_v_meta_tokens': 'new_v', 'new_v_pre_norm_w': 'new_v', 'new_v_w_in': 'new_v', 'new_v_conv_w': 'new_v', 'new_v_a_log': 'new_v', 'new_v_dt_bias': 'new_v', 'new_v_gdn_norm_w': 'new_v', 'new_v_fox_q_norm_w': 'new_v', 'new_v_fox_k_norm_w': 'new_v', 'new_v_fox_f_bias': 'new_v', 'new_v_w_out': 'new_v', 'new_v_post_norm_w': 'new_v'}


def _forward(args):
    return _fwd_reference(*[args[k] for k in FWD_PARAMS])


def _output_shape():
    out = _jax.eval_shape(lambda: _forward(_fwd_setup_inputs(0)))
    return out.shape, out.dtype

N_MICROBATCH = 1
ADAM_LR = 0.001
ADAM_B1 = 0.9
ADAM_B2 = 0.999
ADAM_EPS = 1e-08
ADAM_WD = 0.01
ADAM_STEP = 10
PER_EXAMPLE_BATCH_AXIS = {'x': 0, 'loss_target': 0}
SHARED_INPUTS = []
_WEIGHT_DTYPES = {'meta_tokens': _jnp.float32, 'pre_norm_w': _jnp.float32, 'w_in': _jnp.float32, 'conv_w': _jnp.float32, 'a_log': _jnp.float32, 'dt_bias': _jnp.float32, 'gdn_norm_w': _jnp.float32, 'fox_q_norm_w': _jnp.float32, 'fox_k_norm_w': _jnp.float32, 'fox_f_bias': _jnp.float32, 'w_out': _jnp.float32, 'post_norm_w': _jnp.float32}
MOMENT_SCALE = {'meta_tokens': 3.917205e-03, 'pre_norm_w': 1.551554e-01, 'w_in': 8.244438e-02, 'conv_w': 9.803532e-02, 'a_log': 5.548648e-01, 'dt_bias': 5.467606e-01, 'gdn_norm_w': 5.189619e-01, 'fox_q_norm_w': 1.000061e-01, 'fox_k_norm_w': 1.002720e-01, 'fox_f_bias': 4.795729e-01, 'w_out': 1.050470e-01, 'post_norm_w': 8.010266e+00}


def _to_microbatches(a, axis):
    t = _jnp.moveaxis(a, axis, 0)
    t = t.reshape((N_MICROBATCH, t.shape[0] // N_MICROBATCH) + t.shape[1:])
    return _jnp.moveaxis(t, 1, axis + 1)


def setup_inputs(seed: int = 0) -> dict:
    inp = _fwd_setup_inputs(seed)
    key = _jax.random.fold_in(_jax.random.key(seed), 7919)
    shape, _ = _output_shape()
    out = dict(inp)
    out["loss_target"] = _jax.random.normal(_jax.random.fold_in(key, 0), shape, _jnp.float32)
    for i, name in enumerate(TWIN_WEIGHTS):
        w = inp[name].astype(_jnp.float32)
        if MOMENT_SCALE is None:
            s = _jnp.sqrt(_jnp.mean(_jnp.square(w)) + 1e-30)
        else:
            s = MOMENT_SCALE[name]
        km, kv = _jax.random.split(_jax.random.fold_in(key, i + 1))
        out[name] = w
        out["m_" + name] = s * _jax.random.normal(km, w.shape, _jnp.float32)
        out["v_" + name] = (s * s) * _jax.random.uniform(kv, w.shape, _jnp.float32, 0.5, 1.5)
    if N_MICROBATCH > 1:
        for name, axis in PER_EXAMPLE_BATCH_AXIS.items():
            out[name] = _to_microbatches(out[name], axis)
    return {'x': out['x'], 'meta_tokens': out['meta_tokens'], 'pre_norm_w': out['pre_norm_w'], 'w_in': out['w_in'], 'conv_w': out['conv_w'], 'a_log': out['a_log'], 'dt_bias': out['dt_bias'], 'gdn_norm_w': out['gdn_norm_w'], 'fox_q_norm_w': out['fox_q_norm_w'], 'fox_k_norm_w': out['fox_k_norm_w'], 'fox_f_bias': out['fox_f_bias'], 'w_out': out['w_out'], 'post_norm_w': out['post_norm_w'], 'loss_target': out['loss_target'], 'm_meta_tokens': out['m_meta_tokens'], 'm_pre_norm_w': out['m_pre_norm_w'], 'm_w_in': out['m_w_in'], 'm_conv_w': out['m_conv_w'], 'm_a_log': out['m_a_log'], 'm_dt_bias': out['m_dt_bias'], 'm_gdn_norm_w': out['m_gdn_norm_w'], 'm_fox_q_norm_w': out['m_fox_q_norm_w'], 'm_fox_k_norm_w': out['m_fox_k_norm_w'], 'm_fox_f_bias': out['m_fox_f_bias'], 'm_w_out': out['m_w_out'], 'm_post_norm_w': out['m_post_norm_w'], 'v_meta_tokens': out['v_meta_tokens'], 'v_pre_norm_w': out['v_pre_norm_w'], 'v_w_in': out['v_w_in'], 'v_conv_w': out['v_conv_w'], 'v_a_log': out['v_a_log'], 'v_dt_bias': out['v_dt_bias'], 'v_gdn_norm_w': out['v_gdn_norm_w'], 'v_fox_q_norm_w': out['v_fox_q_norm_w'], 'v_fox_k_norm_w': out['v_fox_k_norm_w'], 'v_fox_f_bias': out['v_fox_f_bias'], 'v_w_out': out['v_w_out'], 'v_post_norm_w': out['v_post_norm_w']}


def _loss(weights, diff, rest, loss_target):
    with _jax.named_scope("forward"):
        args = {**rest, TWIN_DIFF_INPUT: diff, **{k: w.astype(_WEIGHT_DTYPES[k]) for k, w in weights.items()}}
        y = _forward(args)
    with _jax.named_scope("loss_head"):
        err = _jnp.square(y.astype(_jnp.float32) - loss_target)
        return 0.5 * _jnp.sum(_jnp.mean(err, axis=-1)) if err.ndim else 0.5 * err


def _adamw(w, g, m, v):
    m = ADAM_B1 * m + (1.0 - ADAM_B1) * g
    v = ADAM_B2 * v + (1.0 - ADAM_B2) * _jnp.square(g)
    m_hat = m / (1.0 - ADAM_B1 ** ADAM_STEP)
    v_hat = v / (1.0 - ADAM_B2 ** ADAM_STEP)
    delta = -ADAM_LR * (m_hat / (_jnp.sqrt(v_hat) + ADAM_EPS) + ADAM_WD * w)
    return delta, m, v


def reference(x, meta_tokens, pre_norm_w, w_in, conv_w, a_log, dt_bias, gdn_norm_w, fox_q_norm_w, fox_k_norm_w, fox_f_bias, w_out, post_norm_w, loss_target, m_meta_tokens, m_pre_norm_w, m_w_in, m_conv_w, m_a_log, m_dt_bias, m_gdn_norm_w, m_fox_q_norm_w, m_fox_k_norm_w, m_fox_f_bias, m_w_out, m_post_norm_w, v_meta_tokens, v_pre_norm_w, v_w_in, v_conv_w, v_a_log, v_dt_bias, v_gdn_norm_w, v_fox_q_norm_w, v_fox_k_norm_w, v_fox_f_bias, v_w_out, v_post_norm_w):
    given = dict(x=x, meta_tokens=meta_tokens, pre_norm_w=pre_norm_w, w_in=w_in, conv_w=conv_w, a_log=a_log, dt_bias=dt_bias, gdn_norm_w=gdn_norm_w, fox_q_norm_w=fox_q_norm_w, fox_k_norm_w=fox_k_norm_w, fox_f_bias=fox_f_bias, w_out=w_out, post_norm_w=post_norm_w, loss_target=loss_target, m_meta_tokens=m_meta_tokens, m_pre_norm_w=m_pre_norm_w, m_w_in=m_w_in, m_conv_w=m_conv_w, m_a_log=m_a_log, m_dt_bias=m_dt_bias, m_gdn_norm_w=m_gdn_norm_w, m_fox_q_norm_w=m_fox_q_norm_w, m_fox_k_norm_w=m_fox_k_norm_w, m_fox_f_bias=m_fox_f_bias, m_w_out=m_w_out, m_post_norm_w=m_post_norm_w, v_meta_tokens=v_meta_tokens, v_pre_norm_w=v_pre_norm_w, v_w_in=v_w_in, v_conv_w=v_conv_w, v_a_log=v_a_log, v_dt_bias=v_dt_bias, v_gdn_norm_w=v_gdn_norm_w, v_fox_q_norm_w=v_fox_q_norm_w, v_fox_k_norm_w=v_fox_k_norm_w, v_fox_f_bias=v_fox_f_bias, v_w_out=v_w_out, v_post_norm_w=v_post_norm_w)
    weights = {n: given[n] for n in TWIN_WEIGHTS}
    shared = {n: given[n] for n in SHARED_INPUTS}
    per_example = {n: given[n] for n in ['x']}
    grad_fn = _jax.value_and_grad(_loss, argnums=(0, 1))

    def one_microbatch(ex, loss_target):
        ex = dict(ex)
        diff = ex.pop(TWIN_DIFF_INPUT)
        return grad_fn(weights, diff, {**shared, **ex}, loss_target)

    if N_MICROBATCH == 1:
        loss, (grad_w, grad_x) = one_microbatch(per_example, given["loss_target"])
    else:
        def body(carry, xs):
            loss_sum, grad_sum = carry
            l_k, (gw_k, gx_k) = one_microbatch(xs[0], xs[1])
            with _jax.named_scope("update"):
                return (loss_sum + l_k, _jax.tree.map(_jnp.add, grad_sum, gw_k)), gx_k

        init = (_jnp.zeros((), _jnp.float32), _jax.tree.map(_jnp.zeros_like, weights))
        (loss, grad_w), grad_x = _jax.lax.scan(body, init, (per_example, given["loss_target"]))
    with _jax.named_scope("update"):
        delta_w, new_m, new_v = {}, {}, {}
        for n in TWIN_WEIGHTS:
            delta_w[n], new_m[n], new_v[n] = _adamw(weights[n], grad_w[n], given["m_" + n], given["v_" + n])
    return (loss, grad_x, *[grad_w[n] for n in TWIN_WEIGHTS], *[delta_w[n] for n in TWIN_WEIGHTS],
            *[new_m[n] for n in TWIN_WEIGHTS], *[new_v[n] for n in TWIN_WEIGHTS])
```

```python
import jax
import jax.numpy as jnp
from jax import lax
from jax.experimental import pallas as pl
from jax.experimental.pallas import tpu as pltpu

F32, BF16 = jnp.float32, jnp.bfloat16
HEAD_DIM = 128
N_META = 16
CONV_WIDTH = 4
CHUNK = 64
Q_BLOCK = 128
LANES = 128
EPS = 1e-6
PAD_ROWS = Q_BLOCK - N_META
N_DEV = 8
VMEM_LIMIT = 56 * 1024 * 1024
NEG = -1e30

ADAM_LR, ADAM_B1, ADAM_B2, ADAM_EPS, ADAM_WD, ADAM_STEP = 0.001, 0.9, 0.999, 1e-08, 0.01, 10

_DN = {"nn": (((1,), (0,)), ((), ())), "nt": (((1,), (1,)), ((), ())), "tn": (((0,), (0,)), ((), ()))}


def _cp(*sem):
    return pltpu.CompilerParams(dimension_semantics=sem, vmem_limit_bytes=VMEM_LIMIT)


def _dot(a, b, dims="nn", prec=None):
    return lax.dot_general(a, b, _DN[dims], precision=prec, preferred_element_type=F32)


def _bdot(a, b, dims="nn"):
    return _dot(a.astype(BF16), b.astype(BF16), dims)


def _hdot(a, b, dims="nn"):
    return _dot(a, b, dims, prec=lax.Precision.HIGHEST)


def _iota(shape, dim):
    return lax.broadcasted_iota(jnp.int32, shape, dim)


def _sigmoid(z):
    return 1.0 / (1.0 + jnp.exp(-z))


def _softplus(z):
    e = jnp.exp(-jnp.abs(z))
    u = 1.0 + e
    l1p = jnp.where(u == 1.0, e, jnp.log(u) * (e / jnp.where(u == 1.0, 1.0, u - 1.0)))
    return jnp.maximum(z, 0.0) + l1p


def _silu_and_grad(z):
    s = _sigmoid(z)
    return z * s, s * (1.0 + z * (1.0 - s))


def _prenorm(h, w):
    lp, d = h.shape

    def body(h_ref, w_ref, o_ref):
        x = h_ref[...]
        r = lax.rsqrt(jnp.mean(x * x, axis=-1, keepdims=True) + EPS)
        o_ref[...] = (x * r * w_ref[...]).astype(BF16)

    return pl.pallas_call(
        body, grid=(lp // Q_BLOCK,),
        in_specs=[pl.BlockSpec((Q_BLOCK, d), lambda i: (i, 0)), pl.BlockSpec((1, d), lambda i: (0, 0))],
        out_specs=pl.BlockSpec((Q_BLOCK, d), lambda i: (i, 0)),
        out_shape=jax.ShapeDtypeStruct((lp, d), BF16), name="prenorm", compiler_params=_cp("parallel"))(h, w)


def _matmul(a, b, dims, tn, out_dtype, name):
    m = a.shape[1] if dims == "tn" else a.shape[0]
    n = b.shape[0] if dims == "nt" else b.shape[1]
    kdim = b.shape[1] if dims == "nt" else b.shape[0]
    tn = min(tn, n)
    b_spec = pl.BlockSpec((tn, kdim), lambda j: (j, 0)) if dims == "nt" else pl.BlockSpec((kdim, tn), lambda j: (0, j))

    def body(a_ref, b_ref, o_ref):
        o_ref[...] = _dot(a_ref[...], b_ref[...], dims).astype(out_dtype)

    return pl.pallas_call(
        body, grid=(n // tn,),
        in_specs=[pl.BlockSpec(a.shape, lambda j: (0, 0)), b_spec],
        out_specs=pl.BlockSpec((m, tn), lambda j: (0, j)),
        out_shape=jax.ShapeDtypeStruct((m, n), out_dtype), name=name, compiler_params=_cp("parallel"))(a, b)


def _matmul_nt_ktiled(a, b, tk, name):
    m, k = a.shape
    n = b.shape[0]

    def body(a_ref, b_ref, o_ref):
        @pl.when(pl.program_id(0) == 0)
        def _():
            o_ref[...] = jnp.zeros_like(o_ref)
        o_ref[...] += _dot(a_ref[...], b_ref[...], "nt")

    return pl.pallas_call(
        body, grid=(k // tk,),
        in_specs=[pl.BlockSpec((m, tk), lambda j: (0, j)), pl.BlockSpec((n, tk), lambda j: (0, j))],
        out_specs=pl.BlockSpec((m, n), lambda j: (0, 0)),
        out_shape=jax.ShapeDtypeStruct((m, n), F32), name=name, compiler_params=_cp("arbitrary"))(a, b)


def _conv_taps(x, w):
    c = x * w[CONV_WIDTH - 1:CONV_WIDTH, :]
    for j in range(CONV_WIDTH - 1):
        c = c + pltpu.roll(x, CONV_WIDTH - 1 - j, 0) * w[j:j + 1, :]
    return c


def _gdn_prep(proj, conv_wt, nh):
    lp = proj.shape[0]
    scale = HEAD_DIM ** -0.5

    def body(x_ref, w_ref, o_ref):
        which = pl.program_id(0) // nh
        c = _conv_taps(x_ref[...], w_ref[...])
        s = c * _sigmoid(c)
        r = lax.rsqrt(jnp.sum(s * s, axis=-1, keepdims=True) + EPS)
        f = jnp.where(which == 0, r * scale, jnp.where(which == 1, r, 1.0))
        o_ref[...] = jnp.where(_iota(s.shape, 0) >= PAD_ROWS, s * f, 0.0)

    return pl.pallas_call(
        body, grid=(3 * nh,),
        in_specs=[pl.BlockSpec((lp, LANES), lambda s: (0, s)), pl.BlockSpec((CONV_WIDTH, LANES), lambda s: (0, s))],
        out_specs=pl.BlockSpec((lp, LANES), lambda s: (0, s)),
        out_shape=jax.ShapeDtypeStruct((lp, 3 * nh * HEAD_DIM), F32), name="gdn_prep",
        compiler_params=_cp("parallel"))(proj, conv_wt)


def _gdn_prep_bwd(proj, conv_wt, dqkv, nh):
    lp = proj.shape[0]
    scale = HEAD_DIM ** -0.5

    def body(x_ref, w_ref, dy_ref, dx_ref, dw_ref):
        which = pl.program_id(0) // nh
        x = x_ref[...]
        w = w_ref[...]
        c = _conv_taps(x, w)
        sg = _sigmoid(c)
        s = c * sg
        r = lax.rsqrt(jnp.sum(s * s, axis=-1, keepdims=True) + EPS)
        live = _iota(s.shape, 0) >= PAD_ROWS
        dy = jnp.where(live, dy_ref[...], 0.0)
        y0 = s * r
        dy0 = dy * jnp.where(which == 0, scale, 1.0)
        ds_n = r * (dy0 - y0 * jnp.sum(dy0 * y0, axis=-1, keepdims=True))
        ds = jnp.where(which == 2, dy, ds_n)
        dc = ds * (sg * (1.0 + c * (1.0 - sg)))
        dx = dc * w[CONV_WIDTH - 1:CONV_WIDTH, :]
        rows = [jnp.sum(dc * x, axis=0, keepdims=True)]
        for j in range(CONV_WIDTH - 2, -1, -1):
            sh = CONV_WIDTH - 1 - j
            dx = dx + pltpu.roll(dc, lp - sh, 0) * w[j:j + 1, :]
            rows.insert(0, jnp.sum(dc * pltpu.roll(x, sh, 0), axis=0, keepdims=True))
        dx_ref[...] = dx.astype(BF16)
        dw_ref[...] = jnp.concatenate(rows, axis=0)

    return pl.pallas_call(
        body, grid=(3 * nh,),
        in_specs=[pl.BlockSpec((lp, LANES), lambda s: (0, s)), pl.BlockSpec((CONV_WIDTH, LANES), lambda s: (0, s)),
                  pl.BlockSpec((lp, LANES), lambda s: (0, s))],
        out_specs=[pl.BlockSpec((lp, LANES), lambda s: (0, s)), pl.BlockSpec((CONV_WIDTH, LANES), lambda s: (0, s))],
        out_shape=[jax.ShapeDtypeStruct((lp, 3 * nh * HEAD_DIM), BF16),
                   jax.ShapeDtypeStruct((CONV_WIDTH, 3 * nh * HEAD_DIM), F32)],
        name="gdn_prep_bwd", compiler_params=_cp("parallel"))(proj, conv_wt, dqkv)


def _gates(proj_sm, bias_row, nega_row, nh):
    lp = proj_sm.shape[0]
    nc = lp // CHUNK

    def body(p_ref, b_ref, a_ref, g_ref, gt3_ref, gtf_ref):
        lane = _iota((CHUNK, LANES), 1)
        tri = (_iota((CHUNK, CHUNK), 0) >= _iota((CHUNK, CHUNK), 1)).astype(F32)

        def step(n, carry):
            r0 = pl.multiple_of(n * CHUNK, CHUNK)
            z = p_ref[pl.ds(r0, CHUNK), :] + b_ref[...]
            base = jnp.where(lane < nh, _sigmoid(z),
                             jnp.where(lane < 2 * nh, a_ref[...] * _softplus(z),
                                       jnp.where(lane < 3 * nh, -_softplus(-z), 0.0)))
            base = jnp.where(r0 + _iota((CHUNK, LANES), 0) >= PAD_ROWS, base, 0.0)
            cs = _hdot(tri, base)
            run = jnp.where((lane >= 2 * nh) & (lane < 3 * nh), cs + carry, cs)
            sh = pltpu.roll(run, 2 * nh, 1)
            out = base + jnp.where((lane >= 3 * nh) & (lane < 5 * nh), sh, 0.0)
            g_ref[pl.ds(r0, CHUNK), :] = out
            gt3_ref[n] = out.T
            return carry + cs[CHUNK - 1:CHUNK, :]

        lax.fori_loop(0, nc, step, jnp.zeros((1, LANES), F32))
        gtf_ref[...] = g_ref[...].T

    return pl.pallas_call(
        body,
        out_shape=[jax.ShapeDtypeStruct((lp, LANES), F32), jax.ShapeDtypeStruct((nc, LANES, CHUNK), F32),
                   jax.ShapeDtypeStruct((LANES, lp), F32)],
        name="gates", compiler_params=pltpu.CompilerParams(vmem_limit_bytes=VMEM_LIMIT))(proj_sm, bias_row, nega_row)


def _gates_bwd(proj_sm, bias_row, nega_row, gates, dgate_gdn, dc_t, nh):
    lp = proj_sm.shape[0]
    nc = lp // CHUNK

    def body(p_ref, b_ref, a_ref, g_ref, dg_ref, dc_ref, dz_ref, sm_ref, dct_scr):
        lane = _iota((CHUNK, LANES), 1)
        triu = (_iota((CHUNK, CHUNK), 0) <= _iota((CHUNK, CHUNK), 1)).astype(F32)
        dct_scr[...] = dc_ref[...].T
        sm_ref[...] = jnp.zeros_like(sm_ref)

        def step(i, carry):
            n = nc - 1 - i
            r0 = pl.multiple_of(n * CHUNK, CHUNK)
            z = p_ref[pl.ds(r0, CHUNK), :] + b_ref[...]
            gt = g_ref[pl.ds(r0, CHUNK), :]
            dgd = dg_ref[pl.ds(r0, CHUNK), :]
            dch = dct_scr[pl.ds(r0, CHUNK), :]
            rc = _hdot(triu, dch) + carry
            sg = _sigmoid(z)
            dz = jnp.where(lane < nh, dgd * sg * (1.0 - sg),
                           jnp.where(lane < 2 * nh, dgd * a_ref[...] * sg,
                                     jnp.where(lane < 3 * nh, rc * (1.0 - sg), 0.0)))
            dz = jnp.where(r0 + _iota((CHUNK, LANES), 0) >= PAD_ROWS, dz, 0.0)
            dz_ref[pl.ds(r0, CHUNK), :] = dz.astype(BF16)
            sm_ref[0:1, :] += jnp.sum(dz, axis=0, keepdims=True)
            sm_ref[1:2, :] += jnp.sum(jnp.where((lane >= nh) & (lane < 2 * nh), dgd * gt, 0.0), axis=0, keepdims=True)
            return carry + jnp.sum(dch, axis=0, keepdims=True)

        lax.fori_loop(0, nc, step, jnp.zeros((1, LANES), F32))

    return pl.pallas_call(
        body,
        out_shape=[jax.ShapeDtypeStruct((lp, LANES), BF16), jax.ShapeDtypeStruct((8, LANES), F32)],
        scratch_shapes=[pltpu.VMEM((lp, LANES), F32)],
        name="gates_bwd", compiler_params=pltpu.CompilerParams(vmem_limit_bytes=VMEM_LIMIT))(
            proj_sm, bias_row, nega_row, gates, dgate_gdn, dc_t)


def _tri_inv(a):
    t = jnp.where(_iota(a.shape, 0) == _iota(a.shape, 1), 1.0, 0.0) - a
    p = a
    for _ in range(5):
        p = _hdot(p, p)
        t = t + _hdot(t, p)
    return t


def _gdn_chunk(q, k, v, beta, gc, gr):
    ii, jj = _iota((CHUNK, CHUNK), 0), _iota((CHUNK, CHUNK), 1)
    causal, strict = ii >= jj, ii > jj
    dm = jnp.where(causal, jnp.exp(jnp.where(causal, gc - gr, 0.0)), 0.0)
    kk = _bdot(k, k, "nt")
    a = jnp.where(strict, beta * kk * dm, 0.0)
    t = _tri_inv(a)
    eg = jnp.exp(gc)
    glast = gc[CHUNK - 1:CHUNK, :]
    ekd = jnp.exp(glast - gc)
    bv = beta * v
    bk = (beta * eg) * k
    ub = _bdot(t, jnp.concatenate([bv, bk], axis=1))
    qk = _bdot(q, k, "nt")
    return dict(causal=causal, strict=strict, dm=dm, kk=kk, a=a, t=t, eg=eg, ekd=ekd, bv=bv, bk=bk,
                u=ub[:, :HEAD_DIM], w=ub[:, HEAD_DIM:], qk=qk, aqk=jnp.where(causal, qk * dm, 0.0),
                q_dec=q * eg, k_dec=k * ekd, decay=jnp.exp(glast))


def _gdn_fwd(qkv, gates, gt3, nh):
    lp = qkv.shape[0]
    nc = lp // CHUNK
    w = nh * HEAD_DIM

    def body(q_ref, k_ref, v_ref, g_ref, gt_ref, o_ref, sall_ref, s_scr):
        @pl.when(pl.program_id(0) == 0)
        def _():
            s_scr[...] = jnp.zeros_like(s_scr)
        g = g_ref[...]
        gt = gt_ref[0]
        for h in range(nh):
            sl = slice(h * HEAD_DIM, (h + 1) * HEAD_DIM)
            c = _gdn_chunk(q_ref[:, sl], k_ref[:, sl], v_ref[:, sl], g[:, h:h + 1],
                           g[:, 3 * nh + h:3 * nh + h + 1], gt[3 * nh + h:3 * nh + h + 1, :])
            s = s_scr[h]
            sall_ref[0, h] = s
            v_new = c["u"] - _bdot(c["w"], s)
            o_ref[:, sl] = _bdot(c["q_dec"], s) + _bdot(c["aqk"], v_new)
            s_scr[h] = s * c["decay"] + _bdot(c["k_dec"], v_new, "tn")

    return pl.pallas_call(
        body, grid=(nc,),
        in_specs=[pl.BlockSpec((CHUNK, w), lambda n: (n, 0)), pl.BlockSpec((CHUNK, w), lambda n: (n, 1)),
                  pl.BlockSpec((CHUNK, w), lambda n: (n, 2)), pl.BlockSpec((CHUNK, LANES), lambda n: (n, 0)),
                  pl.BlockSpec((1, LANES, CHUNK), lambda n: (n, 0, 0))],
        out_specs=[pl.BlockSpec((CHUNK, w), lambda n: (n, 0)),
                   pl.BlockSpec((1, nh, HEAD_DIM, HEAD_DIM), lambda n: (n, 0, 0, 0))],
        out_shape=[jax.ShapeDtypeStruct((lp, w), F32), jax.ShapeDtypeStruct((nc, nh, HEAD_DIM, HEAD_DIM), F32)],
        scratch_shapes=[pltpu.VMEM((nh, HEAD_DIM, HEAD_DIM), F32)],
        name="gdn_fwd", compiler_params=_cp("arbitrary"))(qkv, qkv, qkv, gates, gt3)


def _gdn_bwd(qkv, gates, gt3, s_all, do, nh):
    lp = qkv.shape[0]
    nc = lp // CHUNK
    w = nh * HEAD_DIM
    rev = lambda n: nc - 1 - n

    def body(q_ref, k_ref, v_ref, g_ref, gt_ref, s_ref, do_ref, dq_ref, dk_ref, dv_ref, dg_ref, ds_scr):
        @pl.when(pl.program_id(0) == 0)
        def _():
            ds_scr[...] = jnp.zeros_like(ds_scr)
        g = g_ref[...]
        gt = gt_ref[0]
        lane = _iota((CHUNK, LANES), 1)
        ones = jnp.ones((CHUNK, LANES), F32)
        row = _iota((CHUNK, 1), 0)
        acc = jnp.zeros((CHUNK, LANES), F32)
        for h in range(nh):
            sl = slice(h * HEAD_DIM, (h + 1) * HEAD_DIM)
            q, k, v = q_ref[:, sl], k_ref[:, sl], v_ref[:, sl]
            beta = g[:, h:h + 1]
            c = _gdn_chunk(q, k, v, beta, g[:, 3 * nh + h:3 * nh + h + 1], gt[3 * nh + h:3 * nh + h + 1, :])
            s = s_ref[0, h]
            dsn = ds_scr[h]
            dout = do_ref[:, sl]
            v_new = c["u"] - _bdot(c["w"], s)
            dq_dec = _bdot(dout, s, "nt")
            daqk = jnp.where(c["causal"], _bdot(dout, v_new, "nt"), 0.0)
            dv_new = _bdot(c["aqk"], dout, "tn") + _bdot(c["k_dec"], dsn)
            dk_dec = _bdot(v_new, dsn, "nt")
            ddecay = jnp.sum(jnp.sum(dsn * s, axis=1, keepdims=True), axis=0, keepdims=True)
            dw = -_bdot(dv_new, s, "nt")
            ds_scr[h] = _bdot(c["q_dec"], dout, "tn") + c["decay"] * dsn - _bdot(c["w"], dv_new, "tn")
            duw = jnp.concatenate([dv_new, dw], axis=1)
            dt = _bdot(duw, jnp.concatenate([c["bv"], c["bk"]], axis=1), "nt")
            dbvk = _bdot(c["t"], duw, "tn")
            dbv, dbk = dbvk[:, :HEAD_DIM], dbvk[:, HEAD_DIM:]
            da = jnp.where(c["strict"], -_bdot(_bdot(c["t"], dt, "tn"), c["t"], "nt"), 0.0)
            dkk = da * beta * c["dm"]
            dqk = daqk * c["dm"]
            e = da * c["a"] + daqk * c["aqk"]
            dq_ref[:, sl] = dq_dec * c["eg"] + _bdot(dqk, k)
            dk_ref[:, sl] = (dk_dec * c["ekd"] + _bdot(dkk, k) + _bdot(dkk, k, "tn") + _bdot(dqk, q, "tn")
                             + (beta * c["eg"]) * dbk)
            dv_ref[:, sl] = beta * dbv
            rs = lambda x: jnp.sum(x, axis=1, keepdims=True)
            dbeta = rs(dbv * v) + c["eg"] * rs(dbk * k) + rs(da * c["kk"] * c["dm"])
            kd_term = rs(dk_dec * c["k_dec"])
            dg_cum = (rs(dq_dec * c["q_dec"]) - kd_term + rs(dbk * c["bk"]) + rs(e)
                      - _hdot(e, ones, "tn")[:, 0:1])
            last = jnp.sum(kd_term, axis=0, keepdims=True) + ddecay * c["decay"]
            dg_cum = dg_cum + jnp.where(row == CHUNK - 1, last, 0.0)
            acc = acc + jnp.where(lane == h, dbeta, 0.0) + jnp.where(lane == nh + h, dg_cum, 0.0)
        triu = (_iota((CHUNK, CHUNK), 0) <= _iota((CHUNK, CHUNK), 1)).astype(F32)
        dg_ref[...] = jnp.where(lane < nh, acc, _hdot(triu, acc))

    return pl.pallas_call(
        body, grid=(nc,),
        in_specs=[pl.BlockSpec((CHUNK, w), lambda n: (rev(n), 0)), pl.BlockSpec((CHUNK, w), lambda n: (rev(n), 1)),
                  pl.BlockSpec((CHUNK, w), lambda n: (rev(n), 2)), pl.BlockSpec((CHUNK, LANES), lambda n: (rev(n), 0)),
                  pl.BlockSpec((1, LANES, CHUNK), lambda n: (rev(n), 0, 0)),
                  pl.BlockSpec((1, nh, HEAD_DIM, HEAD_DIM), lambda n: (rev(n), 0, 0, 0)),
                  pl.BlockSpec((CHUNK, w), lambda n: (rev(n), 0))],
        out_specs=[pl.BlockSpec((CHUNK, w), lambda n: (rev(n), 0))] * 3 + [pl.BlockSpec((CHUNK, LANES), lambda n: (rev(n), 0))],
        out_shape=[jax.ShapeDtypeStruct((lp, w), F32)] * 3 + [jax.ShapeDtypeStruct((lp, LANES), F32)],
        scratch_shapes=[pltpu.VMEM((nh, HEAD_DIM, HEAD_DIM), F32)],
        name="gdn_bwd", compiler_params=_cp("arbitrary"))(qkv, qkv, qkv, gates, gt3, s_all, do)


def _merge_gdn(o_gdn, proj, norm_w, nh):
    lp = o_gdn.shape[0]

    def body(o_ref, z_ref, w_ref, m_ref):
        o = o_ref[...]
        r = lax.rsqrt(jnp.mean(o * o, axis=-1, keepdims=True) + EPS)
        z = z_ref[...]
        m_ref[...] = (o * r * w_ref[...] * (z * _sigmoid(z))).astype(BF16)

    return pl.pallas_call(
        body, grid=(nh,),
        in_specs=[pl.BlockSpec((lp, LANES), lambda s: (0, s)), pl.BlockSpec((lp, LANES), lambda s: (0, 3 * nh + s)),
                  pl.BlockSpec((1, LANES), lambda s: (0, 0))],
        out_specs=pl.BlockSpec((lp, LANES), lambda s: (0, s)),
        out_shape=jax.ShapeDtypeStruct((lp, nh * HEAD_DIM), BF16), name="merge_gdn",
        compiler_params=_cp("parallel"))(o_gdn, proj, norm_w)


def _merge_gdn_bwd(o_gdn, proj, norm_w, dmerged, nh):
    lp = o_gdn.shape[0]

    def body(o_ref, z_ref, w_ref, dm_ref, do_ref, dz_ref, dw_ref):
        o = o_ref[...]
        r = lax.rsqrt(jnp.mean(o * o, axis=-1, keepdims=True) + EPS)
        xh = o * r
        silu, dsilu = _silu_and_grad(z_ref[...])
        dm = dm_ref[...]
        dn = dm * silu
        dz_ref[...] = (dm * (xh * w_ref[...]) * dsilu).astype(BF16)
        dnw = dn * w_ref[...]
        do_ref[...] = r * (dnw - xh * jnp.mean(dnw * xh, axis=-1, keepdims=True))

        @pl.when(pl.program_id(0) == 0)
        def _():
            dw_ref[...] = jnp.zeros_like(dw_ref)
        dw_ref[...] += jnp.sum(dn * xh, axis=0, keepdims=True)

    w = nh * HEAD_DIM
    return pl.pallas_call(
        body, grid=(nh,),
        in_specs=[pl.BlockSpec((lp, LANES), lambda s: (0, s)), pl.BlockSpec((lp, LANES), lambda s: (0, 3 * nh + s)),
                  pl.BlockSpec((1, LANES), lambda s: (0, 0)), pl.BlockSpec((lp, LANES), lambda s: (0, s))],
        out_specs=[pl.BlockSpec((lp, LANES), lambda s: (0, s)), pl.BlockSpec((lp, LANES), lambda s: (0, s)),
                   pl.BlockSpec((1, LANES), lambda s: (0, 0))],
        out_shape=[jax.ShapeDtypeStruct((lp, w), F32), jax.ShapeDtypeStruct((lp, w), BF16),
                   jax.ShapeDtypeStruct((1, LANES), F32)],
        name="merge_gdn_bwd", compiler_params=_cp("arbitrary"))(o_gdn, proj, norm_w, dmerged)


def _fox_prep(proj, qk_w, nh):
    lp = proj.shape[0]

    def body(x_ref, w_ref, o_ref):
        x = x_ref[...]
        r = lax.rsqrt(jnp.mean(x * x, axis=-1, keepdims=True) + EPS)
        o_ref[...] = x * r * w_ref[0]

    return pl.pallas_call(
        body, grid=(2 * nh,),
        in_specs=[pl.BlockSpec((lp, LANES), lambda s: (0, 4 * nh + s)), pl.BlockSpec((1, 1, LANES), lambda s: (s // nh, 0, 0))],
        out_specs=pl.BlockSpec((lp, LANES), lambda s: (0, s)),
        out_shape=jax.ShapeDtypeStruct((lp, 2 * nh * HEAD_DIM), F32), name="fox_prep",
        compiler_params=_cp("parallel"))(proj, qk_w)


def _fox_prep_bwd(proj, qk_w, dqk, nh):
    lp = proj.shape[0]

    def body(x_ref, w_ref, dy_ref, dx_ref, dw_ref):
        x = x_ref[...]
        r = lax.rsqrt(jnp.mean(x * x, axis=-1, keepdims=True) + EPS)
        xh = x * r
        dy = dy_ref[...]
        dyw = dy * w_ref[0]
        dx_ref[...] = (r * (dyw - xh * jnp.mean(dyw * xh, axis=-1, keepdims=True))).astype(BF16)

        @pl.when(pl.program_id(0) % nh == 0)
        def _():
            dw_ref[...] = jnp.zeros_like(dw_ref)
        dw_ref[0] += jnp.sum(dy * xh, axis=0, keepdims=True)

    return pl.pallas_call(
        body, grid=(2 * nh,),
        in_specs=[pl.BlockSpec((lp, LANES), lambda s: (0, 4 * nh + s)), pl.BlockSpec((1, 1, LANES), lambda s: (s // nh, 0, 0)),
                  pl.BlockSpec((lp, LANES), lambda s: (0, s))],
        out_specs=[pl.BlockSpec((lp, LANES), lambda s: (0, s)), pl.BlockSpec((1, 1, LANES), lambda s: (s // nh, 0, 0))],
        out_shape=[jax.ShapeDtypeStruct((lp, 2 * nh * HEAD_DIM), BF16), jax.ShapeDtypeStruct((2, 1, LANES), F32)],
        name="fox_prep_bwd", compiler_params=_cp("arbitrary"))(proj, qk_w, dqk)


def _fox_probs(q, k, gates, crow, h, i, nh, lp):
    lane = _iota((Q_BLOCK, LANES), 1)
    ct = jnp.sum(jnp.where(lane == 4 * nh + h, gates, 0.0), axis=1, keepdims=True)
    s = _bdot(q, k, "nt") * (HEAD_DIM ** -0.5) + (ct - crow)
    t = i * Q_BLOCK + _iota((Q_BLOCK, lp), 0)
    kp = _iota((Q_BLOCK, lp), 1)
    s = jnp.where((kp <= t) & ((kp >= PAD_ROWS) | (t < PAD_ROWS)), s, NEG)
    p = jnp.exp(s - jnp.max(s, axis=1, keepdims=True))
    return p / jnp.sum(p, axis=1, keepdims=True)


def _fox_specs(lp, nh):
    return [pl.BlockSpec((Q_BLOCK, LANES), lambda h, i: (i, h)),
            pl.BlockSpec((lp, LANES), lambda h, i: (0, nh + h)),
            pl.BlockSpec((lp, LANES), lambda h, i: (0, 6 * nh + h)),
            pl.BlockSpec((Q_BLOCK, LANES), lambda h, i: (i, 0)),
            pl.BlockSpec((LANES, lp), lambda h, i: (0, 0))]


def _fox_fwd(qkn, proj, gates, gtf, nh):
    lp = qkn.shape[0]

    def body(q_ref, k_ref, v_ref, g_ref, gt_ref, o_ref):
        h, i = pl.program_id(0), pl.program_id(1)
        p = _fox_probs(q_ref[...], k_ref[...], g_ref[...], gt_ref[pl.ds(4 * nh + h, 1), :], h, i, nh, lp)
        o_ref[...] = _bdot(p, v_ref[...])

    return pl.pallas_call(
        body, grid=(nh, lp // Q_BLOCK), in_specs=_fox_specs(lp, nh),
        out_specs=pl.BlockSpec((Q_BLOCK, LANES), lambda h, i: (i, h)),
        out_shape=jax.ShapeDtypeStruct((lp, nh * HEAD_DIM), F32), name="fox_fwd",
        compiler_params=_cp("parallel", "parallel"))(qkn, qkn, proj, gates, gtf)


def _fox_bwd(qkn, proj, gates, gtf, do, nh):
    lp = qkn.shape[0]
    w = nh * HEAD_DIM
    scale = HEAD_DIM ** -0.5

    def body(q_ref, k_ref, v_ref, g_ref, gt_ref, do_ref, dq_ref, dk_ref, dv_ref, dc_ref):
        h, i = pl.program_id(0), pl.program_id(1)

        @pl.when(i == 0)
        def _():
            dk_ref[...] = jnp.zeros_like(dk_ref)
            dv_ref[...] = jnp.zeros_like(dv_ref)
            dc_ref[...] = jnp.zeros_like(dc_ref)
        q, k = q_ref[...], k_ref[...]
        p = _fox_probs(q, k, g_ref[...], gt_ref[pl.ds(4 * nh + h, 1), :], h, i, nh, lp)
        dout = do_ref[...]
        dp = _bdot(dout, v_ref[...], "nt")
        ds = p * (dp - jnp.sum(p * dp, axis=1, keepdims=True))
        dq_ref[...] = _bdot(ds, k) * scale
        dk_ref[...] += _bdot(ds, q, "tn") * scale
        dv_ref[...] += _bdot(p, dout, "tn")
        dc_ref[0] -= jnp.sum(ds, axis=0, keepdims=True)

    blk = pl.BlockSpec((Q_BLOCK, LANES), lambda h, i: (i, h))
    col = pl.BlockSpec((lp, LANES), lambda h, i: (0, h))
    return pl.pallas_call(
        body, grid=(nh, lp // Q_BLOCK), in_specs=_fox_specs(lp, nh) + [blk],
        out_specs=[blk, col, col, pl.BlockSpec((1, 1, lp), lambda h, i: (h, 0, 0))],
        out_shape=[jax.ShapeDtypeStruct((lp, w), F32)] * 3 + [jax.ShapeDtypeStruct((nh, 1, lp), F32)],
        name="fox_bwd", compiler_params=_cp("parallel", "arbitrary"))(qkn, qkn, proj, gates, gtf, do)


def _merge_fox(o_fox, proj, nh):
    lp = o_fox.shape[0]

    def body(o_ref, z_ref, m_ref):
        z = z_ref[...]
        m_ref[...] = (o_ref[...] * (z * _sigmoid(z))).astype(BF16)

    return pl.pallas_call(
        body, grid=(nh,),
        in_specs=[pl.BlockSpec((lp, LANES), lambda s: (0, s)), pl.BlockSpec((lp, LANES), lambda s: (0, 7 * nh + s))],
        out_specs=pl.BlockSpec((lp, LANES), lambda s: (0, s)),
        out_shape=jax.ShapeDtypeStruct((lp, nh * HEAD_DIM), BF16), name="merge_fox",
        compiler_params=_cp("parallel"))(o_fox, proj)


def _merge_fox_bwd(o_fox, proj, dmerged, nh):
    lp = o_fox.shape[0]

    def body(o_ref, z_ref, dm_ref, do_ref, dz_ref):
        silu, dsilu = _silu_and_grad(z_ref[...])
        dm = dm_ref[...]
        do_ref[...] = dm * silu
        dz_ref[...] = (dm * o_ref[...] * dsilu).astype(BF16)

    w = nh * HEAD_DIM
    return pl.pallas_call(
        body, grid=(nh,),
        in_specs=[pl.BlockSpec((lp, LANES), lambda s: (0, s)), pl.BlockSpec((lp, LANES), lambda s: (0, 7 * nh + s)),
                  pl.BlockSpec((lp, LANES), lambda s: (0, nh + s))],
        out_specs=[pl.BlockSpec((lp, LANES), lambda s: (0, s)), pl.BlockSpec((lp, LANES), lambda s: (0, s))],
        out_shape=[jax.ShapeDtypeStruct((lp, w), F32), jax.ShapeDtypeStruct((lp, w), BF16)],
        name="merge_fox_bwd", compiler_params=_cp("parallel"))(o_fox, proj, dmerged)


def _post(out, h, target, post_w):
    lp, d = out.shape

    def body(o_ref, h_ref, t_ref, w_ref, dy_ref, do_ref, loss_ref, dw_ref):
        i = pl.program_id(0)

        @pl.when(i == 0)
        def _():
            loss_ref[...] = jnp.zeros_like(loss_ref)
            dw_ref[...] = jnp.zeros_like(dw_ref)
        o = o_ref[...]
        r = lax.rsqrt(jnp.mean(o * o, axis=-1, keepdims=True) + EPS)
        nrm = o * r
        err = jnp.where(i > 0, h_ref[...] + nrm * w_ref[...] - t_ref[...], 0.0)
        loss_ref[0:1, :] += 0.5 * jnp.sum(jnp.sum(err * err, axis=1, keepdims=True), axis=0, keepdims=True) / d
        dy = err / d
        dy_ref[...] = dy
        dw_ref[...] += jnp.sum(dy * nrm, axis=0, keepdims=True)
        dyw = dy * w_ref[...]
        do_ref[...] = (r * (dyw - nrm * jnp.mean(dyw * nrm, axis=-1, keepdims=True))).astype(BF16)

    row = pl.BlockSpec((Q_BLOCK, d), lambda i: (i, 0))
    vec = pl.BlockSpec((1, d), lambda i: (0, 0))
    return pl.pallas_call(
        body, grid=(lp // Q_BLOCK,),
        in_specs=[row, row, pl.BlockSpec((Q_BLOCK, d), lambda i: (jnp.maximum(i - 1, 0), 0)), vec],
        out_specs=[row, row, pl.BlockSpec((8, LANES), lambda i: (0, 0)), vec],
        out_shape=[jax.ShapeDtypeStruct((lp, d), F32), jax.ShapeDtypeStruct((lp, d), BF16),
                   jax.ShapeDtypeStruct((8, LANES), F32), jax.ShapeDtypeStruct((1, d), F32)],
        name="post", compiler_params=_cp("arbitrary"))(out, h, target, post_w)


def _prenorm_bwd(dxn, h, w, dy):
    lp, d = h.shape

    def body(dx_ref, h_ref, w_ref, dy_ref, dh_ref, dw_ref):
        @pl.when(pl.program_id(0) == 0)
        def _():
            dw_ref[...] = jnp.zeros_like(dw_ref)
        x = h_ref[...]
        r = lax.rsqrt(jnp.mean(x * x, axis=-1, keepdims=True) + EPS)
        xh = x * r
        dxn_ = dx_ref[...]
        dw_ref[...] += jnp.sum(dxn_ * xh, axis=0, keepdims=True)
        dxw = dxn_ * w_ref[...]
        dh_ref[...] = dy_ref[...] + r * (dxw - xh * jnp.mean(dxw * xh, axis=-1, keepdims=True))

    row = pl.BlockSpec((Q_BLOCK, d), lambda i: (i, 0))
    vec = pl.BlockSpec((1, d), lambda i: (0, 0))
    return pl.pallas_call(
        body, grid=(lp // Q_BLOCK,), in_specs=[row, row, vec, row], out_specs=[row, vec],
        out_shape=[jax.ShapeDtypeStruct((lp, d), F32), jax.ShapeDtypeStruct((1, d), F32)],
        name="prenorm_bwd", compiler_params=_cp("arbitrary"))(dxn, h, w, dy)


def _layer_grads(x, target, meta, pre_w, wp, wsm, conv_w, a_log, dt_bias, gdn_norm_w, fq_w, fk_w, f_bias,
                 w_out, post_w):
    seq, d = x.shape
    nh = a_log.shape[1]
    w = nh * HEAD_DIM
    h = jnp.concatenate([jnp.zeros((PAD_ROWS, d), F32), meta, x], axis=0)
    conv_wt = conv_w.T
    zpad = jnp.zeros((1, LANES - 3 * nh), F32)
    bias_row = jnp.concatenate([jnp.zeros((1, nh), F32), dt_bias, f_bias, zpad], axis=1)
    nega_row = jnp.concatenate([jnp.zeros((1, nh), F32), -jnp.exp(a_log), jnp.zeros((1, nh), F32), zpad], axis=1)
    qk_w = jnp.stack([fq_w, fk_w])

    xn = _prenorm(h, pre_w)
    proj = _matmul(xn, wp, "nn", 512, F32, "proj")
    proj_sm = _matmul(xn, wsm, "nn", LANES, F32, "proj_sm")
    qkv = _gdn_prep(proj, conv_wt, nh)
    gates, gt3, gtf = _gates(proj_sm, bias_row, nega_row, nh)
    o_gdn, s_all = _gdn_fwd(qkv, gates, gt3, nh)
    qkn = _fox_prep(proj, qk_w, nh)
    o_fox = _fox_fwd(qkn, proj, gates, gtf, nh)
    merged = jnp.concatenate([_merge_gdn(o_gdn, proj, gdn_norm_w, nh), _merge_fox(o_fox, proj, nh)], axis=1)
    out = _matmul(merged, w_out, "nn", 512, F32, "out_proj")
    dy, dout, loss_blk, dpost_w = _post(out, h, target, post_w)

    dw_out = _matmul(merged, dout, "tn", 512, BF16, "dw_out")
    dmerged = _matmul(dout, w_out, "nt", 512, F32, "dmerged")
    do_gdn, dgz, dgdn_norm_w = _merge_gdn_bwd(o_gdn, proj, gdn_norm_w, dmerged, nh)
    do_fox, dfg = _merge_fox_bwd(o_fox, proj, dmerged, nh)
    dqn, dkn, dfv, dc_t = _fox_bwd(qkn, proj, gates, gtf, do_fox, nh)
    dfqk, dqk_w = _fox_prep_bwd(proj, qk_w, jnp.concatenate([dqn, dkn], axis=1), nh)
    dgq, dgk, dgv, dgate = _gdn_bwd(qkv, gates, gt3, s_all, do_gdn, nh)
    dqkv_n = jnp.concatenate([dgq, dgk, dgv], axis=1)
    dqkv, dconv_wt = _gdn_prep_bwd(proj, conv_wt, dqkv_n, nh)
    dc_rows = jnp.pad(dc_t.reshape(nh, -1), ((2 * nh, LANES - 3 * nh), (0, 0)))
    dsm, gate_sums = _gates_bwd(proj_sm, bias_row, nega_row, gates, dgate, dc_rows, nh)
    dproj = jnp.concatenate([dqkv, dgz, dfqk, dfv.astype(BF16), dfg], axis=1)
    dwp = _matmul(xn, dproj, "tn", 512, BF16, "dw_in")
    dwsm = _matmul(xn, dsm, "tn", LANES, BF16, "dw_in_sm")
    dxn = _matmul_nt_ktiled(dproj, wp, 512, "dxn") + _matmul(dsm, wsm, "nt", 512, F32, "dxn_sm")
    dh, dpre_w = _prenorm_bwd(dxn, h, pre_w, dy)
    return dict(
        loss=loss_blk[0:1, 0:1], grad_x=dh[Q_BLOCK:], meta=dh[PAD_ROWS:Q_BLOCK], pre_w=dpre_w, wp=dwp, wsm=dwsm,
        conv_w=dconv_wt.T, a_log=gate_sums[1:2, nh:2 * nh], dt_bias=gate_sums[0:1, nh:2 * nh],
        gdn_norm_w=dgdn_norm_w, fq_w=dqk_w[0], fk_w=dqk_w[1], f_bias=gate_sums[0:1, 2 * nh:3 * nh],
        w_out=dw_out, post_w=dpost_w)


def _cast_bf16(a, tr, name):
    r, c = a.shape

    def body(a_ref, o_ref):
        o_ref[...] = a_ref[...].astype(BF16)

    return pl.pallas_call(
        body, grid=(r // tr,), in_specs=[pl.BlockSpec((tr, c), lambda i: (i, 0))],
        out_specs=pl.BlockSpec((tr, c), lambda i: (i, 0)), out_shape=jax.ShapeDtypeStruct((r, c), BF16),
        name=name, compiler_params=_cp("parallel"))(a)


def _exchange(arrays, modes, name):
    n = len(arrays)
    out_shape = [jax.ShapeDtypeStruct(((N_DEV,) + a.shape) if md == "gather" else a.shape, a.dtype)
                 for a, md in zip(arrays, modes)]

    def body(*refs):
        ins, outs = refs[:n], refs[n:2 * n]
        send_sems, recv_sems, local_sems = refs[2 * n:]
        x, y, c = lax.axis_index("x"), lax.axis_index("y"), lax.axis_index("c")
        me = 4 * x + 2 * y + c
        local, sends, recvs = [], [], []
        for a in range(n):
            src = ins[a] if modes[a] == "gather" else ins[a].at[me]
            cp = pltpu.make_async_copy(src, outs[a].at[me], local_sems.at[a])
            cp.start()
            local.append(cp)
        for k in range(1, N_DEV):
            px = 1 - x if k & 4 else x
            py = 1 - y if k & 2 else y
            pc = 1 - c if k & 1 else c
            pidx = 4 * px + 2 * py + pc
            for a in range(n):
                src = ins[a] if modes[a] == "gather" else ins[a].at[pidx]
                kw = dict(send_sem=send_sems.at[a, k - 1], recv_sem=recv_sems.at[a, k - 1], device_id=(px, py, pc),
                          device_id_type=pl.DeviceIdType.MESH)
                cp = pltpu.make_async_remote_copy(src_ref=src, dst_ref=outs[a].at[me], **kw)
                cp.start()
                sends.append(cp)
                recvs.append(pltpu.make_async_remote_copy(src_ref=src, dst_ref=outs[a].at[pidx], **kw))
        for cp in local:
            cp.wait()
        for cp in sends:
            cp.wait_send()
        for cp in recvs:
            cp.wait_recv()

    any_spec = pl.BlockSpec(memory_space=pl.ANY)
    return pl.pallas_call(
        body, in_specs=[any_spec] * n, out_specs=[any_spec] * n, out_shape=out_shape,
        scratch_shapes=[pltpu.SemaphoreType.DMA((n, N_DEV - 1)), pltpu.SemaphoreType.DMA((n, N_DEV - 1)),
                        pltpu.SemaphoreType.DMA((n,))],
        name=name)(*arrays)


def _native_segments(nh):
    w = nh * HEAD_DIM
    return [(0, 4 * w, 0, 0), (4 * w, 4 * w + 2 * nh, 1, 0), (4 * w + 2 * nh, 8 * w + 2 * nh, 0, 4 * w),
            (8 * w + 2 * nh, 8 * w + 3 * nh, 1, 2 * nh)]


def _relayout_w_in(wg, nh, tr):
    _, d, cols = wg.shape
    w = nh * HEAD_DIM

    def native(ref, j0, j1):
        out = []
        while j0 < j1:
            blk = j0 // cols
            end = min(j1, (blk + 1) * cols)
            out.append(ref[blk, :, pl.ds(j0 - blk * cols, end - j0)])
            j0 = end
        return out

    def body(g_ref, wide_ref, narrow_ref):
        for cidx in range(8 * w // LANES):
            j0 = cidx * LANES + (0 if cidx * LANES < 4 * w else 2 * nh)
            pieces = native(g_ref, j0, j0 + LANES)
            wide_ref[:, cidx * LANES:(cidx + 1) * LANES] = pieces[0] if len(pieces) == 1 else jnp.concatenate(pieces, axis=1)
        pieces = (native(g_ref, 4 * w, 4 * w + 2 * nh) + native(g_ref, 8 * w + 2 * nh, 8 * w + 3 * nh)
                  + [jnp.zeros((tr, LANES - 3 * nh), wg.dtype)])
        narrow_ref[...] = jnp.concatenate(pieces, axis=1)

    return pl.pallas_call(
        body, grid=(d // tr,), in_specs=[pl.BlockSpec((N_DEV, tr, cols), lambda i: (0, i, 0))],
        out_specs=[pl.BlockSpec((tr, 8 * w), lambda i: (i, 0)), pl.BlockSpec((tr, LANES), lambda i: (i, 0))],
        out_shape=[jax.ShapeDtypeStruct((d, 8 * w), wg.dtype), jax.ShapeDtypeStruct((d, LANES), wg.dtype)],
        name="relayout_w_in", compiler_params=_cp("parallel"))(wg)


def _relayout_dw_in(dwide, dnarrow, nh, tr):
    d = dwide.shape[0]
    w = nh * HEAD_DIM
    cols = (8 * w + 3 * nh) // N_DEV
    segs = _native_segments(nh)

    def body(wide_ref, narrow_ref, o_ref):
        refs = (wide_ref, narrow_ref)
        for blk in range(N_DEV):
            pieces = []
            for s0, s1, tgt, t0 in segs:
                lo, hi = max(s0, blk * cols), min(s1, (blk + 1) * cols)
                if lo < hi:
                    pieces.append(refs[tgt][:, pl.ds(t0 + lo - s0, hi - lo)])
            o_ref[blk] = pieces[0] if len(pieces) == 1 else jnp.concatenate(pieces, axis=1)

    return pl.pallas_call(
        body, grid=(d // tr,),
        in_specs=[pl.BlockSpec((tr, 8 * w), lambda i: (i, 0)), pl.BlockSpec((tr, LANES), lambda i: (i, 0))],
        out_specs=pl.BlockSpec((N_DEV, tr, cols), lambda i: (0, i, 0)),
        out_shape=jax.ShapeDtypeStruct((N_DEV, d, cols), dwide.dtype),
        name="relayout_dw_in", compiler_params=_cp("parallel"))(dwide, dnarrow)


def _adamw(w, parts, m, v, tr, name):
    r, c = w.shape

    def body(w_ref, p_ref, m_ref, v_ref, g_ref, d_ref, nm_ref, nv_ref):
        g = p_ref[0].astype(F32)
        for s in range(1, N_DEV):
            g = g + p_ref[s].astype(F32)
        m_new = ADAM_B1 * m_ref[...] + (1.0 - ADAM_B1) * g
        v_new = ADAM_B2 * v_ref[...] + (1.0 - ADAM_B2) * (g * g)
        m_hat = m_new / (1.0 - ADAM_B1 ** ADAM_STEP)
        v_hat = v_new / (1.0 - ADAM_B2 ** ADAM_STEP)
        g_ref[...] = g
        d_ref[...] = -ADAM_LR * (m_hat / (jnp.sqrt(v_hat) + ADAM_EPS) + ADAM_WD * w_ref[...])
        nm_ref[...] = m_new
        nv_ref[...] = v_new

    blk = pl.BlockSpec((tr, c), lambda i: (i, 0))
    return pl.pallas_call(
        body, grid=(r // tr,), in_specs=[blk, pl.BlockSpec((N_DEV, tr, c), lambda i: (0, i, 0)), blk, blk],
        out_specs=[blk] * 4, out_shape=[jax.ShapeDtypeStruct((r, c), F32)] * 4, name=name,
        compiler_params=_cp("parallel"))(w, parts, m, v)


def _pack_small(d, pre, post, a_log, dt_bias, f_bias, gdn_w, fq_w, fk_w, extra):
    row2 = jnp.concatenate([a_log, dt_bias, f_bias, gdn_w, fq_w, fk_w, extra], axis=1)
    row2 = jnp.pad(row2, ((0, 0), (0, d - row2.shape[1])))
    return jnp.concatenate([pre, post, row2, jnp.zeros((5, d), F32)], axis=0)


def _unpack_small(p, nh):
    o = 3 * nh
    return dict(pre=p[0:1], post=p[1:2], a_log=p[2:3, 0:nh], dt_bias=p[2:3, nh:2 * nh], f_bias=p[2:3, 2 * nh:o],
                gdn_w=p[2:3, o:o + HEAD_DIM], fq_w=p[2:3, o + HEAD_DIM:o + 2 * HEAD_DIM],
                fk_w=p[2:3, o + 2 * HEAD_DIM:o + 3 * HEAD_DIM], extra=p[2, o + 3 * HEAD_DIM])


def kernel(x, meta_tokens, pre_norm_w, w_in, conv_w, a_log, dt_bias, gdn_norm_w, fox_q_norm_w, fox_k_norm_w, fox_f_bias, w_out, post_norm_w, loss_target, m_meta_tokens, m_pre_norm_w, m_w_in, m_conv_w, m_a_log, m_dt_bias, m_gdn_norm_w, m_fox_q_norm_w, m_fox_k_norm_w, m_fox_f_bias, m_w_out, m_post_norm_w, v_meta_tokens, v_pre_norm_w, v_w_in, v_conv_w, v_a_log, v_dt_bias, v_gdn_norm_w, v_fox_q_norm_w, v_fox_k_norm_w, v_fox_f_bias, v_w_out, v_post_norm_w):
    nh = a_log.shape[1]
    d = x.shape[-1]
    w = nh * HEAD_DIM
    zero = jnp.zeros((1, 1), F32)

    wg, wog, cg, mg = _exchange(
        [_cast_bf16(w_in[0], 256, "cast_w_in"), _cast_bf16(w_out[0], 256, "cast_w_out"), conv_w[0], meta_tokens],
        ["gather"] * 4, "gather_weights")
    wp, wsm = _relayout_w_in(wg, nh, 256)
    meta_full = mg.transpose(1, 0, 2).reshape(N_META, d)
    g = _layer_grads(x[0], loss_target[0], meta_full, pre_norm_w, wp, wsm, cg.reshape(3 * w, CONV_WIDTH), a_log, dt_bias,
                     gdn_norm_w, fox_q_norm_w, fox_k_norm_w, fox_f_bias, wog.reshape(2 * w, d), post_norm_w)

    small = _pack_small(d, g["pre_w"], g["post_w"], g["a_log"], g["dt_bias"], g["f_bias"], g["gdn_norm_w"], g["fq_w"],
                        g["fk_w"], g["loss"])
    cols = meta_tokens.shape[1]
    p_in, p_out, p_conv, p_meta, p_small = _exchange(
        [_relayout_dw_in(g["wp"], g["wsm"], nh, 128), g["w_out"].reshape(N_DEV, 2 * w // N_DEV, d),
         g["conv_w"].reshape(N_DEV, 3 * w // N_DEV, CONV_WIDTH), g["meta"].reshape(N_META, N_DEV, cols).transpose(1, 0, 2),
         small], ["scatter"] * 4 + ["gather"], "exchange_grads")

    r_in = _adamw(w_in[0], p_in, m_w_in[0], v_w_in[0], 128, "adamw_w_in")
    r_out = _adamw(w_out[0], p_out, m_w_out[0], v_w_out[0], 64, "adamw_w_out")
    r_conv = _adamw(conv_w[0], p_conv, m_conv_w[0], v_conv_w[0], conv_w.shape[1], "adamw_conv_w")
    r_meta = _adamw(meta_tokens, p_meta, m_meta_tokens, v_meta_tokens, N_META, "adamw_meta")
    pk = lambda pre, post, a, dt, gw, fq, fk, fb: _pack_small(d, pre, post, a, dt, fb, gw, fq, fk, zero)
    r_small = _adamw(
        pk(pre_norm_w, post_norm_w, a_log, dt_bias, gdn_norm_w, fox_q_norm_w, fox_k_norm_w, fox_f_bias), p_small,
        pk(m_pre_norm_w, m_post_norm_w, m_a_log, m_dt_bias, m_gdn_norm_w, m_fox_q_norm_w, m_fox_k_norm_w, m_fox_f_bias),
        pk(v_pre_norm_w, v_post_norm_w, v_a_log, v_dt_bias, v_gdn_norm_w, v_fox_q_norm_w, v_fox_k_norm_w, v_fox_f_bias),
        8, "adamw_small")

    sm = [_unpack_small(r, nh) for r in r_small]
    outs = []
    for i in range(4):
        s = sm[i]
        outs += [r_meta[i], s["pre"], r_in[i][None], r_conv[i][None], s["a_log"], s["dt_bias"], s["gdn_w"], s["fq_w"],
                 s["fk_w"], s["f_bias"], r_out[i][None], s["post"]]
    return (sm[0]["extra"], g["grad_x"][None], *outs)
```

```python
import jax
import jax.numpy as jnp
from jax import lax
from jax.experimental import pallas as pl
from jax.experimental.pallas import tpu as pltpu

F32, BF16 = jnp.float32, jnp.bfloat16
HEAD_DIM = 128
N_META = 16
CONV_WIDTH = 4
CHUNK = 64
Q_BLOCK = 128
LANES = 128
EPS = 1e-6
PAD_ROWS = Q_BLOCK - N_META
N_DEV = 8
VMEM_LIMIT = 56 * 1024 * 1024
NEG = -1e30

ADAM_LR, ADAM_B1, ADAM_B2, ADAM_EPS, ADAM_WD, ADAM_STEP = 0.001, 0.9, 0.999, 1e-08, 0.01, 10

_DN = {"nn": (((1,), (0,)), ((), ())), "nt": (((1,), (1,)), ((), ())), "tn": (((0,), (0,)), ((), ()))}


def _cp(*sem):
    return pltpu.CompilerParams(dimension_semantics=sem, vmem_limit_bytes=VMEM_LIMIT)


def _dot(a, b, dims="nn", prec=None):
    return lax.dot_general(a, b, _DN[dims], precision=prec, preferred_element_type=F32)


def _bdot(a, b, dims="nn"):
    return _dot(a.astype(BF16), b.astype(BF16), dims)


def _hdot(a, b, dims="nn"):
    return _dot(a, b, dims, prec=lax.Precision.HIGHEST)


def _iota(shape, dim):
    return lax.broadcasted_iota(jnp.int32, shape, dim)


def _sigmoid(z):
    return 1.0 / (1.0 + jnp.exp(-z))


def _softplus(z):
    e = jnp.exp(-jnp.abs(z))
    u = 1.0 + e
    l1p = jnp.where(u == 1.0, e, jnp.log(u) * (e / jnp.where(u == 1.0, 1.0, u - 1.0)))
    return jnp.maximum(z, 0.0) + l1p


def _silu_and_grad(z):
    s = _sigmoid(z)
    return z * s, s * (1.0 + z * (1.0 - s))


def _prenorm(h, w):
    lp, d = h.shape

    def body(h_ref, w_ref, o_ref):
        x = h_ref[...]
        r = lax.rsqrt(jnp.mean(x * x, axis=-1, keepdims=True) + EPS)
        o_ref[...] = (x * r * w_ref[...]).astype(BF16)

    return pl.pallas_call(
        body, grid=(lp // Q_BLOCK,),
        in_specs=[pl.BlockSpec((Q_BLOCK, d), lambda i: (i, 0)), pl.BlockSpec((1, d), lambda i: (0, 0))],
        out_specs=pl.BlockSpec((Q_BLOCK, d), lambda i: (i, 0)),
        out_shape=jax.ShapeDtypeStruct((lp, d), BF16), name="prenorm", compiler_params=_cp("parallel"))(h, w)


def _matmul(a, b, dims, tn, out_dtype, name):
    m = a.shape[1] if dims == "tn" else a.shape[0]
    n = b.shape[0] if dims == "nt" else b.shape[1]
    kdim = b.shape[1] if dims == "nt" else b.shape[0]
    tn = min(tn, n)
    b_spec = pl.BlockSpec((tn, kdim), lambda j: (j, 0)) if dims == "nt" else pl.BlockSpec((kdim, tn), lambda j: (0, j))

    def body(a_ref, b_ref, o_ref):
        o_ref[...] = _dot(a_ref[...], b_ref[...], dims).astype(out_dtype)

    return pl.pallas_call(
        body, grid=(n // tn,),
        in_specs=[pl.BlockSpec(a.shape, lambda j: (0, 0)), b_spec],
        out_specs=pl.BlockSpec((m, tn), lambda j: (0, j)),
        out_shape=jax.ShapeDtypeStruct((m, n), out_dtype), name=name, compiler_params=_cp("parallel"))(a, b)


def _matmul_nt_ktiled(a, b, tk, name):
    m, k = a.shape
    n = b.shape[0]

    def body(a_ref, b_ref, o_ref):
        @pl.when(pl.program_id(0) == 0)
        def _():
            o_ref[...] = jnp.zeros_like(o_ref)
        o_ref[...] += _dot(a_ref[...], b_ref[...], "nt")

    return pl.pallas_call(
        body, grid=(k // tk,),
        in_specs=[pl.BlockSpec((m, tk), lambda j: (0, j)), pl.BlockSpec((n, tk), lambda j: (0, j))],
        out_specs=pl.BlockSpec((m, n), lambda j: (0, 0)),
        out_shape=jax.ShapeDtypeStruct((m, n), F32), name=name, compiler_params=_cp("arbitrary"))(a, b)


def _conv_taps(x, w):
    c = x * w[CONV_WIDTH - 1:CONV_WIDTH, :]
    for j in range(CONV_WIDTH - 1):
        c = c + pltpu.roll(x, CONV_WIDTH - 1 - j, 0) * w[j:j + 1, :]
    return c


def _gdn_prep(proj, conv_wt, nh):
    lp = proj.shape[0]
    scale = HEAD_DIM ** -0.5

    def body(x_ref, w_ref, o_ref):
        which = pl.program_id(0) // nh
        c = _conv_taps(x_ref[...], w_ref[...])
        s = c * _sigmoid(c)
        r = lax.rsqrt(jnp.sum(s * s, axis=-1, keepdims=True) + EPS)
        f = jnp.where(which == 0, r * scale, jnp.where(which == 1, r, 1.0))
        o_ref[...] = jnp.where(_iota(s.shape, 0) >= PAD_ROWS, s * f, 0.0)

    return pl.pallas_call(
        body, grid=(3 * nh,),
        in_specs=[pl.BlockSpec((lp, LANES), lambda s: (0, s)), pl.BlockSpec((CONV_WIDTH, LANES), lambda s: (0, s))],
        out_specs=pl.BlockSpec((lp, LANES), lambda s: (0, s)),
        out_shape=jax.ShapeDtypeStruct((lp, 3 * nh * HEAD_DIM), F32), name="gdn_prep",
        compiler_params=_cp("parallel"))(proj, conv_wt)


def _gdn_prep_bwd(proj, conv_wt, dqkv, nh):
    lp = proj.shape[0]
    scale = HEAD_DIM ** -0.5

    def body(x_ref, w_ref, dy_ref, dx_ref, dw_ref):
        which = pl.program_id(0) // nh
        x = x_ref[...]
        w = w_ref[...]
        c = _conv_taps(x, w)
        sg = _sigmoid(c)
        s = c * sg
        r = lax.rsqrt(jnp.sum(s * s, axis=-1, keepdims=True) + EPS)
        live = _iota(s.shape, 0) >= PAD_ROWS
        dy = jnp.where(live, dy_ref[...], 0.0)
        y0 = s * r
        dy0 = dy * jnp.where(which == 0, scale, 1.0)
        ds_n = r * (dy0 - y0 * jnp.sum(dy0 * y0, axis=-1, keepdims=True))
        ds = jnp.where(which == 2, dy, ds_n)
        dc = ds * (sg * (1.0 + c * (1.0 - sg)))
        dx = dc * w[CONV_WIDTH - 1:CONV_WIDTH, :]
        rows = [jnp.sum(dc * x, axis=0, keepdims=True)]
        for j in range(CONV_WIDTH - 2, -1, -1):
            sh = CONV_WIDTH - 1 - j
            dx = dx + pltpu.roll(dc, lp - sh, 0) * w[j:j + 1, :]
            rows.insert(0, jnp.sum(dc * pltpu.roll(x, sh, 0), axis=0, keepdims=True))
        dx_ref[...] = dx.astype(BF16)
        dw_ref[...] = jnp.concatenate(rows, axis=0)

    return pl.pallas_call(
        body, grid=(3 * nh,),
        in_specs=[pl.BlockSpec((lp, LANES), lambda s: (0, s)), pl.BlockSpec((CONV_WIDTH, LANES), lambda s: (0, s)),
                  pl.BlockSpec((lp, LANES), lambda s: (0, s))],
        out_specs=[pl.BlockSpec((lp, LANES), lambda s: (0, s)), pl.BlockSpec((CONV_WIDTH, LANES), lambda s: (0, s))],
        out_shape=[jax.ShapeDtypeStruct((lp, 3 * nh * HEAD_DIM), BF16),
                   jax.ShapeDtypeStruct((CONV_WIDTH, 3 * nh * HEAD_DIM), F32)],
        name="gdn_prep_bwd", compiler_params=_cp("parallel"))(proj, conv_wt, dqkv)


def _gates(proj_sm, bias_row, nega_row, nh):
    lp = proj_sm.shape[0]
    nc = lp // CHUNK

    def body(p_ref, b_ref, a_ref, g_ref, gt3_ref, gtf_ref):
        lane = _iota((CHUNK, LANES), 1)
        tri = (_iota((CHUNK, CHUNK), 0) >= _iota((CHUNK, CHUNK), 1)).astype(F32)

        def step(n, carry):
            r0 = pl.multiple_of(n * CHUNK, CHUNK)
            z = p_ref[pl.ds(r0, CHUNK), :] + b_ref[...]
            base = jnp.where(lane < nh, _sigmoid(z),
                             jnp.where(lane < 2 * nh, a_ref[...] * _softplus(z),
                                       jnp.where(lane < 3 * nh, -_softplus(-z), 0.0)))
            base = jnp.where(r0 + _iota((CHUNK, LANES), 0) >= PAD_ROWS, base, 0.0)
            cs = _hdot(tri, base)
            run = jnp.where((lane >= 2 * nh) & (lane < 3 * nh), cs + carry, cs)
            sh = pltpu.roll(run, 2 * nh, 1)
            out = base + jnp.where((lane >= 3 * nh) & (lane < 5 * nh), sh, 0.0)
            g_ref[pl.ds(r0, CHUNK), :] = out
            gt3_ref[n] = out.T
            return carry + cs[CHUNK - 1:CHUNK, :]

        lax.fori_loop(0, nc, step, jnp.zeros((1, LANES), F32))
        gtf_ref[...] = g_ref[...].T

    return pl.pallas_call(
        body,
        out_shape=[jax.ShapeDtypeStruct((lp, LANES), F32), jax.ShapeDtypeStruct((nc, LANES, CHUNK), F32),
                   jax.ShapeDtypeStruct((LANES, lp), F32)],
        name="gates", compiler_params=pltpu.CompilerParams(vmem_limit_bytes=VMEM_LIMIT))(proj_sm, bias_row, nega_row)


def _gates_bwd(proj_sm, bias_row, nega_row, gates, dgate_gdn, dc_t, nh):
    lp = proj_sm.shape[0]
    nc = lp // CHUNK

    def body(p_ref, b_ref, a_ref, g_ref, dg_ref, dc_ref, dz_ref, sm_ref, dct_scr):
        lane = _iota((CHUNK, LANES), 1)
        triu = (_iota((CHUNK, CHUNK), 0) <= _iota((CHUNK, CHUNK), 1)).astype(F32)
        dct_scr[...] = dc_ref[...].T
        sm_ref[...] = jnp.zeros_like(sm_ref)

        def step(i, carry):
            n = nc - 1 - i
            r0 = pl.multiple_of(n * CHUNK, CHUNK)
            z = p_ref[pl.ds(r0, CHUNK), :] + b_ref[...]
            gt = g_ref[pl.ds(r0, CHUNK), :]
            dgd = dg_ref[pl.ds(r0, CHUNK), :]
            dch = dct_scr[pl.ds(r0, CHUNK), :]
            rc = _hdot(triu, dch) + carry
            sg = _sigmoid(z)
            dz = jnp.where(lane < nh, dgd * sg * (1.0 - sg),
                           jnp.where(lane < 2 * nh, dgd * a_ref[...] * sg,
                                     jnp.where(lane < 3 * nh, rc * (1.0 - sg), 0.0)))
            dz = jnp.where(r0 + _iota((CHUNK, LANES), 0) >= PAD_ROWS, dz, 0.0)
            dz_ref[pl.ds(r0, CHUNK), :] = dz.astype(BF16)
            sm_ref[0:1, :] += jnp.sum(dz, axis=0, keepdims=True)
            sm_ref[1:2, :] += jnp.sum(jnp.where((lane >= nh) & (lane < 2 * nh), dgd * gt, 0.0), axis=0, keepdims=True)
            return carry + jnp.sum(dch, axis=0, keepdims=True)

        lax.fori_loop(0, nc, step, jnp.zeros((1, LANES), F32))

    return pl.pallas_call(
        body,
        out_shape=[jax.ShapeDtypeStruct((lp, LANES), BF16), jax.ShapeDtypeStruct((8, LANES), F32)],
        scratch_shapes=[pltpu.VMEM((lp, LANES), F32)],
        name="gates_bwd", compiler_params=pltpu.CompilerParams(vmem_limit_bytes=VMEM_LIMIT))(
            proj_sm, bias_row, nega_row, gates, dgate_gdn, dc_t)


def _tri_inv(a):
    t = jnp.where(_iota(a.shape, 0) == _iota(a.shape, 1), 1.0, 0.0) - a
    p = a
    for _ in range(5):
        p = _hdot(p, p)
        t = t + _hdot(t, p)
    return t


def _gdn_chunk(q, k, v, beta, gc, gr):
    ii, jj = _iota((CHUNK, CHUNK), 0), _iota((CHUNK, CHUNK), 1)
    causal, strict = ii >= jj, ii > jj
    dm = jnp.where(causal, jnp.exp(jnp.where(causal, gc - gr, 0.0)), 0.0)
    kk = _bdot(k, k, "nt")
    a = jnp.where(strict, beta * kk * dm, 0.0)
    t = _tri_inv(a)
    eg = jnp.exp(gc)
    glast = gc[CHUNK - 1:CHUNK, :]
    ekd = jnp.exp(glast - gc)
    bv = beta * v
    bk = (beta * eg) * k
    ub = _bdot(t, jnp.concatenate([bv, bk], axis=1))
    qk = _bdot(q, k, "nt")
    return dict(causal=causal, strict=strict, dm=dm, kk=kk, a=a, t=t, eg=eg, ekd=ekd, bv=bv, bk=bk,
                u=ub[:, :HEAD_DIM], w=ub[:, HEAD_DIM:], qk=qk, aqk=jnp.where(causal, qk * dm, 0.0),
                q_dec=q * eg, k_dec=k * ekd, decay=jnp.exp(glast))


def _gdn_fwd(qkv, gates, gt3, nh):
    lp = qkv.shape[0]
    nc = lp // CHUNK
    w = nh * HEAD_DIM

    def body(q_ref, k_ref, v_ref, g_ref, gt_ref, o_ref, sall_ref, s_scr):
        @pl.when(pl.program_id(0) == 0)
        def _():
            s_scr[...] = jnp.zeros_like(s_scr)
        g = g_ref[...]
        gt = gt_ref[0]
        for h in range(nh):
            sl = slice(h * HEAD_DIM, (h + 1) * HEAD_DIM)
            c = _gdn_chunk(q_ref[:, sl], k_ref[:, sl], v_ref[:, sl], g[:, h:h + 1],
                           g[:, 3 * nh + h:3 * nh + h + 1], gt[3 * nh + h:3 * nh + h + 1, :])
            s = s_scr[h]
            sall_ref[0, h] = s
            v_new = c["u"] - _bdot(c["w"], s)
            o_ref[:, sl] = _bdot(c["q_dec"], s) + _bdot(c["aqk"], v_new)
            s_scr[h] = s * c["decay"] + _bdot(c["k_dec"], v_new, "tn")

    return pl.pallas_call(
        body, grid=(nc,),
        in_specs=[pl.BlockSpec((CHUNK, w), lambda n: (n, 0)), pl.BlockSpec((CHUNK, w), lambda n: (n, 1)),
                  pl.BlockSpec((CHUNK, w), lambda n: (n, 2)), pl.BlockSpec((CHUNK, LANES), lambda n: (n, 0)),
                  pl.BlockSpec((1, LANES, CHUNK), lambda n: (n, 0, 0))],
        out_specs=[pl.BlockSpec((CHUNK, w), lambda n: (n, 0)),
                   pl.BlockSpec((1, nh, HEAD_DIM, HEAD_DIM), lambda n: (n, 0, 0, 0))],
        out_shape=[jax.ShapeDtypeStruct((lp, w), F32), jax.ShapeDtypeStruct((nc, nh, HEAD_DIM, HEAD_DIM), F32)],
        scratch_shapes=[pltpu.VMEM((nh, HEAD_DIM, HEAD_DIM), F32)],
        name="gdn_fwd", compiler_params=_cp("arbitrary"))(qkv, qkv, qkv, gates, gt3)


def _gdn_bwd(qkv, gates, gt3, s_all, do, nh):
    lp = qkv.shape[0]
    nc = lp // CHUNK
    w = nh * HEAD_DIM
    rev = lambda n: nc - 1 - n

    def body(q_ref, k_ref, v_ref, g_ref, gt_ref, s_ref, do_ref, dq_ref, dk_ref, dv_ref, dg_ref, ds_scr):
        @pl.when(pl.program_id(0) == 0)
        def _():
            ds_scr[...] = jnp.zeros_like(ds_scr)
        g = g_ref[...]
        gt = gt_ref[0]
        lane = _iota((CHUNK, LANES), 1)
        ones = jnp.ones((CHUNK, LANES), F32)
        row = _iota((CHUNK, 1), 0)
        acc = jnp.zeros((CHUNK, LANES), F32)
        for h in range(nh):
            sl = slice(h * HEAD_DIM, (h + 1) * HEAD_DIM)
            q, k, v = q_ref[:, sl], k_ref[:, sl], v_ref[:, sl]
            beta = g[:, h:h + 1]
            c = _gdn_chunk(q, k, v, beta, g[:, 3 * nh + h:3 * nh + h + 1], gt[3 * nh + h:3 * nh + h + 1, :])
            s = s_ref[0, h]
            dsn = ds_scr[h]
            dout = do_ref[:, sl]
            v_new = c["u"] - _bdot(c["w"], s)
            dq_dec = _bdot(dout, s, "nt")
            daqk = jnp.where(c["causal"], _bdot(dout, v_new, "nt"), 0.0)
            dv_new = _bdot(c["aqk"], dout, "tn") + _bdot(c["k_dec"], dsn)
            dk_dec = _bdot(v_new, dsn, "nt")
            ddecay = jnp.sum(jnp.sum(dsn * s, axis=1, keepdims=True), axis=0, keepdims=True)
            dw = -_bdot(dv_new, s, "nt")
            ds_scr[h] = _bdot(c["q_dec"], dout, "tn") + c["decay"] * dsn - _bdot(c["w"], dv_new, "tn")
            duw = jnp.concatenate([dv_new, dw], axis=1)
            dt = _bdot(duw, jnp.concatenate([c["bv"], c["bk"]], axis=1), "nt")
            dbvk = _bdot(c["t"], duw, "tn")
            dbv, dbk = dbvk[:, :HEAD_DIM], dbvk[:, HEAD_DIM:]
            da = jnp.where(c["strict"], -_bdot(_bdot(c["t"], dt, "tn"), c["t"], "nt"), 0.0)
            dkk = da * beta * c["dm"]
            dqk = daqk * c["dm"]
            e = da * c["a"] + daqk * c["aqk"]
            dq_ref[:, sl] = dq_dec * c["eg"] + _bdot(dqk, k)
            dk_ref[:, sl] = (dk_dec * c["ekd"] + _bdot(dkk, k) + _bdot(dkk, k, "tn") + _bdot(dqk, q, "tn")
                             + (beta * c["eg"]) * dbk)
            dv_ref[:, sl] = beta * dbv
            rs = lambda x: jnp.sum(x, axis=1, keepdims=True)
            dbeta = rs(dbv * v) + c["eg"] * rs(dbk * k) + rs(da * c["kk"] * c["dm"])
            kd_term = rs(dk_dec * c["k_dec"])
            dg_cum = (rs(dq_dec * c["q_dec"]) - kd_term + rs(dbk * c["bk"]) + rs(e)
                      - _hdot(e, ones, "tn")[:, 0:1])
            last = jnp.sum(kd_term, axis=0, keepdims=True) + ddecay * c["decay"]
            dg_cum = dg_cum + jnp.where(row == CHUNK - 1, last, 0.0)
            acc = acc + jnp.where(lane == h, dbeta, 0.0) + jnp.where(lane == nh + h, dg_cum, 0.0)
        triu = (_iota((CHUNK, CHUNK), 0) <= _iota((CHUNK, CHUNK), 1)).astype(F32)
        dg_ref[...] = jnp.where(lane < nh, acc, _hdot(triu, acc))

    return pl.pallas_call(
        body, grid=(nc,),
        in_specs=[pl.BlockSpec((CHUNK, w), lambda n: (rev(n), 0)), pl.BlockSpec((CHUNK, w), lambda n: (rev(n), 1)),
                  pl.BlockSpec((CHUNK, w), lambda n: (rev(n), 2)), pl.BlockSpec((CHUNK, LANES), lambda n: (rev(n), 0)),
                  pl.BlockSpec((1, LANES, CHUNK), lambda n: (rev(n), 0, 0)),
                  pl.BlockSpec((1, nh, HEAD_DIM, HEAD_DIM), lambda n: (rev(n), 0, 0, 0)),
                  pl.BlockSpec((CHUNK, w), lambda n: (rev(n), 0))],
        out_specs=[pl.BlockSpec((CHUNK, w), lambda n: (rev(n), 0))] * 3 + [pl.BlockSpec((CHUNK, LANES), lambda n: (rev(n), 0))],
        out_shape=[jax.ShapeDtypeStruct((lp, w), F32)] * 3 + [jax.ShapeDtypeStruct((lp, LANES), F32)],
        scratch_shapes=[pltpu.VMEM((nh, HEAD_DIM, HEAD_DIM), F32)],
        name="gdn_bwd", compiler_params=_cp("arbitrary"))(qkv, qkv, qkv, gates, gt3, s_all, do)


def _merge_gdn(o_gdn, proj, norm_w, nh):
    lp = o_gdn.shape[0]

    def body(o_ref, z_ref, w_ref, m_ref):
        o = o_ref[...]
        r = lax.rsqrt(jnp.mean(o * o, axis=-1, keepdims=True) + EPS)
        z = z_ref[...]
        m_ref[...] = (o * r * w_ref[...] * (z * _sigmoid(z))).astype(BF16)

    return pl.pallas_call(
        body, grid=(nh,),
        in_specs=[pl.BlockSpec((lp, LANES), lambda s: (0, s)), pl.BlockSpec((lp, LANES), lambda s: (0, 3 * nh + s)),
                  pl.BlockSpec((1, LANES), lambda s: (0, 0))],
        out_specs=pl.BlockSpec((lp, LANES), lambda s: (0, s)),
        out_shape=jax.ShapeDtypeStruct((lp, nh * HEAD_DIM), BF16), name="merge_gdn",
        compiler_params=_cp("parallel"))(o_gdn, proj, norm_w)


def _merge_gdn_bwd(o_gdn, proj, norm_w, dmerged, nh):
    lp = o_gdn.shape[0]

    def body(o_ref, z_ref, w_ref, dm_ref, do_ref, dz_ref, dw_ref):
        o = o_ref[...]
        r = lax.rsqrt(jnp.mean(o * o, axis=-1, keepdims=True) + EPS)
        xh = o * r
        silu, dsilu = _silu_and_grad(z_ref[...])
        dm = dm_ref[...]
        dn = dm * silu
        dz_ref[...] = (dm * (xh * w_ref[...]) * dsilu).astype(BF16)
        dnw = dn * w_ref[...]
        do_ref[...] = r * (dnw - xh * jnp.mean(dnw * xh, axis=-1, keepdims=True))

        @pl.when(pl.program_id(0) == 0)
        def _():
            dw_ref[...] = jnp.zeros_like(dw_ref)
        dw_ref[...] += jnp.sum(dn * xh, axis=0, keepdims=True)

    w = nh * HEAD_DIM
    return pl.pallas_call(
        body, grid=(nh,),
        in_specs=[pl.BlockSpec((lp, LANES), lambda s: (0, s)), pl.BlockSpec((lp, LANES), lambda s: (0, 3 * nh + s)),
                  pl.BlockSpec((1, LANES), lambda s: (0, 0)), pl.BlockSpec((lp, LANES), lambda s: (0, s))],
        out_specs=[pl.BlockSpec((lp, LANES), lambda s: (0, s)), pl.BlockSpec((lp, LANES), lambda s: (0, s)),
                   pl.BlockSpec((1, LANES), lambda s: (0, 0))],
        out_shape=[jax.ShapeDtypeStruct((lp, w), F32), jax.ShapeDtypeStruct((lp, w), BF16),
                   jax.ShapeDtypeStruct((1, LANES), F32)],
        name="merge_gdn_bwd", compiler_params=_cp("arbitrary"))(o_gdn, proj, norm_w, dmerged)


def _fox_prep(proj, qk_w, nh):
    lp = proj.shape[0]

    def body(x_ref, w_ref, o_ref):
        x = x_ref[...]
        r = lax.rsqrt(jnp.mean(x * x, axis=-1, keepdims=True) + EPS)
        o_ref[...] = x * r * w_ref[0]

    return pl.pallas_call(
        body, grid=(2 * nh,),
        in_specs=[pl.BlockSpec((lp, LANES), lambda s: (0, 4 * nh + s)), pl.BlockSpec((1, 1, LANES), lambda s: (s // nh, 0, 0))],
        out_specs=pl.BlockSpec((lp, LANES), lambda s: (0, s)),
        out_shape=jax.ShapeDtypeStruct((lp, 2 * nh * HEAD_DIM), F32), name="fox_prep",
        compiler_params=_cp("parallel"))(proj, qk_w)


def _fox_prep_bwd(proj, qk_w, dqk, nh):
    lp = proj.shape[0]

    def body(x_ref, w_ref, dy_ref, dx_ref, dw_ref):
        x = x_ref[...]
        r = lax.rsqrt(jnp.mean(x * x, axis=-1, keepdims=True) + EPS)
        xh = x * r
        dy = dy_ref[...]
        dyw = dy * w_ref[0]
        dx_ref[...] = (r * (dyw - xh * jnp.mean(dyw * xh, axis=-1, keepdims=True))).astype(BF16)

        @pl.when(pl.program_id(0) % nh == 0)
        def _():
            dw_ref[...] = jnp.zeros_like(dw_ref)
        dw_ref[0] += jnp.sum(dy * xh, axis=0, keepdims=True)

    return pl.pallas_call(
        body, grid=(2 * nh,),
        in_specs=[pl.BlockSpec((lp, LANES), lambda s: (0, 4 * nh + s)), pl.BlockSpec((1, 1, LANES), lambda s: (s // nh, 0, 0)),
                  pl.BlockSpec((lp, LANES), lambda s: (0, s))],
        out_specs=[pl.BlockSpec((lp, LANES), lambda s: (0, s)), pl.BlockSpec((1, 1, LANES), lambda s: (s // nh, 0, 0))],
        out_shape=[jax.ShapeDtypeStruct((lp, 2 * nh * HEAD_DIM), BF16), jax.ShapeDtypeStruct((2, 1, LANES), F32)],
        name="fox_prep_bwd", compiler_params=_cp("arbitrary"))(proj, qk_w, dqk)


def _fox_probs(q, k, gates, crow, h, i, nh):
    kl = k.shape[0]
    lane = _iota((Q_BLOCK, LANES), 1)
    ct = jnp.sum(jnp.where(lane == 4 * nh + h, gates, 0.0), axis=1, keepdims=True)
    s = _bdot(q, k, "nt") * (HEAD_DIM ** -0.5) + (ct - crow)
    t = i * Q_BLOCK + _iota((Q_BLOCK, kl), 0)
    kp = _iota((Q_BLOCK, kl), 1)
    s = jnp.where((kp <= t) & ((kp >= PAD_ROWS) | (t < PAD_ROWS)), s, NEG)
    p = jnp.exp(s - jnp.max(s, axis=1, keepdims=True))
    return p / jnp.sum(p, axis=1, keepdims=True)


def _fox_specs(lp, nh):
    return [pl.BlockSpec((Q_BLOCK, LANES), lambda h, i: (i, h)),
            pl.BlockSpec((lp, LANES), lambda h, i: (0, nh + h)),
            pl.BlockSpec((lp, LANES), lambda h, i: (0, 6 * nh + h)),
            pl.BlockSpec((Q_BLOCK, LANES), lambda h, i: (i, 0)),
            pl.BlockSpec((LANES, lp), lambda h, i: (0, 0))]


def _fox_fwd(qkn, proj, gates, gtf, nh):
    lp = qkn.shape[0]

    def body(q_ref, k_ref, v_ref, g_ref, gt_ref, o_ref):
        h, i = pl.program_id(0), pl.program_id(1)
        for j in range(lp // Q_BLOCK):
            @pl.when(i == j)
            def _(j=j):
                kl = (j + 1) * Q_BLOCK
                p = _fox_probs(q_ref[...], k_ref[0:kl, :], g_ref[...], gt_ref[pl.ds(4 * nh + h, 1), :][:, 0:kl], h, j, nh)
                o_ref[...] = _bdot(p, v_ref[0:kl, :])

    return pl.pallas_call(
        body, grid=(nh, lp // Q_BLOCK), in_specs=_fox_specs(lp, nh),
        out_specs=pl.BlockSpec((Q_BLOCK, LANES), lambda h, i: (i, h)),
        out_shape=jax.ShapeDtypeStruct((lp, nh * HEAD_DIM), F32), name="fox_fwd",
        compiler_params=_cp("parallel", "parallel"))(qkn, qkn, proj, gates, gtf)


def _fox_bwd(qkn, proj, gates, gtf, do, nh):
    lp = qkn.shape[0]
    w = nh * HEAD_DIM
    scale = HEAD_DIM ** -0.5

    def body(q_ref, k_ref, v_ref, g_ref, gt_ref, do_ref, dq_ref, dk_ref, dv_ref, dc_ref):
        h, i = pl.program_id(0), pl.program_id(1)

        @pl.when(i == 0)
        def _():
            dk_ref[...] = jnp.zeros_like(dk_ref)
            dv_ref[...] = jnp.zeros_like(dv_ref)
            dc_ref[...] = jnp.zeros_like(dc_ref)
        for j in range(lp // Q_BLOCK):
            @pl.when(i == j)
            def _(j=j):
                kl = (j + 1) * Q_BLOCK
                q, k = q_ref[...], k_ref[0:kl, :]
                p = _fox_probs(q, k, g_ref[...], gt_ref[pl.ds(4 * nh + h, 1), :][:, 0:kl], h, j, nh)
                dout = do_ref[...]
                dp = _bdot(dout, v_ref[0:kl, :], "nt")
                ds = p * (dp - jnp.sum(p * dp, axis=1, keepdims=True))
                dq_ref[...] = _bdot(ds, k) * scale
                dk_ref[0:kl, :] += _bdot(ds, q, "tn") * scale
                dv_ref[0:kl, :] += _bdot(p, dout, "tn")
                dc_ref[0, :, 0:kl] -= jnp.sum(ds, axis=0, keepdims=True)

    blk = pl.BlockSpec((Q_BLOCK, LANES), lambda h, i: (i, h))
    col = pl.BlockSpec((lp, LANES), lambda h, i: (0, h))
    return pl.pallas_call(
        body, grid=(nh, lp // Q_BLOCK), in_specs=_fox_specs(lp, nh) + [blk],
        out_specs=[blk, col, col, pl.BlockSpec((1, 1, lp), lambda h, i: (h, 0, 0))],
        out_shape=[jax.ShapeDtypeStruct((lp, w), F32)] * 3 + [jax.ShapeDtypeStruct((nh, 1, lp), F32)],
        name="fox_bwd", compiler_params=_cp("parallel", "arbitrary"))(qkn, qkn, proj, gates, gtf, do)


def _merge_fox(o_fox, proj, nh):
    lp = o_fox.shape[0]

    def body(o_ref, z_ref, m_ref):
        z = z_ref[...]
        m_ref[...] = (o_ref[...] * (z * _sigmoid(z))).astype(BF16)

    return pl.pallas_call(
        body, grid=(nh,),
        in_specs=[pl.BlockSpec((lp, LANES), lambda s: (0, s)), pl.BlockSpec((lp, LANES), lambda s: (0, 7 * nh + s))],
        out_specs=pl.BlockSpec((lp, LANES), lambda s: (0, s)),
        out_shape=jax.ShapeDtypeStruct((lp, nh * HEAD_DIM), BF16), name="merge_fox",
        compiler_params=_cp("parallel"))(o_fox, proj)


def _merge_fox_bwd(o_fox, proj, dmerged, nh):
    lp = o_fox.shape[0]

    def body(o_ref, z_ref, dm_ref, do_ref, dz_ref):
        silu, dsilu = _silu_and_grad(z_ref[...])
        dm = dm_ref[...]
        do_ref[...] = dm * silu
        dz_ref[...] = (dm * o_ref[...] * dsilu).astype(BF16)

    w = nh * HEAD_DIM
    return pl.pallas_call(
        body, grid=(nh,),
        in_specs=[pl.BlockSpec((lp, LANES), lambda s: (0, s)), pl.BlockSpec((lp, LANES), lambda s: (0, 7 * nh + s)),
                  pl.BlockSpec((lp, LANES), lambda s: (0, nh + s))],
        out_specs=[pl.BlockSpec((lp, LANES), lambda s: (0, s)), pl.BlockSpec((lp, LANES), lambda s: (0, s))],
        out_shape=[jax.ShapeDtypeStruct((lp, w), F32), jax.ShapeDtypeStruct((lp, w), BF16)],
        name="merge_fox_bwd", compiler_params=_cp("parallel"))(o_fox, proj, dmerged)


def _post(out, h, target, post_w):
    lp, d = out.shape

    def body(o_ref, h_ref, t_ref, w_ref, dy_ref, do_ref, loss_ref, dw_ref):
        i = pl.program_id(0)

        @pl.when(i == 0)
        def _():
            loss_ref[...] = jnp.zeros_like(loss_ref)
            dw_ref[...] = jnp.zeros_like(dw_ref)
        o = o_ref[...]
        r = lax.rsqrt(jnp.mean(o * o, axis=-1, keepdims=True) + EPS)
        nrm = o * r
        err = jnp.where(i > 0, h_ref[...] + nrm * w_ref[...] - t_ref[...], 0.0)
        loss_ref[0:1, :] += 0.5 * jnp.sum(jnp.sum(err * err, axis=1, keepdims=True), axis=0, keepdims=True) / d
        dy = err / d
        dy_ref[...] = dy
        dw_ref[...] += jnp.sum(dy * nrm, axis=0, keepdims=True)
        dyw = dy * w_ref[...]
        do_ref[...] = (r * (dyw - nrm * jnp.mean(dyw * nrm, axis=-1, keepdims=True))).astype(BF16)

    row = pl.BlockSpec((Q_BLOCK, d), lambda i: (i, 0))
    vec = pl.BlockSpec((1, d), lambda i: (0, 0))
    return pl.pallas_call(
        body, grid=(lp // Q_BLOCK,),
        in_specs=[row, row, pl.BlockSpec((Q_BLOCK, d), lambda i: (jnp.maximum(i - 1, 0), 0)), vec],
        out_specs=[row, row, pl.BlockSpec((8, LANES), lambda i: (0, 0)), vec],
        out_shape=[jax.ShapeDtypeStruct((lp, d), F32), jax.ShapeDtypeStruct((lp, d), BF16),
                   jax.ShapeDtypeStruct((8, LANES), F32), jax.ShapeDtypeStruct((1, d), F32)],
        name="post", compiler_params=_cp("arbitrary"))(out, h, target, post_w)


def _prenorm_bwd(dxn, h, w, dy):
    lp, d = h.shape

    def body(dx_ref, h_ref, w_ref, dy_ref, dh_ref, dw_ref):
        @pl.when(pl.program_id(0) == 0)
        def _():
            dw_ref[...] = jnp.zeros_like(dw_ref)
        x = h_ref[...]
        r = lax.rsqrt(jnp.mean(x * x, axis=-1, keepdims=True) + EPS)
        xh = x * r
        dxn_ = dx_ref[...]
        dw_ref[...] += jnp.sum(dxn_ * xh, axis=0, keepdims=True)
        dxw = dxn_ * w_ref[...]
        dh_ref[...] = dy_ref[...] + r * (dxw - xh * jnp.mean(dxw * xh, axis=-1, keepdims=True))

    row = pl.BlockSpec((Q_BLOCK, d), lambda i: (i, 0))
    vec = pl.BlockSpec((1, d), lambda i: (0, 0))
    return pl.pallas_call(
        body, grid=(lp // Q_BLOCK,), in_specs=[row, row, vec, row], out_specs=[row, vec],
        out_shape=[jax.ShapeDtypeStruct((lp, d), F32), jax.ShapeDtypeStruct((1, d), F32)],
        name="prenorm_bwd", compiler_params=_cp("arbitrary"))(dxn, h, w, dy)


def _layer_grads(x, target, meta, pre_w, wp, wsm, conv_w, a_log, dt_bias, gdn_norm_w, fq_w, fk_w, f_bias,
                 w_out, post_w):
    seq, d = x.shape
    nh = a_log.shape[1]
    w = nh * HEAD_DIM
    h = jnp.concatenate([jnp.zeros((PAD_ROWS, d), F32), meta, x], axis=0)
    conv_wt = conv_w.T
    zpad = jnp.zeros((1, LANES - 3 * nh), F32)
    bias_row = jnp.concatenate([jnp.zeros((1, nh), F32), dt_bias, f_bias, zpad], axis=1)
    nega_row = jnp.concatenate([jnp.zeros((1, nh), F32), -jnp.exp(a_log), jnp.zeros((1, nh), F32), zpad], axis=1)
    qk_w = jnp.stack([fq_w, fk_w])

    xn = _prenorm(h, pre_w)
    proj = _matmul(xn, wp, "nn", 512, F32, "proj")
    proj_sm = _matmul(xn, wsm, "nn", LANES, F32, "proj_sm")
    qkv = _gdn_prep(proj, conv_wt, nh)
    gates, gt3, gtf = _gates(proj_sm, bias_row, nega_row, nh)
    o_gdn, s_all = _gdn_fwd(qkv, gates, gt3, nh)
    qkn = _fox_prep(proj, qk_w, nh)
    o_fox = _fox_fwd(qkn, proj, gates, gtf, nh)
    merged = jnp.concatenate([_merge_gdn(o_gdn, proj, gdn_norm_w, nh), _merge_fox(o_fox, proj, nh)], axis=1)
    out = _matmul(merged, w_out, "nn", 512, F32, "out_proj")
    dy, dout, loss_blk, dpost_w = _post(out, h, target, post_w)

    dw_out = _matmul(merged, dout, "tn", 512, BF16, "dw_out")
    dmerged = _matmul(dout, w_out, "nt", 512, F32, "dmerged")
    do_gdn, dgz, dgdn_norm_w = _merge_gdn_bwd(o_gdn, proj, gdn_norm_w, dmerged, nh)
    do_fox, dfg = _merge_fox_bwd(o_fox, proj, dmerged, nh)
    dqn, dkn, dfv, dc_t = _fox_bwd(qkn, proj, gates, gtf, do_fox, nh)
    dfqk, dqk_w = _fox_prep_bwd(proj, qk_w, jnp.concatenate([dqn, dkn], axis=1), nh)
    dgq, dgk, dgv, dgate = _gdn_bwd(qkv, gates, gt3, s_all, do_gdn, nh)
    dqkv_n = jnp.concatenate([dgq, dgk, dgv], axis=1)
    dqkv, dconv_wt = _gdn_prep_bwd(proj, conv_wt, dqkv_n, nh)
    dc_rows = jnp.pad(dc_t.reshape(nh, -1), ((2 * nh, LANES - 3 * nh), (0, 0)))
    dsm, gate_sums = _gates_bwd(proj_sm, bias_row, nega_row, gates, dgate, dc_rows, nh)
    dproj = jnp.concatenate([dqkv, dgz, dfqk, dfv.astype(BF16), dfg], axis=1)
    dwp = _matmul(xn, dproj, "tn", 512, BF16, "dw_in")
    dwsm = _matmul(xn, dsm, "tn", LANES, BF16, "dw_in_sm")
    dxn = _matmul_nt_ktiled(dproj, wp, 512, "dxn") + _matmul(dsm, wsm, "nt", 512, F32, "dxn_sm")
    dh, dpre_w = _prenorm_bwd(dxn, h, pre_w, dy)
    return dict(
        loss=loss_blk[0:1, 0:1], grad_x=dh[Q_BLOCK:], meta=dh[PAD_ROWS:Q_BLOCK], pre_w=dpre_w, wp=dwp, wsm=dwsm,
        conv_w=dconv_wt.T, a_log=gate_sums[1:2, nh:2 * nh], dt_bias=gate_sums[0:1, nh:2 * nh],
        gdn_norm_w=dgdn_norm_w, fq_w=dqk_w[0], fk_w=dqk_w[1], f_bias=gate_sums[0:1, 2 * nh:3 * nh],
        w_out=dw_out, post_w=dpost_w)


def _cast_bf16(a, tr, name):
    r, c = a.shape

    def body(a_ref, o_ref):
        o_ref[...] = a_ref[...].astype(BF16)

    return pl.pallas_call(
        body, grid=(r // tr,), in_specs=[pl.BlockSpec((tr, c), lambda i: (i, 0))],
        out_specs=pl.BlockSpec((tr, c), lambda i: (i, 0)), out_shape=jax.ShapeDtypeStruct((r, c), BF16),
        name=name, compiler_params=_cp("parallel"))(a)


N_CHIP = 4
_ANY = pl.BlockSpec(memory_space=pl.ANY)
_MESH = pl.DeviceIdType.MESH


def _all_gather(arrays, name):
    n = len(arrays)

    def body(*refs):
        ins, outs = refs[:n], refs[n:2 * n]
        send_sems, recv_sems, local_sems = refs[2 * n:]
        x, y, c = lax.axis_index("x"), lax.axis_index("y"), lax.axis_index("c")
        sibling = (x, y, 1 - c)
        chips = [(1 - x, y), (x, 1 - y), (1 - x, 1 - y)]

        def row(a, px, py, pc):
            return outs[a].at[4 * px + 2 * py + pc]

        def copy(a, k, block, to, src=None):
            return pltpu.make_async_remote_copy(
                src_ref=row(a, *block) if src is None else src, dst_ref=row(a, *block), send_sem=send_sems.at[a, k],
                recv_sem=recv_sems.at[a, k], device_id=to, device_id_type=_MESH)

        local, sends = [], []
        for a in range(n):
            local.append(pltpu.make_async_copy(ins[a], row(a, x, y, c), local_sems.at[a]))
            sends.append(copy(a, 0, (x, y, c), sibling, src=ins[a]))
            sends += [copy(a, 1 + j, (x, y, c), (*chip, c), src=ins[a]) for j, chip in enumerate(chips)]
        for cp in local + sends:
            cp.start()
        for j, chip in enumerate(chips):
            for a in range(n):
                copy(a, 1 + j, (*chip, c), (x, y, c)).wait_recv()
                fwd = copy(a, 4 + j, (*chip, c), sibling)
                fwd.start()
                sends.append(fwd)
        for a in range(n):
            copy(a, 0, sibling, (x, y, c)).wait_recv()
            for j, chip in enumerate(chips):
                copy(a, 4 + j, (*chip, 1 - c), (x, y, c)).wait_recv()
        for cp in sends:
            cp.wait_send()
        for cp in local:
            cp.wait()

    return pl.pallas_call(
        body, in_specs=[_ANY] * n, out_specs=[_ANY] * n,
        out_shape=[jax.ShapeDtypeStruct((N_DEV,) + a.shape, a.dtype) for a in arrays],
        scratch_shapes=[pltpu.SemaphoreType.DMA((n, N_DEV - 1)), pltpu.SemaphoreType.DMA((n, N_DEV - 1)),
                        pltpu.SemaphoreType.DMA((n,))],
        name=name)(*arrays)


def _pair_exchange(arrays, name):
    n = len(arrays)

    def body(*refs):
        ins, outs = refs[:n], refs[n:2 * n]
        send_sems, recv_sems = refs[2 * n:]
        x, y, c = lax.axis_index("x"), lax.axis_index("y"), lax.axis_index("c")
        copies = []
        for a in range(n):
            for q in range(N_CHIP):
                copies.append(pltpu.make_async_remote_copy(
                    src_ref=ins[a].at[2 * q + 1 - c], dst_ref=outs[a].at[q], send_sem=send_sems.at[a, q],
                    recv_sem=recv_sems.at[a, q], device_id=(x, y, 1 - c), device_id_type=_MESH))
        for cp in copies:
            cp.start()
        for cp in copies:
            cp.wait()

    return pl.pallas_call(
        body, in_specs=[_ANY] * n, out_specs=[_ANY] * n,
        out_shape=[jax.ShapeDtypeStruct((N_CHIP,) + a.shape[1:], a.dtype) for a in arrays],
        scratch_shapes=[pltpu.SemaphoreType.DMA((n, N_CHIP)), pltpu.SemaphoreType.DMA((n, N_CHIP))],
        name=name)(*arrays)


def _pair_sum(parts, got, core, tr, name):
    _, r, c = parts.shape

    def body(core_ref, p_ref, g_ref, o_ref):
        o_ref[...] = (p_ref[...].astype(F32) + g_ref[...].astype(F32)).astype(o_ref.dtype)

    return pl.pallas_call(
        body,
        grid_spec=pltpu.PrefetchScalarGridSpec(
            num_scalar_prefetch=1, grid=(N_CHIP, r // tr),
            in_specs=[pl.BlockSpec((1, tr, c), lambda q, i, core_ref: (2 * q + core_ref[0], i, 0)),
                      pl.BlockSpec((1, tr, c), lambda q, i, core_ref: (q, i, 0))],
            out_specs=pl.BlockSpec((1, tr, c), lambda q, i, core_ref: (q, i, 0))),
        out_shape=jax.ShapeDtypeStruct((N_CHIP, r, c), parts.dtype), name=name,
        compiler_params=_cp("parallel", "parallel"))(core, parts, got)


def _chip_exchange(arrays, name):
    n = len(arrays)

    def body(*refs):
        ins, outs = refs[:n], refs[n:2 * n]
        send_sems, recv_sems, local_sems = refs[2 * n:]
        x, y, c = lax.axis_index("x"), lax.axis_index("y"), lax.axis_index("c")
        mine = 2 * x + y
        local, sends, recvs = [], [], []
        for a in range(n):
            local.append(pltpu.make_async_copy(ins[a].at[mine], outs[a].at[mine], local_sems.at[a]))
            for k in range(1, N_CHIP):
                px = 1 - x if k & 2 else x
                py = 1 - y if k & 1 else y
                kw = dict(send_sem=send_sems.at[a, k - 1], recv_sem=recv_sems.at[a, k - 1], device_id=(px, py, c),
                          device_id_type=_MESH)
                sends.append(pltpu.make_async_remote_copy(src_ref=ins[a].at[2 * px + py], dst_ref=outs[a].at[mine], **kw))
                recvs.append(pltpu.make_async_remote_copy(src_ref=ins[a].at[mine], dst_ref=outs[a].at[2 * px + py], **kw))
        for cp in local + sends:
            cp.start()
        for cp in local:
            cp.wait()
        for cp in sends:
            cp.wait_send()
        for cp in recvs:
            cp.wait_recv()

    return pl.pallas_call(
        body, in_specs=[_ANY] * n, out_specs=[_ANY] * n,
        out_shape=[jax.ShapeDtypeStruct(a.shape, a.dtype) for a in arrays],
        scratch_shapes=[pltpu.SemaphoreType.DMA((n, N_CHIP - 1)), pltpu.SemaphoreType.DMA((n, N_CHIP - 1)),
                        pltpu.SemaphoreType.DMA((n,))],
        name=name)(*arrays)


def _native_segments(nh):
    w = nh * HEAD_DIM
    return [(0, 4 * w, 0, 0), (4 * w, 4 * w + 2 * nh, 1, 0), (4 * w + 2 * nh, 8 * w + 2 * nh, 0, 4 * w),
            (8 * w + 2 * nh, 8 * w + 3 * nh, 1, 2 * nh)]


def _relayout_w_in(wg, nh, tr):
    _, d, cols = wg.shape
    w = nh * HEAD_DIM

    def native(ref, j0, j1):
        out = []
        while j0 < j1:
            blk = j0 // cols
            end = min(j1, (blk + 1) * cols)
            out.append(ref[blk, :, pl.ds(j0 - blk * cols, end - j0)])
            j0 = end
        return out

    def body(g_ref, wide_ref, narrow_ref):
        for cidx in range(8 * w // LANES):
            j0 = cidx * LANES + (0 if cidx * LANES < 4 * w else 2 * nh)
            pieces = native(g_ref, j0, j0 + LANES)
            wide_ref[:, cidx * LANES:(cidx + 1) * LANES] = pieces[0] if len(pieces) == 1 else jnp.concatenate(pieces, axis=1)
        pieces = (native(g_ref, 4 * w, 4 * w + 2 * nh) + native(g_ref, 8 * w + 2 * nh, 8 * w + 3 * nh)
                  + [jnp.zeros((tr, LANES - 3 * nh), wg.dtype)])
        narrow_ref[...] = jnp.concatenate(pieces, axis=1)

    return pl.pallas_call(
        body, grid=(d // tr,), in_specs=[pl.BlockSpec((N_DEV, tr, cols), lambda i: (0, i, 0))],
        out_specs=[pl.BlockSpec((tr, 8 * w), lambda i: (i, 0)), pl.BlockSpec((tr, LANES), lambda i: (i, 0))],
        out_shape=[jax.ShapeDtypeStruct((d, 8 * w), wg.dtype), jax.ShapeDtypeStruct((d, LANES), wg.dtype)],
        name="relayout_w_in", compiler_params=_cp("parallel"))(wg)


def _relayout_dw_in(dwide, dnarrow, nh, tr):
    d = dwide.shape[0]
    w = nh * HEAD_DIM
    cols = (8 * w + 3 * nh) // N_DEV
    segs = _native_segments(nh)

    def body(wide_ref, narrow_ref, o_ref):
        refs = (wide_ref, narrow_ref)
        for blk in range(N_DEV):
            pieces = []
            for s0, s1, tgt, t0 in segs:
                lo, hi = max(s0, blk * cols), min(s1, (blk + 1) * cols)
                if lo < hi:
                    pieces.append(refs[tgt][:, pl.ds(t0 + lo - s0, hi - lo)])
            o_ref[blk] = pieces[0] if len(pieces) == 1 else jnp.concatenate(pieces, axis=1)

    return pl.pallas_call(
        body, grid=(d // tr,),
        in_specs=[pl.BlockSpec((tr, 8 * w), lambda i: (i, 0)), pl.BlockSpec((tr, LANES), lambda i: (i, 0))],
        out_specs=pl.BlockSpec((N_DEV, tr, cols), lambda i: (0, i, 0)),
        out_shape=jax.ShapeDtypeStruct((N_DEV, d, cols), dwide.dtype),
        name="relayout_dw_in", compiler_params=_cp("parallel"))(dwide, dnarrow)


def _adamw(w, parts, m, v, tr, name):
    r, c = w.shape
    n_parts = parts.shape[0]

    def body(w_ref, p_ref, m_ref, v_ref, g_ref, d_ref, nm_ref, nv_ref):
        g = p_ref[0].astype(F32)
        for s in range(1, n_parts):
            g = g + p_ref[s].astype(F32)
        m_new = ADAM_B1 * m_ref[...] + (1.0 - ADAM_B1) * g
        v_new = ADAM_B2 * v_ref[...] + (1.0 - ADAM_B2) * (g * g)
        m_hat = m_new / (1.0 - ADAM_B1 ** ADAM_STEP)
        v_hat = v_new / (1.0 - ADAM_B2 ** ADAM_STEP)
        g_ref[...] = g
        d_ref[...] = -ADAM_LR * (m_hat / (jnp.sqrt(v_hat) + ADAM_EPS) + ADAM_WD * w_ref[...])
        nm_ref[...] = m_new
        nv_ref[...] = v_new

    blk = pl.BlockSpec((tr, c), lambda i: (i, 0))
    return pl.pallas_call(
        body, grid=(r // tr,), in_specs=[blk, pl.BlockSpec((n_parts, tr, c), lambda i: (0, i, 0)), blk, blk],
        out_specs=[blk] * 4, out_shape=[jax.ShapeDtypeStruct((r, c), F32)] * 4, name=name,
        compiler_params=_cp("parallel"))(w, parts, m, v)


def _pack_small(d, pre, post, a_log, dt_bias, f_bias, gdn_w, fq_w, fk_w, extra):
    row2 = jnp.concatenate([a_log, dt_bias, f_bias, gdn_w, fq_w, fk_w, extra], axis=1)
    row2 = jnp.pad(row2, ((0, 0), (0, d - row2.shape[1])))
    return jnp.concatenate([pre, post, row2, jnp.zeros((5, d), F32)], axis=0)


def _unpack_small(p, nh):
    o = 3 * nh
    return dict(pre=p[0:1], post=p[1:2], a_log=p[2:3, 0:nh], dt_bias=p[2:3, nh:2 * nh], f_bias=p[2:3, 2 * nh:o],
                gdn_w=p[2:3, o:o + HEAD_DIM], fq_w=p[2:3, o + HEAD_DIM:o + 2 * HEAD_DIM],
                fk_w=p[2:3, o + 2 * HEAD_DIM:o + 3 * HEAD_DIM], extra=p[2, o + 3 * HEAD_DIM])


def kernel(x, meta_tokens, pre_norm_w, w_in, conv_w, a_log, dt_bias, gdn_norm_w, fox_q_norm_w, fox_k_norm_w, fox_f_bias, w_out, post_norm_w, loss_target, m_meta_tokens, m_pre_norm_w, m_w_in, m_conv_w, m_a_log, m_dt_bias, m_gdn_norm_w, m_fox_q_norm_w, m_fox_k_norm_w, m_fox_f_bias, m_w_out, m_post_norm_w, v_meta_tokens, v_pre_norm_w, v_w_in, v_conv_w, v_a_log, v_dt_bias, v_gdn_norm_w, v_fox_q_norm_w, v_fox_k_norm_w, v_fox_f_bias, v_w_out, v_post_norm_w):
    nh = a_log.shape[1]
    d = x.shape[-1]
    w = nh * HEAD_DIM
    zero = jnp.zeros((1, 1), F32)

    wg, wog, cg, mg = _all_gather(
        [_cast_bf16(w_in[0], 256, "cast_w_in"), _cast_bf16(w_out[0], 256, "cast_w_out"), conv_w[0], meta_tokens],
        "gather_weights")
    wp, wsm = _relayout_w_in(wg, nh, 256)
    meta_full = mg.transpose(1, 0, 2).reshape(N_META, d)
    g = _layer_grads(x[0], loss_target[0], meta_full, pre_norm_w, wp, wsm, cg.reshape(3 * w, CONV_WIDTH), a_log, dt_bias,
                     gdn_norm_w, fox_q_norm_w, fox_k_norm_w, fox_f_bias, wog.reshape(2 * w, d), post_norm_w)

    core = lax.axis_index("c")
    dev = 4 * lax.axis_index("x") + 2 * lax.axis_index("y") + core
    core_arr = jnp.reshape(core, (1,)).astype(jnp.int32)
    mine_in = _relayout_dw_in(g["wp"], g["wsm"], nh, 128)
    mine_out = g["w_out"].reshape(N_DEV, 2 * w // N_DEV, d)
    got_in, got_out = _pair_exchange([mine_in, mine_out], "pair_exchange")
    p_in, p_out = _chip_exchange([_pair_sum(mine_in, got_in, core_arr, 128, "pair_sum_w_in"),
                                  _pair_sum(mine_out, got_out, core_arr, 256, "pair_sum_w_out")], "chip_exchange")
    small = _pack_small(d, g["pre_w"], g["post_w"], g["a_log"], g["dt_bias"], g["f_bias"], g["gdn_norm_w"], g["fq_w"],
                        g["fk_w"], g["loss"])
    a_conv, a_meta, p_small = _all_gather([g["conv_w"], g["meta"], small], "gather_small_grads")
    p_conv = lax.dynamic_slice_in_dim(a_conv, dev * conv_w.shape[1], conv_w.shape[1], axis=1)
    p_meta = lax.dynamic_slice_in_dim(a_meta, dev * meta_tokens.shape[1], meta_tokens.shape[1], axis=2)

    r_in = _adamw(w_in[0], p_in, m_w_in[0], v_w_in[0], 128, "adamw_w_in")
    r_out = _adamw(w_out[0], p_out, m_w_out[0], v_w_out[0], 64, "adamw_w_out")
    r_conv = _adamw(conv_w[0], p_conv, m_conv_w[0], v_conv_w[0], conv_w.shape[1], "adamw_conv_w")
    r_meta = _adamw(meta_tokens, p_meta, m_meta_tokens, v_meta_tokens, N_META, "adamw_meta")
    pk = lambda pre, post, a, dt, gw, fq, fk, fb: _pack_small(d, pre, post, a, dt, fb, gw, fq, fk, zero)
    r_small = _adamw(
        pk(pre_norm_w, post_norm_w, a_log, dt_bias, gdn_norm_w, fox_q_norm_w, fox_k_norm_w, fox_f_bias), p_small,
        pk(m_pre_norm_w, m_post_norm_w, m_a_log, m_dt_bias, m_gdn_norm_w, m_fox_q_norm_w, m_fox_k_norm_w, m_fox_f_bias),
        pk(v_pre_norm_w, v_post_norm_w, v_a_log, v_dt_bias, v_gdn_norm_w, v_fox_q_norm_w, v_fox_k_norm_w, v_fox_f_bias),
        8, "adamw_small")

    sm = [_unpack_small(r, nh) for r in r_small]
    outs = []
    for i in range(4):
        s = sm[i]
        outs += [r_meta[i], s["pre"], r_in[i][None], r_conv[i][None], s["a_log"], s["dt_bias"], s["gdn_w"], s["fq_w"],
                 s["fk_w"], s["f_bias"], r_out[i][None], s["post"]]
    return (sm[0]["extra"], g["grad_x"][None], *outs)
```

```python
import jax
import jax.numpy as jnp
from jax import lax
from jax.experimental import pallas as pl
from jax.experimental.pallas import tpu as pltpu

F32, BF16 = jnp.float32, jnp.bfloat16
HEAD_DIM = 128
N_META = 16
CONV_WIDTH = 4
CHUNK = 64
Q_BLOCK = 128
LANES = 128
EPS = 1e-6
PAD_ROWS = Q_BLOCK - N_META
N_DEV = 8
VMEM_LIMIT = 56 * 1024 * 1024
NEG = -1e30

ADAM_LR, ADAM_B1, ADAM_B2, ADAM_EPS, ADAM_WD, ADAM_STEP = 0.001, 0.9, 0.999, 1e-08, 0.01, 10

_DN = {"nn": (((1,), (0,)), ((), ())), "nt": (((1,), (1,)), ((), ())), "tn": (((0,), (0,)), ((), ()))}


def _cp(*sem):
    return pltpu.CompilerParams(dimension_semantics=sem, vmem_limit_bytes=VMEM_LIMIT)


def _dot(a, b, dims="nn", prec=None):
    return lax.dot_general(a, b, _DN[dims], precision=prec, preferred_element_type=F32)


def _bdot(a, b, dims="nn"):
    return _dot(a.astype(BF16), b.astype(BF16), dims)


def _hdot(a, b, dims="nn"):
    return _dot(a, b, dims, prec=lax.Precision.HIGHEST)


def _iota(shape, dim):
    return lax.broadcasted_iota(jnp.int32, shape, dim)


def _sigmoid(z):
    return 1.0 / (1.0 + jnp.exp(-z))


def _softplus(z):
    e = jnp.exp(-jnp.abs(z))
    u = 1.0 + e
    l1p = jnp.where(u == 1.0, e, jnp.log(u) * (e / jnp.where(u == 1.0, 1.0, u - 1.0)))
    return jnp.maximum(z, 0.0) + l1p


def _silu_and_grad(z):
    s = _sigmoid(z)
    return z * s, s * (1.0 + z * (1.0 - s))


def _prenorm(h, w):
    lp, d = h.shape

    def body(h_ref, w_ref, o_ref):
        x = h_ref[...]
        r = lax.rsqrt(jnp.mean(x * x, axis=-1, keepdims=True) + EPS)
        o_ref[...] = (x * r * w_ref[...]).astype(BF16)

    return pl.pallas_call(
        body, grid=(lp // Q_BLOCK,),
        in_specs=[pl.BlockSpec((Q_BLOCK, d), lambda i: (i, 0)), pl.BlockSpec((1, d), lambda i: (0, 0))],
        out_specs=pl.BlockSpec((Q_BLOCK, d), lambda i: (i, 0)),
        out_shape=jax.ShapeDtypeStruct((lp, d), BF16), name="prenorm", compiler_params=_cp("parallel"))(h, w)


def _matmul(a, b, dims, tn, out_dtype, name):
    m = a.shape[1] if dims == "tn" else a.shape[0]
    n = b.shape[0] if dims == "nt" else b.shape[1]
    kdim = b.shape[1] if dims == "nt" else b.shape[0]
    tn = min(tn, n)
    b_spec = pl.BlockSpec((tn, kdim), lambda j: (j, 0)) if dims == "nt" else pl.BlockSpec((kdim, tn), lambda j: (0, j))

    def body(a_ref, b_ref, o_ref):
        o_ref[...] = _dot(a_ref[...], b_ref[...], dims).astype(out_dtype)

    return pl.pallas_call(
        body, grid=(n // tn,),
        in_specs=[pl.BlockSpec(a.shape, lambda j: (0, 0)), b_spec],
        out_specs=pl.BlockSpec((m, tn), lambda j: (0, j)),
        out_shape=jax.ShapeDtypeStruct((m, n), out_dtype), name=name, compiler_params=_cp("parallel"))(a, b)


def _matmul_nt_ktiled(a, b, tk, name):
    m, k = a.shape
    n = b.shape[0]

    def body(a_ref, b_ref, o_ref):
        @pl.when(pl.program_id(0) == 0)
        def _():
            o_ref[...] = jnp.zeros_like(o_ref)
        o_ref[...] += _dot(a_ref[...], b_ref[...], "nt")

    return pl.pallas_call(
        body, grid=(k // tk,),
        in_specs=[pl.BlockSpec((m, tk), lambda j: (0, j)), pl.BlockSpec((n, tk), lambda j: (0, j))],
        out_specs=pl.BlockSpec((m, n), lambda j: (0, 0)),
        out_shape=jax.ShapeDtypeStruct((m, n), F32), name=name, compiler_params=_cp("arbitrary"))(a, b)


def _conv_taps(x, w):
    c = x * w[CONV_WIDTH - 1:CONV_WIDTH, :]
    for j in range(CONV_WIDTH - 1):
        c = c + pltpu.roll(x, CONV_WIDTH - 1 - j, 0) * w[j:j + 1, :]
    return c


def _gdn_prep(proj, conv_wt, nh):
    lp = proj.shape[0]
    scale = HEAD_DIM ** -0.5

    def body(x_ref, w_ref, o_ref):
        which = pl.program_id(0) // nh
        c = _conv_taps(x_ref[...], w_ref[...])
        s = c * _sigmoid(c)
        r = lax.rsqrt(jnp.sum(s * s, axis=-1, keepdims=True) + EPS)
        f = jnp.where(which == 0, r * scale, jnp.where(which == 1, r, 1.0))
        o_ref[...] = jnp.where(_iota(s.shape, 0) >= PAD_ROWS, s * f, 0.0)

    return pl.pallas_call(
        body, grid=(3 * nh,),
        in_specs=[pl.BlockSpec((lp, LANES), lambda s: (0, s)), pl.BlockSpec((CONV_WIDTH, LANES), lambda s: (0, s))],
        out_specs=pl.BlockSpec((lp, LANES), lambda s: (0, s)),
        out_shape=jax.ShapeDtypeStruct((lp, 3 * nh * HEAD_DIM), F32), name="gdn_prep",
        compiler_params=_cp("parallel"))(proj, conv_wt)


def _gdn_prep_bwd(proj, conv_wt, dqkv, nh):
    lp = proj.shape[0]
    scale = HEAD_DIM ** -0.5

    def body(x_ref, w_ref, dy_ref, dx_ref, dw_ref):
        which = pl.program_id(0) // nh
        x = x_ref[...]
        w = w_ref[...]
        c = _conv_taps(x, w)
        sg = _sigmoid(c)
        s = c * sg
        r = lax.rsqrt(jnp.sum(s * s, axis=-1, keepdims=True) + EPS)
        live = _iota(s.shape, 0) >= PAD_ROWS
        dy = jnp.where(live, dy_ref[...], 0.0)
        y0 = s * r
        dy0 = dy * jnp.where(which == 0, scale, 1.0)
        ds_n = r * (dy0 - y0 * jnp.sum(dy0 * y0, axis=-1, keepdims=True))
        ds = jnp.where(which == 2, dy, ds_n)
        dc = ds * (sg * (1.0 + c * (1.0 - sg)))
        dx = dc * w[CONV_WIDTH - 1:CONV_WIDTH, :]
        rows = [jnp.sum(dc * x, axis=0, keepdims=True)]
        for j in range(CONV_WIDTH - 2, -1, -1):
            sh = CONV_WIDTH - 1 - j
            dx = dx + pltpu.roll(dc, lp - sh, 0) * w[j:j + 1, :]
            rows.insert(0, jnp.sum(dc * pltpu.roll(x, sh, 0), axis=0, keepdims=True))
        dx_ref[...] = dx.astype(BF16)
        dw_ref[...] = jnp.concatenate(rows, axis=0)

    return pl.pallas_call(
        body, grid=(3 * nh,),
        in_specs=[pl.BlockSpec((lp, LANES), lambda s: (0, s)), pl.BlockSpec((CONV_WIDTH, LANES), lambda s: (0, s)),
                  pl.BlockSpec((lp, LANES), lambda s: (0, s))],
        out_specs=[pl.BlockSpec((lp, LANES), lambda s: (0, s)), pl.BlockSpec((CONV_WIDTH, LANES), lambda s: (0, s))],
        out_shape=[jax.ShapeDtypeStruct((lp, 3 * nh * HEAD_DIM), BF16),
                   jax.ShapeDtypeStruct((CONV_WIDTH, 3 * nh * HEAD_DIM), F32)],
        name="gdn_prep_bwd", compiler_params=_cp("parallel"))(proj, conv_wt, dqkv)


def _gates(proj_sm, bias_row, nega_row, nh):
    lp = proj_sm.shape[0]
    nc = lp // CHUNK

    def body(p_ref, b_ref, a_ref, g_ref, gt3_ref, gtf_ref):
        lane = _iota((CHUNK, LANES), 1)
        tri = (_iota((CHUNK, CHUNK), 0) >= _iota((CHUNK, CHUNK), 1)).astype(F32)

        def step(n, carry):
            r0 = pl.multiple_of(n * CHUNK, CHUNK)
            z = p_ref[pl.ds(r0, CHUNK), :] + b_ref[...]
            base = jnp.where(lane < nh, _sigmoid(z),
                             jnp.where(lane < 2 * nh, a_ref[...] * _softplus(z),
                                       jnp.where(lane < 3 * nh, -_softplus(-z), 0.0)))
            base = jnp.where(r0 + _iota((CHUNK, LANES), 0) >= PAD_ROWS, base, 0.0)
            cs = _hdot(tri, base)
            run = jnp.where((lane >= 2 * nh) & (lane < 3 * nh), cs + carry, cs)
            sh = pltpu.roll(run, 2 * nh, 1)
            out = base + jnp.where((lane >= 3 * nh) & (lane < 5 * nh), sh, 0.0)
            g_ref[pl.ds(r0, CHUNK), :] = out
            gt3_ref[n] = out.T
            return carry + cs[CHUNK - 1:CHUNK, :]

        lax.fori_loop(0, nc, step, jnp.zeros((1, LANES), F32))
        gtf_ref[...] = g_ref[...].T

    return pl.pallas_call(
        body,
        out_shape=[jax.ShapeDtypeStruct((lp, LANES), F32), jax.ShapeDtypeStruct((nc, LANES, CHUNK), F32),
                   jax.ShapeDtypeStruct((LANES, lp), F32)],
        name="gates", compiler_params=pltpu.CompilerParams(vmem_limit_bytes=VMEM_LIMIT))(proj_sm, bias_row, nega_row)


def _gates_bwd(proj_sm, bias_row, nega_row, gates, dgate_gdn, dc_t, nh):
    lp = proj_sm.shape[0]
    nc = lp // CHUNK

    def body(p_ref, b_ref, a_ref, g_ref, dg_ref, dc_ref, dz_ref, sm_ref, dct_scr):
        lane = _iota((CHUNK, LANES), 1)
        triu = (_iota((CHUNK, CHUNK), 0) <= _iota((CHUNK, CHUNK), 1)).astype(F32)
        dct_scr[...] = dc_ref[...].T
        sm_ref[...] = jnp.zeros_like(sm_ref)

        def step(i, carry):
            n = nc - 1 - i
            r0 = pl.multiple_of(n * CHUNK, CHUNK)
            z = p_ref[pl.ds(r0, CHUNK), :] + b_ref[...]
            gt = g_ref[pl.ds(r0, CHUNK), :]
            dgd = dg_ref[pl.ds(r0, CHUNK), :]
            dch = dct_scr[pl.ds(r0, CHUNK), :]
            rc = _hdot(triu, dch) + carry
            sg = _sigmoid(z)
            dz = jnp.where(lane < nh, dgd * sg * (1.0 - sg),
                           jnp.where(lane < 2 * nh, dgd * a_ref[...] * sg,
                                     jnp.where(lane < 3 * nh, rc * (1.0 - sg), 0.0)))
            dz = jnp.where(r0 + _iota((CHUNK, LANES), 0) >= PAD_ROWS, dz, 0.0)
            dz_ref[pl.ds(r0, CHUNK), :] = dz.astype(BF16)
            sm_ref[0:1, :] += jnp.sum(dz, axis=0, keepdims=True)
            sm_ref[1:2, :] += jnp.sum(jnp.where((lane >= nh) & (lane < 2 * nh), dgd * gt, 0.0), axis=0, keepdims=True)
            return carry + jnp.sum(dch, axis=0, keepdims=True)

        lax.fori_loop(0, nc, step, jnp.zeros((1, LANES), F32))

    return pl.pallas_call(
        body,
        out_shape=[jax.ShapeDtypeStruct((lp, LANES), BF16), jax.ShapeDtypeStruct((8, LANES), F32)],
        scratch_shapes=[pltpu.VMEM((lp, LANES), F32)],
        name="gates_bwd", compiler_params=pltpu.CompilerParams(vmem_limit_bytes=VMEM_LIMIT))(
            proj_sm, bias_row, nega_row, gates, dgate_gdn, dc_t)


_DN3 = {"nn": (((2,), (1,)), ((0,), (0,))), "nt": (((2,), (2,)), ((0,), (0,))), "tn": (((1,), (1,)), ((0,), (0,)))}


def _bdot3(a, b, dims="nn"):
    return lax.dot_general(a.astype(BF16), b.astype(BF16), _DN3[dims], preferred_element_type=F32)


def _hdot3(a, b, dims="nn"):
    return lax.dot_general(a, b, _DN3[dims], precision=lax.Precision.HIGHEST, preferred_element_type=F32)


def _tri_inv(a):
    t = jnp.where(_iota(a.shape, 1) == _iota(a.shape, 2), 1.0, 0.0) - a
    p = a
    for _ in range(5):
        p = _hdot3(p, p)
        t = t + _hdot3(t, p)
    return t


def _gdn_chunk(q, k, v, beta, gc, gr):
    ii, jj = _iota((1, CHUNK, CHUNK), 1), _iota((1, CHUNK, CHUNK), 2)
    causal, strict = ii >= jj, ii > jj
    dm = jnp.where(causal, jnp.exp(jnp.where(causal, gc - gr, 0.0)), 0.0)
    kk = _bdot3(k, k, "nt")
    a = jnp.where(strict, beta * kk * dm, 0.0)
    t = _tri_inv(a)
    eg = jnp.exp(gc)
    glast = gc[:, CHUNK - 1:CHUNK, :]
    ekd = jnp.exp(glast - gc)
    bv = beta * v
    bk = (beta * eg) * k
    ub = _bdot3(t, jnp.concatenate([bv, bk], axis=2))
    qk = _bdot3(q, k, "nt")
    return dict(causal=causal, strict=strict, dm=dm, kk=kk, a=a, t=t, eg=eg, ekd=ekd, bv=bv, bk=bk,
                u=ub[:, :, :HEAD_DIM], w=ub[:, :, HEAD_DIM:], qk=qk, aqk=jnp.where(causal, qk * dm, 0.0),
                q_dec=q * eg, k_dec=k * ekd, decay=jnp.exp(glast))


def _gdn_chunk_inputs(q_ref, k_ref, v_ref, g, gt, nh):
    heads = lambda ref: jnp.stack([ref[:, h * HEAD_DIM:(h + 1) * HEAD_DIM] for h in range(nh)], axis=0)
    col = lambda o: jnp.stack([g[:, o + h:o + h + 1] for h in range(nh)], axis=0)
    gr = jnp.stack([gt[3 * nh + h:3 * nh + h + 1, :] for h in range(nh)], axis=0)
    return heads(q_ref), heads(k_ref), heads(v_ref), col(0), col(3 * nh), gr


def _gdn_fwd(qkv, gates, gt3, nh):
    lp = qkv.shape[0]
    nc = lp // CHUNK
    w = nh * HEAD_DIM

    def body(q_ref, k_ref, v_ref, g_ref, gt_ref, o_ref, sall_ref, s_scr):
        @pl.when(pl.program_id(0) == 0)
        def _():
            s_scr[...] = jnp.zeros_like(s_scr)
        c = _gdn_chunk(*_gdn_chunk_inputs(q_ref, k_ref, v_ref, g_ref[...], gt_ref[0], nh))
        s = s_scr[...]
        sall_ref[0] = s
        v_new = c["u"] - _bdot3(c["w"], s)
        o = _bdot3(c["q_dec"], s) + _bdot3(c["aqk"], v_new)
        s_scr[...] = s * c["decay"] + _bdot3(c["k_dec"], v_new, "tn")
        for h in range(nh):
            o_ref[:, h * HEAD_DIM:(h + 1) * HEAD_DIM] = o[h]

    return pl.pallas_call(
        body, grid=(nc,),
        in_specs=[pl.BlockSpec((CHUNK, w), lambda n: (n, 0)), pl.BlockSpec((CHUNK, w), lambda n: (n, 1)),
                  pl.BlockSpec((CHUNK, w), lambda n: (n, 2)), pl.BlockSpec((CHUNK, LANES), lambda n: (n, 0)),
                  pl.BlockSpec((1, LANES, CHUNK), lambda n: (n, 0, 0))],
        out_specs=[pl.BlockSpec((CHUNK, w), lambda n: (n, 0)),
                   pl.BlockSpec((1, nh, HEAD_DIM, HEAD_DIM), lambda n: (n, 0, 0, 0))],
        out_shape=[jax.ShapeDtypeStruct((lp, w), F32), jax.ShapeDtypeStruct((nc, nh, HEAD_DIM, HEAD_DIM), F32)],
        scratch_shapes=[pltpu.VMEM((nh, HEAD_DIM, HEAD_DIM), F32)],
        name="gdn_fwd", compiler_params=_cp("arbitrary"))(qkv, qkv, qkv, gates, gt3)


def _gdn_bwd(qkv, gates, gt3, s_all, do, nh):
    lp = qkv.shape[0]
    nc = lp // CHUNK
    w = nh * HEAD_DIM
    rev = lambda n: nc - 1 - n

    def body(q_ref, k_ref, v_ref, g_ref, gt_ref, s_ref, do_ref, dq_ref, dk_ref, dv_ref, dg_ref, ds_scr):
        @pl.when(pl.program_id(0) == 0)
        def _():
            ds_scr[...] = jnp.zeros_like(ds_scr)
        q, k, v, beta, gc, gr = _gdn_chunk_inputs(q_ref, k_ref, v_ref, g_ref[...], gt_ref[0], nh)
        c = _gdn_chunk(q, k, v, beta, gc, gr)
        s = s_ref[0]
        dsn = ds_scr[...]
        dout = jnp.stack([do_ref[:, h * HEAD_DIM:(h + 1) * HEAD_DIM] for h in range(nh)], axis=0)
        v_new = c["u"] - _bdot3(c["w"], s)
        dq_dec = _bdot3(dout, s, "nt")
        daqk = jnp.where(c["causal"], _bdot3(dout, v_new, "nt"), 0.0)
        dv_new = _bdot3(c["aqk"], dout, "tn") + _bdot3(c["k_dec"], dsn)
        dk_dec = _bdot3(v_new, dsn, "nt")
        ddecay = jnp.sum(jnp.sum(dsn * s, axis=2, keepdims=True), axis=1, keepdims=True)
        dw = -_bdot3(dv_new, s, "nt")
        ds_scr[...] = _bdot3(c["q_dec"], dout, "tn") + c["decay"] * dsn - _bdot3(c["w"], dv_new, "tn")
        duw = jnp.concatenate([dv_new, dw], axis=2)
        dt = _bdot3(duw, jnp.concatenate([c["bv"], c["bk"]], axis=2), "nt")
        dbvk = _bdot3(c["t"], duw, "tn")
        dbv, dbk = dbvk[:, :, :HEAD_DIM], dbvk[:, :, HEAD_DIM:]
        da = jnp.where(c["strict"], -_bdot3(_bdot3(c["t"], dt, "tn"), c["t"], "nt"), 0.0)
        dkk = da * beta * c["dm"]
        dqk = daqk * c["dm"]
        e = da * c["a"] + daqk * c["aqk"]
        dq = dq_dec * c["eg"] + _bdot3(dqk, k)
        dk = (dk_dec * c["ekd"] + _bdot3(dkk, k) + _bdot3(dkk, k, "tn") + _bdot3(dqk, q, "tn")
              + (beta * c["eg"]) * dbk)
        dv = beta * dbv
        rs = lambda x: jnp.sum(x, axis=2, keepdims=True)
        dbeta = rs(dbv * v) + c["eg"] * rs(dbk * k) + rs(da * c["kk"] * c["dm"])
        kd_term = rs(dk_dec * c["k_dec"])
        ones = jnp.ones((nh, CHUNK, LANES), F32)
        dg_cum = (rs(dq_dec * c["q_dec"]) - kd_term + rs(dbk * c["bk"]) + rs(e)
                  - _hdot3(e, ones, "tn")[:, :, 0:1])
        last = jnp.sum(kd_term, axis=1, keepdims=True) + ddecay * c["decay"]
        dg_cum = dg_cum + jnp.where(_iota((1, CHUNK, 1), 1) == CHUNK - 1, last, 0.0)
        lane = _iota((CHUNK, LANES), 1)
        acc = jnp.zeros((CHUNK, LANES), F32)
        for h in range(nh):
            sl = slice(h * HEAD_DIM, (h + 1) * HEAD_DIM)
            dq_ref[:, sl] = dq[h]
            dk_ref[:, sl] = dk[h]
            dv_ref[:, sl] = dv[h]
            acc = acc + jnp.where(lane == h, dbeta[h], 0.0) + jnp.where(lane == nh + h, dg_cum[h], 0.0)
        triu = (_iota((CHUNK, CHUNK), 0) <= _iota((CHUNK, CHUNK), 1)).astype(F32)
        dg_ref[...] = jnp.where(lane < nh, acc, _hdot(triu, acc))

    return pl.pallas_call(
        body, grid=(nc,),
        in_specs=[pl.BlockSpec((CHUNK, w), lambda n: (rev(n), 0)), pl.BlockSpec((CHUNK, w), lambda n: (rev(n), 1)),
                  pl.BlockSpec((CHUNK, w), lambda n: (rev(n), 2)), pl.BlockSpec((CHUNK, LANES), lambda n: (rev(n), 0)),
                  pl.BlockSpec((1, LANES, CHUNK), lambda n: (rev(n), 0, 0)),
                  pl.BlockSpec((1, nh, HEAD_DIM, HEAD_DIM), lambda n: (rev(n), 0, 0, 0)),
                  pl.BlockSpec((CHUNK, w), lambda n: (rev(n), 0))],
        out_specs=[pl.BlockSpec((CHUNK, w), lambda n: (rev(n), 0))] * 3 + [pl.BlockSpec((CHUNK, LANES), lambda n: (rev(n), 0))],
        out_shape=[jax.ShapeDtypeStruct((lp, w), F32)] * 3 + [jax.ShapeDtypeStruct((lp, LANES), F32)],
        scratch_shapes=[pltpu.VMEM((nh, HEAD_DIM, HEAD_DIM), F32)],
        name="gdn_bwd", compiler_params=_cp("arbitrary"))(qkv, qkv, qkv, gates, gt3, s_all, do)


def _merge_gdn(o_gdn, proj, norm_w, nh):
    lp = o_gdn.shape[0]

    def body(o_ref, z_ref, w_ref, m_ref):
        o = o_ref[...]
        r = lax.rsqrt(jnp.mean(o * o, axis=-1, keepdims=True) + EPS)
        z = z_ref[...]
        m_ref[...] = (o * r * w_ref[...] * (z * _sigmoid(z))).astype(BF16)

    return pl.pallas_call(
        body, grid=(nh,),
        in_specs=[pl.BlockSpec((lp, LANES), lambda s: (0, s)), pl.BlockSpec((lp, LANES), lambda s: (0, 3 * nh + s)),
                  pl.BlockSpec((1, LANES), lambda s: (0, 0))],
        out_specs=pl.BlockSpec((lp, LANES), lambda s: (0, s)),
        out_shape=jax.ShapeDtypeStruct((lp, nh * HEAD_DIM), BF16), name="merge_gdn",
        compiler_params=_cp("parallel"))(o_gdn, proj, norm_w)


def _merge_gdn_bwd(o_gdn, proj, norm_w, dmerged, nh):
    lp = o_gdn.shape[0]

    def body(o_ref, z_ref, w_ref, dm_ref, do_ref, dz_ref, dw_ref):
        o = o_ref[...]
        r = lax.rsqrt(jnp.mean(o * o, axis=-1, keepdims=True) + EPS)
        xh = o * r
        silu, dsilu = _silu_and_grad(z_ref[...])
        dm = dm_ref[...]
        dn = dm * silu
        dz_ref[...] = (dm * (xh * w_ref[...]) * dsilu).astype(BF16)
        dnw = dn * w_ref[...]
        do_ref[...] = r * (dnw - xh * jnp.mean(dnw * xh, axis=-1, keepdims=True))

        @pl.when(pl.program_id(0) == 0)
        def _():
            dw_ref[...] = jnp.zeros_like(dw_ref)
        dw_ref[...] += jnp.sum(dn * xh, axis=0, keepdims=True)

    w = nh * HEAD_DIM
    return pl.pallas_call(
        body, grid=(nh,),
        in_specs=[pl.BlockSpec((lp, LANES), lambda s: (0, s)), pl.BlockSpec((lp, LANES), lambda s: (0, 3 * nh + s)),
                  pl.BlockSpec((1, LANES), lambda s: (0, 0)), pl.BlockSpec((lp, LANES), lambda s: (0, s))],
        out_specs=[pl.BlockSpec((lp, LANES), lambda s: (0, s)), pl.BlockSpec((lp, LANES), lambda s: (0, s)),
                   pl.BlockSpec((1, LANES), lambda s: (0, 0))],
        out_shape=[jax.ShapeDtypeStruct((lp, w), F32), jax.ShapeDtypeStruct((lp, w), BF16),
                   jax.ShapeDtypeStruct((1, LANES), F32)],
        name="merge_gdn_bwd", compiler_params=_cp("arbitrary"))(o_gdn, proj, norm_w, dmerged)


def _fox_prep(proj, qk_w, nh):
    lp = proj.shape[0]

    def body(x_ref, w_ref, o_ref):
        x = x_ref[...]
        r = lax.rsqrt(jnp.mean(x * x, axis=-1, keepdims=True) + EPS)
        o_ref[...] = x * r * w_ref[0]

    return pl.pallas_call(
        body, grid=(2 * nh,),
        in_specs=[pl.BlockSpec((lp, LANES), lambda s: (0, 4 * nh + s)), pl.BlockSpec((1, 1, LANES), lambda s: (s // nh, 0, 0))],
        out_specs=pl.BlockSpec((lp, LANES), lambda s: (0, s)),
        out_shape=jax.ShapeDtypeStruct((lp, 2 * nh * HEAD_DIM), F32), name="fox_prep",
        compiler_params=_cp("parallel"))(proj, qk_w)


def _fox_prep_bwd(proj, qk_w, dqk, nh):
    lp = proj.shape[0]

    def body(x_ref, w_ref, dy_ref, dx_ref, dw_ref):
        x = x_ref[...]
        r = lax.rsqrt(jnp.mean(x * x, axis=-1, keepdims=True) + EPS)
        xh = x * r
        dy = dy_ref[...]
        dyw = dy * w_ref[0]
        dx_ref[...] = (r * (dyw - xh * jnp.mean(dyw * xh, axis=-1, keepdims=True))).astype(BF16)

        @pl.when(pl.program_id(0) % nh == 0)
        def _():
            dw_ref[...] = jnp.zeros_like(dw_ref)
        dw_ref[0] += jnp.sum(dy * xh, axis=0, keepdims=True)

    return pl.pallas_call(
        body, grid=(2 * nh,),
        in_specs=[pl.BlockSpec((lp, LANES), lambda s: (0, 4 * nh + s)), pl.BlockSpec((1, 1, LANES), lambda s: (s // nh, 0, 0)),
                  pl.BlockSpec((lp, LANES), lambda s: (0, s))],
        out_specs=[pl.BlockSpec((lp, LANES), lambda s: (0, s)), pl.BlockSpec((1, 1, LANES), lambda s: (s // nh, 0, 0))],
        out_shape=[jax.ShapeDtypeStruct((lp, 2 * nh * HEAD_DIM), BF16), jax.ShapeDtypeStruct((2, 1, LANES), F32)],
        name="fox_prep_bwd", compiler_params=_cp("arbitrary"))(proj, qk_w, dqk)


def _fox_probs(q, k, gates, crow, h, i, nh):
    kl = k.shape[0]
    lane = _iota((Q_BLOCK, LANES), 1)
    ct = jnp.sum(jnp.where(lane == 4 * nh + h, gates, 0.0), axis=1, keepdims=True)
    s = _bdot(q, k, "nt") * (HEAD_DIM ** -0.5) + (ct - crow)
    t = i * Q_BLOCK + _iota((Q_BLOCK, kl), 0)
    kp = _iota((Q_BLOCK, kl), 1)
    s = jnp.where((kp <= t) & ((kp >= PAD_ROWS) | (t < PAD_ROWS)), s, NEG)
    p = jnp.exp(s - jnp.max(s, axis=1, keepdims=True))
    return p / jnp.sum(p, axis=1, keepdims=True)


def _fox_specs(lp, nh):
    return [pl.BlockSpec((Q_BLOCK, LANES), lambda h, i: (i, h)),
            pl.BlockSpec((lp, LANES), lambda h, i: (0, nh + h)),
            pl.BlockSpec((lp, LANES), lambda h, i: (0, 6 * nh + h)),
            pl.BlockSpec((Q_BLOCK, LANES), lambda h, i: (i, 0)),
            pl.BlockSpec((LANES, lp), lambda h, i: (0, 0))]


def _fox_fwd(qkn, proj, gates, gtf, nh):
    lp = qkn.shape[0]

    def body(q_ref, k_ref, v_ref, g_ref, gt_ref, o_ref):
        h, i = pl.program_id(0), pl.program_id(1)
        for j in range(lp // Q_BLOCK):
            @pl.when(i == j)
            def _(j=j):
                kl = (j + 1) * Q_BLOCK
                p = _fox_probs(q_ref[...], k_ref[0:kl, :], g_ref[...], gt_ref[pl.ds(4 * nh + h, 1), :][:, 0:kl], h, j, nh)
                o_ref[...] = _bdot(p, v_ref[0:kl, :])

    return pl.pallas_call(
        body, grid=(nh, lp // Q_BLOCK), in_specs=_fox_specs(lp, nh),
        out_specs=pl.BlockSpec((Q_BLOCK, LANES), lambda h, i: (i, h)),
        out_shape=jax.ShapeDtypeStruct((lp, nh * HEAD_DIM), F32), name="fox_fwd",
        compiler_params=_cp("parallel", "parallel"))(qkn, qkn, proj, gates, gtf)


def _fox_bwd(qkn, proj, gates, gtf, do, nh):
    lp = qkn.shape[0]
    w = nh * HEAD_DIM
    scale = HEAD_DIM ** -0.5

    def body(q_ref, k_ref, v_ref, g_ref, gt_ref, do_ref, dq_ref, dk_ref, dv_ref, dc_ref):
        h, i = pl.program_id(0), pl.program_id(1)

        @pl.when(i == 0)
        def _():
            dk_ref[...] = jnp.zeros_like(dk_ref)
            dv_ref[...] = jnp.zeros_like(dv_ref)
            dc_ref[...] = jnp.zeros_like(dc_ref)
        for j in range(lp // Q_BLOCK):
            @pl.when(i == j)
            def _(j=j):
                kl = (j + 1) * Q_BLOCK
                q, k = q_ref[...], k_ref[0:kl, :]
                p = _fox_probs(q, k, g_ref[...], gt_ref[pl.ds(4 * nh + h, 1), :][:, 0:kl], h, j, nh)
                dout = do_ref[...]
                dp = _bdot(dout, v_ref[0:kl, :], "nt")
                ds = p * (dp - jnp.sum(p * dp, axis=1, keepdims=True))
                dq_ref[...] = _bdot(ds, k) * scale
                dk_ref[0:kl, :] += _bdot(ds, q, "tn") * scale
                dv_ref[0:kl, :] += _bdot(p, dout, "tn")
                dc_ref[0, :, 0:kl] -= jnp.sum(ds, axis=0, keepdims=True)

    blk = pl.BlockSpec((Q_BLOCK, LANES), lambda h, i: (i, h))
    col = pl.BlockSpec((lp, LANES), lambda h, i: (0, h))
    return pl.pallas_call(
        body, grid=(nh, lp // Q_BLOCK), in_specs=_fox_specs(lp, nh) + [blk],
        out_specs=[blk, col, col, pl.BlockSpec((1, 1, lp), lambda h, i: (h, 0, 0))],
        out_shape=[jax.ShapeDtypeStruct((lp, w), F32)] * 3 + [jax.ShapeDtypeStruct((nh, 1, lp), F32)],
        name="fox_bwd", compiler_params=_cp("parallel", "arbitrary"))(qkn, qkn, proj, gates, gtf, do)


def _merge_fox(o_fox, proj, nh):
    lp = o_fox.shape[0]

    def body(o_ref, z_ref, m_ref):
        z = z_ref[...]
        m_ref[...] = (o_ref[...] * (z * _sigmoid(z))).astype(BF16)

    return pl.pallas_call(
        body, grid=(nh,),
        in_specs=[pl.BlockSpec((lp, LANES), lambda s: (0, s)), pl.BlockSpec((lp, LANES), lambda s: (0, 7 * nh + s))],
        out_specs=pl.BlockSpec((lp, LANES), lambda s: (0, s)),
        out_shape=jax.ShapeDtypeStruct((lp, nh * HEAD_DIM), BF16), name="merge_fox",
        compiler_params=_cp("parallel"))(o_fox, proj)


def _merge_fox_bwd(o_fox, proj, dmerged, nh):
    lp = o_fox.shape[0]

    def body(o_ref, z_ref, dm_ref, do_ref, dz_ref):
        silu, dsilu = _silu_and_grad(z_ref[...])
        dm = dm_ref[...]
        do_ref[...] = dm * silu
        dz_ref[...] = (dm * o_ref[...] * dsilu).astype(BF16)

    w = nh * HEAD_DIM
    return pl.pallas_call(
        body, grid=(nh,),
        in_specs=[pl.BlockSpec((lp, LANES), lambda s: (0, s)), pl.BlockSpec((lp, LANES), lambda s: (0, 7 * nh + s)),
                  pl.BlockSpec((lp, LANES), lambda s: (0, nh + s))],
        out_specs=[pl.BlockSpec((lp, LANES), lambda s: (0, s)), pl.BlockSpec((lp, LANES), lambda s: (0, s))],
        out_shape=[jax.ShapeDtypeStruct((lp, w), F32), jax.ShapeDtypeStruct((lp, w), BF16)],
        name="merge_fox_bwd", compiler_params=_cp("parallel"))(o_fox, proj, dmerged)


def _post(out, h, target, post_w):
    lp, d = out.shape

    def body(o_ref, h_ref, t_ref, w_ref, dy_ref, do_ref, loss_ref, dw_ref):
        i = pl.program_id(0)

        @pl.when(i == 0)
        def _():
            loss_ref[...] = jnp.zeros_like(loss_ref)
            dw_ref[...] = jnp.zeros_like(dw_ref)
        o = o_ref[...]
        r = lax.rsqrt(jnp.mean(o * o, axis=-1, keepdims=True) + EPS)
        nrm = o * r
        err = jnp.where(i > 0, h_ref[...] + nrm * w_ref[...] - t_ref[...], 0.0)
        loss_ref[0:1, :] += 0.5 * jnp.sum(jnp.sum(err * err, axis=1, keepdims=True), axis=0, keepdims=True) / d
        dy = err / d
        dy_ref[...] = dy
        dw_ref[...] += jnp.sum(dy * nrm, axis=0, keepdims=True)
        dyw = dy * w_ref[...]
        do_ref[...] = (r * (dyw - nrm * jnp.mean(dyw * nrm, axis=-1, keepdims=True))).astype(BF16)

    row = pl.BlockSpec((Q_BLOCK, d), lambda i: (i, 0))
    vec = pl.BlockSpec((1, d), lambda i: (0, 0))
    return pl.pallas_call(
        body, grid=(lp // Q_BLOCK,),
        in_specs=[row, row, pl.BlockSpec((Q_BLOCK, d), lambda i: (jnp.maximum(i - 1, 0), 0)), vec],
        out_specs=[row, row, pl.BlockSpec((8, LANES), lambda i: (0, 0)), vec],
        out_shape=[jax.ShapeDtypeStruct((lp, d), F32), jax.ShapeDtypeStruct((lp, d), BF16),
                   jax.ShapeDtypeStruct((8, LANES), F32), jax.ShapeDtypeStruct((1, d), F32)],
        name="post", compiler_params=_cp("arbitrary"))(out, h, target, post_w)


def _prenorm_bwd(dxn, h, w, dy):
    lp, d = h.shape

    def body(dx_ref, h_ref, w_ref, dy_ref, dh_ref, dw_ref):
        @pl.when(pl.program_id(0) == 0)
        def _():
            dw_ref[...] = jnp.zeros_like(dw_ref)
        x = h_ref[...]
        r = lax.rsqrt(jnp.mean(x * x, axis=-1, keepdims=True) + EPS)
        xh = x * r
        dxn_ = dx_ref[...]
        dw_ref[...] += jnp.sum(dxn_ * xh, axis=0, keepdims=True)
        dxw = dxn_ * w_ref[...]
        dh_ref[...] = dy_ref[...] + r * (dxw - xh * jnp.mean(dxw * xh, axis=-1, keepdims=True))

    row = pl.BlockSpec((Q_BLOCK, d), lambda i: (i, 0))
    vec = pl.BlockSpec((1, d), lambda i: (0, 0))
    return pl.pallas_call(
        body, grid=(lp // Q_BLOCK,), in_specs=[row, row, vec, row], out_specs=[row, vec],
        out_shape=[jax.ShapeDtypeStruct((lp, d), F32), jax.ShapeDtypeStruct((1, d), F32)],
        name="prenorm_bwd", compiler_params=_cp("arbitrary"))(dxn, h, w, dy)


def _layer_grads(x, target, meta, pre_w, wp, wsm, conv_w, a_log, dt_bias, gdn_norm_w, fq_w, fk_w, f_bias,
                 w_out, post_w):
    seq, d = x.shape
    nh = a_log.shape[1]
    w = nh * HEAD_DIM
    h = jnp.concatenate([jnp.zeros((PAD_ROWS, d), F32), meta, x], axis=0)
    conv_wt = conv_w.T
    zpad = jnp.zeros((1, LANES - 3 * nh), F32)
    bias_row = jnp.concatenate([jnp.zeros((1, nh), F32), dt_bias, f_bias, zpad], axis=1)
    nega_row = jnp.concatenate([jnp.zeros((1, nh), F32), -jnp.exp(a_log), jnp.zeros((1, nh), F32), zpad], axis=1)
    qk_w = jnp.stack([fq_w, fk_w])

    xn = _prenorm(h, pre_w)
    proj = _matmul(xn, wp, "nn", 512, F32, "proj")
    proj_sm = _matmul(xn, wsm, "nn", LANES, F32, "proj_sm")
    qkv = _gdn_prep(proj, conv_wt, nh)
    gates, gt3, gtf = _gates(proj_sm, bias_row, nega_row, nh)
    o_gdn, s_all = _gdn_fwd(qkv, gates, gt3, nh)
    qkn = _fox_prep(proj, qk_w, nh)
    o_fox = _fox_fwd(qkn, proj, gates, gtf, nh)
    merged = jnp.concatenate([_merge_gdn(o_gdn, proj, gdn_norm_w, nh), _merge_fox(o_fox, proj, nh)], axis=1)
    out = _matmul(merged, w_out, "nn", 512, F32, "out_proj")
    dy, dout, loss_blk, dpost_w = _post(out, h, target, post_w)

    dw_out = _matmul(merged, dout, "tn", 512, BF16, "dw_out")
    dmerged = _matmul(dout, w_out, "nt", 512, F32, "dmerged")
    do_gdn, dgz, dgdn_norm_w = _merge_gdn_bwd(o_gdn, proj, gdn_norm_w, dmerged, nh)
    do_fox, dfg = _merge_fox_bwd(o_fox, proj, dmerged, nh)
    dqn, dkn, dfv, dc_t = _fox_bwd(qkn, proj, gates, gtf, do_fox, nh)
    dfqk, dqk_w = _fox_prep_bwd(proj, qk_w, jnp.concatenate([dqn, dkn], axis=1), nh)
    dgq, dgk, dgv, dgate = _gdn_bwd(qkv, gates, gt3, s_all, do_gdn, nh)
    dqkv_n = jnp.concatenate([dgq, dgk, dgv], axis=1)
    dqkv, dconv_wt = _gdn_prep_bwd(proj, conv_wt, dqkv_n, nh)
    dc_rows = jnp.pad(dc_t.reshape(nh, -1), ((2 * nh, LANES - 3 * nh), (0, 0)))
    dsm, gate_sums = _gates_bwd(proj_sm, bias_row, nega_row, gates, dgate, dc_rows, nh)
    dproj = jnp.concatenate([dqkv, dgz, dfqk, dfv.astype(BF16), dfg], axis=1)
    dwp = _matmul(xn, dproj, "tn", 512, BF16, "dw_in")
    dwsm = _matmul(xn, dsm, "tn", LANES, BF16, "dw_in_sm")
    dxn = _matmul_nt_ktiled(dproj, wp, 512, "dxn") + _matmul(dsm, wsm, "nt", 512, F32, "dxn_sm")
    dh, dpre_w = _prenorm_bwd(dxn, h, pre_w, dy)
    return dict(
        loss=loss_blk[0:1, 0:1], grad_x=dh[Q_BLOCK:], meta=dh[PAD_ROWS:Q_BLOCK], pre_w=dpre_w, wp=dwp, wsm=dwsm,
        conv_w=dconv_wt.T, a_log=gate_sums[1:2, nh:2 * nh], dt_bias=gate_sums[0:1, nh:2 * nh],
        gdn_norm_w=dgdn_norm_w, fq_w=dqk_w[0], fk_w=dqk_w[1], f_bias=gate_sums[0:1, 2 * nh:3 * nh],
        w_out=dw_out, post_w=dpost_w)


def _cast_bf16(a, tr, name):
    r, c = a.shape

    def body(a_ref, o_ref):
        o_ref[...] = a_ref[...].astype(BF16)

    return pl.pallas_call(
        body, grid=(r // tr,), in_specs=[pl.BlockSpec((tr, c), lambda i: (i, 0))],
        out_specs=pl.BlockSpec((tr, c), lambda i: (i, 0)), out_shape=jax.ShapeDtypeStruct((r, c), BF16),
        name=name, compiler_params=_cp("parallel"))(a)


N_CHIP = 4
_ANY = pl.BlockSpec(memory_space=pl.ANY)
_MESH = pl.DeviceIdType.MESH


def _all_gather(arrays, name):
    n = len(arrays)

    def body(*refs):
        ins, outs = refs[:n], refs[n:2 * n]
        send_sems, recv_sems, local_sems = refs[2 * n:]
        x, y, c = lax.axis_index("x"), lax.axis_index("y"), lax.axis_index("c")
        sibling = (x, y, 1 - c)
        chips = [(1 - x, y), (x, 1 - y), (1 - x, 1 - y)]

        def row(a, px, py, pc):
            return outs[a].at[4 * px + 2 * py + pc]

        def copy(a, k, block, to, src=None):
            return pltpu.make_async_remote_copy(
                src_ref=row(a, *block) if src is None else src, dst_ref=row(a, *block), send_sem=send_sems.at[a, k],
                recv_sem=recv_sems.at[a, k], device_id=to, device_id_type=_MESH)

        local, sends = [], []
        for a in range(n):
            local.append(pltpu.make_async_copy(ins[a], row(a, x, y, c), local_sems.at[a]))
            sends.append(copy(a, 0, (x, y, c), sibling, src=ins[a]))
            sends += [copy(a, 1 + j, (x, y, c), (*chip, c), src=ins[a]) for j, chip in enumerate(chips)]
        for cp in local + sends:
            cp.start()
        for j, chip in enumerate(chips):
            for a in range(n):
                copy(a, 1 + j, (*chip, c), (x, y, c)).wait_recv()
                fwd = copy(a, 4 + j, (*chip, c), sibling)
                fwd.start()
                sends.append(fwd)
        for a in range(n):
            copy(a, 0, sibling, (x, y, c)).wait_recv()
            for j, chip in enumerate(chips):
                copy(a, 4 + j, (*chip, 1 - c), (x, y, c)).wait_recv()
        for cp in sends:
            cp.wait_send()
        for cp in local:
            cp.wait()

    return pl.pallas_call(
        body, in_specs=[_ANY] * n, out_specs=[_ANY] * n,
        out_shape=[jax.ShapeDtypeStruct((N_DEV,) + a.shape, a.dtype) for a in arrays],
        scratch_shapes=[pltpu.SemaphoreType.DMA((n, N_DEV - 1)), pltpu.SemaphoreType.DMA((n, N_DEV - 1)),
                        pltpu.SemaphoreType.DMA((n,))],
        name=name)(*arrays)


def _pair_exchange(arrays, name):
    n = len(arrays)

    def body(*refs):
        ins, outs = refs[:n], refs[n:2 * n]
        send_sems, recv_sems = refs[2 * n:]
        x, y, c = lax.axis_index("x"), lax.axis_index("y"), lax.axis_index("c")
        copies = []
        for a in range(n):
            for q in range(N_CHIP):
                copies.append(pltpu.make_async_remote_copy(
                    src_ref=ins[a].at[2 * q + 1 - c], dst_ref=outs[a].at[q], send_sem=send_sems.at[a, q],
                    recv_sem=recv_sems.at[a, q], device_id=(x, y, 1 - c), device_id_type=_MESH))
        for cp in copies:
            cp.start()
        for cp in copies:
            cp.wait()

    return pl.pallas_call(
        body, in_specs=[_ANY] * n, out_specs=[_ANY] * n,
        out_shape=[jax.ShapeDtypeStruct((N_CHIP,) + a.shape[1:], a.dtype) for a in arrays],
        scratch_shapes=[pltpu.SemaphoreType.DMA((n, N_CHIP)), pltpu.SemaphoreType.DMA((n, N_CHIP))],
        name=name)(*arrays)


def _pair_sum(parts, got, core, tr, name):
    _, r, c = parts.shape

    def body(core_ref, p_ref, g_ref, o_ref):
        o_ref[...] = (p_ref[...].astype(F32) + g_ref[...].astype(F32)).astype(o_ref.dtype)

    return pl.pallas_call(
        body,
        grid_spec=pltpu.PrefetchScalarGridSpec(
            num_scalar_prefetch=1, grid=(N_CHIP, r // tr),
            in_specs=[pl.BlockSpec((1, tr, c), lambda q, i, core_ref: (2 * q + core_ref[0], i, 0)),
                      pl.BlockSpec((1, tr, c), lambda q, i, core_ref: (q, i, 0))],
            out_specs=pl.BlockSpec((1, tr, c), lambda q, i, core_ref: (q, i, 0))),
        out_shape=jax.ShapeDtypeStruct((N_CHIP, r, c), parts.dtype), name=name,
        compiler_params=_cp("parallel", "parallel"))(core, parts, got)


def _chip_exchange(arrays, name):
    n = len(arrays)

    def body(*refs):
        ins, outs = refs[:n], refs[n:2 * n]
        send_sems, recv_sems, local_sems = refs[2 * n:]
        x, y, c = lax.axis_index("x"), lax.axis_index("y"), lax.axis_index("c")
        mine = 2 * x + y
        local, sends, recvs = [], [], []
        for a in range(n):
            local.append(pltpu.make_async_copy(ins[a].at[mine], outs[a].at[mine], local_sems.at[a]))
            for k in range(1, N_CHIP):
                px = 1 - x if k & 2 else x
                py = 1 - y if k & 1 else y
                kw = dict(send_sem=send_sems.at[a, k - 1], recv_sem=recv_sems.at[a, k - 1], device_id=(px, py, c),
                          device_id_type=_MESH)
                sends.append(pltpu.make_async_remote_copy(src_ref=ins[a].at[2 * px + py], dst_ref=outs[a].at[mine], **kw))
                recvs.append(pltpu.make_async_remote_copy(src_ref=ins[a].at[mine], dst_ref=outs[a].at[2 * px + py], **kw))
        for cp in local + sends:
            cp.start()
        for cp in local:
            cp.wait()
        for cp in sends:
            cp.wait_send()
        for cp in recvs:
            cp.wait_recv()

    return pl.pallas_call(
        body, in_specs=[_ANY] * n, out_specs=[_ANY] * n,
        out_shape=[jax.ShapeDtypeStruct(a.shape, a.dtype) for a in arrays],
        scratch_shapes=[pltpu.SemaphoreType.DMA((n, N_CHIP - 1)), pltpu.SemaphoreType.DMA((n, N_CHIP - 1)),
                        pltpu.SemaphoreType.DMA((n,))],
        name=name)(*arrays)


def _native_segments(nh):
    w = nh * HEAD_DIM
    return [(0, 4 * w, 0, 0), (4 * w, 4 * w + 2 * nh, 1, 0), (4 * w + 2 * nh, 8 * w + 2 * nh, 0, 4 * w),
            (8 * w + 2 * nh, 8 * w + 3 * nh, 1, 2 * nh)]


def _relayout_w_in(wg, nh, tr):
    _, d, cols = wg.shape
    w = nh * HEAD_DIM

    def native(ref, j0, j1):
        out = []
        while j0 < j1:
            blk = j0 // cols
            end = min(j1, (blk + 1) * cols)
            out.append(ref[blk, :, pl.ds(j0 - blk * cols, end - j0)])
            j0 = end
        return out

    def body(g_ref, wide_ref, narrow_ref):
        for cidx in range(8 * w // LANES):
            j0 = cidx * LANES + (0 if cidx * LANES < 4 * w else 2 * nh)
            pieces = native(g_ref, j0, j0 + LANES)
            wide_ref[:, cidx * LANES:(cidx + 1) * LANES] = pieces[0] if len(pieces) == 1 else jnp.concatenate(pieces, axis=1)
        pieces = (native(g_ref, 4 * w, 4 * w + 2 * nh) + native(g_ref, 8 * w + 2 * nh, 8 * w + 3 * nh)
                  + [jnp.zeros((tr, LANES - 3 * nh), wg.dtype)])
        narrow_ref[...] = jnp.concatenate(pieces, axis=1)

    return pl.pallas_call(
        body, grid=(d // tr,), in_specs=[pl.BlockSpec((N_DEV, tr, cols), lambda i: (0, i, 0))],
        out_specs=[pl.BlockSpec((tr, 8 * w), lambda i: (i, 0)), pl.BlockSpec((tr, LANES), lambda i: (i, 0))],
        out_shape=[jax.ShapeDtypeStruct((d, 8 * w), wg.dtype), jax.ShapeDtypeStruct((d, LANES), wg.dtype)],
        name="relayout_w_in", compiler_params=_cp("parallel"))(wg)


def _relayout_dw_in(dwide, dnarrow, nh, tr):
    d = dwide.shape[0]
    w = nh * HEAD_DIM
    cols = (8 * w + 3 * nh) // N_DEV
    segs = _native_segments(nh)

    def body(wide_ref, narrow_ref, o_ref):
        refs = (wide_ref, narrow_ref)
        for blk in range(N_DEV):
            pieces = []
            for s0, s1, tgt, t0 in segs:
                lo, hi = max(s0, blk * cols), min(s1, (blk + 1) * cols)
                if lo < hi:
                    pieces.append(refs[tgt][:, pl.ds(t0 + lo - s0, hi - lo)])
            o_ref[blk] = pieces[0] if len(pieces) == 1 else jnp.concatenate(pieces, axis=1)

    return pl.pallas_call(
        body, grid=(d // tr,),
        in_specs=[pl.BlockSpec((tr, 8 * w), lambda i: (i, 0)), pl.BlockSpec((tr, LANES), lambda i: (i, 0))],
        out_specs=pl.BlockSpec((N_DEV, tr, cols), lambda i: (0, i, 0)),
        out_shape=jax.ShapeDtypeStruct((N_DEV, d, cols), dwide.dtype),
        name="relayout_dw_in", compiler_params=_cp("parallel"))(dwide, dnarrow)


def _adamw(w, parts, m, v, tr, name):
    r, c = w.shape
    n_parts = parts.shape[0]

    def body(w_ref, p_ref, m_ref, v_ref, g_ref, d_ref, nm_ref, nv_ref):
        g = p_ref[0].astype(F32)
        for s in range(1, n_parts):
            g = g + p_ref[s].astype(F32)
        m_new = ADAM_B1 * m_ref[...] + (1.0 - ADAM_B1) * g
        v_new = ADAM_B2 * v_ref[...] + (1.0 - ADAM_B2) * (g * g)
        m_hat = m_new / (1.0 - ADAM_B1 ** ADAM_STEP)
        v_hat = v_new / (1.0 - ADAM_B2 ** ADAM_STEP)
        g_ref[...] = g
        d_ref[...] = -ADAM_LR * (m_hat / (jnp.sqrt(v_hat) + ADAM_EPS) + ADAM_WD * w_ref[...])
        nm_ref[...] = m_new
        nv_ref[...] = v_new

    blk = pl.BlockSpec((tr, c), lambda i: (i, 0))
    return pl.pallas_call(
        body, grid=(r // tr,), in_specs=[blk, pl.BlockSpec((n_parts, tr, c), lambda i: (0, i, 0)), blk, blk],
        out_specs=[blk] * 4, out_shape=[jax.ShapeDtypeStruct((r, c), F32)] * 4, name=name,
        compiler_params=_cp("parallel"))(w, parts, m, v)


def _pack_small(d, pre, post, a_log, dt_bias, f_bias, gdn_w, fq_w, fk_w, extra):
    row2 = jnp.concatenate([a_log, dt_bias, f_bias, gdn_w, fq_w, fk_w, extra], axis=1)
    row2 = jnp.pad(row2, ((0, 0), (0, d - row2.shape[1])))
    return jnp.concatenate([pre, post, row2, jnp.zeros((5, d), F32)], axis=0)


def _unpack_small(p, nh):
    o = 3 * nh
    return dict(pre=p[0:1], post=p[1:2], a_log=p[2:3, 0:nh], dt_bias=p[2:3, nh:2 * nh], f_bias=p[2:3, 2 * nh:o],
                gdn_w=p[2:3, o:o + HEAD_DIM], fq_w=p[2:3, o + HEAD_DIM:o + 2 * HEAD_DIM],
                fk_w=p[2:3, o + 2 * HEAD_DIM:o + 3 * HEAD_DIM], extra=p[2, o + 3 * HEAD_DIM])


def kernel(x, meta_tokens, pre_norm_w, w_in, conv_w, a_log, dt_bias, gdn_norm_w, fox_q_norm_w, fox_k_norm_w, fox_f_bias, w_out, post_norm_w, loss_target, m_meta_tokens, m_pre_norm_w, m_w_in, m_conv_w, m_a_log, m_dt_bias, m_gdn_norm_w, m_fox_q_norm_w, m_fox_k_norm_w, m_fox_f_bias, m_w_out, m_post_norm_w, v_meta_tokens, v_pre_norm_w, v_w_in, v_conv_w, v_a_log, v_dt_bias, v_gdn_norm_w, v_fox_q_norm_w, v_fox_k_norm_w, v_fox_f_bias, v_w_out, v_post_norm_w):
    nh = a_log.shape[1]
    d = x.shape[-1]
    w = nh * HEAD_DIM
    zero = jnp.zeros((1, 1), F32)

    wg, wog, cg, mg = _all_gather(
        [_cast_bf16(w_in[0], 256, "cast_w_in"), _cast_bf16(w_out[0], 256, "cast_w_out"), conv_w[0], meta_tokens],
        "gather_weights")
    wp, wsm = _relayout_w_in(wg, nh, 256)
    meta_full = mg.transpose(1, 0, 2).reshape(N_META, d)
    g = _layer_grads(x[0], loss_target[0], meta_full, pre_norm_w, wp, wsm, cg.reshape(3 * w, CONV_WIDTH), a_log, dt_bias,
                     gdn_norm_w, fox_q_norm_w, fox_k_norm_w, fox_f_bias, wog.reshape(2 * w, d), post_norm_w)

    core = lax.axis_index("c")
    dev = 4 * lax.axis_index("x") + 2 * lax.axis_index("y") + core
    core_arr = jnp.reshape(core, (1,)).astype(jnp.int32)
    mine_in = _relayout_dw_in(g["wp"], g["wsm"], nh, 128)
    mine_out = g["w_out"].reshape(N_DEV, 2 * w // N_DEV, d)
    got_in, got_out = _pair_exchange([mine_in, mine_out], "pair_exchange")
    p_in, p_out = _chip_exchange([_pair_sum(mine_in, got_in, core_arr, 128, "pair_sum_w_in"),
                                  _pair_sum(mine_out, got_out, core_arr, 256, "pair_sum_w_out")], "chip_exchange")
    small = _pack_small(d, g["pre_w"], g["post_w"], g["a_log"], g["dt_bias"], g["f_bias"], g["gdn_norm_w"], g["fq_w"],
                        g["fk_w"], g["loss"])
    a_conv, a_meta, p_small = _all_gather([g["conv_w"], g["meta"], small], "gather_small_grads")
    p_conv = lax.dynamic_slice_in_dim(a_conv, dev * conv_w.shape[1], conv_w.shape[1], axis=1)
    p_meta = lax.dynamic_slice_in_dim(a_meta, dev * meta_tokens.shape[1], meta_tokens.shape[1], axis=2)

    r_in = _adamw(w_in[0], p_in, m_w_in[0], v_w_in[0], 128, "adamw_w_in")
    r_out = _adamw(w_out[0], p_out, m_w_out[0], v_w_out[0], 64, "adamw_w_out")
    r_conv = _adamw(conv_w[0], p_conv, m_conv_w[0], v_conv_w[0], conv_w.shape[1], "adamw_conv_w")
    r_meta = _adamw(meta_tokens, p_meta, m_meta_tokens, v_meta_tokens, N_META, "adamw_meta")
    pk = lambda pre, post, a, dt, gw, fq, fk, fb: _pack_small(d, pre, post, a, dt, fb, gw, fq, fk, zero)
    r_small = _adamw(
        pk(pre_norm_w, post_norm_w, a_log, dt_bias, gdn_norm_w, fox_q_norm_w, fox_k_norm_w, fox_f_bias), p_small,
        pk(m_pre_norm_w, m_post_norm_w, m_a_log, m_dt_bias, m_gdn_norm_w, m_fox_q_norm_w, m_fox_k_norm_w, m_fox_f_bias),
        pk(v_pre_norm_w, v_post_norm_w, v_a_log, v_dt_bias, v_gdn_norm_w, v_fox_q_norm_w, v_fox_k_norm_w, v_fox_f_bias),
        8, "adamw_small")

    sm = [_unpack_small(r, nh) for r in r_small]
    outs = []
    for i in range(4):
        s = sm[i]
        outs += [r_meta[i], s["pre"], r_in[i][None], r_conv[i][None], s["a_log"], s["dt_bias"], s["gdn_w"], s["fq_w"],
                 s["fk_w"], s["f_bias"], r_out[i][None], s["post"]]
    return (sm[0]["extra"], g["grad_x"][None], *outs)
```

```python
import jax
import jax.numpy as jnp
from jax import lax
from jax.experimental import pallas as pl
from jax.experimental.pallas import tpu as pltpu

F32, BF16 = jnp.float32, jnp.bfloat16
HEAD_DIM = 128
N_META = 16
CONV_WIDTH = 4
CHUNK = 64
Q_BLOCK = 128
LANES = 128
EPS = 1e-6
PAD_ROWS = Q_BLOCK - N_META
N_DEV = 8
N_CHIP = 4
VMEM_LIMIT = 56 * 1024 * 1024
NEG = -1e30
MM_TILE = 5 * LANES

ADAM_LR, ADAM_B1, ADAM_B2, ADAM_EPS, ADAM_WD, ADAM_STEP = 0.001, 0.9, 0.999, 1e-08, 0.01, 10

_DN = {"nn": (((1,), (0,)), ((), ())), "nt": (((1,), (1,)), ((), ())), "tn": (((0,), (0,)), ((), ()))}
_DN3 = {"nn": (((2,), (1,)), ((0,), (0,))), "nt": (((2,), (2,)), ((0,), (0,))), "tn": (((1,), (1,)), ((0,), (0,)))}
_ANY = pl.BlockSpec(memory_space=pl.ANY)
_MESH = pl.DeviceIdType.MESH


def _cp(*sem):
    return pltpu.CompilerParams(dimension_semantics=sem, vmem_limit_bytes=VMEM_LIMIT)


def _dot(a, b, dims="nn", prec=None):
    return lax.dot_general(a, b, _DN[dims], precision=prec, preferred_element_type=F32)


def _bdot(a, b, dims="nn"):
    return _dot(a.astype(BF16), b.astype(BF16), dims)


def _hdot(a, b, dims="nn"):
    return _dot(a, b, dims, prec=lax.Precision.HIGHEST)


def _dot3(a, b, dims="nn"):
    return lax.dot_general(a, b, _DN3[dims], preferred_element_type=F32)


def _bdot3(a, b, dims="nn"):
    return _dot3(a.astype(BF16), b.astype(BF16), dims)


def _split(a):
    hi = a.astype(BF16)
    return hi, (a - hi.astype(F32)).astype(BF16)


def _iota(shape, dim):
    return lax.broadcasted_iota(jnp.int32, shape, dim)


def _sigmoid(z):
    return 1.0 / (1.0 + jnp.exp(-z))


def _softplus(z):
    e = jnp.exp(-jnp.abs(z))
    u = 1.0 + e
    l1p = jnp.where(u == 1.0, e, jnp.log(u) * (e / jnp.where(u == 1.0, 1.0, u - 1.0)))
    return jnp.maximum(z, 0.0) + l1p


def _silu_and_grad(z):
    s = _sigmoid(z)
    return z * s, s * (1.0 + z * (1.0 - s))


def _rms(x):
    return lax.rsqrt(jnp.mean(x * x, axis=-1, keepdims=True) + EPS)


def _h_tile(i, x_ref, meta_ref):
    first = jnp.concatenate([jnp.zeros((PAD_ROWS, x_ref.shape[1]), F32), meta_ref[...]], axis=0)
    return jnp.where(i == 0, first, x_ref[...])


def _x_rows(d):
    return pl.BlockSpec((Q_BLOCK, d), lambda i: (jnp.maximum(i - 1, 0), 0))


def _prenorm(x, meta, w):
    seq, d = x.shape
    lp = seq + Q_BLOCK

    def body(x_ref, m_ref, w_ref, o_ref):
        h = _h_tile(pl.program_id(0), x_ref, m_ref)
        o_ref[...] = (h * _rms(h) * w_ref[...]).astype(BF16)

    return pl.pallas_call(
        body, grid=(lp // Q_BLOCK,),
        in_specs=[_x_rows(d), pl.BlockSpec((N_META, d), lambda i: (0, 0)), pl.BlockSpec((1, d), lambda i: (0, 0))],
        out_specs=pl.BlockSpec((Q_BLOCK, d), lambda i: (i, 0)),
        out_shape=jax.ShapeDtypeStruct((lp, d), BF16), name="prenorm", compiler_params=_cp("parallel"))(x, meta, w)


def _tile(n, want):
    return max(t for t in range(LANES, want + 1, LANES) if n % t == 0)


def _matmul(a, b, dims, tn, out_dtype, name):
    m = a.shape[1] if dims == "tn" else a.shape[0]
    n = b.shape[0] if dims == "nt" else b.shape[1]
    kdim = b.shape[1] if dims == "nt" else b.shape[0]
    tn = _tile(n, tn)
    b_spec = pl.BlockSpec((tn, kdim), lambda j: (j, 0)) if dims == "nt" else pl.BlockSpec((kdim, tn), lambda j: (0, j))

    def body(a_ref, b_ref, o_ref):
        o_ref[...] = _dot(a_ref[...], b_ref[...], dims).astype(out_dtype)

    return pl.pallas_call(
        body, grid=(n // tn,),
        in_specs=[pl.BlockSpec(a.shape, lambda j: (0, 0)), b_spec],
        out_specs=pl.BlockSpec((m, tn), lambda j: (0, j)),
        out_shape=jax.ShapeDtypeStruct((m, n), out_dtype), name=name, compiler_params=_cp("parallel"))(a, b)


def _chip_copies(ins, outs, send_sems, recv_sems, local_sems):
    if not ins:
        return [], [], []
    x, y, c = lax.axis_index("x"), lax.axis_index("y"), lax.axis_index("c")
    mine = 2 * x + y
    local, sends, recvs = [], [], []
    for a in range(len(ins)):
        local.append(pltpu.make_async_copy(ins[a].at[mine], outs[a].at[mine], local_sems.at[a]))
        for k in range(1, N_CHIP):
            px = 1 - x if k & 2 else x
            py = 1 - y if k & 1 else y
            kw = dict(send_sem=send_sems.at[a, k - 1], recv_sem=recv_sems.at[a, k - 1], device_id=(px, py, c),
                      device_id_type=_MESH)
            sends.append(pltpu.make_async_remote_copy(src_ref=ins[a].at[2 * px + py], dst_ref=outs[a].at[mine], **kw))
            recvs.append(pltpu.make_async_remote_copy(src_ref=ins[a].at[mine], dst_ref=outs[a].at[2 * px + py], **kw))
    return local, sends, recvs


def _dxn_and_chip_exchange(dproj, wfull, sums, tk, name):
    m, k = dproj.shape
    n = wfull.shape[0]
    na = len(sums)
    tk = _tile(k, tk)
    steps = k // tk

    def body(a_ref, b_ref, *rest):
        ins, o_ref, outs = rest[:na], rest[na], rest[na + 1:2 * na + 1]
        local, sends, recvs = _chip_copies(ins, outs, *rest[2 * na + 1:])
        j = pl.program_id(0)

        @pl.when(j == 0)
        def _():
            for cp in local + sends:
                cp.start()
            o_ref[...] = jnp.zeros_like(o_ref)
        o_ref[...] += _dot(a_ref[...], b_ref[...], "nt")

        @pl.when(j == steps - 1)
        def _():
            for cp in local:
                cp.wait()
            for cp in sends:
                cp.wait_send()
            for cp in recvs:
                cp.wait_recv()

    res = pl.pallas_call(
        body, grid=(steps,),
        in_specs=[pl.BlockSpec((m, tk), lambda j: (0, j)), pl.BlockSpec((n, tk), lambda j: (0, j))] + [_ANY] * na,
        out_specs=[pl.BlockSpec((m, n), lambda j: (0, 0))] + [_ANY] * na,
        out_shape=[jax.ShapeDtypeStruct((m, n), F32)] + [jax.ShapeDtypeStruct(s.shape, s.dtype) for s in sums],
        scratch_shapes=[pltpu.SemaphoreType.DMA((max(na, 1), N_CHIP - 1)), pltpu.SemaphoreType.DMA((max(na, 1), N_CHIP - 1)),
                        pltpu.SemaphoreType.DMA((max(na, 1),))],
        name=name, compiler_params=_cp("arbitrary"))(dproj, wfull, *sums)
    return res[0], res[1:]


def _conv_taps(x, w):
    c = x * w[CONV_WIDTH - 1:CONV_WIDTH, :]
    for j in range(CONV_WIDTH - 1):
        c = c + pltpu.roll(x, CONV_WIDTH - 1 - j, 0) * w[j:j + 1, :]
    return c


def _gdn_prep(proj, conv_wt, nh):
    lp = proj.shape[0]
    scale = HEAD_DIM ** -0.5

    def body(x_ref, w_ref, o_ref):
        which = pl.program_id(0) // nh
        c = _conv_taps(x_ref[...], w_ref[...])
        s = c * _sigmoid(c)
        r = lax.rsqrt(jnp.sum(s * s, axis=-1, keepdims=True) + EPS)
        f = jnp.where(which == 0, r * scale, jnp.where(which == 1, r, 1.0))
        o_ref[...] = jnp.where(_iota(s.shape, 0) >= PAD_ROWS, s * f, 0.0)

    return pl.pallas_call(
        body, grid=(3 * nh,),
        in_specs=[pl.BlockSpec((lp, LANES), lambda s: (0, s)), pl.BlockSpec((CONV_WIDTH, LANES), lambda s: (0, s))],
        out_specs=pl.BlockSpec((lp, LANES), lambda s: (0, s)),
        out_shape=jax.ShapeDtypeStruct((lp, 3 * nh * HEAD_DIM), F32), name="gdn_prep",
        compiler_params=_cp("parallel"))(proj, conv_wt)


def _gdn_prep_bwd(proj, conv_wt, dq, dk, dv, dproj, nh):
    lp = proj.shape[0]
    scale = HEAD_DIM ** -0.5
    part = lambda p: pl.BlockSpec((lp, LANES), lambda s: (0, jnp.clip(s - p * nh, 0, nh - 1)))

    def body(x_ref, w_ref, dq_ref, dk_ref, dv_ref, _, dx_ref, dw_ref):
        which = pl.program_id(0) // nh
        x = x_ref[...]
        w = w_ref[...]
        c = _conv_taps(x, w)
        sg = _sigmoid(c)
        s = c * sg
        r = lax.rsqrt(jnp.sum(s * s, axis=-1, keepdims=True) + EPS)
        dy = jnp.where(which == 0, dq_ref[...], jnp.where(which == 1, dk_ref[...], dv_ref[...]))
        dy = jnp.where(_iota(s.shape, 0) >= PAD_ROWS, dy, 0.0)
        y0 = s * r
        dy0 = dy * jnp.where(which == 0, scale, 1.0)
        ds_n = r * (dy0 - y0 * jnp.sum(dy0 * y0, axis=-1, keepdims=True))
        ds = jnp.where(which == 2, dy, ds_n)
        dc = ds * (sg * (1.0 + c * (1.0 - sg)))
        dx = dc * w[CONV_WIDTH - 1:CONV_WIDTH, :]
        rows = [jnp.sum(dc * x, axis=0, keepdims=True)]
        for j in range(CONV_WIDTH - 2, -1, -1):
            sh = CONV_WIDTH - 1 - j
            dx = dx + pltpu.roll(dc, lp - sh, 0) * w[j:j + 1, :]
            rows.insert(0, jnp.sum(dc * pltpu.roll(x, sh, 0), axis=0, keepdims=True))
        dx_ref[...] = dx.astype(BF16)
        dw_ref[...] = jnp.concatenate(rows, axis=0)

    strip = pl.BlockSpec((lp, LANES), lambda s: (0, s))
    taps = pl.BlockSpec((CONV_WIDTH, LANES), lambda s: (0, s))
    return pl.pallas_call(
        body, grid=(3 * nh,), in_specs=[strip, taps, part(0), part(1), part(2), _ANY], out_specs=[strip, taps],
        out_shape=[jax.ShapeDtypeStruct(dproj.shape, BF16), jax.ShapeDtypeStruct((CONV_WIDTH, 3 * nh * HEAD_DIM), F32)],
        input_output_aliases={5: 0}, name="gdn_prep_bwd", compiler_params=_cp("parallel"))(proj, conv_wt, dq, dk, dv, dproj)


def _gates(proj, bias_row, nega_row, nh):
    lp = proj.shape[0]
    nc = lp // CHUNK

    def body(p_ref, b_ref, a_ref, g_ref, gt3_ref, gtf_ref):
        lane = _iota((CHUNK, LANES), 1)
        tri = (_iota((CHUNK, CHUNK), 0) >= _iota((CHUNK, CHUNK), 1)).astype(F32)

        def step(n, carry):
            r0 = pl.multiple_of(n * CHUNK, CHUNK)
            z = p_ref[pl.ds(r0, CHUNK), :] + b_ref[...]
            base = jnp.where(lane < nh, _sigmoid(z),
                             jnp.where(lane < 2 * nh, a_ref[...] * _softplus(z),
                                       jnp.where(lane < 3 * nh, -_softplus(-z), 0.0)))
            base = jnp.where(r0 + _iota((CHUNK, LANES), 0) >= PAD_ROWS, base, 0.0)
            cs = _hdot(tri, base)
            run = jnp.where((lane >= 2 * nh) & (lane < 3 * nh), cs + carry, cs)
            sh = pltpu.roll(run, 2 * nh, 1)
            out = base + jnp.where((lane >= 3 * nh) & (lane < 5 * nh), sh, 0.0)
            g_ref[pl.ds(r0, CHUNK), :] = out
            gt3_ref[n] = out.T
            return carry + cs[CHUNK - 1:CHUNK, :]

        lax.fori_loop(0, nc, step, jnp.zeros((1, LANES), F32))
        gtf_ref[...] = g_ref[...].T

    vec = pl.BlockSpec((1, LANES), lambda i: (0, 0))
    return pl.pallas_call(
        body, grid=(1,), in_specs=[pl.BlockSpec((lp, LANES), lambda i: (0, 8 * nh)), vec, vec],
        out_specs=[pl.BlockSpec((lp, LANES), lambda i: (0, 0)), pl.BlockSpec((nc, LANES, CHUNK), lambda i: (0, 0, 0)),
                   pl.BlockSpec((LANES, lp), lambda i: (0, 0))],
        out_shape=[jax.ShapeDtypeStruct((lp, LANES), F32), jax.ShapeDtypeStruct((nc, LANES, CHUNK), F32),
                   jax.ShapeDtypeStruct((LANES, lp), F32)],
        name="gates", compiler_params=_cp("arbitrary"))(proj, bias_row, nega_row)


def _gates_bwd(proj, bias_row, nega_row, gates, dgate_gdn, dc_t, dproj, nh):
    lp = proj.shape[0]
    nc = lp // CHUNK

    def body(p_ref, b_ref, a_ref, g_ref, dg_ref, dc_ref, _, dz_ref, sm_ref, dct_scr):
        lane = _iota((CHUNK, LANES), 1)
        triu = (_iota((CHUNK, CHUNK), 0) <= _iota((CHUNK, CHUNK), 1)).astype(F32)
        dct_scr[...] = dc_ref[...].T
        sm_ref[...] = jnp.zeros_like(sm_ref)

        def step(i, carry):
            n = nc - 1 - i
            r0 = pl.multiple_of(n * CHUNK, CHUNK)
            z = p_ref[pl.ds(r0, CHUNK), :] + b_ref[...]
            gt = g_ref[pl.ds(r0, CHUNK), :]
            dgd = dg_ref[pl.ds(r0, CHUNK), :]
            dch = dct_scr[pl.ds(r0, CHUNK), :]
            rc = _hdot(triu, dch) + carry
            sg = _sigmoid(z)
            dz = jnp.where(lane < nh, dgd * sg * (1.0 - sg),
                           jnp.where(lane < 2 * nh, dgd * a_ref[...] * sg,
                                     jnp.where(lane < 3 * nh, rc * (1.0 - sg), 0.0)))
            dz = jnp.where(r0 + _iota((CHUNK, LANES), 0) >= PAD_ROWS, dz, 0.0)
            dz_ref[pl.ds(r0, CHUNK), :] = dz.astype(BF16)
            sm_ref[0:1, :] += jnp.sum(dz, axis=0, keepdims=True)
            sm_ref[1:2, :] += jnp.sum(jnp.where((lane >= nh) & (lane < 2 * nh), dgd * gt, 0.0), axis=0, keepdims=True)
            return carry + jnp.sum(dch, axis=0, keepdims=True)

        lax.fori_loop(0, nc, step, jnp.zeros((1, LANES), F32))

    vec = pl.BlockSpec((1, LANES), lambda i: (0, 0))
    full = pl.BlockSpec((lp, LANES), lambda i: (0, 0))
    last = pl.BlockSpec((lp, LANES), lambda i: (0, 8 * nh))
    return pl.pallas_call(
        body, grid=(1,), in_specs=[last, vec, vec, full, full, pl.BlockSpec((LANES, lp), lambda i: (0, 0)), _ANY],
        out_specs=[last, pl.BlockSpec((8, LANES), lambda i: (0, 0))],
        out_shape=[jax.ShapeDtypeStruct(dproj.shape, BF16), jax.ShapeDtypeStruct((8, LANES), F32)],
        scratch_shapes=[pltpu.VMEM((lp, LANES), F32)], input_output_aliases={6: 0},
        name="gates_bwd", compiler_params=_cp("arbitrary"))(proj, bias_row, nega_row, gates, dgate_gdn, dc_t, dproj)


def _tri_inv(a):
    t = jnp.where(_iota(a.shape, 1) == _iota(a.shape, 2), 1.0, 0.0) - a
    p = a
    for _ in range(5):
        ph, pw = _split(p)
        p = _dot3(ph, ph) + (_dot3(ph, pw) + _dot3(pw, ph))
        ph, pw = _split(p)
        th, tw = _split(t)
        t = t + (_dot3(th, ph) + (_dot3(th, pw) + _dot3(tw, ph)))
    return t


def _gdn_chunk(q, k, v, beta, gc, gr):
    ii, jj = _iota((1, CHUNK, CHUNK), 1), _iota((1, CHUNK, CHUNK), 2)
    causal, strict = ii >= jj, ii > jj
    dm = jnp.where(causal, jnp.exp(jnp.where(causal, gc - gr, 0.0)), 0.0)
    kk = _bdot3(k, k, "nt")
    a = jnp.where(strict, beta * kk * dm, 0.0)
    t = _tri_inv(a)
    eg = jnp.exp(gc)
    glast = gc[:, CHUNK - 1:CHUNK, :]
    ekd = jnp.exp(glast - gc)
    bv = beta * v
    bk = (beta * eg) * k
    ub = _bdot3(t, jnp.concatenate([bv, bk], axis=2))
    qk = _bdot3(q, k, "nt")
    return dict(causal=causal, strict=strict, dm=dm, kk=kk, a=a, t=t, eg=eg, ekd=ekd, bv=bv, bk=bk,
                u=ub[:, :, :HEAD_DIM], w=ub[:, :, HEAD_DIM:], qk=qk, aqk=jnp.where(causal, qk * dm, 0.0),
                q_dec=q * eg, k_dec=k * ekd, decay=jnp.exp(glast))


def _heads(ref, nh):
    return jnp.stack([ref[:, h * HEAD_DIM:(h + 1) * HEAD_DIM] for h in range(nh)], axis=0)


def _gdn_chunk_inputs(q_ref, k_ref, v_ref, g, gt, nh):
    col = lambda o: jnp.stack([g[:, o + h:o + h + 1] for h in range(nh)], axis=0)
    gr = jnp.stack([gt[3 * nh + h:3 * nh + h + 1, :] for h in range(nh)], axis=0)
    return _heads(q_ref, nh), _heads(k_ref, nh), _heads(v_ref, nh), col(0), col(3 * nh), gr


def _gdn_fwd(qkv, gates, gt3, nh):
    lp = qkv.shape[0]
    nc = lp // CHUNK
    w = nh * HEAD_DIM

    def body(q_ref, k_ref, v_ref, g_ref, gt_ref, o_ref, sall_ref, s_scr):
        @pl.when(pl.program_id(0) == 0)
        def _():
            s_scr[...] = jnp.zeros_like(s_scr)
        c = _gdn_chunk(*_gdn_chunk_inputs(q_ref, k_ref, v_ref, g_ref[...], gt_ref[0], nh))
        s = s_scr[...]
        sall_ref[0] = s
        v_new = c["u"] - _bdot3(c["w"], s)
        o = _bdot3(c["q_dec"], s) + _bdot3(c["aqk"], v_new)
        s_scr[...] = s * c["decay"] + _bdot3(c["k_dec"], v_new, "tn")
        for h in range(nh):
            o_ref[:, h * HEAD_DIM:(h + 1) * HEAD_DIM] = o[h]

    return pl.pallas_call(
        body, grid=(nc,),
        in_specs=[pl.BlockSpec((CHUNK, w), lambda n: (n, 0)), pl.BlockSpec((CHUNK, w), lambda n: (n, 1)),
                  pl.BlockSpec((CHUNK, w), lambda n: (n, 2)), pl.BlockSpec((CHUNK, LANES), lambda n: (n, 0)),
                  pl.BlockSpec((1, LANES, CHUNK), lambda n: (n, 0, 0))],
        out_specs=[pl.BlockSpec((CHUNK, w), lambda n: (n, 0)),
                   pl.BlockSpec((1, nh, HEAD_DIM, HEAD_DIM), lambda n: (n, 0, 0, 0))],
        out_shape=[jax.ShapeDtypeStruct((lp, w), F32), jax.ShapeDtypeStruct((nc, nh, HEAD_DIM, HEAD_DIM), F32)],
        scratch_shapes=[pltpu.VMEM((nh, HEAD_DIM, HEAD_DIM), F32)],
        name="gdn_fwd", compiler_params=_cp("arbitrary"))(qkv, qkv, qkv, gates, gt3)


def _gdn_bwd(qkv, gates, gt3, s_all, do, nh):
    lp = qkv.shape[0]
    nc = lp // CHUNK
    w = nh * HEAD_DIM
    rev = lambda n: nc - 1 - n

    def body(q_ref, k_ref, v_ref, g_ref, gt_ref, s_ref, do_ref, dq_ref, dk_ref, dv_ref, dg_ref, ds_scr):
        @pl.when(pl.program_id(0) == 0)
        def _():
            ds_scr[...] = jnp.zeros_like(ds_scr)
        q, k, v, beta, gc, gr = _gdn_chunk_inputs(q_ref, k_ref, v_ref, g_ref[...], gt_ref[0], nh)
        c = _gdn_chunk(q, k, v, beta, gc, gr)
        s = s_ref[0]
        dsn = ds_scr[...]
        dout = _heads(do_ref, nh)
        v_new = c["u"] - _bdot3(c["w"], s)
        dq_dec = _bdot3(dout, s, "nt")
        daqk = jnp.where(c["causal"], _bdot3(dout, v_new, "nt"), 0.0)
        dv_new = _bdot3(c["aqk"], dout, "tn") + _bdot3(c["k_dec"], dsn)
        dk_dec = _bdot3(v_new, dsn, "nt")
        ddecay = jnp.sum(jnp.sum(dsn * s, axis=2, keepdims=True), axis=1, keepdims=True)
        dw = -_bdot3(dv_new, s, "nt")
        ds_scr[...] = _bdot3(c["q_dec"], dout, "tn") + c["decay"] * dsn - _bdot3(c["w"], dv_new, "tn")
        duw = jnp.concatenate([dv_new, dw], axis=2)
        dt = _bdot3(duw, jnp.concatenate([c["bv"], c["bk"]], axis=2), "nt")
        dbvk = _bdot3(c["t"], duw, "tn")
        dbv, dbk = dbvk[:, :, :HEAD_DIM], dbvk[:, :, HEAD_DIM:]
        da = jnp.where(c["strict"], -_bdot3(_bdot3(c["t"], dt, "tn"), c["t"], "nt"), 0.0)
        dkk = da * beta * c["dm"]
        dqk = daqk * c["dm"]
        e = da * c["a"] + daqk * c["aqk"]
        dq = dq_dec * c["eg"] + _bdot3(dqk, k)
        dk = (dk_dec * c["ekd"] + _bdot3(dkk, k) + _bdot3(dkk, k, "tn") + _bdot3(dqk, q, "tn")
              + (beta * c["eg"]) * dbk)
        dv = beta * dbv
        rs = lambda x: jnp.sum(x, axis=2, keepdims=True)
        dbeta = rs(dbv * v) + c["eg"] * rs(dbk * k) + rs(da * c["kk"] * c["dm"])
        kd_term = rs(dk_dec * c["k_dec"])
        eh, ew = _split(e)
        ones = jnp.ones((nh, CHUNK, LANES), BF16)
        col_sums = (_dot3(eh, ones, "tn") + _dot3(ew, ones, "tn"))[:, :, 0:1]
        dg_cum = rs(dq_dec * c["q_dec"]) - kd_term + rs(dbk * c["bk"]) + rs(e) - col_sums
        last = jnp.sum(kd_term, axis=1, keepdims=True) + ddecay * c["decay"]
        dg_cum = dg_cum + jnp.where(_iota((1, CHUNK, 1), 1) == CHUNK - 1, last, 0.0)
        lane = _iota((CHUNK, LANES), 1)
        acc = jnp.zeros((CHUNK, LANES), F32)
        for h in range(nh):
            sl = slice(h * HEAD_DIM, (h + 1) * HEAD_DIM)
            dq_ref[:, sl] = dq[h]
            dk_ref[:, sl] = dk[h]
            dv_ref[:, sl] = dv[h]
            acc = acc + jnp.where(lane == h, dbeta[h], 0.0) + jnp.where(lane == nh + h, dg_cum[h], 0.0)
        triu = (_iota((CHUNK, CHUNK), 0) <= _iota((CHUNK, CHUNK), 1)).astype(F32)
        dg_ref[...] = jnp.where(lane < nh, acc, _hdot(triu, acc))

    return pl.pallas_call(
        body, grid=(nc,),
        in_specs=[pl.BlockSpec((CHUNK, w), lambda n: (rev(n), 0)), pl.BlockSpec((CHUNK, w), lambda n: (rev(n), 1)),
                  pl.BlockSpec((CHUNK, w), lambda n: (rev(n), 2)), pl.BlockSpec((CHUNK, LANES), lambda n: (rev(n), 0)),
                  pl.BlockSpec((1, LANES, CHUNK), lambda n: (rev(n), 0, 0)),
                  pl.BlockSpec((1, nh, HEAD_DIM, HEAD_DIM), lambda n: (rev(n), 0, 0, 0)),
                  pl.BlockSpec((CHUNK, w), lambda n: (rev(n), 0))],
        out_specs=[pl.BlockSpec((CHUNK, w), lambda n: (rev(n), 0))] * 3 + [pl.BlockSpec((CHUNK, LANES), lambda n: (rev(n), 0))],
        out_shape=[jax.ShapeDtypeStruct((lp, w), F32)] * 3 + [jax.ShapeDtypeStruct((lp, LANES), F32)],
        scratch_shapes=[pltpu.VMEM((nh, HEAD_DIM, HEAD_DIM), F32)],
        name="gdn_bwd", compiler_params=_cp("arbitrary"))(qkv, qkv, qkv, gates, gt3, s_all, do)


def _merge_gdn(o_gdn, proj, norm_w, nh):
    lp = o_gdn.shape[0]

    def body(o_ref, z_ref, w_ref, m_ref):
        o = o_ref[...]
        z = z_ref[...]
        m_ref[...] = (o * _rms(o) * w_ref[...] * (z * _sigmoid(z))).astype(BF16)

    return pl.pallas_call(
        body, grid=(nh,),
        in_specs=[pl.BlockSpec((lp, LANES), lambda s: (0, s)), pl.BlockSpec((lp, LANES), lambda s: (0, 3 * nh + s)),
                  pl.BlockSpec((1, LANES), lambda s: (0, 0))],
        out_specs=pl.BlockSpec((lp, LANES), lambda s: (0, s)),
        out_shape=jax.ShapeDtypeStruct((lp, 2 * nh * HEAD_DIM), BF16), name="merge_gdn",
        compiler_params=_cp("parallel"))(o_gdn, proj, norm_w)


def _merge_gdn_bwd(o_gdn, proj, norm_w, dmerged, nh):
    lp = o_gdn.shape[0]

    def body(o_ref, z_ref, w_ref, dm_ref, do_ref, dz_ref, dw_ref):
        o = o_ref[...]
        r = _rms(o)
        xh = o * r
        silu, dsilu = _silu_and_grad(z_ref[...])
        dm = dm_ref[...]
        dn = dm * silu
        dz_ref[...] = (dm * (xh * w_ref[...]) * dsilu).astype(BF16)
        dnw = dn * w_ref[...]
        do_ref[...] = r * (dnw - xh * jnp.mean(dnw * xh, axis=-1, keepdims=True))

        @pl.when(pl.program_id(0) == 0)
        def _():
            dw_ref[...] = jnp.zeros_like(dw_ref)
        dw_ref[...] += jnp.sum(dn * xh, axis=0, keepdims=True)

    w = nh * HEAD_DIM
    return pl.pallas_call(
        body, grid=(nh,),
        in_specs=[pl.BlockSpec((lp, LANES), lambda s: (0, s)), pl.BlockSpec((lp, LANES), lambda s: (0, 3 * nh + s)),
                  pl.BlockSpec((1, LANES), lambda s: (0, 0)), pl.BlockSpec((lp, LANES), lambda s: (0, s))],
        out_specs=[pl.BlockSpec((lp, LANES), lambda s: (0, s)), pl.BlockSpec((lp, LANES), lambda s: (0, 3 * nh + s)),
                   pl.BlockSpec((1, LANES), lambda s: (0, 0))],
        out_shape=[jax.ShapeDtypeStruct((lp, w), F32), jax.ShapeDtypeStruct((lp, 8 * w + LANES), BF16),
                   jax.ShapeDtypeStruct((1, LANES), F32)],
        name="merge_gdn_bwd", compiler_params=_cp("arbitrary"))(o_gdn, proj, norm_w, dmerged)


def _fox_prep(proj, qk_w, nh):
    lp = proj.shape[0]

    def body(x_ref, w_ref, o_ref):
        x = x_ref[...]
        o_ref[...] = x * _rms(x) * w_ref[0]

    return pl.pallas_call(
        body, grid=(2 * nh,),
        in_specs=[pl.BlockSpec((lp, LANES), lambda s: (0, 4 * nh + s)), pl.BlockSpec((1, 1, LANES), lambda s: (s // nh, 0, 0))],
        out_specs=pl.BlockSpec((lp, LANES), lambda s: (0, s)),
        out_shape=jax.ShapeDtypeStruct((lp, 2 * nh * HEAD_DIM), F32), name="fox_prep",
        compiler_params=_cp("parallel"))(proj, qk_w)


def _fox_prep_bwd(proj, qk_w, dq, dk, dproj, nh):
    lp = proj.shape[0]
    part = lambda p: pl.BlockSpec((lp, LANES), lambda s: (0, jnp.clip(s - p * nh, 0, nh - 1)))

    def body(x_ref, w_ref, dq_ref, dk_ref, _, dx_ref, dw_ref):
        x = x_ref[...]
        r = _rms(x)
        xh = x * r
        dy = jnp.where(pl.program_id(0) < nh, dq_ref[...], dk_ref[...])
        dyw = dy * w_ref[0]
        dx_ref[...] = (r * (dyw - xh * jnp.mean(dyw * xh, axis=-1, keepdims=True))).astype(BF16)

        @pl.when(pl.program_id(0) % nh == 0)
        def _():
            dw_ref[...] = jnp.zeros_like(dw_ref)
        dw_ref[0] += jnp.sum(dy * xh, axis=0, keepdims=True)

    strip = pl.BlockSpec((lp, LANES), lambda s: (0, 4 * nh + s))
    wsp = pl.BlockSpec((1, 1, LANES), lambda s: (s // nh, 0, 0))
    return pl.pallas_call(
        body, grid=(2 * nh,), in_specs=[strip, wsp, part(0), part(1), _ANY], out_specs=[strip, wsp],
        out_shape=[jax.ShapeDtypeStruct(dproj.shape, BF16), jax.ShapeDtypeStruct((2, 1, LANES), F32)],
        input_output_aliases={4: 0}, name="fox_prep_bwd", compiler_params=_cp("arbitrary"))(proj, qk_w, dq, dk, dproj)


def _fox_probs(q, k, gates, crow, h, i, nh):
    kl = k.shape[0]
    lane = _iota((Q_BLOCK, LANES), 1)
    ct = jnp.sum(jnp.where(lane == 4 * nh + h, gates, 0.0), axis=1, keepdims=True)
    s = _bdot(q, k, "nt") * (HEAD_DIM ** -0.5) + (ct - crow)
    t = i * Q_BLOCK + _iota((Q_BLOCK, kl), 0)
    kp = _iota((Q_BLOCK, kl), 1)
    s = jnp.where((kp <= t) & ((kp >= PAD_ROWS) | (t < PAD_ROWS)), s, NEG)
    p = jnp.exp(s - jnp.max(s, axis=1, keepdims=True))
    return p / jnp.sum(p, axis=1, keepdims=True)


def _fox_specs(lp, nh):
    return [pl.BlockSpec((Q_BLOCK, LANES), lambda h, i: (i, h)),
            pl.BlockSpec((lp, LANES), lambda h, i: (0, nh + h)),
            pl.BlockSpec((lp, LANES), lambda h, i: (0, 6 * nh + h)),
            pl.BlockSpec((Q_BLOCK, LANES), lambda h, i: (i, 0)),
            pl.BlockSpec((LANES, lp), lambda h, i: (0, 0))]


def _fox_fwd(qkn, proj, gates, gtf, nh):
    lp = qkn.shape[0]

    def body(q_ref, k_ref, v_ref, g_ref, gt_ref, o_ref):
        h, i = pl.program_id(0), pl.program_id(1)
        for j in range(lp // Q_BLOCK):
            @pl.when(i == j)
            def _(j=j):
                kl = (j + 1) * Q_BLOCK
                p = _fox_probs(q_ref[...], k_ref[0:kl, :], g_ref[...], gt_ref[pl.ds(4 * nh + h, 1), :][:, 0:kl], h, j, nh)
                o_ref[...] = _bdot(p, v_ref[0:kl, :])

    return pl.pallas_call(
        body, grid=(nh, lp // Q_BLOCK), in_specs=_fox_specs(lp, nh),
        out_specs=pl.BlockSpec((Q_BLOCK, LANES), lambda h, i: (i, h)),
        out_shape=jax.ShapeDtypeStruct((lp, nh * HEAD_DIM), F32), name="fox_fwd",
        compiler_params=_cp("parallel", "parallel"))(qkn, qkn, proj, gates, gtf)


def _fox_bwd(qkn, proj, gates, gtf, do, dproj, nh):
    lp = qkn.shape[0]
    nq = lp // Q_BLOCK
    w = nh * HEAD_DIM
    scale = HEAD_DIM ** -0.5

    def body(q_ref, k_ref, v_ref, g_ref, gt_ref, do_ref, _, dq_ref, dk_ref, dc_ref, dv_ref, dv_scr):
        h, i = pl.program_id(0), pl.program_id(1)

        @pl.when(i == 0)
        def _():
            dk_ref[...] = jnp.zeros_like(dk_ref)
            dv_scr[...] = jnp.zeros_like(dv_scr)
            dc_ref[...] = jnp.zeros_like(dc_ref)
        for j in range(nq):
            @pl.when(i == j)
            def _(j=j):
                kl = (j + 1) * Q_BLOCK
                q, k = q_ref[...], k_ref[0:kl, :]
                p = _fox_probs(q, k, g_ref[...], gt_ref[pl.ds(4 * nh + h, 1), :][:, 0:kl], h, j, nh)
                dout = do_ref[...]
                dp = _bdot(dout, v_ref[0:kl, :], "nt")
                ds = p * (dp - jnp.sum(p * dp, axis=1, keepdims=True))
                dq_ref[...] = _bdot(ds, k) * scale
                dk_ref[0:kl, :] += _bdot(ds, q, "tn") * scale
                dv_scr[0:kl, :] += _bdot(p, dout, "tn")
                dc_ref[0, :, 0:kl] -= jnp.sum(ds, axis=0, keepdims=True)

        @pl.when(i == nq - 1)
        def _():
            dv_ref[...] = dv_scr[...].astype(BF16)

    blk = pl.BlockSpec((Q_BLOCK, LANES), lambda h, i: (i, h))
    col = pl.BlockSpec((lp, LANES), lambda h, i: (0, h))
    return pl.pallas_call(
        body, grid=(nh, nq), in_specs=_fox_specs(lp, nh) + [blk, _ANY],
        out_specs=[blk, col, pl.BlockSpec((1, 1, lp), lambda h, i: (h, 0, 0)),
                   pl.BlockSpec((lp, LANES), lambda h, i: (0, 6 * nh + h))],
        out_shape=[jax.ShapeDtypeStruct((lp, w), F32)] * 2 + [jax.ShapeDtypeStruct((nh, 1, lp), F32),
                                                             jax.ShapeDtypeStruct(dproj.shape, BF16)],
        scratch_shapes=[pltpu.VMEM((lp, LANES), F32)], input_output_aliases={6: 3},
        name="fox_bwd", compiler_params=_cp("parallel", "arbitrary"))(qkn, qkn, proj, gates, gtf, do, dproj)


def _merge_fox(o_fox, proj, merged, nh):
    lp = o_fox.shape[0]

    def body(o_ref, z_ref, _, m_ref):
        z = z_ref[...]
        m_ref[...] = (o_ref[...] * (z * _sigmoid(z))).astype(BF16)

    return pl.pallas_call(
        body, grid=(nh,),
        in_specs=[pl.BlockSpec((lp, LANES), lambda s: (0, s)), pl.BlockSpec((lp, LANES), lambda s: (0, 7 * nh + s)), _ANY],
        out_specs=pl.BlockSpec((lp, LANES), lambda s: (0, nh + s)),
        out_shape=jax.ShapeDtypeStruct(merged.shape, BF16), input_output_aliases={2: 0}, name="merge_fox",
        compiler_params=_cp("parallel"))(o_fox, proj, merged)


def _merge_fox_bwd(o_fox, proj, dmerged, dproj, nh):
    lp = o_fox.shape[0]

    def body(o_ref, z_ref, dm_ref, _, do_ref, dz_ref):
        silu, dsilu = _silu_and_grad(z_ref[...])
        dm = dm_ref[...]
        do_ref[...] = dm * silu
        dz_ref[...] = (dm * o_ref[...] * dsilu).astype(BF16)

    w = nh * HEAD_DIM
    return pl.pallas_call(
        body, grid=(nh,),
        in_specs=[pl.BlockSpec((lp, LANES), lambda s: (0, s)), pl.BlockSpec((lp, LANES), lambda s: (0, 7 * nh + s)),
                  pl.BlockSpec((lp, LANES), lambda s: (0, nh + s)), _ANY],
        out_specs=[pl.BlockSpec((lp, LANES), lambda s: (0, s)), pl.BlockSpec((lp, LANES), lambda s: (0, 7 * nh + s))],
        out_shape=[jax.ShapeDtypeStruct((lp, w), F32), jax.ShapeDtypeStruct(dproj.shape, BF16)],
        input_output_aliases={3: 1}, name="merge_fox_bwd", compiler_params=_cp("parallel"))(o_fox, proj, dmerged, dproj)


def _post(out, x, target, post_w):
    lp, d = out.shape

    def body(o_ref, x_ref, t_ref, w_ref, dy_ref, do_ref, loss_ref, dw_ref):
        i = pl.program_id(0)

        @pl.when(i == 0)
        def _():
            loss_ref[...] = jnp.zeros_like(loss_ref)
            dw_ref[...] = jnp.zeros_like(dw_ref)
        o = o_ref[...]
        r = _rms(o)
        nrm = o * r
        err = jnp.where(i > 0, x_ref[...] + nrm * w_ref[...] - t_ref[...], 0.0)
        loss_ref[0:1, :] += 0.5 * jnp.sum(jnp.sum(err * err, axis=1, keepdims=True), axis=0, keepdims=True) / d
        dy = err / d
        dy_ref[...] = dy
        dw_ref[...] += jnp.sum(dy * nrm, axis=0, keepdims=True)
        dyw = dy * w_ref[...]
        do_ref[...] = (r * (dyw - nrm * jnp.mean(dyw * nrm, axis=-1, keepdims=True))).astype(BF16)

    row = pl.BlockSpec((Q_BLOCK, d), lambda i: (i, 0))
    vec = pl.BlockSpec((1, d), lambda i: (0, 0))
    return pl.pallas_call(
        body, grid=(lp // Q_BLOCK,), in_specs=[row, _x_rows(d), _x_rows(d), vec],
        out_specs=[_x_rows(d), row, pl.BlockSpec((8, LANES), lambda i: (0, 0)), vec],
        out_shape=[jax.ShapeDtypeStruct(x.shape, F32), jax.ShapeDtypeStruct((lp, d), BF16),
                   jax.ShapeDtypeStruct((8, LANES), F32), jax.ShapeDtypeStruct((1, d), F32)],
        name="post", compiler_params=_cp("arbitrary"))(out, x, target, post_w)


def _prenorm_bwd(dxn, x, meta, w, dy):
    seq, d = x.shape
    lp = seq + Q_BLOCK

    def body(dx_ref, x_ref, m_ref, w_ref, dy_ref, gx_ref, gm_ref, dw_ref):
        i = pl.program_id(0)
        h = _h_tile(i, x_ref, m_ref)
        r = _rms(h)
        xh = h * r
        dxn_ = dx_ref[...]
        dxw = dxn_ * w_ref[...]
        dh = jnp.where(i > 0, dy_ref[...], 0.0) + r * (dxw - xh * jnp.mean(dxw * xh, axis=-1, keepdims=True))
        gx_ref[...] = dh

        @pl.when(i == 0)
        def _():
            dw_ref[...] = jnp.zeros_like(dw_ref)
            gm_ref[...] = dh[PAD_ROWS:, :]
        dw_ref[...] += jnp.sum(dxn_ * xh, axis=0, keepdims=True)

    vec = pl.BlockSpec((1, d), lambda i: (0, 0))
    met = pl.BlockSpec((N_META, d), lambda i: (0, 0))
    return pl.pallas_call(
        body, grid=(lp // Q_BLOCK,),
        in_specs=[pl.BlockSpec((Q_BLOCK, d), lambda i: (i, 0)), _x_rows(d), met, vec, _x_rows(d)],
        out_specs=[_x_rows(d), met, vec],
        out_shape=[jax.ShapeDtypeStruct((seq, d), F32), jax.ShapeDtypeStruct((N_META, d), F32),
                   jax.ShapeDtypeStruct((1, d), F32)],
        name="prenorm_bwd", compiler_params=_cp("arbitrary"))(dxn, x, meta, w, dy)


def _layer_grads(x, target, meta, pre_w, wfull, conv_w, a_log, dt_bias, gdn_norm_w, fq_w, fk_w, f_bias, w_out, post_w):
    nh = a_log.shape[1]
    conv_wt = conv_w.T
    zpad = jnp.zeros((1, LANES - 3 * nh), F32)
    bias_row = jnp.concatenate([jnp.zeros((1, nh), F32), dt_bias, f_bias, zpad], axis=1)
    nega_row = jnp.concatenate([jnp.zeros((1, nh), F32), -jnp.exp(a_log), jnp.zeros((1, nh), F32), zpad], axis=1)
    qk_w = jnp.stack([fq_w, fk_w])

    xn = _prenorm(x, meta, pre_w)
    proj = _matmul(xn, wfull, "nn", MM_TILE, F32, "proj")
    qkv = _gdn_prep(proj, conv_wt, nh)
    gates, gt3, gtf = _gates(proj, bias_row, nega_row, nh)
    o_gdn, s_all = _gdn_fwd(qkv, gates, gt3, nh)
    qkn = _fox_prep(proj, qk_w, nh)
    o_fox = _fox_fwd(qkn, proj, gates, gtf, nh)
    merged = _merge_fox(o_fox, proj, _merge_gdn(o_gdn, proj, gdn_norm_w, nh), nh)
    out = _matmul(merged, w_out, "nn", 4 * LANES, F32, "out_proj")
    dy, dout, loss_blk, dpost_w = _post(out, x, target, post_w)

    dw_out = _matmul(merged, dout, "tn", 4 * LANES, BF16, "dw_out")
    dmerged = _matmul(dout, w_out, "nt", 4 * LANES, F32, "dmerged")
    do_gdn, dproj, dgdn_norm_w = _merge_gdn_bwd(o_gdn, proj, gdn_norm_w, dmerged, nh)
    do_fox, dproj = _merge_fox_bwd(o_fox, proj, dmerged, dproj, nh)
    dqn, dkn, dc_t, dproj = _fox_bwd(qkn, proj, gates, gtf, do_fox, dproj, nh)
    dproj, dqk_w = _fox_prep_bwd(proj, qk_w, dqn, dkn, dproj, nh)
    dgq, dgk, dgv, dgate = _gdn_bwd(qkv, gates, gt3, s_all, do_gdn, nh)
    dproj, dconv_wt = _gdn_prep_bwd(proj, conv_wt, dgq, dgk, dgv, dproj, nh)
    dc_rows = jnp.pad(dc_t.reshape(nh, -1), ((2 * nh, LANES - 3 * nh), (0, 0)))
    dproj, gate_sums = _gates_bwd(proj, bias_row, nega_row, gates, dgate, dc_rows, dproj, nh)
    dwfull = _matmul(xn, dproj, "tn", MM_TILE, BF16, "dw_in")
    return dict(
        loss=loss_blk[0:1, 0:1], dy=dy, dproj=dproj, wfull=dwfull, post_w=dpost_w,
        conv_w=dconv_wt.T, a_log=gate_sums[1:2, nh:2 * nh], dt_bias=gate_sums[0:1, nh:2 * nh],
        gdn_norm_w=dgdn_norm_w, fq_w=dqk_w[0], fk_w=dqk_w[1], f_bias=gate_sums[0:1, 2 * nh:3 * nh], w_out=dw_out)


def _input_grads(g, wfull, x, meta, pre_w, sums):
    dxn, exchanged = _dxn_and_chip_exchange(g["dproj"], wfull, sums, MM_TILE, "dxn_chip_exchange")
    grad_x, dmeta, dpre_w = _prenorm_bwd(dxn, x, meta, pre_w, g["dy"])
    return grad_x, dmeta, dpre_w, exchanged


def _cast_bf16(a, tr, name):
    r, c = a.shape

    def body(a_ref, o_ref):
        o_ref[...] = a_ref[...].astype(BF16)

    return pl.pallas_call(
        body, grid=(r // tr,), in_specs=[pl.BlockSpec((tr, c), lambda i: (i, 0))],
        out_specs=pl.BlockSpec((tr, c), lambda i: (i, 0)), out_shape=jax.ShapeDtypeStruct((r, c), BF16),
        name=name, compiler_params=_cp("parallel"))(a)


def _gather_copies(ins, outs, send_sems, recv_sems, local_sems):
    n = len(ins)
    x, y, c = lax.axis_index("x"), lax.axis_index("y"), lax.axis_index("c")
    sibling = (x, y, 1 - c)
    chips = [(1 - x, y), (x, 1 - y), (1 - x, 1 - y)]

    def row(a, px, py, pc):
        return outs[a].at[4 * px + 2 * py + pc]

    def copy(a, k, block, to, src=None):
        return pltpu.make_async_remote_copy(
            src_ref=row(a, *block) if src is None else src, dst_ref=row(a, *block), send_sem=send_sems.at[a, k],
            recv_sem=recv_sems.at[a, k], device_id=to, device_id_type=_MESH)

    local, sends = [], []
    for a in range(n):
        local.append(pltpu.make_async_copy(ins[a], row(a, x, y, c), local_sems.at[a]))
        sends.append(copy(a, 0, (x, y, c), sibling, src=ins[a]))
        sends += [copy(a, 1 + j, (x, y, c), (*chip, c), src=ins[a]) for j, chip in enumerate(chips)]

    def start():
        for cp in local + sends:
            cp.start()

    def finish():
        for j, chip in enumerate(chips):
            for a in range(n):
                copy(a, 1 + j, (*chip, c), (x, y, c)).wait_recv()
                fwd = copy(a, 4 + j, (*chip, c), sibling)
                fwd.start()
                sends.append(fwd)
        for a in range(n):
            copy(a, 0, sibling, (x, y, c)).wait_recv()
            for j, chip in enumerate(chips):
                copy(a, 4 + j, (*chip, 1 - c), (x, y, c)).wait_recv()
        for cp in sends:
            cp.wait_send()
        for cp in local:
            cp.wait()

    return start, finish


def _gather_scratch(n):
    return [pltpu.SemaphoreType.DMA((n, N_DEV - 1)), pltpu.SemaphoreType.DMA((n, N_DEV - 1)), pltpu.SemaphoreType.DMA((n,))]


def _all_gather(arrays, name):
    n = len(arrays)

    def body(*refs):
        start, finish = _gather_copies(refs[:n], refs[n:2 * n], *refs[2 * n:])
        start()
        finish()

    return pl.pallas_call(
        body, in_specs=[_ANY] * n, out_specs=[_ANY] * n,
        out_shape=[jax.ShapeDtypeStruct((N_DEV,) + a.shape, a.dtype) for a in arrays],
        scratch_shapes=_gather_scratch(n), name=name)(*arrays)


def _pair_exchange(arrays, small, name):
    n, ns = len(arrays), len(small)

    def body(*refs):
        ins, sm_in = refs[:n], refs[n:n + ns]
        outs, sm_out = refs[n + ns:2 * n + ns], refs[2 * n + ns:2 * (n + ns)]
        send_sems, recv_sems = refs[2 * (n + ns):2 * (n + ns) + 2]
        x, y, c = lax.axis_index("x"), lax.axis_index("y"), lax.axis_index("c")
        copies = []
        for a in range(n):
            for q in range(N_CHIP):
                copies.append(pltpu.make_async_remote_copy(
                    src_ref=ins[a].at[2 * q + 1 - c], dst_ref=outs[a].at[q], send_sem=send_sems.at[a, q],
                    recv_sem=recv_sems.at[a, q], device_id=(x, y, 1 - c), device_id_type=_MESH))
        start, finish = _gather_copies(sm_in, sm_out, *refs[2 * (n + ns) + 2:])
        for cp in copies:
            cp.start()
        start()
        finish()
        for cp in copies:
            cp.wait()

    return pl.pallas_call(
        body, in_specs=[_ANY] * (n + ns), out_specs=[_ANY] * (n + ns),
        out_shape=([jax.ShapeDtypeStruct((N_CHIP,) + a.shape[1:], a.dtype) for a in arrays]
                   + [jax.ShapeDtypeStruct((N_DEV,) + a.shape, a.dtype) for a in small]),
        scratch_shapes=[pltpu.SemaphoreType.DMA((n, N_CHIP)), pltpu.SemaphoreType.DMA((n, N_CHIP))] + _gather_scratch(ns),
        name=name)(*arrays, *small)


def _pair_sum(parts, got, core, tr, name):
    _, r, c = parts.shape

    def body(core_ref, p_ref, g_ref, o_ref):
        o_ref[...] = (p_ref[...].astype(F32) + g_ref[...].astype(F32)).astype(o_ref.dtype)

    return pl.pallas_call(
        body,
        grid_spec=pltpu.PrefetchScalarGridSpec(
            num_scalar_prefetch=1, grid=(N_CHIP, r // tr),
            in_specs=[pl.BlockSpec((1, tr, c), lambda q, i, core_ref: (2 * q + core_ref[0], i, 0)),
                      pl.BlockSpec((1, tr, c), lambda q, i, core_ref: (q, i, 0))],
            out_specs=pl.BlockSpec((1, tr, c), lambda q, i, core_ref: (q, i, 0))),
        out_shape=jax.ShapeDtypeStruct((N_CHIP, r, c), parts.dtype), name=name,
        compiler_params=_cp("parallel", "parallel"))(core, parts, got)


def _native_segments(nh):
    w = nh * HEAD_DIM
    return [(0, 4 * w, 0), (4 * w, 4 * w + 2 * nh, 8 * w), (4 * w + 2 * nh, 8 * w + 2 * nh, 4 * w),
            (8 * w + 2 * nh, 8 * w + 3 * nh, 8 * w + 2 * nh)]


def _relayout_w_in(wg, nh, tr):
    _, d, cols = wg.shape
    w = nh * HEAD_DIM

    def native(ref, j0, j1):
        out = []
        while j0 < j1:
            blk = j0 // cols
            end = min(j1, (blk + 1) * cols)
            out.append(ref[blk, :, pl.ds(j0 - blk * cols, end - j0)])
            j0 = end
        return out

    def body(g_ref, o_ref):
        for cidx in range(8 * w // LANES):
            j0 = cidx * LANES + (0 if cidx * LANES < 4 * w else 2 * nh)
            pieces = native(g_ref, j0, j0 + LANES)
            o_ref[:, cidx * LANES:(cidx + 1) * LANES] = pieces[0] if len(pieces) == 1 else jnp.concatenate(pieces, axis=1)
        pieces = (native(g_ref, 4 * w, 4 * w + 2 * nh) + native(g_ref, 8 * w + 2 * nh, 8 * w + 3 * nh)
                  + [jnp.zeros((tr, LANES - 3 * nh), wg.dtype)])
        o_ref[:, 8 * w:] = jnp.concatenate(pieces, axis=1)

    return pl.pallas_call(
        body, grid=(d // tr,), in_specs=[pl.BlockSpec((N_DEV, tr, cols), lambda i: (0, i, 0))],
        out_specs=pl.BlockSpec((tr, 8 * w + LANES), lambda i: (i, 0)),
        out_shape=jax.ShapeDtypeStruct((d, 8 * w + LANES), wg.dtype),
        name="relayout_w_in", compiler_params=_cp("parallel"))(wg)


def _relayout_dw_in(dwfull, nh, tr):
    d = dwfull.shape[0]
    w = nh * HEAD_DIM
    cols = (8 * w + 3 * nh) // N_DEV
    segs = _native_segments(nh)

    def body(f_ref, o_ref):
        for blk in range(N_DEV):
            pieces = []
            for s0, s1, t0 in segs:
                lo, hi = max(s0, blk * cols), min(s1, (blk + 1) * cols)
                if lo < hi:
                    pieces.append(f_ref[:, pl.ds(t0 + lo - s0, hi - lo)])
            o_ref[blk] = pieces[0] if len(pieces) == 1 else jnp.concatenate(pieces, axis=1)

    return pl.pallas_call(
        body, grid=(d // tr,), in_specs=[pl.BlockSpec((tr, 8 * w + LANES), lambda i: (i, 0))],
        out_specs=pl.BlockSpec((N_DEV, tr, cols), lambda i: (0, i, 0)),
        out_shape=jax.ShapeDtypeStruct((N_DEV, d, cols), dwfull.dtype),
        name="relayout_dw_in", compiler_params=_cp("parallel"))(dwfull)


def _adamw(w, parts, m, v, tr, name):
    r, c = w.shape
    n_parts = parts.shape[0]

    def body(w_ref, p_ref, m_ref, v_ref, g_ref, d_ref, nm_ref, nv_ref):
        g = p_ref[0].astype(F32)
        for s in range(1, n_parts):
            g = g + p_ref[s].astype(F32)
        m_new = ADAM_B1 * m_ref[...] + (1.0 - ADAM_B1) * g
        v_new = ADAM_B2 * v_ref[...] + (1.0 - ADAM_B2) * (g * g)
        m_hat = m_new / (1.0 - ADAM_B1 ** ADAM_STEP)
        v_hat = v_new / (1.0 - ADAM_B2 ** ADAM_STEP)
        g_ref[...] = g
        d_ref[...] = -ADAM_LR * (m_hat / (jnp.sqrt(v_hat) + ADAM_EPS) + ADAM_WD * w_ref[...])
        nm_ref[...] = m_new
        nv_ref[...] = v_new

    blk = pl.BlockSpec((tr, c), lambda i: (i, 0))
    return pl.pallas_call(
        body, grid=(r // tr,), in_specs=[blk, pl.BlockSpec((n_parts, tr, c), lambda i: (0, i, 0)), blk, blk],
        out_specs=[blk] * 4, out_shape=[jax.ShapeDtypeStruct((r, c), F32)] * 4, name=name,
        compiler_params=_cp("parallel"))(w, parts, m, v)


def _pack_small(d, pre, post, a_log, dt_bias, f_bias, gdn_w, fq_w, fk_w, extra):
    row2 = jnp.concatenate([a_log, dt_bias, f_bias, gdn_w, fq_w, fk_w, extra], axis=1)
    row2 = jnp.pad(row2, ((0, 0), (0, d - row2.shape[1])))
    return jnp.concatenate([pre, post, row2, jnp.zeros((5, d), F32)], axis=0)


def _unpack_small(p, nh):
    o = 3 * nh
    return dict(pre=p[0:1], post=p[1:2], a_log=p[2:3, 0:nh], dt_bias=p[2:3, nh:2 * nh], f_bias=p[2:3, 2 * nh:o],
                gdn_w=p[2:3, o:o + HEAD_DIM], fq_w=p[2:3, o + HEAD_DIM:o + 2 * HEAD_DIM],
                fk_w=p[2:3, o + 2 * HEAD_DIM:o + 3 * HEAD_DIM], extra=p[2, o + 3 * HEAD_DIM])


def kernel(x, meta_tokens, pre_norm_w, w_in, conv_w, a_log, dt_bias, gdn_norm_w, fox_q_norm_w, fox_k_norm_w, fox_f_bias, w_out, post_norm_w, loss_target, m_meta_tokens, m_pre_norm_w, m_w_in, m_conv_w, m_a_log, m_dt_bias, m_gdn_norm_w, m_fox_q_norm_w, m_fox_k_norm_w, m_fox_f_bias, m_w_out, m_post_norm_w, v_meta_tokens, v_pre_norm_w, v_w_in, v_conv_w, v_a_log, v_dt_bias, v_gdn_norm_w, v_fox_q_norm_w, v_fox_k_norm_w, v_fox_f_bias, v_w_out, v_post_norm_w):
    nh = a_log.shape[1]
    d = x.shape[-1]
    w = nh * HEAD_DIM
    zero = jnp.zeros((1, 1), F32)

    wg, wog, cg, mg = _all_gather(
        [_cast_bf16(w_in[0], 256, "cast_w_in"), _cast_bf16(w_out[0], 256, "cast_w_out"), conv_w[0], meta_tokens],
        "gather_weights")
    wfull = _relayout_w_in(wg, nh, 256)
    meta_full = mg.transpose(1, 0, 2).reshape(N_META, d)
    g = _layer_grads(x[0], loss_target[0], meta_full, pre_norm_w, wfull, cg.reshape(3 * w, CONV_WIDTH), a_log, dt_bias,
                     gdn_norm_w, fox_q_norm_w, fox_k_norm_w, fox_f_bias, wog.reshape(2 * w, d), post_norm_w)

    core = lax.axis_index("c")
    dev = 4 * lax.axis_index("x") + 2 * lax.axis_index("y") + core
    core_arr = jnp.reshape(core, (1,)).astype(jnp.int32)
    mine_in = _relayout_dw_in(g["wfull"], nh, 128)
    mine_out = g["w_out"].reshape(N_DEV, 2 * w // N_DEV, d)
    got_in, got_out, a_conv = _pair_exchange([mine_in, mine_out], [g["conv_w"]], "pair_exchange")
    sums = [_pair_sum(mine_in, got_in, core_arr, 128, "pair_sum_w_in"),
            _pair_sum(mine_out, got_out, core_arr, 256, "pair_sum_w_out")]
    grad_x, dmeta, dpre_w, (p_in, p_out) = _input_grads(g, wfull, x[0], meta_full, pre_norm_w, sums)
    small = _pack_small(d, dpre_w, g["post_w"], g["a_log"], g["dt_bias"], g["f_bias"], g["gdn_norm_w"], g["fq_w"],
                        g["fk_w"], g["loss"])
    a_meta, p_small = _all_gather([dmeta, small], "gather_small_grads")
    p_conv = lax.dynamic_slice_in_dim(a_conv, dev * conv_w.shape[1], conv_w.shape[1], axis=1)
    p_meta = lax.dynamic_slice_in_dim(a_meta, dev * meta_tokens.shape[1], meta_tokens.shape[1], axis=2)

    r_in = _adamw(w_in[0], p_in, m_w_in[0], v_w_in[0], 128, "adamw_w_in")
    r_out = _adamw(w_out[0], p_out, m_w_out[0], v_w_out[0], 64, "adamw_w_out")
    r_conv = _adamw(conv_w[0], p_conv, m_conv_w[0], v_conv_w[0], conv_w.shape[1], "adamw_conv_w")
    r_meta = _adamw(meta_tokens, p_meta, m_meta_tokens, v_meta_tokens, N_META, "adamw_meta")
    pk = lambda pre, post, a, dt, gw, fq, fk, fb: _pack_small(d, pre, post, a, dt, fb, gw, fq, fk, zero)
    r_small = _adamw(
        pk(pre_norm_w, post_norm_w, a_log, dt_bias, gdn_norm_w, fox_q_norm_w, fox_k_norm_w, fox_f_bias), p_small,
        pk(m_pre_norm_w, m_post_norm_w, m_a_log, m_dt_bias, m_gdn_norm_w, m_fox_q_norm_w, m_fox_k_norm_w, m_fox_f_bias),
        pk(v_pre_norm_w, v_post_norm_w, v_a_log, v_dt_bias, v_gdn_norm_w, v_fox_q_norm_w, v_fox_k_norm_w, v_fox_f_bias),
        8, "adamw_small")

    sm = [_unpack_small(r, nh) for r in r_small]
    outs = []
    for i in range(4):
        s = sm[i]
        outs += [r_meta[i], s["pre"], r_in[i][None], r_conv[i][None], s["a_log"], s["dt_bias"], s["gdn_w"], s["fq_w"],
                 s["fk_w"], s["f_bias"], r_out[i][None], s["post"]]
    return (sm[0]["extra"], grad_x[None], *outs)
```

```python
import jax
import jax.numpy as jnp
from jax import lax
from jax.experimental import pallas as pl
from jax.experimental.pallas import tpu as pltpu

F32, BF16 = jnp.float32, jnp.bfloat16
HEAD_DIM = 128
N_META = 16
CONV_WIDTH = 4
CHUNK = 64
Q_BLOCK = 128
LANES = 128
EPS = 1e-6
PAD_ROWS = Q_BLOCK - N_META
N_DEV = 8
N_CHIP = 4
VMEM_LIMIT = 56 * 1024 * 1024
NEG = -1e30
NARROW = 2 * LANES
MM_TILE = 6 * LANES

ADAM_LR, ADAM_B1, ADAM_B2, ADAM_EPS, ADAM_WD, ADAM_STEP = 0.001, 0.9, 0.999, 1e-08, 0.01, 10

_DN = {"nn": (((1,), (0,)), ((), ())), "nt": (((1,), (1,)), ((), ())), "tn": (((0,), (0,)), ((), ()))}
_DN3 = {"nn": (((2,), (1,)), ((0,), (0,))), "nt": (((2,), (2,)), ((0,), (0,))), "tn": (((1,), (1,)), ((0,), (0,)))}
_ANY = pl.BlockSpec(memory_space=pl.ANY)
_MESH = pl.DeviceIdType.MESH


def _cp(*sem):
    return pltpu.CompilerParams(dimension_semantics=sem, vmem_limit_bytes=VMEM_LIMIT)


def _dot(a, b, dims="nn", prec=None):
    return lax.dot_general(a, b, _DN[dims], precision=prec, preferred_element_type=F32)


def _bdot(a, b, dims="nn"):
    return _dot(a.astype(BF16), b.astype(BF16), dims)


def _hdot(a, b, dims="nn"):
    return _dot(a, b, dims, prec=lax.Precision.HIGHEST)


def _dot3(a, b, dims="nn"):
    return lax.dot_general(a, b, _DN3[dims], preferred_element_type=F32)


def _bdot3(a, b, dims="nn"):
    return _dot3(a.astype(BF16), b.astype(BF16), dims)


def _split(a):
    hi = a.astype(BF16)
    return hi, (a - hi.astype(F32)).astype(BF16)


def _iota(shape, dim):
    return lax.broadcasted_iota(jnp.int32, shape, dim)


def _sigmoid(z):
    return 1.0 / (1.0 + jnp.exp(-z))


def _softplus(z):
    e = jnp.exp(-jnp.abs(z))
    u = 1.0 + e
    l1p = jnp.where(u == 1.0, e, jnp.log(u) * (e / jnp.where(u == 1.0, 1.0, u - 1.0)))
    return jnp.maximum(z, 0.0) + l1p


def _silu_and_grad(z):
    s = _sigmoid(z)
    return z * s, s * (1.0 + z * (1.0 - s))


def _rms(x):
    return lax.rsqrt(jnp.mean(x * x, axis=-1, keepdims=True) + EPS)


def _h_tile(i, x_ref, meta_ref):
    first = jnp.concatenate([jnp.zeros((PAD_ROWS, x_ref.shape[1]), F32), meta_ref[...]], axis=0)
    return jnp.where(i == 0, first, x_ref[...])


def _x_rows(d):
    return pl.BlockSpec((Q_BLOCK, d), lambda i: (jnp.maximum(i - 1, 0), 0))


def _prenorm(x, meta, w):
    seq, d = x.shape
    lp = seq + Q_BLOCK

    def body(x_ref, m_ref, w_ref, o_ref):
        h = _h_tile(pl.program_id(0), x_ref, m_ref)
        o_ref[...] = (h * _rms(h) * w_ref[...]).astype(BF16)

    return pl.pallas_call(
        body, grid=(lp // Q_BLOCK,),
        in_specs=[_x_rows(d), pl.BlockSpec((N_META, d), lambda i: (0, 0)), pl.BlockSpec((1, d), lambda i: (0, 0))],
        out_specs=pl.BlockSpec((Q_BLOCK, d), lambda i: (i, 0)),
        out_shape=jax.ShapeDtypeStruct((lp, d), BF16), name="prenorm", compiler_params=_cp("parallel"))(x, meta, w)


def _tile(n, want):
    return max(t for t in range(LANES, want + 1, LANES) if n % t == 0)


def _matmul(a, b, dims, tn, out_dtype, name):
    m = a.shape[1] if dims == "tn" else a.shape[0]
    n = b.shape[0] if dims == "nt" else b.shape[1]
    kdim = b.shape[1] if dims == "nt" else b.shape[0]
    tn = _tile(n, tn)
    b_spec = pl.BlockSpec((tn, kdim), lambda j: (j, 0)) if dims == "nt" else pl.BlockSpec((kdim, tn), lambda j: (0, j))

    def body(a_ref, b_ref, o_ref):
        o_ref[...] = _dot(a_ref[...], b_ref[...], dims).astype(out_dtype)

    return pl.pallas_call(
        body, grid=(n // tn,),
        in_specs=[pl.BlockSpec(a.shape, lambda j: (0, 0)), b_spec],
        out_specs=pl.BlockSpec((m, tn), lambda j: (0, j)),
        out_shape=jax.ShapeDtypeStruct((m, n), out_dtype), name=name, compiler_params=_cp("parallel"))(a, b)


def _chip_copies(ins, outs, send_sems, recv_sems, local_sems):
    if not ins:
        return [], [], []
    x, y, c = lax.axis_index("x"), lax.axis_index("y"), lax.axis_index("c")
    mine = 2 * x + y
    local, sends, recvs = [], [], []
    for a in range(len(ins)):
        local.append(pltpu.make_async_copy(ins[a].at[mine], outs[a].at[mine], local_sems.at[a]))
        for k in range(1, N_CHIP):
            px = 1 - x if k & 2 else x
            py = 1 - y if k & 1 else y
            kw = dict(send_sem=send_sems.at[a, k - 1], recv_sem=recv_sems.at[a, k - 1], device_id=(px, py, c),
                      device_id_type=_MESH)
            sends.append(pltpu.make_async_remote_copy(src_ref=ins[a].at[2 * px + py], dst_ref=outs[a].at[mine], **kw))
            recvs.append(pltpu.make_async_remote_copy(src_ref=ins[a].at[mine], dst_ref=outs[a].at[2 * px + py], **kw))
    return local, sends, recvs


def _dxn_and_chip_exchange(dproj, wfull, sums, tk, name):
    m, k = dproj.shape
    n = wfull.shape[0]
    na = len(sums)
    tk = _tile(k, tk)
    steps = k // tk

    def body(a_ref, b_ref, *rest):
        ins, o_ref, outs = rest[:na], rest[na], rest[na + 1:2 * na + 1]
        local, sends, recvs = _chip_copies(ins, outs, *rest[2 * na + 1:])
        j = pl.program_id(0)

        @pl.when(j == 0)
        def _():
            for cp in local + sends:
                cp.start()
            o_ref[...] = jnp.zeros_like(o_ref)
        o_ref[...] += _dot(a_ref[...], b_ref[...], "nt")

        @pl.when(j == steps - 1)
        def _():
            for cp in local:
                cp.wait()
            for cp in sends:
                cp.wait_send()
            for cp in recvs:
                cp.wait_recv()

    res = pl.pallas_call(
        body, grid=(steps,),
        in_specs=[pl.BlockSpec((m, tk), lambda j: (0, j)), pl.BlockSpec((n, tk), lambda j: (0, j))] + [_ANY] * na,
        out_specs=[pl.BlockSpec((m, n), lambda j: (0, 0))] + [_ANY] * na,
        out_shape=[jax.ShapeDtypeStruct((m, n), F32)] + [jax.ShapeDtypeStruct(s.shape, s.dtype) for s in sums],
        scratch_shapes=[pltpu.SemaphoreType.DMA((max(na, 1), N_CHIP - 1)), pltpu.SemaphoreType.DMA((max(na, 1), N_CHIP - 1)),
                        pltpu.SemaphoreType.DMA((max(na, 1),))],
        name=name, compiler_params=_cp("arbitrary"))(dproj, wfull, *sums)
    return res[0], res[1:]


def _conv_taps(x, w):
    c = x * w[CONV_WIDTH - 1:CONV_WIDTH, :]
    for j in range(CONV_WIDTH - 1):
        c = c + pltpu.roll(x, CONV_WIDTH - 1 - j, 0) * w[j:j + 1, :]
    return c


def _gdn_prep(proj, conv_wt, nh):
    lp = proj.shape[0]
    scale = HEAD_DIM ** -0.5

    def body(x_ref, w_ref, o_ref):
        which = pl.program_id(0) // nh
        c = _conv_taps(x_ref[...], w_ref[...])
        s = c * _sigmoid(c)
        r = lax.rsqrt(jnp.sum(s * s, axis=-1, keepdims=True) + EPS)
        f = jnp.where(which == 0, r * scale, jnp.where(which == 1, r, 1.0))
        o_ref[...] = jnp.where(_iota(s.shape, 0) >= PAD_ROWS, s * f, 0.0)

    return pl.pallas_call(
        body, grid=(3 * nh,),
        in_specs=[pl.BlockSpec((lp, LANES), lambda s: (0, s)), pl.BlockSpec((CONV_WIDTH, LANES), lambda s: (0, s))],
        out_specs=pl.BlockSpec((lp, LANES), lambda s: (0, s)),
        out_shape=jax.ShapeDtypeStruct((lp, 3 * nh * HEAD_DIM), F32), name="gdn_prep",
        compiler_params=_cp("parallel"))(proj, conv_wt)


def _gdn_prep_bwd(proj, conv_wt, dq, dk, dv, dproj, nh):
    lp = proj.shape[0]
    scale = HEAD_DIM ** -0.5
    part = lambda p: pl.BlockSpec((lp, LANES), lambda s: (0, jnp.clip(s - p * nh, 0, nh - 1)))

    def body(x_ref, w_ref, dq_ref, dk_ref, dv_ref, _, dx_ref, dw_ref):
        which = pl.program_id(0) // nh
        x = x_ref[...]
        w = w_ref[...]
        c = _conv_taps(x, w)
        sg = _sigmoid(c)
        s = c * sg
        r = lax.rsqrt(jnp.sum(s * s, axis=-1, keepdims=True) + EPS)
        dy = jnp.where(which == 0, dq_ref[...], jnp.where(which == 1, dk_ref[...], dv_ref[...]))
        dy = jnp.where(_iota(s.shape, 0) >= PAD_ROWS, dy, 0.0)
        y0 = s * r
        dy0 = dy * jnp.where(which == 0, scale, 1.0)
        ds_n = r * (dy0 - y0 * jnp.sum(dy0 * y0, axis=-1, keepdims=True))
        ds = jnp.where(which == 2, dy, ds_n)
        dc = ds * (sg * (1.0 + c * (1.0 - sg)))
        dx = dc * w[CONV_WIDTH - 1:CONV_WIDTH, :]
        rows = [jnp.sum(dc * x, axis=0, keepdims=True)]
        for j in range(CONV_WIDTH - 2, -1, -1):
            sh = CONV_WIDTH - 1 - j
            dx = dx + pltpu.roll(dc, lp - sh, 0) * w[j:j + 1, :]
            rows.insert(0, jnp.sum(dc * pltpu.roll(x, sh, 0), axis=0, keepdims=True))
        dx_ref[...] = dx.astype(BF16)
        dw_ref[...] = jnp.concatenate(rows, axis=0)

    strip = pl.BlockSpec((lp, LANES), lambda s: (0, s))
    taps = pl.BlockSpec((CONV_WIDTH, LANES), lambda s: (0, s))
    return pl.pallas_call(
        body, grid=(3 * nh,), in_specs=[strip, taps, part(0), part(1), part(2), _ANY], out_specs=[strip, taps],
        out_shape=[jax.ShapeDtypeStruct(dproj.shape, BF16), jax.ShapeDtypeStruct((CONV_WIDTH, 3 * nh * HEAD_DIM), F32)],
        input_output_aliases={5: 0}, name="gdn_prep_bwd", compiler_params=_cp("parallel"))(proj, conv_wt, dq, dk, dv, dproj)


def _gates(proj, bias_row, nega_row, nh):
    lp = proj.shape[0]
    nc = lp // CHUNK

    def body(p_ref, b_ref, a_ref, g_ref, gt3_ref, gtf_ref):
        lane = _iota((CHUNK, LANES), 1)
        tri = (_iota((CHUNK, CHUNK), 0) >= _iota((CHUNK, CHUNK), 1)).astype(F32)

        def step(n, carry):
            r0 = pl.multiple_of(n * CHUNK, CHUNK)
            z = p_ref[pl.ds(r0, CHUNK), :] + b_ref[...]
            base = jnp.where(lane < nh, _sigmoid(z),
                             jnp.where(lane < 2 * nh, a_ref[...] * _softplus(z),
                                       jnp.where(lane < 3 * nh, -_softplus(-z), 0.0)))
            base = jnp.where(r0 + _iota((CHUNK, LANES), 0) >= PAD_ROWS, base, 0.0)
            cs = _hdot(tri, base)
            run = jnp.where((lane >= 2 * nh) & (lane < 3 * nh), cs + carry, cs)
            sh = pltpu.roll(run, 2 * nh, 1)
            out = base + jnp.where((lane >= 3 * nh) & (lane < 5 * nh), sh, 0.0)
            g_ref[pl.ds(r0, CHUNK), :] = out
            gt3_ref[n] = out.T
            return carry + cs[CHUNK - 1:CHUNK, :]

        lax.fori_loop(0, nc, step, jnp.zeros((1, LANES), F32))
        gtf_ref[...] = g_ref[...].T

    vec = pl.BlockSpec((1, LANES), lambda i: (0, 0))
    return pl.pallas_call(
        body, grid=(1,), in_specs=[pl.BlockSpec((lp, LANES), lambda i: (0, 8 * nh)), vec, vec],
        out_specs=[pl.BlockSpec((lp, LANES), lambda i: (0, 0)), pl.BlockSpec((nc, LANES, CHUNK), lambda i: (0, 0, 0)),
                   pl.BlockSpec((LANES, lp), lambda i: (0, 0))],
        out_shape=[jax.ShapeDtypeStruct((lp, LANES), F32), jax.ShapeDtypeStruct((nc, LANES, CHUNK), F32),
                   jax.ShapeDtypeStruct((LANES, lp), F32)],
        name="gates", compiler_params=_cp("arbitrary"))(proj, bias_row, nega_row)


def _gates_bwd(proj, bias_row, nega_row, gates, dgate_gdn, dc_t, dproj, nh):
    lp = proj.shape[0]
    nc = lp // CHUNK

    def body(p_ref, b_ref, a_ref, g_ref, dg_ref, dc_ref, _, dz_ref, sm_ref, dct_scr):
        lane = _iota((CHUNK, LANES), 1)
        triu = (_iota((CHUNK, CHUNK), 0) <= _iota((CHUNK, CHUNK), 1)).astype(F32)
        dct_scr[...] = dc_ref[...].T
        sm_ref[...] = jnp.zeros_like(sm_ref)
        dz_ref[:, LANES:] = jnp.zeros((lp, NARROW - LANES), BF16)

        def step(i, carry):
            n = nc - 1 - i
            r0 = pl.multiple_of(n * CHUNK, CHUNK)
            z = p_ref[pl.ds(r0, CHUNK), :] + b_ref[...]
            gt = g_ref[pl.ds(r0, CHUNK), :]
            dgd = dg_ref[pl.ds(r0, CHUNK), :]
            dch = dct_scr[pl.ds(r0, CHUNK), :]
            rc = _hdot(triu, dch) + carry
            sg = _sigmoid(z)
            dz = jnp.where(lane < nh, dgd * sg * (1.0 - sg),
                           jnp.where(lane < 2 * nh, dgd * a_ref[...] * sg,
                                     jnp.where(lane < 3 * nh, rc * (1.0 - sg), 0.0)))
            dz = jnp.where(r0 + _iota((CHUNK, LANES), 0) >= PAD_ROWS, dz, 0.0)
            dz_ref[pl.ds(r0, CHUNK), 0:LANES] = dz.astype(BF16)
            sm_ref[0:1, :] += jnp.sum(dz, axis=0, keepdims=True)
            sm_ref[1:2, :] += jnp.sum(jnp.where((lane >= nh) & (lane < 2 * nh), dgd * gt, 0.0), axis=0, keepdims=True)
            return carry + jnp.sum(dch, axis=0, keepdims=True)

        lax.fori_loop(0, nc, step, jnp.zeros((1, LANES), F32))

    vec = pl.BlockSpec((1, LANES), lambda i: (0, 0))
    full = pl.BlockSpec((lp, LANES), lambda i: (0, 0))
    last = pl.BlockSpec((lp, LANES), lambda i: (0, 8 * nh))
    tail = pl.BlockSpec((lp, NARROW), lambda i: (0, 8 * nh * LANES // NARROW))
    return pl.pallas_call(
        body, grid=(1,), in_specs=[last, vec, vec, full, full, pl.BlockSpec((LANES, lp), lambda i: (0, 0)), _ANY],
        out_specs=[tail, pl.BlockSpec((8, LANES), lambda i: (0, 0))],
        out_shape=[jax.ShapeDtypeStruct(dproj.shape, BF16), jax.ShapeDtypeStruct((8, LANES), F32)],
        scratch_shapes=[pltpu.VMEM((lp, LANES), F32)], input_output_aliases={6: 0},
        name="gates_bwd", compiler_params=_cp("arbitrary"))(proj, bias_row, nega_row, gates, dgate_gdn, dc_t, dproj)


def _tri_inv(a):
    t = jnp.where(_iota(a.shape, 1) == _iota(a.shape, 2), 1.0, 0.0) - a
    p = a
    for _ in range(5):
        ph, pw = _split(p)
        p = _dot3(ph, ph) + (_dot3(ph, pw) + _dot3(pw, ph))
        ph, pw = _split(p)
        th, tw = _split(t)
        t = t + (_dot3(th, ph) + (_dot3(th, pw) + _dot3(tw, ph)))
    return t


def _gdn_chunk(q, k, v, beta, gc, gr):
    ii, jj = _iota((1, CHUNK, CHUNK), 1), _iota((1, CHUNK, CHUNK), 2)
    causal, strict = ii >= jj, ii > jj
    dm = jnp.where(causal, jnp.exp(jnp.where(causal, gc - gr, 0.0)), 0.0)
    kk = _bdot3(k, k, "nt")
    a = jnp.where(strict, beta * kk * dm, 0.0)
    t = _tri_inv(a)
    eg = jnp.exp(gc)
    glast = gc[:, CHUNK - 1:CHUNK, :]
    ekd = jnp.exp(glast - gc)
    bv = beta * v
    bk = (beta * eg) * k
    ub = _bdot3(t, jnp.concatenate([bv, bk], axis=2))
    qk = _bdot3(q, k, "nt")
    return dict(causal=causal, strict=strict, dm=dm, kk=kk, a=a, t=t, eg=eg, ekd=ekd, bv=bv, bk=bk,
                u=ub[:, :, :HEAD_DIM], w=ub[:, :, HEAD_DIM:], qk=qk, aqk=jnp.where(causal, qk * dm, 0.0),
                q_dec=q * eg, k_dec=k * ekd, decay=jnp.exp(glast))


def _heads(ref, nh):
    return jnp.stack([ref[:, h * HEAD_DIM:(h + 1) * HEAD_DIM] for h in range(nh)], axis=0)


def _gdn_chunk_inputs(q_ref, k_ref, v_ref, g, gt, nh):
    col = lambda o: jnp.stack([g[:, o + h:o + h + 1] for h in range(nh)], axis=0)
    gr = jnp.stack([gt[3 * nh + h:3 * nh + h + 1, :] for h in range(nh)], axis=0)
    return _heads(q_ref, nh), _heads(k_ref, nh), _heads(v_ref, nh), col(0), col(3 * nh), gr


def _gdn_fwd(qkv, gates, gt3, nh):
    lp = qkv.shape[0]
    nc = lp // CHUNK
    w = nh * HEAD_DIM

    def body(q_ref, k_ref, v_ref, g_ref, gt_ref, o_ref, sall_ref, s_scr):
        @pl.when(pl.program_id(0) == 0)
        def _():
            s_scr[...] = jnp.zeros_like(s_scr)
        c = _gdn_chunk(*_gdn_chunk_inputs(q_ref, k_ref, v_ref, g_ref[...], gt_ref[0], nh))
        s = s_scr[...]
        sall_ref[0] = s
        v_new = c["u"] - _bdot3(c["w"], s)
        o = _bdot3(c["q_dec"], s) + _bdot3(c["aqk"], v_new)
        s_scr[...] = s * c["decay"] + _bdot3(c["k_dec"], v_new, "tn")
        for h in range(nh):
            o_ref[:, h * HEAD_DIM:(h + 1) * HEAD_DIM] = o[h]

    return pl.pallas_call(
        body, grid=(nc,),
        in_specs=[pl.BlockSpec((CHUNK, w), lambda n: (n, 0)), pl.BlockSpec((CHUNK, w), lambda n: (n, 1)),
                  pl.BlockSpec((CHUNK, w), lambda n: (n, 2)), pl.BlockSpec((CHUNK, LANES), lambda n: (n, 0)),
                  pl.BlockSpec((1, LANES, CHUNK), lambda n: (n, 0, 0))],
        out_specs=[pl.BlockSpec((CHUNK, w), lambda n: (n, 0)),
                   pl.BlockSpec((1, nh, HEAD_DIM, HEAD_DIM), lambda n: (n, 0, 0, 0))],
        out_shape=[jax.ShapeDtypeStruct((lp, w), F32), jax.ShapeDtypeStruct((nc, nh, HEAD_DIM, HEAD_DIM), F32)],
        scratch_shapes=[pltpu.VMEM((nh, HEAD_DIM, HEAD_DIM), F32)],
        name="gdn_fwd", compiler_params=_cp("arbitrary"))(qkv, qkv, qkv, gates, gt3)


def _gdn_bwd(qkv, gates, gt3, s_all, do, nh):
    lp = qkv.shape[0]
    nc = lp // CHUNK
    w = nh * HEAD_DIM
    rev = lambda n: nc - 1 - n

    def body(q_ref, k_ref, v_ref, g_ref, gt_ref, s_ref, do_ref, dq_ref, dk_ref, dv_ref, dg_ref, ds_scr):
        @pl.when(pl.program_id(0) == 0)
        def _():
            ds_scr[...] = jnp.zeros_like(ds_scr)
        q, k, v, beta, gc, gr = _gdn_chunk_inputs(q_ref, k_ref, v_ref, g_ref[...], gt_ref[0], nh)
        c = _gdn_chunk(q, k, v, beta, gc, gr)
        s = s_ref[0]
        dsn = ds_scr[...]
        dout = _heads(do_ref, nh)
        v_new = c["u"] - _bdot3(c["w"], s)
        dq_dec = _bdot3(dout, s, "nt")
        daqk = jnp.where(c["causal"], _bdot3(dout, v_new, "nt"), 0.0)
        dv_new = _bdot3(c["aqk"], dout, "tn") + _bdot3(c["k_dec"], dsn)
        dk_dec = _bdot3(v_new, dsn, "nt")
        ddecay = jnp.sum(jnp.sum(dsn * s, axis=2, keepdims=True), axis=1, keepdims=True)
        dw = -_bdot3(dv_new, s, "nt")
        ds_scr[...] = _bdot3(c["q_dec"], dout, "tn") + c["decay"] * dsn - _bdot3(c["w"], dv_new, "tn")
        duw = jnp.concatenate([dv_new, dw], axis=2)
        dt = _bdot3(duw, jnp.concatenate([c["bv"], c["bk"]], axis=2), "nt")
        dbvk = _bdot3(c["t"], duw, "tn")
        dbv, dbk = dbvk[:, :, :HEAD_DIM], dbvk[:, :, HEAD_DIM:]
        da = jnp.where(c["strict"], -_bdot3(_bdot3(c["t"], dt, "tn"), c["t"], "nt"), 0.0)
        dkk = da * beta * c["dm"]
        dqk = daqk * c["dm"]
        e = da * c["a"] + daqk * c["aqk"]
        dq = dq_dec * c["eg"] + _bdot3(dqk, k)
        dk = (dk_dec * c["ekd"] + _bdot3(dkk, k) + _bdot3(dkk, k, "tn") + _bdot3(dqk, q, "tn")
              + (beta * c["eg"]) * dbk)
        dv = beta * dbv
        rs = lambda x: jnp.sum(x, axis=2, keepdims=True)
        dbeta = rs(dbv * v) + c["eg"] * rs(dbk * k) + rs(da * c["kk"] * c["dm"])
        kd_term = rs(dk_dec * c["k_dec"])
        eh, ew = _split(e)
        ones = jnp.ones((nh, CHUNK, LANES), BF16)
        col_sums = (_dot3(eh, ones, "tn") + _dot3(ew, ones, "tn"))[:, :, 0:1]
        dg_cum = rs(dq_dec * c["q_dec"]) - kd_term + rs(dbk * c["bk"]) + rs(e) - col_sums
        last = jnp.sum(kd_term, axis=1, keepdims=True) + ddecay * c["decay"]
        dg_cum = dg_cum + jnp.where(_iota((1, CHUNK, 1), 1) == CHUNK - 1, last, 0.0)
        lane = _iota((CHUNK, LANES), 1)
        acc = jnp.zeros((CHUNK, LANES), F32)
        for h in range(nh):
            sl = slice(h * HEAD_DIM, (h + 1) * HEAD_DIM)
            dq_ref[:, sl] = dq[h]
            dk_ref[:, sl] = dk[h]
            dv_ref[:, sl] = dv[h]
            acc = acc + jnp.where(lane == h, dbeta[h], 0.0) + jnp.where(lane == nh + h, dg_cum[h], 0.0)
        triu = (_iota((CHUNK, CHUNK), 0) <= _iota((CHUNK, CHUNK), 1)).astype(F32)
        dg_ref[...] = jnp.where(lane < nh, acc, _hdot(triu, acc))

    return pl.pallas_call(
        body, grid=(nc,),
        in_specs=[pl.BlockSpec((CHUNK, w), lambda n: (rev(n), 0)), pl.BlockSpec((CHUNK, w), lambda n: (rev(n), 1)),
                  pl.BlockSpec((CHUNK, w), lambda n: (rev(n), 2)), pl.BlockSpec((CHUNK, LANES), lambda n: (rev(n), 0)),
                  pl.BlockSpec((1, LANES, CHUNK), lambda n: (rev(n), 0, 0)),
                  pl.BlockSpec((1, nh, HEAD_DIM, HEAD_DIM), lambda n: (rev(n), 0, 0, 0)),
                  pl.BlockSpec((CHUNK, w), lambda n: (rev(n), 0))],
        out_specs=[pl.BlockSpec((CHUNK, w), lambda n: (rev(n), 0))] * 3 + [pl.BlockSpec((CHUNK, LANES), lambda n: (rev(n), 0))],
        out_shape=[jax.ShapeDtypeStruct((lp, w), F32)] * 3 + [jax.ShapeDtypeStruct((lp, LANES), F32)],
        scratch_shapes=[pltpu.VMEM((nh, HEAD_DIM, HEAD_DIM), F32)],
        name="gdn_bwd", compiler_params=_cp("arbitrary"))(qkv, qkv, qkv, gates, gt3, s_all, do)


def _merge_gdn(o_gdn, proj, norm_w, nh):
    lp = o_gdn.shape[0]

    def body(o_ref, z_ref, w_ref, m_ref):
        o = o_ref[...]
        z = z_ref[...]
        m_ref[...] = (o * _rms(o) * w_ref[...] * (z * _sigmoid(z))).astype(BF16)

    return pl.pallas_call(
        body, grid=(nh,),
        in_specs=[pl.BlockSpec((lp, LANES), lambda s: (0, s)), pl.BlockSpec((lp, LANES), lambda s: (0, 3 * nh + s)),
                  pl.BlockSpec((1, LANES), lambda s: (0, 0))],
        out_specs=pl.BlockSpec((lp, LANES), lambda s: (0, s)),
        out_shape=jax.ShapeDtypeStruct((lp, 2 * nh * HEAD_DIM), BF16), name="merge_gdn",
        compiler_params=_cp("parallel"))(o_gdn, proj, norm_w)


def _merge_gdn_bwd(o_gdn, proj, norm_w, dmerged, nh):
    lp = o_gdn.shape[0]

    def body(o_ref, z_ref, w_ref, dm_ref, do_ref, dz_ref, dw_ref):
        o = o_ref[...]
        r = _rms(o)
        xh = o * r
        silu, dsilu = _silu_and_grad(z_ref[...])
        dm = dm_ref[...]
        dn = dm * silu
        dz_ref[...] = (dm * (xh * w_ref[...]) * dsilu).astype(BF16)
        dnw = dn * w_ref[...]
        do_ref[...] = r * (dnw - xh * jnp.mean(dnw * xh, axis=-1, keepdims=True))

        @pl.when(pl.program_id(0) == 0)
        def _():
            dw_ref[...] = jnp.zeros_like(dw_ref)
        dw_ref[...] += jnp.sum(dn * xh, axis=0, keepdims=True)

    w = nh * HEAD_DIM
    return pl.pallas_call(
        body, grid=(nh,),
        in_specs=[pl.BlockSpec((lp, LANES), lambda s: (0, s)), pl.BlockSpec((lp, LANES), lambda s: (0, 3 * nh + s)),
                  pl.BlockSpec((1, LANES), lambda s: (0, 0)), pl.BlockSpec((lp, LANES), lambda s: (0, s))],
        out_specs=[pl.BlockSpec((lp, LANES), lambda s: (0, s)), pl.BlockSpec((lp, LANES), lambda s: (0, 3 * nh + s)),
                   pl.BlockSpec((1, LANES), lambda s: (0, 0))],
        out_shape=[jax.ShapeDtypeStruct((lp, w), F32), jax.ShapeDtypeStruct((lp, 8 * w + NARROW), BF16),
                   jax.ShapeDtypeStruct((1, LANES), F32)],
        name="merge_gdn_bwd", compiler_params=_cp("arbitrary"))(o_gdn, proj, norm_w, dmerged)


def _fox_prep(proj, qk_w, nh):
    lp = proj.shape[0]

    def body(x_ref, w_ref, o_ref):
        x = x_ref[...]
        o_ref[...] = x * _rms(x) * w_ref[0]

    return pl.pallas_call(
        body, grid=(2 * nh,),
        in_specs=[pl.BlockSpec((lp, LANES), lambda s: (0, 4 * nh + s)), pl.BlockSpec((1, 1, LANES), lambda s: (s // nh, 0, 0))],
        out_specs=pl.BlockSpec((lp, LANES), lambda s: (0, s)),
        out_shape=jax.ShapeDtypeStruct((lp, 2 * nh * HEAD_DIM), F32), name="fox_prep",
        compiler_params=_cp("parallel"))(proj, qk_w)


def _fox_prep_bwd(proj, qk_w, dq, dk, dproj, nh):
    lp = proj.shape[0]
    part = lambda p: pl.BlockSpec((lp, LANES), lambda s: (0, jnp.clip(s - p * nh, 0, nh - 1)))

    def body(x_ref, w_ref, dq_ref, dk_ref, _, dx_ref, dw_ref):
        x = x_ref[...]
        r = _rms(x)
        xh = x * r
        dy = jnp.where(pl.program_id(0) < nh, dq_ref[...], dk_ref[...])
        dyw = dy * w_ref[0]
        dx_ref[...] = (r * (dyw - xh * jnp.mean(dyw * xh, axis=-1, keepdims=True))).astype(BF16)

        @pl.when(pl.program_id(0) % nh == 0)
        def _():
            dw_ref[...] = jnp.zeros_like(dw_ref)
        dw_ref[0] += jnp.sum(dy * xh, axis=0, keepdims=True)

    strip = pl.BlockSpec((lp, LANES), lambda s: (0, 4 * nh + s))
    wsp = pl.BlockSpec((1, 1, LANES), lambda s: (s // nh, 0, 0))
    return pl.pallas_call(
        body, grid=(2 * nh,), in_specs=[strip, wsp, part(0), part(1), _ANY], out_specs=[strip, wsp],
        out_shape=[jax.ShapeDtypeStruct(dproj.shape, BF16), jax.ShapeDtypeStruct((2, 1, LANES), F32)],
        input_output_aliases={4: 0}, name="fox_prep_bwd", compiler_params=_cp("arbitrary"))(proj, qk_w, dq, dk, dproj)


def _fox_probs(q, k, gates, crow, h, i, nh):
    kl = k.shape[0]
    lane = _iota((Q_BLOCK, LANES), 1)
    ct = jnp.sum(jnp.where(lane == 4 * nh + h, gates, 0.0), axis=1, keepdims=True)
    s = _bdot(q, k, "nt") * (HEAD_DIM ** -0.5) + (ct - crow)
    t = i * Q_BLOCK + _iota((Q_BLOCK, kl), 0)
    kp = _iota((Q_BLOCK, kl), 1)
    s = jnp.where((kp <= t) & ((kp >= PAD_ROWS) | (t < PAD_ROWS)), s, NEG)
    p = jnp.exp(s - jnp.max(s, axis=1, keepdims=True))
    return p / jnp.sum(p, axis=1, keepdims=True)


FOX_HEADS_PER_STEP = 2


def _fox_specs(lp, nh):
    hw = FOX_HEADS_PER_STEP * LANES
    return [pl.BlockSpec((Q_BLOCK, hw), lambda g, i: (i, g)),
            pl.BlockSpec((lp, hw), lambda g, i: (0, nh // FOX_HEADS_PER_STEP + g)),
            pl.BlockSpec((lp, hw), lambda g, i: (0, 6 * nh // FOX_HEADS_PER_STEP + g)),
            pl.BlockSpec((Q_BLOCK, LANES), lambda g, i: (i, 0)),
            pl.BlockSpec((LANES, lp), lambda g, i: (0, 0))]


def _fox_fwd(qkn, proj, gates, gtf, nh):
    lp = qkn.shape[0]

    def body(q_ref, k_ref, v_ref, g_ref, gt_ref, o_ref):
        g, i = pl.program_id(0), pl.program_id(1)
        for j in range(lp // Q_BLOCK):
            @pl.when(i == j)
            def _(j=j):
                kl = (j + 1) * Q_BLOCK
                for hh in range(FOX_HEADS_PER_STEP):
                    h = FOX_HEADS_PER_STEP * g + hh
                    sl = slice(hh * LANES, (hh + 1) * LANES)
                    p = _fox_probs(q_ref[:, sl], k_ref[0:kl, sl], g_ref[...], gt_ref[pl.ds(4 * nh + h, 1), :][:, 0:kl],
                                   h, j, nh)
                    o_ref[:, sl] = _bdot(p, v_ref[0:kl, sl])

    return pl.pallas_call(
        body, grid=(nh // FOX_HEADS_PER_STEP, lp // Q_BLOCK), in_specs=_fox_specs(lp, nh),
        out_specs=pl.BlockSpec((Q_BLOCK, FOX_HEADS_PER_STEP * LANES), lambda g, i: (i, g)),
        out_shape=jax.ShapeDtypeStruct((lp, nh * HEAD_DIM), F32), name="fox_fwd",
        compiler_params=_cp("parallel", "parallel"))(qkn, qkn, proj, gates, gtf)


def _fox_bwd(qkn, proj, gates, gtf, do, dproj, nh):
    lp = qkn.shape[0]
    nq = lp // Q_BLOCK
    w = nh * HEAD_DIM
    scale = HEAD_DIM ** -0.5

    def body(q_ref, k_ref, v_ref, g_ref, gt_ref, do_ref, _, dq_ref, dk_ref, dc_ref, dv_ref, dv_scr):
        g, i = pl.program_id(0), pl.program_id(1)

        @pl.when(i == 0)
        def _():
            dk_ref[...] = jnp.zeros_like(dk_ref)
            dv_scr[...] = jnp.zeros_like(dv_scr)
            dc_ref[...] = jnp.zeros_like(dc_ref)
        for j in range(nq):
            @pl.when(i == j)
            def _(j=j):
                kl = (j + 1) * Q_BLOCK
                for hh in range(FOX_HEADS_PER_STEP):
                    h = FOX_HEADS_PER_STEP * g + hh
                    sl = slice(hh * LANES, (hh + 1) * LANES)
                    q, k = q_ref[:, sl], k_ref[0:kl, sl]
                    p = _fox_probs(q, k, g_ref[...], gt_ref[pl.ds(4 * nh + h, 1), :][:, 0:kl], h, j, nh)
                    dout = do_ref[:, sl]
                    dp = _bdot(dout, v_ref[0:kl, sl], "nt")
                    ds = p * (dp - jnp.sum(p * dp, axis=1, keepdims=True))
                    dq_ref[:, sl] = _bdot(ds, k) * scale
                    dk_ref[0:kl, sl] += _bdot(ds, q, "tn") * scale
                    dv_scr[0:kl, sl] += _bdot(p, dout, "tn")
                    dc_ref[hh, :, 0:kl] -= jnp.sum(ds, axis=0, keepdims=True)

        @pl.when(i == nq - 1)
        def _():
            dv_ref[...] = dv_scr[...].astype(BF16)

    hw = FOX_HEADS_PER_STEP * LANES
    blk = pl.BlockSpec((Q_BLOCK, hw), lambda g, i: (i, g))
    col = pl.BlockSpec((lp, hw), lambda g, i: (0, g))
    return pl.pallas_call(
        body, grid=(nh // FOX_HEADS_PER_STEP, nq), in_specs=_fox_specs(lp, nh) + [blk, _ANY],
        out_specs=[blk, col, pl.BlockSpec((FOX_HEADS_PER_STEP, 1, lp), lambda g, i: (g, 0, 0)),
                   pl.BlockSpec((lp, hw), lambda g, i: (0, 6 * nh // FOX_HEADS_PER_STEP + g))],
        out_shape=[jax.ShapeDtypeStruct((lp, w), F32)] * 2 + [jax.ShapeDtypeStruct((nh, 1, lp), F32),
                                                             jax.ShapeDtypeStruct(dproj.shape, BF16)],
        scratch_shapes=[pltpu.VMEM((lp, hw), F32)], input_output_aliases={6: 3},
        name="fox_bwd", compiler_params=_cp("parallel", "arbitrary"))(qkn, qkn, proj, gates, gtf, do, dproj)


def _merge_fox(o_fox, proj, merged, nh):
    lp = o_fox.shape[0]

    def body(o_ref, z_ref, _, m_ref):
        z = z_ref[...]
        m_ref[...] = (o_ref[...] * (z * _sigmoid(z))).astype(BF16)

    return pl.pallas_call(
        body, grid=(nh,),
        in_specs=[pl.BlockSpec((lp, LANES), lambda s: (0, s)), pl.BlockSpec((lp, LANES), lambda s: (0, 7 * nh + s)), _ANY],
        out_specs=pl.BlockSpec((lp, LANES), lambda s: (0, nh + s)),
        out_shape=jax.ShapeDtypeStruct(merged.shape, BF16), input_output_aliases={2: 0}, name="merge_fox",
        compiler_params=_cp("parallel"))(o_fox, proj, merged)


def _merge_fox_bwd(o_fox, proj, dmerged, dproj, nh):
    lp = o_fox.shape[0]

    def body(o_ref, z_ref, dm_ref, _, do_ref, dz_ref):
        silu, dsilu = _silu_and_grad(z_ref[...])
        dm = dm_ref[...]
        do_ref[...] = dm * silu
        dz_ref[...] = (dm * o_ref[...] * dsilu).astype(BF16)

    w = nh * HEAD_DIM
    return pl.pallas_call(
        body, grid=(nh,),
        in_specs=[pl.BlockSpec((lp, LANES), lambda s: (0, s)), pl.BlockSpec((lp, LANES), lambda s: (0, 7 * nh + s)),
                  pl.BlockSpec((lp, LANES), lambda s: (0, nh + s)), _ANY],
        out_specs=[pl.BlockSpec((lp, LANES), lambda s: (0, s)), pl.BlockSpec((lp, LANES), lambda s: (0, 7 * nh + s))],
        out_shape=[jax.ShapeDtypeStruct((lp, w), F32), jax.ShapeDtypeStruct(dproj.shape, BF16)],
        input_output_aliases={3: 1}, name="merge_fox_bwd", compiler_params=_cp("parallel"))(o_fox, proj, dmerged, dproj)


def _post(out, x, target, post_w):
    lp, d = out.shape

    def body(o_ref, x_ref, t_ref, w_ref, dy_ref, do_ref, loss_ref, dw_ref):
        i = pl.program_id(0)

        @pl.when(i == 0)
        def _():
            loss_ref[...] = jnp.zeros_like(loss_ref)
            dw_ref[...] = jnp.zeros_like(dw_ref)
        o = o_ref[...]
        r = _rms(o)
        nrm = o * r
        err = jnp.where(i > 0, x_ref[...] + nrm * w_ref[...] - t_ref[...], 0.0)
        loss_ref[0:1, :] += 0.5 * jnp.sum(jnp.sum(err * err, axis=1, keepdims=True), axis=0, keepdims=True) / d
        dy = err / d
        dy_ref[...] = dy
        dw_ref[...] += jnp.sum(dy * nrm, axis=0, keepdims=True)
        dyw = dy * w_ref[...]
        do_ref[...] = (r * (dyw - nrm * jnp.mean(dyw * nrm, axis=-1, keepdims=True))).astype(BF16)

    row = pl.BlockSpec((Q_BLOCK, d), lambda i: (i, 0))
    vec = pl.BlockSpec((1, d), lambda i: (0, 0))
    return pl.pallas_call(
        body, grid=(lp // Q_BLOCK,), in_specs=[row, _x_rows(d), _x_rows(d), vec],
        out_specs=[_x_rows(d), row, pl.BlockSpec((8, LANES), lambda i: (0, 0)), vec],
        out_shape=[jax.ShapeDtypeStruct(x.shape, F32), jax.ShapeDtypeStruct((lp, d), BF16),
                   jax.ShapeDtypeStruct((8, LANES), F32), jax.ShapeDtypeStruct((1, d), F32)],
        name="post", compiler_params=_cp("arbitrary"))(out, x, target, post_w)


def _prenorm_bwd(dxn, x, meta, w, dy):
    seq, d = x.shape
    lp = seq + Q_BLOCK

    def body(dx_ref, x_ref, m_ref, w_ref, dy_ref, gx_ref, gm_ref, dw_ref):
        i = pl.program_id(0)
        h = _h_tile(i, x_ref, m_ref)
        r = _rms(h)
        xh = h * r
        dxn_ = dx_ref[...]
        dxw = dxn_ * w_ref[...]
        dh = jnp.where(i > 0, dy_ref[...], 0.0) + r * (dxw - xh * jnp.mean(dxw * xh, axis=-1, keepdims=True))
        gx_ref[...] = dh

        @pl.when(i == 0)
        def _():
            dw_ref[...] = jnp.zeros_like(dw_ref)
            gm_ref[...] = dh[PAD_ROWS:, :]
        dw_ref[...] += jnp.sum(dxn_ * xh, axis=0, keepdims=True)

    vec = pl.BlockSpec((1, d), lambda i: (0, 0))
    met = pl.BlockSpec((N_META, d), lambda i: (0, 0))
    return pl.pallas_call(
        body, grid=(lp // Q_BLOCK,),
        in_specs=[pl.BlockSpec((Q_BLOCK, d), lambda i: (i, 0)), _x_rows(d), met, vec, _x_rows(d)],
        out_specs=[_x_rows(d), met, vec],
        out_shape=[jax.ShapeDtypeStruct((seq, d), F32), jax.ShapeDtypeStruct((N_META, d), F32),
                   jax.ShapeDtypeStruct((1, d), F32)],
        name="prenorm_bwd", compiler_params=_cp("arbitrary"))(dxn, x, meta, w, dy)


def _layer_grads(x, target, meta, pre_w, wfull, conv_w, a_log, dt_bias, gdn_norm_w, fq_w, fk_w, f_bias, w_out, post_w):
    nh = a_log.shape[1]
    conv_wt = conv_w.T
    zpad = jnp.zeros((1, LANES - 3 * nh), F32)
    bias_row = jnp.concatenate([jnp.zeros((1, nh), F32), dt_bias, f_bias, zpad], axis=1)
    nega_row = jnp.concatenate([jnp.zeros((1, nh), F32), -jnp.exp(a_log), jnp.zeros((1, nh), F32), zpad], axis=1)
    qk_w = jnp.stack([fq_w, fk_w])

    xn = _prenorm(x, meta, pre_w)
    proj = _matmul(xn, wfull, "nn", MM_TILE, F32, "proj")
    qkv = _gdn_prep(proj, conv_wt, nh)
    gates, gt3, gtf = _gates(proj, bias_row, nega_row, nh)
    o_gdn, s_all = _gdn_fwd(qkv, gates, gt3, nh)
    qkn = _fox_prep(proj, qk_w, nh)
    o_fox = _fox_fwd(qkn, proj, gates, gtf, nh)
    merged = _merge_fox(o_fox, proj, _merge_gdn(o_gdn, proj, gdn_norm_w, nh), nh)
    out = _matmul(merged, w_out, "nn", 4 * LANES, F32, "out_proj")
    dy, dout, loss_blk, dpost_w = _post(out, x, target, post_w)

    dw_out = _matmul(merged, dout, "tn", 4 * LANES, BF16, "dw_out")
    dmerged = _matmul(dout, w_out, "nt", 4 * LANES, F32, "dmerged")
    do_gdn, dproj, dgdn_norm_w = _merge_gdn_bwd(o_gdn, proj, gdn_norm_w, dmerged, nh)
    do_fox, dproj = _merge_fox_bwd(o_fox, proj, dmerged, dproj, nh)
    dqn, dkn, dc_t, dproj = _fox_bwd(qkn, proj, gates, gtf, do_fox, dproj, nh)
    dproj, dqk_w = _fox_prep_bwd(proj, qk_w, dqn, dkn, dproj, nh)
    dgq, dgk, dgv, dgate = _gdn_bwd(qkv, gates, gt3, s_all, do_gdn, nh)
    dproj, dconv_wt = _gdn_prep_bwd(proj, conv_wt, dgq, dgk, dgv, dproj, nh)
    dc_rows = jnp.pad(dc_t.reshape(nh, -1), ((2 * nh, LANES - 3 * nh), (0, 0)))
    dproj, gate_sums = _gates_bwd(proj, bias_row, nega_row, gates, dgate, dc_rows, dproj, nh)
    dwfull = _matmul(xn, dproj, "tn", MM_TILE, BF16, "dw_in")
    return dict(
        loss=loss_blk[0:1, 0:1], dy=dy, dproj=dproj, wfull=dwfull, post_w=dpost_w,
        conv_w=dconv_wt.T, a_log=gate_sums[1:2, nh:2 * nh], dt_bias=gate_sums[0:1, nh:2 * nh],
        gdn_norm_w=dgdn_norm_w, fq_w=dqk_w[0], fk_w=dqk_w[1], f_bias=gate_sums[0:1, 2 * nh:3 * nh], w_out=dw_out)


def _input_grads(g, wfull, x, meta, pre_w, sums):
    dxn, exchanged = _dxn_and_chip_exchange(g["dproj"], wfull, sums, MM_TILE, "dxn_chip_exchange")
    grad_x, dmeta, dpre_w = _prenorm_bwd(dxn, x, meta, pre_w, g["dy"])
    return grad_x, dmeta, dpre_w, exchanged


def _cast_bf16(a, tr, name):
    r, c = a.shape

    def body(a_ref, o_ref):
        o_ref[...] = a_ref[...].astype(BF16)

    return pl.pallas_call(
        body, grid=(r // tr,), in_specs=[pl.BlockSpec((tr, c), lambda i: (i, 0))],
        out_specs=pl.BlockSpec((tr, c), lambda i: (i, 0)), out_shape=jax.ShapeDtypeStruct((r, c), BF16),
        name=name, compiler_params=_cp("parallel"))(a)


def _gather_copies(ins, outs, send_sems, recv_sems, local_sems):
    n = len(ins)
    x, y, c = lax.axis_index("x"), lax.axis_index("y"), lax.axis_index("c")
    sibling = (x, y, 1 - c)
    chips = [(1 - x, y), (x, 1 - y), (1 - x, 1 - y)]

    def row(a, px, py, pc):
        return outs[a].at[4 * px + 2 * py + pc]

    def copy(a, k, block, to, src=None):
        return pltpu.make_async_remote_copy(
            src_ref=row(a, *block) if src is None else src, dst_ref=row(a, *block), send_sem=send_sems.at[a, k],
            recv_sem=recv_sems.at[a, k], device_id=to, device_id_type=_MESH)

    local, sends = [], []
    for a in range(n):
        local.append(pltpu.make_async_copy(ins[a], row(a, x, y, c), local_sems.at[a]))
        sends.append(copy(a, 0, (x, y, c), sibling, src=ins[a]))
        sends += [copy(a, 1 + j, (x, y, c), (*chip, c), src=ins[a]) for j, chip in enumerate(chips)]

    def start():
        for cp in local + sends:
            cp.start()

    def finish():
        for j, chip in enumerate(chips):
            for a in range(n):
                copy(a, 1 + j, (*chip, c), (x, y, c)).wait_recv()
                fwd = copy(a, 4 + j, (*chip, c), sibling)
                fwd.start()
                sends.append(fwd)
        for a in range(n):
            copy(a, 0, sibling, (x, y, c)).wait_recv()
            for j, chip in enumerate(chips):
                copy(a, 4 + j, (*chip, 1 - c), (x, y, c)).wait_recv()
        for cp in sends:
            cp.wait_send()
        for cp in local:
            cp.wait()

    return start, finish


def _gather_scratch(n):
    return [pltpu.SemaphoreType.DMA((n, N_DEV - 1)), pltpu.SemaphoreType.DMA((n, N_DEV - 1)), pltpu.SemaphoreType.DMA((n,))]


def _all_gather(arrays, name):
    n = len(arrays)

    def body(*refs):
        start, finish = _gather_copies(refs[:n], refs[n:2 * n], *refs[2 * n:])
        start()
        finish()

    return pl.pallas_call(
        body, in_specs=[_ANY] * n, out_specs=[_ANY] * n,
        out_shape=[jax.ShapeDtypeStruct((N_DEV,) + a.shape, a.dtype) for a in arrays],
        scratch_shapes=_gather_scratch(n), name=name)(*arrays)


def _pair_exchange(arrays, small, name):
    n, ns = len(arrays), len(small)

    def body(*refs):
        ins, sm_in = refs[:n], refs[n:n + ns]
        outs, sm_out = refs[n + ns:2 * n + ns], refs[2 * n + ns:2 * (n + ns)]
        send_sems, recv_sems = refs[2 * (n + ns):2 * (n + ns) + 2]
        x, y, c = lax.axis_index("x"), lax.axis_index("y"), lax.axis_index("c")
        copies = []
        for a in range(n):
            for q in range(N_CHIP):
                copies.append(pltpu.make_async_remote_copy(
                    src_ref=ins[a].at[2 * q + 1 - c], dst_ref=outs[a].at[q], send_sem=send_sems.at[a, q],
                    recv_sem=recv_sems.at[a, q], device_id=(x, y, 1 - c), device_id_type=_MESH))
        start, finish = _gather_copies(sm_in, sm_out, *refs[2 * (n + ns) + 2:])
        for cp in copies:
            cp.start()
        start()
        finish()
        for cp in copies:
            cp.wait()

    return pl.pallas_call(
        body, in_specs=[_ANY] * (n + ns), out_specs=[_ANY] * (n + ns),
        out_shape=([jax.ShapeDtypeStruct((N_CHIP,) + a.shape[1:], a.dtype) for a in arrays]
                   + [jax.ShapeDtypeStruct((N_DEV,) + a.shape, a.dtype) for a in small]),
        scratch_shapes=[pltpu.SemaphoreType.DMA((n, N_CHIP)), pltpu.SemaphoreType.DMA((n, N_CHIP))] + _gather_scratch(ns),
        name=name)(*arrays, *small)


def _pair_sum(parts, got, core, tr, name):
    _, r, c = parts.shape

    def body(core_ref, p_ref, g_ref, o_ref):
        o_ref[...] = (p_ref[...].astype(F32) + g_ref[...].astype(F32)).astype(o_ref.dtype)

    return pl.pallas_call(
        body,
        grid_spec=pltpu.PrefetchScalarGridSpec(
            num_scalar_prefetch=1, grid=(N_CHIP, r // tr),
            in_specs=[pl.BlockSpec((1, tr, c), lambda q, i, core_ref: (2 * q + core_ref[0], i, 0)),
                      pl.BlockSpec((1, tr, c), lambda q, i, core_ref: (q, i, 0))],
            out_specs=pl.BlockSpec((1, tr, c), lambda q, i, core_ref: (q, i, 0))),
        out_shape=jax.ShapeDtypeStruct((N_CHIP, r, c), parts.dtype), name=name,
        compiler_params=_cp("parallel", "parallel"))(core, parts, got)


def _native_segments(nh):
    w = nh * HEAD_DIM
    return [(0, 4 * w, 0), (4 * w, 4 * w + 2 * nh, 8 * w), (4 * w + 2 * nh, 8 * w + 2 * nh, 4 * w),
            (8 * w + 2 * nh, 8 * w + 3 * nh, 8 * w + 2 * nh)]


def _relayout_w_in(wg, nh, tr):
    _, d, cols = wg.shape
    w = nh * HEAD_DIM

    def native(ref, j0, j1):
        out = []
        while j0 < j1:
            blk = j0 // cols
            end = min(j1, (blk + 1) * cols)
            out.append(ref[blk, :, pl.ds(j0 - blk * cols, end - j0)])
            j0 = end
        return out

    def body(g_ref, o_ref):
        for cidx in range(8 * w // LANES):
            j0 = cidx * LANES + (0 if cidx * LANES < 4 * w else 2 * nh)
            pieces = native(g_ref, j0, j0 + LANES)
            o_ref[:, cidx * LANES:(cidx + 1) * LANES] = pieces[0] if len(pieces) == 1 else jnp.concatenate(pieces, axis=1)
        pieces = (native(g_ref, 4 * w, 4 * w + 2 * nh) + native(g_ref, 8 * w + 2 * nh, 8 * w + 3 * nh)
                  + [jnp.zeros((tr, NARROW - 3 * nh), wg.dtype)])
        o_ref[:, 8 * w:] = jnp.concatenate(pieces, axis=1)

    return pl.pallas_call(
        body, grid=(d // tr,), in_specs=[pl.BlockSpec((N_DEV, tr, cols), lambda i: (0, i, 0))],
        out_specs=pl.BlockSpec((tr, 8 * w + NARROW), lambda i: (i, 0)),
        out_shape=jax.ShapeDtypeStruct((d, 8 * w + NARROW), wg.dtype),
        name="relayout_w_in", compiler_params=_cp("parallel"))(wg)


def _relayout_dw_in(dwfull, nh, tr):
    d = dwfull.shape[0]
    w = nh * HEAD_DIM
    cols = (8 * w + 3 * nh) // N_DEV
    segs = _native_segments(nh)

    def body(f_ref, o_ref):
        for blk in range(N_DEV):
            pieces = []
            for s0, s1, t0 in segs:
                lo, hi = max(s0, blk * cols), min(s1, (blk + 1) * cols)
                if lo < hi:
                    pieces.append(f_ref[:, pl.ds(t0 + lo - s0, hi - lo)])
            o_ref[blk] = pieces[0] if len(pieces) == 1 else jnp.concatenate(pieces, axis=1)

    return pl.pallas_call(
        body, grid=(d // tr,), in_specs=[pl.BlockSpec((tr, 8 * w + NARROW), lambda i: (i, 0))],
        out_specs=pl.BlockSpec((N_DEV, tr, cols), lambda i: (0, i, 0)),
        out_shape=jax.ShapeDtypeStruct((N_DEV, d, cols), dwfull.dtype),
        name="relayout_dw_in", compiler_params=_cp("parallel"))(dwfull)


def _adamw(w, parts, m, v, tr, name):
    r, c = w.shape
    n_parts = parts.shape[0]

    def body(w_ref, p_ref, m_ref, v_ref, g_ref, d_ref, nm_ref, nv_ref):
        g = p_ref[0].astype(F32)
        for s in range(1, n_parts):
            g = g + p_ref[s].astype(F32)
        m_new = ADAM_B1 * m_ref[...] + (1.0 - ADAM_B1) * g
        v_new = ADAM_B2 * v_ref[...] + (1.0 - ADAM_B2) * (g * g)
        m_hat = m_new / (1.0 - ADAM_B1 ** ADAM_STEP)
        v_hat = v_new / (1.0 - ADAM_B2 ** ADAM_STEP)
        g_ref[...] = g
        d_ref[...] = -ADAM_LR * (m_hat / (jnp.sqrt(v_hat) + ADAM_EPS) + ADAM_WD * w_ref[...])
        nm_ref[...] = m_new
        nv_ref[...] = v_new

    blk = pl.BlockSpec((tr, c), lambda i: (i, 0))
    return pl.pallas_call(
        body, grid=(r // tr,), in_specs=[blk, pl.BlockSpec((n_parts, tr, c), lambda i: (0, i, 0)), blk, blk],
        out_specs=[blk] * 4, out_shape=[jax.ShapeDtypeStruct((r, c), F32)] * 4, name=name,
        compiler_params=_cp("parallel"))(w, parts, m, v)


def _pack_small(d, pre, post, a_log, dt_bias, f_bias, gdn_w, fq_w, fk_w, extra):
    row2 = jnp.concatenate([a_log, dt_bias, f_bias, gdn_w, fq_w, fk_w, extra], axis=1)
    row2 = jnp.pad(row2, ((0, 0), (0, d - row2.shape[1])))
    return jnp.concatenate([pre, post, row2, jnp.zeros((5, d), F32)], axis=0)


def _unpack_small(p, nh):
    o = 3 * nh
    return dict(pre=p[0:1], post=p[1:2], a_log=p[2:3, 0:nh], dt_bias=p[2:3, nh:2 * nh], f_bias=p[2:3, 2 * nh:o],
                gdn_w=p[2:3, o:o + HEAD_DIM], fq_w=p[2:3, o + HEAD_DIM:o + 2 * HEAD_DIM],
                fk_w=p[2:3, o + 2 * HEAD_DIM:o + 3 * HEAD_DIM], extra=p[2, o + 3 * HEAD_DIM])


def kernel(x, meta_tokens, pre_norm_w, w_in, conv_w, a_log, dt_bias, gdn_norm_w, fox_q_norm_w, fox_k_norm_w, fox_f_bias, w_out, post_norm_w, loss_target, m_meta_tokens, m_pre_norm_w, m_w_in, m_conv_w, m_a_log, m_dt_bias, m_gdn_norm_w, m_fox_q_norm_w, m_fox_k_norm_w, m_fox_f_bias, m_w_out, m_post_norm_w, v_meta_tokens, v_pre_norm_w, v_w_in, v_conv_w, v_a_log, v_dt_bias, v_gdn_norm_w, v_fox_q_norm_w, v_fox_k_norm_w, v_fox_f_bias, v_w_out, v_post_norm_w):
    nh = a_log.shape[1]
    d = x.shape[-1]
    w = nh * HEAD_DIM
    zero = jnp.zeros((1, 1), F32)

    wg, wog, cg, mg = _all_gather(
        [_cast_bf16(w_in[0], 256, "cast_w_in"), _cast_bf16(w_out[0], 256, "cast_w_out"), conv_w[0], meta_tokens],
        "gather_weights")
    wfull = _relayout_w_in(wg, nh, 256)
    meta_full = mg.transpose(1, 0, 2).reshape(N_META, d)
    g = _layer_grads(x[0], loss_target[0], meta_full, pre_norm_w, wfull, cg.reshape(3 * w, CONV_WIDTH), a_log, dt_bias,
                     gdn_norm_w, fox_q_norm_w, fox_k_norm_w, fox_f_bias, wog.reshape(2 * w, d), post_norm_w)

    core = lax.axis_index("c")
    dev = 4 * lax.axis_index("x") + 2 * lax.axis_index("y") + core
    core_arr = jnp.reshape(core, (1,)).astype(jnp.int32)
    mine_in = _relayout_dw_in(g["wfull"], nh, 128)
    mine_out = g["w_out"].reshape(N_DEV, 2 * w // N_DEV, d)
    got_in, got_out, a_conv = _pair_exchange([mine_in, mine_out], [g["conv_w"]], "pair_exchange")
    sums = [_pair_sum(mine_in, got_in, core_arr, 128, "pair_sum_w_in"),
            _pair_sum(mine_out, got_out, core_arr, 256, "pair_sum_w_out")]
    grad_x, dmeta, dpre_w, (p_in, p_out) = _input_grads(g, wfull, x[0], meta_full, pre_norm_w, sums)
    small = _pack_small(d, dpre_w, g["post_w"], g["a_log"], g["dt_bias"], g["f_bias"], g["gdn_norm_w"], g["fq_w"],
                        g["fk_w"], g["loss"])
    a_meta, p_small = _all_gather([dmeta, small], "gather_small_grads")
    p_conv = lax.dynamic_slice_in_dim(a_conv, dev * conv_w.shape[1], conv_w.shape[1], axis=1)
    p_meta = lax.dynamic_slice_in_dim(a_meta, dev * meta_tokens.shape[1], meta_tokens.shape[1], axis=2)

    r_in = _adamw(w_in[0], p_in, m_w_in[0], v_w_in[0], 128, "adamw_w_in")
    r_out = _adamw(w_out[0], p_out, m_w_out[0], v_w_out[0], 64, "adamw_w_out")
    r_conv = _adamw(conv_w[0], p_conv, m_conv_w[0], v_conv_w[0], conv_w.shape[1], "adamw_conv_w")
    r_meta = _adamw(meta_tokens, p_meta, m_meta_tokens, v_meta_tokens, N_META, "adamw_meta")
    pk = lambda pre, post, a, dt, gw, fq, fk, fb: _pack_small(d, pre, post, a, dt, fb, gw, fq, fk, zero)
    r_small = _adamw(
        pk(pre_norm_w, post_norm_w, a_log, dt_bias, gdn_norm_w, fox_q_norm_w, fox_k_norm_w, fox_f_bias), p_small,
        pk(m_pre_norm_w, m_post_norm_w, m_a_log, m_dt_bias, m_gdn_norm_w, m_fox_q_norm_w, m_fox_k_norm_w, m_fox_f_bias),
        pk(v_pre_norm_w, v_post_norm_w, v_a_log, v_dt_bias, v_gdn_norm_w, v_fox_q_norm_w, v_fox_k_norm_w, v_fox_f_bias),
        8, "adamw_small")

    sm = [_unpack_small(r, nh) for r in r_small]
    outs = []
    for i in range(4):
        s = sm[i]
        outs += [r_meta[i], s["pre"], r_in[i][None], r_conv[i][None], s["a_log"], s["dt_bias"], s["gdn_w"], s["fq_w"],
                 s["fk_w"], s["f_bias"], r_out[i][None], s["post"]]
    return (sm[0]["extra"], grad_x[None], *outs)
```

```python
import jax
import jax.numpy as jnp
from jax import lax
from jax.experimental import pallas as pl
from jax.experimental.pallas import tpu as pltpu

F32, BF16 = jnp.float32, jnp.bfloat16
HEAD_DIM = 128
N_META = 16
CONV_WIDTH = 4
CHUNK = 64
Q_BLOCK = 128
LANES = 128
EPS = 1e-6
PAD_ROWS = Q_BLOCK - N_META
N_DEV = 8
N_CHIP = 4
VMEM_LIMIT = 56 * 1024 * 1024
NEG = -1e30
NARROW = 2 * LANES
MM_TILE = 6 * LANES

ADAM_LR, ADAM_B1, ADAM_B2, ADAM_EPS, ADAM_WD, ADAM_STEP = 0.001, 0.9, 0.999, 1e-08, 0.01, 10

_DN = {"nn": (((1,), (0,)), ((), ())), "nt": (((1,), (1,)), ((), ())), "tn": (((0,), (0,)), ((), ()))}
_DN3 = {"nn": (((2,), (1,)), ((0,), (0,))), "nt": (((2,), (2,)), ((0,), (0,))), "tn": (((1,), (1,)), ((0,), (0,)))}
_ANY = pl.BlockSpec(memory_space=pl.ANY)
_MESH = pl.DeviceIdType.MESH


def _cp(*sem):
    return pltpu.CompilerParams(dimension_semantics=sem, vmem_limit_bytes=VMEM_LIMIT)


def _dot(a, b, dims="nn", prec=None):
    return lax.dot_general(a, b, _DN[dims], precision=prec, preferred_element_type=F32)


def _bdot(a, b, dims="nn"):
    return _dot(a.astype(BF16), b.astype(BF16), dims)


def _hdot(a, b, dims="nn"):
    return _dot(a, b, dims, prec=lax.Precision.HIGHEST)


def _dot3(a, b, dims="nn"):
    return lax.dot_general(a, b, _DN3[dims], preferred_element_type=F32)


def _bdot3(a, b, dims="nn"):
    return _dot3(a.astype(BF16), b.astype(BF16), dims)


def _split(a):
    hi = a.astype(BF16)
    return hi, (a - hi.astype(F32)).astype(BF16)


def _iota(shape, dim):
    return lax.broadcasted_iota(jnp.int32, shape, dim)


def _sigmoid(z):
    return 1.0 / (1.0 + jnp.exp(-z))


def _softplus(z):
    e = jnp.exp(-jnp.abs(z))
    u = 1.0 + e
    l1p = jnp.where(u == 1.0, e, jnp.log(u) * (e / jnp.where(u == 1.0, 1.0, u - 1.0)))
    return jnp.maximum(z, 0.0) + l1p


def _silu_and_grad(z):
    s = _sigmoid(z)
    return z * s, s * (1.0 + z * (1.0 - s))


def _rms(x):
    return lax.rsqrt(jnp.mean(x * x, axis=-1, keepdims=True) + EPS)


def _h_tile(i, x_ref, meta_ref):
    first = jnp.concatenate([jnp.zeros((PAD_ROWS, x_ref.shape[1]), F32), meta_ref[...]], axis=0)
    return jnp.where(i == 0, first, x_ref[...])


def _x_rows(d):
    return pl.BlockSpec((Q_BLOCK, d), lambda i: (jnp.maximum(i - 1, 0), 0))


def _prenorm(x, meta, w):
    seq, d = x.shape
    lp = seq + Q_BLOCK

    def body(x_ref, m_ref, w_ref, o_ref):
        h = _h_tile(pl.program_id(0), x_ref, m_ref)
        o_ref[...] = (h * _rms(h) * w_ref[...]).astype(BF16)

    return pl.pallas_call(
        body, grid=(lp // Q_BLOCK,),
        in_specs=[_x_rows(d), pl.BlockSpec((N_META, d), lambda i: (0, 0)), pl.BlockSpec((1, d), lambda i: (0, 0))],
        out_specs=pl.BlockSpec((Q_BLOCK, d), lambda i: (i, 0)),
        out_shape=jax.ShapeDtypeStruct((lp, d), BF16), name="prenorm", compiler_params=_cp("parallel"))(x, meta, w)


def _tile(n, want):
    return max(t for t in range(LANES, want + 1, LANES) if n % t == 0)


def _tile_index(tiles):
    first, count, tail = tiles
    return (lambda j: first + j) if tail is None else (lambda j: jnp.where(j < count, first + j, tail))


def _matmul(a, b, dims, tn, out_dtype, name, b_tiles=None, out_tiles=None, n_out=None, into=None):
    m = a.shape[1] if dims == "tn" else a.shape[0]
    n = b.shape[0] if dims == "nt" else b.shape[1]
    kdim = b.shape[1] if dims == "nt" else b.shape[0]
    if b_tiles is None:
        tn = _tile(n, tn)
        b_tiles = out_tiles = (0, n // tn, None)
    steps = b_tiles[1] + (b_tiles[2] is not None)
    bi, oi = _tile_index(b_tiles), _tile_index(out_tiles)
    b_spec = pl.BlockSpec((tn, kdim), lambda j: (bi(j), 0)) if dims == "nt" else pl.BlockSpec((kdim, tn), lambda j: (0, bi(j)))

    def body(a_ref, b_ref, *rest):
        rest[-1][...] = _dot(a_ref[...], b_ref[...], dims).astype(out_dtype)

    extra = [] if into is None else [into]
    return pl.pallas_call(
        body, grid=(steps,),
        in_specs=[pl.BlockSpec(a.shape, lambda j: (0, 0)), b_spec] + [_ANY] * len(extra),
        out_specs=pl.BlockSpec((m, tn), lambda j: (0, oi(j))),
        out_shape=jax.ShapeDtypeStruct((m, n_out or n), out_dtype), name=name,
        input_output_aliases={2: 0} if extra else {}, compiler_params=_cp("parallel"))(a, b, *extra)


def _when(cond, fn):
    if cond is None:
        fn()
    else:
        pl.when(cond)(fn)


class _Rider:
    def __init__(self, inputs, out_shapes, scratch, aliases, make):
        self.inputs, self.out_shapes, self.scratch, self.aliases, self.make = inputs, out_shapes, scratch, aliases, make


def _ride(rider, n_in, n_out, n_scratch, refs):
    if rider is None:
        return refs, (lambda: None,) * 3
    ri, ro, rs = len(rider.inputs), len(rider.out_shapes), len(rider.scratch)
    ins, r_in = refs[:n_in], refs[n_in:n_in + ri]
    outs, r_out = refs[n_in + ri:n_in + ri + n_out], refs[n_in + ri + n_out:n_in + ri + n_out + ro]
    scr, r_scr = refs[n_in + ri + n_out + ro:n_in + ri + n_out + ro + n_scratch], refs[n_in + ri + n_out + ro + n_scratch:]
    assert len(r_scr) == rs
    return tuple(ins) + tuple(outs) + tuple(scr), rider.make(r_in, r_out, r_scr)


def _rider_call(body, rider, n_in, n_out, n_scratch, *, in_specs, out_specs, out_shape, scratch_shapes=(), aliases=None, **kw):
    r_in = [] if rider is None else list(rider.inputs)
    r_out = [] if rider is None else list(rider.out_shapes)
    al = dict(aliases or {})
    if rider is not None:
        al.update({n_in + i: n_out + o for i, o in rider.aliases.items()})

    def full_body(*refs):
        host, hooks = _ride(rider, n_in, n_out, n_scratch, refs)
        body(hooks, *host)

    res = pl.pallas_call(
        full_body, in_specs=list(in_specs) + [_ANY] * len(r_in), out_specs=list(out_specs) + [_ANY] * len(r_out),
        out_shape=list(out_shape) + r_out, scratch_shapes=list(scratch_shapes) + ([] if rider is None else list(rider.scratch)),
        input_output_aliases=al, **kw)
    return lambda *args: (lambda r: (r[:n_out], r[n_out:]))(res(*args, *r_in))


def _chip_rider(sums, dest_x, into=None):
    n = len(sums)
    into = into or [None] * n
    extra = [t for t in into if t is not None]
    aliases = {}
    for a in range(n):
        if into[a] is not None:
            aliases[n + len(aliases)] = a

    def make(in_refs, outs, scratch):
        ins = in_refs[:n]
        send_sems, recv_sems, local_sems = scratch
        x, y, c = lax.axis_index("x"), lax.axis_index("y"), lax.axis_index("c")
        mine = 2 * x + y
        plan = []
        for a in range(n):
            dx = dest_x[a]
            i_recv = None if dx is None else x == dx
            own_row = mine if dx is None else y
            plan.append((i_recv, pltpu.make_async_copy(ins[a].at[own_row], outs[a].at[mine], local_sems.at[a]), "local"))
            for k in range(1, N_CHIP):
                px = 1 - x if k & 2 else x
                py = 1 - y if k & 1 else y
                kw = dict(send_sem=send_sems.at[a, k - 1], recv_sem=recv_sems.at[a, k - 1], device_id=(px, py, c),
                          device_id_type=_MESH)
                row = 2 * px + py if dx is None else py
                plan.append((None if dx is None else px == dx,
                             pltpu.make_async_remote_copy(src_ref=ins[a].at[row], dst_ref=outs[a].at[mine], **kw), "send"))
                plan.append((i_recv, pltpu.make_async_remote_copy(src_ref=ins[a].at[own_row], dst_ref=outs[a].at[2 * px + py],
                                                                  **kw), "recv"))

        def start():
            for cond, cp, kind in plan:
                if kind != "recv":
                    _when(cond, cp.start)

        def finish():
            for cond, cp, kind in plan:
                _when(cond, cp.wait if kind == "local" else (cp.wait_send if kind == "send" else cp.wait_recv))

        return start, (lambda: None), finish

    out_shapes = [jax.ShapeDtypeStruct((N_CHIP,) + s.shape[1:], s.dtype) for s in sums]
    scratch = [pltpu.SemaphoreType.DMA((n, N_CHIP - 1)), pltpu.SemaphoreType.DMA((n, N_CHIP - 1)), pltpu.SemaphoreType.DMA((n,))]
    return _Rider(list(sums) + extra, out_shapes, scratch, aliases, make)


def _dxn(dproj_g, dproj_f, w_g, w_f, nh, rider, name):
    m = dproj_g.shape[0]
    n = w_g.shape[0]
    half = 4 * nh * HEAD_DIM
    tk = _tile(half, 4 * LANES)
    per = half // tk
    steps = 2 * per + 1

    def body(hooks, ag_ref, af_ref, an_ref, bg_ref, bf_ref, bn_ref, o_ref):
        start, _, finish = hooks
        j = pl.program_id(0)

        @pl.when(j == 0)
        def _():
            start()
            o_ref[...] = jnp.zeros_like(o_ref)

        @pl.when(j < per)
        def _():
            o_ref[...] += _dot(ag_ref[...], bg_ref[...], "nt")

        @pl.when((j >= per) & (j < 2 * per))
        def _():
            o_ref[...] += _dot(af_ref[...], bf_ref[...], "nt")

        @pl.when(j == steps - 1)
        def _():
            o_ref[...] += _dot(an_ref[...], bn_ref[...], "nt")
            finish()

    g_tile = lambda j: (0, jnp.minimum(j, per - 1))
    f_tile = lambda j: (0, jnp.clip(j - per, 0, per - 1))
    call = _rider_call(
        body, rider, 6, 1, 0, grid=(steps,),
        in_specs=[pl.BlockSpec((m, tk), g_tile), pl.BlockSpec((m, tk), f_tile),
                  pl.BlockSpec((m, NARROW), lambda j: (0, half // NARROW)),
                  pl.BlockSpec((n, tk), g_tile), pl.BlockSpec((n, tk), f_tile),
                  pl.BlockSpec((n, NARROW), lambda j: (0, half // NARROW))],
        out_specs=[pl.BlockSpec((m, n), lambda j: (0, 0))], out_shape=[jax.ShapeDtypeStruct((m, n), F32)],
        name=name, compiler_params=_cp("arbitrary"))
    return call(dproj_g, dproj_f, dproj_f, w_g, w_f, w_g)


def _conv_taps(x, w):
    c = x * w[CONV_WIDTH - 1:CONV_WIDTH, :]
    for j in range(CONV_WIDTH - 1):
        c = c + pltpu.roll(x, CONV_WIDTH - 1 - j, 0) * w[j:j + 1, :]
    return c


def _gdn_prep(proj, conv_wt, nh):
    lp = proj.shape[0]
    scale = HEAD_DIM ** -0.5

    def body(x_ref, w_ref, o_ref):
        which = pl.program_id(0) // nh
        c = _conv_taps(x_ref[...], w_ref[...])
        s = c * _sigmoid(c)
        r = lax.rsqrt(jnp.sum(s * s, axis=-1, keepdims=True) + EPS)
        f = jnp.where(which == 0, r * scale, jnp.where(which == 1, r, 1.0))
        o_ref[...] = jnp.where(_iota(s.shape, 0) >= PAD_ROWS, s * f, 0.0)

    return pl.pallas_call(
        body, grid=(3 * nh,),
        in_specs=[pl.BlockSpec((lp, LANES), lambda s: (0, s)), pl.BlockSpec((CONV_WIDTH, LANES), lambda s: (0, s))],
        out_specs=pl.BlockSpec((lp, LANES), lambda s: (0, s)),
        out_shape=jax.ShapeDtypeStruct((lp, 3 * nh * HEAD_DIM), F32), name="gdn_prep",
        compiler_params=_cp("parallel"))(proj, conv_wt)


def _gdn_prep_bwd(proj, conv_wt, dq, dk, dv, dproj, nh):
    lp = proj.shape[0]
    scale = HEAD_DIM ** -0.5
    part = lambda p: pl.BlockSpec((lp, LANES), lambda s: (0, jnp.clip(s - p * nh, 0, nh - 1)))

    def body(x_ref, w_ref, dq_ref, dk_ref, dv_ref, _, dx_ref, dw_ref):
        which = pl.program_id(0) // nh
        x = x_ref[...]
        w = w_ref[...]
        c = _conv_taps(x, w)
        sg = _sigmoid(c)
        s = c * sg
        r = lax.rsqrt(jnp.sum(s * s, axis=-1, keepdims=True) + EPS)
        dy = jnp.where(which == 0, dq_ref[...], jnp.where(which == 1, dk_ref[...], dv_ref[...]))
        dy = jnp.where(_iota(s.shape, 0) >= PAD_ROWS, dy, 0.0)
        y0 = s * r
        dy0 = dy * jnp.where(which == 0, scale, 1.0)
        ds_n = r * (dy0 - y0 * jnp.sum(dy0 * y0, axis=-1, keepdims=True))
        ds = jnp.where(which == 2, dy, ds_n)
        dc = ds * (sg * (1.0 + c * (1.0 - sg)))
        dx = dc * w[CONV_WIDTH - 1:CONV_WIDTH, :]
        rows = [jnp.sum(dc * x, axis=0, keepdims=True)]
        for j in range(CONV_WIDTH - 2, -1, -1):
            sh = CONV_WIDTH - 1 - j
            dx = dx + pltpu.roll(dc, lp - sh, 0) * w[j:j + 1, :]
            rows.insert(0, jnp.sum(dc * pltpu.roll(x, sh, 0), axis=0, keepdims=True))
        dx_ref[...] = dx.astype(BF16)
        dw_ref[...] = jnp.concatenate(rows, axis=0)

    strip = pl.BlockSpec((lp, LANES), lambda s: (0, s))
    taps = pl.BlockSpec((CONV_WIDTH, LANES), lambda s: (0, s))
    return pl.pallas_call(
        body, grid=(3 * nh,), in_specs=[strip, taps, part(0), part(1), part(2), _ANY], out_specs=[strip, taps],
        out_shape=[jax.ShapeDtypeStruct(dproj.shape, BF16), jax.ShapeDtypeStruct((CONV_WIDTH, 3 * nh * HEAD_DIM), F32)],
        input_output_aliases={5: 0}, name="gdn_prep_bwd", compiler_params=_cp("parallel"))(proj, conv_wt, dq, dk, dv, dproj)


def _gates(proj, bias_row, nega_row, nh):
    lp = proj.shape[0]
    nc = lp // CHUNK

    def body(p_ref, b_ref, a_ref, g_ref, gt3_ref, gtf_ref):
        lane = _iota((CHUNK, LANES), 1)
        tri = (_iota((CHUNK, CHUNK), 0) >= _iota((CHUNK, CHUNK), 1)).astype(F32)

        def step(n, carry):
            r0 = pl.multiple_of(n * CHUNK, CHUNK)
            z = p_ref[pl.ds(r0, CHUNK), :] + b_ref[...]
            base = jnp.where(lane < nh, _sigmoid(z),
                             jnp.where(lane < 2 * nh, a_ref[...] * _softplus(z),
                                       jnp.where(lane < 3 * nh, -_softplus(-z), 0.0)))
            base = jnp.where(r0 + _iota((CHUNK, LANES), 0) >= PAD_ROWS, base, 0.0)
            cs = _hdot(tri, base)
            run = jnp.where((lane >= 2 * nh) & (lane < 3 * nh), cs + carry, cs)
            sh = pltpu.roll(run, 2 * nh, 1)
            out = base + jnp.where((lane >= 3 * nh) & (lane < 5 * nh), sh, 0.0)
            g_ref[pl.ds(r0, CHUNK), :] = out
            gt3_ref[n] = out.T
            return carry + cs[CHUNK - 1:CHUNK, :]

        lax.fori_loop(0, nc, step, jnp.zeros((1, LANES), F32))
        gtf_ref[...] = g_ref[...].T

    vec = pl.BlockSpec((1, LANES), lambda i: (0, 0))
    return pl.pallas_call(
        body, grid=(1,), in_specs=[pl.BlockSpec((lp, LANES), lambda i: (0, 4 * nh)), vec, vec],
        out_specs=[pl.BlockSpec((lp, LANES), lambda i: (0, 0)), pl.BlockSpec((nc, LANES, CHUNK), lambda i: (0, 0, 0)),
                   pl.BlockSpec((LANES, lp), lambda i: (0, 0))],
        out_shape=[jax.ShapeDtypeStruct((lp, LANES), F32), jax.ShapeDtypeStruct((nc, LANES, CHUNK), F32),
                   jax.ShapeDtypeStruct((LANES, lp), F32)],
        name="gates", compiler_params=_cp("arbitrary"))(proj, bias_row, nega_row)


def _gates_bwd(proj, bias_row, nega_row, gates, dgate_gdn, dc_t, dproj, nh):
    lp = proj.shape[0]
    nc = lp // CHUNK

    def body(p_ref, b_ref, a_ref, g_ref, dg_ref, dc_ref, _, dz_ref, sm_ref, dct_scr):
        lane = _iota((CHUNK, LANES), 1)
        triu = (_iota((CHUNK, CHUNK), 0) <= _iota((CHUNK, CHUNK), 1)).astype(F32)
        dct_scr[...] = dc_ref[...].T
        sm_ref[...] = jnp.zeros_like(sm_ref)
        dz_ref[:, LANES:] = jnp.zeros((lp, NARROW - LANES), BF16)

        def step(i, carry):
            n = nc - 1 - i
            r0 = pl.multiple_of(n * CHUNK, CHUNK)
            z = p_ref[pl.ds(r0, CHUNK), :] + b_ref[...]
            gt = g_ref[pl.ds(r0, CHUNK), :]
            dgd = dg_ref[pl.ds(r0, CHUNK), :]
            dch = dct_scr[pl.ds(r0, CHUNK), :]
            rc = _hdot(triu, dch) + carry
            sg = _sigmoid(z)
            dz = jnp.where(lane < nh, dgd * sg * (1.0 - sg),
                           jnp.where(lane < 2 * nh, dgd * a_ref[...] * sg,
                                     jnp.where(lane < 3 * nh, rc * (1.0 - sg), 0.0)))
            dz = jnp.where(r0 + _iota((CHUNK, LANES), 0) >= PAD_ROWS, dz, 0.0)
            dz_ref[pl.ds(r0, CHUNK), 0:LANES] = dz.astype(BF16)
            sm_ref[0:1, :] += jnp.sum(dz, axis=0, keepdims=True)
            sm_ref[1:2, :] += jnp.sum(jnp.where((lane >= nh) & (lane < 2 * nh), dgd * gt, 0.0), axis=0, keepdims=True)
            return carry + jnp.sum(dch, axis=0, keepdims=True)

        lax.fori_loop(0, nc, step, jnp.zeros((1, LANES), F32))

    vec = pl.BlockSpec((1, LANES), lambda i: (0, 0))
    full = pl.BlockSpec((lp, LANES), lambda i: (0, 0))
    last = pl.BlockSpec((lp, LANES), lambda i: (0, 4 * nh))
    tail = pl.BlockSpec((lp, NARROW), lambda i: (0, 4 * nh * LANES // NARROW))
    return pl.pallas_call(
        body, grid=(1,), in_specs=[last, vec, vec, full, full, pl.BlockSpec((LANES, lp), lambda i: (0, 0)), _ANY],
        out_specs=[tail, pl.BlockSpec((8, LANES), lambda i: (0, 0))],
        out_shape=[jax.ShapeDtypeStruct(dproj.shape, BF16), jax.ShapeDtypeStruct((8, LANES), F32)],
        scratch_shapes=[pltpu.VMEM((lp, LANES), F32)], input_output_aliases={6: 0},
        name="gates_bwd", compiler_params=_cp("arbitrary"))(proj, bias_row, nega_row, gates, dgate_gdn, dc_t, dproj)


def _tri_inv(a):
    t = jnp.where(_iota(a.shape, 1) == _iota(a.shape, 2), 1.0, 0.0) - a
    p = a
    for _ in range(5):
        ph, pw = _split(p)
        p = _dot3(ph, ph) + (_dot3(ph, pw) + _dot3(pw, ph))
        ph, pw = _split(p)
        th, tw = _split(t)
        t = t + (_dot3(th, ph) + (_dot3(th, pw) + _dot3(tw, ph)))
    return t


def _gdn_chunk(q, k, v, beta, gc, gr):
    ii, jj = _iota((1, CHUNK, CHUNK), 1), _iota((1, CHUNK, CHUNK), 2)
    causal, strict = ii >= jj, ii > jj
    dm = jnp.where(causal, jnp.exp(jnp.where(causal, gc - gr, 0.0)), 0.0)
    kk = _bdot3(k, k, "nt")
    a = jnp.where(strict, beta * kk * dm, 0.0)
    t = _tri_inv(a)
    eg = jnp.exp(gc)
    glast = gc[:, CHUNK - 1:CHUNK, :]
    ekd = jnp.exp(glast - gc)
    bv = beta * v
    bk = (beta * eg) * k
    ub = _bdot3(t, jnp.concatenate([bv, bk], axis=2))
    qk = _bdot3(q, k, "nt")
    return dict(causal=causal, strict=strict, dm=dm, kk=kk, a=a, t=t, eg=eg, ekd=ekd, bv=bv, bk=bk,
                u=ub[:, :, :HEAD_DIM], w=ub[:, :, HEAD_DIM:], qk=qk, aqk=jnp.where(causal, qk * dm, 0.0),
                q_dec=q * eg, k_dec=k * ekd, decay=jnp.exp(glast))


def _heads(ref, nh):
    return jnp.stack([ref[:, h * HEAD_DIM:(h + 1) * HEAD_DIM] for h in range(nh)], axis=0)


def _gdn_chunk_inputs(q_ref, k_ref, v_ref, g, gt, nh):
    col = lambda o: jnp.stack([g[:, o + h:o + h + 1] for h in range(nh)], axis=0)
    gr = jnp.stack([gt[3 * nh + h:3 * nh + h + 1, :] for h in range(nh)], axis=0)
    return _heads(q_ref, nh), _heads(k_ref, nh), _heads(v_ref, nh), col(0), col(3 * nh), gr


def _gdn_fwd(qkv, gates, gt3, nh):
    lp = qkv.shape[0]
    nc = lp // CHUNK
    w = nh * HEAD_DIM

    def body(q_ref, k_ref, v_ref, g_ref, gt_ref, o_ref, sall_ref, s_scr):
        @pl.when(pl.program_id(0) == 0)
        def _():
            s_scr[...] = jnp.zeros_like(s_scr)
        c = _gdn_chunk(*_gdn_chunk_inputs(q_ref, k_ref, v_ref, g_ref[...], gt_ref[0], nh))
        s = s_scr[...]
        sall_ref[0] = s
        v_new = c["u"] - _bdot3(c["w"], s)
        o = _bdot3(c["q_dec"], s) + _bdot3(c["aqk"], v_new)
        s_scr[...] = s * c["decay"] + _bdot3(c["k_dec"], v_new, "tn")
        for h in range(nh):
            o_ref[:, h * HEAD_DIM:(h + 1) * HEAD_DIM] = o[h]

    return pl.pallas_call(
        body, grid=(nc,),
        in_specs=[pl.BlockSpec((CHUNK, w), lambda n: (n, 0)), pl.BlockSpec((CHUNK, w), lambda n: (n, 1)),
                  pl.BlockSpec((CHUNK, w), lambda n: (n, 2)), pl.BlockSpec((CHUNK, LANES), lambda n: (n, 0)),
                  pl.BlockSpec((1, LANES, CHUNK), lambda n: (n, 0, 0))],
        out_specs=[pl.BlockSpec((CHUNK, w), lambda n: (n, 0)),
                   pl.BlockSpec((1, nh, HEAD_DIM, HEAD_DIM), lambda n: (n, 0, 0, 0))],
        out_shape=[jax.ShapeDtypeStruct((lp, w), F32), jax.ShapeDtypeStruct((nc, nh, HEAD_DIM, HEAD_DIM), F32)],
        scratch_shapes=[pltpu.VMEM((nh, HEAD_DIM, HEAD_DIM), F32)],
        name="gdn_fwd", compiler_params=_cp("arbitrary"))(qkv, qkv, qkv, gates, gt3)


def _gdn_bwd(qkv, gates, gt3, s_all, do, nh):
    lp = qkv.shape[0]
    nc = lp // CHUNK
    w = nh * HEAD_DIM
    rev = lambda n: nc - 1 - n

    def body(q_ref, k_ref, v_ref, g_ref, gt_ref, s_ref, do_ref, dq_ref, dk_ref, dv_ref, dg_ref, ds_scr):
        @pl.when(pl.program_id(0) == 0)
        def _():
            ds_scr[...] = jnp.zeros_like(ds_scr)
        q, k, v, beta, gc, gr = _gdn_chunk_inputs(q_ref, k_ref, v_ref, g_ref[...], gt_ref[0], nh)
        c = _gdn_chunk(q, k, v, beta, gc, gr)
        s = s_ref[0]
        dsn = ds_scr[...]
        dout = _heads(do_ref, nh)
        v_new = c["u"] - _bdot3(c["w"], s)
        dq_dec = _bdot3(dout, s, "nt")
        daqk = jnp.where(c["causal"], _bdot3(dout, v_new, "nt"), 0.0)
        dv_new = _bdot3(c["aqk"], dout, "tn") + _bdot3(c["k_dec"], dsn)
        dk_dec = _bdot3(v_new, dsn, "nt")
        ddecay = jnp.sum(jnp.sum(dsn * s, axis=2, keepdims=True), axis=1, keepdims=True)
        dw = -_bdot3(dv_new, s, "nt")
        ds_scr[...] = _bdot3(c["q_dec"], dout, "tn") + c["decay"] * dsn - _bdot3(c["w"], dv_new, "tn")
        duw = jnp.concatenate([dv_new, dw], axis=2)
        dt = _bdot3(duw, jnp.concatenate([c["bv"], c["bk"]], axis=2), "nt")
        dbvk = _bdot3(c["t"], duw, "tn")
        dbv, dbk = dbvk[:, :, :HEAD_DIM], dbvk[:, :, HEAD_DIM:]
        da = jnp.where(c["strict"], -_bdot3(_bdot3(c["t"], dt, "tn"), c["t"], "nt"), 0.0)
        dkk = da * beta * c["dm"]
        dqk = daqk * c["dm"]
        e = da * c["a"] + daqk * c["aqk"]
        dq = dq_dec * c["eg"] + _bdot3(dqk, k)
        dk = (dk_dec * c["ekd"] + _bdot3(dkk, k) + _bdot3(dkk, k, "tn") + _bdot3(dqk, q, "tn")
              + (beta * c["eg"]) * dbk)
        dv = beta * dbv
        rs = lambda x: jnp.sum(x, axis=2, keepdims=True)
        dbeta = rs(dbv * v) + c["eg"] * rs(dbk * k) + rs(da * c["kk"] * c["dm"])
        kd_term = rs(dk_dec * c["k_dec"])
        eh, ew = _split(e)
        ones = jnp.ones((nh, CHUNK, LANES), BF16)
        col_sums = (_dot3(eh, ones, "tn") + _dot3(ew, ones, "tn"))[:, :, 0:1]
        dg_cum = rs(dq_dec * c["q_dec"]) - kd_term + rs(dbk * c["bk"]) + rs(e) - col_sums
        last = jnp.sum(kd_term, axis=1, keepdims=True) + ddecay * c["decay"]
        dg_cum = dg_cum + jnp.where(_iota((1, CHUNK, 1), 1) == CHUNK - 1, last, 0.0)
        lane = _iota((CHUNK, LANES), 1)
        acc = jnp.zeros((CHUNK, LANES), F32)
        for h in range(nh):
            sl = slice(h * HEAD_DIM, (h + 1) * HEAD_DIM)
            dq_ref[:, sl] = dq[h]
            dk_ref[:, sl] = dk[h]
            dv_ref[:, sl] = dv[h]
            acc = acc + jnp.where(lane == h, dbeta[h], 0.0) + jnp.where(lane == nh + h, dg_cum[h], 0.0)
        triu = (_iota((CHUNK, CHUNK), 0) <= _iota((CHUNK, CHUNK), 1)).astype(F32)
        dg_ref[...] = jnp.where(lane < nh, acc, _hdot(triu, acc))

    return pl.pallas_call(
        body, grid=(nc,),
        in_specs=[pl.BlockSpec((CHUNK, w), lambda n: (rev(n), 0)), pl.BlockSpec((CHUNK, w), lambda n: (rev(n), 1)),
                  pl.BlockSpec((CHUNK, w), lambda n: (rev(n), 2)), pl.BlockSpec((CHUNK, LANES), lambda n: (rev(n), 0)),
                  pl.BlockSpec((1, LANES, CHUNK), lambda n: (rev(n), 0, 0)),
                  pl.BlockSpec((1, nh, HEAD_DIM, HEAD_DIM), lambda n: (rev(n), 0, 0, 0)),
                  pl.BlockSpec((CHUNK, w), lambda n: (rev(n), 0))],
        out_specs=[pl.BlockSpec((CHUNK, w), lambda n: (rev(n), 0))] * 3 + [pl.BlockSpec((CHUNK, LANES), lambda n: (rev(n), 0))],
        out_shape=[jax.ShapeDtypeStruct((lp, w), F32)] * 3 + [jax.ShapeDtypeStruct((lp, LANES), F32)],
        scratch_shapes=[pltpu.VMEM((nh, HEAD_DIM, HEAD_DIM), F32)],
        name="gdn_bwd", compiler_params=_cp("arbitrary"))(qkv, qkv, qkv, gates, gt3, s_all, do)


def _merge_gdn(o_gdn, proj, norm_w, merged, nh):
    lp = o_gdn.shape[0]

    def body(o_ref, z_ref, w_ref, _, m_ref):
        o = o_ref[...]
        z = z_ref[...]
        m_ref[...] = (o * _rms(o) * w_ref[...] * (z * _sigmoid(z))).astype(BF16)

    return pl.pallas_call(
        body, grid=(nh,),
        in_specs=[pl.BlockSpec((lp, LANES), lambda s: (0, s)), pl.BlockSpec((lp, LANES), lambda s: (0, 3 * nh + s)),
                  pl.BlockSpec((1, LANES), lambda s: (0, 0)), _ANY],
        out_specs=pl.BlockSpec((lp, LANES), lambda s: (0, s)),
        out_shape=jax.ShapeDtypeStruct(merged.shape, BF16), input_output_aliases={3: 0}, name="merge_gdn",
        compiler_params=_cp("parallel"))(o_gdn, proj, norm_w, merged)


def _merge_gdn_bwd(o_gdn, proj, norm_w, dmerged, nh):
    lp = o_gdn.shape[0]

    def body(o_ref, z_ref, w_ref, dm_ref, do_ref, dz_ref, dw_ref):
        o = o_ref[...]
        r = _rms(o)
        xh = o * r
        silu, dsilu = _silu_and_grad(z_ref[...])
        dm = dm_ref[...]
        dn = dm * silu
        dz_ref[...] = (dm * (xh * w_ref[...]) * dsilu).astype(BF16)
        dnw = dn * w_ref[...]
        do_ref[...] = r * (dnw - xh * jnp.mean(dnw * xh, axis=-1, keepdims=True))

        @pl.when(pl.program_id(0) == 0)
        def _():
            dw_ref[...] = jnp.zeros_like(dw_ref)
        dw_ref[...] += jnp.sum(dn * xh, axis=0, keepdims=True)

    w = nh * HEAD_DIM
    return pl.pallas_call(
        body, grid=(nh,),
        in_specs=[pl.BlockSpec((lp, LANES), lambda s: (0, s)), pl.BlockSpec((lp, LANES), lambda s: (0, 3 * nh + s)),
                  pl.BlockSpec((1, LANES), lambda s: (0, 0)), pl.BlockSpec((lp, LANES), lambda s: (0, s))],
        out_specs=[pl.BlockSpec((lp, LANES), lambda s: (0, s)), pl.BlockSpec((lp, LANES), lambda s: (0, 3 * nh + s)),
                   pl.BlockSpec((1, LANES), lambda s: (0, 0))],
        out_shape=[jax.ShapeDtypeStruct((lp, w), F32), jax.ShapeDtypeStruct((lp, 4 * w + NARROW), BF16),
                   jax.ShapeDtypeStruct((1, LANES), F32)],
        name="merge_gdn_bwd", compiler_params=_cp("arbitrary"))(o_gdn, proj, norm_w, dmerged)


def _fox_prep(proj, qk_w, nh):
    lp = proj.shape[0]

    def body(x_ref, w_ref, o_ref):
        x = x_ref[...]
        o_ref[...] = x * _rms(x) * w_ref[0]

    return pl.pallas_call(
        body, grid=(2 * nh,),
        in_specs=[pl.BlockSpec((lp, LANES), lambda s: (0, s)), pl.BlockSpec((1, 1, LANES), lambda s: (s // nh, 0, 0))],
        out_specs=pl.BlockSpec((lp, LANES), lambda s: (0, s)),
        out_shape=jax.ShapeDtypeStruct((lp, 2 * nh * HEAD_DIM), F32), name="fox_prep",
        compiler_params=_cp("parallel"))(proj, qk_w)


def _fox_prep_bwd(proj, qk_w, dq, dk, dproj, nh):
    lp = proj.shape[0]
    part = lambda p: pl.BlockSpec((lp, LANES), lambda s: (0, jnp.clip(s - p * nh, 0, nh - 1)))

    def body(x_ref, w_ref, dq_ref, dk_ref, _, dx_ref, dw_ref):
        x = x_ref[...]
        r = _rms(x)
        xh = x * r
        dy = jnp.where(pl.program_id(0) < nh, dq_ref[...], dk_ref[...])
        dyw = dy * w_ref[0]
        dx_ref[...] = (r * (dyw - xh * jnp.mean(dyw * xh, axis=-1, keepdims=True))).astype(BF16)

        @pl.when(pl.program_id(0) % nh == 0)
        def _():
            dw_ref[...] = jnp.zeros_like(dw_ref)
        dw_ref[0] += jnp.sum(dy * xh, axis=0, keepdims=True)

    strip = pl.BlockSpec((lp, LANES), lambda s: (0, s))
    wsp = pl.BlockSpec((1, 1, LANES), lambda s: (s // nh, 0, 0))
    return pl.pallas_call(
        body, grid=(2 * nh,), in_specs=[strip, wsp, part(0), part(1), _ANY], out_specs=[strip, wsp],
        out_shape=[jax.ShapeDtypeStruct(dproj.shape, BF16), jax.ShapeDtypeStruct((2, 1, LANES), F32)],
        input_output_aliases={4: 0}, name="fox_prep_bwd", compiler_params=_cp("arbitrary"))(proj, qk_w, dq, dk, dproj)


def _fox_probs(q, k, gates, crow, h, i, nh):
    kl = k.shape[0]
    lane = _iota((Q_BLOCK, LANES), 1)
    ct = jnp.sum(jnp.where(lane == 4 * nh + h, gates, 0.0), axis=1, keepdims=True)
    s = _bdot(q, k, "nt") * (HEAD_DIM ** -0.5) + (ct - crow)
    t = i * Q_BLOCK + _iota((Q_BLOCK, kl), 0)
    kp = _iota((Q_BLOCK, kl), 1)
    s = jnp.where((kp <= t) & ((kp >= PAD_ROWS) | (t < PAD_ROWS)), s, NEG)
    p = jnp.exp(s - jnp.max(s, axis=1, keepdims=True))
    return p / jnp.sum(p, axis=1, keepdims=True)


FOX_HEADS_PER_STEP = 2


def _fox_specs(lp, nh):
    hw = FOX_HEADS_PER_STEP * LANES
    return [pl.BlockSpec((Q_BLOCK, hw), lambda g, i: (i, g)),
            pl.BlockSpec((lp, hw), lambda g, i: (0, nh // FOX_HEADS_PER_STEP + g)),
            pl.BlockSpec((lp, hw), lambda g, i: (0, 2 * nh // FOX_HEADS_PER_STEP + g)),
            pl.BlockSpec((Q_BLOCK, LANES), lambda g, i: (i, 0)),
            pl.BlockSpec((LANES, lp), lambda g, i: (0, 0))]


def _fox_fwd(qkn, proj, gates, gtf, nh, rider=None):
    lp = qkn.shape[0]
    ng, nq = nh // FOX_HEADS_PER_STEP, lp // Q_BLOCK

    def body(hooks, q_ref, k_ref, v_ref, g_ref, gt_ref, o_ref):
        g, i = pl.program_id(0), pl.program_id(1)
        _when((g == 0) & (i == 0), hooks[0])
        _when((g == ng // 2) & (i == 0), hooks[1])
        for j in range(lp // Q_BLOCK):
            @pl.when(i == j)
            def _(j=j):
                kl = (j + 1) * Q_BLOCK
                for hh in range(FOX_HEADS_PER_STEP):
                    h = FOX_HEADS_PER_STEP * g + hh
                    sl = slice(hh * LANES, (hh + 1) * LANES)
                    p = _fox_probs(q_ref[:, sl], k_ref[0:kl, sl], g_ref[...], gt_ref[pl.ds(4 * nh + h, 1), :][:, 0:kl],
                                   h, j, nh)
                    o_ref[:, sl] = _bdot(p, v_ref[0:kl, sl])
        _when((g == ng - 1) & (i == nq - 1), hooks[2])

    call = _rider_call(
        body, rider, 5, 1, 0, grid=(ng, nq), in_specs=_fox_specs(lp, nh),
        out_specs=[pl.BlockSpec((Q_BLOCK, FOX_HEADS_PER_STEP * LANES), lambda g, i: (i, g))],
        out_shape=[jax.ShapeDtypeStruct((lp, nh * HEAD_DIM), F32)], name="fox_fwd",
        compiler_params=_cp("arbitrary", "arbitrary"))
    (o,), rest = call(qkn, qkn, proj, gates, gtf)
    return o, rest


def _fox_bwd(qkn, proj, gates, gtf, do, dproj, nh, rider=None):
    lp = qkn.shape[0]
    ng, nq = nh // FOX_HEADS_PER_STEP, lp // Q_BLOCK
    w = nh * HEAD_DIM
    scale = HEAD_DIM ** -0.5

    def body(hooks, q_ref, k_ref, v_ref, g_ref, gt_ref, do_ref, _, dq_ref, dk_ref, dc_ref, dv_ref, dv_scr):
        g, i = pl.program_id(0), pl.program_id(1)
        _when((g == 0) & (i == 0), hooks[0])

        @pl.when(i == 0)
        def _():
            dk_ref[...] = jnp.zeros_like(dk_ref)
            dv_scr[...] = jnp.zeros_like(dv_scr)
            dc_ref[...] = jnp.zeros_like(dc_ref)
        for j in range(nq):
            @pl.when(i == j)
            def _(j=j):
                kl = (j + 1) * Q_BLOCK
                for hh in range(FOX_HEADS_PER_STEP):
                    h = FOX_HEADS_PER_STEP * g + hh
                    sl = slice(hh * LANES, (hh + 1) * LANES)
                    q, k = q_ref[:, sl], k_ref[0:kl, sl]
                    p = _fox_probs(q, k, g_ref[...], gt_ref[pl.ds(4 * nh + h, 1), :][:, 0:kl], h, j, nh)
                    dout = do_ref[:, sl]
                    dp = _bdot(dout, v_ref[0:kl, sl], "nt")
                    ds = p * (dp - jnp.sum(p * dp, axis=1, keepdims=True))
                    dq_ref[:, sl] = _bdot(ds, k) * scale
                    dk_ref[0:kl, sl] += _bdot(ds, q, "tn") * scale
                    dv_scr[0:kl, sl] += _bdot(p, dout, "tn")
                    dc_ref[hh, :, 0:kl] -= jnp.sum(ds, axis=0, keepdims=True)

        @pl.when(i == nq - 1)
        def _():
            dv_ref[...] = dv_scr[...].astype(BF16)
        _when((g == ng - 1) & (i == nq - 1), hooks[2])

    hw = FOX_HEADS_PER_STEP * LANES
    blk = pl.BlockSpec((Q_BLOCK, hw), lambda g, i: (i, g))
    col = pl.BlockSpec((lp, hw), lambda g, i: (0, g))
    call = _rider_call(
        body, rider, 7, 4, 1, grid=(ng, nq), in_specs=_fox_specs(lp, nh) + [blk, _ANY],
        out_specs=[blk, col, pl.BlockSpec((FOX_HEADS_PER_STEP, 1, lp), lambda g, i: (g, 0, 0)),
                   pl.BlockSpec((lp, hw), lambda g, i: (0, 2 * nh // FOX_HEADS_PER_STEP + g))],
        out_shape=[jax.ShapeDtypeStruct((lp, w), F32)] * 2 + [jax.ShapeDtypeStruct((nh, 1, lp), F32),
                                                             jax.ShapeDtypeStruct(dproj.shape, BF16)],
        scratch_shapes=[pltpu.VMEM((lp, hw), F32)], aliases={6: 3},
        name="fox_bwd", compiler_params=_cp("arbitrary", "arbitrary"))
    return call(qkn, qkn, proj, gates, gtf, do, dproj)


def _merge_fox(o_fox, proj, nh):
    lp = o_fox.shape[0]

    def body(o_ref, z_ref, m_ref):
        z = z_ref[...]
        m_ref[...] = (o_ref[...] * (z * _sigmoid(z))).astype(BF16)

    return pl.pallas_call(
        body, grid=(nh,),
        in_specs=[pl.BlockSpec((lp, LANES), lambda s: (0, s)), pl.BlockSpec((lp, LANES), lambda s: (0, 3 * nh + s))],
        out_specs=pl.BlockSpec((lp, LANES), lambda s: (0, nh + s)),
        out_shape=jax.ShapeDtypeStruct((lp, 2 * nh * HEAD_DIM), BF16), name="merge_fox",
        compiler_params=_cp("parallel"))(o_fox, proj)


def _merge_fox_bwd(o_fox, proj, dmerged, nh):
    lp = o_fox.shape[0]

    def body(o_ref, z_ref, dm_ref, do_ref, dz_ref):
        silu, dsilu = _silu_and_grad(z_ref[...])
        dm = dm_ref[...]
        do_ref[...] = dm * silu
        dz_ref[...] = (dm * o_ref[...] * dsilu).astype(BF16)

    w = nh * HEAD_DIM
    return pl.pallas_call(
        body, grid=(nh,),
        in_specs=[pl.BlockSpec((lp, LANES), lambda s: (0, s)), pl.BlockSpec((lp, LANES), lambda s: (0, 3 * nh + s)),
                  pl.BlockSpec((lp, LANES), lambda s: (0, nh + s))],
        out_specs=[pl.BlockSpec((lp, LANES), lambda s: (0, s)), pl.BlockSpec((lp, LANES), lambda s: (0, 3 * nh + s))],
        out_shape=[jax.ShapeDtypeStruct((lp, w), F32), jax.ShapeDtypeStruct((lp, 4 * w + NARROW), BF16)],
        name="merge_fox_bwd", compiler_params=_cp("parallel"))(o_fox, proj, dmerged)


def _post(out, x, target, post_w):
    lp, d = out.shape

    def body(o_ref, x_ref, t_ref, w_ref, dy_ref, do_ref, loss_ref, dw_ref):
        i = pl.program_id(0)

        @pl.when(i == 0)
        def _():
            loss_ref[...] = jnp.zeros_like(loss_ref)
            dw_ref[...] = jnp.zeros_like(dw_ref)
        o = o_ref[...]
        r = _rms(o)
        nrm = o * r
        err = jnp.where(i > 0, x_ref[...] + nrm * w_ref[...] - t_ref[...], 0.0)
        loss_ref[0:1, :] += 0.5 * jnp.sum(jnp.sum(err * err, axis=1, keepdims=True), axis=0, keepdims=True) / d
        dy = err / d
        dy_ref[...] = dy
        dw_ref[...] += jnp.sum(dy * nrm, axis=0, keepdims=True)
        dyw = dy * w_ref[...]
        do_ref[...] = (r * (dyw - nrm * jnp.mean(dyw * nrm, axis=-1, keepdims=True))).astype(BF16)

    row = pl.BlockSpec((Q_BLOCK, d), lambda i: (i, 0))
    vec = pl.BlockSpec((1, d), lambda i: (0, 0))
    return pl.pallas_call(
        body, grid=(lp // Q_BLOCK,), in_specs=[row, _x_rows(d), _x_rows(d), vec],
        out_specs=[_x_rows(d), row, pl.BlockSpec((8, LANES), lambda i: (0, 0)), vec],
        out_shape=[jax.ShapeDtypeStruct(x.shape, F32), jax.ShapeDtypeStruct((lp, d), BF16),
                   jax.ShapeDtypeStruct((8, LANES), F32), jax.ShapeDtypeStruct((1, d), F32)],
        name="post", compiler_params=_cp("arbitrary"))(out, x, target, post_w)


def _prenorm_bwd(dxn, x, meta, w, dy):
    seq, d = x.shape
    lp = seq + Q_BLOCK

    def body(dx_ref, x_ref, m_ref, w_ref, dy_ref, gx_ref, gm_ref, dw_ref):
        i = pl.program_id(0)
        h = _h_tile(i, x_ref, m_ref)
        r = _rms(h)
        xh = h * r
        dxn_ = dx_ref[...]
        dxw = dxn_ * w_ref[...]
        dh = jnp.where(i > 0, dy_ref[...], 0.0) + r * (dxw - xh * jnp.mean(dxw * xh, axis=-1, keepdims=True))
        gx_ref[...] = dh

        @pl.when(i == 0)
        def _():
            dw_ref[...] = jnp.zeros_like(dw_ref)
            gm_ref[...] = dh[PAD_ROWS:, :]
        dw_ref[...] += jnp.sum(dxn_ * xh, axis=0, keepdims=True)

    vec = pl.BlockSpec((1, d), lambda i: (0, 0))
    met = pl.BlockSpec((N_META, d), lambda i: (0, 0))
    return pl.pallas_call(
        body, grid=(lp // Q_BLOCK,),
        in_specs=[pl.BlockSpec((Q_BLOCK, d), lambda i: (i, 0)), _x_rows(d), met, vec, _x_rows(d)],
        out_specs=[_x_rows(d), met, vec],
        out_shape=[jax.ShapeDtypeStruct((seq, d), F32), jax.ShapeDtypeStruct((N_META, d), F32),
                   jax.ShapeDtypeStruct((1, d), F32)],
        name="prenorm_bwd", compiler_params=_cp("arbitrary"))(dxn, x, meta, w, dy)


def _layer_grads(x, target, pre_w, a_log, dt_bias, gdn_norm_w, fq_w, fk_w, f_bias, post_w, comm):
    nh = a_log.shape[1]
    lp = x.shape[0] + Q_BLOCK
    zpad = jnp.zeros((1, LANES - 3 * nh), F32)
    bias_row = jnp.concatenate([jnp.zeros((1, nh), F32), dt_bias, f_bias, zpad], axis=1)
    nega_row = jnp.concatenate([jnp.zeros((1, nh), F32), -jnp.exp(a_log), jnp.zeros((1, nh), F32), zpad], axis=1)
    qk_w = jnp.stack([fq_w, fk_w])

    w_f, conv_w, meta = comm.fox_weights()
    conv_wt = conv_w.T
    xn = _prenorm(x, meta, pre_w)
    proj_f = _matmul(xn, w_f, "nn", MM_TILE, F32, "proj_fox")
    gates_f, _, gtf_f = _gates(proj_f, bias_row, nega_row, nh)
    qkn = _fox_prep(proj_f, qk_w, nh)
    o_fox, got = _fox_fwd(qkn, proj_f, gates_f, gtf_f, nh, comm.gdn_weights_rider())
    merged = _merge_fox(o_fox, proj_f, nh)
    w_g, w_out = comm.gdn_weights(got)
    proj_g = _matmul(xn, w_g, "nn", MM_TILE, F32, "proj_gdn")
    qkv = _gdn_prep(proj_g, conv_wt, nh)
    gates, gt3, gtf = _gates(proj_g, bias_row, nega_row, nh)
    o_gdn, s_all = _gdn_fwd(qkv, gates, gt3, nh)
    merged = _merge_gdn(o_gdn, proj_g, gdn_norm_w, merged, nh)
    out = _matmul(merged, w_out, "nn", 4 * LANES, F32, "out_proj")
    dy, dout, loss_blk, dpost_w = _post(out, x, target, post_w)

    dw_out = _matmul(merged, dout, "tn", 4 * LANES, BF16, "dw_out")
    dmerged = _matmul(dout, w_out, "nt", 4 * LANES, F32, "dmerged")
    do_gdn, dproj_g, dgdn_norm_w = _merge_gdn_bwd(o_gdn, proj_g, gdn_norm_w, dmerged, nh)
    dgq, dgk, dgv, dgate = _gdn_bwd(qkv, gates, gt3, s_all, do_gdn, nh)
    dproj_g, dconv_wt = _gdn_prep_bwd(proj_g, conv_wt, dgq, dgk, dgv, dproj_g, nh)
    dproj_g, _ = _gates_bwd(proj_g, bias_row, nega_row, gates, dgate, jnp.zeros((LANES, lp), F32), dproj_g, nh)
    dw_g = _matmul(xn, dproj_g, "tn", MM_TILE, BF16, "dw_in_gdn")
    rider = comm.gdn_grads(dw_g, dw_out)

    do_fox, dproj_f = _merge_fox_bwd(o_fox, proj_f, dmerged, nh)
    (dqn, dkn, dc_t, dproj_f), got = _fox_bwd(qkn, proj_f, gates, gtf, do_fox, dproj_f, nh, rider)
    dproj_f, dqk_w = _fox_prep_bwd(proj_f, qk_w, dqn, dkn, dproj_f, nh)
    dc_rows = jnp.pad(dc_t.reshape(nh, -1), ((2 * nh, LANES - 3 * nh), (0, 0)))
    dproj_f, gate_sums = _gates_bwd(proj_g, bias_row, nega_row, gates, dgate, dc_rows, dproj_f, nh)
    dw_f = _matmul(xn, dproj_f, "tn", MM_TILE, BF16, "dw_in_fox")
    rider = comm.fox_grads(got, dw_f, dconv_wt.T)
    (dxn,), got = _dxn(dproj_g, dproj_f, w_g, w_f, nh, rider, "dxn")
    grad_x, dmeta, dpre_w = _prenorm_bwd(dxn, x, meta, pre_w, dy)
    return dict(
        loss=loss_blk[0:1, 0:1], grad_x=grad_x, meta=dmeta, pre_w=dpre_w, post_w=dpost_w,
        a_log=gate_sums[1:2, nh:2 * nh], dt_bias=gate_sums[0:1, nh:2 * nh], gdn_norm_w=dgdn_norm_w,
        fq_w=dqk_w[0], fk_w=dqk_w[1], f_bias=gate_sums[0:1, 2 * nh:3 * nh], got=got)


def _cast_bf16(a, tr, name):
    r, c = a.shape

    def body(a_ref, o_ref):
        o_ref[...] = a_ref[...].astype(BF16)

    return pl.pallas_call(
        body, grid=(r // tr,), in_specs=[pl.BlockSpec((tr, c), lambda i: (i, 0))],
        out_specs=pl.BlockSpec((tr, c), lambda i: (i, 0)), out_shape=jax.ShapeDtypeStruct((r, c), BF16),
        name=name, compiler_params=_cp("parallel"))(a)


def _gather_rider(arrays, owner_x, into=None):
    n = len(arrays)
    into = into or [None] * n
    extra = [t for t in into if t is not None]
    aliases = {}
    for a in range(n):
        if into[a] is not None:
            aliases[n + len(aliases)] = a

    def make(in_refs, outs, scratch):
        ins = in_refs[:n]
        send_sems, recv_sems, local_sems = scratch
        x, y, c = lax.axis_index("x"), lax.axis_index("y"), lax.axis_index("c")
        me, sibling = (x, y, c), (x, y, 1 - c)
        chips = [(1 - x, y), (x, 1 - y), (1 - x, 1 - y)]
        owner = lambda a, px: None if owner_x[a] is None else px == owner_x[a]

        def copy(a, k, block, to, src=None):
            px, py, pc = block
            rows = outs[a].at[4 * px + 2 * py + pc]
            return pltpu.make_async_remote_copy(
                src_ref=rows if src is None else src, dst_ref=rows, send_sem=send_sems.at[a, k],
                recv_sem=recv_sems.at[a, k], device_id=to, device_id_type=_MESH)

        def mine(a):
            return ([pltpu.make_async_copy(ins[a], outs[a].at[4 * x + 2 * y + c], local_sems.at[a])],
                    [copy(a, 0, me, sibling, src=ins[a])] + [copy(a, 1 + j, me, (*chip, c), src=ins[a]) for j, chip in enumerate(chips)])

        def start():
            for a in range(n):
                local, sends = mine(a)
                for cp in local + sends:
                    _when(owner(a, x), cp.start)

        def middle():
            for j, chip in enumerate(chips):
                for a in range(n):
                    def pass_on(a=a, j=j, chip=chip):
                        copy(a, 1 + j, (*chip, c), me).wait_recv()
                        copy(a, 4 + j, (*chip, c), sibling).start()
                    _when(owner(a, chip[0]), pass_on)

        def finish():
            for a in range(n):
                def own(a=a):
                    local, sends = mine(a)
                    copy(a, 0, sibling, me).wait_recv()
                    for cp in sends:
                        cp.wait_send()
                    local[0].wait()
                _when(owner(a, x), own)
                for j, chip in enumerate(chips):
                    def passed(a=a, j=j, chip=chip):
                        copy(a, 4 + j, (*chip, 1 - c), me).wait_recv()
                        copy(a, 4 + j, (*chip, c), sibling).wait_send()
                    _when(owner(a, chip[0]), passed)

        return start, middle, finish

    out_shapes = [jax.ShapeDtypeStruct((N_DEV,) + a.shape, a.dtype) for a in arrays]
    scratch = [pltpu.SemaphoreType.DMA((n, N_DEV - 1)), pltpu.SemaphoreType.DMA((n, N_DEV - 1)), pltpu.SemaphoreType.DMA((n,))]
    return _Rider(list(arrays) + extra, out_shapes, scratch, aliases, make)


def _pair_rider(arrays):
    n = len(arrays)
    nq = max(a.shape[0] // 2 for a in arrays)

    def make(ins, outs, scratch):
        send_sems, recv_sems = scratch
        x, y, c = lax.axis_index("x"), lax.axis_index("y"), lax.axis_index("c")
        copies = [pltpu.make_async_remote_copy(
            src_ref=ins[a].at[2 * q + 1 - c], dst_ref=outs[a].at[q], send_sem=send_sems.at[a, q],
            recv_sem=recv_sems.at[a, q], device_id=(x, y, 1 - c), device_id_type=_MESH)
            for a in range(n) for q in range(arrays[a].shape[0] // 2)]

        def start():
            for cp in copies:
                cp.start()

        def finish():
            for cp in copies:
                cp.wait()

        return start, (lambda: None), finish

    out_shapes = [jax.ShapeDtypeStruct((a.shape[0] // 2,) + a.shape[1:], a.dtype) for a in arrays]
    return _Rider(list(arrays), out_shapes, [pltpu.SemaphoreType.DMA((n, nq)), pltpu.SemaphoreType.DMA((n, nq))], {}, make)


def _comm_call(riders, name):
    n_in = [len(r.inputs) for r in riders]
    n_out = [len(r.out_shapes) for r in riders]
    n_scr = [len(r.scratch) for r in riders]
    aliases = {}
    for k, r in enumerate(riders):
        aliases.update({sum(n_in[:k]) + i: sum(n_out[:k]) + o for i, o in r.aliases.items()})

    def body(*refs):
        ins, outs, scr = refs[:sum(n_in)], refs[sum(n_in):sum(n_in) + sum(n_out)], refs[sum(n_in) + sum(n_out):]
        hooks = [r.make(ins[sum(n_in[:k]):sum(n_in[:k + 1])], outs[sum(n_out[:k]):sum(n_out[:k + 1])],
                        scr[sum(n_scr[:k]):sum(n_scr[:k + 1])]) for k, r in enumerate(riders)]
        for phase in range(3):
            for h in hooks:
                h[phase]()

    res = pl.pallas_call(
        body, in_specs=[_ANY] * sum(n_in), out_specs=[_ANY] * sum(n_out),
        out_shape=[s for r in riders for s in r.out_shapes], scratch_shapes=[s for r in riders for s in r.scratch],
        input_output_aliases=aliases, name=name)(*[t for r in riders for t in r.inputs])
    return [res[sum(n_out[:k]):sum(n_out[:k + 1])] for k in range(len(riders))]


def _pair_sum(parts, got, core, tr, name):
    nq, r, c = got.shape

    def body(core_ref, p_ref, g_ref, o_ref):
        o_ref[...] = (p_ref[...].astype(F32) + g_ref[...].astype(F32)).astype(o_ref.dtype)

    return pl.pallas_call(
        body,
        grid_spec=pltpu.PrefetchScalarGridSpec(
            num_scalar_prefetch=1, grid=(nq, r // tr),
            in_specs=[pl.BlockSpec((1, tr, c), lambda q, i, core_ref: (2 * q + core_ref[0], i, 0)),
                      pl.BlockSpec((1, tr, c), lambda q, i, core_ref: (q, i, 0))],
            out_specs=pl.BlockSpec((1, tr, c), lambda q, i, core_ref: (q, i, 0))),
        out_shape=jax.ShapeDtypeStruct((nq, r, c), parts.dtype), name=name,
        compiler_params=_cp("parallel", "parallel"))(core, parts, got)


def _native_segments(nh, half):
    w4 = 4 * nh * HEAD_DIM
    if half == "g":
        return [(0, w4, 0), (w4, w4 + 2 * nh, w4)]
    return [(w4, w4 + 2 * nh, w4), (w4 + 2 * nh, 2 * w4 + 2 * nh, 0), (2 * w4 + 2 * nh, 2 * w4 + 3 * nh, w4 + 2 * nh)]


def _relayout_w_in(wg, nh, tr, half):
    _, d, cols = wg.shape
    w4 = 4 * nh * HEAD_DIM
    first = 0 if half == "g" else N_DEV // 2

    def native(ref, j0, j1):
        out = []
        while j0 < j1:
            blk = j0 // cols
            end = min(j1, (blk + 1) * cols)
            out.append(ref[blk - first, :, pl.ds(j0 - blk * cols, end - j0)] if blk >= first
                       else jnp.zeros((tr, end - j0), wg.dtype))
            j0 = end
        return out

    def body(g_ref, o_ref):
        base = 0 if half == "g" else w4 + 2 * nh
        for cidx in range(w4 // LANES):
            pieces = native(g_ref, base + cidx * LANES, base + (cidx + 1) * LANES)
            o_ref[:, cidx * LANES:(cidx + 1) * LANES] = pieces[0] if len(pieces) == 1 else jnp.concatenate(pieces, axis=1)
        pieces = (native(g_ref, w4, w4 + 2 * nh) + native(g_ref, 2 * w4 + 2 * nh, 2 * w4 + 3 * nh)
                  + [jnp.zeros((tr, NARROW - 3 * nh), wg.dtype)])
        o_ref[:, w4:] = jnp.concatenate(pieces, axis=1)

    view = N_DEV - first
    return pl.pallas_call(
        body, grid=(d // tr,), in_specs=[pl.BlockSpec((view, tr, cols), lambda i: (first // view, i, 0))],
        out_specs=pl.BlockSpec((tr, w4 + NARROW), lambda i: (i, 0)),
        out_shape=jax.ShapeDtypeStruct((d, w4 + NARROW), wg.dtype),
        name="relayout_w_in_" + half, compiler_params=_cp("parallel"))(wg)


def _relayout_dw_in(dwhalf, nh, tr, half):
    d = dwhalf.shape[0]
    w4 = 4 * nh * HEAD_DIM
    cols = (2 * w4 + 3 * nh) // N_DEV
    segs = _native_segments(nh, half)
    first = 0 if half == "g" else N_DEV // 2

    def body(f_ref, o_ref):
        for b in range(N_DEV // 2):
            blk = first + b
            pieces = []
            for s0, s1, t0 in segs:
                lo, hi = max(s0, blk * cols), min(s1, (blk + 1) * cols)
                if lo < hi:
                    pieces.append(f_ref[:, pl.ds(t0 + lo - s0, hi - lo)])
            o_ref[b] = pieces[0] if len(pieces) == 1 else jnp.concatenate(pieces, axis=1)

    return pl.pallas_call(
        body, grid=(d // tr,), in_specs=[pl.BlockSpec((tr, w4 + NARROW), lambda i: (i, 0))],
        out_specs=pl.BlockSpec((N_DEV // 2, tr, cols), lambda i: (0, i, 0)),
        out_shape=jax.ShapeDtypeStruct((N_DEV // 2, d, cols), dwhalf.dtype),
        name="relayout_dw_in_" + half, compiler_params=_cp("parallel"))(dwhalf)


def _adamw(w, parts, m, v, tr, name):
    r, c = w.shape
    n_parts = parts.shape[0]

    def body(w_ref, p_ref, m_ref, v_ref, g_ref, d_ref, nm_ref, nv_ref):
        g = p_ref[0].astype(F32)
        for s in range(1, n_parts):
            g = g + p_ref[s].astype(F32)
        m_new = ADAM_B1 * m_ref[...] + (1.0 - ADAM_B1) * g
        v_new = ADAM_B2 * v_ref[...] + (1.0 - ADAM_B2) * (g * g)
        m_hat = m_new / (1.0 - ADAM_B1 ** ADAM_STEP)
        v_hat = v_new / (1.0 - ADAM_B2 ** ADAM_STEP)
        g_ref[...] = g
        d_ref[...] = -ADAM_LR * (m_hat / (jnp.sqrt(v_hat) + ADAM_EPS) + ADAM_WD * w_ref[...])
        nm_ref[...] = m_new
        nv_ref[...] = v_new

    blk = pl.BlockSpec((tr, c), lambda i: (i, 0))
    return pl.pallas_call(
        body, grid=(r // tr,), in_specs=[blk, pl.BlockSpec((n_parts, tr, c), lambda i: (0, i, 0)), blk, blk],
        out_specs=[blk] * 4, out_shape=[jax.ShapeDtypeStruct((r, c), F32)] * 4, name=name,
        compiler_params=_cp("parallel"))(w, parts, m, v)


def _pack_small(d, pre, post, a_log, dt_bias, f_bias, gdn_w, fq_w, fk_w, extra):
    row2 = jnp.concatenate([a_log, dt_bias, f_bias, gdn_w, fq_w, fk_w, extra], axis=1)
    row2 = jnp.pad(row2, ((0, 0), (0, d - row2.shape[1])))
    return jnp.concatenate([pre, post, row2, jnp.zeros((5, d), F32)], axis=0)


def _unpack_small(p, nh):
    o = 3 * nh
    return dict(pre=p[0:1], post=p[1:2], a_log=p[2:3, 0:nh], dt_bias=p[2:3, nh:2 * nh], f_bias=p[2:3, 2 * nh:o],
                gdn_w=p[2:3, o:o + HEAD_DIM], fq_w=p[2:3, o + HEAD_DIM:o + 2 * HEAD_DIM],
                fk_w=p[2:3, o + 2 * HEAD_DIM:o + 3 * HEAD_DIM], extra=p[2, o + 3 * HEAD_DIM])


def kernel(x, meta_tokens, pre_norm_w, w_in, conv_w, a_log, dt_bias, gdn_norm_w, fox_q_norm_w, fox_k_norm_w, fox_f_bias, w_out, post_norm_w, loss_target, m_meta_tokens, m_pre_norm_w, m_w_in, m_conv_w, m_a_log, m_dt_bias, m_gdn_norm_w, m_fox_q_norm_w, m_fox_k_norm_w, m_fox_f_bias, m_w_out, m_post_norm_w, v_meta_tokens, v_pre_norm_w, v_w_in, v_conv_w, v_a_log, v_dt_bias, v_gdn_norm_w, v_fox_q_norm_w, v_fox_k_norm_w, v_fox_f_bias, v_w_out, v_post_norm_w):
    nh = a_log.shape[1]
    d = x.shape[-1]
    w = nh * HEAD_DIM
    zero = jnp.zeros((1, 1), F32)
    core = lax.axis_index("c")
    dev = 4 * lax.axis_index("x") + 2 * lax.axis_index("y") + core
    core_arr = jnp.reshape(core, (1,)).astype(jnp.int32)
    w_in_b = _cast_bf16(w_in[0], 256, "cast_w_in")

    class MeshComm:
        def fox_weights(self):
            (self.wg, cg, mg), = _comm_call([_gather_rider([w_in_b, conv_w[0], meta_tokens], [1, None, None])],
                                            "gather_fox_weights")
            return (_relayout_w_in(self.wg, nh, 256, "f"), cg.reshape(3 * w, CONV_WIDTH),
                    mg.transpose(1, 0, 2).reshape(N_META, d))

        def gdn_weights_rider(self):
            return _gather_rider([w_in_b, _cast_bf16(w_out[0], 256, "cast_w_out")], [0, None], into=[self.wg, None])

        def gdn_weights(self, got):
            wg, wog = got
            return _relayout_w_in(wg, nh, 256, "g"), wog.reshape(2 * w, d)

        def gdn_grads(self, dw_g, dw_out):
            mine_g = _relayout_dw_in(dw_g, nh, 128, "g")
            mine_out = dw_out.reshape(N_DEV, 2 * w // N_DEV, d)
            (got_g, got_out), = _comm_call([_pair_rider([mine_g, mine_out])], "pair_exchange_gdn")
            return _chip_rider([_pair_sum(mine_g, got_g, core_arr, 128, "pair_sum_w_in_gdn"),
                                _pair_sum(mine_out, got_out, core_arr, 256, "pair_sum_w_out")], [0, None])

        def fox_grads(self, got, dw_f, dconv_w):
            p_in_gdn, self.p_out = got
            mine_f = _relayout_dw_in(dw_f, nh, 128, "f")
            (got_f,), (self.a_conv,) = _comm_call([_pair_rider([mine_f]), _gather_rider([dconv_w], [None])],
                                                  "pair_exchange_fox")
            return _chip_rider([_pair_sum(mine_f, got_f, core_arr, 128, "pair_sum_w_in_fox")], [1], into=[p_in_gdn])

    comm = MeshComm()
    g = _layer_grads(x[0], loss_target[0], pre_norm_w, a_log, dt_bias, gdn_norm_w, fox_q_norm_w, fox_k_norm_w, fox_f_bias,
                     post_norm_w, comm)
    (p_in,), p_out = g["got"], comm.p_out
    small = _pack_small(d, g["pre_w"], g["post_w"], g["a_log"], g["dt_bias"], g["f_bias"], g["gdn_norm_w"], g["fq_w"],
                        g["fk_w"], g["loss"])
    (a_meta, p_small), = _comm_call([_gather_rider([g["meta"], small], [None, None])], "gather_small_grads")
    p_conv = lax.dynamic_slice_in_dim(comm.a_conv, dev * conv_w.shape[1], conv_w.shape[1], axis=1)
    p_meta = lax.dynamic_slice_in_dim(a_meta, dev * meta_tokens.shape[1], meta_tokens.shape[1], axis=2)
    grad_x = g["grad_x"]

    r_in = _adamw(w_in[0], p_in, m_w_in[0], v_w_in[0], 128, "adamw_w_in")
    r_out = _adamw(w_out[0], p_out, m_w_out[0], v_w_out[0], 64, "adamw_w_out")
    r_conv = _adamw(conv_w[0], p_conv, m_conv_w[0], v_conv_w[0], conv_w.shape[1], "adamw_conv_w")
    r_meta = _adamw(meta_tokens, p_meta, m_meta_tokens, v_meta_tokens, N_META, "adamw_meta")
    pk = lambda pre, post, a, dt, gw, fq, fk, fb: _pack_small(d, pre, post, a, dt, fb, gw, fq, fk, zero)
    r_small = _adamw(
        pk(pre_norm_w, post_norm_w, a_log, dt_bias, gdn_norm_w, fox_q_norm_w, fox_k_norm_w, fox_f_bias), p_small,
        pk(m_pre_norm_w, m_post_norm_w, m_a_log, m_dt_bias, m_gdn_norm_w, m_fox_q_norm_w, m_fox_k_norm_w, m_fox_f_bias),
        pk(v_pre_norm_w, v_post_norm_w, v_a_log, v_dt_bias, v_gdn_norm_w, v_fox_q_norm_w, v_fox_k_norm_w, v_fox_f_bias),
        8, "adamw_small")

    sm = [_unpack_small(r, nh) for r in r_small]
    outs = []
    for i in range(4):
        s = sm[i]
        outs += [r_meta[i], s["pre"], r_in[i][None], r_conv[i][None], s["a_log"], s["dt_bias"], s["gdn_w"], s["fq_w"],
                 s["fk_w"], s["f_bias"], r_out[i][None], s["post"]]
    return (sm[0]["extra"], grad_x[None], *outs)
```

```python
import jax
import jax.numpy as jnp
from jax import lax
from jax.experimental import pallas as pl
from jax.experimental.pallas import tpu as pltpu

F32, BF16 = jnp.float32, jnp.bfloat16
HEAD_DIM = 128
N_META = 16
CONV_WIDTH = 4
CHUNK = 64
Q_BLOCK = 128
LANES = 128
EPS = 1e-6
PAD_ROWS = Q_BLOCK - N_META
N_DEV = 8
N_CHIP = 4
VMEM_LIMIT = 56 * 1024 * 1024
NEG = -1e30
NARROW = 2 * LANES
MM_TILE = 6 * LANES

ADAM_LR, ADAM_B1, ADAM_B2, ADAM_EPS, ADAM_WD, ADAM_STEP = 0.001, 0.9, 0.999, 1e-08, 0.01, 10

_DN = {"nn": (((1,), (0,)), ((), ())), "nt": (((1,), (1,)), ((), ())), "tn": (((0,), (0,)), ((), ()))}
_DN3 = {"nn": (((2,), (1,)), ((0,), (0,))), "nt": (((2,), (2,)), ((0,), (0,))), "tn": (((1,), (1,)), ((0,), (0,)))}
_ANY = pl.BlockSpec(memory_space=pl.ANY)
_MESH = pl.DeviceIdType.MESH


def _cp(*sem):
    return pltpu.CompilerParams(dimension_semantics=sem, vmem_limit_bytes=VMEM_LIMIT)


def _dot(a, b, dims="nn", prec=None):
    return lax.dot_general(a, b, _DN[dims], precision=prec, preferred_element_type=F32)


def _bdot(a, b, dims="nn"):
    return _dot(a.astype(BF16), b.astype(BF16), dims)


def _hdot(a, b, dims="nn"):
    return _dot(a, b, dims, prec=lax.Precision.HIGHEST)


def _dot3(a, b, dims="nn"):
    return lax.dot_general(a, b, _DN3[dims], preferred_element_type=F32)


def _bdot3(a, b, dims="nn"):
    return _dot3(a.astype(BF16), b.astype(BF16), dims)


def _split(a):
    hi = a.astype(BF16)
    return hi, (a - hi.astype(F32)).astype(BF16)


def _iota(shape, dim):
    return lax.broadcasted_iota(jnp.int32, shape, dim)


def _sigmoid(z):
    return 1.0 / (1.0 + jnp.exp(-z))


def _softplus(z):
    e = jnp.exp(-jnp.abs(z))
    u = 1.0 + e
    l1p = jnp.where(u == 1.0, e, jnp.log(u) * (e / jnp.where(u == 1.0, 1.0, u - 1.0)))
    return jnp.maximum(z, 0.0) + l1p


def _silu_and_grad(z):
    s = _sigmoid(z)
    return z * s, s * (1.0 + z * (1.0 - s))


def _rms(x):
    return lax.rsqrt(jnp.mean(x * x, axis=-1, keepdims=True) + EPS)


def _h_tile(i, x_ref, meta_ref):
    first = jnp.concatenate([jnp.zeros((PAD_ROWS, x_ref.shape[1]), F32), meta_ref[...]], axis=0)
    return jnp.where(i == 0, first, x_ref[...])


def _x_rows(d):
    return pl.BlockSpec((Q_BLOCK, d), lambda i: (jnp.maximum(i - 1, 0), 0))


def _prenorm(x, meta, w):
    seq, d = x.shape
    lp = seq + Q_BLOCK

    def body(x_ref, m_ref, w_ref, o_ref):
        h = _h_tile(pl.program_id(0), x_ref, m_ref)
        o_ref[...] = (h * _rms(h) * w_ref[...]).astype(BF16)

    return pl.pallas_call(
        body, grid=(lp // Q_BLOCK,),
        in_specs=[_x_rows(d), pl.BlockSpec((N_META, d), lambda i: (0, 0)), pl.BlockSpec((1, d), lambda i: (0, 0))],
        out_specs=pl.BlockSpec((Q_BLOCK, d), lambda i: (i, 0)),
        out_shape=jax.ShapeDtypeStruct((lp, d), BF16), name="prenorm", compiler_params=_cp("parallel"))(x, meta, w)


def _tile(n, want):
    return max(t for t in range(LANES, want + 1, LANES) if n % t == 0)


def _matmul(a, b, dims, tn, out_dtype, name):
    m = a.shape[1] if dims == "tn" else a.shape[0]
    n = b.shape[0] if dims == "nt" else b.shape[1]
    kdim = b.shape[1] if dims == "nt" else b.shape[0]
    tn = _tile(n, tn)
    b_spec = pl.BlockSpec((tn, kdim), lambda j: (j, 0)) if dims == "nt" else pl.BlockSpec((kdim, tn), lambda j: (0, j))

    def body(a_ref, b_ref, o_ref):
        o_ref[...] = _dot(a_ref[...], b_ref[...], dims).astype(out_dtype)

    return pl.pallas_call(
        body, grid=(n // tn,),
        in_specs=[pl.BlockSpec(a.shape, lambda j: (0, 0)), b_spec],
        out_specs=pl.BlockSpec((m, tn), lambda j: (0, j)),
        out_shape=jax.ShapeDtypeStruct((m, n), out_dtype), name=name, compiler_params=_cp("parallel"))(a, b)


def _chip_copies(ins, outs, send_sems, recv_sems, local_sems):
    if not ins:
        return [], [], []
    x, y, c = lax.axis_index("x"), lax.axis_index("y"), lax.axis_index("c")
    mine = 2 * x + y
    local, sends, recvs = [], [], []
    for a in range(len(ins)):
        local.append(pltpu.make_async_copy(ins[a].at[mine], outs[a].at[mine], local_sems.at[a]))
        for k in range(1, N_CHIP):
            px = 1 - x if k & 2 else x
            py = 1 - y if k & 1 else y
            kw = dict(send_sem=send_sems.at[a, k - 1], recv_sem=recv_sems.at[a, k - 1], device_id=(px, py, c),
                      device_id_type=_MESH)
            sends.append(pltpu.make_async_remote_copy(src_ref=ins[a].at[2 * px + py], dst_ref=outs[a].at[mine], **kw))
            recvs.append(pltpu.make_async_remote_copy(src_ref=ins[a].at[mine], dst_ref=outs[a].at[2 * px + py], **kw))
    return local, sends, recvs


def _dxn_and_chip_exchange(dproj, wfull, sums, tk, name):
    m, k = dproj.shape
    n = wfull.shape[0]
    na = len(sums)
    tk = _tile(k, tk)
    steps = k // tk

    def body(a_ref, b_ref, *rest):
        ins, o_ref, outs = rest[:na], rest[na], rest[na + 1:2 * na + 1]
        local, sends, recvs = _chip_copies(ins, outs, *rest[2 * na + 1:])
        j = pl.program_id(0)

        @pl.when(j == 0)
        def _():
            for cp in local + sends:
                cp.start()
            o_ref[...] = jnp.zeros_like(o_ref)
        o_ref[...] += _dot(a_ref[...], b_ref[...], "nt")

        @pl.when(j == steps - 1)
        def _():
            for cp in local:
                cp.wait()
            for cp in sends:
                cp.wait_send()
            for cp in recvs:
                cp.wait_recv()

    res = pl.pallas_call(
        body, grid=(steps,),
        in_specs=[pl.BlockSpec((m, tk), lambda j: (0, j)), pl.BlockSpec((n, tk), lambda j: (0, j))] + [_ANY] * na,
        out_specs=[pl.BlockSpec((m, n), lambda j: (0, 0))] + [_ANY] * na,
        out_shape=[jax.ShapeDtypeStruct((m, n), F32)] + [jax.ShapeDtypeStruct(s.shape, s.dtype) for s in sums],
        scratch_shapes=[pltpu.SemaphoreType.DMA((max(na, 1), N_CHIP - 1)), pltpu.SemaphoreType.DMA((max(na, 1), N_CHIP - 1)),
                        pltpu.SemaphoreType.DMA((max(na, 1),))],
        name=name, compiler_params=_cp("arbitrary"))(dproj, wfull, *sums)
    return res[0], res[1:]


def _conv_taps(x, w):
    c = x * w[CONV_WIDTH - 1:CONV_WIDTH, :]
    for j in range(CONV_WIDTH - 1):
        c = c + pltpu.roll(x, CONV_WIDTH - 1 - j, 0) * w[j:j + 1, :]
    return c


def _gdn_prep(proj, conv_wt, nh):
    lp = proj.shape[0]
    scale = HEAD_DIM ** -0.5

    def body(x_ref, w_ref, o_ref):
        which = pl.program_id(0) // nh
        c = _conv_taps(x_ref[...], w_ref[...])
        s = c * _sigmoid(c)
        r = lax.rsqrt(jnp.sum(s * s, axis=-1, keepdims=True) + EPS)
        f = jnp.where(which == 0, r * scale, jnp.where(which == 1, r, 1.0))
        o_ref[...] = jnp.where(_iota(s.shape, 0) >= PAD_ROWS, s * f, 0.0)

    return pl.pallas_call(
        body, grid=(3 * nh,),
        in_specs=[pl.BlockSpec((lp, LANES), lambda s: (0, s)), pl.BlockSpec((CONV_WIDTH, LANES), lambda s: (0, s))],
        out_specs=pl.BlockSpec((lp, LANES), lambda s: (0, s)),
        out_shape=jax.ShapeDtypeStruct((lp, 3 * nh * HEAD_DIM), F32), name="gdn_prep",
        compiler_params=_cp("parallel"))(proj, conv_wt)


def _gdn_prep_bwd(proj, conv_wt, dq, dk, dv, dproj, nh):
    lp = proj.shape[0]
    scale = HEAD_DIM ** -0.5
    part = lambda p: pl.BlockSpec((lp, LANES), lambda s: (0, jnp.clip(s - p * nh, 0, nh - 1)))

    def body(x_ref, w_ref, dq_ref, dk_ref, dv_ref, _, dx_ref, dw_ref):
        which = pl.program_id(0) // nh
        x = x_ref[...]
        w = w_ref[...]
        c = _conv_taps(x, w)
        sg = _sigmoid(c)
        s = c * sg
        r = lax.rsqrt(jnp.sum(s * s, axis=-1, keepdims=True) + EPS)
        dy = jnp.where(which == 0, dq_ref[...], jnp.where(which == 1, dk_ref[...], dv_ref[...]))
        dy = jnp.where(_iota(s.shape, 0) >= PAD_ROWS, dy, 0.0)
        y0 = s * r
        dy0 = dy * jnp.where(which == 0, scale, 1.0)
        ds_n = r * (dy0 - y0 * jnp.sum(dy0 * y0, axis=-1, keepdims=True))
        ds = jnp.where(which == 2, dy, ds_n)
        dc = ds * (sg * (1.0 + c * (1.0 - sg)))
        dx = dc * w[CONV_WIDTH - 1:CONV_WIDTH, :]
        rows = [jnp.sum(dc * x, axis=0, keepdims=True)]
        for j in range(CONV_WIDTH - 2, -1, -1):
            sh = CONV_WIDTH - 1 - j
            dx = dx + pltpu.roll(dc, lp - sh, 0) * w[j:j + 1, :]
            rows.insert(0, jnp.sum(dc * pltpu.roll(x, sh, 0), axis=0, keepdims=True))
        dx_ref[...] = dx.astype(BF16)
        dw_ref[...] = jnp.concatenate(rows, axis=0)

    strip = pl.BlockSpec((lp, LANES), lambda s: (0, s))
    taps = pl.BlockSpec((CONV_WIDTH, LANES), lambda s: (0, s))
    return pl.pallas_call(
        body, grid=(3 * nh,), in_specs=[strip, taps, part(0), part(1), part(2), _ANY], out_specs=[strip, taps],
        out_shape=[jax.ShapeDtypeStruct(dproj.shape, BF16), jax.ShapeDtypeStruct((CONV_WIDTH, 3 * nh * HEAD_DIM), F32)],
        input_output_aliases={5: 0}, name="gdn_prep_bwd", compiler_params=_cp("parallel"))(proj, conv_wt, dq, dk, dv, dproj)


def _gates(proj, bias_row, nega_row, nh):
    lp = proj.shape[0]
    nc = lp // CHUNK

    def body(p_ref, b_ref, a_ref, g_ref, gt3_ref, gtf_ref):
        lane = _iota((CHUNK, LANES), 1)
        tri = (_iota((CHUNK, CHUNK), 0) >= _iota((CHUNK, CHUNK), 1)).astype(F32)

        def step(n, carry):
            r0 = pl.multiple_of(n * CHUNK, CHUNK)
            z = p_ref[pl.ds(r0, CHUNK), :] + b_ref[...]
            base = jnp.where(lane < nh, _sigmoid(z),
                             jnp.where(lane < 2 * nh, a_ref[...] * _softplus(z),
                                       jnp.where(lane < 3 * nh, -_softplus(-z), 0.0)))
            base = jnp.where(r0 + _iota((CHUNK, LANES), 0) >= PAD_ROWS, base, 0.0)
            cs = _hdot(tri, base)
            run = jnp.where((lane >= 2 * nh) & (lane < 3 * nh), cs + carry, cs)
            sh = pltpu.roll(run, 2 * nh, 1)
            out = base + jnp.where((lane >= 3 * nh) & (lane < 5 * nh), sh, 0.0)
            g_ref[pl.ds(r0, CHUNK), :] = out
            gt3_ref[n] = out.T
            return carry + cs[CHUNK - 1:CHUNK, :]

        lax.fori_loop(0, nc, step, jnp.zeros((1, LANES), F32))
        gtf_ref[...] = g_ref[...].T

    vec = pl.BlockSpec((1, LANES), lambda i: (0, 0))
    return pl.pallas_call(
        body, grid=(1,), in_specs=[pl.BlockSpec((lp, LANES), lambda i: (0, 8 * nh)), vec, vec],
        out_specs=[pl.BlockSpec((lp, LANES), lambda i: (0, 0)), pl.BlockSpec((nc, LANES, CHUNK), lambda i: (0, 0, 0)),
                   pl.BlockSpec((LANES, lp), lambda i: (0, 0))],
        out_shape=[jax.ShapeDtypeStruct((lp, LANES), F32), jax.ShapeDtypeStruct((nc, LANES, CHUNK), F32),
                   jax.ShapeDtypeStruct((LANES, lp), F32)],
        name="gates", compiler_params=_cp("arbitrary"))(proj, bias_row, nega_row)


def _gates_bwd(proj, bias_row, nega_row, gates, dgate_gdn, dc_t, dproj, nh):
    lp = proj.shape[0]
    nc = lp // CHUNK

    def body(p_ref, b_ref, a_ref, g_ref, dg_ref, dc_ref, _, dz_ref, sm_ref, dct_scr):
        lane = _iota((CHUNK, LANES), 1)
        triu = (_iota((CHUNK, CHUNK), 0) <= _iota((CHUNK, CHUNK), 1)).astype(F32)
        dct_scr[...] = dc_ref[...].T
        sm_ref[...] = jnp.zeros_like(sm_ref)
        dz_ref[:, LANES:] = jnp.zeros((lp, NARROW - LANES), BF16)

        def step(i, carry):
            n = nc - 1 - i
            r0 = pl.multiple_of(n * CHUNK, CHUNK)
            z = p_ref[pl.ds(r0, CHUNK), :] + b_ref[...]
            gt = g_ref[pl.ds(r0, CHUNK), :]
            dgd = dg_ref[pl.ds(r0, CHUNK), :]
            dch = dct_scr[pl.ds(r0, CHUNK), :]
            rc = _hdot(triu, dch) + carry
            sg = _sigmoid(z)
            dz = jnp.where(lane < nh, dgd * sg * (1.0 - sg),
                           jnp.where(lane < 2 * nh, dgd * a_ref[...] * sg,
                                     jnp.where(lane < 3 * nh, rc * (1.0 - sg), 0.0)))
            dz = jnp.where(r0 + _iota((CHUNK, LANES), 0) >= PAD_ROWS, dz, 0.0)
            dz_ref[pl.ds(r0, CHUNK), 0:LANES] = dz.astype(BF16)
            sm_ref[0:1, :] += jnp.sum(dz, axis=0, keepdims=True)
            sm_ref[1:2, :] += jnp.sum(jnp.where((lane >= nh) & (lane < 2 * nh), dgd * gt, 0.0), axis=0, keepdims=True)
            return carry + jnp.sum(dch, axis=0, keepdims=True)

        lax.fori_loop(0, nc, step, jnp.zeros((1, LANES), F32))

    vec = pl.BlockSpec((1, LANES), lambda i: (0, 0))
    full = pl.BlockSpec((lp, LANES), lambda i: (0, 0))
    last = pl.BlockSpec((lp, LANES), lambda i: (0, 8 * nh))
    tail = pl.BlockSpec((lp, NARROW), lambda i: (0, 8 * nh * LANES // NARROW))
    return pl.pallas_call(
        body, grid=(1,), in_specs=[last, vec, vec, full, full, pl.BlockSpec((LANES, lp), lambda i: (0, 0)), _ANY],
        out_specs=[tail, pl.BlockSpec((8, LANES), lambda i: (0, 0))],
        out_shape=[jax.ShapeDtypeStruct(dproj.shape, BF16), jax.ShapeDtypeStruct((8, LANES), F32)],
        scratch_shapes=[pltpu.VMEM((lp, LANES), F32)], input_output_aliases={6: 0},
        name="gates_bwd", compiler_params=_cp("arbitrary"))(proj, bias_row, nega_row, gates, dgate_gdn, dc_t, dproj)


def _tri_inv(a):
    t = jnp.where(_iota(a.shape, 1) == _iota(a.shape, 2), 1.0, 0.0) - a
    p = a
    for _ in range(5):
        ph, pw = _split(p)
        p = _dot3(ph, ph) + (_dot3(ph, pw) + _dot3(pw, ph))
        ph, pw = _split(p)
        th, tw = _split(t)
        t = t + (_dot3(th, ph) + (_dot3(th, pw) + _dot3(tw, ph)))
    return t


def _gdn_chunk(q, k, v, beta, gc, gr):
    ii, jj = _iota((1, CHUNK, CHUNK), 1), _iota((1, CHUNK, CHUNK), 2)
    causal, strict = ii >= jj, ii > jj
    dm = jnp.where(causal, jnp.exp(jnp.where(causal, gc - gr, 0.0)), 0.0)
    kk = _bdot3(k, k, "nt")
    a = jnp.where(strict, beta * kk * dm, 0.0)
    t = _tri_inv(a)
    eg = jnp.exp(gc)
    glast = gc[:, CHUNK - 1:CHUNK, :]
    ekd = jnp.exp(glast - gc)
    bv = beta * v
    bk = (beta * eg) * k
    ub = _bdot3(t, jnp.concatenate([bv, bk], axis=2))
    qk = _bdot3(q, k, "nt")
    return dict(causal=causal, strict=strict, dm=dm, kk=kk, a=a, t=t, eg=eg, ekd=ekd, bv=bv, bk=bk,
                u=ub[:, :, :HEAD_DIM], w=ub[:, :, HEAD_DIM:], qk=qk, aqk=jnp.where(causal, qk * dm, 0.0),
                q_dec=q * eg, k_dec=k * ekd, decay=jnp.exp(glast))


def _heads(ref, nh):
    return jnp.stack([ref[:, h * HEAD_DIM:(h + 1) * HEAD_DIM] for h in range(nh)], axis=0)


def _gdn_chunk_inputs(q_ref, k_ref, v_ref, g, gt, nh):
    col = lambda o: jnp.stack([g[:, o + h:o + h + 1] for h in range(nh)], axis=0)
    gr = jnp.stack([gt[3 * nh + h:3 * nh + h + 1, :] for h in range(nh)], axis=0)
    return _heads(q_ref, nh), _heads(k_ref, nh), _heads(v_ref, nh), col(0), col(3 * nh), gr


def _gdn_fwd(qkv, gates, gt3, nh):
    lp = qkv.shape[0]
    nc = lp // CHUNK
    w = nh * HEAD_DIM

    def body(q_ref, k_ref, v_ref, g_ref, gt_ref, o_ref, sall_ref, s_scr):
        @pl.when(pl.program_id(0) == 0)
        def _():
            s_scr[...] = jnp.zeros_like(s_scr)
        c = _gdn_chunk(*_gdn_chunk_inputs(q_ref, k_ref, v_ref, g_ref[...], gt_ref[0], nh))
        s = s_scr[...]
        sall_ref[0] = s
        v_new = c["u"] - _bdot3(c["w"], s)
        o = _bdot3(c["q_dec"], s) + _bdot3(c["aqk"], v_new)
        s_scr[...] = s * c["decay"] + _bdot3(c["k_dec"], v_new, "tn")
        for h in range(nh):
            o_ref[:, h * HEAD_DIM:(h + 1) * HEAD_DIM] = o[h]

    return pl.pallas_call(
        body, grid=(nc,),
        in_specs=[pl.BlockSpec((CHUNK, w), lambda n: (n, 0)), pl.BlockSpec((CHUNK, w), lambda n: (n, 1)),
                  pl.BlockSpec((CHUNK, w), lambda n: (n, 2)), pl.BlockSpec((CHUNK, LANES), lambda n: (n, 0)),
                  pl.BlockSpec((1, LANES, CHUNK), lambda n: (n, 0, 0))],
        out_specs=[pl.BlockSpec((CHUNK, w), lambda n: (n, 0)),
                   pl.BlockSpec((1, nh, HEAD_DIM, HEAD_DIM), lambda n: (n, 0, 0, 0))],
        out_shape=[jax.ShapeDtypeStruct((lp, w), F32), jax.ShapeDtypeStruct((nc, nh, HEAD_DIM, HEAD_DIM), F32)],
        scratch_shapes=[pltpu.VMEM((nh, HEAD_DIM, HEAD_DIM), F32)],
        name="gdn_fwd", compiler_params=_cp("arbitrary"))(qkv, qkv, qkv, gates, gt3)


def _gdn_bwd(qkv, gates, gt3, s_all, do, nh):
    lp = qkv.shape[0]
    nc = lp // CHUNK
    w = nh * HEAD_DIM
    rev = lambda n: nc - 1 - n

    def body(q_ref, k_ref, v_ref, g_ref, gt_ref, s_ref, do_ref, dq_ref, dk_ref, dv_ref, dg_ref, ds_scr):
        @pl.when(pl.program_id(0) == 0)
        def _():
            ds_scr[...] = jnp.zeros_like(ds_scr)
        q, k, v, beta, gc, gr = _gdn_chunk_inputs(q_ref, k_ref, v_ref, g_ref[...], gt_ref[0], nh)
        c = _gdn_chunk(q, k, v, beta, gc, gr)
        s = s_ref[0]
        dsn = ds_scr[...]
        dout = _heads(do_ref, nh)
        v_new = c["u"] - _bdot3(c["w"], s)
        dq_dec = _bdot3(dout, s, "nt")
        daqk = jnp.where(c["causal"], _bdot3(dout, v_new, "nt"), 0.0)
        dv_new = _bdot3(c["aqk"], dout, "tn") + _bdot3(c["k_dec"], dsn)
        dk_dec = _bdot3(v_new, dsn, "nt")
        ddecay = jnp.sum(jnp.sum(dsn * s, axis=2, keepdims=True), axis=1, keepdims=True)
        dw = -_bdot3(dv_new, s, "nt")
        ds_scr[...] = _bdot3(c["q_dec"], dout, "tn") + c["decay"] * dsn - _bdot3(c["w"], dv_new, "tn")
        duw = jnp.concatenate([dv_new, dw], axis=2)
        dt = _bdot3(duw, jnp.concatenate([c["bv"], c["bk"]], axis=2), "nt")
        dbvk = _bdot3(c["t"], duw, "tn")
        dbv, dbk = dbvk[:, :, :HEAD_DIM], dbvk[:, :, HEAD_DIM:]
        da = jnp.where(c["strict"], -_bdot3(_bdot3(c["t"], dt, "tn"), c["t"], "nt"), 0.0)
        dkk = da * beta * c["dm"]
        dqk = daqk * c["dm"]
        e = da * c["a"] + daqk * c["aqk"]
        dq = dq_dec * c["eg"] + _bdot3(dqk, k)
        dk = (dk_dec * c["ekd"] + _bdot3(dkk, k) + _bdot3(dkk, k, "tn") + _bdot3(dqk, q, "tn")
              + (beta * c["eg"]) * dbk)
        dv = beta * dbv
        rs = lambda x: jnp.sum(x, axis=2, keepdims=True)
        dbeta = rs(dbv * v) + c["eg"] * rs(dbk * k) + rs(da * c["kk"] * c["dm"])
        kd_term = rs(dk_dec * c["k_dec"])
        eh, ew = _split(e)
        ones = jnp.ones((nh, CHUNK, LANES), BF16)
        col_sums = (_dot3(eh, ones, "tn") + _dot3(ew, ones, "tn"))[:, :, 0:1]
        dg_cum = rs(dq_dec * c["q_dec"]) - kd_term + rs(dbk * c["bk"]) + rs(e) - col_sums
        last = jnp.sum(kd_term, axis=1, keepdims=True) + ddecay * c["decay"]
        dg_cum = dg_cum + jnp.where(_iota((1, CHUNK, 1), 1) == CHUNK - 1, last, 0.0)
        lane = _iota((CHUNK, LANES), 1)
        acc = jnp.zeros((CHUNK, LANES), F32)
        for h in range(nh):
            sl = slice(h * HEAD_DIM, (h + 1) * HEAD_DIM)
            dq_ref[:, sl] = dq[h]
            dk_ref[:, sl] = dk[h]
            dv_ref[:, sl] = dv[h]
            acc = acc + jnp.where(lane == h, dbeta[h], 0.0) + jnp.where(lane == nh + h, dg_cum[h], 0.0)
        triu = (_iota((CHUNK, CHUNK), 0) <= _iota((CHUNK, CHUNK), 1)).astype(F32)
        dg_ref[...] = jnp.where(lane < nh, acc, _hdot(triu, acc))

    return pl.pallas_call(
        body, grid=(nc,),
        in_specs=[pl.BlockSpec((CHUNK, w), lambda n: (rev(n), 0)), pl.BlockSpec((CHUNK, w), lambda n: (rev(n), 1)),
                  pl.BlockSpec((CHUNK, w), lambda n: (rev(n), 2)), pl.BlockSpec((CHUNK, LANES), lambda n: (rev(n), 0)),
                  pl.BlockSpec((1, LANES, CHUNK), lambda n: (rev(n), 0, 0)),
                  pl.BlockSpec((1, nh, HEAD_DIM, HEAD_DIM), lambda n: (rev(n), 0, 0, 0)),
                  pl.BlockSpec((CHUNK, w), lambda n: (rev(n), 0))],
        out_specs=[pl.BlockSpec((CHUNK, w), lambda n: (rev(n), 0))] * 3 + [pl.BlockSpec((CHUNK, LANES), lambda n: (rev(n), 0))],
        out_shape=[jax.ShapeDtypeStruct((lp, w), F32)] * 3 + [jax.ShapeDtypeStruct((lp, LANES), F32)],
        scratch_shapes=[pltpu.VMEM((nh, HEAD_DIM, HEAD_DIM), F32)],
        name="gdn_bwd", compiler_params=_cp("arbitrary"))(qkv, qkv, qkv, gates, gt3, s_all, do)


def _merge_gdn(o_gdn, proj, norm_w, nh):
    lp = o_gdn.shape[0]

    def body(o_ref, z_ref, w_ref, m_ref):
        o = o_ref[...]
        z = z_ref[...]
        m_ref[...] = (o * _rms(o) * w_ref[...] * (z * _sigmoid(z))).astype(BF16)

    return pl.pallas_call(
        body, grid=(nh,),
        in_specs=[pl.BlockSpec((lp, LANES), lambda s: (0, s)), pl.BlockSpec((lp, LANES), lambda s: (0, 3 * nh + s)),
                  pl.BlockSpec((1, LANES), lambda s: (0, 0))],
        out_specs=pl.BlockSpec((lp, LANES), lambda s: (0, s)),
        out_shape=jax.ShapeDtypeStruct((lp, 2 * nh * HEAD_DIM), BF16), name="merge_gdn",
        compiler_params=_cp("parallel"))(o_gdn, proj, norm_w)


def _merge_gdn_bwd(o_gdn, proj, norm_w, dmerged, nh):
    lp = o_gdn.shape[0]

    def body(o_ref, z_ref, w_ref, dm_ref, do_ref, dz_ref, dw_ref):
        o = o_ref[...]
        r = _rms(o)
        xh = o * r
        silu, dsilu = _silu_and_grad(z_ref[...])
        dm = dm_ref[...]
        dn = dm * silu
        dz_ref[...] = (dm * (xh * w_ref[...]) * dsilu).astype(BF16)
        dnw = dn * w_ref[...]
        do_ref[...] = r * (dnw - xh * jnp.mean(dnw * xh, axis=-1, keepdims=True))

        @pl.when(pl.program_id(0) == 0)
        def _():
            dw_ref[...] = jnp.zeros_like(dw_ref)
        dw_ref[...] += jnp.sum(dn * xh, axis=0, keepdims=True)

    w = nh * HEAD_DIM
    return pl.pallas_call(
        body, grid=(nh,),
        in_specs=[pl.BlockSpec((lp, LANES), lambda s: (0, s)), pl.BlockSpec((lp, LANES), lambda s: (0, 3 * nh + s)),
                  pl.BlockSpec((1, LANES), lambda s: (0, 0)), pl.BlockSpec((lp, LANES), lambda s: (0, s))],
        out_specs=[pl.BlockSpec((lp, LANES), lambda s: (0, s)), pl.BlockSpec((lp, LANES), lambda s: (0, 3 * nh + s)),
                   pl.BlockSpec((1, LANES), lambda s: (0, 0))],
        out_shape=[jax.ShapeDtypeStruct((lp, w), F32), jax.ShapeDtypeStruct((lp, 8 * w + NARROW), BF16),
                   jax.ShapeDtypeStruct((1, LANES), F32)],
        name="merge_gdn_bwd", compiler_params=_cp("arbitrary"))(o_gdn, proj, norm_w, dmerged)


def _fox_prep(proj, qk_w, nh):
    lp = proj.shape[0]

    def body(x_ref, w_ref, o_ref):
        x = x_ref[...]
        o_ref[...] = x * _rms(x) * w_ref[0]

    return pl.pallas_call(
        body, grid=(2 * nh,),
        in_specs=[pl.BlockSpec((lp, LANES), lambda s: (0, 4 * nh + s)), pl.BlockSpec((1, 1, LANES), lambda s: (s // nh, 0, 0))],
        out_specs=pl.BlockSpec((lp, LANES), lambda s: (0, s)),
        out_shape=jax.ShapeDtypeStruct((lp, 2 * nh * HEAD_DIM), F32), name="fox_prep",
        compiler_params=_cp("parallel"))(proj, qk_w)


def _fox_prep_bwd(proj, qk_w, dq, dk, dproj, nh):
    lp = proj.shape[0]
    part = lambda p: pl.BlockSpec((lp, LANES), lambda s: (0, jnp.clip(s - p * nh, 0, nh - 1)))

    def body(x_ref, w_ref, dq_ref, dk_ref, _, dx_ref, dw_ref):
        x = x_ref[...]
        r = _rms(x)
        xh = x * r
        dy = jnp.where(pl.program_id(0) < nh, dq_ref[...], dk_ref[...])
        dyw = dy * w_ref[0]
        dx_ref[...] = (r * (dyw - xh * jnp.mean(dyw * xh, axis=-1, keepdims=True))).astype(BF16)

        @pl.when(pl.program_id(0) % nh == 0)
        def _():
            dw_ref[...] = jnp.zeros_like(dw_ref)
        dw_ref[0] += jnp.sum(dy * xh, axis=0, keepdims=True)

    strip = pl.BlockSpec((lp, LANES), lambda s: (0, 4 * nh + s))
    wsp = pl.BlockSpec((1, 1, LANES), lambda s: (s // nh, 0, 0))
    return pl.pallas_call(
        body, grid=(2 * nh,), in_specs=[strip, wsp, part(0), part(1), _ANY], out_specs=[strip, wsp],
        out_shape=[jax.ShapeDtypeStruct(dproj.shape, BF16), jax.ShapeDtypeStruct((2, 1, LANES), F32)],
        input_output_aliases={4: 0}, name="fox_prep_bwd", compiler_params=_cp("arbitrary"))(proj, qk_w, dq, dk, dproj)


def _fox_probs(q, k, gates, crow, h, i, nh):
    kl = k.shape[0]
    lane = _iota((Q_BLOCK, LANES), 1)
    ct = jnp.sum(jnp.where(lane == 4 * nh + h, gates, 0.0), axis=1, keepdims=True)
    s = _bdot(q, k, "nt") * (HEAD_DIM ** -0.5) + (ct - crow)
    t = i * Q_BLOCK + _iota((Q_BLOCK, kl), 0)
    kp = _iota((Q_BLOCK, kl), 1)
    s = jnp.where((kp <= t) & ((kp >= PAD_ROWS) | (t < PAD_ROWS)), s, NEG)
    p = jnp.exp(s - jnp.max(s, axis=1, keepdims=True))
    return p / jnp.sum(p, axis=1, keepdims=True)


FOX_HEADS_PER_STEP = 2


def _fox_specs(lp, nh):
    hw = FOX_HEADS_PER_STEP * LANES
    return [pl.BlockSpec((Q_BLOCK, hw), lambda g, i: (i, g)),
            pl.BlockSpec((lp, hw), lambda g, i: (0, nh // FOX_HEADS_PER_STEP + g)),
            pl.BlockSpec((lp, hw), lambda g, i: (0, 6 * nh // FOX_HEADS_PER_STEP + g)),
            pl.BlockSpec((Q_BLOCK, LANES), lambda g, i: (i, 0)),
            pl.BlockSpec((LANES, lp), lambda g, i: (0, 0))]


def _fox_fwd(qkn, proj, gates, gtf, nh):
    lp = qkn.shape[0]

    def body(q_ref, k_ref, v_ref, g_ref, gt_ref, o_ref):
        g, i = pl.program_id(0), pl.program_id(1)
        for j in range(lp // Q_BLOCK):
            @pl.when(i == j)
            def _(j=j):
                kl = (j + 1) * Q_BLOCK
                for hh in range(FOX_HEADS_PER_STEP):
                    h = FOX_HEADS_PER_STEP * g + hh
                    sl = slice(hh * LANES, (hh + 1) * LANES)
                    p = _fox_probs(q_ref[:, sl], k_ref[0:kl, sl], g_ref[...], gt_ref[pl.ds(4 * nh + h, 1), :][:, 0:kl],
                                   h, j, nh)
                    o_ref[:, sl] = _bdot(p, v_ref[0:kl, sl])

    return pl.pallas_call(
        body, grid=(nh // FOX_HEADS_PER_STEP, lp // Q_BLOCK), in_specs=_fox_specs(lp, nh),
        out_specs=pl.BlockSpec((Q_BLOCK, FOX_HEADS_PER_STEP * LANES), lambda g, i: (i, g)),
        out_shape=jax.ShapeDtypeStruct((lp, nh * HEAD_DIM), F32), name="fox_fwd",
        compiler_params=_cp("parallel", "parallel"))(qkn, qkn, proj, gates, gtf)


def _fox_bwd(qkn, proj, gates, gtf, do, dproj, nh):
    lp = qkn.shape[0]
    nq = lp // Q_BLOCK
    w = nh * HEAD_DIM
    scale = HEAD_DIM ** -0.5

    def body(q_ref, k_ref, v_ref, g_ref, gt_ref, do_ref, _, dq_ref, dk_ref, dc_ref, dv_ref, dv_scr):
        g, i = pl.program_id(0), pl.program_id(1)

        @pl.when(i == 0)
        def _():
            dk_ref[...] = jnp.zeros_like(dk_ref)
            dv_scr[...] = jnp.zeros_like(dv_scr)
            dc_ref[...] = jnp.zeros_like(dc_ref)
        for j in range(nq):
            @pl.when(i == j)
            def _(j=j):
                kl = (j + 1) * Q_BLOCK
                for hh in range(FOX_HEADS_PER_STEP):
                    h = FOX_HEADS_PER_STEP * g + hh
                    sl = slice(hh * LANES, (hh + 1) * LANES)
                    q, k = q_ref[:, sl], k_ref[0:kl, sl]
                    p = _fox_probs(q, k, g_ref[...], gt_ref[pl.ds(4 * nh + h, 1), :][:, 0:kl], h, j, nh)
                    dout = do_ref[:, sl]
                    dp = _bdot(dout, v_ref[0:kl, sl], "nt")
                    ds = p * (dp - jnp.sum(p * dp, axis=1, keepdims=True))
                    dq_ref[:, sl] = _bdot(ds, k) * scale
                    dk_ref[0:kl, sl] += _bdot(ds, q, "tn") * scale
                    dv_scr[0:kl, sl] += _bdot(p, dout, "tn")
                    dc_ref[hh, :, 0:kl] -= jnp.sum(ds, axis=0, keepdims=True)

        @pl.when(i == nq - 1)
        def _():
            dv_ref[...] = dv_scr[...].astype(BF16)

    hw = FOX_HEADS_PER_STEP * LANES
    blk = pl.BlockSpec((Q_BLOCK, hw), lambda g, i: (i, g))
    col = pl.BlockSpec((lp, hw), lambda g, i: (0, g))
    return pl.pallas_call(
        body, grid=(nh // FOX_HEADS_PER_STEP, nq), in_specs=_fox_specs(lp, nh) + [blk, _ANY],
        out_specs=[blk, col, pl.BlockSpec((FOX_HEADS_PER_STEP, 1, lp), lambda g, i: (g, 0, 0)),
                   pl.BlockSpec((lp, hw), lambda g, i: (0, 6 * nh // FOX_HEADS_PER_STEP + g))],
        out_shape=[jax.ShapeDtypeStruct((lp, w), F32)] * 2 + [jax.ShapeDtypeStruct((nh, 1, lp), F32),
                                                             jax.ShapeDtypeStruct(dproj.shape, BF16)],
        scratch_shapes=[pltpu.VMEM((lp, hw), F32)], input_output_aliases={6: 3},
        name="fox_bwd", compiler_params=_cp("parallel", "arbitrary"))(qkn, qkn, proj, gates, gtf, do, dproj)


def _merge_fox(o_fox, proj, merged, nh):
    lp = o_fox.shape[0]

    def body(o_ref, z_ref, _, m_ref):
        z = z_ref[...]
        m_ref[...] = (o_ref[...] * (z * _sigmoid(z))).astype(BF16)

    return pl.pallas_call(
        body, grid=(nh,),
        in_specs=[pl.BlockSpec((lp, LANES), lambda s: (0, s)), pl.BlockSpec((lp, LANES), lambda s: (0, 7 * nh + s)), _ANY],
        out_specs=pl.BlockSpec((lp, LANES), lambda s: (0, nh + s)),
        out_shape=jax.ShapeDtypeStruct(merged.shape, BF16), input_output_aliases={2: 0}, name="merge_fox",
        compiler_params=_cp("parallel"))(o_fox, proj, merged)


def _merge_fox_bwd(o_fox, proj, dmerged, dproj, nh):
    lp = o_fox.shape[0]

    def body(o_ref, z_ref, dm_ref, _, do_ref, dz_ref):
        silu, dsilu = _silu_and_grad(z_ref[...])
        dm = dm_ref[...]
        do_ref[...] = dm * silu
        dz_ref[...] = (dm * o_ref[...] * dsilu).astype(BF16)

    w = nh * HEAD_DIM
    return pl.pallas_call(
        body, grid=(nh,),
        in_specs=[pl.BlockSpec((lp, LANES), lambda s: (0, s)), pl.BlockSpec((lp, LANES), lambda s: (0, 7 * nh + s)),
                  pl.BlockSpec((lp, LANES), lambda s: (0, nh + s)), _ANY],
        out_specs=[pl.BlockSpec((lp, LANES), lambda s: (0, s)), pl.BlockSpec((lp, LANES), lambda s: (0, 7 * nh + s))],
        out_shape=[jax.ShapeDtypeStruct((lp, w), F32), jax.ShapeDtypeStruct(dproj.shape, BF16)],
        input_output_aliases={3: 1}, name="merge_fox_bwd", compiler_params=_cp("parallel"))(o_fox, proj, dmerged, dproj)


def _post(out, x, target, post_w):
    lp, d = out.shape

    def body(o_ref, x_ref, t_ref, w_ref, dy_ref, do_ref, loss_ref, dw_ref):
        i = pl.program_id(0)

        @pl.when(i == 0)
        def _():
            loss_ref[...] = jnp.zeros_like(loss_ref)
            dw_ref[...] = jnp.zeros_like(dw_ref)
        o = o_ref[...]
        r = _rms(o)
        nrm = o * r
        err = jnp.where(i > 0, x_ref[...] + nrm * w_ref[...] - t_ref[...], 0.0)
        loss_ref[0:1, :] += 0.5 * jnp.sum(jnp.sum(err * err, axis=1, keepdims=True), axis=0, keepdims=True) / d
        dy = err / d
        dy_ref[...] = dy
        dw_ref[...] += jnp.sum(dy * nrm, axis=0, keepdims=True)
        dyw = dy * w_ref[...]
        do_ref[...] = (r * (dyw - nrm * jnp.mean(dyw * nrm, axis=-1, keepdims=True))).astype(BF16)

    row = pl.BlockSpec((Q_BLOCK, d), lambda i: (i, 0))
    vec = pl.BlockSpec((1, d), lambda i: (0, 0))
    return pl.pallas_call(
        body, grid=(lp // Q_BLOCK,), in_specs=[row, _x_rows(d), _x_rows(d), vec],
        out_specs=[_x_rows(d), row, pl.BlockSpec((8, LANES), lambda i: (0, 0)), vec],
        out_shape=[jax.ShapeDtypeStruct(x.shape, F32), jax.ShapeDtypeStruct((lp, d), BF16),
                   jax.ShapeDtypeStruct((8, LANES), F32), jax.ShapeDtypeStruct((1, d), F32)],
        name="post", compiler_params=_cp("arbitrary"))(out, x, target, post_w)


def _prenorm_bwd(dxn, x, meta, w, dy):
    seq, d = x.shape
    lp = seq + Q_BLOCK

    def body(dx_ref, x_ref, m_ref, w_ref, dy_ref, gx_ref, gm_ref, dw_ref):
        i = pl.program_id(0)
        h = _h_tile(i, x_ref, m_ref)
        r = _rms(h)
        xh = h * r
        dxn_ = dx_ref[...]
        dxw = dxn_ * w_ref[...]
        dh = jnp.where(i > 0, dy_ref[...], 0.0) + r * (dxw - xh * jnp.mean(dxw * xh, axis=-1, keepdims=True))
        gx_ref[...] = dh

        @pl.when(i == 0)
        def _():
            dw_ref[...] = jnp.zeros_like(dw_ref)
            gm_ref[...] = dh[PAD_ROWS:, :]
        dw_ref[...] += jnp.sum(dxn_ * xh, axis=0, keepdims=True)

    vec = pl.BlockSpec((1, d), lambda i: (0, 0))
    met = pl.BlockSpec((N_META, d), lambda i: (0, 0))
    return pl.pallas_call(
        body, grid=(lp // Q_BLOCK,),
        in_specs=[pl.BlockSpec((Q_BLOCK, d), lambda i: (i, 0)), _x_rows(d), met, vec, _x_rows(d)],
        out_specs=[_x_rows(d), met, vec],
        out_shape=[jax.ShapeDtypeStruct((seq, d), F32), jax.ShapeDtypeStruct((N_META, d), F32),
                   jax.ShapeDtypeStruct((1, d), F32)],
        name="prenorm_bwd", compiler_params=_cp("arbitrary"))(dxn, x, meta, w, dy)


def _layer_grads(x, target, meta, pre_w, wfull, conv_w, a_log, dt_bias, gdn_norm_w, fq_w, fk_w, f_bias, w_out, post_w):
    nh = a_log.shape[1]
    conv_wt = conv_w.T
    zpad = jnp.zeros((1, LANES - 3 * nh), F32)
    bias_row = jnp.concatenate([jnp.zeros((1, nh), F32), dt_bias, f_bias, zpad], axis=1)
    nega_row = jnp.concatenate([jnp.zeros((1, nh), F32), -jnp.exp(a_log), jnp.zeros((1, nh), F32), zpad], axis=1)
    qk_w = jnp.stack([fq_w, fk_w])

    xn = _prenorm(x, meta, pre_w)
    proj = _matmul(xn, wfull, "nn", MM_TILE, F32, "proj")
    qkv = _gdn_prep(proj, conv_wt, nh)
    gates, gt3, gtf = _gates(proj, bias_row, nega_row, nh)
    o_gdn, s_all = _gdn_fwd(qkv, gates, gt3, nh)
    qkn = _fox_prep(proj, qk_w, nh)
    o_fox = _fox_fwd(qkn, proj, gates, gtf, nh)
    merged = _merge_fox(o_fox, proj, _merge_gdn(o_gdn, proj, gdn_norm_w, nh), nh)
    out = _matmul(merged, w_out, "nn", 4 * LANES, F32, "out_proj")
    dy, dout, loss_blk, dpost_w = _post(out, x, target, post_w)

    dw_out = _matmul(merged, dout, "tn", 4 * LANES, BF16, "dw_out")
    dmerged = _matmul(dout, w_out, "nt", 4 * LANES, F32, "dmerged")
    do_gdn, dproj, dgdn_norm_w = _merge_gdn_bwd(o_gdn, proj, gdn_norm_w, dmerged, nh)
    do_fox, dproj = _merge_fox_bwd(o_fox, proj, dmerged, dproj, nh)
    dqn, dkn, dc_t, dproj = _fox_bwd(qkn, proj, gates, gtf, do_fox, dproj, nh)
    dproj, dqk_w = _fox_prep_bwd(proj, qk_w, dqn, dkn, dproj, nh)
    dgq, dgk, dgv, dgate = _gdn_bwd(qkv, gates, gt3, s_all, do_gdn, nh)
    dproj, dconv_wt = _gdn_prep_bwd(proj, conv_wt, dgq, dgk, dgv, dproj, nh)
    dc_rows = jnp.pad(dc_t.reshape(nh, -1), ((2 * nh, LANES - 3 * nh), (0, 0)))
    dproj, gate_sums = _gates_bwd(proj, bias_row, nega_row, gates, dgate, dc_rows, dproj, nh)
    dwfull = _matmul(xn, dproj, "tn", MM_TILE, BF16, "dw_in")
    return dict(
        loss=loss_blk[0:1, 0:1], dy=dy, dproj=dproj, wfull=dwfull, post_w=dpost_w,
        conv_w=dconv_wt.T, a_log=gate_sums[1:2, nh:2 * nh], dt_bias=gate_sums[0:1, nh:2 * nh],
        gdn_norm_w=dgdn_norm_w, fq_w=dqk_w[0], fk_w=dqk_w[1], f_bias=gate_sums[0:1, 2 * nh:3 * nh], w_out=dw_out)


def _input_grads(g, wfull, x, meta, pre_w, sums):
    dxn, exchanged = _dxn_and_chip_exchange(g["dproj"], wfull, sums, MM_TILE, "dxn_chip_exchange")
    grad_x, dmeta, dpre_w = _prenorm_bwd(dxn, x, meta, pre_w, g["dy"])
    return grad_x, dmeta, dpre_w, exchanged


def _cast_bf16(a, tr, name):
    r, c = a.shape

    def body(a_ref, o_ref):
        o_ref[...] = a_ref[...].astype(BF16)

    return pl.pallas_call(
        body, grid=(r // tr,), in_specs=[pl.BlockSpec((tr, c), lambda i: (i, 0))],
        out_specs=pl.BlockSpec((tr, c), lambda i: (i, 0)), out_shape=jax.ShapeDtypeStruct((r, c), BF16),
        name=name, compiler_params=_cp("parallel"))(a)


def _gather_copies(ins, outs, send_sems, recv_sems, local_sems):
    n = len(ins)
    x, y, c = lax.axis_index("x"), lax.axis_index("y"), lax.axis_index("c")
    me, sibling = (x, y, c), (x, y, 1 - c)
    xn, yn, dg = (1 - x, y), (x, 1 - y), (1 - x, 1 - y)

    def copy(a, k, block, to, src=None):
        px, py, pc = block
        rows = outs[a].at[4 * px + 2 * py + pc]
        return pltpu.make_async_remote_copy(
            src_ref=rows if src is None else src, dst_ref=rows, send_sem=send_sems.at[a, k],
            recv_sem=recv_sems.at[a, k], device_id=to, device_id_type=_MESH)

    local = [pltpu.make_async_copy(ins[a], outs[a].at[4 * x + 2 * y + c], local_sems.at[a]) for a in range(n)]
    own = [cp for a in range(n) for cp in (copy(a, 0, me, sibling, src=ins[a]), copy(a, 1, me, (*xn, c), src=ins[a]),
                                           copy(a, 2, me, (*yn, c), src=ins[a]))]

    def start():
        for cp in local + own:
            cp.start()

    def finish():
        for a in range(n):
            @pl.when(c == 1)
            def _(a=a):
                copy(a, 1, (*xn, c), me).wait_recv()
                copy(a, 3, (*xn, c), (*yn, c)).start()

            @pl.when(c == 0)
            def _(a=a):
                copy(a, 2, (*yn, c), me).wait_recv()
                copy(a, 3, (*yn, c), (*xn, c)).start()
        for a in range(n):
            pl.when(c == 0)(copy(a, 1, (*xn, c), me).wait_recv)
            copy(a, 4, (*xn, c), sibling).start()
            pl.when(c == 1)(copy(a, 2, (*yn, c), me).wait_recv)
            copy(a, 5, (*yn, c), sibling).start()
        for a in range(n):
            copy(a, 3, (*dg, c), me).wait_recv()
            copy(a, 6, (*dg, c), sibling).start()
        for a in range(n):
            copy(a, 0, sibling, me).wait_recv()
            for k, chip in ((4, xn), (5, yn), (6, dg)):
                copy(a, k, (*chip, 1 - c), me).wait_recv()
                copy(a, k, (*chip, c), sibling).wait_send()
            copy(a, 3, (*xn, c), (*yn, c)).wait_send()
        for cp in own:
            cp.wait_send()
        for cp in local:
            cp.wait()

    return start, finish


def _gather_scratch(n):
    return [pltpu.SemaphoreType.DMA((n, N_DEV - 1)), pltpu.SemaphoreType.DMA((n, N_DEV - 1)), pltpu.SemaphoreType.DMA((n,))]


def _all_gather(arrays, name):
    n = len(arrays)

    def body(*refs):
        start, finish = _gather_copies(refs[:n], refs[n:2 * n], *refs[2 * n:])
        start()
        finish()

    return pl.pallas_call(
        body, in_specs=[_ANY] * n, out_specs=[_ANY] * n,
        out_shape=[jax.ShapeDtypeStruct((N_DEV,) + a.shape, a.dtype) for a in arrays],
        scratch_shapes=_gather_scratch(n), name=name)(*arrays)


def _pair_exchange(arrays, small, name):
    n, ns = len(arrays), len(small)

    def body(*refs):
        ins, sm_in = refs[:n], refs[n:n + ns]
        outs, sm_out = refs[n + ns:2 * n + ns], refs[2 * n + ns:2 * (n + ns)]
        send_sems, recv_sems = refs[2 * (n + ns):2 * (n + ns) + 2]
        x, y, c = lax.axis_index("x"), lax.axis_index("y"), lax.axis_index("c")
        copies = []
        for a in range(n):
            for q in range(N_CHIP):
                copies.append(pltpu.make_async_remote_copy(
                    src_ref=ins[a].at[2 * q + 1 - c], dst_ref=outs[a].at[q], send_sem=send_sems.at[a, q],
                    recv_sem=recv_sems.at[a, q], device_id=(x, y, 1 - c), device_id_type=_MESH))
        start, finish = _gather_copies(sm_in, sm_out, *refs[2 * (n + ns) + 2:])
        for cp in copies:
            cp.start()
        start()
        finish()
        for cp in copies:
            cp.wait()

    return pl.pallas_call(
        body, in_specs=[_ANY] * (n + ns), out_specs=[_ANY] * (n + ns),
        out_shape=([jax.ShapeDtypeStruct((N_CHIP,) + a.shape[1:], a.dtype) for a in arrays]
                   + [jax.ShapeDtypeStruct((N_DEV,) + a.shape, a.dtype) for a in small]),
        scratch_shapes=[pltpu.SemaphoreType.DMA((n, N_CHIP)), pltpu.SemaphoreType.DMA((n, N_CHIP))] + _gather_scratch(ns),
        name=name)(*arrays, *small)


def _pair_sum(parts, got, core, tr, name):
    _, r, c = parts.shape

    def body(core_ref, p_ref, g_ref, o_ref):
        o_ref[...] = (p_ref[...].astype(F32) + g_ref[...].astype(F32)).astype(o_ref.dtype)

    return pl.pallas_call(
        body,
        grid_spec=pltpu.PrefetchScalarGridSpec(
            num_scalar_prefetch=1, grid=(N_CHIP, r // tr),
            in_specs=[pl.BlockSpec((1, tr, c), lambda q, i, core_ref: (2 * q + core_ref[0], i, 0)),
                      pl.BlockSpec((1, tr, c), lambda q, i, core_ref: (q, i, 0))],
            out_specs=pl.BlockSpec((1, tr, c), lambda q, i, core_ref: (q, i, 0))),
        out_shape=jax.ShapeDtypeStruct((N_CHIP, r, c), parts.dtype), name=name,
        compiler_params=_cp("parallel", "parallel"))(core, parts, got)


def _native_segments(nh):
    w = nh * HEAD_DIM
    return [(0, 4 * w, 0), (4 * w, 4 * w + 2 * nh, 8 * w), (4 * w + 2 * nh, 8 * w + 2 * nh, 4 * w),
            (8 * w + 2 * nh, 8 * w + 3 * nh, 8 * w + 2 * nh)]


def _relayout_w_in(wg, nh, tr):
    _, d, cols = wg.shape
    w = nh * HEAD_DIM

    def native(ref, j0, j1):
        out = []
        while j0 < j1:
            blk = j0 // cols
            end = min(j1, (blk + 1) * cols)
            out.append(ref[blk, :, pl.ds(j0 - blk * cols, end - j0)])
            j0 = end
        return out

    def body(g_ref, o_ref):
        for cidx in range(8 * w // LANES):
            j0 = cidx * LANES + (0 if cidx * LANES < 4 * w else 2 * nh)
            pieces = native(g_ref, j0, j0 + LANES)
            o_ref[:, cidx * LANES:(cidx + 1) * LANES] = pieces[0] if len(pieces) == 1 else jnp.concatenate(pieces, axis=1)
        pieces = (native(g_ref, 4 * w, 4 * w + 2 * nh) + native(g_ref, 8 * w + 2 * nh, 8 * w + 3 * nh)
                  + [jnp.zeros((tr, NARROW - 3 * nh), wg.dtype)])
        o_ref[:, 8 * w:] = jnp.concatenate(pieces, axis=1)

    return pl.pallas_call(
        body, grid=(d // tr,), in_specs=[pl.BlockSpec((N_DEV, tr, cols), lambda i: (0, i, 0))],
        out_specs=pl.BlockSpec((tr, 8 * w + NARROW), lambda i: (i, 0)),
        out_shape=jax.ShapeDtypeStruct((d, 8 * w + NARROW), wg.dtype),
        name="relayout_w_in", compiler_params=_cp("parallel"))(wg)


def _relayout_dw_in(dwfull, nh, tr):
    d = dwfull.shape[0]
    w = nh * HEAD_DIM
    cols = (8 * w + 3 * nh) // N_DEV
    segs = _native_segments(nh)

    def body(f_ref, o_ref):
        for blk in range(N_DEV):
            pieces = []
            for s0, s1, t0 in segs:
                lo, hi = max(s0, blk * cols), min(s1, (blk + 1) * cols)
                if lo < hi:
                    pieces.append(f_ref[:, pl.ds(t0 + lo - s0, hi - lo)])
            o_ref[blk] = pieces[0] if len(pieces) == 1 else jnp.concatenate(pieces, axis=1)

    return pl.pallas_call(
        body, grid=(d // tr,), in_specs=[pl.BlockSpec((tr, 8 * w + NARROW), lambda i: (i, 0))],
        out_specs=pl.BlockSpec((N_DEV, tr, cols), lambda i: (0, i, 0)),
        out_shape=jax.ShapeDtypeStruct((N_DEV, d, cols), dwfull.dtype),
        name="relayout_dw_in", compiler_params=_cp("parallel"))(dwfull)


def _adamw(w, parts, m, v, tr, name):
    r, c = w.shape
    n_parts = parts.shape[0]

    def body(w_ref, p_ref, m_ref, v_ref, g_ref, d_ref, nm_ref, nv_ref):
        g = p_ref[0].astype(F32)
        for s in range(1, n_parts):
            g = g + p_ref[s].astype(F32)
        m_new = ADAM_B1 * m_ref[...] + (1.0 - ADAM_B1) * g
        v_new = ADAM_B2 * v_ref[...] + (1.0 - ADAM_B2) * (g * g)
        m_hat = m_new / (1.0 - ADAM_B1 ** ADAM_STEP)
        v_hat = v_new / (1.0 - ADAM_B2 ** ADAM_STEP)
        g_ref[...] = g
        d_ref[...] = -ADAM_LR * (m_hat / (jnp.sqrt(v_hat) + ADAM_EPS) + ADAM_WD * w_ref[...])
        nm_ref[...] = m_new
        nv_ref[...] = v_new

    blk = pl.BlockSpec((tr, c), lambda i: (i, 0))
    return pl.pallas_call(
        body, grid=(r // tr,), in_specs=[blk, pl.BlockSpec((n_parts, tr, c), lambda i: (0, i, 0)), blk, blk],
        out_specs=[blk] * 4, out_shape=[jax.ShapeDtypeStruct((r, c), F32)] * 4, name=name,
        compiler_params=_cp("parallel"))(w, parts, m, v)


def _pack_small(d, pre, post, a_log, dt_bias, f_bias, gdn_w, fq_w, fk_w, extra):
    row2 = jnp.concatenate([a_log, dt_bias, f_bias, gdn_w, fq_w, fk_w, extra], axis=1)
    row2 = jnp.pad(row2, ((0, 0), (0, d - row2.shape[1])))
    return jnp.concatenate([pre, post, row2, jnp.zeros((5, d), F32)], axis=0)


def _unpack_small(p, nh):
    o = 3 * nh
    return dict(pre=p[0:1], post=p[1:2], a_log=p[2:3, 0:nh], dt_bias=p[2:3, nh:2 * nh], f_bias=p[2:3, 2 * nh:o],
                gdn_w=p[2:3, o:o + HEAD_DIM], fq_w=p[2:3, o + HEAD_DIM:o + 2 * HEAD_DIM],
                fk_w=p[2:3, o + 2 * HEAD_DIM:o + 3 * HEAD_DIM], extra=p[2, o + 3 * HEAD_DIM])


def kernel(x, meta_tokens, pre_norm_w, w_in, conv_w, a_log, dt_bias, gdn_norm_w, fox_q_norm_w, fox_k_norm_w, fox_f_bias, w_out, post_norm_w, loss_target, m_meta_tokens, m_pre_norm_w, m_w_in, m_conv_w, m_a_log, m_dt_bias, m_gdn_norm_w, m_fox_q_norm_w, m_fox_k_norm_w, m_fox_f_bias, m_w_out, m_post_norm_w, v_meta_tokens, v_pre_norm_w, v_w_in, v_conv_w, v_a_log, v_dt_bias, v_gdn_norm_w, v_fox_q_norm_w, v_fox_k_norm_w, v_fox_f_bias, v_w_out, v_post_norm_w):
    nh = a_log.shape[1]
    d = x.shape[-1]
    w = nh * HEAD_DIM
    zero = jnp.zeros((1, 1), F32)

    wg, wog, cg, mg = _all_gather(
        [_cast_bf16(w_in[0], 256, "cast_w_in"), _cast_bf16(w_out[0], 256, "cast_w_out"), conv_w[0], meta_tokens],
        "gather_weights")
    wfull = _relayout_w_in(wg, nh, 256)
    meta_full = mg.transpose(1, 0, 2).reshape(N_META, d)
    g = _layer_grads(x[0], loss_target[0], meta_full, pre_norm_w, wfull, cg.reshape(3 * w, CONV_WIDTH), a_log, dt_bias,
                     gdn_norm_w, fox_q_norm_w, fox_k_norm_w, fox_f_bias, wog.reshape(2 * w, d), post_norm_w)

    core = lax.axis_index("c")
    dev = 4 * lax.axis_index("x") + 2 * lax.axis_index("y") + core
    core_arr = jnp.reshape(core, (1,)).astype(jnp.int32)
    mine_in = _relayout_dw_in(g["wfull"], nh, 128)
    mine_out = g["w_out"].reshape(N_DEV, 2 * w // N_DEV, d)
    got_in, got_out, a_conv = _pair_exchange([mine_in, mine_out], [g["conv_w"]], "pair_exchange")
    sums = [_pair_sum(mine_in, got_in, core_arr, 128, "pair_sum_w_in"),
            _pair_sum(mine_out, got_out, core_arr, 256, "pair_sum_w_out")]
    grad_x, dmeta, dpre_w, (p_in, p_out) = _input_grads(g, wfull, x[0], meta_full, pre_norm_w, sums)
    small = _pack_small(d, dpre_w, g["post_w"], g["a_log"], g["dt_bias"], g["f_bias"], g["gdn_norm_w"], g["fq_w"],
                        g["fk_w"], g["loss"])
    a_meta, p_small = _all_gather([dmeta, small], "gather_small_grads")
    p_conv = lax.dynamic_slice_in_dim(a_conv, dev * conv_w.shape[1], conv_w.shape[1], axis=1)
    p_meta = lax.dynamic_slice_in_dim(a_meta, dev * meta_tokens.shape[1], meta_tokens.shape[1], axis=2)

    r_in = _adamw(w_in[0], p_in, m_w_in[0], v_w_in[0], 128, "adamw_w_in")
    r_out = _adamw(w_out[0], p_out, m_w_out[0], v_w_out[0], 64, "adamw_w_out")
    r_conv = _adamw(conv_w[0], p_conv, m_conv_w[0], v_conv_w[0], conv_w.shape[1], "adamw_conv_w")
    r_meta = _adamw(meta_tokens, p_meta, m_meta_tokens, v_meta_tokens, N_META, "adamw_meta")
    pk = lambda pre, post, a, dt, gw, fq, fk, fb: _pack_small(d, pre, post, a, dt, fb, gw, fq, fk, zero)
    r_small = _adamw(
        pk(pre_norm_w, post_norm_w, a_log, dt_bias, gdn_norm_w, fox_q_norm_w, fox_k_norm_w, fox_f_bias), p_small,
        pk(m_pre_norm_w, m_post_norm_w, m_a_log, m_dt_bias, m_gdn_norm_w, m_fox_q_norm_w, m_fox_k_norm_w, m_fox_f_bias),
        pk(v_pre_norm_w, v_post_norm_w, v_a_log, v_dt_bias, v_gdn_norm_w, v_fox_q_norm_w, v_fox_k_norm_w, v_fox_f_bias),
        8, "adamw_small")

    sm = [_unpack_small(r, nh) for r in r_small]
    outs = []
    for i in range(4):
        s = sm[i]
        outs += [r_meta[i], s["pre"], r_in[i][None], r_conv[i][None], s["a_log"], s["dt_bias"], s["gdn_w"], s["fq_w"],
                 s["fk_w"], s["f_bias"], r_out[i][None], s["post"]]
    return (sm[0]["extra"], grad_x[None], *outs)
```

```python
import jax
import jax.numpy as jnp
from jax import lax
from jax.experimental import pallas as pl
from jax.experimental.pallas import tpu as pltpu

F32, BF16 = jnp.float32, jnp.bfloat16
HEAD_DIM = 128
N_META = 16
CONV_WIDTH = 4
CHUNK = 64
Q_BLOCK = 128
LANES = 128
EPS = 1e-6
PAD_ROWS = Q_BLOCK - N_META
N_DEV = 8
N_CHIP = 4
VMEM_LIMIT = 56 * 1024 * 1024
NEG = -1e30
NARROW = 2 * LANES
MM_TILE = 6 * LANES

ADAM_LR, ADAM_B1, ADAM_B2, ADAM_EPS, ADAM_WD, ADAM_STEP = 0.001, 0.9, 0.999, 1e-08, 0.01, 10

_DN = {"nn": (((1,), (0,)), ((), ())), "nt": (((1,), (1,)), ((), ())), "tn": (((0,), (0,)), ((), ()))}
_DN3 = {"nn": (((2,), (1,)), ((0,), (0,))), "nt": (((2,), (2,)), ((0,), (0,))), "tn": (((1,), (1,)), ((0,), (0,)))}
_ANY = pl.BlockSpec(memory_space=pl.ANY)
_MESH = pl.DeviceIdType.MESH


def _cp(*sem):
    return pltpu.CompilerParams(dimension_semantics=sem, vmem_limit_bytes=VMEM_LIMIT)


def _dot(a, b, dims="nn", prec=None):
    return lax.dot_general(a, b, _DN[dims], precision=prec, preferred_element_type=F32)


def _bdot(a, b, dims="nn"):
    return _dot(a.astype(BF16), b.astype(BF16), dims)


def _hdot(a, b, dims="nn"):
    return _dot(a, b, dims, prec=lax.Precision.HIGHEST)


def _dot3(a, b, dims="nn"):
    return lax.dot_general(a, b, _DN3[dims], preferred_element_type=F32)


def _bdot3(a, b, dims="nn"):
    return _dot3(a.astype(BF16), b.astype(BF16), dims)


def _split(a):
    hi = a.astype(BF16)
    return hi, (a - hi.astype(F32)).astype(BF16)


def _iota(shape, dim):
    return lax.broadcasted_iota(jnp.int32, shape, dim)


def _sigmoid(z):
    return 1.0 / (1.0 + jnp.exp(-z))


def _softplus(z):
    e = jnp.exp(-jnp.abs(z))
    u = 1.0 + e
    l1p = jnp.where(u == 1.0, e, jnp.log(u) * (e / jnp.where(u == 1.0, 1.0, u - 1.0)))
    return jnp.maximum(z, 0.0) + l1p


def _silu_and_grad(z):
    s = _sigmoid(z)
    return z * s, s * (1.0 + z * (1.0 - s))


def _rms(x):
    return lax.rsqrt(jnp.mean(x * x, axis=-1, keepdims=True) + EPS)


def _h_tile(i, x_ref, meta_ref):
    first = jnp.concatenate([jnp.zeros((PAD_ROWS, x_ref.shape[1]), F32), meta_ref[...]], axis=0)
    return jnp.where(i == 0, first, x_ref[...])


def _x_rows(d):
    return pl.BlockSpec((Q_BLOCK, d), lambda i: (jnp.maximum(i - 1, 0), 0))


def _prenorm(x, meta, w):
    seq, d = x.shape
    lp = seq + Q_BLOCK

    def body(x_ref, m_ref, w_ref, o_ref):
        h = _h_tile(pl.program_id(0), x_ref, m_ref)
        o_ref[...] = (h * _rms(h) * w_ref[...]).astype(BF16)

    return pl.pallas_call(
        body, grid=(lp // Q_BLOCK,),
        in_specs=[_x_rows(d), pl.BlockSpec((N_META, d), lambda i: (0, 0)), pl.BlockSpec((1, d), lambda i: (0, 0))],
        out_specs=pl.BlockSpec((Q_BLOCK, d), lambda i: (i, 0)),
        out_shape=jax.ShapeDtypeStruct((lp, d), BF16), name="prenorm", compiler_params=_cp("parallel"))(x, meta, w)


def _tile(n, want):
    return max(t for t in range(LANES, want + 1, LANES) if n % t == 0)


def _matmul(a, b, dims, tn, out_dtype, name):
    m = a.shape[1] if dims == "tn" else a.shape[0]
    n = b.shape[0] if dims == "nt" else b.shape[1]
    kdim = b.shape[1] if dims == "nt" else b.shape[0]
    tn = _tile(n, tn)
    b_spec = pl.BlockSpec((tn, kdim), lambda j: (j, 0)) if dims == "nt" else pl.BlockSpec((kdim, tn), lambda j: (0, j))

    def body(a_ref, b_ref, o_ref):
        o_ref[...] = _dot(a_ref[...], b_ref[...], dims).astype(out_dtype)

    return pl.pallas_call(
        body, grid=(n // tn,),
        in_specs=[pl.BlockSpec(a.shape, lambda j: (0, 0)), b_spec],
        out_specs=pl.BlockSpec((m, tn), lambda j: (0, j)),
        out_shape=jax.ShapeDtypeStruct((m, n), out_dtype), name=name, compiler_params=_cp("parallel"))(a, b)


def _chip_copies(ins, outs, send_sems, recv_sems, local_sems):
    if not ins:
        return [], [], []
    x, y, c = lax.axis_index("x"), lax.axis_index("y"), lax.axis_index("c")
    mine = 2 * x + y
    local, sends, recvs = [], [], []
    for a in range(len(ins)):
        local.append(pltpu.make_async_copy(ins[a].at[mine], outs[a].at[mine], local_sems.at[a]))
        for k in range(1, N_CHIP):
            px = 1 - x if k & 2 else x
            py = 1 - y if k & 1 else y
            kw = dict(send_sem=send_sems.at[a, k - 1], recv_sem=recv_sems.at[a, k - 1], device_id=(px, py, c),
                      device_id_type=_MESH)
            sends.append(pltpu.make_async_remote_copy(src_ref=ins[a].at[2 * px + py], dst_ref=outs[a].at[mine], **kw))
            recvs.append(pltpu.make_async_remote_copy(src_ref=ins[a].at[mine], dst_ref=outs[a].at[2 * px + py], **kw))
    return local, sends, recvs


def _dxn_and_chip_exchange(dproj, wfull, sums, tk, name):
    m, k = dproj.shape
    n = wfull.shape[0]
    na = len(sums)
    tk = _tile(k, tk)
    steps = k // tk

    def body(a_ref, b_ref, *rest):
        ins, o_ref, outs = rest[:na], rest[na], rest[na + 1:2 * na + 1]
        local, sends, recvs = _chip_copies(ins, outs, *rest[2 * na + 1:])
        j = pl.program_id(0)

        @pl.when(j == 0)
        def _():
            for cp in local + sends:
                cp.start()
            o_ref[...] = jnp.zeros_like(o_ref)
        o_ref[...] += _dot(a_ref[...], b_ref[...], "nt")

        @pl.when(j == steps - 1)
        def _():
            for cp in local:
                cp.wait()
            for cp in sends:
                cp.wait_send()
            for cp in recvs:
                cp.wait_recv()

    res = pl.pallas_call(
        body, grid=(steps,),
        in_specs=[pl.BlockSpec((m, tk), lambda j: (0, j)), pl.BlockSpec((n, tk), lambda j: (0, j))] + [_ANY] * na,
        out_specs=[pl.BlockSpec((m, n), lambda j: (0, 0))] + [_ANY] * na,
        out_shape=[jax.ShapeDtypeStruct((m, n), F32)] + [jax.ShapeDtypeStruct(s.shape, s.dtype) for s in sums],
        scratch_shapes=[pltpu.SemaphoreType.DMA((max(na, 1), N_CHIP - 1)), pltpu.SemaphoreType.DMA((max(na, 1), N_CHIP - 1)),
                        pltpu.SemaphoreType.DMA((max(na, 1),))],
        name=name, compiler_params=_cp("arbitrary"))(dproj, wfull, *sums)
    return res[0], res[1:]


def _conv_taps(x, w):
    c = x * w[CONV_WIDTH - 1:CONV_WIDTH, :]
    for j in range(CONV_WIDTH - 1):
        c = c + pltpu.roll(x, CONV_WIDTH - 1 - j, 0) * w[j:j + 1, :]
    return c


def _gdn_prep(proj, conv_wt, nh):
    lp = proj.shape[0]
    scale = HEAD_DIM ** -0.5

    def body(x_ref, w_ref, o_ref):
        which = pl.program_id(0) // nh
        c = _conv_taps(x_ref[...], w_ref[...])
        s = c * _sigmoid(c)
        r = lax.rsqrt(jnp.sum(s * s, axis=-1, keepdims=True) + EPS)
        f = jnp.where(which == 0, r * scale, jnp.where(which == 1, r, 1.0))
        o_ref[...] = jnp.where(_iota(s.shape, 0) >= PAD_ROWS, s * f, 0.0)

    return pl.pallas_call(
        body, grid=(3 * nh,),
        in_specs=[pl.BlockSpec((lp, LANES), lambda s: (0, s)), pl.BlockSpec((CONV_WIDTH, LANES), lambda s: (0, s))],
        out_specs=pl.BlockSpec((lp, LANES), lambda s: (0, s)),
        out_shape=jax.ShapeDtypeStruct((lp, 3 * nh * HEAD_DIM), F32), name="gdn_prep",
        compiler_params=_cp("parallel"))(proj, conv_wt)


def _gdn_prep_bwd(proj, conv_wt, dq, dk, dv, dproj, nh):
    lp = proj.shape[0]
    scale = HEAD_DIM ** -0.5
    part = lambda p: pl.BlockSpec((lp, LANES), lambda s: (0, jnp.clip(s - p * nh, 0, nh - 1)))

    def body(x_ref, w_ref, dq_ref, dk_ref, dv_ref, _, dx_ref, dw_ref):
        which = pl.program_id(0) // nh
        x = x_ref[...]
        w = w_ref[...]
        c = _conv_taps(x, w)
        sg = _sigmoid(c)
        s = c * sg
        r = lax.rsqrt(jnp.sum(s * s, axis=-1, keepdims=True) + EPS)
        dy = jnp.where(which == 0, dq_ref[...], jnp.where(which == 1, dk_ref[...], dv_ref[...]))
        dy = jnp.where(_iota(s.shape, 0) >= PAD_ROWS, dy, 0.0)
        y0 = s * r
        dy0 = dy * jnp.where(which == 0, scale, 1.0)
        ds_n = r * (dy0 - y0 * jnp.sum(dy0 * y0, axis=-1, keepdims=True))
        ds = jnp.where(which == 2, dy, ds_n)
        dc = ds * (sg * (1.0 + c * (1.0 - sg)))
        dx = dc * w[CONV_WIDTH - 1:CONV_WIDTH, :]
        rows = [jnp.sum(dc * x, axis=0, keepdims=True)]
        for j in range(CONV_WIDTH - 2, -1, -1):
            sh = CONV_WIDTH - 1 - j
            dx = dx + pltpu.roll(dc, lp - sh, 0) * w[j:j + 1, :]
            rows.insert(0, jnp.sum(dc * pltpu.roll(x, sh, 0), axis=0, keepdims=True))
        dx_ref[...] = dx.astype(BF16)
        dw_ref[...] = jnp.concatenate(rows, axis=0)

    strip = pl.BlockSpec((lp, LANES), lambda s: (0, s))
    taps = pl.BlockSpec((CONV_WIDTH, LANES), lambda s: (0, s))
    return pl.pallas_call(
        body, grid=(3 * nh,), in_specs=[strip, taps, part(0), part(1), part(2), _ANY], out_specs=[strip, taps],
        out_shape=[jax.ShapeDtypeStruct(dproj.shape, BF16), jax.ShapeDtypeStruct((CONV_WIDTH, 3 * nh * HEAD_DIM), F32)],
        input_output_aliases={5: 0}, name="gdn_prep_bwd", compiler_params=_cp("parallel"))(proj, conv_wt, dq, dk, dv, dproj)


def _gates(proj, bias_row, nega_row, nh):
    lp = proj.shape[0]
    nc = lp // CHUNK

    def body(p_ref, b_ref, a_ref, g_ref, gt3_ref, gtf_ref):
        lane = _iota((CHUNK, LANES), 1)
        tri = (_iota((CHUNK, CHUNK), 0) >= _iota((CHUNK, CHUNK), 1)).astype(F32)

        def step(n, carry):
            r0 = pl.multiple_of(n * CHUNK, CHUNK)
            z = p_ref[pl.ds(r0, CHUNK), :] + b_ref[...]
            base = jnp.where(lane < nh, _sigmoid(z),
                             jnp.where(lane < 2 * nh, a_ref[...] * _softplus(z),
                                       jnp.where(lane < 3 * nh, -_softplus(-z), 0.0)))
            base = jnp.where(r0 + _iota((CHUNK, LANES), 0) >= PAD_ROWS, base, 0.0)
            cs = _hdot(tri, base)
            run = jnp.where((lane >= 2 * nh) & (lane < 3 * nh), cs + carry, cs)
            sh = pltpu.roll(run, 2 * nh, 1)
            out = base + jnp.where((lane >= 3 * nh) & (lane < 5 * nh), sh, 0.0)
            g_ref[pl.ds(r0, CHUNK), :] = out
            gt3_ref[n] = out.T
            return carry + cs[CHUNK - 1:CHUNK, :]

        lax.fori_loop(0, nc, step, jnp.zeros((1, LANES), F32))
        gtf_ref[...] = g_ref[...].T

    vec = pl.BlockSpec((1, LANES), lambda i: (0, 0))
    return pl.pallas_call(
        body, grid=(1,), in_specs=[pl.BlockSpec((lp, LANES), lambda i: (0, 8 * nh)), vec, vec],
        out_specs=[pl.BlockSpec((lp, LANES), lambda i: (0, 0)), pl.BlockSpec((nc, LANES, CHUNK), lambda i: (0, 0, 0)),
                   pl.BlockSpec((LANES, lp), lambda i: (0, 0))],
        out_shape=[jax.ShapeDtypeStruct((lp, LANES), F32), jax.ShapeDtypeStruct((nc, LANES, CHUNK), F32),
                   jax.ShapeDtypeStruct((LANES, lp), F32)],
        name="gates", compiler_params=_cp("arbitrary"))(proj, bias_row, nega_row)


def _gates_bwd(proj, bias_row, nega_row, gates, dgate_gdn, dc_t, dproj, nh):
    lp = proj.shape[0]
    nc = lp // CHUNK

    def body(p_ref, b_ref, a_ref, g_ref, dg_ref, dc_ref, _, dz_ref, sm_ref, dct_scr):
        lane = _iota((CHUNK, LANES), 1)
        triu = (_iota((CHUNK, CHUNK), 0) <= _iota((CHUNK, CHUNK), 1)).astype(F32)
        dct_scr[...] = dc_ref[...].T
        sm_ref[...] = jnp.zeros_like(sm_ref)
        dz_ref[:, LANES:] = jnp.zeros((lp, NARROW - LANES), BF16)

        def step(i, carry):
            n = nc - 1 - i
            r0 = pl.multiple_of(n * CHUNK, CHUNK)
            z = p_ref[pl.ds(r0, CHUNK), :] + b_ref[...]
            gt = g_ref[pl.ds(r0, CHUNK), :]
            dgd = dg_ref[pl.ds(r0, CHUNK), :]
            dch = dct_scr[pl.ds(r0, CHUNK), :]
            rc = _hdot(triu, dch) + carry
            sg = _sigmoid(z)
            dz = jnp.where(lane < nh, dgd * sg * (1.0 - sg),
                           jnp.where(lane < 2 * nh, dgd * a_ref[...] * sg,
                                     jnp.where(lane < 3 * nh, rc * (1.0 - sg), 0.0)))
            dz = jnp.where(r0 + _iota((CHUNK, LANES), 0) >= PAD_ROWS, dz, 0.0)
            dz_ref[pl.ds(r0, CHUNK), 0:LANES] = dz.astype(BF16)
            sm_ref[0:1, :] += jnp.sum(dz, axis=0, keepdims=True)
            sm_ref[1:2, :] += jnp.sum(jnp.where((lane >= nh) & (lane < 2 * nh), dgd * gt, 0.0), axis=0, keepdims=True)
            return carry + jnp.sum(dch, axis=0, keepdims=True)

        lax.fori_loop(0, nc, step, jnp.zeros((1, LANES), F32))

    vec = pl.BlockSpec((1, LANES), lambda i: (0, 0))
    full = pl.BlockSpec((lp, LANES), lambda i: (0, 0))
    last = pl.BlockSpec((lp, LANES), lambda i: (0, 8 * nh))
    tail = pl.BlockSpec((lp, NARROW), lambda i: (0, 8 * nh * LANES // NARROW))
    return pl.pallas_call(
        body, grid=(1,), in_specs=[last, vec, vec, full, full, pl.BlockSpec((LANES, lp), lambda i: (0, 0)), _ANY],
        out_specs=[tail, pl.BlockSpec((8, LANES), lambda i: (0, 0))],
        out_shape=[jax.ShapeDtypeStruct(dproj.shape, BF16), jax.ShapeDtypeStruct((8, LANES), F32)],
        scratch_shapes=[pltpu.VMEM((lp, LANES), F32)], input_output_aliases={6: 0},
        name="gates_bwd", compiler_params=_cp("arbitrary"))(proj, bias_row, nega_row, gates, dgate_gdn, dc_t, dproj)


def _tri_inv(a):
    t = jnp.where(_iota(a.shape, 1) == _iota(a.shape, 2), 1.0, 0.0) - a
    p = a
    for _ in range(5):
        ph, pw = _split(p)
        p = _dot3(ph, ph) + (_dot3(ph, pw) + _dot3(pw, ph))
        ph, pw = _split(p)
        th, tw = _split(t)
        t = t + (_dot3(th, ph) + (_dot3(th, pw) + _dot3(tw, ph)))
    return t


def _gdn_chunk(q, k, v, beta, gc, gr, t=None):
    ii, jj = _iota((1, CHUNK, CHUNK), 1), _iota((1, CHUNK, CHUNK), 2)
    causal, strict = ii >= jj, ii > jj
    dm = jnp.where(causal, jnp.exp(jnp.where(causal, gc - gr, 0.0)), 0.0)
    kk = _bdot3(k, k, "nt")
    a = jnp.where(strict, beta * kk * dm, 0.0)
    if t is None:
        t = _tri_inv(a)
    eg = jnp.exp(gc)
    glast = gc[:, CHUNK - 1:CHUNK, :]
    ekd = jnp.exp(glast - gc)
    bv = beta * v
    bk = (beta * eg) * k
    ub = _bdot3(t, jnp.concatenate([bv, bk], axis=2))
    qk = _bdot3(q, k, "nt")
    return dict(causal=causal, strict=strict, dm=dm, kk=kk, a=a, t=t, eg=eg, ekd=ekd, bv=bv, bk=bk,
                u=ub[:, :, :HEAD_DIM], w=ub[:, :, HEAD_DIM:], qk=qk, aqk=jnp.where(causal, qk * dm, 0.0),
                q_dec=q * eg, k_dec=k * ekd, decay=jnp.exp(glast))


def _heads(ref, nh):
    return jnp.stack([ref[:, h * HEAD_DIM:(h + 1) * HEAD_DIM] for h in range(nh)], axis=0)


def _gdn_chunk_inputs(q_ref, k_ref, v_ref, g, gt, nh):
    col = lambda o: jnp.stack([g[:, o + h:o + h + 1] for h in range(nh)], axis=0)
    gr = jnp.stack([gt[3 * nh + h:3 * nh + h + 1, :] for h in range(nh)], axis=0)
    return _heads(q_ref, nh), _heads(k_ref, nh), _heads(v_ref, nh), col(0), col(3 * nh), gr


def _gdn_fwd(qkv, gates, gt3, nh):
    lp = qkv.shape[0]
    nc = lp // CHUNK
    w = nh * HEAD_DIM

    def body(q_ref, k_ref, v_ref, g_ref, gt_ref, o_ref, sall_ref, tall_ref, s_scr):
        @pl.when(pl.program_id(0) == 0)
        def _():
            s_scr[...] = jnp.zeros_like(s_scr)
        c = _gdn_chunk(*_gdn_chunk_inputs(q_ref, k_ref, v_ref, g_ref[...], gt_ref[0], nh))
        s = s_scr[...]
        sall_ref[0] = s
        tall_ref[0] = c["t"]
        v_new = c["u"] - _bdot3(c["w"], s)
        o = _bdot3(c["q_dec"], s) + _bdot3(c["aqk"], v_new)
        s_scr[...] = s * c["decay"] + _bdot3(c["k_dec"], v_new, "tn")
        for h in range(nh):
            o_ref[:, h * HEAD_DIM:(h + 1) * HEAD_DIM] = o[h]

    return pl.pallas_call(
        body, grid=(nc,),
        in_specs=[pl.BlockSpec((CHUNK, w), lambda n: (n, 0)), pl.BlockSpec((CHUNK, w), lambda n: (n, 1)),
                  pl.BlockSpec((CHUNK, w), lambda n: (n, 2)), pl.BlockSpec((CHUNK, LANES), lambda n: (n, 0)),
                  pl.BlockSpec((1, LANES, CHUNK), lambda n: (n, 0, 0))],
        out_specs=[pl.BlockSpec((CHUNK, w), lambda n: (n, 0)),
                   pl.BlockSpec((1, nh, HEAD_DIM, HEAD_DIM), lambda n: (n, 0, 0, 0)),
                   pl.BlockSpec((1, nh, CHUNK, CHUNK), lambda n: (n, 0, 0, 0))],
        out_shape=[jax.ShapeDtypeStruct((lp, w), F32), jax.ShapeDtypeStruct((nc, nh, HEAD_DIM, HEAD_DIM), F32),
                   jax.ShapeDtypeStruct((nc, nh, CHUNK, CHUNK), F32)],
        scratch_shapes=[pltpu.VMEM((nh, HEAD_DIM, HEAD_DIM), F32)],
        name="gdn_fwd", compiler_params=_cp("arbitrary"))(qkv, qkv, qkv, gates, gt3)


def _gdn_bwd(qkv, gates, gt3, s_all, t_all, do, nh):
    lp = qkv.shape[0]
    nc = lp // CHUNK
    w = nh * HEAD_DIM
    rev = lambda n: nc - 1 - n

    def body(q_ref, k_ref, v_ref, g_ref, gt_ref, s_ref, t_ref, do_ref, dq_ref, dk_ref, dv_ref, dg_ref, ds_scr):
        @pl.when(pl.program_id(0) == 0)
        def _():
            ds_scr[...] = jnp.zeros_like(ds_scr)
        q, k, v, beta, gc, gr = _gdn_chunk_inputs(q_ref, k_ref, v_ref, g_ref[...], gt_ref[0], nh)
        c = _gdn_chunk(q, k, v, beta, gc, gr, t_ref[0])
        s = s_ref[0]
        dsn = ds_scr[...]
        dout = _heads(do_ref, nh)
        v_new = c["u"] - _bdot3(c["w"], s)
        dq_dec = _bdot3(dout, s, "nt")
        daqk = jnp.where(c["causal"], _bdot3(dout, v_new, "nt"), 0.0)
        dv_new = _bdot3(c["aqk"], dout, "tn") + _bdot3(c["k_dec"], dsn)
        dk_dec = _bdot3(v_new, dsn, "nt")
        ddecay = jnp.sum(jnp.sum(dsn * s, axis=2, keepdims=True), axis=1, keepdims=True)
        dw = -_bdot3(dv_new, s, "nt")
        ds_scr[...] = _bdot3(c["q_dec"], dout, "tn") + c["decay"] * dsn - _bdot3(c["w"], dv_new, "tn")
        duw = jnp.concatenate([dv_new, dw], axis=2)
        dt = _bdot3(duw, jnp.concatenate([c["bv"], c["bk"]], axis=2), "nt")
        dbvk = _bdot3(c["t"], duw, "tn")
        dbv, dbk = dbvk[:, :, :HEAD_DIM], dbvk[:, :, HEAD_DIM:]
        da = jnp.where(c["strict"], -_bdot3(_bdot3(c["t"], dt, "tn"), c["t"], "nt"), 0.0)
        dkk = da * beta * c["dm"]
        dqk = daqk * c["dm"]
        e = da * c["a"] + daqk * c["aqk"]
        dq = dq_dec * c["eg"] + _bdot3(dqk, k)
        dk = (dk_dec * c["ekd"] + _bdot3(dkk, k) + _bdot3(dkk, k, "tn") + _bdot3(dqk, q, "tn")
              + (beta * c["eg"]) * dbk)
        dv = beta * dbv
        rs = lambda x: jnp.sum(x, axis=2, keepdims=True)
        dbeta = rs(dbv * v) + c["eg"] * rs(dbk * k) + rs(da * c["kk"] * c["dm"])
        kd_term = rs(dk_dec * c["k_dec"])
        eh, ew = _split(e)
        ones = jnp.ones((nh, CHUNK, LANES), BF16)
        col_sums = (_dot3(eh, ones, "tn") + _dot3(ew, ones, "tn"))[:, :, 0:1]
        dg_cum = rs(dq_dec * c["q_dec"]) - kd_term + rs(dbk * c["bk"]) + rs(e) - col_sums
        last = jnp.sum(kd_term, axis=1, keepdims=True) + ddecay * c["decay"]
        dg_cum = dg_cum + jnp.where(_iota((1, CHUNK, 1), 1) == CHUNK - 1, last, 0.0)
        lane = _iota((CHUNK, LANES), 1)
        acc = jnp.zeros((CHUNK, LANES), F32)
        for h in range(nh):
            sl = slice(h * HEAD_DIM, (h + 1) * HEAD_DIM)
            dq_ref[:, sl] = dq[h]
            dk_ref[:, sl] = dk[h]
            dv_ref[:, sl] = dv[h]
            acc = acc + jnp.where(lane == h, dbeta[h], 0.0) + jnp.where(lane == nh + h, dg_cum[h], 0.0)
        triu = (_iota((CHUNK, CHUNK), 0) <= _iota((CHUNK, CHUNK), 1)).astype(F32)
        dg_ref[...] = jnp.where(lane < nh, acc, _hdot(triu, acc))

    return pl.pallas_call(
        body, grid=(nc,),
        in_specs=[pl.BlockSpec((CHUNK, w), lambda n: (rev(n), 0)), pl.BlockSpec((CHUNK, w), lambda n: (rev(n), 1)),
                  pl.BlockSpec((CHUNK, w), lambda n: (rev(n), 2)), pl.BlockSpec((CHUNK, LANES), lambda n: (rev(n), 0)),
                  pl.BlockSpec((1, LANES, CHUNK), lambda n: (rev(n), 0, 0)),
                  pl.BlockSpec((1, nh, HEAD_DIM, HEAD_DIM), lambda n: (rev(n), 0, 0, 0)),
                  pl.BlockSpec((1, nh, CHUNK, CHUNK), lambda n: (rev(n), 0, 0, 0)),
                  pl.BlockSpec((CHUNK, w), lambda n: (rev(n), 0))],
        out_specs=[pl.BlockSpec((CHUNK, w), lambda n: (rev(n), 0))] * 3 + [pl.BlockSpec((CHUNK, LANES), lambda n: (rev(n), 0))],
        out_shape=[jax.ShapeDtypeStruct((lp, w), F32)] * 3 + [jax.ShapeDtypeStruct((lp, LANES), F32)],
        scratch_shapes=[pltpu.VMEM((nh, HEAD_DIM, HEAD_DIM), F32)],
        name="gdn_bwd", compiler_params=_cp("arbitrary"))(qkv, qkv, qkv, gates, gt3, s_all, t_all, do)


def _merge_gdn(o_gdn, proj, norm_w, nh):
    lp = o_gdn.shape[0]

    def body(o_ref, z_ref, w_ref, m_ref):
        o = o_ref[...]
        z = z_ref[...]
        m_ref[...] = (o * _rms(o) * w_ref[...] * (z * _sigmoid(z))).astype(BF16)

    return pl.pallas_call(
        body, grid=(nh,),
        in_specs=[pl.BlockSpec((lp, LANES), lambda s: (0, s)), pl.BlockSpec((lp, LANES), lambda s: (0, 3 * nh + s)),
                  pl.BlockSpec((1, LANES), lambda s: (0, 0))],
        out_specs=pl.BlockSpec((lp, LANES), lambda s: (0, s)),
        out_shape=jax.ShapeDtypeStruct((lp, 2 * nh * HEAD_DIM), BF16), name="merge_gdn",
        compiler_params=_cp("parallel"))(o_gdn, proj, norm_w)


def _merge_gdn_bwd(o_gdn, proj, norm_w, dmerged, nh):
    lp = o_gdn.shape[0]

    def body(o_ref, z_ref, w_ref, dm_ref, do_ref, dz_ref, dw_ref):
        o = o_ref[...]
        r = _rms(o)
        xh = o * r
        silu, dsilu = _silu_and_grad(z_ref[...])
        dm = dm_ref[...]
        dn = dm * silu
        dz_ref[...] = (dm * (xh * w_ref[...]) * dsilu).astype(BF16)
        dnw = dn * w_ref[...]
        do_ref[...] = r * (dnw - xh * jnp.mean(dnw * xh, axis=-1, keepdims=True))

        @pl.when(pl.program_id(0) == 0)
        def _():
            dw_ref[...] = jnp.zeros_like(dw_ref)
        dw_ref[...] += jnp.sum(dn * xh, axis=0, keepdims=True)

    w = nh * HEAD_DIM
    return pl.pallas_call(
        body, grid=(nh,),
        in_specs=[pl.BlockSpec((lp, LANES), lambda s: (0, s)), pl.BlockSpec((lp, LANES), lambda s: (0, 3 * nh + s)),
                  pl.BlockSpec((1, LANES), lambda s: (0, 0)), pl.BlockSpec((lp, LANES), lambda s: (0, s))],
        out_specs=[pl.BlockSpec((lp, LANES), lambda s: (0, s)), pl.BlockSpec((lp, LANES), lambda s: (0, 3 * nh + s)),
                   pl.BlockSpec((1, LANES), lambda s: (0, 0))],
        out_shape=[jax.ShapeDtypeStruct((lp, w), F32), jax.ShapeDtypeStruct((lp, 8 * w + NARROW), BF16),
                   jax.ShapeDtypeStruct((1, LANES), F32)],
        name="merge_gdn_bwd", compiler_params=_cp("arbitrary"))(o_gdn, proj, norm_w, dmerged)


def _fox_prep(proj, qk_w, nh):
    lp = proj.shape[0]

    def body(x_ref, w_ref, o_ref):
        x = x_ref[...]
        o_ref[...] = x * _rms(x) * w_ref[0]

    return pl.pallas_call(
        body, grid=(2 * nh,),
        in_specs=[pl.BlockSpec((lp, LANES), lambda s: (0, 4 * nh + s)), pl.BlockSpec((1, 1, LANES), lambda s: (s // nh, 0, 0))],
        out_specs=pl.BlockSpec((lp, LANES), lambda s: (0, s)),
        out_shape=jax.ShapeDtypeStruct((lp, 2 * nh * HEAD_DIM), F32), name="fox_prep",
        compiler_params=_cp("parallel"))(proj, qk_w)


def _fox_prep_bwd(proj, qk_w, dq, dk, dproj, nh):
    lp = proj.shape[0]
    part = lambda p: pl.BlockSpec((lp, LANES), lambda s: (0, jnp.clip(s - p * nh, 0, nh - 1)))

    def body(x_ref, w_ref, dq_ref, dk_ref, _, dx_ref, dw_ref):
        x = x_ref[...]
        r = _rms(x)
        xh = x * r
        dy = jnp.where(pl.program_id(0) < nh, dq_ref[...], dk_ref[...])
        dyw = dy * w_ref[0]
        dx_ref[...] = (r * (dyw - xh * jnp.mean(dyw * xh, axis=-1, keepdims=True))).astype(BF16)

        @pl.when(pl.program_id(0) % nh == 0)
        def _():
            dw_ref[...] = jnp.zeros_like(dw_ref)
        dw_ref[0] += jnp.sum(dy * xh, axis=0, keepdims=True)

    strip = pl.BlockSpec((lp, LANES), lambda s: (0, 4 * nh + s))
    wsp = pl.BlockSpec((1, 1, LANES), lambda s: (s // nh, 0, 0))
    return pl.pallas_call(
        body, grid=(2 * nh,), in_specs=[strip, wsp, part(0), part(1), _ANY], out_specs=[strip, wsp],
        out_shape=[jax.ShapeDtypeStruct(dproj.shape, BF16), jax.ShapeDtypeStruct((2, 1, LANES), F32)],
        input_output_aliases={4: 0}, name="fox_prep_bwd", compiler_params=_cp("arbitrary"))(proj, qk_w, dq, dk, dproj)


def _fox_probs(q, k, gates, crow, h, i, nh):
    kl = k.shape[0]
    lane = _iota((Q_BLOCK, LANES), 1)
    ct = jnp.sum(jnp.where(lane == 4 * nh + h, gates, 0.0), axis=1, keepdims=True)
    s = _bdot(q, k, "nt") * (HEAD_DIM ** -0.5) + (ct - crow)
    t = i * Q_BLOCK + _iota((Q_BLOCK, kl), 0)
    kp = _iota((Q_BLOCK, kl), 1)
    s = jnp.where((kp <= t) & ((kp >= PAD_ROWS) | (t < PAD_ROWS)), s, NEG)
    p = jnp.exp(s - jnp.max(s, axis=1, keepdims=True))
    return p / jnp.sum(p, axis=1, keepdims=True)


FOX_HEADS_PER_STEP = 2


def _fox_specs(lp, nh):
    hw = FOX_HEADS_PER_STEP * LANES
    return [pl.BlockSpec((Q_BLOCK, hw), lambda g, i: (i, g)),
            pl.BlockSpec((lp, hw), lambda g, i: (0, nh // FOX_HEADS_PER_STEP + g)),
            pl.BlockSpec((lp, hw), lambda g, i: (0, 6 * nh // FOX_HEADS_PER_STEP + g)),
            pl.BlockSpec((Q_BLOCK, LANES), lambda g, i: (i, 0)),
            pl.BlockSpec((LANES, lp), lambda g, i: (0, 0))]


def _fox_fwd(qkn, proj, gates, gtf, nh):
    lp = qkn.shape[0]

    def body(q_ref, k_ref, v_ref, g_ref, gt_ref, o_ref):
        g, i = pl.program_id(0), pl.program_id(1)
        for j in range(lp // Q_BLOCK):
            @pl.when(i == j)
            def _(j=j):
                kl = (j + 1) * Q_BLOCK
                for hh in range(FOX_HEADS_PER_STEP):
                    h = FOX_HEADS_PER_STEP * g + hh
                    sl = slice(hh * LANES, (hh + 1) * LANES)
                    p = _fox_probs(q_ref[:, sl], k_ref[0:kl, sl], g_ref[...], gt_ref[pl.ds(4 * nh + h, 1), :][:, 0:kl],
                                   h, j, nh)
                    o_ref[:, sl] = _bdot(p, v_ref[0:kl, sl])

    return pl.pallas_call(
        body, grid=(nh // FOX_HEADS_PER_STEP, lp // Q_BLOCK), in_specs=_fox_specs(lp, nh),
        out_specs=pl.BlockSpec((Q_BLOCK, FOX_HEADS_PER_STEP * LANES), lambda g, i: (i, g)),
        out_shape=jax.ShapeDtypeStruct((lp, nh * HEAD_DIM), F32), name="fox_fwd",
        compiler_params=_cp("parallel", "parallel"))(qkn, qkn, proj, gates, gtf)


def _fox_bwd(qkn, proj, gates, gtf, do, dproj, nh):
    lp = qkn.shape[0]
    nq = lp // Q_BLOCK
    w = nh * HEAD_DIM
    scale = HEAD_DIM ** -0.5

    def body(q_ref, k_ref, v_ref, g_ref, gt_ref, do_ref, _, dq_ref, dk_ref, dc_ref, dv_ref, dv_scr):
        g, i = pl.program_id(0), pl.program_id(1)

        @pl.when(i == 0)
        def _():
            dk_ref[...] = jnp.zeros_like(dk_ref)
            dv_scr[...] = jnp.zeros_like(dv_scr)
            dc_ref[...] = jnp.zeros_like(dc_ref)
        for j in range(nq):
            @pl.when(i == j)
            def _(j=j):
                kl = (j + 1) * Q_BLOCK
                for hh in range(FOX_HEADS_PER_STEP):
                    h = FOX_HEADS_PER_STEP * g + hh
                    sl = slice(hh * LANES, (hh + 1) * LANES)
                    q, k = q_ref[:, sl], k_ref[0:kl, sl]
                    p = _fox_probs(q, k, g_ref[...], gt_ref[pl.ds(4 * nh + h, 1), :][:, 0:kl], h, j, nh)
                    dout = do_ref[:, sl]
                    dp = _bdot(dout, v_ref[0:kl, sl], "nt")
                    ds = p * (dp - jnp.sum(p * dp, axis=1, keepdims=True))
                    dq_ref[:, sl] = _bdot(ds, k) * scale
                    dk_ref[0:kl, sl] += _bdot(ds, q, "tn") * scale
                    dv_scr[0:kl, sl] += _bdot(p, dout, "tn")
                    dc_ref[hh, :, 0:kl] -= jnp.sum(ds, axis=0, keepdims=True)

        @pl.when(i == nq - 1)
        def _():
            dv_ref[...] = dv_scr[...].astype(BF16)

    hw = FOX_HEADS_PER_STEP * LANES
    blk = pl.BlockSpec((Q_BLOCK, hw), lambda g, i: (i, g))
    col = pl.BlockSpec((lp, hw), lambda g, i: (0, g))
    return pl.pallas_call(
        body, grid=(nh // FOX_HEADS_PER_STEP, nq), in_specs=_fox_specs(lp, nh) + [blk, _ANY],
        out_specs=[blk, col, pl.BlockSpec((FOX_HEADS_PER_STEP, 1, lp), lambda g, i: (g, 0, 0)),
                   pl.BlockSpec((lp, hw), lambda g, i: (0, 6 * nh // FOX_HEADS_PER_STEP + g))],
        out_shape=[jax.ShapeDtypeStruct((lp, w), F32)] * 2 + [jax.ShapeDtypeStruct((nh, 1, lp), F32),
                                                             jax.ShapeDtypeStruct(dproj.shape, BF16)],
        scratch_shapes=[pltpu.VMEM((lp, hw), F32)], input_output_aliases={6: 3},
        name="fox_bwd", compiler_params=_cp("parallel", "arbitrary"))(qkn, qkn, proj, gates, gtf, do, dproj)


def _merge_fox(o_fox, proj, merged, nh):
    lp = o_fox.shape[0]

    def body(o_ref, z_ref, _, m_ref):
        z = z_ref[...]
        m_ref[...] = (o_ref[...] * (z * _sigmoid(z))).astype(BF16)

    return pl.pallas_call(
        body, grid=(nh,),
        in_specs=[pl.BlockSpec((lp, LANES), lambda s: (0, s)), pl.BlockSpec((lp, LANES), lambda s: (0, 7 * nh + s)), _ANY],
        out_specs=pl.BlockSpec((lp, LANES), lambda s: (0, nh + s)),
        out_shape=jax.ShapeDtypeStruct(merged.shape, BF16), input_output_aliases={2: 0}, name="merge_fox",
        compiler_params=_cp("parallel"))(o_fox, proj, merged)


def _merge_fox_bwd(o_fox, proj, dmerged, dproj, nh):
    lp = o_fox.shape[0]

    def body(o_ref, z_ref, dm_ref, _, do_ref, dz_ref):
        silu, dsilu = _silu_and_grad(z_ref[...])
        dm = dm_ref[...]
        do_ref[...] = dm * silu
        dz_ref[...] = (dm * o_ref[...] * dsilu).astype(BF16)

    w = nh * HEAD_DIM
    return pl.pallas_call(
        body, grid=(nh,),
        in_specs=[pl.BlockSpec((lp, LANES), lambda s: (0, s)), pl.BlockSpec((lp, LANES), lambda s: (0, 7 * nh + s)),
                  pl.BlockSpec((lp, LANES), lambda s: (0, nh + s)), _ANY],
        out_specs=[pl.BlockSpec((lp, LANES), lambda s: (0, s)), pl.BlockSpec((lp, LANES), lambda s: (0, 7 * nh + s))],
        out_shape=[jax.ShapeDtypeStruct((lp, w), F32), jax.ShapeDtypeStruct(dproj.shape, BF16)],
        input_output_aliases={3: 1}, name="merge_fox_bwd", compiler_params=_cp("parallel"))(o_fox, proj, dmerged, dproj)


def _post(out, x, target, post_w):
    lp, d = out.shape

    def body(o_ref, x_ref, t_ref, w_ref, dy_ref, do_ref, loss_ref, dw_ref):
        i = pl.program_id(0)

        @pl.when(i == 0)
        def _():
            loss_ref[...] = jnp.zeros_like(loss_ref)
            dw_ref[...] = jnp.zeros_like(dw_ref)
        o = o_ref[...]
        r = _rms(o)
        nrm = o * r
        err = jnp.where(i > 0, x_ref[...] + nrm * w_ref[...] - t_ref[...], 0.0)
        loss_ref[0:1, :] += 0.5 * jnp.sum(jnp.sum(err * err, axis=1, keepdims=True), axis=0, keepdims=True) / d
        dy = err / d
        dy_ref[...] = dy
        dw_ref[...] += jnp.sum(dy * nrm, axis=0, keepdims=True)
        dyw = dy * w_ref[...]
        do_ref[...] = (r * (dyw - nrm * jnp.mean(dyw * nrm, axis=-1, keepdims=True))).astype(BF16)

    row = pl.BlockSpec((Q_BLOCK, d), lambda i: (i, 0))
    vec = pl.BlockSpec((1, d), lambda i: (0, 0))
    return pl.pallas_call(
        body, grid=(lp // Q_BLOCK,), in_specs=[row, _x_rows(d), _x_rows(d), vec],
        out_specs=[_x_rows(d), row, pl.BlockSpec((8, LANES), lambda i: (0, 0)), vec],
        out_shape=[jax.ShapeDtypeStruct(x.shape, F32), jax.ShapeDtypeStruct((lp, d), BF16),
                   jax.ShapeDtypeStruct((8, LANES), F32), jax.ShapeDtypeStruct((1, d), F32)],
        name="post", compiler_params=_cp("arbitrary"))(out, x, target, post_w)


def _prenorm_bwd(dxn, x, meta, w, dy):
    seq, d = x.shape
    lp = seq + Q_BLOCK

    def body(dx_ref, x_ref, m_ref, w_ref, dy_ref, gx_ref, gm_ref, dw_ref):
        i = pl.program_id(0)
        h = _h_tile(i, x_ref, m_ref)
        r = _rms(h)
        xh = h * r
        dxn_ = dx_ref[...]
        dxw = dxn_ * w_ref[...]
        dh = jnp.where(i > 0, dy_ref[...], 0.0) + r * (dxw - xh * jnp.mean(dxw * xh, axis=-1, keepdims=True))
        gx_ref[...] = dh

        @pl.when(i == 0)
        def _():
            dw_ref[...] = jnp.zeros_like(dw_ref)
            gm_ref[...] = dh[PAD_ROWS:, :]
        dw_ref[...] += jnp.sum(dxn_ * xh, axis=0, keepdims=True)

    vec = pl.BlockSpec((1, d), lambda i: (0, 0))
    met = pl.BlockSpec((N_META, d), lambda i: (0, 0))
    return pl.pallas_call(
        body, grid=(lp // Q_BLOCK,),
        in_specs=[pl.BlockSpec((Q_BLOCK, d), lambda i: (i, 0)), _x_rows(d), met, vec, _x_rows(d)],
        out_specs=[_x_rows(d), met, vec],
        out_shape=[jax.ShapeDtypeStruct((seq, d), F32), jax.ShapeDtypeStruct((N_META, d), F32),
                   jax.ShapeDtypeStruct((1, d), F32)],
        name="prenorm_bwd", compiler_params=_cp("arbitrary"))(dxn, x, meta, w, dy)


def _layer_grads(x, target, meta, pre_w, wfull, conv_w, a_log, dt_bias, gdn_norm_w, fq_w, fk_w, f_bias, w_out, post_w):
    nh = a_log.shape[1]
    conv_wt = conv_w.T
    zpad = jnp.zeros((1, LANES - 3 * nh), F32)
    bias_row = jnp.concatenate([jnp.zeros((1, nh), F32), dt_bias, f_bias, zpad], axis=1)
    nega_row = jnp.concatenate([jnp.zeros((1, nh), F32), -jnp.exp(a_log), jnp.zeros((1, nh), F32), zpad], axis=1)
    qk_w = jnp.stack([fq_w, fk_w])

    xn = _prenorm(x, meta, pre_w)
    proj = _matmul(xn, wfull, "nn", MM_TILE, F32, "proj")
    qkv = _gdn_prep(proj, conv_wt, nh)
    gates, gt3, gtf = _gates(proj, bias_row, nega_row, nh)
    o_gdn, s_all, t_all = _gdn_fwd(qkv, gates, gt3, nh)
    qkn = _fox_prep(proj, qk_w, nh)
    o_fox = _fox_fwd(qkn, proj, gates, gtf, nh)
    merged = _merge_fox(o_fox, proj, _merge_gdn(o_gdn, proj, gdn_norm_w, nh), nh)
    out = _matmul(merged, w_out, "nn", 4 * LANES, F32, "out_proj")
    dy, dout, loss_blk, dpost_w = _post(out, x, target, post_w)

    dw_out = _matmul(merged, dout, "tn", 4 * LANES, BF16, "dw_out")
    dmerged = _matmul(dout, w_out, "nt", 4 * LANES, F32, "dmerged")
    do_gdn, dproj, dgdn_norm_w = _merge_gdn_bwd(o_gdn, proj, gdn_norm_w, dmerged, nh)
    do_fox, dproj = _merge_fox_bwd(o_fox, proj, dmerged, dproj, nh)
    dqn, dkn, dc_t, dproj = _fox_bwd(qkn, proj, gates, gtf, do_fox, dproj, nh)
    dproj, dqk_w = _fox_prep_bwd(proj, qk_w, dqn, dkn, dproj, nh)
    dgq, dgk, dgv, dgate = _gdn_bwd(qkv, gates, gt3, s_all, t_all, do_gdn, nh)
    dproj, dconv_wt = _gdn_prep_bwd(proj, conv_wt, dgq, dgk, dgv, dproj, nh)
    dc_rows = jnp.pad(dc_t.reshape(nh, -1), ((2 * nh, LANES - 3 * nh), (0, 0)))
    dproj, gate_sums = _gates_bwd(proj, bias_row, nega_row, gates, dgate, dc_rows, dproj, nh)
    dwfull = _matmul(xn, dproj, "tn", MM_TILE, BF16, "dw_in")
    return dict(
        loss=loss_blk[0:1, 0:1], dy=dy, dproj=dproj, wfull=dwfull, post_w=dpost_w,
        conv_w=dconv_wt.T, a_log=gate_sums[1:2, nh:2 * nh], dt_bias=gate_sums[0:1, nh:2 * nh],
        gdn_norm_w=dgdn_norm_w, fq_w=dqk_w[0], fk_w=dqk_w[1], f_bias=gate_sums[0:1, 2 * nh:3 * nh], w_out=dw_out)


def _input_grads(g, wfull, x, meta, pre_w, sums):
    dxn, exchanged = _dxn_and_chip_exchange(g["dproj"], wfull, sums, MM_TILE, "dxn_chip_exchange")
    grad_x, dmeta, dpre_w = _prenorm_bwd(dxn, x, meta, pre_w, g["dy"])
    return grad_x, dmeta, dpre_w, exchanged


def _cast_bf16(a, tr, name):
    r, c = a.shape

    def body(a_ref, o_ref):
        o_ref[...] = a_ref[...].astype(BF16)

    return pl.pallas_call(
        body, grid=(r // tr,), in_specs=[pl.BlockSpec((tr, c), lambda i: (i, 0))],
        out_specs=pl.BlockSpec((tr, c), lambda i: (i, 0)), out_shape=jax.ShapeDtypeStruct((r, c), BF16),
        name=name, compiler_params=_cp("parallel"))(a)


def _gather_copies(ins, outs, send_sems, recv_sems, local_sems):
    n = len(ins)
    x, y, c = lax.axis_index("x"), lax.axis_index("y"), lax.axis_index("c")
    me, sibling = (x, y, c), (x, y, 1 - c)
    xn, yn, dg = (1 - x, y), (x, 1 - y), (1 - x, 1 - y)

    def copy(a, k, block, to, src=None):
        px, py, pc = block
        rows = outs[a].at[4 * px + 2 * py + pc]
        return pltpu.make_async_remote_copy(
            src_ref=rows if src is None else src, dst_ref=rows, send_sem=send_sems.at[a, k],
            recv_sem=recv_sems.at[a, k], device_id=to, device_id_type=_MESH)

    local = [pltpu.make_async_copy(ins[a], outs[a].at[4 * x + 2 * y + c], local_sems.at[a]) for a in range(n)]
    own = [cp for a in range(n) for cp in (copy(a, 0, me, sibling, src=ins[a]), copy(a, 1, me, (*xn, c), src=ins[a]),
                                           copy(a, 2, me, (*yn, c), src=ins[a]))]

    def start():
        for cp in local + own:
            cp.start()

    def finish():
        for a in range(n):
            @pl.when(c == 1)
            def _(a=a):
                copy(a, 1, (*xn, c), me).wait_recv()
                copy(a, 3, (*xn, c), (*yn, c)).start()

            @pl.when(c == 0)
            def _(a=a):
                copy(a, 2, (*yn, c), me).wait_recv()
                copy(a, 3, (*yn, c), (*xn, c)).start()
        for a in range(n):
            pl.when(c == 0)(copy(a, 1, (*xn, c), me).wait_recv)
            copy(a, 4, (*xn, c), sibling).start()
            pl.when(c == 1)(copy(a, 2, (*yn, c), me).wait_recv)
            copy(a, 5, (*yn, c), sibling).start()
        for a in range(n):
            copy(a, 3, (*dg, c), me).wait_recv()
            copy(a, 6, (*dg, c), sibling).start()
        for a in range(n):
            copy(a, 0, sibling, me).wait_recv()
            for k, chip in ((4, xn), (5, yn), (6, dg)):
                copy(a, k, (*chip, 1 - c), me).wait_recv()
                copy(a, k, (*chip, c), sibling).wait_send()
            copy(a, 3, (*xn, c), (*yn, c)).wait_send()
        for cp in own:
            cp.wait_send()
        for cp in local:
            cp.wait()

    return start, finish


def _gather_scratch(n):
    return [pltpu.SemaphoreType.DMA((n, N_DEV - 1)), pltpu.SemaphoreType.DMA((n, N_DEV - 1)), pltpu.SemaphoreType.DMA((n,))]


def _all_gather(arrays, name):
    n = len(arrays)

    def body(*refs):
        start, finish = _gather_copies(refs[:n], refs[n:2 * n], *refs[2 * n:])
        start()
        finish()

    return pl.pallas_call(
        body, in_specs=[_ANY] * n, out_specs=[_ANY] * n,
        out_shape=[jax.ShapeDtypeStruct((N_DEV,) + a.shape, a.dtype) for a in arrays],
        scratch_shapes=_gather_scratch(n), name=name)(*arrays)


SLAB = 10 * LANES


def _slab_start(blk, nh, cols):
    in_second_half = blk >= N_DEV // 2
    shift = (2 * nh if in_second_half else 0) if isinstance(blk, int) else jnp.where(in_second_half, 2 * nh, 0)
    return (blk * cols - shift) // LANES * LANES


def _pair_exchange(dwfull, parts, small, nh, name):
    d, full = dwfull.shape
    ns = len(small)
    cols = (full - NARROW + 3 * nh) // N_DEV

    def body(f_ref, p_ref, *refs):
        sm_in, (slab_out, tail_out, p_out), sm_out = refs[:ns], refs[ns:ns + 3], refs[ns + 3:2 * ns + 3]
        send_sems, recv_sems = refs[2 * ns + 3:2 * ns + 5]
        x, y, c = lax.axis_index("x"), lax.axis_index("y"), lax.axis_index("c")
        kw = dict(device_id=(x, y, 1 - c), device_id_type=_MESH)
        copies = []
        for q in range(N_CHIP):
            start = pl.multiple_of(_slab_start(2 * q + 1 - c, nh, cols), LANES)
            copies.append(pltpu.make_async_remote_copy(
                src_ref=f_ref.at[:, pl.ds(start, SLAB)], dst_ref=slab_out.at[q], send_sem=send_sems.at[0, q],
                recv_sem=recv_sems.at[0, q], **kw))
            copies.append(pltpu.make_async_remote_copy(
                src_ref=p_ref.at[2 * q + 1 - c], dst_ref=p_out.at[q], send_sem=send_sems.at[1, q],
                recv_sem=recv_sems.at[1, q], **kw))
        copies.append(pltpu.make_async_remote_copy(
            src_ref=f_ref.at[:, pl.ds(full - NARROW, NARROW)], dst_ref=tail_out, send_sem=send_sems.at[2, 0],
            recv_sem=recv_sems.at[2, 0], **kw))
        start_small, finish_small = _gather_copies(sm_in, sm_out, *refs[2 * ns + 5:])
        for cp in copies:
            cp.start()
        start_small()
        finish_small()
        for cp in copies:
            cp.wait()

    return pl.pallas_call(
        body, in_specs=[_ANY] * (2 + ns), out_specs=[_ANY] * (3 + ns),
        out_shape=([jax.ShapeDtypeStruct((N_CHIP, d, SLAB), dwfull.dtype), jax.ShapeDtypeStruct((d, NARROW), dwfull.dtype),
                    jax.ShapeDtypeStruct((N_CHIP,) + parts.shape[1:], parts.dtype)]
                   + [jax.ShapeDtypeStruct((N_DEV,) + a.shape, a.dtype) for a in small]),
        scratch_shapes=[pltpu.SemaphoreType.DMA((3, N_CHIP)), pltpu.SemaphoreType.DMA((3, N_CHIP))] + _gather_scratch(ns),
        name=name)(dwfull, parts, *small)


def _relayout_pair_sum(dwfull, got_slabs, got_tail, core, nh, tr):
    d, full = dwfull.shape
    w = nh * HEAD_DIM
    cols = (8 * w + 3 * nh) // N_DEV
    segs = _native_segments(nh)

    def block(f_ref, s_ref, t_ref, q, blk):
        st = _slab_start(blk, nh, cols)
        wide = f_ref[:, st:st + SLAB].astype(F32) + s_ref[q].astype(F32)
        tail = f_ref[:, 8 * w:].astype(F32) + t_ref[...].astype(F32)
        pieces = []
        for s0, s1, t0 in segs:
            lo, hi = max(s0, blk * cols), min(s1, (blk + 1) * cols)
            if lo < hi:
                at = t0 + lo - s0
                pieces.append(tail[:, at - 8 * w:at - 8 * w + hi - lo] if at >= 8 * w else wide[:, at - st:at - st + hi - lo])
        return (pieces[0] if len(pieces) == 1 else jnp.concatenate(pieces, axis=1)).astype(dwfull.dtype)

    def body(core_ref, f_ref, s_ref, t_ref, o_ref):
        for parity in range(2):
            @pl.when(core_ref[0] == parity)
            def _(parity=parity):
                for q in range(N_CHIP):
                    o_ref[q] = block(f_ref, s_ref, t_ref, q, 2 * q + parity)

    return pl.pallas_call(
        body,
        grid_spec=pltpu.PrefetchScalarGridSpec(
            num_scalar_prefetch=1, grid=(d // tr,),
            in_specs=[pl.BlockSpec((tr, full), lambda i, c_ref: (i, 0)), pl.BlockSpec((N_CHIP, tr, SLAB), lambda i, c_ref: (0, i, 0)),
                      pl.BlockSpec((tr, NARROW), lambda i, c_ref: (i, 0))],
            out_specs=pl.BlockSpec((N_CHIP, tr, cols), lambda i, c_ref: (0, i, 0))),
        out_shape=jax.ShapeDtypeStruct((N_CHIP, d, cols), dwfull.dtype), name="relayout_pair_sum",
        compiler_params=_cp("parallel"))(core, dwfull, got_slabs, got_tail)


def _pair_sum(parts, got, core, tr, name):
    _, r, c = parts.shape

    def body(core_ref, p_ref, g_ref, o_ref):
        o_ref[...] = (p_ref[...].astype(F32) + g_ref[...].astype(F32)).astype(o_ref.dtype)

    return pl.pallas_call(
        body,
        grid_spec=pltpu.PrefetchScalarGridSpec(
            num_scalar_prefetch=1, grid=(N_CHIP, r // tr),
            in_specs=[pl.BlockSpec((1, tr, c), lambda q, i, core_ref: (2 * q + core_ref[0], i, 0)),
                      pl.BlockSpec((1, tr, c), lambda q, i, core_ref: (q, i, 0))],
            out_specs=pl.BlockSpec((1, tr, c), lambda q, i, core_ref: (q, i, 0))),
        out_shape=jax.ShapeDtypeStruct((N_CHIP, r, c), parts.dtype), name=name,
        compiler_params=_cp("parallel", "parallel"))(core, parts, got)


def _native_segments(nh):
    w = nh * HEAD_DIM
    return [(0, 4 * w, 0), (4 * w, 4 * w + 2 * nh, 8 * w), (4 * w + 2 * nh, 8 * w + 2 * nh, 4 * w),
            (8 * w + 2 * nh, 8 * w + 3 * nh, 8 * w + 2 * nh)]


def _relayout_w_in(wg, nh, tr):
    _, d, cols = wg.shape
    w = nh * HEAD_DIM

    def native(ref, j0, j1):
        out = []
        while j0 < j1:
            blk = j0 // cols
            end = min(j1, (blk + 1) * cols)
            out.append(ref[blk, :, pl.ds(j0 - blk * cols, end - j0)])
            j0 = end
        return out

    def body(g_ref, o_ref):
        for cidx in range(8 * w // LANES):
            j0 = cidx * LANES + (0 if cidx * LANES < 4 * w else 2 * nh)
            pieces = native(g_ref, j0, j0 + LANES)
            o_ref[:, cidx * LANES:(cidx + 1) * LANES] = pieces[0] if len(pieces) == 1 else jnp.concatenate(pieces, axis=1)
        pieces = (native(g_ref, 4 * w, 4 * w + 2 * nh) + native(g_ref, 8 * w + 2 * nh, 8 * w + 3 * nh)
                  + [jnp.zeros((tr, NARROW - 3 * nh), wg.dtype)])
        o_ref[:, 8 * w:] = jnp.concatenate(pieces, axis=1)

    return pl.pallas_call(
        body, grid=(d // tr,), in_specs=[pl.BlockSpec((N_DEV, tr, cols), lambda i: (0, i, 0))],
        out_specs=pl.BlockSpec((tr, 8 * w + NARROW), lambda i: (i, 0)),
        out_shape=jax.ShapeDtypeStruct((d, 8 * w + NARROW), wg.dtype),
        name="relayout_w_in", compiler_params=_cp("parallel"))(wg)


def _adamw(w, parts, m, v, tr, name):
    r, c = w.shape
    n_parts = parts.shape[0]

    def body(w_ref, p_ref, m_ref, v_ref, g_ref, d_ref, nm_ref, nv_ref):
        g = p_ref[0].astype(F32)
        for s in range(1, n_parts):
            g = g + p_ref[s].astype(F32)
        m_new = ADAM_B1 * m_ref[...] + (1.0 - ADAM_B1) * g
        v_new = ADAM_B2 * v_ref[...] + (1.0 - ADAM_B2) * (g * g)
        m_hat = m_new / (1.0 - ADAM_B1 ** ADAM_STEP)
        v_hat = v_new / (1.0 - ADAM_B2 ** ADAM_STEP)
        g_ref[...] = g
        d_ref[...] = -ADAM_LR * (m_hat / (jnp.sqrt(v_hat) + ADAM_EPS) + ADAM_WD * w_ref[...])
        nm_ref[...] = m_new
        nv_ref[...] = v_new

    blk = pl.BlockSpec((tr, c), lambda i: (i, 0))
    return pl.pallas_call(
        body, grid=(r // tr,), in_specs=[blk, pl.BlockSpec((n_parts, tr, c), lambda i: (0, i, 0)), blk, blk],
        out_specs=[blk] * 4, out_shape=[jax.ShapeDtypeStruct((r, c), F32)] * 4, name=name,
        compiler_params=_cp("parallel"))(w, parts, m, v)


def _pack_small(d, pre, post, a_log, dt_bias, f_bias, gdn_w, fq_w, fk_w, extra):
    row2 = jnp.concatenate([a_log, dt_bias, f_bias, gdn_w, fq_w, fk_w, extra], axis=1)
    row2 = jnp.pad(row2, ((0, 0), (0, d - row2.shape[1])))
    return jnp.concatenate([pre, post, row2, jnp.zeros((5, d), F32)], axis=0)


def _unpack_small(p, nh):
    o = 3 * nh
    return dict(pre=p[0:1], post=p[1:2], a_log=p[2:3, 0:nh], dt_bias=p[2:3, nh:2 * nh], f_bias=p[2:3, 2 * nh:o],
                gdn_w=p[2:3, o:o + HEAD_DIM], fq_w=p[2:3, o + HEAD_DIM:o + 2 * HEAD_DIM],
                fk_w=p[2:3, o + 2 * HEAD_DIM:o + 3 * HEAD_DIM], extra=p[2, o + 3 * HEAD_DIM])


def kernel(x, meta_tokens, pre_norm_w, w_in, conv_w, a_log, dt_bias, gdn_norm_w, fox_q_norm_w, fox_k_norm_w, fox_f_bias, w_out, post_norm_w, loss_target, m_meta_tokens, m_pre_norm_w, m_w_in, m_conv_w, m_a_log, m_dt_bias, m_gdn_norm_w, m_fox_q_norm_w, m_fox_k_norm_w, m_fox_f_bias, m_w_out, m_post_norm_w, v_meta_tokens, v_pre_norm_w, v_w_in, v_conv_w, v_a_log, v_dt_bias, v_gdn_norm_w, v_fox_q_norm_w, v_fox_k_norm_w, v_fox_f_bias, v_w_out, v_post_norm_w):
    nh = a_log.shape[1]
    d = x.shape[-1]
    w = nh * HEAD_DIM
    zero = jnp.zeros((1, 1), F32)

    wg, wog, cg, mg = _all_gather(
        [_cast_bf16(w_in[0], 256, "cast_w_in"), _cast_bf16(w_out[0], 256, "cast_w_out"), conv_w[0], meta_tokens],
        "gather_weights")
    wfull = _relayout_w_in(wg, nh, 256)
    meta_full = mg.transpose(1, 0, 2).reshape(N_META, d)
    g = _layer_grads(x[0], loss_target[0], meta_full, pre_norm_w, wfull, cg.reshape(3 * w, CONV_WIDTH), a_log, dt_bias,
                     gdn_norm_w, fox_q_norm_w, fox_k_norm_w, fox_f_bias, wog.reshape(2 * w, d), post_norm_w)

    core = lax.axis_index("c")
    dev = 4 * lax.axis_index("x") + 2 * lax.axis_index("y") + core
    core_arr = jnp.reshape(core, (1,)).astype(jnp.int32)
    mine_out = g["w_out"].reshape(N_DEV, 2 * w // N_DEV, d)
    got_slabs, got_tail, got_out, a_conv = _pair_exchange(g["wfull"], mine_out, [g["conv_w"]], nh, "pair_exchange")
    sums = [_relayout_pair_sum(g["wfull"], got_slabs, got_tail, core_arr, nh, 128),
            _pair_sum(mine_out, got_out, core_arr, 256, "pair_sum_w_out")]
    grad_x, dmeta, dpre_w, (p_in, p_out) = _input_grads(g, wfull, x[0], meta_full, pre_norm_w, sums)
    small = _pack_small(d, dpre_w, g["post_w"], g["a_log"], g["dt_bias"], g["f_bias"], g["gdn_norm_w"], g["fq_w"],
                        g["fk_w"], g["loss"])
    a_meta, p_small = _all_gather([dmeta, small], "gather_small_grads")
    p_conv = lax.dynamic_slice_in_dim(a_conv, dev * conv_w.shape[1], conv_w.shape[1], axis=1)
    p_meta = lax.dynamic_slice_in_dim(a_meta, dev * meta_tokens.shape[1], meta_tokens.shape[1], axis=2)

    r_in = _adamw(w_in[0], p_in, m_w_in[0], v_w_in[0], 128, "adamw_w_in")
    r_out = _adamw(w_out[0], p_out, m_w_out[0], v_w_out[0], 64, "adamw_w_out")
    r_conv = _adamw(conv_w[0], p_conv, m_conv_w[0], v_conv_w[0], conv_w.shape[1], "adamw_conv_w")
    r_meta = _adamw(meta_tokens, p_meta, m_meta_tokens, v_meta_tokens, N_META, "adamw_meta")
    pk = lambda pre, post, a, dt, gw, fq, fk, fb: _pack_small(d, pre, post, a, dt, fb, gw, fq, fk, zero)
    r_small = _adamw(
        pk(pre_norm_w, post_norm_w, a_log, dt_bias, gdn_norm_w, fox_q_norm_w, fox_k_norm_w, fox_f_bias), p_small,
        pk(m_pre_norm_w, m_post_norm_w, m_a_log, m_dt_bias, m_gdn_norm_w, m_fox_q_norm_w, m_fox_k_norm_w, m_fox_f_bias),
        pk(v_pre_norm_w, v_post_norm_w, v_a_log, v_dt_bias, v_gdn_norm_w, v_fox_q_norm_w, v_fox_k_norm_w, v_fox_f_bias),
        8, "adamw_small")

    sm = [_unpack_small(r, nh) for r in r_small]
    outs = []
    for i in range(4):
        s = sm[i]
        outs += [r_meta[i], s["pre"], r_in[i][None], r_conv[i][None], s["a_log"], s["dt_bias"], s["gdn_w"], s["fq_w"],
                 s["fk_w"], s["f_bias"], r_out[i][None], s["post"]]
    return (sm[0]["extra"], grad_x[None], *outs)
```

```python
import jax
import jax.numpy as jnp
from jax import lax
from jax.experimental import pallas as pl
from jax.experimental.pallas import tpu as pltpu

F32, BF16 = jnp.float32, jnp.bfloat16
HEAD_DIM = 128
N_META = 16
CONV_WIDTH = 4
CHUNK = 64
Q_BLOCK = 128
LANES = 128
EPS = 1e-6
PAD_ROWS = Q_BLOCK - N_META
N_DEV = 8
N_CHIP = 4
VMEM_LIMIT = 56 * 1024 * 1024
NEG = -1e30
NARROW = 2 * LANES
MM_TILE = 6 * LANES

ADAM_LR, ADAM_B1, ADAM_B2, ADAM_EPS, ADAM_WD, ADAM_STEP = 0.001, 0.9, 0.999, 1e-08, 0.01, 10

_DN = {"nn": (((1,), (0,)), ((), ())), "nt": (((1,), (1,)), ((), ())), "tn": (((0,), (0,)), ((), ()))}
_DN3 = {"nn": (((2,), (1,)), ((0,), (0,))), "nt": (((2,), (2,)), ((0,), (0,))), "tn": (((1,), (1,)), ((0,), (0,)))}
_ANY = pl.BlockSpec(memory_space=pl.ANY)
_MESH = pl.DeviceIdType.MESH


def _cp(*sem):
    return pltpu.CompilerParams(dimension_semantics=sem, vmem_limit_bytes=VMEM_LIMIT)


def _dot(a, b, dims="nn", prec=None):
    return lax.dot_general(a, b, _DN[dims], precision=prec, preferred_element_type=F32)


def _bdot(a, b, dims="nn"):
    return _dot(a.astype(BF16), b.astype(BF16), dims)


def _hdot(a, b, dims="nn"):
    return _dot(a, b, dims, prec=lax.Precision.HIGHEST)


def _dot3(a, b, dims="nn"):
    return lax.dot_general(a, b, _DN3[dims], preferred_element_type=F32)


def _bdot3(a, b, dims="nn"):
    return _dot3(a.astype(BF16), b.astype(BF16), dims)


def _split(a):
    hi = a.astype(BF16)
    return hi, (a - hi.astype(F32)).astype(BF16)


def _iota(shape, dim):
    return lax.broadcasted_iota(jnp.int32, shape, dim)


def _sigmoid(z):
    return 1.0 / (1.0 + jnp.exp(-z))


def _softplus(z):
    e = jnp.exp(-jnp.abs(z))
    u = 1.0 + e
    l1p = jnp.where(u == 1.0, e, jnp.log(u) * (e / jnp.where(u == 1.0, 1.0, u - 1.0)))
    return jnp.maximum(z, 0.0) + l1p


def _silu_and_grad(z):
    s = _sigmoid(z)
    return z * s, s * (1.0 + z * (1.0 - s))


def _rms(x):
    return lax.rsqrt(jnp.mean(x * x, axis=-1, keepdims=True) + EPS)


def _h_tile(i, x_ref, meta_ref):
    first = jnp.concatenate([jnp.zeros((PAD_ROWS, x_ref.shape[1]), F32), meta_ref[...]], axis=0)
    return jnp.where(i == 0, first, x_ref[...])


def _x_rows(d):
    return pl.BlockSpec((Q_BLOCK, d), lambda i: (jnp.maximum(i - 1, 0), 0))


def _prenorm(x, meta, w):
    seq, d = x.shape
    lp = seq + Q_BLOCK

    def body(x_ref, m_ref, w_ref, o_ref):
        h = _h_tile(pl.program_id(0), x_ref, m_ref)
        o_ref[...] = (h * _rms(h) * w_ref[...]).astype(BF16)

    return pl.pallas_call(
        body, grid=(lp // Q_BLOCK,),
        in_specs=[_x_rows(d), pl.BlockSpec((N_META, d), lambda i: (0, 0)), pl.BlockSpec((1, d), lambda i: (0, 0))],
        out_specs=pl.BlockSpec((Q_BLOCK, d), lambda i: (i, 0)),
        out_shape=jax.ShapeDtypeStruct((lp, d), BF16), name="prenorm", compiler_params=_cp("parallel"))(x, meta, w)


def _tile(n, want):
    return max(t for t in range(LANES, want + 1, LANES) if n % t == 0)


class _Rider:
    def __init__(self, inputs, out_shapes, scratch, aliases, make):
        self.inputs, self.out_shapes, self.scratch, self.aliases, self.make = inputs, out_shapes, scratch, aliases, make


def _hosted_call(body, riders, n_in, n_out, n_scratch, *, in_specs, out_specs, out_shape, scratch_shapes=(), aliases=None,
                 **kw):
    riders = [r for r in riders if r is not None]
    r_in = [len(r.inputs) for r in riders]
    r_out = [len(r.out_shapes) for r in riders]
    r_scr = [len(r.scratch) for r in riders]
    al = dict(aliases or {})
    for k, r in enumerate(riders):
        al.update({n_in + sum(r_in[:k]) + i: n_out + sum(r_out[:k]) + o for i, o in r.aliases.items()})

    def full_body(*refs):
        ins, rest = refs[:n_in + sum(r_in)], refs[n_in + sum(r_in):]
        outs, scr = rest[:n_out + sum(r_out)], rest[n_out + sum(r_out):]
        hooks = [r.make(ins[n_in + sum(r_in[:k]):n_in + sum(r_in[:k + 1])], outs[n_out + sum(r_out[:k]):n_out + sum(r_out[:k + 1])],
                        scr[n_scratch + sum(r_scr[:k]):n_scratch + sum(r_scr[:k + 1])]) for k, r in enumerate(riders)]

        def start():
            for h in hooks:
                h[0]()

        def finish():
            for h in hooks:
                h[1]()

        body(start, finish, *ins[:n_in], *outs[:n_out], *scr[:n_scratch])

    call = pl.pallas_call(
        full_body, in_specs=list(in_specs) + [_ANY] * sum(r_in), out_specs=list(out_specs) + [_ANY] * sum(r_out),
        out_shape=list(out_shape) + [s for r in riders for s in r.out_shapes],
        scratch_shapes=list(scratch_shapes) + [s for r in riders for s in r.scratch], input_output_aliases=al, **kw)

    def run(*args):
        res = call(*args, *[t for r in riders for t in r.inputs])
        return res[:n_out], [res[n_out + sum(r_out[:k]):n_out + sum(r_out[:k + 1])] for k in range(len(riders))]

    return run


def _matmul(a, b, dims, tn, out_dtype, name, rider=None):
    m = a.shape[1] if dims == "tn" else a.shape[0]
    n = b.shape[0] if dims == "nt" else b.shape[1]
    kdim = b.shape[1] if dims == "nt" else b.shape[0]
    tn = _tile(n, tn)
    steps = n // tn
    b_spec = pl.BlockSpec((tn, kdim), lambda j: (j, 0)) if dims == "nt" else pl.BlockSpec((kdim, tn), lambda j: (0, j))

    def body(start, finish, a_ref, b_ref, o_ref):
        pl.when(pl.program_id(0) == 0)(start)
        o_ref[...] = _dot(a_ref[...], b_ref[...], dims).astype(out_dtype)
        pl.when(pl.program_id(0) == steps - 1)(finish)

    (out,), got = _hosted_call(
        body, [rider], 2, 1, 0, grid=(steps,), in_specs=[pl.BlockSpec(a.shape, lambda j: (0, 0)), b_spec],
        out_specs=[pl.BlockSpec((m, tn), lambda j: (0, j))], out_shape=[jax.ShapeDtypeStruct((m, n), out_dtype)],
        name=name, compiler_params=_cp("parallel" if rider is None else "arbitrary"))(a, b)
    return out if rider is None else (out, got[0])


def _chip_rider(sums, rows_total=None, row0=0, into=None):
    _, r, c = sums.shape
    rows_total = rows_total or r

    def make(ins, outs, scratch):
        send_sems, recv_sems, local_sem = scratch
        x, y, core = lax.axis_index("x"), lax.axis_index("y"), lax.axis_index("c")
        mine = 2 * x + y
        land = lambda chip: outs[0].at[chip].at[pl.ds(row0, r)]
        local = pltpu.make_async_copy(ins[0].at[mine], land(mine), local_sem)
        sends, recvs = [], []
        for k in range(1, N_CHIP):
            px = 1 - x if k & 2 else x
            py = 1 - y if k & 1 else y
            kw = dict(send_sem=send_sems.at[k - 1], recv_sem=recv_sems.at[k - 1], device_id=(px, py, core),
                      device_id_type=_MESH)
            sends.append(pltpu.make_async_remote_copy(src_ref=ins[0].at[2 * px + py], dst_ref=land(mine), **kw))
            recvs.append(pltpu.make_async_remote_copy(src_ref=ins[0].at[mine], dst_ref=land(2 * px + py), **kw))

        def start():
            for cp in [local] + sends:
                cp.start()

        def finish():
            local.wait()
            for cp in sends:
                cp.wait_send()
            for cp in recvs:
                cp.wait_recv()

        return start, finish

    return _Rider([sums] + ([] if into is None else [into]), [jax.ShapeDtypeStruct((N_CHIP, rows_total, c), sums.dtype)],
                  [pltpu.SemaphoreType.DMA((N_CHIP - 1,)), pltpu.SemaphoreType.DMA((N_CHIP - 1,)), pltpu.SemaphoreType.DMA(())],
                  {} if into is None else {1: 0}, make)


def _dxn(dproj, wfull, riders, tk, name):
    m, k = dproj.shape
    n = wfull.shape[0]
    tk = _tile(k, tk)
    steps = k // tk

    def body(start, finish, a_ref, b_ref, o_ref):
        j = pl.program_id(0)

        @pl.when(j == 0)
        def _():
            start()
            o_ref[...] = jnp.zeros_like(o_ref)
        o_ref[...] += _dot(a_ref[...], b_ref[...], "nt")
        pl.when(j == steps - 1)(finish)

    (dxn,), got = _hosted_call(
        body, riders, 2, 1, 0, grid=(steps,),
        in_specs=[pl.BlockSpec((m, tk), lambda j: (0, j)), pl.BlockSpec((n, tk), lambda j: (0, j))],
        out_specs=[pl.BlockSpec((m, n), lambda j: (0, 0))], out_shape=[jax.ShapeDtypeStruct((m, n), F32)],
        name=name, compiler_params=_cp("arbitrary"))(dproj, wfull)
    return dxn, got


def _conv_taps(x, w):
    c = x * w[CONV_WIDTH - 1:CONV_WIDTH, :]
    for j in range(CONV_WIDTH - 1):
        c = c + pltpu.roll(x, CONV_WIDTH - 1 - j, 0) * w[j:j + 1, :]
    return c


def _gdn_prep(proj, conv_wt, nh):
    lp = proj.shape[0]
    scale = HEAD_DIM ** -0.5

    def body(x_ref, w_ref, o_ref):
        which = pl.program_id(0) // nh
        c = _conv_taps(x_ref[...], w_ref[...])
        s = c * _sigmoid(c)
        r = lax.rsqrt(jnp.sum(s * s, axis=-1, keepdims=True) + EPS)
        f = jnp.where(which == 0, r * scale, jnp.where(which == 1, r, 1.0))
        o_ref[...] = jnp.where(_iota(s.shape, 0) >= PAD_ROWS, s * f, 0.0)

    return pl.pallas_call(
        body, grid=(3 * nh,),
        in_specs=[pl.BlockSpec((lp, LANES), lambda s: (0, s)), pl.BlockSpec((CONV_WIDTH, LANES), lambda s: (0, s))],
        out_specs=pl.BlockSpec((lp, LANES), lambda s: (0, s)),
        out_shape=jax.ShapeDtypeStruct((lp, 3 * nh * HEAD_DIM), F32), name="gdn_prep",
        compiler_params=_cp("parallel"))(proj, conv_wt)


def _gdn_prep_bwd(proj, conv_wt, dq, dk, dv, dproj, nh):
    lp = proj.shape[0]
    scale = HEAD_DIM ** -0.5
    part = lambda p: pl.BlockSpec((lp, LANES), lambda s: (0, jnp.clip(s - p * nh, 0, nh - 1)))

    def body(x_ref, w_ref, dq_ref, dk_ref, dv_ref, _, dx_ref, dw_ref):
        which = pl.program_id(0) // nh
        x = x_ref[...]
        w = w_ref[...]
        c = _conv_taps(x, w)
        sg = _sigmoid(c)
        s = c * sg
        r = lax.rsqrt(jnp.sum(s * s, axis=-1, keepdims=True) + EPS)
        dy = jnp.where(which == 0, dq_ref[...], jnp.where(which == 1, dk_ref[...], dv_ref[...]))
        dy = jnp.where(_iota(s.shape, 0) >= PAD_ROWS, dy, 0.0)
        y0 = s * r
        dy0 = dy * jnp.where(which == 0, scale, 1.0)
        ds_n = r * (dy0 - y0 * jnp.sum(dy0 * y0, axis=-1, keepdims=True))
        ds = jnp.where(which == 2, dy, ds_n)
        dc = ds * (sg * (1.0 + c * (1.0 - sg)))
        dx = dc * w[CONV_WIDTH - 1:CONV_WIDTH, :]
        rows = [jnp.sum(dc * x, axis=0, keepdims=True)]
        for j in range(CONV_WIDTH - 2, -1, -1):
            sh = CONV_WIDTH - 1 - j
            dx = dx + pltpu.roll(dc, lp - sh, 0) * w[j:j + 1, :]
            rows.insert(0, jnp.sum(dc * pltpu.roll(x, sh, 0), axis=0, keepdims=True))
        dx_ref[...] = dx.astype(BF16)
        dw_ref[...] = jnp.concatenate(rows, axis=0)

    strip = pl.BlockSpec((lp, LANES), lambda s: (0, s))
    taps = pl.BlockSpec((CONV_WIDTH, LANES), lambda s: (0, s))
    return pl.pallas_call(
        body, grid=(3 * nh,), in_specs=[strip, taps, part(0), part(1), part(2), _ANY], out_specs=[strip, taps],
        out_shape=[jax.ShapeDtypeStruct(dproj.shape, BF16), jax.ShapeDtypeStruct((CONV_WIDTH, 3 * nh * HEAD_DIM), F32)],
        input_output_aliases={5: 0}, name="gdn_prep_bwd", compiler_params=_cp("parallel"))(proj, conv_wt, dq, dk, dv, dproj)


def _gates(proj, bias_row, nega_row, nh):
    lp = proj.shape[0]
    nc = lp // CHUNK

    def body(p_ref, b_ref, a_ref, g_ref, gt3_ref, gtf_ref):
        lane = _iota((CHUNK, LANES), 1)
        tri = (_iota((CHUNK, CHUNK), 0) >= _iota((CHUNK, CHUNK), 1)).astype(F32)

        def step(n, carry):
            r0 = pl.multiple_of(n * CHUNK, CHUNK)
            z = p_ref[pl.ds(r0, CHUNK), :] + b_ref[...]
            base = jnp.where(lane < nh, _sigmoid(z),
                             jnp.where(lane < 2 * nh, a_ref[...] * _softplus(z),
                                       jnp.where(lane < 3 * nh, -_softplus(-z), 0.0)))
            base = jnp.where(r0 + _iota((CHUNK, LANES), 0) >= PAD_ROWS, base, 0.0)
            cs = _hdot(tri, base)
            run = jnp.where((lane >= 2 * nh) & (lane < 3 * nh), cs + carry, cs)
            sh = pltpu.roll(run, 2 * nh, 1)
            out = base + jnp.where((lane >= 3 * nh) & (lane < 5 * nh), sh, 0.0)
            g_ref[pl.ds(r0, CHUNK), :] = out
            gt3_ref[n] = out.T
            return carry + cs[CHUNK - 1:CHUNK, :]

        lax.fori_loop(0, nc, step, jnp.zeros((1, LANES), F32))
        gtf_ref[...] = g_ref[...].T

    vec = pl.BlockSpec((1, LANES), lambda i: (0, 0))
    return pl.pallas_call(
        body, grid=(1,), in_specs=[pl.BlockSpec((lp, LANES), lambda i: (0, 8 * nh)), vec, vec],
        out_specs=[pl.BlockSpec((lp, LANES), lambda i: (0, 0)), pl.BlockSpec((nc, LANES, CHUNK), lambda i: (0, 0, 0)),
                   pl.BlockSpec((LANES, lp), lambda i: (0, 0))],
        out_shape=[jax.ShapeDtypeStruct((lp, LANES), F32), jax.ShapeDtypeStruct((nc, LANES, CHUNK), F32),
                   jax.ShapeDtypeStruct((LANES, lp), F32)],
        name="gates", compiler_params=_cp("arbitrary"))(proj, bias_row, nega_row)


def _gates_bwd(proj, bias_row, nega_row, gates, dgate_gdn, dc_t, dproj, nh):
    lp = proj.shape[0]
    nc = lp // CHUNK

    def body(p_ref, b_ref, a_ref, g_ref, dg_ref, dc_ref, _, dz_ref, sm_ref, dct_scr):
        lane = _iota((CHUNK, LANES), 1)
        triu = (_iota((CHUNK, CHUNK), 0) <= _iota((CHUNK, CHUNK), 1)).astype(F32)
        dct_scr[...] = dc_ref[...].T
        sm_ref[...] = jnp.zeros_like(sm_ref)
        dz_ref[:, LANES:] = jnp.zeros((lp, NARROW - LANES), BF16)

        def step(i, carry):
            n = nc - 1 - i
            r0 = pl.multiple_of(n * CHUNK, CHUNK)
            z = p_ref[pl.ds(r0, CHUNK), :] + b_ref[...]
            gt = g_ref[pl.ds(r0, CHUNK), :]
            dgd = dg_ref[pl.ds(r0, CHUNK), :]
            dch = dct_scr[pl.ds(r0, CHUNK), :]
            rc = _hdot(triu, dch) + carry
            sg = _sigmoid(z)
            dz = jnp.where(lane < nh, dgd * sg * (1.0 - sg),
                           jnp.where(lane < 2 * nh, dgd * a_ref[...] * sg,
                                     jnp.where(lane < 3 * nh, rc * (1.0 - sg), 0.0)))
            dz = jnp.where(r0 + _iota((CHUNK, LANES), 0) >= PAD_ROWS, dz, 0.0)
            dz_ref[pl.ds(r0, CHUNK), 0:LANES] = dz.astype(BF16)
            sm_ref[0:1, :] += jnp.sum(dz, axis=0, keepdims=True)
            sm_ref[1:2, :] += jnp.sum(jnp.where((lane >= nh) & (lane < 2 * nh), dgd * gt, 0.0), axis=0, keepdims=True)
            return carry + jnp.sum(dch, axis=0, keepdims=True)

        lax.fori_loop(0, nc, step, jnp.zeros((1, LANES), F32))

    vec = pl.BlockSpec((1, LANES), lambda i: (0, 0))
    full = pl.BlockSpec((lp, LANES), lambda i: (0, 0))
    last = pl.BlockSpec((lp, LANES), lambda i: (0, 8 * nh))
    tail = pl.BlockSpec((lp, NARROW), lambda i: (0, 8 * nh * LANES // NARROW))
    return pl.pallas_call(
        body, grid=(1,), in_specs=[last, vec, vec, full, full, pl.BlockSpec((LANES, lp), lambda i: (0, 0)), _ANY],
        out_specs=[tail, pl.BlockSpec((8, LANES), lambda i: (0, 0))],
        out_shape=[jax.ShapeDtypeStruct(dproj.shape, BF16), jax.ShapeDtypeStruct((8, LANES), F32)],
        scratch_shapes=[pltpu.VMEM((lp, LANES), F32)], input_output_aliases={6: 0},
        name="gates_bwd", compiler_params=_cp("arbitrary"))(proj, bias_row, nega_row, gates, dgate_gdn, dc_t, dproj)


def _tri_inv(a):
    t = jnp.where(_iota(a.shape, 1) == _iota(a.shape, 2), 1.0, 0.0) - a
    p = a
    for _ in range(5):
        ph, pw = _split(p)
        p = _dot3(ph, ph) + (_dot3(ph, pw) + _dot3(pw, ph))
        ph, pw = _split(p)
        th, tw = _split(t)
        t = t + (_dot3(th, ph) + (_dot3(th, pw) + _dot3(tw, ph)))
    return t


def _gdn_chunk(q, k, v, beta, gc, gr, t=None):
    ii, jj = _iota((1, CHUNK, CHUNK), 1), _iota((1, CHUNK, CHUNK), 2)
    causal, strict = ii >= jj, ii > jj
    dm = jnp.where(causal, jnp.exp(jnp.where(causal, gc - gr, 0.0)), 0.0)
    kk = _bdot3(k, k, "nt")
    a = jnp.where(strict, beta * kk * dm, 0.0)
    if t is None:
        t = _tri_inv(a)
    eg = jnp.exp(gc)
    glast = gc[:, CHUNK - 1:CHUNK, :]
    ekd = jnp.exp(glast - gc)
    bv = beta * v
    bk = (beta * eg) * k
    ub = _bdot3(t, jnp.concatenate([bv, bk], axis=2))
    qk = _bdot3(q, k, "nt")
    return dict(causal=causal, strict=strict, dm=dm, kk=kk, a=a, t=t, eg=eg, ekd=ekd, bv=bv, bk=bk,
                u=ub[:, :, :HEAD_DIM], w=ub[:, :, HEAD_DIM:], qk=qk, aqk=jnp.where(causal, qk * dm, 0.0),
                q_dec=q * eg, k_dec=k * ekd, decay=jnp.exp(glast))


def _heads(ref, nh):
    return jnp.stack([ref[:, h * HEAD_DIM:(h + 1) * HEAD_DIM] for h in range(nh)], axis=0)


def _gdn_chunk_inputs(q_ref, k_ref, v_ref, g, gt, nh):
    col = lambda o: jnp.stack([g[:, o + h:o + h + 1] for h in range(nh)], axis=0)
    gr = jnp.stack([gt[3 * nh + h:3 * nh + h + 1, :] for h in range(nh)], axis=0)
    return _heads(q_ref, nh), _heads(k_ref, nh), _heads(v_ref, nh), col(0), col(3 * nh), gr


def _gdn_fwd(qkv, gates, gt3, nh):
    lp = qkv.shape[0]
    nc = lp // CHUNK
    w = nh * HEAD_DIM

    def body(q_ref, k_ref, v_ref, g_ref, gt_ref, o_ref, sall_ref, tall_ref, s_scr):
        @pl.when(pl.program_id(0) == 0)
        def _():
            s_scr[...] = jnp.zeros_like(s_scr)
        c = _gdn_chunk(*_gdn_chunk_inputs(q_ref, k_ref, v_ref, g_ref[...], gt_ref[0], nh))
        s = s_scr[...]
        sall_ref[0] = s
        tall_ref[0] = c["t"]
        v_new = c["u"] - _bdot3(c["w"], s)
        o = _bdot3(c["q_dec"], s) + _bdot3(c["aqk"], v_new)
        s_scr[...] = s * c["decay"] + _bdot3(c["k_dec"], v_new, "tn")
        for h in range(nh):
            o_ref[:, h * HEAD_DIM:(h + 1) * HEAD_DIM] = o[h]

    return pl.pallas_call(
        body, grid=(nc,),
        in_specs=[pl.BlockSpec((CHUNK, w), lambda n: (n, 0)), pl.BlockSpec((CHUNK, w), lambda n: (n, 1)),
                  pl.BlockSpec((CHUNK, w), lambda n: (n, 2)), pl.BlockSpec((CHUNK, LANES), lambda n: (n, 0)),
                  pl.BlockSpec((1, LANES, CHUNK), lambda n: (n, 0, 0))],
        out_specs=[pl.BlockSpec((CHUNK, w), lambda n: (n, 0)),
                   pl.BlockSpec((1, nh, HEAD_DIM, HEAD_DIM), lambda n: (n, 0, 0, 0)),
                   pl.BlockSpec((1, nh, CHUNK, CHUNK), lambda n: (n, 0, 0, 0))],
        out_shape=[jax.ShapeDtypeStruct((lp, w), F32), jax.ShapeDtypeStruct((nc, nh, HEAD_DIM, HEAD_DIM), F32),
                   jax.ShapeDtypeStruct((nc, nh, CHUNK, CHUNK), F32)],
        scratch_shapes=[pltpu.VMEM((nh, HEAD_DIM, HEAD_DIM), F32)],
        name="gdn_fwd", compiler_params=_cp("arbitrary"))(qkv, qkv, qkv, gates, gt3)


def _gdn_bwd(qkv, gates, gt3, s_all, t_all, do, nh, rider=None):
    lp = qkv.shape[0]
    nc = lp // CHUNK
    w = nh * HEAD_DIM
    rev = lambda n: nc - 1 - n

    def body(start, finish, q_ref, k_ref, v_ref, g_ref, gt_ref, s_ref, t_ref, do_ref, dq_ref, dk_ref, dv_ref, dg_ref, ds_scr):
        @pl.when(pl.program_id(0) == 0)
        def _():
            start()
            ds_scr[...] = jnp.zeros_like(ds_scr)
        q, k, v, beta, gc, gr = _gdn_chunk_inputs(q_ref, k_ref, v_ref, g_ref[...], gt_ref[0], nh)
        c = _gdn_chunk(q, k, v, beta, gc, gr, t_ref[0])
        s = s_ref[0]
        dsn = ds_scr[...]
        dout = _heads(do_ref, nh)
        v_new = c["u"] - _bdot3(c["w"], s)
        dq_dec = _bdot3(dout, s, "nt")
        daqk = jnp.where(c["causal"], _bdot3(dout, v_new, "nt"), 0.0)
        dv_new = _bdot3(c["aqk"], dout, "tn") + _bdot3(c["k_dec"], dsn)
        dk_dec = _bdot3(v_new, dsn, "nt")
        ddecay = jnp.sum(jnp.sum(dsn * s, axis=2, keepdims=True), axis=1, keepdims=True)
        dw = -_bdot3(dv_new, s, "nt")
        ds_scr[...] = _bdot3(c["q_dec"], dout, "tn") + c["decay"] * dsn - _bdot3(c["w"], dv_new, "tn")
        duw = jnp.concatenate([dv_new, dw], axis=2)
        dt = _bdot3(duw, jnp.concatenate([c["bv"], c["bk"]], axis=2), "nt")
        dbvk = _bdot3(c["t"], duw, "tn")
        dbv, dbk = dbvk[:, :, :HEAD_DIM], dbvk[:, :, HEAD_DIM:]
        da = jnp.where(c["strict"], -_bdot3(_bdot3(c["t"], dt, "tn"), c["t"], "nt"), 0.0)
        dkk = da * beta * c["dm"]
        dqk = daqk * c["dm"]
        e = da * c["a"] + daqk * c["aqk"]
        dq = dq_dec * c["eg"] + _bdot3(dqk, k)
        dk = (dk_dec * c["ekd"] + _bdot3(dkk, k) + _bdot3(dkk, k, "tn") + _bdot3(dqk, q, "tn")
              + (beta * c["eg"]) * dbk)
        dv = beta * dbv
        rs = lambda x: jnp.sum(x, axis=2, keepdims=True)
        dbeta = rs(dbv * v) + c["eg"] * rs(dbk * k) + rs(da * c["kk"] * c["dm"])
        kd_term = rs(dk_dec * c["k_dec"])
        eh, ew = _split(e)
        ones = jnp.ones((nh, CHUNK, LANES), BF16)
        col_sums = (_dot3(eh, ones, "tn") + _dot3(ew, ones, "tn"))[:, :, 0:1]
        dg_cum = rs(dq_dec * c["q_dec"]) - kd_term + rs(dbk * c["bk"]) + rs(e) - col_sums
        last = jnp.sum(kd_term, axis=1, keepdims=True) + ddecay * c["decay"]
        dg_cum = dg_cum + jnp.where(_iota((1, CHUNK, 1), 1) == CHUNK - 1, last, 0.0)
        lane = _iota((CHUNK, LANES), 1)
        acc = jnp.zeros((CHUNK, LANES), F32)
        for h in range(nh):
            sl = slice(h * HEAD_DIM, (h + 1) * HEAD_DIM)
            dq_ref[:, sl] = dq[h]
            dk_ref[:, sl] = dk[h]
            dv_ref[:, sl] = dv[h]
            acc = acc + jnp.where(lane == h, dbeta[h], 0.0) + jnp.where(lane == nh + h, dg_cum[h], 0.0)
        triu = (_iota((CHUNK, CHUNK), 0) <= _iota((CHUNK, CHUNK), 1)).astype(F32)
        dg_ref[...] = jnp.where(lane < nh, acc, _hdot(triu, acc))
        pl.when(pl.program_id(0) == nc - 1)(finish)

    outs, got = _hosted_call(
        body, [rider], 8, 4, 1, grid=(nc,),
        in_specs=[pl.BlockSpec((CHUNK, w), lambda n: (rev(n), 0)), pl.BlockSpec((CHUNK, w), lambda n: (rev(n), 1)),
                  pl.BlockSpec((CHUNK, w), lambda n: (rev(n), 2)), pl.BlockSpec((CHUNK, LANES), lambda n: (rev(n), 0)),
                  pl.BlockSpec((1, LANES, CHUNK), lambda n: (rev(n), 0, 0)),
                  pl.BlockSpec((1, nh, HEAD_DIM, HEAD_DIM), lambda n: (rev(n), 0, 0, 0)),
                  pl.BlockSpec((1, nh, CHUNK, CHUNK), lambda n: (rev(n), 0, 0, 0)),
                  pl.BlockSpec((CHUNK, w), lambda n: (rev(n), 0))],
        out_specs=[pl.BlockSpec((CHUNK, w), lambda n: (rev(n), 0))] * 3 + [pl.BlockSpec((CHUNK, LANES), lambda n: (rev(n), 0))],
        out_shape=[jax.ShapeDtypeStruct((lp, w), F32)] * 3 + [jax.ShapeDtypeStruct((lp, LANES), F32)],
        scratch_shapes=[pltpu.VMEM((nh, HEAD_DIM, HEAD_DIM), F32)],
        name="gdn_bwd", compiler_params=_cp("arbitrary"))(qkv, qkv, qkv, gates, gt3, s_all, t_all, do)
    return outs, (got[0] if got else None)


def _merge_gdn(o_gdn, proj, norm_w, nh):
    lp = o_gdn.shape[0]

    def body(o_ref, z_ref, w_ref, m_ref):
        o = o_ref[...]
        z = z_ref[...]
        m_ref[...] = (o * _rms(o) * w_ref[...] * (z * _sigmoid(z))).astype(BF16)

    return pl.pallas_call(
        body, grid=(nh,),
        in_specs=[pl.BlockSpec((lp, LANES), lambda s: (0, s)), pl.BlockSpec((lp, LANES), lambda s: (0, 3 * nh + s)),
                  pl.BlockSpec((1, LANES), lambda s: (0, 0))],
        out_specs=pl.BlockSpec((lp, LANES), lambda s: (0, s)),
        out_shape=jax.ShapeDtypeStruct((lp, 2 * nh * HEAD_DIM), BF16), name="merge_gdn",
        compiler_params=_cp("parallel"))(o_gdn, proj, norm_w)


def _merge_gdn_bwd(o_gdn, proj, norm_w, dmerged, nh):
    lp = o_gdn.shape[0]

    def body(o_ref, z_ref, w_ref, dm_ref, do_ref, dz_ref, dw_ref):
        o = o_ref[...]
        r = _rms(o)
        xh = o * r
        silu, dsilu = _silu_and_grad(z_ref[...])
        dm = dm_ref[...]
        dn = dm * silu
        dz_ref[...] = (dm * (xh * w_ref[...]) * dsilu).astype(BF16)
        dnw = dn * w_ref[...]
        do_ref[...] = r * (dnw - xh * jnp.mean(dnw * xh, axis=-1, keepdims=True))

        @pl.when(pl.program_id(0) == 0)
        def _():
            dw_ref[...] = jnp.zeros_like(dw_ref)
        dw_ref[...] += jnp.sum(dn * xh, axis=0, keepdims=True)

    w = nh * HEAD_DIM
    return pl.pallas_call(
        body, grid=(nh,),
        in_specs=[pl.BlockSpec((lp, LANES), lambda s: (0, s)), pl.BlockSpec((lp, LANES), lambda s: (0, 3 * nh + s)),
                  pl.BlockSpec((1, LANES), lambda s: (0, 0)), pl.BlockSpec((lp, LANES), lambda s: (0, s))],
        out_specs=[pl.BlockSpec((lp, LANES), lambda s: (0, s)), pl.BlockSpec((lp, LANES), lambda s: (0, 3 * nh + s)),
                   pl.BlockSpec((1, LANES), lambda s: (0, 0))],
        out_shape=[jax.ShapeDtypeStruct((lp, w), F32), jax.ShapeDtypeStruct((lp, 8 * w + NARROW), BF16),
                   jax.ShapeDtypeStruct((1, LANES), F32)],
        name="merge_gdn_bwd", compiler_params=_cp("arbitrary"))(o_gdn, proj, norm_w, dmerged)


def _fox_prep(proj, qk_w, nh):
    lp = proj.shape[0]

    def body(x_ref, w_ref, o_ref):
        x = x_ref[...]
        o_ref[...] = x * _rms(x) * w_ref[0]

    return pl.pallas_call(
        body, grid=(2 * nh,),
        in_specs=[pl.BlockSpec((lp, LANES), lambda s: (0, 4 * nh + s)), pl.BlockSpec((1, 1, LANES), lambda s: (s // nh, 0, 0))],
        out_specs=pl.BlockSpec((lp, LANES), lambda s: (0, s)),
        out_shape=jax.ShapeDtypeStruct((lp, 2 * nh * HEAD_DIM), F32), name="fox_prep",
        compiler_params=_cp("parallel"))(proj, qk_w)


def _fox_prep_bwd(proj, qk_w, dq, dk, dproj, nh):
    lp = proj.shape[0]
    part = lambda p: pl.BlockSpec((lp, LANES), lambda s: (0, jnp.clip(s - p * nh, 0, nh - 1)))

    def body(x_ref, w_ref, dq_ref, dk_ref, _, dx_ref, dw_ref):
        x = x_ref[...]
        r = _rms(x)
        xh = x * r
        dy = jnp.where(pl.program_id(0) < nh, dq_ref[...], dk_ref[...])
        dyw = dy * w_ref[0]
        dx_ref[...] = (r * (dyw - xh * jnp.mean(dyw * xh, axis=-1, keepdims=True))).astype(BF16)

        @pl.when(pl.program_id(0) % nh == 0)
        def _():
            dw_ref[...] = jnp.zeros_like(dw_ref)
        dw_ref[0] += jnp.sum(dy * xh, axis=0, keepdims=True)

    strip = pl.BlockSpec((lp, LANES), lambda s: (0, 4 * nh + s))
    wsp = pl.BlockSpec((1, 1, LANES), lambda s: (s // nh, 0, 0))
    return pl.pallas_call(
        body, grid=(2 * nh,), in_specs=[strip, wsp, part(0), part(1), _ANY], out_specs=[strip, wsp],
        out_shape=[jax.ShapeDtypeStruct(dproj.shape, BF16), jax.ShapeDtypeStruct((2, 1, LANES), F32)],
        input_output_aliases={4: 0}, name="fox_prep_bwd", compiler_params=_cp("arbitrary"))(proj, qk_w, dq, dk, dproj)


def _fox_probs(q, k, gates, crow, h, i, nh):
    kl = k.shape[0]
    lane = _iota((Q_BLOCK, LANES), 1)
    ct = jnp.sum(jnp.where(lane == 4 * nh + h, gates, 0.0), axis=1, keepdims=True)
    s = _bdot(q, k, "nt") * (HEAD_DIM ** -0.5) + (ct - crow)
    t = i * Q_BLOCK + _iota((Q_BLOCK, kl), 0)
    kp = _iota((Q_BLOCK, kl), 1)
    s = jnp.where((kp <= t) & ((kp >= PAD_ROWS) | (t < PAD_ROWS)), s, NEG)
    p = jnp.exp(s - jnp.max(s, axis=1, keepdims=True))
    return p / jnp.sum(p, axis=1, keepdims=True)


FOX_HEADS_PER_STEP = 2


def _fox_specs(lp, nh):
    hw = FOX_HEADS_PER_STEP * LANES
    return [pl.BlockSpec((Q_BLOCK, hw), lambda g, i: (i, g)),
            pl.BlockSpec((lp, hw), lambda g, i: (0, nh // FOX_HEADS_PER_STEP + g)),
            pl.BlockSpec((lp, hw), lambda g, i: (0, 6 * nh // FOX_HEADS_PER_STEP + g)),
            pl.BlockSpec((Q_BLOCK, LANES), lambda g, i: (i, 0)),
            pl.BlockSpec((LANES, lp), lambda g, i: (0, 0))]


def _fox_fwd(qkn, proj, gates, gtf, nh):
    lp = qkn.shape[0]

    def body(q_ref, k_ref, v_ref, g_ref, gt_ref, o_ref):
        g, i = pl.program_id(0), pl.program_id(1)
        for j in range(lp // Q_BLOCK):
            @pl.when(i == j)
            def _(j=j):
                kl = (j + 1) * Q_BLOCK
                for hh in range(FOX_HEADS_PER_STEP):
                    h = FOX_HEADS_PER_STEP * g + hh
                    sl = slice(hh * LANES, (hh + 1) * LANES)
                    p = _fox_probs(q_ref[:, sl], k_ref[0:kl, sl], g_ref[...], gt_ref[pl.ds(4 * nh + h, 1), :][:, 0:kl],
                                   h, j, nh)
                    o_ref[:, sl] = _bdot(p, v_ref[0:kl, sl])

    return pl.pallas_call(
        body, grid=(nh // FOX_HEADS_PER_STEP, lp // Q_BLOCK), in_specs=_fox_specs(lp, nh),
        out_specs=pl.BlockSpec((Q_BLOCK, FOX_HEADS_PER_STEP * LANES), lambda g, i: (i, g)),
        out_shape=jax.ShapeDtypeStruct((lp, nh * HEAD_DIM), F32), name="fox_fwd",
        compiler_params=_cp("parallel", "parallel"))(qkn, qkn, proj, gates, gtf)


def _fox_bwd(qkn, proj, gates, gtf, do, dproj, nh):
    lp = qkn.shape[0]
    nq = lp // Q_BLOCK
    w = nh * HEAD_DIM
    scale = HEAD_DIM ** -0.5

    def body(q_ref, k_ref, v_ref, g_ref, gt_ref, do_ref, _, dq_ref, dk_ref, dc_ref, dv_ref, dv_scr):
        g, i = pl.program_id(0), pl.program_id(1)

        @pl.when(i == 0)
        def _():
            dk_ref[...] = jnp.zeros_like(dk_ref)
            dv_scr[...] = jnp.zeros_like(dv_scr)
            dc_ref[...] = jnp.zeros_like(dc_ref)
        for j in range(nq):
            @pl.when(i == j)
            def _(j=j):
                kl = (j + 1) * Q_BLOCK
                for hh in range(FOX_HEADS_PER_STEP):
                    h = FOX_HEADS_PER_STEP * g + hh
                    sl = slice(hh * LANES, (hh + 1) * LANES)
                    q, k = q_ref[:, sl], k_ref[0:kl, sl]
                    p = _fox_probs(q, k, g_ref[...], gt_ref[pl.ds(4 * nh + h, 1), :][:, 0:kl], h, j, nh)
                    dout = do_ref[:, sl]
                    dp = _bdot(dout, v_ref[0:kl, sl], "nt")
                    ds = p * (dp - jnp.sum(p * dp, axis=1, keepdims=True))
                    dq_ref[:, sl] = _bdot(ds, k) * scale
                    dk_ref[0:kl, sl] += _bdot(ds, q, "tn") * scale
                    dv_scr[0:kl, sl] += _bdot(p, dout, "tn")
                    dc_ref[hh, :, 0:kl] -= jnp.sum(ds, axis=0, keepdims=True)

        @pl.when(i == nq - 1)
        def _():
            dv_ref[...] = dv_scr[...].astype(BF16)

    hw = FOX_HEADS_PER_STEP * LANES
    blk = pl.BlockSpec((Q_BLOCK, hw), lambda g, i: (i, g))
    col = pl.BlockSpec((lp, hw), lambda g, i: (0, g))
    return pl.pallas_call(
        body, grid=(nh // FOX_HEADS_PER_STEP, nq), in_specs=_fox_specs(lp, nh) + [blk, _ANY],
        out_specs=[blk, col, pl.BlockSpec((FOX_HEADS_PER_STEP, 1, lp), lambda g, i: (g, 0, 0)),
                   pl.BlockSpec((lp, hw), lambda g, i: (0, 6 * nh // FOX_HEADS_PER_STEP + g))],
        out_shape=[jax.ShapeDtypeStruct((lp, w), F32)] * 2 + [jax.ShapeDtypeStruct((nh, 1, lp), F32),
                                                             jax.ShapeDtypeStruct(dproj.shape, BF16)],
        scratch_shapes=[pltpu.VMEM((lp, hw), F32)], input_output_aliases={6: 3},
        name="fox_bwd", compiler_params=_cp("parallel", "arbitrary"))(qkn, qkn, proj, gates, gtf, do, dproj)


def _merge_fox(o_fox, proj, merged, nh):
    lp = o_fox.shape[0]

    def body(o_ref, z_ref, _, m_ref):
        z = z_ref[...]
        m_ref[...] = (o_ref[...] * (z * _sigmoid(z))).astype(BF16)

    return pl.pallas_call(
        body, grid=(nh,),
        in_specs=[pl.BlockSpec((lp, LANES), lambda s: (0, s)), pl.BlockSpec((lp, LANES), lambda s: (0, 7 * nh + s)), _ANY],
        out_specs=pl.BlockSpec((lp, LANES), lambda s: (0, nh + s)),
        out_shape=jax.ShapeDtypeStruct(merged.shape, BF16), input_output_aliases={2: 0}, name="merge_fox",
        compiler_params=_cp("parallel"))(o_fox, proj, merged)


def _merge_fox_bwd(o_fox, proj, dmerged, dproj, nh):
    lp = o_fox.shape[0]

    def body(o_ref, z_ref, dm_ref, _, do_ref, dz_ref):
        silu, dsilu = _silu_and_grad(z_ref[...])
        dm = dm_ref[...]
        do_ref[...] = dm * silu
        dz_ref[...] = (dm * o_ref[...] * dsilu).astype(BF16)

    w = nh * HEAD_DIM
    return pl.pallas_call(
        body, grid=(nh,),
        in_specs=[pl.BlockSpec((lp, LANES), lambda s: (0, s)), pl.BlockSpec((lp, LANES), lambda s: (0, 7 * nh + s)),
                  pl.BlockSpec((lp, LANES), lambda s: (0, nh + s)), _ANY],
        out_specs=[pl.BlockSpec((lp, LANES), lambda s: (0, s)), pl.BlockSpec((lp, LANES), lambda s: (0, 7 * nh + s))],
        out_shape=[jax.ShapeDtypeStruct((lp, w), F32), jax.ShapeDtypeStruct(dproj.shape, BF16)],
        input_output_aliases={3: 1}, name="merge_fox_bwd", compiler_params=_cp("parallel"))(o_fox, proj, dmerged, dproj)


def _post(out, x, target, post_w):
    lp, d = out.shape

    def body(o_ref, x_ref, t_ref, w_ref, dy_ref, do_ref, loss_ref, dw_ref):
        i = pl.program_id(0)

        @pl.when(i == 0)
        def _():
            loss_ref[...] = jnp.zeros_like(loss_ref)
            dw_ref[...] = jnp.zeros_like(dw_ref)
        o = o_ref[...]
        r = _rms(o)
        nrm = o * r
        err = jnp.where(i > 0, x_ref[...] + nrm * w_ref[...] - t_ref[...], 0.0)
        loss_ref[0:1, :] += 0.5 * jnp.sum(jnp.sum(err * err, axis=1, keepdims=True), axis=0, keepdims=True) / d
        dy = err / d
        dy_ref[...] = dy
        dw_ref[...] += jnp.sum(dy * nrm, axis=0, keepdims=True)
        dyw = dy * w_ref[...]
        do_ref[...] = (r * (dyw - nrm * jnp.mean(dyw * nrm, axis=-1, keepdims=True))).astype(BF16)

    row = pl.BlockSpec((Q_BLOCK, d), lambda i: (i, 0))
    vec = pl.BlockSpec((1, d), lambda i: (0, 0))
    return pl.pallas_call(
        body, grid=(lp // Q_BLOCK,), in_specs=[row, _x_rows(d), _x_rows(d), vec],
        out_specs=[_x_rows(d), row, pl.BlockSpec((8, LANES), lambda i: (0, 0)), vec],
        out_shape=[jax.ShapeDtypeStruct(x.shape, F32), jax.ShapeDtypeStruct((lp, d), BF16),
                   jax.ShapeDtypeStruct((8, LANES), F32), jax.ShapeDtypeStruct((1, d), F32)],
        name="post", compiler_params=_cp("arbitrary"))(out, x, target, post_w)


def _prenorm_bwd(dxn, x, meta, w, dy, rider=None):
    seq, d = x.shape
    lp = seq + Q_BLOCK

    def body(start, finish, dx_ref, x_ref, m_ref, w_ref, dy_ref, gx_ref, gm_ref, dw_ref):
        i = pl.program_id(0)
        pl.when(i == 0)(start)
        h = _h_tile(i, x_ref, m_ref)
        r = _rms(h)
        xh = h * r
        dxn_ = dx_ref[...]
        dxw = dxn_ * w_ref[...]
        dh = jnp.where(i > 0, dy_ref[...], 0.0) + r * (dxw - xh * jnp.mean(dxw * xh, axis=-1, keepdims=True))
        gx_ref[...] = dh

        @pl.when(i == 0)
        def _():
            dw_ref[...] = jnp.zeros_like(dw_ref)
            gm_ref[...] = dh[PAD_ROWS:, :]
        dw_ref[...] += jnp.sum(dxn_ * xh, axis=0, keepdims=True)
        pl.when(i == lp // Q_BLOCK - 1)(finish)

    vec = pl.BlockSpec((1, d), lambda i: (0, 0))
    met = pl.BlockSpec((N_META, d), lambda i: (0, 0))
    outs, got = _hosted_call(
        body, [rider], 5, 3, 0, grid=(lp // Q_BLOCK,),
        in_specs=[pl.BlockSpec((Q_BLOCK, d), lambda i: (i, 0)), _x_rows(d), met, vec, _x_rows(d)],
        out_specs=[_x_rows(d), met, vec],
        out_shape=[jax.ShapeDtypeStruct((seq, d), F32), jax.ShapeDtypeStruct((N_META, d), F32),
                   jax.ShapeDtypeStruct((1, d), F32)],
        name="prenorm_bwd", compiler_params=_cp("arbitrary"))(dxn, x, meta, w, dy)
    return outs, (got[0] if got else None)


def _layer_grads(x, target, meta, pre_w, wfull, conv_w, a_log, dt_bias, gdn_norm_w, fq_w, fk_w, f_bias, w_out, post_w,
                 w_out_grads=None):
    nh = a_log.shape[1]
    conv_wt = conv_w.T
    zpad = jnp.zeros((1, LANES - 3 * nh), F32)
    bias_row = jnp.concatenate([jnp.zeros((1, nh), F32), dt_bias, f_bias, zpad], axis=1)
    nega_row = jnp.concatenate([jnp.zeros((1, nh), F32), -jnp.exp(a_log), jnp.zeros((1, nh), F32), zpad], axis=1)
    qk_w = jnp.stack([fq_w, fk_w])

    xn = _prenorm(x, meta, pre_w)
    proj = _matmul(xn, wfull, "nn", MM_TILE, F32, "proj")
    qkv = _gdn_prep(proj, conv_wt, nh)
    gates, gt3, gtf = _gates(proj, bias_row, nega_row, nh)
    o_gdn, s_all, t_all = _gdn_fwd(qkv, gates, gt3, nh)
    qkn = _fox_prep(proj, qk_w, nh)
    o_fox = _fox_fwd(qkn, proj, gates, gtf, nh)
    merged = _merge_fox(o_fox, proj, _merge_gdn(o_gdn, proj, gdn_norm_w, nh), nh)
    out = _matmul(merged, w_out, "nn", 4 * LANES, F32, "out_proj")
    dy, dout, loss_blk, dpost_w = _post(out, x, target, post_w)

    dw_out = _matmul(merged, dout, "tn", 4 * LANES, BF16, "dw_out")
    if w_out_grads is None:
        dmerged, gdn_rider = _matmul(dout, w_out, "nt", 4 * LANES, F32, "dmerged"), None
    else:
        dmerged, got = _matmul(dout, w_out, "nt", 4 * LANES, F32, "dmerged", w_out_grads[0](dw_out))
        gdn_rider = w_out_grads[1](dw_out, got)
    do_gdn, dproj, dgdn_norm_w = _merge_gdn_bwd(o_gdn, proj, gdn_norm_w, dmerged, nh)
    do_fox, dproj = _merge_fox_bwd(o_fox, proj, dmerged, dproj, nh)
    dqn, dkn, dc_t, dproj = _fox_bwd(qkn, proj, gates, gtf, do_fox, dproj, nh)
    dproj, dqk_w = _fox_prep_bwd(proj, qk_w, dqn, dkn, dproj, nh)
    (dgq, dgk, dgv, dgate), w_out_parts = _gdn_bwd(qkv, gates, gt3, s_all, t_all, do_gdn, nh, gdn_rider)
    dproj, dconv_wt = _gdn_prep_bwd(proj, conv_wt, dgq, dgk, dgv, dproj, nh)
    dc_rows = jnp.pad(dc_t.reshape(nh, -1), ((2 * nh, LANES - 3 * nh), (0, 0)))
    dproj, gate_sums = _gates_bwd(proj, bias_row, nega_row, gates, dgate, dc_rows, dproj, nh)
    return dict(
        loss=loss_blk[0:1, 0:1], dy=dy, xn=xn, dproj=dproj, post_w=dpost_w,
        conv_w=dconv_wt.T, a_log=gate_sums[1:2, nh:2 * nh], dt_bias=gate_sums[0:1, nh:2 * nh],
        gdn_norm_w=dgdn_norm_w, fq_w=dqk_w[0], fk_w=dqk_w[1], f_bias=gate_sums[0:1, 2 * nh:3 * nh], w_out=dw_out,
        w_out_parts=w_out_parts)


def _cast_bf16(a, tr, name):
    r, c = a.shape

    def body(a_ref, o_ref):
        o_ref[...] = a_ref[...].astype(BF16)

    return pl.pallas_call(
        body, grid=(r // tr,), in_specs=[pl.BlockSpec((tr, c), lambda i: (i, 0))],
        out_specs=pl.BlockSpec((tr, c), lambda i: (i, 0)), out_shape=jax.ShapeDtypeStruct((r, c), BF16),
        name=name, compiler_params=_cp("parallel"))(a)


def _gather_copies(ins, outs, send_sems, recv_sems, local_sems):
    n = len(ins)
    x, y, c = lax.axis_index("x"), lax.axis_index("y"), lax.axis_index("c")
    me, sibling = (x, y, c), (x, y, 1 - c)
    xn, yn, dg = (1 - x, y), (x, 1 - y), (1 - x, 1 - y)

    def copy(a, k, block, to, src=None):
        px, py, pc = block
        rows = outs[a].at[4 * px + 2 * py + pc]
        return pltpu.make_async_remote_copy(
            src_ref=rows if src is None else src, dst_ref=rows, send_sem=send_sems.at[a, k],
            recv_sem=recv_sems.at[a, k], device_id=to, device_id_type=_MESH)

    local = [pltpu.make_async_copy(ins[a], outs[a].at[4 * x + 2 * y + c], local_sems.at[a]) for a in range(n)]
    own = [cp for a in range(n) for cp in (copy(a, 0, me, sibling, src=ins[a]), copy(a, 1, me, (*xn, c), src=ins[a]),
                                           copy(a, 2, me, (*yn, c), src=ins[a]))]

    def start():
        for cp in local + own:
            cp.start()

    def finish():
        for a in range(n):
            @pl.when(c == 1)
            def _(a=a):
                copy(a, 1, (*xn, c), me).wait_recv()
                copy(a, 3, (*xn, c), (*yn, c)).start()

            @pl.when(c == 0)
            def _(a=a):
                copy(a, 2, (*yn, c), me).wait_recv()
                copy(a, 3, (*yn, c), (*xn, c)).start()
        for a in range(n):
            pl.when(c == 0)(copy(a, 1, (*xn, c), me).wait_recv)
            copy(a, 4, (*xn, c), sibling).start()
            pl.when(c == 1)(copy(a, 2, (*yn, c), me).wait_recv)
            copy(a, 5, (*yn, c), sibling).start()
        for a in range(n):
            copy(a, 3, (*dg, c), me).wait_recv()
            copy(a, 6, (*dg, c), sibling).start()
        for a in range(n):
            copy(a, 0, sibling, me).wait_recv()
            for k, chip in ((4, xn), (5, yn), (6, dg)):
                copy(a, k, (*chip, 1 - c), me).wait_recv()
                copy(a, k, (*chip, c), sibling).wait_send()
            copy(a, 3, (*xn, c), (*yn, c)).wait_send()
        for cp in own:
            cp.wait_send()
        for cp in local:
            cp.wait()

    return start, finish


def _gather_scratch(n):
    return [pltpu.SemaphoreType.DMA((n, N_DEV - 1)), pltpu.SemaphoreType.DMA((n, N_DEV - 1)), pltpu.SemaphoreType.DMA((n,))]


def _all_gather(arrays, name):
    n = len(arrays)

    def body(*refs):
        start, finish = _gather_copies(refs[:n], refs[n:2 * n], *refs[2 * n:])
        start()
        finish()

    return pl.pallas_call(
        body, in_specs=[_ANY] * n, out_specs=[_ANY] * n,
        out_shape=[jax.ShapeDtypeStruct((N_DEV,) + a.shape, a.dtype) for a in arrays],
        scratch_shapes=_gather_scratch(n), name=name)(*arrays)


SLAB = 10 * LANES


def _slab_start(blk, nh, cols):
    in_second_half = blk >= N_DEV // 2
    shift = (2 * nh if in_second_half else 0) if isinstance(blk, int) else jnp.where(in_second_half, 2 * nh, 0)
    return (blk * cols - shift) // LANES * LANES


def _pair_rider(dw_rows=None, parts=None, nh=None):
    if dw_rows is not None:
        r, full = dw_rows.shape
        cols = (full - NARROW + 3 * nh) // N_DEV
        out_shapes = [jax.ShapeDtypeStruct((N_CHIP, r, SLAB), dw_rows.dtype), jax.ShapeDtypeStruct((r, NARROW), dw_rows.dtype)]
    else:
        out_shapes = [jax.ShapeDtypeStruct((N_CHIP,) + parts.shape[1:], parts.dtype)]

    def make(ins, outs, scratch):
        send_sems, recv_sems = scratch
        x, y, c = lax.axis_index("x"), lax.axis_index("y"), lax.axis_index("c")
        kw = lambda k: dict(send_sem=send_sems.at[k], recv_sem=recv_sems.at[k], device_id=(x, y, 1 - c), device_id_type=_MESH)
        copies = []
        for q in range(N_CHIP):
            if dw_rows is not None:
                first = pl.multiple_of(_slab_start(2 * q + 1 - c, nh, cols), LANES)
                copies.append(pltpu.make_async_remote_copy(src_ref=ins[0].at[:, pl.ds(first, SLAB)], dst_ref=outs[0].at[q], **kw(q)))
            else:
                copies.append(pltpu.make_async_remote_copy(src_ref=ins[0].at[2 * q + 1 - c], dst_ref=outs[0].at[q], **kw(q)))
        if dw_rows is not None:
            copies.append(pltpu.make_async_remote_copy(src_ref=ins[0].at[:, pl.ds(full - NARROW, NARROW)], dst_ref=outs[1],
                                                       **kw(N_CHIP)))

        def start():
            for cp in copies:
                cp.start()

        def finish():
            for cp in copies:
                cp.wait()

        return start, finish

    return _Rider([dw_rows if dw_rows is not None else parts], out_shapes,
                  [pltpu.SemaphoreType.DMA((N_CHIP + 1,)), pltpu.SemaphoreType.DMA((N_CHIP + 1,))], {}, make)


def _relayout_pair_sum(dwfull, got_slabs, got_tail, core, nh, tr, name):
    d, full = dwfull.shape
    w = nh * HEAD_DIM
    cols = (8 * w + 3 * nh) // N_DEV
    segs = _native_segments(nh)

    def block(f_ref, s_ref, t_ref, q, blk):
        st = _slab_start(blk, nh, cols)
        wide = f_ref[:, st:st + SLAB].astype(F32) + s_ref[q].astype(F32)
        tail = f_ref[:, 8 * w:].astype(F32) + t_ref[...].astype(F32)
        pieces = []
        for s0, s1, t0 in segs:
            lo, hi = max(s0, blk * cols), min(s1, (blk + 1) * cols)
            if lo < hi:
                at = t0 + lo - s0
                pieces.append(tail[:, at - 8 * w:at - 8 * w + hi - lo] if at >= 8 * w else wide[:, at - st:at - st + hi - lo])
        return (pieces[0] if len(pieces) == 1 else jnp.concatenate(pieces, axis=1)).astype(dwfull.dtype)

    def body(core_ref, f_ref, s_ref, t_ref, o_ref):
        for parity in range(2):
            @pl.when(core_ref[0] == parity)
            def _(parity=parity):
                for q in range(N_CHIP):
                    o_ref[q] = block(f_ref, s_ref, t_ref, q, 2 * q + parity)

    return pl.pallas_call(
        body,
        grid_spec=pltpu.PrefetchScalarGridSpec(
            num_scalar_prefetch=1, grid=(d // tr,),
            in_specs=[pl.BlockSpec((tr, full), lambda i, c_ref: (i, 0)), pl.BlockSpec((N_CHIP, tr, SLAB), lambda i, c_ref: (0, i, 0)),
                      pl.BlockSpec((tr, NARROW), lambda i, c_ref: (i, 0))],
            out_specs=pl.BlockSpec((N_CHIP, tr, cols), lambda i, c_ref: (0, i, 0))),
        out_shape=jax.ShapeDtypeStruct((N_CHIP, d, cols), dwfull.dtype), name=name,
        compiler_params=_cp("parallel"))(core, dwfull, got_slabs, got_tail)


def _pair_sum(parts, got, core, tr, name):
    _, r, c = parts.shape

    def body(core_ref, p_ref, g_ref, o_ref):
        o_ref[...] = (p_ref[...].astype(F32) + g_ref[...].astype(F32)).astype(o_ref.dtype)

    return pl.pallas_call(
        body,
        grid_spec=pltpu.PrefetchScalarGridSpec(
            num_scalar_prefetch=1, grid=(N_CHIP, r // tr),
            in_specs=[pl.BlockSpec((1, tr, c), lambda q, i, core_ref: (2 * q + core_ref[0], i, 0)),
                      pl.BlockSpec((1, tr, c), lambda q, i, core_ref: (q, i, 0))],
            out_specs=pl.BlockSpec((1, tr, c), lambda q, i, core_ref: (q, i, 0))),
        out_shape=jax.ShapeDtypeStruct((N_CHIP, r, c), parts.dtype), name=name,
        compiler_params=_cp("parallel", "parallel"))(core, parts, got)


def _native_segments(nh):
    w = nh * HEAD_DIM
    return [(0, 4 * w, 0), (4 * w, 4 * w + 2 * nh, 8 * w), (4 * w + 2 * nh, 8 * w + 2 * nh, 4 * w),
            (8 * w + 2 * nh, 8 * w + 3 * nh, 8 * w + 2 * nh)]


def _relayout_w_in(wg, nh, tr):
    _, d, cols = wg.shape
    w = nh * HEAD_DIM

    def native(ref, j0, j1):
        out = []
        while j0 < j1:
            blk = j0 // cols
            end = min(j1, (blk + 1) * cols)
            out.append(ref[blk, :, pl.ds(j0 - blk * cols, end - j0)])
            j0 = end
        return out

    def body(g_ref, o_ref):
        for cidx in range(8 * w // LANES):
            j0 = cidx * LANES + (0 if cidx * LANES < 4 * w else 2 * nh)
            pieces = native(g_ref, j0, j0 + LANES)
            o_ref[:, cidx * LANES:(cidx + 1) * LANES] = pieces[0] if len(pieces) == 1 else jnp.concatenate(pieces, axis=1)
        pieces = (native(g_ref, 4 * w, 4 * w + 2 * nh) + native(g_ref, 8 * w + 2 * nh, 8 * w + 3 * nh)
                  + [jnp.zeros((tr, NARROW - 3 * nh), wg.dtype)])
        o_ref[:, 8 * w:] = jnp.concatenate(pieces, axis=1)

    return pl.pallas_call(
        body, grid=(d // tr,), in_specs=[pl.BlockSpec((N_DEV, tr, cols), lambda i: (0, i, 0))],
        out_specs=pl.BlockSpec((tr, 8 * w + NARROW), lambda i: (i, 0)),
        out_shape=jax.ShapeDtypeStruct((d, 8 * w + NARROW), wg.dtype),
        name="relayout_w_in", compiler_params=_cp("parallel"))(wg)


def _adamw(w, parts, m, v, tr, name):
    r, c = w.shape
    n_parts = parts.shape[0]

    def body(w_ref, p_ref, m_ref, v_ref, g_ref, d_ref, nm_ref, nv_ref):
        g = p_ref[0].astype(F32)
        for s in range(1, n_parts):
            g = g + p_ref[s].astype(F32)
        m_new = ADAM_B1 * m_ref[...] + (1.0 - ADAM_B1) * g
        v_new = ADAM_B2 * v_ref[...] + (1.0 - ADAM_B2) * (g * g)
        m_hat = m_new / (1.0 - ADAM_B1 ** ADAM_STEP)
        v_hat = v_new / (1.0 - ADAM_B2 ** ADAM_STEP)
        g_ref[...] = g
        d_ref[...] = -ADAM_LR * (m_hat / (jnp.sqrt(v_hat) + ADAM_EPS) + ADAM_WD * w_ref[...])
        nm_ref[...] = m_new
        nv_ref[...] = v_new

    blk = pl.BlockSpec((tr, c), lambda i: (i, 0))
    return pl.pallas_call(
        body, grid=(r // tr,), in_specs=[blk, pl.BlockSpec((n_parts, tr, c), lambda i: (0, i, 0)), blk, blk],
        out_specs=[blk] * 4, out_shape=[jax.ShapeDtypeStruct((r, c), F32)] * 4, name=name,
        compiler_params=_cp("parallel"))(w, parts, m, v)


def _pack_small(d, pre, post, a_log, dt_bias, f_bias, gdn_w, fq_w, fk_w, extra):
    row2 = jnp.concatenate([a_log, dt_bias, f_bias, gdn_w, fq_w, fk_w, extra], axis=1)
    row2 = jnp.pad(row2, ((0, 0), (0, d - row2.shape[1])))
    return jnp.concatenate([pre, post, row2, jnp.zeros((5, d), F32)], axis=0)


def _unpack_small(p, nh):
    o = 3 * nh
    return dict(pre=p[0:1], post=p[1:2], a_log=p[2:3, 0:nh], dt_bias=p[2:3, nh:2 * nh], f_bias=p[2:3, 2 * nh:o],
                gdn_w=p[2:3, o:o + HEAD_DIM], fq_w=p[2:3, o + HEAD_DIM:o + 2 * HEAD_DIM],
                fk_w=p[2:3, o + 2 * HEAD_DIM:o + 3 * HEAD_DIM], extra=p[2, o + 3 * HEAD_DIM])


def kernel(x, meta_tokens, pre_norm_w, w_in, conv_w, a_log, dt_bias, gdn_norm_w, fox_q_norm_w, fox_k_norm_w, fox_f_bias, w_out, post_norm_w, loss_target, m_meta_tokens, m_pre_norm_w, m_w_in, m_conv_w, m_a_log, m_dt_bias, m_gdn_norm_w, m_fox_q_norm_w, m_fox_k_norm_w, m_fox_f_bias, m_w_out, m_post_norm_w, v_meta_tokens, v_pre_norm_w, v_w_in, v_conv_w, v_a_log, v_dt_bias, v_gdn_norm_w, v_fox_q_norm_w, v_fox_k_norm_w, v_fox_f_bias, v_w_out, v_post_norm_w):
    nh = a_log.shape[1]
    d = x.shape[-1]
    w = nh * HEAD_DIM
    zero = jnp.zeros((1, 1), F32)

    wg, wog, cg, mg = _all_gather(
        [_cast_bf16(w_in[0], 256, "cast_w_in"), _cast_bf16(w_out[0], 256, "cast_w_out"), conv_w[0], meta_tokens],
        "gather_weights")
    wfull = _relayout_w_in(wg, nh, 256)
    meta_full = mg.transpose(1, 0, 2).reshape(N_META, d)
    core = lax.axis_index("c")
    dev = 4 * lax.axis_index("x") + 2 * lax.axis_index("y") + core
    core_arr = jnp.reshape(core, (1,)).astype(jnp.int32)

    out_parts = lambda dw_out: dw_out.reshape(N_DEV, 2 * w // N_DEV, d)
    g = _layer_grads(
        x[0], loss_target[0], meta_full, pre_norm_w, wfull, cg.reshape(3 * w, CONV_WIDTH), a_log, dt_bias, gdn_norm_w,
        fox_q_norm_w, fox_k_norm_w, fox_f_bias, wog.reshape(2 * w, d), post_norm_w,
        w_out_grads=(lambda dw_out: _pair_rider(parts=out_parts(dw_out)),
                     lambda dw_out, got: _chip_rider(_pair_sum(out_parts(dw_out), got[0], core_arr, 256, "pair_sum_w_out"))))
    p_out = g["w_out_parts"][0]
    xn, dproj, half = g["xn"], g["dproj"], d // 2
    dw_a = _matmul(xn[:, :half], dproj, "tn", MM_TILE, BF16, "dw_in_a")
    dw_b, got_a = _matmul(xn[:, half:], dproj, "tn", MM_TILE, BF16, "dw_in_b", _pair_rider(dw_rows=dw_a, nh=nh))
    sums_a = _relayout_pair_sum(dw_a, got_a[0], got_a[1], core_arr, nh, 128, "relayout_pair_sum_a")
    dxn, (p_in_a, got_b) = _dxn(dproj, wfull, [_chip_rider(sums_a, rows_total=d), _pair_rider(dw_rows=dw_b, nh=nh)],
                                MM_TILE, "dxn")
    sums_b = _relayout_pair_sum(dw_b, got_b[0], got_b[1], core_arr, nh, 128, "relayout_pair_sum_b")
    (grad_x, dmeta, dpre_w), p_in = _prenorm_bwd(dxn, x[0], meta_full, pre_norm_w, g["dy"],
                                                 _chip_rider(sums_b, rows_total=d, row0=half, into=p_in_a[0]))
    p_in = p_in[0]
    small = _pack_small(d, dpre_w, g["post_w"], g["a_log"], g["dt_bias"], g["f_bias"], g["gdn_norm_w"], g["fq_w"],
                        g["fk_w"], g["loss"])
    a_conv, a_meta, p_small = _all_gather([g["conv_w"], dmeta, small], "gather_small_grads")
    p_conv = lax.dynamic_slice_in_dim(a_conv, dev * conv_w.shape[1], conv_w.shape[1], axis=1)
    p_meta = lax.dynamic_slice_in_dim(a_meta, dev * meta_tokens.shape[1], meta_tokens.shape[1], axis=2)

    r_in = _adamw(w_in[0], p_in, m_w_in[0], v_w_in[0], 128, "adamw_w_in")
    r_out = _adamw(w_out[0], p_out, m_w_out[0], v_w_out[0], 64, "adamw_w_out")
    r_conv = _adamw(conv_w[0], p_conv, m_conv_w[0], v_conv_w[0], conv_w.shape[1], "adamw_conv_w")
    r_meta = _adamw(meta_tokens, p_meta, m_meta_tokens, v_meta_tokens, N_META, "adamw_meta")
    pk = lambda pre, post, a, dt, gw, fq, fk, fb: _pack_small(d, pre, post, a, dt, fb, gw, fq, fk, zero)
    r_small = _adamw(
        pk(pre_norm_w, post_norm_w, a_log, dt_bias, gdn_norm_w, fox_q_norm_w, fox_k_norm_w, fox_f_bias), p_small,
        pk(m_pre_norm_w, m_post_norm_w, m_a_log, m_dt_bias, m_gdn_norm_w, m_fox_q_norm_w, m_fox_k_norm_w, m_fox_f_bias),
        pk(v_pre_norm_w, v_post_norm_w, v_a_log, v_dt_bias, v_gdn_norm_w, v_fox_q_norm_w, v_fox_k_norm_w, v_fox_f_bias),
        8, "adamw_small")

    sm = [_unpack_small(r, nh) for r in r_small]
    outs = []
    for i in range(4):
        s = sm[i]
        outs += [r_meta[i], s["pre"], r_in[i][None], r_conv[i][None], s["a_log"], s["dt_bias"], s["gdn_w"], s["fq_w"],
                 s["fk_w"], s["f_bias"], r_out[i][None], s["post"]]
    return (sm[0]["extra"], grad_x[None], *outs)
```

```python
import jax
import jax.numpy as jnp
from jax import lax
from jax.experimental import pallas as pl
from jax.experimental.pallas import tpu as pltpu

F32, BF16 = jnp.float32, jnp.bfloat16
HEAD_DIM = 128
N_META = 16
CONV_WIDTH = 4
CHUNK = 64
Q_BLOCK = 128
LANES = 128
EPS = 1e-6
PAD_ROWS = Q_BLOCK - N_META
N_DEV = 8
N_CHIP = 4
VMEM_LIMIT = 56 * 1024 * 1024
NEG = -1e30
NARROW = 2 * LANES
MM_TILE = 6 * LANES

ADAM_LR, ADAM_B1, ADAM_B2, ADAM_EPS, ADAM_WD, ADAM_STEP = 0.001, 0.9, 0.999, 1e-08, 0.01, 10

_DN = {"nn": (((1,), (0,)), ((), ())), "nt": (((1,), (1,)), ((), ())), "tn": (((0,), (0,)), ((), ()))}
_DN3 = {"nn": (((2,), (1,)), ((0,), (0,))), "nt": (((2,), (2,)), ((0,), (0,))), "tn": (((1,), (1,)), ((0,), (0,)))}
_ANY = pl.BlockSpec(memory_space=pl.ANY)
_MESH = pl.DeviceIdType.MESH


def _cp(*sem):
    return pltpu.CompilerParams(dimension_semantics=sem, vmem_limit_bytes=VMEM_LIMIT)


def _dot(a, b, dims="nn", prec=None):
    return lax.dot_general(a, b, _DN[dims], precision=prec, preferred_element_type=F32)


def _bdot(a, b, dims="nn"):
    return _dot(a.astype(BF16), b.astype(BF16), dims)


def _hdot(a, b, dims="nn"):
    return _dot(a, b, dims, prec=lax.Precision.HIGHEST)


def _dot3(a, b, dims="nn"):
    return lax.dot_general(a, b, _DN3[dims], preferred_element_type=F32)


def _bdot3(a, b, dims="nn"):
    return _dot3(a.astype(BF16), b.astype(BF16), dims)


def _split(a):
    hi = a.astype(BF16)
    return hi, (a - hi.astype(F32)).astype(BF16)


def _iota(shape, dim):
    return lax.broadcasted_iota(jnp.int32, shape, dim)


def _sigmoid(z):
    return 1.0 / (1.0 + jnp.exp(-z))


def _softplus(z):
    e = jnp.exp(-jnp.abs(z))
    u = 1.0 + e
    l1p = jnp.where(u == 1.0, e, jnp.log(u) * (e / jnp.where(u == 1.0, 1.0, u - 1.0)))
    return jnp.maximum(z, 0.0) + l1p


def _silu_and_grad(z):
    s = _sigmoid(z)
    return z * s, s * (1.0 + z * (1.0 - s))


def _rms(x):
    return lax.rsqrt(jnp.mean(x * x, axis=-1, keepdims=True) + EPS)


def _h_tile(i, x_ref, meta_ref):
    first = jnp.concatenate([jnp.zeros((PAD_ROWS, x_ref.shape[1]), F32), meta_ref[...]], axis=0)
    return jnp.where(i == 0, first, x_ref[...])


def _x_rows(d):
    return pl.BlockSpec((Q_BLOCK, d), lambda i: (jnp.maximum(i - 1, 0), 0))


def _prenorm(x, meta, w):
    seq, d = x.shape
    lp = seq + Q_BLOCK

    def body(x_ref, m_ref, w_ref, o_ref):
        h = _h_tile(pl.program_id(0), x_ref, m_ref)
        o_ref[...] = (h * _rms(h) * w_ref[...]).astype(BF16)

    return pl.pallas_call(
        body, grid=(lp // Q_BLOCK,),
        in_specs=[_x_rows(d), pl.BlockSpec((N_META, d), lambda i: (0, 0)), pl.BlockSpec((1, d), lambda i: (0, 0))],
        out_specs=pl.BlockSpec((Q_BLOCK, d), lambda i: (i, 0)),
        out_shape=jax.ShapeDtypeStruct((lp, d), BF16), name="prenorm", compiler_params=_cp("parallel"))(x, meta, w)


def _tile(n, want):
    return max(t for t in range(LANES, want + 1, LANES) if n % t == 0)


class _Rider:
    def __init__(self, inputs, out_shapes, scratch, aliases, make):
        self.inputs, self.out_shapes, self.scratch, self.aliases, self.make = inputs, out_shapes, scratch, aliases, make


def _hosted_call(body, riders, n_in, n_out, n_scratch, *, in_specs, out_specs, out_shape, scratch_shapes=(), aliases=None,
                 **kw):
    riders = [r for r in riders if r is not None]
    r_in = [len(r.inputs) for r in riders]
    r_out = [len(r.out_shapes) for r in riders]
    r_scr = [len(r.scratch) for r in riders]
    al = dict(aliases or {})
    for k, r in enumerate(riders):
        al.update({n_in + sum(r_in[:k]) + i: n_out + sum(r_out[:k]) + o for i, o in r.aliases.items()})

    def full_body(*refs):
        ins, rest = refs[:n_in + sum(r_in)], refs[n_in + sum(r_in):]
        outs, scr = rest[:n_out + sum(r_out)], rest[n_out + sum(r_out):]
        hooks = [r.make(ins[n_in + sum(r_in[:k]):n_in + sum(r_in[:k + 1])], outs[n_out + sum(r_out[:k]):n_out + sum(r_out[:k + 1])],
                        scr[n_scratch + sum(r_scr[:k]):n_scratch + sum(r_scr[:k + 1])]) for k, r in enumerate(riders)]

        def start():
            for h in hooks:
                h[0]()

        def finish():
            for h in hooks:
                h[1]()

        body(start, finish, *ins[:n_in], *outs[:n_out], *scr[:n_scratch])

    call = pl.pallas_call(
        full_body, in_specs=list(in_specs) + [_ANY] * sum(r_in), out_specs=list(out_specs) + [_ANY] * sum(r_out),
        out_shape=list(out_shape) + [s for r in riders for s in r.out_shapes],
        scratch_shapes=list(scratch_shapes) + [s for r in riders for s in r.scratch], input_output_aliases=al, **kw)

    def run(*args):
        res = call(*args, *[t for r in riders for t in r.inputs])
        return res[:n_out], [res[n_out + sum(r_out[:k]):n_out + sum(r_out[:k + 1])] for k in range(len(riders))]

    return run


def _matmul(a, b, dims, tn, out_dtype, name, rider=None):
    m = a.shape[1] if dims == "tn" else a.shape[0]
    n = b.shape[0] if dims == "nt" else b.shape[1]
    kdim = b.shape[1] if dims == "nt" else b.shape[0]
    tn = _tile(n, tn)
    steps = n // tn
    b_spec = pl.BlockSpec((tn, kdim), lambda j: (j, 0)) if dims == "nt" else pl.BlockSpec((kdim, tn), lambda j: (0, j))

    def body(start, finish, a_ref, b_ref, o_ref):
        pl.when(pl.program_id(0) == 0)(start)
        o_ref[...] = _dot(a_ref[...], b_ref[...], dims).astype(out_dtype)
        pl.when(pl.program_id(0) == steps - 1)(finish)

    (out,), got = _hosted_call(
        body, [rider], 2, 1, 0, grid=(steps,), in_specs=[pl.BlockSpec(a.shape, lambda j: (0, 0)), b_spec],
        out_specs=[pl.BlockSpec((m, tn), lambda j: (0, j))], out_shape=[jax.ShapeDtypeStruct((m, n), out_dtype)],
        name=name, compiler_params=_cp("parallel" if rider is None else "arbitrary"))(a, b)
    return out if rider is None else (out, got[0])


def _chip_rider(sums, rows_total=None, row0=0, into=None):
    _, r, c = sums.shape
    rows_total = rows_total or r

    def make(ins, outs, scratch):
        send_sems, recv_sems, local_sem = scratch
        x, y, core = lax.axis_index("x"), lax.axis_index("y"), lax.axis_index("c")
        mine = 2 * x + y
        land = lambda chip: outs[0].at[chip].at[pl.ds(row0, r)]
        local = pltpu.make_async_copy(ins[0].at[mine], land(mine), local_sem)
        sends, recvs = [], []
        for k in range(1, N_CHIP):
            px = 1 - x if k & 2 else x
            py = 1 - y if k & 1 else y
            kw = dict(send_sem=send_sems.at[k - 1], recv_sem=recv_sems.at[k - 1], device_id=(px, py, core),
                      device_id_type=_MESH)
            sends.append(pltpu.make_async_remote_copy(src_ref=ins[0].at[2 * px + py], dst_ref=land(mine), **kw))
            recvs.append(pltpu.make_async_remote_copy(src_ref=ins[0].at[mine], dst_ref=land(2 * px + py), **kw))

        def start():
            for cp in [local] + sends:
                cp.start()

        def finish():
            local.wait()
            for cp in sends:
                cp.wait_send()
            for cp in recvs:
                cp.wait_recv()

        return start, finish

    return _Rider([sums] + ([] if into is None else [into]), [jax.ShapeDtypeStruct((N_CHIP, rows_total, c), sums.dtype)],
                  [pltpu.SemaphoreType.DMA((N_CHIP - 1,)), pltpu.SemaphoreType.DMA((N_CHIP - 1,)), pltpu.SemaphoreType.DMA(())],
                  {} if into is None else {1: 0}, make)


def _dxn(dproj, wfull, riders, tk, name):
    m, k = dproj.shape
    n = wfull.shape[0]
    tk = _tile(k, tk)
    steps = k // tk

    def body(start, finish, a_ref, b_ref, o_ref):
        j = pl.program_id(0)

        @pl.when(j == 0)
        def _():
            start()
            o_ref[...] = jnp.zeros_like(o_ref)
        o_ref[...] += _dot(a_ref[...], b_ref[...], "nt")
        pl.when(j == steps - 1)(finish)

    (dxn,), got = _hosted_call(
        body, riders, 2, 1, 0, grid=(steps,),
        in_specs=[pl.BlockSpec((m, tk), lambda j: (0, j)), pl.BlockSpec((n, tk), lambda j: (0, j))],
        out_specs=[pl.BlockSpec((m, n), lambda j: (0, 0))], out_shape=[jax.ShapeDtypeStruct((m, n), F32)],
        name=name, compiler_params=_cp("arbitrary"))(dproj, wfull)
    return dxn, got


def _conv_taps(x, w):
    c = x * w[CONV_WIDTH - 1:CONV_WIDTH, :]
    for j in range(CONV_WIDTH - 1):
        c = c + pltpu.roll(x, CONV_WIDTH - 1 - j, 0) * w[j:j + 1, :]
    return c


def _gdn_prep(proj, conv_wt, nh):
    lp = proj.shape[0]
    scale = HEAD_DIM ** -0.5

    def body(x_ref, w_ref, o_ref):
        which = pl.program_id(0) // nh
        c = _conv_taps(x_ref[...], w_ref[...])
        s = c * _sigmoid(c)
        r = lax.rsqrt(jnp.sum(s * s, axis=-1, keepdims=True) + EPS)
        f = jnp.where(which == 0, r * scale, jnp.where(which == 1, r, 1.0))
        o_ref[...] = jnp.where(_iota(s.shape, 0) >= PAD_ROWS, s * f, 0.0)

    return pl.pallas_call(
        body, grid=(3 * nh,),
        in_specs=[pl.BlockSpec((lp, LANES), lambda s: (0, s)), pl.BlockSpec((CONV_WIDTH, LANES), lambda s: (0, s))],
        out_specs=pl.BlockSpec((lp, LANES), lambda s: (0, s)),
        out_shape=jax.ShapeDtypeStruct((lp, 3 * nh * HEAD_DIM), F32), name="gdn_prep",
        compiler_params=_cp("parallel"))(proj, conv_wt)


def _gdn_prep_bwd(proj, conv_wt, dq, dk, dv, dproj, nh):
    lp = proj.shape[0]
    scale = HEAD_DIM ** -0.5
    part = lambda p: pl.BlockSpec((lp, LANES), lambda s: (0, jnp.clip(s - p * nh, 0, nh - 1)))

    def body(x_ref, w_ref, dq_ref, dk_ref, dv_ref, _, dx_ref, dw_ref):
        which = pl.program_id(0) // nh
        x = x_ref[...]
        w = w_ref[...]
        c = _conv_taps(x, w)
        sg = _sigmoid(c)
        s = c * sg
        r = lax.rsqrt(jnp.sum(s * s, axis=-1, keepdims=True) + EPS)
        dy = jnp.where(which == 0, dq_ref[...], jnp.where(which == 1, dk_ref[...], dv_ref[...]))
        dy = jnp.where(_iota(s.shape, 0) >= PAD_ROWS, dy, 0.0)
        y0 = s * r
        dy0 = dy * jnp.where(which == 0, scale, 1.0)
        ds_n = r * (dy0 - y0 * jnp.sum(dy0 * y0, axis=-1, keepdims=True))
        ds = jnp.where(which == 2, dy, ds_n)
        dc = ds * (sg * (1.0 + c * (1.0 - sg)))
        dx = dc * w[CONV_WIDTH - 1:CONV_WIDTH, :]
        rows = [jnp.sum(dc * x, axis=0, keepdims=True)]
        for j in range(CONV_WIDTH - 2, -1, -1):
            sh = CONV_WIDTH - 1 - j
            dx = dx + pltpu.roll(dc, lp - sh, 0) * w[j:j + 1, :]
            rows.insert(0, jnp.sum(dc * pltpu.roll(x, sh, 0), axis=0, keepdims=True))
        dx_ref[...] = dx.astype(BF16)
        dw_ref[...] = jnp.concatenate(rows, axis=0)

    strip = pl.BlockSpec((lp, LANES), lambda s: (0, s))
    taps = pl.BlockSpec((CONV_WIDTH, LANES), lambda s: (0, s))
    return pl.pallas_call(
        body, grid=(3 * nh,), in_specs=[strip, taps, part(0), part(1), part(2), _ANY], out_specs=[strip, taps],
        out_shape=[jax.ShapeDtypeStruct(dproj.shape, BF16), jax.ShapeDtypeStruct((CONV_WIDTH, 3 * nh * HEAD_DIM), F32)],
        input_output_aliases={5: 0}, name="gdn_prep_bwd", compiler_params=_cp("parallel"))(proj, conv_wt, dq, dk, dv, dproj)


def _gates(proj, bias_row, nega_row, nh):
    lp = proj.shape[0]
    nc = lp // CHUNK

    def body(p_ref, b_ref, a_ref, g_ref, gt3_ref, gtf_ref):
        lane = _iota((CHUNK, LANES), 1)
        tri = (_iota((CHUNK, CHUNK), 0) >= _iota((CHUNK, CHUNK), 1)).astype(F32)

        def step(n, carry):
            r0 = pl.multiple_of(n * CHUNK, CHUNK)
            z = p_ref[pl.ds(r0, CHUNK), :] + b_ref[...]
            base = jnp.where(lane < nh, _sigmoid(z),
                             jnp.where(lane < 2 * nh, a_ref[...] * _softplus(z),
                                       jnp.where(lane < 3 * nh, -_softplus(-z), 0.0)))
            base = jnp.where(r0 + _iota((CHUNK, LANES), 0) >= PAD_ROWS, base, 0.0)
            cs = _hdot(tri, base)
            run = jnp.where((lane >= 2 * nh) & (lane < 3 * nh), cs + carry, cs)
            sh = pltpu.roll(run, 2 * nh, 1)
            out = base + jnp.where((lane >= 3 * nh) & (lane < 5 * nh), sh, 0.0)
            g_ref[pl.ds(r0, CHUNK), :] = out
            gt3_ref[n] = out.T
            return carry + cs[CHUNK - 1:CHUNK, :]

        lax.fori_loop(0, nc, step, jnp.zeros((1, LANES), F32))
        gtf_ref[...] = g_ref[...].T

    vec = pl.BlockSpec((1, LANES), lambda i: (0, 0))
    return pl.pallas_call(
        body, grid=(1,), in_specs=[pl.BlockSpec((lp, LANES), lambda i: (0, 8 * nh)), vec, vec],
        out_specs=[pl.BlockSpec((lp, LANES), lambda i: (0, 0)), pl.BlockSpec((nc, LANES, CHUNK), lambda i: (0, 0, 0)),
                   pl.BlockSpec((LANES, lp), lambda i: (0, 0))],
        out_shape=[jax.ShapeDtypeStruct((lp, LANES), F32), jax.ShapeDtypeStruct((nc, LANES, CHUNK), F32),
                   jax.ShapeDtypeStruct((LANES, lp), F32)],
        name="gates", compiler_params=_cp("arbitrary"))(proj, bias_row, nega_row)


def _gates_bwd(proj, bias_row, nega_row, gates, dgate_gdn, dc_t, dproj, nh):
    lp = proj.shape[0]
    nc = lp // CHUNK

    def body(p_ref, b_ref, a_ref, g_ref, dg_ref, dc_ref, _, dz_ref, sm_ref, dct_scr):
        lane = _iota((CHUNK, LANES), 1)
        triu = (_iota((CHUNK, CHUNK), 0) <= _iota((CHUNK, CHUNK), 1)).astype(F32)
        dct_scr[...] = dc_ref[...].T
        sm_ref[...] = jnp.zeros_like(sm_ref)
        dz_ref[:, LANES:] = jnp.zeros((lp, NARROW - LANES), BF16)

        def step(i, carry):
            n = nc - 1 - i
            r0 = pl.multiple_of(n * CHUNK, CHUNK)
            z = p_ref[pl.ds(r0, CHUNK), :] + b_ref[...]
            gt = g_ref[pl.ds(r0, CHUNK), :]
            dgd = dg_ref[pl.ds(r0, CHUNK), :]
            dch = dct_scr[pl.ds(r0, CHUNK), :]
            rc = _hdot(triu, dch) + carry
            sg = _sigmoid(z)
            dz = jnp.where(lane < nh, dgd * sg * (1.0 - sg),
                           jnp.where(lane < 2 * nh, dgd * a_ref[...] * sg,
                                     jnp.where(lane < 3 * nh, rc * (1.0 - sg), 0.0)))
            dz = jnp.where(r0 + _iota((CHUNK, LANES), 0) >= PAD_ROWS, dz, 0.0)
            dz_ref[pl.ds(r0, CHUNK), 0:LANES] = dz.astype(BF16)
            sm_ref[0:1, :] += jnp.sum(dz, axis=0, keepdims=True)
            sm_ref[1:2, :] += jnp.sum(jnp.where((lane >= nh) & (lane < 2 * nh), dgd * gt, 0.0), axis=0, keepdims=True)
            return carry + jnp.sum(dch, axis=0, keepdims=True)

        lax.fori_loop(0, nc, step, jnp.zeros((1, LANES), F32))

    vec = pl.BlockSpec((1, LANES), lambda i: (0, 0))
    full = pl.BlockSpec((lp, LANES), lambda i: (0, 0))
    last = pl.BlockSpec((lp, LANES), lambda i: (0, 8 * nh))
    tail = pl.BlockSpec((lp, NARROW), lambda i: (0, 8 * nh * LANES // NARROW))
    return pl.pallas_call(
        body, grid=(1,), in_specs=[last, vec, vec, full, full, pl.BlockSpec((LANES, lp), lambda i: (0, 0)), _ANY],
        out_specs=[tail, pl.BlockSpec((8, LANES), lambda i: (0, 0))],
        out_shape=[jax.ShapeDtypeStruct(dproj.shape, BF16), jax.ShapeDtypeStruct((8, LANES), F32)],
        scratch_shapes=[pltpu.VMEM((lp, LANES), F32)], input_output_aliases={6: 0},
        name="gates_bwd", compiler_params=_cp("arbitrary"))(proj, bias_row, nega_row, gates, dgate_gdn, dc_t, dproj)


def _tri_inv(a):
    t = jnp.where(_iota(a.shape, 1) == _iota(a.shape, 2), 1.0, 0.0) - a
    p = a
    for _ in range(5):
        ph, pw = _split(p)
        p = _dot3(ph, ph) + (_dot3(ph, pw) + _dot3(pw, ph))
        ph, pw = _split(p)
        th, tw = _split(t)
        t = t + (_dot3(th, ph) + (_dot3(th, pw) + _dot3(tw, ph)))
    return t


def _gdn_chunk(q, k, v, beta, gc, gr, t=None):
    ii, jj = _iota((1, CHUNK, CHUNK), 1), _iota((1, CHUNK, CHUNK), 2)
    causal, strict = ii >= jj, ii > jj
    dm = jnp.where(causal, jnp.exp(jnp.where(causal, gc - gr, 0.0)), 0.0)
    kk = _bdot3(k, k, "nt")
    a = jnp.where(strict, beta * kk * dm, 0.0)
    if t is None:
        t = _tri_inv(a)
    eg = jnp.exp(gc)
    glast = gc[:, CHUNK - 1:CHUNK, :]
    ekd = jnp.exp(glast - gc)
    bv = beta * v
    bk = (beta * eg) * k
    ub = _bdot3(t, jnp.concatenate([bv, bk], axis=2))
    qk = _bdot3(q, k, "nt")
    return dict(causal=causal, strict=strict, dm=dm, kk=kk, a=a, t=t, eg=eg, ekd=ekd, bv=bv, bk=bk,
                u=ub[:, :, :HEAD_DIM], w=ub[:, :, HEAD_DIM:], qk=qk, aqk=jnp.where(causal, qk * dm, 0.0),
                q_dec=q * eg, k_dec=k * ekd, decay=jnp.exp(glast))


def _heads(ref, nh):
    return jnp.stack([ref[:, h * HEAD_DIM:(h + 1) * HEAD_DIM] for h in range(nh)], axis=0)


def _gdn_chunk_inputs(q_ref, k_ref, v_ref, g, gt, nh):
    col = lambda o: jnp.stack([g[:, o + h:o + h + 1] for h in range(nh)], axis=0)
    gr = jnp.stack([gt[3 * nh + h:3 * nh + h + 1, :] for h in range(nh)], axis=0)
    return _heads(q_ref, nh), _heads(k_ref, nh), _heads(v_ref, nh), col(0), col(3 * nh), gr


def _gdn_fwd(qkv, gates, gt3, nh):
    lp = qkv.shape[0]
    nc = lp // CHUNK
    w = nh * HEAD_DIM

    def body(q_ref, k_ref, v_ref, g_ref, gt_ref, o_ref, sall_ref, tall_ref, s_scr):
        @pl.when(pl.program_id(0) == 0)
        def _():
            s_scr[...] = jnp.zeros_like(s_scr)
        c = _gdn_chunk(*_gdn_chunk_inputs(q_ref, k_ref, v_ref, g_ref[...], gt_ref[0], nh))
        s = s_scr[...]
        sall_ref[0] = s
        tall_ref[0] = c["t"]
        v_new = c["u"] - _bdot3(c["w"], s)
        o = _bdot3(c["q_dec"], s) + _bdot3(c["aqk"], v_new)
        s_scr[...] = s * c["decay"] + _bdot3(c["k_dec"], v_new, "tn")
        for h in range(nh):
            o_ref[:, h * HEAD_DIM:(h + 1) * HEAD_DIM] = o[h]

    return pl.pallas_call(
        body, grid=(nc,),
        in_specs=[pl.BlockSpec((CHUNK, w), lambda n: (n, 0)), pl.BlockSpec((CHUNK, w), lambda n: (n, 1)),
                  pl.BlockSpec((CHUNK, w), lambda n: (n, 2)), pl.BlockSpec((CHUNK, LANES), lambda n: (n, 0)),
                  pl.BlockSpec((1, LANES, CHUNK), lambda n: (n, 0, 0))],
        out_specs=[pl.BlockSpec((CHUNK, w), lambda n: (n, 0)),
                   pl.BlockSpec((1, nh, HEAD_DIM, HEAD_DIM), lambda n: (n, 0, 0, 0)),
                   pl.BlockSpec((1, nh, CHUNK, CHUNK), lambda n: (n, 0, 0, 0))],
        out_shape=[jax.ShapeDtypeStruct((lp, w), F32), jax.ShapeDtypeStruct((nc, nh, HEAD_DIM, HEAD_DIM), F32),
                   jax.ShapeDtypeStruct((nc, nh, CHUNK, CHUNK), F32)],
        scratch_shapes=[pltpu.VMEM((nh, HEAD_DIM, HEAD_DIM), F32)],
        name="gdn_fwd", compiler_params=_cp("arbitrary"))(qkv, qkv, qkv, gates, gt3)


def _gdn_bwd(qkv, gates, gt3, s_all, t_all, do, nh, rider=None):
    lp = qkv.shape[0]
    nc = lp // CHUNK
    w = nh * HEAD_DIM
    rev = lambda n: nc - 1 - n

    def body(start, finish, q_ref, k_ref, v_ref, g_ref, gt_ref, s_ref, t_ref, do_ref, dq_ref, dk_ref, dv_ref, dg_ref, ds_scr):
        @pl.when(pl.program_id(0) == 0)
        def _():
            start()
            ds_scr[...] = jnp.zeros_like(ds_scr)
        q, k, v, beta, gc, gr = _gdn_chunk_inputs(q_ref, k_ref, v_ref, g_ref[...], gt_ref[0], nh)
        c = _gdn_chunk(q, k, v, beta, gc, gr, t_ref[0])
        s = s_ref[0]
        dsn = ds_scr[...]
        dout = _heads(do_ref, nh)
        v_new = c["u"] - _bdot3(c["w"], s)
        dq_dec = _bdot3(dout, s, "nt")
        daqk = jnp.where(c["causal"], _bdot3(dout, v_new, "nt"), 0.0)
        dv_new = _bdot3(c["aqk"], dout, "tn") + _bdot3(c["k_dec"], dsn)
        dk_dec = _bdot3(v_new, dsn, "nt")
        ddecay = jnp.sum(jnp.sum(dsn * s, axis=2, keepdims=True), axis=1, keepdims=True)
        dw = -_bdot3(dv_new, s, "nt")
        ds_scr[...] = _bdot3(c["q_dec"], dout, "tn") + c["decay"] * dsn - _bdot3(c["w"], dv_new, "tn")
        duw = jnp.concatenate([dv_new, dw], axis=2)
        dt = _bdot3(duw, jnp.concatenate([c["bv"], c["bk"]], axis=2), "nt")
        dbvk = _bdot3(c["t"], duw, "tn")
        dbv, dbk = dbvk[:, :, :HEAD_DIM], dbvk[:, :, HEAD_DIM:]
        da = jnp.where(c["strict"], -_bdot3(_bdot3(c["t"], dt, "tn"), c["t"], "nt"), 0.0)
        dkk = da * beta * c["dm"]
        dqk = daqk * c["dm"]
        e = da * c["a"] + daqk * c["aqk"]
        dq = dq_dec * c["eg"] + _bdot3(dqk, k)
        dk = (dk_dec * c["ekd"] + _bdot3(dkk, k) + _bdot3(dkk, k, "tn") + _bdot3(dqk, q, "tn")
              + (beta * c["eg"]) * dbk)
        dv = beta * dbv
        rs = lambda x: jnp.sum(x, axis=2, keepdims=True)
        dbeta = rs(dbv * v) + c["eg"] * rs(dbk * k) + rs(da * c["kk"] * c["dm"])
        kd_term = rs(dk_dec * c["k_dec"])
        eh, ew = _split(e)
        ones = jnp.ones((nh, CHUNK, LANES), BF16)
        col_sums = (_dot3(eh, ones, "tn") + _dot3(ew, ones, "tn"))[:, :, 0:1]
        dg_cum = rs(dq_dec * c["q_dec"]) - kd_term + rs(dbk * c["bk"]) + rs(e) - col_sums
        last = jnp.sum(kd_term, axis=1, keepdims=True) + ddecay * c["decay"]
        dg_cum = dg_cum + jnp.where(_iota((1, CHUNK, 1), 1) == CHUNK - 1, last, 0.0)
        lane = _iota((CHUNK, LANES), 1)
        acc = jnp.zeros((CHUNK, LANES), F32)
        for h in range(nh):
            sl = slice(h * HEAD_DIM, (h + 1) * HEAD_DIM)
            dq_ref[:, sl] = dq[h]
            dk_ref[:, sl] = dk[h]
            dv_ref[:, sl] = dv[h]
            acc = acc + jnp.where(lane == h, dbeta[h], 0.0) + jnp.where(lane == nh + h, dg_cum[h], 0.0)
        triu = (_iota((CHUNK, CHUNK), 0) <= _iota((CHUNK, CHUNK), 1)).astype(F32)
        dg_ref[...] = jnp.where(lane < nh, acc, _hdot(triu, acc))
        pl.when(pl.program_id(0) == nc - 1)(finish)

    outs, got = _hosted_call(
        body, [rider], 8, 4, 1, grid=(nc,),
        in_specs=[pl.BlockSpec((CHUNK, w), lambda n: (rev(n), 0)), pl.BlockSpec((CHUNK, w), lambda n: (rev(n), 1)),
                  pl.BlockSpec((CHUNK, w), lambda n: (rev(n), 2)), pl.BlockSpec((CHUNK, LANES), lambda n: (rev(n), 0)),
                  pl.BlockSpec((1, LANES, CHUNK), lambda n: (rev(n), 0, 0)),
                  pl.BlockSpec((1, nh, HEAD_DIM, HEAD_DIM), lambda n: (rev(n), 0, 0, 0)),
                  pl.BlockSpec((1, nh, CHUNK, CHUNK), lambda n: (rev(n), 0, 0, 0)),
                  pl.BlockSpec((CHUNK, w), lambda n: (rev(n), 0))],
        out_specs=[pl.BlockSpec((CHUNK, w), lambda n: (rev(n), 0))] * 3 + [pl.BlockSpec((CHUNK, LANES), lambda n: (rev(n), 0))],
        out_shape=[jax.ShapeDtypeStruct((lp, w), F32)] * 3 + [jax.ShapeDtypeStruct((lp, LANES), F32)],
        scratch_shapes=[pltpu.VMEM((nh, HEAD_DIM, HEAD_DIM), F32)],
        name="gdn_bwd", compiler_params=_cp("arbitrary"))(qkv, qkv, qkv, gates, gt3, s_all, t_all, do)
    return outs, (got[0] if got else None)


def _merge_gdn(o_gdn, proj, norm_w, nh):
    lp = o_gdn.shape[0]

    def body(o_ref, z_ref, w_ref, m_ref):
        o = o_ref[...]
        z = z_ref[...]
        m_ref[...] = (o * _rms(o) * w_ref[...] * (z * _sigmoid(z))).astype(BF16)

    return pl.pallas_call(
        body, grid=(nh,),
        in_specs=[pl.BlockSpec((lp, LANES), lambda s: (0, s)), pl.BlockSpec((lp, LANES), lambda s: (0, 3 * nh + s)),
                  pl.BlockSpec((1, LANES), lambda s: (0, 0))],
        out_specs=pl.BlockSpec((lp, LANES), lambda s: (0, s)),
        out_shape=jax.ShapeDtypeStruct((lp, 2 * nh * HEAD_DIM), BF16), name="merge_gdn",
        compiler_params=_cp("parallel"))(o_gdn, proj, norm_w)


def _merge_gdn_bwd(o_gdn, proj, norm_w, dmerged, nh):
    lp = o_gdn.shape[0]

    def body(o_ref, z_ref, w_ref, dm_ref, do_ref, dz_ref, dw_ref):
        o = o_ref[...]
        r = _rms(o)
        xh = o * r
        silu, dsilu = _silu_and_grad(z_ref[...])
        dm = dm_ref[...]
        dn = dm * silu
        dz_ref[...] = (dm * (xh * w_ref[...]) * dsilu).astype(BF16)
        dnw = dn * w_ref[...]
        do_ref[...] = r * (dnw - xh * jnp.mean(dnw * xh, axis=-1, keepdims=True))

        @pl.when(pl.program_id(0) == 0)
        def _():
            dw_ref[...] = jnp.zeros_like(dw_ref)
        dw_ref[...] += jnp.sum(dn * xh, axis=0, keepdims=True)

    w = nh * HEAD_DIM
    return pl.pallas_call(
        body, grid=(nh,),
        in_specs=[pl.BlockSpec((lp, LANES), lambda s: (0, s)), pl.BlockSpec((lp, LANES), lambda s: (0, 3 * nh + s)),
                  pl.BlockSpec((1, LANES), lambda s: (0, 0)), pl.BlockSpec((lp, LANES), lambda s: (0, s))],
        out_specs=[pl.BlockSpec((lp, LANES), lambda s: (0, s)), pl.BlockSpec((lp, LANES), lambda s: (0, 3 * nh + s)),
                   pl.BlockSpec((1, LANES), lambda s: (0, 0))],
        out_shape=[jax.ShapeDtypeStruct((lp, w), F32), jax.ShapeDtypeStruct((lp, 8 * w + NARROW), BF16),
                   jax.ShapeDtypeStruct((1, LANES), F32)],
        name="merge_gdn_bwd", compiler_params=_cp("arbitrary"))(o_gdn, proj, norm_w, dmerged)


def _fox_prep(proj, qk_w, nh):
    lp = proj.shape[0]

    def body(x_ref, w_ref, o_ref):
        x = x_ref[...]
        o_ref[...] = x * _rms(x) * w_ref[0]

    return pl.pallas_call(
        body, grid=(2 * nh,),
        in_specs=[pl.BlockSpec((lp, LANES), lambda s: (0, 4 * nh + s)), pl.BlockSpec((1, 1, LANES), lambda s: (s // nh, 0, 0))],
        out_specs=pl.BlockSpec((lp, LANES), lambda s: (0, s)),
        out_shape=jax.ShapeDtypeStruct((lp, 2 * nh * HEAD_DIM), F32), name="fox_prep",
        compiler_params=_cp("parallel"))(proj, qk_w)


def _fox_prep_bwd(proj, qk_w, dq, dk, dproj, nh):
    lp = proj.shape[0]
    part = lambda p: pl.BlockSpec((lp, LANES), lambda s: (0, jnp.clip(s - p * nh, 0, nh - 1)))

    def body(x_ref, w_ref, dq_ref, dk_ref, _, dx_ref, dw_ref):
        x = x_ref[...]
        r = _rms(x)
        xh = x * r
        dy = jnp.where(pl.program_id(0) < nh, dq_ref[...], dk_ref[...])
        dyw = dy * w_ref[0]
        dx_ref[...] = (r * (dyw - xh * jnp.mean(dyw * xh, axis=-1, keepdims=True))).astype(BF16)

        @pl.when(pl.program_id(0) % nh == 0)
        def _():
            dw_ref[...] = jnp.zeros_like(dw_ref)
        dw_ref[0] += jnp.sum(dy * xh, axis=0, keepdims=True)

    strip = pl.BlockSpec((lp, LANES), lambda s: (0, 4 * nh + s))
    wsp = pl.BlockSpec((1, 1, LANES), lambda s: (s // nh, 0, 0))
    return pl.pallas_call(
        body, grid=(2 * nh,), in_specs=[strip, wsp, part(0), part(1), _ANY], out_specs=[strip, wsp],
        out_shape=[jax.ShapeDtypeStruct(dproj.shape, BF16), jax.ShapeDtypeStruct((2, 1, LANES), F32)],
        input_output_aliases={4: 0}, name="fox_prep_bwd", compiler_params=_cp("arbitrary"))(proj, qk_w, dq, dk, dproj)


def _fox_probs(q, k, gates, crow, h, i, nh):
    kl = k.shape[0]
    lane = _iota((Q_BLOCK, LANES), 1)
    ct = jnp.sum(jnp.where(lane == 4 * nh + h, gates, 0.0), axis=1, keepdims=True)
    s = _bdot(q, k, "nt") * (HEAD_DIM ** -0.5) + (ct - crow)
    t = i * Q_BLOCK + _iota((Q_BLOCK, kl), 0)
    kp = _iota((Q_BLOCK, kl), 1)
    s = jnp.where((kp <= t) & ((kp >= PAD_ROWS) | (t < PAD_ROWS)), s, NEG)
    p = jnp.exp(s - jnp.max(s, axis=1, keepdims=True))
    return p / jnp.sum(p, axis=1, keepdims=True)


FOX_HEADS_PER_STEP = 2


def _fox_specs(lp, nh):
    hw = FOX_HEADS_PER_STEP * LANES
    return [pl.BlockSpec((Q_BLOCK, hw), lambda g, i: (i, g)),
            pl.BlockSpec((lp, hw), lambda g, i: (0, nh // FOX_HEADS_PER_STEP + g)),
            pl.BlockSpec((lp, hw), lambda g, i: (0, 6 * nh // FOX_HEADS_PER_STEP + g)),
            pl.BlockSpec((Q_BLOCK, LANES), lambda g, i: (i, 0)),
            pl.BlockSpec((LANES, lp), lambda g, i: (0, 0))]


def _fox_fwd(qkn, proj, gates, gtf, nh):
    lp = qkn.shape[0]

    def body(q_ref, k_ref, v_ref, g_ref, gt_ref, o_ref):
        g, i = pl.program_id(0), pl.program_id(1)
        for j in range(lp // Q_BLOCK):
            @pl.when(i == j)
            def _(j=j):
                kl = (j + 1) * Q_BLOCK
                for hh in range(FOX_HEADS_PER_STEP):
                    h = FOX_HEADS_PER_STEP * g + hh
                    sl = slice(hh * LANES, (hh + 1) * LANES)
                    p = _fox_probs(q_ref[:, sl], k_ref[0:kl, sl], g_ref[...], gt_ref[pl.ds(4 * nh + h, 1), :][:, 0:kl],
                                   h, j, nh)
                    o_ref[:, sl] = _bdot(p, v_ref[0:kl, sl])

    return pl.pallas_call(
        body, grid=(nh // FOX_HEADS_PER_STEP, lp // Q_BLOCK), in_specs=_fox_specs(lp, nh),
        out_specs=pl.BlockSpec((Q_BLOCK, FOX_HEADS_PER_STEP * LANES), lambda g, i: (i, g)),
        out_shape=jax.ShapeDtypeStruct((lp, nh * HEAD_DIM), F32), name="fox_fwd",
        compiler_params=_cp("parallel", "parallel"))(qkn, qkn, proj, gates, gtf)


def _fox_bwd(qkn, proj, gates, gtf, do, dproj, nh):
    lp = qkn.shape[0]
    nq = lp // Q_BLOCK
    w = nh * HEAD_DIM
    scale = HEAD_DIM ** -0.5

    def body(q_ref, k_ref, v_ref, g_ref, gt_ref, do_ref, _, dq_ref, dk_ref, dc_ref, dv_ref, dv_scr):
        g, i = pl.program_id(0), pl.program_id(1)

        @pl.when(i == 0)
        def _():
            dk_ref[...] = jnp.zeros_like(dk_ref)
            dv_scr[...] = jnp.zeros_like(dv_scr)
            dc_ref[...] = jnp.zeros_like(dc_ref)
        for j in range(nq):
            @pl.when(i == j)
            def _(j=j):
                kl = (j + 1) * Q_BLOCK
                for hh in range(FOX_HEADS_PER_STEP):
                    h = FOX_HEADS_PER_STEP * g + hh
                    sl = slice(hh * LANES, (hh + 1) * LANES)
                    q, k = q_ref[:, sl], k_ref[0:kl, sl]
                    p = _fox_probs(q, k, g_ref[...], gt_ref[pl.ds(4 * nh + h, 1), :][:, 0:kl], h, j, nh)
                    dout = do_ref[:, sl]
                    dp = _bdot(dout, v_ref[0:kl, sl], "nt")
                    ds = p * (dp - jnp.sum(p * dp, axis=1, keepdims=True))
                    dq_ref[:, sl] = _bdot(ds, k) * scale
                    dk_ref[0:kl, sl] += _bdot(ds, q, "tn") * scale
                    dv_scr[0:kl, sl] += _bdot(p, dout, "tn")
                    dc_ref[hh, :, 0:kl] -= jnp.sum(ds, axis=0, keepdims=True)

        @pl.when(i == nq - 1)
        def _():
            dv_ref[...] = dv_scr[...].astype(BF16)

    hw = FOX_HEADS_PER_STEP * LANES
    blk = pl.BlockSpec((Q_BLOCK, hw), lambda g, i: (i, g))
    col = pl.BlockSpec((lp, hw), lambda g, i: (0, g))
    return pl.pallas_call(
        body, grid=(nh // FOX_HEADS_PER_STEP, nq), in_specs=_fox_specs(lp, nh) + [blk, _ANY],
        out_specs=[blk, col, pl.BlockSpec((FOX_HEADS_PER_STEP, 1, lp), lambda g, i: (g, 0, 0)),
                   pl.BlockSpec((lp, hw), lambda g, i: (0, 6 * nh // FOX_HEADS_PER_STEP + g))],
        out_shape=[jax.ShapeDtypeStruct((lp, w), F32)] * 2 + [jax.ShapeDtypeStruct((nh, 1, lp), F32),
                                                             jax.ShapeDtypeStruct(dproj.shape, BF16)],
        scratch_shapes=[pltpu.VMEM((lp, hw), F32)], input_output_aliases={6: 3},
        name="fox_bwd", compiler_params=_cp("parallel", "arbitrary"))(qkn, qkn, proj, gates, gtf, do, dproj)


def _merge_fox(o_fox, proj, merged, nh):
    lp = o_fox.shape[0]

    def body(o_ref, z_ref, _, m_ref):
        z = z_ref[...]
        m_ref[...] = (o_ref[...] * (z * _sigmoid(z))).astype(BF16)

    return pl.pallas_call(
        body, grid=(nh,),
        in_specs=[pl.BlockSpec((lp, LANES), lambda s: (0, s)), pl.BlockSpec((lp, LANES), lambda s: (0, 7 * nh + s)), _ANY],
        out_specs=pl.BlockSpec((lp, LANES), lambda s: (0, nh + s)),
        out_shape=jax.ShapeDtypeStruct(merged.shape, BF16), input_output_aliases={2: 0}, name="merge_fox",
        compiler_params=_cp("parallel"))(o_fox, proj, merged)


def _merge_fox_bwd(o_fox, proj, dmerged, dproj, nh):
    lp = o_fox.shape[0]

    def body(o_ref, z_ref, dm_ref, _, do_ref, dz_ref):
        silu, dsilu = _silu_and_grad(z_ref[...])
        dm = dm_ref[...]
        do_ref[...] = dm * silu
        dz_ref[...] = (dm * o_ref[...] * dsilu).astype(BF16)

    w = nh * HEAD_DIM
    return pl.pallas_call(
        body, grid=(nh,),
        in_specs=[pl.BlockSpec((lp, LANES), lambda s: (0, s)), pl.BlockSpec((lp, LANES), lambda s: (0, 7 * nh + s)),
                  pl.BlockSpec((lp, LANES), lambda s: (0, nh + s)), _ANY],
        out_specs=[pl.BlockSpec((lp, LANES), lambda s: (0, s)), pl.BlockSpec((lp, LANES), lambda s: (0, 7 * nh + s))],
        out_shape=[jax.ShapeDtypeStruct((lp, w), F32), jax.ShapeDtypeStruct(dproj.shape, BF16)],
        input_output_aliases={3: 1}, name="merge_fox_bwd", compiler_params=_cp("parallel"))(o_fox, proj, dmerged, dproj)


def _post(out, x, target, post_w):
    lp, d = out.shape

    def body(o_ref, x_ref, t_ref, w_ref, dy_ref, do_ref, loss_ref, dw_ref):
        i = pl.program_id(0)

        @pl.when(i == 0)
        def _():
            loss_ref[...] = jnp.zeros_like(loss_ref)
            dw_ref[...] = jnp.zeros_like(dw_ref)
        o = o_ref[...]
        r = _rms(o)
        nrm = o * r
        err = jnp.where(i > 0, x_ref[...] + nrm * w_ref[...] - t_ref[...], 0.0)
        loss_ref[0:1, :] += 0.5 * jnp.sum(jnp.sum(err * err, axis=1, keepdims=True), axis=0, keepdims=True) / d
        dy = err / d
        dy_ref[...] = dy
        dw_ref[...] += jnp.sum(dy * nrm, axis=0, keepdims=True)
        dyw = dy * w_ref[...]
        do_ref[...] = (r * (dyw - nrm * jnp.mean(dyw * nrm, axis=-1, keepdims=True))).astype(BF16)

    row = pl.BlockSpec((Q_BLOCK, d), lambda i: (i, 0))
    vec = pl.BlockSpec((1, d), lambda i: (0, 0))
    return pl.pallas_call(
        body, grid=(lp // Q_BLOCK,), in_specs=[row, _x_rows(d), _x_rows(d), vec],
        out_specs=[_x_rows(d), row, pl.BlockSpec((8, LANES), lambda i: (0, 0)), vec],
        out_shape=[jax.ShapeDtypeStruct(x.shape, F32), jax.ShapeDtypeStruct((lp, d), BF16),
                   jax.ShapeDtypeStruct((8, LANES), F32), jax.ShapeDtypeStruct((1, d), F32)],
        name="post", compiler_params=_cp("arbitrary"))(out, x, target, post_w)


def _prenorm_bwd(dxn, x, meta, w, dy, rider=None):
    seq, d = x.shape
    lp = seq + Q_BLOCK

    def body(start, finish, dx_ref, x_ref, m_ref, w_ref, dy_ref, gx_ref, gm_ref, dw_ref):
        i = pl.program_id(0)
        pl.when(i == 0)(start)
        h = _h_tile(i, x_ref, m_ref)
        r = _rms(h)
        xh = h * r
        dxn_ = dx_ref[...]
        dxw = dxn_ * w_ref[...]
        dh = jnp.where(i > 0, dy_ref[...], 0.0) + r * (dxw - xh * jnp.mean(dxw * xh, axis=-1, keepdims=True))
        gx_ref[...] = dh

        @pl.when(i == 0)
        def _():
            dw_ref[...] = jnp.zeros_like(dw_ref)
            gm_ref[...] = dh[PAD_ROWS:, :]
        dw_ref[...] += jnp.sum(dxn_ * xh, axis=0, keepdims=True)
        pl.when(i == lp // Q_BLOCK - 1)(finish)

    vec = pl.BlockSpec((1, d), lambda i: (0, 0))
    met = pl.BlockSpec((N_META, d), lambda i: (0, 0))
    outs, got = _hosted_call(
        body, [rider], 5, 3, 0, grid=(lp // Q_BLOCK,),
        in_specs=[pl.BlockSpec((Q_BLOCK, d), lambda i: (i, 0)), _x_rows(d), met, vec, _x_rows(d)],
        out_specs=[_x_rows(d), met, vec],
        out_shape=[jax.ShapeDtypeStruct((seq, d), F32), jax.ShapeDtypeStruct((N_META, d), F32),
                   jax.ShapeDtypeStruct((1, d), F32)],
        name="prenorm_bwd", compiler_params=_cp("arbitrary"))(dxn, x, meta, w, dy)
    return outs, (got[0] if got else None)


def _layer_grads(x, target, meta, pre_w, wfull, conv_wt, a_log, dt_bias, gdn_norm_w, fq_w, fk_w, f_bias, w_out, post_w,
                 late_weights=None, w_out_grads=None):
    nh = a_log.shape[1]
    zpad = jnp.zeros((1, LANES - 3 * nh), F32)
    bias_row = jnp.concatenate([jnp.zeros((1, nh), F32), dt_bias, f_bias, zpad], axis=1)
    nega_row = jnp.concatenate([jnp.zeros((1, nh), F32), -jnp.exp(a_log), jnp.zeros((1, nh), F32), zpad], axis=1)
    qk_w = jnp.stack([fq_w, fk_w])

    xn = _prenorm(x, meta, pre_w)
    if late_weights is None:
        proj = _matmul(xn, wfull, "nn", MM_TILE, F32, "proj")
    else:
        proj, got = _matmul(xn, wfull, "nn", MM_TILE, F32, "proj", late_weights[0])
        conv_wt, w_out = late_weights[1](got)
    qkv = _gdn_prep(proj, conv_wt, nh)
    gates, gt3, gtf = _gates(proj, bias_row, nega_row, nh)
    o_gdn, s_all, t_all = _gdn_fwd(qkv, gates, gt3, nh)
    qkn = _fox_prep(proj, qk_w, nh)
    o_fox = _fox_fwd(qkn, proj, gates, gtf, nh)
    merged = _merge_fox(o_fox, proj, _merge_gdn(o_gdn, proj, gdn_norm_w, nh), nh)
    out = _matmul(merged, w_out, "nn", 4 * LANES, F32, "out_proj")
    dy, dout, loss_blk, dpost_w = _post(out, x, target, post_w)

    dw_out = _matmul(merged, dout, "tn", 4 * LANES, BF16, "dw_out")
    if w_out_grads is None:
        dmerged, gdn_rider = _matmul(dout, w_out, "nt", 4 * LANES, F32, "dmerged"), None
    else:
        dmerged, got = _matmul(dout, w_out, "nt", 4 * LANES, F32, "dmerged", w_out_grads[0](dw_out))
        gdn_rider = w_out_grads[1](dw_out, got)
    do_gdn, dproj, dgdn_norm_w = _merge_gdn_bwd(o_gdn, proj, gdn_norm_w, dmerged, nh)
    do_fox, dproj = _merge_fox_bwd(o_fox, proj, dmerged, dproj, nh)
    dqn, dkn, dc_t, dproj = _fox_bwd(qkn, proj, gates, gtf, do_fox, dproj, nh)
    dproj, dqk_w = _fox_prep_bwd(proj, qk_w, dqn, dkn, dproj, nh)
    (dgq, dgk, dgv, dgate), w_out_parts = _gdn_bwd(qkv, gates, gt3, s_all, t_all, do_gdn, nh, gdn_rider)
    dproj, dconv_wt = _gdn_prep_bwd(proj, conv_wt, dgq, dgk, dgv, dproj, nh)
    dc_rows = jnp.pad(dc_t.reshape(nh, -1), ((2 * nh, LANES - 3 * nh), (0, 0)))
    dproj, gate_sums = _gates_bwd(proj, bias_row, nega_row, gates, dgate, dc_rows, dproj, nh)
    return dict(
        loss=loss_blk[0:1, 0:1], dy=dy, xn=xn, dproj=dproj, post_w=dpost_w,
        conv_wt=dconv_wt, a_log=gate_sums[1:2, nh:2 * nh], dt_bias=gate_sums[0:1, nh:2 * nh],
        gdn_norm_w=dgdn_norm_w, fq_w=dqk_w[0], fk_w=dqk_w[1], f_bias=gate_sums[0:1, 2 * nh:3 * nh], w_out=dw_out,
        w_out_parts=w_out_parts)


def _cast_bf16(a, tr, name):
    r, c = a.shape

    def body(a_ref, o_ref):
        o_ref[...] = a_ref[...].astype(BF16)

    return pl.pallas_call(
        body, grid=(r // tr,), in_specs=[pl.BlockSpec((tr, c), lambda i: (i, 0))],
        out_specs=pl.BlockSpec((tr, c), lambda i: (i, 0)), out_shape=jax.ShapeDtypeStruct((r, c), BF16),
        name=name, compiler_params=_cp("parallel"))(a)


def _gather_copies(ins, outs, send_sems, recv_sems, local_sems):
    n = len(ins)
    x, y, c = lax.axis_index("x"), lax.axis_index("y"), lax.axis_index("c")
    me, sibling = (x, y, c), (x, y, 1 - c)
    xn, yn, dg = (1 - x, y), (x, 1 - y), (1 - x, 1 - y)

    def copy(a, k, block, to, src=None):
        px, py, pc = block
        rows = outs[a].at[4 * px + 2 * py + pc]
        return pltpu.make_async_remote_copy(
            src_ref=rows if src is None else src, dst_ref=rows, send_sem=send_sems.at[a, k],
            recv_sem=recv_sems.at[a, k], device_id=to, device_id_type=_MESH)

    local = [pltpu.make_async_copy(ins[a], outs[a].at[4 * x + 2 * y + c], local_sems.at[a]) for a in range(n)]
    own = [cp for a in range(n) for cp in (copy(a, 0, me, sibling, src=ins[a]), copy(a, 1, me, (*xn, c), src=ins[a]),
                                           copy(a, 2, me, (*yn, c), src=ins[a]))]

    def start():
        for cp in local + own:
            cp.start()

    def finish():
        for a in range(n):
            @pl.when(c == 1)
            def _(a=a):
                copy(a, 1, (*xn, c), me).wait_recv()
                copy(a, 3, (*xn, c), (*yn, c)).start()

            @pl.when(c == 0)
            def _(a=a):
                copy(a, 2, (*yn, c), me).wait_recv()
                copy(a, 3, (*yn, c), (*xn, c)).start()
        for a in range(n):
            pl.when(c == 0)(copy(a, 1, (*xn, c), me).wait_recv)
            copy(a, 4, (*xn, c), sibling).start()
            pl.when(c == 1)(copy(a, 2, (*yn, c), me).wait_recv)
            copy(a, 5, (*yn, c), sibling).start()
        for a in range(n):
            copy(a, 3, (*dg, c), me).wait_recv()
            copy(a, 6, (*dg, c), sibling).start()
        for a in range(n):
            copy(a, 0, sibling, me).wait_recv()
            for k, chip in ((4, xn), (5, yn), (6, dg)):
                copy(a, k, (*chip, 1 - c), me).wait_recv()
                copy(a, k, (*chip, c), sibling).wait_send()
            copy(a, 3, (*xn, c), (*yn, c)).wait_send()
        for cp in own:
            cp.wait_send()
        for cp in local:
            cp.wait()

    return start, finish


def _gather_scratch(n):
    return [pltpu.SemaphoreType.DMA((n, N_DEV - 1)), pltpu.SemaphoreType.DMA((n, N_DEV - 1)), pltpu.SemaphoreType.DMA((n,))]


def _gather_rider(arrays):
    return _Rider(list(arrays), [jax.ShapeDtypeStruct((N_DEV,) + a.shape, a.dtype) for a in arrays],
                  _gather_scratch(len(arrays)), {}, lambda ins, outs, scratch: _gather_copies(ins, outs, *scratch))


def _all_gather(arrays, name):
    n = len(arrays)

    def body(*refs):
        start, finish = _gather_copies(refs[:n], refs[n:2 * n], *refs[2 * n:])
        start()
        finish()

    return pl.pallas_call(
        body, in_specs=[_ANY] * n, out_specs=[_ANY] * n,
        out_shape=[jax.ShapeDtypeStruct((N_DEV,) + a.shape, a.dtype) for a in arrays],
        scratch_shapes=_gather_scratch(n), name=name)(*arrays)


SLAB = 10 * LANES


def _slab_start(blk, nh, cols):
    in_second_half = blk >= N_DEV // 2
    shift = (2 * nh if in_second_half else 0) if isinstance(blk, int) else jnp.where(in_second_half, 2 * nh, 0)
    return (blk * cols - shift) // LANES * LANES


def _pair_rider(dw_rows=None, parts=None, nh=None):
    if dw_rows is not None:
        r, full = dw_rows.shape
        cols = (full - NARROW + 3 * nh) // N_DEV
        out_shapes = [jax.ShapeDtypeStruct((N_CHIP, r, SLAB), dw_rows.dtype), jax.ShapeDtypeStruct((r, NARROW), dw_rows.dtype)]
    else:
        out_shapes = [jax.ShapeDtypeStruct((N_CHIP,) + parts.shape[1:], parts.dtype)]

    def make(ins, outs, scratch):
        send_sems, recv_sems = scratch
        x, y, c = lax.axis_index("x"), lax.axis_index("y"), lax.axis_index("c")
        kw = lambda k: dict(send_sem=send_sems.at[k], recv_sem=recv_sems.at[k], device_id=(x, y, 1 - c), device_id_type=_MESH)
        copies = []
        for q in range(N_CHIP):
            if dw_rows is not None:
                first = pl.multiple_of(_slab_start(2 * q + 1 - c, nh, cols), LANES)
                copies.append(pltpu.make_async_remote_copy(src_ref=ins[0].at[:, pl.ds(first, SLAB)], dst_ref=outs[0].at[q], **kw(q)))
            else:
                copies.append(pltpu.make_async_remote_copy(src_ref=ins[0].at[2 * q + 1 - c], dst_ref=outs[0].at[q], **kw(q)))
        if dw_rows is not None:
            copies.append(pltpu.make_async_remote_copy(src_ref=ins[0].at[:, pl.ds(full - NARROW, NARROW)], dst_ref=outs[1],
                                                       **kw(N_CHIP)))

        def start():
            for cp in copies:
                cp.start()

        def finish():
            for cp in copies:
                cp.wait()

        return start, finish

    return _Rider([dw_rows if dw_rows is not None else parts], out_shapes,
                  [pltpu.SemaphoreType.DMA((N_CHIP + 1,)), pltpu.SemaphoreType.DMA((N_CHIP + 1,))], {}, make)


def _relayout_pair_sum(dwfull, got_slabs, got_tail, core, nh, tr, name):
    d, full = dwfull.shape
    w = nh * HEAD_DIM
    cols = (8 * w + 3 * nh) // N_DEV
    segs = _native_segments(nh)

    def block(f_ref, s_ref, t_ref, q, blk):
        st = _slab_start(blk, nh, cols)
        wide = f_ref[:, st:st + SLAB].astype(F32) + s_ref[q].astype(F32)
        tail = f_ref[:, 8 * w:].astype(F32) + t_ref[...].astype(F32)
        pieces = []
        for s0, s1, t0 in segs:
            lo, hi = max(s0, blk * cols), min(s1, (blk + 1) * cols)
            if lo < hi:
                at = t0 + lo - s0
                pieces.append(tail[:, at - 8 * w:at - 8 * w + hi - lo] if at >= 8 * w else wide[:, at - st:at - st + hi - lo])
        return (pieces[0] if len(pieces) == 1 else jnp.concatenate(pieces, axis=1)).astype(dwfull.dtype)

    def body(core_ref, f_ref, s_ref, t_ref, o_ref):
        for parity in range(2):
            @pl.when(core_ref[0] == parity)
            def _(parity=parity):
                for q in range(N_CHIP):
                    o_ref[q] = block(f_ref, s_ref, t_ref, q, 2 * q + parity)

    return pl.pallas_call(
        body,
        grid_spec=pltpu.PrefetchScalarGridSpec(
            num_scalar_prefetch=1, grid=(d // tr,),
            in_specs=[pl.BlockSpec((tr, full), lambda i, c_ref: (i, 0)), pl.BlockSpec((N_CHIP, tr, SLAB), lambda i, c_ref: (0, i, 0)),
                      pl.BlockSpec((tr, NARROW), lambda i, c_ref: (i, 0))],
            out_specs=pl.BlockSpec((N_CHIP, tr, cols), lambda i, c_ref: (0, i, 0))),
        out_shape=jax.ShapeDtypeStruct((N_CHIP, d, cols), dwfull.dtype), name=name,
        compiler_params=_cp("parallel"))(core, dwfull, got_slabs, got_tail)


def _pair_sum(parts, got, core, tr, name):
    _, r, c = parts.shape

    def body(core_ref, p_ref, g_ref, o_ref):
        o_ref[...] = (p_ref[...].astype(F32) + g_ref[...].astype(F32)).astype(o_ref.dtype)

    return pl.pallas_call(
        body,
        grid_spec=pltpu.PrefetchScalarGridSpec(
            num_scalar_prefetch=1, grid=(N_CHIP, r // tr),
            in_specs=[pl.BlockSpec((1, tr, c), lambda q, i, core_ref: (2 * q + core_ref[0], i, 0)),
                      pl.BlockSpec((1, tr, c), lambda q, i, core_ref: (q, i, 0))],
            out_specs=pl.BlockSpec((1, tr, c), lambda q, i, core_ref: (q, i, 0))),
        out_shape=jax.ShapeDtypeStruct((N_CHIP, r, c), parts.dtype), name=name,
        compiler_params=_cp("parallel", "parallel"))(core, parts, got)


def _native_segments(nh):
    w = nh * HEAD_DIM
    return [(0, 4 * w, 0), (4 * w, 4 * w + 2 * nh, 8 * w), (4 * w + 2 * nh, 8 * w + 2 * nh, 4 * w),
            (8 * w + 2 * nh, 8 * w + 3 * nh, 8 * w + 2 * nh)]


def _relayout_w_in(wg, nh, tr):
    _, d, cols = wg.shape
    w = nh * HEAD_DIM

    def native(ref, j0, j1):
        out = []
        while j0 < j1:
            blk = j0 // cols
            end = min(j1, (blk + 1) * cols)
            out.append(ref[blk, :, pl.ds(j0 - blk * cols, end - j0)])
            j0 = end
        return out

    def body(g_ref, o_ref):
        for cidx in range(8 * w // LANES):
            j0 = cidx * LANES + (0 if cidx * LANES < 4 * w else 2 * nh)
            pieces = native(g_ref, j0, j0 + LANES)
            o_ref[:, cidx * LANES:(cidx + 1) * LANES] = pieces[0] if len(pieces) == 1 else jnp.concatenate(pieces, axis=1)
        pieces = (native(g_ref, 4 * w, 4 * w + 2 * nh) + native(g_ref, 8 * w + 2 * nh, 8 * w + 3 * nh)
                  + [jnp.zeros((tr, NARROW - 3 * nh), wg.dtype)])
        o_ref[:, 8 * w:] = jnp.concatenate(pieces, axis=1)

    return pl.pallas_call(
        body, grid=(d // tr,), in_specs=[pl.BlockSpec((N_DEV, tr, cols), lambda i: (0, i, 0))],
        out_specs=pl.BlockSpec((tr, 8 * w + NARROW), lambda i: (i, 0)),
        out_shape=jax.ShapeDtypeStruct((d, 8 * w + NARROW), wg.dtype),
        name="relayout_w_in", compiler_params=_cp("parallel"))(wg)


def _adamw(w, parts, m, v, tr, name):
    r, c = w.shape
    n_parts = parts.shape[0]

    def body(w_ref, p_ref, m_ref, v_ref, g_ref, d_ref, nm_ref, nv_ref):
        g = p_ref[0].astype(F32)
        for s in range(1, n_parts):
            g = g + p_ref[s].astype(F32)
        m_new = ADAM_B1 * m_ref[...] + (1.0 - ADAM_B1) * g
        v_new = ADAM_B2 * v_ref[...] + (1.0 - ADAM_B2) * (g * g)
        m_hat = m_new / (1.0 - ADAM_B1 ** ADAM_STEP)
        v_hat = v_new / (1.0 - ADAM_B2 ** ADAM_STEP)
        g_ref[...] = g
        d_ref[...] = -ADAM_LR * (m_hat / (jnp.sqrt(v_hat) + ADAM_EPS) + ADAM_WD * w_ref[...])
        nm_ref[...] = m_new
        nv_ref[...] = v_new

    blk = pl.BlockSpec((tr, c), lambda i: (i, 0))
    return pl.pallas_call(
        body, grid=(r // tr,), in_specs=[blk, pl.BlockSpec((n_parts, tr, c), lambda i: (0, i, 0)), blk, blk],
        out_specs=[blk] * 4, out_shape=[jax.ShapeDtypeStruct((r, c), F32)] * 4, name=name,
        compiler_params=_cp("parallel"))(w, parts, m, v)


def _pack_small(d, pre, post, a_log, dt_bias, f_bias, gdn_w, fq_w, fk_w, extra):
    row2 = jnp.concatenate([a_log, dt_bias, f_bias, gdn_w, fq_w, fk_w, extra], axis=1)
    row2 = jnp.pad(row2, ((0, 0), (0, d - row2.shape[1])))
    return jnp.concatenate([pre, post, row2, jnp.zeros((5, d), F32)], axis=0)


def _unpack_small(p, nh):
    o = 3 * nh
    return dict(pre=p[0:1], post=p[1:2], a_log=p[2:3, 0:nh], dt_bias=p[2:3, nh:2 * nh], f_bias=p[2:3, 2 * nh:o],
                gdn_w=p[2:3, o:o + HEAD_DIM], fq_w=p[2:3, o + HEAD_DIM:o + 2 * HEAD_DIM],
                fk_w=p[2:3, o + 2 * HEAD_DIM:o + 3 * HEAD_DIM], extra=p[2, o + 3 * HEAD_DIM])


def kernel(x, meta_tokens, pre_norm_w, w_in, conv_w, a_log, dt_bias, gdn_norm_w, fox_q_norm_w, fox_k_norm_w, fox_f_bias, w_out, post_norm_w, loss_target, m_meta_tokens, m_pre_norm_w, m_w_in, m_conv_w, m_a_log, m_dt_bias, m_gdn_norm_w, m_fox_q_norm_w, m_fox_k_norm_w, m_fox_f_bias, m_w_out, m_post_norm_w, v_meta_tokens, v_pre_norm_w, v_w_in, v_conv_w, v_a_log, v_dt_bias, v_gdn_norm_w, v_fox_q_norm_w, v_fox_k_norm_w, v_fox_f_bias, v_w_out, v_post_norm_w):
    nh = a_log.shape[1]
    d = x.shape[-1]
    w = nh * HEAD_DIM
    zero = jnp.zeros((1, 1), F32)

    wg, mg = _all_gather([_cast_bf16(w_in[0], 256, "cast_w_in"), meta_tokens], "gather_weights")
    wfull = _relayout_w_in(wg, nh, 256)
    meta_full = mg.transpose(1, 0, 2).reshape(N_META, d)
    late_weights = (_gather_rider([conv_w[0].T, _cast_bf16(w_out[0], 256, "cast_w_out")]),
                    lambda got: (got[0].transpose(1, 0, 2).reshape(CONV_WIDTH, 3 * w), got[1].reshape(2 * w, d)))
    core = lax.axis_index("c")
    dev = 4 * lax.axis_index("x") + 2 * lax.axis_index("y") + core
    core_arr = jnp.reshape(core, (1,)).astype(jnp.int32)

    out_parts = lambda dw_out: dw_out.reshape(N_DEV, 2 * w // N_DEV, d)
    g = _layer_grads(
        x[0], loss_target[0], meta_full, pre_norm_w, wfull, None, a_log, dt_bias, gdn_norm_w,
        fox_q_norm_w, fox_k_norm_w, fox_f_bias, None, post_norm_w, late_weights=late_weights,
        w_out_grads=(lambda dw_out: _pair_rider(parts=out_parts(dw_out)),
                     lambda dw_out, got: _chip_rider(_pair_sum(out_parts(dw_out), got[0], core_arr, 256, "pair_sum_w_out"))))
    p_out = g["w_out_parts"][0]
    xn, dproj, half = g["xn"], g["dproj"], d // 2
    dw_a = _matmul(xn[:, :half], dproj, "tn", MM_TILE, BF16, "dw_in_a")
    dw_b, got_a = _matmul(xn[:, half:], dproj, "tn", MM_TILE, BF16, "dw_in_b", _pair_rider(dw_rows=dw_a, nh=nh))
    sums_a = _relayout_pair_sum(dw_a, got_a[0], got_a[1], core_arr, nh, 128, "relayout_pair_sum_a")
    dxn, (p_in_a, got_b) = _dxn(dproj, wfull, [_chip_rider(sums_a, rows_total=d), _pair_rider(dw_rows=dw_b, nh=nh)],
                                MM_TILE, "dxn")
    sums_b = _relayout_pair_sum(dw_b, got_b[0], got_b[1], core_arr, nh, 128, "relayout_pair_sum_b")
    (grad_x, dmeta, dpre_w), p_in = _prenorm_bwd(dxn, x[0], meta_full, pre_norm_w, g["dy"],
                                                 _chip_rider(sums_b, rows_total=d, row0=half, into=p_in_a[0]))
    p_in = p_in[0]
    small = _pack_small(d, dpre_w, g["post_w"], g["a_log"], g["dt_bias"], g["f_bias"], g["gdn_norm_w"], g["fq_w"],
                        g["fk_w"], g["loss"])
    a_conv, a_meta, p_small = _all_gather([g["conv_wt"], dmeta, small], "gather_small_grads")
    p_conv = lax.dynamic_slice_in_dim(a_conv, dev * conv_w.shape[1], conv_w.shape[1], axis=2).transpose(0, 2, 1)
    p_meta = lax.dynamic_slice_in_dim(a_meta, dev * meta_tokens.shape[1], meta_tokens.shape[1], axis=2)

    r_in = _adamw(w_in[0], p_in, m_w_in[0], v_w_in[0], 128, "adamw_w_in")
    r_out = _adamw(w_out[0], p_out, m_w_out[0], v_w_out[0], 64, "adamw_w_out")
    r_conv = _adamw(conv_w[0], p_conv, m_conv_w[0], v_conv_w[0], conv_w.shape[1], "adamw_conv_w")
    r_meta = _adamw(meta_tokens, p_meta, m_meta_tokens, v_meta_tokens, N_META, "adamw_meta")
    pk = lambda pre, post, a, dt, gw, fq, fk, fb: _pack_small(d, pre, post, a, dt, fb, gw, fq, fk, zero)
    r_small = _adamw(
        pk(pre_norm_w, post_norm_w, a_log, dt_bias, gdn_norm_w, fox_q_norm_w, fox_k_norm_w, fox_f_bias), p_small,
        pk(m_pre_norm_w, m_post_norm_w, m_a_log, m_dt_bias, m_gdn_norm_w, m_fox_q_norm_w, m_fox_k_norm_w, m_fox_f_bias),
        pk(v_pre_norm_w, v_post_norm_w, v_a_log, v_dt_bias, v_gdn_norm_w, v_fox_q_norm_w, v_fox_k_norm_w, v_fox_f_bias),
        8, "adamw_small")

    sm = [_unpack_small(r, nh) for r in r_small]
    outs = []
    for i in range(4):
        s = sm[i]
        outs += [r_meta[i], s["pre"], r_in[i][None], r_conv[i][None], s["a_log"], s["dt_bias"], s["gdn_w"], s["fq_w"],
                 s["fk_w"], s["f_bias"], r_out[i][None], s["post"]]
    return (sm[0]["extra"], grad_x[None], *outs)
```

```python
import jax
import jax.numpy as jnp
from jax import lax
from jax.experimental import pallas as pl
from jax.experimental.pallas import tpu as pltpu

F32, BF16 = jnp.float32, jnp.bfloat16
HEAD_DIM = 128
N_META = 16
CONV_WIDTH = 4
CHUNK = 64
Q_BLOCK = 128
LANES = 128
EPS = 1e-6
PAD_ROWS = Q_BLOCK - N_META
N_DEV = 8
N_CHIP = 4
VMEM_LIMIT = 56 * 1024 * 1024
NEG = -1e30
NARROW = 2 * LANES
MM_TILE = 6 * LANES

ADAM_LR, ADAM_B1, ADAM_B2, ADAM_EPS, ADAM_WD, ADAM_STEP = 0.001, 0.9, 0.999, 1e-08, 0.01, 10

_DN = {"nn": (((1,), (0,)), ((), ())), "nt": (((1,), (1,)), ((), ())), "tn": (((0,), (0,)), ((), ()))}
_DN3 = {"nn": (((2,), (1,)), ((0,), (0,))), "nt": (((2,), (2,)), ((0,), (0,))), "tn": (((1,), (1,)), ((0,), (0,)))}
_ANY = pl.BlockSpec(memory_space=pl.ANY)
_MESH = pl.DeviceIdType.MESH


def _cp(*sem):
    return pltpu.CompilerParams(dimension_semantics=sem, vmem_limit_bytes=VMEM_LIMIT)


def _dot(a, b, dims="nn", prec=None):
    return lax.dot_general(a, b, _DN[dims], precision=prec, preferred_element_type=F32)


def _bdot(a, b, dims="nn"):
    return _dot(a.astype(BF16), b.astype(BF16), dims)


def _hdot(a, b, dims="nn"):
    return _dot(a, b, dims, prec=lax.Precision.HIGHEST)


def _dot3(a, b, dims="nn"):
    return lax.dot_general(a, b, _DN3[dims], preferred_element_type=F32)


def _bdot3(a, b, dims="nn"):
    return _dot3(a.astype(BF16), b.astype(BF16), dims)


def _split(a):
    hi = a.astype(BF16)
    return hi, (a - hi.astype(F32)).astype(BF16)


def _iota(shape, dim):
    return lax.broadcasted_iota(jnp.int32, shape, dim)


def _sigmoid(z):
    return 1.0 / (1.0 + jnp.exp(-z))


def _softplus(z):
    e = jnp.exp(-jnp.abs(z))
    u = 1.0 + e
    l1p = jnp.where(u == 1.0, e, jnp.log(u) * (e / jnp.where(u == 1.0, 1.0, u - 1.0)))
    return jnp.maximum(z, 0.0) + l1p


def _silu_and_grad(z):
    s = _sigmoid(z)
    return z * s, s * (1.0 + z * (1.0 - s))


def _rms(x):
    return lax.rsqrt(jnp.mean(x * x, axis=-1, keepdims=True) + EPS)


def _h_tile(i, x_ref, meta_ref):
    first = jnp.concatenate([jnp.zeros((PAD_ROWS, x_ref.shape[1]), F32), meta_ref[...]], axis=0)
    return jnp.where(i == 0, first, x_ref[...])


def _x_rows(d):
    return pl.BlockSpec((Q_BLOCK, d), lambda i: (jnp.maximum(i - 1, 0), 0))


def _prenorm(x, meta, w):
    seq, d = x.shape
    lp = seq + Q_BLOCK

    def body(x_ref, m_ref, w_ref, o_ref):
        h = _h_tile(pl.program_id(0), x_ref, m_ref)
        o_ref[...] = (h * _rms(h) * w_ref[...]).astype(BF16)

    return pl.pallas_call(
        body, grid=(lp // Q_BLOCK,),
        in_specs=[_x_rows(d), pl.BlockSpec((N_META, d), lambda i: (0, 0)), pl.BlockSpec((1, d), lambda i: (0, 0))],
        out_specs=pl.BlockSpec((Q_BLOCK, d), lambda i: (i, 0)),
        out_shape=jax.ShapeDtypeStruct((lp, d), BF16), name="prenorm", compiler_params=_cp("parallel"))(x, meta, w)


def _tile(n, want):
    return max(t for t in range(LANES, want + 1, LANES) if n % t == 0)


class _Rider:
    def __init__(self, inputs, out_shapes, scratch, aliases, make):
        self.inputs, self.out_shapes, self.scratch, self.aliases, self.make = inputs, out_shapes, scratch, aliases, make


def _hosted_call(body, riders, n_in, n_out, n_scratch, *, in_specs, out_specs, out_shape, scratch_shapes=(), aliases=None,
                 **kw):
    riders = [r for r in riders if r is not None]
    r_in = [len(r.inputs) for r in riders]
    r_out = [len(r.out_shapes) for r in riders]
    r_scr = [len(r.scratch) for r in riders]
    al = dict(aliases or {})
    for k, r in enumerate(riders):
        al.update({n_in + sum(r_in[:k]) + i: n_out + sum(r_out[:k]) + o for i, o in r.aliases.items()})

    def full_body(*refs):
        ins, rest = refs[:n_in + sum(r_in)], refs[n_in + sum(r_in):]
        outs, scr = rest[:n_out + sum(r_out)], rest[n_out + sum(r_out):]
        hooks = [r.make(ins[n_in + sum(r_in[:k]):n_in + sum(r_in[:k + 1])], outs[n_out + sum(r_out[:k]):n_out + sum(r_out[:k + 1])],
                        scr[n_scratch + sum(r_scr[:k]):n_scratch + sum(r_scr[:k + 1])]) for k, r in enumerate(riders)]

        def start():
            for h in hooks:
                h[0]()

        def finish():
            for h in hooks:
                h[1]()

        body(start, finish, *ins[:n_in], *outs[:n_out], *scr[:n_scratch])

    call = pl.pallas_call(
        full_body, in_specs=list(in_specs) + [_ANY] * sum(r_in), out_specs=list(out_specs) + [_ANY] * sum(r_out),
        out_shape=list(out_shape) + [s for r in riders for s in r.out_shapes],
        scratch_shapes=list(scratch_shapes) + [s for r in riders for s in r.scratch], input_output_aliases=al, **kw)

    def run(*args):
        res = call(*args, *[t for r in riders for t in r.inputs])
        return res[:n_out], [res[n_out + sum(r_out[:k]):n_out + sum(r_out[:k + 1])] for k in range(len(riders))]

    return run


def _matmul(a, b, dims, tn, out_dtype, name, rider=None):
    m = a.shape[1] if dims == "tn" else a.shape[0]
    n = b.shape[0] if dims == "nt" else b.shape[1]
    kdim = b.shape[1] if dims == "nt" else b.shape[0]
    tn = _tile(n, tn)
    steps = n // tn
    b_spec = pl.BlockSpec((tn, kdim), lambda j: (j, 0)) if dims == "nt" else pl.BlockSpec((kdim, tn), lambda j: (0, j))

    def body(start, finish, a_ref, b_ref, o_ref):
        pl.when(pl.program_id(0) == 0)(start)
        o_ref[...] = _dot(a_ref[...], b_ref[...], dims).astype(out_dtype)
        pl.when(pl.program_id(0) == steps - 1)(finish)

    (out,), got = _hosted_call(
        body, [rider], 2, 1, 0, grid=(steps,), in_specs=[pl.BlockSpec(a.shape, lambda j: (0, 0)), b_spec],
        out_specs=[pl.BlockSpec((m, tn), lambda j: (0, j))], out_shape=[jax.ShapeDtypeStruct((m, n), out_dtype)],
        name=name, compiler_params=_cp("parallel" if rider is None else "arbitrary"))(a, b)
    return out if rider is None else (out, got[0])


def _chip_rider(sums, rows_total=None, row0=0, into=None):
    _, r, c = sums.shape
    rows_total = rows_total or r

    def make(ins, outs, scratch):
        send_sems, recv_sems, local_sem = scratch
        x, y, core = lax.axis_index("x"), lax.axis_index("y"), lax.axis_index("c")
        mine = 2 * x + y
        land = lambda chip: outs[0].at[chip].at[pl.ds(row0, r)]
        local = pltpu.make_async_copy(ins[0].at[mine], land(mine), local_sem)
        sends, recvs = [], []
        for k in range(1, N_CHIP):
            px = 1 - x if k & 2 else x
            py = 1 - y if k & 1 else y
            kw = dict(send_sem=send_sems.at[k - 1], recv_sem=recv_sems.at[k - 1], device_id=(px, py, core),
                      device_id_type=_MESH)
            sends.append(pltpu.make_async_remote_copy(src_ref=ins[0].at[2 * px + py], dst_ref=land(mine), **kw))
            recvs.append(pltpu.make_async_remote_copy(src_ref=ins[0].at[mine], dst_ref=land(2 * px + py), **kw))

        def start():
            for cp in [local] + sends:
                cp.start()

        def finish():
            local.wait()
            for cp in sends:
                cp.wait_send()
            for cp in recvs:
                cp.wait_recv()

        return start, finish

    return _Rider([sums] + ([] if into is None else [into]), [jax.ShapeDtypeStruct((N_CHIP, rows_total, c), sums.dtype)],
                  [pltpu.SemaphoreType.DMA((N_CHIP - 1,)), pltpu.SemaphoreType.DMA((N_CHIP - 1,)), pltpu.SemaphoreType.DMA(())],
                  {} if into is None else {1: 0}, make)


def _dxn(dproj, wfull, riders, tk, name):
    m, k = dproj.shape
    n = wfull.shape[0]
    tk = _tile(k, tk)
    steps = k // tk

    def body(start, finish, a_ref, b_ref, o_ref):
        j = pl.program_id(0)

        @pl.when(j == 0)
        def _():
            start()
            o_ref[...] = jnp.zeros_like(o_ref)
        o_ref[...] += _dot(a_ref[...], b_ref[...], "nt")
        pl.when(j == steps - 1)(finish)

    (dxn,), got = _hosted_call(
        body, riders, 2, 1, 0, grid=(steps,),
        in_specs=[pl.BlockSpec((m, tk), lambda j: (0, j)), pl.BlockSpec((n, tk), lambda j: (0, j))],
        out_specs=[pl.BlockSpec((m, n), lambda j: (0, 0))], out_shape=[jax.ShapeDtypeStruct((m, n), F32)],
        name=name, compiler_params=_cp("arbitrary"))(dproj, wfull)
    return dxn, got


def _conv_taps(x, w):
    c = x * w[CONV_WIDTH - 1:CONV_WIDTH, :]
    for j in range(CONV_WIDTH - 1):
        c = c + pltpu.roll(x, CONV_WIDTH - 1 - j, 0) * w[j:j + 1, :]
    return c


def _gdn_prep(proj, conv_wt, nh):
    lp = proj.shape[0]
    scale = HEAD_DIM ** -0.5

    def body(x_ref, w_ref, o_ref):
        which = pl.program_id(0) // nh
        c = _conv_taps(x_ref[...], w_ref[...])
        s = c * _sigmoid(c)
        r = lax.rsqrt(jnp.sum(s * s, axis=-1, keepdims=True) + EPS)
        f = jnp.where(which == 0, r * scale, jnp.where(which == 1, r, 1.0))
        o_ref[...] = jnp.where(_iota(s.shape, 0) >= PAD_ROWS, s * f, 0.0)

    return pl.pallas_call(
        body, grid=(3 * nh,),
        in_specs=[pl.BlockSpec((lp, LANES), lambda s: (0, s)), pl.BlockSpec((CONV_WIDTH, LANES), lambda s: (0, s))],
        out_specs=pl.BlockSpec((lp, LANES), lambda s: (0, s)),
        out_shape=jax.ShapeDtypeStruct((lp, 3 * nh * HEAD_DIM), F32), name="gdn_prep",
        compiler_params=_cp("parallel"))(proj, conv_wt)


def _gdn_prep_bwd(proj, conv_wt, dq, dk, dv, dproj, nh):
    lp = proj.shape[0]
    scale = HEAD_DIM ** -0.5
    part = lambda p: pl.BlockSpec((lp, LANES), lambda s: (0, jnp.clip(s - p * nh, 0, nh - 1)))

    def body(x_ref, w_ref, dq_ref, dk_ref, dv_ref, _, dx_ref, dw_ref):
        which = pl.program_id(0) // nh
        x = x_ref[...]
        w = w_ref[...]
        c = _conv_taps(x, w)
        sg = _sigmoid(c)
        s = c * sg
        r = lax.rsqrt(jnp.sum(s * s, axis=-1, keepdims=True) + EPS)
        dy = jnp.where(which == 0, dq_ref[...], jnp.where(which == 1, dk_ref[...], dv_ref[...]))
        dy = jnp.where(_iota(s.shape, 0) >= PAD_ROWS, dy, 0.0)
        y0 = s * r
        dy0 = dy * jnp.where(which == 0, scale, 1.0)
        ds_n = r * (dy0 - y0 * jnp.sum(dy0 * y0, axis=-1, keepdims=True))
        ds = jnp.where(which == 2, dy, ds_n)
        dc = ds * (sg * (1.0 + c * (1.0 - sg)))
        dx = dc * w[CONV_WIDTH - 1:CONV_WIDTH, :]
        rows = [jnp.sum(dc * x, axis=0, keepdims=True)]
        for j in range(CONV_WIDTH - 2, -1, -1):
            sh = CONV_WIDTH - 1 - j
            dx = dx + pltpu.roll(dc, lp - sh, 0) * w[j:j + 1, :]
            rows.insert(0, jnp.sum(dc * pltpu.roll(x, sh, 0), axis=0, keepdims=True))
        dx_ref[...] = dx.astype(BF16)
        dw_ref[...] = jnp.concatenate(rows, axis=0)

    strip = pl.BlockSpec((lp, LANES), lambda s: (0, s))
    taps = pl.BlockSpec((CONV_WIDTH, LANES), lambda s: (0, s))
    return pl.pallas_call(
        body, grid=(3 * nh,), in_specs=[strip, taps, part(0), part(1), part(2), _ANY], out_specs=[strip, taps],
        out_shape=[jax.ShapeDtypeStruct(dproj.shape, BF16), jax.ShapeDtypeStruct((CONV_WIDTH, 3 * nh * HEAD_DIM), F32)],
        input_output_aliases={5: 0}, name="gdn_prep_bwd", compiler_params=_cp("parallel"))(proj, conv_wt, dq, dk, dv, dproj)


def _gates(proj, bias_row, nega_row, nh):
    lp = proj.shape[0]
    nc = lp // CHUNK

    def body(p_ref, b_ref, a_ref, g_ref, gt3_ref, gtf_ref):
        lane = _iota((CHUNK, LANES), 1)
        tri = (_iota((CHUNK, CHUNK), 0) >= _iota((CHUNK, CHUNK), 1)).astype(F32)

        def step(n, carry):
            r0 = pl.multiple_of(n * CHUNK, CHUNK)
            z = p_ref[pl.ds(r0, CHUNK), :] + b_ref[...]
            base = jnp.where(lane < nh, _sigmoid(z),
                             jnp.where(lane < 2 * nh, a_ref[...] * _softplus(z),
                                       jnp.where(lane < 3 * nh, -_softplus(-z), 0.0)))
            base = jnp.where(r0 + _iota((CHUNK, LANES), 0) >= PAD_ROWS, base, 0.0)
            cs = _hdot(tri, base)
            run = jnp.where((lane >= 2 * nh) & (lane < 3 * nh), cs + carry, cs)
            sh = pltpu.roll(run, 2 * nh, 1)
            out = base + jnp.where((lane >= 3 * nh) & (lane < 5 * nh), sh, 0.0)
            g_ref[pl.ds(r0, CHUNK), :] = out
            gt3_ref[n] = out.T
            return carry + cs[CHUNK - 1:CHUNK, :]

        lax.fori_loop(0, nc, step, jnp.zeros((1, LANES), F32))
        gtf_ref[...] = g_ref[...].T

    vec = pl.BlockSpec((1, LANES), lambda i: (0, 0))
    return pl.pallas_call(
        body, grid=(1,), in_specs=[pl.BlockSpec((lp, LANES), lambda i: (0, 8 * nh)), vec, vec],
        out_specs=[pl.BlockSpec((lp, LANES), lambda i: (0, 0)), pl.BlockSpec((nc, LANES, CHUNK), lambda i: (0, 0, 0)),
                   pl.BlockSpec((LANES, lp), lambda i: (0, 0))],
        out_shape=[jax.ShapeDtypeStruct((lp, LANES), F32), jax.ShapeDtypeStruct((nc, LANES, CHUNK), F32),
                   jax.ShapeDtypeStruct((LANES, lp), F32)],
        name="gates", compiler_params=_cp("arbitrary"))(proj, bias_row, nega_row)


def _gates_bwd(proj, bias_row, nega_row, gates, dgate_gdn, dc_t, dproj, nh):
    lp = proj.shape[0]
    nc = lp // CHUNK

    def body(p_ref, b_ref, a_ref, g_ref, dg_ref, dc_ref, _, dz_ref, sm_ref, dct_scr):
        lane = _iota((CHUNK, LANES), 1)
        triu = (_iota((CHUNK, CHUNK), 0) <= _iota((CHUNK, CHUNK), 1)).astype(F32)
        dct_scr[...] = dc_ref[...].T
        sm_ref[...] = jnp.zeros_like(sm_ref)
        dz_ref[:, LANES:] = jnp.zeros((lp, NARROW - LANES), BF16)

        def step(i, carry):
            n = nc - 1 - i
            r0 = pl.multiple_of(n * CHUNK, CHUNK)
            z = p_ref[pl.ds(r0, CHUNK), :] + b_ref[...]
            gt = g_ref[pl.ds(r0, CHUNK), :]
            dgd = dg_ref[pl.ds(r0, CHUNK), :]
            dch = dct_scr[pl.ds(r0, CHUNK), :]
            rc = _hdot(triu, dch) + carry
            sg = _sigmoid(z)
            dz = jnp.where(lane < nh, dgd * sg * (1.0 - sg),
                           jnp.where(lane < 2 * nh, dgd * a_ref[...] * sg,
                                     jnp.where(lane < 3 * nh, rc * (1.0 - sg), 0.0)))
            dz = jnp.where(r0 + _iota((CHUNK, LANES), 0) >= PAD_ROWS, dz, 0.0)
            dz_ref[pl.ds(r0, CHUNK), 0:LANES] = dz.astype(BF16)
            sm_ref[0:1, :] += jnp.sum(dz, axis=0, keepdims=True)
            sm_ref[1:2, :] += jnp.sum(jnp.where((lane >= nh) & (lane < 2 * nh), dgd * gt, 0.0), axis=0, keepdims=True)
            return carry + jnp.sum(dch, axis=0, keepdims=True)

        lax.fori_loop(0, nc, step, jnp.zeros((1, LANES), F32))

    vec = pl.BlockSpec((1, LANES), lambda i: (0, 0))
    full = pl.BlockSpec((lp, LANES), lambda i: (0, 0))
    last = pl.BlockSpec((lp, LANES), lambda i: (0, 8 * nh))
    tail = pl.BlockSpec((lp, NARROW), lambda i: (0, 8 * nh * LANES // NARROW))
    return pl.pallas_call(
        body, grid=(1,), in_specs=[last, vec, vec, full, full, pl.BlockSpec((LANES, lp), lambda i: (0, 0)), _ANY],
        out_specs=[tail, pl.BlockSpec((8, LANES), lambda i: (0, 0))],
        out_shape=[jax.ShapeDtypeStruct(dproj.shape, BF16), jax.ShapeDtypeStruct((8, LANES), F32)],
        scratch_shapes=[pltpu.VMEM((lp, LANES), F32)], input_output_aliases={6: 0},
        name="gates_bwd", compiler_params=_cp("arbitrary"))(proj, bias_row, nega_row, gates, dgate_gdn, dc_t, dproj)


def _tri_inv(a):
    t = jnp.where(_iota(a.shape, 1) == _iota(a.shape, 2), 1.0, 0.0) - a
    p = a
    for _ in range(5):
        ph, pw = _split(p)
        p = _dot3(ph, ph) + (_dot3(ph, pw) + _dot3(pw, ph))
        ph, pw = _split(p)
        th, tw = _split(t)
        t = t + (_dot3(th, ph) + (_dot3(th, pw) + _dot3(tw, ph)))
    return t


def _gdn_chunk(q, k, v, beta, gc, gr, t=None):
    ii, jj = _iota((1, CHUNK, CHUNK), 1), _iota((1, CHUNK, CHUNK), 2)
    causal, strict = ii >= jj, ii > jj
    dm = jnp.where(causal, jnp.exp(jnp.where(causal, gc - gr, 0.0)), 0.0)
    kk = _bdot3(k, k, "nt")
    a = jnp.where(strict, beta * kk * dm, 0.0)
    if t is None:
        t = _tri_inv(a)
    eg = jnp.exp(gc)
    glast = gc[:, CHUNK - 1:CHUNK, :]
    ekd = jnp.exp(glast - gc)
    bv = beta * v
    bk = (beta * eg) * k
    ub = _bdot3(t, jnp.concatenate([bv, bk], axis=2))
    qk = _bdot3(q, k, "nt")
    return dict(causal=causal, strict=strict, dm=dm, kk=kk, a=a, t=t, eg=eg, ekd=ekd, bv=bv, bk=bk,
                u=ub[:, :, :HEAD_DIM], w=ub[:, :, HEAD_DIM:], qk=qk, aqk=jnp.where(causal, qk * dm, 0.0),
                q_dec=q * eg, k_dec=k * ekd, decay=jnp.exp(glast))


def _heads(ref, nh):
    return jnp.stack([ref[:, h * HEAD_DIM:(h + 1) * HEAD_DIM] for h in range(nh)], axis=0)


def _gdn_chunk_inputs(q_ref, k_ref, v_ref, g, gt, nh):
    col = lambda o: jnp.stack([g[:, o + h:o + h + 1] for h in range(nh)], axis=0)
    gr = jnp.stack([gt[3 * nh + h:3 * nh + h + 1, :] for h in range(nh)], axis=0)
    return _heads(q_ref, nh), _heads(k_ref, nh), _heads(v_ref, nh), col(0), col(3 * nh), gr


def _gdn_fwd(qkv, gates, gt3, nh):
    lp = qkv.shape[0]
    nc = lp // CHUNK
    w = nh * HEAD_DIM

    def body(q_ref, k_ref, v_ref, g_ref, gt_ref, o_ref, sall_ref, tall_ref, s_scr):
        @pl.when(pl.program_id(0) == 0)
        def _():
            s_scr[...] = jnp.zeros_like(s_scr)
        c = _gdn_chunk(*_gdn_chunk_inputs(q_ref, k_ref, v_ref, g_ref[...], gt_ref[0], nh))
        s = s_scr[...]
        sall_ref[0] = s
        tall_ref[0] = c["t"]
        v_new = c["u"] - _bdot3(c["w"], s)
        o = _bdot3(c["q_dec"], s) + _bdot3(c["aqk"], v_new)
        s_scr[...] = s * c["decay"] + _bdot3(c["k_dec"], v_new, "tn")
        for h in range(nh):
            o_ref[:, h * HEAD_DIM:(h + 1) * HEAD_DIM] = o[h]

    return pl.pallas_call(
        body, grid=(nc,),
        in_specs=[pl.BlockSpec((CHUNK, w), lambda n: (n, 0)), pl.BlockSpec((CHUNK, w), lambda n: (n, 1)),
                  pl.BlockSpec((CHUNK, w), lambda n: (n, 2)), pl.BlockSpec((CHUNK, LANES), lambda n: (n, 0)),
                  pl.BlockSpec((1, LANES, CHUNK), lambda n: (n, 0, 0))],
        out_specs=[pl.BlockSpec((CHUNK, w), lambda n: (n, 0)),
                   pl.BlockSpec((1, nh, HEAD_DIM, HEAD_DIM), lambda n: (n, 0, 0, 0)),
                   pl.BlockSpec((1, nh, CHUNK, CHUNK), lambda n: (n, 0, 0, 0))],
        out_shape=[jax.ShapeDtypeStruct((lp, w), F32), jax.ShapeDtypeStruct((nc, nh, HEAD_DIM, HEAD_DIM), F32),
                   jax.ShapeDtypeStruct((nc, nh, CHUNK, CHUNK), F32)],
        scratch_shapes=[pltpu.VMEM((nh, HEAD_DIM, HEAD_DIM), F32)],
        name="gdn_fwd", compiler_params=_cp("arbitrary"))(qkv, qkv, qkv, gates, gt3)


def _gdn_bwd(qkv, gates, gt3, s_all, t_all, do, nh, rider=None):
    lp = qkv.shape[0]
    nc = lp // CHUNK
    w = nh * HEAD_DIM
    rev = lambda n: nc - 1 - n

    def body(start, finish, q_ref, k_ref, v_ref, g_ref, gt_ref, s_ref, t_ref, do_ref, dq_ref, dk_ref, dv_ref, dg_ref, ds_scr):
        @pl.when(pl.program_id(0) == 0)
        def _():
            start()
            ds_scr[...] = jnp.zeros_like(ds_scr)
        q, k, v, beta, gc, gr = _gdn_chunk_inputs(q_ref, k_ref, v_ref, g_ref[...], gt_ref[0], nh)
        c = _gdn_chunk(q, k, v, beta, gc, gr, t_ref[0])
        s = s_ref[0]
        dsn = ds_scr[...]
        dout = _heads(do_ref, nh)
        v_new = c["u"] - _bdot3(c["w"], s)
        dq_dec = _bdot3(dout, s, "nt")
        daqk = jnp.where(c["causal"], _bdot3(dout, v_new, "nt"), 0.0)
        dv_new = _bdot3(c["aqk"], dout, "tn") + _bdot3(c["k_dec"], dsn)
        dk_dec = _bdot3(v_new, dsn, "nt")
        ddecay = jnp.sum(jnp.sum(dsn * s, axis=2, keepdims=True), axis=1, keepdims=True)
        dw = -_bdot3(dv_new, s, "nt")
        ds_scr[...] = _bdot3(c["q_dec"], dout, "tn") + c["decay"] * dsn - _bdot3(c["w"], dv_new, "tn")
        duw = jnp.concatenate([dv_new, dw], axis=2)
        dt = _bdot3(duw, jnp.concatenate([c["bv"], c["bk"]], axis=2), "nt")
        dbvk = _bdot3(c["t"], duw, "tn")
        dbv, dbk = dbvk[:, :, :HEAD_DIM], dbvk[:, :, HEAD_DIM:]
        da = jnp.where(c["strict"], -_bdot3(_bdot3(c["t"], dt, "tn"), c["t"], "nt"), 0.0)
        dkk = da * beta * c["dm"]
        dqk = daqk * c["dm"]
        e = da * c["a"] + daqk * c["aqk"]
        dq = dq_dec * c["eg"] + _bdot3(dqk, k)
        dk = (dk_dec * c["ekd"] + _bdot3(dkk, k) + _bdot3(dkk, k, "tn") + _bdot3(dqk, q, "tn")
              + (beta * c["eg"]) * dbk)
        dv = beta * dbv
        rs = lambda x: jnp.sum(x, axis=2, keepdims=True)
        dbeta = rs(dbv * v) + c["eg"] * rs(dbk * k) + rs(da * c["kk"] * c["dm"])
        kd_term = rs(dk_dec * c["k_dec"])
        eh, ew = _split(e)
        ones = jnp.ones((nh, CHUNK, LANES), BF16)
        col_sums = (_dot3(eh, ones, "tn") + _dot3(ew, ones, "tn"))[:, :, 0:1]
        dg_cum = rs(dq_dec * c["q_dec"]) - kd_term + rs(dbk * c["bk"]) + rs(e) - col_sums
        last = jnp.sum(kd_term, axis=1, keepdims=True) + ddecay * c["decay"]
        dg_cum = dg_cum + jnp.where(_iota((1, CHUNK, 1), 1) == CHUNK - 1, last, 0.0)
        lane = _iota((CHUNK, LANES), 1)
        acc = jnp.zeros((CHUNK, LANES), F32)
        for h in range(nh):
            sl = slice(h * HEAD_DIM, (h + 1) * HEAD_DIM)
            dq_ref[:, sl] = dq[h]
            dk_ref[:, sl] = dk[h]
            dv_ref[:, sl] = dv[h]
            acc = acc + jnp.where(lane == h, dbeta[h], 0.0) + jnp.where(lane == nh + h, dg_cum[h], 0.0)
        triu = (_iota((CHUNK, CHUNK), 0) <= _iota((CHUNK, CHUNK), 1)).astype(F32)
        dg_ref[...] = jnp.where(lane < nh, acc, _hdot(triu, acc))
        pl.when(pl.program_id(0) == nc - 1)(finish)

    outs, got = _hosted_call(
        body, [rider], 8, 4, 1, grid=(nc,),
        in_specs=[pl.BlockSpec((CHUNK, w), lambda n: (rev(n), 0)), pl.BlockSpec((CHUNK, w), lambda n: (rev(n), 1)),
                  pl.BlockSpec((CHUNK, w), lambda n: (rev(n), 2)), pl.BlockSpec((CHUNK, LANES), lambda n: (rev(n), 0)),
                  pl.BlockSpec((1, LANES, CHUNK), lambda n: (rev(n), 0, 0)),
                  pl.BlockSpec((1, nh, HEAD_DIM, HEAD_DIM), lambda n: (rev(n), 0, 0, 0)),
                  pl.BlockSpec((1, nh, CHUNK, CHUNK), lambda n: (rev(n), 0, 0, 0)),
                  pl.BlockSpec((CHUNK, w), lambda n: (rev(n), 0))],
        out_specs=[pl.BlockSpec((CHUNK, w), lambda n: (rev(n), 0))] * 3 + [pl.BlockSpec((CHUNK, LANES), lambda n: (rev(n), 0))],
        out_shape=[jax.ShapeDtypeStruct((lp, w), F32)] * 3 + [jax.ShapeDtypeStruct((lp, LANES), F32)],
        scratch_shapes=[pltpu.VMEM((nh, HEAD_DIM, HEAD_DIM), F32)],
        name="gdn_bwd", compiler_params=_cp("arbitrary"))(qkv, qkv, qkv, gates, gt3, s_all, t_all, do)
    return outs, (got[0] if got else None)


def _merge_gdn(o_gdn, proj, norm_w, nh):
    lp = o_gdn.shape[0]

    def body(o_ref, z_ref, w_ref, m_ref):
        o = o_ref[...]
        z = z_ref[...]
        m_ref[...] = (o * _rms(o) * w_ref[...] * (z * _sigmoid(z))).astype(BF16)

    return pl.pallas_call(
        body, grid=(nh,),
        in_specs=[pl.BlockSpec((lp, LANES), lambda s: (0, s)), pl.BlockSpec((lp, LANES), lambda s: (0, 3 * nh + s)),
                  pl.BlockSpec((1, LANES), lambda s: (0, 0))],
        out_specs=pl.BlockSpec((lp, LANES), lambda s: (0, s)),
        out_shape=jax.ShapeDtypeStruct((lp, 2 * nh * HEAD_DIM), BF16), name="merge_gdn",
        compiler_params=_cp("parallel"))(o_gdn, proj, norm_w)


def _merge_gdn_bwd(o_gdn, proj, norm_w, dmerged, nh):
    lp = o_gdn.shape[0]

    def body(o_ref, z_ref, w_ref, dm_ref, do_ref, dz_ref, dw_ref):
        o = o_ref[...]
        r = _rms(o)
        xh = o * r
        silu, dsilu = _silu_and_grad(z_ref[...])
        dm = dm_ref[...]
        dn = dm * silu
        dz_ref[...] = (dm * (xh * w_ref[...]) * dsilu).astype(BF16)
        dnw = dn * w_ref[...]
        do_ref[...] = r * (dnw - xh * jnp.mean(dnw * xh, axis=-1, keepdims=True))

        @pl.when(pl.program_id(0) == 0)
        def _():
            dw_ref[...] = jnp.zeros_like(dw_ref)
        dw_ref[...] += jnp.sum(dn * xh, axis=0, keepdims=True)

    w = nh * HEAD_DIM
    return pl.pallas_call(
        body, grid=(nh,),
        in_specs=[pl.BlockSpec((lp, LANES), lambda s: (0, s)), pl.BlockSpec((lp, LANES), lambda s: (0, 3 * nh + s)),
                  pl.BlockSpec((1, LANES), lambda s: (0, 0)), pl.BlockSpec((lp, LANES), lambda s: (0, s))],
        out_specs=[pl.BlockSpec((lp, LANES), lambda s: (0, s)), pl.BlockSpec((lp, LANES), lambda s: (0, 3 * nh + s)),
                   pl.BlockSpec((1, LANES), lambda s: (0, 0))],
        out_shape=[jax.ShapeDtypeStruct((lp, w), F32), jax.ShapeDtypeStruct((lp, 8 * w + NARROW), BF16),
                   jax.ShapeDtypeStruct((1, LANES), F32)],
        name="merge_gdn_bwd", compiler_params=_cp("arbitrary"))(o_gdn, proj, norm_w, dmerged)


def _fox_prep(proj, qk_w, nh):
    lp = proj.shape[0]

    def body(x_ref, w_ref, o_ref):
        x = x_ref[...]
        o_ref[...] = x * _rms(x) * w_ref[0]

    return pl.pallas_call(
        body, grid=(2 * nh,),
        in_specs=[pl.BlockSpec((lp, LANES), lambda s: (0, 4 * nh + s)), pl.BlockSpec((1, 1, LANES), lambda s: (s // nh, 0, 0))],
        out_specs=pl.BlockSpec((lp, LANES), lambda s: (0, s)),
        out_shape=jax.ShapeDtypeStruct((lp, 2 * nh * HEAD_DIM), F32), name="fox_prep",
        compiler_params=_cp("parallel"))(proj, qk_w)


def _fox_prep_bwd(proj, qk_w, dq, dk, dproj, nh):
    lp = proj.shape[0]
    part = lambda p: pl.BlockSpec((lp, LANES), lambda s: (0, jnp.clip(s - p * nh, 0, nh - 1)))

    def body(x_ref, w_ref, dq_ref, dk_ref, _, dx_ref, dw_ref):
        x = x_ref[...]
        r = _rms(x)
        xh = x * r
        dy = jnp.where(pl.program_id(0) < nh, dq_ref[...], dk_ref[...])
        dyw = dy * w_ref[0]
        dx_ref[...] = (r * (dyw - xh * jnp.mean(dyw * xh, axis=-1, keepdims=True))).astype(BF16)

        @pl.when(pl.program_id(0) % nh == 0)
        def _():
            dw_ref[...] = jnp.zeros_like(dw_ref)
        dw_ref[0] += jnp.sum(dy * xh, axis=0, keepdims=True)

    strip = pl.BlockSpec((lp, LANES), lambda s: (0, 4 * nh + s))
    wsp = pl.BlockSpec((1, 1, LANES), lambda s: (s // nh, 0, 0))
    return pl.pallas_call(
        body, grid=(2 * nh,), in_specs=[strip, wsp, part(0), part(1), _ANY], out_specs=[strip, wsp],
        out_shape=[jax.ShapeDtypeStruct(dproj.shape, BF16), jax.ShapeDtypeStruct((2, 1, LANES), F32)],
        input_output_aliases={4: 0}, name="fox_prep_bwd", compiler_params=_cp("arbitrary"))(proj, qk_w, dq, dk, dproj)


def _fox_probs(q, k, gates, crow, h, i, nh):
    kl = k.shape[0]
    lane = _iota((Q_BLOCK, LANES), 1)
    ct = jnp.sum(jnp.where(lane == 4 * nh + h, gates, 0.0), axis=1, keepdims=True)
    tq, kq = _iota((Q_BLOCK, Q_BLOCK), 0), _iota((Q_BLOCK, Q_BLOCK), 1)
    if i == 0:
        s = _bdot(q, k, "nt") * (HEAD_DIM ** -0.5) + (ct - crow)
        s = jnp.where((kq <= tq) & ((kq >= PAD_ROWS) | (tq < PAD_ROWS)), s, NEG)
    else:
        crow = jnp.where(_iota((1, kl), 1) < PAD_ROWS, -NEG, crow)
        s = _bdot(q, k, "nt") * (HEAD_DIM ** -0.5) + (ct - crow)
        s = jnp.concatenate([s[:, :kl - Q_BLOCK], jnp.where(kq <= tq, s[:, kl - Q_BLOCK:], NEG)], axis=1)
    p = jnp.exp(s - jnp.max(s, axis=1, keepdims=True))
    return p / jnp.sum(p, axis=1, keepdims=True)


FOX_HEADS_PER_STEP = 2


def _fox_specs(lp, nh):
    hw = FOX_HEADS_PER_STEP * LANES
    return [pl.BlockSpec((Q_BLOCK, hw), lambda g, i: (i, g)),
            pl.BlockSpec((lp, hw), lambda g, i: (0, nh // FOX_HEADS_PER_STEP + g)),
            pl.BlockSpec((lp, hw), lambda g, i: (0, 6 * nh // FOX_HEADS_PER_STEP + g)),
            pl.BlockSpec((Q_BLOCK, LANES), lambda g, i: (i, 0)),
            pl.BlockSpec((LANES, lp), lambda g, i: (0, 0))]


def _fox_fwd(qkn, proj, gates, gtf, nh):
    lp = qkn.shape[0]

    def body(q_ref, k_ref, v_ref, g_ref, gt_ref, o_ref):
        g, i = pl.program_id(0), pl.program_id(1)
        for j in range(lp // Q_BLOCK):
            @pl.when(i == j)
            def _(j=j):
                kl = (j + 1) * Q_BLOCK
                for hh in range(FOX_HEADS_PER_STEP):
                    h = FOX_HEADS_PER_STEP * g + hh
                    sl = slice(hh * LANES, (hh + 1) * LANES)
                    p = _fox_probs(q_ref[:, sl], k_ref[0:kl, sl], g_ref[...], gt_ref[pl.ds(4 * nh + h, 1), :][:, 0:kl],
                                   h, j, nh)
                    o_ref[:, sl] = _bdot(p, v_ref[0:kl, sl])

    return pl.pallas_call(
        body, grid=(nh // FOX_HEADS_PER_STEP, lp // Q_BLOCK), in_specs=_fox_specs(lp, nh),
        out_specs=pl.BlockSpec((Q_BLOCK, FOX_HEADS_PER_STEP * LANES), lambda g, i: (i, g)),
        out_shape=jax.ShapeDtypeStruct((lp, nh * HEAD_DIM), F32), name="fox_fwd",
        compiler_params=_cp("parallel", "parallel"))(qkn, qkn, proj, gates, gtf)


def _fox_bwd(qkn, proj, gates, gtf, do, dproj, nh):
    lp = qkn.shape[0]
    nq = lp // Q_BLOCK
    w = nh * HEAD_DIM
    scale = HEAD_DIM ** -0.5

    def body(q_ref, k_ref, v_ref, g_ref, gt_ref, do_ref, _, dq_ref, dk_ref, dc_ref, dv_ref, dv_scr):
        g, i = pl.program_id(0), pl.program_id(1)

        @pl.when(i == 0)
        def _():
            dk_ref[...] = jnp.zeros_like(dk_ref)
            dv_scr[...] = jnp.zeros_like(dv_scr)
            dc_ref[...] = jnp.zeros_like(dc_ref)
        for j in range(nq):
            @pl.when(i == j)
            def _(j=j):
                kl = (j + 1) * Q_BLOCK
                for hh in range(FOX_HEADS_PER_STEP):
                    h = FOX_HEADS_PER_STEP * g + hh
                    sl = slice(hh * LANES, (hh + 1) * LANES)
                    q, k = q_ref[:, sl], k_ref[0:kl, sl]
                    p = _fox_probs(q, k, g_ref[...], gt_ref[pl.ds(4 * nh + h, 1), :][:, 0:kl], h, j, nh)
                    dout = do_ref[:, sl]
                    dp = _bdot(dout, v_ref[0:kl, sl], "nt")
                    ds = p * (dp - jnp.sum(p * dp, axis=1, keepdims=True))
                    dq_ref[:, sl] = _bdot(ds, k) * scale
                    dk_ref[0:kl, sl] += _bdot(ds, q, "tn") * scale
                    dv_scr[0:kl, sl] += _bdot(p, dout, "tn")
                    dc_ref[hh, :, 0:kl] -= jnp.sum(ds, axis=0, keepdims=True)

        @pl.when(i == nq - 1)
        def _():
            dv_ref[...] = dv_scr[...].astype(BF16)

    hw = FOX_HEADS_PER_STEP * LANES
    blk = pl.BlockSpec((Q_BLOCK, hw), lambda g, i: (i, g))
    col = pl.BlockSpec((lp, hw), lambda g, i: (0, g))
    return pl.pallas_call(
        body, grid=(nh // FOX_HEADS_PER_STEP, nq), in_specs=_fox_specs(lp, nh) + [blk, _ANY],
        out_specs=[blk, col, pl.BlockSpec((FOX_HEADS_PER_STEP, 1, lp), lambda g, i: (g, 0, 0)),
                   pl.BlockSpec((lp, hw), lambda g, i: (0, 6 * nh // FOX_HEADS_PER_STEP + g))],
        out_shape=[jax.ShapeDtypeStruct((lp, w), F32)] * 2 + [jax.ShapeDtypeStruct((nh, 1, lp), F32),
                                                             jax.ShapeDtypeStruct(dproj.shape, BF16)],
        scratch_shapes=[pltpu.VMEM((lp, hw), F32)], input_output_aliases={6: 3},
        name="fox_bwd", compiler_params=_cp("parallel", "arbitrary"))(qkn, qkn, proj, gates, gtf, do, dproj)


def _merge_fox(o_fox, proj, merged, nh):
    lp = o_fox.shape[0]

    def body(o_ref, z_ref, _, m_ref):
        z = z_ref[...]
        m_ref[...] = (o_ref[...] * (z * _sigmoid(z))).astype(BF16)

    return pl.pallas_call(
        body, grid=(nh,),
        in_specs=[pl.BlockSpec((lp, LANES), lambda s: (0, s)), pl.BlockSpec((lp, LANES), lambda s: (0, 7 * nh + s)), _ANY],
        out_specs=pl.BlockSpec((lp, LANES), lambda s: (0, nh + s)),
        out_shape=jax.ShapeDtypeStruct(merged.shape, BF16), input_output_aliases={2: 0}, name="merge_fox",
        compiler_params=_cp("parallel"))(o_fox, proj, merged)


def _merge_fox_bwd(o_fox, proj, dmerged, dproj, nh):
    lp = o_fox.shape[0]

    def body(o_ref, z_ref, dm_ref, _, do_ref, dz_ref):
        silu, dsilu = _silu_and_grad(z_ref[...])
        dm = dm_ref[...]
        do_ref[...] = dm * silu
        dz_ref[...] = (dm * o_ref[...] * dsilu).astype(BF16)

    w = nh * HEAD_DIM
    return pl.pallas_call(
        body, grid=(nh,),
        in_specs=[pl.BlockSpec((lp, LANES), lambda s: (0, s)), pl.BlockSpec((lp, LANES), lambda s: (0, 7 * nh + s)),
                  pl.BlockSpec((lp, LANES), lambda s: (0, nh + s)), _ANY],
        out_specs=[pl.BlockSpec((lp, LANES), lambda s: (0, s)), pl.BlockSpec((lp, LANES), lambda s: (0, 7 * nh + s))],
        out_shape=[jax.ShapeDtypeStruct((lp, w), F32), jax.ShapeDtypeStruct(dproj.shape, BF16)],
        input_output_aliases={3: 1}, name="merge_fox_bwd", compiler_params=_cp("parallel"))(o_fox, proj, dmerged, dproj)


def _post(out, x, target, post_w):
    lp, d = out.shape

    def body(o_ref, x_ref, t_ref, w_ref, dy_ref, do_ref, loss_ref, dw_ref):
        i = pl.program_id(0)

        @pl.when(i == 0)
        def _():
            loss_ref[...] = jnp.zeros_like(loss_ref)
            dw_ref[...] = jnp.zeros_like(dw_ref)
        o = o_ref[...]
        r = _rms(o)
        nrm = o * r
        err = jnp.where(i > 0, x_ref[...] + nrm * w_ref[...] - t_ref[...], 0.0)
        loss_ref[0:1, :] += 0.5 * jnp.sum(jnp.sum(err * err, axis=1, keepdims=True), axis=0, keepdims=True) / d
        dy = err / d
        dy_ref[...] = dy
        dw_ref[...] += jnp.sum(dy * nrm, axis=0, keepdims=True)
        dyw = dy * w_ref[...]
        do_ref[...] = (r * (dyw - nrm * jnp.mean(dyw * nrm, axis=-1, keepdims=True))).astype(BF16)

    row = pl.BlockSpec((Q_BLOCK, d), lambda i: (i, 0))
    vec = pl.BlockSpec((1, d), lambda i: (0, 0))
    return pl.pallas_call(
        body, grid=(lp // Q_BLOCK,), in_specs=[row, _x_rows(d), _x_rows(d), vec],
        out_specs=[_x_rows(d), row, pl.BlockSpec((8, LANES), lambda i: (0, 0)), vec],
        out_shape=[jax.ShapeDtypeStruct(x.shape, F32), jax.ShapeDtypeStruct((lp, d), BF16),
                   jax.ShapeDtypeStruct((8, LANES), F32), jax.ShapeDtypeStruct((1, d), F32)],
        name="post", compiler_params=_cp("arbitrary"))(out, x, target, post_w)


def _prenorm_bwd(dxn, x, meta, w, dy, rider=None):
    seq, d = x.shape
    lp = seq + Q_BLOCK

    def body(start, finish, dx_ref, x_ref, m_ref, w_ref, dy_ref, gx_ref, gm_ref, dw_ref):
        i = pl.program_id(0)
        pl.when(i == 0)(start)
        h = _h_tile(i, x_ref, m_ref)
        r = _rms(h)
        xh = h * r
        dxn_ = dx_ref[...]
        dxw = dxn_ * w_ref[...]
        dh = jnp.where(i > 0, dy_ref[...], 0.0) + r * (dxw - xh * jnp.mean(dxw * xh, axis=-1, keepdims=True))
        gx_ref[...] = dh

        @pl.when(i == 0)
        def _():
            dw_ref[...] = jnp.zeros_like(dw_ref)
            gm_ref[...] = dh[PAD_ROWS:, :]
        dw_ref[...] += jnp.sum(dxn_ * xh, axis=0, keepdims=True)
        pl.when(i == lp // Q_BLOCK - 1)(finish)

    vec = pl.BlockSpec((1, d), lambda i: (0, 0))
    met = pl.BlockSpec((N_META, d), lambda i: (0, 0))
    outs, got = _hosted_call(
        body, [rider], 5, 3, 0, grid=(lp // Q_BLOCK,),
        in_specs=[pl.BlockSpec((Q_BLOCK, d), lambda i: (i, 0)), _x_rows(d), met, vec, _x_rows(d)],
        out_specs=[_x_rows(d), met, vec],
        out_shape=[jax.ShapeDtypeStruct((seq, d), F32), jax.ShapeDtypeStruct((N_META, d), F32),
                   jax.ShapeDtypeStruct((1, d), F32)],
        name="prenorm_bwd", compiler_params=_cp("arbitrary"))(dxn, x, meta, w, dy)
    return outs, (got[0] if got else None)


def _layer_grads(x, target, meta, pre_w, wfull, conv_wt, a_log, dt_bias, gdn_norm_w, fq_w, fk_w, f_bias, w_out, post_w,
                 late_weights=None, w_out_grads=None):
    nh = a_log.shape[1]
    zpad = jnp.zeros((1, LANES - 3 * nh), F32)
    bias_row = jnp.concatenate([jnp.zeros((1, nh), F32), dt_bias, f_bias, zpad], axis=1)
    nega_row = jnp.concatenate([jnp.zeros((1, nh), F32), -jnp.exp(a_log), jnp.zeros((1, nh), F32), zpad], axis=1)
    qk_w = jnp.stack([fq_w, fk_w])

    xn = _prenorm(x, meta, pre_w)
    if late_weights is None:
        proj = _matmul(xn, wfull, "nn", MM_TILE, F32, "proj")
    else:
        proj, got = _matmul(xn, wfull, "nn", MM_TILE, F32, "proj", late_weights[0])
        conv_wt, w_out = late_weights[1](got)
    qkv = _gdn_prep(proj, conv_wt, nh)
    gates, gt3, gtf = _gates(proj, bias_row, nega_row, nh)
    o_gdn, s_all, t_all = _gdn_fwd(qkv, gates, gt3, nh)
    qkn = _fox_prep(proj, qk_w, nh)
    o_fox = _fox_fwd(qkn, proj, gates, gtf, nh)
    merged = _merge_fox(o_fox, proj, _merge_gdn(o_gdn, proj, gdn_norm_w, nh), nh)
    out = _matmul(merged, w_out, "nn", 4 * LANES, F32, "out_proj")
    dy, dout, loss_blk, dpost_w = _post(out, x, target, post_w)

    dw_out = _matmul(merged, dout, "tn", 4 * LANES, BF16, "dw_out")
    if w_out_grads is None:
        dmerged, gdn_rider = _matmul(dout, w_out, "nt", 4 * LANES, F32, "dmerged"), None
    else:
        dmerged, got = _matmul(dout, w_out, "nt", 4 * LANES, F32, "dmerged", w_out_grads[0](dw_out))
        gdn_rider = w_out_grads[1](dw_out, got)
    do_gdn, dproj, dgdn_norm_w = _merge_gdn_bwd(o_gdn, proj, gdn_norm_w, dmerged, nh)
    do_fox, dproj = _merge_fox_bwd(o_fox, proj, dmerged, dproj, nh)
    dqn, dkn, dc_t, dproj = _fox_bwd(qkn, proj, gates, gtf, do_fox, dproj, nh)
    dproj, dqk_w = _fox_prep_bwd(proj, qk_w, dqn, dkn, dproj, nh)
    (dgq, dgk, dgv, dgate), w_out_parts = _gdn_bwd(qkv, gates, gt3, s_all, t_all, do_gdn, nh, gdn_rider)
    dproj, dconv_wt = _gdn_prep_bwd(proj, conv_wt, dgq, dgk, dgv, dproj, nh)
    dc_rows = jnp.pad(dc_t.reshape(nh, -1), ((2 * nh, LANES - 3 * nh), (0, 0)))
    dproj, gate_sums = _gates_bwd(proj, bias_row, nega_row, gates, dgate, dc_rows, dproj, nh)
    return dict(
        loss=loss_blk[0:1, 0:1], dy=dy, xn=xn, dproj=dproj, post_w=dpost_w,
        conv_wt=dconv_wt, a_log=gate_sums[1:2, nh:2 * nh], dt_bias=gate_sums[0:1, nh:2 * nh],
        gdn_norm_w=dgdn_norm_w, fq_w=dqk_w[0], fk_w=dqk_w[1], f_bias=gate_sums[0:1, 2 * nh:3 * nh], w_out=dw_out,
        w_out_parts=w_out_parts)


def _cast_bf16(a, tr, name):
    r, c = a.shape

    def body(a_ref, o_ref):
        o_ref[...] = a_ref[...].astype(BF16)

    return pl.pallas_call(
        body, grid=(r // tr,), in_specs=[pl.BlockSpec((tr, c), lambda i: (i, 0))],
        out_specs=pl.BlockSpec((tr, c), lambda i: (i, 0)), out_shape=jax.ShapeDtypeStruct((r, c), BF16),
        name=name, compiler_params=_cp("parallel"))(a)


def _column_major(a):
    return jnp.transpose(a, (2, 0, 1))


def _cast_bf16_column_major(a3, name):
    _, r, c = a3.shape

    def body(a_ref, o_ref):
        o_ref[...] = a_ref[...].reshape(LANES, r).T.astype(BF16)

    return pl.pallas_call(
        body, grid=(pl.cdiv(c, LANES),), in_specs=[pl.BlockSpec((LANES, 1, r), lambda i: (i, 0, 0))],
        out_specs=pl.BlockSpec((r, LANES), lambda i: (0, i)), out_shape=jax.ShapeDtypeStruct((r, c), BF16),
        name=name, compiler_params=_cp("parallel"))(_column_major(a3))


def _adamw_column_major(w3, parts, m3, v3, name):
    _, r, c = w3.shape
    n_parts = parts.shape[0]

    def body(w_ref, p_ref, m_ref, v_ref, g_ref, d_ref, nm_ref, nv_ref):
        g = p_ref[0].astype(F32)
        for s in range(1, n_parts):
            g = g + p_ref[s].astype(F32)
        g = g.T
        flat = lambda ref: ref[...].reshape(LANES, r)
        m_new = ADAM_B1 * flat(m_ref) + (1.0 - ADAM_B1) * g
        v_new = ADAM_B2 * flat(v_ref) + (1.0 - ADAM_B2) * (g * g)
        m_hat = m_new / (1.0 - ADAM_B1 ** ADAM_STEP)
        v_hat = v_new / (1.0 - ADAM_B2 ** ADAM_STEP)
        delta = -ADAM_LR * (m_hat / (jnp.sqrt(v_hat) + ADAM_EPS) + ADAM_WD * flat(w_ref))
        for ref, val in ((g_ref, g), (d_ref, delta), (nm_ref, m_new), (nv_ref, v_new)):
            ref[...] = val.reshape(LANES, 1, r)

    blk = pl.BlockSpec((LANES, 1, r), lambda i: (i, 0, 0))
    outs = pl.pallas_call(
        body, grid=(pl.cdiv(c, LANES),), in_specs=[blk, pl.BlockSpec((n_parts, r, LANES), lambda i: (0, 0, i)), blk, blk],
        out_specs=[blk] * 4, out_shape=[jax.ShapeDtypeStruct((c, 1, r), F32)] * 4, name=name,
        compiler_params=_cp("parallel"))(_column_major(w3), parts, _column_major(m3), _column_major(v3))
    return [jnp.transpose(o, (1, 2, 0)) for o in outs]


def _gather_copies(ins, outs, send_sems, recv_sems, local_sems):
    n = len(ins)
    x, y, c = lax.axis_index("x"), lax.axis_index("y"), lax.axis_index("c")
    me, sibling = (x, y, c), (x, y, 1 - c)
    xn, yn, dg = (1 - x, y), (x, 1 - y), (1 - x, 1 - y)

    def copy(a, k, block, to, src=None):
        px, py, pc = block
        rows = outs[a].at[4 * px + 2 * py + pc]
        return pltpu.make_async_remote_copy(
            src_ref=rows if src is None else src, dst_ref=rows, send_sem=send_sems.at[a, k],
            recv_sem=recv_sems.at[a, k], device_id=to, device_id_type=_MESH)

    local = [pltpu.make_async_copy(ins[a], outs[a].at[4 * x + 2 * y + c], local_sems.at[a]) for a in range(n)]
    own = [cp for a in range(n) for cp in (copy(a, 0, me, sibling, src=ins[a]), copy(a, 1, me, (*xn, c), src=ins[a]),
                                           copy(a, 2, me, (*yn, c), src=ins[a]))]

    def start():
        for cp in local + own:
            cp.start()

    def finish():
        for a in range(n):
            @pl.when(c == 1)
            def _(a=a):
                copy(a, 1, (*xn, c), me).wait_recv()
                copy(a, 3, (*xn, c), (*yn, c)).start()

            @pl.when(c == 0)
            def _(a=a):
                copy(a, 2, (*yn, c), me).wait_recv()
                copy(a, 3, (*yn, c), (*xn, c)).start()
        for a in range(n):
            pl.when(c == 0)(copy(a, 1, (*xn, c), me).wait_recv)
            copy(a, 4, (*xn, c), sibling).start()
            pl.when(c == 1)(copy(a, 2, (*yn, c), me).wait_recv)
            copy(a, 5, (*yn, c), sibling).start()
        for a in range(n):
            copy(a, 3, (*dg, c), me).wait_recv()
            copy(a, 6, (*dg, c), sibling).start()
        for a in range(n):
            copy(a, 0, sibling, me).wait_recv()
            for k, chip in ((4, xn), (5, yn), (6, dg)):
                copy(a, k, (*chip, 1 - c), me).wait_recv()
                copy(a, k, (*chip, c), sibling).wait_send()
            copy(a, 3, (*xn, c), (*yn, c)).wait_send()
        for cp in own:
            cp.wait_send()
        for cp in local:
            cp.wait()

    return start, finish


def _gather_scratch(n):
    return [pltpu.SemaphoreType.DMA((n, N_DEV - 1)), pltpu.SemaphoreType.DMA((n, N_DEV - 1)), pltpu.SemaphoreType.DMA((n,))]


def _gather_rider(arrays):
    return _Rider(list(arrays), [jax.ShapeDtypeStruct((N_DEV,) + a.shape, a.dtype) for a in arrays],
                  _gather_scratch(len(arrays)), {}, lambda ins, outs, scratch: _gather_copies(ins, outs, *scratch))


def _all_gather(arrays, name):
    n = len(arrays)

    def body(*refs):
        start, finish = _gather_copies(refs[:n], refs[n:2 * n], *refs[2 * n:])
        start()
        finish()

    return pl.pallas_call(
        body, in_specs=[_ANY] * n, out_specs=[_ANY] * n,
        out_shape=[jax.ShapeDtypeStruct((N_DEV,) + a.shape, a.dtype) for a in arrays],
        scratch_shapes=_gather_scratch(n), name=name)(*arrays)


SLAB = 10 * LANES


def _slab_start(blk, nh, cols):
    in_second_half = blk >= N_DEV // 2
    shift = (2 * nh if in_second_half else 0) if isinstance(blk, int) else jnp.where(in_second_half, 2 * nh, 0)
    return (blk * cols - shift) // LANES * LANES


def _pair_rider(dw_rows=None, parts=None, nh=None):
    if dw_rows is not None:
        r, full = dw_rows.shape
        cols = (full - NARROW + 3 * nh) // N_DEV
        out_shapes = [jax.ShapeDtypeStruct((N_CHIP, r, SLAB), dw_rows.dtype), jax.ShapeDtypeStruct((r, NARROW), dw_rows.dtype)]
    else:
        out_shapes = [jax.ShapeDtypeStruct((N_CHIP,) + parts.shape[1:], parts.dtype)]

    def make(ins, outs, scratch):
        send_sems, recv_sems = scratch
        x, y, c = lax.axis_index("x"), lax.axis_index("y"), lax.axis_index("c")
        kw = lambda k: dict(send_sem=send_sems.at[k], recv_sem=recv_sems.at[k], device_id=(x, y, 1 - c), device_id_type=_MESH)
        copies = []
        for q in range(N_CHIP):
            if dw_rows is not None:
                first = pl.multiple_of(_slab_start(2 * q + 1 - c, nh, cols), LANES)
                copies.append(pltpu.make_async_remote_copy(src_ref=ins[0].at[:, pl.ds(first, SLAB)], dst_ref=outs[0].at[q], **kw(q)))
            else:
                copies.append(pltpu.make_async_remote_copy(src_ref=ins[0].at[2 * q + 1 - c], dst_ref=outs[0].at[q], **kw(q)))
        if dw_rows is not None:
            copies.append(pltpu.make_async_remote_copy(src_ref=ins[0].at[:, pl.ds(full - NARROW, NARROW)], dst_ref=outs[1],
                                                       **kw(N_CHIP)))

        def start():
            for cp in copies:
                cp.start()

        def finish():
            for cp in copies:
                cp.wait()

        return start, finish

    return _Rider([dw_rows if dw_rows is not None else parts], out_shapes,
                  [pltpu.SemaphoreType.DMA((N_CHIP + 1,)), pltpu.SemaphoreType.DMA((N_CHIP + 1,))], {}, make)


def _relayout_pair_sum(dwfull, got_slabs, got_tail, core, nh, tr, name):
    d, full = dwfull.shape
    w = nh * HEAD_DIM
    cols = (8 * w + 3 * nh) // N_DEV
    segs = _native_segments(nh)

    def block(f_ref, s_ref, t_ref, q, blk):
        st = _slab_start(blk, nh, cols)
        wide = f_ref[:, st:st + SLAB].astype(F32) + s_ref[q].astype(F32)
        tail = f_ref[:, 8 * w:].astype(F32) + t_ref[...].astype(F32)
        pieces = []
        for s0, s1, t0 in segs:
            lo, hi = max(s0, blk * cols), min(s1, (blk + 1) * cols)
            if lo < hi:
                at = t0 + lo - s0
                pieces.append(tail[:, at - 8 * w:at - 8 * w + hi - lo] if at >= 8 * w else wide[:, at - st:at - st + hi - lo])
        return (pieces[0] if len(pieces) == 1 else jnp.concatenate(pieces, axis=1)).astype(dwfull.dtype)

    def body(core_ref, f_ref, s_ref, t_ref, o_ref):
        for parity in range(2):
            @pl.when(core_ref[0] == parity)
            def _(parity=parity):
                for q in range(N_CHIP):
                    o_ref[q] = block(f_ref, s_ref, t_ref, q, 2 * q + parity)

    return pl.pallas_call(
        body,
        grid_spec=pltpu.PrefetchScalarGridSpec(
            num_scalar_prefetch=1, grid=(d // tr,),
            in_specs=[pl.BlockSpec((tr, full), lambda i, c_ref: (i, 0)), pl.BlockSpec((N_CHIP, tr, SLAB), lambda i, c_ref: (0, i, 0)),
                      pl.BlockSpec((tr, NARROW), lambda i, c_ref: (i, 0))],
            out_specs=pl.BlockSpec((N_CHIP, tr, cols), lambda i, c_ref: (0, i, 0))),
        out_shape=jax.ShapeDtypeStruct((N_CHIP, d, cols), dwfull.dtype), name=name,
        compiler_params=_cp("parallel"))(core, dwfull, got_slabs, got_tail)


def _pair_sum(parts, got, core, tr, name):
    _, r, c = parts.shape

    def body(core_ref, p_ref, g_ref, o_ref):
        o_ref[...] = (p_ref[...].astype(F32) + g_ref[...].astype(F32)).astype(o_ref.dtype)

    return pl.pallas_call(
        body,
        grid_spec=pltpu.PrefetchScalarGridSpec(
            num_scalar_prefetch=1, grid=(N_CHIP, r // tr),
            in_specs=[pl.BlockSpec((1, tr, c), lambda q, i, core_ref: (2 * q + core_ref[0], i, 0)),
                      pl.BlockSpec((1, tr, c), lambda q, i, core_ref: (q, i, 0))],
            out_specs=pl.BlockSpec((1, tr, c), lambda q, i, core_ref: (q, i, 0))),
        out_shape=jax.ShapeDtypeStruct((N_CHIP, r, c), parts.dtype), name=name,
        compiler_params=_cp("parallel", "parallel"))(core, parts, got)


def _native_segments(nh):
    w = nh * HEAD_DIM
    return [(0, 4 * w, 0), (4 * w, 4 * w + 2 * nh, 8 * w), (4 * w + 2 * nh, 8 * w + 2 * nh, 4 * w),
            (8 * w + 2 * nh, 8 * w + 3 * nh, 8 * w + 2 * nh)]


def _relayout_w_in(wg, nh, tr):
    _, d, cols = wg.shape
    w = nh * HEAD_DIM

    def native(ref, j0, j1):
        out = []
        while j0 < j1:
            blk = j0 // cols
            end = min(j1, (blk + 1) * cols)
            out.append(ref[blk, :, pl.ds(j0 - blk * cols, end - j0)])
            j0 = end
        return out

    def body(g_ref, o_ref):
        for cidx in range(8 * w // LANES):
            j0 = cidx * LANES + (0 if cidx * LANES < 4 * w else 2 * nh)
            pieces = native(g_ref, j0, j0 + LANES)
            o_ref[:, cidx * LANES:(cidx + 1) * LANES] = pieces[0] if len(pieces) == 1 else jnp.concatenate(pieces, axis=1)
        pieces = (native(g_ref, 4 * w, 4 * w + 2 * nh) + native(g_ref, 8 * w + 2 * nh, 8 * w + 3 * nh)
                  + [jnp.zeros((tr, NARROW - 3 * nh), wg.dtype)])
        o_ref[:, 8 * w:] = jnp.concatenate(pieces, axis=1)

    return pl.pallas_call(
        body, grid=(d // tr,), in_specs=[pl.BlockSpec((N_DEV, tr, cols), lambda i: (0, i, 0))],
        out_specs=pl.BlockSpec((tr, 8 * w + NARROW), lambda i: (i, 0)),
        out_shape=jax.ShapeDtypeStruct((d, 8 * w + NARROW), wg.dtype),
        name="relayout_w_in", compiler_params=_cp("parallel"))(wg)


def _adamw(w, parts, m, v, tr, name):
    r, c = w.shape
    n_parts = parts.shape[0]

    def body(w_ref, p_ref, m_ref, v_ref, g_ref, d_ref, nm_ref, nv_ref):
        g = p_ref[0].astype(F32)
        for s in range(1, n_parts):
            g = g + p_ref[s].astype(F32)
        m_new = ADAM_B1 * m_ref[...] + (1.0 - ADAM_B1) * g
        v_new = ADAM_B2 * v_ref[...] + (1.0 - ADAM_B2) * (g * g)
        m_hat = m_new / (1.0 - ADAM_B1 ** ADAM_STEP)
        v_hat = v_new / (1.0 - ADAM_B2 ** ADAM_STEP)
        g_ref[...] = g
        d_ref[...] = -ADAM_LR * (m_hat / (jnp.sqrt(v_hat) + ADAM_EPS) + ADAM_WD * w_ref[...])
        nm_ref[...] = m_new
        nv_ref[...] = v_new

    blk = pl.BlockSpec((tr, c), lambda i: (i, 0))
    return pl.pallas_call(
        body, grid=(r // tr,), in_specs=[blk, pl.BlockSpec((n_parts, tr, c), lambda i: (0, i, 0)), blk, blk],
        out_specs=[blk] * 4, out_shape=[jax.ShapeDtypeStruct((r, c), F32)] * 4, name=name,
        compiler_params=_cp("parallel"))(w, parts, m, v)


def _pack_small(d, pre, post, a_log, dt_bias, f_bias, gdn_w, fq_w, fk_w, extra):
    row2 = jnp.concatenate([a_log, dt_bias, f_bias, gdn_w, fq_w, fk_w, extra], axis=1)
    row2 = jnp.pad(row2, ((0, 0), (0, d - row2.shape[1])))
    return jnp.concatenate([pre, post, row2, jnp.zeros((5, d), F32)], axis=0)


def _unpack_small(p, nh):
    o = 3 * nh
    return dict(pre=p[0:1], post=p[1:2], a_log=p[2:3, 0:nh], dt_bias=p[2:3, nh:2 * nh], f_bias=p[2:3, 2 * nh:o],
                gdn_w=p[2:3, o:o + HEAD_DIM], fq_w=p[2:3, o + HEAD_DIM:o + 2 * HEAD_DIM],
                fk_w=p[2:3, o + 2 * HEAD_DIM:o + 3 * HEAD_DIM], extra=p[2, o + 3 * HEAD_DIM])


def kernel(x, meta_tokens, pre_norm_w, w_in, conv_w, a_log, dt_bias, gdn_norm_w, fox_q_norm_w, fox_k_norm_w, fox_f_bias, w_out, post_norm_w, loss_target, m_meta_tokens, m_pre_norm_w, m_w_in, m_conv_w, m_a_log, m_dt_bias, m_gdn_norm_w, m_fox_q_norm_w, m_fox_k_norm_w, m_fox_f_bias, m_w_out, m_post_norm_w, v_meta_tokens, v_pre_norm_w, v_w_in, v_conv_w, v_a_log, v_dt_bias, v_gdn_norm_w, v_fox_q_norm_w, v_fox_k_norm_w, v_fox_f_bias, v_w_out, v_post_norm_w):
    nh = a_log.shape[1]
    d = x.shape[-1]
    w = nh * HEAD_DIM
    zero = jnp.zeros((1, 1), F32)

    wg, mg = _all_gather([_cast_bf16_column_major(w_in, "cast_w_in"), meta_tokens], "gather_weights")
    wfull = _relayout_w_in(wg, nh, 256)
    meta_full = mg.transpose(1, 0, 2).reshape(N_META, d)
    late_weights = (_gather_rider([conv_w[0].T, _cast_bf16(w_out[0], 256, "cast_w_out")]),
                    lambda got: (got[0].transpose(1, 0, 2).reshape(CONV_WIDTH, 3 * w), got[1].reshape(2 * w, d)))
    core = lax.axis_index("c")
    dev = 4 * lax.axis_index("x") + 2 * lax.axis_index("y") + core
    core_arr = jnp.reshape(core, (1,)).astype(jnp.int32)

    out_parts = lambda dw_out: dw_out.reshape(N_DEV, 2 * w // N_DEV, d)
    g = _layer_grads(
        x[0], loss_target[0], meta_full, pre_norm_w, wfull, None, a_log, dt_bias, gdn_norm_w,
        fox_q_norm_w, fox_k_norm_w, fox_f_bias, None, post_norm_w, late_weights=late_weights,
        w_out_grads=(lambda dw_out: _pair_rider(parts=out_parts(dw_out)),
                     lambda dw_out, got: _chip_rider(_pair_sum(out_parts(dw_out), got[0], core_arr, 256, "pair_sum_w_out"))))
    p_out = g["w_out_parts"][0]
    xn, dproj, half = g["xn"], g["dproj"], d // 2
    dw_a = _matmul(xn[:, :half], dproj, "tn", MM_TILE, BF16, "dw_in_a")
    dw_b, got_a = _matmul(xn[:, half:], dproj, "tn", MM_TILE, BF16, "dw_in_b", _pair_rider(dw_rows=dw_a, nh=nh))
    sums_a = _relayout_pair_sum(dw_a, got_a[0], got_a[1], core_arr, nh, 128, "relayout_pair_sum_a")
    dxn, (p_in_a, got_b) = _dxn(dproj, wfull, [_chip_rider(sums_a, rows_total=d), _pair_rider(dw_rows=dw_b, nh=nh)],
                                MM_TILE, "dxn")
    sums_b = _relayout_pair_sum(dw_b, got_b[0], got_b[1], core_arr, nh, 128, "relayout_pair_sum_b")
    (grad_x, dmeta, dpre_w), p_in = _prenorm_bwd(dxn, x[0], meta_full, pre_norm_w, g["dy"],
                                                 _chip_rider(sums_b, rows_total=d, row0=half, into=p_in_a[0]))
    p_in = p_in[0]
    small = _pack_small(d, dpre_w, g["post_w"], g["a_log"], g["dt_bias"], g["f_bias"], g["gdn_norm_w"], g["fq_w"],
                        g["fk_w"], g["loss"])
    a_conv, a_meta, p_small = _all_gather([g["conv_wt"], dmeta, small], "gather_small_grads")
    p_conv = lax.dynamic_slice_in_dim(a_conv, dev * conv_w.shape[1], conv_w.shape[1], axis=2).transpose(0, 2, 1)
    p_meta = lax.dynamic_slice_in_dim(a_meta, dev * meta_tokens.shape[1], meta_tokens.shape[1], axis=2)

    r_in = _adamw_column_major(w_in, p_in, m_w_in, v_w_in, "adamw_w_in")
    r_out = _adamw(w_out[0], p_out, m_w_out[0], v_w_out[0], 64, "adamw_w_out")
    r_conv = _adamw(conv_w[0], p_conv, m_conv_w[0], v_conv_w[0], conv_w.shape[1], "adamw_conv_w")
    r_meta = _adamw(meta_tokens, p_meta, m_meta_tokens, v_meta_tokens, N_META, "adamw_meta")
    pk = lambda pre, post, a, dt, gw, fq, fk, fb: _pack_small(d, pre, post, a, dt, fb, gw, fq, fk, zero)
    r_small = _adamw(
        pk(pre_norm_w, post_norm_w, a_log, dt_bias, gdn_norm_w, fox_q_norm_w, fox_k_norm_w, fox_f_bias), p_small,
        pk(m_pre_norm_w, m_post_norm_w, m_a_log, m_dt_bias, m_gdn_norm_w, m_fox_q_norm_w, m_fox_k_norm_w, m_fox_f_bias),
        pk(v_pre_norm_w, v_post_norm_w, v_a_log, v_dt_bias, v_gdn_norm_w, v_fox_q_norm_w, v_fox_k_norm_w, v_fox_f_bias),
        8, "adamw_small")

    sm = [_unpack_small(r, nh) for r in r_small]
    outs = []
    for i in range(4):
        s = sm[i]
        outs += [r_meta[i], s["pre"], r_in[i], r_conv[i][None], s["a_log"], s["dt_bias"], s["gdn_w"], s["fq_w"],
                 s["fk_w"], s["f_bias"], r_out[i][None], s["post"]]
    return (sm[0]["extra"], grad_x[None], *outs)
```

```python
import jax
import jax.numpy as jnp
from jax import lax
from jax.experimental import pallas as pl
from jax.experimental.pallas import tpu as pltpu

F32, BF16 = jnp.float32, jnp.bfloat16
HEAD_DIM = 128
N_META = 16
CONV_WIDTH = 4
CHUNK = 64
Q_BLOCK = 128
LANES = 128
EPS = 1e-6
PAD_ROWS = Q_BLOCK - N_META
N_DEV = 8
N_CHIP = 4
VMEM_LIMIT = 56 * 1024 * 1024
NEG = -1e30
NARROW = 2 * LANES
MM_TILE = 6 * LANES

ADAM_LR, ADAM_B1, ADAM_B2, ADAM_EPS, ADAM_WD, ADAM_STEP = 0.001, 0.9, 0.999, 1e-08, 0.01, 10

_DN = {"nn": (((1,), (0,)), ((), ())), "nt": (((1,), (1,)), ((), ())), "tn": (((0,), (0,)), ((), ()))}
_DN3 = {"nn": (((2,), (1,)), ((0,), (0,))), "nt": (((2,), (2,)), ((0,), (0,))), "tn": (((1,), (1,)), ((0,), (0,)))}
_ANY = pl.BlockSpec(memory_space=pl.ANY)
_MESH = pl.DeviceIdType.MESH


def _cp(*sem):
    return pltpu.CompilerParams(dimension_semantics=sem, vmem_limit_bytes=VMEM_LIMIT)


def _dot(a, b, dims="nn", prec=None):
    return lax.dot_general(a, b, _DN[dims], precision=prec, preferred_element_type=F32)


def _bdot(a, b, dims="nn"):
    return _dot(a.astype(BF16), b.astype(BF16), dims)


def _hdot(a, b, dims="nn"):
    return _dot(a, b, dims, prec=lax.Precision.HIGHEST)


def _dot3(a, b, dims="nn"):
    return lax.dot_general(a, b, _DN3[dims], preferred_element_type=F32)


def _bdot3(a, b, dims="nn"):
    return _dot3(a.astype(BF16), b.astype(BF16), dims)


def _split(a):
    hi = a.astype(BF16)
    return hi, (a - hi.astype(F32)).astype(BF16)


def _iota(shape, dim):
    return lax.broadcasted_iota(jnp.int32, shape, dim)


def _sigmoid(z):
    return 1.0 / (1.0 + jnp.exp(-z))


def _softplus(z):
    e = jnp.exp(-jnp.abs(z))
    u = 1.0 + e
    l1p = jnp.where(u == 1.0, e, jnp.log(u) * (e / jnp.where(u == 1.0, 1.0, u - 1.0)))
    return jnp.maximum(z, 0.0) + l1p


def _silu_and_grad(z):
    s = _sigmoid(z)
    return z * s, s * (1.0 + z * (1.0 - s))


def _rms(x):
    return lax.rsqrt(jnp.mean(x * x, axis=-1, keepdims=True) + EPS)


def _h_tile(i, x_ref, meta_ref):
    first = jnp.concatenate([jnp.zeros((PAD_ROWS, x_ref.shape[1]), F32), meta_ref[...]], axis=0)
    return jnp.where(i == 0, first, x_ref[...])


def _x_rows(d):
    return pl.BlockSpec((Q_BLOCK, d), lambda i: (jnp.maximum(i - 1, 0), 0))


def _prenorm(x, meta, w):
    seq, d = x.shape
    lp = seq + Q_BLOCK

    def body(x_ref, m_ref, w_ref, o_ref):
        h = _h_tile(pl.program_id(0), x_ref, m_ref)
        o_ref[...] = (h * _rms(h) * w_ref[...]).astype(BF16)

    return pl.pallas_call(
        body, grid=(lp // Q_BLOCK,),
        in_specs=[_x_rows(d), pl.BlockSpec((N_META, d), lambda i: (0, 0)), pl.BlockSpec((1, d), lambda i: (0, 0))],
        out_specs=pl.BlockSpec((Q_BLOCK, d), lambda i: (i, 0)),
        out_shape=jax.ShapeDtypeStruct((lp, d), BF16), name="prenorm", compiler_params=_cp("parallel"))(x, meta, w)


def _tile(n, want):
    return max(t for t in range(LANES, want + 1, LANES) if n % t == 0)


class _Rider:
    def __init__(self, inputs, out_shapes, scratch, aliases, make):
        self.inputs, self.out_shapes, self.scratch, self.aliases, self.make = inputs, out_shapes, scratch, aliases, make


def _hosted_call(body, riders, n_in, n_out, n_scratch, *, in_specs, out_specs, out_shape, scratch_shapes=(), aliases=None,
                 **kw):
    riders = [r for r in riders if r is not None]
    r_in = [len(r.inputs) for r in riders]
    r_out = [len(r.out_shapes) for r in riders]
    r_scr = [len(r.scratch) for r in riders]
    al = dict(aliases or {})
    for k, r in enumerate(riders):
        al.update({n_in + sum(r_in[:k]) + i: n_out + sum(r_out[:k]) + o for i, o in r.aliases.items()})

    def full_body(*refs):
        ins, rest = refs[:n_in + sum(r_in)], refs[n_in + sum(r_in):]
        outs, scr = rest[:n_out + sum(r_out)], rest[n_out + sum(r_out):]
        hooks = [r.make(ins[n_in + sum(r_in[:k]):n_in + sum(r_in[:k + 1])], outs[n_out + sum(r_out[:k]):n_out + sum(r_out[:k + 1])],
                        scr[n_scratch + sum(r_scr[:k]):n_scratch + sum(r_scr[:k + 1])]) for k, r in enumerate(riders)]

        def start():
            for h in hooks:
                h[0]()

        def finish():
            for h in hooks:
                h[1]()

        body(start, finish, *ins[:n_in], *outs[:n_out], *scr[:n_scratch])

    call = pl.pallas_call(
        full_body, in_specs=list(in_specs) + [_ANY] * sum(r_in), out_specs=list(out_specs) + [_ANY] * sum(r_out),
        out_shape=list(out_shape) + [s for r in riders for s in r.out_shapes],
        scratch_shapes=list(scratch_shapes) + [s for r in riders for s in r.scratch], input_output_aliases=al, **kw)

    def run(*args):
        res = call(*args, *[t for r in riders for t in r.inputs])
        return res[:n_out], [res[n_out + sum(r_out[:k]):n_out + sum(r_out[:k + 1])] for k in range(len(riders))]

    return run


def _matmul(a, b, dims, tn, out_dtype, name, rider=None, a_cols=None):
    a_shape = a.shape if a_cols is None else (a.shape[0], a_cols[0])
    a_index = 0 if a_cols is None else a_cols[1]
    m = a_shape[1] if dims == "tn" else a_shape[0]
    n = b.shape[0] if dims == "nt" else b.shape[1]
    kdim = b.shape[1] if dims == "nt" else b.shape[0]
    tn = _tile(n, tn)
    steps = n // tn
    b_spec = pl.BlockSpec((tn, kdim), lambda j: (j, 0)) if dims == "nt" else pl.BlockSpec((kdim, tn), lambda j: (0, j))

    def body(start, finish, a_ref, b_ref, o_ref):
        pl.when(pl.program_id(0) == 0)(start)
        o_ref[...] = _dot(a_ref[...], b_ref[...], dims).astype(out_dtype)
        pl.when(pl.program_id(0) == steps - 1)(finish)

    (out,), got = _hosted_call(
        body, [rider], 2, 1, 0, grid=(steps,), in_specs=[pl.BlockSpec(a_shape, lambda j: (0, a_index)), b_spec],
        out_specs=[pl.BlockSpec((m, tn), lambda j: (0, j))], out_shape=[jax.ShapeDtypeStruct((m, n), out_dtype)],
        name=name, compiler_params=_cp("parallel" if rider is None else "arbitrary"))(a, b)
    return out if rider is None else (out, got[0])


def _chip_rider(sums, rows_total=None, row0=0, into=None):
    _, r, c = sums.shape
    rows_total = rows_total or r

    def make(ins, outs, scratch):
        send_sems, recv_sems, local_sem = scratch
        x, y, core = lax.axis_index("x"), lax.axis_index("y"), lax.axis_index("c")
        mine = 2 * x + y
        land = lambda chip: outs[0].at[chip].at[pl.ds(row0, r)]
        local = pltpu.make_async_copy(ins[0].at[mine], land(mine), local_sem)
        sends, recvs = [], []
        for k in range(1, N_CHIP):
            px = 1 - x if k & 2 else x
            py = 1 - y if k & 1 else y
            kw = dict(send_sem=send_sems.at[k - 1], recv_sem=recv_sems.at[k - 1], device_id=(px, py, core),
                      device_id_type=_MESH)
            sends.append(pltpu.make_async_remote_copy(src_ref=ins[0].at[2 * px + py], dst_ref=land(mine), **kw))
            recvs.append(pltpu.make_async_remote_copy(src_ref=ins[0].at[mine], dst_ref=land(2 * px + py), **kw))

        def start():
            for cp in [local] + sends:
                cp.start()

        def finish():
            local.wait()
            for cp in sends:
                cp.wait_send()
            for cp in recvs:
                cp.wait_recv()

        return start, finish

    return _Rider([sums] + ([] if into is None else [into]), [jax.ShapeDtypeStruct((N_CHIP, rows_total, c), sums.dtype)],
                  [pltpu.SemaphoreType.DMA((N_CHIP - 1,)), pltpu.SemaphoreType.DMA((N_CHIP - 1,)), pltpu.SemaphoreType.DMA(())],
                  {} if into is None else {1: 0}, make)


def _dxn(dproj, wfull, riders, tk, name):
    m, k = dproj.shape
    n = wfull.shape[0]
    tk = _tile(k, tk)
    steps = k // tk

    def body(start, finish, a_ref, b_ref, o_ref):
        j = pl.program_id(0)

        @pl.when(j == 0)
        def _():
            start()
            o_ref[...] = jnp.zeros_like(o_ref)
        o_ref[...] += _dot(a_ref[...], b_ref[...], "nt")
        pl.when(j == steps - 1)(finish)

    (dxn,), got = _hosted_call(
        body, riders, 2, 1, 0, grid=(steps,),
        in_specs=[pl.BlockSpec((m, tk), lambda j: (0, j)), pl.BlockSpec((n, tk), lambda j: (0, j))],
        out_specs=[pl.BlockSpec((m, n), lambda j: (0, 0))], out_shape=[jax.ShapeDtypeStruct((m, n), F32)],
        name=name, compiler_params=_cp("arbitrary"))(dproj, wfull)
    return dxn, got


def _conv_taps(x, w):
    c = x * w[CONV_WIDTH - 1:CONV_WIDTH, :]
    for j in range(CONV_WIDTH - 1):
        c = c + pltpu.roll(x, CONV_WIDTH - 1 - j, 0) * w[j:j + 1, :]
    return c


def _gdn_prep(proj, conv_wt, nh):
    lp = proj.shape[0]
    scale = HEAD_DIM ** -0.5

    def body(x_ref, w_ref, o_ref):
        which = pl.program_id(0) // nh
        c = _conv_taps(x_ref[...], w_ref[...])
        s = c * _sigmoid(c)
        r = lax.rsqrt(jnp.sum(s * s, axis=-1, keepdims=True) + EPS)
        f = jnp.where(which == 0, r * scale, jnp.where(which == 1, r, 1.0))
        o_ref[...] = jnp.where(_iota(s.shape, 0) >= PAD_ROWS, s * f, 0.0)

    return pl.pallas_call(
        body, grid=(3 * nh,),
        in_specs=[pl.BlockSpec((lp, LANES), lambda s: (0, s)), pl.BlockSpec((CONV_WIDTH, LANES), lambda s: (0, s))],
        out_specs=pl.BlockSpec((lp, LANES), lambda s: (0, s)),
        out_shape=jax.ShapeDtypeStruct((lp, 3 * nh * HEAD_DIM), F32), name="gdn_prep",
        compiler_params=_cp("parallel"))(proj, conv_wt)


def _gdn_prep_bwd(proj, conv_wt, dq, dk, dv, dproj, nh):
    lp = proj.shape[0]
    scale = HEAD_DIM ** -0.5
    part = lambda p: pl.BlockSpec((lp, LANES), lambda s: (0, jnp.clip(s - p * nh, 0, nh - 1)))

    def body(x_ref, w_ref, dq_ref, dk_ref, dv_ref, _, dx_ref, dw_ref):
        which = pl.program_id(0) // nh
        x = x_ref[...]
        w = w_ref[...]
        c = _conv_taps(x, w)
        sg = _sigmoid(c)
        s = c * sg
        r = lax.rsqrt(jnp.sum(s * s, axis=-1, keepdims=True) + EPS)
        dy = jnp.where(which == 0, dq_ref[...], jnp.where(which == 1, dk_ref[...], dv_ref[...]))
        dy = jnp.where(_iota(s.shape, 0) >= PAD_ROWS, dy, 0.0)
        y0 = s * r
        dy0 = dy * jnp.where(which == 0, scale, 1.0)
        ds_n = r * (dy0 - y0 * jnp.sum(dy0 * y0, axis=-1, keepdims=True))
        ds = jnp.where(which == 2, dy, ds_n)
        dc = ds * (sg * (1.0 + c * (1.0 - sg)))
        dx = dc * w[CONV_WIDTH - 1:CONV_WIDTH, :]
        rows = [jnp.sum(dc * x, axis=0, keepdims=True)]
        for j in range(CONV_WIDTH - 2, -1, -1):
            sh = CONV_WIDTH - 1 - j
            dx = dx + pltpu.roll(dc, lp - sh, 0) * w[j:j + 1, :]
            rows.insert(0, jnp.sum(dc * pltpu.roll(x, sh, 0), axis=0, keepdims=True))
        dx_ref[...] = dx.astype(BF16)
        dw_ref[...] = jnp.concatenate(rows, axis=0)

    strip = pl.BlockSpec((lp, LANES), lambda s: (0, s))
    taps = pl.BlockSpec((CONV_WIDTH, LANES), lambda s: (0, s))
    return pl.pallas_call(
        body, grid=(3 * nh,), in_specs=[strip, taps, part(0), part(1), part(2), _ANY], out_specs=[strip, taps],
        out_shape=[jax.ShapeDtypeStruct(dproj.shape, BF16), jax.ShapeDtypeStruct((CONV_WIDTH, 3 * nh * HEAD_DIM), F32)],
        input_output_aliases={5: 0}, name="gdn_prep_bwd", compiler_params=_cp("parallel"))(proj, conv_wt, dq, dk, dv, dproj)


def _gates(proj, bias_row, nega_row, nh):
    lp = proj.shape[0]
    nc = lp // CHUNK

    def body(p_ref, b_ref, a_ref, g_ref, gt3_ref, gtf_ref):
        lane = _iota((CHUNK, LANES), 1)
        tri = (_iota((CHUNK, CHUNK), 0) >= _iota((CHUNK, CHUNK), 1)).astype(F32)

        def step(n, carry):
            r0 = pl.multiple_of(n * CHUNK, CHUNK)
            z = p_ref[pl.ds(r0, CHUNK), :] + b_ref[...]
            base = jnp.where(lane < nh, _sigmoid(z),
                             jnp.where(lane < 2 * nh, a_ref[...] * _softplus(z),
                                       jnp.where(lane < 3 * nh, -_softplus(-z), 0.0)))
            base = jnp.where(r0 + _iota((CHUNK, LANES), 0) >= PAD_ROWS, base, 0.0)
            cs = _hdot(tri, base)
            run = jnp.where((lane >= 2 * nh) & (lane < 3 * nh), cs + carry, cs)
            sh = pltpu.roll(run, 2 * nh, 1)
            out = base + jnp.where((lane >= 3 * nh) & (lane < 5 * nh), sh, 0.0)
            g_ref[pl.ds(r0, CHUNK), :] = out
            gt3_ref[n] = out.T
            return carry + cs[CHUNK - 1:CHUNK, :]

        lax.fori_loop(0, nc, step, jnp.zeros((1, LANES), F32))
        gtf_ref[...] = g_ref[...].T

    vec = pl.BlockSpec((1, LANES), lambda i: (0, 0))
    return pl.pallas_call(
        body, grid=(1,), in_specs=[pl.BlockSpec((lp, LANES), lambda i: (0, 8 * nh)), vec, vec],
        out_specs=[pl.BlockSpec((lp, LANES), lambda i: (0, 0)), pl.BlockSpec((nc, LANES, CHUNK), lambda i: (0, 0, 0)),
                   pl.BlockSpec((LANES, lp), lambda i: (0, 0))],
        out_shape=[jax.ShapeDtypeStruct((lp, LANES), F32), jax.ShapeDtypeStruct((nc, LANES, CHUNK), F32),
                   jax.ShapeDtypeStruct((LANES, lp), F32)],
        name="gates", compiler_params=_cp("arbitrary"))(proj, bias_row, nega_row)


def _gates_bwd(proj, bias_row, nega_row, gates, dgate_gdn, dc_t, dproj, nh):
    lp = proj.shape[0]
    nc = lp // CHUNK

    def body(p_ref, b_ref, a_ref, g_ref, dg_ref, dc_ref, _, dz_ref, sm_ref, dct_scr):
        lane = _iota((CHUNK, LANES), 1)
        triu = (_iota((CHUNK, CHUNK), 0) <= _iota((CHUNK, CHUNK), 1)).astype(F32)
        dct_scr[...] = dc_ref[...].T
        sm_ref[...] = jnp.zeros_like(sm_ref)
        dz_ref[:, LANES:] = jnp.zeros((lp, NARROW - LANES), BF16)

        def step(i, carry):
            n = nc - 1 - i
            r0 = pl.multiple_of(n * CHUNK, CHUNK)
            z = p_ref[pl.ds(r0, CHUNK), :] + b_ref[...]
            gt = g_ref[pl.ds(r0, CHUNK), :]
            dgd = dg_ref[pl.ds(r0, CHUNK), :]
            dch = dct_scr[pl.ds(r0, CHUNK), :]
            rc = _hdot(triu, dch) + carry
            sg = _sigmoid(z)
            dz = jnp.where(lane < nh, dgd * sg * (1.0 - sg),
                           jnp.where(lane < 2 * nh, dgd * a_ref[...] * sg,
                                     jnp.where(lane < 3 * nh, rc * (1.0 - sg), 0.0)))
            dz = jnp.where(r0 + _iota((CHUNK, LANES), 0) >= PAD_ROWS, dz, 0.0)
            dz_ref[pl.ds(r0, CHUNK), 0:LANES] = dz.astype(BF16)
            sm_ref[0:1, :] += jnp.sum(dz, axis=0, keepdims=True)
            sm_ref[1:2, :] += jnp.sum(jnp.where((lane >= nh) & (lane < 2 * nh), dgd * gt, 0.0), axis=0, keepdims=True)
            return carry + jnp.sum(dch, axis=0, keepdims=True)

        lax.fori_loop(0, nc, step, jnp.zeros((1, LANES), F32))

    vec = pl.BlockSpec((1, LANES), lambda i: (0, 0))
    full = pl.BlockSpec((lp, LANES), lambda i: (0, 0))
    last = pl.BlockSpec((lp, LANES), lambda i: (0, 8 * nh))
    tail = pl.BlockSpec((lp, NARROW), lambda i: (0, 8 * nh * LANES // NARROW))
    return pl.pallas_call(
        body, grid=(1,), in_specs=[last, vec, vec, full, full, pl.BlockSpec((LANES, lp), lambda i: (0, 0)), _ANY],
        out_specs=[tail, pl.BlockSpec((8, LANES), lambda i: (0, 0))],
        out_shape=[jax.ShapeDtypeStruct(dproj.shape, BF16), jax.ShapeDtypeStruct((8, LANES), F32)],
        scratch_shapes=[pltpu.VMEM((lp, LANES), F32)], input_output_aliases={6: 0},
        name="gates_bwd", compiler_params=_cp("arbitrary"))(proj, bias_row, nega_row, gates, dgate_gdn, dc_t, dproj)


def _tri_inv(a):
    t = jnp.where(_iota(a.shape, 1) == _iota(a.shape, 2), 1.0, 0.0) - a
    p = a
    for _ in range(5):
        ph, pw = _split(p)
        p = _dot3(ph, ph) + (_dot3(ph, pw) + _dot3(pw, ph))
        ph, pw = _split(p)
        th, tw = _split(t)
        t = t + (_dot3(th, ph) + (_dot3(th, pw) + _dot3(tw, ph)))
    return t


def _gdn_chunk(q, k, v, beta, gc, gr, t=None):
    ii, jj = _iota((1, CHUNK, CHUNK), 1), _iota((1, CHUNK, CHUNK), 2)
    causal, strict = ii >= jj, ii > jj
    dm = jnp.where(causal, jnp.exp(jnp.where(causal, gc - gr, 0.0)), 0.0)
    kk = _bdot3(k, k, "nt")
    a = jnp.where(strict, beta * kk * dm, 0.0)
    if t is None:
        t = _tri_inv(a)
    eg = jnp.exp(gc)
    glast = gc[:, CHUNK - 1:CHUNK, :]
    ekd = jnp.exp(glast - gc)
    bv = beta * v
    bk = (beta * eg) * k
    ub = _bdot3(t, jnp.concatenate([bv, bk], axis=2))
    qk = _bdot3(q, k, "nt")
    return dict(causal=causal, strict=strict, dm=dm, kk=kk, a=a, t=t, eg=eg, ekd=ekd, bv=bv, bk=bk,
                u=ub[:, :, :HEAD_DIM], w=ub[:, :, HEAD_DIM:], qk=qk, aqk=jnp.where(causal, qk * dm, 0.0),
                q_dec=q * eg, k_dec=k * ekd, decay=jnp.exp(glast))


def _heads(ref, nh):
    return jnp.stack([ref[:, h * HEAD_DIM:(h + 1) * HEAD_DIM] for h in range(nh)], axis=0)


def _gdn_chunk_inputs(q_ref, k_ref, v_ref, g, gt, nh):
    col = lambda o: jnp.stack([g[:, o + h:o + h + 1] for h in range(nh)], axis=0)
    gr = jnp.stack([gt[3 * nh + h:3 * nh + h + 1, :] for h in range(nh)], axis=0)
    return _heads(q_ref, nh), _heads(k_ref, nh), _heads(v_ref, nh), col(0), col(3 * nh), gr


def _gdn_fwd(qkv, gates, gt3, nh):
    lp = qkv.shape[0]
    nc = lp // CHUNK
    w = nh * HEAD_DIM

    def body(q_ref, k_ref, v_ref, g_ref, gt_ref, o_ref, sall_ref, tall_ref, s_scr):
        @pl.when(pl.program_id(0) == 0)
        def _():
            s_scr[...] = jnp.zeros_like(s_scr)
        c = _gdn_chunk(*_gdn_chunk_inputs(q_ref, k_ref, v_ref, g_ref[...], gt_ref[0], nh))
        s = s_scr[...]
        sall_ref[0] = s
        tall_ref[0] = c["t"]
        v_new = c["u"] - _bdot3(c["w"], s)
        o = _bdot3(c["q_dec"], s) + _bdot3(c["aqk"], v_new)
        s_scr[...] = s * c["decay"] + _bdot3(c["k_dec"], v_new, "tn")
        for h in range(nh):
            o_ref[:, h * HEAD_DIM:(h + 1) * HEAD_DIM] = o[h]

    return pl.pallas_call(
        body, grid=(nc,),
        in_specs=[pl.BlockSpec((CHUNK, w), lambda n: (n, 0)), pl.BlockSpec((CHUNK, w), lambda n: (n, 1)),
                  pl.BlockSpec((CHUNK, w), lambda n: (n, 2)), pl.BlockSpec((CHUNK, LANES), lambda n: (n, 0)),
                  pl.BlockSpec((1, LANES, CHUNK), lambda n: (n, 0, 0))],
        out_specs=[pl.BlockSpec((CHUNK, w), lambda n: (n, 0)),
                   pl.BlockSpec((1, nh, HEAD_DIM, HEAD_DIM), lambda n: (n, 0, 0, 0)),
                   pl.BlockSpec((1, nh, CHUNK, CHUNK), lambda n: (n, 0, 0, 0))],
        out_shape=[jax.ShapeDtypeStruct((lp, w), F32), jax.ShapeDtypeStruct((nc, nh, HEAD_DIM, HEAD_DIM), F32),
                   jax.ShapeDtypeStruct((nc, nh, CHUNK, CHUNK), F32)],
        scratch_shapes=[pltpu.VMEM((nh, HEAD_DIM, HEAD_DIM), F32)],
        name="gdn_fwd", compiler_params=_cp("arbitrary"))(qkv, qkv, qkv, gates, gt3)


def _gdn_bwd(qkv, gates, gt3, s_all, t_all, do, nh, rider=None):
    lp = qkv.shape[0]
    nc = lp // CHUNK
    w = nh * HEAD_DIM
    rev = lambda n: nc - 1 - n

    def body(start, finish, q_ref, k_ref, v_ref, g_ref, gt_ref, s_ref, t_ref, do_ref, dq_ref, dk_ref, dv_ref, dg_ref, ds_scr):
        @pl.when(pl.program_id(0) == 0)
        def _():
            start()
            ds_scr[...] = jnp.zeros_like(ds_scr)
        q, k, v, beta, gc, gr = _gdn_chunk_inputs(q_ref, k_ref, v_ref, g_ref[...], gt_ref[0], nh)
        c = _gdn_chunk(q, k, v, beta, gc, gr, t_ref[0])
        s = s_ref[0]
        dsn = ds_scr[...]
        dout = _heads(do_ref, nh)
        v_new = c["u"] - _bdot3(c["w"], s)
        dq_dec = _bdot3(dout, s, "nt")
        daqk = jnp.where(c["causal"], _bdot3(dout, v_new, "nt"), 0.0)
        dv_new = _bdot3(c["aqk"], dout, "tn") + _bdot3(c["k_dec"], dsn)
        dk_dec = _bdot3(v_new, dsn, "nt")
        ddecay = jnp.sum(jnp.sum(dsn * s, axis=2, keepdims=True), axis=1, keepdims=True)
        dw = -_bdot3(dv_new, s, "nt")
        ds_scr[...] = _bdot3(c["q_dec"], dout, "tn") + c["decay"] * dsn - _bdot3(c["w"], dv_new, "tn")
        duw = jnp.concatenate([dv_new, dw], axis=2)
        dt = _bdot3(duw, jnp.concatenate([c["bv"], c["bk"]], axis=2), "nt")
        dbvk = _bdot3(c["t"], duw, "tn")
        dbv, dbk = dbvk[:, :, :HEAD_DIM], dbvk[:, :, HEAD_DIM:]
        da = jnp.where(c["strict"], -_bdot3(_bdot3(c["t"], dt, "tn"), c["t"], "nt"), 0.0)
        dkk = da * beta * c["dm"]
        dqk = daqk * c["dm"]
        e = da * c["a"] + daqk * c["aqk"]
        dq = dq_dec * c["eg"] + _bdot3(dqk, k)
        dk = (dk_dec * c["ekd"] + _bdot3(dkk, k) + _bdot3(dkk, k, "tn") + _bdot3(dqk, q, "tn")
              + (beta * c["eg"]) * dbk)
        dv = beta * dbv
        rs = lambda x: jnp.sum(x, axis=2, keepdims=True)
        dbeta = rs(dbv * v) + c["eg"] * rs(dbk * k) + rs(da * c["kk"] * c["dm"])
        kd_term = rs(dk_dec * c["k_dec"])
        eh, ew = _split(e)
        ones = jnp.ones((nh, CHUNK, LANES), BF16)
        col_sums = (_dot3(eh, ones, "tn") + _dot3(ew, ones, "tn"))[:, :, 0:1]
        dg_cum = rs(dq_dec * c["q_dec"]) - kd_term + rs(dbk * c["bk"]) + rs(e) - col_sums
        last = jnp.sum(kd_term, axis=1, keepdims=True) + ddecay * c["decay"]
        dg_cum = dg_cum + jnp.where(_iota((1, CHUNK, 1), 1) == CHUNK - 1, last, 0.0)
        lane = _iota((CHUNK, LANES), 1)
        acc = jnp.zeros((CHUNK, LANES), F32)
        for h in range(nh):
            sl = slice(h * HEAD_DIM, (h + 1) * HEAD_DIM)
            dq_ref[:, sl] = dq[h]
            dk_ref[:, sl] = dk[h]
            dv_ref[:, sl] = dv[h]
            acc = acc + jnp.where(lane == h, dbeta[h], 0.0) + jnp.where(lane == nh + h, dg_cum[h], 0.0)
        triu = (_iota((CHUNK, CHUNK), 0) <= _iota((CHUNK, CHUNK), 1)).astype(F32)
        dg_ref[...] = jnp.where(lane < nh, acc, _hdot(triu, acc))
        pl.when(pl.program_id(0) == nc - 1)(finish)

    outs, got = _hosted_call(
        body, [rider], 8, 4, 1, grid=(nc,),
        in_specs=[pl.BlockSpec((CHUNK, w), lambda n: (rev(n), 0)), pl.BlockSpec((CHUNK, w), lambda n: (rev(n), 1)),
                  pl.BlockSpec((CHUNK, w), lambda n: (rev(n), 2)), pl.BlockSpec((CHUNK, LANES), lambda n: (rev(n), 0)),
                  pl.BlockSpec((1, LANES, CHUNK), lambda n: (rev(n), 0, 0)),
                  pl.BlockSpec((1, nh, HEAD_DIM, HEAD_DIM), lambda n: (rev(n), 0, 0, 0)),
                  pl.BlockSpec((1, nh, CHUNK, CHUNK), lambda n: (rev(n), 0, 0, 0)),
                  pl.BlockSpec((CHUNK, w), lambda n: (rev(n), 0))],
        out_specs=[pl.BlockSpec((CHUNK, w), lambda n: (rev(n), 0))] * 3 + [pl.BlockSpec((CHUNK, LANES), lambda n: (rev(n), 0))],
        out_shape=[jax.ShapeDtypeStruct((lp, w), F32)] * 3 + [jax.ShapeDtypeStruct((lp, LANES), F32)],
        scratch_shapes=[pltpu.VMEM((nh, HEAD_DIM, HEAD_DIM), F32)],
        name="gdn_bwd", compiler_params=_cp("arbitrary"))(qkv, qkv, qkv, gates, gt3, s_all, t_all, do)
    return outs, (got[0] if got else None)


def _merge_gdn(o_gdn, proj, norm_w, nh):
    lp = o_gdn.shape[0]

    def body(o_ref, z_ref, w_ref, m_ref):
        o = o_ref[...]
        z = z_ref[...]
        m_ref[...] = (o * _rms(o) * w_ref[...] * (z * _sigmoid(z))).astype(BF16)

    return pl.pallas_call(
        body, grid=(nh,),
        in_specs=[pl.BlockSpec((lp, LANES), lambda s: (0, s)), pl.BlockSpec((lp, LANES), lambda s: (0, 3 * nh + s)),
                  pl.BlockSpec((1, LANES), lambda s: (0, 0))],
        out_specs=pl.BlockSpec((lp, LANES), lambda s: (0, s)),
        out_shape=jax.ShapeDtypeStruct((lp, 2 * nh * HEAD_DIM), BF16), name="merge_gdn",
        compiler_params=_cp("parallel"))(o_gdn, proj, norm_w)


def _merge_gdn_bwd(o_gdn, proj, norm_w, dmerged, nh):
    lp = o_gdn.shape[0]

    def body(o_ref, z_ref, w_ref, dm_ref, do_ref, dz_ref, dw_ref):
        o = o_ref[...]
        r = _rms(o)
        xh = o * r
        silu, dsilu = _silu_and_grad(z_ref[...])
        dm = dm_ref[...]
        dn = dm * silu
        dz_ref[...] = (dm * (xh * w_ref[...]) * dsilu).astype(BF16)
        dnw = dn * w_ref[...]
        do_ref[...] = r * (dnw - xh * jnp.mean(dnw * xh, axis=-1, keepdims=True))

        @pl.when(pl.program_id(0) == 0)
        def _():
            dw_ref[...] = jnp.zeros_like(dw_ref)
        dw_ref[...] += jnp.sum(dn * xh, axis=0, keepdims=True)

    w = nh * HEAD_DIM
    return pl.pallas_call(
        body, grid=(nh,),
        in_specs=[pl.BlockSpec((lp, LANES), lambda s: (0, s)), pl.BlockSpec((lp, LANES), lambda s: (0, 3 * nh + s)),
                  pl.BlockSpec((1, LANES), lambda s: (0, 0)), pl.BlockSpec((lp, LANES), lambda s: (0, s))],
        out_specs=[pl.BlockSpec((lp, LANES), lambda s: (0, s)), pl.BlockSpec((lp, LANES), lambda s: (0, 3 * nh + s)),
                   pl.BlockSpec((1, LANES), lambda s: (0, 0))],
        out_shape=[jax.ShapeDtypeStruct((lp, w), F32), jax.ShapeDtypeStruct((lp, 8 * w + NARROW), BF16),
                   jax.ShapeDtypeStruct((1, LANES), F32)],
        name="merge_gdn_bwd", compiler_params=_cp("arbitrary"))(o_gdn, proj, norm_w, dmerged)


def _fox_prep(proj, qk_w, nh):
    lp = proj.shape[0]

    def body(x_ref, w_ref, o_ref):
        x = x_ref[...]
        o_ref[...] = x * _rms(x) * w_ref[0]

    return pl.pallas_call(
        body, grid=(2 * nh,),
        in_specs=[pl.BlockSpec((lp, LANES), lambda s: (0, 4 * nh + s)), pl.BlockSpec((1, 1, LANES), lambda s: (s // nh, 0, 0))],
        out_specs=pl.BlockSpec((lp, LANES), lambda s: (0, s)),
        out_shape=jax.ShapeDtypeStruct((lp, 2 * nh * HEAD_DIM), F32), name="fox_prep",
        compiler_params=_cp("parallel"))(proj, qk_w)


def _fox_prep_bwd(proj, qk_w, dq, dk, dproj, nh):
    lp = proj.shape[0]
    part = lambda p: pl.BlockSpec((lp, LANES), lambda s: (0, jnp.clip(s - p * nh, 0, nh - 1)))

    def body(x_ref, w_ref, dq_ref, dk_ref, _, dx_ref, dw_ref):
        x = x_ref[...]
        r = _rms(x)
        xh = x * r
        dy = jnp.where(pl.program_id(0) < nh, dq_ref[...], dk_ref[...])
        dyw = dy * w_ref[0]
        dx_ref[...] = (r * (dyw - xh * jnp.mean(dyw * xh, axis=-1, keepdims=True))).astype(BF16)

        @pl.when(pl.program_id(0) % nh == 0)
        def _():
            dw_ref[...] = jnp.zeros_like(dw_ref)
        dw_ref[0] += jnp.sum(dy * xh, axis=0, keepdims=True)

    strip = pl.BlockSpec((lp, LANES), lambda s: (0, 4 * nh + s))
    wsp = pl.BlockSpec((1, 1, LANES), lambda s: (s // nh, 0, 0))
    return pl.pallas_call(
        body, grid=(2 * nh,), in_specs=[strip, wsp, part(0), part(1), _ANY], out_specs=[strip, wsp],
        out_shape=[jax.ShapeDtypeStruct(dproj.shape, BF16), jax.ShapeDtypeStruct((2, 1, LANES), F32)],
        input_output_aliases={4: 0}, name="fox_prep_bwd", compiler_params=_cp("arbitrary"))(proj, qk_w, dq, dk, dproj)


def _fox_probs(q, k, gates, crow, h, i, nh):
    kl = k.shape[0]
    lane = _iota((Q_BLOCK, LANES), 1)
    ct = jnp.sum(jnp.where(lane == 4 * nh + h, gates, 0.0), axis=1, keepdims=True)
    tq, kq = _iota((Q_BLOCK, Q_BLOCK), 0), _iota((Q_BLOCK, Q_BLOCK), 1)
    if i == 0:
        s = _bdot(q, k, "nt") * (HEAD_DIM ** -0.5) + (ct - crow)
        s = jnp.where((kq <= tq) & ((kq >= PAD_ROWS) | (tq < PAD_ROWS)), s, NEG)
    else:
        crow = jnp.where(_iota((1, kl), 1) < PAD_ROWS, -NEG, crow)
        s = _bdot(q, k, "nt") * (HEAD_DIM ** -0.5) + (ct - crow)
        s = jnp.concatenate([s[:, :kl - Q_BLOCK], jnp.where(kq <= tq, s[:, kl - Q_BLOCK:], NEG)], axis=1)
    p = jnp.exp(s - jnp.max(s, axis=1, keepdims=True))
    return p / jnp.sum(p, axis=1, keepdims=True)


FOX_HEADS_PER_STEP = 2


def _fox_specs(lp, nh):
    hw = FOX_HEADS_PER_STEP * LANES
    return [pl.BlockSpec((Q_BLOCK, hw), lambda g, i: (i, g)),
            pl.BlockSpec((lp, hw), lambda g, i: (0, nh // FOX_HEADS_PER_STEP + g)),
            pl.BlockSpec((lp, hw), lambda g, i: (0, 6 * nh // FOX_HEADS_PER_STEP + g)),
            pl.BlockSpec((Q_BLOCK, LANES), lambda g, i: (i, 0)),
            pl.BlockSpec((LANES, lp), lambda g, i: (0, 0))]


def _fox_fwd(qkn, proj, gates, gtf, nh):
    lp = qkn.shape[0]

    def body(q_ref, k_ref, v_ref, g_ref, gt_ref, o_ref):
        g, i = pl.program_id(0), pl.program_id(1)
        for j in range(lp // Q_BLOCK):
            @pl.when(i == j)
            def _(j=j):
                kl = (j + 1) * Q_BLOCK
                for hh in range(FOX_HEADS_PER_STEP):
                    h = FOX_HEADS_PER_STEP * g + hh
                    sl = slice(hh * LANES, (hh + 1) * LANES)
                    p = _fox_probs(q_ref[:, sl], k_ref[0:kl, sl], g_ref[...], gt_ref[pl.ds(4 * nh + h, 1), :][:, 0:kl],
                                   h, j, nh)
                    o_ref[:, sl] = _bdot(p, v_ref[0:kl, sl])

    return pl.pallas_call(
        body, grid=(nh // FOX_HEADS_PER_STEP, lp // Q_BLOCK), in_specs=_fox_specs(lp, nh),
        out_specs=pl.BlockSpec((Q_BLOCK, FOX_HEADS_PER_STEP * LANES), lambda g, i: (i, g)),
        out_shape=jax.ShapeDtypeStruct((lp, nh * HEAD_DIM), F32), name="fox_fwd",
        compiler_params=_cp("parallel", "parallel"))(qkn, qkn, proj, gates, gtf)


def _fox_bwd(qkn, proj, gates, gtf, do, dproj, nh):
    lp = qkn.shape[0]
    nq = lp // Q_BLOCK
    w = nh * HEAD_DIM
    scale = HEAD_DIM ** -0.5

    def body(q_ref, k_ref, v_ref, g_ref, gt_ref, do_ref, _, dq_ref, dk_ref, dc_ref, dv_ref, dv_scr):
        g, i = pl.program_id(0), pl.program_id(1)

        @pl.when(i == 0)
        def _():
            dk_ref[...] = jnp.zeros_like(dk_ref)
            dv_scr[...] = jnp.zeros_like(dv_scr)
            dc_ref[...] = jnp.zeros_like(dc_ref)
        for j in range(nq):
            @pl.when(i == j)
            def _(j=j):
                kl = (j + 1) * Q_BLOCK
                for hh in range(FOX_HEADS_PER_STEP):
                    h = FOX_HEADS_PER_STEP * g + hh
                    sl = slice(hh * LANES, (hh + 1) * LANES)
                    q, k = q_ref[:, sl], k_ref[0:kl, sl]
                    p = _fox_probs(q, k, g_ref[...], gt_ref[pl.ds(4 * nh + h, 1), :][:, 0:kl], h, j, nh)
                    dout = do_ref[:, sl]
                    dp = _bdot(dout, v_ref[0:kl, sl], "nt")
                    ds = p * (dp - jnp.sum(p * dp, axis=1, keepdims=True))
                    dq_ref[:, sl] = _bdot(ds, k) * scale
                    dk_ref[0:kl, sl] += _bdot(ds, q, "tn") * scale
                    dv_scr[0:kl, sl] += _bdot(p, dout, "tn")
                    dc_ref[hh, :, 0:kl] -= jnp.sum(ds, axis=0, keepdims=True)

        @pl.when(i == nq - 1)
        def _():
            dv_ref[...] = dv_scr[...].astype(BF16)

    hw = FOX_HEADS_PER_STEP * LANES
    blk = pl.BlockSpec((Q_BLOCK, hw), lambda g, i: (i, g))
    col = pl.BlockSpec((lp, hw), lambda g, i: (0, g))
    return pl.pallas_call(
        body, grid=(nh // FOX_HEADS_PER_STEP, nq), in_specs=_fox_specs(lp, nh) + [blk, _ANY],
        out_specs=[blk, col, pl.BlockSpec((FOX_HEADS_PER_STEP, 1, lp), lambda g, i: (g, 0, 0)),
                   pl.BlockSpec((lp, hw), lambda g, i: (0, 6 * nh // FOX_HEADS_PER_STEP + g))],
        out_shape=[jax.ShapeDtypeStruct((lp, w), F32)] * 2 + [jax.ShapeDtypeStruct((nh, 1, lp), F32),
                                                             jax.ShapeDtypeStruct(dproj.shape, BF16)],
        scratch_shapes=[pltpu.VMEM((lp, hw), F32)], input_output_aliases={6: 3},
        name="fox_bwd", compiler_params=_cp("parallel", "arbitrary"))(qkn, qkn, proj, gates, gtf, do, dproj)


def _merge_fox(o_fox, proj, merged, nh):
    lp = o_fox.shape[0]

    def body(o_ref, z_ref, _, m_ref):
        z = z_ref[...]
        m_ref[...] = (o_ref[...] * (z * _sigmoid(z))).astype(BF16)

    return pl.pallas_call(
        body, grid=(nh,),
        in_specs=[pl.BlockSpec((lp, LANES), lambda s: (0, s)), pl.BlockSpec((lp, LANES), lambda s: (0, 7 * nh + s)), _ANY],
        out_specs=pl.BlockSpec((lp, LANES), lambda s: (0, nh + s)),
        out_shape=jax.ShapeDtypeStruct(merged.shape, BF16), input_output_aliases={2: 0}, name="merge_fox",
        compiler_params=_cp("parallel"))(o_fox, proj, merged)


def _merge_fox_bwd(o_fox, proj, dmerged, dproj, nh):
    lp = o_fox.shape[0]

    def body(o_ref, z_ref, dm_ref, _, do_ref, dz_ref):
        silu, dsilu = _silu_and_grad(z_ref[...])
        dm = dm_ref[...]
        do_ref[...] = dm * silu
        dz_ref[...] = (dm * o_ref[...] * dsilu).astype(BF16)

    w = nh * HEAD_DIM
    return pl.pallas_call(
        body, grid=(nh,),
        in_specs=[pl.BlockSpec((lp, LANES), lambda s: (0, s)), pl.BlockSpec((lp, LANES), lambda s: (0, 7 * nh + s)),
                  pl.BlockSpec((lp, LANES), lambda s: (0, nh + s)), _ANY],
        out_specs=[pl.BlockSpec((lp, LANES), lambda s: (0, s)), pl.BlockSpec((lp, LANES), lambda s: (0, 7 * nh + s))],
        out_shape=[jax.ShapeDtypeStruct((lp, w), F32), jax.ShapeDtypeStruct(dproj.shape, BF16)],
        input_output_aliases={3: 1}, name="merge_fox_bwd", compiler_params=_cp("parallel"))(o_fox, proj, dmerged, dproj)


def _post(out, x, target, post_w):
    lp, d = out.shape

    def body(o_ref, x_ref, t_ref, w_ref, dy_ref, do_ref, loss_ref, dw_ref):
        i = pl.program_id(0)

        @pl.when(i == 0)
        def _():
            loss_ref[...] = jnp.zeros_like(loss_ref)
            dw_ref[...] = jnp.zeros_like(dw_ref)
        o = o_ref[...]
        r = _rms(o)
        nrm = o * r
        err = jnp.where(i > 0, x_ref[...] + nrm * w_ref[...] - t_ref[...], 0.0)
        loss_ref[0:1, :] += 0.5 * jnp.sum(jnp.sum(err * err, axis=1, keepdims=True), axis=0, keepdims=True) / d
        dy = err / d
        dy_ref[...] = dy
        dw_ref[...] += jnp.sum(dy * nrm, axis=0, keepdims=True)
        dyw = dy * w_ref[...]
        do_ref[...] = (r * (dyw - nrm * jnp.mean(dyw * nrm, axis=-1, keepdims=True))).astype(BF16)

    row = pl.BlockSpec((Q_BLOCK, d), lambda i: (i, 0))
    vec = pl.BlockSpec((1, d), lambda i: (0, 0))
    return pl.pallas_call(
        body, grid=(lp // Q_BLOCK,), in_specs=[row, _x_rows(d), _x_rows(d), vec],
        out_specs=[_x_rows(d), row, pl.BlockSpec((8, LANES), lambda i: (0, 0)), vec],
        out_shape=[jax.ShapeDtypeStruct(x.shape, F32), jax.ShapeDtypeStruct((lp, d), BF16),
                   jax.ShapeDtypeStruct((8, LANES), F32), jax.ShapeDtypeStruct((1, d), F32)],
        name="post", compiler_params=_cp("arbitrary"))(out, x, target, post_w)


def _prenorm_bwd(dxn, x, meta, w, dy, rider=None):
    seq, d = x.shape
    lp = seq + Q_BLOCK

    def body(start, finish, dx_ref, x_ref, m_ref, w_ref, dy_ref, gx_ref, gm_ref, dw_ref):
        i = pl.program_id(0)
        pl.when(i == 0)(start)
        h = _h_tile(i, x_ref, m_ref)
        r = _rms(h)
        xh = h * r
        dxn_ = dx_ref[...]
        dxw = dxn_ * w_ref[...]
        dh = jnp.where(i > 0, dy_ref[...], 0.0) + r * (dxw - xh * jnp.mean(dxw * xh, axis=-1, keepdims=True))
        gx_ref[...] = dh

        @pl.when(i == 0)
        def _():
            dw_ref[...] = jnp.zeros_like(dw_ref)
            gm_ref[...] = dh[PAD_ROWS:, :]
        dw_ref[...] += jnp.sum(dxn_ * xh, axis=0, keepdims=True)
        pl.when(i == lp // Q_BLOCK - 1)(finish)

    vec = pl.BlockSpec((1, d), lambda i: (0, 0))
    met = pl.BlockSpec((N_META, d), lambda i: (0, 0))
    outs, got = _hosted_call(
        body, [rider], 5, 3, 0, grid=(lp // Q_BLOCK,),
        in_specs=[pl.BlockSpec((Q_BLOCK, d), lambda i: (i, 0)), _x_rows(d), met, vec, _x_rows(d)],
        out_specs=[_x_rows(d), met, vec],
        out_shape=[jax.ShapeDtypeStruct((seq, d), F32), jax.ShapeDtypeStruct((N_META, d), F32),
                   jax.ShapeDtypeStruct((1, d), F32)],
        name="prenorm_bwd", compiler_params=_cp("arbitrary"))(dxn, x, meta, w, dy)
    return outs, (got[0] if got else None)


def _layer_grads(x, target, meta, pre_w, wfull, conv_wt, a_log, dt_bias, gdn_norm_w, fq_w, fk_w, f_bias, w_out, post_w,
                 late_weights=None, w_out_grads=None):
    nh = a_log.shape[1]
    zpad = jnp.zeros((1, LANES - 3 * nh), F32)
    bias_row = jnp.concatenate([jnp.zeros((1, nh), F32), dt_bias, f_bias, zpad], axis=1)
    nega_row = jnp.concatenate([jnp.zeros((1, nh), F32), -jnp.exp(a_log), jnp.zeros((1, nh), F32), zpad], axis=1)
    qk_w = jnp.stack([fq_w, fk_w])

    xn = _prenorm(x, meta, pre_w)
    if late_weights is None:
        proj = _matmul(xn, wfull, "nn", MM_TILE, F32, "proj")
    else:
        proj, got = _matmul(xn, wfull, "nn", MM_TILE, F32, "proj", late_weights[0])
        conv_wt, w_out = late_weights[1](got)
    qkv = _gdn_prep(proj, conv_wt, nh)
    gates, gt3, gtf = _gates(proj, bias_row, nega_row, nh)
    o_gdn, s_all, t_all = _gdn_fwd(qkv, gates, gt3, nh)
    qkn = _fox_prep(proj, qk_w, nh)
    o_fox = _fox_fwd(qkn, proj, gates, gtf, nh)
    merged = _merge_fox(o_fox, proj, _merge_gdn(o_gdn, proj, gdn_norm_w, nh), nh)
    out = _matmul(merged, w_out, "nn", 4 * LANES, F32, "out_proj")
    dy, dout, loss_blk, dpost_w = _post(out, x, target, post_w)

    dw_out = _matmul(merged, dout, "tn", 4 * LANES, BF16, "dw_out")
    if w_out_grads is None:
        dmerged, gdn_rider = _matmul(dout, w_out, "nt", 4 * LANES, F32, "dmerged"), None
    else:
        dmerged, got = _matmul(dout, w_out, "nt", 4 * LANES, F32, "dmerged", w_out_grads[0](dw_out))
        gdn_rider = w_out_grads[1](dw_out, got)
    do_gdn, dproj, dgdn_norm_w = _merge_gdn_bwd(o_gdn, proj, gdn_norm_w, dmerged, nh)
    do_fox, dproj = _merge_fox_bwd(o_fox, proj, dmerged, dproj, nh)
    dqn, dkn, dc_t, dproj = _fox_bwd(qkn, proj, gates, gtf, do_fox, dproj, nh)
    dproj, dqk_w = _fox_prep_bwd(proj, qk_w, dqn, dkn, dproj, nh)
    (dgq, dgk, dgv, dgate), w_out_parts = _gdn_bwd(qkv, gates, gt3, s_all, t_all, do_gdn, nh, gdn_rider)
    dproj, dconv_wt = _gdn_prep_bwd(proj, conv_wt, dgq, dgk, dgv, dproj, nh)
    dc_rows = jnp.pad(dc_t.reshape(nh, -1), ((2 * nh, LANES - 3 * nh), (0, 0)))
    dproj, gate_sums = _gates_bwd(proj, bias_row, nega_row, gates, dgate, dc_rows, dproj, nh)
    return dict(
        loss=loss_blk[0:1, 0:1], dy=dy, xn=xn, dproj=dproj, post_w=dpost_w,
        conv_wt=dconv_wt, a_log=gate_sums[1:2, nh:2 * nh], dt_bias=gate_sums[0:1, nh:2 * nh],
        gdn_norm_w=dgdn_norm_w, fq_w=dqk_w[0], fk_w=dqk_w[1], f_bias=gate_sums[0:1, 2 * nh:3 * nh], w_out=dw_out,
        w_out_parts=w_out_parts)


def _cast_bf16(a, tr, name):
    r, c = a.shape

    def body(a_ref, o_ref):
        o_ref[...] = a_ref[...].astype(BF16)

    return pl.pallas_call(
        body, grid=(r // tr,), in_specs=[pl.BlockSpec((tr, c), lambda i: (i, 0))],
        out_specs=pl.BlockSpec((tr, c), lambda i: (i, 0)), out_shape=jax.ShapeDtypeStruct((r, c), BF16),
        name=name, compiler_params=_cp("parallel"))(a)


def _column_major(a):
    return jnp.transpose(a, (2, 0, 1))


def _cast_bf16_column_major(a3, name):
    _, r, c = a3.shape

    def body(a_ref, o_ref):
        o_ref[...] = a_ref[...].reshape(LANES, r).T.astype(BF16)

    return pl.pallas_call(
        body, grid=(pl.cdiv(c, LANES),), in_specs=[pl.BlockSpec((LANES, 1, r), lambda i: (i, 0, 0))],
        out_specs=pl.BlockSpec((r, LANES), lambda i: (0, i)), out_shape=jax.ShapeDtypeStruct((r, c), BF16),
        name=name, compiler_params=_cp("parallel"))(_column_major(a3))


def _adamw_column_major(w3, parts, m3, v3, name):
    _, r, c = w3.shape
    n_parts = parts.shape[0]

    def body(w_ref, p_ref, m_ref, v_ref, g_ref, d_ref, nm_ref, nv_ref):
        g = p_ref[0].astype(F32)
        for s in range(1, n_parts):
            g = g + p_ref[s].astype(F32)
        g = g.T
        flat = lambda ref: ref[...].reshape(LANES, r)
        m_new = ADAM_B1 * flat(m_ref) + (1.0 - ADAM_B1) * g
        v_new = ADAM_B2 * flat(v_ref) + (1.0 - ADAM_B2) * (g * g)
        m_hat = m_new / (1.0 - ADAM_B1 ** ADAM_STEP)
        v_hat = v_new / (1.0 - ADAM_B2 ** ADAM_STEP)
        delta = -ADAM_LR * (m_hat / (jnp.sqrt(v_hat) + ADAM_EPS) + ADAM_WD * flat(w_ref))
        for ref, val in ((g_ref, g), (d_ref, delta), (nm_ref, m_new), (nv_ref, v_new)):
            ref[...] = val.reshape(LANES, 1, r)

    blk = pl.BlockSpec((LANES, 1, r), lambda i: (i, 0, 0))
    outs = pl.pallas_call(
        body, grid=(pl.cdiv(c, LANES),), in_specs=[blk, pl.BlockSpec((n_parts, r, LANES), lambda i: (0, 0, i)), blk, blk],
        out_specs=[blk] * 4, out_shape=[jax.ShapeDtypeStruct((c, 1, r), F32)] * 4, name=name,
        compiler_params=_cp("parallel"))(_column_major(w3), parts, _column_major(m3), _column_major(v3))
    return [jnp.transpose(o, (1, 2, 0)) for o in outs]


def _gather_copies(ins, outs, send_sems, recv_sems, local_sems):
    n = len(ins)
    x, y, c = lax.axis_index("x"), lax.axis_index("y"), lax.axis_index("c")
    me, sibling = (x, y, c), (x, y, 1 - c)
    xn, yn, dg = (1 - x, y), (x, 1 - y), (1 - x, 1 - y)

    def copy(a, k, block, to, src=None):
        px, py, pc = block
        rows = outs[a].at[4 * px + 2 * py + pc]
        return pltpu.make_async_remote_copy(
            src_ref=rows if src is None else src, dst_ref=rows, send_sem=send_sems.at[a, k],
            recv_sem=recv_sems.at[a, k], device_id=to, device_id_type=_MESH)

    local = [pltpu.make_async_copy(ins[a], outs[a].at[4 * x + 2 * y + c], local_sems.at[a]) for a in range(n)]
    own = [cp for a in range(n) for cp in (copy(a, 0, me, sibling, src=ins[a]), copy(a, 1, me, (*xn, c), src=ins[a]),
                                           copy(a, 2, me, (*yn, c), src=ins[a]))]

    def start():
        for cp in local + own:
            cp.start()

    def finish():
        for a in range(n):
            @pl.when(c == 1)
            def _(a=a):
                copy(a, 1, (*xn, c), me).wait_recv()
                copy(a, 3, (*xn, c), (*yn, c)).start()

            @pl.when(c == 0)
            def _(a=a):
                copy(a, 2, (*yn, c), me).wait_recv()
                copy(a, 3, (*yn, c), (*xn, c)).start()
        for a in range(n):
            pl.when(c == 0)(copy(a, 1, (*xn, c), me).wait_recv)
            copy(a, 4, (*xn, c), sibling).start()
            pl.when(c == 1)(copy(a, 2, (*yn, c), me).wait_recv)
            copy(a, 5, (*yn, c), sibling).start()
        for a in range(n):
            copy(a, 3, (*dg, c), me).wait_recv()
            copy(a, 6, (*dg, c), sibling).start()
        for a in range(n):
            copy(a, 0, sibling, me).wait_recv()
            for k, chip in ((4, xn), (5, yn), (6, dg)):
                copy(a, k, (*chip, 1 - c), me).wait_recv()
                copy(a, k, (*chip, c), sibling).wait_send()
            copy(a, 3, (*xn, c), (*yn, c)).wait_send()
        for cp in own:
            cp.wait_send()
        for cp in local:
            cp.wait()

    return start, finish


def _gather_scratch(n):
    return [pltpu.SemaphoreType.DMA((n, N_DEV - 1)), pltpu.SemaphoreType.DMA((n, N_DEV - 1)), pltpu.SemaphoreType.DMA((n,))]


def _gather_rider(arrays):
    return _Rider(list(arrays), [jax.ShapeDtypeStruct((N_DEV,) + a.shape, a.dtype) for a in arrays],
                  _gather_scratch(len(arrays)), {}, lambda ins, outs, scratch: _gather_copies(ins, outs, *scratch))


def _all_gather(arrays, name):
    n = len(arrays)

    def body(*refs):
        start, finish = _gather_copies(refs[:n], refs[n:2 * n], *refs[2 * n:])
        start()
        finish()

    return pl.pallas_call(
        body, in_specs=[_ANY] * n, out_specs=[_ANY] * n,
        out_shape=[jax.ShapeDtypeStruct((N_DEV,) + a.shape, a.dtype) for a in arrays],
        scratch_shapes=_gather_scratch(n), name=name)(*arrays)


SLAB = 10 * LANES


def _slab_start(blk, nh, cols):
    in_second_half = blk >= N_DEV // 2
    shift = (2 * nh if in_second_half else 0) if isinstance(blk, int) else jnp.where(in_second_half, 2 * nh, 0)
    return (blk * cols - shift) // LANES * LANES


def _pair_rider(dw_rows=None, parts=None, nh=None):
    if dw_rows is not None:
        r, full = dw_rows.shape
        cols = (full - NARROW + 3 * nh) // N_DEV
        out_shapes = [jax.ShapeDtypeStruct((N_CHIP, r, SLAB), dw_rows.dtype), jax.ShapeDtypeStruct((r, NARROW), dw_rows.dtype)]
    else:
        out_shapes = [jax.ShapeDtypeStruct((N_CHIP,) + parts.shape[1:], parts.dtype)]

    def make(ins, outs, scratch):
        send_sems, recv_sems = scratch
        x, y, c = lax.axis_index("x"), lax.axis_index("y"), lax.axis_index("c")
        kw = lambda k: dict(send_sem=send_sems.at[k], recv_sem=recv_sems.at[k], device_id=(x, y, 1 - c), device_id_type=_MESH)
        copies = []
        for q in range(N_CHIP):
            if dw_rows is not None:
                first = pl.multiple_of(_slab_start(2 * q + 1 - c, nh, cols), LANES)
                copies.append(pltpu.make_async_remote_copy(src_ref=ins[0].at[:, pl.ds(first, SLAB)], dst_ref=outs[0].at[q], **kw(q)))
            else:
                copies.append(pltpu.make_async_remote_copy(src_ref=ins[0].at[2 * q + 1 - c], dst_ref=outs[0].at[q], **kw(q)))
        if dw_rows is not None:
            copies.append(pltpu.make_async_remote_copy(src_ref=ins[0].at[:, pl.ds(full - NARROW, NARROW)], dst_ref=outs[1],
                                                       **kw(N_CHIP)))

        def start():
            for cp in copies:
                cp.start()

        def finish():
            for cp in copies:
                cp.wait()

        return start, finish

    return _Rider([dw_rows if dw_rows is not None else parts], out_shapes,
                  [pltpu.SemaphoreType.DMA((N_CHIP + 1,)), pltpu.SemaphoreType.DMA((N_CHIP + 1,))], {}, make)


def _relayout_pair_sum(dwfull, got_slabs, got_tail, core, nh, tr, name):
    d, full = dwfull.shape
    w = nh * HEAD_DIM
    cols = (8 * w + 3 * nh) // N_DEV
    segs = _native_segments(nh)

    def block(f_ref, s_ref, t_ref, q, blk):
        st = _slab_start(blk, nh, cols)
        wide = f_ref[:, st:st + SLAB].astype(F32) + s_ref[q].astype(F32)
        tail = f_ref[:, 8 * w:].astype(F32) + t_ref[...].astype(F32)
        pieces = []
        for s0, s1, t0 in segs:
            lo, hi = max(s0, blk * cols), min(s1, (blk + 1) * cols)
            if lo < hi:
                at = t0 + lo - s0
                pieces.append(tail[:, at - 8 * w:at - 8 * w + hi - lo] if at >= 8 * w else wide[:, at - st:at - st + hi - lo])
        return (pieces[0] if len(pieces) == 1 else jnp.concatenate(pieces, axis=1)).astype(dwfull.dtype)

    def body(core_ref, f_ref, s_ref, t_ref, o_ref):
        for parity in range(2):
            @pl.when(core_ref[0] == parity)
            def _(parity=parity):
                for q in range(N_CHIP):
                    o_ref[q] = block(f_ref, s_ref, t_ref, q, 2 * q + parity)

    return pl.pallas_call(
        body,
        grid_spec=pltpu.PrefetchScalarGridSpec(
            num_scalar_prefetch=1, grid=(d // tr,),
            in_specs=[pl.BlockSpec((tr, full), lambda i, c_ref: (i, 0)), pl.BlockSpec((N_CHIP, tr, SLAB), lambda i, c_ref: (0, i, 0)),
                      pl.BlockSpec((tr, NARROW), lambda i, c_ref: (i, 0))],
            out_specs=pl.BlockSpec((N_CHIP, tr, cols), lambda i, c_ref: (0, i, 0))),
        out_shape=jax.ShapeDtypeStruct((N_CHIP, d, cols), dwfull.dtype), name=name,
        compiler_params=_cp("parallel"))(core, dwfull, got_slabs, got_tail)


def _pair_sum(parts, got, core, tr, name):
    _, r, c = parts.shape

    def body(core_ref, p_ref, g_ref, o_ref):
        o_ref[...] = (p_ref[...].astype(F32) + g_ref[...].astype(F32)).astype(o_ref.dtype)

    return pl.pallas_call(
        body,
        grid_spec=pltpu.PrefetchScalarGridSpec(
            num_scalar_prefetch=1, grid=(N_CHIP, r // tr),
            in_specs=[pl.BlockSpec((1, tr, c), lambda q, i, core_ref: (2 * q + core_ref[0], i, 0)),
                      pl.BlockSpec((1, tr, c), lambda q, i, core_ref: (q, i, 0))],
            out_specs=pl.BlockSpec((1, tr, c), lambda q, i, core_ref: (q, i, 0))),
        out_shape=jax.ShapeDtypeStruct((N_CHIP, r, c), parts.dtype), name=name,
        compiler_params=_cp("parallel", "parallel"))(core, parts, got)


def _native_segments(nh):
    w = nh * HEAD_DIM
    return [(0, 4 * w, 0), (4 * w, 4 * w + 2 * nh, 8 * w), (4 * w + 2 * nh, 8 * w + 2 * nh, 4 * w),
            (8 * w + 2 * nh, 8 * w + 3 * nh, 8 * w + 2 * nh)]


def _relayout_w_in(wg, nh, tr):
    _, d, cols = wg.shape
    w = nh * HEAD_DIM

    def native(ref, j0, j1):
        out = []
        while j0 < j1:
            blk = j0 // cols
            end = min(j1, (blk + 1) * cols)
            out.append(ref[blk, :, pl.ds(j0 - blk * cols, end - j0)])
            j0 = end
        return out

    def body(g_ref, o_ref):
        for cidx in range(8 * w // LANES):
            j0 = cidx * LANES + (0 if cidx * LANES < 4 * w else 2 * nh)
            pieces = native(g_ref, j0, j0 + LANES)
            o_ref[:, cidx * LANES:(cidx + 1) * LANES] = pieces[0] if len(pieces) == 1 else jnp.concatenate(pieces, axis=1)
        pieces = (native(g_ref, 4 * w, 4 * w + 2 * nh) + native(g_ref, 8 * w + 2 * nh, 8 * w + 3 * nh)
                  + [jnp.zeros((tr, NARROW - 3 * nh), wg.dtype)])
        o_ref[:, 8 * w:] = jnp.concatenate(pieces, axis=1)

    return pl.pallas_call(
        body, grid=(d // tr,), in_specs=[pl.BlockSpec((N_DEV, tr, cols), lambda i: (0, i, 0))],
        out_specs=pl.BlockSpec((tr, 8 * w + NARROW), lambda i: (i, 0)),
        out_shape=jax.ShapeDtypeStruct((d, 8 * w + NARROW), wg.dtype),
        name="relayout_w_in", compiler_params=_cp("parallel"))(wg)


def _adamw(w, parts, m, v, tr, name):
    r, c = w.shape
    n_parts = parts.shape[0]

    def body(w_ref, p_ref, m_ref, v_ref, g_ref, d_ref, nm_ref, nv_ref):
        g = p_ref[0].astype(F32)
        for s in range(1, n_parts):
            g = g + p_ref[s].astype(F32)
        m_new = ADAM_B1 * m_ref[...] + (1.0 - ADAM_B1) * g
        v_new = ADAM_B2 * v_ref[...] + (1.0 - ADAM_B2) * (g * g)
        m_hat = m_new / (1.0 - ADAM_B1 ** ADAM_STEP)
        v_hat = v_new / (1.0 - ADAM_B2 ** ADAM_STEP)
        g_ref[...] = g
        d_ref[...] = -ADAM_LR * (m_hat / (jnp.sqrt(v_hat) + ADAM_EPS) + ADAM_WD * w_ref[...])
        nm_ref[...] = m_new
        nv_ref[...] = v_new

    blk = pl.BlockSpec((tr, c), lambda i: (i, 0))
    return pl.pallas_call(
        body, grid=(r // tr,), in_specs=[blk, pl.BlockSpec((n_parts, tr, c), lambda i: (0, i, 0)), blk, blk],
        out_specs=[blk] * 4, out_shape=[jax.ShapeDtypeStruct((r, c), F32)] * 4, name=name,
        compiler_params=_cp("parallel"))(w, parts, m, v)


def _pack_small(d, pre, post, a_log, dt_bias, f_bias, gdn_w, fq_w, fk_w, extra):
    row2 = jnp.concatenate([a_log, dt_bias, f_bias, gdn_w, fq_w, fk_w, extra], axis=1)
    row2 = jnp.pad(row2, ((0, 0), (0, d - row2.shape[1])))
    return jnp.concatenate([pre, post, row2, jnp.zeros((5, d), F32)], axis=0)


def _unpack_small(p, nh):
    o = 3 * nh
    return dict(pre=p[0:1], post=p[1:2], a_log=p[2:3, 0:nh], dt_bias=p[2:3, nh:2 * nh], f_bias=p[2:3, 2 * nh:o],
                gdn_w=p[2:3, o:o + HEAD_DIM], fq_w=p[2:3, o + HEAD_DIM:o + 2 * HEAD_DIM],
                fk_w=p[2:3, o + 2 * HEAD_DIM:o + 3 * HEAD_DIM], extra=p[2, o + 3 * HEAD_DIM])


def kernel(x, meta_tokens, pre_norm_w, w_in, conv_w, a_log, dt_bias, gdn_norm_w, fox_q_norm_w, fox_k_norm_w, fox_f_bias, w_out, post_norm_w, loss_target, m_meta_tokens, m_pre_norm_w, m_w_in, m_conv_w, m_a_log, m_dt_bias, m_gdn_norm_w, m_fox_q_norm_w, m_fox_k_norm_w, m_fox_f_bias, m_w_out, m_post_norm_w, v_meta_tokens, v_pre_norm_w, v_w_in, v_conv_w, v_a_log, v_dt_bias, v_gdn_norm_w, v_fox_q_norm_w, v_fox_k_norm_w, v_fox_f_bias, v_w_out, v_post_norm_w):
    nh = a_log.shape[1]
    d = x.shape[-1]
    w = nh * HEAD_DIM
    zero = jnp.zeros((1, 1), F32)

    wg, mg = _all_gather([_cast_bf16_column_major(w_in, "cast_w_in"), meta_tokens], "gather_weights")
    wfull = _relayout_w_in(wg, nh, 256)
    meta_full = mg.transpose(1, 0, 2).reshape(N_META, d)
    late_weights = (_gather_rider([conv_w[0].T, _cast_bf16(w_out[0], 256, "cast_w_out")]),
                    lambda got: (got[0].transpose(1, 0, 2).reshape(CONV_WIDTH, 3 * w), got[1].reshape(2 * w, d)))
    core = lax.axis_index("c")
    dev = 4 * lax.axis_index("x") + 2 * lax.axis_index("y") + core
    core_arr = jnp.reshape(core, (1,)).astype(jnp.int32)

    out_parts = lambda dw_out: dw_out.reshape(N_DEV, 2 * w // N_DEV, d)
    g = _layer_grads(
        x[0], loss_target[0], meta_full, pre_norm_w, wfull, None, a_log, dt_bias, gdn_norm_w,
        fox_q_norm_w, fox_k_norm_w, fox_f_bias, None, post_norm_w, late_weights=late_weights,
        w_out_grads=(lambda dw_out: _pair_rider(parts=out_parts(dw_out)),
                     lambda dw_out, got: _chip_rider(_pair_sum(out_parts(dw_out), got[0], core_arr, 256, "pair_sum_w_out"))))
    p_out = g["w_out_parts"][0]
    xn, dproj, half = g["xn"], g["dproj"], d // 2
    dw_a = _matmul(xn, dproj, "tn", MM_TILE, BF16, "dw_in_a", a_cols=(half, 0))
    dw_b, got_a = _matmul(xn, dproj, "tn", MM_TILE, BF16, "dw_in_b", _pair_rider(dw_rows=dw_a, nh=nh), a_cols=(half, 1))
    sums_a = _relayout_pair_sum(dw_a, got_a[0], got_a[1], core_arr, nh, 128, "relayout_pair_sum_a")
    dxn, (p_in_a, got_b) = _dxn(dproj, wfull, [_chip_rider(sums_a, rows_total=d), _pair_rider(dw_rows=dw_b, nh=nh)],
                                MM_TILE, "dxn")
    sums_b = _relayout_pair_sum(dw_b, got_b[0], got_b[1], core_arr, nh, 128, "relayout_pair_sum_b")
    (grad_x, dmeta, dpre_w), p_in = _prenorm_bwd(dxn, x[0], meta_full, pre_norm_w, g["dy"],
                                                 _chip_rider(sums_b, rows_total=d, row0=half, into=p_in_a[0]))
    p_in = p_in[0]
    small = _pack_small(d, dpre_w, g["post_w"], g["a_log"], g["dt_bias"], g["f_bias"], g["gdn_norm_w"], g["fq_w"],
                        g["fk_w"], g["loss"])
    a_conv, a_meta, p_small = _all_gather([g["conv_wt"], dmeta, small], "gather_small_grads")
    p_conv = lax.dynamic_slice_in_dim(a_conv, dev * conv_w.shape[1], conv_w.shape[1], axis=2).transpose(0, 2, 1)
    p_meta = lax.dynamic_slice_in_dim(a_meta, dev * meta_tokens.shape[1], meta_tokens.shape[1], axis=2)

    r_in = _adamw_column_major(w_in, p_in, m_w_in, v_w_in, "adamw_w_in")
    r_out = _adamw(w_out[0], p_out, m_w_out[0], v_w_out[0], 64, "adamw_w_out")
    r_conv = _adamw(conv_w[0], p_conv, m_conv_w[0], v_conv_w[0], conv_w.shape[1], "adamw_conv_w")
    r_meta = _adamw(meta_tokens, p_meta, m_meta_tokens, v_meta_tokens, N_META, "adamw_meta")
    pk = lambda pre, post, a, dt, gw, fq, fk, fb: _pack_small(d, pre, post, a, dt, fb, gw, fq, fk, zero)
    r_small = _adamw(
        pk(pre_norm_w, post_norm_w, a_log, dt_bias, gdn_norm_w, fox_q_norm_w, fox_k_norm_w, fox_f_bias), p_small,
        pk(m_pre_norm_w, m_post_norm_w, m_a_log, m_dt_bias, m_gdn_norm_w, m_fox_q_norm_w, m_fox_k_norm_w, m_fox_f_bias),
        pk(v_pre_norm_w, v_post_norm_w, v_a_log, v_dt_bias, v_gdn_norm_w, v_fox_q_norm_w, v_fox_k_norm_w, v_fox_f_bias),
        8, "adamw_small")

    sm = [_unpack_small(r, nh) for r in r_small]
    outs = []
    for i in range(4):
        s = sm[i]
        outs += [r_meta[i], s["pre"], r_in[i], r_conv[i][None], s["a_log"], s["dt_bias"], s["gdn_w"], s["fq_w"],
                 s["fk_w"], s["f_bias"], r_out[i][None], s["post"]]
    return (sm[0]["extra"], grad_x[None], *outs)
```

```python
import jax
import jax.numpy as jnp
from jax import lax
from jax.experimental import pallas as pl
from jax.experimental.pallas import tpu as pltpu

F32, BF16 = jnp.float32, jnp.bfloat16
HEAD_DIM = 128
N_META = 16
CONV_WIDTH = 4
CHUNK = 64
Q_BLOCK = 128
LANES = 128
EPS = 1e-6
PAD_ROWS = Q_BLOCK - N_META
N_DEV = 8
N_CHIP = 4
VMEM_LIMIT = 56 * 1024 * 1024
NEG = -1e30
NARROW = 2 * LANES
MM_TILE = 6 * LANES

ADAM_LR, ADAM_B1, ADAM_B2, ADAM_EPS, ADAM_WD, ADAM_STEP = 0.001, 0.9, 0.999, 1e-08, 0.01, 10

_DN = {"nn": (((1,), (0,)), ((), ())), "nt": (((1,), (1,)), ((), ())), "tn": (((0,), (0,)), ((), ()))}
_DN3 = {"nn": (((2,), (1,)), ((0,), (0,))), "nt": (((2,), (2,)), ((0,), (0,))), "tn": (((1,), (1,)), ((0,), (0,)))}
_ANY = pl.BlockSpec(memory_space=pl.ANY)
_MESH = pl.DeviceIdType.MESH


def _cp(*sem):
    return pltpu.CompilerParams(dimension_semantics=sem, vmem_limit_bytes=VMEM_LIMIT)


def _dot(a, b, dims="nn", prec=None):
    return lax.dot_general(a, b, _DN[dims], precision=prec, preferred_element_type=F32)


def _bdot(a, b, dims="nn"):
    return _dot(a.astype(BF16), b.astype(BF16), dims)


def _hdot(a, b, dims="nn"):
    return _dot(a, b, dims, prec=lax.Precision.HIGHEST)


def _dot3(a, b, dims="nn"):
    return lax.dot_general(a, b, _DN3[dims], preferred_element_type=F32)


def _bdot3(a, b, dims="nn"):
    return _dot3(a.astype(BF16), b.astype(BF16), dims)


def _split(a):
    hi = a.astype(BF16)
    return hi, (a - hi.astype(F32)).astype(BF16)


def _iota(shape, dim):
    return lax.broadcasted_iota(jnp.int32, shape, dim)


def _sigmoid(z):
    return 1.0 / (1.0 + jnp.exp(-z))


def _softplus(z):
    e = jnp.exp(-jnp.abs(z))
    u = 1.0 + e
    l1p = jnp.where(u == 1.0, e, jnp.log(u) * (e / jnp.where(u == 1.0, 1.0, u - 1.0)))
    return jnp.maximum(z, 0.0) + l1p


def _silu_and_grad(z):
    s = _sigmoid(z)
    return z * s, s * (1.0 + z * (1.0 - s))


def _rms(x):
    return lax.rsqrt(jnp.mean(x * x, axis=-1, keepdims=True) + EPS)


def _h_tile(i, x_ref, meta_ref):
    first = jnp.concatenate([jnp.zeros((PAD_ROWS, x_ref.shape[1]), F32), meta_ref[...]], axis=0)
    return jnp.where(i == 0, first, x_ref[...])


def _x_rows(d):
    return pl.BlockSpec((Q_BLOCK, d), lambda i: (jnp.maximum(i - 1, 0), 0))


def _prenorm(x, meta, w):
    seq, d = x.shape
    lp = seq + Q_BLOCK

    def body(x_ref, m_ref, w_ref, o_ref):
        h = _h_tile(pl.program_id(0), x_ref, m_ref)
        o_ref[...] = (h * _rms(h) * w_ref[...]).astype(BF16)

    return pl.pallas_call(
        body, grid=(lp // Q_BLOCK,),
        in_specs=[_x_rows(d), pl.BlockSpec((N_META, d), lambda i: (0, 0)), pl.BlockSpec((1, d), lambda i: (0, 0))],
        out_specs=pl.BlockSpec((Q_BLOCK, d), lambda i: (i, 0)),
        out_shape=jax.ShapeDtypeStruct((lp, d), BF16), name="prenorm", compiler_params=_cp("parallel"))(x, meta, w)


def _tile(n, want):
    return max(t for t in range(LANES, want + 1, LANES) if n % t == 0)


class _Rider:
    def __init__(self, inputs, out_shapes, scratch, aliases, make):
        self.inputs, self.out_shapes, self.scratch, self.aliases, self.make = inputs, out_shapes, scratch, aliases, make


def _hosted_call(body, riders, n_in, n_out, n_scratch, *, in_specs, out_specs, out_shape, scratch_shapes=(), aliases=None,
                 **kw):
    riders = [r for r in riders if r is not None]
    r_in = [len(r.inputs) for r in riders]
    r_out = [len(r.out_shapes) for r in riders]
    r_scr = [len(r.scratch) for r in riders]
    al = dict(aliases or {})
    for k, r in enumerate(riders):
        al.update({n_in + sum(r_in[:k]) + i: n_out + sum(r_out[:k]) + o for i, o in r.aliases.items()})

    def full_body(*refs):
        ins, rest = refs[:n_in + sum(r_in)], refs[n_in + sum(r_in):]
        outs, scr = rest[:n_out + sum(r_out)], rest[n_out + sum(r_out):]
        hooks = [r.make(ins[n_in + sum(r_in[:k]):n_in + sum(r_in[:k + 1])], outs[n_out + sum(r_out[:k]):n_out + sum(r_out[:k + 1])],
                        scr[n_scratch + sum(r_scr[:k]):n_scratch + sum(r_scr[:k + 1])]) for k, r in enumerate(riders)]

        def start():
            for h in hooks:
                h[0]()

        def finish():
            for h in hooks:
                h[1]()

        body(start, finish, *ins[:n_in], *outs[:n_out], *scr[:n_scratch])

    call = pl.pallas_call(
        full_body, in_specs=list(in_specs) + [_ANY] * sum(r_in), out_specs=list(out_specs) + [_ANY] * sum(r_out),
        out_shape=list(out_shape) + [s for r in riders for s in r.out_shapes],
        scratch_shapes=list(scratch_shapes) + [s for r in riders for s in r.scratch], input_output_aliases=al, **kw)

    def run(*args):
        res = call(*args, *[t for r in riders for t in r.inputs])
        return res[:n_out], [res[n_out + sum(r_out[:k]):n_out + sum(r_out[:k + 1])] for k in range(len(riders))]

    return run


def _matmul(a, b, dims, tn, out_dtype, name, rider=None, a_cols=None):
    a_shape = a.shape if a_cols is None else (a.shape[0], a_cols[0])
    a_index = 0 if a_cols is None else a_cols[1]
    m = a_shape[1] if dims == "tn" else a_shape[0]
    n = b.shape[0] if dims == "nt" else b.shape[1]
    kdim = b.shape[1] if dims == "nt" else b.shape[0]
    tn = _tile(n, tn)
    steps = n // tn
    b_spec = pl.BlockSpec((tn, kdim), lambda j: (j, 0)) if dims == "nt" else pl.BlockSpec((kdim, tn), lambda j: (0, j))

    def body(start, finish, a_ref, b_ref, o_ref):
        pl.when(pl.program_id(0) == 0)(start)
        o_ref[...] = _dot(a_ref[...], b_ref[...], dims).astype(out_dtype)
        pl.when(pl.program_id(0) == steps - 1)(finish)

    (out,), got = _hosted_call(
        body, [rider], 2, 1, 0, grid=(steps,), in_specs=[pl.BlockSpec(a_shape, lambda j: (0, a_index)), b_spec],
        out_specs=[pl.BlockSpec((m, tn), lambda j: (0, j))], out_shape=[jax.ShapeDtypeStruct((m, n), out_dtype)],
        name=name, compiler_params=_cp("parallel" if rider is None else "arbitrary"))(a, b)
    return out if rider is None else (out, got[0])


def _chip_rider(sums, rows_total=None, row0=0, into=None):
    _, r, c = sums.shape
    rows_total = rows_total or r

    def make(ins, outs, scratch):
        send_sems, recv_sems, local_sem = scratch
        x, y, core = lax.axis_index("x"), lax.axis_index("y"), lax.axis_index("c")
        mine = 2 * x + y
        land = lambda chip: outs[0].at[chip].at[pl.ds(row0, r)]
        local = pltpu.make_async_copy(ins[0].at[mine], land(mine), local_sem)
        sends, recvs = [], []
        for k in range(1, N_CHIP):
            px = 1 - x if k & 2 else x
            py = 1 - y if k & 1 else y
            kw = dict(send_sem=send_sems.at[k - 1], recv_sem=recv_sems.at[k - 1], device_id=(px, py, core),
                      device_id_type=_MESH)
            sends.append(pltpu.make_async_remote_copy(src_ref=ins[0].at[2 * px + py], dst_ref=land(mine), **kw))
            recvs.append(pltpu.make_async_remote_copy(src_ref=ins[0].at[mine], dst_ref=land(2 * px + py), **kw))

        def start():
            for cp in [local] + sends:
                cp.start()

        def finish():
            local.wait()
            for cp in sends:
                cp.wait_send()
            for cp in recvs:
                cp.wait_recv()

        return start, finish

    return _Rider([sums] + ([] if into is None else [into]), [jax.ShapeDtypeStruct((N_CHIP, rows_total, c), sums.dtype)],
                  [pltpu.SemaphoreType.DMA((N_CHIP - 1,)), pltpu.SemaphoreType.DMA((N_CHIP - 1,)), pltpu.SemaphoreType.DMA(())],
                  {} if into is None else {1: 0}, make)


def _dxn(dproj, wfull, riders, tk, name):
    m, k = dproj.shape
    n = wfull.shape[0]
    tk = _tile(k, tk)
    steps = k // tk

    def body(start, finish, a_ref, b_ref, o_ref):
        j = pl.program_id(0)

        @pl.when(j == 0)
        def _():
            start()
            o_ref[...] = jnp.zeros_like(o_ref)
        o_ref[...] += _dot(a_ref[...], b_ref[...], "nt")
        pl.when(j == steps - 1)(finish)

    (dxn,), got = _hosted_call(
        body, riders, 2, 1, 0, grid=(steps,),
        in_specs=[pl.BlockSpec((m, tk), lambda j: (0, j)), pl.BlockSpec((n, tk), lambda j: (0, j))],
        out_specs=[pl.BlockSpec((m, n), lambda j: (0, 0))], out_shape=[jax.ShapeDtypeStruct((m, n), F32)],
        name=name, compiler_params=_cp("arbitrary"))(dproj, wfull)
    return dxn, got


def _conv_taps(x, w):
    c = x * w[CONV_WIDTH - 1:CONV_WIDTH, :]
    for j in range(CONV_WIDTH - 1):
        c = c + pltpu.roll(x, CONV_WIDTH - 1 - j, 0) * w[j:j + 1, :]
    return c


def _gdn_prep(proj, conv_wt, nh):
    lp = proj.shape[0]
    scale = HEAD_DIM ** -0.5

    def body(x_ref, w_ref, o_ref):
        which = pl.program_id(0) // nh
        c = _conv_taps(x_ref[...], w_ref[...])
        s = c * _sigmoid(c)
        r = lax.rsqrt(jnp.sum(s * s, axis=-1, keepdims=True) + EPS)
        f = jnp.where(which == 0, r * scale, jnp.where(which == 1, r, 1.0))
        o_ref[...] = jnp.where(_iota(s.shape, 0) >= PAD_ROWS, s * f, 0.0)

    return pl.pallas_call(
        body, grid=(3 * nh,),
        in_specs=[pl.BlockSpec((lp, LANES), lambda s: (0, s)), pl.BlockSpec((CONV_WIDTH, LANES), lambda s: (0, s))],
        out_specs=pl.BlockSpec((lp, LANES), lambda s: (0, s)),
        out_shape=jax.ShapeDtypeStruct((lp, 3 * nh * HEAD_DIM), F32), name="gdn_prep",
        compiler_params=_cp("parallel"))(proj, conv_wt)


def _gdn_prep_bwd(proj, conv_wt, dq, dk, dv, dproj, nh):
    lp = proj.shape[0]
    scale = HEAD_DIM ** -0.5
    part = lambda p: pl.BlockSpec((lp, LANES), lambda s: (0, jnp.clip(s - p * nh, 0, nh - 1)))

    def body(x_ref, w_ref, dq_ref, dk_ref, dv_ref, _, dx_ref, dw_ref):
        which = pl.program_id(0) // nh
        x = x_ref[...]
        w = w_ref[...]
        c = _conv_taps(x, w)
        sg = _sigmoid(c)
        s = c * sg
        r = lax.rsqrt(jnp.sum(s * s, axis=-1, keepdims=True) + EPS)
        dy = jnp.where(which == 0, dq_ref[...], jnp.where(which == 1, dk_ref[...], dv_ref[...]))
        dy = jnp.where(_iota(s.shape, 0) >= PAD_ROWS, dy, 0.0)
        y0 = s * r
        dy0 = dy * jnp.where(which == 0, scale, 1.0)
        ds_n = r * (dy0 - y0 * jnp.sum(dy0 * y0, axis=-1, keepdims=True))
        ds = jnp.where(which == 2, dy, ds_n)
        dc = ds * (sg * (1.0 + c * (1.0 - sg)))
        dx = dc * w[CONV_WIDTH - 1:CONV_WIDTH, :]
        rows = [jnp.sum(dc * x, axis=0, keepdims=True)]
        for j in range(CONV_WIDTH - 2, -1, -1):
            sh = CONV_WIDTH - 1 - j
            dx = dx + pltpu.roll(dc, lp - sh, 0) * w[j:j + 1, :]
            rows.insert(0, jnp.sum(dc * pltpu.roll(x, sh, 0), axis=0, keepdims=True))
        dx_ref[...] = dx.astype(BF16)
        dw_ref[...] = jnp.concatenate(rows, axis=0)

    strip = pl.BlockSpec((lp, LANES), lambda s: (0, s))
    taps = pl.BlockSpec((CONV_WIDTH, LANES), lambda s: (0, s))
    return pl.pallas_call(
        body, grid=(3 * nh,), in_specs=[strip, taps, part(0), part(1), part(2), _ANY], out_specs=[strip, taps],
        out_shape=[jax.ShapeDtypeStruct(dproj.shape, BF16), jax.ShapeDtypeStruct((CONV_WIDTH, 3 * nh * HEAD_DIM), F32)],
        input_output_aliases={5: 0}, name="gdn_prep_bwd", compiler_params=_cp("parallel"))(proj, conv_wt, dq, dk, dv, dproj)


def _gates(proj, bias_row, nega_row, nh):
    lp = proj.shape[0]
    nc = lp // CHUNK

    def body(p_ref, b_ref, a_ref, g_ref, gt3_ref, gtf_ref):
        lane = _iota((CHUNK, LANES), 1)
        tri = (_iota((CHUNK, CHUNK), 0) >= _iota((CHUNK, CHUNK), 1)).astype(F32)

        def step(n, carry):
            r0 = pl.multiple_of(n * CHUNK, CHUNK)
            z = p_ref[pl.ds(r0, CHUNK), :] + b_ref[...]
            base = jnp.where(lane < nh, _sigmoid(z),
                             jnp.where(lane < 2 * nh, a_ref[...] * _softplus(z),
                                       jnp.where(lane < 3 * nh, -_softplus(-z), 0.0)))
            base = jnp.where(r0 + _iota((CHUNK, LANES), 0) >= PAD_ROWS, base, 0.0)
            cs = _hdot(tri, base)
            run = jnp.where((lane >= 2 * nh) & (lane < 3 * nh), cs + carry, cs)
            sh = pltpu.roll(run, 2 * nh, 1)
            out = base + jnp.where((lane >= 3 * nh) & (lane < 5 * nh), sh, 0.0)
            g_ref[pl.ds(r0, CHUNK), :] = out
            gt3_ref[n] = out.T
            return carry + cs[CHUNK - 1:CHUNK, :]

        lax.fori_loop(0, nc, step, jnp.zeros((1, LANES), F32))
        gtf_ref[...] = g_ref[...].T

    vec = pl.BlockSpec((1, LANES), lambda i: (0, 0))
    return pl.pallas_call(
        body, grid=(1,), in_specs=[pl.BlockSpec((lp, LANES), lambda i: (0, 8 * nh)), vec, vec],
        out_specs=[pl.BlockSpec((lp, LANES), lambda i: (0, 0)), pl.BlockSpec((nc, LANES, CHUNK), lambda i: (0, 0, 0)),
                   pl.BlockSpec((LANES, lp), lambda i: (0, 0))],
        out_shape=[jax.ShapeDtypeStruct((lp, LANES), F32), jax.ShapeDtypeStruct((nc, LANES, CHUNK), F32),
                   jax.ShapeDtypeStruct((LANES, lp), F32)],
        name="gates", compiler_params=_cp("arbitrary"))(proj, bias_row, nega_row)


def _gates_bwd(proj, bias_row, nega_row, gates, dgate_gdn, dc_t, dproj, nh):
    lp = proj.shape[0]
    nc = lp // CHUNK

    def body(p_ref, b_ref, a_ref, g_ref, dg_ref, dc_ref, _, dz_ref, sm_ref, dct_scr):
        lane = _iota((CHUNK, LANES), 1)
        triu = (_iota((CHUNK, CHUNK), 0) <= _iota((CHUNK, CHUNK), 1)).astype(F32)
        dct_scr[...] = dc_ref[...].T
        sm_ref[...] = jnp.zeros_like(sm_ref)
        dz_ref[:, LANES:] = jnp.zeros((lp, NARROW - LANES), BF16)

        def step(i, carry):
            n = nc - 1 - i
            r0 = pl.multiple_of(n * CHUNK, CHUNK)
            z = p_ref[pl.ds(r0, CHUNK), :] + b_ref[...]
            gt = g_ref[pl.ds(r0, CHUNK), :]
            dgd = dg_ref[pl.ds(r0, CHUNK), :]
            dch = dct_scr[pl.ds(r0, CHUNK), :]
            rc = _hdot(triu, dch) + carry
            sg = _sigmoid(z)
            dz = jnp.where(lane < nh, dgd * sg * (1.0 - sg),
                           jnp.where(lane < 2 * nh, dgd * a_ref[...] * sg,
                                     jnp.where(lane < 3 * nh, rc * (1.0 - sg), 0.0)))
            dz = jnp.where(r0 + _iota((CHUNK, LANES), 0) >= PAD_ROWS, dz, 0.0)
            dz_ref[pl.ds(r0, CHUNK), 0:LANES] = dz.astype(BF16)
            sm_ref[0:1, :] += jnp.sum(dz, axis=0, keepdims=True)
            sm_ref[1:2, :] += jnp.sum(jnp.where((lane >= nh) & (lane < 2 * nh), dgd * gt, 0.0), axis=0, keepdims=True)
            return carry + jnp.sum(dch, axis=0, keepdims=True)

        lax.fori_loop(0, nc, step, jnp.zeros((1, LANES), F32))

    vec = pl.BlockSpec((1, LANES), lambda i: (0, 0))
    full = pl.BlockSpec((lp, LANES), lambda i: (0, 0))
    last = pl.BlockSpec((lp, LANES), lambda i: (0, 8 * nh))
    tail = pl.BlockSpec((lp, NARROW), lambda i: (0, 8 * nh * LANES // NARROW))
    return pl.pallas_call(
        body, grid=(1,), in_specs=[last, vec, vec, full, full, pl.BlockSpec((LANES, lp), lambda i: (0, 0)), _ANY],
        out_specs=[tail, pl.BlockSpec((8, LANES), lambda i: (0, 0))],
        out_shape=[jax.ShapeDtypeStruct(dproj.shape, BF16), jax.ShapeDtypeStruct((8, LANES), F32)],
        scratch_shapes=[pltpu.VMEM((lp, LANES), F32)], input_output_aliases={6: 0},
        name="gates_bwd", compiler_params=_cp("arbitrary"))(proj, bias_row, nega_row, gates, dgate_gdn, dc_t, dproj)


def _tri_inv(a):
    t = jnp.where(_iota(a.shape, 1) == _iota(a.shape, 2), 1.0, 0.0) - a
    p = a
    for _ in range(5):
        ph, pw = _split(p)
        p = _dot3(ph, ph) + (_dot3(ph, pw) + _dot3(pw, ph))
        ph, pw = _split(p)
        th, tw = _split(t)
        t = t + (_dot3(th, ph) + (_dot3(th, pw) + _dot3(tw, ph)))
    return t


def _gdn_chunk(q, k, v, beta, gc, gr, t=None):
    ii, jj = _iota((1, CHUNK, CHUNK), 1), _iota((1, CHUNK, CHUNK), 2)
    causal, strict = ii >= jj, ii > jj
    dm = jnp.where(causal, jnp.exp(jnp.where(causal, gc - gr, 0.0)), 0.0)
    kk = _bdot3(k, k, "nt")
    a = jnp.where(strict, beta * kk * dm, 0.0)
    if t is None:
        t = _tri_inv(a)
    eg = jnp.exp(gc)
    glast = gc[:, CHUNK - 1:CHUNK, :]
    ekd = jnp.exp(glast - gc)
    bv = beta * v
    bk = (beta * eg) * k
    ub = _bdot3(t, jnp.concatenate([bv, bk], axis=2))
    qk = _bdot3(q, k, "nt")
    return dict(causal=causal, strict=strict, dm=dm, kk=kk, a=a, t=t, eg=eg, ekd=ekd, bv=bv, bk=bk,
                u=ub[:, :, :HEAD_DIM], w=ub[:, :, HEAD_DIM:], qk=qk, aqk=jnp.where(causal, qk * dm, 0.0),
                q_dec=q * eg, k_dec=k * ekd, decay=jnp.exp(glast))


def _heads(ref, nh):
    return jnp.stack([ref[:, h * HEAD_DIM:(h + 1) * HEAD_DIM] for h in range(nh)], axis=0)


def _gdn_chunk_inputs(q_ref, k_ref, v_ref, g, gt, nh):
    col = lambda o: jnp.stack([g[:, o + h:o + h + 1] for h in range(nh)], axis=0)
    gr = jnp.stack([gt[3 * nh + h:3 * nh + h + 1, :] for h in range(nh)], axis=0)
    return _heads(q_ref, nh), _heads(k_ref, nh), _heads(v_ref, nh), col(0), col(3 * nh), gr


def _gdn_fwd(qkv, gates, gt3, nh):
    lp = qkv.shape[0]
    nc = lp // CHUNK
    w = nh * HEAD_DIM

    def body(q_ref, k_ref, v_ref, g_ref, gt_ref, o_ref, sall_ref, tall_ref, s_scr):
        @pl.when(pl.program_id(0) == 0)
        def _():
            s_scr[...] = jnp.zeros_like(s_scr)
        c = _gdn_chunk(*_gdn_chunk_inputs(q_ref, k_ref, v_ref, g_ref[...], gt_ref[0], nh))
        s = s_scr[...]
        sall_ref[0] = s
        tall_ref[0] = c["t"]
        v_new = c["u"] - _bdot3(c["w"], s)
        o = _bdot3(c["q_dec"], s) + _bdot3(c["aqk"], v_new)
        s_scr[...] = s * c["decay"] + _bdot3(c["k_dec"], v_new, "tn")
        for h in range(nh):
            o_ref[:, h * HEAD_DIM:(h + 1) * HEAD_DIM] = o[h]

    return pl.pallas_call(
        body, grid=(nc,),
        in_specs=[pl.BlockSpec((CHUNK, w), lambda n: (n, 0)), pl.BlockSpec((CHUNK, w), lambda n: (n, 1)),
                  pl.BlockSpec((CHUNK, w), lambda n: (n, 2)), pl.BlockSpec((CHUNK, LANES), lambda n: (n, 0)),
                  pl.BlockSpec((1, LANES, CHUNK), lambda n: (n, 0, 0))],
        out_specs=[pl.BlockSpec((CHUNK, w), lambda n: (n, 0)),
                   pl.BlockSpec((1, nh, HEAD_DIM, HEAD_DIM), lambda n: (n, 0, 0, 0)),
                   pl.BlockSpec((1, nh, CHUNK, CHUNK), lambda n: (n, 0, 0, 0))],
        out_shape=[jax.ShapeDtypeStruct((lp, w), F32), jax.ShapeDtypeStruct((nc, nh, HEAD_DIM, HEAD_DIM), F32),
                   jax.ShapeDtypeStruct((nc, nh, CHUNK, CHUNK), F32)],
        scratch_shapes=[pltpu.VMEM((nh, HEAD_DIM, HEAD_DIM), F32)],
        name="gdn_fwd", compiler_params=_cp("arbitrary"))(qkv, qkv, qkv, gates, gt3)


def _gdn_bwd(qkv, gates, gt3, s_all, t_all, do, nh, rider=None):
    lp = qkv.shape[0]
    nc = lp // CHUNK
    w = nh * HEAD_DIM
    rev = lambda n: nc - 1 - n

    def body(start, finish, q_ref, k_ref, v_ref, g_ref, gt_ref, s_ref, t_ref, do_ref, dq_ref, dk_ref, dv_ref, dg_ref, ds_scr):
        @pl.when(pl.program_id(0) == 0)
        def _():
            start()
            ds_scr[...] = jnp.zeros_like(ds_scr)
        q, k, v, beta, gc, gr = _gdn_chunk_inputs(q_ref, k_ref, v_ref, g_ref[...], gt_ref[0], nh)
        c = _gdn_chunk(q, k, v, beta, gc, gr, t_ref[0])
        s = s_ref[0]
        dsn = ds_scr[...]
        dout = _heads(do_ref, nh)
        v_new = c["u"] - _bdot3(c["w"], s)
        dq_dec = _bdot3(dout, s, "nt")
        daqk = jnp.where(c["causal"], _bdot3(dout, v_new, "nt"), 0.0)
        dv_new = _bdot3(c["aqk"], dout, "tn") + _bdot3(c["k_dec"], dsn)
        dk_dec = _bdot3(v_new, dsn, "nt")
        ddecay = jnp.sum(jnp.sum(dsn * s, axis=2, keepdims=True), axis=1, keepdims=True)
        dw = -_bdot3(dv_new, s, "nt")
        ds_scr[...] = _bdot3(c["q_dec"], dout, "tn") + c["decay"] * dsn - _bdot3(c["w"], dv_new, "tn")
        duw = jnp.concatenate([dv_new, dw], axis=2)
        dt = _bdot3(duw, jnp.concatenate([c["bv"], c["bk"]], axis=2), "nt")
        dbvk = _bdot3(c["t"], duw, "tn")
        dbv, dbk = dbvk[:, :, :HEAD_DIM], dbvk[:, :, HEAD_DIM:]
        da = jnp.where(c["strict"], -_bdot3(_bdot3(c["t"], dt, "tn"), c["t"], "nt"), 0.0)
        dkk = da * beta * c["dm"]
        dqk = daqk * c["dm"]
        e = da * c["a"] + daqk * c["aqk"]
        dq = dq_dec * c["eg"] + _bdot3(dqk, k)
        dk = (dk_dec * c["ekd"] + _bdot3(dkk, k) + _bdot3(dkk, k, "tn") + _bdot3(dqk, q, "tn")
              + (beta * c["eg"]) * dbk)
        dv = beta * dbv
        rs = lambda x: jnp.sum(x, axis=2, keepdims=True)
        dbeta = rs(dbv * v) + c["eg"] * rs(dbk * k) + rs(da * c["kk"] * c["dm"])
        kd_term = rs(dk_dec * c["k_dec"])
        eh, ew = _split(e)
        ones = jnp.ones((nh, CHUNK, LANES), BF16)
        col_sums = (_dot3(eh, ones, "tn") + _dot3(ew, ones, "tn"))[:, :, 0:1]
        dg_cum = rs(dq_dec * c["q_dec"]) - kd_term + rs(dbk * c["bk"]) + rs(e) - col_sums
        last = jnp.sum(kd_term, axis=1, keepdims=True) + ddecay * c["decay"]
        dg_cum = dg_cum + jnp.where(_iota((1, CHUNK, 1), 1) == CHUNK - 1, last, 0.0)
        lane = _iota((CHUNK, LANES), 1)
        acc = jnp.zeros((CHUNK, LANES), F32)
        for h in range(nh):
            sl = slice(h * HEAD_DIM, (h + 1) * HEAD_DIM)
            dq_ref[:, sl] = dq[h]
            dk_ref[:, sl] = dk[h]
            dv_ref[:, sl] = dv[h]
            acc = acc + jnp.where(lane == h, dbeta[h], 0.0) + jnp.where(lane == nh + h, dg_cum[h], 0.0)
        triu = (_iota((CHUNK, CHUNK), 0) <= _iota((CHUNK, CHUNK), 1)).astype(F32)
        dg_ref[...] = jnp.where(lane < nh, acc, _hdot(triu, acc))
        pl.when(pl.program_id(0) == nc - 1)(finish)

    outs, got = _hosted_call(
        body, [rider], 8, 4, 1, grid=(nc,),
        in_specs=[pl.BlockSpec((CHUNK, w), lambda n: (rev(n), 0)), pl.BlockSpec((CHUNK, w), lambda n: (rev(n), 1)),
                  pl.BlockSpec((CHUNK, w), lambda n: (rev(n), 2)), pl.BlockSpec((CHUNK, LANES), lambda n: (rev(n), 0)),
                  pl.BlockSpec((1, LANES, CHUNK), lambda n: (rev(n), 0, 0)),
                  pl.BlockSpec((1, nh, HEAD_DIM, HEAD_DIM), lambda n: (rev(n), 0, 0, 0)),
                  pl.BlockSpec((1, nh, CHUNK, CHUNK), lambda n: (rev(n), 0, 0, 0)),
                  pl.BlockSpec((CHUNK, w), lambda n: (rev(n), 0))],
        out_specs=[pl.BlockSpec((CHUNK, w), lambda n: (rev(n), 0))] * 3 + [pl.BlockSpec((CHUNK, LANES), lambda n: (rev(n), 0))],
        out_shape=[jax.ShapeDtypeStruct((lp, w), F32)] * 3 + [jax.ShapeDtypeStruct((lp, LANES), F32)],
        scratch_shapes=[pltpu.VMEM((nh, HEAD_DIM, HEAD_DIM), F32)],
        name="gdn_bwd", compiler_params=_cp("arbitrary"))(qkv, qkv, qkv, gates, gt3, s_all, t_all, do)
    return outs, (got[0] if got else None)


def _merge_gdn(o_gdn, proj, norm_w, nh):
    lp = o_gdn.shape[0]

    def body(o_ref, z_ref, w_ref, m_ref):
        o = o_ref[...]
        z = z_ref[...]
        m_ref[...] = (o * _rms(o) * w_ref[...] * (z * _sigmoid(z))).astype(BF16)

    return pl.pallas_call(
        body, grid=(nh,),
        in_specs=[pl.BlockSpec((lp, LANES), lambda s: (0, s)), pl.BlockSpec((lp, LANES), lambda s: (0, 3 * nh + s)),
                  pl.BlockSpec((1, LANES), lambda s: (0, 0))],
        out_specs=pl.BlockSpec((lp, LANES), lambda s: (0, s)),
        out_shape=jax.ShapeDtypeStruct((lp, 2 * nh * HEAD_DIM), BF16), name="merge_gdn",
        compiler_params=_cp("parallel"))(o_gdn, proj, norm_w)


def _merge_gdn_bwd(o_gdn, proj, norm_w, dmerged, nh):
    lp = o_gdn.shape[0]

    def body(o_ref, z_ref, w_ref, dm_ref, do_ref, dz_ref, dw_ref):
        o = o_ref[...]
        r = _rms(o)
        xh = o * r
        silu, dsilu = _silu_and_grad(z_ref[...])
        dm = dm_ref[...]
        dn = dm * silu
        dz_ref[...] = (dm * (xh * w_ref[...]) * dsilu).astype(BF16)
        dnw = dn * w_ref[...]
        do_ref[...] = r * (dnw - xh * jnp.mean(dnw * xh, axis=-1, keepdims=True))

        @pl.when(pl.program_id(0) == 0)
        def _():
            dw_ref[...] = jnp.zeros_like(dw_ref)
        dw_ref[...] += jnp.sum(dn * xh, axis=0, keepdims=True)

    w = nh * HEAD_DIM
    return pl.pallas_call(
        body, grid=(nh,),
        in_specs=[pl.BlockSpec((lp, LANES), lambda s: (0, s)), pl.BlockSpec((lp, LANES), lambda s: (0, 3 * nh + s)),
                  pl.BlockSpec((1, LANES), lambda s: (0, 0)), pl.BlockSpec((lp, LANES), lambda s: (0, s))],
        out_specs=[pl.BlockSpec((lp, LANES), lambda s: (0, s)), pl.BlockSpec((lp, LANES), lambda s: (0, 3 * nh + s)),
                   pl.BlockSpec((1, LANES), lambda s: (0, 0))],
        out_shape=[jax.ShapeDtypeStruct((lp, w), F32), jax.ShapeDtypeStruct((lp, 8 * w + NARROW), BF16),
                   jax.ShapeDtypeStruct((1, LANES), F32)],
        name="merge_gdn_bwd", compiler_params=_cp("arbitrary"))(o_gdn, proj, norm_w, dmerged)


def _fox_prep(proj, qk_w, nh):
    lp = proj.shape[0]

    def body(x_ref, w_ref, o_ref):
        x = x_ref[...]
        o_ref[...] = x * _rms(x) * w_ref[0]

    return pl.pallas_call(
        body, grid=(2 * nh,),
        in_specs=[pl.BlockSpec((lp, LANES), lambda s: (0, 4 * nh + s)), pl.BlockSpec((1, 1, LANES), lambda s: (s // nh, 0, 0))],
        out_specs=pl.BlockSpec((lp, LANES), lambda s: (0, s)),
        out_shape=jax.ShapeDtypeStruct((lp, 2 * nh * HEAD_DIM), F32), name="fox_prep",
        compiler_params=_cp("parallel"))(proj, qk_w)


def _fox_prep_bwd(proj, qk_w, dq, dk, dproj, nh):
    lp = proj.shape[0]
    part = lambda p: pl.BlockSpec((lp, LANES), lambda s: (0, jnp.clip(s - p * nh, 0, nh - 1)))

    def body(x_ref, w_ref, dq_ref, dk_ref, _, dx_ref, dw_ref):
        x = x_ref[...]
        r = _rms(x)
        xh = x * r
        dy = jnp.where(pl.program_id(0) < nh, dq_ref[...], dk_ref[...])
        dyw = dy * w_ref[0]
        dx_ref[...] = (r * (dyw - xh * jnp.mean(dyw * xh, axis=-1, keepdims=True))).astype(BF16)

        @pl.when(pl.program_id(0) % nh == 0)
        def _():
            dw_ref[...] = jnp.zeros_like(dw_ref)
        dw_ref[0] += jnp.sum(dy * xh, axis=0, keepdims=True)

    strip = pl.BlockSpec((lp, LANES), lambda s: (0, 4 * nh + s))
    wsp = pl.BlockSpec((1, 1, LANES), lambda s: (s // nh, 0, 0))
    return pl.pallas_call(
        body, grid=(2 * nh,), in_specs=[strip, wsp, part(0), part(1), _ANY], out_specs=[strip, wsp],
        out_shape=[jax.ShapeDtypeStruct(dproj.shape, BF16), jax.ShapeDtypeStruct((2, 1, LANES), F32)],
        input_output_aliases={4: 0}, name="fox_prep_bwd", compiler_params=_cp("arbitrary"))(proj, qk_w, dq, dk, dproj)


def _fox_probs(q, k, gates, crow, h, i, nh, lse=None):
    kl = k.shape[0]
    lane = _iota((Q_BLOCK, LANES), 1)
    ct = jnp.sum(jnp.where(lane == 4 * nh + h, gates, 0.0), axis=1, keepdims=True)
    tq, kq = _iota((Q_BLOCK, Q_BLOCK), 0), _iota((Q_BLOCK, Q_BLOCK), 1)
    if i == 0:
        s = _bdot(q, k, "nt") * (HEAD_DIM ** -0.5) + (ct - crow)
        s = jnp.where((kq <= tq) & ((kq >= PAD_ROWS) | (tq < PAD_ROWS)), s, NEG)
    else:
        crow = jnp.where(_iota((1, kl), 1) < PAD_ROWS, -NEG, crow)
        s = _bdot(q, k, "nt") * (HEAD_DIM ** -0.5) + (ct - crow)
        s = jnp.concatenate([s[:, :kl - Q_BLOCK], jnp.where(kq <= tq, s[:, kl - Q_BLOCK:], NEG)], axis=1)
    if lse is not None:
        return jnp.exp(s - lse)
    m = jnp.max(s, axis=1, keepdims=True)
    p = jnp.exp(s - m)
    tot = jnp.sum(p, axis=1, keepdims=True)
    return p / tot, m + jnp.log(tot)


FOX_HEADS_PER_STEP = 2


def _fox_specs(lp, nh):
    hw = FOX_HEADS_PER_STEP * LANES
    return [pl.BlockSpec((Q_BLOCK, hw), lambda g, i: (i, g)),
            pl.BlockSpec((lp, hw), lambda g, i: (0, nh // FOX_HEADS_PER_STEP + g)),
            pl.BlockSpec((lp, hw), lambda g, i: (0, 6 * nh // FOX_HEADS_PER_STEP + g)),
            pl.BlockSpec((Q_BLOCK, LANES), lambda g, i: (i, 0)),
            pl.BlockSpec((LANES, lp), lambda g, i: (0, 0))]


def _fox_fwd(qkn, proj, gates, gtf, nh):
    lp = qkn.shape[0]

    def body(q_ref, k_ref, v_ref, g_ref, gt_ref, o_ref, lse_ref):
        g, i = pl.program_id(0), pl.program_id(1)
        for j in range(lp // Q_BLOCK):
            @pl.when(i == j)
            def _(j=j):
                kl = (j + 1) * Q_BLOCK
                for hh in range(FOX_HEADS_PER_STEP):
                    h = FOX_HEADS_PER_STEP * g + hh
                    sl = slice(hh * LANES, (hh + 1) * LANES)
                    p, lse = _fox_probs(q_ref[:, sl], k_ref[0:kl, sl], g_ref[...], gt_ref[pl.ds(4 * nh + h, 1), :][:, 0:kl],
                                        h, j, nh)
                    o_ref[:, sl] = _bdot(p, v_ref[0:kl, sl])
                    lse_ref[:, sl] = jnp.broadcast_to(lse, (Q_BLOCK, LANES))

    blk = pl.BlockSpec((Q_BLOCK, FOX_HEADS_PER_STEP * LANES), lambda g, i: (i, g))
    return pl.pallas_call(
        body, grid=(nh // FOX_HEADS_PER_STEP, lp // Q_BLOCK), in_specs=_fox_specs(lp, nh), out_specs=[blk, blk],
        out_shape=[jax.ShapeDtypeStruct((lp, nh * HEAD_DIM), F32)] * 2, name="fox_fwd",
        compiler_params=_cp("parallel", "parallel"))(qkn, qkn, proj, gates, gtf)


def _fox_bwd(qkn, proj, gates, gtf, lse, do, dproj, nh):
    lp = qkn.shape[0]
    nq = lp // Q_BLOCK
    w = nh * HEAD_DIM
    scale = HEAD_DIM ** -0.5

    def body(q_ref, k_ref, v_ref, g_ref, gt_ref, lse_ref, do_ref, _, dq_ref, dk_ref, dc_ref, dv_ref, dv_scr):
        g, i = pl.program_id(0), pl.program_id(1)

        @pl.when(i == 0)
        def _():
            dk_ref[...] = jnp.zeros_like(dk_ref)
            dv_scr[...] = jnp.zeros_like(dv_scr)
            dc_ref[...] = jnp.zeros_like(dc_ref)
        for j in range(nq):
            @pl.when(i == j)
            def _(j=j):
                kl = (j + 1) * Q_BLOCK
                for hh in range(FOX_HEADS_PER_STEP):
                    h = FOX_HEADS_PER_STEP * g + hh
                    sl = slice(hh * LANES, (hh + 1) * LANES)
                    q, k = q_ref[:, sl], k_ref[0:kl, sl]
                    p = _fox_probs(q, k, g_ref[...], gt_ref[pl.ds(4 * nh + h, 1), :][:, 0:kl], h, j, nh,
                                   lse_ref[:, sl][:, 0:1])
                    dout = do_ref[:, sl]
                    dp = _bdot(dout, v_ref[0:kl, sl], "nt")
                    ds = p * (dp - jnp.sum(p * dp, axis=1, keepdims=True))
                    dq_ref[:, sl] = _bdot(ds, k) * scale
                    dk_ref[0:kl, sl] += _bdot(ds, q, "tn") * scale
                    dv_scr[0:kl, sl] += _bdot(p, dout, "tn")
                    dc_ref[hh, :, 0:kl] -= jnp.sum(ds, axis=0, keepdims=True)

        @pl.when(i == nq - 1)
        def _():
            dv_ref[...] = dv_scr[...].astype(BF16)

    hw = FOX_HEADS_PER_STEP * LANES
    blk = pl.BlockSpec((Q_BLOCK, hw), lambda g, i: (i, g))
    col = pl.BlockSpec((lp, hw), lambda g, i: (0, g))
    return pl.pallas_call(
        body, grid=(nh // FOX_HEADS_PER_STEP, nq), in_specs=_fox_specs(lp, nh) + [blk, blk, _ANY],
        out_specs=[blk, col, pl.BlockSpec((FOX_HEADS_PER_STEP, 1, lp), lambda g, i: (g, 0, 0)),
                   pl.BlockSpec((lp, hw), lambda g, i: (0, 6 * nh // FOX_HEADS_PER_STEP + g))],
        out_shape=[jax.ShapeDtypeStruct((lp, w), F32)] * 2 + [jax.ShapeDtypeStruct((nh, 1, lp), F32),
                                                             jax.ShapeDtypeStruct(dproj.shape, BF16)],
        scratch_shapes=[pltpu.VMEM((lp, hw), F32)], input_output_aliases={7: 3},
        name="fox_bwd", compiler_params=_cp("parallel", "arbitrary"))(qkn, qkn, proj, gates, gtf, lse, do, dproj)


def _merge_fox(o_fox, proj, merged, nh):
    lp = o_fox.shape[0]

    def body(o_ref, z_ref, _, m_ref):
        z = z_ref[...]
        m_ref[...] = (o_ref[...] * (z * _sigmoid(z))).astype(BF16)

    return pl.pallas_call(
        body, grid=(nh,),
        in_specs=[pl.BlockSpec((lp, LANES), lambda s: (0, s)), pl.BlockSpec((lp, LANES), lambda s: (0, 7 * nh + s)), _ANY],
        out_specs=pl.BlockSpec((lp, LANES), lambda s: (0, nh + s)),
        out_shape=jax.ShapeDtypeStruct(merged.shape, BF16), input_output_aliases={2: 0}, name="merge_fox",
        compiler_params=_cp("parallel"))(o_fox, proj, merged)


def _merge_fox_bwd(o_fox, proj, dmerged, dproj, nh):
    lp = o_fox.shape[0]

    def body(o_ref, z_ref, dm_ref, _, do_ref, dz_ref):
        silu, dsilu = _silu_and_grad(z_ref[...])
        dm = dm_ref[...]
        do_ref[...] = dm * silu
        dz_ref[...] = (dm * o_ref[...] * dsilu).astype(BF16)

    w = nh * HEAD_DIM
    return pl.pallas_call(
        body, grid=(nh,),
        in_specs=[pl.BlockSpec((lp, LANES), lambda s: (0, s)), pl.BlockSpec((lp, LANES), lambda s: (0, 7 * nh + s)),
                  pl.BlockSpec((lp, LANES), lambda s: (0, nh + s)), _ANY],
        out_specs=[pl.BlockSpec((lp, LANES), lambda s: (0, s)), pl.BlockSpec((lp, LANES), lambda s: (0, 7 * nh + s))],
        out_shape=[jax.ShapeDtypeStruct((lp, w), F32), jax.ShapeDtypeStruct(dproj.shape, BF16)],
        input_output_aliases={3: 1}, name="merge_fox_bwd", compiler_params=_cp("parallel"))(o_fox, proj, dmerged, dproj)


def _post(out, x, target, post_w):
    lp, d = out.shape

    def body(o_ref, x_ref, t_ref, w_ref, dy_ref, do_ref, loss_ref, dw_ref):
        i = pl.program_id(0)

        @pl.when(i == 0)
        def _():
            loss_ref[...] = jnp.zeros_like(loss_ref)
            dw_ref[...] = jnp.zeros_like(dw_ref)
        o = o_ref[...]
        r = _rms(o)
        nrm = o * r
        err = jnp.where(i > 0, x_ref[...] + nrm * w_ref[...] - t_ref[...], 0.0)
        loss_ref[0:1, :] += 0.5 * jnp.sum(jnp.sum(err * err, axis=1, keepdims=True), axis=0, keepdims=True) / d
        dy = err / d
        dy_ref[...] = dy
        dw_ref[...] += jnp.sum(dy * nrm, axis=0, keepdims=True)
        dyw = dy * w_ref[...]
        do_ref[...] = (r * (dyw - nrm * jnp.mean(dyw * nrm, axis=-1, keepdims=True))).astype(BF16)

    row = pl.BlockSpec((Q_BLOCK, d), lambda i: (i, 0))
    vec = pl.BlockSpec((1, d), lambda i: (0, 0))
    return pl.pallas_call(
        body, grid=(lp // Q_BLOCK,), in_specs=[row, _x_rows(d), _x_rows(d), vec],
        out_specs=[_x_rows(d), row, pl.BlockSpec((8, LANES), lambda i: (0, 0)), vec],
        out_shape=[jax.ShapeDtypeStruct(x.shape, F32), jax.ShapeDtypeStruct((lp, d), BF16),
                   jax.ShapeDtypeStruct((8, LANES), F32), jax.ShapeDtypeStruct((1, d), F32)],
        name="post", compiler_params=_cp("arbitrary"))(out, x, target, post_w)


def _prenorm_bwd(dxn, x, meta, w, dy, rider=None):
    seq, d = x.shape
    lp = seq + Q_BLOCK

    def body(start, finish, dx_ref, x_ref, m_ref, w_ref, dy_ref, gx_ref, gm_ref, dw_ref):
        i = pl.program_id(0)
        pl.when(i == 0)(start)
        h = _h_tile(i, x_ref, m_ref)
        r = _rms(h)
        xh = h * r
        dxn_ = dx_ref[...]
        dxw = dxn_ * w_ref[...]
        dh = jnp.where(i > 0, dy_ref[...], 0.0) + r * (dxw - xh * jnp.mean(dxw * xh, axis=-1, keepdims=True))
        gx_ref[...] = dh

        @pl.when(i == 0)
        def _():
            dw_ref[...] = jnp.zeros_like(dw_ref)
            gm_ref[...] = dh[PAD_ROWS:, :]
        dw_ref[...] += jnp.sum(dxn_ * xh, axis=0, keepdims=True)
        pl.when(i == lp // Q_BLOCK - 1)(finish)

    vec = pl.BlockSpec((1, d), lambda i: (0, 0))
    met = pl.BlockSpec((N_META, d), lambda i: (0, 0))
    outs, got = _hosted_call(
        body, [rider], 5, 3, 0, grid=(lp // Q_BLOCK,),
        in_specs=[pl.BlockSpec((Q_BLOCK, d), lambda i: (i, 0)), _x_rows(d), met, vec, _x_rows(d)],
        out_specs=[_x_rows(d), met, vec],
        out_shape=[jax.ShapeDtypeStruct((seq, d), F32), jax.ShapeDtypeStruct((N_META, d), F32),
                   jax.ShapeDtypeStruct((1, d), F32)],
        name="prenorm_bwd", compiler_params=_cp("arbitrary"))(dxn, x, meta, w, dy)
    return outs, (got[0] if got else None)


def _layer_grads(x, target, meta, pre_w, wfull, conv_wt, a_log, dt_bias, gdn_norm_w, fq_w, fk_w, f_bias, w_out, post_w,
                 late_weights=None, w_out_grads=None):
    nh = a_log.shape[1]
    zpad = jnp.zeros((1, LANES - 3 * nh), F32)
    bias_row = jnp.concatenate([jnp.zeros((1, nh), F32), dt_bias, f_bias, zpad], axis=1)
    nega_row = jnp.concatenate([jnp.zeros((1, nh), F32), -jnp.exp(a_log), jnp.zeros((1, nh), F32), zpad], axis=1)
    qk_w = jnp.stack([fq_w, fk_w])

    xn = _prenorm(x, meta, pre_w)
    if late_weights is None:
        proj = _matmul(xn, wfull, "nn", MM_TILE, F32, "proj")
    else:
        proj, got = _matmul(xn, wfull, "nn", MM_TILE, F32, "proj", late_weights[0])
        conv_wt, w_out = late_weights[1](got)
    qkv = _gdn_prep(proj, conv_wt, nh)
    gates, gt3, gtf = _gates(proj, bias_row, nega_row, nh)
    o_gdn, s_all, t_all = _gdn_fwd(qkv, gates, gt3, nh)
    qkn = _fox_prep(proj, qk_w, nh)
    o_fox, fox_lse = _fox_fwd(qkn, proj, gates, gtf, nh)
    merged = _merge_fox(o_fox, proj, _merge_gdn(o_gdn, proj, gdn_norm_w, nh), nh)
    out = _matmul(merged, w_out, "nn", 4 * LANES, F32, "out_proj")
    dy, dout, loss_blk, dpost_w = _post(out, x, target, post_w)

    dw_out = _matmul(merged, dout, "tn", 4 * LANES, BF16, "dw_out")
    if w_out_grads is None:
        dmerged, gdn_rider = _matmul(dout, w_out, "nt", 4 * LANES, F32, "dmerged"), None
    else:
        dmerged, got = _matmul(dout, w_out, "nt", 4 * LANES, F32, "dmerged", w_out_grads[0](dw_out))
        gdn_rider = w_out_grads[1](dw_out, got)
    do_gdn, dproj, dgdn_norm_w = _merge_gdn_bwd(o_gdn, proj, gdn_norm_w, dmerged, nh)
    do_fox, dproj = _merge_fox_bwd(o_fox, proj, dmerged, dproj, nh)
    dqn, dkn, dc_t, dproj = _fox_bwd(qkn, proj, gates, gtf, fox_lse, do_fox, dproj, nh)
    dproj, dqk_w = _fox_prep_bwd(proj, qk_w, dqn, dkn, dproj, nh)
    (dgq, dgk, dgv, dgate), w_out_parts = _gdn_bwd(qkv, gates, gt3, s_all, t_all, do_gdn, nh, gdn_rider)
    dproj, dconv_wt = _gdn_prep_bwd(proj, conv_wt, dgq, dgk, dgv, dproj, nh)
    dc_rows = jnp.pad(dc_t.reshape(nh, -1), ((2 * nh, LANES - 3 * nh), (0, 0)))
    dproj, gate_sums = _gates_bwd(proj, bias_row, nega_row, gates, dgate, dc_rows, dproj, nh)
    return dict(
        loss=loss_blk[0:1, 0:1], dy=dy, xn=xn, dproj=dproj, post_w=dpost_w,
        conv_wt=dconv_wt, a_log=gate_sums[1:2, nh:2 * nh], dt_bias=gate_sums[0:1, nh:2 * nh],
        gdn_norm_w=dgdn_norm_w, fq_w=dqk_w[0], fk_w=dqk_w[1], f_bias=gate_sums[0:1, 2 * nh:3 * nh], w_out=dw_out,
        w_out_parts=w_out_parts)


def _cast_bf16(a, tr, name):
    r, c = a.shape

    def body(a_ref, o_ref):
        o_ref[...] = a_ref[...].astype(BF16)

    return pl.pallas_call(
        body, grid=(r // tr,), in_specs=[pl.BlockSpec((tr, c), lambda i: (i, 0))],
        out_specs=pl.BlockSpec((tr, c), lambda i: (i, 0)), out_shape=jax.ShapeDtypeStruct((r, c), BF16),
        name=name, compiler_params=_cp("parallel"))(a)


def _column_major(a):
    return jnp.transpose(a, (2, 0, 1))


def _cast_bf16_column_major(a3, name):
    _, r, c = a3.shape

    def body(a_ref, o_ref):
        o_ref[...] = a_ref[...].reshape(LANES, r).T.astype(BF16)

    return pl.pallas_call(
        body, grid=(pl.cdiv(c, LANES),), in_specs=[pl.BlockSpec((LANES, 1, r), lambda i: (i, 0, 0))],
        out_specs=pl.BlockSpec((r, LANES), lambda i: (0, i)), out_shape=jax.ShapeDtypeStruct((r, c), BF16),
        name=name, compiler_params=_cp("parallel"))(_column_major(a3))


def _adamw_column_major(w3, parts, m3, v3, name):
    _, r, c = w3.shape
    n_parts = parts.shape[0]

    def body(w_ref, p_ref, m_ref, v_ref, g_ref, d_ref, nm_ref, nv_ref):
        g = p_ref[0].astype(F32)
        for s in range(1, n_parts):
            g = g + p_ref[s].astype(F32)
        g = g.T
        flat = lambda ref: ref[...].reshape(LANES, r)
        m_new = ADAM_B1 * flat(m_ref) + (1.0 - ADAM_B1) * g
        v_new = ADAM_B2 * flat(v_ref) + (1.0 - ADAM_B2) * (g * g)
        m_hat = m_new / (1.0 - ADAM_B1 ** ADAM_STEP)
        v_hat = v_new / (1.0 - ADAM_B2 ** ADAM_STEP)
        delta = -ADAM_LR * (m_hat / (jnp.sqrt(v_hat) + ADAM_EPS) + ADAM_WD * flat(w_ref))
        for ref, val in ((g_ref, g), (d_ref, delta), (nm_ref, m_new), (nv_ref, v_new)):
            ref[...] = val.reshape(LANES, 1, r)

    blk = pl.BlockSpec((LANES, 1, r), lambda i: (i, 0, 0))
    outs = pl.pallas_call(
        body, grid=(pl.cdiv(c, LANES),), in_specs=[blk, pl.BlockSpec((n_parts, r, LANES), lambda i: (0, 0, i)), blk, blk],
        out_specs=[blk] * 4, out_shape=[jax.ShapeDtypeStruct((c, 1, r), F32)] * 4, name=name,
        compiler_params=_cp("parallel"))(_column_major(w3), parts, _column_major(m3), _column_major(v3))
    return [jnp.transpose(o, (1, 2, 0)) for o in outs]


def _gather_copies(ins, outs, send_sems, recv_sems, local_sems):
    n = len(ins)
    x, y, c = lax.axis_index("x"), lax.axis_index("y"), lax.axis_index("c")
    me, sibling = (x, y, c), (x, y, 1 - c)
    xn, yn, dg = (1 - x, y), (x, 1 - y), (1 - x, 1 - y)

    def copy(a, k, block, to, src=None):
        px, py, pc = block
        rows = outs[a].at[4 * px + 2 * py + pc]
        return pltpu.make_async_remote_copy(
            src_ref=rows if src is None else src, dst_ref=rows, send_sem=send_sems.at[a, k],
            recv_sem=recv_sems.at[a, k], device_id=to, device_id_type=_MESH)

    local = [pltpu.make_async_copy(ins[a], outs[a].at[4 * x + 2 * y + c], local_sems.at[a]) for a in range(n)]
    own = [cp for a in range(n) for cp in (copy(a, 0, me, sibling, src=ins[a]), copy(a, 1, me, (*xn, c), src=ins[a]),
                                           copy(a, 2, me, (*yn, c), src=ins[a]))]

    def start():
        for cp in local + own:
            cp.start()

    def finish():
        for a in range(n):
            @pl.when(c == 1)
            def _(a=a):
                copy(a, 1, (*xn, c), me).wait_recv()
                copy(a, 3, (*xn, c), (*yn, c)).start()

            @pl.when(c == 0)
            def _(a=a):
                copy(a, 2, (*yn, c), me).wait_recv()
                copy(a, 3, (*yn, c), (*xn, c)).start()
        for a in range(n):
            pl.when(c == 0)(copy(a, 1, (*xn, c), me).wait_recv)
            copy(a, 4, (*xn, c), sibling).start()
            pl.when(c == 1)(copy(a, 2, (*yn, c), me).wait_recv)
            copy(a, 5, (*yn, c), sibling).start()
        for a in range(n):
            copy(a, 3, (*dg, c), me).wait_recv()
            copy(a, 6, (*dg, c), sibling).start()
        for a in range(n):
            copy(a, 0, sibling, me).wait_recv()
            for k, chip in ((4, xn), (5, yn), (6, dg)):
                copy(a, k, (*chip, 1 - c), me).wait_recv()
                copy(a, k, (*chip, c), sibling).wait_send()
            copy(a, 3, (*xn, c), (*yn, c)).wait_send()
        for cp in own:
            cp.wait_send()
        for cp in local:
            cp.wait()

    return start, finish


def _gather_scratch(n):
    return [pltpu.SemaphoreType.DMA((n, N_DEV - 1)), pltpu.SemaphoreType.DMA((n, N_DEV - 1)), pltpu.SemaphoreType.DMA((n,))]


def _gather_rider(arrays):
    return _Rider(list(arrays), [jax.ShapeDtypeStruct((N_DEV,) + a.shape, a.dtype) for a in arrays],
                  _gather_scratch(len(arrays)), {}, lambda ins, outs, scratch: _gather_copies(ins, outs, *scratch))


def _all_gather(arrays, name):
    n = len(arrays)

    def body(*refs):
        start, finish = _gather_copies(refs[:n], refs[n:2 * n], *refs[2 * n:])
        start()
        finish()

    return pl.pallas_call(
        body, in_specs=[_ANY] * n, out_specs=[_ANY] * n,
        out_shape=[jax.ShapeDtypeStruct((N_DEV,) + a.shape, a.dtype) for a in arrays],
        scratch_shapes=_gather_scratch(n), name=name)(*arrays)


SLAB = 10 * LANES


def _slab_start(blk, nh, cols):
    in_second_half = blk >= N_DEV // 2
    shift = (2 * nh if in_second_half else 0) if isinstance(blk, int) else jnp.where(in_second_half, 2 * nh, 0)
    return (blk * cols - shift) // LANES * LANES


def _pair_rider(dw_rows=None, parts=None, nh=None):
    if dw_rows is not None:
        r, full = dw_rows.shape
        cols = (full - NARROW + 3 * nh) // N_DEV
        out_shapes = [jax.ShapeDtypeStruct((N_CHIP, r, SLAB), dw_rows.dtype), jax.ShapeDtypeStruct((r, NARROW), dw_rows.dtype)]
    else:
        out_shapes = [jax.ShapeDtypeStruct((N_CHIP,) + parts.shape[1:], parts.dtype)]

    def make(ins, outs, scratch):
        send_sems, recv_sems = scratch
        x, y, c = lax.axis_index("x"), lax.axis_index("y"), lax.axis_index("c")
        kw = lambda k: dict(send_sem=send_sems.at[k], recv_sem=recv_sems.at[k], device_id=(x, y, 1 - c), device_id_type=_MESH)
        copies = []
        for q in range(N_CHIP):
            if dw_rows is not None:
                first = pl.multiple_of(_slab_start(2 * q + 1 - c, nh, cols), LANES)
                copies.append(pltpu.make_async_remote_copy(src_ref=ins[0].at[:, pl.ds(first, SLAB)], dst_ref=outs[0].at[q], **kw(q)))
            else:
                copies.append(pltpu.make_async_remote_copy(src_ref=ins[0].at[2 * q + 1 - c], dst_ref=outs[0].at[q], **kw(q)))
        if dw_rows is not None:
            copies.append(pltpu.make_async_remote_copy(src_ref=ins[0].at[:, pl.ds(full - NARROW, NARROW)], dst_ref=outs[1],
                                                       **kw(N_CHIP)))

        def start():
            for cp in copies:
                cp.start()

        def finish():
            for cp in copies:
                cp.wait()

        return start, finish

    return _Rider([dw_rows if dw_rows is not None else parts], out_shapes,
                  [pltpu.SemaphoreType.DMA((N_CHIP + 1,)), pltpu.SemaphoreType.DMA((N_CHIP + 1,))], {}, make)


def _relayout_pair_sum(dwfull, got_slabs, got_tail, core, nh, tr, name):
    d, full = dwfull.shape
    w = nh * HEAD_DIM
    cols = (8 * w + 3 * nh) // N_DEV
    segs = _native_segments(nh)

    def block(f_ref, s_ref, t_ref, q, blk):
        st = _slab_start(blk, nh, cols)
        wide = f_ref[:, st:st + SLAB].astype(F32) + s_ref[q].astype(F32)
        tail = f_ref[:, 8 * w:].astype(F32) + t_ref[...].astype(F32)
        pieces = []
        for s0, s1, t0 in segs:
            lo, hi = max(s0, blk * cols), min(s1, (blk + 1) * cols)
            if lo < hi:
                at = t0 + lo - s0
                pieces.append(tail[:, at - 8 * w:at - 8 * w + hi - lo] if at >= 8 * w else wide[:, at - st:at - st + hi - lo])
        return (pieces[0] if len(pieces) == 1 else jnp.concatenate(pieces, axis=1)).astype(dwfull.dtype)

    def body(core_ref, f_ref, s_ref, t_ref, o_ref):
        for parity in range(2):
            @pl.when(core_ref[0] == parity)
            def _(parity=parity):
                for q in range(N_CHIP):
                    o_ref[q] = block(f_ref, s_ref, t_ref, q, 2 * q + parity)

    return pl.pallas_call(
        body,
        grid_spec=pltpu.PrefetchScalarGridSpec(
            num_scalar_prefetch=1, grid=(d // tr,),
            in_specs=[pl.BlockSpec((tr, full), lambda i, c_ref: (i, 0)), pl.BlockSpec((N_CHIP, tr, SLAB), lambda i, c_ref: (0, i, 0)),
                      pl.BlockSpec((tr, NARROW), lambda i, c_ref: (i, 0))],
            out_specs=pl.BlockSpec((N_CHIP, tr, cols), lambda i, c_ref: (0, i, 0))),
        out_shape=jax.ShapeDtypeStruct((N_CHIP, d, cols), dwfull.dtype), name=name,
        compiler_params=_cp("parallel"))(core, dwfull, got_slabs, got_tail)


def _pair_sum(parts, got, core, tr, name):
    _, r, c = parts.shape

    def body(core_ref, p_ref, g_ref, o_ref):
        o_ref[...] = (p_ref[...].astype(F32) + g_ref[...].astype(F32)).astype(o_ref.dtype)

    return pl.pallas_call(
        body,
        grid_spec=pltpu.PrefetchScalarGridSpec(
            num_scalar_prefetch=1, grid=(N_CHIP, r // tr),
            in_specs=[pl.BlockSpec((1, tr, c), lambda q, i, core_ref: (2 * q + core_ref[0], i, 0)),
                      pl.BlockSpec((1, tr, c), lambda q, i, core_ref: (q, i, 0))],
            out_specs=pl.BlockSpec((1, tr, c), lambda q, i, core_ref: (q, i, 0))),
        out_shape=jax.ShapeDtypeStruct((N_CHIP, r, c), parts.dtype), name=name,
        compiler_params=_cp("parallel", "parallel"))(core, parts, got)


def _native_segments(nh):
    w = nh * HEAD_DIM
    return [(0, 4 * w, 0), (4 * w, 4 * w + 2 * nh, 8 * w), (4 * w + 2 * nh, 8 * w + 2 * nh, 4 * w),
            (8 * w + 2 * nh, 8 * w + 3 * nh, 8 * w + 2 * nh)]


def _relayout_w_in(wg, nh, tr):
    _, d, cols = wg.shape
    w = nh * HEAD_DIM

    def native(ref, j0, j1):
        out = []
        while j0 < j1:
            blk = j0 // cols
            end = min(j1, (blk + 1) * cols)
            out.append(ref[blk, :, pl.ds(j0 - blk * cols, end - j0)])
            j0 = end
        return out

    def body(g_ref, o_ref):
        for cidx in range(8 * w // LANES):
            j0 = cidx * LANES + (0 if cidx * LANES < 4 * w else 2 * nh)
            pieces = native(g_ref, j0, j0 + LANES)
            o_ref[:, cidx * LANES:(cidx + 1) * LANES] = pieces[0] if len(pieces) == 1 else jnp.concatenate(pieces, axis=1)
        pieces = (native(g_ref, 4 * w, 4 * w + 2 * nh) + native(g_ref, 8 * w + 2 * nh, 8 * w + 3 * nh)
                  + [jnp.zeros((tr, NARROW - 3 * nh), wg.dtype)])
        o_ref[:, 8 * w:] = jnp.concatenate(pieces, axis=1)

    return pl.pallas_call(
        body, grid=(d // tr,), in_specs=[pl.BlockSpec((N_DEV, tr, cols), lambda i: (0, i, 0))],
        out_specs=pl.BlockSpec((tr, 8 * w + NARROW), lambda i: (i, 0)),
        out_shape=jax.ShapeDtypeStruct((d, 8 * w + NARROW), wg.dtype),
        name="relayout_w_in", compiler_params=_cp("parallel"))(wg)


def _adamw(w, parts, m, v, tr, name):
    r, c = w.shape
    n_parts = parts.shape[0]

    def body(w_ref, p_ref, m_ref, v_ref, g_ref, d_ref, nm_ref, nv_ref):
        g = p_ref[0].astype(F32)
        for s in range(1, n_parts):
            g = g + p_ref[s].astype(F32)
        m_new = ADAM_B1 * m_ref[...] + (1.0 - ADAM_B1) * g
        v_new = ADAM_B2 * v_ref[...] + (1.0 - ADAM_B2) * (g * g)
        m_hat = m_new / (1.0 - ADAM_B1 ** ADAM_STEP)
        v_hat = v_new / (1.0 - ADAM_B2 ** ADAM_STEP)
        g_ref[...] = g
        d_ref[...] = -ADAM_LR * (m_hat / (jnp.sqrt(v_hat) + ADAM_EPS) + ADAM_WD * w_ref[...])
        nm_ref[...] = m_new
        nv_ref[...] = v_new

    blk = pl.BlockSpec((tr, c), lambda i: (i, 0))
    return pl.pallas_call(
        body, grid=(r // tr,), in_specs=[blk, pl.BlockSpec((n_parts, tr, c), lambda i: (0, i, 0)), blk, blk],
        out_specs=[blk] * 4, out_shape=[jax.ShapeDtypeStruct((r, c), F32)] * 4, name=name,
        compiler_params=_cp("parallel"))(w, parts, m, v)


def _pack_small(d, pre, post, a_log, dt_bias, f_bias, gdn_w, fq_w, fk_w, extra):
    row2 = jnp.concatenate([a_log, dt_bias, f_bias, gdn_w, fq_w, fk_w, extra], axis=1)
    row2 = jnp.pad(row2, ((0, 0), (0, d - row2.shape[1])))
    return jnp.concatenate([pre, post, row2, jnp.zeros((5, d), F32)], axis=0)


def _unpack_small(p, nh):
    o = 3 * nh
    return dict(pre=p[0:1], post=p[1:2], a_log=p[2:3, 0:nh], dt_bias=p[2:3, nh:2 * nh], f_bias=p[2:3, 2 * nh:o],
                gdn_w=p[2:3, o:o + HEAD_DIM], fq_w=p[2:3, o + HEAD_DIM:o + 2 * HEAD_DIM],
                fk_w=p[2:3, o + 2 * HEAD_DIM:o + 3 * HEAD_DIM], extra=p[2, o + 3 * HEAD_DIM])


def kernel(x, meta_tokens, pre_norm_w, w_in, conv_w, a_log, dt_bias, gdn_norm_w, fox_q_norm_w, fox_k_norm_w, fox_f_bias, w_out, post_norm_w, loss_target, m_meta_tokens, m_pre_norm_w, m_w_in, m_conv_w, m_a_log, m_dt_bias, m_gdn_norm_w, m_fox_q_norm_w, m_fox_k_norm_w, m_fox_f_bias, m_w_out, m_post_norm_w, v_meta_tokens, v_pre_norm_w, v_w_in, v_conv_w, v_a_log, v_dt_bias, v_gdn_norm_w, v_fox_q_norm_w, v_fox_k_norm_w, v_fox_f_bias, v_w_out, v_post_norm_w):
    nh = a_log.shape[1]
    d = x.shape[-1]
    w = nh * HEAD_DIM
    zero = jnp.zeros((1, 1), F32)

    wg, mg = _all_gather([_cast_bf16_column_major(w_in, "cast_w_in"), meta_tokens], "gather_weights")
    wfull = _relayout_w_in(wg, nh, 256)
    meta_full = mg.transpose(1, 0, 2).reshape(N_META, d)
    late_weights = (_gather_rider([conv_w[0].T, _cast_bf16(w_out[0], 256, "cast_w_out")]),
                    lambda got: (got[0].transpose(1, 0, 2).reshape(CONV_WIDTH, 3 * w), got[1].reshape(2 * w, d)))
    core = lax.axis_index("c")
    dev = 4 * lax.axis_index("x") + 2 * lax.axis_index("y") + core
    core_arr = jnp.reshape(core, (1,)).astype(jnp.int32)

    out_parts = lambda dw_out: dw_out.reshape(N_DEV, 2 * w // N_DEV, d)
    g = _layer_grads(
        x[0], loss_target[0], meta_full, pre_norm_w, wfull, None, a_log, dt_bias, gdn_norm_w,
        fox_q_norm_w, fox_k_norm_w, fox_f_bias, None, post_norm_w, late_weights=late_weights,
        w_out_grads=(lambda dw_out: _pair_rider(parts=out_parts(dw_out)),
                     lambda dw_out, got: _chip_rider(_pair_sum(out_parts(dw_out), got[0], core_arr, 256, "pair_sum_w_out"))))
    p_out = g["w_out_parts"][0]
    xn, dproj, half = g["xn"], g["dproj"], d // 2
    dw_a = _matmul(xn, dproj, "tn", MM_TILE, BF16, "dw_in_a", a_cols=(half, 0))
    dw_b, got_a = _matmul(xn, dproj, "tn", MM_TILE, BF16, "dw_in_b", _pair_rider(dw_rows=dw_a, nh=nh), a_cols=(half, 1))
    sums_a = _relayout_pair_sum(dw_a, got_a[0], got_a[1], core_arr, nh, 128, "relayout_pair_sum_a")
    dxn, (p_in_a, got_b) = _dxn(dproj, wfull, [_chip_rider(sums_a, rows_total=d), _pair_rider(dw_rows=dw_b, nh=nh)],
                                MM_TILE, "dxn")
    sums_b = _relayout_pair_sum(dw_b, got_b[0], got_b[1], core_arr, nh, 128, "relayout_pair_sum_b")
    (grad_x, dmeta, dpre_w), p_in = _prenorm_bwd(dxn, x[0], meta_full, pre_norm_w, g["dy"],
                                                 _chip_rider(sums_b, rows_total=d, row0=half, into=p_in_a[0]))
    p_in = p_in[0]
    small = _pack_small(d, dpre_w, g["post_w"], g["a_log"], g["dt_bias"], g["f_bias"], g["gdn_norm_w"], g["fq_w"],
                        g["fk_w"], g["loss"])
    a_conv, a_meta, p_small = _all_gather([g["conv_wt"], dmeta, small], "gather_small_grads")
    p_conv = lax.dynamic_slice_in_dim(a_conv, dev * conv_w.shape[1], conv_w.shape[1], axis=2).transpose(0, 2, 1)
    p_meta = lax.dynamic_slice_in_dim(a_meta, dev * meta_tokens.shape[1], meta_tokens.shape[1], axis=2)

    r_in = _adamw_column_major(w_in, p_in, m_w_in, v_w_in, "adamw_w_in")
    r_out = _adamw(w_out[0], p_out, m_w_out[0], v_w_out[0], 64, "adamw_w_out")
    r_conv = _adamw(conv_w[0], p_conv, m_conv_w[0], v_conv_w[0], conv_w.shape[1], "adamw_conv_w")
    r_meta = _adamw(meta_tokens, p_meta, m_meta_tokens, v_meta_tokens, N_META, "adamw_meta")
    pk = lambda pre, post, a, dt, gw, fq, fk, fb: _pack_small(d, pre, post, a, dt, fb, gw, fq, fk, zero)
    r_small = _adamw(
        pk(pre_norm_w, post_norm_w, a_log, dt_bias, gdn_norm_w, fox_q_norm_w, fox_k_norm_w, fox_f_bias), p_small,
        pk(m_pre_norm_w, m_post_norm_w, m_a_log, m_dt_bias, m_gdn_norm_w, m_fox_q_norm_w, m_fox_k_norm_w, m_fox_f_bias),
        pk(v_pre_norm_w, v_post_norm_w, v_a_log, v_dt_bias, v_gdn_norm_w, v_fox_q_norm_w, v_fox_k_norm_w, v_fox_f_bias),
        8, "adamw_small")

    sm = [_unpack_small(r, nh) for r in r_small]
    outs = []
    for i in range(4):
        s = sm[i]
        outs += [r_meta[i], s["pre"], r_in[i], r_conv[i][None], s["a_log"], s["dt_bias"], s["gdn_w"], s["fq_w"],
                 s["fk_w"], s["f_bias"], r_out[i][None], s["post"]]
    return (sm[0]["extra"], grad_x[None], *outs)
```

```python
import jax
import jax.numpy as jnp
from jax import lax
from jax.experimental import pallas as pl
from jax.experimental.pallas import tpu as pltpu

F32, BF16 = jnp.float32, jnp.bfloat16
HEAD_DIM = 128
N_META = 16
CONV_WIDTH = 4
CHUNK = 128
Q_BLOCK = 128
LANES = 128
EPS = 1e-6
PAD_ROWS = Q_BLOCK - N_META
N_DEV = 8
N_CHIP = 4
VMEM_LIMIT = 56 * 1024 * 1024
NEG = -1e30
NARROW = 2 * LANES
MM_TILE = 6 * LANES

ADAM_LR, ADAM_B1, ADAM_B2, ADAM_EPS, ADAM_WD, ADAM_STEP = 0.001, 0.9, 0.999, 1e-08, 0.01, 10

_DN = {"nn": (((1,), (0,)), ((), ())), "nt": (((1,), (1,)), ((), ())), "tn": (((0,), (0,)), ((), ()))}
_DN3 = {"nn": (((2,), (1,)), ((0,), (0,))), "nt": (((2,), (2,)), ((0,), (0,))), "tn": (((1,), (1,)), ((0,), (0,)))}
_ANY = pl.BlockSpec(memory_space=pl.ANY)
_MESH = pl.DeviceIdType.MESH


def _cp(*sem):
    return pltpu.CompilerParams(dimension_semantics=sem, vmem_limit_bytes=VMEM_LIMIT)


def _dot(a, b, dims="nn", prec=None):
    return lax.dot_general(a, b, _DN[dims], precision=prec, preferred_element_type=F32)


def _bdot(a, b, dims="nn"):
    return _dot(a.astype(BF16), b.astype(BF16), dims)


def _hdot(a, b, dims="nn"):
    return _dot(a, b, dims, prec=lax.Precision.HIGHEST)


def _dot3(a, b, dims="nn"):
    return lax.dot_general(a, b, _DN3[dims], preferred_element_type=F32)


def _bdot3(a, b, dims="nn"):
    return _dot3(a.astype(BF16), b.astype(BF16), dims)


def _split(a):
    hi = a.astype(BF16)
    return hi, (a - hi.astype(F32)).astype(BF16)


def _iota(shape, dim):
    return lax.broadcasted_iota(jnp.int32, shape, dim)


def _sigmoid(z):
    return 1.0 / (1.0 + jnp.exp(-z))


def _softplus(z):
    e = jnp.exp(-jnp.abs(z))
    u = 1.0 + e
    l1p = jnp.where(u == 1.0, e, jnp.log(u) * (e / jnp.where(u == 1.0, 1.0, u - 1.0)))
    return jnp.maximum(z, 0.0) + l1p


def _silu_and_grad(z):
    s = _sigmoid(z)
    return z * s, s * (1.0 + z * (1.0 - s))


def _rms(x):
    return lax.rsqrt(jnp.mean(x * x, axis=-1, keepdims=True) + EPS)


def _h_tile(i, x_ref, meta_ref):
    first = jnp.concatenate([jnp.zeros((PAD_ROWS, x_ref.shape[1]), F32), meta_ref[...]], axis=0)
    return jnp.where(i == 0, first, x_ref[...])


def _x_rows(d):
    return pl.BlockSpec((Q_BLOCK, d), lambda i: (jnp.maximum(i - 1, 0), 0))


def _prenorm(x, meta, w):
    seq, d = x.shape
    lp = seq + Q_BLOCK

    def body(x_ref, m_ref, w_ref, o_ref):
        h = _h_tile(pl.program_id(0), x_ref, m_ref)
        o_ref[...] = (h * _rms(h) * w_ref[...]).astype(BF16)

    return pl.pallas_call(
        body, grid=(lp // Q_BLOCK,),
        in_specs=[_x_rows(d), pl.BlockSpec((N_META, d), lambda i: (0, 0)), pl.BlockSpec((1, d), lambda i: (0, 0))],
        out_specs=pl.BlockSpec((Q_BLOCK, d), lambda i: (i, 0)),
        out_shape=jax.ShapeDtypeStruct((lp, d), BF16), name="prenorm", compiler_params=_cp("parallel"))(x, meta, w)


def _tile(n, want):
    return max(t for t in range(LANES, want + 1, LANES) if n % t == 0)


class _Rider:
    def __init__(self, inputs, out_shapes, scratch, aliases, make):
        self.inputs, self.out_shapes, self.scratch, self.aliases, self.make = inputs, out_shapes, scratch, aliases, make


def _hosted_call(body, riders, n_in, n_out, n_scratch, *, in_specs, out_specs, out_shape, scratch_shapes=(), aliases=None,
                 **kw):
    riders = [r for r in riders if r is not None]
    r_in = [len(r.inputs) for r in riders]
    r_out = [len(r.out_shapes) for r in riders]
    r_scr = [len(r.scratch) for r in riders]
    al = dict(aliases or {})
    for k, r in enumerate(riders):
        al.update({n_in + sum(r_in[:k]) + i: n_out + sum(r_out[:k]) + o for i, o in r.aliases.items()})

    def full_body(*refs):
        ins, rest = refs[:n_in + sum(r_in)], refs[n_in + sum(r_in):]
        outs, scr = rest[:n_out + sum(r_out)], rest[n_out + sum(r_out):]
        hooks = [r.make(ins[n_in + sum(r_in[:k]):n_in + sum(r_in[:k + 1])], outs[n_out + sum(r_out[:k]):n_out + sum(r_out[:k + 1])],
                        scr[n_scratch + sum(r_scr[:k]):n_scratch + sum(r_scr[:k + 1])]) for k, r in enumerate(riders)]

        def start():
            for h in hooks:
                h[0]()

        def finish():
            for h in hooks:
                h[1]()

        body(start, finish, *ins[:n_in], *outs[:n_out], *scr[:n_scratch])

    call = pl.pallas_call(
        full_body, in_specs=list(in_specs) + [_ANY] * sum(r_in), out_specs=list(out_specs) + [_ANY] * sum(r_out),
        out_shape=list(out_shape) + [s for r in riders for s in r.out_shapes],
        scratch_shapes=list(scratch_shapes) + [s for r in riders for s in r.scratch], input_output_aliases=al, **kw)

    def run(*args):
        res = call(*args, *[t for r in riders for t in r.inputs])
        return res[:n_out], [res[n_out + sum(r_out[:k]):n_out + sum(r_out[:k + 1])] for k in range(len(riders))]

    return run


def _matmul(a, b, dims, tn, out_dtype, name, rider=None, a_cols=None):
    a_shape = a.shape if a_cols is None else (a.shape[0], a_cols[0])
    a_index = 0 if a_cols is None else a_cols[1]
    m = a_shape[1] if dims == "tn" else a_shape[0]
    n = b.shape[0] if dims == "nt" else b.shape[1]
    kdim = b.shape[1] if dims == "nt" else b.shape[0]
    tn = _tile(n, tn)
    steps = n // tn
    b_spec = pl.BlockSpec((tn, kdim), lambda j: (j, 0)) if dims == "nt" else pl.BlockSpec((kdim, tn), lambda j: (0, j))

    def body(start, finish, a_ref, b_ref, o_ref):
        pl.when(pl.program_id(0) == 0)(start)
        o_ref[...] = _dot(a_ref[...], b_ref[...], dims).astype(out_dtype)
        pl.when(pl.program_id(0) == steps - 1)(finish)

    (out,), got = _hosted_call(
        body, [rider], 2, 1, 0, grid=(steps,), in_specs=[pl.BlockSpec(a_shape, lambda j: (0, a_index)), b_spec],
        out_specs=[pl.BlockSpec((m, tn), lambda j: (0, j))], out_shape=[jax.ShapeDtypeStruct((m, n), out_dtype)],
        name=name, compiler_params=_cp("parallel" if rider is None else "arbitrary"))(a, b)
    return out if rider is None else (out, got[0])


def _chip_rider(sums, rows_total=None, row0=0, into=None):
    _, r, c = sums.shape
    rows_total = rows_total or r

    def make(ins, outs, scratch):
        send_sems, recv_sems, local_sem = scratch
        x, y, core = lax.axis_index("x"), lax.axis_index("y"), lax.axis_index("c")
        mine = 2 * x + y
        land = lambda chip: outs[0].at[chip].at[pl.ds(row0, r)]
        local = pltpu.make_async_copy(ins[0].at[mine], land(mine), local_sem)
        sends, recvs = [], []
        for k in range(1, N_CHIP):
            px = 1 - x if k & 2 else x
            py = 1 - y if k & 1 else y
            kw = dict(send_sem=send_sems.at[k - 1], recv_sem=recv_sems.at[k - 1], device_id=(px, py, core),
                      device_id_type=_MESH)
            sends.append(pltpu.make_async_remote_copy(src_ref=ins[0].at[2 * px + py], dst_ref=land(mine), **kw))
            recvs.append(pltpu.make_async_remote_copy(src_ref=ins[0].at[mine], dst_ref=land(2 * px + py), **kw))

        def start():
            for cp in [local] + sends:
                cp.start()

        def finish():
            local.wait()
            for cp in sends:
                cp.wait_send()
            for cp in recvs:
                cp.wait_recv()

        return start, finish

    return _Rider([sums] + ([] if into is None else [into]), [jax.ShapeDtypeStruct((N_CHIP, rows_total, c), sums.dtype)],
                  [pltpu.SemaphoreType.DMA((N_CHIP - 1,)), pltpu.SemaphoreType.DMA((N_CHIP - 1,)), pltpu.SemaphoreType.DMA(())],
                  {} if into is None else {1: 0}, make)


def _dxn(dproj, wfull, riders, tk, name):
    m, k = dproj.shape
    n = wfull.shape[0]
    tk = _tile(k, tk)
    steps = k // tk

    def body(start, finish, a_ref, b_ref, o_ref):
        j = pl.program_id(0)

        @pl.when(j == 0)
        def _():
            start()
            o_ref[...] = jnp.zeros_like(o_ref)
        o_ref[...] += _dot(a_ref[...], b_ref[...], "nt")
        pl.when(j == steps - 1)(finish)

    (dxn,), got = _hosted_call(
        body, riders, 2, 1, 0, grid=(steps,),
        in_specs=[pl.BlockSpec((m, tk), lambda j: (0, j)), pl.BlockSpec((n, tk), lambda j: (0, j))],
        out_specs=[pl.BlockSpec((m, n), lambda j: (0, 0))], out_shape=[jax.ShapeDtypeStruct((m, n), F32)],
        name=name, compiler_params=_cp("arbitrary"))(dproj, wfull)
    return dxn, got


def _conv_taps(x, w):
    c = x * w[CONV_WIDTH - 1:CONV_WIDTH, :]
    for j in range(CONV_WIDTH - 1):
        c = c + pltpu.roll(x, CONV_WIDTH - 1 - j, 0) * w[j:j + 1, :]
    return c


def _gdn_prep(proj, conv_wt, nh):
    lp = proj.shape[0]
    scale = HEAD_DIM ** -0.5

    def body(x_ref, w_ref, o_ref):
        which = pl.program_id(0) // nh
        c = _conv_taps(x_ref[...], w_ref[...])
        s = c * _sigmoid(c)
        r = lax.rsqrt(jnp.sum(s * s, axis=-1, keepdims=True) + EPS)
        f = jnp.where(which == 0, r * scale, jnp.where(which == 1, r, 1.0))
        o_ref[...] = jnp.where(_iota(s.shape, 0) >= PAD_ROWS, s * f, 0.0)

    return pl.pallas_call(
        body, grid=(3 * nh,),
        in_specs=[pl.BlockSpec((lp, LANES), lambda s: (0, s)), pl.BlockSpec((CONV_WIDTH, LANES), lambda s: (0, s))],
        out_specs=pl.BlockSpec((lp, LANES), lambda s: (0, s)),
        out_shape=jax.ShapeDtypeStruct((lp, 3 * nh * HEAD_DIM), F32), name="gdn_prep",
        compiler_params=_cp("parallel"))(proj, conv_wt)


def _gdn_prep_bwd(proj, conv_wt, dq, dk, dv, dproj, nh):
    lp = proj.shape[0]
    scale = HEAD_DIM ** -0.5
    part = lambda p: pl.BlockSpec((lp, LANES), lambda s: (0, jnp.clip(s - p * nh, 0, nh - 1)))

    def body(x_ref, w_ref, dq_ref, dk_ref, dv_ref, _, dx_ref, dw_ref):
        which = pl.program_id(0) // nh
        x = x_ref[...]
        w = w_ref[...]
        c = _conv_taps(x, w)
        sg = _sigmoid(c)
        s = c * sg
        r = lax.rsqrt(jnp.sum(s * s, axis=-1, keepdims=True) + EPS)
        dy = jnp.where(which == 0, dq_ref[...], jnp.where(which == 1, dk_ref[...], dv_ref[...]))
        dy = jnp.where(_iota(s.shape, 0) >= PAD_ROWS, dy, 0.0)
        y0 = s * r
        dy0 = dy * jnp.where(which == 0, scale, 1.0)
        ds_n = r * (dy0 - y0 * jnp.sum(dy0 * y0, axis=-1, keepdims=True))
        ds = jnp.where(which == 2, dy, ds_n)
        dc = ds * (sg * (1.0 + c * (1.0 - sg)))
        dx = dc * w[CONV_WIDTH - 1:CONV_WIDTH, :]
        rows = [jnp.sum(dc * x, axis=0, keepdims=True)]
        for j in range(CONV_WIDTH - 2, -1, -1):
            sh = CONV_WIDTH - 1 - j
            dx = dx + pltpu.roll(dc, lp - sh, 0) * w[j:j + 1, :]
            rows.insert(0, jnp.sum(dc * pltpu.roll(x, sh, 0), axis=0, keepdims=True))
        dx_ref[...] = dx.astype(BF16)
        dw_ref[...] = jnp.concatenate(rows, axis=0)

    strip = pl.BlockSpec((lp, LANES), lambda s: (0, s))
    taps = pl.BlockSpec((CONV_WIDTH, LANES), lambda s: (0, s))
    return pl.pallas_call(
        body, grid=(3 * nh,), in_specs=[strip, taps, part(0), part(1), part(2), _ANY], out_specs=[strip, taps],
        out_shape=[jax.ShapeDtypeStruct(dproj.shape, BF16), jax.ShapeDtypeStruct((CONV_WIDTH, 3 * nh * HEAD_DIM), F32)],
        input_output_aliases={5: 0}, name="gdn_prep_bwd", compiler_params=_cp("parallel"))(proj, conv_wt, dq, dk, dv, dproj)


def _gates(proj, bias_row, nega_row, nh):
    lp = proj.shape[0]
    nc = lp // CHUNK

    def body(p_ref, b_ref, a_ref, g_ref, gt3_ref, gtf_ref):
        lane = _iota((CHUNK, LANES), 1)
        tri = (_iota((CHUNK, CHUNK), 0) >= _iota((CHUNK, CHUNK), 1)).astype(F32)

        def step(n, carry):
            r0 = pl.multiple_of(n * CHUNK, CHUNK)
            z = p_ref[pl.ds(r0, CHUNK), :] + b_ref[...]
            base = jnp.where(lane < nh, _sigmoid(z),
                             jnp.where(lane < 2 * nh, a_ref[...] * _softplus(z),
                                       jnp.where(lane < 3 * nh, -_softplus(-z), 0.0)))
            base = jnp.where(r0 + _iota((CHUNK, LANES), 0) >= PAD_ROWS, base, 0.0)
            cs = _hdot(tri, base)
            run = jnp.where((lane >= 2 * nh) & (lane < 3 * nh), cs + carry, cs)
            sh = pltpu.roll(run, 2 * nh, 1)
            out = base + jnp.where((lane >= 3 * nh) & (lane < 5 * nh), sh, 0.0)
            g_ref[pl.ds(r0, CHUNK), :] = out
            gt3_ref[n] = out.T
            return carry + cs[CHUNK - 1:CHUNK, :]

        lax.fori_loop(0, nc, step, jnp.zeros((1, LANES), F32))
        gtf_ref[...] = g_ref[...].T

    vec = pl.BlockSpec((1, LANES), lambda i: (0, 0))
    return pl.pallas_call(
        body, grid=(1,), in_specs=[pl.BlockSpec((lp, LANES), lambda i: (0, 8 * nh)), vec, vec],
        out_specs=[pl.BlockSpec((lp, LANES), lambda i: (0, 0)), pl.BlockSpec((nc, LANES, CHUNK), lambda i: (0, 0, 0)),
                   pl.BlockSpec((LANES, lp), lambda i: (0, 0))],
        out_shape=[jax.ShapeDtypeStruct((lp, LANES), F32), jax.ShapeDtypeStruct((nc, LANES, CHUNK), F32),
                   jax.ShapeDtypeStruct((LANES, lp), F32)],
        name="gates", compiler_params=_cp("arbitrary"))(proj, bias_row, nega_row)


def _gates_bwd(proj, bias_row, nega_row, gates, dgate_gdn, dc_t, dproj, nh):
    lp = proj.shape[0]
    nc = lp // CHUNK

    def body(p_ref, b_ref, a_ref, g_ref, dg_ref, dc_ref, _, dz_ref, sm_ref, dct_scr):
        lane = _iota((CHUNK, LANES), 1)
        triu = (_iota((CHUNK, CHUNK), 0) <= _iota((CHUNK, CHUNK), 1)).astype(F32)
        dct_scr[...] = dc_ref[...].T
        sm_ref[...] = jnp.zeros_like(sm_ref)
        dz_ref[:, LANES:] = jnp.zeros((lp, NARROW - LANES), BF16)

        def step(i, carry):
            n = nc - 1 - i
            r0 = pl.multiple_of(n * CHUNK, CHUNK)
            z = p_ref[pl.ds(r0, CHUNK), :] + b_ref[...]
            gt = g_ref[pl.ds(r0, CHUNK), :]
            dgd = dg_ref[pl.ds(r0, CHUNK), :]
            dch = dct_scr[pl.ds(r0, CHUNK), :]
            rc = _hdot(triu, dch) + carry
            sg = _sigmoid(z)
            dz = jnp.where(lane < nh, dgd * sg * (1.0 - sg),
                           jnp.where(lane < 2 * nh, dgd * a_ref[...] * sg,
                                     jnp.where(lane < 3 * nh, rc * (1.0 - sg), 0.0)))
            dz = jnp.where(r0 + _iota((CHUNK, LANES), 0) >= PAD_ROWS, dz, 0.0)
            dz_ref[pl.ds(r0, CHUNK), 0:LANES] = dz.astype(BF16)
            sm_ref[0:1, :] += jnp.sum(dz, axis=0, keepdims=True)
            sm_ref[1:2, :] += jnp.sum(jnp.where((lane >= nh) & (lane < 2 * nh), dgd * gt, 0.0), axis=0, keepdims=True)
            return carry + jnp.sum(dch, axis=0, keepdims=True)

        lax.fori_loop(0, nc, step, jnp.zeros((1, LANES), F32))

    vec = pl.BlockSpec((1, LANES), lambda i: (0, 0))
    full = pl.BlockSpec((lp, LANES), lambda i: (0, 0))
    last = pl.BlockSpec((lp, LANES), lambda i: (0, 8 * nh))
    tail = pl.BlockSpec((lp, NARROW), lambda i: (0, 8 * nh * LANES // NARROW))
    return pl.pallas_call(
        body, grid=(1,), in_specs=[last, vec, vec, full, full, pl.BlockSpec((LANES, lp), lambda i: (0, 0)), _ANY],
        out_specs=[tail, pl.BlockSpec((8, LANES), lambda i: (0, 0))],
        out_shape=[jax.ShapeDtypeStruct(dproj.shape, BF16), jax.ShapeDtypeStruct((8, LANES), F32)],
        scratch_shapes=[pltpu.VMEM((lp, LANES), F32)], input_output_aliases={6: 0},
        name="gates_bwd", compiler_params=_cp("arbitrary"))(proj, bias_row, nega_row, gates, dgate_gdn, dc_t, dproj)


def _tri_inv(a):
    t = jnp.where(_iota(a.shape, 1) == _iota(a.shape, 2), 1.0, 0.0) - a
    p = a
    for _ in range(CHUNK.bit_length() - 2):
        ph, pw = _split(p)
        p = _dot3(ph, ph) + (_dot3(ph, pw) + _dot3(pw, ph))
        ph, pw = _split(p)
        th, tw = _split(t)
        t = t + (_dot3(th, ph) + (_dot3(th, pw) + _dot3(tw, ph)))
    return t


def _gdn_chunk(q, k, v, beta, gc, gr, t=None):
    ii, jj = _iota((1, CHUNK, CHUNK), 1), _iota((1, CHUNK, CHUNK), 2)
    causal, strict = ii >= jj, ii > jj
    dm = jnp.where(causal, jnp.exp(jnp.where(causal, gc - gr, 0.0)), 0.0)
    kk = _bdot3(k, k, "nt")
    a = jnp.where(strict, beta * kk * dm, 0.0)
    if t is None:
        t = _tri_inv(a)
    eg = jnp.exp(gc)
    glast = gc[:, CHUNK - 1:CHUNK, :]
    ekd = jnp.exp(glast - gc)
    bv = beta * v
    bk = (beta * eg) * k
    ub = _bdot3(t, jnp.concatenate([bv, bk], axis=2))
    qk = _bdot3(q, k, "nt")
    return dict(causal=causal, strict=strict, dm=dm, kk=kk, a=a, t=t, eg=eg, ekd=ekd, bv=bv, bk=bk,
                u=ub[:, :, :HEAD_DIM], w=ub[:, :, HEAD_DIM:], qk=qk, aqk=jnp.where(causal, qk * dm, 0.0),
                q_dec=q * eg, k_dec=k * ekd, decay=jnp.exp(glast))


def _heads(ref, nh):
    return jnp.stack([ref[:, h * HEAD_DIM:(h + 1) * HEAD_DIM] for h in range(nh)], axis=0)


def _gdn_chunk_inputs(q_ref, k_ref, v_ref, g, gt, nh):
    col = lambda o: jnp.stack([g[:, o + h:o + h + 1] for h in range(nh)], axis=0)
    gr = jnp.stack([gt[3 * nh + h:3 * nh + h + 1, :] for h in range(nh)], axis=0)
    return _heads(q_ref, nh), _heads(k_ref, nh), _heads(v_ref, nh), col(0), col(3 * nh), gr


def _gdn_fwd(qkv, gates, gt3, nh):
    lp = qkv.shape[0]
    nc = lp // CHUNK
    w = nh * HEAD_DIM

    def body(q_ref, k_ref, v_ref, g_ref, gt_ref, o_ref, sall_ref, tall_ref, s_scr):
        @pl.when(pl.program_id(0) == 0)
        def _():
            s_scr[...] = jnp.zeros_like(s_scr)
        c = _gdn_chunk(*_gdn_chunk_inputs(q_ref, k_ref, v_ref, g_ref[...], gt_ref[0], nh))
        s = s_scr[...]
        sall_ref[0] = s
        tall_ref[0] = c["t"]
        v_new = c["u"] - _bdot3(c["w"], s)
        o = _bdot3(c["q_dec"], s) + _bdot3(c["aqk"], v_new)
        s_scr[...] = s * c["decay"] + _bdot3(c["k_dec"], v_new, "tn")
        for h in range(nh):
            o_ref[:, h * HEAD_DIM:(h + 1) * HEAD_DIM] = o[h]

    return pl.pallas_call(
        body, grid=(nc,),
        in_specs=[pl.BlockSpec((CHUNK, w), lambda n: (n, 0)), pl.BlockSpec((CHUNK, w), lambda n: (n, 1)),
                  pl.BlockSpec((CHUNK, w), lambda n: (n, 2)), pl.BlockSpec((CHUNK, LANES), lambda n: (n, 0)),
                  pl.BlockSpec((1, LANES, CHUNK), lambda n: (n, 0, 0))],
        out_specs=[pl.BlockSpec((CHUNK, w), lambda n: (n, 0)),
                   pl.BlockSpec((1, nh, HEAD_DIM, HEAD_DIM), lambda n: (n, 0, 0, 0)),
                   pl.BlockSpec((1, nh, CHUNK, CHUNK), lambda n: (n, 0, 0, 0))],
        out_shape=[jax.ShapeDtypeStruct((lp, w), F32), jax.ShapeDtypeStruct((nc, nh, HEAD_DIM, HEAD_DIM), F32),
                   jax.ShapeDtypeStruct((nc, nh, CHUNK, CHUNK), F32)],
        scratch_shapes=[pltpu.VMEM((nh, HEAD_DIM, HEAD_DIM), F32)],
        name="gdn_fwd", compiler_params=_cp("arbitrary"))(qkv, qkv, qkv, gates, gt3)


def _gdn_bwd(qkv, gates, gt3, s_all, t_all, do, nh, rider=None):
    lp = qkv.shape[0]
    nc = lp // CHUNK
    w = nh * HEAD_DIM
    rev = lambda n: nc - 1 - n

    def body(start, finish, q_ref, k_ref, v_ref, g_ref, gt_ref, s_ref, t_ref, do_ref, dq_ref, dk_ref, dv_ref, dg_ref, ds_scr):
        @pl.when(pl.program_id(0) == 0)
        def _():
            start()
            ds_scr[...] = jnp.zeros_like(ds_scr)
        q, k, v, beta, gc, gr = _gdn_chunk_inputs(q_ref, k_ref, v_ref, g_ref[...], gt_ref[0], nh)
        c = _gdn_chunk(q, k, v, beta, gc, gr, t_ref[0])
        s = s_ref[0]
        dsn = ds_scr[...]
        dout = _heads(do_ref, nh)
        v_new = c["u"] - _bdot3(c["w"], s)
        dq_dec = _bdot3(dout, s, "nt")
        daqk = jnp.where(c["causal"], _bdot3(dout, v_new, "nt"), 0.0)
        dv_new = _bdot3(c["aqk"], dout, "tn") + _bdot3(c["k_dec"], dsn)
        dk_dec = _bdot3(v_new, dsn, "nt")
        ddecay = jnp.sum(jnp.sum(dsn * s, axis=2, keepdims=True), axis=1, keepdims=True)
        dw = -_bdot3(dv_new, s, "nt")
        ds_scr[...] = _bdot3(c["q_dec"], dout, "tn") + c["decay"] * dsn - _bdot3(c["w"], dv_new, "tn")
        duw = jnp.concatenate([dv_new, dw], axis=2)
        dt = _bdot3(duw, jnp.concatenate([c["bv"], c["bk"]], axis=2), "nt")
        dbvk = _bdot3(c["t"], duw, "tn")
        dbv, dbk = dbvk[:, :, :HEAD_DIM], dbvk[:, :, HEAD_DIM:]
        da = jnp.where(c["strict"], -_bdot3(_bdot3(c["t"], dt, "tn"), c["t"], "nt"), 0.0)
        dkk = da * beta * c["dm"]
        dqk = daqk * c["dm"]
        e = da * c["a"] + daqk * c["aqk"]
        dq = dq_dec * c["eg"] + _bdot3(dqk, k)
        dk = (dk_dec * c["ekd"] + _bdot3(dkk, k) + _bdot3(dkk, k, "tn") + _bdot3(dqk, q, "tn")
              + (beta * c["eg"]) * dbk)
        dv = beta * dbv
        rs = lambda x: jnp.sum(x, axis=2, keepdims=True)
        dbeta = rs(dbv * v) + c["eg"] * rs(dbk * k) + rs(da * c["kk"] * c["dm"])
        kd_term = rs(dk_dec * c["k_dec"])
        eh, ew = _split(e)
        ones = jnp.ones((nh, CHUNK, LANES), BF16)
        col_sums = (_dot3(eh, ones, "tn") + _dot3(ew, ones, "tn"))[:, :, 0:1]
        dg_cum = rs(dq_dec * c["q_dec"]) - kd_term + rs(dbk * c["bk"]) + rs(e) - col_sums
        last = jnp.sum(kd_term, axis=1, keepdims=True) + ddecay * c["decay"]
        dg_cum = dg_cum + jnp.where(_iota((1, CHUNK, 1), 1) == CHUNK - 1, last, 0.0)
        lane = _iota((CHUNK, LANES), 1)
        acc = jnp.zeros((CHUNK, LANES), F32)
        for h in range(nh):
            sl = slice(h * HEAD_DIM, (h + 1) * HEAD_DIM)
            dq_ref[:, sl] = dq[h]
            dk_ref[:, sl] = dk[h]
            dv_ref[:, sl] = dv[h]
            acc = acc + jnp.where(lane == h, dbeta[h], 0.0) + jnp.where(lane == nh + h, dg_cum[h], 0.0)
        triu = (_iota((CHUNK, CHUNK), 0) <= _iota((CHUNK, CHUNK), 1)).astype(F32)
        dg_ref[...] = jnp.where(lane < nh, acc, _hdot(triu, acc))
        pl.when(pl.program_id(0) == nc - 1)(finish)

    outs, got = _hosted_call(
        body, [rider], 8, 4, 1, grid=(nc,),
        in_specs=[pl.BlockSpec((CHUNK, w), lambda n: (rev(n), 0)), pl.BlockSpec((CHUNK, w), lambda n: (rev(n), 1)),
                  pl.BlockSpec((CHUNK, w), lambda n: (rev(n), 2)), pl.BlockSpec((CHUNK, LANES), lambda n: (rev(n), 0)),
                  pl.BlockSpec((1, LANES, CHUNK), lambda n: (rev(n), 0, 0)),
                  pl.BlockSpec((1, nh, HEAD_DIM, HEAD_DIM), lambda n: (rev(n), 0, 0, 0)),
                  pl.BlockSpec((1, nh, CHUNK, CHUNK), lambda n: (rev(n), 0, 0, 0)),
                  pl.BlockSpec((CHUNK, w), lambda n: (rev(n), 0))],
        out_specs=[pl.BlockSpec((CHUNK, w), lambda n: (rev(n), 0))] * 3 + [pl.BlockSpec((CHUNK, LANES), lambda n: (rev(n), 0))],
        out_shape=[jax.ShapeDtypeStruct((lp, w), F32)] * 3 + [jax.ShapeDtypeStruct((lp, LANES), F32)],
        scratch_shapes=[pltpu.VMEM((nh, HEAD_DIM, HEAD_DIM), F32)],
        name="gdn_bwd", compiler_params=_cp("arbitrary"))(qkv, qkv, qkv, gates, gt3, s_all, t_all, do)
    return outs, (got[0] if got else None)


def _merge_gdn(o_gdn, proj, norm_w, nh):
    lp = o_gdn.shape[0]

    def body(o_ref, z_ref, w_ref, m_ref):
        o = o_ref[...]
        z = z_ref[...]
        m_ref[...] = (o * _rms(o) * w_ref[...] * (z * _sigmoid(z))).astype(BF16)

    return pl.pallas_call(
        body, grid=(nh,),
        in_specs=[pl.BlockSpec((lp, LANES), lambda s: (0, s)), pl.BlockSpec((lp, LANES), lambda s: (0, 3 * nh + s)),
                  pl.BlockSpec((1, LANES), lambda s: (0, 0))],
        out_specs=pl.BlockSpec((lp, LANES), lambda s: (0, s)),
        out_shape=jax.ShapeDtypeStruct((lp, 2 * nh * HEAD_DIM), BF16), name="merge_gdn",
        compiler_params=_cp("parallel"))(o_gdn, proj, norm_w)


def _merge_gdn_bwd(o_gdn, proj, norm_w, dmerged, nh):
    lp = o_gdn.shape[0]

    def body(o_ref, z_ref, w_ref, dm_ref, do_ref, dz_ref, dw_ref):
        o = o_ref[...]
        r = _rms(o)
        xh = o * r
        silu, dsilu = _silu_and_grad(z_ref[...])
        dm = dm_ref[...]
        dn = dm * silu
        dz_ref[...] = (dm * (xh * w_ref[...]) * dsilu).astype(BF16)
        dnw = dn * w_ref[...]
        do_ref[...] = r * (dnw - xh * jnp.mean(dnw * xh, axis=-1, keepdims=True))

        @pl.when(pl.program_id(0) == 0)
        def _():
            dw_ref[...] = jnp.zeros_like(dw_ref)
        dw_ref[...] += jnp.sum(dn * xh, axis=0, keepdims=True)

    w = nh * HEAD_DIM
    return pl.pallas_call(
        body, grid=(nh,),
        in_specs=[pl.BlockSpec((lp, LANES), lambda s: (0, s)), pl.BlockSpec((lp, LANES), lambda s: (0, 3 * nh + s)),
                  pl.BlockSpec((1, LANES), lambda s: (0, 0)), pl.BlockSpec((lp, LANES), lambda s: (0, s))],
        out_specs=[pl.BlockSpec((lp, LANES), lambda s: (0, s)), pl.BlockSpec((lp, LANES), lambda s: (0, 3 * nh + s)),
                   pl.BlockSpec((1, LANES), lambda s: (0, 0))],
        out_shape=[jax.ShapeDtypeStruct((lp, w), F32), jax.ShapeDtypeStruct((lp, 8 * w + NARROW), BF16),
                   jax.ShapeDtypeStruct((1, LANES), F32)],
        name="merge_gdn_bwd", compiler_params=_cp("arbitrary"))(o_gdn, proj, norm_w, dmerged)


def _fox_prep(proj, qk_w, nh):
    lp = proj.shape[0]

    def body(x_ref, w_ref, o_ref):
        x = x_ref[...]
        o_ref[...] = x * _rms(x) * w_ref[0]

    return pl.pallas_call(
        body, grid=(2 * nh,),
        in_specs=[pl.BlockSpec((lp, LANES), lambda s: (0, 4 * nh + s)), pl.BlockSpec((1, 1, LANES), lambda s: (s // nh, 0, 0))],
        out_specs=pl.BlockSpec((lp, LANES), lambda s: (0, s)),
        out_shape=jax.ShapeDtypeStruct((lp, 2 * nh * HEAD_DIM), F32), name="fox_prep",
        compiler_params=_cp("parallel"))(proj, qk_w)


def _fox_prep_bwd(proj, qk_w, dq, dk, dproj, nh):
    lp = proj.shape[0]
    part = lambda p: pl.BlockSpec((lp, LANES), lambda s: (0, jnp.clip(s - p * nh, 0, nh - 1)))

    def body(x_ref, w_ref, dq_ref, dk_ref, _, dx_ref, dw_ref):
        x = x_ref[...]
        r = _rms(x)
        xh = x * r
        dy = jnp.where(pl.program_id(0) < nh, dq_ref[...], dk_ref[...])
        dyw = dy * w_ref[0]
        dx_ref[...] = (r * (dyw - xh * jnp.mean(dyw * xh, axis=-1, keepdims=True))).astype(BF16)

        @pl.when(pl.program_id(0) % nh == 0)
        def _():
            dw_ref[...] = jnp.zeros_like(dw_ref)
        dw_ref[0] += jnp.sum(dy * xh, axis=0, keepdims=True)

    strip = pl.BlockSpec((lp, LANES), lambda s: (0, 4 * nh + s))
    wsp = pl.BlockSpec((1, 1, LANES), lambda s: (s // nh, 0, 0))
    return pl.pallas_call(
        body, grid=(2 * nh,), in_specs=[strip, wsp, part(0), part(1), _ANY], out_specs=[strip, wsp],
        out_shape=[jax.ShapeDtypeStruct(dproj.shape, BF16), jax.ShapeDtypeStruct((2, 1, LANES), F32)],
        input_output_aliases={4: 0}, name="fox_prep_bwd", compiler_params=_cp("arbitrary"))(proj, qk_w, dq, dk, dproj)


def _fox_probs(q, k, gates, crow, h, i, nh, lse=None):
    kl = k.shape[0]
    lane = _iota((Q_BLOCK, LANES), 1)
    ct = jnp.sum(jnp.where(lane == 4 * nh + h, gates, 0.0), axis=1, keepdims=True)
    tq, kq = _iota((Q_BLOCK, Q_BLOCK), 0), _iota((Q_BLOCK, Q_BLOCK), 1)
    if i == 0:
        s = _bdot(q, k, "nt") * (HEAD_DIM ** -0.5) + (ct - crow)
        s = jnp.where((kq <= tq) & ((kq >= PAD_ROWS) | (tq < PAD_ROWS)), s, NEG)
    else:
        crow = jnp.where(_iota((1, kl), 1) < PAD_ROWS, -NEG, crow)
        s = _bdot(q, k, "nt") * (HEAD_DIM ** -0.5) + (ct - crow)
        s = jnp.concatenate([s[:, :kl - Q_BLOCK], jnp.where(kq <= tq, s[:, kl - Q_BLOCK:], NEG)], axis=1)
    if lse is not None:
        return jnp.exp(s - lse)
    m = jnp.max(s, axis=1, keepdims=True)
    p = jnp.exp(s - m)
    tot = jnp.sum(p, axis=1, keepdims=True)
    return p / tot, m + jnp.log(tot)


FOX_HEADS_PER_STEP = 2


def _fox_specs(lp, nh):
    hw = FOX_HEADS_PER_STEP * LANES
    return [pl.BlockSpec((Q_BLOCK, hw), lambda g, i: (i, g)),
            pl.BlockSpec((lp, hw), lambda g, i: (0, nh // FOX_HEADS_PER_STEP + g)),
            pl.BlockSpec((lp, hw), lambda g, i: (0, 6 * nh // FOX_HEADS_PER_STEP + g)),
            pl.BlockSpec((Q_BLOCK, LANES), lambda g, i: (i, 0)),
            pl.BlockSpec((LANES, lp), lambda g, i: (0, 0))]


def _fox_fwd(qkn, proj, gates, gtf, nh):
    lp = qkn.shape[0]

    def body(q_ref, k_ref, v_ref, g_ref, gt_ref, o_ref, lse_ref):
        g, i = pl.program_id(0), pl.program_id(1)
        for j in range(lp // Q_BLOCK):
            @pl.when(i == j)
            def _(j=j):
                kl = (j + 1) * Q_BLOCK
                for hh in range(FOX_HEADS_PER_STEP):
                    h = FOX_HEADS_PER_STEP * g + hh
                    sl = slice(hh * LANES, (hh + 1) * LANES)
                    p, lse = _fox_probs(q_ref[:, sl], k_ref[0:kl, sl], g_ref[...], gt_ref[pl.ds(4 * nh + h, 1), :][:, 0:kl],
                                        h, j, nh)
                    o_ref[:, sl] = _bdot(p, v_ref[0:kl, sl])
                    lse_ref[:, sl] = jnp.broadcast_to(lse, (Q_BLOCK, LANES))

    blk = pl.BlockSpec((Q_BLOCK, FOX_HEADS_PER_STEP * LANES), lambda g, i: (i, g))
    return pl.pallas_call(
        body, grid=(nh // FOX_HEADS_PER_STEP, lp // Q_BLOCK), in_specs=_fox_specs(lp, nh), out_specs=[blk, blk],
        out_shape=[jax.ShapeDtypeStruct((lp, nh * HEAD_DIM), F32)] * 2, name="fox_fwd",
        compiler_params=_cp("parallel", "parallel"))(qkn, qkn, proj, gates, gtf)


def _fox_bwd(qkn, proj, gates, gtf, lse, do, dproj, nh):
    lp = qkn.shape[0]
    nq = lp // Q_BLOCK
    w = nh * HEAD_DIM
    scale = HEAD_DIM ** -0.5

    def body(q_ref, k_ref, v_ref, g_ref, gt_ref, lse_ref, do_ref, _, dq_ref, dk_ref, dc_ref, dv_ref, dv_scr):
        g, i = pl.program_id(0), pl.program_id(1)

        @pl.when(i == 0)
        def _():
            dk_ref[...] = jnp.zeros_like(dk_ref)
            dv_scr[...] = jnp.zeros_like(dv_scr)
            dc_ref[...] = jnp.zeros_like(dc_ref)
        for j in range(nq):
            @pl.when(i == j)
            def _(j=j):
                kl = (j + 1) * Q_BLOCK
                for hh in range(FOX_HEADS_PER_STEP):
                    h = FOX_HEADS_PER_STEP * g + hh
                    sl = slice(hh * LANES, (hh + 1) * LANES)
                    q, k = q_ref[:, sl], k_ref[0:kl, sl]
                    p = _fox_probs(q, k, g_ref[...], gt_ref[pl.ds(4 * nh + h, 1), :][:, 0:kl], h, j, nh,
                                   lse_ref[:, sl][:, 0:1])
                    dout = do_ref[:, sl]
                    dp = _bdot(dout, v_ref[0:kl, sl], "nt")
                    ds = p * (dp - jnp.sum(p * dp, axis=1, keepdims=True))
                    dq_ref[:, sl] = _bdot(ds, k) * scale
                    dk_ref[0:kl, sl] += _bdot(ds, q, "tn") * scale
                    dv_scr[0:kl, sl] += _bdot(p, dout, "tn")
                    dc_ref[hh, :, 0:kl] -= jnp.sum(ds, axis=0, keepdims=True)

        @pl.when(i == nq - 1)
        def _():
            dv_ref[...] = dv_scr[...].astype(BF16)

    hw = FOX_HEADS_PER_STEP * LANES
    blk = pl.BlockSpec((Q_BLOCK, hw), lambda g, i: (i, g))
    col = pl.BlockSpec((lp, hw), lambda g, i: (0, g))
    return pl.pallas_call(
        body, grid=(nh // FOX_HEADS_PER_STEP, nq), in_specs=_fox_specs(lp, nh) + [blk, blk, _ANY],
        out_specs=[blk, col, pl.BlockSpec((FOX_HEADS_PER_STEP, 1, lp), lambda g, i: (g, 0, 0)),
                   pl.BlockSpec((lp, hw), lambda g, i: (0, 6 * nh // FOX_HEADS_PER_STEP + g))],
        out_shape=[jax.ShapeDtypeStruct((lp, w), F32)] * 2 + [jax.ShapeDtypeStruct((nh, 1, lp), F32),
                                                             jax.ShapeDtypeStruct(dproj.shape, BF16)],
        scratch_shapes=[pltpu.VMEM((lp, hw), F32)], input_output_aliases={7: 3},
        name="fox_bwd", compiler_params=_cp("parallel", "arbitrary"))(qkn, qkn, proj, gates, gtf, lse, do, dproj)


def _merge_fox(o_fox, proj, merged, nh):
    lp = o_fox.shape[0]

    def body(o_ref, z_ref, _, m_ref):
        z = z_ref[...]
        m_ref[...] = (o_ref[...] * (z * _sigmoid(z))).astype(BF16)

    return pl.pallas_call(
        body, grid=(nh,),
        in_specs=[pl.BlockSpec((lp, LANES), lambda s: (0, s)), pl.BlockSpec((lp, LANES), lambda s: (0, 7 * nh + s)), _ANY],
        out_specs=pl.BlockSpec((lp, LANES), lambda s: (0, nh + s)),
        out_shape=jax.ShapeDtypeStruct(merged.shape, BF16), input_output_aliases={2: 0}, name="merge_fox",
        compiler_params=_cp("parallel"))(o_fox, proj, merged)


def _merge_fox_bwd(o_fox, proj, dmerged, dproj, nh):
    lp = o_fox.shape[0]

    def body(o_ref, z_ref, dm_ref, _, do_ref, dz_ref):
        silu, dsilu = _silu_and_grad(z_ref[...])
        dm = dm_ref[...]
        do_ref[...] = dm * silu
        dz_ref[...] = (dm * o_ref[...] * dsilu).astype(BF16)

    w = nh * HEAD_DIM
    return pl.pallas_call(
        body, grid=(nh,),
        in_specs=[pl.BlockSpec((lp, LANES), lambda s: (0, s)), pl.BlockSpec((lp, LANES), lambda s: (0, 7 * nh + s)),
                  pl.BlockSpec((lp, LANES), lambda s: (0, nh + s)), _ANY],
        out_specs=[pl.BlockSpec((lp, LANES), lambda s: (0, s)), pl.BlockSpec((lp, LANES), lambda s: (0, 7 * nh + s))],
        out_shape=[jax.ShapeDtypeStruct((lp, w), F32), jax.ShapeDtypeStruct(dproj.shape, BF16)],
        input_output_aliases={3: 1}, name="merge_fox_bwd", compiler_params=_cp("parallel"))(o_fox, proj, dmerged, dproj)


def _post(out, x, target, post_w):
    lp, d = out.shape

    def body(o_ref, x_ref, t_ref, w_ref, dy_ref, do_ref, loss_ref, dw_ref):
        i = pl.program_id(0)

        @pl.when(i == 0)
        def _():
            loss_ref[...] = jnp.zeros_like(loss_ref)
            dw_ref[...] = jnp.zeros_like(dw_ref)
        o = o_ref[...]
        r = _rms(o)
        nrm = o * r
        err = jnp.where(i > 0, x_ref[...] + nrm * w_ref[...] - t_ref[...], 0.0)
        loss_ref[0:1, :] += 0.5 * jnp.sum(jnp.sum(err * err, axis=1, keepdims=True), axis=0, keepdims=True) / d
        dy = err / d
        dy_ref[...] = dy
        dw_ref[...] += jnp.sum(dy * nrm, axis=0, keepdims=True)
        dyw = dy * w_ref[...]
        do_ref[...] = (r * (dyw - nrm * jnp.mean(dyw * nrm, axis=-1, keepdims=True))).astype(BF16)

    row = pl.BlockSpec((Q_BLOCK, d), lambda i: (i, 0))
    vec = pl.BlockSpec((1, d), lambda i: (0, 0))
    return pl.pallas_call(
        body, grid=(lp // Q_BLOCK,), in_specs=[row, _x_rows(d), _x_rows(d), vec],
        out_specs=[_x_rows(d), row, pl.BlockSpec((8, LANES), lambda i: (0, 0)), vec],
        out_shape=[jax.ShapeDtypeStruct(x.shape, F32), jax.ShapeDtypeStruct((lp, d), BF16),
                   jax.ShapeDtypeStruct((8, LANES), F32), jax.ShapeDtypeStruct((1, d), F32)],
        name="post", compiler_params=_cp("arbitrary"))(out, x, target, post_w)


def _prenorm_bwd(dxn, x, meta, w, dy, rider=None):
    seq, d = x.shape
    lp = seq + Q_BLOCK

    def body(start, finish, dx_ref, x_ref, m_ref, w_ref, dy_ref, gx_ref, gm_ref, dw_ref):
        i = pl.program_id(0)
        pl.when(i == 0)(start)
        h = _h_tile(i, x_ref, m_ref)
        r = _rms(h)
        xh = h * r
        dxn_ = dx_ref[...]
        dxw = dxn_ * w_ref[...]
        dh = jnp.where(i > 0, dy_ref[...], 0.0) + r * (dxw - xh * jnp.mean(dxw * xh, axis=-1, keepdims=True))
        gx_ref[...] = dh

        @pl.when(i == 0)
        def _():
            dw_ref[...] = jnp.zeros_like(dw_ref)
            gm_ref[...] = dh[PAD_ROWS:, :]
        dw_ref[...] += jnp.sum(dxn_ * xh, axis=0, keepdims=True)
        pl.when(i == lp // Q_BLOCK - 1)(finish)

    vec = pl.BlockSpec((1, d), lambda i: (0, 0))
    met = pl.BlockSpec((N_META, d), lambda i: (0, 0))
    outs, got = _hosted_call(
        body, [rider], 5, 3, 0, grid=(lp // Q_BLOCK,),
        in_specs=[pl.BlockSpec((Q_BLOCK, d), lambda i: (i, 0)), _x_rows(d), met, vec, _x_rows(d)],
        out_specs=[_x_rows(d), met, vec],
        out_shape=[jax.ShapeDtypeStruct((seq, d), F32), jax.ShapeDtypeStruct((N_META, d), F32),
                   jax.ShapeDtypeStruct((1, d), F32)],
        name="prenorm_bwd", compiler_params=_cp("arbitrary"))(dxn, x, meta, w, dy)
    return outs, (got[0] if got else None)


def _layer_grads(x, target, meta, pre_w, wfull, conv_wt, a_log, dt_bias, gdn_norm_w, fq_w, fk_w, f_bias, w_out, post_w,
                 late_weights=None, w_out_grads=None):
    nh = a_log.shape[1]
    zpad = jnp.zeros((1, LANES - 3 * nh), F32)
    bias_row = jnp.concatenate([jnp.zeros((1, nh), F32), dt_bias, f_bias, zpad], axis=1)
    nega_row = jnp.concatenate([jnp.zeros((1, nh), F32), -jnp.exp(a_log), jnp.zeros((1, nh), F32), zpad], axis=1)
    qk_w = jnp.stack([fq_w, fk_w])

    xn = _prenorm(x, meta, pre_w)
    if late_weights is None:
        proj = _matmul(xn, wfull, "nn", MM_TILE, F32, "proj")
    else:
        proj, got = _matmul(xn, wfull, "nn", MM_TILE, F32, "proj", late_weights[0])
        conv_wt, w_out = late_weights[1](got)
    qkv = _gdn_prep(proj, conv_wt, nh)
    gates, gt3, gtf = _gates(proj, bias_row, nega_row, nh)
    o_gdn, s_all, t_all = _gdn_fwd(qkv, gates, gt3, nh)
    qkn = _fox_prep(proj, qk_w, nh)
    o_fox, fox_lse = _fox_fwd(qkn, proj, gates, gtf, nh)
    merged = _merge_fox(o_fox, proj, _merge_gdn(o_gdn, proj, gdn_norm_w, nh), nh)
    out = _matmul(merged, w_out, "nn", 4 * LANES, F32, "out_proj")
    dy, dout, loss_blk, dpost_w = _post(out, x, target, post_w)

    dw_out = _matmul(merged, dout, "tn", 4 * LANES, BF16, "dw_out")
    if w_out_grads is None:
        dmerged, gdn_rider = _matmul(dout, w_out, "nt", 4 * LANES, F32, "dmerged"), None
    else:
        dmerged, got = _matmul(dout, w_out, "nt", 4 * LANES, F32, "dmerged", w_out_grads[0](dw_out))
        gdn_rider = w_out_grads[1](dw_out, got)
    do_gdn, dproj, dgdn_norm_w = _merge_gdn_bwd(o_gdn, proj, gdn_norm_w, dmerged, nh)
    do_fox, dproj = _merge_fox_bwd(o_fox, proj, dmerged, dproj, nh)
    dqn, dkn, dc_t, dproj = _fox_bwd(qkn, proj, gates, gtf, fox_lse, do_fox, dproj, nh)
    dproj, dqk_w = _fox_prep_bwd(proj, qk_w, dqn, dkn, dproj, nh)
    (dgq, dgk, dgv, dgate), w_out_parts = _gdn_bwd(qkv, gates, gt3, s_all, t_all, do_gdn, nh, gdn_rider)
    dproj, dconv_wt = _gdn_prep_bwd(proj, conv_wt, dgq, dgk, dgv, dproj, nh)
    dc_rows = jnp.pad(dc_t.reshape(nh, -1), ((2 * nh, LANES - 3 * nh), (0, 0)))
    dproj, gate_sums = _gates_bwd(proj, bias_row, nega_row, gates, dgate, dc_rows, dproj, nh)
    return dict(
        loss=loss_blk[0:1, 0:1], dy=dy, xn=xn, dproj=dproj, post_w=dpost_w,
        conv_wt=dconv_wt, a_log=gate_sums[1:2, nh:2 * nh], dt_bias=gate_sums[0:1, nh:2 * nh],
        gdn_norm_w=dgdn_norm_w, fq_w=dqk_w[0], fk_w=dqk_w[1], f_bias=gate_sums[0:1, 2 * nh:3 * nh], w_out=dw_out,
        w_out_parts=w_out_parts)


def _cast_bf16(a, tr, name):
    r, c = a.shape

    def body(a_ref, o_ref):
        o_ref[...] = a_ref[...].astype(BF16)

    return pl.pallas_call(
        body, grid=(r // tr,), in_specs=[pl.BlockSpec((tr, c), lambda i: (i, 0))],
        out_specs=pl.BlockSpec((tr, c), lambda i: (i, 0)), out_shape=jax.ShapeDtypeStruct((r, c), BF16),
        name=name, compiler_params=_cp("parallel"))(a)


def _column_major(a):
    return jnp.transpose(a, (2, 0, 1))


def _cast_bf16_column_major(a3, name):
    _, r, c = a3.shape

    def body(a_ref, o_ref):
        o_ref[...] = a_ref[...].reshape(LANES, r).T.astype(BF16)

    return pl.pallas_call(
        body, grid=(pl.cdiv(c, LANES),), in_specs=[pl.BlockSpec((LANES, 1, r), lambda i: (i, 0, 0))],
        out_specs=pl.BlockSpec((r, LANES), lambda i: (0, i)), out_shape=jax.ShapeDtypeStruct((r, c), BF16),
        name=name, compiler_params=_cp("parallel"))(_column_major(a3))


def _adamw_column_major(w3, parts, m3, v3, name):
    _, r, c = w3.shape
    n_parts = parts.shape[0]

    def body(w_ref, p_ref, m_ref, v_ref, g_ref, d_ref, nm_ref, nv_ref):
        g = p_ref[0].astype(F32)
        for s in range(1, n_parts):
            g = g + p_ref[s].astype(F32)
        g = g.T
        flat = lambda ref: ref[...].reshape(LANES, r)
        m_new = ADAM_B1 * flat(m_ref) + (1.0 - ADAM_B1) * g
        v_new = ADAM_B2 * flat(v_ref) + (1.0 - ADAM_B2) * (g * g)
        m_hat = m_new / (1.0 - ADAM_B1 ** ADAM_STEP)
        v_hat = v_new / (1.0 - ADAM_B2 ** ADAM_STEP)
        delta = -ADAM_LR * (m_hat / (jnp.sqrt(v_hat) + ADAM_EPS) + ADAM_WD * flat(w_ref))
        for ref, val in ((g_ref, g), (d_ref, delta), (nm_ref, m_new), (nv_ref, v_new)):
            ref[...] = val.reshape(LANES, 1, r)

    blk = pl.BlockSpec((LANES, 1, r), lambda i: (i, 0, 0))
    outs = pl.pallas_call(
        body, grid=(pl.cdiv(c, LANES),), in_specs=[blk, pl.BlockSpec((n_parts, r, LANES), lambda i: (0, 0, i)), blk, blk],
        out_specs=[blk] * 4, out_shape=[jax.ShapeDtypeStruct((c, 1, r), F32)] * 4, name=name,
        compiler_params=_cp("parallel"))(_column_major(w3), parts, _column_major(m3), _column_major(v3))
    return [jnp.transpose(o, (1, 2, 0)) for o in outs]


def _gather_copies(ins, outs, send_sems, recv_sems, local_sems):
    n = len(ins)
    x, y, c = lax.axis_index("x"), lax.axis_index("y"), lax.axis_index("c")
    me, sibling = (x, y, c), (x, y, 1 - c)
    xn, yn, dg = (1 - x, y), (x, 1 - y), (1 - x, 1 - y)

    def copy(a, k, block, to, src=None):
        px, py, pc = block
        rows = outs[a].at[4 * px + 2 * py + pc]
        return pltpu.make_async_remote_copy(
            src_ref=rows if src is None else src, dst_ref=rows, send_sem=send_sems.at[a, k],
            recv_sem=recv_sems.at[a, k], device_id=to, device_id_type=_MESH)

    local = [pltpu.make_async_copy(ins[a], outs[a].at[4 * x + 2 * y + c], local_sems.at[a]) for a in range(n)]
    own = [cp for a in range(n) for cp in (copy(a, 0, me, sibling, src=ins[a]), copy(a, 1, me, (*xn, c), src=ins[a]),
                                           copy(a, 2, me, (*yn, c), src=ins[a]))]

    def start():
        for cp in local + own:
            cp.start()

    def finish():
        for a in range(n):
            @pl.when(c == 1)
            def _(a=a):
                copy(a, 1, (*xn, c), me).wait_recv()
                copy(a, 3, (*xn, c), (*yn, c)).start()

            @pl.when(c == 0)
            def _(a=a):
                copy(a, 2, (*yn, c), me).wait_recv()
                copy(a, 3, (*yn, c), (*xn, c)).start()
        for a in range(n):
            pl.when(c == 0)(copy(a, 1, (*xn, c), me).wait_recv)
            copy(a, 4, (*xn, c), sibling).start()
            pl.when(c == 1)(copy(a, 2, (*yn, c), me).wait_recv)
            copy(a, 5, (*yn, c), sibling).start()
        for a in range(n):
            copy(a, 3, (*dg, c), me).wait_recv()
            copy(a, 6, (*dg, c), sibling).start()
        for a in range(n):
            copy(a, 0, sibling, me).wait_recv()
            for k, chip in ((4, xn), (5, yn), (6, dg)):
                copy(a, k, (*chip, 1 - c), me).wait_recv()
                copy(a, k, (*chip, c), sibling).wait_send()
            copy(a, 3, (*xn, c), (*yn, c)).wait_send()
        for cp in own:
            cp.wait_send()
        for cp in local:
            cp.wait()

    return start, finish


def _gather_scratch(n):
    return [pltpu.SemaphoreType.DMA((n, N_DEV - 1)), pltpu.SemaphoreType.DMA((n, N_DEV - 1)), pltpu.SemaphoreType.DMA((n,))]


def _gather_rider(arrays):
    return _Rider(list(arrays), [jax.ShapeDtypeStruct((N_DEV,) + a.shape, a.dtype) for a in arrays],
                  _gather_scratch(len(arrays)), {}, lambda ins, outs, scratch: _gather_copies(ins, outs, *scratch))


def _all_gather(arrays, name):
    n = len(arrays)

    def body(*refs):
        start, finish = _gather_copies(refs[:n], refs[n:2 * n], *refs[2 * n:])
        start()
        finish()

    return pl.pallas_call(
        body, in_specs=[_ANY] * n, out_specs=[_ANY] * n,
        out_shape=[jax.ShapeDtypeStruct((N_DEV,) + a.shape, a.dtype) for a in arrays],
        scratch_shapes=_gather_scratch(n), name=name)(*arrays)


SLAB = 10 * LANES


def _slab_start(blk, nh, cols):
    in_second_half = blk >= N_DEV // 2
    shift = (2 * nh if in_second_half else 0) if isinstance(blk, int) else jnp.where(in_second_half, 2 * nh, 0)
    return (blk * cols - shift) // LANES * LANES


def _pair_rider(dw_rows=None, parts=None, nh=None):
    if dw_rows is not None:
        r, full = dw_rows.shape
        cols = (full - NARROW + 3 * nh) // N_DEV
        out_shapes = [jax.ShapeDtypeStruct((N_CHIP, r, SLAB), dw_rows.dtype), jax.ShapeDtypeStruct((r, NARROW), dw_rows.dtype)]
    else:
        out_shapes = [jax.ShapeDtypeStruct((N_CHIP,) + parts.shape[1:], parts.dtype)]

    def make(ins, outs, scratch):
        send_sems, recv_sems = scratch
        x, y, c = lax.axis_index("x"), lax.axis_index("y"), lax.axis_index("c")
        kw = lambda k: dict(send_sem=send_sems.at[k], recv_sem=recv_sems.at[k], device_id=(x, y, 1 - c), device_id_type=_MESH)
        copies = []
        for q in range(N_CHIP):
            if dw_rows is not None:
                first = pl.multiple_of(_slab_start(2 * q + 1 - c, nh, cols), LANES)
                copies.append(pltpu.make_async_remote_copy(src_ref=ins[0].at[:, pl.ds(first, SLAB)], dst_ref=outs[0].at[q], **kw(q)))
            else:
                copies.append(pltpu.make_async_remote_copy(src_ref=ins[0].at[2 * q + 1 - c], dst_ref=outs[0].at[q], **kw(q)))
        if dw_rows is not None:
            copies.append(pltpu.make_async_remote_copy(src_ref=ins[0].at[:, pl.ds(full - NARROW, NARROW)], dst_ref=outs[1],
                                                       **kw(N_CHIP)))

        def start():
            for cp in copies:
                cp.start()

        def finish():
            for cp in copies:
                cp.wait()

        return start, finish

    return _Rider([dw_rows if dw_rows is not None else parts], out_shapes,
                  [pltpu.SemaphoreType.DMA((N_CHIP + 1,)), pltpu.SemaphoreType.DMA((N_CHIP + 1,))], {}, make)


def _relayout_pair_sum(dwfull, got_slabs, got_tail, core, nh, tr, name):
    d, full = dwfull.shape
    w = nh * HEAD_DIM
    cols = (8 * w + 3 * nh) // N_DEV
    segs = _native_segments(nh)

    def block(f_ref, s_ref, t_ref, q, blk):
        st = _slab_start(blk, nh, cols)
        wide = f_ref[:, st:st + SLAB].astype(F32) + s_ref[q].astype(F32)
        tail = f_ref[:, 8 * w:].astype(F32) + t_ref[...].astype(F32)
        pieces = []
        for s0, s1, t0 in segs:
            lo, hi = max(s0, blk * cols), min(s1, (blk + 1) * cols)
            if lo < hi:
                at = t0 + lo - s0
                pieces.append(tail[:, at - 8 * w:at - 8 * w + hi - lo] if at >= 8 * w else wide[:, at - st:at - st + hi - lo])
        return (pieces[0] if len(pieces) == 1 else jnp.concatenate(pieces, axis=1)).astype(dwfull.dtype)

    def body(core_ref, f_ref, s_ref, t_ref, o_ref):
        for parity in range(2):
            @pl.when(core_ref[0] == parity)
            def _(parity=parity):
                for q in range(N_CHIP):
                    o_ref[q] = block(f_ref, s_ref, t_ref, q, 2 * q + parity)

    return pl.pallas_call(
        body,
        grid_spec=pltpu.PrefetchScalarGridSpec(
            num_scalar_prefetch=1, grid=(d // tr,),
            in_specs=[pl.BlockSpec((tr, full), lambda i, c_ref: (i, 0)), pl.BlockSpec((N_CHIP, tr, SLAB), lambda i, c_ref: (0, i, 0)),
                      pl.BlockSpec((tr, NARROW), lambda i, c_ref: (i, 0))],
            out_specs=pl.BlockSpec((N_CHIP, tr, cols), lambda i, c_ref: (0, i, 0))),
        out_shape=jax.ShapeDtypeStruct((N_CHIP, d, cols), dwfull.dtype), name=name,
        compiler_params=_cp("parallel"))(core, dwfull, got_slabs, got_tail)


def _pair_sum(parts, got, core, tr, name):
    _, r, c = parts.shape

    def body(core_ref, p_ref, g_ref, o_ref):
        o_ref[...] = (p_ref[...].astype(F32) + g_ref[...].astype(F32)).astype(o_ref.dtype)

    return pl.pallas_call(
        body,
        grid_spec=pltpu.PrefetchScalarGridSpec(
            num_scalar_prefetch=1, grid=(N_CHIP, r // tr),
            in_specs=[pl.BlockSpec((1, tr, c), lambda q, i, core_ref: (2 * q + core_ref[0], i, 0)),
                      pl.BlockSpec((1, tr, c), lambda q, i, core_ref: (q, i, 0))],
            out_specs=pl.BlockSpec((1, tr, c), lambda q, i, core_ref: (q, i, 0))),
        out_shape=jax.ShapeDtypeStruct((N_CHIP, r, c), parts.dtype), name=name,
        compiler_params=_cp("parallel", "parallel"))(core, parts, got)


def _native_segments(nh):
    w = nh * HEAD_DIM
    return [(0, 4 * w, 0), (4 * w, 4 * w + 2 * nh, 8 * w), (4 * w + 2 * nh, 8 * w + 2 * nh, 4 * w),
            (8 * w + 2 * nh, 8 * w + 3 * nh, 8 * w + 2 * nh)]


def _relayout_w_in(wg, nh, tr):
    _, d, cols = wg.shape
    w = nh * HEAD_DIM

    def native(ref, j0, j1):
        out = []
        while j0 < j1:
            blk = j0 // cols
            end = min(j1, (blk + 1) * cols)
            out.append(ref[blk, :, pl.ds(j0 - blk * cols, end - j0)])
            j0 = end
        return out

    def body(g_ref, o_ref):
        for cidx in range(8 * w // LANES):
            j0 = cidx * LANES + (0 if cidx * LANES < 4 * w else 2 * nh)
            pieces = native(g_ref, j0, j0 + LANES)
            o_ref[:, cidx * LANES:(cidx + 1) * LANES] = pieces[0] if len(pieces) == 1 else jnp.concatenate(pieces, axis=1)
        pieces = (native(g_ref, 4 * w, 4 * w + 2 * nh) + native(g_ref, 8 * w + 2 * nh, 8 * w + 3 * nh)
                  + [jnp.zeros((tr, NARROW - 3 * nh), wg.dtype)])
        o_ref[:, 8 * w:] = jnp.concatenate(pieces, axis=1)

    return pl.pallas_call(
        body, grid=(d // tr,), in_specs=[pl.BlockSpec((N_DEV, tr, cols), lambda i: (0, i, 0))],
        out_specs=pl.BlockSpec((tr, 8 * w + NARROW), lambda i: (i, 0)),
        out_shape=jax.ShapeDtypeStruct((d, 8 * w + NARROW), wg.dtype),
        name="relayout_w_in", compiler_params=_cp("parallel"))(wg)


def _adamw(w, parts, m, v, tr, name):
    r, c = w.shape
    n_parts = parts.shape[0]

    def body(w_ref, p_ref, m_ref, v_ref, g_ref, d_ref, nm_ref, nv_ref):
        g = p_ref[0].astype(F32)
        for s in range(1, n_parts):
            g = g + p_ref[s].astype(F32)
        m_new = ADAM_B1 * m_ref[...] + (1.0 - ADAM_B1) * g
        v_new = ADAM_B2 * v_ref[...] + (1.0 - ADAM_B2) * (g * g)
        m_hat = m_new / (1.0 - ADAM_B1 ** ADAM_STEP)
        v_hat = v_new / (1.0 - ADAM_B2 ** ADAM_STEP)
        g_ref[...] = g
        d_ref[...] = -ADAM_LR * (m_hat / (jnp.sqrt(v_hat) + ADAM_EPS) + ADAM_WD * w_ref[...])
        nm_ref[...] = m_new
        nv_ref[...] = v_new

    blk = pl.BlockSpec((tr, c), lambda i: (i, 0))
    return pl.pallas_call(
        body, grid=(r // tr,), in_specs=[blk, pl.BlockSpec((n_parts, tr, c), lambda i: (0, i, 0)), blk, blk],
        out_specs=[blk] * 4, out_shape=[jax.ShapeDtypeStruct((r, c), F32)] * 4, name=name,
        compiler_params=_cp("parallel"))(w, parts, m, v)


def _pack_small(d, pre, post, a_log, dt_bias, f_bias, gdn_w, fq_w, fk_w, extra):
    row2 = jnp.concatenate([a_log, dt_bias, f_bias, gdn_w, fq_w, fk_w, extra], axis=1)
    row2 = jnp.pad(row2, ((0, 0), (0, d - row2.shape[1])))
    return jnp.concatenate([pre, post, row2, jnp.zeros((5, d), F32)], axis=0)


def _unpack_small(p, nh):
    o = 3 * nh
    return dict(pre=p[0:1], post=p[1:2], a_log=p[2:3, 0:nh], dt_bias=p[2:3, nh:2 * nh], f_bias=p[2:3, 2 * nh:o],
                gdn_w=p[2:3, o:o + HEAD_DIM], fq_w=p[2:3, o + HEAD_DIM:o + 2 * HEAD_DIM],
                fk_w=p[2:3, o + 2 * HEAD_DIM:o + 3 * HEAD_DIM], extra=p[2, o + 3 * HEAD_DIM])


def kernel(x, meta_tokens, pre_norm_w, w_in, conv_w, a_log, dt_bias, gdn_norm_w, fox_q_norm_w, fox_k_norm_w, fox_f_bias, w_out, post_norm_w, loss_target, m_meta_tokens, m_pre_norm_w, m_w_in, m_conv_w, m_a_log, m_dt_bias, m_gdn_norm_w, m_fox_q_norm_w, m_fox_k_norm_w, m_fox_f_bias, m_w_out, m_post_norm_w, v_meta_tokens, v_pre_norm_w, v_w_in, v_conv_w, v_a_log, v_dt_bias, v_gdn_norm_w, v_fox_q_norm_w, v_fox_k_norm_w, v_fox_f_bias, v_w_out, v_post_norm_w):
    nh = a_log.shape[1]
    d = x.shape[-1]
    w = nh * HEAD_DIM
    zero = jnp.zeros((1, 1), F32)

    wg, mg = _all_gather([_cast_bf16_column_major(w_in, "cast_w_in"), meta_tokens], "gather_weights")
    wfull = _relayout_w_in(wg, nh, 256)
    meta_full = mg.transpose(1, 0, 2).reshape(N_META, d)
    late_weights = (_gather_rider([conv_w[0].T, _cast_bf16(w_out[0], 256, "cast_w_out")]),
                    lambda got: (got[0].transpose(1, 0, 2).reshape(CONV_WIDTH, 3 * w), got[1].reshape(2 * w, d)))
    core = lax.axis_index("c")
    dev = 4 * lax.axis_index("x") + 2 * lax.axis_index("y") + core
    core_arr = jnp.reshape(core, (1,)).astype(jnp.int32)

    out_parts = lambda dw_out: dw_out.reshape(N_DEV, 2 * w // N_DEV, d)
    g = _layer_grads(
        x[0], loss_target[0], meta_full, pre_norm_w, wfull, None, a_log, dt_bias, gdn_norm_w,
        fox_q_norm_w, fox_k_norm_w, fox_f_bias, None, post_norm_w, late_weights=late_weights,
        w_out_grads=(lambda dw_out: _pair_rider(parts=out_parts(dw_out)),
                     lambda dw_out, got: _chip_rider(_pair_sum(out_parts(dw_out), got[0], core_arr, 256, "pair_sum_w_out"))))
    p_out = g["w_out_parts"][0]
    xn, dproj, half = g["xn"], g["dproj"], d // 2
    dw_a = _matmul(xn, dproj, "tn", MM_TILE, BF16, "dw_in_a", a_cols=(half, 0))
    dw_b, got_a = _matmul(xn, dproj, "tn", MM_TILE, BF16, "dw_in_b", _pair_rider(dw_rows=dw_a, nh=nh), a_cols=(half, 1))
    sums_a = _relayout_pair_sum(dw_a, got_a[0], got_a[1], core_arr, nh, 128, "relayout_pair_sum_a")
    dxn, (p_in_a, got_b) = _dxn(dproj, wfull, [_chip_rider(sums_a, rows_total=d), _pair_rider(dw_rows=dw_b, nh=nh)],
                                MM_TILE, "dxn")
    sums_b = _relayout_pair_sum(dw_b, got_b[0], got_b[1], core_arr, nh, 128, "relayout_pair_sum_b")
    (grad_x, dmeta, dpre_w), p_in = _prenorm_bwd(dxn, x[0], meta_full, pre_norm_w, g["dy"],
                                                 _chip_rider(sums_b, rows_total=d, row0=half, into=p_in_a[0]))
    p_in = p_in[0]
    small = _pack_small(d, dpre_w, g["post_w"], g["a_log"], g["dt_bias"], g["f_bias"], g["gdn_norm_w"], g["fq_w"],
                        g["fk_w"], g["loss"])
    a_conv, a_meta, p_small = _all_gather([g["conv_wt"], dmeta, small], "gather_small_grads")
    p_conv = lax.dynamic_slice_in_dim(a_conv, dev * conv_w.shape[1], conv_w.shape[1], axis=2).transpose(0, 2, 1)
    p_meta = lax.dynamic_slice_in_dim(a_meta, dev * meta_tokens.shape[1], meta_tokens.shape[1], axis=2)

    r_in = _adamw_column_major(w_in, p_in, m_w_in, v_w_in, "adamw_w_in")
    r_out = _adamw(w_out[0], p_out, m_w_out[0], v_w_out[0], 64, "adamw_w_out")
    r_conv = _adamw(conv_w[0], p_conv, m_conv_w[0], v_conv_w[0], conv_w.shape[1], "adamw_conv_w")
    r_meta = _adamw(meta_tokens, p_meta, m_meta_tokens, v_meta_tokens, N_META, "adamw_meta")
    pk = lambda pre, post, a, dt, gw, fq, fk, fb: _pack_small(d, pre, post, a, dt, fb, gw, fq, fk, zero)
    r_small = _adamw(
        pk(pre_norm_w, post_norm_w, a_log, dt_bias, gdn_norm_w, fox_q_norm_w, fox_k_norm_w, fox_f_bias), p_small,
        pk(m_pre_norm_w, m_post_norm_w, m_a_log, m_dt_bias, m_gdn_norm_w, m_fox_q_norm_w, m_fox_k_norm_w, m_fox_f_bias),
        pk(v_pre_norm_w, v_post_norm_w, v_a_log, v_dt_bias, v_gdn_norm_w, v_fox_q_norm_w, v_fox_k_norm_w, v_fox_f_bias),
        8, "adamw_small")

    sm = [_unpack_small(r, nh) for r in r_small]
    outs = []
    for i in range(4):
        s = sm[i]
        outs += [r_meta[i], s["pre"], r_in[i], r_conv[i][None], s["a_log"], s["dt_bias"], s["gdn_w"], s["fq_w"],
                 s["fk_w"], s["f_bias"], r_out[i][None], s["post"]]
    return (sm[0]["extra"], grad_x[None], *outs)
```

```python
import jax
import jax.numpy as jnp
from jax import lax
from jax.experimental import pallas as pl
from jax.experimental.pallas import tpu as pltpu

F32, BF16 = jnp.float32, jnp.bfloat16
HEAD_DIM = 128
N_META = 16
CONV_WIDTH = 4
CHUNK = 128
Q_BLOCK = 128
LANES = 128
EPS = 1e-6
PAD_ROWS = Q_BLOCK - N_META
N_DEV = 8
N_CHIP = 4
VMEM_LIMIT = 56 * 1024 * 1024
NEG = -1e30
NARROW = 2 * LANES
MM_TILE = 6 * LANES

ADAM_LR, ADAM_B1, ADAM_B2, ADAM_EPS, ADAM_WD, ADAM_STEP = 0.001, 0.9, 0.999, 1e-08, 0.01, 10

_DN = {"nn": (((1,), (0,)), ((), ())), "nt": (((1,), (1,)), ((), ())), "tn": (((0,), (0,)), ((), ()))}
_DN3 = {"nn": (((2,), (1,)), ((0,), (0,))), "nt": (((2,), (2,)), ((0,), (0,))), "tn": (((1,), (1,)), ((0,), (0,)))}
_ANY = pl.BlockSpec(memory_space=pl.ANY)
_MESH = pl.DeviceIdType.MESH


def _cp(*sem):
    return pltpu.CompilerParams(dimension_semantics=sem, vmem_limit_bytes=VMEM_LIMIT)


def _dot(a, b, dims="nn", prec=None):
    return lax.dot_general(a, b, _DN[dims], precision=prec, preferred_element_type=F32)


def _bdot(a, b, dims="nn"):
    return _dot(a.astype(BF16), b.astype(BF16), dims)


def _hdot(a, b, dims="nn"):
    return _dot(a, b, dims, prec=lax.Precision.HIGHEST)


def _dot3(a, b, dims="nn"):
    return lax.dot_general(a, b, _DN3[dims], preferred_element_type=F32)


def _bdot3(a, b, dims="nn"):
    return _dot3(a.astype(BF16), b.astype(BF16), dims)


def _split(a):
    hi = a.astype(BF16)
    return hi, (a - hi.astype(F32)).astype(BF16)


def _iota(shape, dim):
    return lax.broadcasted_iota(jnp.int32, shape, dim)


def _sigmoid(z):
    return 1.0 / (1.0 + jnp.exp(-z))


def _softplus(z):
    e = jnp.exp(-jnp.abs(z))
    u = 1.0 + e
    l1p = jnp.where(u == 1.0, e, jnp.log(u) * (e / jnp.where(u == 1.0, 1.0, u - 1.0)))
    return jnp.maximum(z, 0.0) + l1p


def _silu_and_grad(z):
    s = _sigmoid(z)
    return z * s, s * (1.0 + z * (1.0 - s))


def _rms(x):
    return lax.rsqrt(jnp.mean(x * x, axis=-1, keepdims=True) + EPS)


def _h_tile(i, x_ref, meta_ref):
    first = jnp.concatenate([jnp.zeros((PAD_ROWS, x_ref.shape[1]), F32), meta_ref[...]], axis=0)
    return jnp.where(i == 0, first, x_ref[...])


def _x_rows(d):
    return pl.BlockSpec((Q_BLOCK, d), lambda i: (jnp.maximum(i - 1, 0), 0))


def _prenorm(x, meta, w):
    seq, d = x.shape
    lp = seq + Q_BLOCK

    def body(x_ref, m_ref, w_ref, o_ref):
        h = _h_tile(pl.program_id(0), x_ref, m_ref)
        o_ref[...] = (h * _rms(h) * w_ref[...]).astype(BF16)

    return pl.pallas_call(
        body, grid=(lp // Q_BLOCK,),
        in_specs=[_x_rows(d), pl.BlockSpec((N_META, d), lambda i: (0, 0)), pl.BlockSpec((1, d), lambda i: (0, 0))],
        out_specs=pl.BlockSpec((Q_BLOCK, d), lambda i: (i, 0)),
        out_shape=jax.ShapeDtypeStruct((lp, d), BF16), name="prenorm", compiler_params=_cp("parallel"))(x, meta, w)


def _tile(n, want):
    return max(t for t in range(LANES, want + 1, LANES) if n % t == 0)


class _Rider:
    def __init__(self, inputs, out_shapes, scratch, aliases, make):
        self.inputs, self.out_shapes, self.scratch, self.aliases, self.make = inputs, out_shapes, scratch, aliases, make


def _hosted_call(body, riders, n_in, n_out, n_scratch, *, in_specs, out_specs, out_shape, scratch_shapes=(), aliases=None,
                 **kw):
    riders = [r for r in riders if r is not None]
    r_in = [len(r.inputs) for r in riders]
    r_out = [len(r.out_shapes) for r in riders]
    r_scr = [len(r.scratch) for r in riders]
    al = dict(aliases or {})
    for k, r in enumerate(riders):
        al.update({n_in + sum(r_in[:k]) + i: n_out + sum(r_out[:k]) + o for i, o in r.aliases.items()})

    def full_body(*refs):
        ins, rest = refs[:n_in + sum(r_in)], refs[n_in + sum(r_in):]
        outs, scr = rest[:n_out + sum(r_out)], rest[n_out + sum(r_out):]
        hooks = [r.make(ins[n_in + sum(r_in[:k]):n_in + sum(r_in[:k + 1])], outs[n_out + sum(r_out[:k]):n_out + sum(r_out[:k + 1])],
                        scr[n_scratch + sum(r_scr[:k]):n_scratch + sum(r_scr[:k + 1])]) for k, r in enumerate(riders)]

        def start():
            for h in hooks:
                h[0]()

        def finish():
            for h in hooks:
                h[1]()

        body(start, finish, *ins[:n_in], *outs[:n_out], *scr[:n_scratch])

    call = pl.pallas_call(
        full_body, in_specs=list(in_specs) + [_ANY] * sum(r_in), out_specs=list(out_specs) + [_ANY] * sum(r_out),
        out_shape=list(out_shape) + [s for r in riders for s in r.out_shapes],
        scratch_shapes=list(scratch_shapes) + [s for r in riders for s in r.scratch], input_output_aliases=al, **kw)

    def run(*args):
        res = call(*args, *[t for r in riders for t in r.inputs])
        return res[:n_out], [res[n_out + sum(r_out[:k]):n_out + sum(r_out[:k + 1])] for k in range(len(riders))]

    return run


def _matmul(a, b, dims, tn, out_dtype, name, rider=None, a_cols=None):
    a_shape = a.shape if a_cols is None else (a.shape[0], a_cols[0])
    a_index = 0 if a_cols is None else a_cols[1]
    m = a_shape[1] if dims == "tn" else a_shape[0]
    n = b.shape[0] if dims == "nt" else b.shape[1]
    kdim = b.shape[1] if dims == "nt" else b.shape[0]
    tn = _tile(n, tn)
    steps = n // tn
    b_spec = pl.BlockSpec((tn, kdim), lambda j: (j, 0)) if dims == "nt" else pl.BlockSpec((kdim, tn), lambda j: (0, j))

    def body(start, finish, a_ref, b_ref, o_ref):
        pl.when(pl.program_id(0) == 0)(start)
        o_ref[...] = _dot(a_ref[...], b_ref[...], dims).astype(out_dtype)
        pl.when(pl.program_id(0) == steps - 1)(finish)

    (out,), got = _hosted_call(
        body, [rider], 2, 1, 0, grid=(steps,), in_specs=[pl.BlockSpec(a_shape, lambda j: (0, a_index)), b_spec],
        out_specs=[pl.BlockSpec((m, tn), lambda j: (0, j))], out_shape=[jax.ShapeDtypeStruct((m, n), out_dtype)],
        name=name, compiler_params=_cp("parallel" if rider is None else "arbitrary"))(a, b)
    return out if rider is None else (out, got[0])


def _chip_rider(sums, rows_total=None, row0=0, into=None):
    _, r, c = sums.shape
    rows_total = rows_total or r

    def make(ins, outs, scratch):
        send_sems, recv_sems, local_sem = scratch
        x, y, core = lax.axis_index("x"), lax.axis_index("y"), lax.axis_index("c")
        mine = 2 * x + y
        land = lambda chip: outs[0].at[chip].at[pl.ds(row0, r)]
        local = pltpu.make_async_copy(ins[0].at[mine], land(mine), local_sem)
        sends, recvs = [], []
        for k in range(1, N_CHIP):
            px = 1 - x if k & 2 else x
            py = 1 - y if k & 1 else y
            kw = dict(send_sem=send_sems.at[k - 1], recv_sem=recv_sems.at[k - 1], device_id=(px, py, core),
                      device_id_type=_MESH)
            sends.append(pltpu.make_async_remote_copy(src_ref=ins[0].at[2 * px + py], dst_ref=land(mine), **kw))
            recvs.append(pltpu.make_async_remote_copy(src_ref=ins[0].at[mine], dst_ref=land(2 * px + py), **kw))

        def start():
            for cp in [local] + sends:
                cp.start()

        def finish():
            local.wait()
            for cp in sends:
                cp.wait_send()
            for cp in recvs:
                cp.wait_recv()

        return start, finish

    return _Rider([sums] + ([] if into is None else [into]), [jax.ShapeDtypeStruct((N_CHIP, rows_total, c), sums.dtype)],
                  [pltpu.SemaphoreType.DMA((N_CHIP - 1,)), pltpu.SemaphoreType.DMA((N_CHIP - 1,)), pltpu.SemaphoreType.DMA(())],
                  {} if into is None else {1: 0}, make)


def _dxn(dproj, wfull, riders, tk, name):
    m, k = dproj.shape
    n = wfull.shape[0]
    tk = _tile(k, tk)
    steps = k // tk

    def body(start, finish, a_ref, b_ref, o_ref):
        j = pl.program_id(0)

        @pl.when(j == 0)
        def _():
            start()
            o_ref[...] = jnp.zeros_like(o_ref)
        o_ref[...] += _dot(a_ref[...], b_ref[...], "nt")
        pl.when(j == steps - 1)(finish)

    (dxn,), got = _hosted_call(
        body, riders, 2, 1, 0, grid=(steps,),
        in_specs=[pl.BlockSpec((m, tk), lambda j: (0, j)), pl.BlockSpec((n, tk), lambda j: (0, j))],
        out_specs=[pl.BlockSpec((m, n), lambda j: (0, 0))], out_shape=[jax.ShapeDtypeStruct((m, n), F32)],
        name=name, compiler_params=_cp("arbitrary"))(dproj, wfull)
    return dxn, got


def _conv_taps(x, w):
    c = x * w[CONV_WIDTH - 1:CONV_WIDTH, :]
    for j in range(CONV_WIDTH - 1):
        c = c + pltpu.roll(x, CONV_WIDTH - 1 - j, 0) * w[j:j + 1, :]
    return c


def _gdn_prep(proj, conv_wt, nh):
    lp = proj.shape[0]
    scale = HEAD_DIM ** -0.5

    def body(x_ref, w_ref, o_ref):
        which = pl.program_id(0) // nh
        c = _conv_taps(x_ref[...], w_ref[...])
        s = c * _sigmoid(c)
        r = lax.rsqrt(jnp.sum(s * s, axis=-1, keepdims=True) + EPS)
        f = jnp.where(which == 0, r * scale, jnp.where(which == 1, r, 1.0))
        o_ref[...] = jnp.where(_iota(s.shape, 0) >= PAD_ROWS, s * f, 0.0)

    return pl.pallas_call(
        body, grid=(3 * nh,),
        in_specs=[pl.BlockSpec((lp, LANES), lambda s: (0, s)), pl.BlockSpec((CONV_WIDTH, LANES), lambda s: (0, s))],
        out_specs=pl.BlockSpec((lp, LANES), lambda s: (0, s)),
        out_shape=jax.ShapeDtypeStruct((lp, 3 * nh * HEAD_DIM), F32), name="gdn_prep",
        compiler_params=_cp("parallel"))(proj, conv_wt)


def _gdn_prep_bwd(proj, conv_wt, dq, dk, dv, dproj, nh):
    lp = proj.shape[0]
    scale = HEAD_DIM ** -0.5
    part = lambda p: pl.BlockSpec((lp, LANES), lambda s: (0, jnp.clip(s - p * nh, 0, nh - 1)))

    def body(x_ref, w_ref, dq_ref, dk_ref, dv_ref, _, dx_ref, dw_ref):
        which = pl.program_id(0) // nh
        x = x_ref[...]
        w = w_ref[...]
        c = _conv_taps(x, w)
        sg = _sigmoid(c)
        s = c * sg
        r = lax.rsqrt(jnp.sum(s * s, axis=-1, keepdims=True) + EPS)
        dy = jnp.where(which == 0, dq_ref[...], jnp.where(which == 1, dk_ref[...], dv_ref[...]))
        dy = jnp.where(_iota(s.shape, 0) >= PAD_ROWS, dy, 0.0)
        y0 = s * r
        dy0 = dy * jnp.where(which == 0, scale, 1.0)
        ds_n = r * (dy0 - y0 * jnp.sum(dy0 * y0, axis=-1, keepdims=True))
        ds = jnp.where(which == 2, dy, ds_n)
        dc = ds * (sg * (1.0 + c * (1.0 - sg)))
        dx = dc * w[CONV_WIDTH - 1:CONV_WIDTH, :]
        rows = [jnp.sum(dc * x, axis=0, keepdims=True)]
        for j in range(CONV_WIDTH - 2, -1, -1):
            sh = CONV_WIDTH - 1 - j
            dx = dx + pltpu.roll(dc, lp - sh, 0) * w[j:j + 1, :]
            rows.insert(0, jnp.sum(dc * pltpu.roll(x, sh, 0), axis=0, keepdims=True))
        dx_ref[...] = dx.astype(BF16)
        dw_ref[...] = jnp.concatenate(rows, axis=0)

    strip = pl.BlockSpec((lp, LANES), lambda s: (0, s))
    taps = pl.BlockSpec((CONV_WIDTH, LANES), lambda s: (0, s))
    return pl.pallas_call(
        body, grid=(3 * nh,), in_specs=[strip, taps, part(0), part(1), part(2), _ANY], out_specs=[strip, taps],
        out_shape=[jax.ShapeDtypeStruct(dproj.shape, BF16), jax.ShapeDtypeStruct((CONV_WIDTH, 3 * nh * HEAD_DIM), F32)],
        input_output_aliases={5: 0}, name="gdn_prep_bwd", compiler_params=_cp("parallel"))(proj, conv_wt, dq, dk, dv, dproj)


def _gates(proj, bias_row, nega_row, nh):
    lp = proj.shape[0]
    nc = lp // CHUNK

    def body(p_ref, b_ref, a_ref, g_ref, gt3_ref, gtf_ref):
        lane = _iota((CHUNK, LANES), 1)
        tri = (_iota((CHUNK, CHUNK), 0) >= _iota((CHUNK, CHUNK), 1)).astype(F32)

        def step(n, carry):
            r0 = pl.multiple_of(n * CHUNK, CHUNK)
            z = p_ref[pl.ds(r0, CHUNK), :] + b_ref[...]
            base = jnp.where(lane < nh, _sigmoid(z),
                             jnp.where(lane < 2 * nh, a_ref[...] * _softplus(z),
                                       jnp.where(lane < 3 * nh, -_softplus(-z), 0.0)))
            base = jnp.where(r0 + _iota((CHUNK, LANES), 0) >= PAD_ROWS, base, 0.0)
            cs = _hdot(tri, base)
            run = jnp.where((lane >= 2 * nh) & (lane < 3 * nh), cs + carry, cs)
            sh = pltpu.roll(run, 2 * nh, 1)
            out = base + jnp.where((lane >= 3 * nh) & (lane < 5 * nh), sh, 0.0)
            g_ref[pl.ds(r0, CHUNK), :] = out
            gt3_ref[n] = out.T
            return carry + cs[CHUNK - 1:CHUNK, :]

        lax.fori_loop(0, nc, step, jnp.zeros((1, LANES), F32))
        gtf_ref[...] = g_ref[...].T

    vec = pl.BlockSpec((1, LANES), lambda i: (0, 0))
    return pl.pallas_call(
        body, grid=(1,), in_specs=[pl.BlockSpec((lp, LANES), lambda i: (0, 8 * nh)), vec, vec],
        out_specs=[pl.BlockSpec((lp, LANES), lambda i: (0, 0)), pl.BlockSpec((nc, LANES, CHUNK), lambda i: (0, 0, 0)),
                   pl.BlockSpec((LANES, lp), lambda i: (0, 0))],
        out_shape=[jax.ShapeDtypeStruct((lp, LANES), F32), jax.ShapeDtypeStruct((nc, LANES, CHUNK), F32),
                   jax.ShapeDtypeStruct((LANES, lp), F32)],
        name="gates", compiler_params=_cp("arbitrary"))(proj, bias_row, nega_row)


def _gates_bwd(proj, bias_row, nega_row, gates, dgate_gdn, dc_t, dproj, nh):
    lp = proj.shape[0]
    nc = lp // CHUNK

    def body(p_ref, b_ref, a_ref, g_ref, dg_ref, dc_ref, _, dz_ref, sm_ref, dct_scr):
        lane = _iota((CHUNK, LANES), 1)
        triu = (_iota((CHUNK, CHUNK), 0) <= _iota((CHUNK, CHUNK), 1)).astype(F32)
        dct_scr[...] = dc_ref[...].T
        sm_ref[...] = jnp.zeros_like(sm_ref)
        dz_ref[:, LANES:] = jnp.zeros((lp, NARROW - LANES), BF16)

        def step(i, carry):
            n = nc - 1 - i
            r0 = pl.multiple_of(n * CHUNK, CHUNK)
            z = p_ref[pl.ds(r0, CHUNK), :] + b_ref[...]
            gt = g_ref[pl.ds(r0, CHUNK), :]
            dgd = dg_ref[pl.ds(r0, CHUNK), :]
            dch = dct_scr[pl.ds(r0, CHUNK), :]
            rc = _hdot(triu, dch) + carry
            sg = _sigmoid(z)
            dz = jnp.where(lane < nh, dgd * sg * (1.0 - sg),
                           jnp.where(lane < 2 * nh, dgd * a_ref[...] * sg,
                                     jnp.where(lane < 3 * nh, rc * (1.0 - sg), 0.0)))
            dz = jnp.where(r0 + _iota((CHUNK, LANES), 0) >= PAD_ROWS, dz, 0.0)
            dz_ref[pl.ds(r0, CHUNK), 0:LANES] = dz.astype(BF16)
            sm_ref[0:1, :] += jnp.sum(dz, axis=0, keepdims=True)
            sm_ref[1:2, :] += jnp.sum(jnp.where((lane >= nh) & (lane < 2 * nh), dgd * gt, 0.0), axis=0, keepdims=True)
            return carry + jnp.sum(dch, axis=0, keepdims=True)

        lax.fori_loop(0, nc, step, jnp.zeros((1, LANES), F32))

    vec = pl.BlockSpec((1, LANES), lambda i: (0, 0))
    full = pl.BlockSpec((lp, LANES), lambda i: (0, 0))
    last = pl.BlockSpec((lp, LANES), lambda i: (0, 8 * nh))
    tail = pl.BlockSpec((lp, NARROW), lambda i: (0, 8 * nh * LANES // NARROW))
    return pl.pallas_call(
        body, grid=(1,), in_specs=[last, vec, vec, full, full, pl.BlockSpec((LANES, lp), lambda i: (0, 0)), _ANY],
        out_specs=[tail, pl.BlockSpec((8, LANES), lambda i: (0, 0))],
        out_shape=[jax.ShapeDtypeStruct(dproj.shape, BF16), jax.ShapeDtypeStruct((8, LANES), F32)],
        scratch_shapes=[pltpu.VMEM((lp, LANES), F32)], input_output_aliases={6: 0},
        name="gates_bwd", compiler_params=_cp("arbitrary"))(proj, bias_row, nega_row, gates, dgate_gdn, dc_t, dproj)


def _tri_inv(a):
    t = jnp.where(_iota(a.shape, 1) == _iota(a.shape, 2), 1.0, 0.0) - a
    p = a
    for _ in range(CHUNK.bit_length() - 2):
        ph, pw = _split(p)
        p = _dot3(ph, ph) + (_dot3(ph, pw) + _dot3(pw, ph))
        ph, pw = _split(p)
        th, tw = _split(t)
        t = t + (_dot3(th, ph) + (_dot3(th, pw) + _dot3(tw, ph)))
    return t


def _gdn_chunk(q, k, v, beta, gc, gr, t=None):
    ii, jj = _iota((1, CHUNK, CHUNK), 1), _iota((1, CHUNK, CHUNK), 2)
    causal, strict = ii >= jj, ii > jj
    dm = jnp.where(causal, jnp.exp(jnp.where(causal, gc - gr, 0.0)), 0.0)
    kk = _bdot3(k, k, "nt")
    a = jnp.where(strict, beta * kk * dm, 0.0)
    if t is None:
        t = _tri_inv(a)
    eg = jnp.exp(gc)
    glast = gc[:, CHUNK - 1:CHUNK, :]
    ekd = jnp.exp(glast - gc)
    bv = beta * v
    bk = (beta * eg) * k
    ub = _bdot3(t, jnp.concatenate([bv, bk], axis=2))
    qk = _bdot3(q, k, "nt")
    return dict(causal=causal, strict=strict, dm=dm, kk=kk, a=a, t=t, eg=eg, ekd=ekd, bv=bv, bk=bk,
                u=ub[:, :, :HEAD_DIM], w=ub[:, :, HEAD_DIM:], qk=qk, aqk=jnp.where(causal, qk * dm, 0.0),
                q_dec=q * eg, k_dec=k * ekd, decay=jnp.exp(glast))


def _heads(ref, nh):
    return jnp.stack([ref[:, h * HEAD_DIM:(h + 1) * HEAD_DIM] for h in range(nh)], axis=0)


def _gdn_chunk_inputs(q_ref, k_ref, v_ref, g, gt, nh):
    col = lambda o: jnp.stack([g[:, o + h:o + h + 1] for h in range(nh)], axis=0)
    gr = jnp.stack([gt[3 * nh + h:3 * nh + h + 1, :] for h in range(nh)], axis=0)
    return _heads(q_ref, nh), _heads(k_ref, nh), _heads(v_ref, nh), col(0), col(3 * nh), gr


def _gdn_fwd(qkv, gates, gt3, nh, rider=None):
    lp = qkv.shape[0]
    nc = lp // CHUNK
    w = nh * HEAD_DIM

    def body(start, finish, q_ref, k_ref, v_ref, g_ref, gt_ref, o_ref, sall_ref, tall_ref, s_scr):
        @pl.when(pl.program_id(0) == 0)
        def _():
            start()
            s_scr[...] = jnp.zeros_like(s_scr)
        c = _gdn_chunk(*_gdn_chunk_inputs(q_ref, k_ref, v_ref, g_ref[...], gt_ref[0], nh))
        s = s_scr[...]
        sall_ref[0] = s
        tall_ref[0] = c["t"]
        v_new = c["u"] - _bdot3(c["w"], s)
        o = _bdot3(c["q_dec"], s) + _bdot3(c["aqk"], v_new)
        s_scr[...] = s * c["decay"] + _bdot3(c["k_dec"], v_new, "tn")
        for h in range(nh):
            o_ref[:, h * HEAD_DIM:(h + 1) * HEAD_DIM] = o[h]
        pl.when(pl.program_id(0) == nc - 1)(finish)

    outs, got = _hosted_call(
        body, [rider], 5, 3, 1, grid=(nc,),
        in_specs=[pl.BlockSpec((CHUNK, w), lambda n: (n, 0)), pl.BlockSpec((CHUNK, w), lambda n: (n, 1)),
                  pl.BlockSpec((CHUNK, w), lambda n: (n, 2)), pl.BlockSpec((CHUNK, LANES), lambda n: (n, 0)),
                  pl.BlockSpec((1, LANES, CHUNK), lambda n: (n, 0, 0))],
        out_specs=[pl.BlockSpec((CHUNK, w), lambda n: (n, 0)),
                   pl.BlockSpec((1, nh, HEAD_DIM, HEAD_DIM), lambda n: (n, 0, 0, 0)),
                   pl.BlockSpec((1, nh, CHUNK, CHUNK), lambda n: (n, 0, 0, 0))],
        out_shape=[jax.ShapeDtypeStruct((lp, w), F32), jax.ShapeDtypeStruct((nc, nh, HEAD_DIM, HEAD_DIM), F32),
                   jax.ShapeDtypeStruct((nc, nh, CHUNK, CHUNK), F32)],
        scratch_shapes=[pltpu.VMEM((nh, HEAD_DIM, HEAD_DIM), F32)],
        name="gdn_fwd", compiler_params=_cp("arbitrary"))(qkv, qkv, qkv, gates, gt3)
    return outs, (got[0] if got else None)


def _gdn_bwd(qkv, gates, gt3, s_all, t_all, do, nh, rider=None):
    lp = qkv.shape[0]
    nc = lp // CHUNK
    w = nh * HEAD_DIM
    rev = lambda n: nc - 1 - n

    def body(start, finish, q_ref, k_ref, v_ref, g_ref, gt_ref, s_ref, t_ref, do_ref, dq_ref, dk_ref, dv_ref, dg_ref, ds_scr):
        @pl.when(pl.program_id(0) == 0)
        def _():
            start()
            ds_scr[...] = jnp.zeros_like(ds_scr)
        q, k, v, beta, gc, gr = _gdn_chunk_inputs(q_ref, k_ref, v_ref, g_ref[...], gt_ref[0], nh)
        c = _gdn_chunk(q, k, v, beta, gc, gr, t_ref[0])
        s = s_ref[0]
        dsn = ds_scr[...]
        dout = _heads(do_ref, nh)
        v_new = c["u"] - _bdot3(c["w"], s)
        dq_dec = _bdot3(dout, s, "nt")
        daqk = jnp.where(c["causal"], _bdot3(dout, v_new, "nt"), 0.0)
        dv_new = _bdot3(c["aqk"], dout, "tn") + _bdot3(c["k_dec"], dsn)
        dk_dec = _bdot3(v_new, dsn, "nt")
        ddecay = jnp.sum(jnp.sum(dsn * s, axis=2, keepdims=True), axis=1, keepdims=True)
        dw = -_bdot3(dv_new, s, "nt")
        ds_scr[...] = _bdot3(c["q_dec"], dout, "tn") + c["decay"] * dsn - _bdot3(c["w"], dv_new, "tn")
        duw = jnp.concatenate([dv_new, dw], axis=2)
        dt = _bdot3(duw, jnp.concatenate([c["bv"], c["bk"]], axis=2), "nt")
        dbvk = _bdot3(c["t"], duw, "tn")
        dbv, dbk = dbvk[:, :, :HEAD_DIM], dbvk[:, :, HEAD_DIM:]
        da = jnp.where(c["strict"], -_bdot3(_bdot3(c["t"], dt, "tn"), c["t"], "nt"), 0.0)
        dkk = da * beta * c["dm"]
        dqk = daqk * c["dm"]
        e = da * c["a"] + daqk * c["aqk"]
        dq = dq_dec * c["eg"] + _bdot3(dqk, k)
        dk = (dk_dec * c["ekd"] + _bdot3(dkk, k) + _bdot3(dkk, k, "tn") + _bdot3(dqk, q, "tn")
              + (beta * c["eg"]) * dbk)
        dv = beta * dbv
        rs = lambda x: jnp.sum(x, axis=2, keepdims=True)
        dbeta = rs(dbv * v) + c["eg"] * rs(dbk * k) + rs(da * c["kk"] * c["dm"])
        kd_term = rs(dk_dec * c["k_dec"])
        eh, ew = _split(e)
        ones = jnp.ones((nh, CHUNK, LANES), BF16)
        col_sums = (_dot3(eh, ones, "tn") + _dot3(ew, ones, "tn"))[:, :, 0:1]
        dg_cum = rs(dq_dec * c["q_dec"]) - kd_term + rs(dbk * c["bk"]) + rs(e) - col_sums
        last = jnp.sum(kd_term, axis=1, keepdims=True) + ddecay * c["decay"]
        dg_cum = dg_cum + jnp.where(_iota((1, CHUNK, 1), 1) == CHUNK - 1, last, 0.0)
        lane = _iota((CHUNK, LANES), 1)
        acc = jnp.zeros((CHUNK, LANES), F32)
        for h in range(nh):
            sl = slice(h * HEAD_DIM, (h + 1) * HEAD_DIM)
            dq_ref[:, sl] = dq[h]
            dk_ref[:, sl] = dk[h]
            dv_ref[:, sl] = dv[h]
            acc = acc + jnp.where(lane == h, dbeta[h], 0.0) + jnp.where(lane == nh + h, dg_cum[h], 0.0)
        triu = (_iota((CHUNK, CHUNK), 0) <= _iota((CHUNK, CHUNK), 1)).astype(F32)
        dg_ref[...] = jnp.where(lane < nh, acc, _hdot(triu, acc))
        pl.when(pl.program_id(0) == nc - 1)(finish)

    outs, got = _hosted_call(
        body, [rider], 8, 4, 1, grid=(nc,),
        in_specs=[pl.BlockSpec((CHUNK, w), lambda n: (rev(n), 0)), pl.BlockSpec((CHUNK, w), lambda n: (rev(n), 1)),
                  pl.BlockSpec((CHUNK, w), lambda n: (rev(n), 2)), pl.BlockSpec((CHUNK, LANES), lambda n: (rev(n), 0)),
                  pl.BlockSpec((1, LANES, CHUNK), lambda n: (rev(n), 0, 0)),
                  pl.BlockSpec((1, nh, HEAD_DIM, HEAD_DIM), lambda n: (rev(n), 0, 0, 0)),
                  pl.BlockSpec((1, nh, CHUNK, CHUNK), lambda n: (rev(n), 0, 0, 0)),
                  pl.BlockSpec((CHUNK, w), lambda n: (rev(n), 0))],
        out_specs=[pl.BlockSpec((CHUNK, w), lambda n: (rev(n), 0))] * 3 + [pl.BlockSpec((CHUNK, LANES), lambda n: (rev(n), 0))],
        out_shape=[jax.ShapeDtypeStruct((lp, w), F32)] * 3 + [jax.ShapeDtypeStruct((lp, LANES), F32)],
        scratch_shapes=[pltpu.VMEM((nh, HEAD_DIM, HEAD_DIM), F32)],
        name="gdn_bwd", compiler_params=_cp("arbitrary"))(qkv, qkv, qkv, gates, gt3, s_all, t_all, do)
    return outs, (got[0] if got else None)


def _merge_gdn(o_gdn, proj, norm_w, nh):
    lp = o_gdn.shape[0]

    def body(o_ref, z_ref, w_ref, m_ref):
        o = o_ref[...]
        z = z_ref[...]
        m_ref[...] = (o * _rms(o) * w_ref[...] * (z * _sigmoid(z))).astype(BF16)

    return pl.pallas_call(
        body, grid=(nh,),
        in_specs=[pl.BlockSpec((lp, LANES), lambda s: (0, s)), pl.BlockSpec((lp, LANES), lambda s: (0, 3 * nh + s)),
                  pl.BlockSpec((1, LANES), lambda s: (0, 0))],
        out_specs=pl.BlockSpec((lp, LANES), lambda s: (0, s)),
        out_shape=jax.ShapeDtypeStruct((lp, 2 * nh * HEAD_DIM), BF16), name="merge_gdn",
        compiler_params=_cp("parallel"))(o_gdn, proj, norm_w)


def _merge_gdn_bwd(o_gdn, proj, norm_w, dmerged, nh):
    lp = o_gdn.shape[0]

    def body(o_ref, z_ref, w_ref, dm_ref, do_ref, dz_ref, dw_ref):
        o = o_ref[...]
        r = _rms(o)
        xh = o * r
        silu, dsilu = _silu_and_grad(z_ref[...])
        dm = dm_ref[...]
        dn = dm * silu
        dz_ref[...] = (dm * (xh * w_ref[...]) * dsilu).astype(BF16)
        dnw = dn * w_ref[...]
        do_ref[...] = r * (dnw - xh * jnp.mean(dnw * xh, axis=-1, keepdims=True))

        @pl.when(pl.program_id(0) == 0)
        def _():
            dw_ref[...] = jnp.zeros_like(dw_ref)
        dw_ref[...] += jnp.sum(dn * xh, axis=0, keepdims=True)

    w = nh * HEAD_DIM
    return pl.pallas_call(
        body, grid=(nh,),
        in_specs=[pl.BlockSpec((lp, LANES), lambda s: (0, s)), pl.BlockSpec((lp, LANES), lambda s: (0, 3 * nh + s)),
                  pl.BlockSpec((1, LANES), lambda s: (0, 0)), pl.BlockSpec((lp, LANES), lambda s: (0, s))],
        out_specs=[pl.BlockSpec((lp, LANES), lambda s: (0, s)), pl.BlockSpec((lp, LANES), lambda s: (0, 3 * nh + s)),
                   pl.BlockSpec((1, LANES), lambda s: (0, 0))],
        out_shape=[jax.ShapeDtypeStruct((lp, w), F32), jax.ShapeDtypeStruct((lp, 8 * w + NARROW), BF16),
                   jax.ShapeDtypeStruct((1, LANES), F32)],
        name="merge_gdn_bwd", compiler_params=_cp("arbitrary"))(o_gdn, proj, norm_w, dmerged)


def _fox_prep(proj, qk_w, nh):
    lp = proj.shape[0]

    def body(x_ref, w_ref, o_ref):
        x = x_ref[...]
        o_ref[...] = x * _rms(x) * w_ref[0]

    return pl.pallas_call(
        body, grid=(2 * nh,),
        in_specs=[pl.BlockSpec((lp, LANES), lambda s: (0, 4 * nh + s)), pl.BlockSpec((1, 1, LANES), lambda s: (s // nh, 0, 0))],
        out_specs=pl.BlockSpec((lp, LANES), lambda s: (0, s)),
        out_shape=jax.ShapeDtypeStruct((lp, 2 * nh * HEAD_DIM), F32), name="fox_prep",
        compiler_params=_cp("parallel"))(proj, qk_w)


def _fox_prep_bwd(proj, qk_w, dq, dk, dproj, nh):
    lp = proj.shape[0]
    part = lambda p: pl.BlockSpec((lp, LANES), lambda s: (0, jnp.clip(s - p * nh, 0, nh - 1)))

    def body(x_ref, w_ref, dq_ref, dk_ref, _, dx_ref, dw_ref):
        x = x_ref[...]
        r = _rms(x)
        xh = x * r
        dy = jnp.where(pl.program_id(0) < nh, dq_ref[...], dk_ref[...])
        dyw = dy * w_ref[0]
        dx_ref[...] = (r * (dyw - xh * jnp.mean(dyw * xh, axis=-1, keepdims=True))).astype(BF16)

        @pl.when(pl.program_id(0) % nh == 0)
        def _():
            dw_ref[...] = jnp.zeros_like(dw_ref)
        dw_ref[0] += jnp.sum(dy * xh, axis=0, keepdims=True)

    strip = pl.BlockSpec((lp, LANES), lambda s: (0, 4 * nh + s))
    wsp = pl.BlockSpec((1, 1, LANES), lambda s: (s // nh, 0, 0))
    return pl.pallas_call(
        body, grid=(2 * nh,), in_specs=[strip, wsp, part(0), part(1), _ANY], out_specs=[strip, wsp],
        out_shape=[jax.ShapeDtypeStruct(dproj.shape, BF16), jax.ShapeDtypeStruct((2, 1, LANES), F32)],
        input_output_aliases={4: 0}, name="fox_prep_bwd", compiler_params=_cp("arbitrary"))(proj, qk_w, dq, dk, dproj)


def _fox_probs(q, k, gates, crow, h, i, nh, lse=None):
    kl = k.shape[0]
    lane = _iota((Q_BLOCK, LANES), 1)
    ct = jnp.sum(jnp.where(lane == 4 * nh + h, gates, 0.0), axis=1, keepdims=True)
    tq, kq = _iota((Q_BLOCK, Q_BLOCK), 0), _iota((Q_BLOCK, Q_BLOCK), 1)
    qs = q * (HEAD_DIM ** -0.5)
    if i == 0:
        s = _bdot(qs, k, "nt") + (ct - crow)
        s = jnp.where((kq <= tq) & ((kq >= PAD_ROWS) | (tq < PAD_ROWS)), s, NEG)
    else:
        crow = jnp.where(_iota((1, kl), 1) < PAD_ROWS, -NEG, crow)
        s = _bdot(qs, k, "nt") + (ct - crow)
        s = jnp.concatenate([s[:, :kl - Q_BLOCK], jnp.where(kq <= tq, s[:, kl - Q_BLOCK:], NEG)], axis=1)
    if lse is not None:
        return jnp.exp(s - lse)
    m = jnp.max(s, axis=1, keepdims=True)
    p = jnp.exp(s - m)
    tot = jnp.sum(p, axis=1, keepdims=True)
    return p / tot, m + jnp.log(tot)


FOX_HEADS_PER_STEP = 4


def _fox_specs(lp, nh):
    hw = FOX_HEADS_PER_STEP * LANES
    return [pl.BlockSpec((Q_BLOCK, hw), lambda g, i: (i, g)),
            pl.BlockSpec((lp, hw), lambda g, i: (0, nh // FOX_HEADS_PER_STEP + g)),
            pl.BlockSpec((lp, hw), lambda g, i: (0, 6 * nh // FOX_HEADS_PER_STEP + g)),
            pl.BlockSpec((Q_BLOCK, LANES), lambda g, i: (i, 0)),
            pl.BlockSpec((LANES, lp), lambda g, i: (0, 0))]


def _fox_fwd(qkn, proj, gates, gtf, nh):
    lp = qkn.shape[0]

    def body(q_ref, k_ref, v_ref, g_ref, gt_ref, o_ref, lse_ref):
        g, i = pl.program_id(0), pl.program_id(1)
        for j in range(lp // Q_BLOCK):
            @pl.when(i == j)
            def _(j=j):
                kl = (j + 1) * Q_BLOCK
                for hh in range(FOX_HEADS_PER_STEP):
                    h = FOX_HEADS_PER_STEP * g + hh
                    sl = slice(hh * LANES, (hh + 1) * LANES)
                    p, lse = _fox_probs(q_ref[:, sl], k_ref[0:kl, sl], g_ref[...], gt_ref[pl.ds(4 * nh + h, 1), :][:, 0:kl],
                                        h, j, nh)
                    o_ref[:, sl] = _bdot(p, v_ref[0:kl, sl])
                    lse_ref[:, sl] = jnp.broadcast_to(lse, (Q_BLOCK, LANES))

    blk = pl.BlockSpec((Q_BLOCK, FOX_HEADS_PER_STEP * LANES), lambda g, i: (i, g))
    return pl.pallas_call(
        body, grid=(nh // FOX_HEADS_PER_STEP, lp // Q_BLOCK), in_specs=_fox_specs(lp, nh), out_specs=[blk, blk],
        out_shape=[jax.ShapeDtypeStruct((lp, nh * HEAD_DIM), F32)] * 2, name="fox_fwd",
        compiler_params=_cp("parallel", "parallel"))(qkn, qkn, proj, gates, gtf)


def _fox_bwd(qkn, proj, gates, gtf, lse, do, dproj, nh):
    lp = qkn.shape[0]
    nq = lp // Q_BLOCK
    w = nh * HEAD_DIM
    scale = HEAD_DIM ** -0.5

    def body(q_ref, k_ref, v_ref, g_ref, gt_ref, lse_ref, do_ref, _, dq_ref, dk_ref, dc_ref, dv_ref, dv_scr):
        g, i = pl.program_id(0), pl.program_id(1)

        @pl.when(i == 0)
        def _():
            dk_ref[...] = jnp.zeros_like(dk_ref)
            dv_scr[...] = jnp.zeros_like(dv_scr)
            dc_ref[...] = jnp.zeros_like(dc_ref)
        for j in range(nq):
            @pl.when(i == j)
            def _(j=j):
                kl = (j + 1) * Q_BLOCK
                for hh in range(FOX_HEADS_PER_STEP):
                    h = FOX_HEADS_PER_STEP * g + hh
                    sl = slice(hh * LANES, (hh + 1) * LANES)
                    q, k = q_ref[:, sl], k_ref[0:kl, sl]
                    p = _fox_probs(q, k, g_ref[...], gt_ref[pl.ds(4 * nh + h, 1), :][:, 0:kl], h, j, nh,
                                   lse_ref[:, sl][:, 0:1])
                    dout = do_ref[:, sl]
                    dp = _bdot(dout, v_ref[0:kl, sl], "nt")
                    ds = p * (dp - jnp.sum(p * dp, axis=1, keepdims=True))
                    dq_ref[:, sl] = _bdot(ds, k) * scale
                    dk_ref[0:kl, sl] += _bdot(ds, q * scale, "tn")
                    dv_scr[0:kl, sl] += _bdot(p, dout, "tn")
                    dc_ref[hh, :, 0:kl] -= jnp.sum(ds, axis=0, keepdims=True)

        @pl.when(i == nq - 1)
        def _():
            dv_ref[...] = dv_scr[...].astype(BF16)

    hw = FOX_HEADS_PER_STEP * LANES
    blk = pl.BlockSpec((Q_BLOCK, hw), lambda g, i: (i, g))
    col = pl.BlockSpec((lp, hw), lambda g, i: (0, g))
    return pl.pallas_call(
        body, grid=(nh // FOX_HEADS_PER_STEP, nq), in_specs=_fox_specs(lp, nh) + [blk, blk, _ANY],
        out_specs=[blk, col, pl.BlockSpec((FOX_HEADS_PER_STEP, 1, lp), lambda g, i: (g, 0, 0)),
                   pl.BlockSpec((lp, hw), lambda g, i: (0, 6 * nh // FOX_HEADS_PER_STEP + g))],
        out_shape=[jax.ShapeDtypeStruct((lp, w), F32)] * 2 + [jax.ShapeDtypeStruct((nh, 1, lp), F32),
                                                             jax.ShapeDtypeStruct(dproj.shape, BF16)],
        scratch_shapes=[pltpu.VMEM((lp, hw), F32)], input_output_aliases={7: 3},
        name="fox_bwd", compiler_params=_cp("parallel", "arbitrary"))(qkn, qkn, proj, gates, gtf, lse, do, dproj)


def _merge_fox(o_fox, proj, merged, nh):
    lp = o_fox.shape[0]

    def body(o_ref, z_ref, _, m_ref):
        z = z_ref[...]
        m_ref[...] = (o_ref[...] * (z * _sigmoid(z))).astype(BF16)

    return pl.pallas_call(
        body, grid=(nh,),
        in_specs=[pl.BlockSpec((lp, LANES), lambda s: (0, s)), pl.BlockSpec((lp, LANES), lambda s: (0, 7 * nh + s)), _ANY],
        out_specs=pl.BlockSpec((lp, LANES), lambda s: (0, nh + s)),
        out_shape=jax.ShapeDtypeStruct(merged.shape, BF16), input_output_aliases={2: 0}, name="merge_fox",
        compiler_params=_cp("parallel"))(o_fox, proj, merged)


def _merge_fox_bwd(o_fox, proj, dmerged, dproj, nh):
    lp = o_fox.shape[0]

    def body(o_ref, z_ref, dm_ref, _, do_ref, dz_ref):
        silu, dsilu = _silu_and_grad(z_ref[...])
        dm = dm_ref[...]
        do_ref[...] = dm * silu
        dz_ref[...] = (dm * o_ref[...] * dsilu).astype(BF16)

    w = nh * HEAD_DIM
    return pl.pallas_call(
        body, grid=(nh,),
        in_specs=[pl.BlockSpec((lp, LANES), lambda s: (0, s)), pl.BlockSpec((lp, LANES), lambda s: (0, 7 * nh + s)),
                  pl.BlockSpec((lp, LANES), lambda s: (0, nh + s)), _ANY],
        out_specs=[pl.BlockSpec((lp, LANES), lambda s: (0, s)), pl.BlockSpec((lp, LANES), lambda s: (0, 7 * nh + s))],
        out_shape=[jax.ShapeDtypeStruct((lp, w), F32), jax.ShapeDtypeStruct(dproj.shape, BF16)],
        input_output_aliases={3: 1}, name="merge_fox_bwd", compiler_params=_cp("parallel"))(o_fox, proj, dmerged, dproj)


def _post(out, x, target, post_w):
    lp, d = out.shape

    def body(o_ref, x_ref, t_ref, w_ref, dy_ref, do_ref, loss_ref, dw_ref):
        i = pl.program_id(0)

        @pl.when(i == 0)
        def _():
            loss_ref[...] = jnp.zeros_like(loss_ref)
            dw_ref[...] = jnp.zeros_like(dw_ref)
        o = o_ref[...]
        r = _rms(o)
        nrm = o * r
        err = jnp.where(i > 0, x_ref[...] + nrm * w_ref[...] - t_ref[...], 0.0)
        loss_ref[0:1, :] += 0.5 * jnp.sum(jnp.sum(err * err, axis=1, keepdims=True), axis=0, keepdims=True) / d
        dy = err / d
        dy_ref[...] = dy
        dw_ref[...] += jnp.sum(dy * nrm, axis=0, keepdims=True)
        dyw = dy * w_ref[...]
        do_ref[...] = (r * (dyw - nrm * jnp.mean(dyw * nrm, axis=-1, keepdims=True))).astype(BF16)

    row = pl.BlockSpec((Q_BLOCK, d), lambda i: (i, 0))
    vec = pl.BlockSpec((1, d), lambda i: (0, 0))
    return pl.pallas_call(
        body, grid=(lp // Q_BLOCK,), in_specs=[row, _x_rows(d), _x_rows(d), vec],
        out_specs=[_x_rows(d), row, pl.BlockSpec((8, LANES), lambda i: (0, 0)), vec],
        out_shape=[jax.ShapeDtypeStruct(x.shape, F32), jax.ShapeDtypeStruct((lp, d), BF16),
                   jax.ShapeDtypeStruct((8, LANES), F32), jax.ShapeDtypeStruct((1, d), F32)],
        name="post", compiler_params=_cp("arbitrary"))(out, x, target, post_w)


def _prenorm_bwd(dxn, x, meta, w, dy, rider=None):
    seq, d = x.shape
    lp = seq + Q_BLOCK

    def body(start, finish, dx_ref, x_ref, m_ref, w_ref, dy_ref, gx_ref, gm_ref, dw_ref):
        i = pl.program_id(0)
        pl.when(i == 0)(start)
        h = _h_tile(i, x_ref, m_ref)
        r = _rms(h)
        xh = h * r
        dxn_ = dx_ref[...]
        dxw = dxn_ * w_ref[...]
        dh = jnp.where(i > 0, dy_ref[...], 0.0) + r * (dxw - xh * jnp.mean(dxw * xh, axis=-1, keepdims=True))
        gx_ref[...] = dh

        @pl.when(i == 0)
        def _():
            dw_ref[...] = jnp.zeros_like(dw_ref)
            gm_ref[...] = dh[PAD_ROWS:, :]
        dw_ref[...] += jnp.sum(dxn_ * xh, axis=0, keepdims=True)
        pl.when(i == lp // Q_BLOCK - 1)(finish)

    vec = pl.BlockSpec((1, d), lambda i: (0, 0))
    met = pl.BlockSpec((N_META, d), lambda i: (0, 0))
    outs, got = _hosted_call(
        body, [rider], 5, 3, 0, grid=(lp // Q_BLOCK,),
        in_specs=[pl.BlockSpec((Q_BLOCK, d), lambda i: (i, 0)), _x_rows(d), met, vec, _x_rows(d)],
        out_specs=[_x_rows(d), met, vec],
        out_shape=[jax.ShapeDtypeStruct((seq, d), F32), jax.ShapeDtypeStruct((N_META, d), F32),
                   jax.ShapeDtypeStruct((1, d), F32)],
        name="prenorm_bwd", compiler_params=_cp("arbitrary"))(dxn, x, meta, w, dy)
    return outs, (got[0] if got else None)


def _layer_grads(x, target, meta, pre_w, wfull, conv_wt, a_log, dt_bias, gdn_norm_w, fq_w, fk_w, f_bias, w_out, post_w,
                 late_weights=None, w_out_grads=None):
    nh = a_log.shape[1]
    zpad = jnp.zeros((1, LANES - 3 * nh), F32)
    bias_row = jnp.concatenate([jnp.zeros((1, nh), F32), dt_bias, f_bias, zpad], axis=1)
    nega_row = jnp.concatenate([jnp.zeros((1, nh), F32), -jnp.exp(a_log), jnp.zeros((1, nh), F32), zpad], axis=1)
    qk_w = jnp.stack([fq_w, fk_w])

    xn = _prenorm(x, meta, pre_w)
    proj = _matmul(xn, wfull, "nn", MM_TILE, F32, "proj")
    qkv = _gdn_prep(proj, conv_wt, nh)
    gates, gt3, gtf = _gates(proj, bias_row, nega_row, nh)
    (o_gdn, s_all, t_all), got = _gdn_fwd(qkv, gates, gt3, nh, None if late_weights is None else late_weights[0])
    if late_weights is not None:
        w_out = late_weights[1](got)
    qkn = _fox_prep(proj, qk_w, nh)
    o_fox, fox_lse = _fox_fwd(qkn, proj, gates, gtf, nh)
    merged = _merge_fox(o_fox, proj, _merge_gdn(o_gdn, proj, gdn_norm_w, nh), nh)
    out = _matmul(merged, w_out, "nn", 4 * LANES, F32, "out_proj")
    dy, dout, loss_blk, dpost_w = _post(out, x, target, post_w)

    dw_out = _matmul(merged, dout, "tn", 4 * LANES, BF16, "dw_out")
    if w_out_grads is None:
        dmerged, gdn_rider = _matmul(dout, w_out, "nt", 4 * LANES, F32, "dmerged"), None
    else:
        dmerged, got = _matmul(dout, w_out, "nt", 4 * LANES, F32, "dmerged", w_out_grads[0](dw_out))
        gdn_rider = w_out_grads[1](dw_out, got)
    do_gdn, dproj, dgdn_norm_w = _merge_gdn_bwd(o_gdn, proj, gdn_norm_w, dmerged, nh)
    do_fox, dproj = _merge_fox_bwd(o_fox, proj, dmerged, dproj, nh)
    dqn, dkn, dc_t, dproj = _fox_bwd(qkn, proj, gates, gtf, fox_lse, do_fox, dproj, nh)
    dproj, dqk_w = _fox_prep_bwd(proj, qk_w, dqn, dkn, dproj, nh)
    (dgq, dgk, dgv, dgate), w_out_parts = _gdn_bwd(qkv, gates, gt3, s_all, t_all, do_gdn, nh, gdn_rider)
    dproj, dconv_wt = _gdn_prep_bwd(proj, conv_wt, dgq, dgk, dgv, dproj, nh)
    dc_rows = jnp.pad(dc_t.reshape(nh, -1), ((2 * nh, LANES - 3 * nh), (0, 0)))
    dproj, gate_sums = _gates_bwd(proj, bias_row, nega_row, gates, dgate, dc_rows, dproj, nh)
    return dict(
        loss=loss_blk[0:1, 0:1], dy=dy, xn=xn, dproj=dproj, post_w=dpost_w,
        conv_wt=dconv_wt, a_log=gate_sums[1:2, nh:2 * nh], dt_bias=gate_sums[0:1, nh:2 * nh],
        gdn_norm_w=dgdn_norm_w, fq_w=dqk_w[0], fk_w=dqk_w[1], f_bias=gate_sums[0:1, 2 * nh:3 * nh], w_out=dw_out,
        w_out_parts=w_out_parts)


def _cast_bf16(a, tr, name):
    r, c = a.shape

    def body(a_ref, o_ref):
        o_ref[...] = a_ref[...].astype(BF16)

    return pl.pallas_call(
        body, grid=(r // tr,), in_specs=[pl.BlockSpec((tr, c), lambda i: (i, 0))],
        out_specs=pl.BlockSpec((tr, c), lambda i: (i, 0)), out_shape=jax.ShapeDtypeStruct((r, c), BF16),
        name=name, compiler_params=_cp("parallel"))(a)


def _column_major(a):
    return jnp.transpose(a, (2, 0, 1))


def _cast_bf16_column_major(a3, name):
    _, r, c = a3.shape

    def body(a_ref, o_ref):
        o_ref[...] = a_ref[...].reshape(LANES, r).T.astype(BF16)

    return pl.pallas_call(
        body, grid=(pl.cdiv(c, LANES),), in_specs=[pl.BlockSpec((LANES, 1, r), lambda i: (i, 0, 0))],
        out_specs=pl.BlockSpec((r, LANES), lambda i: (0, i)), out_shape=jax.ShapeDtypeStruct((r, c), BF16),
        name=name, compiler_params=_cp("parallel"))(_column_major(a3))


def _adamw_column_major(w3, parts, m3, v3, name):
    _, r, c = w3.shape
    n_parts = parts.shape[0]

    def body(w_ref, p_ref, m_ref, v_ref, g_ref, d_ref, nm_ref, nv_ref):
        g = p_ref[0].astype(F32)
        for s in range(1, n_parts):
            g = g + p_ref[s].astype(F32)
        g = g.T
        flat = lambda ref: ref[...].reshape(LANES, r)
        m_new = ADAM_B1 * flat(m_ref) + (1.0 - ADAM_B1) * g
        v_new = ADAM_B2 * flat(v_ref) + (1.0 - ADAM_B2) * (g * g)
        m_hat = m_new / (1.0 - ADAM_B1 ** ADAM_STEP)
        v_hat = v_new / (1.0 - ADAM_B2 ** ADAM_STEP)
        delta = -ADAM_LR * (m_hat / (jnp.sqrt(v_hat) + ADAM_EPS) + ADAM_WD * flat(w_ref))
        for ref, val in ((g_ref, g), (d_ref, delta), (nm_ref, m_new), (nv_ref, v_new)):
            ref[...] = val.reshape(LANES, 1, r)

    blk = pl.BlockSpec((LANES, 1, r), lambda i: (i, 0, 0))
    outs = pl.pallas_call(
        body, grid=(pl.cdiv(c, LANES),), in_specs=[blk, pl.BlockSpec((n_parts, r, LANES), lambda i: (0, 0, i)), blk, blk],
        out_specs=[blk] * 4, out_shape=[jax.ShapeDtypeStruct((c, 1, r), F32)] * 4, name=name,
        compiler_params=_cp("parallel"))(_column_major(w3), parts, _column_major(m3), _column_major(v3))
    return [jnp.transpose(o, (1, 2, 0)) for o in outs]


def _gather_copies(ins, outs, send_sems, recv_sems, local_sems):
    n = len(ins)
    x, y, c = lax.axis_index("x"), lax.axis_index("y"), lax.axis_index("c")
    me, sibling = (x, y, c), (x, y, 1 - c)
    xn, yn, dg = (1 - x, y), (x, 1 - y), (1 - x, 1 - y)

    def copy(a, k, block, to, src=None):
        px, py, pc = block
        rows = outs[a].at[4 * px + 2 * py + pc]
        return pltpu.make_async_remote_copy(
            src_ref=rows if src is None else src, dst_ref=rows, send_sem=send_sems.at[a, k],
            recv_sem=recv_sems.at[a, k], device_id=to, device_id_type=_MESH)

    local = [pltpu.make_async_copy(ins[a], outs[a].at[4 * x + 2 * y + c], local_sems.at[a]) for a in range(n)]
    own = [cp for a in range(n) for cp in (copy(a, 0, me, sibling, src=ins[a]), copy(a, 1, me, (*xn, c), src=ins[a]),
                                           copy(a, 2, me, (*yn, c), src=ins[a]))]

    def start():
        for cp in local + own:
            cp.start()

    def finish():
        for a in range(n):
            @pl.when(c == 1)
            def _(a=a):
                copy(a, 1, (*xn, c), me).wait_recv()
                copy(a, 3, (*xn, c), (*yn, c)).start()

            @pl.when(c == 0)
            def _(a=a):
                copy(a, 2, (*yn, c), me).wait_recv()
                copy(a, 3, (*yn, c), (*xn, c)).start()
        for a in range(n):
            pl.when(c == 0)(copy(a, 1, (*xn, c), me).wait_recv)
            copy(a, 4, (*xn, c), sibling).start()
            pl.when(c == 1)(copy(a, 2, (*yn, c), me).wait_recv)
            copy(a, 5, (*yn, c), sibling).start()
        for a in range(n):
            copy(a, 3, (*dg, c), me).wait_recv()
            copy(a, 6, (*dg, c), sibling).start()
        for a in range(n):
            copy(a, 0, sibling, me).wait_recv()
            for k, chip in ((4, xn), (5, yn), (6, dg)):
                copy(a, k, (*chip, 1 - c), me).wait_recv()
                copy(a, k, (*chip, c), sibling).wait_send()
            copy(a, 3, (*xn, c), (*yn, c)).wait_send()
        for cp in own:
            cp.wait_send()
        for cp in local:
            cp.wait()

    return start, finish


def _gather_scratch(n):
    return [pltpu.SemaphoreType.DMA((n, N_DEV - 1)), pltpu.SemaphoreType.DMA((n, N_DEV - 1)), pltpu.SemaphoreType.DMA((n,))]


def _gather_rider(arrays):
    return _Rider(list(arrays), [jax.ShapeDtypeStruct((N_DEV,) + a.shape, a.dtype) for a in arrays],
                  _gather_scratch(len(arrays)), {}, lambda ins, outs, scratch: _gather_copies(ins, outs, *scratch))


def _all_gather(arrays, name):
    n = len(arrays)

    def body(*refs):
        start, finish = _gather_copies(refs[:n], refs[n:2 * n], *refs[2 * n:])
        start()
        finish()

    return pl.pallas_call(
        body, in_specs=[_ANY] * n, out_specs=[_ANY] * n,
        out_shape=[jax.ShapeDtypeStruct((N_DEV,) + a.shape, a.dtype) for a in arrays],
        scratch_shapes=_gather_scratch(n), name=name)(*arrays)


SLAB = 10 * LANES


def _slab_start(blk, nh, cols):
    in_second_half = blk >= N_DEV // 2
    shift = (2 * nh if in_second_half else 0) if isinstance(blk, int) else jnp.where(in_second_half, 2 * nh, 0)
    return (blk * cols - shift) // LANES * LANES


def _pair_rider(dw_rows=None, parts=None, nh=None):
    if dw_rows is not None:
        r, full = dw_rows.shape
        cols = (full - NARROW + 3 * nh) // N_DEV
        out_shapes = [jax.ShapeDtypeStruct((N_CHIP, r, SLAB), dw_rows.dtype), jax.ShapeDtypeStruct((r, NARROW), dw_rows.dtype)]
    else:
        out_shapes = [jax.ShapeDtypeStruct((N_CHIP,) + parts.shape[1:], parts.dtype)]

    def make(ins, outs, scratch):
        send_sems, recv_sems = scratch
        x, y, c = lax.axis_index("x"), lax.axis_index("y"), lax.axis_index("c")
        kw = lambda k: dict(send_sem=send_sems.at[k], recv_sem=recv_sems.at[k], device_id=(x, y, 1 - c), device_id_type=_MESH)
        copies = []
        for q in range(N_CHIP):
            if dw_rows is not None:
                first = pl.multiple_of(_slab_start(2 * q + 1 - c, nh, cols), LANES)
                copies.append(pltpu.make_async_remote_copy(src_ref=ins[0].at[:, pl.ds(first, SLAB)], dst_ref=outs[0].at[q], **kw(q)))
            else:
                copies.append(pltpu.make_async_remote_copy(src_ref=ins[0].at[2 * q + 1 - c], dst_ref=outs[0].at[q], **kw(q)))
        if dw_rows is not None:
            copies.append(pltpu.make_async_remote_copy(src_ref=ins[0].at[:, pl.ds(full - NARROW, NARROW)], dst_ref=outs[1],
                                                       **kw(N_CHIP)))

        def start():
            for cp in copies:
                cp.start()

        def finish():
            for cp in copies:
                cp.wait()

        return start, finish

    return _Rider([dw_rows if dw_rows is not None else parts], out_shapes,
                  [pltpu.SemaphoreType.DMA((N_CHIP + 1,)), pltpu.SemaphoreType.DMA((N_CHIP + 1,))], {}, make)


def _relayout_pair_sum(dwfull, got_slabs, got_tail, core, nh, tr, name):
    d, full = dwfull.shape
    w = nh * HEAD_DIM
    cols = (8 * w + 3 * nh) // N_DEV
    segs = _native_segments(nh)

    def block(f_ref, s_ref, t_ref, q, blk):
        st = _slab_start(blk, nh, cols)
        wide = f_ref[:, st:st + SLAB].astype(F32) + s_ref[q].astype(F32)
        tail = f_ref[:, 8 * w:].astype(F32) + t_ref[...].astype(F32)
        pieces = []
        for s0, s1, t0 in segs:
            lo, hi = max(s0, blk * cols), min(s1, (blk + 1) * cols)
            if lo < hi:
                at = t0 + lo - s0
                pieces.append(tail[:, at - 8 * w:at - 8 * w + hi - lo] if at >= 8 * w else wide[:, at - st:at - st + hi - lo])
        return (pieces[0] if len(pieces) == 1 else jnp.concatenate(pieces, axis=1)).astype(dwfull.dtype)

    def body(core_ref, f_ref, s_ref, t_ref, o_ref):
        for parity in range(2):
            @pl.when(core_ref[0] == parity)
            def _(parity=parity):
                for q in range(N_CHIP):
                    o_ref[q] = block(f_ref, s_ref, t_ref, q, 2 * q + parity)

    return pl.pallas_call(
        body,
        grid_spec=pltpu.PrefetchScalarGridSpec(
            num_scalar_prefetch=1, grid=(d // tr,),
            in_specs=[pl.BlockSpec((tr, full), lambda i, c_ref: (i, 0)), pl.BlockSpec((N_CHIP, tr, SLAB), lambda i, c_ref: (0, i, 0)),
                      pl.BlockSpec((tr, NARROW), lambda i, c_ref: (i, 0))],
            out_specs=pl.BlockSpec((N_CHIP, tr, cols), lambda i, c_ref: (0, i, 0))),
        out_shape=jax.ShapeDtypeStruct((N_CHIP, d, cols), dwfull.dtype), name=name,
        compiler_params=_cp("parallel"))(core, dwfull, got_slabs, got_tail)


def _pair_sum(parts, got, core, tr, name):
    _, r, c = parts.shape

    def body(core_ref, p_ref, g_ref, o_ref):
        o_ref[...] = (p_ref[...].astype(F32) + g_ref[...].astype(F32)).astype(o_ref.dtype)

    return pl.pallas_call(
        body,
        grid_spec=pltpu.PrefetchScalarGridSpec(
            num_scalar_prefetch=1, grid=(N_CHIP, r // tr),
            in_specs=[pl.BlockSpec((1, tr, c), lambda q, i, core_ref: (2 * q + core_ref[0], i, 0)),
                      pl.BlockSpec((1, tr, c), lambda q, i, core_ref: (q, i, 0))],
            out_specs=pl.BlockSpec((1, tr, c), lambda q, i, core_ref: (q, i, 0))),
        out_shape=jax.ShapeDtypeStruct((N_CHIP, r, c), parts.dtype), name=name,
        compiler_params=_cp("parallel", "parallel"))(core, parts, got)


def _native_segments(nh):
    w = nh * HEAD_DIM
    return [(0, 4 * w, 0), (4 * w, 4 * w + 2 * nh, 8 * w), (4 * w + 2 * nh, 8 * w + 2 * nh, 4 * w),
            (8 * w + 2 * nh, 8 * w + 3 * nh, 8 * w + 2 * nh)]


def _relayout_w_in(wg, nh, tr):
    _, d, cols = wg.shape
    w = nh * HEAD_DIM

    def native(ref, j0, j1):
        out = []
        while j0 < j1:
            blk = j0 // cols
            end = min(j1, (blk + 1) * cols)
            out.append(ref[blk, :, pl.ds(j0 - blk * cols, end - j0)])
            j0 = end
        return out

    def body(g_ref, o_ref):
        for cidx in range(8 * w // LANES):
            j0 = cidx * LANES + (0 if cidx * LANES < 4 * w else 2 * nh)
            pieces = native(g_ref, j0, j0 + LANES)
            o_ref[:, cidx * LANES:(cidx + 1) * LANES] = pieces[0] if len(pieces) == 1 else jnp.concatenate(pieces, axis=1)
        pieces = (native(g_ref, 4 * w, 4 * w + 2 * nh) + native(g_ref, 8 * w + 2 * nh, 8 * w + 3 * nh)
                  + [jnp.zeros((tr, NARROW - 3 * nh), wg.dtype)])
        o_ref[:, 8 * w:] = jnp.concatenate(pieces, axis=1)

    return pl.pallas_call(
        body, grid=(d // tr,), in_specs=[pl.BlockSpec((N_DEV, tr, cols), lambda i: (0, i, 0))],
        out_specs=pl.BlockSpec((tr, 8 * w + NARROW), lambda i: (i, 0)),
        out_shape=jax.ShapeDtypeStruct((d, 8 * w + NARROW), wg.dtype),
        name="relayout_w_in", compiler_params=_cp("parallel"))(wg)


def _adamw(w, parts, m, v, tr, name):
    r, c = w.shape
    n_parts = parts.shape[0]

    def body(w_ref, p_ref, m_ref, v_ref, g_ref, d_ref, nm_ref, nv_ref):
        g = p_ref[0].astype(F32)
        for s in range(1, n_parts):
            g = g + p_ref[s].astype(F32)
        m_new = ADAM_B1 * m_ref[...] + (1.0 - ADAM_B1) * g
        v_new = ADAM_B2 * v_ref[...] + (1.0 - ADAM_B2) * (g * g)
        m_hat = m_new / (1.0 - ADAM_B1 ** ADAM_STEP)
        v_hat = v_new / (1.0 - ADAM_B2 ** ADAM_STEP)
        g_ref[...] = g
        d_ref[...] = -ADAM_LR * (m_hat / (jnp.sqrt(v_hat) + ADAM_EPS) + ADAM_WD * w_ref[...])
        nm_ref[...] = m_new
        nv_ref[...] = v_new

    blk = pl.BlockSpec((tr, c), lambda i: (i, 0))
    return pl.pallas_call(
        body, grid=(r // tr,), in_specs=[blk, pl.BlockSpec((n_parts, tr, c), lambda i: (0, i, 0)), blk, blk],
        out_specs=[blk] * 4, out_shape=[jax.ShapeDtypeStruct((r, c), F32)] * 4, name=name,
        compiler_params=_cp("parallel"))(w, parts, m, v)


def _pack_small(d, pre, post, a_log, dt_bias, f_bias, gdn_w, fq_w, fk_w, extra):
    row2 = jnp.concatenate([a_log, dt_bias, f_bias, gdn_w, fq_w, fk_w, extra], axis=1)
    row2 = jnp.pad(row2, ((0, 0), (0, d - row2.shape[1])))
    return jnp.concatenate([pre, post, row2, jnp.zeros((5, d), F32)], axis=0)


def _unpack_small(p, nh):
    o = 3 * nh
    return dict(pre=p[0:1], post=p[1:2], a_log=p[2:3, 0:nh], dt_bias=p[2:3, nh:2 * nh], f_bias=p[2:3, 2 * nh:o],
                gdn_w=p[2:3, o:o + HEAD_DIM], fq_w=p[2:3, o + HEAD_DIM:o + 2 * HEAD_DIM],
                fk_w=p[2:3, o + 2 * HEAD_DIM:o + 3 * HEAD_DIM], extra=p[2, o + 3 * HEAD_DIM])


def kernel(x, meta_tokens, pre_norm_w, w_in, conv_w, a_log, dt_bias, gdn_norm_w, fox_q_norm_w, fox_k_norm_w, fox_f_bias, w_out, post_norm_w, loss_target, m_meta_tokens, m_pre_norm_w, m_w_in, m_conv_w, m_a_log, m_dt_bias, m_gdn_norm_w, m_fox_q_norm_w, m_fox_k_norm_w, m_fox_f_bias, m_w_out, m_post_norm_w, v_meta_tokens, v_pre_norm_w, v_w_in, v_conv_w, v_a_log, v_dt_bias, v_gdn_norm_w, v_fox_q_norm_w, v_fox_k_norm_w, v_fox_f_bias, v_w_out, v_post_norm_w):
    nh = a_log.shape[1]
    d = x.shape[-1]
    w = nh * HEAD_DIM
    zero = jnp.zeros((1, 1), F32)

    wg, cg, mg = _all_gather([_cast_bf16_column_major(w_in, "cast_w_in"), conv_w[0].T, meta_tokens], "gather_weights")
    wfull = _relayout_w_in(wg, nh, 256)
    conv_wt = cg.transpose(1, 0, 2).reshape(CONV_WIDTH, 3 * w)
    meta_full = mg.transpose(1, 0, 2).reshape(N_META, d)
    late_weights = (_gather_rider([_cast_bf16(w_out[0], 256, "cast_w_out")]), lambda got: got[0].reshape(2 * w, d))
    core = lax.axis_index("c")
    dev = 4 * lax.axis_index("x") + 2 * lax.axis_index("y") + core
    core_arr = jnp.reshape(core, (1,)).astype(jnp.int32)

    out_parts = lambda dw_out: dw_out.reshape(N_DEV, 2 * w // N_DEV, d)
    g = _layer_grads(
        x[0], loss_target[0], meta_full, pre_norm_w, wfull, conv_wt, a_log, dt_bias, gdn_norm_w,
        fox_q_norm_w, fox_k_norm_w, fox_f_bias, None, post_norm_w, late_weights=late_weights,
        w_out_grads=(lambda dw_out: _pair_rider(parts=out_parts(dw_out)),
                     lambda dw_out, got: _chip_rider(_pair_sum(out_parts(dw_out), got[0], core_arr, 256, "pair_sum_w_out"))))
    p_out = g["w_out_parts"][0]
    xn, dproj, half = g["xn"], g["dproj"], d // 2
    dw_a = _matmul(xn, dproj, "tn", MM_TILE, BF16, "dw_in_a", a_cols=(half, 0))
    dw_b, got_a = _matmul(xn, dproj, "tn", MM_TILE, BF16, "dw_in_b", _pair_rider(dw_rows=dw_a, nh=nh), a_cols=(half, 1))
    sums_a = _relayout_pair_sum(dw_a, got_a[0], got_a[1], core_arr, nh, 128, "relayout_pair_sum_a")
    dxn, (p_in_a, got_b) = _dxn(dproj, wfull, [_chip_rider(sums_a, rows_total=d), _pair_rider(dw_rows=dw_b, nh=nh)],
                                MM_TILE, "dxn")
    sums_b = _relayout_pair_sum(dw_b, got_b[0], got_b[1], core_arr, nh, 128, "relayout_pair_sum_b")
    (grad_x, dmeta, dpre_w), p_in = _prenorm_bwd(dxn, x[0], meta_full, pre_norm_w, g["dy"],
                                                 _chip_rider(sums_b, rows_total=d, row0=half, into=p_in_a[0]))
    p_in = p_in[0]
    small = _pack_small(d, dpre_w, g["post_w"], g["a_log"], g["dt_bias"], g["f_bias"], g["gdn_norm_w"], g["fq_w"],
                        g["fk_w"], g["loss"])
    a_conv, a_meta, p_small = _all_gather([g["conv_wt"], dmeta, small], "gather_small_grads")
    p_conv = lax.dynamic_slice_in_dim(a_conv, dev * conv_w.shape[1], conv_w.shape[1], axis=2).transpose(0, 2, 1)
    p_meta = lax.dynamic_slice_in_dim(a_meta, dev * meta_tokens.shape[1], meta_tokens.shape[1], axis=2)

    r_in = _adamw_column_major(w_in, p_in, m_w_in, v_w_in, "adamw_w_in")
    r_out = _adamw(w_out[0], p_out, m_w_out[0], v_w_out[0], 64, "adamw_w_out")
    r_conv = _adamw(conv_w[0], p_conv, m_conv_w[0], v_conv_w[0], conv_w.shape[1], "adamw_conv_w")
    r_meta = _adamw(meta_tokens, p_meta, m_meta_tokens, v_meta_tokens, N_META, "adamw_meta")
    pk = lambda pre, post, a, dt, gw, fq, fk, fb: _pack_small(d, pre, post, a, dt, fb, gw, fq, fk, zero)
    r_small = _adamw(
        pk(pre_norm_w, post_norm_w, a_log, dt_bias, gdn_norm_w, fox_q_norm_w, fox_k_norm_w, fox_f_bias), p_small,
        pk(m_pre_norm_w, m_post_norm_w, m_a_log, m_dt_bias, m_gdn_norm_w, m_fox_q_norm_w, m_fox_k_norm_w, m_fox_f_bias),
        pk(v_pre_norm_w, v_post_norm_w, v_a_log, v_dt_bias, v_gdn_norm_w, v_fox_q_norm_w, v_fox_k_norm_w, v_fox_f_bias),
        8, "adamw_small")

    sm = [_unpack_small(r, nh) for r in r_small]
    outs = []
    for i in range(4):
        s = sm[i]
        outs += [r_meta[i], s["pre"], r_in[i], r_conv[i][None], s["a_log"], s["dt_bias"], s["gdn_w"], s["fq_w"],
                 s["fk_w"], s["f_bias"], r_out[i][None], s["post"]]
    return (sm[0]["extra"], grad_x[None], *outs)
```

```python
import jax
import jax.numpy as jnp
from jax import lax
from jax.experimental import pallas as pl
from jax.experimental.pallas import tpu as pltpu

F32, BF16 = jnp.float32, jnp.bfloat16
HEAD_DIM = 128
N_META = 16
CONV_WIDTH = 4
CHUNK = 128
Q_BLOCK = 128
LANES = 128
EPS = 1e-6
PAD_ROWS = Q_BLOCK - N_META
N_DEV = 8
N_CHIP = 4
VMEM_LIMIT = 56 * 1024 * 1024
NEG = -1e30
NARROW = 2 * LANES
MM_TILE = 6 * LANES

ADAM_LR, ADAM_B1, ADAM_B2, ADAM_EPS, ADAM_WD, ADAM_STEP = 0.001, 0.9, 0.999, 1e-08, 0.01, 10

_DN = {"nn": (((1,), (0,)), ((), ())), "nt": (((1,), (1,)), ((), ())), "tn": (((0,), (0,)), ((), ()))}
_DN3 = {"nn": (((2,), (1,)), ((0,), (0,))), "nt": (((2,), (2,)), ((0,), (0,))), "tn": (((1,), (1,)), ((0,), (0,)))}
_ANY = pl.BlockSpec(memory_space=pl.ANY)
_MESH = pl.DeviceIdType.MESH


def _cp(*sem):
    return pltpu.CompilerParams(dimension_semantics=sem, vmem_limit_bytes=VMEM_LIMIT)


def _dot(a, b, dims="nn", prec=None):
    return lax.dot_general(a, b, _DN[dims], precision=prec, preferred_element_type=F32)


def _bdot(a, b, dims="nn"):
    return _dot(a.astype(BF16), b.astype(BF16), dims)


def _hdot(a, b, dims="nn"):
    return _dot(a, b, dims, prec=lax.Precision.HIGHEST)


def _dot3(a, b, dims="nn"):
    return lax.dot_general(a, b, _DN3[dims], preferred_element_type=F32)


def _bdot3(a, b, dims="nn"):
    return _dot3(a.astype(BF16), b.astype(BF16), dims)


def _split(a):
    hi = a.astype(BF16)
    return hi, (a - hi.astype(F32)).astype(BF16)


def _iota(shape, dim):
    return lax.broadcasted_iota(jnp.int32, shape, dim)


def _sigmoid(z):
    return 1.0 / (1.0 + jnp.exp(-z))


def _softplus(z):
    e = jnp.exp(-jnp.abs(z))
    u = 1.0 + e
    l1p = jnp.where(u == 1.0, e, jnp.log(u) * (e / jnp.where(u == 1.0, 1.0, u - 1.0)))
    return jnp.maximum(z, 0.0) + l1p


def _silu_and_grad(z):
    s = _sigmoid(z)
    return z * s, s * (1.0 + z * (1.0 - s))


def _rms(x):
    return lax.rsqrt(jnp.mean(x * x, axis=-1, keepdims=True) + EPS)


def _h_tile(i, x_ref, meta_ref):
    first = jnp.concatenate([jnp.zeros((PAD_ROWS, x_ref.shape[1]), F32), meta_ref[...]], axis=0)
    return jnp.where(i == 0, first, x_ref[...])


def _x_rows(d):
    return pl.BlockSpec((Q_BLOCK, d), lambda i: (jnp.maximum(i - 1, 0), 0))


def _prenorm(x, meta, w):
    seq, d = x.shape
    lp = seq + Q_BLOCK

    def body(x_ref, m_ref, w_ref, o_ref):
        h = _h_tile(pl.program_id(0), x_ref, m_ref)
        o_ref[...] = (h * _rms(h) * w_ref[...]).astype(BF16)

    return pl.pallas_call(
        body, grid=(lp // Q_BLOCK,),
        in_specs=[_x_rows(d), pl.BlockSpec((N_META, d), lambda i: (0, 0)), pl.BlockSpec((1, d), lambda i: (0, 0))],
        out_specs=pl.BlockSpec((Q_BLOCK, d), lambda i: (i, 0)),
        out_shape=jax.ShapeDtypeStruct((lp, d), BF16), name="prenorm", compiler_params=_cp("parallel"))(x, meta, w)


def _tile(n, want):
    return max(t for t in range(LANES, want + 1, LANES) if n % t == 0)


class _Rider:
    def __init__(self, inputs, out_shapes, scratch, aliases, make):
        self.inputs, self.out_shapes, self.scratch, self.aliases, self.make = inputs, out_shapes, scratch, aliases, make


def _hosted_call(body, riders, n_in, n_out, n_scratch, *, in_specs, out_specs, out_shape, scratch_shapes=(), aliases=None,
                 **kw):
    riders = [r for r in riders if r is not None]
    r_in = [len(r.inputs) for r in riders]
    r_out = [len(r.out_shapes) for r in riders]
    r_scr = [len(r.scratch) for r in riders]
    al = dict(aliases or {})
    for k, r in enumerate(riders):
        al.update({n_in + sum(r_in[:k]) + i: n_out + sum(r_out[:k]) + o for i, o in r.aliases.items()})

    def full_body(*refs):
        ins, rest = refs[:n_in + sum(r_in)], refs[n_in + sum(r_in):]
        outs, scr = rest[:n_out + sum(r_out)], rest[n_out + sum(r_out):]
        hooks = [r.make(ins[n_in + sum(r_in[:k]):n_in + sum(r_in[:k + 1])], outs[n_out + sum(r_out[:k]):n_out + sum(r_out[:k + 1])],
                        scr[n_scratch + sum(r_scr[:k]):n_scratch + sum(r_scr[:k + 1])]) for k, r in enumerate(riders)]

        def start():
            for h in hooks:
                h[0]()

        def finish():
            for h in hooks:
                h[1]()

        body(start, finish, *ins[:n_in], *outs[:n_out], *scr[:n_scratch])

    call = pl.pallas_call(
        full_body, in_specs=list(in_specs) + [_ANY] * sum(r_in), out_specs=list(out_specs) + [_ANY] * sum(r_out),
        out_shape=list(out_shape) + [s for r in riders for s in r.out_shapes],
        scratch_shapes=list(scratch_shapes) + [s for r in riders for s in r.scratch], input_output_aliases=al, **kw)

    def run(*args):
        res = call(*args, *[t for r in riders for t in r.inputs])
        return res[:n_out], [res[n_out + sum(r_out[:k]):n_out + sum(r_out[:k + 1])] for k in range(len(riders))]

    return run


def _matmul(a, b, dims, tn, out_dtype, name, rider=None, a_cols=None):
    a_shape = a.shape if a_cols is None else (a.shape[0], a_cols[0])
    a_index = 0 if a_cols is None else a_cols[1]
    m = a_shape[1] if dims == "tn" else a_shape[0]
    n = b.shape[0] if dims == "nt" else b.shape[1]
    kdim = b.shape[1] if dims == "nt" else b.shape[0]
    tn = _tile(n, tn)
    steps = n // tn
    b_spec = pl.BlockSpec((tn, kdim), lambda j: (j, 0)) if dims == "nt" else pl.BlockSpec((kdim, tn), lambda j: (0, j))

    def body(start, finish, a_ref, b_ref, o_ref):
        pl.when(pl.program_id(0) == 0)(start)
        o_ref[...] = _dot(a_ref[...], b_ref[...], dims).astype(out_dtype)
        pl.when(pl.program_id(0) == steps - 1)(finish)

    (out,), got = _hosted_call(
        body, [rider], 2, 1, 0, grid=(steps,), in_specs=[pl.BlockSpec(a_shape, lambda j: (0, a_index)), b_spec],
        out_specs=[pl.BlockSpec((m, tn), lambda j: (0, j))], out_shape=[jax.ShapeDtypeStruct((m, n), out_dtype)],
        name=name, compiler_params=_cp("parallel" if rider is None else "arbitrary"))(a, b)
    return out if rider is None else (out, got[0])


def _chip_rider(sums, rows_total=None, row0=0, into=None):
    _, r, c = sums.shape
    rows_total = rows_total or r

    def make(ins, outs, scratch):
        send_sems, recv_sems, local_sem = scratch
        x, y, core = lax.axis_index("x"), lax.axis_index("y"), lax.axis_index("c")
        mine = 2 * x + y
        land = lambda chip: outs[0].at[chip].at[pl.ds(row0, r)]
        local = pltpu.make_async_copy(ins[0].at[mine], land(mine), local_sem)
        sends, recvs = [], []
        for k in range(1, N_CHIP):
            px = 1 - x if k & 2 else x
            py = 1 - y if k & 1 else y
            kw = dict(send_sem=send_sems.at[k - 1], recv_sem=recv_sems.at[k - 1], device_id=(px, py, core),
                      device_id_type=_MESH)
            sends.append(pltpu.make_async_remote_copy(src_ref=ins[0].at[2 * px + py], dst_ref=land(mine), **kw))
            recvs.append(pltpu.make_async_remote_copy(src_ref=ins[0].at[mine], dst_ref=land(2 * px + py), **kw))

        def start():
            for cp in [local] + sends:
                cp.start()

        def finish():
            local.wait()
            for cp in sends:
                cp.wait_send()
            for cp in recvs:
                cp.wait_recv()

        return start, finish

    return _Rider([sums] + ([] if into is None else [into]), [jax.ShapeDtypeStruct((N_CHIP, rows_total, c), sums.dtype)],
                  [pltpu.SemaphoreType.DMA((N_CHIP - 1,)), pltpu.SemaphoreType.DMA((N_CHIP - 1,)), pltpu.SemaphoreType.DMA(())],
                  {} if into is None else {1: 0}, make)


def _dxn(dproj, wfull, riders, tk, name):
    m, k = dproj.shape
    n = wfull.shape[0]
    tk = _tile(k, tk)
    steps = k // tk

    def body(start, finish, a_ref, b_ref, o_ref):
        j = pl.program_id(0)

        @pl.when(j == 0)
        def _():
            start()
            o_ref[...] = jnp.zeros_like(o_ref)
        o_ref[...] += _dot(a_ref[...], b_ref[...], "nt")
        pl.when(j == steps - 1)(finish)

    (dxn,), got = _hosted_call(
        body, riders, 2, 1, 0, grid=(steps,),
        in_specs=[pl.BlockSpec((m, tk), lambda j: (0, j)), pl.BlockSpec((n, tk), lambda j: (0, j))],
        out_specs=[pl.BlockSpec((m, n), lambda j: (0, 0))], out_shape=[jax.ShapeDtypeStruct((m, n), F32)],
        name=name, compiler_params=_cp("arbitrary"))(dproj, wfull)
    return dxn, got


def _conv_taps(x, w):
    c = x * w[CONV_WIDTH - 1:CONV_WIDTH, :]
    for j in range(CONV_WIDTH - 1):
        c = c + pltpu.roll(x, CONV_WIDTH - 1 - j, 0) * w[j:j + 1, :]
    return c


def _gdn_prep(proj, conv_wt, nh):
    lp = proj.shape[0]
    scale = HEAD_DIM ** -0.5

    def body(x_ref, w_ref, o_ref):
        which = pl.program_id(0) // nh
        c = _conv_taps(x_ref[...], w_ref[...])
        s = c * _sigmoid(c)
        r = lax.rsqrt(jnp.sum(s * s, axis=-1, keepdims=True) + EPS)
        f = jnp.where(which == 0, r * scale, jnp.where(which == 1, r, 1.0))
        o_ref[...] = jnp.where(_iota(s.shape, 0) >= PAD_ROWS, s * f, 0.0)

    return pl.pallas_call(
        body, grid=(3 * nh,),
        in_specs=[pl.BlockSpec((lp, LANES), lambda s: (0, s)), pl.BlockSpec((CONV_WIDTH, LANES), lambda s: (0, s))],
        out_specs=pl.BlockSpec((lp, LANES), lambda s: (0, s)),
        out_shape=jax.ShapeDtypeStruct((lp, 3 * nh * HEAD_DIM), F32), name="gdn_prep",
        compiler_params=_cp("parallel"))(proj, conv_wt)


def _gdn_prep_bwd(proj, conv_wt, dq, dk, dv, dproj, nh):
    lp = proj.shape[0]
    scale = HEAD_DIM ** -0.5
    part = lambda p: pl.BlockSpec((lp, LANES), lambda s: (0, jnp.clip(s - p * nh, 0, nh - 1)))

    def body(x_ref, w_ref, dq_ref, dk_ref, dv_ref, _, dx_ref, dw_ref):
        which = pl.program_id(0) // nh
        x = x_ref[...]
        w = w_ref[...]
        c = _conv_taps(x, w)
        sg = _sigmoid(c)
        s = c * sg
        r = lax.rsqrt(jnp.sum(s * s, axis=-1, keepdims=True) + EPS)
        dy = jnp.where(which == 0, dq_ref[...], jnp.where(which == 1, dk_ref[...], dv_ref[...]))
        dy = jnp.where(_iota(s.shape, 0) >= PAD_ROWS, dy, 0.0)
        y0 = s * r
        dy0 = dy * jnp.where(which == 0, scale, 1.0)
        ds_n = r * (dy0 - y0 * jnp.sum(dy0 * y0, axis=-1, keepdims=True))
        ds = jnp.where(which == 2, dy, ds_n)
        dc = ds * (sg * (1.0 + c * (1.0 - sg)))
        dx = dc * w[CONV_WIDTH - 1:CONV_WIDTH, :]
        rows = [jnp.sum(dc * x, axis=0, keepdims=True)]
        for j in range(CONV_WIDTH - 2, -1, -1):
            sh = CONV_WIDTH - 1 - j
            dx = dx + pltpu.roll(dc, lp - sh, 0) * w[j:j + 1, :]
            rows.insert(0, jnp.sum(dc * pltpu.roll(x, sh, 0), axis=0, keepdims=True))
        dx_ref[...] = dx.astype(BF16)
        dw_ref[...] = jnp.concatenate(rows, axis=0)

    strip = pl.BlockSpec((lp, LANES), lambda s: (0, s))
    taps = pl.BlockSpec((CONV_WIDTH, LANES), lambda s: (0, s))
    return pl.pallas_call(
        body, grid=(3 * nh,), in_specs=[strip, taps, part(0), part(1), part(2), _ANY], out_specs=[strip, taps],
        out_shape=[jax.ShapeDtypeStruct(dproj.shape, BF16), jax.ShapeDtypeStruct((CONV_WIDTH, 3 * nh * HEAD_DIM), F32)],
        input_output_aliases={5: 0}, name="gdn_prep_bwd", compiler_params=_cp("parallel"))(proj, conv_wt, dq, dk, dv, dproj)


def _gates(proj, bias_row, nega_row, nh):
    lp = proj.shape[0]
    nc = lp // CHUNK

    def body(p_ref, b_ref, a_ref, g_ref, gt3_ref, gtf_ref):
        lane = _iota((CHUNK, LANES), 1)
        tri = (_iota((CHUNK, CHUNK), 0) >= _iota((CHUNK, CHUNK), 1)).astype(F32)

        def step(n, carry):
            r0 = pl.multiple_of(n * CHUNK, CHUNK)
            z = p_ref[pl.ds(r0, CHUNK), :] + b_ref[...]
            base = jnp.where(lane < nh, _sigmoid(z),
                             jnp.where(lane < 2 * nh, a_ref[...] * _softplus(z),
                                       jnp.where(lane < 3 * nh, -_softplus(-z), 0.0)))
            base = jnp.where(r0 + _iota((CHUNK, LANES), 0) >= PAD_ROWS, base, 0.0)
            cs = _hdot(tri, base)
            run = jnp.where((lane >= 2 * nh) & (lane < 3 * nh), cs + carry, cs)
            sh = pltpu.roll(run, 2 * nh, 1)
            out = base + jnp.where((lane >= 3 * nh) & (lane < 5 * nh), sh, 0.0)
            g_ref[pl.ds(r0, CHUNK), :] = out
            gt3_ref[n] = out.T
            return carry + cs[CHUNK - 1:CHUNK, :]

        lax.fori_loop(0, nc, step, jnp.zeros((1, LANES), F32))
        gtf_ref[...] = g_ref[...].T

    vec = pl.BlockSpec((1, LANES), lambda i: (0, 0))
    return pl.pallas_call(
        body, grid=(1,), in_specs=[pl.BlockSpec((lp, LANES), lambda i: (0, 8 * nh)), vec, vec],
        out_specs=[pl.BlockSpec((lp, LANES), lambda i: (0, 0)), pl.BlockSpec((nc, LANES, CHUNK), lambda i: (0, 0, 0)),
                   pl.BlockSpec((LANES, lp), lambda i: (0, 0))],
        out_shape=[jax.ShapeDtypeStruct((lp, LANES), F32), jax.ShapeDtypeStruct((nc, LANES, CHUNK), F32),
                   jax.ShapeDtypeStruct((LANES, lp), F32)],
        name="gates", compiler_params=_cp("arbitrary"))(proj, bias_row, nega_row)


def _gates_bwd(proj, bias_row, nega_row, gates, dgate_gdn, dc_t, dproj, nh):
    lp = proj.shape[0]
    nc = lp // CHUNK

    def body(p_ref, b_ref, a_ref, g_ref, dg_ref, dc_ref, _, dz_ref, sm_ref, dct_scr):
        lane = _iota((CHUNK, LANES), 1)
        triu = (_iota((CHUNK, CHUNK), 0) <= _iota((CHUNK, CHUNK), 1)).astype(F32)
        dct_scr[...] = dc_ref[...].T
        sm_ref[...] = jnp.zeros_like(sm_ref)
        dz_ref[:, LANES:] = jnp.zeros((lp, NARROW - LANES), BF16)

        def step(i, carry):
            n = nc - 1 - i
            r0 = pl.multiple_of(n * CHUNK, CHUNK)
            z = p_ref[pl.ds(r0, CHUNK), :] + b_ref[...]
            gt = g_ref[pl.ds(r0, CHUNK), :]
            dgd = dg_ref[pl.ds(r0, CHUNK), :]
            dch = dct_scr[pl.ds(r0, CHUNK), :]
            rc = _hdot(triu, dch) + carry
            sg = _sigmoid(z)
            dz = jnp.where(lane < nh, dgd * sg * (1.0 - sg),
                           jnp.where(lane < 2 * nh, dgd * a_ref[...] * sg,
                                     jnp.where(lane < 3 * nh, rc * (1.0 - sg), 0.0)))
            dz = jnp.where(r0 + _iota((CHUNK, LANES), 0) >= PAD_ROWS, dz, 0.0)
            dz_ref[pl.ds(r0, CHUNK), 0:LANES] = dz.astype(BF16)
            sm_ref[0:1, :] += jnp.sum(dz, axis=0, keepdims=True)
            sm_ref[1:2, :] += jnp.sum(jnp.where((lane >= nh) & (lane < 2 * nh), dgd * gt, 0.0), axis=0, keepdims=True)
            return carry + jnp.sum(dch, axis=0, keepdims=True)

        lax.fori_loop(0, nc, step, jnp.zeros((1, LANES), F32))

    vec = pl.BlockSpec((1, LANES), lambda i: (0, 0))
    full = pl.BlockSpec((lp, LANES), lambda i: (0, 0))
    last = pl.BlockSpec((lp, LANES), lambda i: (0, 8 * nh))
    tail = pl.BlockSpec((lp, NARROW), lambda i: (0, 8 * nh * LANES // NARROW))
    return pl.pallas_call(
        body, grid=(1,), in_specs=[last, vec, vec, full, full, pl.BlockSpec((LANES, lp), lambda i: (0, 0)), _ANY],
        out_specs=[tail, pl.BlockSpec((8, LANES), lambda i: (0, 0))],
        out_shape=[jax.ShapeDtypeStruct(dproj.shape, BF16), jax.ShapeDtypeStruct((8, LANES), F32)],
        scratch_shapes=[pltpu.VMEM((lp, LANES), F32)], input_output_aliases={6: 0},
        name="gates_bwd", compiler_params=_cp("arbitrary"))(proj, bias_row, nega_row, gates, dgate_gdn, dc_t, dproj)


def _tri_inv(a):
    t = jnp.where(_iota(a.shape, 1) == _iota(a.shape, 2), 1.0, 0.0) - a
    p = a
    for _ in range(CHUNK.bit_length() - 2):
        ph, pw = _split(p)
        p = _dot3(ph, ph) + (_dot3(ph, pw) + _dot3(pw, ph))
        ph, pw = _split(p)
        th, tw = _split(t)
        t = t + (_dot3(th, ph) + (_dot3(th, pw) + _dot3(tw, ph)))
    return t


def _gdn_chunk(q, k, v, beta, gc, gr, t=None):
    ii, jj = _iota((1, CHUNK, CHUNK), 1), _iota((1, CHUNK, CHUNK), 2)
    causal, strict = ii >= jj, ii > jj
    dm = jnp.where(causal, jnp.exp(jnp.where(causal, gc - gr, 0.0)), 0.0)
    kk = _bdot3(k, k, "nt")
    a = jnp.where(strict, beta * kk * dm, 0.0)
    if t is None:
        t = _tri_inv(a)
    eg = jnp.exp(gc)
    glast = gc[:, CHUNK - 1:CHUNK, :]
    ekd = jnp.exp(glast - gc)
    bv = beta * v
    bk = (beta * eg) * k
    ub = _bdot3(t, jnp.concatenate([bv, bk], axis=2))
    qk = _bdot3(q, k, "nt")
    return dict(causal=causal, strict=strict, dm=dm, kk=kk, a=a, t=t, eg=eg, ekd=ekd, bv=bv, bk=bk,
                u=ub[:, :, :HEAD_DIM], w=ub[:, :, HEAD_DIM:], qk=qk, aqk=jnp.where(causal, qk * dm, 0.0),
                q_dec=q * eg, k_dec=k * ekd, decay=jnp.exp(glast))


def _heads(ref, nh):
    return jnp.stack([ref[:, h * HEAD_DIM:(h + 1) * HEAD_DIM] for h in range(nh)], axis=0)


def _gdn_chunk_inputs(q_ref, k_ref, v_ref, g, gt, nh):
    col = lambda o: jnp.stack([g[:, o + h:o + h + 1] for h in range(nh)], axis=0)
    gr = jnp.stack([gt[3 * nh + h:3 * nh + h + 1, :] for h in range(nh)], axis=0)
    return _heads(q_ref, nh), _heads(k_ref, nh), _heads(v_ref, nh), col(0), col(3 * nh), gr


def _gdn_fwd(qkv, gates, gt3, nh, rider=None):
    lp = qkv.shape[0]
    nc = lp // CHUNK
    w = nh * HEAD_DIM

    def body(start, finish, q_ref, k_ref, v_ref, g_ref, gt_ref, o_ref, sall_ref, tall_ref, s_scr):
        @pl.when(pl.program_id(0) == 0)
        def _():
            start()
            s_scr[...] = jnp.zeros_like(s_scr)
        c = _gdn_chunk(*_gdn_chunk_inputs(q_ref, k_ref, v_ref, g_ref[...], gt_ref[0], nh))
        s = s_scr[...]
        sall_ref[0] = s
        tall_ref[0] = c["t"]
        v_new = c["u"] - _bdot3(c["w"], s)
        o = _bdot3(c["q_dec"], s) + _bdot3(c["aqk"], v_new)
        s_scr[...] = s * c["decay"] + _bdot3(c["k_dec"], v_new, "tn")
        for h in range(nh):
            o_ref[:, h * HEAD_DIM:(h + 1) * HEAD_DIM] = o[h]
        pl.when(pl.program_id(0) == nc - 1)(finish)

    outs, got = _hosted_call(
        body, [rider], 5, 3, 1, grid=(nc,),
        in_specs=[pl.BlockSpec((CHUNK, w), lambda n: (n, 0)), pl.BlockSpec((CHUNK, w), lambda n: (n, 1)),
                  pl.BlockSpec((CHUNK, w), lambda n: (n, 2)), pl.BlockSpec((CHUNK, LANES), lambda n: (n, 0)),
                  pl.BlockSpec((1, LANES, CHUNK), lambda n: (n, 0, 0))],
        out_specs=[pl.BlockSpec((CHUNK, w), lambda n: (n, 0)),
                   pl.BlockSpec((1, nh, HEAD_DIM, HEAD_DIM), lambda n: (n, 0, 0, 0)),
                   pl.BlockSpec((1, nh, CHUNK, CHUNK), lambda n: (n, 0, 0, 0))],
        out_shape=[jax.ShapeDtypeStruct((lp, w), F32), jax.ShapeDtypeStruct((nc, nh, HEAD_DIM, HEAD_DIM), F32),
                   jax.ShapeDtypeStruct((nc, nh, CHUNK, CHUNK), F32)],
        scratch_shapes=[pltpu.VMEM((nh, HEAD_DIM, HEAD_DIM), F32)],
        name="gdn_fwd", compiler_params=_cp("arbitrary"))(qkv, qkv, qkv, gates, gt3)
    return outs, (got[0] if got else None)


def _gdn_bwd(qkv, gates, gt3, s_all, t_all, do, nh, rider=None):
    lp = qkv.shape[0]
    nc = lp // CHUNK
    w = nh * HEAD_DIM
    rev = lambda n: nc - 1 - n

    def body(start, finish, q_ref, k_ref, v_ref, g_ref, gt_ref, s_ref, t_ref, do_ref, dq_ref, dk_ref, dv_ref, dg_ref, ds_scr):
        @pl.when(pl.program_id(0) == 0)
        def _():
            start()
            ds_scr[...] = jnp.zeros_like(ds_scr)
        q, k, v, beta, gc, gr = _gdn_chunk_inputs(q_ref, k_ref, v_ref, g_ref[...], gt_ref[0], nh)
        c = _gdn_chunk(q, k, v, beta, gc, gr, t_ref[0])
        s = s_ref[0]
        dsn = ds_scr[...]
        dout = _heads(do_ref, nh)
        v_new = c["u"] - _bdot3(c["w"], s)
        dq_dec = _bdot3(dout, s, "nt")
        daqk = jnp.where(c["causal"], _bdot3(dout, v_new, "nt"), 0.0)
        dv_new = _bdot3(c["aqk"], dout, "tn") + _bdot3(c["k_dec"], dsn)
        dk_dec = _bdot3(v_new, dsn, "nt")
        ddecay = jnp.sum(jnp.sum(dsn * s, axis=2, keepdims=True), axis=1, keepdims=True)
        dw = -_bdot3(dv_new, s, "nt")
        ds_scr[...] = _bdot3(c["q_dec"], dout, "tn") + c["decay"] * dsn - _bdot3(c["w"], dv_new, "tn")
        duw = jnp.concatenate([dv_new, dw], axis=2)
        dt = _bdot3(duw, jnp.concatenate([c["bv"], c["bk"]], axis=2), "nt")
        dbvk = _bdot3(c["t"], duw, "tn")
        dbv, dbk = dbvk[:, :, :HEAD_DIM], dbvk[:, :, HEAD_DIM:]
        da = jnp.where(c["strict"], -_bdot3(_bdot3(c["t"], dt, "tn"), c["t"], "nt"), 0.0)
        dkk = da * beta * c["dm"]
        dqk = daqk * c["dm"]
        e = da * c["a"] + daqk * c["aqk"]
        dq = dq_dec * c["eg"] + _bdot3(dqk, k)
        dk = (dk_dec * c["ekd"] + _bdot3(dkk, k) + _bdot3(dkk, k, "tn") + _bdot3(dqk, q, "tn")
              + (beta * c["eg"]) * dbk)
        dv = beta * dbv
        rs = lambda x: jnp.sum(x, axis=2, keepdims=True)
        dbeta = rs(dbv * v) + c["eg"] * rs(dbk * k) + rs(da * c["kk"] * c["dm"])
        kd_term = rs(dk_dec * c["k_dec"])
        eh, ew = _split(e)
        ones = jnp.ones((nh, CHUNK, LANES), BF16)
        col_sums = (_dot3(eh, ones, "tn") + _dot3(ew, ones, "tn"))[:, :, 0:1]
        dg_cum = rs(dq_dec * c["q_dec"]) - kd_term + rs(dbk * c["bk"]) + rs(e) - col_sums
        last = jnp.sum(kd_term, axis=1, keepdims=True) + ddecay * c["decay"]
        dg_cum = dg_cum + jnp.where(_iota((1, CHUNK, 1), 1) == CHUNK - 1, last, 0.0)
        lane = _iota((CHUNK, LANES), 1)
        acc = jnp.zeros((CHUNK, LANES), F32)
        for h in range(nh):
            sl = slice(h * HEAD_DIM, (h + 1) * HEAD_DIM)
            dq_ref[:, sl] = dq[h]
            dk_ref[:, sl] = dk[h]
            dv_ref[:, sl] = dv[h]
            acc = acc + jnp.where(lane == h, dbeta[h], 0.0) + jnp.where(lane == nh + h, dg_cum[h], 0.0)
        triu = (_iota((CHUNK, CHUNK), 0) <= _iota((CHUNK, CHUNK), 1)).astype(F32)
        dg_ref[...] = jnp.where(lane < nh, acc, _hdot(triu, acc))
        pl.when(pl.program_id(0) == nc - 1)(finish)

    outs, got = _hosted_call(
        body, [rider], 8, 4, 1, grid=(nc,),
        in_specs=[pl.BlockSpec((CHUNK, w), lambda n: (rev(n), 0)), pl.BlockSpec((CHUNK, w), lambda n: (rev(n), 1)),
                  pl.BlockSpec((CHUNK, w), lambda n: (rev(n), 2)), pl.BlockSpec((CHUNK, LANES), lambda n: (rev(n), 0)),
                  pl.BlockSpec((1, LANES, CHUNK), lambda n: (rev(n), 0, 0)),
                  pl.BlockSpec((1, nh, HEAD_DIM, HEAD_DIM), lambda n: (rev(n), 0, 0, 0)),
                  pl.BlockSpec((1, nh, CHUNK, CHUNK), lambda n: (rev(n), 0, 0, 0)),
                  pl.BlockSpec((CHUNK, w), lambda n: (rev(n), 0))],
        out_specs=[pl.BlockSpec((CHUNK, w), lambda n: (rev(n), 0))] * 3 + [pl.BlockSpec((CHUNK, LANES), lambda n: (rev(n), 0))],
        out_shape=[jax.ShapeDtypeStruct((lp, w), F32)] * 3 + [jax.ShapeDtypeStruct((lp, LANES), F32)],
        scratch_shapes=[pltpu.VMEM((nh, HEAD_DIM, HEAD_DIM), F32)],
        name="gdn_bwd", compiler_params=_cp("arbitrary"))(qkv, qkv, qkv, gates, gt3, s_all, t_all, do)
    return outs, (got[0] if got else None)


def _merge_gdn(o_gdn, proj, norm_w, nh):
    lp = o_gdn.shape[0]

    def body(o_ref, z_ref, w_ref, m_ref):
        o = o_ref[...]
        z = z_ref[...]
        m_ref[...] = (o * _rms(o) * w_ref[...] * (z * _sigmoid(z))).astype(BF16)

    return pl.pallas_call(
        body, grid=(nh,),
        in_specs=[pl.BlockSpec((lp, LANES), lambda s: (0, s)), pl.BlockSpec((lp, LANES), lambda s: (0, 3 * nh + s)),
                  pl.BlockSpec((1, LANES), lambda s: (0, 0))],
        out_specs=pl.BlockSpec((lp, LANES), lambda s: (0, s)),
        out_shape=jax.ShapeDtypeStruct((lp, 2 * nh * HEAD_DIM), BF16), name="merge_gdn",
        compiler_params=_cp("parallel"))(o_gdn, proj, norm_w)


def _merge_gdn_bwd(o_gdn, proj, norm_w, dmerged, nh):
    lp = o_gdn.shape[0]

    def body(o_ref, z_ref, w_ref, dm_ref, do_ref, dz_ref, dw_ref):
        o = o_ref[...]
        r = _rms(o)
        xh = o * r
        silu, dsilu = _silu_and_grad(z_ref[...])
        dm = dm_ref[...]
        dn = dm * silu
        dz_ref[...] = (dm * (xh * w_ref[...]) * dsilu).astype(BF16)
        dnw = dn * w_ref[...]
        do_ref[...] = r * (dnw - xh * jnp.mean(dnw * xh, axis=-1, keepdims=True))

        @pl.when(pl.program_id(0) == 0)
        def _():
            dw_ref[...] = jnp.zeros_like(dw_ref)
        dw_ref[...] += jnp.sum(dn * xh, axis=0, keepdims=True)

    w = nh * HEAD_DIM
    return pl.pallas_call(
        body, grid=(nh,),
        in_specs=[pl.BlockSpec((lp, LANES), lambda s: (0, s)), pl.BlockSpec((lp, LANES), lambda s: (0, 3 * nh + s)),
                  pl.BlockSpec((1, LANES), lambda s: (0, 0)), pl.BlockSpec((lp, LANES), lambda s: (0, s))],
        out_specs=[pl.BlockSpec((lp, LANES), lambda s: (0, s)), pl.BlockSpec((lp, LANES), lambda s: (0, 3 * nh + s)),
                   pl.BlockSpec((1, LANES), lambda s: (0, 0))],
        out_shape=[jax.ShapeDtypeStruct((lp, w), F32), jax.ShapeDtypeStruct((lp, 8 * w + NARROW), BF16),
                   jax.ShapeDtypeStruct((1, LANES), F32)],
        name="merge_gdn_bwd", compiler_params=_cp("arbitrary"))(o_gdn, proj, norm_w, dmerged)


def _fox_prep(proj, qk_w, nh):
    lp = proj.shape[0]

    def body(x_ref, w_ref, o_ref):
        x = x_ref[...]
        o_ref[...] = x * _rms(x) * w_ref[0]

    return pl.pallas_call(
        body, grid=(2 * nh,),
        in_specs=[pl.BlockSpec((lp, LANES), lambda s: (0, 4 * nh + s)), pl.BlockSpec((1, 1, LANES), lambda s: (s // nh, 0, 0))],
        out_specs=pl.BlockSpec((lp, LANES), lambda s: (0, s)),
        out_shape=jax.ShapeDtypeStruct((lp, 2 * nh * HEAD_DIM), F32), name="fox_prep",
        compiler_params=_cp("parallel"))(proj, qk_w)


def _fox_prep_bwd(proj, qk_w, dq, dk, dproj, nh):
    lp = proj.shape[0]
    part = lambda p: pl.BlockSpec((lp, LANES), lambda s: (0, jnp.clip(s - p * nh, 0, nh - 1)))

    def body(x_ref, w_ref, dq_ref, dk_ref, _, dx_ref, dw_ref):
        x = x_ref[...]
        r = _rms(x)
        xh = x * r
        dy = jnp.where(pl.program_id(0) < nh, dq_ref[...], dk_ref[...])
        dyw = dy * w_ref[0]
        dx_ref[...] = (r * (dyw - xh * jnp.mean(dyw * xh, axis=-1, keepdims=True))).astype(BF16)

        @pl.when(pl.program_id(0) % nh == 0)
        def _():
            dw_ref[...] = jnp.zeros_like(dw_ref)
        dw_ref[0] += jnp.sum(dy * xh, axis=0, keepdims=True)

    strip = pl.BlockSpec((lp, LANES), lambda s: (0, 4 * nh + s))
    wsp = pl.BlockSpec((1, 1, LANES), lambda s: (s // nh, 0, 0))
    return pl.pallas_call(
        body, grid=(2 * nh,), in_specs=[strip, wsp, part(0), part(1), _ANY], out_specs=[strip, wsp],
        out_shape=[jax.ShapeDtypeStruct(dproj.shape, BF16), jax.ShapeDtypeStruct((2, 1, LANES), F32)],
        input_output_aliases={4: 0}, name="fox_prep_bwd", compiler_params=_cp("arbitrary"))(proj, qk_w, dq, dk, dproj)


def _fox_probs(q, k, gates, crow, h, i, nh, lse=None):
    kl = k.shape[0]
    lane = _iota((Q_BLOCK, LANES), 1)
    ct = jnp.sum(jnp.where(lane == 4 * nh + h, gates, 0.0), axis=1, keepdims=True)
    tq, kq = _iota((Q_BLOCK, Q_BLOCK), 0), _iota((Q_BLOCK, Q_BLOCK), 1)
    qs = q * (HEAD_DIM ** -0.5)
    if i == 0:
        s = _bdot(qs, k, "nt") + (ct - crow)
        s = jnp.where((kq <= tq) & ((kq >= PAD_ROWS) | (tq < PAD_ROWS)), s, NEG)
    else:
        crow = jnp.where(_iota((1, kl), 1) < PAD_ROWS, -NEG, crow)
        s = _bdot(qs, k, "nt") + (ct - crow)
        s = jnp.concatenate([s[:, :kl - Q_BLOCK], jnp.where(kq <= tq, s[:, kl - Q_BLOCK:], NEG)], axis=1)
    if lse is not None:
        return jnp.exp(s - lse)
    m = jnp.max(s, axis=1, keepdims=True)
    p = jnp.exp(s - m)
    tot = jnp.sum(p, axis=1, keepdims=True)
    return p / tot, m + jnp.log(tot)


FOX_HEADS_PER_STEP = 2


def _fox_specs(lp, nh):
    hw = FOX_HEADS_PER_STEP * LANES
    return [pl.BlockSpec((Q_BLOCK, hw), lambda g, i: (i, g)),
            pl.BlockSpec((lp, hw), lambda g, i: (0, nh // FOX_HEADS_PER_STEP + g)),
            pl.BlockSpec((lp, hw), lambda g, i: (0, 6 * nh // FOX_HEADS_PER_STEP + g)),
            pl.BlockSpec((Q_BLOCK, LANES), lambda g, i: (i, 0)),
            pl.BlockSpec((LANES, lp), lambda g, i: (0, 0))]


def _fox_fwd(qkn, proj, gates, gtf, nh):
    lp = qkn.shape[0]

    def body(q_ref, k_ref, v_ref, g_ref, gt_ref, o_ref, lse_ref):
        g, i = pl.program_id(0), pl.program_id(1)
        for j in range(lp // Q_BLOCK):
            @pl.when(i == j)
            def _(j=j):
                kl = (j + 1) * Q_BLOCK
                for hh in range(FOX_HEADS_PER_STEP):
                    h = FOX_HEADS_PER_STEP * g + hh
                    sl = slice(hh * LANES, (hh + 1) * LANES)
                    p, lse = _fox_probs(q_ref[:, sl], k_ref[0:kl, sl], g_ref[...], gt_ref[pl.ds(4 * nh + h, 1), :][:, 0:kl],
                                        h, j, nh)
                    o_ref[:, sl] = _bdot(p, v_ref[0:kl, sl])
                    lse_ref[:, sl] = jnp.broadcast_to(lse, (Q_BLOCK, LANES))

    blk = pl.BlockSpec((Q_BLOCK, FOX_HEADS_PER_STEP * LANES), lambda g, i: (i, g))
    return pl.pallas_call(
        body, grid=(nh // FOX_HEADS_PER_STEP, lp // Q_BLOCK), in_specs=_fox_specs(lp, nh), out_specs=[blk, blk],
        out_shape=[jax.ShapeDtypeStruct((lp, nh * HEAD_DIM), F32)] * 2, name="fox_fwd",
        compiler_params=_cp("parallel", "parallel"))(qkn, qkn, proj, gates, gtf)


def _fox_bwd(qkn, proj, gates, gtf, lse, do, dproj, nh):
    lp = qkn.shape[0]
    nq = lp // Q_BLOCK
    w = nh * HEAD_DIM
    scale = HEAD_DIM ** -0.5

    def body(q_ref, k_ref, v_ref, g_ref, gt_ref, lse_ref, do_ref, _, dq_ref, dk_ref, dc_ref, dv_ref, dv_scr):
        g, i = pl.program_id(0), pl.program_id(1)

        @pl.when(i == 0)
        def _():
            dk_ref[...] = jnp.zeros_like(dk_ref)
            dv_scr[...] = jnp.zeros_like(dv_scr)
            dc_ref[...] = jnp.zeros_like(dc_ref)
        for j in range(nq):
            @pl.when(i == j)
            def _(j=j):
                kl = (j + 1) * Q_BLOCK
                for hh in range(FOX_HEADS_PER_STEP):
                    h = FOX_HEADS_PER_STEP * g + hh
                    sl = slice(hh * LANES, (hh + 1) * LANES)
                    q, k = q_ref[:, sl], k_ref[0:kl, sl]
                    p = _fox_probs(q, k, g_ref[...], gt_ref[pl.ds(4 * nh + h, 1), :][:, 0:kl], h, j, nh,
                                   lse_ref[:, sl][:, 0:1])
                    dout = do_ref[:, sl]
                    dp = _bdot(dout, v_ref[0:kl, sl], "nt")
                    ds = p * (dp - jnp.sum(p * dp, axis=1, keepdims=True))
                    dq_ref[:, sl] = _bdot(ds, k) * scale
                    dk_ref[0:kl, sl] += _bdot(ds, q * scale, "tn")
                    dv_scr[0:kl, sl] += _bdot(p, dout, "tn")
                    dc_ref[hh, :, 0:kl] -= jnp.sum(ds, axis=0, keepdims=True)

        @pl.when(i == nq - 1)
        def _():
            dv_ref[...] = dv_scr[...].astype(BF16)

    hw = FOX_HEADS_PER_STEP * LANES
    blk = pl.BlockSpec((Q_BLOCK, hw), lambda g, i: (i, g))
    col = pl.BlockSpec((lp, hw), lambda g, i: (0, g))
    return pl.pallas_call(
        body, grid=(nh // FOX_HEADS_PER_STEP, nq), in_specs=_fox_specs(lp, nh) + [blk, blk, _ANY],
        out_specs=[blk, col, pl.BlockSpec((FOX_HEADS_PER_STEP, 1, lp), lambda g, i: (g, 0, 0)),
                   pl.BlockSpec((lp, hw), lambda g, i: (0, 6 * nh // FOX_HEADS_PER_STEP + g))],
        out_shape=[jax.ShapeDtypeStruct((lp, w), F32)] * 2 + [jax.ShapeDtypeStruct((nh, 1, lp), F32),
                                                             jax.ShapeDtypeStruct(dproj.shape, BF16)],
        scratch_shapes=[pltpu.VMEM((lp, hw), F32)], input_output_aliases={7: 3},
        name="fox_bwd", compiler_params=_cp("parallel", "arbitrary"))(qkn, qkn, proj, gates, gtf, lse, do, dproj)


def _merge_fox(o_fox, proj, merged, nh):
    lp = o_fox.shape[0]

    def body(o_ref, z_ref, _, m_ref):
        z = z_ref[...]
        m_ref[...] = (o_ref[...] * (z * _sigmoid(z))).astype(BF16)

    return pl.pallas_call(
        body, grid=(nh,),
        in_specs=[pl.BlockSpec((lp, LANES), lambda s: (0, s)), pl.BlockSpec((lp, LANES), lambda s: (0, 7 * nh + s)), _ANY],
        out_specs=pl.BlockSpec((lp, LANES), lambda s: (0, nh + s)),
        out_shape=jax.ShapeDtypeStruct(merged.shape, BF16), input_output_aliases={2: 0}, name="merge_fox",
        compiler_params=_cp("parallel"))(o_fox, proj, merged)


def _merge_fox_bwd(o_fox, proj, dmerged, dproj, nh):
    lp = o_fox.shape[0]

    def body(o_ref, z_ref, dm_ref, _, do_ref, dz_ref):
        silu, dsilu = _silu_and_grad(z_ref[...])
        dm = dm_ref[...]
        do_ref[...] = dm * silu
        dz_ref[...] = (dm * o_ref[...] * dsilu).astype(BF16)

    w = nh * HEAD_DIM
    return pl.pallas_call(
        body, grid=(nh,),
        in_specs=[pl.BlockSpec((lp, LANES), lambda s: (0, s)), pl.BlockSpec((lp, LANES), lambda s: (0, 7 * nh + s)),
                  pl.BlockSpec((lp, LANES), lambda s: (0, nh + s)), _ANY],
        out_specs=[pl.BlockSpec((lp, LANES), lambda s: (0, s)), pl.BlockSpec((lp, LANES), lambda s: (0, 7 * nh + s))],
        out_shape=[jax.ShapeDtypeStruct((lp, w), F32), jax.ShapeDtypeStruct(dproj.shape, BF16)],
        input_output_aliases={3: 1}, name="merge_fox_bwd", compiler_params=_cp("parallel"))(o_fox, proj, dmerged, dproj)


def _post(out, x, target, post_w):
    lp, d = out.shape

    def body(o_ref, x_ref, t_ref, w_ref, dy_ref, do_ref, loss_ref, dw_ref):
        i = pl.program_id(0)

        @pl.when(i == 0)
        def _():
            loss_ref[...] = jnp.zeros_like(loss_ref)
            dw_ref[...] = jnp.zeros_like(dw_ref)
        o = o_ref[...]
        r = _rms(o)
        nrm = o * r
        err = jnp.where(i > 0, x_ref[...] + nrm * w_ref[...] - t_ref[...], 0.0)
        loss_ref[0:1, :] += 0.5 * jnp.sum(jnp.sum(err * err, axis=1, keepdims=True), axis=0, keepdims=True) / d
        dy = err / d
        dy_ref[...] = dy
        dw_ref[...] += jnp.sum(dy * nrm, axis=0, keepdims=True)
        dyw = dy * w_ref[...]
        do_ref[...] = (r * (dyw - nrm * jnp.mean(dyw * nrm, axis=-1, keepdims=True))).astype(BF16)

    row = pl.BlockSpec((Q_BLOCK, d), lambda i: (i, 0))
    vec = pl.BlockSpec((1, d), lambda i: (0, 0))
    return pl.pallas_call(
        body, grid=(lp // Q_BLOCK,), in_specs=[row, _x_rows(d), _x_rows(d), vec],
        out_specs=[_x_rows(d), row, pl.BlockSpec((8, LANES), lambda i: (0, 0)), vec],
        out_shape=[jax.ShapeDtypeStruct(x.shape, F32), jax.ShapeDtypeStruct((lp, d), BF16),
                   jax.ShapeDtypeStruct((8, LANES), F32), jax.ShapeDtypeStruct((1, d), F32)],
        name="post", compiler_params=_cp("arbitrary"))(out, x, target, post_w)


def _prenorm_bwd(dxn, x, meta, w, dy, rider=None):
    seq, d = x.shape
    lp = seq + Q_BLOCK

    def body(start, finish, dx_ref, x_ref, m_ref, w_ref, dy_ref, gx_ref, gm_ref, dw_ref):
        i = pl.program_id(0)
        pl.when(i == 0)(start)
        h = _h_tile(i, x_ref, m_ref)
        r = _rms(h)
        xh = h * r
        dxn_ = dx_ref[...]
        dxw = dxn_ * w_ref[...]
        dh = jnp.where(i > 0, dy_ref[...], 0.0) + r * (dxw - xh * jnp.mean(dxw * xh, axis=-1, keepdims=True))
        gx_ref[...] = dh

        @pl.when(i == 0)
        def _():
            dw_ref[...] = jnp.zeros_like(dw_ref)
            gm_ref[...] = dh[PAD_ROWS:, :]
        dw_ref[...] += jnp.sum(dxn_ * xh, axis=0, keepdims=True)
        pl.when(i == lp // Q_BLOCK - 1)(finish)

    vec = pl.BlockSpec((1, d), lambda i: (0, 0))
    met = pl.BlockSpec((N_META, d), lambda i: (0, 0))
    outs, got = _hosted_call(
        body, [rider], 5, 3, 0, grid=(lp // Q_BLOCK,),
        in_specs=[pl.BlockSpec((Q_BLOCK, d), lambda i: (i, 0)), _x_rows(d), met, vec, _x_rows(d)],
        out_specs=[_x_rows(d), met, vec],
        out_shape=[jax.ShapeDtypeStruct((seq, d), F32), jax.ShapeDtypeStruct((N_META, d), F32),
                   jax.ShapeDtypeStruct((1, d), F32)],
        name="prenorm_bwd", compiler_params=_cp("arbitrary"))(dxn, x, meta, w, dy)
    return outs, (got[0] if got else None)


def _layer_grads(x, target, meta, pre_w, wfull, conv_wt, a_log, dt_bias, gdn_norm_w, fq_w, fk_w, f_bias, w_out, post_w,
                 late_weights=None, w_out_grads=None):
    nh = a_log.shape[1]
    zpad = jnp.zeros((1, LANES - 3 * nh), F32)
    bias_row = jnp.concatenate([jnp.zeros((1, nh), F32), dt_bias, f_bias, zpad], axis=1)
    nega_row = jnp.concatenate([jnp.zeros((1, nh), F32), -jnp.exp(a_log), jnp.zeros((1, nh), F32), zpad], axis=1)
    qk_w = jnp.stack([fq_w, fk_w])

    xn = _prenorm(x, meta, pre_w)
    proj = _matmul(xn, wfull, "nn", MM_TILE, F32, "proj")
    qkv = _gdn_prep(proj, conv_wt, nh)
    gates, gt3, gtf = _gates(proj, bias_row, nega_row, nh)
    (o_gdn, s_all, t_all), got = _gdn_fwd(qkv, gates, gt3, nh, None if late_weights is None else late_weights[0])
    if late_weights is not None:
        w_out = late_weights[1](got)
    qkn = _fox_prep(proj, qk_w, nh)
    o_fox, fox_lse = _fox_fwd(qkn, proj, gates, gtf, nh)
    merged = _merge_fox(o_fox, proj, _merge_gdn(o_gdn, proj, gdn_norm_w, nh), nh)
    out = _matmul(merged, w_out, "nn", 4 * LANES, F32, "out_proj")
    dy, dout, loss_blk, dpost_w = _post(out, x, target, post_w)

    dw_out = _matmul(merged, dout, "tn", 4 * LANES, BF16, "dw_out")
    if w_out_grads is None:
        dmerged, gdn_rider = _matmul(dout, w_out, "nt", 4 * LANES, F32, "dmerged"), None
    else:
        dmerged, got = _matmul(dout, w_out, "nt", 4 * LANES, F32, "dmerged", w_out_grads[0](dw_out))
        gdn_rider = w_out_grads[1](dw_out, got)
    do_gdn, dproj, dgdn_norm_w = _merge_gdn_bwd(o_gdn, proj, gdn_norm_w, dmerged, nh)
    do_fox, dproj = _merge_fox_bwd(o_fox, proj, dmerged, dproj, nh)
    dqn, dkn, dc_t, dproj = _fox_bwd(qkn, proj, gates, gtf, fox_lse, do_fox, dproj, nh)
    dproj, dqk_w = _fox_prep_bwd(proj, qk_w, dqn, dkn, dproj, nh)
    (dgq, dgk, dgv, dgate), w_out_parts = _gdn_bwd(qkv, gates, gt3, s_all, t_all, do_gdn, nh, gdn_rider)
    dproj, dconv_wt = _gdn_prep_bwd(proj, conv_wt, dgq, dgk, dgv, dproj, nh)
    dc_rows = jnp.pad(dc_t.reshape(nh, -1), ((2 * nh, LANES - 3 * nh), (0, 0)))
    dproj, gate_sums = _gates_bwd(proj, bias_row, nega_row, gates, dgate, dc_rows, dproj, nh)
    return dict(
        loss=loss_blk[0:1, 0:1], dy=dy, xn=xn, dproj=dproj, post_w=dpost_w,
        conv_wt=dconv_wt, a_log=gate_sums[1:2, nh:2 * nh], dt_bias=gate_sums[0:1, nh:2 * nh],
        gdn_norm_w=dgdn_norm_w, fq_w=dqk_w[0], fk_w=dqk_w[1], f_bias=gate_sums[0:1, 2 * nh:3 * nh], w_out=dw_out,
        w_out_parts=w_out_parts)


def _cast_bf16(a, tr, name):
    r, c = a.shape

    def body(a_ref, o_ref):
        o_ref[...] = a_ref[...].astype(BF16)

    return pl.pallas_call(
        body, grid=(r // tr,), in_specs=[pl.BlockSpec((tr, c), lambda i: (i, 0))],
        out_specs=pl.BlockSpec((tr, c), lambda i: (i, 0)), out_shape=jax.ShapeDtypeStruct((r, c), BF16),
        name=name, compiler_params=_cp("parallel"))(a)


def _column_major(a):
    return jnp.transpose(a, (2, 0, 1))


def _cast_bf16_column_major(a3, name):
    _, r, c = a3.shape

    def body(a_ref, o_ref):
        o_ref[...] = a_ref[...].reshape(LANES, r).T.astype(BF16)

    return pl.pallas_call(
        body, grid=(pl.cdiv(c, LANES),), in_specs=[pl.BlockSpec((LANES, 1, r), lambda i: (i, 0, 0))],
        out_specs=pl.BlockSpec((r, LANES), lambda i: (0, i)), out_shape=jax.ShapeDtypeStruct((r, c), BF16),
        name=name, compiler_params=_cp("parallel"))(_column_major(a3))


def _adamw_column_major(w3, parts, m3, v3, name):
    _, r, c = w3.shape
    n_parts = parts.shape[0]

    def body(w_ref, p_ref, m_ref, v_ref, g_ref, d_ref, nm_ref, nv_ref):
        g = p_ref[0].astype(F32)
        for s in range(1, n_parts):
            g = g + p_ref[s].astype(F32)
        g = g.T
        flat = lambda ref: ref[...].reshape(LANES, r)
        m_new = ADAM_B1 * flat(m_ref) + (1.0 - ADAM_B1) * g
        v_new = ADAM_B2 * flat(v_ref) + (1.0 - ADAM_B2) * (g * g)
        m_hat = m_new / (1.0 - ADAM_B1 ** ADAM_STEP)
        v_hat = v_new / (1.0 - ADAM_B2 ** ADAM_STEP)
        delta = -ADAM_LR * (m_hat / (jnp.sqrt(v_hat) + ADAM_EPS) + ADAM_WD * flat(w_ref))
        for ref, val in ((g_ref, g), (d_ref, delta), (nm_ref, m_new), (nv_ref, v_new)):
            ref[...] = val.reshape(LANES, 1, r)

    blk = pl.BlockSpec((LANES, 1, r), lambda i: (i, 0, 0))
    outs = pl.pallas_call(
        body, grid=(pl.cdiv(c, LANES),), in_specs=[blk, pl.BlockSpec((n_parts, r, LANES), lambda i: (0, 0, i)), blk, blk],
        out_specs=[blk] * 4, out_shape=[jax.ShapeDtypeStruct((c, 1, r), F32)] * 4, name=name,
        compiler_params=_cp("parallel"))(_column_major(w3), parts, _column_major(m3), _column_major(v3))
    return [jnp.transpose(o, (1, 2, 0)) for o in outs]


def _gather_copies(ins, outs, send_sems, recv_sems, local_sems):
    n = len(ins)
    x, y, c = lax.axis_index("x"), lax.axis_index("y"), lax.axis_index("c")
    me, sibling = (x, y, c), (x, y, 1 - c)
    xn, yn, dg = (1 - x, y), (x, 1 - y), (1 - x, 1 - y)

    def copy(a, k, block, to, src=None):
        px, py, pc = block
        rows = outs[a].at[4 * px + 2 * py + pc]
        return pltpu.make_async_remote_copy(
            src_ref=rows if src is None else src, dst_ref=rows, send_sem=send_sems.at[a, k],
            recv_sem=recv_sems.at[a, k], device_id=to, device_id_type=_MESH)

    local = [pltpu.make_async_copy(ins[a], outs[a].at[4 * x + 2 * y + c], local_sems.at[a]) for a in range(n)]
    own = [cp for a in range(n) for cp in (copy(a, 0, me, sibling, src=ins[a]), copy(a, 1, me, (*xn, c), src=ins[a]),
                                           copy(a, 2, me, (*yn, c), src=ins[a]))]

    def start():
        for cp in local + own:
            cp.start()

    def finish():
        for a in range(n):
            @pl.when(c == 1)
            def _(a=a):
                copy(a, 1, (*xn, c), me).wait_recv()
                copy(a, 3, (*xn, c), (*yn, c)).start()

            @pl.when(c == 0)
            def _(a=a):
                copy(a, 2, (*yn, c), me).wait_recv()
                copy(a, 3, (*yn, c), (*xn, c)).start()
        for a in range(n):
            pl.when(c == 0)(copy(a, 1, (*xn, c), me).wait_recv)
            copy(a, 4, (*xn, c), sibling).start()
            pl.when(c == 1)(copy(a, 2, (*yn, c), me).wait_recv)
            copy(a, 5, (*yn, c), sibling).start()
        for a in range(n):
            copy(a, 3, (*dg, c), me).wait_recv()
            copy(a, 6, (*dg, c), sibling).start()
        for a in range(n):
            copy(a, 0, sibling, me).wait_recv()
            for k, chip in ((4, xn), (5, yn), (6, dg)):
                copy(a, k, (*chip, 1 - c), me).wait_recv()
                copy(a, k, (*chip, c), sibling).wait_send()
            copy(a, 3, (*xn, c), (*yn, c)).wait_send()
        for cp in own:
            cp.wait_send()
        for cp in local:
            cp.wait()

    return start, finish


def _gather_scratch(n):
    return [pltpu.SemaphoreType.DMA((n, N_DEV - 1)), pltpu.SemaphoreType.DMA((n, N_DEV - 1)), pltpu.SemaphoreType.DMA((n,))]


def _gather_rider(arrays):
    return _Rider(list(arrays), [jax.ShapeDtypeStruct((N_DEV,) + a.shape, a.dtype) for a in arrays],
                  _gather_scratch(len(arrays)), {}, lambda ins, outs, scratch: _gather_copies(ins, outs, *scratch))


def _all_gather(arrays, name):
    n = len(arrays)

    def body(*refs):
        start, finish = _gather_copies(refs[:n], refs[n:2 * n], *refs[2 * n:])
        start()
        finish()

    return pl.pallas_call(
        body, in_specs=[_ANY] * n, out_specs=[_ANY] * n,
        out_shape=[jax.ShapeDtypeStruct((N_DEV,) + a.shape, a.dtype) for a in arrays],
        scratch_shapes=_gather_scratch(n), name=name)(*arrays)


SLAB = 10 * LANES


def _slab_start(blk, nh, cols):
    in_second_half = blk >= N_DEV // 2
    shift = (2 * nh if in_second_half else 0) if isinstance(blk, int) else jnp.where(in_second_half, 2 * nh, 0)
    return (blk * cols - shift) // LANES * LANES


def _pair_rider(dw_rows=None, parts=None, nh=None):
    if dw_rows is not None:
        r, full = dw_rows.shape
        cols = (full - NARROW + 3 * nh) // N_DEV
        out_shapes = [jax.ShapeDtypeStruct((N_CHIP, r, SLAB), dw_rows.dtype), jax.ShapeDtypeStruct((r, NARROW), dw_rows.dtype)]
    else:
        out_shapes = [jax.ShapeDtypeStruct((N_CHIP,) + parts.shape[1:], parts.dtype)]

    def make(ins, outs, scratch):
        send_sems, recv_sems = scratch
        x, y, c = lax.axis_index("x"), lax.axis_index("y"), lax.axis_index("c")
        kw = lambda k: dict(send_sem=send_sems.at[k], recv_sem=recv_sems.at[k], device_id=(x, y, 1 - c), device_id_type=_MESH)
        copies = []
        for q in range(N_CHIP):
            if dw_rows is not None:
                first = pl.multiple_of(_slab_start(2 * q + 1 - c, nh, cols), LANES)
                copies.append(pltpu.make_async_remote_copy(src_ref=ins[0].at[:, pl.ds(first, SLAB)], dst_ref=outs[0].at[q], **kw(q)))
            else:
                copies.append(pltpu.make_async_remote_copy(src_ref=ins[0].at[2 * q + 1 - c], dst_ref=outs[0].at[q], **kw(q)))
        if dw_rows is not None:
            copies.append(pltpu.make_async_remote_copy(src_ref=ins[0].at[:, pl.ds(full - NARROW, NARROW)], dst_ref=outs[1],
                                                       **kw(N_CHIP)))

        def start():
            for cp in copies:
                cp.start()

        def finish():
            for cp in copies:
                cp.wait()

        return start, finish

    return _Rider([dw_rows if dw_rows is not None else parts], out_shapes,
                  [pltpu.SemaphoreType.DMA((N_CHIP + 1,)), pltpu.SemaphoreType.DMA((N_CHIP + 1,))], {}, make)


def _relayout_pair_sum(dwfull, got_slabs, got_tail, core, nh, tr, name):
    d, full = dwfull.shape
    w = nh * HEAD_DIM
    cols = (8 * w + 3 * nh) // N_DEV
    segs = _native_segments(nh)

    def block(f_ref, s_ref, t_ref, q, blk):
        st = _slab_start(blk, nh, cols)
        wide = f_ref[:, st:st + SLAB].astype(F32) + s_ref[q].astype(F32)
        tail = f_ref[:, 8 * w:].astype(F32) + t_ref[...].astype(F32)
        pieces = []
        for s0, s1, t0 in segs:
            lo, hi = max(s0, blk * cols), min(s1, (blk + 1) * cols)
            if lo < hi:
                at = t0 + lo - s0
                pieces.append(tail[:, at - 8 * w:at - 8 * w + hi - lo] if at >= 8 * w else wide[:, at - st:at - st + hi - lo])
        return (pieces[0] if len(pieces) == 1 else jnp.concatenate(pieces, axis=1)).astype(dwfull.dtype)

    def body(core_ref, f_ref, s_ref, t_ref, o_ref):
        for parity in range(2):
            @pl.when(core_ref[0] == parity)
            def _(parity=parity):
                for q in range(N_CHIP):
                    o_ref[q] = block(f_ref, s_ref, t_ref, q, 2 * q + parity)

    return pl.pallas_call(
        body,
        grid_spec=pltpu.PrefetchScalarGridSpec(
            num_scalar_prefetch=1, grid=(d // tr,),
            in_specs=[pl.BlockSpec((tr, full), lambda i, c_ref: (i, 0)), pl.BlockSpec((N_CHIP, tr, SLAB), lambda i, c_ref: (0, i, 0)),
                      pl.BlockSpec((tr, NARROW), lambda i, c_ref: (i, 0))],
            out_specs=pl.BlockSpec((N_CHIP, tr, cols), lambda i, c_ref: (0, i, 0))),
        out_shape=jax.ShapeDtypeStruct((N_CHIP, d, cols), dwfull.dtype), name=name,
        compiler_params=_cp("parallel"))(core, dwfull, got_slabs, got_tail)


def _pair_sum(parts, got, core, tr, name):
    _, r, c = parts.shape

    def body(core_ref, p_ref, g_ref, o_ref):
        o_ref[...] = (p_ref[...].astype(F32) + g_ref[...].astype(F32)).astype(o_ref.dtype)

    return pl.pallas_call(
        body,
        grid_spec=pltpu.PrefetchScalarGridSpec(
            num_scalar_prefetch=1, grid=(N_CHIP, r // tr),
            in_specs=[pl.BlockSpec((1, tr, c), lambda q, i, core_ref: (2 * q + core_ref[0], i, 0)),
                      pl.BlockSpec((1, tr, c), lambda q, i, core_ref: (q, i, 0))],
            out_specs=pl.BlockSpec((1, tr, c), lambda q, i, core_ref: (q, i, 0))),
        out_shape=jax.ShapeDtypeStruct((N_CHIP, r, c), parts.dtype), name=name,
        compiler_params=_cp("parallel", "parallel"))(core, parts, got)


def _native_segments(nh):
    w = nh * HEAD_DIM
    return [(0, 4 * w, 0), (4 * w, 4 * w + 2 * nh, 8 * w), (4 * w + 2 * nh, 8 * w + 2 * nh, 4 * w),
            (8 * w + 2 * nh, 8 * w + 3 * nh, 8 * w + 2 * nh)]


def _relayout_w_in(wg, nh, tr):
    _, d, cols = wg.shape
    w = nh * HEAD_DIM

    def native(ref, j0, j1):
        out = []
        while j0 < j1:
            blk = j0 // cols
            end = min(j1, (blk + 1) * cols)
            out.append(ref[blk, :, pl.ds(j0 - blk * cols, end - j0)])
            j0 = end
        return out

    def body(g_ref, o_ref):
        for cidx in range(8 * w // LANES):
            j0 = cidx * LANES + (0 if cidx * LANES < 4 * w else 2 * nh)
            pieces = native(g_ref, j0, j0 + LANES)
            o_ref[:, cidx * LANES:(cidx + 1) * LANES] = pieces[0] if len(pieces) == 1 else jnp.concatenate(pieces, axis=1)
        pieces = (native(g_ref, 4 * w, 4 * w + 2 * nh) + native(g_ref, 8 * w + 2 * nh, 8 * w + 3 * nh)
                  + [jnp.zeros((tr, NARROW - 3 * nh), wg.dtype)])
        o_ref[:, 8 * w:] = jnp.concatenate(pieces, axis=1)

    return pl.pallas_call(
        body, grid=(d // tr,), in_specs=[pl.BlockSpec((N_DEV, tr, cols), lambda i: (0, i, 0))],
        out_specs=pl.BlockSpec((tr, 8 * w + NARROW), lambda i: (i, 0)),
        out_shape=jax.ShapeDtypeStruct((d, 8 * w + NARROW), wg.dtype),
        name="relayout_w_in", compiler_params=_cp("parallel"))(wg)


def _adamw(w, parts, m, v, tr, name):
    r, c = w.shape
    n_parts = parts.shape[0]

    def body(w_ref, p_ref, m_ref, v_ref, g_ref, d_ref, nm_ref, nv_ref):
        g = p_ref[0].astype(F32)
        for s in range(1, n_parts):
            g = g + p_ref[s].astype(F32)
        m_new = ADAM_B1 * m_ref[...] + (1.0 - ADAM_B1) * g
        v_new = ADAM_B2 * v_ref[...] + (1.0 - ADAM_B2) * (g * g)
        m_hat = m_new / (1.0 - ADAM_B1 ** ADAM_STEP)
        v_hat = v_new / (1.0 - ADAM_B2 ** ADAM_STEP)
        g_ref[...] = g
        d_ref[...] = -ADAM_LR * (m_hat / (jnp.sqrt(v_hat) + ADAM_EPS) + ADAM_WD * w_ref[...])
        nm_ref[...] = m_new
        nv_ref[...] = v_new

    blk = pl.BlockSpec((tr, c), lambda i: (i, 0))
    return pl.pallas_call(
        body, grid=(r // tr,), in_specs=[blk, pl.BlockSpec((n_parts, tr, c), lambda i: (0, i, 0)), blk, blk],
        out_specs=[blk] * 4, out_shape=[jax.ShapeDtypeStruct((r, c), F32)] * 4, name=name,
        compiler_params=_cp("parallel"))(w, parts, m, v)


def _pack_small(d, pre, post, a_log, dt_bias, f_bias, gdn_w, fq_w, fk_w, extra):
    row2 = jnp.concatenate([a_log, dt_bias, f_bias, gdn_w, fq_w, fk_w, extra], axis=1)
    row2 = jnp.pad(row2, ((0, 0), (0, d - row2.shape[1])))
    return jnp.concatenate([pre, post, row2, jnp.zeros((5, d), F32)], axis=0)


def _unpack_small(p, nh):
    o = 3 * nh
    return dict(pre=p[0:1], post=p[1:2], a_log=p[2:3, 0:nh], dt_bias=p[2:3, nh:2 * nh], f_bias=p[2:3, 2 * nh:o],
                gdn_w=p[2:3, o:o + HEAD_DIM], fq_w=p[2:3, o + HEAD_DIM:o + 2 * HEAD_DIM],
                fk_w=p[2:3, o + 2 * HEAD_DIM:o + 3 * HEAD_DIM], extra=p[2, o + 3 * HEAD_DIM])


def kernel(x, meta_tokens, pre_norm_w, w_in, conv_w, a_log, dt_bias, gdn_norm_w, fox_q_norm_w, fox_k_norm_w, fox_f_bias, w_out, post_norm_w, loss_target, m_meta_tokens, m_pre_norm_w, m_w_in, m_conv_w, m_a_log, m_dt_bias, m_gdn_norm_w, m_fox_q_norm_w, m_fox_k_norm_w, m_fox_f_bias, m_w_out, m_post_norm_w, v_meta_tokens, v_pre_norm_w, v_w_in, v_conv_w, v_a_log, v_dt_bias, v_gdn_norm_w, v_fox_q_norm_w, v_fox_k_norm_w, v_fox_f_bias, v_w_out, v_post_norm_w):
    nh = a_log.shape[1]
    d = x.shape[-1]
    w = nh * HEAD_DIM
    zero = jnp.zeros((1, 1), F32)

    wg, cg, mg = _all_gather([_cast_bf16_column_major(w_in, "cast_w_in"), conv_w[0].T, meta_tokens], "gather_weights")
    wfull = _relayout_w_in(wg, nh, 256)
    conv_wt = cg.transpose(1, 0, 2).reshape(CONV_WIDTH, 3 * w)
    meta_full = mg.transpose(1, 0, 2).reshape(N_META, d)
    late_weights = (_gather_rider([_cast_bf16(w_out[0], 256, "cast_w_out")]), lambda got: got[0].reshape(2 * w, d))
    core = lax.axis_index("c")
    dev = 4 * lax.axis_index("x") + 2 * lax.axis_index("y") + core
    core_arr = jnp.reshape(core, (1,)).astype(jnp.int32)

    out_parts = lambda dw_out: dw_out.reshape(N_DEV, 2 * w // N_DEV, d)
    g = _layer_grads(
        x[0], loss_target[0], meta_full, pre_norm_w, wfull, conv_wt, a_log, dt_bias, gdn_norm_w,
        fox_q_norm_w, fox_k_norm_w, fox_f_bias, None, post_norm_w, late_weights=late_weights,
        w_out_grads=(lambda dw_out: _pair_rider(parts=out_parts(dw_out)),
                     lambda dw_out, got: _chip_rider(_pair_sum(out_parts(dw_out), got[0], core_arr, 256, "pair_sum_w_out"))))
    p_out = g["w_out_parts"][0]
    xn, dproj, half = g["xn"], g["dproj"], d // 2
    dw_a = _matmul(xn, dproj, "tn", MM_TILE, BF16, "dw_in_a", a_cols=(half, 0))
    dw_b, got_a = _matmul(xn, dproj, "tn", MM_TILE, BF16, "dw_in_b", _pair_rider(dw_rows=dw_a, nh=nh), a_cols=(half, 1))
    sums_a = _relayout_pair_sum(dw_a, got_a[0], got_a[1], core_arr, nh, 128, "relayout_pair_sum_a")
    dxn, (p_in_a, got_b) = _dxn(dproj, wfull, [_chip_rider(sums_a, rows_total=d), _pair_rider(dw_rows=dw_b, nh=nh)],
                                MM_TILE, "dxn")
    sums_b = _relayout_pair_sum(dw_b, got_b[0], got_b[1], core_arr, nh, 128, "relayout_pair_sum_b")
    (grad_x, dmeta, dpre_w), p_in = _prenorm_bwd(dxn, x[0], meta_full, pre_norm_w, g["dy"],
                                                 _chip_rider(sums_b, rows_total=d, row0=half, into=p_in_a[0]))
    p_in = p_in[0]
    small = _pack_small(d, dpre_w, g["post_w"], g["a_log"], g["dt_bias"], g["f_bias"], g["gdn_norm_w"], g["fq_w"],
                        g["fk_w"], g["loss"])
    a_conv, a_meta, p_small = _all_gather([g["conv_wt"], dmeta, small], "gather_small_grads")
    p_conv = lax.dynamic_slice_in_dim(a_conv, dev * conv_w.shape[1], conv_w.shape[1], axis=2).transpose(0, 2, 1)
    p_meta = lax.dynamic_slice_in_dim(a_meta, dev * meta_tokens.shape[1], meta_tokens.shape[1], axis=2)

    r_in = _adamw_column_major(w_in, p_in, m_w_in, v_w_in, "adamw_w_in")
    r_out = _adamw(w_out[0], p_out, m_w_out[0], v_w_out[0], 64, "adamw_w_out")
    r_conv = _adamw(conv_w[0], p_conv, m_conv_w[0], v_conv_w[0], conv_w.shape[1], "adamw_conv_w")
    r_meta = _adamw(meta_tokens, p_meta, m_meta_tokens, v_meta_tokens, N_META, "adamw_meta")
    pk = lambda pre, post, a, dt, gw, fq, fk, fb: _pack_small(d, pre, post, a, dt, fb, gw, fq, fk, zero)
    r_small = _adamw(
        pk(pre_norm_w, post_norm_w, a_log, dt_bias, gdn_norm_w, fox_q_norm_w, fox_k_norm_w, fox_f_bias), p_small,
        pk(m_pre_norm_w, m_post_norm_w, m_a_log, m_dt_bias, m_gdn_norm_w, m_fox_q_norm_w, m_fox_k_norm_w, m_fox_f_bias),
        pk(v_pre_norm_w, v_post_norm_w, v_a_log, v_dt_bias, v_gdn_norm_w, v_fox_q_norm_w, v_fox_k_norm_w, v_fox_f_bias),
        8, "adamw_small")

    sm = [_unpack_small(r, nh) for r in r_small]
    outs = []
    for i in range(4):
        s = sm[i]
        outs += [r_meta[i], s["pre"], r_in[i], r_conv[i][None], s["a_log"], s["dt_bias"], s["gdn_w"], s["fq_w"],
                 s["fk_w"], s["f_bias"], r_out[i][None], s["post"]]
    return (sm[0]["extra"], grad_x[None], *outs)
```

```python
import jax
import jax.numpy as jnp
from jax import lax
from jax.experimental import pallas as pl
from jax.experimental.pallas import tpu as pltpu

F32, BF16 = jnp.float32, jnp.bfloat16
HEAD_DIM = 128
N_META = 16
CONV_WIDTH = 4
CHUNK = 128
Q_BLOCK = 128
LANES = 128
EPS = 1e-6
PAD_ROWS = Q_BLOCK - N_META
N_DEV = 8
N_CHIP = 4
VMEM_LIMIT = 56 * 1024 * 1024
NEG = -1e30
NARROW = 2 * LANES
MM_TILE = 6 * LANES

ADAM_LR, ADAM_B1, ADAM_B2, ADAM_EPS, ADAM_WD, ADAM_STEP = 0.001, 0.9, 0.999, 1e-08, 0.01, 10

_DN = {"nn": (((1,), (0,)), ((), ())), "nt": (((1,), (1,)), ((), ())), "tn": (((0,), (0,)), ((), ()))}
_DN3 = {"nn": (((2,), (1,)), ((0,), (0,))), "nt": (((2,), (2,)), ((0,), (0,))), "tn": (((1,), (1,)), ((0,), (0,)))}
_ANY = pl.BlockSpec(memory_space=pl.ANY)
_MESH = pl.DeviceIdType.MESH


def _cp(*sem):
    return pltpu.CompilerParams(dimension_semantics=sem, vmem_limit_bytes=VMEM_LIMIT)


def _dot(a, b, dims="nn", prec=None):
    return lax.dot_general(a, b, _DN[dims], precision=prec, preferred_element_type=F32)


def _bdot(a, b, dims="nn"):
    return _dot(a.astype(BF16), b.astype(BF16), dims)


def _hdot(a, b, dims="nn"):
    return _dot(a, b, dims, prec=lax.Precision.HIGHEST)


def _dot3(a, b, dims="nn"):
    return lax.dot_general(a, b, _DN3[dims], preferred_element_type=F32)


def _bdot3(a, b, dims="nn"):
    return _dot3(a.astype(BF16), b.astype(BF16), dims)


def _split(a):
    hi = a.astype(BF16)
    return hi, (a - hi.astype(F32)).astype(BF16)


def _iota(shape, dim):
    return lax.broadcasted_iota(jnp.int32, shape, dim)


def _sigmoid(z):
    return 1.0 / (1.0 + jnp.exp(-z))


def _softplus(z):
    e = jnp.exp(-jnp.abs(z))
    u = 1.0 + e
    l1p = jnp.where(u == 1.0, e, jnp.log(u) * (e / jnp.where(u == 1.0, 1.0, u - 1.0)))
    return jnp.maximum(z, 0.0) + l1p


def _silu_and_grad(z):
    s = _sigmoid(z)
    return z * s, s * (1.0 + z * (1.0 - s))


def _rms(x):
    return lax.rsqrt(jnp.mean(x * x, axis=-1, keepdims=True) + EPS)


def _h_tile(i, x_ref, meta_ref):
    first = jnp.concatenate([jnp.zeros((PAD_ROWS, x_ref.shape[1]), F32), meta_ref[...]], axis=0)
    return jnp.where(i == 0, first, x_ref[...])


def _x_rows(d):
    return pl.BlockSpec((Q_BLOCK, d), lambda i: (jnp.maximum(i - 1, 0), 0))


def _prenorm(x, meta, w):
    seq, d = x.shape
    lp = seq + Q_BLOCK

    def body(x_ref, m_ref, w_ref, o_ref):
        h = _h_tile(pl.program_id(0), x_ref, m_ref)
        o_ref[...] = (h * _rms(h) * w_ref[...]).astype(BF16)

    return pl.pallas_call(
        body, grid=(lp // Q_BLOCK,),
        in_specs=[_x_rows(d), pl.BlockSpec((N_META, d), lambda i: (0, 0)), pl.BlockSpec((1, d), lambda i: (0, 0))],
        out_specs=pl.BlockSpec((Q_BLOCK, d), lambda i: (i, 0)),
        out_shape=jax.ShapeDtypeStruct((lp, d), BF16), name="prenorm", compiler_params=_cp("parallel"))(x, meta, w)


def _tile(n, want):
    return max(t for t in range(LANES, want + 1, LANES) if n % t == 0)


class _Rider:
    def __init__(self, inputs, out_shapes, scratch, aliases, make):
        self.inputs, self.out_shapes, self.scratch, self.aliases, self.make = inputs, out_shapes, scratch, aliases, make


def _hosted_call(body, riders, n_in, n_out, n_scratch, *, in_specs, out_specs, out_shape, scratch_shapes=(), aliases=None,
                 **kw):
    riders = [r for r in riders if r is not None]
    r_in = [len(r.inputs) for r in riders]
    r_out = [len(r.out_shapes) for r in riders]
    r_scr = [len(r.scratch) for r in riders]
    al = dict(aliases or {})
    for k, r in enumerate(riders):
        al.update({n_in + sum(r_in[:k]) + i: n_out + sum(r_out[:k]) + o for i, o in r.aliases.items()})

    def full_body(*refs):
        ins, rest = refs[:n_in + sum(r_in)], refs[n_in + sum(r_in):]
        outs, scr = rest[:n_out + sum(r_out)], rest[n_out + sum(r_out):]
        hooks = [r.make(ins[n_in + sum(r_in[:k]):n_in + sum(r_in[:k + 1])], outs[n_out + sum(r_out[:k]):n_out + sum(r_out[:k + 1])],
                        scr[n_scratch + sum(r_scr[:k]):n_scratch + sum(r_scr[:k + 1])]) for k, r in enumerate(riders)]

        def start():
            for h in hooks:
                h[0]()

        def finish():
            for h in hooks:
                h[1]()

        body(start, finish, *ins[:n_in], *outs[:n_out], *scr[:n_scratch])

    call = pl.pallas_call(
        full_body, in_specs=list(in_specs) + [_ANY] * sum(r_in), out_specs=list(out_specs) + [_ANY] * sum(r_out),
        out_shape=list(out_shape) + [s for r in riders for s in r.out_shapes],
        scratch_shapes=list(scratch_shapes) + [s for r in riders for s in r.scratch], input_output_aliases=al, **kw)

    def run(*args):
        res = call(*args, *[t for r in riders for t in r.inputs])
        return res[:n_out], [res[n_out + sum(r_out[:k]):n_out + sum(r_out[:k + 1])] for k in range(len(riders))]

    return run


def _matmul(a, b, dims, tn, out_dtype, name, rider=None, a_cols=None):
    a_shape = a.shape if a_cols is None else (a.shape[0], a_cols[0])
    a_index = 0 if a_cols is None else a_cols[1]
    m = a_shape[1] if dims == "tn" else a_shape[0]
    n = b.shape[0] if dims == "nt" else b.shape[1]
    kdim = b.shape[1] if dims == "nt" else b.shape[0]
    tn = _tile(n, tn)
    steps = n // tn
    b_spec = pl.BlockSpec((tn, kdim), lambda j: (j, 0)) if dims == "nt" else pl.BlockSpec((kdim, tn), lambda j: (0, j))

    def body(start, finish, a_ref, b_ref, o_ref):
        pl.when(pl.program_id(0) == 0)(start)
        o_ref[...] = _dot(a_ref[...], b_ref[...], dims).astype(out_dtype)
        pl.when(pl.program_id(0) == steps - 1)(finish)

    (out,), got = _hosted_call(
        body, [rider], 2, 1, 0, grid=(steps,), in_specs=[pl.BlockSpec(a_shape, lambda j: (0, a_index)), b_spec],
        out_specs=[pl.BlockSpec((m, tn), lambda j: (0, j))], out_shape=[jax.ShapeDtypeStruct((m, n), out_dtype)],
        name=name, compiler_params=_cp("parallel" if rider is None else "arbitrary"))(a, b)
    return out if rider is None else (out, got[0])


def _chip_rider(sums, rows_total=None, row0=0, into=None):
    _, r, c = sums.shape
    rows_total = rows_total or r

    def make(ins, outs, scratch):
        send_sems, recv_sems, local_sem = scratch
        x, y, core = lax.axis_index("x"), lax.axis_index("y"), lax.axis_index("c")
        mine = 2 * x + y
        land = lambda chip: outs[0].at[chip].at[pl.ds(row0, r)]
        local = pltpu.make_async_copy(ins[0].at[mine], land(mine), local_sem)
        sends, recvs = [], []
        for k in range(1, N_CHIP):
            px = 1 - x if k & 2 else x
            py = 1 - y if k & 1 else y
            kw = dict(send_sem=send_sems.at[k - 1], recv_sem=recv_sems.at[k - 1], device_id=(px, py, core),
                      device_id_type=_MESH)
            sends.append(pltpu.make_async_remote_copy(src_ref=ins[0].at[2 * px + py], dst_ref=land(mine), **kw))
            recvs.append(pltpu.make_async_remote_copy(src_ref=ins[0].at[mine], dst_ref=land(2 * px + py), **kw))

        def start():
            for cp in [local] + sends:
                cp.start()

        def finish():
            local.wait()
            for cp in sends:
                cp.wait_send()
            for cp in recvs:
                cp.wait_recv()

        return start, finish

    return _Rider([sums] + ([] if into is None else [into]), [jax.ShapeDtypeStruct((N_CHIP, rows_total, c), sums.dtype)],
                  [pltpu.SemaphoreType.DMA((N_CHIP - 1,)), pltpu.SemaphoreType.DMA((N_CHIP - 1,)), pltpu.SemaphoreType.DMA(())],
                  {} if into is None else {1: 0}, make)


def _dxn(dproj, wfull, riders, tk, name):
    m, k = dproj.shape
    n = wfull.shape[0]
    tk = _tile(k, tk)
    steps = k // tk

    def body(start, finish, a_ref, b_ref, o_ref):
        j = pl.program_id(0)

        @pl.when(j == 0)
        def _():
            start()
            o_ref[...] = jnp.zeros_like(o_ref)
        o_ref[...] += _dot(a_ref[...], b_ref[...], "nt")
        pl.when(j == steps - 1)(finish)

    (dxn,), got = _hosted_call(
        body, riders, 2, 1, 0, grid=(steps,),
        in_specs=[pl.BlockSpec((m, tk), lambda j: (0, j)), pl.BlockSpec((n, tk), lambda j: (0, j))],
        out_specs=[pl.BlockSpec((m, n), lambda j: (0, 0))], out_shape=[jax.ShapeDtypeStruct((m, n), F32)],
        name=name, compiler_params=_cp("arbitrary"))(dproj, wfull)
    return dxn, got


def _conv_taps(x, w):
    c = x * w[CONV_WIDTH - 1:CONV_WIDTH, :]
    for j in range(CONV_WIDTH - 1):
        c = c + pltpu.roll(x, CONV_WIDTH - 1 - j, 0) * w[j:j + 1, :]
    return c


def _gdn_prep(proj, conv_wt, nh):
    lp = proj.shape[0]
    scale = HEAD_DIM ** -0.5

    def body(x_ref, w_ref, o_ref):
        which = pl.program_id(0) // nh
        c = _conv_taps(x_ref[...], w_ref[...])
        s = c * _sigmoid(c)
        r = lax.rsqrt(jnp.sum(s * s, axis=-1, keepdims=True) + EPS)
        f = jnp.where(which == 0, r * scale, jnp.where(which == 1, r, 1.0))
        o_ref[...] = jnp.where(_iota(s.shape, 0) >= PAD_ROWS, s * f, 0.0)

    return pl.pallas_call(
        body, grid=(3 * nh,),
        in_specs=[pl.BlockSpec((lp, LANES), lambda s: (0, s)), pl.BlockSpec((CONV_WIDTH, LANES), lambda s: (0, s))],
        out_specs=pl.BlockSpec((lp, LANES), lambda s: (0, s)),
        out_shape=jax.ShapeDtypeStruct((lp, 3 * nh * HEAD_DIM), F32), name="gdn_prep",
        compiler_params=_cp("parallel"))(proj, conv_wt)


def _gdn_prep_bwd(proj, conv_wt, dq, dk, dv, dproj, nh):
    lp = proj.shape[0]
    scale = HEAD_DIM ** -0.5
    part = lambda p: pl.BlockSpec((lp, LANES), lambda s: (0, jnp.clip(s - p * nh, 0, nh - 1)))

    def body(x_ref, w_ref, dq_ref, dk_ref, dv_ref, _, dx_ref, dw_ref):
        which = pl.program_id(0) // nh
        x = x_ref[...]
        w = w_ref[...]
        c = _conv_taps(x, w)
        sg = _sigmoid(c)
        s = c * sg
        r = lax.rsqrt(jnp.sum(s * s, axis=-1, keepdims=True) + EPS)
        dy = jnp.where(which == 0, dq_ref[...], jnp.where(which == 1, dk_ref[...], dv_ref[...]))
        dy = jnp.where(_iota(s.shape, 0) >= PAD_ROWS, dy, 0.0)
        y0 = s * r
        dy0 = dy * jnp.where(which == 0, scale, 1.0)
        ds_n = r * (dy0 - y0 * jnp.sum(dy0 * y0, axis=-1, keepdims=True))
        ds = jnp.where(which == 2, dy, ds_n)
        dc = ds * (sg * (1.0 + c * (1.0 - sg)))
        dx = dc * w[CONV_WIDTH - 1:CONV_WIDTH, :]
        rows = [jnp.sum(dc * x, axis=0, keepdims=True)]
        for j in range(CONV_WIDTH - 2, -1, -1):
            sh = CONV_WIDTH - 1 - j
            dx = dx + pltpu.roll(dc, lp - sh, 0) * w[j:j + 1, :]
            rows.insert(0, jnp.sum(dc * pltpu.roll(x, sh, 0), axis=0, keepdims=True))
        dx_ref[...] = dx.astype(BF16)
        dw_ref[...] = jnp.concatenate(rows, axis=0)

    strip = pl.BlockSpec((lp, LANES), lambda s: (0, s))
    taps = pl.BlockSpec((CONV_WIDTH, LANES), lambda s: (0, s))
    return pl.pallas_call(
        body, grid=(3 * nh,), in_specs=[strip, taps, part(0), part(1), part(2), _ANY], out_specs=[strip, taps],
        out_shape=[jax.ShapeDtypeStruct(dproj.shape, BF16), jax.ShapeDtypeStruct((CONV_WIDTH, 3 * nh * HEAD_DIM), F32)],
        input_output_aliases={5: 0}, name="gdn_prep_bwd", compiler_params=_cp("parallel"))(proj, conv_wt, dq, dk, dv, dproj)


def _gates(proj, bias_row, nega_row, nh):
    lp = proj.shape[0]
    nc = lp // CHUNK

    def body(p_ref, b_ref, a_ref, g_ref, gt3_ref, gtf_ref):
        lane = _iota((CHUNK, LANES), 1)
        tri = (_iota((CHUNK, CHUNK), 0) >= _iota((CHUNK, CHUNK), 1)).astype(F32)

        def step(n, carry):
            r0 = pl.multiple_of(n * CHUNK, CHUNK)
            z = p_ref[pl.ds(r0, CHUNK), :] + b_ref[...]
            base = jnp.where(lane < nh, _sigmoid(z),
                             jnp.where(lane < 2 * nh, a_ref[...] * _softplus(z),
                                       jnp.where(lane < 3 * nh, -_softplus(-z), 0.0)))
            base = jnp.where(r0 + _iota((CHUNK, LANES), 0) >= PAD_ROWS, base, 0.0)
            cs = _hdot(tri, base)
            run = jnp.where((lane >= 2 * nh) & (lane < 3 * nh), cs + carry, cs)
            sh = pltpu.roll(run, 2 * nh, 1)
            out = base + jnp.where((lane >= 3 * nh) & (lane < 5 * nh), sh, 0.0)
            g_ref[pl.ds(r0, CHUNK), :] = out
            gt3_ref[n] = out.T
            return carry + cs[CHUNK - 1:CHUNK, :]

        lax.fori_loop(0, nc, step, jnp.zeros((1, LANES), F32))
        gtf_ref[...] = g_ref[...].T

    vec = pl.BlockSpec((1, LANES), lambda i: (0, 0))
    return pl.pallas_call(
        body, grid=(1,), in_specs=[pl.BlockSpec((lp, LANES), lambda i: (0, 8 * nh)), vec, vec],
        out_specs=[pl.BlockSpec((lp, LANES), lambda i: (0, 0)), pl.BlockSpec((nc, LANES, CHUNK), lambda i: (0, 0, 0)),
                   pl.BlockSpec((LANES, lp), lambda i: (0, 0))],
        out_shape=[jax.ShapeDtypeStruct((lp, LANES), F32), jax.ShapeDtypeStruct((nc, LANES, CHUNK), F32),
                   jax.ShapeDtypeStruct((LANES, lp), F32)],
        name="gates", compiler_params=_cp("arbitrary"))(proj, bias_row, nega_row)


def _gates_bwd(proj, bias_row, nega_row, gates, dgate_gdn, dc_t, dproj, nh):
    lp = proj.shape[0]
    nc = lp // CHUNK

    def body(p_ref, b_ref, a_ref, g_ref, dg_ref, dc_ref, _, dz_ref, sm_ref, dct_scr):
        lane = _iota((CHUNK, LANES), 1)
        triu = (_iota((CHUNK, CHUNK), 0) <= _iota((CHUNK, CHUNK), 1)).astype(F32)
        dct_scr[...] = dc_ref[...].T
        sm_ref[...] = jnp.zeros_like(sm_ref)
        dz_ref[:, LANES:] = jnp.zeros((lp, NARROW - LANES), BF16)

        def step(i, carry):
            n = nc - 1 - i
            r0 = pl.multiple_of(n * CHUNK, CHUNK)
            z = p_ref[pl.ds(r0, CHUNK), :] + b_ref[...]
            gt = g_ref[pl.ds(r0, CHUNK), :]
            dgd = dg_ref[pl.ds(r0, CHUNK), :]
            dch = dct_scr[pl.ds(r0, CHUNK), :]
            rc = _hdot(triu, dch) + carry
            sg = _sigmoid(z)
            dz = jnp.where(lane < nh, dgd * sg * (1.0 - sg),
                           jnp.where(lane < 2 * nh, dgd * a_ref[...] * sg,
                                     jnp.where(lane < 3 * nh, rc * (1.0 - sg), 0.0)))
            dz = jnp.where(r0 + _iota((CHUNK, LANES), 0) >= PAD_ROWS, dz, 0.0)
            dz_ref[pl.ds(r0, CHUNK), 0:LANES] = dz.astype(BF16)
            sm_ref[0:1, :] += jnp.sum(dz, axis=0, keepdims=True)
            sm_ref[1:2, :] += jnp.sum(jnp.where((lane >= nh) & (lane < 2 * nh), dgd * gt, 0.0), axis=0, keepdims=True)
            return carry + jnp.sum(dch, axis=0, keepdims=True)

        lax.fori_loop(0, nc, step, jnp.zeros((1, LANES), F32))

    vec = pl.BlockSpec((1, LANES), lambda i: (0, 0))
    full = pl.BlockSpec((lp, LANES), lambda i: (0, 0))
    last = pl.BlockSpec((lp, LANES), lambda i: (0, 8 * nh))
    tail = pl.BlockSpec((lp, NARROW), lambda i: (0, 8 * nh * LANES // NARROW))
    return pl.pallas_call(
        body, grid=(1,), in_specs=[last, vec, vec, full, full, pl.BlockSpec((LANES, lp), lambda i: (0, 0)), _ANY],
        out_specs=[tail, pl.BlockSpec((8, LANES), lambda i: (0, 0))],
        out_shape=[jax.ShapeDtypeStruct(dproj.shape, BF16), jax.ShapeDtypeStruct((8, LANES), F32)],
        scratch_shapes=[pltpu.VMEM((lp, LANES), F32)], input_output_aliases={6: 0},
        name="gates_bwd", compiler_params=_cp("arbitrary"))(proj, bias_row, nega_row, gates, dgate_gdn, dc_t, dproj)


def _tri_inv(a):
    t = jnp.where(_iota(a.shape, 1) == _iota(a.shape, 2), 1.0, 0.0) - a
    p = a
    for _ in range(CHUNK.bit_length() - 2):
        ph, pw = _split(p)
        p = _dot3(ph, ph) + (_dot3(ph, pw) + _dot3(pw, ph))
        ph, pw = _split(p)
        th, tw = _split(t)
        t = t + (_dot3(th, ph) + (_dot3(th, pw) + _dot3(tw, ph)))
    return t


def _gdn_chunk(q, k, v, beta, gc, gr, t=None):
    ii, jj = _iota((1, CHUNK, CHUNK), 1), _iota((1, CHUNK, CHUNK), 2)
    causal, strict = ii >= jj, ii > jj
    dm = jnp.where(causal, jnp.exp(jnp.where(causal, gc - gr, 0.0)), 0.0)
    kk = _bdot3(k, k, "nt")
    a = jnp.where(strict, beta * kk * dm, 0.0)
    if t is None:
        t = _tri_inv(a)
    eg = jnp.exp(gc)
    glast = gc[:, CHUNK - 1:CHUNK, :]
    ekd = jnp.exp(glast - gc)
    bv = beta * v
    bk = (beta * eg) * k
    ub = _bdot3(t, jnp.concatenate([bv, bk], axis=2))
    qk = _bdot3(q, k, "nt")
    return dict(causal=causal, strict=strict, dm=dm, kk=kk, a=a, t=t, eg=eg, ekd=ekd, bv=bv, bk=bk,
                u=ub[:, :, :HEAD_DIM], w=ub[:, :, HEAD_DIM:], qk=qk, aqk=jnp.where(causal, qk * dm, 0.0),
                q_dec=q * eg, k_dec=k * ekd, decay=jnp.exp(glast))


def _heads(ref, nh):
    return jnp.stack([ref[:, h * HEAD_DIM:(h + 1) * HEAD_DIM] for h in range(nh)], axis=0)


def _gdn_chunk_inputs(q_ref, k_ref, v_ref, g, gt, nh):
    col = lambda o: jnp.stack([g[:, o + h:o + h + 1] for h in range(nh)], axis=0)
    gr = jnp.stack([gt[3 * nh + h:3 * nh + h + 1, :] for h in range(nh)], axis=0)
    return _heads(q_ref, nh), _heads(k_ref, nh), _heads(v_ref, nh), col(0), col(3 * nh), gr


def _gdn_fwd(qkv, gates, gt3, nh, rider=None):
    lp = qkv.shape[0]
    nc = lp // CHUNK
    w = nh * HEAD_DIM

    def body(start, finish, q_ref, k_ref, v_ref, g_ref, gt_ref, o_ref, sall_ref, tall_ref, s_scr):
        @pl.when(pl.program_id(0) == 0)
        def _():
            start()
            s_scr[...] = jnp.zeros_like(s_scr)
        c = _gdn_chunk(*_gdn_chunk_inputs(q_ref, k_ref, v_ref, g_ref[...], gt_ref[0], nh))
        s = s_scr[...]
        sall_ref[0] = s
        tall_ref[0] = c["t"]
        v_new = c["u"] - _bdot3(c["w"], s)
        o = _bdot3(c["q_dec"], s) + _bdot3(c["aqk"], v_new)
        s_scr[...] = s * c["decay"] + _bdot3(c["k_dec"], v_new, "tn")
        for h in range(nh):
            o_ref[:, h * HEAD_DIM:(h + 1) * HEAD_DIM] = o[h]
        pl.when(pl.program_id(0) == nc - 1)(finish)

    outs, got = _hosted_call(
        body, [rider], 5, 3, 1, grid=(nc,),
        in_specs=[pl.BlockSpec((CHUNK, w), lambda n: (n, 0)), pl.BlockSpec((CHUNK, w), lambda n: (n, 1)),
                  pl.BlockSpec((CHUNK, w), lambda n: (n, 2)), pl.BlockSpec((CHUNK, LANES), lambda n: (n, 0)),
                  pl.BlockSpec((1, LANES, CHUNK), lambda n: (n, 0, 0))],
        out_specs=[pl.BlockSpec((CHUNK, w), lambda n: (n, 0)),
                   pl.BlockSpec((1, nh, HEAD_DIM, HEAD_DIM), lambda n: (n, 0, 0, 0)),
                   pl.BlockSpec((1, nh, CHUNK, CHUNK), lambda n: (n, 0, 0, 0))],
        out_shape=[jax.ShapeDtypeStruct((lp, w), F32), jax.ShapeDtypeStruct((nc, nh, HEAD_DIM, HEAD_DIM), F32),
                   jax.ShapeDtypeStruct((nc, nh, CHUNK, CHUNK), F32)],
        scratch_shapes=[pltpu.VMEM((nh, HEAD_DIM, HEAD_DIM), F32)],
        name="gdn_fwd", compiler_params=_cp("arbitrary"))(qkv, qkv, qkv, gates, gt3)
    return outs, (got[0] if got else None)


def _gdn_bwd(qkv, gates, gt3, s_all, t_all, do, nh, rider=None):
    lp = qkv.shape[0]
    nc = lp // CHUNK
    w = nh * HEAD_DIM
    rev = lambda n: nc - 1 - n

    def body(start, finish, q_ref, k_ref, v_ref, g_ref, gt_ref, s_ref, t_ref, do_ref, dq_ref, dk_ref, dv_ref, dg_ref, ds_scr):
        @pl.when(pl.program_id(0) == 0)
        def _():
            start()
            ds_scr[...] = jnp.zeros_like(ds_scr)
        q, k, v, beta, gc, gr = _gdn_chunk_inputs(q_ref, k_ref, v_ref, g_ref[...], gt_ref[0], nh)
        c = _gdn_chunk(q, k, v, beta, gc, gr, t_ref[0])
        s = s_ref[0]
        dsn = ds_scr[...]
        dout = _heads(do_ref, nh)
        v_new = c["u"] - _bdot3(c["w"], s)
        dq_dec = _bdot3(dout, s, "nt")
        daqk = jnp.where(c["causal"], _bdot3(dout, v_new, "nt"), 0.0)
        dv_new = _bdot3(c["aqk"], dout, "tn") + _bdot3(c["k_dec"], dsn)
        dk_dec = _bdot3(v_new, dsn, "nt")
        ddecay = jnp.sum(jnp.sum(dsn * s, axis=2, keepdims=True), axis=1, keepdims=True)
        dw = -_bdot3(dv_new, s, "nt")
        ds_scr[...] = _bdot3(c["q_dec"], dout, "tn") + c["decay"] * dsn - _bdot3(c["w"], dv_new, "tn")
        duw = jnp.concatenate([dv_new, dw], axis=2)
        dt = _bdot3(duw, jnp.concatenate([c["bv"], c["bk"]], axis=2), "nt")
        dbvk = _bdot3(c["t"], duw, "tn")
        dbv, dbk = dbvk[:, :, :HEAD_DIM], dbvk[:, :, HEAD_DIM:]
        da = jnp.where(c["strict"], -_bdot3(_bdot3(c["t"], dt, "tn"), c["t"], "nt"), 0.0)
        dkk = da * beta * c["dm"]
        dqk = daqk * c["dm"]
        e = da * c["a"] + daqk * c["aqk"]
        dq = dq_dec * c["eg"] + _bdot3(dqk, k)
        dk = (dk_dec * c["ekd"] + _bdot3(dkk, k) + _bdot3(dkk, k, "tn") + _bdot3(dqk, q, "tn")
              + (beta * c["eg"]) * dbk)
        dv = beta * dbv
        rs = lambda x: jnp.sum(x, axis=2, keepdims=True)
        dbeta = rs(dbv * v) + c["eg"] * rs(dbk * k) + rs(da * c["kk"] * c["dm"])
        kd_term = rs(dk_dec * c["k_dec"])
        eh, ew = _split(e)
        ones = jnp.ones((nh, CHUNK, LANES), BF16)
        col_sums = (_dot3(eh, ones, "tn") + _dot3(ew, ones, "tn"))[:, :, 0:1]
        dg_cum = rs(dq_dec * c["q_dec"]) - kd_term + rs(dbk * c["bk"]) + rs(e) - col_sums
        last = jnp.sum(kd_term, axis=1, keepdims=True) + ddecay * c["decay"]
        dg_cum = dg_cum + jnp.where(_iota((1, CHUNK, 1), 1) == CHUNK - 1, last, 0.0)
        lane = _iota((CHUNK, LANES), 1)
        acc = jnp.zeros((CHUNK, LANES), F32)
        for h in range(nh):
            sl = slice(h * HEAD_DIM, (h + 1) * HEAD_DIM)
            dq_ref[:, sl] = dq[h]
            dk_ref[:, sl] = dk[h]
            dv_ref[:, sl] = dv[h]
            acc = acc + jnp.where(lane == h, dbeta[h], 0.0) + jnp.where(lane == nh + h, dg_cum[h], 0.0)
        triu = (_iota((CHUNK, CHUNK), 0) <= _iota((CHUNK, CHUNK), 1)).astype(F32)
        dg_ref[...] = jnp.where(lane < nh, acc, _hdot(triu, acc))
        pl.when(pl.program_id(0) == nc - 1)(finish)

    outs, got = _hosted_call(
        body, [rider], 8, 4, 1, grid=(nc,),
        in_specs=[pl.BlockSpec((CHUNK, w), lambda n: (rev(n), 0)), pl.BlockSpec((CHUNK, w), lambda n: (rev(n), 1)),
                  pl.BlockSpec((CHUNK, w), lambda n: (rev(n), 2)), pl.BlockSpec((CHUNK, LANES), lambda n: (rev(n), 0)),
                  pl.BlockSpec((1, LANES, CHUNK), lambda n: (rev(n), 0, 0)),
                  pl.BlockSpec((1, nh, HEAD_DIM, HEAD_DIM), lambda n: (rev(n), 0, 0, 0)),
                  pl.BlockSpec((1, nh, CHUNK, CHUNK), lambda n: (rev(n), 0, 0, 0)),
                  pl.BlockSpec((CHUNK, w), lambda n: (rev(n), 0))],
        out_specs=[pl.BlockSpec((CHUNK, w), lambda n: (rev(n), 0))] * 3 + [pl.BlockSpec((CHUNK, LANES), lambda n: (rev(n), 0))],
        out_shape=[jax.ShapeDtypeStruct((lp, w), F32)] * 3 + [jax.ShapeDtypeStruct((lp, LANES), F32)],
        scratch_shapes=[pltpu.VMEM((nh, HEAD_DIM, HEAD_DIM), F32)],
        name="gdn_bwd", compiler_params=_cp("arbitrary"))(qkv, qkv, qkv, gates, gt3, s_all, t_all, do)
    return outs, (got[0] if got else None)


def _merge_gdn(o_gdn, proj, norm_w, nh):
    lp = o_gdn.shape[0]

    def body(o_ref, z_ref, w_ref, m_ref):
        o = o_ref[...]
        z = z_ref[...]
        m_ref[...] = (o * _rms(o) * w_ref[...] * (z * _sigmoid(z))).astype(BF16)

    return pl.pallas_call(
        body, grid=(nh,),
        in_specs=[pl.BlockSpec((lp, LANES), lambda s: (0, s)), pl.BlockSpec((lp, LANES), lambda s: (0, 3 * nh + s)),
                  pl.BlockSpec((1, LANES), lambda s: (0, 0))],
        out_specs=pl.BlockSpec((lp, LANES), lambda s: (0, s)),
        out_shape=jax.ShapeDtypeStruct((lp, 2 * nh * HEAD_DIM), BF16), name="merge_gdn",
        compiler_params=_cp("parallel"))(o_gdn, proj, norm_w)


def _merge_gdn_bwd(o_gdn, proj, norm_w, dmerged, nh):
    lp = o_gdn.shape[0]

    def body(o_ref, z_ref, w_ref, dm_ref, do_ref, dz_ref, dw_ref):
        o = o_ref[...]
        r = _rms(o)
        xh = o * r
        silu, dsilu = _silu_and_grad(z_ref[...])
        dm = dm_ref[...]
        dn = dm * silu
        dz_ref[...] = (dm * (xh * w_ref[...]) * dsilu).astype(BF16)
        dnw = dn * w_ref[...]
        do_ref[...] = r * (dnw - xh * jnp.mean(dnw * xh, axis=-1, keepdims=True))

        @pl.when(pl.program_id(0) == 0)
        def _():
            dw_ref[...] = jnp.zeros_like(dw_ref)
        dw_ref[...] += jnp.sum(dn * xh, axis=0, keepdims=True)

    w = nh * HEAD_DIM
    return pl.pallas_call(
        body, grid=(nh,),
        in_specs=[pl.BlockSpec((lp, LANES), lambda s: (0, s)), pl.BlockSpec((lp, LANES), lambda s: (0, 3 * nh + s)),
                  pl.BlockSpec((1, LANES), lambda s: (0, 0)), pl.BlockSpec((lp, LANES), lambda s: (0, s))],
        out_specs=[pl.BlockSpec((lp, LANES), lambda s: (0, s)), pl.BlockSpec((lp, LANES), lambda s: (0, 3 * nh + s)),
                   pl.BlockSpec((1, LANES), lambda s: (0, 0))],
        out_shape=[jax.ShapeDtypeStruct((lp, w), F32), jax.ShapeDtypeStruct((lp, 8 * w + NARROW), BF16),
                   jax.ShapeDtypeStruct((1, LANES), F32)],
        name="merge_gdn_bwd", compiler_params=_cp("arbitrary"))(o_gdn, proj, norm_w, dmerged)


def _fox_prep(proj, qk_w, nh):
    lp = proj.shape[0]

    def body(x_ref, w_ref, o_ref):
        x = x_ref[...]
        o_ref[...] = (x * _rms(x) * w_ref[0]).astype(BF16)

    return pl.pallas_call(
        body, grid=(2 * nh,),
        in_specs=[pl.BlockSpec((lp, LANES), lambda s: (0, 4 * nh + s)), pl.BlockSpec((1, 1, LANES), lambda s: (s // nh, 0, 0))],
        out_specs=pl.BlockSpec((lp, LANES), lambda s: (0, s)),
        out_shape=jax.ShapeDtypeStruct((lp, 2 * nh * HEAD_DIM), BF16), name="fox_prep",
        compiler_params=_cp("parallel"))(proj, qk_w)


def _fox_prep_bwd(proj, qk_w, dq, dk, dproj, nh):
    lp = proj.shape[0]
    part = lambda p: pl.BlockSpec((lp, LANES), lambda s: (0, jnp.clip(s - p * nh, 0, nh - 1)))

    def body(x_ref, w_ref, dq_ref, dk_ref, _, dx_ref, dw_ref):
        x = x_ref[...]
        r = _rms(x)
        xh = x * r
        dy = jnp.where(pl.program_id(0) < nh, dq_ref[...], dk_ref[...])
        dyw = dy * w_ref[0]
        dx_ref[...] = (r * (dyw - xh * jnp.mean(dyw * xh, axis=-1, keepdims=True))).astype(BF16)

        @pl.when(pl.program_id(0) % nh == 0)
        def _():
            dw_ref[...] = jnp.zeros_like(dw_ref)
        dw_ref[0] += jnp.sum(dy * xh, axis=0, keepdims=True)

    strip = pl.BlockSpec((lp, LANES), lambda s: (0, 4 * nh + s))
    wsp = pl.BlockSpec((1, 1, LANES), lambda s: (s // nh, 0, 0))
    return pl.pallas_call(
        body, grid=(2 * nh,), in_specs=[strip, wsp, part(0), part(1), _ANY], out_specs=[strip, wsp],
        out_shape=[jax.ShapeDtypeStruct(dproj.shape, BF16), jax.ShapeDtypeStruct((2, 1, LANES), F32)],
        input_output_aliases={4: 0}, name="fox_prep_bwd", compiler_params=_cp("arbitrary"))(proj, qk_w, dq, dk, dproj)


def _fox_probs(q, k, gates, crow, h, i, nh, lse=None):
    kl = k.shape[0]
    lane = _iota((Q_BLOCK, LANES), 1)
    ct = jnp.sum(jnp.where(lane == 4 * nh + h, gates, 0.0), axis=1, keepdims=True)
    tq, kq = _iota((Q_BLOCK, Q_BLOCK), 0), _iota((Q_BLOCK, Q_BLOCK), 1)
    qs = q * (HEAD_DIM ** -0.5)
    if i == 0:
        s = _bdot(qs, k, "nt") + (ct - crow)
        s = jnp.where((kq <= tq) & ((kq >= PAD_ROWS) | (tq < PAD_ROWS)), s, NEG)
    else:
        crow = jnp.where(_iota((1, kl), 1) < PAD_ROWS, -NEG, crow)
        s = _bdot(qs, k, "nt") + (ct - crow)
        s = jnp.concatenate([s[:, :kl - Q_BLOCK], jnp.where(kq <= tq, s[:, kl - Q_BLOCK:], NEG)], axis=1)
    if lse is not None:
        return jnp.exp(s - lse)
    m = jnp.max(s, axis=1, keepdims=True)
    p = jnp.exp(s - m)
    tot = jnp.sum(p, axis=1, keepdims=True)
    return p / tot, m + jnp.log(tot)


FOX_HEADS_PER_STEP = 2


def _fox_specs(lp, nh):
    hw = FOX_HEADS_PER_STEP * LANES
    return [pl.BlockSpec((Q_BLOCK, hw), lambda g, i: (i, g)),
            pl.BlockSpec((lp, hw), lambda g, i: (0, nh // FOX_HEADS_PER_STEP + g)),
            pl.BlockSpec((lp, hw), lambda g, i: (0, 6 * nh // FOX_HEADS_PER_STEP + g)),
            pl.BlockSpec((Q_BLOCK, LANES), lambda g, i: (i, 0)),
            pl.BlockSpec((LANES, lp), lambda g, i: (0, 0))]


def _fox_fwd(qkn, proj, gates, gtf, nh):
    lp = qkn.shape[0]

    def body(q_ref, k_ref, v_ref, g_ref, gt_ref, o_ref, lse_ref):
        g, i = pl.program_id(0), pl.program_id(1)
        for j in range(lp // Q_BLOCK):
            @pl.when(i == j)
            def _(j=j):
                kl = (j + 1) * Q_BLOCK
                for hh in range(FOX_HEADS_PER_STEP):
                    h = FOX_HEADS_PER_STEP * g + hh
                    sl = slice(hh * LANES, (hh + 1) * LANES)
                    p, lse = _fox_probs(q_ref[:, sl], k_ref[0:kl, sl], g_ref[...], gt_ref[pl.ds(4 * nh + h, 1), :][:, 0:kl],
                                        h, j, nh)
                    o_ref[:, sl] = _bdot(p, v_ref[0:kl, sl])
                    lse_ref[:, sl] = jnp.broadcast_to(lse, (Q_BLOCK, LANES))

    blk = pl.BlockSpec((Q_BLOCK, FOX_HEADS_PER_STEP * LANES), lambda g, i: (i, g))
    return pl.pallas_call(
        body, grid=(nh // FOX_HEADS_PER_STEP, lp // Q_BLOCK), in_specs=_fox_specs(lp, nh), out_specs=[blk, blk],
        out_shape=[jax.ShapeDtypeStruct((lp, nh * HEAD_DIM), F32)] * 2, name="fox_fwd",
        compiler_params=_cp("parallel", "parallel"))(qkn, qkn, proj, gates, gtf)


def _fox_bwd(qkn, proj, gates, gtf, lse, do, dproj, nh):
    lp = qkn.shape[0]
    nq = lp // Q_BLOCK
    w = nh * HEAD_DIM
    scale = HEAD_DIM ** -0.5

    def body(q_ref, k_ref, v_ref, g_ref, gt_ref, lse_ref, do_ref, _, dq_ref, dk_ref, dc_ref, dv_ref, dv_scr):
        g, i = pl.program_id(0), pl.program_id(1)

        @pl.when(i == 0)
        def _():
            dk_ref[...] = jnp.zeros_like(dk_ref)
            dv_scr[...] = jnp.zeros_like(dv_scr)
            dc_ref[...] = jnp.zeros_like(dc_ref)
        for j in range(nq):
            @pl.when(i == j)
            def _(j=j):
                kl = (j + 1) * Q_BLOCK
                for hh in range(FOX_HEADS_PER_STEP):
                    h = FOX_HEADS_PER_STEP * g + hh
                    sl = slice(hh * LANES, (hh + 1) * LANES)
                    q, k = q_ref[:, sl], k_ref[0:kl, sl]
                    p = _fox_probs(q, k, g_ref[...], gt_ref[pl.ds(4 * nh + h, 1), :][:, 0:kl], h, j, nh,
                                   lse_ref[:, sl][:, 0:1])
                    dout = do_ref[:, sl]
                    dp = _bdot(dout, v_ref[0:kl, sl], "nt")
                    ds = p * (dp - jnp.sum(p * dp, axis=1, keepdims=True))
                    dq_ref[:, sl] = _bdot(ds, k) * scale
                    dk_ref[0:kl, sl] += _bdot(ds, q * scale, "tn")
                    dv_scr[0:kl, sl] += _bdot(p, dout, "tn")
                    dc_ref[hh, :, 0:kl] -= jnp.sum(ds, axis=0, keepdims=True)

        @pl.when(i == nq - 1)
        def _():
            dv_ref[...] = dv_scr[...].astype(BF16)

    hw = FOX_HEADS_PER_STEP * LANES
    blk = pl.BlockSpec((Q_BLOCK, hw), lambda g, i: (i, g))
    col = pl.BlockSpec((lp, hw), lambda g, i: (0, g))
    return pl.pallas_call(
        body, grid=(nh // FOX_HEADS_PER_STEP, nq), in_specs=_fox_specs(lp, nh) + [blk, blk, _ANY],
        out_specs=[blk, col, pl.BlockSpec((FOX_HEADS_PER_STEP, 1, lp), lambda g, i: (g, 0, 0)),
                   pl.BlockSpec((lp, hw), lambda g, i: (0, 6 * nh // FOX_HEADS_PER_STEP + g))],
        out_shape=[jax.ShapeDtypeStruct((lp, w), F32)] * 2 + [jax.ShapeDtypeStruct((nh, 1, lp), F32),
                                                             jax.ShapeDtypeStruct(dproj.shape, BF16)],
        scratch_shapes=[pltpu.VMEM((lp, hw), F32)], input_output_aliases={7: 3},
        name="fox_bwd", compiler_params=_cp("parallel", "arbitrary"))(qkn, qkn, proj, gates, gtf, lse, do, dproj)


def _merge_fox(o_fox, proj, merged, nh):
    lp = o_fox.shape[0]

    def body(o_ref, z_ref, _, m_ref):
        z = z_ref[...]
        m_ref[...] = (o_ref[...] * (z * _sigmoid(z))).astype(BF16)

    return pl.pallas_call(
        body, grid=(nh,),
        in_specs=[pl.BlockSpec((lp, LANES), lambda s: (0, s)), pl.BlockSpec((lp, LANES), lambda s: (0, 7 * nh + s)), _ANY],
        out_specs=pl.BlockSpec((lp, LANES), lambda s: (0, nh + s)),
        out_shape=jax.ShapeDtypeStruct(merged.shape, BF16), input_output_aliases={2: 0}, name="merge_fox",
        compiler_params=_cp("parallel"))(o_fox, proj, merged)


def _merge_fox_bwd(o_fox, proj, dmerged, dproj, nh):
    lp = o_fox.shape[0]

    def body(o_ref, z_ref, dm_ref, _, do_ref, dz_ref):
        silu, dsilu = _silu_and_grad(z_ref[...])
        dm = dm_ref[...]
        do_ref[...] = dm * silu
        dz_ref[...] = (dm * o_ref[...] * dsilu).astype(BF16)

    w = nh * HEAD_DIM
    return pl.pallas_call(
        body, grid=(nh,),
        in_specs=[pl.BlockSpec((lp, LANES), lambda s: (0, s)), pl.BlockSpec((lp, LANES), lambda s: (0, 7 * nh + s)),
                  pl.BlockSpec((lp, LANES), lambda s: (0, nh + s)), _ANY],
        out_specs=[pl.BlockSpec((lp, LANES), lambda s: (0, s)), pl.BlockSpec((lp, LANES), lambda s: (0, 7 * nh + s))],
        out_shape=[jax.ShapeDtypeStruct((lp, w), F32), jax.ShapeDtypeStruct(dproj.shape, BF16)],
        input_output_aliases={3: 1}, name="merge_fox_bwd", compiler_params=_cp("parallel"))(o_fox, proj, dmerged, dproj)


def _post(out, x, target, post_w):
    lp, d = out.shape

    def body(o_ref, x_ref, t_ref, w_ref, dy_ref, do_ref, loss_ref, dw_ref):
        i = pl.program_id(0)

        @pl.when(i == 0)
        def _():
            loss_ref[...] = jnp.zeros_like(loss_ref)
            dw_ref[...] = jnp.zeros_like(dw_ref)
        o = o_ref[...]
        r = _rms(o)
        nrm = o * r
        err = jnp.where(i > 0, x_ref[...] + nrm * w_ref[...] - t_ref[...], 0.0)
        loss_ref[0:1, :] += 0.5 * jnp.sum(jnp.sum(err * err, axis=1, keepdims=True), axis=0, keepdims=True) / d
        dy = err / d
        dy_ref[...] = dy
        dw_ref[...] += jnp.sum(dy * nrm, axis=0, keepdims=True)
        dyw = dy * w_ref[...]
        do_ref[...] = (r * (dyw - nrm * jnp.mean(dyw * nrm, axis=-1, keepdims=True))).astype(BF16)

    row = pl.BlockSpec((Q_BLOCK, d), lambda i: (i, 0))
    vec = pl.BlockSpec((1, d), lambda i: (0, 0))
    return pl.pallas_call(
        body, grid=(lp // Q_BLOCK,), in_specs=[row, _x_rows(d), _x_rows(d), vec],
        out_specs=[_x_rows(d), row, pl.BlockSpec((8, LANES), lambda i: (0, 0)), vec],
        out_shape=[jax.ShapeDtypeStruct(x.shape, F32), jax.ShapeDtypeStruct((lp, d), BF16),
                   jax.ShapeDtypeStruct((8, LANES), F32), jax.ShapeDtypeStruct((1, d), F32)],
        name="post", compiler_params=_cp("arbitrary"))(out, x, target, post_w)


def _prenorm_bwd(dxn, x, meta, w, dy, rider=None):
    seq, d = x.shape
    lp = seq + Q_BLOCK

    def body(start, finish, dx_ref, x_ref, m_ref, w_ref, dy_ref, gx_ref, gm_ref, dw_ref):
        i = pl.program_id(0)
        pl.when(i == 0)(start)
        h = _h_tile(i, x_ref, m_ref)
        r = _rms(h)
        xh = h * r
        dxn_ = dx_ref[...]
        dxw = dxn_ * w_ref[...]
        dh = jnp.where(i > 0, dy_ref[...], 0.0) + r * (dxw - xh * jnp.mean(dxw * xh, axis=-1, keepdims=True))
        gx_ref[...] = dh

        @pl.when(i == 0)
        def _():
            dw_ref[...] = jnp.zeros_like(dw_ref)
            gm_ref[...] = dh[PAD_ROWS:, :]
        dw_ref[...] += jnp.sum(dxn_ * xh, axis=0, keepdims=True)
        pl.when(i == lp // Q_BLOCK - 1)(finish)

    vec = pl.BlockSpec((1, d), lambda i: (0, 0))
    met = pl.BlockSpec((N_META, d), lambda i: (0, 0))
    outs, got = _hosted_call(
        body, [rider], 5, 3, 0, grid=(lp // Q_BLOCK,),
        in_specs=[pl.BlockSpec((Q_BLOCK, d), lambda i: (i, 0)), _x_rows(d), met, vec, _x_rows(d)],
        out_specs=[_x_rows(d), met, vec],
        out_shape=[jax.ShapeDtypeStruct((seq, d), F32), jax.ShapeDtypeStruct((N_META, d), F32),
                   jax.ShapeDtypeStruct((1, d), F32)],
        name="prenorm_bwd", compiler_params=_cp("arbitrary"))(dxn, x, meta, w, dy)
    return outs, (got[0] if got else None)


def _layer_grads(x, target, meta, pre_w, wfull, conv_wt, a_log, dt_bias, gdn_norm_w, fq_w, fk_w, f_bias, w_out, post_w,
                 late_weights=None, w_out_grads=None):
    nh = a_log.shape[1]
    zpad = jnp.zeros((1, LANES - 3 * nh), F32)
    bias_row = jnp.concatenate([jnp.zeros((1, nh), F32), dt_bias, f_bias, zpad], axis=1)
    nega_row = jnp.concatenate([jnp.zeros((1, nh), F32), -jnp.exp(a_log), jnp.zeros((1, nh), F32), zpad], axis=1)
    qk_w = jnp.stack([fq_w, fk_w])

    xn = _prenorm(x, meta, pre_w)
    proj = _matmul(xn, wfull, "nn", MM_TILE, F32, "proj")
    qkv = _gdn_prep(proj, conv_wt, nh)
    gates, gt3, gtf = _gates(proj, bias_row, nega_row, nh)
    (o_gdn, s_all, t_all), got = _gdn_fwd(qkv, gates, gt3, nh, None if late_weights is None else late_weights[0])
    if late_weights is not None:
        w_out = late_weights[1](got)
    qkn = _fox_prep(proj, qk_w, nh)
    o_fox, fox_lse = _fox_fwd(qkn, proj, gates, gtf, nh)
    merged = _merge_fox(o_fox, proj, _merge_gdn(o_gdn, proj, gdn_norm_w, nh), nh)
    out = _matmul(merged, w_out, "nn", 4 * LANES, F32, "out_proj")
    dy, dout, loss_blk, dpost_w = _post(out, x, target, post_w)

    dw_out = _matmul(merged, dout, "tn", 4 * LANES, BF16, "dw_out")
    if w_out_grads is None:
        dmerged, gdn_rider = _matmul(dout, w_out, "nt", 4 * LANES, F32, "dmerged"), None
    else:
        dmerged, got = _matmul(dout, w_out, "nt", 4 * LANES, F32, "dmerged", w_out_grads[0](dw_out))
        gdn_rider = w_out_grads[1](dw_out, got)
    do_gdn, dproj, dgdn_norm_w = _merge_gdn_bwd(o_gdn, proj, gdn_norm_w, dmerged, nh)
    do_fox, dproj = _merge_fox_bwd(o_fox, proj, dmerged, dproj, nh)
    dqn, dkn, dc_t, dproj = _fox_bwd(qkn, proj, gates, gtf, fox_lse, do_fox, dproj, nh)
    dproj, dqk_w = _fox_prep_bwd(proj, qk_w, dqn, dkn, dproj, nh)
    (dgq, dgk, dgv, dgate), w_out_parts = _gdn_bwd(qkv, gates, gt3, s_all, t_all, do_gdn, nh, gdn_rider)
    dproj, dconv_wt = _gdn_prep_bwd(proj, conv_wt, dgq, dgk, dgv, dproj, nh)
    dc_rows = jnp.pad(dc_t.reshape(nh, -1), ((2 * nh, LANES - 3 * nh), (0, 0)))
    dproj, gate_sums = _gates_bwd(proj, bias_row, nega_row, gates, dgate, dc_rows, dproj, nh)
    return dict(
        loss=loss_blk[0:1, 0:1], dy=dy, xn=xn, dproj=dproj, post_w=dpost_w,
        conv_wt=dconv_wt, a_log=gate_sums[1:2, nh:2 * nh], dt_bias=gate_sums[0:1, nh:2 * nh],
        gdn_norm_w=dgdn_norm_w, fq_w=dqk_w[0], fk_w=dqk_w[1], f_bias=gate_sums[0:1, 2 * nh:3 * nh], w_out=dw_out,
        w_out_parts=w_out_parts)


def _cast_bf16(a, tr, name):
    r, c = a.shape

    def body(a_ref, o_ref):
        o_ref[...] = a_ref[...].astype(BF16)

    return pl.pallas_call(
        body, grid=(r // tr,), in_specs=[pl.BlockSpec((tr, c), lambda i: (i, 0))],
        out_specs=pl.BlockSpec((tr, c), lambda i: (i, 0)), out_shape=jax.ShapeDtypeStruct((r, c), BF16),
        name=name, compiler_params=_cp("parallel"))(a)


def _column_major(a):
    return jnp.transpose(a, (2, 0, 1))


def _cast_bf16_column_major(a3, name):
    _, r, c = a3.shape

    def body(a_ref, o_ref):
        o_ref[...] = a_ref[...].reshape(LANES, r).T.astype(BF16)

    return pl.pallas_call(
        body, grid=(pl.cdiv(c, LANES),), in_specs=[pl.BlockSpec((LANES, 1, r), lambda i: (i, 0, 0))],
        out_specs=pl.BlockSpec((r, LANES), lambda i: (0, i)), out_shape=jax.ShapeDtypeStruct((r, c), BF16),
        name=name, compiler_params=_cp("parallel"))(_column_major(a3))


def _adamw_column_major(w3, parts, m3, v3, name):
    _, r, c = w3.shape
    n_parts = parts.shape[0]

    def body(w_ref, p_ref, m_ref, v_ref, g_ref, d_ref, nm_ref, nv_ref):
        g = p_ref[0].astype(F32)
        for s in range(1, n_parts):
            g = g + p_ref[s].astype(F32)
        g = g.T
        flat = lambda ref: ref[...].reshape(LANES, r)
        m_new = ADAM_B1 * flat(m_ref) + (1.0 - ADAM_B1) * g
        v_new = ADAM_B2 * flat(v_ref) + (1.0 - ADAM_B2) * (g * g)
        m_hat = m_new / (1.0 - ADAM_B1 ** ADAM_STEP)
        v_hat = v_new / (1.0 - ADAM_B2 ** ADAM_STEP)
        delta = -ADAM_LR * (m_hat / (jnp.sqrt(v_hat) + ADAM_EPS) + ADAM_WD * flat(w_ref))
        for ref, val in ((g_ref, g), (d_ref, delta), (nm_ref, m_new), (nv_ref, v_new)):
            ref[...] = val.reshape(LANES, 1, r)

    blk = pl.BlockSpec((LANES, 1, r), lambda i: (i, 0, 0))
    outs = pl.pallas_call(
        body, grid=(pl.cdiv(c, LANES),), in_specs=[blk, pl.BlockSpec((n_parts, r, LANES), lambda i: (0, 0, i)), blk, blk],
        out_specs=[blk] * 4, out_shape=[jax.ShapeDtypeStruct((c, 1, r), F32)] * 4, name=name,
        compiler_params=_cp("parallel"))(_column_major(w3), parts, _column_major(m3), _column_major(v3))
    return [jnp.transpose(o, (1, 2, 0)) for o in outs]


def _gather_copies(ins, outs, send_sems, recv_sems, local_sems):
    n = len(ins)
    x, y, c = lax.axis_index("x"), lax.axis_index("y"), lax.axis_index("c")
    me, sibling = (x, y, c), (x, y, 1 - c)
    xn, yn, dg = (1 - x, y), (x, 1 - y), (1 - x, 1 - y)

    def copy(a, k, block, to, src=None):
        px, py, pc = block
        rows = outs[a].at[4 * px + 2 * py + pc]
        return pltpu.make_async_remote_copy(
            src_ref=rows if src is None else src, dst_ref=rows, send_sem=send_sems.at[a, k],
            recv_sem=recv_sems.at[a, k], device_id=to, device_id_type=_MESH)

    local = [pltpu.make_async_copy(ins[a], outs[a].at[4 * x + 2 * y + c], local_sems.at[a]) for a in range(n)]
    own = [cp for a in range(n) for cp in (copy(a, 0, me, sibling, src=ins[a]), copy(a, 1, me, (*xn, c), src=ins[a]),
                                           copy(a, 2, me, (*yn, c), src=ins[a]))]

    def start():
        for cp in local + own:
            cp.start()

    def finish():
        for a in range(n):
            @pl.when(c == 1)
            def _(a=a):
                copy(a, 1, (*xn, c), me).wait_recv()
                copy(a, 3, (*xn, c), (*yn, c)).start()

            @pl.when(c == 0)
            def _(a=a):
                copy(a, 2, (*yn, c), me).wait_recv()
                copy(a, 3, (*yn, c), (*xn, c)).start()
        for a in range(n):
            pl.when(c == 0)(copy(a, 1, (*xn, c), me).wait_recv)
            copy(a, 4, (*xn, c), sibling).start()
            pl.when(c == 1)(copy(a, 2, (*yn, c), me).wait_recv)
            copy(a, 5, (*yn, c), sibling).start()
        for a in range(n):
            copy(a, 3, (*dg, c), me).wait_recv()
            copy(a, 6, (*dg, c), sibling).start()
        for a in range(n):
            copy(a, 0, sibling, me).wait_recv()
            for k, chip in ((4, xn), (5, yn), (6, dg)):
                copy(a, k, (*chip, 1 - c), me).wait_recv()
                copy(a, k, (*chip, c), sibling).wait_send()
            copy(a, 3, (*xn, c), (*yn, c)).wait_send()
        for cp in own:
            cp.wait_send()
        for cp in local:
            cp.wait()

    return start, finish


def _gather_scratch(n):
    return [pltpu.SemaphoreType.DMA((n, N_DEV - 1)), pltpu.SemaphoreType.DMA((n, N_DEV - 1)), pltpu.SemaphoreType.DMA((n,))]


def _gather_rider(arrays):
    return _Rider(list(arrays), [jax.ShapeDtypeStruct((N_DEV,) + a.shape, a.dtype) for a in arrays],
                  _gather_scratch(len(arrays)), {}, lambda ins, outs, scratch: _gather_copies(ins, outs, *scratch))


def _all_gather(arrays, name):
    n = len(arrays)

    def body(*refs):
        start, finish = _gather_copies(refs[:n], refs[n:2 * n], *refs[2 * n:])
        start()
        finish()

    return pl.pallas_call(
        body, in_specs=[_ANY] * n, out_specs=[_ANY] * n,
        out_shape=[jax.ShapeDtypeStruct((N_DEV,) + a.shape, a.dtype) for a in arrays],
        scratch_shapes=_gather_scratch(n), name=name)(*arrays)


SLAB = 10 * LANES


def _slab_start(blk, nh, cols):
    in_second_half = blk >= N_DEV // 2
    shift = (2 * nh if in_second_half else 0) if isinstance(blk, int) else jnp.where(in_second_half, 2 * nh, 0)
    return (blk * cols - shift) // LANES * LANES


def _pair_rider(dw_rows=None, parts=None, nh=None):
    if dw_rows is not None:
        r, full = dw_rows.shape
        cols = (full - NARROW + 3 * nh) // N_DEV
        out_shapes = [jax.ShapeDtypeStruct((N_CHIP, r, SLAB), dw_rows.dtype), jax.ShapeDtypeStruct((r, NARROW), dw_rows.dtype)]
    else:
        out_shapes = [jax.ShapeDtypeStruct((N_CHIP,) + parts.shape[1:], parts.dtype)]

    def make(ins, outs, scratch):
        send_sems, recv_sems = scratch
        x, y, c = lax.axis_index("x"), lax.axis_index("y"), lax.axis_index("c")
        kw = lambda k: dict(send_sem=send_sems.at[k], recv_sem=recv_sems.at[k], device_id=(x, y, 1 - c), device_id_type=_MESH)
        copies = []
        for q in range(N_CHIP):
            if dw_rows is not None:
                first = pl.multiple_of(_slab_start(2 * q + 1 - c, nh, cols), LANES)
                copies.append(pltpu.make_async_remote_copy(src_ref=ins[0].at[:, pl.ds(first, SLAB)], dst_ref=outs[0].at[q], **kw(q)))
            else:
                copies.append(pltpu.make_async_remote_copy(src_ref=ins[0].at[2 * q + 1 - c], dst_ref=outs[0].at[q], **kw(q)))
        if dw_rows is not None:
            copies.append(pltpu.make_async_remote_copy(src_ref=ins[0].at[:, pl.ds(full - NARROW, NARROW)], dst_ref=outs[1],
                                                       **kw(N_CHIP)))

        def start():
            for cp in copies:
                cp.start()

        def finish():
            for cp in copies:
                cp.wait()

        return start, finish

    return _Rider([dw_rows if dw_rows is not None else parts], out_shapes,
                  [pltpu.SemaphoreType.DMA((N_CHIP + 1,)), pltpu.SemaphoreType.DMA((N_CHIP + 1,))], {}, make)


def _relayout_pair_sum(dwfull, got_slabs, got_tail, core, nh, tr, name):
    d, full = dwfull.shape
    w = nh * HEAD_DIM
    cols = (8 * w + 3 * nh) // N_DEV
    segs = _native_segments(nh)

    def block(f_ref, s_ref, t_ref, q, blk):
        st = _slab_start(blk, nh, cols)
        wide = f_ref[:, st:st + SLAB].astype(F32) + s_ref[q].astype(F32)
        tail = f_ref[:, 8 * w:].astype(F32) + t_ref[...].astype(F32)
        pieces = []
        for s0, s1, t0 in segs:
            lo, hi = max(s0, blk * cols), min(s1, (blk + 1) * cols)
            if lo < hi:
                at = t0 + lo - s0
                pieces.append(tail[:, at - 8 * w:at - 8 * w + hi - lo] if at >= 8 * w else wide[:, at - st:at - st + hi - lo])
        return (pieces[0] if len(pieces) == 1 else jnp.concatenate(pieces, axis=1)).astype(dwfull.dtype)

    def body(core_ref, f_ref, s_ref, t_ref, o_ref):
        for parity in range(2):
            @pl.when(core_ref[0] == parity)
            def _(parity=parity):
                for q in range(N_CHIP):
                    o_ref[q] = block(f_ref, s_ref, t_ref, q, 2 * q + parity)

    return pl.pallas_call(
        body,
        grid_spec=pltpu.PrefetchScalarGridSpec(
            num_scalar_prefetch=1, grid=(d // tr,),
            in_specs=[pl.BlockSpec((tr, full), lambda i, c_ref: (i, 0)), pl.BlockSpec((N_CHIP, tr, SLAB), lambda i, c_ref: (0, i, 0)),
                      pl.BlockSpec((tr, NARROW), lambda i, c_ref: (i, 0))],
            out_specs=pl.BlockSpec((N_CHIP, tr, cols), lambda i, c_ref: (0, i, 0))),
        out_shape=jax.ShapeDtypeStruct((N_CHIP, d, cols), dwfull.dtype), name=name,
        compiler_params=_cp("parallel"))(core, dwfull, got_slabs, got_tail)


def _pair_sum(parts, got, core, tr, name):
    _, r, c = parts.shape

    def body(core_ref, p_ref, g_ref, o_ref):
        o_ref[...] = (p_ref[...].astype(F32) + g_ref[...].astype(F32)).astype(o_ref.dtype)

    return pl.pallas_call(
        body,
        grid_spec=pltpu.PrefetchScalarGridSpec(
            num_scalar_prefetch=1, grid=(N_CHIP, r // tr),
            in_specs=[pl.BlockSpec((1, tr, c), lambda q, i, core_ref: (2 * q + core_ref[0], i, 0)),
                      pl.BlockSpec((1, tr, c), lambda q, i, core_ref: (q, i, 0))],
            out_specs=pl.BlockSpec((1, tr, c), lambda q, i, core_ref: (q, i, 0))),
        out_shape=jax.ShapeDtypeStruct((N_CHIP, r, c), parts.dtype), name=name,
        compiler_params=_cp("parallel", "parallel"))(core, parts, got)


def _native_segments(nh):
    w = nh * HEAD_DIM
    return [(0, 4 * w, 0), (4 * w, 4 * w + 2 * nh, 8 * w), (4 * w + 2 * nh, 8 * w + 2 * nh, 4 * w),
            (8 * w + 2 * nh, 8 * w + 3 * nh, 8 * w + 2 * nh)]


def _relayout_w_in(wg, nh, tr):
    _, d, cols = wg.shape
    w = nh * HEAD_DIM

    def native(ref, j0, j1):
        out = []
        while j0 < j1:
            blk = j0 // cols
            end = min(j1, (blk + 1) * cols)
            out.append(ref[blk, :, pl.ds(j0 - blk * cols, end - j0)])
            j0 = end
        return out

    def body(g_ref, o_ref):
        for cidx in range(8 * w // LANES):
            j0 = cidx * LANES + (0 if cidx * LANES < 4 * w else 2 * nh)
            pieces = native(g_ref, j0, j0 + LANES)
            o_ref[:, cidx * LANES:(cidx + 1) * LANES] = pieces[0] if len(pieces) == 1 else jnp.concatenate(pieces, axis=1)
        pieces = (native(g_ref, 4 * w, 4 * w + 2 * nh) + native(g_ref, 8 * w + 2 * nh, 8 * w + 3 * nh)
                  + [jnp.zeros((tr, NARROW - 3 * nh), wg.dtype)])
        o_ref[:, 8 * w:] = jnp.concatenate(pieces, axis=1)

    return pl.pallas_call(
        body, grid=(d // tr,), in_specs=[pl.BlockSpec((N_DEV, tr, cols), lambda i: (0, i, 0))],
        out_specs=pl.BlockSpec((tr, 8 * w + NARROW), lambda i: (i, 0)),
        out_shape=jax.ShapeDtypeStruct((d, 8 * w + NARROW), wg.dtype),
        name="relayout_w_in", compiler_params=_cp("parallel"))(wg)


def _adamw(w, parts, m, v, tr, name):
    r, c = w.shape
    n_parts = parts.shape[0]

    def body(w_ref, p_ref, m_ref, v_ref, g_ref, d_ref, nm_ref, nv_ref):
        g = p_ref[0].astype(F32)
        for s in range(1, n_parts):
            g = g + p_ref[s].astype(F32)
        m_new = ADAM_B1 * m_ref[...] + (1.0 - ADAM_B1) * g
        v_new = ADAM_B2 * v_ref[...] + (1.0 - ADAM_B2) * (g * g)
        m_hat = m_new / (1.0 - ADAM_B1 ** ADAM_STEP)
        v_hat = v_new / (1.0 - ADAM_B2 ** ADAM_STEP)
        g_ref[...] = g
        d_ref[...] = -ADAM_LR * (m_hat / (jnp.sqrt(v_hat) + ADAM_EPS) + ADAM_WD * w_ref[...])
        nm_ref[...] = m_new
        nv_ref[...] = v_new

    blk = pl.BlockSpec((tr, c), lambda i: (i, 0))
    return pl.pallas_call(
        body, grid=(r // tr,), in_specs=[blk, pl.BlockSpec((n_parts, tr, c), lambda i: (0, i, 0)), blk, blk],
        out_specs=[blk] * 4, out_shape=[jax.ShapeDtypeStruct((r, c), F32)] * 4, name=name,
        compiler_params=_cp("parallel"))(w, parts, m, v)


def _pack_small(d, pre, post, a_log, dt_bias, f_bias, gdn_w, fq_w, fk_w, extra):
    row2 = jnp.concatenate([a_log, dt_bias, f_bias, gdn_w, fq_w, fk_w, extra], axis=1)
    row2 = jnp.pad(row2, ((0, 0), (0, d - row2.shape[1])))
    return jnp.concatenate([pre, post, row2, jnp.zeros((5, d), F32)], axis=0)


def _unpack_small(p, nh):
    o = 3 * nh
    return dict(pre=p[0:1], post=p[1:2], a_log=p[2:3, 0:nh], dt_bias=p[2:3, nh:2 * nh], f_bias=p[2:3, 2 * nh:o],
                gdn_w=p[2:3, o:o + HEAD_DIM], fq_w=p[2:3, o + HEAD_DIM:o + 2 * HEAD_DIM],
                fk_w=p[2:3, o + 2 * HEAD_DIM:o + 3 * HEAD_DIM], extra=p[2, o + 3 * HEAD_DIM])


def kernel(x, meta_tokens, pre_norm_w, w_in, conv_w, a_log, dt_bias, gdn_norm_w, fox_q_norm_w, fox_k_norm_w, fox_f_bias, w_out, post_norm_w, loss_target, m_meta_tokens, m_pre_norm_w, m_w_in, m_conv_w, m_a_log, m_dt_bias, m_gdn_norm_w, m_fox_q_norm_w, m_fox_k_norm_w, m_fox_f_bias, m_w_out, m_post_norm_w, v_meta_tokens, v_pre_norm_w, v_w_in, v_conv_w, v_a_log, v_dt_bias, v_gdn_norm_w, v_fox_q_norm_w, v_fox_k_norm_w, v_fox_f_bias, v_w_out, v_post_norm_w):
    nh = a_log.shape[1]
    d = x.shape[-1]
    w = nh * HEAD_DIM
    zero = jnp.zeros((1, 1), F32)

    wg, cg, mg = _all_gather([_cast_bf16_column_major(w_in, "cast_w_in"), conv_w[0].T, meta_tokens], "gather_weights")
    wfull = _relayout_w_in(wg, nh, 256)
    conv_wt = cg.transpose(1, 0, 2).reshape(CONV_WIDTH, 3 * w)
    meta_full = mg.transpose(1, 0, 2).reshape(N_META, d)
    late_weights = (_gather_rider([_cast_bf16(w_out[0], 256, "cast_w_out")]), lambda got: got[0].reshape(2 * w, d))
    core = lax.axis_index("c")
    dev = 4 * lax.axis_index("x") + 2 * lax.axis_index("y") + core
    core_arr = jnp.reshape(core, (1,)).astype(jnp.int32)

    out_parts = lambda dw_out: dw_out.reshape(N_DEV, 2 * w // N_DEV, d)
    g = _layer_grads(
        x[0], loss_target[0], meta_full, pre_norm_w, wfull, conv_wt, a_log, dt_bias, gdn_norm_w,
        fox_q_norm_w, fox_k_norm_w, fox_f_bias, None, post_norm_w, late_weights=late_weights,
        w_out_grads=(lambda dw_out: _pair_rider(parts=out_parts(dw_out)),
                     lambda dw_out, got: _chip_rider(_pair_sum(out_parts(dw_out), got[0], core_arr, 256, "pair_sum_w_out"))))
    p_out = g["w_out_parts"][0]
    xn, dproj, half = g["xn"], g["dproj"], d // 2
    dw_a = _matmul(xn, dproj, "tn", MM_TILE, BF16, "dw_in_a", a_cols=(half, 0))
    dw_b, got_a = _matmul(xn, dproj, "tn", MM_TILE, BF16, "dw_in_b", _pair_rider(dw_rows=dw_a, nh=nh), a_cols=(half, 1))
    sums_a = _relayout_pair_sum(dw_a, got_a[0], got_a[1], core_arr, nh, 128, "relayout_pair_sum_a")
    dxn, (p_in_a, got_b) = _dxn(dproj, wfull, [_chip_rider(sums_a, rows_total=d), _pair_rider(dw_rows=dw_b, nh=nh)],
                                MM_TILE, "dxn")
    sums_b = _relayout_pair_sum(dw_b, got_b[0], got_b[1], core_arr, nh, 128, "relayout_pair_sum_b")
    (grad_x, dmeta, dpre_w), p_in = _prenorm_bwd(dxn, x[0], meta_full, pre_norm_w, g["dy"],
                                                 _chip_rider(sums_b, rows_total=d, row0=half, into=p_in_a[0]))
    p_in = p_in[0]
    small = _pack_small(d, dpre_w, g["post_w"], g["a_log"], g["dt_bias"], g["f_bias"], g["gdn_norm_w"], g["fq_w"],
                        g["fk_w"], g["loss"])
    a_conv, a_meta, p_small = _all_gather([g["conv_wt"], dmeta, small], "gather_small_grads")
    p_conv = lax.dynamic_slice_in_dim(a_conv, dev * conv_w.shape[1], conv_w.shape[1], axis=2).transpose(0, 2, 1)
    p_meta = lax.dynamic_slice_in_dim(a_meta, dev * meta_tokens.shape[1], meta_tokens.shape[1], axis=2)

    r_in = _adamw_column_major(w_in, p_in, m_w_in, v_w_in, "adamw_w_in")
    r_out = _adamw(w_out[0], p_out, m_w_out[0], v_w_out[0], 64, "adamw_w_out")
    r_conv = _adamw(conv_w[0], p_conv, m_conv_w[0], v_conv_w[0], conv_w.shape[1], "adamw_conv_w")
    r_meta = _adamw(meta_tokens, p_meta, m_meta_tokens, v_meta_tokens, N_META, "adamw_meta")
    pk = lambda pre, post, a, dt, gw, fq, fk, fb: _pack_small(d, pre, post, a, dt, fb, gw, fq, fk, zero)
    r_small = _adamw(
        pk(pre_norm_w, post_norm_w, a_log, dt_bias, gdn_norm_w, fox_q_norm_w, fox_k_norm_w, fox_f_bias), p_small,
        pk(m_pre_norm_w, m_post_norm_w, m_a_log, m_dt_bias, m_gdn_norm_w, m_fox_q_norm_w, m_fox_k_norm_w, m_fox_f_bias),
        pk(v_pre_norm_w, v_post_norm_w, v_a_log, v_dt_bias, v_gdn_norm_w, v_fox_q_norm_w, v_fox_k_norm_w, v_fox_f_bias),
        8, "adamw_small")

    sm = [_unpack_small(r, nh) for r in r_small]
    outs = []
    for i in range(4):
        s = sm[i]
        outs += [r_meta[i], s["pre"], r_in[i], r_conv[i][None], s["a_log"], s["dt_bias"], s["gdn_w"], s["fq_w"],
                 s["fk_w"], s["f_bias"], r_out[i][None], s["post"]]
    return (sm[0]["extra"], grad_x[None], *outs)
```

```python
import jax
import jax.numpy as jnp
from jax import lax
from jax.experimental import pallas as pl
from jax.experimental.pallas import tpu as pltpu

F32, BF16 = jnp.float32, jnp.bfloat16
HEAD_DIM = 128
N_META = 16
CONV_WIDTH = 4
CHUNK = 128
Q_BLOCK = 128
LANES = 128
EPS = 1e-6
PAD_ROWS = Q_BLOCK - N_META
N_DEV = 8
N_CHIP = 4
VMEM_LIMIT = 56 * 1024 * 1024
NEG = -1e30
NARROW = 2 * LANES
MM_TILE = 6 * LANES

ADAM_LR, ADAM_B1, ADAM_B2, ADAM_EPS, ADAM_WD, ADAM_STEP = 0.001, 0.9, 0.999, 1e-08, 0.01, 10

_DN = {"nn": (((1,), (0,)), ((), ())), "nt": (((1,), (1,)), ((), ())), "tn": (((0,), (0,)), ((), ()))}
_DN3 = {"nn": (((2,), (1,)), ((0,), (0,))), "nt": (((2,), (2,)), ((0,), (0,))), "tn": (((1,), (1,)), ((0,), (0,)))}
_ANY = pl.BlockSpec(memory_space=pl.ANY)
_MESH = pl.DeviceIdType.MESH


def _cp(*sem):
    return pltpu.CompilerParams(dimension_semantics=sem, vmem_limit_bytes=VMEM_LIMIT)


def _dot(a, b, dims="nn", prec=None):
    return lax.dot_general(a, b, _DN[dims], precision=prec, preferred_element_type=F32)


def _bdot(a, b, dims="nn"):
    return _dot(a.astype(BF16), b.astype(BF16), dims)


def _hdot(a, b, dims="nn"):
    return _dot(a, b, dims, prec=lax.Precision.HIGHEST)


def _dot3(a, b, dims="nn"):
    return lax.dot_general(a, b, _DN3[dims], preferred_element_type=F32)


def _bdot3(a, b, dims="nn"):
    return _dot3(a.astype(BF16), b.astype(BF16), dims)


def _split(a):
    hi = a.astype(BF16)
    return hi, (a - hi.astype(F32)).astype(BF16)


def _iota(shape, dim):
    return lax.broadcasted_iota(jnp.int32, shape, dim)


def _sigmoid(z):
    return 1.0 / (1.0 + jnp.exp(-z))


def _softplus(z):
    e = jnp.exp(-jnp.abs(z))
    u = 1.0 + e
    l1p = jnp.where(u == 1.0, e, jnp.log(u) * (e / jnp.where(u == 1.0, 1.0, u - 1.0)))
    return jnp.maximum(z, 0.0) + l1p


def _silu_and_grad(z):
    s = _sigmoid(z)
    return z * s, s * (1.0 + z * (1.0 - s))


def _rms(x):
    return lax.rsqrt(jnp.mean(x * x, axis=-1, keepdims=True) + EPS)


def _h_tile(i, x_ref, meta_ref):
    first = jnp.concatenate([jnp.zeros((PAD_ROWS, x_ref.shape[1]), F32), meta_ref[...]], axis=0)
    return jnp.where(i == 0, first, x_ref[...])


def _x_rows(d):
    return pl.BlockSpec((Q_BLOCK, d), lambda i: (jnp.maximum(i - 1, 0), 0))


def _prenorm(x, meta, w):
    seq, d = x.shape
    lp = seq + Q_BLOCK

    def body(x_ref, m_ref, w_ref, o_ref):
        h = _h_tile(pl.program_id(0), x_ref, m_ref)
        o_ref[...] = (h * _rms(h) * w_ref[...]).astype(BF16)

    return pl.pallas_call(
        body, grid=(lp // Q_BLOCK,),
        in_specs=[_x_rows(d), pl.BlockSpec((N_META, d), lambda i: (0, 0)), pl.BlockSpec((1, d), lambda i: (0, 0))],
        out_specs=pl.BlockSpec((Q_BLOCK, d), lambda i: (i, 0)),
        out_shape=jax.ShapeDtypeStruct((lp, d), BF16), name="prenorm", compiler_params=_cp("parallel"))(x, meta, w)


def _tile(n, want):
    return max(t for t in range(LANES, want + 1, LANES) if n % t == 0)


class _Rider:
    def __init__(self, inputs, out_shapes, scratch, aliases, make):
        self.inputs, self.out_shapes, self.scratch, self.aliases, self.make = inputs, out_shapes, scratch, aliases, make


def _hosted_call(body, riders, n_in, n_out, n_scratch, *, in_specs, out_specs, out_shape, scratch_shapes=(), aliases=None,
                 **kw):
    riders = [r for r in riders if r is not None]
    r_in = [len(r.inputs) for r in riders]
    r_out = [len(r.out_shapes) for r in riders]
    r_scr = [len(r.scratch) for r in riders]
    al = dict(aliases or {})
    for k, r in enumerate(riders):
        al.update({n_in + sum(r_in[:k]) + i: n_out + sum(r_out[:k]) + o for i, o in r.aliases.items()})

    def full_body(*refs):
        ins, rest = refs[:n_in + sum(r_in)], refs[n_in + sum(r_in):]
        outs, scr = rest[:n_out + sum(r_out)], rest[n_out + sum(r_out):]
        hooks = [r.make(ins[n_in + sum(r_in[:k]):n_in + sum(r_in[:k + 1])], outs[n_out + sum(r_out[:k]):n_out + sum(r_out[:k + 1])],
                        scr[n_scratch + sum(r_scr[:k]):n_scratch + sum(r_scr[:k + 1])]) for k, r in enumerate(riders)]

        def start():
            for h in hooks:
                h[0]()

        def finish():
            for h in hooks:
                h[1]()

        body(start, finish, *ins[:n_in], *outs[:n_out], *scr[:n_scratch])

    call = pl.pallas_call(
        full_body, in_specs=list(in_specs) + [_ANY] * sum(r_in), out_specs=list(out_specs) + [_ANY] * sum(r_out),
        out_shape=list(out_shape) + [s for r in riders for s in r.out_shapes],
        scratch_shapes=list(scratch_shapes) + [s for r in riders for s in r.scratch], input_output_aliases=al, **kw)

    def run(*args):
        res = call(*args, *[t for r in riders for t in r.inputs])
        return res[:n_out], [res[n_out + sum(r_out[:k]):n_out + sum(r_out[:k + 1])] for k in range(len(riders))]

    return run


def _matmul(a, b, dims, tn, out_dtype, name, rider=None, a_cols=None):
    a_shape = a.shape if a_cols is None else (a.shape[0], a_cols[0])
    a_index = 0 if a_cols is None else a_cols[1]
    m = a_shape[1] if dims == "tn" else a_shape[0]
    n = b.shape[0] if dims == "nt" else b.shape[1]
    kdim = b.shape[1] if dims == "nt" else b.shape[0]
    tn = _tile(n, tn)
    steps = n // tn
    b_spec = pl.BlockSpec((tn, kdim), lambda j: (j, 0)) if dims == "nt" else pl.BlockSpec((kdim, tn), lambda j: (0, j))

    def body(start, finish, a_ref, b_ref, o_ref):
        pl.when(pl.program_id(0) == 0)(start)
        o_ref[...] = _dot(a_ref[...], b_ref[...], dims).astype(out_dtype)
        pl.when(pl.program_id(0) == steps - 1)(finish)

    (out,), got = _hosted_call(
        body, [rider], 2, 1, 0, grid=(steps,), in_specs=[pl.BlockSpec(a_shape, lambda j: (0, a_index)), b_spec],
        out_specs=[pl.BlockSpec((m, tn), lambda j: (0, j))], out_shape=[jax.ShapeDtypeStruct((m, n), out_dtype)],
        name=name, compiler_params=_cp("parallel" if rider is None else "arbitrary"))(a, b)
    return out if rider is None else (out, got[0])


def _chip_rider(sums, rows_total=None, row0=0, into=None):
    _, r, c = sums.shape
    rows_total = rows_total or r

    def make(ins, outs, scratch):
        send_sems, recv_sems, local_sem = scratch
        x, y, core = lax.axis_index("x"), lax.axis_index("y"), lax.axis_index("c")
        mine = 2 * x + y
        land = lambda chip: outs[0].at[chip].at[pl.ds(row0, r)]
        local = pltpu.make_async_copy(ins[0].at[mine], land(mine), local_sem)
        sends, recvs = [], []
        for k in range(1, N_CHIP):
            px = 1 - x if k & 2 else x
            py = 1 - y if k & 1 else y
            kw = dict(send_sem=send_sems.at[k - 1], recv_sem=recv_sems.at[k - 1], device_id=(px, py, core),
                      device_id_type=_MESH)
            sends.append(pltpu.make_async_remote_copy(src_ref=ins[0].at[2 * px + py], dst_ref=land(mine), **kw))
            recvs.append(pltpu.make_async_remote_copy(src_ref=ins[0].at[mine], dst_ref=land(2 * px + py), **kw))

        def start():
            for cp in [local] + sends:
                cp.start()

        def finish():
            local.wait()
            for cp in sends:
                cp.wait_send()
            for cp in recvs:
                cp.wait_recv()

        return start, finish

    return _Rider([sums] + ([] if into is None else [into]), [jax.ShapeDtypeStruct((N_CHIP, rows_total, c), sums.dtype)],
                  [pltpu.SemaphoreType.DMA((N_CHIP - 1,)), pltpu.SemaphoreType.DMA((N_CHIP - 1,)), pltpu.SemaphoreType.DMA(())],
                  {} if into is None else {1: 0}, make)


_HBM = pl.BlockSpec(memory_space=pltpu.HBM)
_SEM = pl.BlockSpec(memory_space=pltpu.SEMAPHORE)


def _chip_exchange_copies(sums_ref, land_ref, send_sems, recv_sems, row0):
    _, r, _ = sums_ref.shape
    x, y, core = lax.axis_index("x"), lax.axis_index("y"), lax.axis_index("c")
    copies = []
    for k in range(1, N_CHIP):
        px = 1 - x if k & 2 else x
        py = 1 - y if k & 1 else y
        copies.append(pltpu.make_async_remote_copy(
            src_ref=sums_ref.at[2 * px + py], dst_ref=land_ref.at[2 * x + y].at[pl.ds(row0, r)], send_sem=send_sems.at[k - 1],
            recv_sem=recv_sems.at[k - 1], device_id=(px, py, core), device_id_type=_MESH))
    return copies


def _chip_exchange_start(sums, land, row0, name):
    def body(s_ref, land_ref, send_sems, recv_sems, s_thru, land_thru, token):
        for cp in _chip_exchange_copies(s_ref, land_ref, send_sems, recv_sems, row0):
            cp.start()
        token[...] = jnp.zeros_like(token)

    return pl.pallas_call(
        body, name=name,
        out_shape=(pltpu.SemaphoreType.DMA((N_CHIP - 1,)), pltpu.SemaphoreType.DMA((N_CHIP - 1,)),
                   pltpu.HBM(sums.shape, sums.dtype), pltpu.HBM(land.shape, land.dtype), jax.ShapeDtypeStruct((8, LANES), F32)),
        in_specs=(_HBM, _HBM), out_specs=(_SEM, _SEM, _HBM, _HBM, pl.BlockSpec(memory_space=pltpu.VMEM)),
        input_output_aliases={0: 2, 1: 3},
        compiler_params=pltpu.CompilerParams(has_side_effects=pltpu.SideEffectType.DATAFLOW_SIDE_EFFECTING))(
            pltpu.with_memory_space_constraint(sums, pltpu.HBM), pltpu.with_memory_space_constraint(land, pltpu.HBM))


def _chip_exchange_wait(send_sems, recv_sems, sums, land, row0, after, name):
    def body(s_ref, land_ref, send_sems, recv_sems, after_ref, s_dead, got_ref):
        for cp in _chip_exchange_copies(s_ref, land_ref, send_sems, recv_sems, row0):
            cp.wait_send()
            cp.wait_recv()

    return pl.pallas_call(
        body, name=name, out_shape=(pltpu.HBM(sums.shape, sums.dtype), pltpu.HBM(land.shape, land.dtype)),
        in_specs=(_HBM, _HBM, _SEM, _SEM, _ANY), out_specs=(_HBM, _HBM), input_output_aliases={0: 0, 1: 1},
        compiler_params=pltpu.CompilerParams(has_side_effects=pltpu.SideEffectType.DATAFLOW_SIDE_EFFECTING))(
            sums, land, send_sems, recv_sems, after)[1]


def _dxn(dproj, wfull, riders, tk, name):
    m, k = dproj.shape
    n = wfull.shape[0]
    tk = _tile(k, tk)
    steps = k // tk

    def body(start, finish, a_ref, b_ref, o_ref):
        j = pl.program_id(0)

        @pl.when(j == 0)
        def _():
            start()
            o_ref[...] = jnp.zeros_like(o_ref)
        o_ref[...] += _dot(a_ref[...], b_ref[...], "nt")
        pl.when(j == steps - 1)(finish)

    (dxn,), got = _hosted_call(
        body, riders, 2, 1, 0, grid=(steps,),
        in_specs=[pl.BlockSpec((m, tk), lambda j: (0, j)), pl.BlockSpec((n, tk), lambda j: (0, j))],
        out_specs=[pl.BlockSpec((m, n), lambda j: (0, 0))], out_shape=[jax.ShapeDtypeStruct((m, n), F32)],
        name=name, compiler_params=_cp("arbitrary"))(dproj, wfull)
    return dxn, got


def _conv_taps(x, w):
    c = x * w[CONV_WIDTH - 1:CONV_WIDTH, :]
    for j in range(CONV_WIDTH - 1):
        c = c + pltpu.roll(x, CONV_WIDTH - 1 - j, 0) * w[j:j + 1, :]
    return c


def _gdn_prep(proj, conv_wt, nh):
    lp = proj.shape[0]
    scale = HEAD_DIM ** -0.5

    def body(x_ref, w_ref, o_ref):
        which = pl.program_id(0) // nh
        c = _conv_taps(x_ref[...], w_ref[...])
        s = c * _sigmoid(c)
        r = lax.rsqrt(jnp.sum(s * s, axis=-1, keepdims=True) + EPS)
        f = jnp.where(which == 0, r * scale, jnp.where(which == 1, r, 1.0))
        o_ref[...] = jnp.where(_iota(s.shape, 0) >= PAD_ROWS, s * f, 0.0)

    return pl.pallas_call(
        body, grid=(3 * nh,),
        in_specs=[pl.BlockSpec((lp, LANES), lambda s: (0, s)), pl.BlockSpec((CONV_WIDTH, LANES), lambda s: (0, s))],
        out_specs=pl.BlockSpec((lp, LANES), lambda s: (0, s)),
        out_shape=jax.ShapeDtypeStruct((lp, 3 * nh * HEAD_DIM), F32), name="gdn_prep",
        compiler_params=_cp("parallel"))(proj, conv_wt)


def _gdn_prep_bwd(proj, conv_wt, dq, dk, dv, dproj, nh):
    lp = proj.shape[0]
    scale = HEAD_DIM ** -0.5
    part = lambda p: pl.BlockSpec((lp, LANES), lambda s: (0, jnp.clip(s - p * nh, 0, nh - 1)))

    def body(x_ref, w_ref, dq_ref, dk_ref, dv_ref, _, dx_ref, dw_ref):
        which = pl.program_id(0) // nh
        x = x_ref[...]
        w = w_ref[...]
        c = _conv_taps(x, w)
        sg = _sigmoid(c)
        s = c * sg
        r = lax.rsqrt(jnp.sum(s * s, axis=-1, keepdims=True) + EPS)
        dy = jnp.where(which == 0, dq_ref[...], jnp.where(which == 1, dk_ref[...], dv_ref[...]))
        dy = jnp.where(_iota(s.shape, 0) >= PAD_ROWS, dy, 0.0)
        y0 = s * r
        dy0 = dy * jnp.where(which == 0, scale, 1.0)
        ds_n = r * (dy0 - y0 * jnp.sum(dy0 * y0, axis=-1, keepdims=True))
        ds = jnp.where(which == 2, dy, ds_n)
        dc = ds * (sg * (1.0 + c * (1.0 - sg)))
        dx = dc * w[CONV_WIDTH - 1:CONV_WIDTH, :]
        rows = [jnp.sum(dc * x, axis=0, keepdims=True)]
        for j in range(CONV_WIDTH - 2, -1, -1):
            sh = CONV_WIDTH - 1 - j
            dx = dx + pltpu.roll(dc, lp - sh, 0) * w[j:j + 1, :]
            rows.insert(0, jnp.sum(dc * pltpu.roll(x, sh, 0), axis=0, keepdims=True))
        dx_ref[...] = dx.astype(BF16)
        dw_ref[...] = jnp.concatenate(rows, axis=0)

    strip = pl.BlockSpec((lp, LANES), lambda s: (0, s))
    taps = pl.BlockSpec((CONV_WIDTH, LANES), lambda s: (0, s))
    return pl.pallas_call(
        body, grid=(3 * nh,), in_specs=[strip, taps, part(0), part(1), part(2), _ANY], out_specs=[strip, taps],
        out_shape=[jax.ShapeDtypeStruct(dproj.shape, BF16), jax.ShapeDtypeStruct((CONV_WIDTH, 3 * nh * HEAD_DIM), F32)],
        input_output_aliases={5: 0}, name="gdn_prep_bwd", compiler_params=_cp("parallel"))(proj, conv_wt, dq, dk, dv, dproj)


def _gates(proj, bias_row, nega_row, nh):
    lp = proj.shape[0]
    nc = lp // CHUNK

    def body(p_ref, b_ref, a_ref, g_ref, gt3_ref, gtf_ref):
        lane = _iota((CHUNK, LANES), 1)
        tri = (_iota((CHUNK, CHUNK), 0) >= _iota((CHUNK, CHUNK), 1)).astype(F32)

        def step(n, carry):
            r0 = pl.multiple_of(n * CHUNK, CHUNK)
            z = p_ref[pl.ds(r0, CHUNK), :] + b_ref[...]
            base = jnp.where(lane < nh, _sigmoid(z),
                             jnp.where(lane < 2 * nh, a_ref[...] * _softplus(z),
                                       jnp.where(lane < 3 * nh, -_softplus(-z), 0.0)))
            base = jnp.where(r0 + _iota((CHUNK, LANES), 0) >= PAD_ROWS, base, 0.0)
            cs = _hdot(tri, base)
            run = jnp.where((lane >= 2 * nh) & (lane < 3 * nh), cs + carry, cs)
            sh = pltpu.roll(run, 2 * nh, 1)
            out = base + jnp.where((lane >= 3 * nh) & (lane < 5 * nh), sh, 0.0)
            g_ref[pl.ds(r0, CHUNK), :] = out
            gt3_ref[n] = out.T
            return carry + cs[CHUNK - 1:CHUNK, :]

        lax.fori_loop(0, nc, step, jnp.zeros((1, LANES), F32))
        gtf_ref[...] = g_ref[...].T

    vec = pl.BlockSpec((1, LANES), lambda i: (0, 0))
    return pl.pallas_call(
        body, grid=(1,), in_specs=[pl.BlockSpec((lp, LANES), lambda i: (0, 8 * nh)), vec, vec],
        out_specs=[pl.BlockSpec((lp, LANES), lambda i: (0, 0)), pl.BlockSpec((nc, LANES, CHUNK), lambda i: (0, 0, 0)),
                   pl.BlockSpec((LANES, lp), lambda i: (0, 0))],
        out_shape=[jax.ShapeDtypeStruct((lp, LANES), F32), jax.ShapeDtypeStruct((nc, LANES, CHUNK), F32),
                   jax.ShapeDtypeStruct((LANES, lp), F32)],
        name="gates", compiler_params=_cp("arbitrary"))(proj, bias_row, nega_row)


def _gates_bwd(proj, bias_row, nega_row, gates, dgate_gdn, dc_t, dproj, nh):
    lp = proj.shape[0]
    nc = lp // CHUNK

    def body(p_ref, b_ref, a_ref, g_ref, dg_ref, dc_ref, _, dz_ref, sm_ref, dct_scr):
        lane = _iota((CHUNK, LANES), 1)
        triu = (_iota((CHUNK, CHUNK), 0) <= _iota((CHUNK, CHUNK), 1)).astype(F32)
        dct_scr[...] = dc_ref[...].T
        sm_ref[...] = jnp.zeros_like(sm_ref)
        dz_ref[:, LANES:] = jnp.zeros((lp, NARROW - LANES), BF16)

        def step(i, carry):
            n = nc - 1 - i
            r0 = pl.multiple_of(n * CHUNK, CHUNK)
            z = p_ref[pl.ds(r0, CHUNK), :] + b_ref[...]
            gt = g_ref[pl.ds(r0, CHUNK), :]
            dgd = dg_ref[pl.ds(r0, CHUNK), :]
            dch = dct_scr[pl.ds(r0, CHUNK), :]
            rc = _hdot(triu, dch) + carry
            sg = _sigmoid(z)
            dz = jnp.where(lane < nh, dgd * sg * (1.0 - sg),
                           jnp.where(lane < 2 * nh, dgd * a_ref[...] * sg,
                                     jnp.where(lane < 3 * nh, rc * (1.0 - sg), 0.0)))
            dz = jnp.where(r0 + _iota((CHUNK, LANES), 0) >= PAD_ROWS, dz, 0.0)
            dz_ref[pl.ds(r0, CHUNK), 0:LANES] = dz.astype(BF16)
            sm_ref[0:1, :] += jnp.sum(dz, axis=0, keepdims=True)
            sm_ref[1:2, :] += jnp.sum(jnp.where((lane >= nh) & (lane < 2 * nh), dgd * gt, 0.0), axis=0, keepdims=True)
            return carry + jnp.sum(dch, axis=0, keepdims=True)

        lax.fori_loop(0, nc, step, jnp.zeros((1, LANES), F32))

    vec = pl.BlockSpec((1, LANES), lambda i: (0, 0))
    full = pl.BlockSpec((lp, LANES), lambda i: (0, 0))
    last = pl.BlockSpec((lp, LANES), lambda i: (0, 8 * nh))
    tail = pl.BlockSpec((lp, NARROW), lambda i: (0, 8 * nh * LANES // NARROW))
    return pl.pallas_call(
        body, grid=(1,), in_specs=[last, vec, vec, full, full, pl.BlockSpec((LANES, lp), lambda i: (0, 0)), _ANY],
        out_specs=[tail, pl.BlockSpec((8, LANES), lambda i: (0, 0))],
        out_shape=[jax.ShapeDtypeStruct(dproj.shape, BF16), jax.ShapeDtypeStruct((8, LANES), F32)],
        scratch_shapes=[pltpu.VMEM((lp, LANES), F32)], input_output_aliases={6: 0},
        name="gates_bwd", compiler_params=_cp("arbitrary"))(proj, bias_row, nega_row, gates, dgate_gdn, dc_t, dproj)


def _tri_inv(a):
    t = jnp.where(_iota(a.shape, 1) == _iota(a.shape, 2), 1.0, 0.0) - a
    p = a
    for _ in range(CHUNK.bit_length() - 2):
        ph, pw = _split(p)
        p = _dot3(ph, ph) + (_dot3(ph, pw) + _dot3(pw, ph))
        ph, pw = _split(p)
        th, tw = _split(t)
        t = t + (_dot3(th, ph) + (_dot3(th, pw) + _dot3(tw, ph)))
    return t


def _gdn_chunk(q, k, v, beta, gc, gr, t=None):
    ii, jj = _iota((1, CHUNK, CHUNK), 1), _iota((1, CHUNK, CHUNK), 2)
    causal, strict = ii >= jj, ii > jj
    dm = jnp.where(causal, jnp.exp(jnp.where(causal, gc - gr, 0.0)), 0.0)
    kk = _bdot3(k, k, "nt")
    a = jnp.where(strict, beta * kk * dm, 0.0)
    if t is None:
        t = _tri_inv(a)
    eg = jnp.exp(gc)
    glast = gc[:, CHUNK - 1:CHUNK, :]
    ekd = jnp.exp(glast - gc)
    bv = beta * v
    bk = (beta * eg) * k
    ub = _bdot3(t, jnp.concatenate([bv, bk], axis=2))
    qk = _bdot3(q, k, "nt")
    return dict(causal=causal, strict=strict, dm=dm, kk=kk, a=a, t=t, eg=eg, ekd=ekd, bv=bv, bk=bk,
                u=ub[:, :, :HEAD_DIM], w=ub[:, :, HEAD_DIM:], qk=qk, aqk=jnp.where(causal, qk * dm, 0.0),
                q_dec=q * eg, k_dec=k * ekd, decay=jnp.exp(glast))


def _heads(ref, nh):
    return jnp.stack([ref[:, h * HEAD_DIM:(h + 1) * HEAD_DIM] for h in range(nh)], axis=0)


def _gdn_chunk_inputs(q_ref, k_ref, v_ref, g, gt, nh):
    col = lambda o: jnp.stack([g[:, o + h:o + h + 1] for h in range(nh)], axis=0)
    gr = jnp.stack([gt[3 * nh + h:3 * nh + h + 1, :] for h in range(nh)], axis=0)
    return _heads(q_ref, nh), _heads(k_ref, nh), _heads(v_ref, nh), col(0), col(3 * nh), gr


def _gdn_fwd(qkv, gates, gt3, nh, rider=None):
    lp = qkv.shape[0]
    nc = lp // CHUNK
    w = nh * HEAD_DIM

    def body(start, finish, q_ref, k_ref, v_ref, g_ref, gt_ref, o_ref, sall_ref, tall_ref, s_scr):
        @pl.when(pl.program_id(0) == 0)
        def _():
            start()
            s_scr[...] = jnp.zeros_like(s_scr)
        c = _gdn_chunk(*_gdn_chunk_inputs(q_ref, k_ref, v_ref, g_ref[...], gt_ref[0], nh))
        s = s_scr[...]
        sall_ref[0] = s
        tall_ref[0] = c["t"]
        v_new = c["u"] - _bdot3(c["w"], s)
        o = _bdot3(c["q_dec"], s) + _bdot3(c["aqk"], v_new)
        s_scr[...] = s * c["decay"] + _bdot3(c["k_dec"], v_new, "tn")
        for h in range(nh):
            o_ref[:, h * HEAD_DIM:(h + 1) * HEAD_DIM] = o[h]
        pl.when(pl.program_id(0) == nc - 1)(finish)

    outs, got = _hosted_call(
        body, [rider], 5, 3, 1, grid=(nc,),
        in_specs=[pl.BlockSpec((CHUNK, w), lambda n: (n, 0)), pl.BlockSpec((CHUNK, w), lambda n: (n, 1)),
                  pl.BlockSpec((CHUNK, w), lambda n: (n, 2)), pl.BlockSpec((CHUNK, LANES), lambda n: (n, 0)),
                  pl.BlockSpec((1, LANES, CHUNK), lambda n: (n, 0, 0))],
        out_specs=[pl.BlockSpec((CHUNK, w), lambda n: (n, 0)),
                   pl.BlockSpec((1, nh, HEAD_DIM, HEAD_DIM), lambda n: (n, 0, 0, 0)),
                   pl.BlockSpec((1, nh, CHUNK, CHUNK), lambda n: (n, 0, 0, 0))],
        out_shape=[jax.ShapeDtypeStruct((lp, w), F32), jax.ShapeDtypeStruct((nc, nh, HEAD_DIM, HEAD_DIM), F32),
                   jax.ShapeDtypeStruct((nc, nh, CHUNK, CHUNK), F32)],
        scratch_shapes=[pltpu.VMEM((nh, HEAD_DIM, HEAD_DIM), F32)],
        name="gdn_fwd", compiler_params=_cp("arbitrary"))(qkv, qkv, qkv, gates, gt3)
    return outs, (got[0] if got else None)


def _gdn_bwd(qkv, gates, gt3, s_all, t_all, do, nh, rider=None):
    lp = qkv.shape[0]
    nc = lp // CHUNK
    w = nh * HEAD_DIM
    rev = lambda n: nc - 1 - n

    def body(start, finish, q_ref, k_ref, v_ref, g_ref, gt_ref, s_ref, t_ref, do_ref, dq_ref, dk_ref, dv_ref, dg_ref, ds_scr):
        @pl.when(pl.program_id(0) == 0)
        def _():
            start()
            ds_scr[...] = jnp.zeros_like(ds_scr)
        q, k, v, beta, gc, gr = _gdn_chunk_inputs(q_ref, k_ref, v_ref, g_ref[...], gt_ref[0], nh)
        c = _gdn_chunk(q, k, v, beta, gc, gr, t_ref[0])
        s = s_ref[0]
        dsn = ds_scr[...]
        dout = _heads(do_ref, nh)
        v_new = c["u"] - _bdot3(c["w"], s)
        dq_dec = _bdot3(dout, s, "nt")
        daqk = jnp.where(c["causal"], _bdot3(dout, v_new, "nt"), 0.0)
        dv_new = _bdot3(c["aqk"], dout, "tn") + _bdot3(c["k_dec"], dsn)
        dk_dec = _bdot3(v_new, dsn, "nt")
        ddecay = jnp.sum(jnp.sum(dsn * s, axis=2, keepdims=True), axis=1, keepdims=True)
        dw = -_bdot3(dv_new, s, "nt")
        ds_scr[...] = _bdot3(c["q_dec"], dout, "tn") + c["decay"] * dsn - _bdot3(c["w"], dv_new, "tn")
        duw = jnp.concatenate([dv_new, dw], axis=2)
        dt = _bdot3(duw, jnp.concatenate([c["bv"], c["bk"]], axis=2), "nt")
        dbvk = _bdot3(c["t"], duw, "tn")
        dbv, dbk = dbvk[:, :, :HEAD_DIM], dbvk[:, :, HEAD_DIM:]
        da = jnp.where(c["strict"], -_bdot3(_bdot3(c["t"], dt, "tn"), c["t"], "nt"), 0.0)
        dkk = da * beta * c["dm"]
        dqk = daqk * c["dm"]
        e = da * c["a"] + daqk * c["aqk"]
        dq = dq_dec * c["eg"] + _bdot3(dqk, k)
        dk = (dk_dec * c["ekd"] + _bdot3(dkk, k) + _bdot3(dkk, k, "tn") + _bdot3(dqk, q, "tn")
              + (beta * c["eg"]) * dbk)
        dv = beta * dbv
        rs = lambda x: jnp.sum(x, axis=2, keepdims=True)
        dbeta = rs(dbv * v) + c["eg"] * rs(dbk * k) + rs(da * c["kk"] * c["dm"])
        kd_term = rs(dk_dec * c["k_dec"])
        eh, ew = _split(e)
        ones = jnp.ones((nh, CHUNK, LANES), BF16)
        col_sums = (_dot3(eh, ones, "tn") + _dot3(ew, ones, "tn"))[:, :, 0:1]
        dg_cum = rs(dq_dec * c["q_dec"]) - kd_term + rs(dbk * c["bk"]) + rs(e) - col_sums
        last = jnp.sum(kd_term, axis=1, keepdims=True) + ddecay * c["decay"]
        dg_cum = dg_cum + jnp.where(_iota((1, CHUNK, 1), 1) == CHUNK - 1, last, 0.0)
        lane = _iota((CHUNK, LANES), 1)
        acc = jnp.zeros((CHUNK, LANES), F32)
        for h in range(nh):
            sl = slice(h * HEAD_DIM, (h + 1) * HEAD_DIM)
            dq_ref[:, sl] = dq[h]
            dk_ref[:, sl] = dk[h]
            dv_ref[:, sl] = dv[h]
            acc = acc + jnp.where(lane == h, dbeta[h], 0.0) + jnp.where(lane == nh + h, dg_cum[h], 0.0)
        triu = (_iota((CHUNK, CHUNK), 0) <= _iota((CHUNK, CHUNK), 1)).astype(F32)
        dg_ref[...] = jnp.where(lane < nh, acc, _hdot(triu, acc))
        pl.when(pl.program_id(0) == nc - 1)(finish)

    outs, got = _hosted_call(
        body, [rider], 8, 4, 1, grid=(nc,),
        in_specs=[pl.BlockSpec((CHUNK, w), lambda n: (rev(n), 0)), pl.BlockSpec((CHUNK, w), lambda n: (rev(n), 1)),
                  pl.BlockSpec((CHUNK, w), lambda n: (rev(n), 2)), pl.BlockSpec((CHUNK, LANES), lambda n: (rev(n), 0)),
                  pl.BlockSpec((1, LANES, CHUNK), lambda n: (rev(n), 0, 0)),
                  pl.BlockSpec((1, nh, HEAD_DIM, HEAD_DIM), lambda n: (rev(n), 0, 0, 0)),
                  pl.BlockSpec((1, nh, CHUNK, CHUNK), lambda n: (rev(n), 0, 0, 0)),
                  pl.BlockSpec((CHUNK, w), lambda n: (rev(n), 0))],
        out_specs=[pl.BlockSpec((CHUNK, w), lambda n: (rev(n), 0))] * 3 + [pl.BlockSpec((CHUNK, LANES), lambda n: (rev(n), 0))],
        out_shape=[jax.ShapeDtypeStruct((lp, w), F32)] * 3 + [jax.ShapeDtypeStruct((lp, LANES), F32)],
        scratch_shapes=[pltpu.VMEM((nh, HEAD_DIM, HEAD_DIM), F32)],
        name="gdn_bwd", compiler_params=_cp("arbitrary"))(qkv, qkv, qkv, gates, gt3, s_all, t_all, do)
    return outs, (got[0] if got else None)


def _merge_gdn(o_gdn, proj, norm_w, nh):
    lp = o_gdn.shape[0]

    def body(o_ref, z_ref, w_ref, m_ref):
        o = o_ref[...]
        z = z_ref[...]
        m_ref[...] = (o * _rms(o) * w_ref[...] * (z * _sigmoid(z))).astype(BF16)

    return pl.pallas_call(
        body, grid=(nh,),
        in_specs=[pl.BlockSpec((lp, LANES), lambda s: (0, s)), pl.BlockSpec((lp, LANES), lambda s: (0, 3 * nh + s)),
                  pl.BlockSpec((1, LANES), lambda s: (0, 0))],
        out_specs=pl.BlockSpec((lp, LANES), lambda s: (0, s)),
        out_shape=jax.ShapeDtypeStruct((lp, 2 * nh * HEAD_DIM), BF16), name="merge_gdn",
        compiler_params=_cp("parallel"))(o_gdn, proj, norm_w)


def _merge_gdn_bwd(o_gdn, proj, norm_w, dmerged, nh):
    lp = o_gdn.shape[0]

    def body(o_ref, z_ref, w_ref, dm_ref, do_ref, dz_ref, dw_ref):
        o = o_ref[...]
        r = _rms(o)
        xh = o * r
        silu, dsilu = _silu_and_grad(z_ref[...])
        dm = dm_ref[...]
        dn = dm * silu
        dz_ref[...] = (dm * (xh * w_ref[...]) * dsilu).astype(BF16)
        dnw = dn * w_ref[...]
        do_ref[...] = r * (dnw - xh * jnp.mean(dnw * xh, axis=-1, keepdims=True))

        @pl.when(pl.program_id(0) == 0)
        def _():
            dw_ref[...] = jnp.zeros_like(dw_ref)
        dw_ref[...] += jnp.sum(dn * xh, axis=0, keepdims=True)

    w = nh * HEAD_DIM
    return pl.pallas_call(
        body, grid=(nh,),
        in_specs=[pl.BlockSpec((lp, LANES), lambda s: (0, s)), pl.BlockSpec((lp, LANES), lambda s: (0, 3 * nh + s)),
                  pl.BlockSpec((1, LANES), lambda s: (0, 0)), pl.BlockSpec((lp, LANES), lambda s: (0, s))],
        out_specs=[pl.BlockSpec((lp, LANES), lambda s: (0, s)), pl.BlockSpec((lp, LANES), lambda s: (0, 3 * nh + s)),
                   pl.BlockSpec((1, LANES), lambda s: (0, 0))],
        out_shape=[jax.ShapeDtypeStruct((lp, w), F32), jax.ShapeDtypeStruct((lp, 8 * w + NARROW), BF16),
                   jax.ShapeDtypeStruct((1, LANES), F32)],
        name="merge_gdn_bwd", compiler_params=_cp("arbitrary"))(o_gdn, proj, norm_w, dmerged)


def _fox_prep(proj, qk_w, nh):
    lp = proj.shape[0]

    def body(x_ref, w_ref, o_ref):
        x = x_ref[...]
        o_ref[...] = x * _rms(x) * w_ref[0]

    return pl.pallas_call(
        body, grid=(2 * nh,),
        in_specs=[pl.BlockSpec((lp, LANES), lambda s: (0, 4 * nh + s)), pl.BlockSpec((1, 1, LANES), lambda s: (s // nh, 0, 0))],
        out_specs=pl.BlockSpec((lp, LANES), lambda s: (0, s)),
        out_shape=jax.ShapeDtypeStruct((lp, 2 * nh * HEAD_DIM), F32), name="fox_prep",
        compiler_params=_cp("parallel"))(proj, qk_w)


def _fox_prep_bwd(proj, qk_w, dq, dk, dproj, nh):
    lp = proj.shape[0]
    part = lambda p: pl.BlockSpec((lp, LANES), lambda s: (0, jnp.clip(s - p * nh, 0, nh - 1)))

    def body(x_ref, w_ref, dq_ref, dk_ref, _, dx_ref, dw_ref):
        x = x_ref[...]
        r = _rms(x)
        xh = x * r
        dy = jnp.where(pl.program_id(0) < nh, dq_ref[...], dk_ref[...])
        dyw = dy * w_ref[0]
        dx_ref[...] = (r * (dyw - xh * jnp.mean(dyw * xh, axis=-1, keepdims=True))).astype(BF16)

        @pl.when(pl.program_id(0) % nh == 0)
        def _():
            dw_ref[...] = jnp.zeros_like(dw_ref)
        dw_ref[0] += jnp.sum(dy * xh, axis=0, keepdims=True)

    strip = pl.BlockSpec((lp, LANES), lambda s: (0, 4 * nh + s))
    wsp = pl.BlockSpec((1, 1, LANES), lambda s: (s // nh, 0, 0))
    return pl.pallas_call(
        body, grid=(2 * nh,), in_specs=[strip, wsp, part(0), part(1), _ANY], out_specs=[strip, wsp],
        out_shape=[jax.ShapeDtypeStruct(dproj.shape, BF16), jax.ShapeDtypeStruct((2, 1, LANES), F32)],
        input_output_aliases={4: 0}, name="fox_prep_bwd", compiler_params=_cp("arbitrary"))(proj, qk_w, dq, dk, dproj)


def _fox_probs(q, k, gates, crow, h, i, nh, lse=None):
    kl = k.shape[0]
    lane = _iota((Q_BLOCK, LANES), 1)
    ct = jnp.sum(jnp.where(lane == 4 * nh + h, gates, 0.0), axis=1, keepdims=True)
    tq, kq = _iota((Q_BLOCK, Q_BLOCK), 0), _iota((Q_BLOCK, Q_BLOCK), 1)
    qs = q * (HEAD_DIM ** -0.5)
    if i == 0:
        s = _bdot(qs, k, "nt") + (ct - crow)
        s = jnp.where((kq <= tq) & ((kq >= PAD_ROWS) | (tq < PAD_ROWS)), s, NEG)
    else:
        crow = jnp.where(_iota((1, kl), 1) < PAD_ROWS, -NEG, crow)
        s = _bdot(qs, k, "nt") + (ct - crow)
        s = jnp.concatenate([s[:, :kl - Q_BLOCK], jnp.where(kq <= tq, s[:, kl - Q_BLOCK:], NEG)], axis=1)
    if lse is not None:
        return jnp.exp(s - lse)
    m = jnp.max(s, axis=1, keepdims=True)
    p = jnp.exp(s - m)
    tot = jnp.sum(p, axis=1, keepdims=True)
    return p / tot, m + jnp.log(tot)


FOX_HEADS_PER_STEP = 2


def _fox_specs(lp, nh):
    hw = FOX_HEADS_PER_STEP * LANES
    return [pl.BlockSpec((Q_BLOCK, hw), lambda g, i: (i, g)),
            pl.BlockSpec((lp, hw), lambda g, i: (0, nh // FOX_HEADS_PER_STEP + g)),
            pl.BlockSpec((lp, hw), lambda g, i: (0, 6 * nh // FOX_HEADS_PER_STEP + g)),
            pl.BlockSpec((Q_BLOCK, LANES), lambda g, i: (i, 0)),
            pl.BlockSpec((LANES, lp), lambda g, i: (0, 0))]


def _fox_fwd(qkn, proj, gates, gtf, nh):
    lp = qkn.shape[0]

    def body(q_ref, k_ref, v_ref, g_ref, gt_ref, o_ref, lse_ref):
        g, i = pl.program_id(0), pl.program_id(1)
        for j in range(lp // Q_BLOCK):
            @pl.when(i == j)
            def _(j=j):
                kl = (j + 1) * Q_BLOCK
                for hh in range(FOX_HEADS_PER_STEP):
                    h = FOX_HEADS_PER_STEP * g + hh
                    sl = slice(hh * LANES, (hh + 1) * LANES)
                    p, lse = _fox_probs(q_ref[:, sl], k_ref[0:kl, sl], g_ref[...], gt_ref[pl.ds(4 * nh + h, 1), :][:, 0:kl],
                                        h, j, nh)
                    o_ref[:, sl] = _bdot(p, v_ref[0:kl, sl])
                    lse_ref[:, sl] = jnp.broadcast_to(lse, (Q_BLOCK, LANES))

    blk = pl.BlockSpec((Q_BLOCK, FOX_HEADS_PER_STEP * LANES), lambda g, i: (i, g))
    return pl.pallas_call(
        body, grid=(nh // FOX_HEADS_PER_STEP, lp // Q_BLOCK), in_specs=_fox_specs(lp, nh), out_specs=[blk, blk],
        out_shape=[jax.ShapeDtypeStruct((lp, nh * HEAD_DIM), F32)] * 2, name="fox_fwd",
        compiler_params=_cp("parallel", "parallel"))(qkn, qkn, proj, gates, gtf)


def _fox_bwd(qkn, proj, gates, gtf, lse, do, dproj, nh):
    lp = qkn.shape[0]
    nq = lp // Q_BLOCK
    w = nh * HEAD_DIM
    scale = HEAD_DIM ** -0.5

    def body(q_ref, k_ref, v_ref, g_ref, gt_ref, lse_ref, do_ref, _, dq_ref, dk_ref, dc_ref, dv_ref, dv_scr):
        g, i = pl.program_id(0), pl.program_id(1)

        @pl.when(i == 0)
        def _():
            dk_ref[...] = jnp.zeros_like(dk_ref)
            dv_scr[...] = jnp.zeros_like(dv_scr)
            dc_ref[...] = jnp.zeros_like(dc_ref)
        for j in range(nq):
            @pl.when(i == j)
            def _(j=j):
                kl = (j + 1) * Q_BLOCK
                for hh in range(FOX_HEADS_PER_STEP):
                    h = FOX_HEADS_PER_STEP * g + hh
                    sl = slice(hh * LANES, (hh + 1) * LANES)
                    q, k = q_ref[:, sl], k_ref[0:kl, sl]
                    p = _fox_probs(q, k, g_ref[...], gt_ref[pl.ds(4 * nh + h, 1), :][:, 0:kl], h, j, nh,
                                   lse_ref[:, sl][:, 0:1])
                    dout = do_ref[:, sl]
                    dp = _bdot(dout, v_ref[0:kl, sl], "nt")
                    ds = p * (dp - jnp.sum(p * dp, axis=1, keepdims=True))
                    dq_ref[:, sl] = _bdot(ds, k) * scale
                    dk_ref[0:kl, sl] += _bdot(ds, q * scale, "tn")
                    dv_scr[0:kl, sl] += _bdot(p, dout, "tn")
                    dc_ref[hh, :, 0:kl] -= jnp.sum(ds, axis=0, keepdims=True)

        @pl.when(i == nq - 1)
        def _():
            dv_ref[...] = dv_scr[...].astype(BF16)

    hw = FOX_HEADS_PER_STEP * LANES
    blk = pl.BlockSpec((Q_BLOCK, hw), lambda g, i: (i, g))
    col = pl.BlockSpec((lp, hw), lambda g, i: (0, g))
    return pl.pallas_call(
        body, grid=(nh // FOX_HEADS_PER_STEP, nq), in_specs=_fox_specs(lp, nh) + [blk, blk, _ANY],
        out_specs=[blk, col, pl.BlockSpec((FOX_HEADS_PER_STEP, 1, lp), lambda g, i: (g, 0, 0)),
                   pl.BlockSpec((lp, hw), lambda g, i: (0, 6 * nh // FOX_HEADS_PER_STEP + g))],
        out_shape=[jax.ShapeDtypeStruct((lp, w), F32)] * 2 + [jax.ShapeDtypeStruct((nh, 1, lp), F32),
                                                             jax.ShapeDtypeStruct(dproj.shape, BF16)],
        scratch_shapes=[pltpu.VMEM((lp, hw), F32)], input_output_aliases={7: 3},
        name="fox_bwd", compiler_params=_cp("parallel", "arbitrary"))(qkn, qkn, proj, gates, gtf, lse, do, dproj)


def _merge_fox(o_fox, proj, merged, nh):
    lp = o_fox.shape[0]

    def body(o_ref, z_ref, _, m_ref):
        z = z_ref[...]
        m_ref[...] = (o_ref[...] * (z * _sigmoid(z))).astype(BF16)

    return pl.pallas_call(
        body, grid=(nh,),
        in_specs=[pl.BlockSpec((lp, LANES), lambda s: (0, s)), pl.BlockSpec((lp, LANES), lambda s: (0, 7 * nh + s)), _ANY],
        out_specs=pl.BlockSpec((lp, LANES), lambda s: (0, nh + s)),
        out_shape=jax.ShapeDtypeStruct(merged.shape, BF16), input_output_aliases={2: 0}, name="merge_fox",
        compiler_params=_cp("parallel"))(o_fox, proj, merged)


def _merge_fox_bwd(o_fox, proj, dmerged, dproj, nh):
    lp = o_fox.shape[0]

    def body(o_ref, z_ref, dm_ref, _, do_ref, dz_ref):
        silu, dsilu = _silu_and_grad(z_ref[...])
        dm = dm_ref[...]
        do_ref[...] = dm * silu
        dz_ref[...] = (dm * o_ref[...] * dsilu).astype(BF16)

    w = nh * HEAD_DIM
    return pl.pallas_call(
        body, grid=(nh,),
        in_specs=[pl.BlockSpec((lp, LANES), lambda s: (0, s)), pl.BlockSpec((lp, LANES), lambda s: (0, 7 * nh + s)),
                  pl.BlockSpec((lp, LANES), lambda s: (0, nh + s)), _ANY],
        out_specs=[pl.BlockSpec((lp, LANES), lambda s: (0, s)), pl.BlockSpec((lp, LANES), lambda s: (0, 7 * nh + s))],
        out_shape=[jax.ShapeDtypeStruct((lp, w), F32), jax.ShapeDtypeStruct(dproj.shape, BF16)],
        input_output_aliases={3: 1}, name="merge_fox_bwd", compiler_params=_cp("parallel"))(o_fox, proj, dmerged, dproj)


def _post(out, x, target, post_w):
    lp, d = out.shape

    def body(o_ref, x_ref, t_ref, w_ref, dy_ref, do_ref, loss_ref, dw_ref):
        i = pl.program_id(0)

        @pl.when(i == 0)
        def _():
            loss_ref[...] = jnp.zeros_like(loss_ref)
            dw_ref[...] = jnp.zeros_like(dw_ref)
        o = o_ref[...]
        r = _rms(o)
        nrm = o * r
        err = jnp.where(i > 0, x_ref[...] + nrm * w_ref[...] - t_ref[...], 0.0)
        loss_ref[0:1, :] += 0.5 * jnp.sum(jnp.sum(err * err, axis=1, keepdims=True), axis=0, keepdims=True) / d
        dy = err / d
        dy_ref[...] = dy
        dw_ref[...] += jnp.sum(dy * nrm, axis=0, keepdims=True)
        dyw = dy * w_ref[...]
        do_ref[...] = (r * (dyw - nrm * jnp.mean(dyw * nrm, axis=-1, keepdims=True))).astype(BF16)

    row = pl.BlockSpec((Q_BLOCK, d), lambda i: (i, 0))
    vec = pl.BlockSpec((1, d), lambda i: (0, 0))
    return pl.pallas_call(
        body, grid=(lp // Q_BLOCK,), in_specs=[row, _x_rows(d), _x_rows(d), vec],
        out_specs=[_x_rows(d), row, pl.BlockSpec((8, LANES), lambda i: (0, 0)), vec],
        out_shape=[jax.ShapeDtypeStruct(x.shape, F32), jax.ShapeDtypeStruct((lp, d), BF16),
                   jax.ShapeDtypeStruct((8, LANES), F32), jax.ShapeDtypeStruct((1, d), F32)],
        name="post", compiler_params=_cp("arbitrary"))(out, x, target, post_w)


def _prenorm_bwd(dxn, x, meta, w, dy, rider=None):
    seq, d = x.shape
    lp = seq + Q_BLOCK

    def body(start, finish, dx_ref, x_ref, m_ref, w_ref, dy_ref, gx_ref, gm_ref, dw_ref):
        i = pl.program_id(0)
        pl.when(i == 0)(start)
        h = _h_tile(i, x_ref, m_ref)
        r = _rms(h)
        xh = h * r
        dxn_ = dx_ref[...]
        dxw = dxn_ * w_ref[...]
        dh = jnp.where(i > 0, dy_ref[...], 0.0) + r * (dxw - xh * jnp.mean(dxw * xh, axis=-1, keepdims=True))
        gx_ref[...] = dh

        @pl.when(i == 0)
        def _():
            dw_ref[...] = jnp.zeros_like(dw_ref)
            gm_ref[...] = dh[PAD_ROWS:, :]
        dw_ref[...] += jnp.sum(dxn_ * xh, axis=0, keepdims=True)
        pl.when(i == lp // Q_BLOCK - 1)(finish)

    vec = pl.BlockSpec((1, d), lambda i: (0, 0))
    met = pl.BlockSpec((N_META, d), lambda i: (0, 0))
    outs, got = _hosted_call(
        body, [rider], 5, 3, 0, grid=(lp // Q_BLOCK,),
        in_specs=[pl.BlockSpec((Q_BLOCK, d), lambda i: (i, 0)), _x_rows(d), met, vec, _x_rows(d)],
        out_specs=[_x_rows(d), met, vec],
        out_shape=[jax.ShapeDtypeStruct((seq, d), F32), jax.ShapeDtypeStruct((N_META, d), F32),
                   jax.ShapeDtypeStruct((1, d), F32)],
        name="prenorm_bwd", compiler_params=_cp("arbitrary"))(dxn, x, meta, w, dy)
    return outs, (got[0] if got else None)


def _layer_grads(x, target, meta, pre_w, wfull, conv_wt, a_log, dt_bias, gdn_norm_w, fq_w, fk_w, f_bias, w_out, post_w,
                 late_weights=None, w_out_grads=None):
    nh = a_log.shape[1]
    zpad = jnp.zeros((1, LANES - 3 * nh), F32)
    bias_row = jnp.concatenate([jnp.zeros((1, nh), F32), dt_bias, f_bias, zpad], axis=1)
    nega_row = jnp.concatenate([jnp.zeros((1, nh), F32), -jnp.exp(a_log), jnp.zeros((1, nh), F32), zpad], axis=1)
    qk_w = jnp.stack([fq_w, fk_w])

    xn = _prenorm(x, meta, pre_w)
    proj = _matmul(xn, wfull, "nn", MM_TILE, F32, "proj")
    qkv = _gdn_prep(proj, conv_wt, nh)
    gates, gt3, gtf = _gates(proj, bias_row, nega_row, nh)
    (o_gdn, s_all, t_all), got = _gdn_fwd(qkv, gates, gt3, nh, None if late_weights is None else late_weights[0])
    if late_weights is not None:
        w_out = late_weights[1](got)
    qkn = _fox_prep(proj, qk_w, nh)
    o_fox, fox_lse = _fox_fwd(qkn, proj, gates, gtf, nh)
    merged = _merge_fox(o_fox, proj, _merge_gdn(o_gdn, proj, gdn_norm_w, nh), nh)
    out = _matmul(merged, w_out, "nn", 4 * LANES, F32, "out_proj")
    dy, dout, loss_blk, dpost_w = _post(out, x, target, post_w)

    dw_out = _matmul(merged, dout, "tn", 4 * LANES, BF16, "dw_out")
    if w_out_grads is None:
        dmerged, gdn_rider = _matmul(dout, w_out, "nt", 4 * LANES, F32, "dmerged"), None
    else:
        dmerged, got = _matmul(dout, w_out, "nt", 4 * LANES, F32, "dmerged", w_out_grads[0](dw_out))
        gdn_rider = w_out_grads[1](dw_out, got)
    do_gdn, dproj, dgdn_norm_w = _merge_gdn_bwd(o_gdn, proj, gdn_norm_w, dmerged, nh)
    do_fox, dproj = _merge_fox_bwd(o_fox, proj, dmerged, dproj, nh)
    dqn, dkn, dc_t, dproj = _fox_bwd(qkn, proj, gates, gtf, fox_lse, do_fox, dproj, nh)
    dproj, dqk_w = _fox_prep_bwd(proj, qk_w, dqn, dkn, dproj, nh)
    (dgq, dgk, dgv, dgate), w_out_parts = _gdn_bwd(qkv, gates, gt3, s_all, t_all, do_gdn, nh, gdn_rider)
    dproj, dconv_wt = _gdn_prep_bwd(proj, conv_wt, dgq, dgk, dgv, dproj, nh)
    dc_rows = jnp.pad(dc_t.reshape(nh, -1), ((2 * nh, LANES - 3 * nh), (0, 0)))
    dproj, gate_sums = _gates_bwd(proj, bias_row, nega_row, gates, dgate, dc_rows, dproj, nh)
    return dict(
        loss=loss_blk[0:1, 0:1], dy=dy, xn=xn, dproj=dproj, post_w=dpost_w,
        conv_wt=dconv_wt, a_log=gate_sums[1:2, nh:2 * nh], dt_bias=gate_sums[0:1, nh:2 * nh],
        gdn_norm_w=dgdn_norm_w, fq_w=dqk_w[0], fk_w=dqk_w[1], f_bias=gate_sums[0:1, 2 * nh:3 * nh], w_out=dw_out,
        w_out_parts=w_out_parts)


def _cast_bf16(a, tr, name):
    r, c = a.shape

    def body(a_ref, o_ref):
        o_ref[...] = a_ref[...].astype(BF16)

    return pl.pallas_call(
        body, grid=(r // tr,), in_specs=[pl.BlockSpec((tr, c), lambda i: (i, 0))],
        out_specs=pl.BlockSpec((tr, c), lambda i: (i, 0)), out_shape=jax.ShapeDtypeStruct((r, c), BF16),
        name=name, compiler_params=_cp("parallel"))(a)


def _column_major(a):
    return jnp.transpose(a, (2, 0, 1))


def _cast_bf16_column_major(a3, name):
    _, r, c = a3.shape

    def body(a_ref, o_ref):
        o_ref[...] = a_ref[...].reshape(LANES, r).T.astype(BF16)

    return pl.pallas_call(
        body, grid=(pl.cdiv(c, LANES),), in_specs=[pl.BlockSpec((LANES, 1, r), lambda i: (i, 0, 0))],
        out_specs=pl.BlockSpec((r, LANES), lambda i: (0, i)), out_shape=jax.ShapeDtypeStruct((r, c), BF16),
        name=name, compiler_params=_cp("parallel"))(_column_major(a3))


def _adamw_column_major(w3, parts, m3, v3, name):
    _, r, c = w3.shape
    n_parts = parts.shape[0]

    def body(w_ref, p_ref, m_ref, v_ref, g_ref, d_ref, nm_ref, nv_ref):
        g = p_ref[0].astype(F32)
        for s in range(1, n_parts):
            g = g + p_ref[s].astype(F32)
        g = g.T
        flat = lambda ref: ref[...].reshape(LANES, r)
        m_new = ADAM_B1 * flat(m_ref) + (1.0 - ADAM_B1) * g
        v_new = ADAM_B2 * flat(v_ref) + (1.0 - ADAM_B2) * (g * g)
        m_hat = m_new / (1.0 - ADAM_B1 ** ADAM_STEP)
        v_hat = v_new / (1.0 - ADAM_B2 ** ADAM_STEP)
        delta = -ADAM_LR * (m_hat / (jnp.sqrt(v_hat) + ADAM_EPS) + ADAM_WD * flat(w_ref))
        for ref, val in ((g_ref, g), (d_ref, delta), (nm_ref, m_new), (nv_ref, v_new)):
            ref[...] = val.reshape(LANES, 1, r)

    blk = pl.BlockSpec((LANES, 1, r), lambda i: (i, 0, 0))
    outs = pl.pallas_call(
        body, grid=(pl.cdiv(c, LANES),), in_specs=[blk, pl.BlockSpec((n_parts, r, LANES), lambda i: (0, 0, i)), blk, blk],
        out_specs=[blk] * 4, out_shape=[jax.ShapeDtypeStruct((c, 1, r), F32)] * 4, name=name,
        compiler_params=_cp("parallel"))(_column_major(w3), parts, _column_major(m3), _column_major(v3))
    return [jnp.transpose(o, (1, 2, 0)) for o in outs]


def _gather_copies(ins, outs, send_sems, recv_sems, local_sems):
    n = len(ins)
    x, y, c = lax.axis_index("x"), lax.axis_index("y"), lax.axis_index("c")
    me, sibling = (x, y, c), (x, y, 1 - c)
    xn, yn, dg = (1 - x, y), (x, 1 - y), (1 - x, 1 - y)

    def copy(a, k, block, to, src=None):
        px, py, pc = block
        rows = outs[a].at[4 * px + 2 * py + pc]
        return pltpu.make_async_remote_copy(
            src_ref=rows if src is None else src, dst_ref=rows, send_sem=send_sems.at[a, k],
            recv_sem=recv_sems.at[a, k], device_id=to, device_id_type=_MESH)

    local = [pltpu.make_async_copy(ins[a], outs[a].at[4 * x + 2 * y + c], local_sems.at[a]) for a in range(n)]
    own = [cp for a in range(n) for cp in (copy(a, 0, me, sibling, src=ins[a]), copy(a, 1, me, (*xn, c), src=ins[a]),
                                           copy(a, 2, me, (*yn, c), src=ins[a]))]

    def start():
        for cp in local + own:
            cp.start()

    def finish():
        for a in range(n):
            @pl.when(c == 1)
            def _(a=a):
                copy(a, 1, (*xn, c), me).wait_recv()
                copy(a, 3, (*xn, c), (*yn, c)).start()

            @pl.when(c == 0)
            def _(a=a):
                copy(a, 2, (*yn, c), me).wait_recv()
                copy(a, 3, (*yn, c), (*xn, c)).start()
        for a in range(n):
            pl.when(c == 0)(copy(a, 1, (*xn, c), me).wait_recv)
            copy(a, 4, (*xn, c), sibling).start()
            pl.when(c == 1)(copy(a, 2, (*yn, c), me).wait_recv)
            copy(a, 5, (*yn, c), sibling).start()
        for a in range(n):
            copy(a, 3, (*dg, c), me).wait_recv()
            copy(a, 6, (*dg, c), sibling).start()
        for a in range(n):
            copy(a, 0, sibling, me).wait_recv()
            for k, chip in ((4, xn), (5, yn), (6, dg)):
                copy(a, k, (*chip, 1 - c), me).wait_recv()
                copy(a, k, (*chip, c), sibling).wait_send()
            copy(a, 3, (*xn, c), (*yn, c)).wait_send()
        for cp in own:
            cp.wait_send()
        for cp in local:
            cp.wait()

    return start, finish


def _gather_scratch(n):
    return [pltpu.SemaphoreType.DMA((n, N_DEV - 1)), pltpu.SemaphoreType.DMA((n, N_DEV - 1)), pltpu.SemaphoreType.DMA((n,))]


def _gather_rider(arrays):
    return _Rider(list(arrays), [jax.ShapeDtypeStruct((N_DEV,) + a.shape, a.dtype) for a in arrays],
                  _gather_scratch(len(arrays)), {}, lambda ins, outs, scratch: _gather_copies(ins, outs, *scratch))


def _all_gather(arrays, name):
    n = len(arrays)

    def body(*refs):
        start, finish = _gather_copies(refs[:n], refs[n:2 * n], *refs[2 * n:])
        start()
        finish()

    return pl.pallas_call(
        body, in_specs=[_ANY] * n, out_specs=[_ANY] * n,
        out_shape=[jax.ShapeDtypeStruct((N_DEV,) + a.shape, a.dtype) for a in arrays],
        scratch_shapes=_gather_scratch(n), name=name)(*arrays)


SLAB = 10 * LANES


def _slab_start(blk, nh, cols):
    in_second_half = blk >= N_DEV // 2
    shift = (2 * nh if in_second_half else 0) if isinstance(blk, int) else jnp.where(in_second_half, 2 * nh, 0)
    return (blk * cols - shift) // LANES * LANES


def _pair_rider(dw_rows=None, parts=None, nh=None):
    if dw_rows is not None:
        r, full = dw_rows.shape
        cols = (full - NARROW + 3 * nh) // N_DEV
        out_shapes = [jax.ShapeDtypeStruct((N_CHIP, r, SLAB), dw_rows.dtype), jax.ShapeDtypeStruct((r, NARROW), dw_rows.dtype)]
    else:
        out_shapes = [jax.ShapeDtypeStruct((N_CHIP,) + parts.shape[1:], parts.dtype)]

    def make(ins, outs, scratch):
        send_sems, recv_sems = scratch
        x, y, c = lax.axis_index("x"), lax.axis_index("y"), lax.axis_index("c")
        kw = lambda k: dict(send_sem=send_sems.at[k], recv_sem=recv_sems.at[k], device_id=(x, y, 1 - c), device_id_type=_MESH)
        copies = []
        for q in range(N_CHIP):
            if dw_rows is not None:
                first = pl.multiple_of(_slab_start(2 * q + 1 - c, nh, cols), LANES)
                copies.append(pltpu.make_async_remote_copy(src_ref=ins[0].at[:, pl.ds(first, SLAB)], dst_ref=outs[0].at[q], **kw(q)))
            else:
                copies.append(pltpu.make_async_remote_copy(src_ref=ins[0].at[2 * q + 1 - c], dst_ref=outs[0].at[q], **kw(q)))
        if dw_rows is not None:
            copies.append(pltpu.make_async_remote_copy(src_ref=ins[0].at[:, pl.ds(full - NARROW, NARROW)], dst_ref=outs[1],
                                                       **kw(N_CHIP)))

        def start():
            for cp in copies:
                cp.start()

        def finish():
            for cp in copies:
                cp.wait()

        return start, finish

    return _Rider([dw_rows if dw_rows is not None else parts], out_shapes,
                  [pltpu.SemaphoreType.DMA((N_CHIP + 1,)), pltpu.SemaphoreType.DMA((N_CHIP + 1,))], {}, make)


def _relayout_pair_sum(dwfull, got_slabs, got_tail, core, nh, tr, name):
    d, full = dwfull.shape
    w = nh * HEAD_DIM
    cols = (8 * w + 3 * nh) // N_DEV
    segs = _native_segments(nh)

    def block(f_ref, s_ref, t_ref, q, blk):
        st = _slab_start(blk, nh, cols)
        wide = f_ref[:, st:st + SLAB].astype(F32) + s_ref[q].astype(F32)
        tail = f_ref[:, 8 * w:].astype(F32) + t_ref[...].astype(F32)
        pieces = []
        for s0, s1, t0 in segs:
            lo, hi = max(s0, blk * cols), min(s1, (blk + 1) * cols)
            if lo < hi:
                at = t0 + lo - s0
                pieces.append(tail[:, at - 8 * w:at - 8 * w + hi - lo] if at >= 8 * w else wide[:, at - st:at - st + hi - lo])
        return (pieces[0] if len(pieces) == 1 else jnp.concatenate(pieces, axis=1)).astype(dwfull.dtype)

    def body(core_ref, f_ref, s_ref, t_ref, o_ref):
        for parity in range(2):
            @pl.when(core_ref[0] == parity)
            def _(parity=parity):
                for q in range(N_CHIP):
                    o_ref[q] = block(f_ref, s_ref, t_ref, q, 2 * q + parity)

    return pl.pallas_call(
        body,
        grid_spec=pltpu.PrefetchScalarGridSpec(
            num_scalar_prefetch=1, grid=(d // tr,),
            in_specs=[pl.BlockSpec((tr, full), lambda i, c_ref: (i, 0)), pl.BlockSpec((N_CHIP, tr, SLAB), lambda i, c_ref: (0, i, 0)),
                      pl.BlockSpec((tr, NARROW), lambda i, c_ref: (i, 0))],
            out_specs=pl.BlockSpec((N_CHIP, tr, cols), lambda i, c_ref: (0, i, 0))),
        out_shape=jax.ShapeDtypeStruct((N_CHIP, d, cols), dwfull.dtype), name=name,
        compiler_params=_cp("parallel"))(core, dwfull, got_slabs, got_tail)


def _pair_sum(parts, got, core, tr, name):
    _, r, c = parts.shape

    def body(core_ref, p_ref, g_ref, o_ref):
        o_ref[...] = (p_ref[...].astype(F32) + g_ref[...].astype(F32)).astype(o_ref.dtype)

    return pl.pallas_call(
        body,
        grid_spec=pltpu.PrefetchScalarGridSpec(
            num_scalar_prefetch=1, grid=(N_CHIP, r // tr),
            in_specs=[pl.BlockSpec((1, tr, c), lambda q, i, core_ref: (2 * q + core_ref[0], i, 0)),
                      pl.BlockSpec((1, tr, c), lambda q, i, core_ref: (q, i, 0))],
            out_specs=pl.BlockSpec((1, tr, c), lambda q, i, core_ref: (q, i, 0))),
        out_shape=jax.ShapeDtypeStruct((N_CHIP, r, c), parts.dtype), name=name,
        compiler_params=_cp("parallel", "parallel"))(core, parts, got)


def _native_segments(nh):
    w = nh * HEAD_DIM
    return [(0, 4 * w, 0), (4 * w, 4 * w + 2 * nh, 8 * w), (4 * w + 2 * nh, 8 * w + 2 * nh, 4 * w),
            (8 * w + 2 * nh, 8 * w + 3 * nh, 8 * w + 2 * nh)]


def _relayout_w_in(wg, nh, tr):
    _, d, cols = wg.shape
    w = nh * HEAD_DIM

    def native(ref, j0, j1):
        out = []
        while j0 < j1:
            blk = j0 // cols
            end = min(j1, (blk + 1) * cols)
            out.append(ref[blk, :, pl.ds(j0 - blk * cols, end - j0)])
            j0 = end
        return out

    def body(g_ref, o_ref):
        for cidx in range(8 * w // LANES):
            j0 = cidx * LANES + (0 if cidx * LANES < 4 * w else 2 * nh)
            pieces = native(g_ref, j0, j0 + LANES)
            o_ref[:, cidx * LANES:(cidx + 1) * LANES] = pieces[0] if len(pieces) == 1 else jnp.concatenate(pieces, axis=1)
        pieces = (native(g_ref, 4 * w, 4 * w + 2 * nh) + native(g_ref, 8 * w + 2 * nh, 8 * w + 3 * nh)
                  + [jnp.zeros((tr, NARROW - 3 * nh), wg.dtype)])
        o_ref[:, 8 * w:] = jnp.concatenate(pieces, axis=1)

    return pl.pallas_call(
        body, grid=(d // tr,), in_specs=[pl.BlockSpec((N_DEV, tr, cols), lambda i: (0, i, 0))],
        out_specs=pl.BlockSpec((tr, 8 * w + NARROW), lambda i: (i, 0)),
        out_shape=jax.ShapeDtypeStruct((d, 8 * w + NARROW), wg.dtype),
        name="relayout_w_in", compiler_params=_cp("parallel"))(wg)


def _adamw(w, parts, m, v, tr, name):
    r, c = w.shape
    n_parts = parts.shape[0]

    def body(w_ref, p_ref, m_ref, v_ref, g_ref, d_ref, nm_ref, nv_ref):
        g = p_ref[0].astype(F32)
        for s in range(1, n_parts):
            g = g + p_ref[s].astype(F32)
        m_new = ADAM_B1 * m_ref[...] + (1.0 - ADAM_B1) * g
        v_new = ADAM_B2 * v_ref[...] + (1.0 - ADAM_B2) * (g * g)
        m_hat = m_new / (1.0 - ADAM_B1 ** ADAM_STEP)
        v_hat = v_new / (1.0 - ADAM_B2 ** ADAM_STEP)
        g_ref[...] = g
        d_ref[...] = -ADAM_LR * (m_hat / (jnp.sqrt(v_hat) + ADAM_EPS) + ADAM_WD * w_ref[...])
        nm_ref[...] = m_new
        nv_ref[...] = v_new

    blk = pl.BlockSpec((tr, c), lambda i: (i, 0))
    return pl.pallas_call(
        body, grid=(r // tr,), in_specs=[blk, pl.BlockSpec((n_parts, tr, c), lambda i: (0, i, 0)), blk, blk],
        out_specs=[blk] * 4, out_shape=[jax.ShapeDtypeStruct((r, c), F32)] * 4, name=name,
        compiler_params=_cp("parallel"))(w, parts, m, v)


def _pack_small(d, pre, post, a_log, dt_bias, f_bias, gdn_w, fq_w, fk_w, extra):
    row2 = jnp.concatenate([a_log, dt_bias, f_bias, gdn_w, fq_w, fk_w, extra], axis=1)
    row2 = jnp.pad(row2, ((0, 0), (0, d - row2.shape[1])))
    return jnp.concatenate([pre, post, row2, jnp.zeros((5, d), F32)], axis=0)


def _unpack_small(p, nh):
    o = 3 * nh
    return dict(pre=p[0:1], post=p[1:2], a_log=p[2:3, 0:nh], dt_bias=p[2:3, nh:2 * nh], f_bias=p[2:3, 2 * nh:o],
                gdn_w=p[2:3, o:o + HEAD_DIM], fq_w=p[2:3, o + HEAD_DIM:o + 2 * HEAD_DIM],
                fk_w=p[2:3, o + 2 * HEAD_DIM:o + 3 * HEAD_DIM], extra=p[2, o + 3 * HEAD_DIM])


def kernel(x, meta_tokens, pre_norm_w, w_in, conv_w, a_log, dt_bias, gdn_norm_w, fox_q_norm_w, fox_k_norm_w, fox_f_bias, w_out, post_norm_w, loss_target, m_meta_tokens, m_pre_norm_w, m_w_in, m_conv_w, m_a_log, m_dt_bias, m_gdn_norm_w, m_fox_q_norm_w, m_fox_k_norm_w, m_fox_f_bias, m_w_out, m_post_norm_w, v_meta_tokens, v_pre_norm_w, v_w_in, v_conv_w, v_a_log, v_dt_bias, v_gdn_norm_w, v_fox_q_norm_w, v_fox_k_norm_w, v_fox_f_bias, v_w_out, v_post_norm_w):
    nh = a_log.shape[1]
    d = x.shape[-1]
    w = nh * HEAD_DIM
    zero = jnp.zeros((1, 1), F32)

    wg, cg, mg = _all_gather([_cast_bf16_column_major(w_in, "cast_w_in"), conv_w[0].T, meta_tokens], "gather_weights")
    wfull = _relayout_w_in(wg, nh, 256)
    conv_wt = cg.transpose(1, 0, 2).reshape(CONV_WIDTH, 3 * w)
    meta_full = mg.transpose(1, 0, 2).reshape(N_META, d)
    late_weights = (_gather_rider([_cast_bf16(w_out[0], 256, "cast_w_out")]), lambda got: got[0].reshape(2 * w, d))
    core = lax.axis_index("c")
    dev = 4 * lax.axis_index("x") + 2 * lax.axis_index("y") + core
    core_arr = jnp.reshape(core, (1,)).astype(jnp.int32)

    out_parts = lambda dw_out: dw_out.reshape(N_DEV, 2 * w // N_DEV, d)
    g = _layer_grads(
        x[0], loss_target[0], meta_full, pre_norm_w, wfull, conv_wt, a_log, dt_bias, gdn_norm_w,
        fox_q_norm_w, fox_k_norm_w, fox_f_bias, None, post_norm_w, late_weights=late_weights,
        w_out_grads=(lambda dw_out: _pair_rider(parts=out_parts(dw_out)),
                     lambda dw_out, got: _chip_rider(_pair_sum(out_parts(dw_out), got[0], core_arr, 256, "pair_sum_w_out"))))
    p_out = g["w_out_parts"][0]
    xn, dproj, half = g["xn"], g["dproj"], d // 2
    dw_a = _matmul(xn, dproj, "tn", MM_TILE, BF16, "dw_in_a", a_cols=(half, 0))
    dw_b, got_a = _matmul(xn, dproj, "tn", MM_TILE, BF16, "dw_in_b", _pair_rider(dw_rows=dw_a, nh=nh), a_cols=(half, 1))
    sums_a = _relayout_pair_sum(dw_a, got_a[0], got_a[1], core_arr, nh, 128, "relayout_pair_sum_a")
    dxn, (p_in_a, got_b) = _dxn(dproj, wfull, [_chip_rider(sums_a, rows_total=d), _pair_rider(dw_rows=dw_b, nh=nh)],
                                MM_TILE, "dxn")
    sums_b = _relayout_pair_sum(dw_b, got_b[0], got_b[1], core_arr, nh, 128, "relayout_pair_sum_b")
    chip = 2 * lax.axis_index("x") + lax.axis_index("y")
    land = lax.dynamic_update_slice(p_in_a[0], lax.dynamic_slice_in_dim(sums_b, chip, 1, axis=0), (chip, half, 0))
    send_sems, recv_sems, sums_b, land, token = _chip_exchange_start(sums_b, land, half, "chip_exchange_b_start")
    (grad_x, dmeta, dpre_w), _ = _prenorm_bwd(dxn, x[0], meta_full, pre_norm_w + token[0:1, 0:1], g["dy"])
    small = _pack_small(d, dpre_w, g["post_w"], g["a_log"], g["dt_bias"], g["f_bias"], g["gdn_norm_w"], g["fq_w"],
                        g["fk_w"], g["loss"])
    a_conv, a_meta, p_small = _all_gather([g["conv_wt"], dmeta, small], "gather_small_grads")
    p_conv = lax.dynamic_slice_in_dim(a_conv, dev * conv_w.shape[1], conv_w.shape[1], axis=2).transpose(0, 2, 1)
    p_meta = lax.dynamic_slice_in_dim(a_meta, dev * meta_tokens.shape[1], meta_tokens.shape[1], axis=2)

    r_out = _adamw(w_out[0], p_out, m_w_out[0], v_w_out[0], 64, "adamw_w_out")
    r_conv = _adamw(conv_w[0], p_conv, m_conv_w[0], v_conv_w[0], conv_w.shape[1], "adamw_conv_w")
    r_meta = _adamw(meta_tokens, p_meta, m_meta_tokens, v_meta_tokens, N_META, "adamw_meta")
    pk = lambda pre, post, a, dt, gw, fq, fk, fb: _pack_small(d, pre, post, a, dt, fb, gw, fq, fk, zero)
    r_small = _adamw(
        pk(pre_norm_w, post_norm_w, a_log, dt_bias, gdn_norm_w, fox_q_norm_w, fox_k_norm_w, fox_f_bias), p_small,
        pk(m_pre_norm_w, m_post_norm_w, m_a_log, m_dt_bias, m_gdn_norm_w, m_fox_q_norm_w, m_fox_k_norm_w, m_fox_f_bias),
        pk(v_pre_norm_w, v_post_norm_w, v_a_log, v_dt_bias, v_gdn_norm_w, v_fox_q_norm_w, v_fox_k_norm_w, v_fox_f_bias),
        8, "adamw_small")
    p_in = _chip_exchange_wait(send_sems, recv_sems, sums_b, land, half, r_small[0], "chip_exchange_b_wait")
    r_in = _adamw_column_major(w_in, p_in, m_w_in, v_w_in, "adamw_w_in")

    sm = [_unpack_small(r, nh) for r in r_small]
    outs = []
    for i in range(4):
        s = sm[i]
        outs += [r_meta[i], s["pre"], r_in[i], r_conv[i][None], s["a_log"], s["dt_bias"], s["gdn_w"], s["fq_w"],
                 s["fk_w"], s["f_bias"], r_out[i][None], s["post"]]
    return (sm[0]["extra"], grad_x[None], *outs)
```

```python
import jax
import jax.numpy as jnp
from jax import lax
from jax.experimental import pallas as pl
from jax.experimental.pallas import tpu as pltpu

F32, BF16 = jnp.float32, jnp.bfloat16
HEAD_DIM = 128
N_META = 16
CONV_WIDTH = 4
CHUNK = 128
Q_BLOCK = 128
LANES = 128
EPS = 1e-6
PAD_ROWS = Q_BLOCK - N_META
N_DEV = 8
N_CHIP = 4
VMEM_LIMIT = 56 * 1024 * 1024
NEG = -1e30
NARROW = 2 * LANES
MM_TILE = 6 * LANES

ADAM_LR, ADAM_B1, ADAM_B2, ADAM_EPS, ADAM_WD, ADAM_STEP = 0.001, 0.9, 0.999, 1e-08, 0.01, 10

_DN = {"nn": (((1,), (0,)), ((), ())), "nt": (((1,), (1,)), ((), ())), "tn": (((0,), (0,)), ((), ()))}
_DN3 = {"nn": (((2,), (1,)), ((0,), (0,))), "nt": (((2,), (2,)), ((0,), (0,))), "tn": (((1,), (1,)), ((0,), (0,)))}
_ANY = pl.BlockSpec(memory_space=pl.ANY)
_MESH = pl.DeviceIdType.MESH


def _cp(*sem):
    return pltpu.CompilerParams(dimension_semantics=sem, vmem_limit_bytes=VMEM_LIMIT)


def _dot(a, b, dims="nn", prec=None):
    return lax.dot_general(a, b, _DN[dims], precision=prec, preferred_element_type=F32)


def _bdot(a, b, dims="nn"):
    return _dot(a.astype(BF16), b.astype(BF16), dims)


def _hdot(a, b, dims="nn"):
    return _dot(a, b, dims, prec=lax.Precision.HIGHEST)


def _dot3(a, b, dims="nn"):
    return lax.dot_general(a, b, _DN3[dims], preferred_element_type=F32)


def _bdot3(a, b, dims="nn"):
    return _dot3(a.astype(BF16), b.astype(BF16), dims)


def _split(a):
    hi = a.astype(BF16)
    return hi, (a - hi.astype(F32)).astype(BF16)


def _iota(shape, dim):
    return lax.broadcasted_iota(jnp.int32, shape, dim)


def _sigmoid(z):
    return 1.0 / (1.0 + jnp.exp(-z))


def _softplus(z):
    e = jnp.exp(-jnp.abs(z))
    u = 1.0 + e
    l1p = jnp.where(u == 1.0, e, jnp.log(u) * (e / jnp.where(u == 1.0, 1.0, u - 1.0)))
    return jnp.maximum(z, 0.0) + l1p


def _silu_and_grad(z):
    s = _sigmoid(z)
    return z * s, s * (1.0 + z * (1.0 - s))


def _rms(x):
    return lax.rsqrt(jnp.mean(x * x, axis=-1, keepdims=True) + EPS)


def _h_tile(i, x_ref, meta_ref):
    first = jnp.concatenate([jnp.zeros((PAD_ROWS, x_ref.shape[1]), F32), meta_ref[...]], axis=0)
    return jnp.where(i == 0, first, x_ref[...])


def _x_rows(d):
    return pl.BlockSpec((Q_BLOCK, d), lambda i: (jnp.maximum(i - 1, 0), 0))


def _prenorm(x, meta, w):
    seq, d = x.shape
    lp = seq + Q_BLOCK

    def body(x_ref, m_ref, w_ref, o_ref):
        h = _h_tile(pl.program_id(0), x_ref, m_ref)
        o_ref[...] = (h * _rms(h) * w_ref[...]).astype(BF16)

    return pl.pallas_call(
        body, grid=(lp // Q_BLOCK,),
        in_specs=[_x_rows(d), pl.BlockSpec((N_META, d), lambda i: (0, 0)), pl.BlockSpec((1, d), lambda i: (0, 0))],
        out_specs=pl.BlockSpec((Q_BLOCK, d), lambda i: (i, 0)),
        out_shape=jax.ShapeDtypeStruct((lp, d), BF16), name="prenorm", compiler_params=_cp("parallel"))(x, meta, w)


def _tile(n, want):
    return max(t for t in range(LANES, want + 1, LANES) if n % t == 0)


class _Rider:
    def __init__(self, inputs, out_shapes, scratch, aliases, make):
        self.inputs, self.out_shapes, self.scratch, self.aliases, self.make = inputs, out_shapes, scratch, aliases, make


def _hosted_call(body, riders, n_in, n_out, n_scratch, *, in_specs, out_specs, out_shape, scratch_shapes=(), aliases=None,
                 **kw):
    riders = [r for r in riders if r is not None]
    r_in = [len(r.inputs) for r in riders]
    r_out = [len(r.out_shapes) for r in riders]
    r_scr = [len(r.scratch) for r in riders]
    al = dict(aliases or {})
    for k, r in enumerate(riders):
        al.update({n_in + sum(r_in[:k]) + i: n_out + sum(r_out[:k]) + o for i, o in r.aliases.items()})

    def full_body(*refs):
        ins, rest = refs[:n_in + sum(r_in)], refs[n_in + sum(r_in):]
        outs, scr = rest[:n_out + sum(r_out)], rest[n_out + sum(r_out):]
        hooks = [r.make(ins[n_in + sum(r_in[:k]):n_in + sum(r_in[:k + 1])], outs[n_out + sum(r_out[:k]):n_out + sum(r_out[:k + 1])],
                        scr[n_scratch + sum(r_scr[:k]):n_scratch + sum(r_scr[:k + 1])]) for k, r in enumerate(riders)]

        def start():
            for h in hooks:
                h[0]()

        def finish():
            for h in hooks:
                h[1]()

        body(start, finish, *ins[:n_in], *outs[:n_out], *scr[:n_scratch])

    call = pl.pallas_call(
        full_body, in_specs=list(in_specs) + [_ANY] * sum(r_in), out_specs=list(out_specs) + [_ANY] * sum(r_out),
        out_shape=list(out_shape) + [s for r in riders for s in r.out_shapes],
        scratch_shapes=list(scratch_shapes) + [s for r in riders for s in r.scratch], input_output_aliases=al, **kw)

    def run(*args):
        res = call(*args, *[t for r in riders for t in r.inputs])
        return res[:n_out], [res[n_out + sum(r_out[:k]):n_out + sum(r_out[:k + 1])] for k in range(len(riders))]

    return run


def _matmul(a, b, dims, tn, out_dtype, name, rider=None, a_cols=None):
    a_shape = a.shape if a_cols is None else (a.shape[0], a_cols[0])
    a_index = 0 if a_cols is None else a_cols[1]
    m = a_shape[1] if dims == "tn" else a_shape[0]
    n = b.shape[0] if dims == "nt" else b.shape[1]
    kdim = b.shape[1] if dims == "nt" else b.shape[0]
    tn = _tile(n, tn)
    steps = n // tn
    b_spec = pl.BlockSpec((tn, kdim), lambda j: (j, 0)) if dims == "nt" else pl.BlockSpec((kdim, tn), lambda j: (0, j))

    def body(start, finish, a_ref, b_ref, o_ref):
        pl.when(pl.program_id(0) == 0)(start)
        o_ref[...] = _dot(a_ref[...], b_ref[...], dims).astype(out_dtype)
        pl.when(pl.program_id(0) == steps - 1)(finish)

    (out,), got = _hosted_call(
        body, [rider], 2, 1, 0, grid=(steps,), in_specs=[pl.BlockSpec(a_shape, lambda j: (0, a_index)), b_spec],
        out_specs=[pl.BlockSpec((m, tn), lambda j: (0, j))], out_shape=[jax.ShapeDtypeStruct((m, n), out_dtype)],
        name=name, compiler_params=_cp("parallel" if rider is None else "arbitrary"))(a, b)
    return out if rider is None else (out, got[0])


def _chip_rider(sums):
    def make(ins, outs, scratch):
        local, remote = _chip_exchange_copies(ins[0], outs[0], *scratch)

        def start():
            for cp in [local] + remote:
                cp.start()

        def finish():
            local.wait()
            for cp in remote:
                cp.wait_send()
                cp.wait_recv()

        return start, finish

    return _Rider([sums], [jax.ShapeDtypeStruct(sums.shape, sums.dtype)],
                  [pltpu.SemaphoreType.DMA((N_CHIP - 1,)), pltpu.SemaphoreType.DMA((N_CHIP - 1,)), pltpu.SemaphoreType.DMA((1,))],
                  {}, make)


_HBM = pl.BlockSpec(memory_space=pltpu.HBM)
_SEM = pl.BlockSpec(memory_space=pltpu.SEMAPHORE)


def _chip_exchange_copies(sums_ref, land_ref, send_sems, recv_sems, local_sem):
    x, y, core = lax.axis_index("x"), lax.axis_index("y"), lax.axis_index("c")
    mine = 2 * x + y
    local = pltpu.make_async_copy(sums_ref.at[mine], land_ref.at[mine], local_sem.at[0])
    remote = []
    for k in range(1, N_CHIP):
        px = 1 - x if k & 2 else x
        py = 1 - y if k & 1 else y
        remote.append(pltpu.make_async_remote_copy(
            src_ref=sums_ref.at[2 * px + py], dst_ref=land_ref.at[mine], send_sem=send_sems.at[k - 1],
            recv_sem=recv_sems.at[k - 1], device_id=(px, py, core), device_id_type=_MESH))
    return local, remote


def _chip_exchange_start(sums, name):
    def body(s_ref, send_sems, recv_sems, local_sem, s_thru, land_ref, token):
        local, remote = _chip_exchange_copies(s_ref, land_ref, send_sems, recv_sems, local_sem)
        for cp in [local] + remote:
            cp.start()
        token[...] = jnp.zeros_like(token)

    *flight, token = pl.pallas_call(
        body, name=name,
        out_shape=(pltpu.SemaphoreType.DMA((N_CHIP - 1,)), pltpu.SemaphoreType.DMA((N_CHIP - 1,)), pltpu.SemaphoreType.DMA((1,)),
                   pltpu.HBM(sums.shape, sums.dtype), pltpu.HBM(sums.shape, sums.dtype), jax.ShapeDtypeStruct((8, LANES), F32)),
        in_specs=(_HBM,), out_specs=(_SEM, _SEM, _SEM, _HBM, _HBM, pl.BlockSpec(memory_space=pltpu.VMEM)),
        input_output_aliases={0: 3},
        compiler_params=pltpu.CompilerParams(has_side_effects=pltpu.SideEffectType.DATAFLOW_SIDE_EFFECTING))(
            pltpu.with_memory_space_constraint(sums, pltpu.HBM))
    return flight, token


def _chip_exchange_wait(flights, after, name):
    n = len(flights)

    def body(*refs):
        for i in range(n):
            s_ref, land_ref = refs[2 * i:2 * i + 2]
            local, remote = _chip_exchange_copies(s_ref, land_ref, *refs[2 * n + 3 * i:2 * n + 3 * i + 3])
            local.wait()
            for cp in remote:
                cp.wait_send()
                cp.wait_recv()

    buffers = [b for f in flights for b in f[3:]]
    res = pl.pallas_call(
        body, name=name, out_shape=tuple(pltpu.HBM(b.shape, b.dtype) for b in buffers),
        in_specs=(_HBM,) * (2 * n) + (_SEM,) * (3 * n) + (_ANY,), out_specs=(_HBM,) * (2 * n),
        input_output_aliases={i: i for i in range(2 * n)},
        compiler_params=pltpu.CompilerParams(has_side_effects=pltpu.SideEffectType.DATAFLOW_SIDE_EFFECTING))(
            *buffers, *[s for f in flights for s in f[:3]], after)
    return res[1::2]


def _dxn(dproj, wfull, riders, tk, name):
    m, k = dproj.shape
    n = wfull.shape[0]
    tk = _tile(k, tk)
    steps = k // tk

    def body(start, finish, a_ref, b_ref, o_ref):
        j = pl.program_id(0)

        @pl.when(j == 0)
        def _():
            start()
            o_ref[...] = jnp.zeros_like(o_ref)
        o_ref[...] += _dot(a_ref[...], b_ref[...], "nt")
        pl.when(j == steps - 1)(finish)

    (dxn,), got = _hosted_call(
        body, riders, 2, 1, 0, grid=(steps,),
        in_specs=[pl.BlockSpec((m, tk), lambda j: (0, j)), pl.BlockSpec((n, tk), lambda j: (0, j))],
        out_specs=[pl.BlockSpec((m, n), lambda j: (0, 0))], out_shape=[jax.ShapeDtypeStruct((m, n), F32)],
        name=name, compiler_params=_cp("arbitrary"))(dproj, wfull)
    return dxn, got


def _conv_taps(x, w):
    c = x * w[CONV_WIDTH - 1:CONV_WIDTH, :]
    for j in range(CONV_WIDTH - 1):
        c = c + pltpu.roll(x, CONV_WIDTH - 1 - j, 0) * w[j:j + 1, :]
    return c


def _gdn_prep(proj, conv_wt, nh):
    lp = proj.shape[0]
    scale = HEAD_DIM ** -0.5

    def body(x_ref, w_ref, o_ref):
        which = pl.program_id(0) // nh
        c = _conv_taps(x_ref[...], w_ref[...])
        s = c * _sigmoid(c)
        r = lax.rsqrt(jnp.sum(s * s, axis=-1, keepdims=True) + EPS)
        f = jnp.where(which == 0, r * scale, jnp.where(which == 1, r, 1.0))
        o_ref[...] = jnp.where(_iota(s.shape, 0) >= PAD_ROWS, s * f, 0.0)

    return pl.pallas_call(
        body, grid=(3 * nh,),
        in_specs=[pl.BlockSpec((lp, LANES), lambda s: (0, s)), pl.BlockSpec((CONV_WIDTH, LANES), lambda s: (0, s))],
        out_specs=pl.BlockSpec((lp, LANES), lambda s: (0, s)),
        out_shape=jax.ShapeDtypeStruct((lp, 3 * nh * HEAD_DIM), F32), name="gdn_prep",
        compiler_params=_cp("parallel"))(proj, conv_wt)


def _gdn_prep_bwd(proj, conv_wt, dq, dk, dv, dproj, nh):
    lp = proj.shape[0]
    scale = HEAD_DIM ** -0.5
    part = lambda p: pl.BlockSpec((lp, LANES), lambda s: (0, jnp.clip(s - p * nh, 0, nh - 1)))

    def body(x_ref, w_ref, dq_ref, dk_ref, dv_ref, _, dx_ref, dw_ref):
        which = pl.program_id(0) // nh
        x = x_ref[...]
        w = w_ref[...]
        c = _conv_taps(x, w)
        sg = _sigmoid(c)
        s = c * sg
        r = lax.rsqrt(jnp.sum(s * s, axis=-1, keepdims=True) + EPS)
        dy = jnp.where(which == 0, dq_ref[...], jnp.where(which == 1, dk_ref[...], dv_ref[...]))
        dy = jnp.where(_iota(s.shape, 0) >= PAD_ROWS, dy, 0.0)
        y0 = s * r
        dy0 = dy * jnp.where(which == 0, scale, 1.0)
        ds_n = r * (dy0 - y0 * jnp.sum(dy0 * y0, axis=-1, keepdims=True))
        ds = jnp.where(which == 2, dy, ds_n)
        dc = ds * (sg * (1.0 + c * (1.0 - sg)))
        dx = dc * w[CONV_WIDTH - 1:CONV_WIDTH, :]
        rows = [jnp.sum(dc * x, axis=0, keepdims=True)]
        for j in range(CONV_WIDTH - 2, -1, -1):
            sh = CONV_WIDTH - 1 - j
            dx = dx + pltpu.roll(dc, lp - sh, 0) * w[j:j + 1, :]
            rows.insert(0, jnp.sum(dc * pltpu.roll(x, sh, 0), axis=0, keepdims=True))
        dx_ref[...] = dx.astype(BF16)
        dw_ref[...] = jnp.concatenate(rows, axis=0)

    strip = pl.BlockSpec((lp, LANES), lambda s: (0, s))
    taps = pl.BlockSpec((CONV_WIDTH, LANES), lambda s: (0, s))
    return pl.pallas_call(
        body, grid=(3 * nh,), in_specs=[strip, taps, part(0), part(1), part(2), _ANY], out_specs=[strip, taps],
        out_shape=[jax.ShapeDtypeStruct(dproj.shape, BF16), jax.ShapeDtypeStruct((CONV_WIDTH, 3 * nh * HEAD_DIM), F32)],
        input_output_aliases={5: 0}, name="gdn_prep_bwd", compiler_params=_cp("parallel"))(proj, conv_wt, dq, dk, dv, dproj)


def _gates(proj, bias_row, nega_row, nh):
    lp = proj.shape[0]
    nc = lp // CHUNK

    def body(p_ref, b_ref, a_ref, g_ref, gt3_ref, gtf_ref):
        lane = _iota((CHUNK, LANES), 1)
        tri = (_iota((CHUNK, CHUNK), 0) >= _iota((CHUNK, CHUNK), 1)).astype(F32)

        def step(n, carry):
            r0 = pl.multiple_of(n * CHUNK, CHUNK)
            z = p_ref[pl.ds(r0, CHUNK), :] + b_ref[...]
            base = jnp.where(lane < nh, _sigmoid(z),
                             jnp.where(lane < 2 * nh, a_ref[...] * _softplus(z),
                                       jnp.where(lane < 3 * nh, -_softplus(-z), 0.0)))
            base = jnp.where(r0 + _iota((CHUNK, LANES), 0) >= PAD_ROWS, base, 0.0)
            cs = _hdot(tri, base)
            run = jnp.where((lane >= 2 * nh) & (lane < 3 * nh), cs + carry, cs)
            sh = pltpu.roll(run, 2 * nh, 1)
            out = base + jnp.where((lane >= 3 * nh) & (lane < 5 * nh), sh, 0.0)
            g_ref[pl.ds(r0, CHUNK), :] = out
            gt3_ref[n] = out.T
            return carry + cs[CHUNK - 1:CHUNK, :]

        lax.fori_loop(0, nc, step, jnp.zeros((1, LANES), F32))
        gtf_ref[...] = g_ref[...].T

    vec = pl.BlockSpec((1, LANES), lambda i: (0, 0))
    return pl.pallas_call(
        body, grid=(1,), in_specs=[pl.BlockSpec((lp, LANES), lambda i: (0, 8 * nh)), vec, vec],
        out_specs=[pl.BlockSpec((lp, LANES), lambda i: (0, 0)), pl.BlockSpec((nc, LANES, CHUNK), lambda i: (0, 0, 0)),
                   pl.BlockSpec((LANES, lp), lambda i: (0, 0))],
        out_shape=[jax.ShapeDtypeStruct((lp, LANES), F32), jax.ShapeDtypeStruct((nc, LANES, CHUNK), F32),
                   jax.ShapeDtypeStruct((LANES, lp), F32)],
        name="gates", compiler_params=_cp("arbitrary"))(proj, bias_row, nega_row)


def _gates_bwd(proj, bias_row, nega_row, gates, dgate_gdn, dc_t, dproj, nh):
    lp = proj.shape[0]
    nc = lp // CHUNK

    def body(p_ref, b_ref, a_ref, g_ref, dg_ref, dc_ref, _, dz_ref, sm_ref, dct_scr):
        lane = _iota((CHUNK, LANES), 1)
        triu = (_iota((CHUNK, CHUNK), 0) <= _iota((CHUNK, CHUNK), 1)).astype(F32)
        dct_scr[...] = dc_ref[...].T
        sm_ref[...] = jnp.zeros_like(sm_ref)
        dz_ref[:, LANES:] = jnp.zeros((lp, NARROW - LANES), BF16)

        def step(i, carry):
            n = nc - 1 - i
            r0 = pl.multiple_of(n * CHUNK, CHUNK)
            z = p_ref[pl.ds(r0, CHUNK), :] + b_ref[...]
            gt = g_ref[pl.ds(r0, CHUNK), :]
            dgd = dg_ref[pl.ds(r0, CHUNK), :]
            dch = dct_scr[pl.ds(r0, CHUNK), :]
            rc = _hdot(triu, dch) + carry
            sg = _sigmoid(z)
            dz = jnp.where(lane < nh, dgd * sg * (1.0 - sg),
                           jnp.where(lane < 2 * nh, dgd * a_ref[...] * sg,
                                     jnp.where(lane < 3 * nh, rc * (1.0 - sg), 0.0)))
            dz = jnp.where(r0 + _iota((CHUNK, LANES), 0) >= PAD_ROWS, dz, 0.0)
            dz_ref[pl.ds(r0, CHUNK), 0:LANES] = dz.astype(BF16)
            sm_ref[0:1, :] += jnp.sum(dz, axis=0, keepdims=True)
            sm_ref[1:2, :] += jnp.sum(jnp.where((lane >= nh) & (lane < 2 * nh), dgd * gt, 0.0), axis=0, keepdims=True)
            return carry + jnp.sum(dch, axis=0, keepdims=True)

        lax.fori_loop(0, nc, step, jnp.zeros((1, LANES), F32))

    vec = pl.BlockSpec((1, LANES), lambda i: (0, 0))
    full = pl.BlockSpec((lp, LANES), lambda i: (0, 0))
    last = pl.BlockSpec((lp, LANES), lambda i: (0, 8 * nh))
    tail = pl.BlockSpec((lp, NARROW), lambda i: (0, 8 * nh * LANES // NARROW))
    return pl.pallas_call(
        body, grid=(1,), in_specs=[last, vec, vec, full, full, pl.BlockSpec((LANES, lp), lambda i: (0, 0)), _ANY],
        out_specs=[tail, pl.BlockSpec((8, LANES), lambda i: (0, 0))],
        out_shape=[jax.ShapeDtypeStruct(dproj.shape, BF16), jax.ShapeDtypeStruct((8, LANES), F32)],
        scratch_shapes=[pltpu.VMEM((lp, LANES), F32)], input_output_aliases={6: 0},
        name="gates_bwd", compiler_params=_cp("arbitrary"))(proj, bias_row, nega_row, gates, dgate_gdn, dc_t, dproj)


def _tri_inv(a):
    t = jnp.where(_iota(a.shape, 1) == _iota(a.shape, 2), 1.0, 0.0) - a
    p = a
    for _ in range(CHUNK.bit_length() - 2):
        ph, pw = _split(p)
        p = _dot3(ph, ph) + (_dot3(ph, pw) + _dot3(pw, ph))
        ph, pw = _split(p)
        th, tw = _split(t)
        t = t + (_dot3(th, ph) + (_dot3(th, pw) + _dot3(tw, ph)))
    return t


def _gdn_chunk(q, k, v, beta, gc, gr, t=None):
    ii, jj = _iota((1, CHUNK, CHUNK), 1), _iota((1, CHUNK, CHUNK), 2)
    causal, strict = ii >= jj, ii > jj
    dm = jnp.where(causal, jnp.exp(jnp.where(causal, gc - gr, 0.0)), 0.0)
    kk = _bdot3(k, k, "nt")
    a = jnp.where(strict, beta * kk * dm, 0.0)
    if t is None:
        t = _tri_inv(a)
    eg = jnp.exp(gc)
    glast = gc[:, CHUNK - 1:CHUNK, :]
    ekd = jnp.exp(glast - gc)
    bv = beta * v
    bk = (beta * eg) * k
    ub = _bdot3(t, jnp.concatenate([bv, bk], axis=2))
    qk = _bdot3(q, k, "nt")
    return dict(causal=causal, strict=strict, dm=dm, kk=kk, a=a, t=t, eg=eg, ekd=ekd, bv=bv, bk=bk,
                u=ub[:, :, :HEAD_DIM], w=ub[:, :, HEAD_DIM:], qk=qk, aqk=jnp.where(causal, qk * dm, 0.0),
                q_dec=q * eg, k_dec=k * ekd, decay=jnp.exp(glast))


def _heads(ref, nh):
    return jnp.stack([ref[:, h * HEAD_DIM:(h + 1) * HEAD_DIM] for h in range(nh)], axis=0)


def _gdn_chunk_inputs(q_ref, k_ref, v_ref, g, gt, nh):
    col = lambda o: jnp.stack([g[:, o + h:o + h + 1] for h in range(nh)], axis=0)
    gr = jnp.stack([gt[3 * nh + h:3 * nh + h + 1, :] for h in range(nh)], axis=0)
    return _heads(q_ref, nh), _heads(k_ref, nh), _heads(v_ref, nh), col(0), col(3 * nh), gr


def _gdn_fwd(qkv, gates, gt3, nh, rider=None):
    lp = qkv.shape[0]
    nc = lp // CHUNK
    w = nh * HEAD_DIM

    def body(start, finish, q_ref, k_ref, v_ref, g_ref, gt_ref, o_ref, sall_ref, tall_ref, s_scr):
        @pl.when(pl.program_id(0) == 0)
        def _():
            start()
            s_scr[...] = jnp.zeros_like(s_scr)
        c = _gdn_chunk(*_gdn_chunk_inputs(q_ref, k_ref, v_ref, g_ref[...], gt_ref[0], nh))
        s = s_scr[...]
        sall_ref[0] = s
        tall_ref[0] = c["t"]
        v_new = c["u"] - _bdot3(c["w"], s)
        o = _bdot3(c["q_dec"], s) + _bdot3(c["aqk"], v_new)
        s_scr[...] = s * c["decay"] + _bdot3(c["k_dec"], v_new, "tn")
        for h in range(nh):
            o_ref[:, h * HEAD_DIM:(h + 1) * HEAD_DIM] = o[h]
        pl.when(pl.program_id(0) == nc - 1)(finish)

    outs, got = _hosted_call(
        body, [rider], 5, 3, 1, grid=(nc,),
        in_specs=[pl.BlockSpec((CHUNK, w), lambda n: (n, 0)), pl.BlockSpec((CHUNK, w), lambda n: (n, 1)),
                  pl.BlockSpec((CHUNK, w), lambda n: (n, 2)), pl.BlockSpec((CHUNK, LANES), lambda n: (n, 0)),
                  pl.BlockSpec((1, LANES, CHUNK), lambda n: (n, 0, 0))],
        out_specs=[pl.BlockSpec((CHUNK, w), lambda n: (n, 0)),
                   pl.BlockSpec((1, nh, HEAD_DIM, HEAD_DIM), lambda n: (n, 0, 0, 0)),
                   pl.BlockSpec((1, nh, CHUNK, CHUNK), lambda n: (n, 0, 0, 0))],
        out_shape=[jax.ShapeDtypeStruct((lp, w), F32), jax.ShapeDtypeStruct((nc, nh, HEAD_DIM, HEAD_DIM), F32),
                   jax.ShapeDtypeStruct((nc, nh, CHUNK, CHUNK), F32)],
        scratch_shapes=[pltpu.VMEM((nh, HEAD_DIM, HEAD_DIM), F32)],
        name="gdn_fwd", compiler_params=_cp("arbitrary"))(qkv, qkv, qkv, gates, gt3)
    return outs, (got[0] if got else None)


def _gdn_bwd(qkv, gates, gt3, s_all, t_all, do, nh, rider=None):
    lp = qkv.shape[0]
    nc = lp // CHUNK
    w = nh * HEAD_DIM
    rev = lambda n: nc - 1 - n

    def body(start, finish, q_ref, k_ref, v_ref, g_ref, gt_ref, s_ref, t_ref, do_ref, dq_ref, dk_ref, dv_ref, dg_ref, ds_scr):
        @pl.when(pl.program_id(0) == 0)
        def _():
            start()
            ds_scr[...] = jnp.zeros_like(ds_scr)
        q, k, v, beta, gc, gr = _gdn_chunk_inputs(q_ref, k_ref, v_ref, g_ref[...], gt_ref[0], nh)
        c = _gdn_chunk(q, k, v, beta, gc, gr, t_ref[0])
        s = s_ref[0]
        dsn = ds_scr[...]
        dout = _heads(do_ref, nh)
        v_new = c["u"] - _bdot3(c["w"], s)
        dq_dec = _bdot3(dout, s, "nt")
        daqk = jnp.where(c["causal"], _bdot3(dout, v_new, "nt"), 0.0)
        dv_new = _bdot3(c["aqk"], dout, "tn") + _bdot3(c["k_dec"], dsn)
        dk_dec = _bdot3(v_new, dsn, "nt")
        ddecay = jnp.sum(jnp.sum(dsn * s, axis=2, keepdims=True), axis=1, keepdims=True)
        dw = -_bdot3(dv_new, s, "nt")
        ds_scr[...] = _bdot3(c["q_dec"], dout, "tn") + c["decay"] * dsn - _bdot3(c["w"], dv_new, "tn")
        duw = jnp.concatenate([dv_new, dw], axis=2)
        dt = _bdot3(duw, jnp.concatenate([c["bv"], c["bk"]], axis=2), "nt")
        dbvk = _bdot3(c["t"], duw, "tn")
        dbv, dbk = dbvk[:, :, :HEAD_DIM], dbvk[:, :, HEAD_DIM:]
        da = jnp.where(c["strict"], -_bdot3(_bdot3(c["t"], dt, "tn"), c["t"], "nt"), 0.0)
        dkk = da * beta * c["dm"]
        dqk = daqk * c["dm"]
        e = da * c["a"] + daqk * c["aqk"]
        dq = dq_dec * c["eg"] + _bdot3(dqk, k)
        dk = (dk_dec * c["ekd"] + _bdot3(dkk, k) + _bdot3(dkk, k, "tn") + _bdot3(dqk, q, "tn")
              + (beta * c["eg"]) * dbk)
        dv = beta * dbv
        rs = lambda x: jnp.sum(x, axis=2, keepdims=True)
        dbeta = rs(dbv * v) + c["eg"] * rs(dbk * k) + rs(da * c["kk"] * c["dm"])
        kd_term = rs(dk_dec * c["k_dec"])
        eh, ew = _split(e)
        ones = jnp.ones((nh, CHUNK, LANES), BF16)
        col_sums = (_dot3(eh, ones, "tn") + _dot3(ew, ones, "tn"))[:, :, 0:1]
        dg_cum = rs(dq_dec * c["q_dec"]) - kd_term + rs(dbk * c["bk"]) + rs(e) - col_sums
        last = jnp.sum(kd_term, axis=1, keepdims=True) + ddecay * c["decay"]
        dg_cum = dg_cum + jnp.where(_iota((1, CHUNK, 1), 1) == CHUNK - 1, last, 0.0)
        lane = _iota((CHUNK, LANES), 1)
        acc = jnp.zeros((CHUNK, LANES), F32)
        for h in range(nh):
            sl = slice(h * HEAD_DIM, (h + 1) * HEAD_DIM)
            dq_ref[:, sl] = dq[h]
            dk_ref[:, sl] = dk[h]
            dv_ref[:, sl] = dv[h]
            acc = acc + jnp.where(lane == h, dbeta[h], 0.0) + jnp.where(lane == nh + h, dg_cum[h], 0.0)
        triu = (_iota((CHUNK, CHUNK), 0) <= _iota((CHUNK, CHUNK), 1)).astype(F32)
        dg_ref[...] = jnp.where(lane < nh, acc, _hdot(triu, acc))
        pl.when(pl.program_id(0) == nc - 1)(finish)

    outs, got = _hosted_call(
        body, [rider], 8, 4, 1, grid=(nc,),
        in_specs=[pl.BlockSpec((CHUNK, w), lambda n: (rev(n), 0)), pl.BlockSpec((CHUNK, w), lambda n: (rev(n), 1)),
                  pl.BlockSpec((CHUNK, w), lambda n: (rev(n), 2)), pl.BlockSpec((CHUNK, LANES), lambda n: (rev(n), 0)),
                  pl.BlockSpec((1, LANES, CHUNK), lambda n: (rev(n), 0, 0)),
                  pl.BlockSpec((1, nh, HEAD_DIM, HEAD_DIM), lambda n: (rev(n), 0, 0, 0)),
                  pl.BlockSpec((1, nh, CHUNK, CHUNK), lambda n: (rev(n), 0, 0, 0)),
                  pl.BlockSpec((CHUNK, w), lambda n: (rev(n), 0))],
        out_specs=[pl.BlockSpec((CHUNK, w), lambda n: (rev(n), 0))] * 3 + [pl.BlockSpec((CHUNK, LANES), lambda n: (rev(n), 0))],
        out_shape=[jax.ShapeDtypeStruct((lp, w), F32)] * 3 + [jax.ShapeDtypeStruct((lp, LANES), F32)],
        scratch_shapes=[pltpu.VMEM((nh, HEAD_DIM, HEAD_DIM), F32)],
        name="gdn_bwd", compiler_params=_cp("arbitrary"))(qkv, qkv, qkv, gates, gt3, s_all, t_all, do)
    return outs, (got[0] if got else None)


def _merge_gdn(o_gdn, proj, norm_w, nh):
    lp = o_gdn.shape[0]

    def body(o_ref, z_ref, w_ref, m_ref):
        o = o_ref[...]
        z = z_ref[...]
        m_ref[...] = (o * _rms(o) * w_ref[...] * (z * _sigmoid(z))).astype(BF16)

    return pl.pallas_call(
        body, grid=(nh,),
        in_specs=[pl.BlockSpec((lp, LANES), lambda s: (0, s)), pl.BlockSpec((lp, LANES), lambda s: (0, 3 * nh + s)),
                  pl.BlockSpec((1, LANES), lambda s: (0, 0))],
        out_specs=pl.BlockSpec((lp, LANES), lambda s: (0, s)),
        out_shape=jax.ShapeDtypeStruct((lp, 2 * nh * HEAD_DIM), BF16), name="merge_gdn",
        compiler_params=_cp("parallel"))(o_gdn, proj, norm_w)


def _merge_gdn_bwd(o_gdn, proj, norm_w, dmerged, nh):
    lp = o_gdn.shape[0]

    def body(o_ref, z_ref, w_ref, dm_ref, do_ref, dz_ref, dw_ref):
        o = o_ref[...]
        r = _rms(o)
        xh = o * r
        silu, dsilu = _silu_and_grad(z_ref[...])
        dm = dm_ref[...]
        dn = dm * silu
        dz_ref[...] = (dm * (xh * w_ref[...]) * dsilu).astype(BF16)
        dnw = dn * w_ref[...]
        do_ref[...] = r * (dnw - xh * jnp.mean(dnw * xh, axis=-1, keepdims=True))

        @pl.when(pl.program_id(0) == 0)
        def _():
            dw_ref[...] = jnp.zeros_like(dw_ref)
        dw_ref[...] += jnp.sum(dn * xh, axis=0, keepdims=True)

    w = nh * HEAD_DIM
    return pl.pallas_call(
        body, grid=(nh,),
        in_specs=[pl.BlockSpec((lp, LANES), lambda s: (0, s)), pl.BlockSpec((lp, LANES), lambda s: (0, 3 * nh + s)),
                  pl.BlockSpec((1, LANES), lambda s: (0, 0)), pl.BlockSpec((lp, LANES), lambda s: (0, s))],
        out_specs=[pl.BlockSpec((lp, LANES), lambda s: (0, s)), pl.BlockSpec((lp, LANES), lambda s: (0, 3 * nh + s)),
                   pl.BlockSpec((1, LANES), lambda s: (0, 0))],
        out_shape=[jax.ShapeDtypeStruct((lp, w), F32), jax.ShapeDtypeStruct((lp, 8 * w + NARROW), BF16),
                   jax.ShapeDtypeStruct((1, LANES), F32)],
        name="merge_gdn_bwd", compiler_params=_cp("arbitrary"))(o_gdn, proj, norm_w, dmerged)


def _fox_prep(proj, qk_w, nh):
    lp = proj.shape[0]

    def body(x_ref, w_ref, o_ref):
        x = x_ref[...]
        o_ref[...] = x * _rms(x) * w_ref[0]

    return pl.pallas_call(
        body, grid=(2 * nh,),
        in_specs=[pl.BlockSpec((lp, LANES), lambda s: (0, 4 * nh + s)), pl.BlockSpec((1, 1, LANES), lambda s: (s // nh, 0, 0))],
        out_specs=pl.BlockSpec((lp, LANES), lambda s: (0, s)),
        out_shape=jax.ShapeDtypeStruct((lp, 2 * nh * HEAD_DIM), F32), name="fox_prep",
        compiler_params=_cp("parallel"))(proj, qk_w)


def _fox_prep_bwd(proj, qk_w, dq, dk, dproj, nh):
    lp = proj.shape[0]
    part = lambda p: pl.BlockSpec((lp, LANES), lambda s: (0, jnp.clip(s - p * nh, 0, nh - 1)))

    def body(x_ref, w_ref, dq_ref, dk_ref, _, dx_ref, dw_ref):
        x = x_ref[...]
        r = _rms(x)
        xh = x * r
        dy = jnp.where(pl.program_id(0) < nh, dq_ref[...], dk_ref[...])
        dyw = dy * w_ref[0]
        dx_ref[...] = (r * (dyw - xh * jnp.mean(dyw * xh, axis=-1, keepdims=True))).astype(BF16)

        @pl.when(pl.program_id(0) % nh == 0)
        def _():
            dw_ref[...] = jnp.zeros_like(dw_ref)
        dw_ref[0] += jnp.sum(dy * xh, axis=0, keepdims=True)

    strip = pl.BlockSpec((lp, LANES), lambda s: (0, 4 * nh + s))
    wsp = pl.BlockSpec((1, 1, LANES), lambda s: (s // nh, 0, 0))
    return pl.pallas_call(
        body, grid=(2 * nh,), in_specs=[strip, wsp, part(0), part(1), _ANY], out_specs=[strip, wsp],
        out_shape=[jax.ShapeDtypeStruct(dproj.shape, BF16), jax.ShapeDtypeStruct((2, 1, LANES), F32)],
        input_output_aliases={4: 0}, name="fox_prep_bwd", compiler_params=_cp("arbitrary"))(proj, qk_w, dq, dk, dproj)


def _fox_probs(q, k, gates, crow, h, i, nh, lse=None):
    kl = k.shape[0]
    lane = _iota((Q_BLOCK, LANES), 1)
    ct = jnp.sum(jnp.where(lane == 4 * nh + h, gates, 0.0), axis=1, keepdims=True)
    tq, kq = _iota((Q_BLOCK, Q_BLOCK), 0), _iota((Q_BLOCK, Q_BLOCK), 1)
    qs = q * (HEAD_DIM ** -0.5)
    if i == 0:
        s = _bdot(qs, k, "nt") + (ct - crow)
        s = jnp.where((kq <= tq) & ((kq >= PAD_ROWS) | (tq < PAD_ROWS)), s, NEG)
    else:
        crow = jnp.where(_iota((1, kl), 1) < PAD_ROWS, -NEG, crow)
        s = _bdot(qs, k, "nt") + (ct - crow)
        s = jnp.concatenate([s[:, :kl - Q_BLOCK], jnp.where(kq <= tq, s[:, kl - Q_BLOCK:], NEG)], axis=1)
    if lse is not None:
        return jnp.exp(s - lse)
    m = jnp.max(s, axis=1, keepdims=True)
    p = jnp.exp(s - m)
    tot = jnp.sum(p, axis=1, keepdims=True)
    return p / tot, m + jnp.log(tot)


FOX_HEADS_PER_STEP = 2


def _fox_specs(lp, nh):
    hw = FOX_HEADS_PER_STEP * LANES
    return [pl.BlockSpec((Q_BLOCK, hw), lambda g, i: (i, g)),
            pl.BlockSpec((lp, hw), lambda g, i: (0, nh // FOX_HEADS_PER_STEP + g)),
            pl.BlockSpec((lp, hw), lambda g, i: (0, 6 * nh // FOX_HEADS_PER_STEP + g)),
            pl.BlockSpec((Q_BLOCK, LANES), lambda g, i: (i, 0)),
            pl.BlockSpec((LANES, lp), lambda g, i: (0, 0))]


def _fox_fwd(qkn, proj, gates, gtf, nh):
    lp = qkn.shape[0]

    def body(q_ref, k_ref, v_ref, g_ref, gt_ref, o_ref, lse_ref):
        g, i = pl.program_id(0), pl.program_id(1)
        for j in range(lp // Q_BLOCK):
            @pl.when(i == j)
            def _(j=j):
                kl = (j + 1) * Q_BLOCK
                for hh in range(FOX_HEADS_PER_STEP):
                    h = FOX_HEADS_PER_STEP * g + hh
                    sl = slice(hh * LANES, (hh + 1) * LANES)
                    p, lse = _fox_probs(q_ref[:, sl], k_ref[0:kl, sl], g_ref[...], gt_ref[pl.ds(4 * nh + h, 1), :][:, 0:kl],
                                        h, j, nh)
                    o_ref[:, sl] = _bdot(p, v_ref[0:kl, sl])
                    lse_ref[:, sl] = jnp.broadcast_to(lse, (Q_BLOCK, LANES))

    blk = pl.BlockSpec((Q_BLOCK, FOX_HEADS_PER_STEP * LANES), lambda g, i: (i, g))
    return pl.pallas_call(
        body, grid=(nh // FOX_HEADS_PER_STEP, lp // Q_BLOCK), in_specs=_fox_specs(lp, nh), out_specs=[blk, blk],
        out_shape=[jax.ShapeDtypeStruct((lp, nh * HEAD_DIM), F32)] * 2, name="fox_fwd",
        compiler_params=_cp("parallel", "parallel"))(qkn, qkn, proj, gates, gtf)


def _fox_bwd(qkn, proj, gates, gtf, lse, do, dproj, nh):
    lp = qkn.shape[0]
    nq = lp // Q_BLOCK
    w = nh * HEAD_DIM
    scale = HEAD_DIM ** -0.5

    def body(q_ref, k_ref, v_ref, g_ref, gt_ref, lse_ref, do_ref, _, dq_ref, dk_ref, dc_ref, dv_ref, dv_scr):
        g, i = pl.program_id(0), pl.program_id(1)

        @pl.when(i == 0)
        def _():
            dk_ref[...] = jnp.zeros_like(dk_ref)
            dv_scr[...] = jnp.zeros_like(dv_scr)
            dc_ref[...] = jnp.zeros_like(dc_ref)
        for j in range(nq):
            @pl.when(i == j)
            def _(j=j):
                kl = (j + 1) * Q_BLOCK
                for hh in range(FOX_HEADS_PER_STEP):
                    h = FOX_HEADS_PER_STEP * g + hh
                    sl = slice(hh * LANES, (hh + 1) * LANES)
                    q, k = q_ref[:, sl], k_ref[0:kl, sl]
                    p = _fox_probs(q, k, g_ref[...], gt_ref[pl.ds(4 * nh + h, 1), :][:, 0:kl], h, j, nh,
                                   lse_ref[:, sl][:, 0:1])
                    dout = do_ref[:, sl]
                    dp = _bdot(dout, v_ref[0:kl, sl], "nt")
                    ds = p * (dp - jnp.sum(p * dp, axis=1, keepdims=True))
                    dq_ref[:, sl] = _bdot(ds, k) * scale
                    dk_ref[0:kl, sl] += _bdot(ds, q * scale, "tn")
                    dv_scr[0:kl, sl] += _bdot(p, dout, "tn")
                    dc_ref[hh, :, 0:kl] -= jnp.sum(ds, axis=0, keepdims=True)

        @pl.when(i == nq - 1)
        def _():
            dv_ref[...] = dv_scr[...].astype(BF16)

    hw = FOX_HEADS_PER_STEP * LANES
    blk = pl.BlockSpec((Q_BLOCK, hw), lambda g, i: (i, g))
    col = pl.BlockSpec((lp, hw), lambda g, i: (0, g))
    return pl.pallas_call(
        body, grid=(nh // FOX_HEADS_PER_STEP, nq), in_specs=_fox_specs(lp, nh) + [blk, blk, _ANY],
        out_specs=[blk, col, pl.BlockSpec((FOX_HEADS_PER_STEP, 1, lp), lambda g, i: (g, 0, 0)),
                   pl.BlockSpec((lp, hw), lambda g, i: (0, 6 * nh // FOX_HEADS_PER_STEP + g))],
        out_shape=[jax.ShapeDtypeStruct((lp, w), F32)] * 2 + [jax.ShapeDtypeStruct((nh, 1, lp), F32),
                                                             jax.ShapeDtypeStruct(dproj.shape, BF16)],
        scratch_shapes=[pltpu.VMEM((lp, hw), F32)], input_output_aliases={7: 3},
        name="fox_bwd", compiler_params=_cp("parallel", "arbitrary"))(qkn, qkn, proj, gates, gtf, lse, do, dproj)


def _merge_fox(o_fox, proj, merged, nh):
    lp = o_fox.shape[0]

    def body(o_ref, z_ref, _, m_ref):
        z = z_ref[...]
        m_ref[...] = (o_ref[...] * (z * _sigmoid(z))).astype(BF16)

    return pl.pallas_call(
        body, grid=(nh,),
        in_specs=[pl.BlockSpec((lp, LANES), lambda s: (0, s)), pl.BlockSpec((lp, LANES), lambda s: (0, 7 * nh + s)), _ANY],
        out_specs=pl.BlockSpec((lp, LANES), lambda s: (0, nh + s)),
        out_shape=jax.ShapeDtypeStruct(merged.shape, BF16), input_output_aliases={2: 0}, name="merge_fox",
        compiler_params=_cp("parallel"))(o_fox, proj, merged)


def _merge_fox_bwd(o_fox, proj, dmerged, dproj, nh):
    lp = o_fox.shape[0]

    def body(o_ref, z_ref, dm_ref, _, do_ref, dz_ref):
        silu, dsilu = _silu_and_grad(z_ref[...])
        dm = dm_ref[...]
        do_ref[...] = dm * silu
        dz_ref[...] = (dm * o_ref[...] * dsilu).astype(BF16)

    w = nh * HEAD_DIM
    return pl.pallas_call(
        body, grid=(nh,),
        in_specs=[pl.BlockSpec((lp, LANES), lambda s: (0, s)), pl.BlockSpec((lp, LANES), lambda s: (0, 7 * nh + s)),
                  pl.BlockSpec((lp, LANES), lambda s: (0, nh + s)), _ANY],
        out_specs=[pl.BlockSpec((lp, LANES), lambda s: (0, s)), pl.BlockSpec((lp, LANES), lambda s: (0, 7 * nh + s))],
        out_shape=[jax.ShapeDtypeStruct((lp, w), F32), jax.ShapeDtypeStruct(dproj.shape, BF16)],
        input_output_aliases={3: 1}, name="merge_fox_bwd", compiler_params=_cp("parallel"))(o_fox, proj, dmerged, dproj)


def _post(out, x, target, post_w):
    lp, d = out.shape

    def body(o_ref, x_ref, t_ref, w_ref, dy_ref, do_ref, loss_ref, dw_ref):
        i = pl.program_id(0)

        @pl.when(i == 0)
        def _():
            loss_ref[...] = jnp.zeros_like(loss_ref)
            dw_ref[...] = jnp.zeros_like(dw_ref)
        o = o_ref[...]
        r = _rms(o)
        nrm = o * r
        err = jnp.where(i > 0, x_ref[...] + nrm * w_ref[...] - t_ref[...], 0.0)
        loss_ref[0:1, :] += 0.5 * jnp.sum(jnp.sum(err * err, axis=1, keepdims=True), axis=0, keepdims=True) / d
        dy = err / d
        dy_ref[...] = dy
        dw_ref[...] += jnp.sum(dy * nrm, axis=0, keepdims=True)
        dyw = dy * w_ref[...]
        do_ref[...] = (r * (dyw - nrm * jnp.mean(dyw * nrm, axis=-1, keepdims=True))).astype(BF16)

    row = pl.BlockSpec((Q_BLOCK, d), lambda i: (i, 0))
    vec = pl.BlockSpec((1, d), lambda i: (0, 0))
    return pl.pallas_call(
        body, grid=(lp // Q_BLOCK,), in_specs=[row, _x_rows(d), _x_rows(d), vec],
        out_specs=[_x_rows(d), row, pl.BlockSpec((8, LANES), lambda i: (0, 0)), vec],
        out_shape=[jax.ShapeDtypeStruct(x.shape, F32), jax.ShapeDtypeStruct((lp, d), BF16),
                   jax.ShapeDtypeStruct((8, LANES), F32), jax.ShapeDtypeStruct((1, d), F32)],
        name="post", compiler_params=_cp("arbitrary"))(out, x, target, post_w)


def _prenorm_bwd(dxn, x, meta, w, dy, rider=None):
    seq, d = x.shape
    lp = seq + Q_BLOCK

    def body(start, finish, dx_ref, x_ref, m_ref, w_ref, dy_ref, gx_ref, gm_ref, dw_ref):
        i = pl.program_id(0)
        pl.when(i == 0)(start)
        h = _h_tile(i, x_ref, m_ref)
        r = _rms(h)
        xh = h * r
        dxn_ = dx_ref[...]
        dxw = dxn_ * w_ref[...]
        dh = jnp.where(i > 0, dy_ref[...], 0.0) + r * (dxw - xh * jnp.mean(dxw * xh, axis=-1, keepdims=True))
        gx_ref[...] = dh

        @pl.when(i == 0)
        def _():
            dw_ref[...] = jnp.zeros_like(dw_ref)
            gm_ref[...] = dh[PAD_ROWS:, :]
        dw_ref[...] += jnp.sum(dxn_ * xh, axis=0, keepdims=True)
        pl.when(i == lp // Q_BLOCK - 1)(finish)

    vec = pl.BlockSpec((1, d), lambda i: (0, 0))
    met = pl.BlockSpec((N_META, d), lambda i: (0, 0))
    outs, got = _hosted_call(
        body, [rider], 5, 3, 0, grid=(lp // Q_BLOCK,),
        in_specs=[pl.BlockSpec((Q_BLOCK, d), lambda i: (i, 0)), _x_rows(d), met, vec, _x_rows(d)],
        out_specs=[_x_rows(d), met, vec],
        out_shape=[jax.ShapeDtypeStruct((seq, d), F32), jax.ShapeDtypeStruct((N_META, d), F32),
                   jax.ShapeDtypeStruct((1, d), F32)],
        name="prenorm_bwd", compiler_params=_cp("arbitrary"))(dxn, x, meta, w, dy)
    return outs, (got[0] if got else None)


def _layer_grads(x, target, meta, pre_w, wfull, conv_wt, a_log, dt_bias, gdn_norm_w, fq_w, fk_w, f_bias, w_out, post_w,
                 late_weights=None, w_out_grads=None):
    nh = a_log.shape[1]
    zpad = jnp.zeros((1, LANES - 3 * nh), F32)
    bias_row = jnp.concatenate([jnp.zeros((1, nh), F32), dt_bias, f_bias, zpad], axis=1)
    nega_row = jnp.concatenate([jnp.zeros((1, nh), F32), -jnp.exp(a_log), jnp.zeros((1, nh), F32), zpad], axis=1)
    qk_w = jnp.stack([fq_w, fk_w])

    xn = _prenorm(x, meta, pre_w)
    proj = _matmul(xn, wfull, "nn", MM_TILE, F32, "proj")
    qkv = _gdn_prep(proj, conv_wt, nh)
    gates, gt3, gtf = _gates(proj, bias_row, nega_row, nh)
    (o_gdn, s_all, t_all), got = _gdn_fwd(qkv, gates, gt3, nh, None if late_weights is None else late_weights[0])
    if late_weights is not None:
        w_out = late_weights[1](got)
    qkn = _fox_prep(proj, qk_w, nh)
    o_fox, fox_lse = _fox_fwd(qkn, proj, gates, gtf, nh)
    merged = _merge_fox(o_fox, proj, _merge_gdn(o_gdn, proj, gdn_norm_w, nh), nh)
    out = _matmul(merged, w_out, "nn", 4 * LANES, F32, "out_proj")
    dy, dout, loss_blk, dpost_w = _post(out, x, target, post_w)

    dw_out = _matmul(merged, dout, "tn", 4 * LANES, BF16, "dw_out")
    if w_out_grads is None:
        dmerged, gdn_rider = _matmul(dout, w_out, "nt", 4 * LANES, F32, "dmerged"), None
    else:
        dmerged, got = _matmul(dout, w_out, "nt", 4 * LANES, F32, "dmerged", w_out_grads[0](dw_out))
        gdn_rider = w_out_grads[1](dw_out, got)
    do_gdn, dproj, dgdn_norm_w = _merge_gdn_bwd(o_gdn, proj, gdn_norm_w, dmerged, nh)
    do_fox, dproj = _merge_fox_bwd(o_fox, proj, dmerged, dproj, nh)
    dqn, dkn, dc_t, dproj = _fox_bwd(qkn, proj, gates, gtf, fox_lse, do_fox, dproj, nh)
    dproj, dqk_w = _fox_prep_bwd(proj, qk_w, dqn, dkn, dproj, nh)
    (dgq, dgk, dgv, dgate), w_out_parts = _gdn_bwd(qkv, gates, gt3, s_all, t_all, do_gdn, nh, gdn_rider)
    dproj, dconv_wt = _gdn_prep_bwd(proj, conv_wt, dgq, dgk, dgv, dproj, nh)
    dc_rows = jnp.pad(dc_t.reshape(nh, -1), ((2 * nh, LANES - 3 * nh), (0, 0)))
    dproj, gate_sums = _gates_bwd(proj, bias_row, nega_row, gates, dgate, dc_rows, dproj, nh)
    return dict(
        loss=loss_blk[0:1, 0:1], dy=dy, xn=xn, dproj=dproj, post_w=dpost_w,
        conv_wt=dconv_wt, a_log=gate_sums[1:2, nh:2 * nh], dt_bias=gate_sums[0:1, nh:2 * nh],
        gdn_norm_w=dgdn_norm_w, fq_w=dqk_w[0], fk_w=dqk_w[1], f_bias=gate_sums[0:1, 2 * nh:3 * nh], w_out=dw_out,
        w_out_parts=w_out_parts)


def _cast_bf16(a, tr, name):
    r, c = a.shape

    def body(a_ref, o_ref):
        o_ref[...] = a_ref[...].astype(BF16)

    return pl.pallas_call(
        body, grid=(r // tr,), in_specs=[pl.BlockSpec((tr, c), lambda i: (i, 0))],
        out_specs=pl.BlockSpec((tr, c), lambda i: (i, 0)), out_shape=jax.ShapeDtypeStruct((r, c), BF16),
        name=name, compiler_params=_cp("parallel"))(a)


def _column_major(a):
    return jnp.transpose(a, (2, 0, 1))


def _cast_bf16_column_major(a3, name):
    _, r, c = a3.shape

    def body(a_ref, o_ref):
        o_ref[...] = a_ref[...].reshape(LANES, r).T.astype(BF16)

    return pl.pallas_call(
        body, grid=(pl.cdiv(c, LANES),), in_specs=[pl.BlockSpec((LANES, 1, r), lambda i: (i, 0, 0))],
        out_specs=pl.BlockSpec((r, LANES), lambda i: (0, i)), out_shape=jax.ShapeDtypeStruct((r, c), BF16),
        name=name, compiler_params=_cp("parallel"))(_column_major(a3))


def _adamw_column_major(w3, parts, m3, v3, name):
    _, r, c = w3.shape
    n_parts = parts[0].shape[0]

    def body(w_ref, *refs):
        p_refs, (m_ref, v_ref, g_ref, d_ref, nm_ref, nv_ref) = refs[:len(parts)], refs[len(parts):]
        sums = []
        for p_ref in p_refs:
            g = p_ref[0].astype(F32)
            for s in range(1, n_parts):
                g = g + p_ref[s].astype(F32)
            sums.append(g)
        g = jnp.concatenate(sums, axis=0).T
        flat = lambda ref: ref[...].reshape(LANES, r)
        m_new = ADAM_B1 * flat(m_ref) + (1.0 - ADAM_B1) * g
        v_new = ADAM_B2 * flat(v_ref) + (1.0 - ADAM_B2) * (g * g)
        m_hat = m_new / (1.0 - ADAM_B1 ** ADAM_STEP)
        v_hat = v_new / (1.0 - ADAM_B2 ** ADAM_STEP)
        delta = -ADAM_LR * (m_hat / (jnp.sqrt(v_hat) + ADAM_EPS) + ADAM_WD * flat(w_ref))
        for ref, val in ((g_ref, g), (d_ref, delta), (nm_ref, m_new), (nv_ref, v_new)):
            ref[...] = val.reshape(LANES, 1, r)

    blk = pl.BlockSpec((LANES, 1, r), lambda i: (i, 0, 0))
    outs = pl.pallas_call(
        body, grid=(pl.cdiv(c, LANES),),
        in_specs=[blk] + [pl.BlockSpec((n_parts, p.shape[1], LANES), lambda i: (0, 0, i)) for p in parts] + [blk, blk],
        out_specs=[blk] * 4, out_shape=[jax.ShapeDtypeStruct((c, 1, r), F32)] * 4, name=name,
        compiler_params=_cp("parallel"))(_column_major(w3), *parts, _column_major(m3), _column_major(v3))
    return [jnp.transpose(o, (1, 2, 0)) for o in outs]


def _gather_copies(ins, outs, send_sems, recv_sems, local_sems):
    n = len(ins)
    x, y, c = lax.axis_index("x"), lax.axis_index("y"), lax.axis_index("c")
    me, sibling = (x, y, c), (x, y, 1 - c)
    xn, yn, dg = (1 - x, y), (x, 1 - y), (1 - x, 1 - y)

    def copy(a, k, block, to, src=None):
        px, py, pc = block
        rows = outs[a].at[4 * px + 2 * py + pc]
        return pltpu.make_async_remote_copy(
            src_ref=rows if src is None else src, dst_ref=rows, send_sem=send_sems.at[a, k],
            recv_sem=recv_sems.at[a, k], device_id=to, device_id_type=_MESH)

    local = [pltpu.make_async_copy(ins[a], outs[a].at[4 * x + 2 * y + c], local_sems.at[a]) for a in range(n)]
    own = [cp for a in range(n) for cp in (copy(a, 0, me, sibling, src=ins[a]), copy(a, 1, me, (*xn, c), src=ins[a]),
                                           copy(a, 2, me, (*yn, c), src=ins[a]))]

    def start():
        for cp in local + own:
            cp.start()

    def finish():
        for a in range(n):
            @pl.when(c == 1)
            def _(a=a):
                copy(a, 1, (*xn, c), me).wait_recv()
                copy(a, 3, (*xn, c), (*yn, c)).start()

            @pl.when(c == 0)
            def _(a=a):
                copy(a, 2, (*yn, c), me).wait_recv()
                copy(a, 3, (*yn, c), (*xn, c)).start()
        for a in range(n):
            pl.when(c == 0)(copy(a, 1, (*xn, c), me).wait_recv)
            copy(a, 4, (*xn, c), sibling).start()
            pl.when(c == 1)(copy(a, 2, (*yn, c), me).wait_recv)
            copy(a, 5, (*yn, c), sibling).start()
        for a in range(n):
            copy(a, 3, (*dg, c), me).wait_recv()
            copy(a, 6, (*dg, c), sibling).start()
        for a in range(n):
            copy(a, 0, sibling, me).wait_recv()
            for k, chip in ((4, xn), (5, yn), (6, dg)):
                copy(a, k, (*chip, 1 - c), me).wait_recv()
                copy(a, k, (*chip, c), sibling).wait_send()
            copy(a, 3, (*xn, c), (*yn, c)).wait_send()
        for cp in own:
            cp.wait_send()
        for cp in local:
            cp.wait()

    return start, finish


def _gather_scratch(n):
    return [pltpu.SemaphoreType.DMA((n, N_DEV - 1)), pltpu.SemaphoreType.DMA((n, N_DEV - 1)), pltpu.SemaphoreType.DMA((n,))]


def _gather_rider(arrays):
    return _Rider(list(arrays), [jax.ShapeDtypeStruct((N_DEV,) + a.shape, a.dtype) for a in arrays],
                  _gather_scratch(len(arrays)), {}, lambda ins, outs, scratch: _gather_copies(ins, outs, *scratch))


def _all_gather(arrays, name):
    n = len(arrays)

    def body(*refs):
        start, finish = _gather_copies(refs[:n], refs[n:2 * n], *refs[2 * n:])
        start()
        finish()

    return pl.pallas_call(
        body, in_specs=[_ANY] * n, out_specs=[_ANY] * n,
        out_shape=[jax.ShapeDtypeStruct((N_DEV,) + a.shape, a.dtype) for a in arrays],
        scratch_shapes=_gather_scratch(n), name=name)(*arrays)


SLAB = 10 * LANES


def _slab_start(blk, nh, cols):
    in_second_half = blk >= N_DEV // 2
    shift = (2 * nh if in_second_half else 0) if isinstance(blk, int) else jnp.where(in_second_half, 2 * nh, 0)
    return (blk * cols - shift) // LANES * LANES


def _pair_rider(dw_rows=None, parts=None, nh=None, after=None):
    if dw_rows is not None:
        r, full = dw_rows.shape
        cols = (full - NARROW + 3 * nh) // N_DEV
        out_shapes = [jax.ShapeDtypeStruct((N_CHIP, r, SLAB), dw_rows.dtype), jax.ShapeDtypeStruct((r, NARROW), dw_rows.dtype)]
    else:
        out_shapes = [jax.ShapeDtypeStruct((N_CHIP,) + parts.shape[1:], parts.dtype)]

    def make(ins, outs, scratch):
        send_sems, recv_sems = scratch
        x, y, c = lax.axis_index("x"), lax.axis_index("y"), lax.axis_index("c")
        kw = lambda k: dict(send_sem=send_sems.at[k], recv_sem=recv_sems.at[k], device_id=(x, y, 1 - c), device_id_type=_MESH)
        copies = []
        for q in range(N_CHIP):
            if dw_rows is not None:
                first = pl.multiple_of(_slab_start(2 * q + 1 - c, nh, cols), LANES)
                copies.append(pltpu.make_async_remote_copy(src_ref=ins[0].at[:, pl.ds(first, SLAB)], dst_ref=outs[0].at[q], **kw(q)))
            else:
                copies.append(pltpu.make_async_remote_copy(src_ref=ins[0].at[2 * q + 1 - c], dst_ref=outs[0].at[q], **kw(q)))
        if dw_rows is not None:
            copies.append(pltpu.make_async_remote_copy(src_ref=ins[0].at[:, pl.ds(full - NARROW, NARROW)], dst_ref=outs[1],
                                                       **kw(N_CHIP)))

        def start():
            for cp in copies:
                cp.start()

        def finish():
            for cp in copies:
                cp.wait()

        return start, finish

    return _Rider([dw_rows if dw_rows is not None else parts] + ([] if after is None else [after]), out_shapes,
                  [pltpu.SemaphoreType.DMA((N_CHIP + 1,)), pltpu.SemaphoreType.DMA((N_CHIP + 1,))], {}, make)


def _relayout_pair_sum(dwfull, got_slabs, got_tail, core, nh, tr, name):
    d, full = dwfull.shape
    w = nh * HEAD_DIM
    cols = (8 * w + 3 * nh) // N_DEV
    segs = _native_segments(nh)

    def block(f_ref, s_ref, t_ref, q, blk):
        st = _slab_start(blk, nh, cols)
        wide = f_ref[:, st:st + SLAB].astype(F32) + s_ref[q].astype(F32)
        tail = f_ref[:, 8 * w:].astype(F32) + t_ref[...].astype(F32)
        pieces = []
        for s0, s1, t0 in segs:
            lo, hi = max(s0, blk * cols), min(s1, (blk + 1) * cols)
            if lo < hi:
                at = t0 + lo - s0
                pieces.append(tail[:, at - 8 * w:at - 8 * w + hi - lo] if at >= 8 * w else wide[:, at - st:at - st + hi - lo])
        return (pieces[0] if len(pieces) == 1 else jnp.concatenate(pieces, axis=1)).astype(dwfull.dtype)

    def body(core_ref, f_ref, s_ref, t_ref, o_ref):
        for parity in range(2):
            @pl.when(core_ref[0] == parity)
            def _(parity=parity):
                for q in range(N_CHIP):
                    o_ref[q] = block(f_ref, s_ref, t_ref, q, 2 * q + parity)

    return pl.pallas_call(
        body,
        grid_spec=pltpu.PrefetchScalarGridSpec(
            num_scalar_prefetch=1, grid=(d // tr,),
            in_specs=[pl.BlockSpec((tr, full), lambda i, c_ref: (i, 0)), pl.BlockSpec((N_CHIP, tr, SLAB), lambda i, c_ref: (0, i, 0)),
                      pl.BlockSpec((tr, NARROW), lambda i, c_ref: (i, 0))],
            out_specs=pl.BlockSpec((N_CHIP, tr, cols), lambda i, c_ref: (0, i, 0))),
        out_shape=jax.ShapeDtypeStruct((N_CHIP, d, cols), dwfull.dtype), name=name,
        compiler_params=_cp("parallel"))(core, dwfull, got_slabs, got_tail)


def _pair_sum(parts, got, core, tr, name):
    _, r, c = parts.shape

    def body(core_ref, p_ref, g_ref, o_ref):
        o_ref[...] = (p_ref[...].astype(F32) + g_ref[...].astype(F32)).astype(o_ref.dtype)

    return pl.pallas_call(
        body,
        grid_spec=pltpu.PrefetchScalarGridSpec(
            num_scalar_prefetch=1, grid=(N_CHIP, r // tr),
            in_specs=[pl.BlockSpec((1, tr, c), lambda q, i, core_ref: (2 * q + core_ref[0], i, 0)),
                      pl.BlockSpec((1, tr, c), lambda q, i, core_ref: (q, i, 0))],
            out_specs=pl.BlockSpec((1, tr, c), lambda q, i, core_ref: (q, i, 0))),
        out_shape=jax.ShapeDtypeStruct((N_CHIP, r, c), parts.dtype), name=name,
        compiler_params=_cp("parallel", "parallel"))(core, parts, got)


def _native_segments(nh):
    w = nh * HEAD_DIM
    return [(0, 4 * w, 0), (4 * w, 4 * w + 2 * nh, 8 * w), (4 * w + 2 * nh, 8 * w + 2 * nh, 4 * w),
            (8 * w + 2 * nh, 8 * w + 3 * nh, 8 * w + 2 * nh)]


def _relayout_w_in(wg, nh, tr):
    _, d, cols = wg.shape
    w = nh * HEAD_DIM

    def native(ref, j0, j1):
        out = []
        while j0 < j1:
            blk = j0 // cols
            end = min(j1, (blk + 1) * cols)
            out.append(ref[blk, :, pl.ds(j0 - blk * cols, end - j0)])
            j0 = end
        return out

    def body(g_ref, o_ref):
        for cidx in range(8 * w // LANES):
            j0 = cidx * LANES + (0 if cidx * LANES < 4 * w else 2 * nh)
            pieces = native(g_ref, j0, j0 + LANES)
            o_ref[:, cidx * LANES:(cidx + 1) * LANES] = pieces[0] if len(pieces) == 1 else jnp.concatenate(pieces, axis=1)
        pieces = (native(g_ref, 4 * w, 4 * w + 2 * nh) + native(g_ref, 8 * w + 2 * nh, 8 * w + 3 * nh)
                  + [jnp.zeros((tr, NARROW - 3 * nh), wg.dtype)])
        o_ref[:, 8 * w:] = jnp.concatenate(pieces, axis=1)

    return pl.pallas_call(
        body, grid=(d // tr,), in_specs=[pl.BlockSpec((N_DEV, tr, cols), lambda i: (0, i, 0))],
        out_specs=pl.BlockSpec((tr, 8 * w + NARROW), lambda i: (i, 0)),
        out_shape=jax.ShapeDtypeStruct((d, 8 * w + NARROW), wg.dtype),
        name="relayout_w_in", compiler_params=_cp("parallel"))(wg)


def _adamw(w, parts, m, v, tr, name):
    r, c = w.shape
    n_parts = parts.shape[0]

    def body(w_ref, p_ref, m_ref, v_ref, g_ref, d_ref, nm_ref, nv_ref):
        g = p_ref[0].astype(F32)
        for s in range(1, n_parts):
            g = g + p_ref[s].astype(F32)
        m_new = ADAM_B1 * m_ref[...] + (1.0 - ADAM_B1) * g
        v_new = ADAM_B2 * v_ref[...] + (1.0 - ADAM_B2) * (g * g)
        m_hat = m_new / (1.0 - ADAM_B1 ** ADAM_STEP)
        v_hat = v_new / (1.0 - ADAM_B2 ** ADAM_STEP)
        g_ref[...] = g
        d_ref[...] = -ADAM_LR * (m_hat / (jnp.sqrt(v_hat) + ADAM_EPS) + ADAM_WD * w_ref[...])
        nm_ref[...] = m_new
        nv_ref[...] = v_new

    blk = pl.BlockSpec((tr, c), lambda i: (i, 0))
    return pl.pallas_call(
        body, grid=(r // tr,), in_specs=[blk, pl.BlockSpec((n_parts, tr, c), lambda i: (0, i, 0)), blk, blk],
        out_specs=[blk] * 4, out_shape=[jax.ShapeDtypeStruct((r, c), F32)] * 4, name=name,
        compiler_params=_cp("parallel"))(w, parts, m, v)


def _pack_small(d, pre, post, a_log, dt_bias, f_bias, gdn_w, fq_w, fk_w, extra):
    row2 = jnp.concatenate([a_log, dt_bias, f_bias, gdn_w, fq_w, fk_w, extra], axis=1)
    row2 = jnp.pad(row2, ((0, 0), (0, d - row2.shape[1])))
    return jnp.concatenate([pre, post, row2, jnp.zeros((5, d), F32)], axis=0)


def _unpack_small(p, nh):
    o = 3 * nh
    return dict(pre=p[0:1], post=p[1:2], a_log=p[2:3, 0:nh], dt_bias=p[2:3, nh:2 * nh], f_bias=p[2:3, 2 * nh:o],
                gdn_w=p[2:3, o:o + HEAD_DIM], fq_w=p[2:3, o + HEAD_DIM:o + 2 * HEAD_DIM],
                fk_w=p[2:3, o + 2 * HEAD_DIM:o + 3 * HEAD_DIM], extra=p[2, o + 3 * HEAD_DIM])


def kernel(x, meta_tokens, pre_norm_w, w_in, conv_w, a_log, dt_bias, gdn_norm_w, fox_q_norm_w, fox_k_norm_w, fox_f_bias, w_out, post_norm_w, loss_target, m_meta_tokens, m_pre_norm_w, m_w_in, m_conv_w, m_a_log, m_dt_bias, m_gdn_norm_w, m_fox_q_norm_w, m_fox_k_norm_w, m_fox_f_bias, m_w_out, m_post_norm_w, v_meta_tokens, v_pre_norm_w, v_w_in, v_conv_w, v_a_log, v_dt_bias, v_gdn_norm_w, v_fox_q_norm_w, v_fox_k_norm_w, v_fox_f_bias, v_w_out, v_post_norm_w):
    nh = a_log.shape[1]
    d = x.shape[-1]
    w = nh * HEAD_DIM
    zero = jnp.zeros((1, 1), F32)

    wg, cg, mg = _all_gather([_cast_bf16_column_major(w_in, "cast_w_in"), conv_w[0].T, meta_tokens], "gather_weights")
    wfull = _relayout_w_in(wg, nh, 256)
    conv_wt = cg.transpose(1, 0, 2).reshape(CONV_WIDTH, 3 * w)
    meta_full = mg.transpose(1, 0, 2).reshape(N_META, d)
    late_weights = (_gather_rider([_cast_bf16(w_out[0], 256, "cast_w_out")]), lambda got: got[0].reshape(2 * w, d))
    core = lax.axis_index("c")
    dev = 4 * lax.axis_index("x") + 2 * lax.axis_index("y") + core
    core_arr = jnp.reshape(core, (1,)).astype(jnp.int32)

    out_parts = lambda dw_out: dw_out.reshape(N_DEV, 2 * w // N_DEV, d)
    g = _layer_grads(
        x[0], loss_target[0], meta_full, pre_norm_w, wfull, conv_wt, a_log, dt_bias, gdn_norm_w,
        fox_q_norm_w, fox_k_norm_w, fox_f_bias, None, post_norm_w, late_weights=late_weights,
        w_out_grads=(lambda dw_out: _pair_rider(parts=out_parts(dw_out)),
                     lambda dw_out, got: _chip_rider(_pair_sum(out_parts(dw_out), got[0], core_arr, 256, "pair_sum_w_out"))))
    p_out = g["w_out_parts"][0]
    xn, dproj, half = g["xn"], g["dproj"], d // 2
    dw_a = _matmul(xn, dproj, "tn", MM_TILE, BF16, "dw_in_a", a_cols=(half, 0))
    dw_b, got_a = _matmul(xn, dproj, "tn", MM_TILE, BF16, "dw_in_b", _pair_rider(dw_rows=dw_a, nh=nh), a_cols=(half, 1))
    sums_a = _relayout_pair_sum(dw_a, got_a[0], got_a[1], core_arr, nh, 128, "relayout_pair_sum_a")
    flight_a, token_a = _chip_exchange_start(sums_a, "chip_exchange_a_start")
    dxn, (got_b,) = _dxn(dproj, wfull, [_pair_rider(dw_rows=dw_b, nh=nh, after=token_a)], MM_TILE, "dxn")
    sums_b = _relayout_pair_sum(dw_b, got_b[0], got_b[1], core_arr, nh, 128, "relayout_pair_sum_b")
    flight_b, token_b = _chip_exchange_start(sums_b, "chip_exchange_b_start")
    (grad_x, dmeta, dpre_w), _ = _prenorm_bwd(dxn, x[0], meta_full, pre_norm_w + token_b[0:1, 0:1], g["dy"])
    small = _pack_small(d, dpre_w, g["post_w"], g["a_log"], g["dt_bias"], g["f_bias"], g["gdn_norm_w"], g["fq_w"],
                        g["fk_w"], g["loss"])
    a_conv, a_meta, p_small = _all_gather([g["conv_wt"], dmeta, small], "gather_small_grads")
    p_conv = lax.dynamic_slice_in_dim(a_conv, dev * conv_w.shape[1], conv_w.shape[1], axis=2).transpose(0, 2, 1)
    p_meta = lax.dynamic_slice_in_dim(a_meta, dev * meta_tokens.shape[1], meta_tokens.shape[1], axis=2)

    r_out = _adamw(w_out[0], p_out, m_w_out[0], v_w_out[0], 64, "adamw_w_out")
    r_conv = _adamw(conv_w[0], p_conv, m_conv_w[0], v_conv_w[0], conv_w.shape[1], "adamw_conv_w")
    r_meta = _adamw(meta_tokens, p_meta, m_meta_tokens, v_meta_tokens, N_META, "adamw_meta")
    pk = lambda pre, post, a, dt, gw, fq, fk, fb: _pack_small(d, pre, post, a, dt, fb, gw, fq, fk, zero)
    r_small = _adamw(
        pk(pre_norm_w, post_norm_w, a_log, dt_bias, gdn_norm_w, fox_q_norm_w, fox_k_norm_w, fox_f_bias), p_small,
        pk(m_pre_norm_w, m_post_norm_w, m_a_log, m_dt_bias, m_gdn_norm_w, m_fox_q_norm_w, m_fox_k_norm_w, m_fox_f_bias),
        pk(v_pre_norm_w, v_post_norm_w, v_a_log, v_dt_bias, v_gdn_norm_w, v_fox_q_norm_w, v_fox_k_norm_w, v_fox_f_bias),
        8, "adamw_small")
    p_in = _chip_exchange_wait([flight_a, flight_b], r_small[0], "chip_exchange_wait")
    r_in = _adamw_column_major(w_in, p_in, m_w_in, v_w_in, "adamw_w_in")

    sm = [_unpack_small(r, nh) for r in r_small]
    outs = []
    for i in range(4):
        s = sm[i]
        outs += [r_meta[i], s["pre"], r_in[i], r_conv[i][None], s["a_log"], s["dt_bias"], s["gdn_w"], s["fq_w"],
                 s["fk_w"], s["f_bias"], r_out[i][None], s["post"]]
    return (sm[0]["extra"], grad_x[None], *outs)
```

```python
import jax
import jax.numpy as jnp
from jax import lax
from jax.experimental import pallas as pl
from jax.experimental.pallas import tpu as pltpu

F32, BF16 = jnp.float32, jnp.bfloat16
HEAD_DIM = 128
N_META = 16
CONV_WIDTH = 4
CHUNK = 128
Q_BLOCK = 128
LANES = 128
EPS = 1e-6
PAD_ROWS = Q_BLOCK - N_META
N_DEV = 8
N_CHIP = 4
VMEM_LIMIT = 56 * 1024 * 1024
NEG = -1e30
NARROW = 2 * LANES
MM_TILE = 6 * LANES
DW_IN_PIECES = 4

ADAM_LR, ADAM_B1, ADAM_B2, ADAM_EPS, ADAM_WD, ADAM_STEP = 0.001, 0.9, 0.999, 1e-08, 0.01, 10

_DN = {"nn": (((1,), (0,)), ((), ())), "nt": (((1,), (1,)), ((), ())), "tn": (((0,), (0,)), ((), ()))}
_DN3 = {"nn": (((2,), (1,)), ((0,), (0,))), "nt": (((2,), (2,)), ((0,), (0,))), "tn": (((1,), (1,)), ((0,), (0,)))}
_ANY = pl.BlockSpec(memory_space=pl.ANY)
_MESH = pl.DeviceIdType.MESH


def _cp(*sem):
    return pltpu.CompilerParams(dimension_semantics=sem, vmem_limit_bytes=VMEM_LIMIT)


def _dot(a, b, dims="nn", prec=None):
    return lax.dot_general(a, b, _DN[dims], precision=prec, preferred_element_type=F32)


def _bdot(a, b, dims="nn"):
    return _dot(a.astype(BF16), b.astype(BF16), dims)


def _hdot(a, b, dims="nn"):
    return _dot(a, b, dims, prec=lax.Precision.HIGHEST)


def _dot3(a, b, dims="nn"):
    return lax.dot_general(a, b, _DN3[dims], preferred_element_type=F32)


def _bdot3(a, b, dims="nn"):
    return _dot3(a.astype(BF16), b.astype(BF16), dims)


def _split(a):
    hi = a.astype(BF16)
    return hi, (a - hi.astype(F32)).astype(BF16)


def _iota(shape, dim):
    return lax.broadcasted_iota(jnp.int32, shape, dim)


def _sigmoid(z):
    return 1.0 / (1.0 + jnp.exp(-z))


def _softplus(z):
    e = jnp.exp(-jnp.abs(z))
    u = 1.0 + e
    l1p = jnp.where(u == 1.0, e, jnp.log(u) * (e / jnp.where(u == 1.0, 1.0, u - 1.0)))
    return jnp.maximum(z, 0.0) + l1p


def _silu_and_grad(z):
    s = _sigmoid(z)
    return z * s, s * (1.0 + z * (1.0 - s))


def _rms(x):
    return lax.rsqrt(jnp.mean(x * x, axis=-1, keepdims=True) + EPS)


def _h_tile(i, x_ref, meta_ref):
    first = jnp.concatenate([jnp.zeros((PAD_ROWS, x_ref.shape[1]), F32), meta_ref[...]], axis=0)
    return jnp.where(i == 0, first, x_ref[...])


def _x_rows(d):
    return pl.BlockSpec((Q_BLOCK, d), lambda i: (jnp.maximum(i - 1, 0), 0))


def _prenorm(x, meta, w):
    seq, d = x.shape
    lp = seq + Q_BLOCK

    def body(x_ref, m_ref, w_ref, o_ref):
        h = _h_tile(pl.program_id(0), x_ref, m_ref)
        o_ref[...] = (h * _rms(h) * w_ref[...]).astype(BF16)

    return pl.pallas_call(
        body, grid=(lp // Q_BLOCK,),
        in_specs=[_x_rows(d), pl.BlockSpec((N_META, d), lambda i: (0, 0)), pl.BlockSpec((1, d), lambda i: (0, 0))],
        out_specs=pl.BlockSpec((Q_BLOCK, d), lambda i: (i, 0)),
        out_shape=jax.ShapeDtypeStruct((lp, d), BF16), name="prenorm", compiler_params=_cp("parallel"))(x, meta, w)


def _tile(n, want):
    return max(t for t in range(LANES, want + 1, LANES) if n % t == 0)


class _Rider:
    def __init__(self, inputs, out_shapes, scratch, aliases, make):
        self.inputs, self.out_shapes, self.scratch, self.aliases, self.make = inputs, out_shapes, scratch, aliases, make


def _hosted_call(body, riders, n_in, n_out, n_scratch, *, in_specs, out_specs, out_shape, scratch_shapes=(), aliases=None,
                 **kw):
    riders = [r for r in riders if r is not None]
    r_in = [len(r.inputs) for r in riders]
    r_out = [len(r.out_shapes) for r in riders]
    r_scr = [len(r.scratch) for r in riders]
    al = dict(aliases or {})
    for k, r in enumerate(riders):
        al.update({n_in + sum(r_in[:k]) + i: n_out + sum(r_out[:k]) + o for i, o in r.aliases.items()})

    def full_body(*refs):
        ins, rest = refs[:n_in + sum(r_in)], refs[n_in + sum(r_in):]
        outs, scr = rest[:n_out + sum(r_out)], rest[n_out + sum(r_out):]
        hooks = [r.make(ins[n_in + sum(r_in[:k]):n_in + sum(r_in[:k + 1])], outs[n_out + sum(r_out[:k]):n_out + sum(r_out[:k + 1])],
                        scr[n_scratch + sum(r_scr[:k]):n_scratch + sum(r_scr[:k + 1])]) for k, r in enumerate(riders)]

        def start():
            for h in hooks:
                h[0]()

        def finish():
            for h in hooks:
                h[1]()

        body(start, finish, *ins[:n_in], *outs[:n_out], *scr[:n_scratch])

    call = pl.pallas_call(
        full_body, in_specs=list(in_specs) + [_ANY] * sum(r_in), out_specs=list(out_specs) + [_ANY] * sum(r_out),
        out_shape=list(out_shape) + [s for r in riders for s in r.out_shapes],
        scratch_shapes=list(scratch_shapes) + [s for r in riders for s in r.scratch], input_output_aliases=al, **kw)

    def run(*args):
        res = call(*args, *[t for r in riders for t in r.inputs])
        return res[:n_out], [res[n_out + sum(r_out[:k]):n_out + sum(r_out[:k + 1])] for k in range(len(riders))]

    return run


def _matmul(a, b, dims, tn, out_dtype, name, rider=None, a_cols=None):
    a_shape = a.shape if a_cols is None else (a.shape[0], a_cols[0])
    a_index = 0 if a_cols is None else a_cols[1]
    m = a_shape[1] if dims == "tn" else a_shape[0]
    n = b.shape[0] if dims == "nt" else b.shape[1]
    kdim = b.shape[1] if dims == "nt" else b.shape[0]
    tn = _tile(n, tn)
    steps = n // tn
    b_spec = pl.BlockSpec((tn, kdim), lambda j: (j, 0)) if dims == "nt" else pl.BlockSpec((kdim, tn), lambda j: (0, j))

    def body(start, finish, a_ref, b_ref, o_ref):
        pl.when(pl.program_id(0) == 0)(start)
        o_ref[...] = _dot(a_ref[...], b_ref[...], dims).astype(out_dtype)
        pl.when(pl.program_id(0) == steps - 1)(finish)

    (out,), got = _hosted_call(
        body, [rider], 2, 1, 0, grid=(steps,), in_specs=[pl.BlockSpec(a_shape, lambda j: (0, a_index)), b_spec],
        out_specs=[pl.BlockSpec((m, tn), lambda j: (0, j))], out_shape=[jax.ShapeDtypeStruct((m, n), out_dtype)],
        name=name, compiler_params=_cp("parallel" if rider is None else "arbitrary"))(a, b)
    return out if rider is None else (out, got[0])


def _chip_rider(sums):
    def make(ins, outs, scratch):
        local, remote = _chip_exchange_copies(ins[0], outs[0], *scratch)

        def start():
            for cp in [local] + remote:
                cp.start()

        def finish():
            local.wait()
            for cp in remote:
                cp.wait_send()
                cp.wait_recv()

        return start, finish

    return _Rider([sums], [jax.ShapeDtypeStruct(sums.shape, sums.dtype)],
                  [pltpu.SemaphoreType.DMA((N_CHIP - 1,)), pltpu.SemaphoreType.DMA((N_CHIP - 1,)), pltpu.SemaphoreType.DMA((1,))],
                  {}, make)


_HBM = pl.BlockSpec(memory_space=pltpu.HBM)
_SEM = pl.BlockSpec(memory_space=pltpu.SEMAPHORE)


def _chip_exchange_copies(sums_ref, land_ref, send_sems, recv_sems, local_sem):
    x, y, core = lax.axis_index("x"), lax.axis_index("y"), lax.axis_index("c")
    mine = 2 * x + y
    local = pltpu.make_async_copy(sums_ref.at[mine], land_ref.at[mine], local_sem.at[0])
    remote = []
    for k in range(1, N_CHIP):
        px = 1 - x if k & 2 else x
        py = 1 - y if k & 1 else y
        remote.append(pltpu.make_async_remote_copy(
            src_ref=sums_ref.at[2 * px + py], dst_ref=land_ref.at[mine], send_sem=send_sems.at[k - 1],
            recv_sem=recv_sems.at[k - 1], device_id=(px, py, core), device_id_type=_MESH))
    return local, remote


def _chip_exchange_start(sums, name):
    def body(s_ref, send_sems, recv_sems, local_sem, s_thru, land_ref, token):
        local, remote = _chip_exchange_copies(s_ref, land_ref, send_sems, recv_sems, local_sem)
        for cp in [local] + remote:
            cp.start()
        token[...] = jnp.zeros_like(token)

    *flight, token = pl.pallas_call(
        body, name=name,
        out_shape=(pltpu.SemaphoreType.DMA((N_CHIP - 1,)), pltpu.SemaphoreType.DMA((N_CHIP - 1,)), pltpu.SemaphoreType.DMA((1,)),
                   pltpu.HBM(sums.shape, sums.dtype), pltpu.HBM(sums.shape, sums.dtype), jax.ShapeDtypeStruct((8, LANES), F32)),
        in_specs=(_HBM,), out_specs=(_SEM, _SEM, _SEM, _HBM, _HBM, pl.BlockSpec(memory_space=pltpu.VMEM)),
        input_output_aliases={0: 3},
        compiler_params=pltpu.CompilerParams(has_side_effects=pltpu.SideEffectType.DATAFLOW_SIDE_EFFECTING))(
            pltpu.with_memory_space_constraint(sums, pltpu.HBM))
    return flight, token


def _chip_exchange_wait(flights, after, name):
    n = len(flights)

    def body(*refs):
        for i in range(n):
            s_ref, land_ref = refs[2 * i:2 * i + 2]
            local, remote = _chip_exchange_copies(s_ref, land_ref, *refs[2 * n + 3 * i:2 * n + 3 * i + 3])
            local.wait()
            for cp in remote:
                cp.wait_send()
                cp.wait_recv()

    buffers = [b for f in flights for b in f[3:]]
    res = pl.pallas_call(
        body, name=name, out_shape=tuple(pltpu.HBM(b.shape, b.dtype) for b in buffers),
        in_specs=(_HBM,) * (2 * n) + (_SEM,) * (3 * n) + (_ANY,), out_specs=(_HBM,) * (2 * n),
        input_output_aliases={i: i for i in range(2 * n)},
        compiler_params=pltpu.CompilerParams(has_side_effects=pltpu.SideEffectType.DATAFLOW_SIDE_EFFECTING))(
            *buffers, *[s for f in flights for s in f[:3]], after)
    return res[1::2]


def _dxn(dproj, wfull, riders, tk, name):
    m, k = dproj.shape
    n = wfull.shape[0]
    tk = _tile(k, tk)
    steps = k // tk

    def body(start, finish, a_ref, b_ref, o_ref):
        j = pl.program_id(0)

        @pl.when(j == 0)
        def _():
            start()
            o_ref[...] = jnp.zeros_like(o_ref)
        o_ref[...] += _dot(a_ref[...], b_ref[...], "nt")
        pl.when(j == steps - 1)(finish)

    (dxn,), got = _hosted_call(
        body, riders, 2, 1, 0, grid=(steps,),
        in_specs=[pl.BlockSpec((m, tk), lambda j: (0, j)), pl.BlockSpec((n, tk), lambda j: (0, j))],
        out_specs=[pl.BlockSpec((m, n), lambda j: (0, 0))], out_shape=[jax.ShapeDtypeStruct((m, n), F32)],
        name=name, compiler_params=_cp("arbitrary"))(dproj, wfull)
    return dxn, got


def _conv_taps(x, w):
    c = x * w[CONV_WIDTH - 1:CONV_WIDTH, :]
    for j in range(CONV_WIDTH - 1):
        c = c + pltpu.roll(x, CONV_WIDTH - 1 - j, 0) * w[j:j + 1, :]
    return c


def _gdn_prep(proj, conv_wt, nh):
    lp = proj.shape[0]
    scale = HEAD_DIM ** -0.5

    def body(x_ref, w_ref, o_ref):
        which = pl.program_id(0) // nh
        c = _conv_taps(x_ref[...], w_ref[...])
        s = c * _sigmoid(c)
        r = lax.rsqrt(jnp.sum(s * s, axis=-1, keepdims=True) + EPS)
        f = jnp.where(which == 0, r * scale, jnp.where(which == 1, r, 1.0))
        o_ref[...] = jnp.where(_iota(s.shape, 0) >= PAD_ROWS, s * f, 0.0)

    return pl.pallas_call(
        body, grid=(3 * nh,),
        in_specs=[pl.BlockSpec((lp, LANES), lambda s: (0, s)), pl.BlockSpec((CONV_WIDTH, LANES), lambda s: (0, s))],
        out_specs=pl.BlockSpec((lp, LANES), lambda s: (0, s)),
        out_shape=jax.ShapeDtypeStruct((lp, 3 * nh * HEAD_DIM), F32), name="gdn_prep",
        compiler_params=_cp("parallel"))(proj, conv_wt)


def _gdn_prep_bwd(proj, conv_wt, dq, dk, dv, dproj, nh):
    lp = proj.shape[0]
    scale = HEAD_DIM ** -0.5
    part = lambda p: pl.BlockSpec((lp, LANES), lambda s: (0, jnp.clip(s - p * nh, 0, nh - 1)))

    def body(x_ref, w_ref, dq_ref, dk_ref, dv_ref, _, dx_ref, dw_ref):
        which = pl.program_id(0) // nh
        x = x_ref[...]
        w = w_ref[...]
        c = _conv_taps(x, w)
        sg = _sigmoid(c)
        s = c * sg
        r = lax.rsqrt(jnp.sum(s * s, axis=-1, keepdims=True) + EPS)
        dy = jnp.where(which == 0, dq_ref[...], jnp.where(which == 1, dk_ref[...], dv_ref[...]))
        dy = jnp.where(_iota(s.shape, 0) >= PAD_ROWS, dy, 0.0)
        y0 = s * r
        dy0 = dy * jnp.where(which == 0, scale, 1.0)
        ds_n = r * (dy0 - y0 * jnp.sum(dy0 * y0, axis=-1, keepdims=True))
        ds = jnp.where(which == 2, dy, ds_n)
        dc = ds * (sg * (1.0 + c * (1.0 - sg)))
        dx = dc * w[CONV_WIDTH - 1:CONV_WIDTH, :]
        rows = [jnp.sum(dc * x, axis=0, keepdims=True)]
        for j in range(CONV_WIDTH - 2, -1, -1):
            sh = CONV_WIDTH - 1 - j
            dx = dx + pltpu.roll(dc, lp - sh, 0) * w[j:j + 1, :]
            rows.insert(0, jnp.sum(dc * pltpu.roll(x, sh, 0), axis=0, keepdims=True))
        dx_ref[...] = dx.astype(BF16)
        dw_ref[...] = jnp.concatenate(rows, axis=0)

    strip = pl.BlockSpec((lp, LANES), lambda s: (0, s))
    taps = pl.BlockSpec((CONV_WIDTH, LANES), lambda s: (0, s))
    return pl.pallas_call(
        body, grid=(3 * nh,), in_specs=[strip, taps, part(0), part(1), part(2), _ANY], out_specs=[strip, taps],
        out_shape=[jax.ShapeDtypeStruct(dproj.shape, BF16), jax.ShapeDtypeStruct((CONV_WIDTH, 3 * nh * HEAD_DIM), F32)],
        input_output_aliases={5: 0}, name="gdn_prep_bwd", compiler_params=_cp("parallel"))(proj, conv_wt, dq, dk, dv, dproj)


def _gates(proj, bias_row, nega_row, nh):
    lp = proj.shape[0]
    nc = lp // CHUNK

    def body(p_ref, b_ref, a_ref, g_ref, gt3_ref, gtf_ref):
        lane = _iota((CHUNK, LANES), 1)
        tri = (_iota((CHUNK, CHUNK), 0) >= _iota((CHUNK, CHUNK), 1)).astype(F32)

        def step(n, carry):
            r0 = pl.multiple_of(n * CHUNK, CHUNK)
            z = p_ref[pl.ds(r0, CHUNK), :] + b_ref[...]
            base = jnp.where(lane < nh, _sigmoid(z),
                             jnp.where(lane < 2 * nh, a_ref[...] * _softplus(z),
                                       jnp.where(lane < 3 * nh, -_softplus(-z), 0.0)))
            base = jnp.where(r0 + _iota((CHUNK, LANES), 0) >= PAD_ROWS, base, 0.0)
            cs = _hdot(tri, base)
            run = jnp.where((lane >= 2 * nh) & (lane < 3 * nh), cs + carry, cs)
            sh = pltpu.roll(run, 2 * nh, 1)
            out = base + jnp.where((lane >= 3 * nh) & (lane < 5 * nh), sh, 0.0)
            g_ref[pl.ds(r0, CHUNK), :] = out
            gt3_ref[n] = out.T
            return carry + cs[CHUNK - 1:CHUNK, :]

        lax.fori_loop(0, nc, step, jnp.zeros((1, LANES), F32))
        gtf_ref[...] = g_ref[...].T

    vec = pl.BlockSpec((1, LANES), lambda i: (0, 0))
    return pl.pallas_call(
        body, grid=(1,), in_specs=[pl.BlockSpec((lp, LANES), lambda i: (0, 8 * nh)), vec, vec],
        out_specs=[pl.BlockSpec((lp, LANES), lambda i: (0, 0)), pl.BlockSpec((nc, LANES, CHUNK), lambda i: (0, 0, 0)),
                   pl.BlockSpec((LANES, lp), lambda i: (0, 0))],
        out_shape=[jax.ShapeDtypeStruct((lp, LANES), F32), jax.ShapeDtypeStruct((nc, LANES, CHUNK), F32),
                   jax.ShapeDtypeStruct((LANES, lp), F32)],
        name="gates", compiler_params=_cp("arbitrary"))(proj, bias_row, nega_row)


def _gates_bwd(proj, bias_row, nega_row, gates, dgate_gdn, dc_t, dproj, nh):
    lp = proj.shape[0]
    nc = lp // CHUNK

    def body(p_ref, b_ref, a_ref, g_ref, dg_ref, dc_ref, _, dz_ref, sm_ref, dct_scr):
        lane = _iota((CHUNK, LANES), 1)
        triu = (_iota((CHUNK, CHUNK), 0) <= _iota((CHUNK, CHUNK), 1)).astype(F32)
        dct_scr[...] = dc_ref[...].T
        sm_ref[...] = jnp.zeros_like(sm_ref)
        dz_ref[:, LANES:] = jnp.zeros((lp, NARROW - LANES), BF16)

        def step(i, carry):
            n = nc - 1 - i
            r0 = pl.multiple_of(n * CHUNK, CHUNK)
            z = p_ref[pl.ds(r0, CHUNK), :] + b_ref[...]
            gt = g_ref[pl.ds(r0, CHUNK), :]
            dgd = dg_ref[pl.ds(r0, CHUNK), :]
            dch = dct_scr[pl.ds(r0, CHUNK), :]
            rc = _hdot(triu, dch) + carry
            sg = _sigmoid(z)
            dz = jnp.where(lane < nh, dgd * sg * (1.0 - sg),
                           jnp.where(lane < 2 * nh, dgd * a_ref[...] * sg,
                                     jnp.where(lane < 3 * nh, rc * (1.0 - sg), 0.0)))
            dz = jnp.where(r0 + _iota((CHUNK, LANES), 0) >= PAD_ROWS, dz, 0.0)
            dz_ref[pl.ds(r0, CHUNK), 0:LANES] = dz.astype(BF16)
            sm_ref[0:1, :] += jnp.sum(dz, axis=0, keepdims=True)
            sm_ref[1:2, :] += jnp.sum(jnp.where((lane >= nh) & (lane < 2 * nh), dgd * gt, 0.0), axis=0, keepdims=True)
            return carry + jnp.sum(dch, axis=0, keepdims=True)

        lax.fori_loop(0, nc, step, jnp.zeros((1, LANES), F32))

    vec = pl.BlockSpec((1, LANES), lambda i: (0, 0))
    full = pl.BlockSpec((lp, LANES), lambda i: (0, 0))
    last = pl.BlockSpec((lp, LANES), lambda i: (0, 8 * nh))
    tail = pl.BlockSpec((lp, NARROW), lambda i: (0, 8 * nh * LANES // NARROW))
    return pl.pallas_call(
        body, grid=(1,), in_specs=[last, vec, vec, full, full, pl.BlockSpec((LANES, lp), lambda i: (0, 0)), _ANY],
        out_specs=[tail, pl.BlockSpec((8, LANES), lambda i: (0, 0))],
        out_shape=[jax.ShapeDtypeStruct(dproj.shape, BF16), jax.ShapeDtypeStruct((8, LANES), F32)],
        scratch_shapes=[pltpu.VMEM((lp, LANES), F32)], input_output_aliases={6: 0},
        name="gates_bwd", compiler_params=_cp("arbitrary"))(proj, bias_row, nega_row, gates, dgate_gdn, dc_t, dproj)


def _tri_inv(a):
    t = jnp.where(_iota(a.shape, 1) == _iota(a.shape, 2), 1.0, 0.0) - a
    p = a
    for _ in range(CHUNK.bit_length() - 2):
        ph, pw = _split(p)
        p = _dot3(ph, ph) + (_dot3(ph, pw) + _dot3(pw, ph))
        ph, pw = _split(p)
        th, tw = _split(t)
        t = t + (_dot3(th, ph) + (_dot3(th, pw) + _dot3(tw, ph)))
    return t


def _gdn_chunk(q, k, v, beta, gc, gr, t=None):
    ii, jj = _iota((1, CHUNK, CHUNK), 1), _iota((1, CHUNK, CHUNK), 2)
    causal, strict = ii >= jj, ii > jj
    dm = jnp.where(causal, jnp.exp(jnp.where(causal, gc - gr, 0.0)), 0.0)
    kk = _bdot3(k, k, "nt")
    a = jnp.where(strict, beta * kk * dm, 0.0)
    if t is None:
        t = _tri_inv(a)
    eg = jnp.exp(gc)
    glast = gc[:, CHUNK - 1:CHUNK, :]
    ekd = jnp.exp(glast - gc)
    bv = beta * v
    bk = (beta * eg) * k
    ub = _bdot3(t, jnp.concatenate([bv, bk], axis=2))
    qk = _bdot3(q, k, "nt")
    return dict(causal=causal, strict=strict, dm=dm, kk=kk, a=a, t=t, eg=eg, ekd=ekd, bv=bv, bk=bk,
                u=ub[:, :, :HEAD_DIM], w=ub[:, :, HEAD_DIM:], qk=qk, aqk=jnp.where(causal, qk * dm, 0.0),
                q_dec=q * eg, k_dec=k * ekd, decay=jnp.exp(glast))


def _heads(ref, nh):
    return jnp.stack([ref[:, h * HEAD_DIM:(h + 1) * HEAD_DIM] for h in range(nh)], axis=0)


def _gdn_chunk_inputs(q_ref, k_ref, v_ref, g, gt, nh):
    col = lambda o: jnp.stack([g[:, o + h:o + h + 1] for h in range(nh)], axis=0)
    gr = jnp.stack([gt[3 * nh + h:3 * nh + h + 1, :] for h in range(nh)], axis=0)
    return _heads(q_ref, nh), _heads(k_ref, nh), _heads(v_ref, nh), col(0), col(3 * nh), gr


def _gdn_fwd(qkv, gates, gt3, nh, rider=None):
    lp = qkv.shape[0]
    nc = lp // CHUNK
    w = nh * HEAD_DIM

    def body(start, finish, q_ref, k_ref, v_ref, g_ref, gt_ref, o_ref, sall_ref, tall_ref, s_scr):
        @pl.when(pl.program_id(0) == 0)
        def _():
            start()
            s_scr[...] = jnp.zeros_like(s_scr)
        c = _gdn_chunk(*_gdn_chunk_inputs(q_ref, k_ref, v_ref, g_ref[...], gt_ref[0], nh))
        s = s_scr[...]
        sall_ref[0] = s
        tall_ref[0] = c["t"]
        v_new = c["u"] - _bdot3(c["w"], s)
        o = _bdot3(c["q_dec"], s) + _bdot3(c["aqk"], v_new)
        s_scr[...] = s * c["decay"] + _bdot3(c["k_dec"], v_new, "tn")
        for h in range(nh):
            o_ref[:, h * HEAD_DIM:(h + 1) * HEAD_DIM] = o[h]
        pl.when(pl.program_id(0) == nc - 1)(finish)

    outs, got = _hosted_call(
        body, [rider], 5, 3, 1, grid=(nc,),
        in_specs=[pl.BlockSpec((CHUNK, w), lambda n: (n, 0)), pl.BlockSpec((CHUNK, w), lambda n: (n, 1)),
                  pl.BlockSpec((CHUNK, w), lambda n: (n, 2)), pl.BlockSpec((CHUNK, LANES), lambda n: (n, 0)),
                  pl.BlockSpec((1, LANES, CHUNK), lambda n: (n, 0, 0))],
        out_specs=[pl.BlockSpec((CHUNK, w), lambda n: (n, 0)),
                   pl.BlockSpec((1, nh, HEAD_DIM, HEAD_DIM), lambda n: (n, 0, 0, 0)),
                   pl.BlockSpec((1, nh, CHUNK, CHUNK), lambda n: (n, 0, 0, 0))],
        out_shape=[jax.ShapeDtypeStruct((lp, w), F32), jax.ShapeDtypeStruct((nc, nh, HEAD_DIM, HEAD_DIM), F32),
                   jax.ShapeDtypeStruct((nc, nh, CHUNK, CHUNK), F32)],
        scratch_shapes=[pltpu.VMEM((nh, HEAD_DIM, HEAD_DIM), F32)],
        name="gdn_fwd", compiler_params=_cp("arbitrary"))(qkv, qkv, qkv, gates, gt3)
    return outs, (got[0] if got else None)


def _gdn_bwd(qkv, gates, gt3, s_all, t_all, do, nh, rider=None):
    lp = qkv.shape[0]
    nc = lp // CHUNK
    w = nh * HEAD_DIM
    rev = lambda n: nc - 1 - n

    def body(start, finish, q_ref, k_ref, v_ref, g_ref, gt_ref, s_ref, t_ref, do_ref, dq_ref, dk_ref, dv_ref, dg_ref, ds_scr):
        @pl.when(pl.program_id(0) == 0)
        def _():
            start()
            ds_scr[...] = jnp.zeros_like(ds_scr)
        q, k, v, beta, gc, gr = _gdn_chunk_inputs(q_ref, k_ref, v_ref, g_ref[...], gt_ref[0], nh)
        c = _gdn_chunk(q, k, v, beta, gc, gr, t_ref[0])
        s = s_ref[0]
        dsn = ds_scr[...]
        dout = _heads(do_ref, nh)
        v_new = c["u"] - _bdot3(c["w"], s)
        dq_dec = _bdot3(dout, s, "nt")
        daqk = jnp.where(c["causal"], _bdot3(dout, v_new, "nt"), 0.0)
        dv_new = _bdot3(c["aqk"], dout, "tn") + _bdot3(c["k_dec"], dsn)
        dk_dec = _bdot3(v_new, dsn, "nt")
        ddecay = jnp.sum(jnp.sum(dsn * s, axis=2, keepdims=True), axis=1, keepdims=True)
        dw = -_bdot3(dv_new, s, "nt")
        ds_scr[...] = _bdot3(c["q_dec"], dout, "tn") + c["decay"] * dsn - _bdot3(c["w"], dv_new, "tn")
        duw = jnp.concatenate([dv_new, dw], axis=2)
        dt = _bdot3(duw, jnp.concatenate([c["bv"], c["bk"]], axis=2), "nt")
        dbvk = _bdot3(c["t"], duw, "tn")
        dbv, dbk = dbvk[:, :, :HEAD_DIM], dbvk[:, :, HEAD_DIM:]
        da = jnp.where(c["strict"], -_bdot3(_bdot3(c["t"], dt, "tn"), c["t"], "nt"), 0.0)
        dkk = da * beta * c["dm"]
        dqk = daqk * c["dm"]
        e = da * c["a"] + daqk * c["aqk"]
        dq = dq_dec * c["eg"] + _bdot3(dqk, k)
        dk = (dk_dec * c["ekd"] + _bdot3(dkk, k) + _bdot3(dkk, k, "tn") + _bdot3(dqk, q, "tn")
              + (beta * c["eg"]) * dbk)
        dv = beta * dbv
        rs = lambda x: jnp.sum(x, axis=2, keepdims=True)
        dbeta = rs(dbv * v) + c["eg"] * rs(dbk * k) + rs(da * c["kk"] * c["dm"])
        kd_term = rs(dk_dec * c["k_dec"])
        eh, ew = _split(e)
        ones = jnp.ones((nh, CHUNK, LANES), BF16)
        col_sums = (_dot3(eh, ones, "tn") + _dot3(ew, ones, "tn"))[:, :, 0:1]
        dg_cum = rs(dq_dec * c["q_dec"]) - kd_term + rs(dbk * c["bk"]) + rs(e) - col_sums
        last = jnp.sum(kd_term, axis=1, keepdims=True) + ddecay * c["decay"]
        dg_cum = dg_cum + jnp.where(_iota((1, CHUNK, 1), 1) == CHUNK - 1, last, 0.0)
        lane = _iota((CHUNK, LANES), 1)
        acc = jnp.zeros((CHUNK, LANES), F32)
        for h in range(nh):
            sl = slice(h * HEAD_DIM, (h + 1) * HEAD_DIM)
            dq_ref[:, sl] = dq[h]
            dk_ref[:, sl] = dk[h]
            dv_ref[:, sl] = dv[h]
            acc = acc + jnp.where(lane == h, dbeta[h], 0.0) + jnp.where(lane == nh + h, dg_cum[h], 0.0)
        triu = (_iota((CHUNK, CHUNK), 0) <= _iota((CHUNK, CHUNK), 1)).astype(F32)
        dg_ref[...] = jnp.where(lane < nh, acc, _hdot(triu, acc))
        pl.when(pl.program_id(0) == nc - 1)(finish)

    outs, got = _hosted_call(
        body, [rider], 8, 4, 1, grid=(nc,),
        in_specs=[pl.BlockSpec((CHUNK, w), lambda n: (rev(n), 0)), pl.BlockSpec((CHUNK, w), lambda n: (rev(n), 1)),
                  pl.BlockSpec((CHUNK, w), lambda n: (rev(n), 2)), pl.BlockSpec((CHUNK, LANES), lambda n: (rev(n), 0)),
                  pl.BlockSpec((1, LANES, CHUNK), lambda n: (rev(n), 0, 0)),
                  pl.BlockSpec((1, nh, HEAD_DIM, HEAD_DIM), lambda n: (rev(n), 0, 0, 0)),
                  pl.BlockSpec((1, nh, CHUNK, CHUNK), lambda n: (rev(n), 0, 0, 0)),
                  pl.BlockSpec((CHUNK, w), lambda n: (rev(n), 0))],
        out_specs=[pl.BlockSpec((CHUNK, w), lambda n: (rev(n), 0))] * 3 + [pl.BlockSpec((CHUNK, LANES), lambda n: (rev(n), 0))],
        out_shape=[jax.ShapeDtypeStruct((lp, w), F32)] * 3 + [jax.ShapeDtypeStruct((lp, LANES), F32)],
        scratch_shapes=[pltpu.VMEM((nh, HEAD_DIM, HEAD_DIM), F32)],
        name="gdn_bwd", compiler_params=_cp("arbitrary"))(qkv, qkv, qkv, gates, gt3, s_all, t_all, do)
    return outs, (got[0] if got else None)


def _merge_gdn(o_gdn, proj, norm_w, nh):
    lp = o_gdn.shape[0]

    def body(o_ref, z_ref, w_ref, m_ref):
        o = o_ref[...]
        z = z_ref[...]
        m_ref[...] = (o * _rms(o) * w_ref[...] * (z * _sigmoid(z))).astype(BF16)

    return pl.pallas_call(
        body, grid=(nh,),
        in_specs=[pl.BlockSpec((lp, LANES), lambda s: (0, s)), pl.BlockSpec((lp, LANES), lambda s: (0, 3 * nh + s)),
                  pl.BlockSpec((1, LANES), lambda s: (0, 0))],
        out_specs=pl.BlockSpec((lp, LANES), lambda s: (0, s)),
        out_shape=jax.ShapeDtypeStruct((lp, 2 * nh * HEAD_DIM), BF16), name="merge_gdn",
        compiler_params=_cp("parallel"))(o_gdn, proj, norm_w)


def _merge_gdn_bwd(o_gdn, proj, norm_w, dmerged, nh):
    lp = o_gdn.shape[0]

    def body(o_ref, z_ref, w_ref, dm_ref, do_ref, dz_ref, dw_ref):
        o = o_ref[...]
        r = _rms(o)
        xh = o * r
        silu, dsilu = _silu_and_grad(z_ref[...])
        dm = dm_ref[...]
        dn = dm * silu
        dz_ref[...] = (dm * (xh * w_ref[...]) * dsilu).astype(BF16)
        dnw = dn * w_ref[...]
        do_ref[...] = r * (dnw - xh * jnp.mean(dnw * xh, axis=-1, keepdims=True))

        @pl.when(pl.program_id(0) == 0)
        def _():
            dw_ref[...] = jnp.zeros_like(dw_ref)
        dw_ref[...] += jnp.sum(dn * xh, axis=0, keepdims=True)

    w = nh * HEAD_DIM
    return pl.pallas_call(
        body, grid=(nh,),
        in_specs=[pl.BlockSpec((lp, LANES), lambda s: (0, s)), pl.BlockSpec((lp, LANES), lambda s: (0, 3 * nh + s)),
                  pl.BlockSpec((1, LANES), lambda s: (0, 0)), pl.BlockSpec((lp, LANES), lambda s: (0, s))],
        out_specs=[pl.BlockSpec((lp, LANES), lambda s: (0, s)), pl.BlockSpec((lp, LANES), lambda s: (0, 3 * nh + s)),
                   pl.BlockSpec((1, LANES), lambda s: (0, 0))],
        out_shape=[jax.ShapeDtypeStruct((lp, w), F32), jax.ShapeDtypeStruct((lp, 8 * w + NARROW), BF16),
                   jax.ShapeDtypeStruct((1, LANES), F32)],
        name="merge_gdn_bwd", compiler_params=_cp("arbitrary"))(o_gdn, proj, norm_w, dmerged)


def _fox_prep(proj, qk_w, nh):
    lp = proj.shape[0]

    def body(x_ref, w_ref, o_ref):
        x = x_ref[...]
        o_ref[...] = x * _rms(x) * w_ref[0]

    return pl.pallas_call(
        body, grid=(2 * nh,),
        in_specs=[pl.BlockSpec((lp, LANES), lambda s: (0, 4 * nh + s)), pl.BlockSpec((1, 1, LANES), lambda s: (s // nh, 0, 0))],
        out_specs=pl.BlockSpec((lp, LANES), lambda s: (0, s)),
        out_shape=jax.ShapeDtypeStruct((lp, 2 * nh * HEAD_DIM), F32), name="fox_prep",
        compiler_params=_cp("parallel"))(proj, qk_w)


def _fox_prep_bwd(proj, qk_w, dq, dk, dproj, nh):
    lp = proj.shape[0]
    part = lambda p: pl.BlockSpec((lp, LANES), lambda s: (0, jnp.clip(s - p * nh, 0, nh - 1)))

    def body(x_ref, w_ref, dq_ref, dk_ref, _, dx_ref, dw_ref):
        x = x_ref[...]
        r = _rms(x)
        xh = x * r
        dy = jnp.where(pl.program_id(0) < nh, dq_ref[...], dk_ref[...])
        dyw = dy * w_ref[0]
        dx_ref[...] = (r * (dyw - xh * jnp.mean(dyw * xh, axis=-1, keepdims=True))).astype(BF16)

        @pl.when(pl.program_id(0) % nh == 0)
        def _():
            dw_ref[...] = jnp.zeros_like(dw_ref)
        dw_ref[0] += jnp.sum(dy * xh, axis=0, keepdims=True)

    strip = pl.BlockSpec((lp, LANES), lambda s: (0, 4 * nh + s))
    wsp = pl.BlockSpec((1, 1, LANES), lambda s: (s // nh, 0, 0))
    return pl.pallas_call(
        body, grid=(2 * nh,), in_specs=[strip, wsp, part(0), part(1), _ANY], out_specs=[strip, wsp],
        out_shape=[jax.ShapeDtypeStruct(dproj.shape, BF16), jax.ShapeDtypeStruct((2, 1, LANES), F32)],
        input_output_aliases={4: 0}, name="fox_prep_bwd", compiler_params=_cp("arbitrary"))(proj, qk_w, dq, dk, dproj)


def _fox_probs(q, k, gates, crow, h, i, nh, lse=None):
    kl = k.shape[0]
    lane = _iota((Q_BLOCK, LANES), 1)
    ct = jnp.sum(jnp.where(lane == 4 * nh + h, gates, 0.0), axis=1, keepdims=True)
    tq, kq = _iota((Q_BLOCK, Q_BLOCK), 0), _iota((Q_BLOCK, Q_BLOCK), 1)
    qs = q * (HEAD_DIM ** -0.5)
    if i == 0:
        s = _bdot(qs, k, "nt") + (ct - crow)
        s = jnp.where((kq <= tq) & ((kq >= PAD_ROWS) | (tq < PAD_ROWS)), s, NEG)
    else:
        crow = jnp.where(_iota((1, kl), 1) < PAD_ROWS, -NEG, crow)
        s = _bdot(qs, k, "nt") + (ct - crow)
        s = jnp.concatenate([s[:, :kl - Q_BLOCK], jnp.where(kq <= tq, s[:, kl - Q_BLOCK:], NEG)], axis=1)
    if lse is not None:
        return jnp.exp(s - lse)
    m = jnp.max(s, axis=1, keepdims=True)
    p = jnp.exp(s - m)
    tot = jnp.sum(p, axis=1, keepdims=True)
    return p / tot, m + jnp.log(tot)


FOX_HEADS_PER_STEP = 2


def _fox_specs(lp, nh):
    hw = FOX_HEADS_PER_STEP * LANES
    return [pl.BlockSpec((Q_BLOCK, hw), lambda g, i: (i, g)),
            pl.BlockSpec((lp, hw), lambda g, i: (0, nh // FOX_HEADS_PER_STEP + g)),
            pl.BlockSpec((lp, hw), lambda g, i: (0, 6 * nh // FOX_HEADS_PER_STEP + g)),
            pl.BlockSpec((Q_BLOCK, LANES), lambda g, i: (i, 0)),
            pl.BlockSpec((LANES, lp), lambda g, i: (0, 0))]


def _fox_fwd(qkn, proj, gates, gtf, nh):
    lp = qkn.shape[0]

    def body(q_ref, k_ref, v_ref, g_ref, gt_ref, o_ref, lse_ref):
        g, i = pl.program_id(0), pl.program_id(1)
        for j in range(lp // Q_BLOCK):
            @pl.when(i == j)
            def _(j=j):
                kl = (j + 1) * Q_BLOCK
                for hh in range(FOX_HEADS_PER_STEP):
                    h = FOX_HEADS_PER_STEP * g + hh
                    sl = slice(hh * LANES, (hh + 1) * LANES)
                    p, lse = _fox_probs(q_ref[:, sl], k_ref[0:kl, sl], g_ref[...], gt_ref[pl.ds(4 * nh + h, 1), :][:, 0:kl],
                                        h, j, nh)
                    o_ref[:, sl] = _bdot(p, v_ref[0:kl, sl])
                    lse_ref[:, sl] = jnp.broadcast_to(lse, (Q_BLOCK, LANES))

    blk = pl.BlockSpec((Q_BLOCK, FOX_HEADS_PER_STEP * LANES), lambda g, i: (i, g))
    return pl.pallas_call(
        body, grid=(nh // FOX_HEADS_PER_STEP, lp // Q_BLOCK), in_specs=_fox_specs(lp, nh), out_specs=[blk, blk],
        out_shape=[jax.ShapeDtypeStruct((lp, nh * HEAD_DIM), F32)] * 2, name="fox_fwd",
        compiler_params=_cp("parallel", "parallel"))(qkn, qkn, proj, gates, gtf)


def _fox_bwd(qkn, proj, gates, gtf, lse, do, dproj, nh):
    lp = qkn.shape[0]
    nq = lp // Q_BLOCK
    w = nh * HEAD_DIM
    scale = HEAD_DIM ** -0.5

    def body(q_ref, k_ref, v_ref, g_ref, gt_ref, lse_ref, do_ref, _, dq_ref, dk_ref, dc_ref, dv_ref, dv_scr):
        g, i = pl.program_id(0), pl.program_id(1)

        @pl.when(i == 0)
        def _():
            dk_ref[...] = jnp.zeros_like(dk_ref)
            dv_scr[...] = jnp.zeros_like(dv_scr)
            dc_ref[...] = jnp.zeros_like(dc_ref)
        for j in range(nq):
            @pl.when(i == j)
            def _(j=j):
                kl = (j + 1) * Q_BLOCK
                for hh in range(FOX_HEADS_PER_STEP):
                    h = FOX_HEADS_PER_STEP * g + hh
                    sl = slice(hh * LANES, (hh + 1) * LANES)
                    q, k = q_ref[:, sl], k_ref[0:kl, sl]
                    p = _fox_probs(q, k, g_ref[...], gt_ref[pl.ds(4 * nh + h, 1), :][:, 0:kl], h, j, nh,
                                   lse_ref[:, sl][:, 0:1])
                    dout = do_ref[:, sl]
                    dp = _bdot(dout, v_ref[0:kl, sl], "nt")
                    ds = p * (dp - jnp.sum(p * dp, axis=1, keepdims=True))
                    dq_ref[:, sl] = _bdot(ds, k) * scale
                    dk_ref[0:kl, sl] += _bdot(ds, q * scale, "tn")
                    dv_scr[0:kl, sl] += _bdot(p, dout, "tn")
                    dc_ref[hh, :, 0:kl] -= jnp.sum(ds, axis=0, keepdims=True)

        @pl.when(i == nq - 1)
        def _():
            dv_ref[...] = dv_scr[...].astype(BF16)

    hw = FOX_HEADS_PER_STEP * LANES
    blk = pl.BlockSpec((Q_BLOCK, hw), lambda g, i: (i, g))
    col = pl.BlockSpec((lp, hw), lambda g, i: (0, g))
    return pl.pallas_call(
        body, grid=(nh // FOX_HEADS_PER_STEP, nq), in_specs=_fox_specs(lp, nh) + [blk, blk, _ANY],
        out_specs=[blk, col, pl.BlockSpec((FOX_HEADS_PER_STEP, 1, lp), lambda g, i: (g, 0, 0)),
                   pl.BlockSpec((lp, hw), lambda g, i: (0, 6 * nh // FOX_HEADS_PER_STEP + g))],
        out_shape=[jax.ShapeDtypeStruct((lp, w), F32)] * 2 + [jax.ShapeDtypeStruct((nh, 1, lp), F32),
                                                             jax.ShapeDtypeStruct(dproj.shape, BF16)],
        scratch_shapes=[pltpu.VMEM((lp, hw), F32)], input_output_aliases={7: 3},
        name="fox_bwd", compiler_params=_cp("parallel", "arbitrary"))(qkn, qkn, proj, gates, gtf, lse, do, dproj)


def _merge_fox(o_fox, proj, merged, nh):
    lp = o_fox.shape[0]

    def body(o_ref, z_ref, _, m_ref):
        z = z_ref[...]
        m_ref[...] = (o_ref[...] * (z * _sigmoid(z))).astype(BF16)

    return pl.pallas_call(
        body, grid=(nh,),
        in_specs=[pl.BlockSpec((lp, LANES), lambda s: (0, s)), pl.BlockSpec((lp, LANES), lambda s: (0, 7 * nh + s)), _ANY],
        out_specs=pl.BlockSpec((lp, LANES), lambda s: (0, nh + s)),
        out_shape=jax.ShapeDtypeStruct(merged.shape, BF16), input_output_aliases={2: 0}, name="merge_fox",
        compiler_params=_cp("parallel"))(o_fox, proj, merged)


def _merge_fox_bwd(o_fox, proj, dmerged, dproj, nh):
    lp = o_fox.shape[0]

    def body(o_ref, z_ref, dm_ref, _, do_ref, dz_ref):
        silu, dsilu = _silu_and_grad(z_ref[...])
        dm = dm_ref[...]
        do_ref[...] = dm * silu
        dz_ref[...] = (dm * o_ref[...] * dsilu).astype(BF16)

    w = nh * HEAD_DIM
    return pl.pallas_call(
        body, grid=(nh,),
        in_specs=[pl.BlockSpec((lp, LANES), lambda s: (0, s)), pl.BlockSpec((lp, LANES), lambda s: (0, 7 * nh + s)),
                  pl.BlockSpec((lp, LANES), lambda s: (0, nh + s)), _ANY],
        out_specs=[pl.BlockSpec((lp, LANES), lambda s: (0, s)), pl.BlockSpec((lp, LANES), lambda s: (0, 7 * nh + s))],
        out_shape=[jax.ShapeDtypeStruct((lp, w), F32), jax.ShapeDtypeStruct(dproj.shape, BF16)],
        input_output_aliases={3: 1}, name="merge_fox_bwd", compiler_params=_cp("parallel"))(o_fox, proj, dmerged, dproj)


def _post(out, x, target, post_w):
    lp, d = out.shape

    def body(o_ref, x_ref, t_ref, w_ref, dy_ref, do_ref, loss_ref, dw_ref):
        i = pl.program_id(0)

        @pl.when(i == 0)
        def _():
            loss_ref[...] = jnp.zeros_like(loss_ref)
            dw_ref[...] = jnp.zeros_like(dw_ref)
        o = o_ref[...]
        r = _rms(o)
        nrm = o * r
        err = jnp.where(i > 0, x_ref[...] + nrm * w_ref[...] - t_ref[...], 0.0)
        loss_ref[0:1, :] += 0.5 * jnp.sum(jnp.sum(err * err, axis=1, keepdims=True), axis=0, keepdims=True) / d
        dy = err / d
        dy_ref[...] = dy
        dw_ref[...] += jnp.sum(dy * nrm, axis=0, keepdims=True)
        dyw = dy * w_ref[...]
        do_ref[...] = (r * (dyw - nrm * jnp.mean(dyw * nrm, axis=-1, keepdims=True))).astype(BF16)

    row = pl.BlockSpec((Q_BLOCK, d), lambda i: (i, 0))
    vec = pl.BlockSpec((1, d), lambda i: (0, 0))
    return pl.pallas_call(
        body, grid=(lp // Q_BLOCK,), in_specs=[row, _x_rows(d), _x_rows(d), vec],
        out_specs=[_x_rows(d), row, pl.BlockSpec((8, LANES), lambda i: (0, 0)), vec],
        out_shape=[jax.ShapeDtypeStruct(x.shape, F32), jax.ShapeDtypeStruct((lp, d), BF16),
                   jax.ShapeDtypeStruct((8, LANES), F32), jax.ShapeDtypeStruct((1, d), F32)],
        name="post", compiler_params=_cp("arbitrary"))(out, x, target, post_w)


def _prenorm_bwd(dxn, x, meta, w, dy, rider=None):
    seq, d = x.shape
    lp = seq + Q_BLOCK

    def body(start, finish, dx_ref, x_ref, m_ref, w_ref, dy_ref, gx_ref, gm_ref, dw_ref):
        i = pl.program_id(0)
        pl.when(i == 0)(start)
        h = _h_tile(i, x_ref, m_ref)
        r = _rms(h)
        xh = h * r
        dxn_ = dx_ref[...]
        dxw = dxn_ * w_ref[...]
        dh = jnp.where(i > 0, dy_ref[...], 0.0) + r * (dxw - xh * jnp.mean(dxw * xh, axis=-1, keepdims=True))
        gx_ref[...] = dh

        @pl.when(i == 0)
        def _():
            dw_ref[...] = jnp.zeros_like(dw_ref)
            gm_ref[...] = dh[PAD_ROWS:, :]
        dw_ref[...] += jnp.sum(dxn_ * xh, axis=0, keepdims=True)
        pl.when(i == lp // Q_BLOCK - 1)(finish)

    vec = pl.BlockSpec((1, d), lambda i: (0, 0))
    met = pl.BlockSpec((N_META, d), lambda i: (0, 0))
    outs, got = _hosted_call(
        body, [rider], 5, 3, 0, grid=(lp // Q_BLOCK,),
        in_specs=[pl.BlockSpec((Q_BLOCK, d), lambda i: (i, 0)), _x_rows(d), met, vec, _x_rows(d)],
        out_specs=[_x_rows(d), met, vec],
        out_shape=[jax.ShapeDtypeStruct((seq, d), F32), jax.ShapeDtypeStruct((N_META, d), F32),
                   jax.ShapeDtypeStruct((1, d), F32)],
        name="prenorm_bwd", compiler_params=_cp("arbitrary"))(dxn, x, meta, w, dy)
    return outs, (got[0] if got else None)


def _layer_grads(x, target, meta, pre_w, wfull, conv_wt, a_log, dt_bias, gdn_norm_w, fq_w, fk_w, f_bias, w_out, post_w,
                 late_weights=None, w_out_grads=None):
    nh = a_log.shape[1]
    zpad = jnp.zeros((1, LANES - 3 * nh), F32)
    bias_row = jnp.concatenate([jnp.zeros((1, nh), F32), dt_bias, f_bias, zpad], axis=1)
    nega_row = jnp.concatenate([jnp.zeros((1, nh), F32), -jnp.exp(a_log), jnp.zeros((1, nh), F32), zpad], axis=1)
    qk_w = jnp.stack([fq_w, fk_w])

    xn = _prenorm(x, meta, pre_w)
    proj = _matmul(xn, wfull, "nn", MM_TILE, F32, "proj")
    qkv = _gdn_prep(proj, conv_wt, nh)
    gates, gt3, gtf = _gates(proj, bias_row, nega_row, nh)
    (o_gdn, s_all, t_all), got = _gdn_fwd(qkv, gates, gt3, nh, None if late_weights is None else late_weights[0])
    if late_weights is not None:
        w_out = late_weights[1](got)
    qkn = _fox_prep(proj, qk_w, nh)
    o_fox, fox_lse = _fox_fwd(qkn, proj, gates, gtf, nh)
    merged = _merge_fox(o_fox, proj, _merge_gdn(o_gdn, proj, gdn_norm_w, nh), nh)
    out = _matmul(merged, w_out, "nn", 4 * LANES, F32, "out_proj")
    dy, dout, loss_blk, dpost_w = _post(out, x, target, post_w)

    dw_out = _matmul(merged, dout, "tn", 4 * LANES, BF16, "dw_out")
    if w_out_grads is None:
        dmerged, gdn_rider = _matmul(dout, w_out, "nt", 4 * LANES, F32, "dmerged"), None
    else:
        dmerged, got = _matmul(dout, w_out, "nt", 4 * LANES, F32, "dmerged", w_out_grads[0](dw_out))
        gdn_rider = w_out_grads[1](dw_out, got)
    do_gdn, dproj, dgdn_norm_w = _merge_gdn_bwd(o_gdn, proj, gdn_norm_w, dmerged, nh)
    do_fox, dproj = _merge_fox_bwd(o_fox, proj, dmerged, dproj, nh)
    dqn, dkn, dc_t, dproj = _fox_bwd(qkn, proj, gates, gtf, fox_lse, do_fox, dproj, nh)
    dproj, dqk_w = _fox_prep_bwd(proj, qk_w, dqn, dkn, dproj, nh)
    (dgq, dgk, dgv, dgate), w_out_parts = _gdn_bwd(qkv, gates, gt3, s_all, t_all, do_gdn, nh, gdn_rider)
    dproj, dconv_wt = _gdn_prep_bwd(proj, conv_wt, dgq, dgk, dgv, dproj, nh)
    dc_rows = jnp.pad(dc_t.reshape(nh, -1), ((2 * nh, LANES - 3 * nh), (0, 0)))
    dproj, gate_sums = _gates_bwd(proj, bias_row, nega_row, gates, dgate, dc_rows, dproj, nh)
    return dict(
        loss=loss_blk[0:1, 0:1], dy=dy, xn=xn, dproj=dproj, post_w=dpost_w,
        conv_wt=dconv_wt, a_log=gate_sums[1:2, nh:2 * nh], dt_bias=gate_sums[0:1, nh:2 * nh],
        gdn_norm_w=dgdn_norm_w, fq_w=dqk_w[0], fk_w=dqk_w[1], f_bias=gate_sums[0:1, 2 * nh:3 * nh], w_out=dw_out,
        w_out_parts=w_out_parts)


def _cast_bf16(a, tr, name):
    r, c = a.shape

    def body(a_ref, o_ref):
        o_ref[...] = a_ref[...].astype(BF16)

    return pl.pallas_call(
        body, grid=(r // tr,), in_specs=[pl.BlockSpec((tr, c), lambda i: (i, 0))],
        out_specs=pl.BlockSpec((tr, c), lambda i: (i, 0)), out_shape=jax.ShapeDtypeStruct((r, c), BF16),
        name=name, compiler_params=_cp("parallel"))(a)


def _column_major(a):
    return jnp.transpose(a, (2, 0, 1))


def _cast_bf16_column_major(a3, name):
    _, r, c = a3.shape

    def body(a_ref, o_ref):
        o_ref[...] = a_ref[...].reshape(LANES, r).T.astype(BF16)

    return pl.pallas_call(
        body, grid=(pl.cdiv(c, LANES),), in_specs=[pl.BlockSpec((LANES, 1, r), lambda i: (i, 0, 0))],
        out_specs=pl.BlockSpec((r, LANES), lambda i: (0, i)), out_shape=jax.ShapeDtypeStruct((r, c), BF16),
        name=name, compiler_params=_cp("parallel"))(_column_major(a3))


def _adamw_column_major(w3, parts, m3, v3, name):
    _, r, c = w3.shape
    n_parts = parts[0].shape[0]

    def body(w_ref, *refs):
        p_refs, (m_ref, v_ref, g_ref, d_ref, nm_ref, nv_ref) = refs[:len(parts)], refs[len(parts):]
        sums = []
        for p_ref in p_refs:
            g = p_ref[0].astype(F32)
            for s in range(1, n_parts):
                g = g + p_ref[s].astype(F32)
            sums.append(g)
        g = jnp.concatenate(sums, axis=0).T
        flat = lambda ref: ref[...].reshape(LANES, r)
        m_new = ADAM_B1 * flat(m_ref) + (1.0 - ADAM_B1) * g
        v_new = ADAM_B2 * flat(v_ref) + (1.0 - ADAM_B2) * (g * g)
        m_hat = m_new / (1.0 - ADAM_B1 ** ADAM_STEP)
        v_hat = v_new / (1.0 - ADAM_B2 ** ADAM_STEP)
        delta = -ADAM_LR * (m_hat / (jnp.sqrt(v_hat) + ADAM_EPS) + ADAM_WD * flat(w_ref))
        for ref, val in ((g_ref, g), (d_ref, delta), (nm_ref, m_new), (nv_ref, v_new)):
            ref[...] = val.reshape(LANES, 1, r)

    blk = pl.BlockSpec((LANES, 1, r), lambda i: (i, 0, 0))
    outs = pl.pallas_call(
        body, grid=(pl.cdiv(c, LANES),),
        in_specs=[blk] + [pl.BlockSpec((n_parts, p.shape[1], LANES), lambda i: (0, 0, i)) for p in parts] + [blk, blk],
        out_specs=[blk] * 4, out_shape=[jax.ShapeDtypeStruct((c, 1, r), F32)] * 4, name=name,
        compiler_params=_cp("parallel"))(_column_major(w3), *parts, _column_major(m3), _column_major(v3))
    return [jnp.transpose(o, (1, 2, 0)) for o in outs]


def _gather_copies(ins, outs, send_sems, recv_sems, local_sems):
    n = len(ins)
    x, y, c = lax.axis_index("x"), lax.axis_index("y"), lax.axis_index("c")
    me, sibling = (x, y, c), (x, y, 1 - c)
    xn, yn, dg = (1 - x, y), (x, 1 - y), (1 - x, 1 - y)

    def copy(a, k, block, to, src=None):
        px, py, pc = block
        rows = outs[a].at[4 * px + 2 * py + pc]
        return pltpu.make_async_remote_copy(
            src_ref=rows if src is None else src, dst_ref=rows, send_sem=send_sems.at[a, k],
            recv_sem=recv_sems.at[a, k], device_id=to, device_id_type=_MESH)

    local = [pltpu.make_async_copy(ins[a], outs[a].at[4 * x + 2 * y + c], local_sems.at[a]) for a in range(n)]
    own = [cp for a in range(n) for cp in (copy(a, 0, me, sibling, src=ins[a]), copy(a, 1, me, (*xn, c), src=ins[a]),
                                           copy(a, 2, me, (*yn, c), src=ins[a]))]

    def start():
        for cp in local + own:
            cp.start()

    def finish():
        for a in range(n):
            @pl.when(c == 1)
            def _(a=a):
                copy(a, 1, (*xn, c), me).wait_recv()
                copy(a, 3, (*xn, c), (*yn, c)).start()

            @pl.when(c == 0)
            def _(a=a):
                copy(a, 2, (*yn, c), me).wait_recv()
                copy(a, 3, (*yn, c), (*xn, c)).start()
        for a in range(n):
            pl.when(c == 0)(copy(a, 1, (*xn, c), me).wait_recv)
            copy(a, 4, (*xn, c), sibling).start()
            pl.when(c == 1)(copy(a, 2, (*yn, c), me).wait_recv)
            copy(a, 5, (*yn, c), sibling).start()
        for a in range(n):
            copy(a, 3, (*dg, c), me).wait_recv()
            copy(a, 6, (*dg, c), sibling).start()
        for a in range(n):
            copy(a, 0, sibling, me).wait_recv()
            for k, chip in ((4, xn), (5, yn), (6, dg)):
                copy(a, k, (*chip, 1 - c), me).wait_recv()
                copy(a, k, (*chip, c), sibling).wait_send()
            copy(a, 3, (*xn, c), (*yn, c)).wait_send()
        for cp in own:
            cp.wait_send()
        for cp in local:
            cp.wait()

    return start, finish


def _gather_scratch(n):
    return [pltpu.SemaphoreType.DMA((n, N_DEV - 1)), pltpu.SemaphoreType.DMA((n, N_DEV - 1)), pltpu.SemaphoreType.DMA((n,))]


def _gather_rider(arrays):
    return _Rider(list(arrays), [jax.ShapeDtypeStruct((N_DEV,) + a.shape, a.dtype) for a in arrays],
                  _gather_scratch(len(arrays)), {}, lambda ins, outs, scratch: _gather_copies(ins, outs, *scratch))


def _all_gather(arrays, name):
    n = len(arrays)

    def body(*refs):
        start, finish = _gather_copies(refs[:n], refs[n:2 * n], *refs[2 * n:])
        start()
        finish()

    return pl.pallas_call(
        body, in_specs=[_ANY] * n, out_specs=[_ANY] * n,
        out_shape=[jax.ShapeDtypeStruct((N_DEV,) + a.shape, a.dtype) for a in arrays],
        scratch_shapes=_gather_scratch(n), name=name)(*arrays)


SLAB = 10 * LANES


def _slab_start(blk, nh, cols):
    in_second_half = blk >= N_DEV // 2
    shift = (2 * nh if in_second_half else 0) if isinstance(blk, int) else jnp.where(in_second_half, 2 * nh, 0)
    return (blk * cols - shift) // LANES * LANES


def _pair_rider(dw_rows=None, parts=None, nh=None, after=None):
    if dw_rows is not None:
        r, full = dw_rows.shape
        cols = (full - NARROW + 3 * nh) // N_DEV
        out_shapes = [jax.ShapeDtypeStruct((N_CHIP, r, SLAB), dw_rows.dtype), jax.ShapeDtypeStruct((r, NARROW), dw_rows.dtype)]
    else:
        out_shapes = [jax.ShapeDtypeStruct((N_CHIP,) + parts.shape[1:], parts.dtype)]

    def make(ins, outs, scratch):
        send_sems, recv_sems = scratch
        x, y, c = lax.axis_index("x"), lax.axis_index("y"), lax.axis_index("c")
        kw = lambda k: dict(send_sem=send_sems.at[k], recv_sem=recv_sems.at[k], device_id=(x, y, 1 - c), device_id_type=_MESH)
        copies = []
        for q in range(N_CHIP):
            if dw_rows is not None:
                first = pl.multiple_of(_slab_start(2 * q + 1 - c, nh, cols), LANES)
                copies.append(pltpu.make_async_remote_copy(src_ref=ins[0].at[:, pl.ds(first, SLAB)], dst_ref=outs[0].at[q], **kw(q)))
            else:
                copies.append(pltpu.make_async_remote_copy(src_ref=ins[0].at[2 * q + 1 - c], dst_ref=outs[0].at[q], **kw(q)))
        if dw_rows is not None:
            copies.append(pltpu.make_async_remote_copy(src_ref=ins[0].at[:, pl.ds(full - NARROW, NARROW)], dst_ref=outs[1],
                                                       **kw(N_CHIP)))

        def start():
            for cp in copies:
                cp.start()

        def finish():
            for cp in copies:
                cp.wait()

        return start, finish

    return _Rider([dw_rows if dw_rows is not None else parts] + ([] if after is None else [after]), out_shapes,
                  [pltpu.SemaphoreType.DMA((N_CHIP + 1,)), pltpu.SemaphoreType.DMA((N_CHIP + 1,))], {}, make)


def _relayout_pair_sum(dwfull, got_slabs, got_tail, core, nh, tr, name):
    d, full = dwfull.shape
    w = nh * HEAD_DIM
    cols = (8 * w + 3 * nh) // N_DEV
    segs = _native_segments(nh)

    def block(f_ref, s_ref, t_ref, q, blk):
        st = _slab_start(blk, nh, cols)
        wide = f_ref[:, st:st + SLAB].astype(F32) + s_ref[q].astype(F32)
        tail = f_ref[:, 8 * w:].astype(F32) + t_ref[...].astype(F32)
        pieces = []
        for s0, s1, t0 in segs:
            lo, hi = max(s0, blk * cols), min(s1, (blk + 1) * cols)
            if lo < hi:
                at = t0 + lo - s0
                pieces.append(tail[:, at - 8 * w:at - 8 * w + hi - lo] if at >= 8 * w else wide[:, at - st:at - st + hi - lo])
        return (pieces[0] if len(pieces) == 1 else jnp.concatenate(pieces, axis=1)).astype(dwfull.dtype)

    def body(core_ref, f_ref, s_ref, t_ref, o_ref):
        for parity in range(2):
            @pl.when(core_ref[0] == parity)
            def _(parity=parity):
                for q in range(N_CHIP):
                    o_ref[q] = block(f_ref, s_ref, t_ref, q, 2 * q + parity)

    return pl.pallas_call(
        body,
        grid_spec=pltpu.PrefetchScalarGridSpec(
            num_scalar_prefetch=1, grid=(d // tr,),
            in_specs=[pl.BlockSpec((tr, full), lambda i, c_ref: (i, 0)), pl.BlockSpec((N_CHIP, tr, SLAB), lambda i, c_ref: (0, i, 0)),
                      pl.BlockSpec((tr, NARROW), lambda i, c_ref: (i, 0))],
            out_specs=pl.BlockSpec((N_CHIP, tr, cols), lambda i, c_ref: (0, i, 0))),
        out_shape=jax.ShapeDtypeStruct((N_CHIP, d, cols), dwfull.dtype), name=name,
        compiler_params=_cp("parallel"))(core, dwfull, got_slabs, got_tail)


def _pair_sum(parts, got, core, tr, name):
    _, r, c = parts.shape

    def body(core_ref, p_ref, g_ref, o_ref):
        o_ref[...] = (p_ref[...].astype(F32) + g_ref[...].astype(F32)).astype(o_ref.dtype)

    return pl.pallas_call(
        body,
        grid_spec=pltpu.PrefetchScalarGridSpec(
            num_scalar_prefetch=1, grid=(N_CHIP, r // tr),
            in_specs=[pl.BlockSpec((1, tr, c), lambda q, i, core_ref: (2 * q + core_ref[0], i, 0)),
                      pl.BlockSpec((1, tr, c), lambda q, i, core_ref: (q, i, 0))],
            out_specs=pl.BlockSpec((1, tr, c), lambda q, i, core_ref: (q, i, 0))),
        out_shape=jax.ShapeDtypeStruct((N_CHIP, r, c), parts.dtype), name=name,
        compiler_params=_cp("parallel", "parallel"))(core, parts, got)


def _native_segments(nh):
    w = nh * HEAD_DIM
    return [(0, 4 * w, 0), (4 * w, 4 * w + 2 * nh, 8 * w), (4 * w + 2 * nh, 8 * w + 2 * nh, 4 * w),
            (8 * w + 2 * nh, 8 * w + 3 * nh, 8 * w + 2 * nh)]


def _relayout_w_in(wg, nh, tr):
    _, d, cols = wg.shape
    w = nh * HEAD_DIM

    def native(ref, j0, j1):
        out = []
        while j0 < j1:
            blk = j0 // cols
            end = min(j1, (blk + 1) * cols)
            out.append(ref[blk, :, pl.ds(j0 - blk * cols, end - j0)])
            j0 = end
        return out

    def body(g_ref, o_ref):
        for cidx in range(8 * w // LANES):
            j0 = cidx * LANES + (0 if cidx * LANES < 4 * w else 2 * nh)
            pieces = native(g_ref, j0, j0 + LANES)
            o_ref[:, cidx * LANES:(cidx + 1) * LANES] = pieces[0] if len(pieces) == 1 else jnp.concatenate(pieces, axis=1)
        pieces = (native(g_ref, 4 * w, 4 * w + 2 * nh) + native(g_ref, 8 * w + 2 * nh, 8 * w + 3 * nh)
                  + [jnp.zeros((tr, NARROW - 3 * nh), wg.dtype)])
        o_ref[:, 8 * w:] = jnp.concatenate(pieces, axis=1)

    return pl.pallas_call(
        body, grid=(d // tr,), in_specs=[pl.BlockSpec((N_DEV, tr, cols), lambda i: (0, i, 0))],
        out_specs=pl.BlockSpec((tr, 8 * w + NARROW), lambda i: (i, 0)),
        out_shape=jax.ShapeDtypeStruct((d, 8 * w + NARROW), wg.dtype),
        name="relayout_w_in", compiler_params=_cp("parallel"))(wg)


def _adamw(w, parts, m, v, tr, name):
    r, c = w.shape
    n_parts = parts.shape[0]

    def body(w_ref, p_ref, m_ref, v_ref, g_ref, d_ref, nm_ref, nv_ref):
        g = p_ref[0].astype(F32)
        for s in range(1, n_parts):
            g = g + p_ref[s].astype(F32)
        m_new = ADAM_B1 * m_ref[...] + (1.0 - ADAM_B1) * g
        v_new = ADAM_B2 * v_ref[...] + (1.0 - ADAM_B2) * (g * g)
        m_hat = m_new / (1.0 - ADAM_B1 ** ADAM_STEP)
        v_hat = v_new / (1.0 - ADAM_B2 ** ADAM_STEP)
        g_ref[...] = g
        d_ref[...] = -ADAM_LR * (m_hat / (jnp.sqrt(v_hat) + ADAM_EPS) + ADAM_WD * w_ref[...])
        nm_ref[...] = m_new
        nv_ref[...] = v_new

    blk = pl.BlockSpec((tr, c), lambda i: (i, 0))
    return pl.pallas_call(
        body, grid=(r // tr,), in_specs=[blk, pl.BlockSpec((n_parts, tr, c), lambda i: (0, i, 0)), blk, blk],
        out_specs=[blk] * 4, out_shape=[jax.ShapeDtypeStruct((r, c), F32)] * 4, name=name,
        compiler_params=_cp("parallel"))(w, parts, m, v)


def _pack_small(d, pre, post, a_log, dt_bias, f_bias, gdn_w, fq_w, fk_w, extra):
    row2 = jnp.concatenate([a_log, dt_bias, f_bias, gdn_w, fq_w, fk_w, extra], axis=1)
    row2 = jnp.pad(row2, ((0, 0), (0, d - row2.shape[1])))
    return jnp.concatenate([pre, post, row2, jnp.zeros((5, d), F32)], axis=0)


def _unpack_small(p, nh):
    o = 3 * nh
    return dict(pre=p[0:1], post=p[1:2], a_log=p[2:3, 0:nh], dt_bias=p[2:3, nh:2 * nh], f_bias=p[2:3, 2 * nh:o],
                gdn_w=p[2:3, o:o + HEAD_DIM], fq_w=p[2:3, o + HEAD_DIM:o + 2 * HEAD_DIM],
                fk_w=p[2:3, o + 2 * HEAD_DIM:o + 3 * HEAD_DIM], extra=p[2, o + 3 * HEAD_DIM])


def kernel(x, meta_tokens, pre_norm_w, w_in, conv_w, a_log, dt_bias, gdn_norm_w, fox_q_norm_w, fox_k_norm_w, fox_f_bias, w_out, post_norm_w, loss_target, m_meta_tokens, m_pre_norm_w, m_w_in, m_conv_w, m_a_log, m_dt_bias, m_gdn_norm_w, m_fox_q_norm_w, m_fox_k_norm_w, m_fox_f_bias, m_w_out, m_post_norm_w, v_meta_tokens, v_pre_norm_w, v_w_in, v_conv_w, v_a_log, v_dt_bias, v_gdn_norm_w, v_fox_q_norm_w, v_fox_k_norm_w, v_fox_f_bias, v_w_out, v_post_norm_w):
    nh = a_log.shape[1]
    d = x.shape[-1]
    w = nh * HEAD_DIM
    zero = jnp.zeros((1, 1), F32)

    wg, cg, mg = _all_gather([_cast_bf16_column_major(w_in, "cast_w_in"), conv_w[0].T, meta_tokens], "gather_weights")
    wfull = _relayout_w_in(wg, nh, 256)
    conv_wt = cg.transpose(1, 0, 2).reshape(CONV_WIDTH, 3 * w)
    meta_full = mg.transpose(1, 0, 2).reshape(N_META, d)
    late_weights = (_gather_rider([_cast_bf16(w_out[0], 256, "cast_w_out")]), lambda got: got[0].reshape(2 * w, d))
    core = lax.axis_index("c")
    dev = 4 * lax.axis_index("x") + 2 * lax.axis_index("y") + core
    core_arr = jnp.reshape(core, (1,)).astype(jnp.int32)

    out_parts = lambda dw_out: dw_out.reshape(N_DEV, 2 * w // N_DEV, d)
    g = _layer_grads(
        x[0], loss_target[0], meta_full, pre_norm_w, wfull, conv_wt, a_log, dt_bias, gdn_norm_w,
        fox_q_norm_w, fox_k_norm_w, fox_f_bias, None, post_norm_w, late_weights=late_weights,
        w_out_grads=(lambda dw_out: _pair_rider(parts=out_parts(dw_out)),
                     lambda dw_out, got: _chip_rider(_pair_sum(out_parts(dw_out), got[0], core_arr, 256, "pair_sum_w_out"))))
    p_out = g["w_out_parts"][0]
    xn, dproj, rows = g["xn"], g["dproj"], d // DW_IN_PIECES
    flights, token, dw, rider = [], None, None, None

    def exchange(dw, got, i):
        sums = _relayout_pair_sum(dw, got[0], got[1], core_arr, nh, 128, f"relayout_pair_sum_{i}")
        flight, token = _chip_exchange_start(sums, f"chip_exchange_start_{i}")
        flights.append(flight)
        return token

    for i in range(DW_IN_PIECES):
        res = _matmul(xn, dproj, "tn", MM_TILE, BF16, f"dw_in_{i}", rider, a_cols=(rows, i))
        if i:
            token = exchange(dw, res[1], i - 1)
        dw = res[0] if i else res
        rider = _pair_rider(dw_rows=dw, nh=nh, after=token)
    dxn, (got,) = _dxn(dproj, wfull, [rider], MM_TILE, "dxn")
    token = exchange(dw, got, DW_IN_PIECES - 1)
    (grad_x, dmeta, dpre_w), _ = _prenorm_bwd(dxn, x[0], meta_full, pre_norm_w + token[0:1, 0:1], g["dy"])
    small = _pack_small(d, dpre_w, g["post_w"], g["a_log"], g["dt_bias"], g["f_bias"], g["gdn_norm_w"], g["fq_w"],
                        g["fk_w"], g["loss"])
    a_conv, a_meta, p_small = _all_gather([g["conv_wt"], dmeta, small], "gather_small_grads")
    p_conv = lax.dynamic_slice_in_dim(a_conv, dev * conv_w.shape[1], conv_w.shape[1], axis=2).transpose(0, 2, 1)
    p_meta = lax.dynamic_slice_in_dim(a_meta, dev * meta_tokens.shape[1], meta_tokens.shape[1], axis=2)

    r_out = _adamw(w_out[0], p_out, m_w_out[0], v_w_out[0], 64, "adamw_w_out")
    r_conv = _adamw(conv_w[0], p_conv, m_conv_w[0], v_conv_w[0], conv_w.shape[1], "adamw_conv_w")
    r_meta = _adamw(meta_tokens, p_meta, m_meta_tokens, v_meta_tokens, N_META, "adamw_meta")
    pk = lambda pre, post, a, dt, gw, fq, fk, fb: _pack_small(d, pre, post, a, dt, fb, gw, fq, fk, zero)
    r_small = _adamw(
        pk(pre_norm_w, post_norm_w, a_log, dt_bias, gdn_norm_w, fox_q_norm_w, fox_k_norm_w, fox_f_bias), p_small,
        pk(m_pre_norm_w, m_post_norm_w, m_a_log, m_dt_bias, m_gdn_norm_w, m_fox_q_norm_w, m_fox_k_norm_w, m_fox_f_bias),
        pk(v_pre_norm_w, v_post_norm_w, v_a_log, v_dt_bias, v_gdn_norm_w, v_fox_q_norm_w, v_fox_k_norm_w, v_fox_f_bias),
        8, "adamw_small")
    p_in = _chip_exchange_wait(flights, r_small[0], "chip_exchange_wait")
    r_in = _adamw_column_major(w_in, p_in, m_w_in, v_w_in, "adamw_w_in")

    sm = [_unpack_small(r, nh) for r in r_small]
    outs = []
    for i in range(4):
        s = sm[i]
        outs += [r_meta[i], s["pre"], r_in[i], r_conv[i][None], s["a_log"], s["dt_bias"], s["gdn_w"], s["fq_w"],
                 s["fk_w"], s["f_bias"], r_out[i][None], s["post"]]
    return (sm[0]["extra"], grad_x[None], *outs)
```

```python
import jax
import jax.numpy as jnp
from jax import lax
from jax.experimental import pallas as pl
from jax.experimental.pallas import tpu as pltpu

F32, BF16 = jnp.float32, jnp.bfloat16
HEAD_DIM = 128
N_META = 16
CONV_WIDTH = 4
CHUNK = 128
Q_BLOCK = 128
LANES = 128
EPS = 1e-6
PAD_ROWS = Q_BLOCK - N_META
N_DEV = 8
N_CHIP = 4
VMEM_LIMIT = 56 * 1024 * 1024
NEG = -1e30
NARROW = 2 * LANES
MM_TILE = 6 * LANES
DW_IN_PIECES = ((0, 2), (2, 4), (3, 4))

ADAM_LR, ADAM_B1, ADAM_B2, ADAM_EPS, ADAM_WD, ADAM_STEP = 0.001, 0.9, 0.999, 1e-08, 0.01, 10

_DN = {"nn": (((1,), (0,)), ((), ())), "nt": (((1,), (1,)), ((), ())), "tn": (((0,), (0,)), ((), ()))}
_DN3 = {"nn": (((2,), (1,)), ((0,), (0,))), "nt": (((2,), (2,)), ((0,), (0,))), "tn": (((1,), (1,)), ((0,), (0,)))}
_ANY = pl.BlockSpec(memory_space=pl.ANY)
_MESH = pl.DeviceIdType.MESH


def _cp(*sem):
    return pltpu.CompilerParams(dimension_semantics=sem, vmem_limit_bytes=VMEM_LIMIT)


def _dot(a, b, dims="nn", prec=None):
    return lax.dot_general(a, b, _DN[dims], precision=prec, preferred_element_type=F32)


def _bdot(a, b, dims="nn"):
    return _dot(a.astype(BF16), b.astype(BF16), dims)


def _hdot(a, b, dims="nn"):
    return _dot(a, b, dims, prec=lax.Precision.HIGHEST)


def _dot3(a, b, dims="nn"):
    return lax.dot_general(a, b, _DN3[dims], preferred_element_type=F32)


def _bdot3(a, b, dims="nn"):
    return _dot3(a.astype(BF16), b.astype(BF16), dims)


def _split(a):
    hi = a.astype(BF16)
    return hi, (a - hi.astype(F32)).astype(BF16)


def _iota(shape, dim):
    return lax.broadcasted_iota(jnp.int32, shape, dim)


def _sigmoid(z):
    return 1.0 / (1.0 + jnp.exp(-z))


def _softplus(z):
    e = jnp.exp(-jnp.abs(z))
    u = 1.0 + e
    l1p = jnp.where(u == 1.0, e, jnp.log(u) * (e / jnp.where(u == 1.0, 1.0, u - 1.0)))
    return jnp.maximum(z, 0.0) + l1p


def _silu_and_grad(z):
    s = _sigmoid(z)
    return z * s, s * (1.0 + z * (1.0 - s))


def _rms(x):
    return lax.rsqrt(jnp.mean(x * x, axis=-1, keepdims=True) + EPS)


def _h_tile(i, x_ref, meta_ref):
    first = jnp.concatenate([jnp.zeros((PAD_ROWS, x_ref.shape[1]), F32), meta_ref[...]], axis=0)
    return jnp.where(i == 0, first, x_ref[...])


def _x_rows(d):
    return pl.BlockSpec((Q_BLOCK, d), lambda i: (jnp.maximum(i - 1, 0), 0))


def _prenorm(x, meta, w):
    seq, d = x.shape
    lp = seq + Q_BLOCK

    def body(x_ref, m_ref, w_ref, o_ref):
        h = _h_tile(pl.program_id(0), x_ref, m_ref)
        o_ref[...] = (h * _rms(h) * w_ref[...]).astype(BF16)

    return pl.pallas_call(
        body, grid=(lp // Q_BLOCK,),
        in_specs=[_x_rows(d), pl.BlockSpec((N_META, d), lambda i: (0, 0)), pl.BlockSpec((1, d), lambda i: (0, 0))],
        out_specs=pl.BlockSpec((Q_BLOCK, d), lambda i: (i, 0)),
        out_shape=jax.ShapeDtypeStruct((lp, d), BF16), name="prenorm", compiler_params=_cp("parallel"))(x, meta, w)


def _tile(n, want):
    return max(t for t in range(LANES, want + 1, LANES) if n % t == 0)


class _Rider:
    def __init__(self, inputs, out_shapes, scratch, aliases, make):
        self.inputs, self.out_shapes, self.scratch, self.aliases, self.make = inputs, out_shapes, scratch, aliases, make


def _hosted_call(body, riders, n_in, n_out, n_scratch, *, in_specs, out_specs, out_shape, scratch_shapes=(), aliases=None,
                 **kw):
    riders = [r for r in riders if r is not None]
    r_in = [len(r.inputs) for r in riders]
    r_out = [len(r.out_shapes) for r in riders]
    r_scr = [len(r.scratch) for r in riders]
    al = dict(aliases or {})
    for k, r in enumerate(riders):
        al.update({n_in + sum(r_in[:k]) + i: n_out + sum(r_out[:k]) + o for i, o in r.aliases.items()})

    def full_body(*refs):
        ins, rest = refs[:n_in + sum(r_in)], refs[n_in + sum(r_in):]
        outs, scr = rest[:n_out + sum(r_out)], rest[n_out + sum(r_out):]
        hooks = [r.make(ins[n_in + sum(r_in[:k]):n_in + sum(r_in[:k + 1])], outs[n_out + sum(r_out[:k]):n_out + sum(r_out[:k + 1])],
                        scr[n_scratch + sum(r_scr[:k]):n_scratch + sum(r_scr[:k + 1])]) for k, r in enumerate(riders)]

        def start():
            for h in hooks:
                h[0]()

        def finish():
            for h in hooks:
                h[1]()

        body(start, finish, *ins[:n_in], *outs[:n_out], *scr[:n_scratch])

    call = pl.pallas_call(
        full_body, in_specs=list(in_specs) + [_ANY] * sum(r_in), out_specs=list(out_specs) + [_ANY] * sum(r_out),
        out_shape=list(out_shape) + [s for r in riders for s in r.out_shapes],
        scratch_shapes=list(scratch_shapes) + [s for r in riders for s in r.scratch], input_output_aliases=al, **kw)

    def run(*args):
        res = call(*args, *[t for r in riders for t in r.inputs])
        return res[:n_out], [res[n_out + sum(r_out[:k]):n_out + sum(r_out[:k + 1])] for k in range(len(riders))]

    return run


def _matmul(a, b, dims, tn, out_dtype, name, rider=None, a_cols=None):
    a_shape = a.shape if a_cols is None else (a.shape[0], a_cols[0])
    a_index = 0 if a_cols is None else a_cols[1]
    m = a_shape[1] if dims == "tn" else a_shape[0]
    n = b.shape[0] if dims == "nt" else b.shape[1]
    kdim = b.shape[1] if dims == "nt" else b.shape[0]
    tn = _tile(n, tn)
    steps = n // tn
    b_spec = pl.BlockSpec((tn, kdim), lambda j: (j, 0)) if dims == "nt" else pl.BlockSpec((kdim, tn), lambda j: (0, j))

    def body(start, finish, a_ref, b_ref, o_ref):
        pl.when(pl.program_id(0) == 0)(start)
        o_ref[...] = _dot(a_ref[...], b_ref[...], dims).astype(out_dtype)
        pl.when(pl.program_id(0) == steps - 1)(finish)

    (out,), got = _hosted_call(
        body, [rider], 2, 1, 0, grid=(steps,), in_specs=[pl.BlockSpec(a_shape, lambda j: (0, a_index)), b_spec],
        out_specs=[pl.BlockSpec((m, tn), lambda j: (0, j))], out_shape=[jax.ShapeDtypeStruct((m, n), out_dtype)],
        name=name, compiler_params=_cp("parallel" if rider is None else "arbitrary"))(a, b)
    return out if rider is None else (out, got[0])


def _chip_rider(sums):
    def make(ins, outs, scratch):
        local, remote = _chip_exchange_copies(ins[0], outs[0], *scratch)

        def start():
            for cp in [local] + remote:
                cp.start()

        def finish():
            local.wait()
            for cp in remote:
                cp.wait_send()
                cp.wait_recv()

        return start, finish

    return _Rider([sums], [jax.ShapeDtypeStruct(sums.shape, sums.dtype)],
                  [pltpu.SemaphoreType.DMA((N_CHIP - 1,)), pltpu.SemaphoreType.DMA((N_CHIP - 1,)), pltpu.SemaphoreType.DMA((1,))],
                  {}, make)


_HBM = pl.BlockSpec(memory_space=pltpu.HBM)
_SEM = pl.BlockSpec(memory_space=pltpu.SEMAPHORE)


def _chip_exchange_copies(sums_ref, land_ref, send_sems, recv_sems, local_sem):
    x, y, core = lax.axis_index("x"), lax.axis_index("y"), lax.axis_index("c")
    mine = 2 * x + y
    local = pltpu.make_async_copy(sums_ref.at[mine], land_ref.at[mine], local_sem.at[0])
    remote = []
    for k in range(1, N_CHIP):
        px = 1 - x if k & 2 else x
        py = 1 - y if k & 1 else y
        remote.append(pltpu.make_async_remote_copy(
            src_ref=sums_ref.at[2 * px + py], dst_ref=land_ref.at[mine], send_sem=send_sems.at[k - 1],
            recv_sem=recv_sems.at[k - 1], device_id=(px, py, core), device_id_type=_MESH))
    return local, remote


def _chip_exchange_start(sums, name):
    def body(s_ref, send_sems, recv_sems, local_sem, s_thru, land_ref, token):
        local, remote = _chip_exchange_copies(s_ref, land_ref, send_sems, recv_sems, local_sem)
        for cp in [local] + remote:
            cp.start()
        token[...] = jnp.zeros_like(token)

    *flight, token = pl.pallas_call(
        body, name=name,
        out_shape=(pltpu.SemaphoreType.DMA((N_CHIP - 1,)), pltpu.SemaphoreType.DMA((N_CHIP - 1,)), pltpu.SemaphoreType.DMA((1,)),
                   pltpu.HBM(sums.shape, sums.dtype), pltpu.HBM(sums.shape, sums.dtype), jax.ShapeDtypeStruct((8, LANES), F32)),
        in_specs=(_HBM,), out_specs=(_SEM, _SEM, _SEM, _HBM, _HBM, pl.BlockSpec(memory_space=pltpu.VMEM)),
        input_output_aliases={0: 3},
        compiler_params=pltpu.CompilerParams(has_side_effects=pltpu.SideEffectType.DATAFLOW_SIDE_EFFECTING))(
            pltpu.with_memory_space_constraint(sums, pltpu.HBM))
    return flight, token


def _chip_exchange_wait(flights, after, name):
    n = len(flights)

    def body(*refs):
        for i in range(n):
            s_ref, land_ref = refs[2 * i:2 * i + 2]
            local, remote = _chip_exchange_copies(s_ref, land_ref, *refs[2 * n + 3 * i:2 * n + 3 * i + 3])
            local.wait()
            for cp in remote:
                cp.wait_send()
                cp.wait_recv()

    buffers = [b for f in flights for b in f[3:]]
    res = pl.pallas_call(
        body, name=name, out_shape=tuple(pltpu.HBM(b.shape, b.dtype) for b in buffers),
        in_specs=(_HBM,) * (2 * n) + (_SEM,) * (3 * n) + (_ANY,), out_specs=(_HBM,) * (2 * n),
        input_output_aliases={i: i for i in range(2 * n)},
        compiler_params=pltpu.CompilerParams(has_side_effects=pltpu.SideEffectType.DATAFLOW_SIDE_EFFECTING))(
            *buffers, *[s for f in flights for s in f[:3]], after)
    return res[1::2]


def _dxn(dproj, wfull, riders, tk, name):
    m, k = dproj.shape
    n = wfull.shape[0]
    tk = _tile(k, tk)
    steps = k // tk

    def body(start, finish, a_ref, b_ref, o_ref):
        j = pl.program_id(0)

        @pl.when(j == 0)
        def _():
            start()
            o_ref[...] = jnp.zeros_like(o_ref)
        o_ref[...] += _dot(a_ref[...], b_ref[...], "nt")
        pl.when(j == steps - 1)(finish)

    (dxn,), got = _hosted_call(
        body, riders, 2, 1, 0, grid=(steps,),
        in_specs=[pl.BlockSpec((m, tk), lambda j: (0, j)), pl.BlockSpec((n, tk), lambda j: (0, j))],
        out_specs=[pl.BlockSpec((m, n), lambda j: (0, 0))], out_shape=[jax.ShapeDtypeStruct((m, n), F32)],
        name=name, compiler_params=_cp("arbitrary"))(dproj, wfull)
    return dxn, got


def _conv_taps(x, w):
    c = x * w[CONV_WIDTH - 1:CONV_WIDTH, :]
    for j in range(CONV_WIDTH - 1):
        c = c + pltpu.roll(x, CONV_WIDTH - 1 - j, 0) * w[j:j + 1, :]
    return c


def _gdn_prep(proj, conv_wt, nh):
    lp = proj.shape[0]
    scale = HEAD_DIM ** -0.5

    def body(x_ref, w_ref, o_ref):
        which = pl.program_id(0) // nh
        c = _conv_taps(x_ref[...], w_ref[...])
        s = c * _sigmoid(c)
        r = lax.rsqrt(jnp.sum(s * s, axis=-1, keepdims=True) + EPS)
        f = jnp.where(which == 0, r * scale, jnp.where(which == 1, r, 1.0))
        o_ref[...] = jnp.where(_iota(s.shape, 0) >= PAD_ROWS, s * f, 0.0)

    return pl.pallas_call(
        body, grid=(3 * nh,),
        in_specs=[pl.BlockSpec((lp, LANES), lambda s: (0, s)), pl.BlockSpec((CONV_WIDTH, LANES), lambda s: (0, s))],
        out_specs=pl.BlockSpec((lp, LANES), lambda s: (0, s)),
        out_shape=jax.ShapeDtypeStruct((lp, 3 * nh * HEAD_DIM), F32), name="gdn_prep",
        compiler_params=_cp("parallel"))(proj, conv_wt)


def _gdn_prep_bwd(proj, conv_wt, dq, dk, dv, dproj, nh):
    lp = proj.shape[0]
    scale = HEAD_DIM ** -0.5
    part = lambda p: pl.BlockSpec((lp, LANES), lambda s: (0, jnp.clip(s - p * nh, 0, nh - 1)))

    def body(x_ref, w_ref, dq_ref, dk_ref, dv_ref, _, dx_ref, dw_ref):
        which = pl.program_id(0) // nh
        x = x_ref[...]
        w = w_ref[...]
        c = _conv_taps(x, w)
        sg = _sigmoid(c)
        s = c * sg
        r = lax.rsqrt(jnp.sum(s * s, axis=-1, keepdims=True) + EPS)
        dy = jnp.where(which == 0, dq_ref[...], jnp.where(which == 1, dk_ref[...], dv_ref[...]))
        dy = jnp.where(_iota(s.shape, 0) >= PAD_ROWS, dy, 0.0)
        y0 = s * r
        dy0 = dy * jnp.where(which == 0, scale, 1.0)
        ds_n = r * (dy0 - y0 * jnp.sum(dy0 * y0, axis=-1, keepdims=True))
        ds = jnp.where(which == 2, dy, ds_n)
        dc = ds * (sg * (1.0 + c * (1.0 - sg)))
        dx = dc * w[CONV_WIDTH - 1:CONV_WIDTH, :]
        rows = [jnp.sum(dc * x, axis=0, keepdims=True)]
        for j in range(CONV_WIDTH - 2, -1, -1):
            sh = CONV_WIDTH - 1 - j
            dx = dx + pltpu.roll(dc, lp - sh, 0) * w[j:j + 1, :]
            rows.insert(0, jnp.sum(dc * pltpu.roll(x, sh, 0), axis=0, keepdims=True))
        dx_ref[...] = dx.astype(BF16)
        dw_ref[...] = jnp.concatenate(rows, axis=0)

    strip = pl.BlockSpec((lp, LANES), lambda s: (0, s))
    taps = pl.BlockSpec((CONV_WIDTH, LANES), lambda s: (0, s))
    return pl.pallas_call(
        body, grid=(3 * nh,), in_specs=[strip, taps, part(0), part(1), part(2), _ANY], out_specs=[strip, taps],
        out_shape=[jax.ShapeDtypeStruct(dproj.shape, BF16), jax.ShapeDtypeStruct((CONV_WIDTH, 3 * nh * HEAD_DIM), F32)],
        input_output_aliases={5: 0}, name="gdn_prep_bwd", compiler_params=_cp("parallel"))(proj, conv_wt, dq, dk, dv, dproj)


def _gates(proj, bias_row, nega_row, nh):
    lp = proj.shape[0]
    nc = lp // CHUNK

    def body(p_ref, b_ref, a_ref, g_ref, gt3_ref, gtf_ref):
        lane = _iota((CHUNK, LANES), 1)
        tri = (_iota((CHUNK, CHUNK), 0) >= _iota((CHUNK, CHUNK), 1)).astype(F32)

        def step(n, carry):
            r0 = pl.multiple_of(n * CHUNK, CHUNK)
            z = p_ref[pl.ds(r0, CHUNK), :] + b_ref[...]
            base = jnp.where(lane < nh, _sigmoid(z),
                             jnp.where(lane < 2 * nh, a_ref[...] * _softplus(z),
                                       jnp.where(lane < 3 * nh, -_softplus(-z), 0.0)))
            base = jnp.where(r0 + _iota((CHUNK, LANES), 0) >= PAD_ROWS, base, 0.0)
            cs = _hdot(tri, base)
            run = jnp.where((lane >= 2 * nh) & (lane < 3 * nh), cs + carry, cs)
            sh = pltpu.roll(run, 2 * nh, 1)
            out = base + jnp.where((lane >= 3 * nh) & (lane < 5 * nh), sh, 0.0)
            g_ref[pl.ds(r0, CHUNK), :] = out
            gt3_ref[n] = out.T
            return carry + cs[CHUNK - 1:CHUNK, :]

        lax.fori_loop(0, nc, step, jnp.zeros((1, LANES), F32))
        gtf_ref[...] = g_ref[...].T

    vec = pl.BlockSpec((1, LANES), lambda i: (0, 0))
    return pl.pallas_call(
        body, grid=(1,), in_specs=[pl.BlockSpec((lp, LANES), lambda i: (0, 8 * nh)), vec, vec],
        out_specs=[pl.BlockSpec((lp, LANES), lambda i: (0, 0)), pl.BlockSpec((nc, LANES, CHUNK), lambda i: (0, 0, 0)),
                   pl.BlockSpec((LANES, lp), lambda i: (0, 0))],
        out_shape=[jax.ShapeDtypeStruct((lp, LANES), F32), jax.ShapeDtypeStruct((nc, LANES, CHUNK), F32),
                   jax.ShapeDtypeStruct((LANES, lp), F32)],
        name="gates", compiler_params=_cp("arbitrary"))(proj, bias_row, nega_row)


def _gates_bwd(proj, bias_row, nega_row, gates, dgate_gdn, dc_t, dproj, nh):
    lp = proj.shape[0]
    nc = lp // CHUNK

    def body(p_ref, b_ref, a_ref, g_ref, dg_ref, dc_ref, _, dz_ref, sm_ref, dct_scr):
        lane = _iota((CHUNK, LANES), 1)
        triu = (_iota((CHUNK, CHUNK), 0) <= _iota((CHUNK, CHUNK), 1)).astype(F32)
        dct_scr[...] = dc_ref[...].T
        sm_ref[...] = jnp.zeros_like(sm_ref)
        dz_ref[:, LANES:] = jnp.zeros((lp, NARROW - LANES), BF16)

        def step(i, carry):
            n = nc - 1 - i
            r0 = pl.multiple_of(n * CHUNK, CHUNK)
            z = p_ref[pl.ds(r0, CHUNK), :] + b_ref[...]
            gt = g_ref[pl.ds(r0, CHUNK), :]
            dgd = dg_ref[pl.ds(r0, CHUNK), :]
            dch = dct_scr[pl.ds(r0, CHUNK), :]
            rc = _hdot(triu, dch) + carry
            sg = _sigmoid(z)
            dz = jnp.where(lane < nh, dgd * sg * (1.0 - sg),
                           jnp.where(lane < 2 * nh, dgd * a_ref[...] * sg,
                                     jnp.where(lane < 3 * nh, rc * (1.0 - sg), 0.0)))
            dz = jnp.where(r0 + _iota((CHUNK, LANES), 0) >= PAD_ROWS, dz, 0.0)
            dz_ref[pl.ds(r0, CHUNK), 0:LANES] = dz.astype(BF16)
            sm_ref[0:1, :] += jnp.sum(dz, axis=0, keepdims=True)
            sm_ref[1:2, :] += jnp.sum(jnp.where((lane >= nh) & (lane < 2 * nh), dgd * gt, 0.0), axis=0, keepdims=True)
            return carry + jnp.sum(dch, axis=0, keepdims=True)

        lax.fori_loop(0, nc, step, jnp.zeros((1, LANES), F32))

    vec = pl.BlockSpec((1, LANES), lambda i: (0, 0))
    full = pl.BlockSpec((lp, LANES), lambda i: (0, 0))
    last = pl.BlockSpec((lp, LANES), lambda i: (0, 8 * nh))
    tail = pl.BlockSpec((lp, NARROW), lambda i: (0, 8 * nh * LANES // NARROW))
    return pl.pallas_call(
        body, grid=(1,), in_specs=[last, vec, vec, full, full, pl.BlockSpec((LANES, lp), lambda i: (0, 0)), _ANY],
        out_specs=[tail, pl.BlockSpec((8, LANES), lambda i: (0, 0))],
        out_shape=[jax.ShapeDtypeStruct(dproj.shape, BF16), jax.ShapeDtypeStruct((8, LANES), F32)],
        scratch_shapes=[pltpu.VMEM((lp, LANES), F32)], input_output_aliases={6: 0},
        name="gates_bwd", compiler_params=_cp("arbitrary"))(proj, bias_row, nega_row, gates, dgate_gdn, dc_t, dproj)


def _tri_inv(a):
    t = jnp.where(_iota(a.shape, 1) == _iota(a.shape, 2), 1.0, 0.0) - a
    p = a
    for _ in range(CHUNK.bit_length() - 2):
        ph, pw = _split(p)
        p = _dot3(ph, ph) + (_dot3(ph, pw) + _dot3(pw, ph))
        ph, pw = _split(p)
        th, tw = _split(t)
        t = t + (_dot3(th, ph) + (_dot3(th, pw) + _dot3(tw, ph)))
    return t


def _gdn_chunk(q, k, v, beta, gc, gr, t=None):
    ii, jj = _iota((1, CHUNK, CHUNK), 1), _iota((1, CHUNK, CHUNK), 2)
    causal, strict = ii >= jj, ii > jj
    dm = jnp.where(causal, jnp.exp(jnp.where(causal, gc - gr, 0.0)), 0.0)
    kk = _bdot3(k, k, "nt")
    a = jnp.where(strict, beta * kk * dm, 0.0)
    if t is None:
        t = _tri_inv(a)
    eg = jnp.exp(gc)
    glast = gc[:, CHUNK - 1:CHUNK, :]
    ekd = jnp.exp(glast - gc)
    bv = beta * v
    bk = (beta * eg) * k
    ub = _bdot3(t, jnp.concatenate([bv, bk], axis=2))
    qk = _bdot3(q, k, "nt")
    return dict(causal=causal, strict=strict, dm=dm, kk=kk, a=a, t=t, eg=eg, ekd=ekd, bv=bv, bk=bk,
                u=ub[:, :, :HEAD_DIM], w=ub[:, :, HEAD_DIM:], qk=qk, aqk=jnp.where(causal, qk * dm, 0.0),
                q_dec=q * eg, k_dec=k * ekd, decay=jnp.exp(glast))


def _heads(ref, nh):
    return jnp.stack([ref[:, h * HEAD_DIM:(h + 1) * HEAD_DIM] for h in range(nh)], axis=0)


def _gdn_chunk_inputs(q_ref, k_ref, v_ref, g, gt, nh):
    col = lambda o: jnp.stack([g[:, o + h:o + h + 1] for h in range(nh)], axis=0)
    gr = jnp.stack([gt[3 * nh + h:3 * nh + h + 1, :] for h in range(nh)], axis=0)
    return _heads(q_ref, nh), _heads(k_ref, nh), _heads(v_ref, nh), col(0), col(3 * nh), gr


def _gdn_fwd(qkv, gates, gt3, nh, rider=None):
    lp = qkv.shape[0]
    nc = lp // CHUNK
    w = nh * HEAD_DIM

    def body(start, finish, q_ref, k_ref, v_ref, g_ref, gt_ref, o_ref, sall_ref, tall_ref, s_scr):
        @pl.when(pl.program_id(0) == 0)
        def _():
            start()
            s_scr[...] = jnp.zeros_like(s_scr)
        c = _gdn_chunk(*_gdn_chunk_inputs(q_ref, k_ref, v_ref, g_ref[...], gt_ref[0], nh))
        s = s_scr[...]
        sall_ref[0] = s
        tall_ref[0] = c["t"]
        v_new = c["u"] - _bdot3(c["w"], s)
        o = _bdot3(c["q_dec"], s) + _bdot3(c["aqk"], v_new)
        s_scr[...] = s * c["decay"] + _bdot3(c["k_dec"], v_new, "tn")
        for h in range(nh):
            o_ref[:, h * HEAD_DIM:(h + 1) * HEAD_DIM] = o[h]
        pl.when(pl.program_id(0) == nc - 1)(finish)

    outs, got = _hosted_call(
        body, [rider], 5, 3, 1, grid=(nc,),
        in_specs=[pl.BlockSpec((CHUNK, w), lambda n: (n, 0)), pl.BlockSpec((CHUNK, w), lambda n: (n, 1)),
                  pl.BlockSpec((CHUNK, w), lambda n: (n, 2)), pl.BlockSpec((CHUNK, LANES), lambda n: (n, 0)),
                  pl.BlockSpec((1, LANES, CHUNK), lambda n: (n, 0, 0))],
        out_specs=[pl.BlockSpec((CHUNK, w), lambda n: (n, 0)),
                   pl.BlockSpec((1, nh, HEAD_DIM, HEAD_DIM), lambda n: (n, 0, 0, 0)),
                   pl.BlockSpec((1, nh, CHUNK, CHUNK), lambda n: (n, 0, 0, 0))],
        out_shape=[jax.ShapeDtypeStruct((lp, w), F32), jax.ShapeDtypeStruct((nc, nh, HEAD_DIM, HEAD_DIM), F32),
                   jax.ShapeDtypeStruct((nc, nh, CHUNK, CHUNK), F32)],
        scratch_shapes=[pltpu.VMEM((nh, HEAD_DIM, HEAD_DIM), F32)],
        name="gdn_fwd", compiler_params=_cp("arbitrary"))(qkv, qkv, qkv, gates, gt3)
    return outs, (got[0] if got else None)


def _gdn_bwd(qkv, gates, gt3, s_all, t_all, do, nh, rider=None):
    lp = qkv.shape[0]
    nc = lp // CHUNK
    w = nh * HEAD_DIM
    rev = lambda n: nc - 1 - n

    def body(start, finish, q_ref, k_ref, v_ref, g_ref, gt_ref, s_ref, t_ref, do_ref, dq_ref, dk_ref, dv_ref, dg_ref, ds_scr):
        @pl.when(pl.program_id(0) == 0)
        def _():
            start()
            ds_scr[...] = jnp.zeros_like(ds_scr)
        q, k, v, beta, gc, gr = _gdn_chunk_inputs(q_ref, k_ref, v_ref, g_ref[...], gt_ref[0], nh)
        c = _gdn_chunk(q, k, v, beta, gc, gr, t_ref[0])
        s = s_ref[0]
        dsn = ds_scr[...]
        dout = _heads(do_ref, nh)
        v_new = c["u"] - _bdot3(c["w"], s)
        dq_dec = _bdot3(dout, s, "nt")
        daqk = jnp.where(c["causal"], _bdot3(dout, v_new, "nt"), 0.0)
        dv_new = _bdot3(c["aqk"], dout, "tn") + _bdot3(c["k_dec"], dsn)
        dk_dec = _bdot3(v_new, dsn, "nt")
        ddecay = jnp.sum(jnp.sum(dsn * s, axis=2, keepdims=True), axis=1, keepdims=True)
        dw = -_bdot3(dv_new, s, "nt")
        ds_scr[...] = _bdot3(c["q_dec"], dout, "tn") + c["decay"] * dsn - _bdot3(c["w"], dv_new, "tn")
        duw = jnp.concatenate([dv_new, dw], axis=2)
        dt = _bdot3(duw, jnp.concatenate([c["bv"], c["bk"]], axis=2), "nt")
        dbvk = _bdot3(c["t"], duw, "tn")
        dbv, dbk = dbvk[:, :, :HEAD_DIM], dbvk[:, :, HEAD_DIM:]
        da = jnp.where(c["strict"], -_bdot3(_bdot3(c["t"], dt, "tn"), c["t"], "nt"), 0.0)
        dkk = da * beta * c["dm"]
        dqk = daqk * c["dm"]
        e = da * c["a"] + daqk * c["aqk"]
        dq = dq_dec * c["eg"] + _bdot3(dqk, k)
        dk = (dk_dec * c["ekd"] + _bdot3(dkk, k) + _bdot3(dkk, k, "tn") + _bdot3(dqk, q, "tn")
              + (beta * c["eg"]) * dbk)
        dv = beta * dbv
        rs = lambda x: jnp.sum(x, axis=2, keepdims=True)
        dbeta = rs(dbv * v) + c["eg"] * rs(dbk * k) + rs(da * c["kk"] * c["dm"])
        kd_term = rs(dk_dec * c["k_dec"])
        eh, ew = _split(e)
        ones = jnp.ones((nh, CHUNK, LANES), BF16)
        col_sums = (_dot3(eh, ones, "tn") + _dot3(ew, ones, "tn"))[:, :, 0:1]
        dg_cum = rs(dq_dec * c["q_dec"]) - kd_term + rs(dbk * c["bk"]) + rs(e) - col_sums
        last = jnp.sum(kd_term, axis=1, keepdims=True) + ddecay * c["decay"]
        dg_cum = dg_cum + jnp.where(_iota((1, CHUNK, 1), 1) == CHUNK - 1, last, 0.0)
        lane = _iota((CHUNK, LANES), 1)
        acc = jnp.zeros((CHUNK, LANES), F32)
        for h in range(nh):
            sl = slice(h * HEAD_DIM, (h + 1) * HEAD_DIM)
            dq_ref[:, sl] = dq[h]
            dk_ref[:, sl] = dk[h]
            dv_ref[:, sl] = dv[h]
            acc = acc + jnp.where(lane == h, dbeta[h], 0.0) + jnp.where(lane == nh + h, dg_cum[h], 0.0)
        triu = (_iota((CHUNK, CHUNK), 0) <= _iota((CHUNK, CHUNK), 1)).astype(F32)
        dg_ref[...] = jnp.where(lane < nh, acc, _hdot(triu, acc))
        pl.when(pl.program_id(0) == nc - 1)(finish)

    outs, got = _hosted_call(
        body, [rider], 8, 4, 1, grid=(nc,),
        in_specs=[pl.BlockSpec((CHUNK, w), lambda n: (rev(n), 0)), pl.BlockSpec((CHUNK, w), lambda n: (rev(n), 1)),
                  pl.BlockSpec((CHUNK, w), lambda n: (rev(n), 2)), pl.BlockSpec((CHUNK, LANES), lambda n: (rev(n), 0)),
                  pl.BlockSpec((1, LANES, CHUNK), lambda n: (rev(n), 0, 0)),
                  pl.BlockSpec((1, nh, HEAD_DIM, HEAD_DIM), lambda n: (rev(n), 0, 0, 0)),
                  pl.BlockSpec((1, nh, CHUNK, CHUNK), lambda n: (rev(n), 0, 0, 0)),
                  pl.BlockSpec((CHUNK, w), lambda n: (rev(n), 0))],
        out_specs=[pl.BlockSpec((CHUNK, w), lambda n: (rev(n), 0))] * 3 + [pl.BlockSpec((CHUNK, LANES), lambda n: (rev(n), 0))],
        out_shape=[jax.ShapeDtypeStruct((lp, w), F32)] * 3 + [jax.ShapeDtypeStruct((lp, LANES), F32)],
        scratch_shapes=[pltpu.VMEM((nh, HEAD_DIM, HEAD_DIM), F32)],
        name="gdn_bwd", compiler_params=_cp("arbitrary"))(qkv, qkv, qkv, gates, gt3, s_all, t_all, do)
    return outs, (got[0] if got else None)


def _merge_gdn(o_gdn, proj, norm_w, nh):
    lp = o_gdn.shape[0]

    def body(o_ref, z_ref, w_ref, m_ref):
        o = o_ref[...]
        z = z_ref[...]
        m_ref[...] = (o * _rms(o) * w_ref[...] * (z * _sigmoid(z))).astype(BF16)

    return pl.pallas_call(
        body, grid=(nh,),
        in_specs=[pl.BlockSpec((lp, LANES), lambda s: (0, s)), pl.BlockSpec((lp, LANES), lambda s: (0, 3 * nh + s)),
                  pl.BlockSpec((1, LANES), lambda s: (0, 0))],
        out_specs=pl.BlockSpec((lp, LANES), lambda s: (0, s)),
        out_shape=jax.ShapeDtypeStruct((lp, 2 * nh * HEAD_DIM), BF16), name="merge_gdn",
        compiler_params=_cp("parallel"))(o_gdn, proj, norm_w)


def _merge_gdn_bwd(o_gdn, proj, norm_w, dmerged, nh):
    lp = o_gdn.shape[0]

    def body(o_ref, z_ref, w_ref, dm_ref, do_ref, dz_ref, dw_ref):
        o = o_ref[...]
        r = _rms(o)
        xh = o * r
        silu, dsilu = _silu_and_grad(z_ref[...])
        dm = dm_ref[...]
        dn = dm * silu
        dz_ref[...] = (dm * (xh * w_ref[...]) * dsilu).astype(BF16)
        dnw = dn * w_ref[...]
        do_ref[...] = r * (dnw - xh * jnp.mean(dnw * xh, axis=-1, keepdims=True))

        @pl.when(pl.program_id(0) == 0)
        def _():
            dw_ref[...] = jnp.zeros_like(dw_ref)
        dw_ref[...] += jnp.sum(dn * xh, axis=0, keepdims=True)

    w = nh * HEAD_DIM
    return pl.pallas_call(
        body, grid=(nh,),
        in_specs=[pl.BlockSpec((lp, LANES), lambda s: (0, s)), pl.BlockSpec((lp, LANES), lambda s: (0, 3 * nh + s)),
                  pl.BlockSpec((1, LANES), lambda s: (0, 0)), pl.BlockSpec((lp, LANES), lambda s: (0, s))],
        out_specs=[pl.BlockSpec((lp, LANES), lambda s: (0, s)), pl.BlockSpec((lp, LANES), lambda s: (0, 3 * nh + s)),
                   pl.BlockSpec((1, LANES), lambda s: (0, 0))],
        out_shape=[jax.ShapeDtypeStruct((lp, w), F32), jax.ShapeDtypeStruct((lp, 8 * w + NARROW), BF16),
                   jax.ShapeDtypeStruct((1, LANES), F32)],
        name="merge_gdn_bwd", compiler_params=_cp("arbitrary"))(o_gdn, proj, norm_w, dmerged)


def _fox_prep(proj, qk_w, nh):
    lp = proj.shape[0]

    def body(x_ref, w_ref, o_ref):
        x = x_ref[...]
        o_ref[...] = x * _rms(x) * w_ref[0]

    return pl.pallas_call(
        body, grid=(2 * nh,),
        in_specs=[pl.BlockSpec((lp, LANES), lambda s: (0, 4 * nh + s)), pl.BlockSpec((1, 1, LANES), lambda s: (s // nh, 0, 0))],
        out_specs=pl.BlockSpec((lp, LANES), lambda s: (0, s)),
        out_shape=jax.ShapeDtypeStruct((lp, 2 * nh * HEAD_DIM), F32), name="fox_prep",
        compiler_params=_cp("parallel"))(proj, qk_w)


def _fox_prep_bwd(proj, qk_w, dq, dk, dproj, nh):
    lp = proj.shape[0]
    part = lambda p: pl.BlockSpec((lp, LANES), lambda s: (0, jnp.clip(s - p * nh, 0, nh - 1)))

    def body(x_ref, w_ref, dq_ref, dk_ref, _, dx_ref, dw_ref):
        x = x_ref[...]
        r = _rms(x)
        xh = x * r
        dy = jnp.where(pl.program_id(0) < nh, dq_ref[...], dk_ref[...])
        dyw = dy * w_ref[0]
        dx_ref[...] = (r * (dyw - xh * jnp.mean(dyw * xh, axis=-1, keepdims=True))).astype(BF16)

        @pl.when(pl.program_id(0) % nh == 0)
        def _():
            dw_ref[...] = jnp.zeros_like(dw_ref)
        dw_ref[0] += jnp.sum(dy * xh, axis=0, keepdims=True)

    strip = pl.BlockSpec((lp, LANES), lambda s: (0, 4 * nh + s))
    wsp = pl.BlockSpec((1, 1, LANES), lambda s: (s // nh, 0, 0))
    return pl.pallas_call(
        body, grid=(2 * nh,), in_specs=[strip, wsp, part(0), part(1), _ANY], out_specs=[strip, wsp],
        out_shape=[jax.ShapeDtypeStruct(dproj.shape, BF16), jax.ShapeDtypeStruct((2, 1, LANES), F32)],
        input_output_aliases={4: 0}, name="fox_prep_bwd", compiler_params=_cp("arbitrary"))(proj, qk_w, dq, dk, dproj)


def _fox_probs(q, k, gates, crow, h, i, nh, lse=None):
    kl = k.shape[0]
    lane = _iota((Q_BLOCK, LANES), 1)
    ct = jnp.sum(jnp.where(lane == 4 * nh + h, gates, 0.0), axis=1, keepdims=True)
    tq, kq = _iota((Q_BLOCK, Q_BLOCK), 0), _iota((Q_BLOCK, Q_BLOCK), 1)
    qs = q * (HEAD_DIM ** -0.5)
    if i == 0:
        s = _bdot(qs, k, "nt") + (ct - crow)
        s = jnp.where((kq <= tq) & ((kq >= PAD_ROWS) | (tq < PAD_ROWS)), s, NEG)
    else:
        crow = jnp.where(_iota((1, kl), 1) < PAD_ROWS, -NEG, crow)
        s = _bdot(qs, k, "nt") + (ct - crow)
        s = jnp.concatenate([s[:, :kl - Q_BLOCK], jnp.where(kq <= tq, s[:, kl - Q_BLOCK:], NEG)], axis=1)
    if lse is not None:
        return jnp.exp(s - lse)
    m = jnp.max(s, axis=1, keepdims=True)
    p = jnp.exp(s - m)
    tot = jnp.sum(p, axis=1, keepdims=True)
    return p / tot, m + jnp.log(tot)


FOX_HEADS_PER_STEP = 2


def _fox_specs(lp, nh):
    hw = FOX_HEADS_PER_STEP * LANES
    return [pl.BlockSpec((Q_BLOCK, hw), lambda g, i: (i, g)),
            pl.BlockSpec((lp, hw), lambda g, i: (0, nh // FOX_HEADS_PER_STEP + g)),
            pl.BlockSpec((lp, hw), lambda g, i: (0, 6 * nh // FOX_HEADS_PER_STEP + g)),
            pl.BlockSpec((Q_BLOCK, LANES), lambda g, i: (i, 0)),
            pl.BlockSpec((LANES, lp), lambda g, i: (0, 0))]


def _fox_fwd(qkn, proj, gates, gtf, nh):
    lp = qkn.shape[0]

    def body(q_ref, k_ref, v_ref, g_ref, gt_ref, o_ref, lse_ref):
        g, i = pl.program_id(0), pl.program_id(1)
        for j in range(lp // Q_BLOCK):
            @pl.when(i == j)
            def _(j=j):
                kl = (j + 1) * Q_BLOCK
                for hh in range(FOX_HEADS_PER_STEP):
                    h = FOX_HEADS_PER_STEP * g + hh
                    sl = slice(hh * LANES, (hh + 1) * LANES)
                    p, lse = _fox_probs(q_ref[:, sl], k_ref[0:kl, sl], g_ref[...], gt_ref[pl.ds(4 * nh + h, 1), :][:, 0:kl],
                                        h, j, nh)
                    o_ref[:, sl] = _bdot(p, v_ref[0:kl, sl])
                    lse_ref[:, sl] = jnp.broadcast_to(lse, (Q_BLOCK, LANES))

    blk = pl.BlockSpec((Q_BLOCK, FOX_HEADS_PER_STEP * LANES), lambda g, i: (i, g))
    return pl.pallas_call(
        body, grid=(nh // FOX_HEADS_PER_STEP, lp // Q_BLOCK), in_specs=_fox_specs(lp, nh), out_specs=[blk, blk],
        out_shape=[jax.ShapeDtypeStruct((lp, nh * HEAD_DIM), F32)] * 2, name="fox_fwd",
        compiler_params=_cp("parallel", "parallel"))(qkn, qkn, proj, gates, gtf)


def _fox_bwd(qkn, proj, gates, gtf, lse, do, dproj, nh):
    lp = qkn.shape[0]
    nq = lp // Q_BLOCK
    w = nh * HEAD_DIM
    scale = HEAD_DIM ** -0.5

    def body(q_ref, k_ref, v_ref, g_ref, gt_ref, lse_ref, do_ref, _, dq_ref, dk_ref, dc_ref, dv_ref, dv_scr):
        g, i = pl.program_id(0), pl.program_id(1)

        @pl.when(i == 0)
        def _():
            dk_ref[...] = jnp.zeros_like(dk_ref)
            dv_scr[...] = jnp.zeros_like(dv_scr)
            dc_ref[...] = jnp.zeros_like(dc_ref)
        for j in range(nq):
            @pl.when(i == j)
            def _(j=j):
                kl = (j + 1) * Q_BLOCK
                for hh in range(FOX_HEADS_PER_STEP):
                    h = FOX_HEADS_PER_STEP * g + hh
                    sl = slice(hh * LANES, (hh + 1) * LANES)
                    q, k = q_ref[:, sl], k_ref[0:kl, sl]
                    p = _fox_probs(q, k, g_ref[...], gt_ref[pl.ds(4 * nh + h, 1), :][:, 0:kl], h, j, nh,
                                   lse_ref[:, sl][:, 0:1])
                    dout = do_ref[:, sl]
                    dp = _bdot(dout, v_ref[0:kl, sl], "nt")
                    ds = p * (dp - jnp.sum(p * dp, axis=1, keepdims=True))
                    dq_ref[:, sl] = _bdot(ds, k) * scale
                    dk_ref[0:kl, sl] += _bdot(ds, q * scale, "tn")
                    dv_scr[0:kl, sl] += _bdot(p, dout, "tn")
                    dc_ref[hh, :, 0:kl] -= jnp.sum(ds, axis=0, keepdims=True)

        @pl.when(i == nq - 1)
        def _():
            dv_ref[...] = dv_scr[...].astype(BF16)

    hw = FOX_HEADS_PER_STEP * LANES
    blk = pl.BlockSpec((Q_BLOCK, hw), lambda g, i: (i, g))
    col = pl.BlockSpec((lp, hw), lambda g, i: (0, g))
    return pl.pallas_call(
        body, grid=(nh // FOX_HEADS_PER_STEP, nq), in_specs=_fox_specs(lp, nh) + [blk, blk, _ANY],
        out_specs=[blk, col, pl.BlockSpec((FOX_HEADS_PER_STEP, 1, lp), lambda g, i: (g, 0, 0)),
                   pl.BlockSpec((lp, hw), lambda g, i: (0, 6 * nh // FOX_HEADS_PER_STEP + g))],
        out_shape=[jax.ShapeDtypeStruct((lp, w), F32)] * 2 + [jax.ShapeDtypeStruct((nh, 1, lp), F32),
                                                             jax.ShapeDtypeStruct(dproj.shape, BF16)],
        scratch_shapes=[pltpu.VMEM((lp, hw), F32)], input_output_aliases={7: 3},
        name="fox_bwd", compiler_params=_cp("parallel", "arbitrary"))(qkn, qkn, proj, gates, gtf, lse, do, dproj)


def _merge_fox(o_fox, proj, merged, nh):
    lp = o_fox.shape[0]

    def body(o_ref, z_ref, _, m_ref):
        z = z_ref[...]
        m_ref[...] = (o_ref[...] * (z * _sigmoid(z))).astype(BF16)

    return pl.pallas_call(
        body, grid=(nh,),
        in_specs=[pl.BlockSpec((lp, LANES), lambda s: (0, s)), pl.BlockSpec((lp, LANES), lambda s: (0, 7 * nh + s)), _ANY],
        out_specs=pl.BlockSpec((lp, LANES), lambda s: (0, nh + s)),
        out_shape=jax.ShapeDtypeStruct(merged.shape, BF16), input_output_aliases={2: 0}, name="merge_fox",
        compiler_params=_cp("parallel"))(o_fox, proj, merged)


def _merge_fox_bwd(o_fox, proj, dmerged, dproj, nh):
    lp = o_fox.shape[0]

    def body(o_ref, z_ref, dm_ref, _, do_ref, dz_ref):
        silu, dsilu = _silu_and_grad(z_ref[...])
        dm = dm_ref[...]
        do_ref[...] = dm * silu
        dz_ref[...] = (dm * o_ref[...] * dsilu).astype(BF16)

    w = nh * HEAD_DIM
    return pl.pallas_call(
        body, grid=(nh,),
        in_specs=[pl.BlockSpec((lp, LANES), lambda s: (0, s)), pl.BlockSpec((lp, LANES), lambda s: (0, 7 * nh + s)),
                  pl.BlockSpec((lp, LANES), lambda s: (0, nh + s)), _ANY],
        out_specs=[pl.BlockSpec((lp, LANES), lambda s: (0, s)), pl.BlockSpec((lp, LANES), lambda s: (0, 7 * nh + s))],
        out_shape=[jax.ShapeDtypeStruct((lp, w), F32), jax.ShapeDtypeStruct(dproj.shape, BF16)],
        input_output_aliases={3: 1}, name="merge_fox_bwd", compiler_params=_cp("parallel"))(o_fox, proj, dmerged, dproj)


def _post(out, x, target, post_w):
    lp, d = out.shape

    def body(o_ref, x_ref, t_ref, w_ref, dy_ref, do_ref, loss_ref, dw_ref):
        i = pl.program_id(0)

        @pl.when(i == 0)
        def _():
            loss_ref[...] = jnp.zeros_like(loss_ref)
            dw_ref[...] = jnp.zeros_like(dw_ref)
        o = o_ref[...]
        r = _rms(o)
        nrm = o * r
        err = jnp.where(i > 0, x_ref[...] + nrm * w_ref[...] - t_ref[...], 0.0)
        loss_ref[0:1, :] += 0.5 * jnp.sum(jnp.sum(err * err, axis=1, keepdims=True), axis=0, keepdims=True) / d
        dy = err / d
        dy_ref[...] = dy
        dw_ref[...] += jnp.sum(dy * nrm, axis=0, keepdims=True)
        dyw = dy * w_ref[...]
        do_ref[...] = (r * (dyw - nrm * jnp.mean(dyw * nrm, axis=-1, keepdims=True))).astype(BF16)

    row = pl.BlockSpec((Q_BLOCK, d), lambda i: (i, 0))
    vec = pl.BlockSpec((1, d), lambda i: (0, 0))
    return pl.pallas_call(
        body, grid=(lp // Q_BLOCK,), in_specs=[row, _x_rows(d), _x_rows(d), vec],
        out_specs=[_x_rows(d), row, pl.BlockSpec((8, LANES), lambda i: (0, 0)), vec],
        out_shape=[jax.ShapeDtypeStruct(x.shape, F32), jax.ShapeDtypeStruct((lp, d), BF16),
                   jax.ShapeDtypeStruct((8, LANES), F32), jax.ShapeDtypeStruct((1, d), F32)],
        name="post", compiler_params=_cp("arbitrary"))(out, x, target, post_w)


def _prenorm_bwd(dxn, x, meta, w, dy, rider=None):
    seq, d = x.shape
    lp = seq + Q_BLOCK

    def body(start, finish, dx_ref, x_ref, m_ref, w_ref, dy_ref, gx_ref, gm_ref, dw_ref):
        i = pl.program_id(0)
        pl.when(i == 0)(start)
        h = _h_tile(i, x_ref, m_ref)
        r = _rms(h)
        xh = h * r
        dxn_ = dx_ref[...]
        dxw = dxn_ * w_ref[...]
        dh = jnp.where(i > 0, dy_ref[...], 0.0) + r * (dxw - xh * jnp.mean(dxw * xh, axis=-1, keepdims=True))
        gx_ref[...] = dh

        @pl.when(i == 0)
        def _():
            dw_ref[...] = jnp.zeros_like(dw_ref)
            gm_ref[...] = dh[PAD_ROWS:, :]
        dw_ref[...] += jnp.sum(dxn_ * xh, axis=0, keepdims=True)
        pl.when(i == lp // Q_BLOCK - 1)(finish)

    vec = pl.BlockSpec((1, d), lambda i: (0, 0))
    met = pl.BlockSpec((N_META, d), lambda i: (0, 0))
    outs, got = _hosted_call(
        body, [rider], 5, 3, 0, grid=(lp // Q_BLOCK,),
        in_specs=[pl.BlockSpec((Q_BLOCK, d), lambda i: (i, 0)), _x_rows(d), met, vec, _x_rows(d)],
        out_specs=[_x_rows(d), met, vec],
        out_shape=[jax.ShapeDtypeStruct((seq, d), F32), jax.ShapeDtypeStruct((N_META, d), F32),
                   jax.ShapeDtypeStruct((1, d), F32)],
        name="prenorm_bwd", compiler_params=_cp("arbitrary"))(dxn, x, meta, w, dy)
    return outs, (got[0] if got else None)


def _layer_grads(x, target, meta, pre_w, wfull, conv_wt, a_log, dt_bias, gdn_norm_w, fq_w, fk_w, f_bias, w_out, post_w,
                 late_weights=None, w_out_grads=None):
    nh = a_log.shape[1]
    zpad = jnp.zeros((1, LANES - 3 * nh), F32)
    bias_row = jnp.concatenate([jnp.zeros((1, nh), F32), dt_bias, f_bias, zpad], axis=1)
    nega_row = jnp.concatenate([jnp.zeros((1, nh), F32), -jnp.exp(a_log), jnp.zeros((1, nh), F32), zpad], axis=1)
    qk_w = jnp.stack([fq_w, fk_w])

    xn = _prenorm(x, meta, pre_w)
    proj = _matmul(xn, wfull, "nn", MM_TILE, F32, "proj")
    qkv = _gdn_prep(proj, conv_wt, nh)
    gates, gt3, gtf = _gates(proj, bias_row, nega_row, nh)
    (o_gdn, s_all, t_all), got = _gdn_fwd(qkv, gates, gt3, nh, None if late_weights is None else late_weights[0])
    if late_weights is not None:
        w_out = late_weights[1](got)
    qkn = _fox_prep(proj, qk_w, nh)
    o_fox, fox_lse = _fox_fwd(qkn, proj, gates, gtf, nh)
    merged = _merge_fox(o_fox, proj, _merge_gdn(o_gdn, proj, gdn_norm_w, nh), nh)
    out = _matmul(merged, w_out, "nn", 4 * LANES, F32, "out_proj")
    dy, dout, loss_blk, dpost_w = _post(out, x, target, post_w)

    dw_out = _matmul(merged, dout, "tn", 4 * LANES, BF16, "dw_out")
    if w_out_grads is None:
        dmerged, gdn_rider = _matmul(dout, w_out, "nt", 4 * LANES, F32, "dmerged"), None
    else:
        dmerged, got = _matmul(dout, w_out, "nt", 4 * LANES, F32, "dmerged", w_out_grads[0](dw_out))
        gdn_rider = w_out_grads[1](dw_out, got)
    do_gdn, dproj, dgdn_norm_w = _merge_gdn_bwd(o_gdn, proj, gdn_norm_w, dmerged, nh)
    do_fox, dproj = _merge_fox_bwd(o_fox, proj, dmerged, dproj, nh)
    dqn, dkn, dc_t, dproj = _fox_bwd(qkn, proj, gates, gtf, fox_lse, do_fox, dproj, nh)
    dproj, dqk_w = _fox_prep_bwd(proj, qk_w, dqn, dkn, dproj, nh)
    (dgq, dgk, dgv, dgate), w_out_parts = _gdn_bwd(qkv, gates, gt3, s_all, t_all, do_gdn, nh, gdn_rider)
    dproj, dconv_wt = _gdn_prep_bwd(proj, conv_wt, dgq, dgk, dgv, dproj, nh)
    dc_rows = jnp.pad(dc_t.reshape(nh, -1), ((2 * nh, LANES - 3 * nh), (0, 0)))
    dproj, gate_sums = _gates_bwd(proj, bias_row, nega_row, gates, dgate, dc_rows, dproj, nh)
    return dict(
        loss=loss_blk[0:1, 0:1], dy=dy, xn=xn, dproj=dproj, post_w=dpost_w,
        conv_wt=dconv_wt, a_log=gate_sums[1:2, nh:2 * nh], dt_bias=gate_sums[0:1, nh:2 * nh],
        gdn_norm_w=dgdn_norm_w, fq_w=dqk_w[0], fk_w=dqk_w[1], f_bias=gate_sums[0:1, 2 * nh:3 * nh], w_out=dw_out,
        w_out_parts=w_out_parts)


def _cast_bf16(a, tr, name):
    r, c = a.shape

    def body(a_ref, o_ref):
        o_ref[...] = a_ref[...].astype(BF16)

    return pl.pallas_call(
        body, grid=(r // tr,), in_specs=[pl.BlockSpec((tr, c), lambda i: (i, 0))],
        out_specs=pl.BlockSpec((tr, c), lambda i: (i, 0)), out_shape=jax.ShapeDtypeStruct((r, c), BF16),
        name=name, compiler_params=_cp("parallel"))(a)


def _column_major(a):
    return jnp.transpose(a, (2, 0, 1))


def _cast_bf16_column_major(a3, name):
    _, r, c = a3.shape

    def body(a_ref, o_ref):
        o_ref[...] = a_ref[...].reshape(LANES, r).T.astype(BF16)

    return pl.pallas_call(
        body, grid=(pl.cdiv(c, LANES),), in_specs=[pl.BlockSpec((LANES, 1, r), lambda i: (i, 0, 0))],
        out_specs=pl.BlockSpec((r, LANES), lambda i: (0, i)), out_shape=jax.ShapeDtypeStruct((r, c), BF16),
        name=name, compiler_params=_cp("parallel"))(_column_major(a3))


def _adamw_column_major(w3, parts, m3, v3, name):
    _, r, c = w3.shape
    n_parts = parts[0].shape[0]

    def body(w_ref, *refs):
        p_refs, (m_ref, v_ref, g_ref, d_ref, nm_ref, nv_ref) = refs[:len(parts)], refs[len(parts):]
        sums = []
        for p_ref in p_refs:
            g = p_ref[0].astype(F32)
            for s in range(1, n_parts):
                g = g + p_ref[s].astype(F32)
            sums.append(g)
        g = jnp.concatenate(sums, axis=0).T
        flat = lambda ref: ref[...].reshape(LANES, r)
        m_new = ADAM_B1 * flat(m_ref) + (1.0 - ADAM_B1) * g
        v_new = ADAM_B2 * flat(v_ref) + (1.0 - ADAM_B2) * (g * g)
        m_hat = m_new / (1.0 - ADAM_B1 ** ADAM_STEP)
        v_hat = v_new / (1.0 - ADAM_B2 ** ADAM_STEP)
        delta = -ADAM_LR * (m_hat / (jnp.sqrt(v_hat) + ADAM_EPS) + ADAM_WD * flat(w_ref))
        for ref, val in ((g_ref, g), (d_ref, delta), (nm_ref, m_new), (nv_ref, v_new)):
            ref[...] = val.reshape(LANES, 1, r)

    blk = pl.BlockSpec((LANES, 1, r), lambda i: (i, 0, 0))
    outs = pl.pallas_call(
        body, grid=(pl.cdiv(c, LANES),),
        in_specs=[blk] + [pl.BlockSpec((n_parts, p.shape[1], LANES), lambda i: (0, 0, i)) for p in parts] + [blk, blk],
        out_specs=[blk] * 4, out_shape=[jax.ShapeDtypeStruct((c, 1, r), F32)] * 4, name=name,
        compiler_params=_cp("parallel"))(_column_major(w3), *parts, _column_major(m3), _column_major(v3))
    return [jnp.transpose(o, (1, 2, 0)) for o in outs]


def _gather_copies(ins, outs, send_sems, recv_sems, local_sems):
    n = len(ins)
    x, y, c = lax.axis_index("x"), lax.axis_index("y"), lax.axis_index("c")
    me, sibling = (x, y, c), (x, y, 1 - c)
    xn, yn, dg = (1 - x, y), (x, 1 - y), (1 - x, 1 - y)

    def copy(a, k, block, to, src=None):
        px, py, pc = block
        rows = outs[a].at[4 * px + 2 * py + pc]
        return pltpu.make_async_remote_copy(
            src_ref=rows if src is None else src, dst_ref=rows, send_sem=send_sems.at[a, k],
            recv_sem=recv_sems.at[a, k], device_id=to, device_id_type=_MESH)

    local = [pltpu.make_async_copy(ins[a], outs[a].at[4 * x + 2 * y + c], local_sems.at[a]) for a in range(n)]
    own = [cp for a in range(n) for cp in (copy(a, 0, me, sibling, src=ins[a]), copy(a, 1, me, (*xn, c), src=ins[a]),
                                           copy(a, 2, me, (*yn, c), src=ins[a]))]

    def start():
        for cp in local + own:
            cp.start()

    def finish():
        for a in range(n):
            @pl.when(c == 1)
            def _(a=a):
                copy(a, 1, (*xn, c), me).wait_recv()
                copy(a, 3, (*xn, c), (*yn, c)).start()

            @pl.when(c == 0)
            def _(a=a):
                copy(a, 2, (*yn, c), me).wait_recv()
                copy(a, 3, (*yn, c), (*xn, c)).start()
        for a in range(n):
            pl.when(c == 0)(copy(a, 1, (*xn, c), me).wait_recv)
            copy(a, 4, (*xn, c), sibling).start()
            pl.when(c == 1)(copy(a, 2, (*yn, c), me).wait_recv)
            copy(a, 5, (*yn, c), sibling).start()
        for a in range(n):
            copy(a, 3, (*dg, c), me).wait_recv()
            copy(a, 6, (*dg, c), sibling).start()
        for a in range(n):
            copy(a, 0, sibling, me).wait_recv()
            for k, chip in ((4, xn), (5, yn), (6, dg)):
                copy(a, k, (*chip, 1 - c), me).wait_recv()
                copy(a, k, (*chip, c), sibling).wait_send()
            copy(a, 3, (*xn, c), (*yn, c)).wait_send()
        for cp in own:
            cp.wait_send()
        for cp in local:
            cp.wait()

    return start, finish


def _gather_scratch(n):
    return [pltpu.SemaphoreType.DMA((n, N_DEV - 1)), pltpu.SemaphoreType.DMA((n, N_DEV - 1)), pltpu.SemaphoreType.DMA((n,))]


def _gather_rider(arrays):
    return _Rider(list(arrays), [jax.ShapeDtypeStruct((N_DEV,) + a.shape, a.dtype) for a in arrays],
                  _gather_scratch(len(arrays)), {}, lambda ins, outs, scratch: _gather_copies(ins, outs, *scratch))


def _all_gather(arrays, name):
    n = len(arrays)

    def body(*refs):
        start, finish = _gather_copies(refs[:n], refs[n:2 * n], *refs[2 * n:])
        start()
        finish()

    return pl.pallas_call(
        body, in_specs=[_ANY] * n, out_specs=[_ANY] * n,
        out_shape=[jax.ShapeDtypeStruct((N_DEV,) + a.shape, a.dtype) for a in arrays],
        scratch_shapes=_gather_scratch(n), name=name)(*arrays)


SLAB = 10 * LANES


def _slab_start(blk, nh, cols):
    in_second_half = blk >= N_DEV // 2
    shift = (2 * nh if in_second_half else 0) if isinstance(blk, int) else jnp.where(in_second_half, 2 * nh, 0)
    return (blk * cols - shift) // LANES * LANES


def _pair_rider(dw_rows=None, parts=None, nh=None, after=None):
    if dw_rows is not None:
        r, full = dw_rows.shape
        cols = (full - NARROW + 3 * nh) // N_DEV
        out_shapes = [jax.ShapeDtypeStruct((N_CHIP, r, SLAB), dw_rows.dtype), jax.ShapeDtypeStruct((r, NARROW), dw_rows.dtype)]
    else:
        out_shapes = [jax.ShapeDtypeStruct((N_CHIP,) + parts.shape[1:], parts.dtype)]

    def make(ins, outs, scratch):
        send_sems, recv_sems = scratch
        x, y, c = lax.axis_index("x"), lax.axis_index("y"), lax.axis_index("c")
        kw = lambda k: dict(send_sem=send_sems.at[k], recv_sem=recv_sems.at[k], device_id=(x, y, 1 - c), device_id_type=_MESH)
        copies = []
        for q in range(N_CHIP):
            if dw_rows is not None:
                first = pl.multiple_of(_slab_start(2 * q + 1 - c, nh, cols), LANES)
                copies.append(pltpu.make_async_remote_copy(src_ref=ins[0].at[:, pl.ds(first, SLAB)], dst_ref=outs[0].at[q], **kw(q)))
            else:
                copies.append(pltpu.make_async_remote_copy(src_ref=ins[0].at[2 * q + 1 - c], dst_ref=outs[0].at[q], **kw(q)))
        if dw_rows is not None:
            copies.append(pltpu.make_async_remote_copy(src_ref=ins[0].at[:, pl.ds(full - NARROW, NARROW)], dst_ref=outs[1],
                                                       **kw(N_CHIP)))

        def start():
            for cp in copies:
                cp.start()

        def finish():
            for cp in copies:
                cp.wait()

        return start, finish

    return _Rider([dw_rows if dw_rows is not None else parts] + ([] if after is None else [after]), out_shapes,
                  [pltpu.SemaphoreType.DMA((N_CHIP + 1,)), pltpu.SemaphoreType.DMA((N_CHIP + 1,))], {}, make)


def _relayout_pair_sum(dwfull, got_slabs, got_tail, core, nh, tr, name):
    d, full = dwfull.shape
    w = nh * HEAD_DIM
    cols = (8 * w + 3 * nh) // N_DEV
    segs = _native_segments(nh)

    def block(f_ref, s_ref, t_ref, q, blk):
        st = _slab_start(blk, nh, cols)
        wide = f_ref[:, st:st + SLAB].astype(F32) + s_ref[q].astype(F32)
        tail = f_ref[:, 8 * w:].astype(F32) + t_ref[...].astype(F32)
        pieces = []
        for s0, s1, t0 in segs:
            lo, hi = max(s0, blk * cols), min(s1, (blk + 1) * cols)
            if lo < hi:
                at = t0 + lo - s0
                pieces.append(tail[:, at - 8 * w:at - 8 * w + hi - lo] if at >= 8 * w else wide[:, at - st:at - st + hi - lo])
        return (pieces[0] if len(pieces) == 1 else jnp.concatenate(pieces, axis=1)).astype(dwfull.dtype)

    def body(core_ref, f_ref, s_ref, t_ref, o_ref):
        for parity in range(2):
            @pl.when(core_ref[0] == parity)
            def _(parity=parity):
                for q in range(N_CHIP):
                    o_ref[q] = block(f_ref, s_ref, t_ref, q, 2 * q + parity)

    return pl.pallas_call(
        body,
        grid_spec=pltpu.PrefetchScalarGridSpec(
            num_scalar_prefetch=1, grid=(d // tr,),
            in_specs=[pl.BlockSpec((tr, full), lambda i, c_ref: (i, 0)), pl.BlockSpec((N_CHIP, tr, SLAB), lambda i, c_ref: (0, i, 0)),
                      pl.BlockSpec((tr, NARROW), lambda i, c_ref: (i, 0))],
            out_specs=pl.BlockSpec((N_CHIP, tr, cols), lambda i, c_ref: (0, i, 0))),
        out_shape=jax.ShapeDtypeStruct((N_CHIP, d, cols), dwfull.dtype), name=name,
        compiler_params=_cp("parallel"))(core, dwfull, got_slabs, got_tail)


def _pair_sum(parts, got, core, tr, name):
    _, r, c = parts.shape

    def body(core_ref, p_ref, g_ref, o_ref):
        o_ref[...] = (p_ref[...].astype(F32) + g_ref[...].astype(F32)).astype(o_ref.dtype)

    return pl.pallas_call(
        body,
        grid_spec=pltpu.PrefetchScalarGridSpec(
            num_scalar_prefetch=1, grid=(N_CHIP, r // tr),
            in_specs=[pl.BlockSpec((1, tr, c), lambda q, i, core_ref: (2 * q + core_ref[0], i, 0)),
                      pl.BlockSpec((1, tr, c), lambda q, i, core_ref: (q, i, 0))],
            out_specs=pl.BlockSpec((1, tr, c), lambda q, i, core_ref: (q, i, 0))),
        out_shape=jax.ShapeDtypeStruct((N_CHIP, r, c), parts.dtype), name=name,
        compiler_params=_cp("parallel", "parallel"))(core, parts, got)


def _native_segments(nh):
    w = nh * HEAD_DIM
    return [(0, 4 * w, 0), (4 * w, 4 * w + 2 * nh, 8 * w), (4 * w + 2 * nh, 8 * w + 2 * nh, 4 * w),
            (8 * w + 2 * nh, 8 * w + 3 * nh, 8 * w + 2 * nh)]


def _relayout_w_in(wg, nh, tr):
    _, d, cols = wg.shape
    w = nh * HEAD_DIM

    def native(ref, j0, j1):
        out = []
        while j0 < j1:
            blk = j0 // cols
            end = min(j1, (blk + 1) * cols)
            out.append(ref[blk, :, pl.ds(j0 - blk * cols, end - j0)])
            j0 = end
        return out

    def body(g_ref, o_ref):
        for cidx in range(8 * w // LANES):
            j0 = cidx * LANES + (0 if cidx * LANES < 4 * w else 2 * nh)
            pieces = native(g_ref, j0, j0 + LANES)
            o_ref[:, cidx * LANES:(cidx + 1) * LANES] = pieces[0] if len(pieces) == 1 else jnp.concatenate(pieces, axis=1)
        pieces = (native(g_ref, 4 * w, 4 * w + 2 * nh) + native(g_ref, 8 * w + 2 * nh, 8 * w + 3 * nh)
                  + [jnp.zeros((tr, NARROW - 3 * nh), wg.dtype)])
        o_ref[:, 8 * w:] = jnp.concatenate(pieces, axis=1)

    return pl.pallas_call(
        body, grid=(d // tr,), in_specs=[pl.BlockSpec((N_DEV, tr, cols), lambda i: (0, i, 0))],
        out_specs=pl.BlockSpec((tr, 8 * w + NARROW), lambda i: (i, 0)),
        out_shape=jax.ShapeDtypeStruct((d, 8 * w + NARROW), wg.dtype),
        name="relayout_w_in", compiler_params=_cp("parallel"))(wg)


def _adamw(w, parts, m, v, tr, name):
    r, c = w.shape
    n_parts = parts.shape[0]

    def body(w_ref, p_ref, m_ref, v_ref, g_ref, d_ref, nm_ref, nv_ref):
        g = p_ref[0].astype(F32)
        for s in range(1, n_parts):
            g = g + p_ref[s].astype(F32)
        m_new = ADAM_B1 * m_ref[...] + (1.0 - ADAM_B1) * g
        v_new = ADAM_B2 * v_ref[...] + (1.0 - ADAM_B2) * (g * g)
        m_hat = m_new / (1.0 - ADAM_B1 ** ADAM_STEP)
        v_hat = v_new / (1.0 - ADAM_B2 ** ADAM_STEP)
        g_ref[...] = g
        d_ref[...] = -ADAM_LR * (m_hat / (jnp.sqrt(v_hat) + ADAM_EPS) + ADAM_WD * w_ref[...])
        nm_ref[...] = m_new
        nv_ref[...] = v_new

    blk = pl.BlockSpec((tr, c), lambda i: (i, 0))
    return pl.pallas_call(
        body, grid=(r // tr,), in_specs=[blk, pl.BlockSpec((n_parts, tr, c), lambda i: (0, i, 0)), blk, blk],
        out_specs=[blk] * 4, out_shape=[jax.ShapeDtypeStruct((r, c), F32)] * 4, name=name,
        compiler_params=_cp("parallel"))(w, parts, m, v)


def _pack_small(d, pre, post, a_log, dt_bias, f_bias, gdn_w, fq_w, fk_w, extra):
    row2 = jnp.concatenate([a_log, dt_bias, f_bias, gdn_w, fq_w, fk_w, extra], axis=1)
    row2 = jnp.pad(row2, ((0, 0), (0, d - row2.shape[1])))
    return jnp.concatenate([pre, post, row2, jnp.zeros((5, d), F32)], axis=0)


def _unpack_small(p, nh):
    o = 3 * nh
    return dict(pre=p[0:1], post=p[1:2], a_log=p[2:3, 0:nh], dt_bias=p[2:3, nh:2 * nh], f_bias=p[2:3, 2 * nh:o],
                gdn_w=p[2:3, o:o + HEAD_DIM], fq_w=p[2:3, o + HEAD_DIM:o + 2 * HEAD_DIM],
                fk_w=p[2:3, o + 2 * HEAD_DIM:o + 3 * HEAD_DIM], extra=p[2, o + 3 * HEAD_DIM])


def kernel(x, meta_tokens, pre_norm_w, w_in, conv_w, a_log, dt_bias, gdn_norm_w, fox_q_norm_w, fox_k_norm_w, fox_f_bias, w_out, post_norm_w, loss_target, m_meta_tokens, m_pre_norm_w, m_w_in, m_conv_w, m_a_log, m_dt_bias, m_gdn_norm_w, m_fox_q_norm_w, m_fox_k_norm_w, m_fox_f_bias, m_w_out, m_post_norm_w, v_meta_tokens, v_pre_norm_w, v_w_in, v_conv_w, v_a_log, v_dt_bias, v_gdn_norm_w, v_fox_q_norm_w, v_fox_k_norm_w, v_fox_f_bias, v_w_out, v_post_norm_w):
    nh = a_log.shape[1]
    d = x.shape[-1]
    w = nh * HEAD_DIM
    zero = jnp.zeros((1, 1), F32)

    wg, cg, mg = _all_gather([_cast_bf16_column_major(w_in, "cast_w_in"), conv_w[0].T, meta_tokens], "gather_weights")
    wfull = _relayout_w_in(wg, nh, 256)
    conv_wt = cg.transpose(1, 0, 2).reshape(CONV_WIDTH, 3 * w)
    meta_full = mg.transpose(1, 0, 2).reshape(N_META, d)
    late_weights = (_gather_rider([_cast_bf16(w_out[0], 256, "cast_w_out")]), lambda got: got[0].reshape(2 * w, d))
    core = lax.axis_index("c")
    dev = 4 * lax.axis_index("x") + 2 * lax.axis_index("y") + core
    core_arr = jnp.reshape(core, (1,)).astype(jnp.int32)

    out_parts = lambda dw_out: dw_out.reshape(N_DEV, 2 * w // N_DEV, d)
    g = _layer_grads(
        x[0], loss_target[0], meta_full, pre_norm_w, wfull, conv_wt, a_log, dt_bias, gdn_norm_w,
        fox_q_norm_w, fox_k_norm_w, fox_f_bias, None, post_norm_w, late_weights=late_weights,
        w_out_grads=(lambda dw_out: _pair_rider(parts=out_parts(dw_out)),
                     lambda dw_out, got: _chip_rider(_pair_sum(out_parts(dw_out), got[0], core_arr, 256, "pair_sum_w_out"))))
    p_out = g["w_out_parts"][0]
    xn, dproj = g["xn"], g["dproj"]
    flights, token, dw, rider = [], None, None, None

    def exchange(dw, got, i):
        sums = _relayout_pair_sum(dw, got[0], got[1], core_arr, nh, 128, f"relayout_pair_sum_{i}")
        flight, token = _chip_exchange_start(sums, f"chip_exchange_start_{i}")
        flights.append(flight)
        return token

    for i, (index, parts) in enumerate(DW_IN_PIECES):
        res = _matmul(xn, dproj, "tn", MM_TILE, BF16, f"dw_in_{i}", rider, a_cols=(d // parts, index))
        if i:
            token = exchange(dw, res[1], i - 1)
        dw = res[0] if i else res
        rider = _pair_rider(dw_rows=dw, nh=nh, after=token)
    dxn, (got,) = _dxn(dproj, wfull, [rider], MM_TILE, "dxn")
    token = exchange(dw, got, len(DW_IN_PIECES) - 1)
    (grad_x, dmeta, dpre_w), _ = _prenorm_bwd(dxn, x[0], meta_full, pre_norm_w + token[0:1, 0:1], g["dy"])
    small = _pack_small(d, dpre_w, g["post_w"], g["a_log"], g["dt_bias"], g["f_bias"], g["gdn_norm_w"], g["fq_w"],
                        g["fk_w"], g["loss"])
    a_conv, a_meta, p_small = _all_gather([g["conv_wt"], dmeta, small], "gather_small_grads")
    p_conv = lax.dynamic_slice_in_dim(a_conv, dev * conv_w.shape[1], conv_w.shape[1], axis=2).transpose(0, 2, 1)
    p_meta = lax.dynamic_slice_in_dim(a_meta, dev * meta_tokens.shape[1], meta_tokens.shape[1], axis=2)

    r_out = _adamw(w_out[0], p_out, m_w_out[0], v_w_out[0], 64, "adamw_w_out")
    r_conv = _adamw(conv_w[0], p_conv, m_conv_w[0], v_conv_w[0], conv_w.shape[1], "adamw_conv_w")
    r_meta = _adamw(meta_tokens, p_meta, m_meta_tokens, v_meta_tokens, N_META, "adamw_meta")
    pk = lambda pre, post, a, dt, gw, fq, fk, fb: _pack_small(d, pre, post, a, dt, fb, gw, fq, fk, zero)
    r_small = _adamw(
        pk(pre_norm_w, post_norm_w, a_log, dt_bias, gdn_norm_w, fox_q_norm_w, fox_k_norm_w, fox_f_bias), p_small,
        pk(m_pre_norm_w, m_post_norm_w, m_a_log, m_dt_bias, m_gdn_norm_w, m_fox_q_norm_w, m_fox_k_norm_w, m_fox_f_bias),
        pk(v_pre_norm_w, v_post_norm_w, v_a_log, v_dt_bias, v_gdn_norm_w, v_fox_q_norm_w, v_fox_k_norm_w, v_fox_f_bias),
        8, "adamw_small")
    p_in = _chip_exchange_wait(flights, r_small[0], "chip_exchange_wait")
    r_in = _adamw_column_major(w_in, p_in, m_w_in, v_w_in, "adamw_w_in")

    sm = [_unpack_small(r, nh) for r in r_small]
    outs = []
    for i in range(4):
        s = sm[i]
        outs += [r_meta[i], s["pre"], r_in[i], r_conv[i][None], s["a_log"], s["dt_bias"], s["gdn_w"], s["fq_w"],
                 s["fk_w"], s["f_bias"], r_out[i][None], s["post"]]
    return (sm[0]["extra"], grad_x[None], *outs)
```

```python
import jax
import jax.numpy as jnp
from jax import lax
from jax.experimental import pallas as pl
from jax.experimental.pallas import tpu as pltpu

F32, BF16 = jnp.float32, jnp.bfloat16
HEAD_DIM = 128
N_META = 16
CONV_WIDTH = 4
CHUNK = 128
Q_BLOCK = 128
LANES = 128
EPS = 1e-6
PAD_ROWS = Q_BLOCK - N_META
N_DEV = 8
N_CHIP = 4
VMEM_LIMIT = 56 * 1024 * 1024
NEG = -1e30
NARROW = 2 * LANES
MM_TILE = 6 * LANES
DW_IN_PIECES = ((0, 2), (2, 4), (3, 4))

ADAM_LR, ADAM_B1, ADAM_B2, ADAM_EPS, ADAM_WD, ADAM_STEP = 0.001, 0.9, 0.999, 1e-08, 0.01, 10

_DN = {"nn": (((1,), (0,)), ((), ())), "nt": (((1,), (1,)), ((), ())), "tn": (((0,), (0,)), ((), ()))}
_DN3 = {"nn": (((2,), (1,)), ((0,), (0,))), "nt": (((2,), (2,)), ((0,), (0,))), "tn": (((1,), (1,)), ((0,), (0,)))}
_ANY = pl.BlockSpec(memory_space=pl.ANY)
_MESH = pl.DeviceIdType.MESH


def _cp(*sem):
    return pltpu.CompilerParams(dimension_semantics=sem, vmem_limit_bytes=VMEM_LIMIT)


def _dot(a, b, dims="nn", prec=None):
    return lax.dot_general(a, b, _DN[dims], precision=prec, preferred_element_type=F32)


def _bdot(a, b, dims="nn"):
    return _dot(a.astype(BF16), b.astype(BF16), dims)


def _hdot(a, b, dims="nn"):
    return _dot(a, b, dims, prec=lax.Precision.HIGHEST)


def _dot3(a, b, dims="nn"):
    return lax.dot_general(a, b, _DN3[dims], preferred_element_type=F32)


def _bdot3(a, b, dims="nn"):
    return _dot3(a.astype(BF16), b.astype(BF16), dims)


def _split(a):
    hi = a.astype(BF16)
    return hi, (a - hi.astype(F32)).astype(BF16)


def _iota(shape, dim):
    return lax.broadcasted_iota(jnp.int32, shape, dim)


def _sigmoid(z):
    return 1.0 / (1.0 + jnp.exp(-z))


def _softplus(z):
    e = jnp.exp(-jnp.abs(z))
    u = 1.0 + e
    l1p = jnp.where(u == 1.0, e, jnp.log(u) * (e / jnp.where(u == 1.0, 1.0, u - 1.0)))
    return jnp.maximum(z, 0.0) + l1p


def _silu_and_grad(z):
    s = _sigmoid(z)
    return z * s, s * (1.0 + z * (1.0 - s))


def _rms(x):
    return lax.rsqrt(jnp.mean(x * x, axis=-1, keepdims=True) + EPS)


def _h_tile(i, x_ref, meta_ref):
    first = jnp.concatenate([jnp.zeros((PAD_ROWS, x_ref.shape[1]), F32), meta_ref[...]], axis=0)
    return jnp.where(i == 0, first, x_ref[...])


def _x_rows(d):
    return pl.BlockSpec((Q_BLOCK, d), lambda i: (jnp.maximum(i - 1, 0), 0))


def _prenorm(x, meta, w):
    seq, d = x.shape
    lp = seq + Q_BLOCK

    def body(x_ref, m_ref, w_ref, o_ref):
        h = _h_tile(pl.program_id(0), x_ref, m_ref)
        o_ref[...] = (h * _rms(h) * w_ref[...]).astype(BF16)

    return pl.pallas_call(
        body, grid=(lp // Q_BLOCK,),
        in_specs=[_x_rows(d), pl.BlockSpec((N_META, d), lambda i: (0, 0)), pl.BlockSpec((1, d), lambda i: (0, 0))],
        out_specs=pl.BlockSpec((Q_BLOCK, d), lambda i: (i, 0)),
        out_shape=jax.ShapeDtypeStruct((lp, d), BF16), name="prenorm", compiler_params=_cp("parallel"))(x, meta, w)


def _tile(n, want):
    return max(t for t in range(LANES, want + 1, LANES) if n % t == 0)


class _Rider:
    def __init__(self, inputs, out_shapes, scratch, aliases, make):
        self.inputs, self.out_shapes, self.scratch, self.aliases, self.make = inputs, out_shapes, scratch, aliases, make


def _hosted_call(body, riders, n_in, n_out, n_scratch, *, in_specs, out_specs, out_shape, scratch_shapes=(), aliases=None,
                 **kw):
    riders = [r for r in riders if r is not None]
    r_in = [len(r.inputs) for r in riders]
    r_out = [len(r.out_shapes) for r in riders]
    r_scr = [len(r.scratch) for r in riders]
    al = dict(aliases or {})
    for k, r in enumerate(riders):
        al.update({n_in + sum(r_in[:k]) + i: n_out + sum(r_out[:k]) + o for i, o in r.aliases.items()})

    def full_body(*refs):
        ins, rest = refs[:n_in + sum(r_in)], refs[n_in + sum(r_in):]
        outs, scr = rest[:n_out + sum(r_out)], rest[n_out + sum(r_out):]
        hooks = [r.make(ins[n_in + sum(r_in[:k]):n_in + sum(r_in[:k + 1])], outs[n_out + sum(r_out[:k]):n_out + sum(r_out[:k + 1])],
                        scr[n_scratch + sum(r_scr[:k]):n_scratch + sum(r_scr[:k + 1])]) for k, r in enumerate(riders)]

        def start():
            for h in hooks:
                h[0]()

        def finish():
            for h in hooks:
                h[1]()

        body(start, finish, *ins[:n_in], *outs[:n_out], *scr[:n_scratch])

    call = pl.pallas_call(
        full_body, in_specs=list(in_specs) + [_ANY] * sum(r_in), out_specs=list(out_specs) + [_ANY] * sum(r_out),
        out_shape=list(out_shape) + [s for r in riders for s in r.out_shapes],
        scratch_shapes=list(scratch_shapes) + [s for r in riders for s in r.scratch], input_output_aliases=al, **kw)

    def run(*args):
        res = call(*args, *[t for r in riders for t in r.inputs])
        return res[:n_out], [res[n_out + sum(r_out[:k]):n_out + sum(r_out[:k + 1])] for k in range(len(riders))]

    return run


def _matmul(a, b, dims, tn, out_dtype, name, rider=None, a_cols=None, b_rows=None, acc=None):
    a_shape = a.shape if a_cols is None else (a.shape[0], a_cols[0])
    a_index = 0 if a_cols is None else a_cols[1]
    m = a_shape[1] if dims == "tn" else a_shape[0]
    n = b.shape[0] if dims == "nt" else b.shape[1]
    kdim = b.shape[1] if dims == "nt" else b.shape[0] if b_rows is None else b_rows[0]
    b_index = 0 if b_rows is None else b_rows[1]
    tn = _tile(n, tn)
    steps = n // tn
    b_spec = pl.BlockSpec((tn, kdim), lambda j: (j, 0)) if dims == "nt" else pl.BlockSpec((kdim, tn), lambda j: (b_index, j))
    o_spec = pl.BlockSpec((m, tn), lambda j: (0, j))

    def body(start, finish, a_ref, b_ref, *refs):
        pl.when(pl.program_id(0) == 0)(start)
        prod = _dot(a_ref[...], b_ref[...], dims)
        refs[-1][...] = (prod if acc is None else prod + refs[0][...]).astype(out_dtype)
        pl.when(pl.program_id(0) == steps - 1)(finish)

    (out,), got = _hosted_call(
        body, [rider], 2 + (acc is not None), 1, 0, grid=(steps,),
        in_specs=[pl.BlockSpec(a_shape, lambda j: (0, a_index)), b_spec] + [o_spec] * (acc is not None),
        out_specs=[o_spec], out_shape=[jax.ShapeDtypeStruct((m, n), out_dtype)], aliases={2: 0} if acc is not None else None,
        name=name, compiler_params=_cp("parallel" if rider is None else "arbitrary"))(a, b, *([acc] if acc is not None else []))
    return out if rider is None else (out, got[0])


def _chip_rider(sums):
    def make(ins, outs, scratch):
        local, remote = _chip_exchange_copies(ins[0], outs[0], *scratch)

        def start():
            for cp in [local] + remote:
                cp.start()

        def finish():
            local.wait()
            for cp in remote:
                cp.wait_send()
                cp.wait_recv()

        return start, finish

    return _Rider([sums], [jax.ShapeDtypeStruct(sums.shape, sums.dtype)],
                  [pltpu.SemaphoreType.DMA((N_CHIP - 1,)), pltpu.SemaphoreType.DMA((N_CHIP - 1,)), pltpu.SemaphoreType.DMA((1,))],
                  {}, make)


_HBM = pl.BlockSpec(memory_space=pltpu.HBM)
_SEM = pl.BlockSpec(memory_space=pltpu.SEMAPHORE)


def _chip_exchange_copies(sums_ref, land_ref, send_sems, recv_sems, local_sem):
    x, y, core = lax.axis_index("x"), lax.axis_index("y"), lax.axis_index("c")
    mine = 2 * x + y
    local = pltpu.make_async_copy(sums_ref.at[mine], land_ref.at[mine], local_sem.at[0])
    remote = []
    for k in range(1, N_CHIP):
        px = 1 - x if k & 2 else x
        py = 1 - y if k & 1 else y
        remote.append(pltpu.make_async_remote_copy(
            src_ref=sums_ref.at[2 * px + py], dst_ref=land_ref.at[mine], send_sem=send_sems.at[k - 1],
            recv_sem=recv_sems.at[k - 1], device_id=(px, py, core), device_id_type=_MESH))
    return local, remote


def _chip_exchange_start(sums, name):
    def body(s_ref, send_sems, recv_sems, local_sem, s_thru, land_ref, token):
        local, remote = _chip_exchange_copies(s_ref, land_ref, send_sems, recv_sems, local_sem)
        for cp in [local] + remote:
            cp.start()
        token[...] = jnp.zeros_like(token)

    *flight, token = pl.pallas_call(
        body, name=name,
        out_shape=(pltpu.SemaphoreType.DMA((N_CHIP - 1,)), pltpu.SemaphoreType.DMA((N_CHIP - 1,)), pltpu.SemaphoreType.DMA((1,)),
                   pltpu.HBM(sums.shape, sums.dtype), pltpu.HBM(sums.shape, sums.dtype), jax.ShapeDtypeStruct((8, LANES), F32)),
        in_specs=(_HBM,), out_specs=(_SEM, _SEM, _SEM, _HBM, _HBM, pl.BlockSpec(memory_space=pltpu.VMEM)),
        input_output_aliases={0: 3},
        compiler_params=pltpu.CompilerParams(has_side_effects=pltpu.SideEffectType.DATAFLOW_SIDE_EFFECTING))(
            pltpu.with_memory_space_constraint(sums, pltpu.HBM))
    return flight, token


def _chip_exchange_wait(flights, after, name):
    n = len(flights)

    def body(*refs):
        for i in range(n):
            s_ref, land_ref = refs[2 * i:2 * i + 2]
            local, remote = _chip_exchange_copies(s_ref, land_ref, *refs[2 * n + 3 * i:2 * n + 3 * i + 3])
            local.wait()
            for cp in remote:
                cp.wait_send()
                cp.wait_recv()

    buffers = [b for f in flights for b in f[3:]]
    res = pl.pallas_call(
        body, name=name, out_shape=tuple(pltpu.HBM(b.shape, b.dtype) for b in buffers),
        in_specs=(_HBM,) * (2 * n) + (_SEM,) * (3 * n) + (_ANY,), out_specs=(_HBM,) * (2 * n),
        input_output_aliases={i: i for i in range(2 * n)},
        compiler_params=pltpu.CompilerParams(has_side_effects=pltpu.SideEffectType.DATAFLOW_SIDE_EFFECTING))(
            *buffers, *[s for f in flights for s in f[:3]], after)
    return res[1::2]


def _dxn(dproj, wfull, riders, tk, name):
    m, k = dproj.shape
    n = wfull.shape[0]
    tk = _tile(k, tk)
    steps = k // tk

    def body(start, finish, a_ref, b_ref, o_ref):
        j = pl.program_id(0)

        @pl.when(j == 0)
        def _():
            start()
            o_ref[...] = jnp.zeros_like(o_ref)
        o_ref[...] += _dot(a_ref[...], b_ref[...], "nt")
        pl.when(j == steps - 1)(finish)

    (dxn,), got = _hosted_call(
        body, riders, 2, 1, 0, grid=(steps,),
        in_specs=[pl.BlockSpec((m, tk), lambda j: (0, j)), pl.BlockSpec((n, tk), lambda j: (0, j))],
        out_specs=[pl.BlockSpec((m, n), lambda j: (0, 0))], out_shape=[jax.ShapeDtypeStruct((m, n), F32)],
        name=name, compiler_params=_cp("arbitrary"))(dproj, wfull)
    return dxn, got


def _conv_taps(x, w):
    c = x * w[CONV_WIDTH - 1:CONV_WIDTH, :]
    for j in range(CONV_WIDTH - 1):
        c = c + pltpu.roll(x, CONV_WIDTH - 1 - j, 0) * w[j:j + 1, :]
    return c


def _gdn_prep(proj, conv_wt, nh):
    lp = proj.shape[0]
    scale = HEAD_DIM ** -0.5

    def body(x_ref, w_ref, o_ref):
        which = pl.program_id(0) // nh
        c = _conv_taps(x_ref[...], w_ref[...])
        s = c * _sigmoid(c)
        r = lax.rsqrt(jnp.sum(s * s, axis=-1, keepdims=True) + EPS)
        f = jnp.where(which == 0, r * scale, jnp.where(which == 1, r, 1.0))
        o_ref[...] = jnp.where(_iota(s.shape, 0) >= PAD_ROWS, s * f, 0.0)

    return pl.pallas_call(
        body, grid=(3 * nh,),
        in_specs=[pl.BlockSpec((lp, LANES), lambda s: (0, s)), pl.BlockSpec((CONV_WIDTH, LANES), lambda s: (0, s))],
        out_specs=pl.BlockSpec((lp, LANES), lambda s: (0, s)),
        out_shape=jax.ShapeDtypeStruct((lp, 3 * nh * HEAD_DIM), F32), name="gdn_prep",
        compiler_params=_cp("parallel"))(proj, conv_wt)


def _gdn_prep_bwd(proj, conv_wt, dq, dk, dv, dproj, nh):
    lp = proj.shape[0]
    scale = HEAD_DIM ** -0.5
    part = lambda p: pl.BlockSpec((lp, LANES), lambda s: (0, jnp.clip(s - p * nh, 0, nh - 1)))

    def body(x_ref, w_ref, dq_ref, dk_ref, dv_ref, _, dx_ref, dw_ref):
        which = pl.program_id(0) // nh
        x = x_ref[...]
        w = w_ref[...]
        c = _conv_taps(x, w)
        sg = _sigmoid(c)
        s = c * sg
        r = lax.rsqrt(jnp.sum(s * s, axis=-1, keepdims=True) + EPS)
        dy = jnp.where(which == 0, dq_ref[...], jnp.where(which == 1, dk_ref[...], dv_ref[...]))
        dy = jnp.where(_iota(s.shape, 0) >= PAD_ROWS, dy, 0.0)
        y0 = s * r
        dy0 = dy * jnp.where(which == 0, scale, 1.0)
        ds_n = r * (dy0 - y0 * jnp.sum(dy0 * y0, axis=-1, keepdims=True))
        ds = jnp.where(which == 2, dy, ds_n)
        dc = ds * (sg * (1.0 + c * (1.0 - sg)))
        dx = dc * w[CONV_WIDTH - 1:CONV_WIDTH, :]
        rows = [jnp.sum(dc * x, axis=0, keepdims=True)]
        for j in range(CONV_WIDTH - 2, -1, -1):
            sh = CONV_WIDTH - 1 - j
            dx = dx + pltpu.roll(dc, lp - sh, 0) * w[j:j + 1, :]
            rows.insert(0, jnp.sum(dc * pltpu.roll(x, sh, 0), axis=0, keepdims=True))
        dx_ref[...] = dx.astype(BF16)
        dw_ref[...] = jnp.concatenate(rows, axis=0)

    strip = pl.BlockSpec((lp, LANES), lambda s: (0, s))
    taps = pl.BlockSpec((CONV_WIDTH, LANES), lambda s: (0, s))
    return pl.pallas_call(
        body, grid=(3 * nh,), in_specs=[strip, taps, part(0), part(1), part(2), _ANY], out_specs=[strip, taps],
        out_shape=[jax.ShapeDtypeStruct(dproj.shape, BF16), jax.ShapeDtypeStruct((CONV_WIDTH, 3 * nh * HEAD_DIM), F32)],
        input_output_aliases={5: 0}, name="gdn_prep_bwd", compiler_params=_cp("parallel"))(proj, conv_wt, dq, dk, dv, dproj)


def _gates(proj, bias_row, nega_row, nh):
    lp = proj.shape[0]
    nc = lp // CHUNK

    def body(p_ref, b_ref, a_ref, g_ref, gt3_ref, gtf_ref):
        lane = _iota((CHUNK, LANES), 1)
        tri = (_iota((CHUNK, CHUNK), 0) >= _iota((CHUNK, CHUNK), 1)).astype(F32)

        def step(n, carry):
            r0 = pl.multiple_of(n * CHUNK, CHUNK)
            z = p_ref[pl.ds(r0, CHUNK), :] + b_ref[...]
            base = jnp.where(lane < nh, _sigmoid(z),
                             jnp.where(lane < 2 * nh, a_ref[...] * _softplus(z),
                                       jnp.where(lane < 3 * nh, -_softplus(-z), 0.0)))
            base = jnp.where(r0 + _iota((CHUNK, LANES), 0) >= PAD_ROWS, base, 0.0)
            cs = _hdot(tri, base)
            run = jnp.where((lane >= 2 * nh) & (lane < 3 * nh), cs + carry, cs)
            sh = pltpu.roll(run, 2 * nh, 1)
            out = base + jnp.where((lane >= 3 * nh) & (lane < 5 * nh), sh, 0.0)
            g_ref[pl.ds(r0, CHUNK), :] = out
            gt3_ref[n] = out.T
            return carry + cs[CHUNK - 1:CHUNK, :]

        lax.fori_loop(0, nc, step, jnp.zeros((1, LANES), F32))
        gtf_ref[...] = g_ref[...].T

    vec = pl.BlockSpec((1, LANES), lambda i: (0, 0))
    return pl.pallas_call(
        body, grid=(1,), in_specs=[pl.BlockSpec((lp, LANES), lambda i: (0, 8 * nh)), vec, vec],
        out_specs=[pl.BlockSpec((lp, LANES), lambda i: (0, 0)), pl.BlockSpec((nc, LANES, CHUNK), lambda i: (0, 0, 0)),
                   pl.BlockSpec((LANES, lp), lambda i: (0, 0))],
        out_shape=[jax.ShapeDtypeStruct((lp, LANES), F32), jax.ShapeDtypeStruct((nc, LANES, CHUNK), F32),
                   jax.ShapeDtypeStruct((LANES, lp), F32)],
        name="gates", compiler_params=_cp("arbitrary"))(proj, bias_row, nega_row)


def _gates_bwd(proj, bias_row, nega_row, gates, dgate_gdn, dc_t, dproj, nh):
    lp = proj.shape[0]
    nc = lp // CHUNK

    def body(p_ref, b_ref, a_ref, g_ref, dg_ref, dc_ref, _, dz_ref, sm_ref, dct_scr):
        lane = _iota((CHUNK, LANES), 1)
        triu = (_iota((CHUNK, CHUNK), 0) <= _iota((CHUNK, CHUNK), 1)).astype(F32)
        dct_scr[...] = dc_ref[...].T
        sm_ref[...] = jnp.zeros_like(sm_ref)
        dz_ref[:, LANES:] = jnp.zeros((lp, NARROW - LANES), BF16)

        def step(i, carry):
            n = nc - 1 - i
            r0 = pl.multiple_of(n * CHUNK, CHUNK)
            z = p_ref[pl.ds(r0, CHUNK), :] + b_ref[...]
            gt = g_ref[pl.ds(r0, CHUNK), :]
            dgd = dg_ref[pl.ds(r0, CHUNK), :]
            dch = dct_scr[pl.ds(r0, CHUNK), :]
            rc = _hdot(triu, dch) + carry
            sg = _sigmoid(z)
            dz = jnp.where(lane < nh, dgd * sg * (1.0 - sg),
                           jnp.where(lane < 2 * nh, dgd * a_ref[...] * sg,
                                     jnp.where(lane < 3 * nh, rc * (1.0 - sg), 0.0)))
            dz = jnp.where(r0 + _iota((CHUNK, LANES), 0) >= PAD_ROWS, dz, 0.0)
            dz_ref[pl.ds(r0, CHUNK), 0:LANES] = dz.astype(BF16)
            sm_ref[0:1, :] += jnp.sum(dz, axis=0, keepdims=True)
            sm_ref[1:2, :] += jnp.sum(jnp.where((lane >= nh) & (lane < 2 * nh), dgd * gt, 0.0), axis=0, keepdims=True)
            return carry + jnp.sum(dch, axis=0, keepdims=True)

        lax.fori_loop(0, nc, step, jnp.zeros((1, LANES), F32))

    vec = pl.BlockSpec((1, LANES), lambda i: (0, 0))
    full = pl.BlockSpec((lp, LANES), lambda i: (0, 0))
    last = pl.BlockSpec((lp, LANES), lambda i: (0, 8 * nh))
    tail = pl.BlockSpec((lp, NARROW), lambda i: (0, 8 * nh * LANES // NARROW))
    return pl.pallas_call(
        body, grid=(1,), in_specs=[last, vec, vec, full, full, pl.BlockSpec((LANES, lp), lambda i: (0, 0)), _ANY],
        out_specs=[tail, pl.BlockSpec((8, LANES), lambda i: (0, 0))],
        out_shape=[jax.ShapeDtypeStruct(dproj.shape, BF16), jax.ShapeDtypeStruct((8, LANES), F32)],
        scratch_shapes=[pltpu.VMEM((lp, LANES), F32)], input_output_aliases={6: 0},
        name="gates_bwd", compiler_params=_cp("arbitrary"))(proj, bias_row, nega_row, gates, dgate_gdn, dc_t, dproj)


def _tri_inv(a):
    t = jnp.where(_iota(a.shape, 1) == _iota(a.shape, 2), 1.0, 0.0) - a
    p = a
    for _ in range(CHUNK.bit_length() - 2):
        ph, pw = _split(p)
        p = _dot3(ph, ph) + (_dot3(ph, pw) + _dot3(pw, ph))
        ph, pw = _split(p)
        th, tw = _split(t)
        t = t + (_dot3(th, ph) + (_dot3(th, pw) + _dot3(tw, ph)))
    return t


def _gdn_chunk(q, k, v, beta, gc, gr, t=None):
    ii, jj = _iota((1, CHUNK, CHUNK), 1), _iota((1, CHUNK, CHUNK), 2)
    causal, strict = ii >= jj, ii > jj
    dm = jnp.where(causal, jnp.exp(jnp.where(causal, gc - gr, 0.0)), 0.0)
    kk = _bdot3(k, k, "nt")
    a = jnp.where(strict, beta * kk * dm, 0.0)
    if t is None:
        t = _tri_inv(a)
    eg = jnp.exp(gc)
    glast = gc[:, CHUNK - 1:CHUNK, :]
    ekd = jnp.exp(glast - gc)
    bv = beta * v
    bk = (beta * eg) * k
    ub = _bdot3(t, jnp.concatenate([bv, bk], axis=2))
    qk = _bdot3(q, k, "nt")
    return dict(causal=causal, strict=strict, dm=dm, kk=kk, a=a, t=t, eg=eg, ekd=ekd, bv=bv, bk=bk,
                u=ub[:, :, :HEAD_DIM], w=ub[:, :, HEAD_DIM:], qk=qk, aqk=jnp.where(causal, qk * dm, 0.0),
                q_dec=q * eg, k_dec=k * ekd, decay=jnp.exp(glast))


def _heads(ref, nh):
    return jnp.stack([ref[:, h * HEAD_DIM:(h + 1) * HEAD_DIM] for h in range(nh)], axis=0)


def _gdn_chunk_inputs(q_ref, k_ref, v_ref, g, gt, nh):
    col = lambda o: jnp.stack([g[:, o + h:o + h + 1] for h in range(nh)], axis=0)
    gr = jnp.stack([gt[3 * nh + h:3 * nh + h + 1, :] for h in range(nh)], axis=0)
    return _heads(q_ref, nh), _heads(k_ref, nh), _heads(v_ref, nh), col(0), col(3 * nh), gr


def _gdn_fwd(qkv, gates, gt3, nh, rider=None):
    lp = qkv.shape[0]
    nc = lp // CHUNK
    w = nh * HEAD_DIM

    def body(start, finish, q_ref, k_ref, v_ref, g_ref, gt_ref, o_ref, sall_ref, tall_ref, s_scr):
        @pl.when(pl.program_id(0) == 0)
        def _():
            start()
            s_scr[...] = jnp.zeros_like(s_scr)
        c = _gdn_chunk(*_gdn_chunk_inputs(q_ref, k_ref, v_ref, g_ref[...], gt_ref[0], nh))
        s = s_scr[...]
        sall_ref[0] = s
        tall_ref[0] = c["t"]
        v_new = c["u"] - _bdot3(c["w"], s)
        o = _bdot3(c["q_dec"], s) + _bdot3(c["aqk"], v_new)
        s_scr[...] = s * c["decay"] + _bdot3(c["k_dec"], v_new, "tn")
        for h in range(nh):
            o_ref[:, h * HEAD_DIM:(h + 1) * HEAD_DIM] = o[h]
        pl.when(pl.program_id(0) == nc - 1)(finish)

    outs, got = _hosted_call(
        body, [rider], 5, 3, 1, grid=(nc,),
        in_specs=[pl.BlockSpec((CHUNK, w), lambda n: (n, 0)), pl.BlockSpec((CHUNK, w), lambda n: (n, 1)),
                  pl.BlockSpec((CHUNK, w), lambda n: (n, 2)), pl.BlockSpec((CHUNK, LANES), lambda n: (n, 0)),
                  pl.BlockSpec((1, LANES, CHUNK), lambda n: (n, 0, 0))],
        out_specs=[pl.BlockSpec((CHUNK, w), lambda n: (n, 0)),
                   pl.BlockSpec((1, nh, HEAD_DIM, HEAD_DIM), lambda n: (n, 0, 0, 0)),
                   pl.BlockSpec((1, nh, CHUNK, CHUNK), lambda n: (n, 0, 0, 0))],
        out_shape=[jax.ShapeDtypeStruct((lp, w), F32), jax.ShapeDtypeStruct((nc, nh, HEAD_DIM, HEAD_DIM), F32),
                   jax.ShapeDtypeStruct((nc, nh, CHUNK, CHUNK), F32)],
        scratch_shapes=[pltpu.VMEM((nh, HEAD_DIM, HEAD_DIM), F32)],
        name="gdn_fwd", compiler_params=_cp("arbitrary"))(qkv, qkv, qkv, gates, gt3)
    return outs, (got[0] if got else None)


def _gdn_bwd(qkv, gates, gt3, s_all, t_all, do, nh, rider=None):
    lp = qkv.shape[0]
    nc = lp // CHUNK
    w = nh * HEAD_DIM
    rev = lambda n: nc - 1 - n

    def body(start, finish, q_ref, k_ref, v_ref, g_ref, gt_ref, s_ref, t_ref, do_ref, dq_ref, dk_ref, dv_ref, dg_ref, ds_scr):
        @pl.when(pl.program_id(0) == 0)
        def _():
            start()
            ds_scr[...] = jnp.zeros_like(ds_scr)
        q, k, v, beta, gc, gr = _gdn_chunk_inputs(q_ref, k_ref, v_ref, g_ref[...], gt_ref[0], nh)
        c = _gdn_chunk(q, k, v, beta, gc, gr, t_ref[0])
        s = s_ref[0]
        dsn = ds_scr[...]
        dout = _heads(do_ref, nh)
        v_new = c["u"] - _bdot3(c["w"], s)
        dq_dec = _bdot3(dout, s, "nt")
        daqk = jnp.where(c["causal"], _bdot3(dout, v_new, "nt"), 0.0)
        dv_new = _bdot3(c["aqk"], dout, "tn") + _bdot3(c["k_dec"], dsn)
        dk_dec = _bdot3(v_new, dsn, "nt")
        ddecay = jnp.sum(jnp.sum(dsn * s, axis=2, keepdims=True), axis=1, keepdims=True)
        dw = -_bdot3(dv_new, s, "nt")
        ds_scr[...] = _bdot3(c["q_dec"], dout, "tn") + c["decay"] * dsn - _bdot3(c["w"], dv_new, "tn")
        duw = jnp.concatenate([dv_new, dw], axis=2)
        dt = _bdot3(duw, jnp.concatenate([c["bv"], c["bk"]], axis=2), "nt")
        dbvk = _bdot3(c["t"], duw, "tn")
        dbv, dbk = dbvk[:, :, :HEAD_DIM], dbvk[:, :, HEAD_DIM:]
        da = jnp.where(c["strict"], -_bdot3(_bdot3(c["t"], dt, "tn"), c["t"], "nt"), 0.0)
        dkk = da * beta * c["dm"]
        dqk = daqk * c["dm"]
        e = da * c["a"] + daqk * c["aqk"]
        dq = dq_dec * c["eg"] + _bdot3(dqk, k)
        dk = (dk_dec * c["ekd"] + _bdot3(dkk, k) + _bdot3(dkk, k, "tn") + _bdot3(dqk, q, "tn")
              + (beta * c["eg"]) * dbk)
        dv = beta * dbv
        rs = lambda x: jnp.sum(x, axis=2, keepdims=True)
        dbeta = rs(dbv * v) + c["eg"] * rs(dbk * k) + rs(da * c["kk"] * c["dm"])
        kd_term = rs(dk_dec * c["k_dec"])
        eh, ew = _split(e)
        ones = jnp.ones((nh, CHUNK, LANES), BF16)
        col_sums = (_dot3(eh, ones, "tn") + _dot3(ew, ones, "tn"))[:, :, 0:1]
        dg_cum = rs(dq_dec * c["q_dec"]) - kd_term + rs(dbk * c["bk"]) + rs(e) - col_sums
        last = jnp.sum(kd_term, axis=1, keepdims=True) + ddecay * c["decay"]
        dg_cum = dg_cum + jnp.where(_iota((1, CHUNK, 1), 1) == CHUNK - 1, last, 0.0)
        lane = _iota((CHUNK, LANES), 1)
        acc = jnp.zeros((CHUNK, LANES), F32)
        for h in range(nh):
            sl = slice(h * HEAD_DIM, (h + 1) * HEAD_DIM)
            dq_ref[:, sl] = dq[h]
            dk_ref[:, sl] = dk[h]
            dv_ref[:, sl] = dv[h]
            acc = acc + jnp.where(lane == h, dbeta[h], 0.0) + jnp.where(lane == nh + h, dg_cum[h], 0.0)
        triu = (_iota((CHUNK, CHUNK), 0) <= _iota((CHUNK, CHUNK), 1)).astype(F32)
        dg_ref[...] = jnp.where(lane < nh, acc, _hdot(triu, acc))
        pl.when(pl.program_id(0) == nc - 1)(finish)

    outs, got = _hosted_call(
        body, [rider], 8, 4, 1, grid=(nc,),
        in_specs=[pl.BlockSpec((CHUNK, w), lambda n: (rev(n), 0)), pl.BlockSpec((CHUNK, w), lambda n: (rev(n), 1)),
                  pl.BlockSpec((CHUNK, w), lambda n: (rev(n), 2)), pl.BlockSpec((CHUNK, LANES), lambda n: (rev(n), 0)),
                  pl.BlockSpec((1, LANES, CHUNK), lambda n: (rev(n), 0, 0)),
                  pl.BlockSpec((1, nh, HEAD_DIM, HEAD_DIM), lambda n: (rev(n), 0, 0, 0)),
                  pl.BlockSpec((1, nh, CHUNK, CHUNK), lambda n: (rev(n), 0, 0, 0)),
                  pl.BlockSpec((CHUNK, w), lambda n: (rev(n), 0))],
        out_specs=[pl.BlockSpec((CHUNK, w), lambda n: (rev(n), 0))] * 3 + [pl.BlockSpec((CHUNK, LANES), lambda n: (rev(n), 0))],
        out_shape=[jax.ShapeDtypeStruct((lp, w), F32)] * 3 + [jax.ShapeDtypeStruct((lp, LANES), F32)],
        scratch_shapes=[pltpu.VMEM((nh, HEAD_DIM, HEAD_DIM), F32)],
        name="gdn_bwd", compiler_params=_cp("arbitrary"))(qkv, qkv, qkv, gates, gt3, s_all, t_all, do)
    return outs, (got[0] if got else None)


def _merge_gdn(o_gdn, proj, norm_w, nh):
    lp = o_gdn.shape[0]

    def body(o_ref, z_ref, w_ref, m_ref):
        o = o_ref[...]
        z = z_ref[...]
        m_ref[...] = (o * _rms(o) * w_ref[...] * (z * _sigmoid(z))).astype(BF16)

    return pl.pallas_call(
        body, grid=(nh,),
        in_specs=[pl.BlockSpec((lp, LANES), lambda s: (0, s)), pl.BlockSpec((lp, LANES), lambda s: (0, 3 * nh + s)),
                  pl.BlockSpec((1, LANES), lambda s: (0, 0))],
        out_specs=pl.BlockSpec((lp, LANES), lambda s: (0, s)),
        out_shape=jax.ShapeDtypeStruct((lp, 2 * nh * HEAD_DIM), BF16), name="merge_gdn",
        compiler_params=_cp("parallel"))(o_gdn, proj, norm_w)


def _merge_gdn_bwd(o_gdn, proj, norm_w, dmerged, nh):
    lp = o_gdn.shape[0]

    def body(o_ref, z_ref, w_ref, dm_ref, do_ref, dz_ref, dw_ref):
        o = o_ref[...]
        r = _rms(o)
        xh = o * r
        silu, dsilu = _silu_and_grad(z_ref[...])
        dm = dm_ref[...]
        dn = dm * silu
        dz_ref[...] = (dm * (xh * w_ref[...]) * dsilu).astype(BF16)
        dnw = dn * w_ref[...]
        do_ref[...] = r * (dnw - xh * jnp.mean(dnw * xh, axis=-1, keepdims=True))

        @pl.when(pl.program_id(0) == 0)
        def _():
            dw_ref[...] = jnp.zeros_like(dw_ref)
        dw_ref[...] += jnp.sum(dn * xh, axis=0, keepdims=True)

    w = nh * HEAD_DIM
    return pl.pallas_call(
        body, grid=(nh,),
        in_specs=[pl.BlockSpec((lp, LANES), lambda s: (0, s)), pl.BlockSpec((lp, LANES), lambda s: (0, 3 * nh + s)),
                  pl.BlockSpec((1, LANES), lambda s: (0, 0)), pl.BlockSpec((lp, LANES), lambda s: (0, s))],
        out_specs=[pl.BlockSpec((lp, LANES), lambda s: (0, s)), pl.BlockSpec((lp, LANES), lambda s: (0, 3 * nh + s)),
                   pl.BlockSpec((1, LANES), lambda s: (0, 0))],
        out_shape=[jax.ShapeDtypeStruct((lp, w), F32), jax.ShapeDtypeStruct((lp, 8 * w + NARROW), BF16),
                   jax.ShapeDtypeStruct((1, LANES), F32)],
        name="merge_gdn_bwd", compiler_params=_cp("arbitrary"))(o_gdn, proj, norm_w, dmerged)


def _fox_prep(proj, qk_w, nh):
    lp = proj.shape[0]

    def body(x_ref, w_ref, o_ref):
        x = x_ref[...]
        o_ref[...] = x * _rms(x) * w_ref[0]

    return pl.pallas_call(
        body, grid=(2 * nh,),
        in_specs=[pl.BlockSpec((lp, LANES), lambda s: (0, 4 * nh + s)), pl.BlockSpec((1, 1, LANES), lambda s: (s // nh, 0, 0))],
        out_specs=pl.BlockSpec((lp, LANES), lambda s: (0, s)),
        out_shape=jax.ShapeDtypeStruct((lp, 2 * nh * HEAD_DIM), F32), name="fox_prep",
        compiler_params=_cp("parallel"))(proj, qk_w)


def _fox_prep_bwd(proj, qk_w, dq, dk, dproj, nh):
    lp = proj.shape[0]
    part = lambda p: pl.BlockSpec((lp, LANES), lambda s: (0, jnp.clip(s - p * nh, 0, nh - 1)))

    def body(x_ref, w_ref, dq_ref, dk_ref, _, dx_ref, dw_ref):
        x = x_ref[...]
        r = _rms(x)
        xh = x * r
        dy = jnp.where(pl.program_id(0) < nh, dq_ref[...], dk_ref[...])
        dyw = dy * w_ref[0]
        dx_ref[...] = (r * (dyw - xh * jnp.mean(dyw * xh, axis=-1, keepdims=True))).astype(BF16)

        @pl.when(pl.program_id(0) % nh == 0)
        def _():
            dw_ref[...] = jnp.zeros_like(dw_ref)
        dw_ref[0] += jnp.sum(dy * xh, axis=0, keepdims=True)

    strip = pl.BlockSpec((lp, LANES), lambda s: (0, 4 * nh + s))
    wsp = pl.BlockSpec((1, 1, LANES), lambda s: (s // nh, 0, 0))
    return pl.pallas_call(
        body, grid=(2 * nh,), in_specs=[strip, wsp, part(0), part(1), _ANY], out_specs=[strip, wsp],
        out_shape=[jax.ShapeDtypeStruct(dproj.shape, BF16), jax.ShapeDtypeStruct((2, 1, LANES), F32)],
        input_output_aliases={4: 0}, name="fox_prep_bwd", compiler_params=_cp("arbitrary"))(proj, qk_w, dq, dk, dproj)


def _fox_probs(q, k, gates, crow, h, i, nh, lse=None):
    kl = k.shape[0]
    lane = _iota((Q_BLOCK, LANES), 1)
    ct = jnp.sum(jnp.where(lane == 4 * nh + h, gates, 0.0), axis=1, keepdims=True)
    tq, kq = _iota((Q_BLOCK, Q_BLOCK), 0), _iota((Q_BLOCK, Q_BLOCK), 1)
    qs = q * (HEAD_DIM ** -0.5)
    if i == 0:
        s = _bdot(qs, k, "nt") + (ct - crow)
        s = jnp.where((kq <= tq) & ((kq >= PAD_ROWS) | (tq < PAD_ROWS)), s, NEG)
    else:
        crow = jnp.where(_iota((1, kl), 1) < PAD_ROWS, -NEG, crow)
        s = _bdot(qs, k, "nt") + (ct - crow)
        s = jnp.concatenate([s[:, :kl - Q_BLOCK], jnp.where(kq <= tq, s[:, kl - Q_BLOCK:], NEG)], axis=1)
    if lse is not None:
        return jnp.exp(s - lse)
    m = jnp.max(s, axis=1, keepdims=True)
    p = jnp.exp(s - m)
    tot = jnp.sum(p, axis=1, keepdims=True)
    return p / tot, m + jnp.log(tot)


FOX_HEADS_PER_STEP = 2


def _fox_specs(lp, nh):
    hw = FOX_HEADS_PER_STEP * LANES
    return [pl.BlockSpec((Q_BLOCK, hw), lambda g, i: (i, g)),
            pl.BlockSpec((lp, hw), lambda g, i: (0, nh // FOX_HEADS_PER_STEP + g)),
            pl.BlockSpec((lp, hw), lambda g, i: (0, 6 * nh // FOX_HEADS_PER_STEP + g)),
            pl.BlockSpec((Q_BLOCK, LANES), lambda g, i: (i, 0)),
            pl.BlockSpec((LANES, lp), lambda g, i: (0, 0))]


def _fox_fwd(qkn, proj, gates, gtf, nh):
    lp = qkn.shape[0]

    def body(q_ref, k_ref, v_ref, g_ref, gt_ref, o_ref, lse_ref):
        g, i = pl.program_id(0), pl.program_id(1)
        for j in range(lp // Q_BLOCK):
            @pl.when(i == j)
            def _(j=j):
                kl = (j + 1) * Q_BLOCK
                for hh in range(FOX_HEADS_PER_STEP):
                    h = FOX_HEADS_PER_STEP * g + hh
                    sl = slice(hh * LANES, (hh + 1) * LANES)
                    p, lse = _fox_probs(q_ref[:, sl], k_ref[0:kl, sl], g_ref[...], gt_ref[pl.ds(4 * nh + h, 1), :][:, 0:kl],
                                        h, j, nh)
                    o_ref[:, sl] = _bdot(p, v_ref[0:kl, sl])
                    lse_ref[:, sl] = jnp.broadcast_to(lse, (Q_BLOCK, LANES))

    blk = pl.BlockSpec((Q_BLOCK, FOX_HEADS_PER_STEP * LANES), lambda g, i: (i, g))
    return pl.pallas_call(
        body, grid=(nh // FOX_HEADS_PER_STEP, lp // Q_BLOCK), in_specs=_fox_specs(lp, nh), out_specs=[blk, blk],
        out_shape=[jax.ShapeDtypeStruct((lp, nh * HEAD_DIM), F32)] * 2, name="fox_fwd",
        compiler_params=_cp("parallel", "parallel"))(qkn, qkn, proj, gates, gtf)


def _fox_bwd(qkn, proj, gates, gtf, lse, do, dproj, nh):
    lp = qkn.shape[0]
    nq = lp // Q_BLOCK
    w = nh * HEAD_DIM
    scale = HEAD_DIM ** -0.5

    def body(q_ref, k_ref, v_ref, g_ref, gt_ref, lse_ref, do_ref, _, dq_ref, dk_ref, dc_ref, dv_ref, dv_scr):
        g, i = pl.program_id(0), pl.program_id(1)

        @pl.when(i == 0)
        def _():
            dk_ref[...] = jnp.zeros_like(dk_ref)
            dv_scr[...] = jnp.zeros_like(dv_scr)
            dc_ref[...] = jnp.zeros_like(dc_ref)
        for j in range(nq):
            @pl.when(i == j)
            def _(j=j):
                kl = (j + 1) * Q_BLOCK
                for hh in range(FOX_HEADS_PER_STEP):
                    h = FOX_HEADS_PER_STEP * g + hh
                    sl = slice(hh * LANES, (hh + 1) * LANES)
                    q, k = q_ref[:, sl], k_ref[0:kl, sl]
                    p = _fox_probs(q, k, g_ref[...], gt_ref[pl.ds(4 * nh + h, 1), :][:, 0:kl], h, j, nh,
                                   lse_ref[:, sl][:, 0:1])
                    dout = do_ref[:, sl]
                    dp = _bdot(dout, v_ref[0:kl, sl], "nt")
                    ds = p * (dp - jnp.sum(p * dp, axis=1, keepdims=True))
                    dq_ref[:, sl] = _bdot(ds, k) * scale
                    dk_ref[0:kl, sl] += _bdot(ds, q * scale, "tn")
                    dv_scr[0:kl, sl] += _bdot(p, dout, "tn")
                    dc_ref[hh, :, 0:kl] -= jnp.sum(ds, axis=0, keepdims=True)

        @pl.when(i == nq - 1)
        def _():
            dv_ref[...] = dv_scr[...].astype(BF16)

    hw = FOX_HEADS_PER_STEP * LANES
    blk = pl.BlockSpec((Q_BLOCK, hw), lambda g, i: (i, g))
    col = pl.BlockSpec((lp, hw), lambda g, i: (0, g))
    return pl.pallas_call(
        body, grid=(nh // FOX_HEADS_PER_STEP, nq), in_specs=_fox_specs(lp, nh) + [blk, blk, _ANY],
        out_specs=[blk, col, pl.BlockSpec((FOX_HEADS_PER_STEP, 1, lp), lambda g, i: (g, 0, 0)),
                   pl.BlockSpec((lp, hw), lambda g, i: (0, 6 * nh // FOX_HEADS_PER_STEP + g))],
        out_shape=[jax.ShapeDtypeStruct((lp, w), F32)] * 2 + [jax.ShapeDtypeStruct((nh, 1, lp), F32),
                                                             jax.ShapeDtypeStruct(dproj.shape, BF16)],
        scratch_shapes=[pltpu.VMEM((lp, hw), F32)], input_output_aliases={7: 3},
        name="fox_bwd", compiler_params=_cp("parallel", "arbitrary"))(qkn, qkn, proj, gates, gtf, lse, do, dproj)


def _merge_fox(o_fox, proj, merged, nh):
    lp = o_fox.shape[0]

    def body(o_ref, z_ref, _, m_ref):
        z = z_ref[...]
        m_ref[...] = (o_ref[...] * (z * _sigmoid(z))).astype(BF16)

    return pl.pallas_call(
        body, grid=(nh,),
        in_specs=[pl.BlockSpec((lp, LANES), lambda s: (0, s)), pl.BlockSpec((lp, LANES), lambda s: (0, 7 * nh + s)), _ANY],
        out_specs=pl.BlockSpec((lp, LANES), lambda s: (0, nh + s)),
        out_shape=jax.ShapeDtypeStruct(merged.shape, BF16), input_output_aliases={2: 0}, name="merge_fox",
        compiler_params=_cp("parallel"))(o_fox, proj, merged)


def _merge_fox_bwd(o_fox, proj, dmerged, dproj, nh):
    lp = o_fox.shape[0]

    def body(o_ref, z_ref, dm_ref, _, do_ref, dz_ref):
        silu, dsilu = _silu_and_grad(z_ref[...])
        dm = dm_ref[...]
        do_ref[...] = dm * silu
        dz_ref[...] = (dm * o_ref[...] * dsilu).astype(BF16)

    w = nh * HEAD_DIM
    return pl.pallas_call(
        body, grid=(nh,),
        in_specs=[pl.BlockSpec((lp, LANES), lambda s: (0, s)), pl.BlockSpec((lp, LANES), lambda s: (0, 7 * nh + s)),
                  pl.BlockSpec((lp, LANES), lambda s: (0, nh + s)), _ANY],
        out_specs=[pl.BlockSpec((lp, LANES), lambda s: (0, s)), pl.BlockSpec((lp, LANES), lambda s: (0, 7 * nh + s))],
        out_shape=[jax.ShapeDtypeStruct((lp, w), F32), jax.ShapeDtypeStruct(dproj.shape, BF16)],
        input_output_aliases={3: 1}, name="merge_fox_bwd", compiler_params=_cp("parallel"))(o_fox, proj, dmerged, dproj)


def _post(out, x, target, post_w):
    lp, d = out.shape

    def body(o_ref, x_ref, t_ref, w_ref, dy_ref, do_ref, loss_ref, dw_ref):
        i = pl.program_id(0)

        @pl.when(i == 0)
        def _():
            loss_ref[...] = jnp.zeros_like(loss_ref)
            dw_ref[...] = jnp.zeros_like(dw_ref)
        o = o_ref[...]
        r = _rms(o)
        nrm = o * r
        err = jnp.where(i > 0, x_ref[...] + nrm * w_ref[...] - t_ref[...], 0.0)
        loss_ref[0:1, :] += 0.5 * jnp.sum(jnp.sum(err * err, axis=1, keepdims=True), axis=0, keepdims=True) / d
        dy = err / d
        dy_ref[...] = dy
        dw_ref[...] += jnp.sum(dy * nrm, axis=0, keepdims=True)
        dyw = dy * w_ref[...]
        do_ref[...] = (r * (dyw - nrm * jnp.mean(dyw * nrm, axis=-1, keepdims=True))).astype(BF16)

    row = pl.BlockSpec((Q_BLOCK, d), lambda i: (i, 0))
    vec = pl.BlockSpec((1, d), lambda i: (0, 0))
    return pl.pallas_call(
        body, grid=(lp // Q_BLOCK,), in_specs=[row, _x_rows(d), _x_rows(d), vec],
        out_specs=[_x_rows(d), row, pl.BlockSpec((8, LANES), lambda i: (0, 0)), vec],
        out_shape=[jax.ShapeDtypeStruct(x.shape, F32), jax.ShapeDtypeStruct((lp, d), BF16),
                   jax.ShapeDtypeStruct((8, LANES), F32), jax.ShapeDtypeStruct((1, d), F32)],
        name="post", compiler_params=_cp("arbitrary"))(out, x, target, post_w)


def _prenorm_bwd(dxn, x, meta, w, dy, rider=None):
    seq, d = x.shape
    lp = seq + Q_BLOCK

    def body(start, finish, dx_ref, x_ref, m_ref, w_ref, dy_ref, gx_ref, gm_ref, dw_ref):
        i = pl.program_id(0)
        pl.when(i == 0)(start)
        h = _h_tile(i, x_ref, m_ref)
        r = _rms(h)
        xh = h * r
        dxn_ = dx_ref[...]
        dxw = dxn_ * w_ref[...]
        dh = jnp.where(i > 0, dy_ref[...], 0.0) + r * (dxw - xh * jnp.mean(dxw * xh, axis=-1, keepdims=True))
        gx_ref[...] = dh

        @pl.when(i == 0)
        def _():
            dw_ref[...] = jnp.zeros_like(dw_ref)
            gm_ref[...] = dh[PAD_ROWS:, :]
        dw_ref[...] += jnp.sum(dxn_ * xh, axis=0, keepdims=True)
        pl.when(i == lp // Q_BLOCK - 1)(finish)

    vec = pl.BlockSpec((1, d), lambda i: (0, 0))
    met = pl.BlockSpec((N_META, d), lambda i: (0, 0))
    outs, got = _hosted_call(
        body, [rider], 5, 3, 0, grid=(lp // Q_BLOCK,),
        in_specs=[pl.BlockSpec((Q_BLOCK, d), lambda i: (i, 0)), _x_rows(d), met, vec, _x_rows(d)],
        out_specs=[_x_rows(d), met, vec],
        out_shape=[jax.ShapeDtypeStruct((seq, d), F32), jax.ShapeDtypeStruct((N_META, d), F32),
                   jax.ShapeDtypeStruct((1, d), F32)],
        name="prenorm_bwd", compiler_params=_cp("arbitrary"))(dxn, x, meta, w, dy)
    return outs, (got[0] if got else None)


def _layer_grads(x, target, meta, pre_w, wfull, conv_wt, a_log, dt_bias, gdn_norm_w, fq_w, fk_w, f_bias, w_out, post_w,
                 late_weights=None, w_out_grads=None):
    nh = a_log.shape[1]
    zpad = jnp.zeros((1, LANES - 3 * nh), F32)
    bias_row = jnp.concatenate([jnp.zeros((1, nh), F32), dt_bias, f_bias, zpad], axis=1)
    nega_row = jnp.concatenate([jnp.zeros((1, nh), F32), -jnp.exp(a_log), jnp.zeros((1, nh), F32), zpad], axis=1)
    qk_w = jnp.stack([fq_w, fk_w])

    xn = _prenorm(x, meta, pre_w)
    proj, wfull = wfull(xn) if callable(wfull) else (_matmul(xn, wfull, "nn", MM_TILE, F32, "proj"), wfull)
    qkv = _gdn_prep(proj, conv_wt, nh)
    gates, gt3, gtf = _gates(proj, bias_row, nega_row, nh)
    (o_gdn, s_all, t_all), got = _gdn_fwd(qkv, gates, gt3, nh, None if late_weights is None else late_weights[0])
    if late_weights is not None:
        w_out = late_weights[1](got)
    qkn = _fox_prep(proj, qk_w, nh)
    o_fox, fox_lse = _fox_fwd(qkn, proj, gates, gtf, nh)
    merged = _merge_fox(o_fox, proj, _merge_gdn(o_gdn, proj, gdn_norm_w, nh), nh)
    out = _matmul(merged, w_out, "nn", 4 * LANES, F32, "out_proj")
    dy, dout, loss_blk, dpost_w = _post(out, x, target, post_w)

    dw_out = _matmul(merged, dout, "tn", 4 * LANES, BF16, "dw_out")
    if w_out_grads is None:
        dmerged, gdn_rider = _matmul(dout, w_out, "nt", 4 * LANES, F32, "dmerged"), None
    else:
        dmerged, got = _matmul(dout, w_out, "nt", 4 * LANES, F32, "dmerged", w_out_grads[0](dw_out))
        gdn_rider = w_out_grads[1](dw_out, got)
    do_gdn, dproj, dgdn_norm_w = _merge_gdn_bwd(o_gdn, proj, gdn_norm_w, dmerged, nh)
    do_fox, dproj = _merge_fox_bwd(o_fox, proj, dmerged, dproj, nh)
    dqn, dkn, dc_t, dproj = _fox_bwd(qkn, proj, gates, gtf, fox_lse, do_fox, dproj, nh)
    dproj, dqk_w = _fox_prep_bwd(proj, qk_w, dqn, dkn, dproj, nh)
    (dgq, dgk, dgv, dgate), w_out_parts = _gdn_bwd(qkv, gates, gt3, s_all, t_all, do_gdn, nh, gdn_rider)
    dproj, dconv_wt = _gdn_prep_bwd(proj, conv_wt, dgq, dgk, dgv, dproj, nh)
    dc_rows = jnp.pad(dc_t.reshape(nh, -1), ((2 * nh, LANES - 3 * nh), (0, 0)))
    dproj, gate_sums = _gates_bwd(proj, bias_row, nega_row, gates, dgate, dc_rows, dproj, nh)
    return dict(
        loss=loss_blk[0:1, 0:1], dy=dy, xn=xn, dproj=dproj, post_w=dpost_w,
        conv_wt=dconv_wt, a_log=gate_sums[1:2, nh:2 * nh], dt_bias=gate_sums[0:1, nh:2 * nh],
        gdn_norm_w=dgdn_norm_w, fq_w=dqk_w[0], fk_w=dqk_w[1], f_bias=gate_sums[0:1, 2 * nh:3 * nh], w_out=dw_out,
        w_out_parts=w_out_parts, wfull=wfull)


def _cast_bf16(a, tr, name):
    r, c = a.shape

    def body(a_ref, o_ref):
        o_ref[...] = a_ref[...].astype(BF16)

    return pl.pallas_call(
        body, grid=(r // tr,), in_specs=[pl.BlockSpec((tr, c), lambda i: (i, 0))],
        out_specs=pl.BlockSpec((tr, c), lambda i: (i, 0)), out_shape=jax.ShapeDtypeStruct((r, c), BF16),
        name=name, compiler_params=_cp("parallel"))(a)


def _column_major(a):
    return jnp.transpose(a, (2, 0, 1))


def _cast_bf16_column_major(a3, pieces, name):
    _, r, c = a3.shape
    rows = r // pieces

    def body(a_ref, *o_refs):
        t = a_ref[...].reshape(LANES, r).T.astype(BF16)
        for k, o_ref in enumerate(o_refs):
            o_ref[...] = t[k * rows:(k + 1) * rows]

    return pl.pallas_call(
        body, grid=(pl.cdiv(c, LANES),), in_specs=[pl.BlockSpec((LANES, 1, r), lambda i: (i, 0, 0))],
        out_specs=[pl.BlockSpec((rows, LANES), lambda i: (0, i))] * pieces,
        out_shape=[jax.ShapeDtypeStruct((rows, c), BF16)] * pieces, name=name, compiler_params=_cp("parallel"))(_column_major(a3))


def _adamw_column_major(w3, parts, m3, v3, name):
    _, r, c = w3.shape
    n_parts = parts[0].shape[0]

    def body(w_ref, *refs):
        p_refs, (m_ref, v_ref, g_ref, d_ref, nm_ref, nv_ref) = refs[:len(parts)], refs[len(parts):]
        sums = []
        for p_ref in p_refs:
            g = p_ref[0].astype(F32)
            for s in range(1, n_parts):
                g = g + p_ref[s].astype(F32)
            sums.append(g)
        g = jnp.concatenate(sums, axis=0).T
        flat = lambda ref: ref[...].reshape(LANES, r)
        m_new = ADAM_B1 * flat(m_ref) + (1.0 - ADAM_B1) * g
        v_new = ADAM_B2 * flat(v_ref) + (1.0 - ADAM_B2) * (g * g)
        m_hat = m_new / (1.0 - ADAM_B1 ** ADAM_STEP)
        v_hat = v_new / (1.0 - ADAM_B2 ** ADAM_STEP)
        delta = -ADAM_LR * (m_hat / (jnp.sqrt(v_hat) + ADAM_EPS) + ADAM_WD * flat(w_ref))
        for ref, val in ((g_ref, g), (d_ref, delta), (nm_ref, m_new), (nv_ref, v_new)):
            ref[...] = val.reshape(LANES, 1, r)

    blk = pl.BlockSpec((LANES, 1, r), lambda i: (i, 0, 0))
    outs = pl.pallas_call(
        body, grid=(pl.cdiv(c, LANES),),
        in_specs=[blk] + [pl.BlockSpec((n_parts, p.shape[1], LANES), lambda i: (0, 0, i)) for p in parts] + [blk, blk],
        out_specs=[blk] * 4, out_shape=[jax.ShapeDtypeStruct((c, 1, r), F32)] * 4, name=name,
        compiler_params=_cp("parallel"))(_column_major(w3), *parts, _column_major(m3), _column_major(v3))
    return [jnp.transpose(o, (1, 2, 0)) for o in outs]


def _gather_copies(ins, outs, send_sems, recv_sems, local_sems):
    n = len(ins)
    x, y, c = lax.axis_index("x"), lax.axis_index("y"), lax.axis_index("c")
    me, sibling = (x, y, c), (x, y, 1 - c)
    xn, yn, dg = (1 - x, y), (x, 1 - y), (1 - x, 1 - y)

    def copy(a, k, block, to, src=None):
        px, py, pc = block
        rows = outs[a].at[4 * px + 2 * py + pc]
        return pltpu.make_async_remote_copy(
            src_ref=rows if src is None else src, dst_ref=rows, send_sem=send_sems.at[a, k],
            recv_sem=recv_sems.at[a, k], device_id=to, device_id_type=_MESH)

    local = [pltpu.make_async_copy(ins[a], outs[a].at[4 * x + 2 * y + c], local_sems.at[a]) for a in range(n)]
    own = [cp for a in range(n) for cp in (copy(a, 0, me, sibling, src=ins[a]), copy(a, 1, me, (*xn, c), src=ins[a]),
                                           copy(a, 2, me, (*yn, c), src=ins[a]))]

    def start():
        for cp in local + own:
            cp.start()

    def finish():
        for a in range(n):
            @pl.when(c == 1)
            def _(a=a):
                copy(a, 1, (*xn, c), me).wait_recv()
                copy(a, 3, (*xn, c), (*yn, c)).start()

            @pl.when(c == 0)
            def _(a=a):
                copy(a, 2, (*yn, c), me).wait_recv()
                copy(a, 3, (*yn, c), (*xn, c)).start()
        for a in range(n):
            pl.when(c == 0)(copy(a, 1, (*xn, c), me).wait_recv)
            copy(a, 4, (*xn, c), sibling).start()
            pl.when(c == 1)(copy(a, 2, (*yn, c), me).wait_recv)
            copy(a, 5, (*yn, c), sibling).start()
        for a in range(n):
            copy(a, 3, (*dg, c), me).wait_recv()
            copy(a, 6, (*dg, c), sibling).start()
        for a in range(n):
            copy(a, 0, sibling, me).wait_recv()
            for k, chip in ((4, xn), (5, yn), (6, dg)):
                copy(a, k, (*chip, 1 - c), me).wait_recv()
                copy(a, k, (*chip, c), sibling).wait_send()
            copy(a, 3, (*xn, c), (*yn, c)).wait_send()
        for cp in own:
            cp.wait_send()
        for cp in local:
            cp.wait()

    return start, finish


def _gather_scratch(n):
    return [pltpu.SemaphoreType.DMA((n, N_DEV - 1)), pltpu.SemaphoreType.DMA((n, N_DEV - 1)), pltpu.SemaphoreType.DMA((n,))]


def _gather_rider(arrays):
    return _Rider(list(arrays), [jax.ShapeDtypeStruct((N_DEV,) + a.shape, a.dtype) for a in arrays],
                  _gather_scratch(len(arrays)), {}, lambda ins, outs, scratch: _gather_copies(ins, outs, *scratch))


def _all_gather(arrays, name):
    n = len(arrays)

    def body(*refs):
        start, finish = _gather_copies(refs[:n], refs[n:2 * n], *refs[2 * n:])
        start()
        finish()

    return pl.pallas_call(
        body, in_specs=[_ANY] * n, out_specs=[_ANY] * n,
        out_shape=[jax.ShapeDtypeStruct((N_DEV,) + a.shape, a.dtype) for a in arrays],
        scratch_shapes=_gather_scratch(n), name=name)(*arrays)


SLAB = 10 * LANES


def _slab_start(blk, nh, cols):
    in_second_half = blk >= N_DEV // 2
    shift = (2 * nh if in_second_half else 0) if isinstance(blk, int) else jnp.where(in_second_half, 2 * nh, 0)
    return (blk * cols - shift) // LANES * LANES


def _pair_rider(dw_rows=None, parts=None, nh=None, after=None):
    if dw_rows is not None:
        r, full = dw_rows.shape
        cols = (full - NARROW + 3 * nh) // N_DEV
        out_shapes = [jax.ShapeDtypeStruct((N_CHIP, r, SLAB), dw_rows.dtype), jax.ShapeDtypeStruct((r, NARROW), dw_rows.dtype)]
    else:
        out_shapes = [jax.ShapeDtypeStruct((N_CHIP,) + parts.shape[1:], parts.dtype)]

    def make(ins, outs, scratch):
        send_sems, recv_sems = scratch
        x, y, c = lax.axis_index("x"), lax.axis_index("y"), lax.axis_index("c")
        kw = lambda k: dict(send_sem=send_sems.at[k], recv_sem=recv_sems.at[k], device_id=(x, y, 1 - c), device_id_type=_MESH)
        copies = []
        for q in range(N_CHIP):
            if dw_rows is not None:
                first = pl.multiple_of(_slab_start(2 * q + 1 - c, nh, cols), LANES)
                copies.append(pltpu.make_async_remote_copy(src_ref=ins[0].at[:, pl.ds(first, SLAB)], dst_ref=outs[0].at[q], **kw(q)))
            else:
                copies.append(pltpu.make_async_remote_copy(src_ref=ins[0].at[2 * q + 1 - c], dst_ref=outs[0].at[q], **kw(q)))
        if dw_rows is not None:
            copies.append(pltpu.make_async_remote_copy(src_ref=ins[0].at[:, pl.ds(full - NARROW, NARROW)], dst_ref=outs[1],
                                                       **kw(N_CHIP)))

        def start():
            for cp in copies:
                cp.start()

        def finish():
            for cp in copies:
                cp.wait()

        return start, finish

    return _Rider([dw_rows if dw_rows is not None else parts] + ([] if after is None else [after]), out_shapes,
                  [pltpu.SemaphoreType.DMA((N_CHIP + 1,)), pltpu.SemaphoreType.DMA((N_CHIP + 1,))], {}, make)


def _relayout_pair_sum(dwfull, got_slabs, got_tail, core, nh, tr, name):
    d, full = dwfull.shape
    w = nh * HEAD_DIM
    cols = (8 * w + 3 * nh) // N_DEV
    segs = _native_segments(nh)

    def block(f_ref, s_ref, t_ref, q, blk):
        st = _slab_start(blk, nh, cols)
        wide = f_ref[:, st:st + SLAB].astype(F32) + s_ref[q].astype(F32)
        tail = f_ref[:, 8 * w:].astype(F32) + t_ref[...].astype(F32)
        pieces = []
        for s0, s1, t0 in segs:
            lo, hi = max(s0, blk * cols), min(s1, (blk + 1) * cols)
            if lo < hi:
                at = t0 + lo - s0
                pieces.append(tail[:, at - 8 * w:at - 8 * w + hi - lo] if at >= 8 * w else wide[:, at - st:at - st + hi - lo])
        return (pieces[0] if len(pieces) == 1 else jnp.concatenate(pieces, axis=1)).astype(dwfull.dtype)

    def body(core_ref, f_ref, s_ref, t_ref, o_ref):
        for parity in range(2):
            @pl.when(core_ref[0] == parity)
            def _(parity=parity):
                for q in range(N_CHIP):
                    o_ref[q] = block(f_ref, s_ref, t_ref, q, 2 * q + parity)

    return pl.pallas_call(
        body,
        grid_spec=pltpu.PrefetchScalarGridSpec(
            num_scalar_prefetch=1, grid=(d // tr,),
            in_specs=[pl.BlockSpec((tr, full), lambda i, c_ref: (i, 0)), pl.BlockSpec((N_CHIP, tr, SLAB), lambda i, c_ref: (0, i, 0)),
                      pl.BlockSpec((tr, NARROW), lambda i, c_ref: (i, 0))],
            out_specs=pl.BlockSpec((N_CHIP, tr, cols), lambda i, c_ref: (0, i, 0))),
        out_shape=jax.ShapeDtypeStruct((N_CHIP, d, cols), dwfull.dtype), name=name,
        compiler_params=_cp("parallel"))(core, dwfull, got_slabs, got_tail)


def _pair_sum(parts, got, core, tr, name):
    _, r, c = parts.shape

    def body(core_ref, p_ref, g_ref, o_ref):
        o_ref[...] = (p_ref[...].astype(F32) + g_ref[...].astype(F32)).astype(o_ref.dtype)

    return pl.pallas_call(
        body,
        grid_spec=pltpu.PrefetchScalarGridSpec(
            num_scalar_prefetch=1, grid=(N_CHIP, r // tr),
            in_specs=[pl.BlockSpec((1, tr, c), lambda q, i, core_ref: (2 * q + core_ref[0], i, 0)),
                      pl.BlockSpec((1, tr, c), lambda q, i, core_ref: (q, i, 0))],
            out_specs=pl.BlockSpec((1, tr, c), lambda q, i, core_ref: (q, i, 0))),
        out_shape=jax.ShapeDtypeStruct((N_CHIP, r, c), parts.dtype), name=name,
        compiler_params=_cp("parallel", "parallel"))(core, parts, got)


def _native_segments(nh):
    w = nh * HEAD_DIM
    return [(0, 4 * w, 0), (4 * w, 4 * w + 2 * nh, 8 * w), (4 * w + 2 * nh, 8 * w + 2 * nh, 4 * w),
            (8 * w + 2 * nh, 8 * w + 3 * nh, 8 * w + 2 * nh)]


def _relayout_w_in(wg, nh, tr, name, rows_total=None, into=None):
    _, d, cols = wg.shape
    w = nh * HEAD_DIM
    rows_total = into.shape[0] if into is not None else rows_total or d
    first = (rows_total - d) // tr if into is not None else 0

    def native(ref, j0, j1):
        out = []
        while j0 < j1:
            blk = j0 // cols
            end = min(j1, (blk + 1) * cols)
            out.append(ref[blk, :, pl.ds(j0 - blk * cols, end - j0)])
            j0 = end
        return out

    def body(g_ref, *refs):
        o_ref = refs[-1]
        for cidx in range(8 * w // LANES):
            j0 = cidx * LANES + (0 if cidx * LANES < 4 * w else 2 * nh)
            pieces = native(g_ref, j0, j0 + LANES)
            o_ref[:, cidx * LANES:(cidx + 1) * LANES] = pieces[0] if len(pieces) == 1 else jnp.concatenate(pieces, axis=1)
        pieces = (native(g_ref, 4 * w, 4 * w + 2 * nh) + native(g_ref, 8 * w + 2 * nh, 8 * w + 3 * nh)
                  + [jnp.zeros((tr, NARROW - 3 * nh), wg.dtype)])
        o_ref[:, 8 * w:] = jnp.concatenate(pieces, axis=1)

    return pl.pallas_call(
        body, grid=(d // tr,), in_specs=[pl.BlockSpec((N_DEV, tr, cols), lambda i: (0, i, 0))] + [_ANY] * (into is not None),
        out_specs=pl.BlockSpec((tr, 8 * w + NARROW), lambda i: (first + i, 0)),
        out_shape=jax.ShapeDtypeStruct((rows_total, 8 * w + NARROW), wg.dtype),
        input_output_aliases={1: 0} if into is not None else {},
        name=name, compiler_params=_cp("parallel"))(wg, *([into] if into is not None else []))


def _adamw(w, parts, m, v, tr, name):
    r, c = w.shape
    n_parts = parts.shape[0]

    def body(w_ref, p_ref, m_ref, v_ref, g_ref, d_ref, nm_ref, nv_ref):
        g = p_ref[0].astype(F32)
        for s in range(1, n_parts):
            g = g + p_ref[s].astype(F32)
        m_new = ADAM_B1 * m_ref[...] + (1.0 - ADAM_B1) * g
        v_new = ADAM_B2 * v_ref[...] + (1.0 - ADAM_B2) * (g * g)
        m_hat = m_new / (1.0 - ADAM_B1 ** ADAM_STEP)
        v_hat = v_new / (1.0 - ADAM_B2 ** ADAM_STEP)
        g_ref[...] = g
        d_ref[...] = -ADAM_LR * (m_hat / (jnp.sqrt(v_hat) + ADAM_EPS) + ADAM_WD * w_ref[...])
        nm_ref[...] = m_new
        nv_ref[...] = v_new

    blk = pl.BlockSpec((tr, c), lambda i: (i, 0))
    return pl.pallas_call(
        body, grid=(r // tr,), in_specs=[blk, pl.BlockSpec((n_parts, tr, c), lambda i: (0, i, 0)), blk, blk],
        out_specs=[blk] * 4, out_shape=[jax.ShapeDtypeStruct((r, c), F32)] * 4, name=name,
        compiler_params=_cp("parallel"))(w, parts, m, v)


def _pack_small(d, pre, post, a_log, dt_bias, f_bias, gdn_w, fq_w, fk_w, extra):
    row2 = jnp.concatenate([a_log, dt_bias, f_bias, gdn_w, fq_w, fk_w, extra], axis=1)
    row2 = jnp.pad(row2, ((0, 0), (0, d - row2.shape[1])))
    return jnp.concatenate([pre, post, row2, jnp.zeros((5, d), F32)], axis=0)


def _unpack_small(p, nh):
    o = 3 * nh
    return dict(pre=p[0:1], post=p[1:2], a_log=p[2:3, 0:nh], dt_bias=p[2:3, nh:2 * nh], f_bias=p[2:3, 2 * nh:o],
                gdn_w=p[2:3, o:o + HEAD_DIM], fq_w=p[2:3, o + HEAD_DIM:o + 2 * HEAD_DIM],
                fk_w=p[2:3, o + 2 * HEAD_DIM:o + 3 * HEAD_DIM], extra=p[2, o + 3 * HEAD_DIM])


def kernel(x, meta_tokens, pre_norm_w, w_in, conv_w, a_log, dt_bias, gdn_norm_w, fox_q_norm_w, fox_k_norm_w, fox_f_bias, w_out, post_norm_w, loss_target, m_meta_tokens, m_pre_norm_w, m_w_in, m_conv_w, m_a_log, m_dt_bias, m_gdn_norm_w, m_fox_q_norm_w, m_fox_k_norm_w, m_fox_f_bias, m_w_out, m_post_norm_w, v_meta_tokens, v_pre_norm_w, v_w_in, v_conv_w, v_a_log, v_dt_bias, v_gdn_norm_w, v_fox_q_norm_w, v_fox_k_norm_w, v_fox_f_bias, v_w_out, v_post_norm_w):
    nh = a_log.shape[1]
    d = x.shape[-1]
    w = nh * HEAD_DIM
    zero = jnp.zeros((1, 1), F32)

    w_in_a, w_in_b = _cast_bf16_column_major(w_in, 2, "cast_w_in")
    wg, cg, mg = _all_gather([w_in_a, conv_w[0].T, meta_tokens], "gather_weights")

    def project(xn):
        half = (d // 2, 0), (d // 2, 1)
        wfull = _relayout_w_in(wg, nh, 256, "relayout_w_in_a", rows_total=d)
        proj, got = _matmul(xn, wfull, "nn", MM_TILE, F32, "proj_a", _gather_rider([w_in_b]), a_cols=half[0], b_rows=half[0])
        wfull = _relayout_w_in(got[0], nh, 256, "relayout_w_in_b", into=wfull)
        return _matmul(xn, wfull, "nn", MM_TILE, F32, "proj_b", a_cols=half[1], b_rows=half[1], acc=proj), wfull

    conv_wt = cg.transpose(1, 0, 2).reshape(CONV_WIDTH, 3 * w)
    meta_full = mg.transpose(1, 0, 2).reshape(N_META, d)
    late_weights = (_gather_rider([_cast_bf16(w_out[0], 256, "cast_w_out")]), lambda got: got[0].reshape(2 * w, d))
    core = lax.axis_index("c")
    dev = 4 * lax.axis_index("x") + 2 * lax.axis_index("y") + core
    core_arr = jnp.reshape(core, (1,)).astype(jnp.int32)

    out_parts = lambda dw_out: dw_out.reshape(N_DEV, 2 * w // N_DEV, d)
    g = _layer_grads(
        x[0], loss_target[0], meta_full, pre_norm_w, project, conv_wt, a_log, dt_bias, gdn_norm_w,
        fox_q_norm_w, fox_k_norm_w, fox_f_bias, None, post_norm_w, late_weights=late_weights,
        w_out_grads=(lambda dw_out: _pair_rider(parts=out_parts(dw_out)),
                     lambda dw_out, got: _chip_rider(_pair_sum(out_parts(dw_out), got[0], core_arr, 256, "pair_sum_w_out"))))
    p_out = g["w_out_parts"][0]
    xn, dproj, wfull = g["xn"], g["dproj"], g["wfull"]
    flights, token, dw, rider = [], None, None, None

    def exchange(dw, got, i):
        sums = _relayout_pair_sum(dw, got[0], got[1], core_arr, nh, 128, f"relayout_pair_sum_{i}")
        flight, token = _chip_exchange_start(sums, f"chip_exchange_start_{i}")
        flights.append(flight)
        return token

    for i, (index, parts) in enumerate(DW_IN_PIECES):
        res = _matmul(xn, dproj, "tn", MM_TILE, BF16, f"dw_in_{i}", rider, a_cols=(d // parts, index))
        if i:
            token = exchange(dw, res[1], i - 1)
        dw = res[0] if i else res
        rider = _pair_rider(dw_rows=dw, nh=nh, after=token)
    dxn, (got,) = _dxn(dproj, wfull, [rider], MM_TILE, "dxn")
    token = exchange(dw, got, len(DW_IN_PIECES) - 1)
    (grad_x, dmeta, dpre_w), _ = _prenorm_bwd(dxn, x[0], meta_full, pre_norm_w + token[0:1, 0:1], g["dy"])
    small = _pack_small(d, dpre_w, g["post_w"], g["a_log"], g["dt_bias"], g["f_bias"], g["gdn_norm_w"], g["fq_w"],
                        g["fk_w"], g["loss"])
    a_conv, a_meta, p_small = _all_gather([g["conv_wt"], dmeta, small], "gather_small_grads")
    p_conv = lax.dynamic_slice_in_dim(a_conv, dev * conv_w.shape[1], conv_w.shape[1], axis=2).transpose(0, 2, 1)
    p_meta = lax.dynamic_slice_in_dim(a_meta, dev * meta_tokens.shape[1], meta_tokens.shape[1], axis=2)

    r_out = _adamw(w_out[0], p_out, m_w_out[0], v_w_out[0], 64, "adamw_w_out")
    r_conv = _adamw(conv_w[0], p_conv, m_conv_w[0], v_conv_w[0], conv_w.shape[1], "adamw_conv_w")
    r_meta = _adamw(meta_tokens, p_meta, m_meta_tokens, v_meta_tokens, N_META, "adamw_meta")
    pk = lambda pre, post, a, dt, gw, fq, fk, fb: _pack_small(d, pre, post, a, dt, fb, gw, fq, fk, zero)
    r_small = _adamw(
        pk(pre_norm_w, post_norm_w, a_log, dt_bias, gdn_norm_w, fox_q_norm_w, fox_k_norm_w, fox_f_bias), p_small,
        pk(m_pre_norm_w, m_post_norm_w, m_a_log, m_dt_bias, m_gdn_norm_w, m_fox_q_norm_w, m_fox_k_norm_w, m_fox_f_bias),
        pk(v_pre_norm_w, v_post_norm_w, v_a_log, v_dt_bias, v_gdn_norm_w, v_fox_q_norm_w, v_fox_k_norm_w, v_fox_f_bias),
        8, "adamw_small")
    p_in = _chip_exchange_wait(flights, r_small[0], "chip_exchange_wait")
    r_in = _adamw_column_major(w_in, p_in, m_w_in, v_w_in, "adamw_w_in")

    sm = [_unpack_small(r, nh) for r in r_small]
    outs = []
    for i in range(4):
        s = sm[i]
        outs += [r_meta[i], s["pre"], r_in[i], r_conv[i][None], s["a_log"], s["dt_bias"], s["gdn_w"], s["fq_w"],
                 s["fk_w"], s["f_bias"], r_out[i][None], s["post"]]
    return (sm[0]["extra"], grad_x[None], *outs)
```

```python
import jax
import jax.numpy as jnp
from jax import lax
from jax.experimental import pallas as pl
from jax.experimental.pallas import tpu as pltpu

F32, BF16 = jnp.float32, jnp.bfloat16
HEAD_DIM = 128
N_META = 16
CONV_WIDTH = 4
CHUNK = 128
Q_BLOCK = 128
LANES = 128
EPS = 1e-6
PAD_ROWS = Q_BLOCK - N_META
N_DEV = 8
N_CHIP = 4
VMEM_LIMIT = 56 * 1024 * 1024
NEG = -1e30
NARROW = 2 * LANES
MM_TILE = 6 * LANES
DW_IN_PIECES = ((0, 2), (2, 4), (3, 4))

ADAM_LR, ADAM_B1, ADAM_B2, ADAM_EPS, ADAM_WD, ADAM_STEP = 0.001, 0.9, 0.999, 1e-08, 0.01, 10

_DN = {"nn": (((1,), (0,)), ((), ())), "nt": (((1,), (1,)), ((), ())), "tn": (((0,), (0,)), ((), ()))}
_DN3 = {"nn": (((2,), (1,)), ((0,), (0,))), "nt": (((2,), (2,)), ((0,), (0,))), "tn": (((1,), (1,)), ((0,), (0,)))}
_ANY = pl.BlockSpec(memory_space=pl.ANY)
_MESH = pl.DeviceIdType.MESH


def _cp(*sem):
    return pltpu.CompilerParams(dimension_semantics=sem, vmem_limit_bytes=VMEM_LIMIT)


def _dot(a, b, dims="nn", prec=None):
    return lax.dot_general(a, b, _DN[dims], precision=prec, preferred_element_type=F32)


def _bdot(a, b, dims="nn"):
    return _dot(a.astype(BF16), b.astype(BF16), dims)


def _hdot(a, b, dims="nn"):
    return _dot(a, b, dims, prec=lax.Precision.HIGHEST)


def _dot3(a, b, dims="nn"):
    return lax.dot_general(a, b, _DN3[dims], preferred_element_type=F32)


def _bdot3(a, b, dims="nn"):
    return _dot3(a.astype(BF16), b.astype(BF16), dims)


def _split(a):
    hi = a.astype(BF16)
    return hi, (a - hi.astype(F32)).astype(BF16)


def _iota(shape, dim):
    return lax.broadcasted_iota(jnp.int32, shape, dim)


def _sigmoid(z):
    return 1.0 / (1.0 + jnp.exp(-z))


def _softplus(z):
    e = jnp.exp(-jnp.abs(z))
    u = 1.0 + e
    l1p = jnp.where(u == 1.0, e, jnp.log(u) * (e / jnp.where(u == 1.0, 1.0, u - 1.0)))
    return jnp.maximum(z, 0.0) + l1p


def _silu_and_grad(z):
    s = _sigmoid(z)
    return z * s, s * (1.0 + z * (1.0 - s))


def _rms(x):
    return lax.rsqrt(jnp.mean(x * x, axis=-1, keepdims=True) + EPS)


def _h_tile(i, x_ref, meta_ref):
    first = jnp.concatenate([jnp.zeros((PAD_ROWS, x_ref.shape[1]), F32), meta_ref[...]], axis=0)
    return jnp.where(i == 0, first, x_ref[...])


def _x_rows(d):
    return pl.BlockSpec((Q_BLOCK, d), lambda i: (jnp.maximum(i - 1, 0), 0))


def _prenorm(x, meta, w, rider=None):
    seq, d = x.shape
    lp = seq + Q_BLOCK
    steps = lp // Q_BLOCK

    def body(start, finish, x_ref, m_ref, w_ref, o_ref):
        pl.when(pl.program_id(0) == 0)(start)
        h = _h_tile(pl.program_id(0), x_ref, m_ref)
        o_ref[...] = (h * _rms(h) * w_ref[...]).astype(BF16)
        pl.when(pl.program_id(0) == steps - 1)(finish)

    (xn,), got = _hosted_call(
        body, [rider], 3, 1, 0, grid=(steps,),
        in_specs=[_x_rows(d), pl.BlockSpec((N_META, d), lambda i: (0, 0)), pl.BlockSpec((1, d), lambda i: (0, 0))],
        out_specs=[pl.BlockSpec((Q_BLOCK, d), lambda i: (i, 0))], out_shape=[jax.ShapeDtypeStruct((lp, d), BF16)],
        name="prenorm", compiler_params=_cp("parallel" if rider is None else "arbitrary"))(x, meta, w)
    return xn if rider is None else (xn, got[0])


def _tile(n, want):
    return max(t for t in range(LANES, want + 1, LANES) if n % t == 0)


class _Rider:
    def __init__(self, inputs, out_shapes, scratch, aliases, make):
        self.inputs, self.out_shapes, self.scratch, self.aliases, self.make = inputs, out_shapes, scratch, aliases, make


def _hosted_call(body, riders, n_in, n_out, n_scratch, *, in_specs, out_specs, out_shape, scratch_shapes=(), aliases=None,
                 **kw):
    riders = [r for r in riders if r is not None]
    r_in = [len(r.inputs) for r in riders]
    r_out = [len(r.out_shapes) for r in riders]
    r_scr = [len(r.scratch) for r in riders]
    al = dict(aliases or {})
    for k, r in enumerate(riders):
        al.update({n_in + sum(r_in[:k]) + i: n_out + sum(r_out[:k]) + o for i, o in r.aliases.items()})

    def full_body(*refs):
        ins, rest = refs[:n_in + sum(r_in)], refs[n_in + sum(r_in):]
        outs, scr = rest[:n_out + sum(r_out)], rest[n_out + sum(r_out):]
        hooks = [r.make(ins[n_in + sum(r_in[:k]):n_in + sum(r_in[:k + 1])], outs[n_out + sum(r_out[:k]):n_out + sum(r_out[:k + 1])],
                        scr[n_scratch + sum(r_scr[:k]):n_scratch + sum(r_scr[:k + 1])]) for k, r in enumerate(riders)]

        def start():
            for h in hooks:
                h[0]()

        def finish():
            for h in hooks:
                h[1]()

        body(start, finish, *ins[:n_in], *outs[:n_out], *scr[:n_scratch])

    call = pl.pallas_call(
        full_body, in_specs=list(in_specs) + [_ANY] * sum(r_in), out_specs=list(out_specs) + [_ANY] * sum(r_out),
        out_shape=list(out_shape) + [s for r in riders for s in r.out_shapes],
        scratch_shapes=list(scratch_shapes) + [s for r in riders for s in r.scratch], input_output_aliases=al, **kw)

    def run(*args):
        res = call(*args, *[t for r in riders for t in r.inputs])
        return res[:n_out], [res[n_out + sum(r_out[:k]):n_out + sum(r_out[:k + 1])] for k in range(len(riders))]

    return run


def _matmul(a, b, dims, tn, out_dtype, name, rider=None, a_cols=None, b_rows=None, acc=None):
    a_shape = a.shape if a_cols is None else (a.shape[0], a_cols[0])
    a_index = 0 if a_cols is None else a_cols[1]
    m = a_shape[1] if dims == "tn" else a_shape[0]
    n = b.shape[0] if dims == "nt" else b.shape[1]
    kdim = b.shape[1] if dims == "nt" else b.shape[0] if b_rows is None else b_rows[0]
    b_index = 0 if b_rows is None else b_rows[1]
    tn = _tile(n, tn)
    steps = n // tn
    b_spec = pl.BlockSpec((tn, kdim), lambda j: (j, 0)) if dims == "nt" else pl.BlockSpec((kdim, tn), lambda j: (b_index, j))
    o_spec = pl.BlockSpec((m, tn), lambda j: (0, j))

    def body(start, finish, a_ref, b_ref, *refs):
        pl.when(pl.program_id(0) == 0)(start)
        prod = _dot(a_ref[...], b_ref[...], dims)
        refs[-1][...] = (prod if acc is None else prod + refs[0][...]).astype(out_dtype)
        pl.when(pl.program_id(0) == steps - 1)(finish)

    (out,), got = _hosted_call(
        body, [rider], 2 + (acc is not None), 1, 0, grid=(steps,),
        in_specs=[pl.BlockSpec(a_shape, lambda j: (0, a_index)), b_spec] + [o_spec] * (acc is not None),
        out_specs=[o_spec], out_shape=[jax.ShapeDtypeStruct((m, n), out_dtype)], aliases={2: 0} if acc is not None else None,
        name=name, compiler_params=_cp("parallel" if rider is None else "arbitrary"))(a, b, *([acc] if acc is not None else []))
    return out if rider is None else (out, got[0])


def _chip_rider(sums):
    def make(ins, outs, scratch):
        local, remote = _chip_exchange_copies(ins[0], outs[0], *scratch)

        def start():
            for cp in [local] + remote:
                cp.start()

        def finish():
            local.wait()
            for cp in remote:
                cp.wait_send()
                cp.wait_recv()

        return start, finish

    return _Rider([sums], [jax.ShapeDtypeStruct(sums.shape, sums.dtype)],
                  [pltpu.SemaphoreType.DMA((N_CHIP - 1,)), pltpu.SemaphoreType.DMA((N_CHIP - 1,)), pltpu.SemaphoreType.DMA((1,))],
                  {}, make)


_HBM = pl.BlockSpec(memory_space=pltpu.HBM)
_SEM = pl.BlockSpec(memory_space=pltpu.SEMAPHORE)


def _chip_exchange_copies(sums_ref, land_ref, send_sems, recv_sems, local_sem):
    x, y, core = lax.axis_index("x"), lax.axis_index("y"), lax.axis_index("c")
    mine = 2 * x + y
    local = pltpu.make_async_copy(sums_ref.at[mine], land_ref.at[mine], local_sem.at[0])
    remote = []
    for k in range(1, N_CHIP):
        px = 1 - x if k & 2 else x
        py = 1 - y if k & 1 else y
        remote.append(pltpu.make_async_remote_copy(
            src_ref=sums_ref.at[2 * px + py], dst_ref=land_ref.at[mine], send_sem=send_sems.at[k - 1],
            recv_sem=recv_sems.at[k - 1], device_id=(px, py, core), device_id_type=_MESH))
    return local, remote


def _chip_exchange_start(sums, name):
    def body(s_ref, send_sems, recv_sems, local_sem, s_thru, land_ref, token):
        local, remote = _chip_exchange_copies(s_ref, land_ref, send_sems, recv_sems, local_sem)
        for cp in [local] + remote:
            cp.start()
        token[...] = jnp.zeros_like(token)

    *flight, token = pl.pallas_call(
        body, name=name,
        out_shape=(pltpu.SemaphoreType.DMA((N_CHIP - 1,)), pltpu.SemaphoreType.DMA((N_CHIP - 1,)), pltpu.SemaphoreType.DMA((1,)),
                   pltpu.HBM(sums.shape, sums.dtype), pltpu.HBM(sums.shape, sums.dtype), jax.ShapeDtypeStruct((8, LANES), F32)),
        in_specs=(_HBM,), out_specs=(_SEM, _SEM, _SEM, _HBM, _HBM, pl.BlockSpec(memory_space=pltpu.VMEM)),
        input_output_aliases={0: 3},
        compiler_params=pltpu.CompilerParams(has_side_effects=pltpu.SideEffectType.DATAFLOW_SIDE_EFFECTING))(
            pltpu.with_memory_space_constraint(sums, pltpu.HBM))
    return flight, token


def _chip_exchange_wait(flights, after, name):
    n = len(flights)

    def body(*refs):
        for i in range(n):
            s_ref, land_ref = refs[2 * i:2 * i + 2]
            local, remote = _chip_exchange_copies(s_ref, land_ref, *refs[2 * n + 3 * i:2 * n + 3 * i + 3])
            local.wait()
            for cp in remote:
                cp.wait_send()
                cp.wait_recv()

    buffers = [b for f in flights for b in f[3:]]
    res = pl.pallas_call(
        body, name=name, out_shape=tuple(pltpu.HBM(b.shape, b.dtype) for b in buffers),
        in_specs=(_HBM,) * (2 * n) + (_SEM,) * (3 * n) + (_ANY,), out_specs=(_HBM,) * (2 * n),
        input_output_aliases={i: i for i in range(2 * n)},
        compiler_params=pltpu.CompilerParams(has_side_effects=pltpu.SideEffectType.DATAFLOW_SIDE_EFFECTING))(
            *buffers, *[s for f in flights for s in f[:3]], after)
    return res[1::2]


def _dxn(dproj, wfull, riders, tk, name):
    m, k = dproj.shape
    n = wfull.shape[0]
    tk = _tile(k, tk)
    steps = k // tk

    def body(start, finish, a_ref, b_ref, o_ref):
        j = pl.program_id(0)

        @pl.when(j == 0)
        def _():
            start()
            o_ref[...] = jnp.zeros_like(o_ref)
        o_ref[...] += _dot(a_ref[...], b_ref[...], "nt")
        pl.when(j == steps - 1)(finish)

    (dxn,), got = _hosted_call(
        body, riders, 2, 1, 0, grid=(steps,),
        in_specs=[pl.BlockSpec((m, tk), lambda j: (0, j)), pl.BlockSpec((n, tk), lambda j: (0, j))],
        out_specs=[pl.BlockSpec((m, n), lambda j: (0, 0))], out_shape=[jax.ShapeDtypeStruct((m, n), F32)],
        name=name, compiler_params=_cp("arbitrary"))(dproj, wfull)
    return dxn, got


def _conv_taps(x, w):
    c = x * w[CONV_WIDTH - 1:CONV_WIDTH, :]
    for j in range(CONV_WIDTH - 1):
        c = c + pltpu.roll(x, CONV_WIDTH - 1 - j, 0) * w[j:j + 1, :]
    return c


def _gdn_prep(proj, conv_wt, nh):
    lp = proj.shape[0]
    scale = HEAD_DIM ** -0.5

    def body(x_ref, w_ref, o_ref):
        which = pl.program_id(0) // nh
        c = _conv_taps(x_ref[...], w_ref[...])
        s = c * _sigmoid(c)
        r = lax.rsqrt(jnp.sum(s * s, axis=-1, keepdims=True) + EPS)
        f = jnp.where(which == 0, r * scale, jnp.where(which == 1, r, 1.0))
        o_ref[...] = jnp.where(_iota(s.shape, 0) >= PAD_ROWS, s * f, 0.0)

    return pl.pallas_call(
        body, grid=(3 * nh,),
        in_specs=[pl.BlockSpec((lp, LANES), lambda s: (0, s)), pl.BlockSpec((CONV_WIDTH, LANES), lambda s: (0, s))],
        out_specs=pl.BlockSpec((lp, LANES), lambda s: (0, s)),
        out_shape=jax.ShapeDtypeStruct((lp, 3 * nh * HEAD_DIM), F32), name="gdn_prep",
        compiler_params=_cp("parallel"))(proj, conv_wt)


def _gdn_prep_bwd(proj, conv_wt, dq, dk, dv, dproj, nh):
    lp = proj.shape[0]
    scale = HEAD_DIM ** -0.5
    part = lambda p: pl.BlockSpec((lp, LANES), lambda s: (0, jnp.clip(s - p * nh, 0, nh - 1)))

    def body(x_ref, w_ref, dq_ref, dk_ref, dv_ref, _, dx_ref, dw_ref):
        which = pl.program_id(0) // nh
        x = x_ref[...]
        w = w_ref[...]
        c = _conv_taps(x, w)
        sg = _sigmoid(c)
        s = c * sg
        r = lax.rsqrt(jnp.sum(s * s, axis=-1, keepdims=True) + EPS)
        dy = jnp.where(which == 0, dq_ref[...], jnp.where(which == 1, dk_ref[...], dv_ref[...]))
        dy = jnp.where(_iota(s.shape, 0) >= PAD_ROWS, dy, 0.0)
        y0 = s * r
        dy0 = dy * jnp.where(which == 0, scale, 1.0)
        ds_n = r * (dy0 - y0 * jnp.sum(dy0 * y0, axis=-1, keepdims=True))
        ds = jnp.where(which == 2, dy, ds_n)
        dc = ds * (sg * (1.0 + c * (1.0 - sg)))
        dx = dc * w[CONV_WIDTH - 1:CONV_WIDTH, :]
        rows = [jnp.sum(dc * x, axis=0, keepdims=True)]
        for j in range(CONV_WIDTH - 2, -1, -1):
            sh = CONV_WIDTH - 1 - j
            dx = dx + pltpu.roll(dc, lp - sh, 0) * w[j:j + 1, :]
            rows.insert(0, jnp.sum(dc * pltpu.roll(x, sh, 0), axis=0, keepdims=True))
        dx_ref[...] = dx.astype(BF16)
        dw_ref[...] = jnp.concatenate(rows, axis=0)

    strip = pl.BlockSpec((lp, LANES), lambda s: (0, s))
    taps = pl.BlockSpec((CONV_WIDTH, LANES), lambda s: (0, s))
    return pl.pallas_call(
        body, grid=(3 * nh,), in_specs=[strip, taps, part(0), part(1), part(2), _ANY], out_specs=[strip, taps],
        out_shape=[jax.ShapeDtypeStruct(dproj.shape, BF16), jax.ShapeDtypeStruct((CONV_WIDTH, 3 * nh * HEAD_DIM), F32)],
        input_output_aliases={5: 0}, name="gdn_prep_bwd", compiler_params=_cp("parallel"))(proj, conv_wt, dq, dk, dv, dproj)


def _gates(proj, bias_row, nega_row, nh):
    lp = proj.shape[0]
    nc = lp // CHUNK

    def body(p_ref, b_ref, a_ref, g_ref, gt3_ref, gtf_ref):
        lane = _iota((CHUNK, LANES), 1)
        tri = (_iota((CHUNK, CHUNK), 0) >= _iota((CHUNK, CHUNK), 1)).astype(F32)

        def step(n, carry):
            r0 = pl.multiple_of(n * CHUNK, CHUNK)
            z = p_ref[pl.ds(r0, CHUNK), :] + b_ref[...]
            base = jnp.where(lane < nh, _sigmoid(z),
                             jnp.where(lane < 2 * nh, a_ref[...] * _softplus(z),
                                       jnp.where(lane < 3 * nh, -_softplus(-z), 0.0)))
            base = jnp.where(r0 + _iota((CHUNK, LANES), 0) >= PAD_ROWS, base, 0.0)
            cs = _hdot(tri, base)
            run = jnp.where((lane >= 2 * nh) & (lane < 3 * nh), cs + carry, cs)
            sh = pltpu.roll(run, 2 * nh, 1)
            out = base + jnp.where((lane >= 3 * nh) & (lane < 5 * nh), sh, 0.0)
            g_ref[pl.ds(r0, CHUNK), :] = out
            gt3_ref[n] = out.T
            return carry + cs[CHUNK - 1:CHUNK, :]

        lax.fori_loop(0, nc, step, jnp.zeros((1, LANES), F32))
        gtf_ref[...] = g_ref[...].T

    vec = pl.BlockSpec((1, LANES), lambda i: (0, 0))
    return pl.pallas_call(
        body, grid=(1,), in_specs=[pl.BlockSpec((lp, LANES), lambda i: (0, 8 * nh)), vec, vec],
        out_specs=[pl.BlockSpec((lp, LANES), lambda i: (0, 0)), pl.BlockSpec((nc, LANES, CHUNK), lambda i: (0, 0, 0)),
                   pl.BlockSpec((LANES, lp), lambda i: (0, 0))],
        out_shape=[jax.ShapeDtypeStruct((lp, LANES), F32), jax.ShapeDtypeStruct((nc, LANES, CHUNK), F32),
                   jax.ShapeDtypeStruct((LANES, lp), F32)],
        name="gates", compiler_params=_cp("arbitrary"))(proj, bias_row, nega_row)


def _gates_bwd(proj, bias_row, nega_row, gates, dgate_gdn, dc_t, dproj, nh):
    lp = proj.shape[0]
    nc = lp // CHUNK

    def body(p_ref, b_ref, a_ref, g_ref, dg_ref, dc_ref, _, dz_ref, sm_ref, dct_scr):
        lane = _iota((CHUNK, LANES), 1)
        triu = (_iota((CHUNK, CHUNK), 0) <= _iota((CHUNK, CHUNK), 1)).astype(F32)
        dct_scr[...] = dc_ref[...].T
        sm_ref[...] = jnp.zeros_like(sm_ref)
        dz_ref[:, LANES:] = jnp.zeros((lp, NARROW - LANES), BF16)

        def step(i, carry):
            n = nc - 1 - i
            r0 = pl.multiple_of(n * CHUNK, CHUNK)
            z = p_ref[pl.ds(r0, CHUNK), :] + b_ref[...]
            gt = g_ref[pl.ds(r0, CHUNK), :]
            dgd = dg_ref[pl.ds(r0, CHUNK), :]
            dch = dct_scr[pl.ds(r0, CHUNK), :]
            rc = _hdot(triu, dch) + carry
            sg = _sigmoid(z)
            dz = jnp.where(lane < nh, dgd * sg * (1.0 - sg),
                           jnp.where(lane < 2 * nh, dgd * a_ref[...] * sg,
                                     jnp.where(lane < 3 * nh, rc * (1.0 - sg), 0.0)))
            dz = jnp.where(r0 + _iota((CHUNK, LANES), 0) >= PAD_ROWS, dz, 0.0)
            dz_ref[pl.ds(r0, CHUNK), 0:LANES] = dz.astype(BF16)
            sm_ref[0:1, :] += jnp.sum(dz, axis=0, keepdims=True)
            sm_ref[1:2, :] += jnp.sum(jnp.where((lane >= nh) & (lane < 2 * nh), dgd * gt, 0.0), axis=0, keepdims=True)
            return carry + jnp.sum(dch, axis=0, keepdims=True)

        lax.fori_loop(0, nc, step, jnp.zeros((1, LANES), F32))

    vec = pl.BlockSpec((1, LANES), lambda i: (0, 0))
    full = pl.BlockSpec((lp, LANES), lambda i: (0, 0))
    last = pl.BlockSpec((lp, LANES), lambda i: (0, 8 * nh))
    tail = pl.BlockSpec((lp, NARROW), lambda i: (0, 8 * nh * LANES // NARROW))
    return pl.pallas_call(
        body, grid=(1,), in_specs=[last, vec, vec, full, full, pl.BlockSpec((LANES, lp), lambda i: (0, 0)), _ANY],
        out_specs=[tail, pl.BlockSpec((8, LANES), lambda i: (0, 0))],
        out_shape=[jax.ShapeDtypeStruct(dproj.shape, BF16), jax.ShapeDtypeStruct((8, LANES), F32)],
        scratch_shapes=[pltpu.VMEM((lp, LANES), F32)], input_output_aliases={6: 0},
        name="gates_bwd", compiler_params=_cp("arbitrary"))(proj, bias_row, nega_row, gates, dgate_gdn, dc_t, dproj)


def _tri_inv(a):
    t = jnp.where(_iota(a.shape, 1) == _iota(a.shape, 2), 1.0, 0.0) - a
    p = a
    for _ in range(CHUNK.bit_length() - 2):
        ph, pw = _split(p)
        p = _dot3(ph, ph) + (_dot3(ph, pw) + _dot3(pw, ph))
        ph, pw = _split(p)
        th, tw = _split(t)
        t = t + (_dot3(th, ph) + (_dot3(th, pw) + _dot3(tw, ph)))
    return t


def _gdn_chunk(q, k, v, beta, gc, gr, t=None):
    ii, jj = _iota((1, CHUNK, CHUNK), 1), _iota((1, CHUNK, CHUNK), 2)
    causal, strict = ii >= jj, ii > jj
    dm = jnp.where(causal, jnp.exp(jnp.where(causal, gc - gr, 0.0)), 0.0)
    kk = _bdot3(k, k, "nt")
    a = jnp.where(strict, beta * kk * dm, 0.0)
    if t is None:
        t = _tri_inv(a)
    eg = jnp.exp(gc)
    glast = gc[:, CHUNK - 1:CHUNK, :]
    ekd = jnp.exp(glast - gc)
    bv = beta * v
    bk = (beta * eg) * k
    ub = _bdot3(t, jnp.concatenate([bv, bk], axis=2))
    qk = _bdot3(q, k, "nt")
    return dict(causal=causal, strict=strict, dm=dm, kk=kk, a=a, t=t, eg=eg, ekd=ekd, bv=bv, bk=bk,
                u=ub[:, :, :HEAD_DIM], w=ub[:, :, HEAD_DIM:], qk=qk, aqk=jnp.where(causal, qk * dm, 0.0),
                q_dec=q * eg, k_dec=k * ekd, decay=jnp.exp(glast))


def _heads(ref, nh):
    return jnp.stack([ref[:, h * HEAD_DIM:(h + 1) * HEAD_DIM] for h in range(nh)], axis=0)


def _gdn_chunk_inputs(q_ref, k_ref, v_ref, g, gt, nh):
    col = lambda o: jnp.stack([g[:, o + h:o + h + 1] for h in range(nh)], axis=0)
    gr = jnp.stack([gt[3 * nh + h:3 * nh + h + 1, :] for h in range(nh)], axis=0)
    return _heads(q_ref, nh), _heads(k_ref, nh), _heads(v_ref, nh), col(0), col(3 * nh), gr


def _gdn_fwd(qkv, gates, gt3, nh, rider=None):
    lp = qkv.shape[0]
    nc = lp // CHUNK
    w = nh * HEAD_DIM

    def body(start, finish, q_ref, k_ref, v_ref, g_ref, gt_ref, o_ref, sall_ref, tall_ref, s_scr):
        @pl.when(pl.program_id(0) == 0)
        def _():
            start()
            s_scr[...] = jnp.zeros_like(s_scr)
        c = _gdn_chunk(*_gdn_chunk_inputs(q_ref, k_ref, v_ref, g_ref[...], gt_ref[0], nh))
        s = s_scr[...]
        sall_ref[0] = s
        tall_ref[0] = c["t"]
        v_new = c["u"] - _bdot3(c["w"], s)
        o = _bdot3(c["q_dec"], s) + _bdot3(c["aqk"], v_new)
        s_scr[...] = s * c["decay"] + _bdot3(c["k_dec"], v_new, "tn")
        for h in range(nh):
            o_ref[:, h * HEAD_DIM:(h + 1) * HEAD_DIM] = o[h]
        pl.when(pl.program_id(0) == nc - 1)(finish)

    outs, got = _hosted_call(
        body, [rider], 5, 3, 1, grid=(nc,),
        in_specs=[pl.BlockSpec((CHUNK, w), lambda n: (n, 0)), pl.BlockSpec((CHUNK, w), lambda n: (n, 1)),
                  pl.BlockSpec((CHUNK, w), lambda n: (n, 2)), pl.BlockSpec((CHUNK, LANES), lambda n: (n, 0)),
                  pl.BlockSpec((1, LANES, CHUNK), lambda n: (n, 0, 0))],
        out_specs=[pl.BlockSpec((CHUNK, w), lambda n: (n, 0)),
                   pl.BlockSpec((1, nh, HEAD_DIM, HEAD_DIM), lambda n: (n, 0, 0, 0)),
                   pl.BlockSpec((1, nh, CHUNK, CHUNK), lambda n: (n, 0, 0, 0))],
        out_shape=[jax.ShapeDtypeStruct((lp, w), F32), jax.ShapeDtypeStruct((nc, nh, HEAD_DIM, HEAD_DIM), F32),
                   jax.ShapeDtypeStruct((nc, nh, CHUNK, CHUNK), F32)],
        scratch_shapes=[pltpu.VMEM((nh, HEAD_DIM, HEAD_DIM), F32)],
        name="gdn_fwd", compiler_params=_cp("arbitrary"))(qkv, qkv, qkv, gates, gt3)
    return outs, (got[0] if got else None)


def _gdn_bwd(qkv, gates, gt3, s_all, t_all, do, nh, rider=None):
    lp = qkv.shape[0]
    nc = lp // CHUNK
    w = nh * HEAD_DIM
    rev = lambda n: nc - 1 - n

    def body(start, finish, q_ref, k_ref, v_ref, g_ref, gt_ref, s_ref, t_ref, do_ref, dq_ref, dk_ref, dv_ref, dg_ref, ds_scr):
        @pl.when(pl.program_id(0) == 0)
        def _():
            start()
            ds_scr[...] = jnp.zeros_like(ds_scr)
        q, k, v, beta, gc, gr = _gdn_chunk_inputs(q_ref, k_ref, v_ref, g_ref[...], gt_ref[0], nh)
        c = _gdn_chunk(q, k, v, beta, gc, gr, t_ref[0])
        s = s_ref[0]
        dsn = ds_scr[...]
        dout = _heads(do_ref, nh)
        v_new = c["u"] - _bdot3(c["w"], s)
        dq_dec = _bdot3(dout, s, "nt")
        daqk = jnp.where(c["causal"], _bdot3(dout, v_new, "nt"), 0.0)
        dv_new = _bdot3(c["aqk"], dout, "tn") + _bdot3(c["k_dec"], dsn)
        dk_dec = _bdot3(v_new, dsn, "nt")
        ddecay = jnp.sum(jnp.sum(dsn * s, axis=2, keepdims=True), axis=1, keepdims=True)
        dw = -_bdot3(dv_new, s, "nt")
        ds_scr[...] = _bdot3(c["q_dec"], dout, "tn") + c["decay"] * dsn - _bdot3(c["w"], dv_new, "tn")
        duw = jnp.concatenate([dv_new, dw], axis=2)
        dt = _bdot3(duw, jnp.concatenate([c["bv"], c["bk"]], axis=2), "nt")
        dbvk = _bdot3(c["t"], duw, "tn")
        dbv, dbk = dbvk[:, :, :HEAD_DIM], dbvk[:, :, HEAD_DIM:]
        da = jnp.where(c["strict"], -_bdot3(_bdot3(c["t"], dt, "tn"), c["t"], "nt"), 0.0)
        dkk = da * beta * c["dm"]
        dqk = daqk * c["dm"]
        e = da * c["a"] + daqk * c["aqk"]
        dq = dq_dec * c["eg"] + _bdot3(dqk, k)
        dk = (dk_dec * c["ekd"] + _bdot3(dkk, k) + _bdot3(dkk, k, "tn") + _bdot3(dqk, q, "tn")
              + (beta * c["eg"]) * dbk)
        dv = beta * dbv
        rs = lambda x: jnp.sum(x, axis=2, keepdims=True)
        dbeta = rs(dbv * v) + c["eg"] * rs(dbk * k) + rs(da * c["kk"] * c["dm"])
        kd_term = rs(dk_dec * c["k_dec"])
        eh, ew = _split(e)
        ones = jnp.ones((nh, CHUNK, LANES), BF16)
        col_sums = (_dot3(eh, ones, "tn") + _dot3(ew, ones, "tn"))[:, :, 0:1]
        dg_cum = rs(dq_dec * c["q_dec"]) - kd_term + rs(dbk * c["bk"]) + rs(e) - col_sums
        last = jnp.sum(kd_term, axis=1, keepdims=True) + ddecay * c["decay"]
        dg_cum = dg_cum + jnp.where(_iota((1, CHUNK, 1), 1) == CHUNK - 1, last, 0.0)
        lane = _iota((CHUNK, LANES), 1)
        acc = jnp.zeros((CHUNK, LANES), F32)
        for h in range(nh):
            sl = slice(h * HEAD_DIM, (h + 1) * HEAD_DIM)
            dq_ref[:, sl] = dq[h]
            dk_ref[:, sl] = dk[h]
            dv_ref[:, sl] = dv[h]
            acc = acc + jnp.where(lane == h, dbeta[h], 0.0) + jnp.where(lane == nh + h, dg_cum[h], 0.0)
        triu = (_iota((CHUNK, CHUNK), 0) <= _iota((CHUNK, CHUNK), 1)).astype(F32)
        dg_ref[...] = jnp.where(lane < nh, acc, _hdot(triu, acc))
        pl.when(pl.program_id(0) == nc - 1)(finish)

    outs, got = _hosted_call(
        body, [rider], 8, 4, 1, grid=(nc,),
        in_specs=[pl.BlockSpec((CHUNK, w), lambda n: (rev(n), 0)), pl.BlockSpec((CHUNK, w), lambda n: (rev(n), 1)),
                  pl.BlockSpec((CHUNK, w), lambda n: (rev(n), 2)), pl.BlockSpec((CHUNK, LANES), lambda n: (rev(n), 0)),
                  pl.BlockSpec((1, LANES, CHUNK), lambda n: (rev(n), 0, 0)),
                  pl.BlockSpec((1, nh, HEAD_DIM, HEAD_DIM), lambda n: (rev(n), 0, 0, 0)),
                  pl.BlockSpec((1, nh, CHUNK, CHUNK), lambda n: (rev(n), 0, 0, 0)),
                  pl.BlockSpec((CHUNK, w), lambda n: (rev(n), 0))],
        out_specs=[pl.BlockSpec((CHUNK, w), lambda n: (rev(n), 0))] * 3 + [pl.BlockSpec((CHUNK, LANES), lambda n: (rev(n), 0))],
        out_shape=[jax.ShapeDtypeStruct((lp, w), F32)] * 3 + [jax.ShapeDtypeStruct((lp, LANES), F32)],
        scratch_shapes=[pltpu.VMEM((nh, HEAD_DIM, HEAD_DIM), F32)],
        name="gdn_bwd", compiler_params=_cp("arbitrary"))(qkv, qkv, qkv, gates, gt3, s_all, t_all, do)
    return outs, (got[0] if got else None)


def _merge_gdn(o_gdn, proj, norm_w, nh):
    lp = o_gdn.shape[0]

    def body(o_ref, z_ref, w_ref, m_ref):
        o = o_ref[...]
        z = z_ref[...]
        m_ref[...] = (o * _rms(o) * w_ref[...] * (z * _sigmoid(z))).astype(BF16)

    return pl.pallas_call(
        body, grid=(nh,),
        in_specs=[pl.BlockSpec((lp, LANES), lambda s: (0, s)), pl.BlockSpec((lp, LANES), lambda s: (0, 3 * nh + s)),
                  pl.BlockSpec((1, LANES), lambda s: (0, 0))],
        out_specs=pl.BlockSpec((lp, LANES), lambda s: (0, s)),
        out_shape=jax.ShapeDtypeStruct((lp, 2 * nh * HEAD_DIM), BF16), name="merge_gdn",
        compiler_params=_cp("parallel"))(o_gdn, proj, norm_w)


def _merge_gdn_bwd(o_gdn, proj, norm_w, dmerged, nh):
    lp = o_gdn.shape[0]

    def body(o_ref, z_ref, w_ref, dm_ref, do_ref, dz_ref, dw_ref):
        o = o_ref[...]
        r = _rms(o)
        xh = o * r
        silu, dsilu = _silu_and_grad(z_ref[...])
        dm = dm_ref[...]
        dn = dm * silu
        dz_ref[...] = (dm * (xh * w_ref[...]) * dsilu).astype(BF16)
        dnw = dn * w_ref[...]
        do_ref[...] = r * (dnw - xh * jnp.mean(dnw * xh, axis=-1, keepdims=True))

        @pl.when(pl.program_id(0) == 0)
        def _():
            dw_ref[...] = jnp.zeros_like(dw_ref)
        dw_ref[...] += jnp.sum(dn * xh, axis=0, keepdims=True)

    w = nh * HEAD_DIM
    return pl.pallas_call(
        body, grid=(nh,),
        in_specs=[pl.BlockSpec((lp, LANES), lambda s: (0, s)), pl.BlockSpec((lp, LANES), lambda s: (0, 3 * nh + s)),
                  pl.BlockSpec((1, LANES), lambda s: (0, 0)), pl.BlockSpec((lp, LANES), lambda s: (0, s))],
        out_specs=[pl.BlockSpec((lp, LANES), lambda s: (0, s)), pl.BlockSpec((lp, LANES), lambda s: (0, 3 * nh + s)),
                   pl.BlockSpec((1, LANES), lambda s: (0, 0))],
        out_shape=[jax.ShapeDtypeStruct((lp, w), F32), jax.ShapeDtypeStruct((lp, 8 * w + NARROW), BF16),
                   jax.ShapeDtypeStruct((1, LANES), F32)],
        name="merge_gdn_bwd", compiler_params=_cp("arbitrary"))(o_gdn, proj, norm_w, dmerged)


def _fox_prep(proj, qk_w, nh):
    lp = proj.shape[0]

    def body(x_ref, w_ref, o_ref):
        x = x_ref[...]
        o_ref[...] = x * _rms(x) * w_ref[0]

    return pl.pallas_call(
        body, grid=(2 * nh,),
        in_specs=[pl.BlockSpec((lp, LANES), lambda s: (0, 4 * nh + s)), pl.BlockSpec((1, 1, LANES), lambda s: (s // nh, 0, 0))],
        out_specs=pl.BlockSpec((lp, LANES), lambda s: (0, s)),
        out_shape=jax.ShapeDtypeStruct((lp, 2 * nh * HEAD_DIM), F32), name="fox_prep",
        compiler_params=_cp("parallel"))(proj, qk_w)


def _fox_prep_bwd(proj, qk_w, dq, dk, dproj, nh):
    lp = proj.shape[0]
    part = lambda p: pl.BlockSpec((lp, LANES), lambda s: (0, jnp.clip(s - p * nh, 0, nh - 1)))

    def body(x_ref, w_ref, dq_ref, dk_ref, _, dx_ref, dw_ref):
        x = x_ref[...]
        r = _rms(x)
        xh = x * r
        dy = jnp.where(pl.program_id(0) < nh, dq_ref[...], dk_ref[...])
        dyw = dy * w_ref[0]
        dx_ref[...] = (r * (dyw - xh * jnp.mean(dyw * xh, axis=-1, keepdims=True))).astype(BF16)

        @pl.when(pl.program_id(0) % nh == 0)
        def _():
            dw_ref[...] = jnp.zeros_like(dw_ref)
        dw_ref[0] += jnp.sum(dy * xh, axis=0, keepdims=True)

    strip = pl.BlockSpec((lp, LANES), lambda s: (0, 4 * nh + s))
    wsp = pl.BlockSpec((1, 1, LANES), lambda s: (s // nh, 0, 0))
    return pl.pallas_call(
        body, grid=(2 * nh,), in_specs=[strip, wsp, part(0), part(1), _ANY], out_specs=[strip, wsp],
        out_shape=[jax.ShapeDtypeStruct(dproj.shape, BF16), jax.ShapeDtypeStruct((2, 1, LANES), F32)],
        input_output_aliases={4: 0}, name="fox_prep_bwd", compiler_params=_cp("arbitrary"))(proj, qk_w, dq, dk, dproj)


def _fox_probs(q, k, gates, crow, h, i, nh, lse=None):
    kl = k.shape[0]
    lane = _iota((Q_BLOCK, LANES), 1)
    ct = jnp.sum(jnp.where(lane == 4 * nh + h, gates, 0.0), axis=1, keepdims=True)
    tq, kq = _iota((Q_BLOCK, Q_BLOCK), 0), _iota((Q_BLOCK, Q_BLOCK), 1)
    qs = q * (HEAD_DIM ** -0.5)
    if i == 0:
        s = _bdot(qs, k, "nt") + (ct - crow)
        s = jnp.where((kq <= tq) & ((kq >= PAD_ROWS) | (tq < PAD_ROWS)), s, NEG)
    else:
        crow = jnp.where(_iota((1, kl), 1) < PAD_ROWS, -NEG, crow)
        s = _bdot(qs, k, "nt") + (ct - crow)
        s = jnp.concatenate([s[:, :kl - Q_BLOCK], jnp.where(kq <= tq, s[:, kl - Q_BLOCK:], NEG)], axis=1)
    if lse is not None:
        return jnp.exp(s - lse)
    m = jnp.max(s, axis=1, keepdims=True)
    p = jnp.exp(s - m)
    tot = jnp.sum(p, axis=1, keepdims=True)
    return p / tot, m + jnp.log(tot)


FOX_HEADS_PER_STEP = 2


def _fox_specs(lp, nh):
    hw = FOX_HEADS_PER_STEP * LANES
    return [pl.BlockSpec((Q_BLOCK, hw), lambda g, i: (i, g)),
            pl.BlockSpec((lp, hw), lambda g, i: (0, nh // FOX_HEADS_PER_STEP + g)),
            pl.BlockSpec((lp, hw), lambda g, i: (0, 6 * nh // FOX_HEADS_PER_STEP + g)),
            pl.BlockSpec((Q_BLOCK, LANES), lambda g, i: (i, 0)),
            pl.BlockSpec((LANES, lp), lambda g, i: (0, 0))]


def _fox_fwd(qkn, proj, gates, gtf, nh):
    lp = qkn.shape[0]

    def body(q_ref, k_ref, v_ref, g_ref, gt_ref, o_ref, lse_ref):
        g, i = pl.program_id(0), pl.program_id(1)
        for j in range(lp // Q_BLOCK):
            @pl.when(i == j)
            def _(j=j):
                kl = (j + 1) * Q_BLOCK
                for hh in range(FOX_HEADS_PER_STEP):
                    h = FOX_HEADS_PER_STEP * g + hh
                    sl = slice(hh * LANES, (hh + 1) * LANES)
                    p, lse = _fox_probs(q_ref[:, sl], k_ref[0:kl, sl], g_ref[...], gt_ref[pl.ds(4 * nh + h, 1), :][:, 0:kl],
                                        h, j, nh)
                    o_ref[:, sl] = _bdot(p, v_ref[0:kl, sl])
                    lse_ref[:, sl] = jnp.broadcast_to(lse, (Q_BLOCK, LANES))

    blk = pl.BlockSpec((Q_BLOCK, FOX_HEADS_PER_STEP * LANES), lambda g, i: (i, g))
    return pl.pallas_call(
        body, grid=(nh // FOX_HEADS_PER_STEP, lp // Q_BLOCK), in_specs=_fox_specs(lp, nh), out_specs=[blk, blk],
        out_shape=[jax.ShapeDtypeStruct((lp, nh * HEAD_DIM), F32)] * 2, name="fox_fwd",
        compiler_params=_cp("parallel", "parallel"))(qkn, qkn, proj, gates, gtf)


def _fox_bwd(qkn, proj, gates, gtf, lse, do, dproj, nh):
    lp = qkn.shape[0]
    nq = lp // Q_BLOCK
    w = nh * HEAD_DIM
    scale = HEAD_DIM ** -0.5

    def body(q_ref, k_ref, v_ref, g_ref, gt_ref, lse_ref, do_ref, _, dq_ref, dk_ref, dc_ref, dv_ref, dv_scr):
        g, i = pl.program_id(0), pl.program_id(1)

        @pl.when(i == 0)
        def _():
            dk_ref[...] = jnp.zeros_like(dk_ref)
            dv_scr[...] = jnp.zeros_like(dv_scr)
            dc_ref[...] = jnp.zeros_like(dc_ref)
        for j in range(nq):
            @pl.when(i == j)
            def _(j=j):
                kl = (j + 1) * Q_BLOCK
                for hh in range(FOX_HEADS_PER_STEP):
                    h = FOX_HEADS_PER_STEP * g + hh
                    sl = slice(hh * LANES, (hh + 1) * LANES)
                    q, k = q_ref[:, sl], k_ref[0:kl, sl]
                    p = _fox_probs(q, k, g_ref[...], gt_ref[pl.ds(4 * nh + h, 1), :][:, 0:kl], h, j, nh,
                                   lse_ref[:, sl][:, 0:1])
                    dout = do_ref[:, sl]
                    dp = _bdot(dout, v_ref[0:kl, sl], "nt")
                    ds = p * (dp - jnp.sum(p * dp, axis=1, keepdims=True))
                    dq_ref[:, sl] = _bdot(ds, k) * scale
                    dk_ref[0:kl, sl] += _bdot(ds, q * scale, "tn")
                    dv_scr[0:kl, sl] += _bdot(p, dout, "tn")
                    dc_ref[hh, :, 0:kl] -= jnp.sum(ds, axis=0, keepdims=True)

        @pl.when(i == nq - 1)
        def _():
            dv_ref[...] = dv_scr[...].astype(BF16)

    hw = FOX_HEADS_PER_STEP * LANES
    blk = pl.BlockSpec((Q_BLOCK, hw), lambda g, i: (i, g))
    col = pl.BlockSpec((lp, hw), lambda g, i: (0, g))
    return pl.pallas_call(
        body, grid=(nh // FOX_HEADS_PER_STEP, nq), in_specs=_fox_specs(lp, nh) + [blk, blk, _ANY],
        out_specs=[blk, col, pl.BlockSpec((FOX_HEADS_PER_STEP, 1, lp), lambda g, i: (g, 0, 0)),
                   pl.BlockSpec((lp, hw), lambda g, i: (0, 6 * nh // FOX_HEADS_PER_STEP + g))],
        out_shape=[jax.ShapeDtypeStruct((lp, w), F32)] * 2 + [jax.ShapeDtypeStruct((nh, 1, lp), F32),
                                                             jax.ShapeDtypeStruct(dproj.shape, BF16)],
        scratch_shapes=[pltpu.VMEM((lp, hw), F32)], input_output_aliases={7: 3},
        name="fox_bwd", compiler_params=_cp("parallel", "arbitrary"))(qkn, qkn, proj, gates, gtf, lse, do, dproj)


def _merge_fox(o_fox, proj, merged, nh):
    lp = o_fox.shape[0]

    def body(o_ref, z_ref, _, m_ref):
        z = z_ref[...]
        m_ref[...] = (o_ref[...] * (z * _sigmoid(z))).astype(BF16)

    return pl.pallas_call(
        body, grid=(nh,),
        in_specs=[pl.BlockSpec((lp, LANES), lambda s: (0, s)), pl.BlockSpec((lp, LANES), lambda s: (0, 7 * nh + s)), _ANY],
        out_specs=pl.BlockSpec((lp, LANES), lambda s: (0, nh + s)),
        out_shape=jax.ShapeDtypeStruct(merged.shape, BF16), input_output_aliases={2: 0}, name="merge_fox",
        compiler_params=_cp("parallel"))(o_fox, proj, merged)


def _merge_fox_bwd(o_fox, proj, dmerged, dproj, nh):
    lp = o_fox.shape[0]

    def body(o_ref, z_ref, dm_ref, _, do_ref, dz_ref):
        silu, dsilu = _silu_and_grad(z_ref[...])
        dm = dm_ref[...]
        do_ref[...] = dm * silu
        dz_ref[...] = (dm * o_ref[...] * dsilu).astype(BF16)

    w = nh * HEAD_DIM
    return pl.pallas_call(
        body, grid=(nh,),
        in_specs=[pl.BlockSpec((lp, LANES), lambda s: (0, s)), pl.BlockSpec((lp, LANES), lambda s: (0, 7 * nh + s)),
                  pl.BlockSpec((lp, LANES), lambda s: (0, nh + s)), _ANY],
        out_specs=[pl.BlockSpec((lp, LANES), lambda s: (0, s)), pl.BlockSpec((lp, LANES), lambda s: (0, 7 * nh + s))],
        out_shape=[jax.ShapeDtypeStruct((lp, w), F32), jax.ShapeDtypeStruct(dproj.shape, BF16)],
        input_output_aliases={3: 1}, name="merge_fox_bwd", compiler_params=_cp("parallel"))(o_fox, proj, dmerged, dproj)


def _post(out, x, target, post_w):
    lp, d = out.shape

    def body(o_ref, x_ref, t_ref, w_ref, dy_ref, do_ref, loss_ref, dw_ref):
        i = pl.program_id(0)

        @pl.when(i == 0)
        def _():
            loss_ref[...] = jnp.zeros_like(loss_ref)
            dw_ref[...] = jnp.zeros_like(dw_ref)
        o = o_ref[...]
        r = _rms(o)
        nrm = o * r
        err = jnp.where(i > 0, x_ref[...] + nrm * w_ref[...] - t_ref[...], 0.0)
        loss_ref[0:1, :] += 0.5 * jnp.sum(jnp.sum(err * err, axis=1, keepdims=True), axis=0, keepdims=True) / d
        dy = err / d
        dy_ref[...] = dy
        dw_ref[...] += jnp.sum(dy * nrm, axis=0, keepdims=True)
        dyw = dy * w_ref[...]
        do_ref[...] = (r * (dyw - nrm * jnp.mean(dyw * nrm, axis=-1, keepdims=True))).astype(BF16)

    row = pl.BlockSpec((Q_BLOCK, d), lambda i: (i, 0))
    vec = pl.BlockSpec((1, d), lambda i: (0, 0))
    return pl.pallas_call(
        body, grid=(lp // Q_BLOCK,), in_specs=[row, _x_rows(d), _x_rows(d), vec],
        out_specs=[_x_rows(d), row, pl.BlockSpec((8, LANES), lambda i: (0, 0)), vec],
        out_shape=[jax.ShapeDtypeStruct(x.shape, F32), jax.ShapeDtypeStruct((lp, d), BF16),
                   jax.ShapeDtypeStruct((8, LANES), F32), jax.ShapeDtypeStruct((1, d), F32)],
        name="post", compiler_params=_cp("arbitrary"))(out, x, target, post_w)


def _prenorm_bwd(dxn, x, meta, w, dy, rider=None):
    seq, d = x.shape
    lp = seq + Q_BLOCK

    def body(start, finish, dx_ref, x_ref, m_ref, w_ref, dy_ref, gx_ref, gm_ref, dw_ref):
        i = pl.program_id(0)
        pl.when(i == 0)(start)
        h = _h_tile(i, x_ref, m_ref)
        r = _rms(h)
        xh = h * r
        dxn_ = dx_ref[...]
        dxw = dxn_ * w_ref[...]
        dh = jnp.where(i > 0, dy_ref[...], 0.0) + r * (dxw - xh * jnp.mean(dxw * xh, axis=-1, keepdims=True))
        gx_ref[...] = dh

        @pl.when(i == 0)
        def _():
            dw_ref[...] = jnp.zeros_like(dw_ref)
            gm_ref[...] = dh[PAD_ROWS:, :]
        dw_ref[...] += jnp.sum(dxn_ * xh, axis=0, keepdims=True)
        pl.when(i == lp // Q_BLOCK - 1)(finish)

    vec = pl.BlockSpec((1, d), lambda i: (0, 0))
    met = pl.BlockSpec((N_META, d), lambda i: (0, 0))
    outs, got = _hosted_call(
        body, [rider], 5, 3, 0, grid=(lp // Q_BLOCK,),
        in_specs=[pl.BlockSpec((Q_BLOCK, d), lambda i: (i, 0)), _x_rows(d), met, vec, _x_rows(d)],
        out_specs=[_x_rows(d), met, vec],
        out_shape=[jax.ShapeDtypeStruct((seq, d), F32), jax.ShapeDtypeStruct((N_META, d), F32),
                   jax.ShapeDtypeStruct((1, d), F32)],
        name="prenorm_bwd", compiler_params=_cp("arbitrary"))(dxn, x, meta, w, dy)
    return outs, (got[0] if got else None)


def _layer_grads(x, target, meta, pre_w, wfull, conv_wt, a_log, dt_bias, gdn_norm_w, fq_w, fk_w, f_bias, w_out, post_w,
                 late_weights=None, w_out_grads=None):
    nh = a_log.shape[1]
    zpad = jnp.zeros((1, LANES - 3 * nh), F32)
    bias_row = jnp.concatenate([jnp.zeros((1, nh), F32), dt_bias, f_bias, zpad], axis=1)
    nega_row = jnp.concatenate([jnp.zeros((1, nh), F32), -jnp.exp(a_log), jnp.zeros((1, nh), F32), zpad], axis=1)
    qk_w = jnp.stack([fq_w, fk_w])

    if isinstance(wfull, tuple):
        xn, got = _prenorm(x, meta, pre_w, wfull[0])
        proj, wfull = wfull[1](xn, got)
    else:
        xn = _prenorm(x, meta, pre_w)
        proj = _matmul(xn, wfull, "nn", MM_TILE, F32, "proj")
    qkv = _gdn_prep(proj, conv_wt, nh)
    gates, gt3, gtf = _gates(proj, bias_row, nega_row, nh)
    (o_gdn, s_all, t_all), got = _gdn_fwd(qkv, gates, gt3, nh, None if late_weights is None else late_weights[0])
    if late_weights is not None:
        w_out = late_weights[1](got)
    qkn = _fox_prep(proj, qk_w, nh)
    o_fox, fox_lse = _fox_fwd(qkn, proj, gates, gtf, nh)
    merged = _merge_fox(o_fox, proj, _merge_gdn(o_gdn, proj, gdn_norm_w, nh), nh)
    out = _matmul(merged, w_out, "nn", 4 * LANES, F32, "out_proj")
    dy, dout, loss_blk, dpost_w = _post(out, x, target, post_w)

    dw_out = _matmul(merged, dout, "tn", 4 * LANES, BF16, "dw_out")
    if w_out_grads is None:
        dmerged, gdn_rider = _matmul(dout, w_out, "nt", 4 * LANES, F32, "dmerged"), None
    else:
        dmerged, got = _matmul(dout, w_out, "nt", 4 * LANES, F32, "dmerged", w_out_grads[0](dw_out))
        gdn_rider = w_out_grads[1](dw_out, got)
    do_gdn, dproj, dgdn_norm_w = _merge_gdn_bwd(o_gdn, proj, gdn_norm_w, dmerged, nh)
    do_fox, dproj = _merge_fox_bwd(o_fox, proj, dmerged, dproj, nh)
    dqn, dkn, dc_t, dproj = _fox_bwd(qkn, proj, gates, gtf, fox_lse, do_fox, dproj, nh)
    dproj, dqk_w = _fox_prep_bwd(proj, qk_w, dqn, dkn, dproj, nh)
    (dgq, dgk, dgv, dgate), w_out_parts = _gdn_bwd(qkv, gates, gt3, s_all, t_all, do_gdn, nh, gdn_rider)
    dproj, dconv_wt = _gdn_prep_bwd(proj, conv_wt, dgq, dgk, dgv, dproj, nh)
    dc_rows = jnp.pad(dc_t.reshape(nh, -1), ((2 * nh, LANES - 3 * nh), (0, 0)))
    dproj, gate_sums = _gates_bwd(proj, bias_row, nega_row, gates, dgate, dc_rows, dproj, nh)
    return dict(
        loss=loss_blk[0:1, 0:1], dy=dy, xn=xn, dproj=dproj, post_w=dpost_w,
        conv_wt=dconv_wt, a_log=gate_sums[1:2, nh:2 * nh], dt_bias=gate_sums[0:1, nh:2 * nh],
        gdn_norm_w=dgdn_norm_w, fq_w=dqk_w[0], fk_w=dqk_w[1], f_bias=gate_sums[0:1, 2 * nh:3 * nh], w_out=dw_out,
        w_out_parts=w_out_parts, wfull=wfull)


def _cast_bf16(a, tr, name):
    r, c = a.shape

    def body(a_ref, o_ref):
        o_ref[...] = a_ref[...].astype(BF16)

    return pl.pallas_call(
        body, grid=(r // tr,), in_specs=[pl.BlockSpec((tr, c), lambda i: (i, 0))],
        out_specs=pl.BlockSpec((tr, c), lambda i: (i, 0)), out_shape=jax.ShapeDtypeStruct((r, c), BF16),
        name=name, compiler_params=_cp("parallel"))(a)


def _column_major(a):
    return jnp.transpose(a, (2, 0, 1))


def _cast_bf16_column_major(a3, pieces, name):
    _, r, c = a3.shape
    rows = r // pieces

    def body(a_ref, *o_refs):
        t = a_ref[...].reshape(LANES, r).T.astype(BF16)
        for k, o_ref in enumerate(o_refs):
            o_ref[...] = t[k * rows:(k + 1) * rows]

    return pl.pallas_call(
        body, grid=(pl.cdiv(c, LANES),), in_specs=[pl.BlockSpec((LANES, 1, r), lambda i: (i, 0, 0))],
        out_specs=[pl.BlockSpec((rows, LANES), lambda i: (0, i))] * pieces,
        out_shape=[jax.ShapeDtypeStruct((rows, c), BF16)] * pieces, name=name, compiler_params=_cp("parallel"))(_column_major(a3))


def _adamw_column_major(w3, parts, m3, v3, name):
    _, r, c = w3.shape
    n_parts = parts[0].shape[0]

    def body(w_ref, *refs):
        p_refs, (m_ref, v_ref, g_ref, d_ref, nm_ref, nv_ref) = refs[:len(parts)], refs[len(parts):]
        sums = []
        for p_ref in p_refs:
            g = p_ref[0].astype(F32)
            for s in range(1, n_parts):
                g = g + p_ref[s].astype(F32)
            sums.append(g)
        g = jnp.concatenate(sums, axis=0).T
        flat = lambda ref: ref[...].reshape(LANES, r)
        m_new = ADAM_B1 * flat(m_ref) + (1.0 - ADAM_B1) * g
        v_new = ADAM_B2 * flat(v_ref) + (1.0 - ADAM_B2) * (g * g)
        m_hat = m_new / (1.0 - ADAM_B1 ** ADAM_STEP)
        v_hat = v_new / (1.0 - ADAM_B2 ** ADAM_STEP)
        delta = -ADAM_LR * (m_hat / (jnp.sqrt(v_hat) + ADAM_EPS) + ADAM_WD * flat(w_ref))
        for ref, val in ((g_ref, g), (d_ref, delta), (nm_ref, m_new), (nv_ref, v_new)):
            ref[...] = val.reshape(LANES, 1, r)

    blk = pl.BlockSpec((LANES, 1, r), lambda i: (i, 0, 0))
    outs = pl.pallas_call(
        body, grid=(pl.cdiv(c, LANES),),
        in_specs=[blk] + [pl.BlockSpec((n_parts, p.shape[1], LANES), lambda i: (0, 0, i)) for p in parts] + [blk, blk],
        out_specs=[blk] * 4, out_shape=[jax.ShapeDtypeStruct((c, 1, r), F32)] * 4, name=name,
        compiler_params=_cp("parallel"))(_column_major(w3), *parts, _column_major(m3), _column_major(v3))
    return [jnp.transpose(o, (1, 2, 0)) for o in outs]


def _gather_copies(ins, outs, send_sems, recv_sems, local_sems):
    n = len(ins)
    x, y, c = lax.axis_index("x"), lax.axis_index("y"), lax.axis_index("c")
    me, sibling = (x, y, c), (x, y, 1 - c)
    xn, yn, dg = (1 - x, y), (x, 1 - y), (1 - x, 1 - y)

    def copy(a, k, block, to, src=None):
        px, py, pc = block
        rows = outs[a].at[4 * px + 2 * py + pc]
        return pltpu.make_async_remote_copy(
            src_ref=rows if src is None else src, dst_ref=rows, send_sem=send_sems.at[a, k],
            recv_sem=recv_sems.at[a, k], device_id=to, device_id_type=_MESH)

    local = [pltpu.make_async_copy(ins[a], outs[a].at[4 * x + 2 * y + c], local_sems.at[a]) for a in range(n)]
    own = [cp for a in range(n) for cp in (copy(a, 0, me, sibling, src=ins[a]), copy(a, 1, me, (*xn, c), src=ins[a]),
                                           copy(a, 2, me, (*yn, c), src=ins[a]))]

    def start():
        for cp in local + own:
            cp.start()

    def finish():
        for a in range(n):
            @pl.when(c == 1)
            def _(a=a):
                copy(a, 1, (*xn, c), me).wait_recv()
                copy(a, 3, (*xn, c), (*yn, c)).start()

            @pl.when(c == 0)
            def _(a=a):
                copy(a, 2, (*yn, c), me).wait_recv()
                copy(a, 3, (*yn, c), (*xn, c)).start()
        for a in range(n):
            pl.when(c == 0)(copy(a, 1, (*xn, c), me).wait_recv)
            copy(a, 4, (*xn, c), sibling).start()
            pl.when(c == 1)(copy(a, 2, (*yn, c), me).wait_recv)
            copy(a, 5, (*yn, c), sibling).start()
        for a in range(n):
            copy(a, 3, (*dg, c), me).wait_recv()
            copy(a, 6, (*dg, c), sibling).start()
        for a in range(n):
            copy(a, 0, sibling, me).wait_recv()
            for k, chip in ((4, xn), (5, yn), (6, dg)):
                copy(a, k, (*chip, 1 - c), me).wait_recv()
                copy(a, k, (*chip, c), sibling).wait_send()
            copy(a, 3, (*xn, c), (*yn, c)).wait_send()
        for cp in own:
            cp.wait_send()
        for cp in local:
            cp.wait()

    return start, finish


def _gather_scratch(n):
    return [pltpu.SemaphoreType.DMA((n, N_DEV - 1)), pltpu.SemaphoreType.DMA((n, N_DEV - 1)), pltpu.SemaphoreType.DMA((n,))]


def _gather_rider(arrays):
    return _Rider(list(arrays), [jax.ShapeDtypeStruct((N_DEV,) + a.shape, a.dtype) for a in arrays],
                  _gather_scratch(len(arrays)), {}, lambda ins, outs, scratch: _gather_copies(ins, outs, *scratch))


def _all_gather(arrays, name):
    n = len(arrays)

    def body(*refs):
        start, finish = _gather_copies(refs[:n], refs[n:2 * n], *refs[2 * n:])
        start()
        finish()

    return pl.pallas_call(
        body, in_specs=[_ANY] * n, out_specs=[_ANY] * n,
        out_shape=[jax.ShapeDtypeStruct((N_DEV,) + a.shape, a.dtype) for a in arrays],
        scratch_shapes=_gather_scratch(n), name=name)(*arrays)


SLAB = 10 * LANES


def _slab_start(blk, nh, cols):
    in_second_half = blk >= N_DEV // 2
    shift = (2 * nh if in_second_half else 0) if isinstance(blk, int) else jnp.where(in_second_half, 2 * nh, 0)
    return (blk * cols - shift) // LANES * LANES


def _pair_rider(dw_rows=None, parts=None, nh=None, after=None):
    if dw_rows is not None:
        r, full = dw_rows.shape
        cols = (full - NARROW + 3 * nh) // N_DEV
        out_shapes = [jax.ShapeDtypeStruct((N_CHIP, r, SLAB), dw_rows.dtype), jax.ShapeDtypeStruct((r, NARROW), dw_rows.dtype)]
    else:
        out_shapes = [jax.ShapeDtypeStruct((N_CHIP,) + parts.shape[1:], parts.dtype)]

    def make(ins, outs, scratch):
        send_sems, recv_sems = scratch
        x, y, c = lax.axis_index("x"), lax.axis_index("y"), lax.axis_index("c")
        kw = lambda k: dict(send_sem=send_sems.at[k], recv_sem=recv_sems.at[k], device_id=(x, y, 1 - c), device_id_type=_MESH)
        copies = []
        for q in range(N_CHIP):
            if dw_rows is not None:
                first = pl.multiple_of(_slab_start(2 * q + 1 - c, nh, cols), LANES)
                copies.append(pltpu.make_async_remote_copy(src_ref=ins[0].at[:, pl.ds(first, SLAB)], dst_ref=outs[0].at[q], **kw(q)))
            else:
                copies.append(pltpu.make_async_remote_copy(src_ref=ins[0].at[2 * q + 1 - c], dst_ref=outs[0].at[q], **kw(q)))
        if dw_rows is not None:
            copies.append(pltpu.make_async_remote_copy(src_ref=ins[0].at[:, pl.ds(full - NARROW, NARROW)], dst_ref=outs[1],
                                                       **kw(N_CHIP)))

        def start():
            for cp in copies:
                cp.start()

        def finish():
            for cp in copies:
                cp.wait()

        return start, finish

    return _Rider([dw_rows if dw_rows is not None else parts] + ([] if after is None else [after]), out_shapes,
                  [pltpu.SemaphoreType.DMA((N_CHIP + 1,)), pltpu.SemaphoreType.DMA((N_CHIP + 1,))], {}, make)


def _relayout_pair_sum(dwfull, got_slabs, got_tail, core, nh, tr, name):
    d, full = dwfull.shape
    w = nh * HEAD_DIM
    cols = (8 * w + 3 * nh) // N_DEV
    segs = _native_segments(nh)

    def block(f_ref, s_ref, t_ref, q, blk):
        st = _slab_start(blk, nh, cols)
        wide = f_ref[:, st:st + SLAB].astype(F32) + s_ref[q].astype(F32)
        tail = f_ref[:, 8 * w:].astype(F32) + t_ref[...].astype(F32)
        pieces = []
        for s0, s1, t0 in segs:
            lo, hi = max(s0, blk * cols), min(s1, (blk + 1) * cols)
            if lo < hi:
                at = t0 + lo - s0
                pieces.append(tail[:, at - 8 * w:at - 8 * w + hi - lo] if at >= 8 * w else wide[:, at - st:at - st + hi - lo])
        return (pieces[0] if len(pieces) == 1 else jnp.concatenate(pieces, axis=1)).astype(dwfull.dtype)

    def body(core_ref, f_ref, s_ref, t_ref, o_ref):
        for parity in range(2):
            @pl.when(core_ref[0] == parity)
            def _(parity=parity):
                for q in range(N_CHIP):
                    o_ref[q] = block(f_ref, s_ref, t_ref, q, 2 * q + parity)

    return pl.pallas_call(
        body,
        grid_spec=pltpu.PrefetchScalarGridSpec(
            num_scalar_prefetch=1, grid=(d // tr,),
            in_specs=[pl.BlockSpec((tr, full), lambda i, c_ref: (i, 0)), pl.BlockSpec((N_CHIP, tr, SLAB), lambda i, c_ref: (0, i, 0)),
                      pl.BlockSpec((tr, NARROW), lambda i, c_ref: (i, 0))],
            out_specs=pl.BlockSpec((N_CHIP, tr, cols), lambda i, c_ref: (0, i, 0))),
        out_shape=jax.ShapeDtypeStruct((N_CHIP, d, cols), dwfull.dtype), name=name,
        compiler_params=_cp("parallel"))(core, dwfull, got_slabs, got_tail)


def _pair_sum(parts, got, core, tr, name):
    _, r, c = parts.shape

    def body(core_ref, p_ref, g_ref, o_ref):
        o_ref[...] = (p_ref[...].astype(F32) + g_ref[...].astype(F32)).astype(o_ref.dtype)

    return pl.pallas_call(
        body,
        grid_spec=pltpu.PrefetchScalarGridSpec(
            num_scalar_prefetch=1, grid=(N_CHIP, r // tr),
            in_specs=[pl.BlockSpec((1, tr, c), lambda q, i, core_ref: (2 * q + core_ref[0], i, 0)),
                      pl.BlockSpec((1, tr, c), lambda q, i, core_ref: (q, i, 0))],
            out_specs=pl.BlockSpec((1, tr, c), lambda q, i, core_ref: (q, i, 0))),
        out_shape=jax.ShapeDtypeStruct((N_CHIP, r, c), parts.dtype), name=name,
        compiler_params=_cp("parallel", "parallel"))(core, parts, got)


def _native_segments(nh):
    w = nh * HEAD_DIM
    return [(0, 4 * w, 0), (4 * w, 4 * w + 2 * nh, 8 * w), (4 * w + 2 * nh, 8 * w + 2 * nh, 4 * w),
            (8 * w + 2 * nh, 8 * w + 3 * nh, 8 * w + 2 * nh)]


def _relayout_w_in(wg, nh, tr, name, rows_total=None, into=None):
    _, d, cols = wg.shape
    w = nh * HEAD_DIM
    rows_total = into.shape[0] if into is not None else rows_total or d
    first = (rows_total - d) // tr if into is not None else 0

    def native(ref, j0, j1):
        out = []
        while j0 < j1:
            blk = j0 // cols
            end = min(j1, (blk + 1) * cols)
            out.append(ref[blk, :, pl.ds(j0 - blk * cols, end - j0)])
            j0 = end
        return out

    def body(g_ref, *refs):
        o_ref = refs[-1]
        for cidx in range(8 * w // LANES):
            j0 = cidx * LANES + (0 if cidx * LANES < 4 * w else 2 * nh)
            pieces = native(g_ref, j0, j0 + LANES)
            o_ref[:, cidx * LANES:(cidx + 1) * LANES] = pieces[0] if len(pieces) == 1 else jnp.concatenate(pieces, axis=1)
        pieces = (native(g_ref, 4 * w, 4 * w + 2 * nh) + native(g_ref, 8 * w + 2 * nh, 8 * w + 3 * nh)
                  + [jnp.zeros((tr, NARROW - 3 * nh), wg.dtype)])
        o_ref[:, 8 * w:] = jnp.concatenate(pieces, axis=1)

    return pl.pallas_call(
        body, grid=(d // tr,), in_specs=[pl.BlockSpec((N_DEV, tr, cols), lambda i: (0, i, 0))] + [_ANY] * (into is not None),
        out_specs=pl.BlockSpec((tr, 8 * w + NARROW), lambda i: (first + i, 0)),
        out_shape=jax.ShapeDtypeStruct((rows_total, 8 * w + NARROW), wg.dtype),
        input_output_aliases={1: 0} if into is not None else {},
        name=name, compiler_params=_cp("parallel"))(wg, *([into] if into is not None else []))


def _adamw(w, parts, m, v, tr, name, rider=None):
    r, c = w.shape
    n_parts = parts.shape[0]
    steps = r // tr

    def body(start, finish, w_ref, p_ref, m_ref, v_ref, g_ref, d_ref, nm_ref, nv_ref):
        pl.when(pl.program_id(0) == 0)(start)
        g = p_ref[0].astype(F32)
        for s in range(1, n_parts):
            g = g + p_ref[s].astype(F32)
        m_new = ADAM_B1 * m_ref[...] + (1.0 - ADAM_B1) * g
        v_new = ADAM_B2 * v_ref[...] + (1.0 - ADAM_B2) * (g * g)
        m_hat = m_new / (1.0 - ADAM_B1 ** ADAM_STEP)
        v_hat = v_new / (1.0 - ADAM_B2 ** ADAM_STEP)
        g_ref[...] = g
        d_ref[...] = -ADAM_LR * (m_hat / (jnp.sqrt(v_hat) + ADAM_EPS) + ADAM_WD * w_ref[...])
        nm_ref[...] = m_new
        nv_ref[...] = v_new
        pl.when(pl.program_id(0) == steps - 1)(finish)

    blk = pl.BlockSpec((tr, c), lambda i: (i, 0))
    outs, got = _hosted_call(
        body, [rider], 4, 4, 0, grid=(steps,), in_specs=[blk, pl.BlockSpec((n_parts, tr, c), lambda i: (0, i, 0)), blk, blk],
        out_specs=[blk] * 4, out_shape=[jax.ShapeDtypeStruct((r, c), F32)] * 4, name=name,
        compiler_params=_cp("parallel" if rider is None else "arbitrary"))(w, parts, m, v)
    return outs if rider is None else (outs, got[0])


def _pack_small(d, pre, post, a_log, dt_bias, f_bias, gdn_w, fq_w, fk_w, extra):
    row2 = jnp.concatenate([a_log, dt_bias, f_bias, gdn_w, fq_w, fk_w, extra], axis=1)
    row2 = jnp.pad(row2, ((0, 0), (0, d - row2.shape[1])))
    return jnp.concatenate([pre, post, row2, jnp.zeros((5, d), F32)], axis=0)


def _unpack_small(p, nh):
    o = 3 * nh
    return dict(pre=p[0:1], post=p[1:2], a_log=p[2:3, 0:nh], dt_bias=p[2:3, nh:2 * nh], f_bias=p[2:3, 2 * nh:o],
                gdn_w=p[2:3, o:o + HEAD_DIM], fq_w=p[2:3, o + HEAD_DIM:o + 2 * HEAD_DIM],
                fk_w=p[2:3, o + 2 * HEAD_DIM:o + 3 * HEAD_DIM], extra=p[2, o + 3 * HEAD_DIM])


def kernel(x, meta_tokens, pre_norm_w, w_in, conv_w, a_log, dt_bias, gdn_norm_w, fox_q_norm_w, fox_k_norm_w, fox_f_bias, w_out, post_norm_w, loss_target, m_meta_tokens, m_pre_norm_w, m_w_in, m_conv_w, m_a_log, m_dt_bias, m_gdn_norm_w, m_fox_q_norm_w, m_fox_k_norm_w, m_fox_f_bias, m_w_out, m_post_norm_w, v_meta_tokens, v_pre_norm_w, v_w_in, v_conv_w, v_a_log, v_dt_bias, v_gdn_norm_w, v_fox_q_norm_w, v_fox_k_norm_w, v_fox_f_bias, v_w_out, v_post_norm_w):
    nh = a_log.shape[1]
    d = x.shape[-1]
    w = nh * HEAD_DIM
    zero = jnp.zeros((1, 1), F32)

    w_in_a, w_in_b = _cast_bf16_column_major(w_in, 2, "cast_w_in")
    cg, mg = _all_gather([conv_w[0].T, meta_tokens], "gather_small_weights")

    def project(xn, got):
        half = (d // 2, 0), (d // 2, 1)
        wfull = _relayout_w_in(got[0], nh, 256, "relayout_w_in_a", rows_total=d)
        proj, got = _matmul(xn, wfull, "nn", MM_TILE, F32, "proj_a", _gather_rider([w_in_b]), a_cols=half[0], b_rows=half[0])
        wfull = _relayout_w_in(got[0], nh, 256, "relayout_w_in_b", into=wfull)
        return _matmul(xn, wfull, "nn", MM_TILE, F32, "proj_b", a_cols=half[1], b_rows=half[1], acc=proj), wfull

    conv_wt = cg.transpose(1, 0, 2).reshape(CONV_WIDTH, 3 * w)
    meta_full = mg.transpose(1, 0, 2).reshape(N_META, d)
    late_weights = (_gather_rider([_cast_bf16(w_out[0], 256, "cast_w_out")]), lambda got: got[0].reshape(2 * w, d))
    core = lax.axis_index("c")
    dev = 4 * lax.axis_index("x") + 2 * lax.axis_index("y") + core
    core_arr = jnp.reshape(core, (1,)).astype(jnp.int32)

    out_parts = lambda dw_out: dw_out.reshape(N_DEV, 2 * w // N_DEV, d)
    g = _layer_grads(
        x[0], loss_target[0], meta_full, pre_norm_w, (_gather_rider([w_in_a]), project), conv_wt, a_log, dt_bias, gdn_norm_w,
        fox_q_norm_w, fox_k_norm_w, fox_f_bias, None, post_norm_w, late_weights=late_weights,
        w_out_grads=(lambda dw_out: _pair_rider(parts=out_parts(dw_out)),
                     lambda dw_out, got: _chip_rider(_pair_sum(out_parts(dw_out), got[0], core_arr, 256, "pair_sum_w_out"))))
    p_out = g["w_out_parts"][0]
    xn, dproj, wfull = g["xn"], g["dproj"], g["wfull"]
    flights, token, dw, rider = [], None, None, None

    def exchange(dw, got, i):
        sums = _relayout_pair_sum(dw, got[0], got[1], core_arr, nh, 128, f"relayout_pair_sum_{i}")
        flight, token = _chip_exchange_start(sums, f"chip_exchange_start_{i}")
        flights.append(flight)
        return token

    for i, (index, parts) in enumerate(DW_IN_PIECES):
        res = _matmul(xn, dproj, "tn", MM_TILE, BF16, f"dw_in_{i}", rider, a_cols=(d // parts, index))
        if i:
            token = exchange(dw, res[1], i - 1)
        dw = res[0] if i else res
        rider = _pair_rider(dw_rows=dw, nh=nh, after=token)
    dxn, (got,) = _dxn(dproj, wfull, [rider], MM_TILE, "dxn")
    token = exchange(dw, got, len(DW_IN_PIECES) - 1)
    (grad_x, dmeta, dpre_w), _ = _prenorm_bwd(dxn, x[0], meta_full, pre_norm_w + token[0:1, 0:1], g["dy"])
    small = _pack_small(d, dpre_w, g["post_w"], g["a_log"], g["dt_bias"], g["f_bias"], g["gdn_norm_w"], g["fq_w"],
                        g["fk_w"], g["loss"])
    r_out, (a_conv, a_meta, p_small) = _adamw(w_out[0], p_out, m_w_out[0], v_w_out[0], 64, "adamw_w_out",
                                              _gather_rider([g["conv_wt"], dmeta, small]))
    p_conv = lax.dynamic_slice_in_dim(a_conv, dev * conv_w.shape[1], conv_w.shape[1], axis=2).transpose(0, 2, 1)
    p_meta = lax.dynamic_slice_in_dim(a_meta, dev * meta_tokens.shape[1], meta_tokens.shape[1], axis=2)

    r_conv = _adamw(conv_w[0], p_conv, m_conv_w[0], v_conv_w[0], conv_w.shape[1], "adamw_conv_w")
    r_meta = _adamw(meta_tokens, p_meta, m_meta_tokens, v_meta_tokens, N_META, "adamw_meta")
    pk = lambda pre, post, a, dt, gw, fq, fk, fb: _pack_small(d, pre, post, a, dt, fb, gw, fq, fk, zero)
    r_small = _adamw(
        pk(pre_norm_w, post_norm_w, a_log, dt_bias, gdn_norm_w, fox_q_norm_w, fox_k_norm_w, fox_f_bias), p_small,
        pk(m_pre_norm_w, m_post_norm_w, m_a_log, m_dt_bias, m_gdn_norm_w, m_fox_q_norm_w, m_fox_k_norm_w, m_fox_f_bias),
        pk(v_pre_norm_w, v_post_norm_w, v_a_log, v_dt_bias, v_gdn_norm_w, v_fox_q_norm_w, v_fox_k_norm_w, v_fox_f_bias),
        8, "adamw_small")
    p_in = _chip_exchange_wait(flights, r_small[0], "chip_exchange_wait")
    r_in = _adamw_column_major(w_in, p_in, m_w_in, v_w_in, "adamw_w_in")

    sm = [_unpack_small(r, nh) for r in r_small]
    outs = []
    for i in range(4):
        s = sm[i]
        outs += [r_meta[i], s["pre"], r_in[i], r_conv[i][None], s["a_log"], s["dt_bias"], s["gdn_w"], s["fq_w"],
                 s["fk_w"], s["f_bias"], r_out[i][None], s["post"]]
    return (sm[0]["extra"], grad_x[None], *outs)
```

```python
import jax
import jax.numpy as jnp
from jax import lax
from jax.experimental import pallas as pl
from jax.experimental.pallas import tpu as pltpu

F32, BF16 = jnp.float32, jnp.bfloat16
HEAD_DIM = 128
N_META = 16
CONV_WIDTH = 4
CHUNK = 128
Q_BLOCK = 128
LANES = 128
EPS = 1e-6
PAD_ROWS = Q_BLOCK - N_META
N_DEV = 8
N_CHIP = 4
VMEM_LIMIT = 56 * 1024 * 1024
NEG = -1e30
NARROW = 2 * LANES
MM_TILE = 6 * LANES
DW_IN_PIECES = ((0, 2), (2, 4), (3, 4))

ADAM_LR, ADAM_B1, ADAM_B2, ADAM_EPS, ADAM_WD, ADAM_STEP = 0.001, 0.9, 0.999, 1e-08, 0.01, 10

_DN = {"nn": (((1,), (0,)), ((), ())), "nt": (((1,), (1,)), ((), ())), "tn": (((0,), (0,)), ((), ()))}
_DN3 = {"nn": (((2,), (1,)), ((0,), (0,))), "nt": (((2,), (2,)), ((0,), (0,))), "tn": (((1,), (1,)), ((0,), (0,)))}
_ANY = pl.BlockSpec(memory_space=pl.ANY)
_MESH = pl.DeviceIdType.MESH


def _cp(*sem):
    return pltpu.CompilerParams(dimension_semantics=sem, vmem_limit_bytes=VMEM_LIMIT)


def _dot(a, b, dims="nn", prec=None):
    return lax.dot_general(a, b, _DN[dims], precision=prec, preferred_element_type=F32)


def _bdot(a, b, dims="nn"):
    return _dot(a.astype(BF16), b.astype(BF16), dims)


def _hdot(a, b, dims="nn"):
    return _dot(a, b, dims, prec=lax.Precision.HIGHEST)


def _dot3(a, b, dims="nn"):
    return lax.dot_general(a, b, _DN3[dims], preferred_element_type=F32)


def _bdot3(a, b, dims="nn"):
    return _dot3(a.astype(BF16), b.astype(BF16), dims)


def _split(a):
    hi = a.astype(BF16)
    return hi, (a - hi.astype(F32)).astype(BF16)


def _iota(shape, dim):
    return lax.broadcasted_iota(jnp.int32, shape, dim)


def _sigmoid(z):
    return 1.0 / (1.0 + jnp.exp(-z))


def _softplus(z):
    e = jnp.exp(-jnp.abs(z))
    u = 1.0 + e
    l1p = jnp.where(u == 1.0, e, jnp.log(u) * (e / jnp.where(u == 1.0, 1.0, u - 1.0)))
    return jnp.maximum(z, 0.0) + l1p


def _silu_and_grad(z):
    s = _sigmoid(z)
    return z * s, s * (1.0 + z * (1.0 - s))


def _rms(x):
    return lax.rsqrt(jnp.mean(x * x, axis=-1, keepdims=True) + EPS)


def _h_tile(i, x_ref, meta_ref):
    first = jnp.concatenate([jnp.zeros((PAD_ROWS, x_ref.shape[1]), F32), meta_ref[...]], axis=0)
    return jnp.where(i == 0, first, x_ref[...])


def _x_rows(d):
    return pl.BlockSpec((Q_BLOCK, d), lambda i: (jnp.maximum(i - 1, 0), 0))


def _prenorm(x, meta, w):
    seq, d = x.shape
    lp = seq + Q_BLOCK

    def body(x_ref, m_ref, w_ref, o_ref):
        h = _h_tile(pl.program_id(0), x_ref, m_ref)
        o_ref[...] = (h * _rms(h) * w_ref[...]).astype(BF16)

    return pl.pallas_call(
        body, grid=(lp // Q_BLOCK,),
        in_specs=[_x_rows(d), pl.BlockSpec((N_META, d), lambda i: (0, 0)), pl.BlockSpec((1, d), lambda i: (0, 0))],
        out_specs=pl.BlockSpec((Q_BLOCK, d), lambda i: (i, 0)),
        out_shape=jax.ShapeDtypeStruct((lp, d), BF16), name="prenorm", compiler_params=_cp("parallel"))(x, meta, w)


def _prenorm_gathering(x, w, rider, meta_at):
    seq, d = x.shape
    steps = seq // Q_BLOCK + 1

    def body(start, finish, x_ref, w_ref, o_ref, meta_buf, meta_sem):
        i = pl.program_id(0)
        pl.when(i == 0)(start)

        @pl.when(i < steps - 1)
        def _():
            h = x_ref[...]
            o_ref[...] = (h * _rms(h) * w_ref[...]).astype(BF16)

        @pl.when(i == steps - 1)
        def _():
            finish()
            cp = pltpu.make_async_copy(finish.results[0][meta_at], meta_buf, meta_sem)
            cp.start()
            cp.wait()
            h = jnp.concatenate([jnp.zeros((PAD_ROWS, d), F32), jnp.concatenate([meta_buf[s] for s in range(N_DEV)], axis=1)], axis=0)
            o_ref[...] = (h * _rms(h) * w_ref[...]).astype(BF16)

    (xn,), got = _hosted_call(
        body, [rider], 2, 1, 2, grid=(steps,),
        in_specs=[pl.BlockSpec((Q_BLOCK, d), lambda i: (jnp.minimum(i, steps - 2), 0)), pl.BlockSpec((1, d), lambda i: (0, 0))],
        out_specs=[pl.BlockSpec((Q_BLOCK, d), lambda i: ((i + 1) % steps, 0))],
        out_shape=[jax.ShapeDtypeStruct((seq + Q_BLOCK, d), BF16)],
        scratch_shapes=[pltpu.VMEM((N_DEV, N_META, d // N_DEV), F32), pltpu.SemaphoreType.DMA(())],
        name="prenorm", compiler_params=_cp("arbitrary"))(x, w)
    return xn, got[0]


def _tile(n, want):
    return max(t for t in range(LANES, want + 1, LANES) if n % t == 0)


class _Rider:
    def __init__(self, inputs, out_shapes, scratch, aliases, make):
        self.inputs, self.out_shapes, self.scratch, self.aliases, self.make = inputs, out_shapes, scratch, aliases, make


def _hosted_call(body, riders, n_in, n_out, n_scratch, *, in_specs, out_specs, out_shape, scratch_shapes=(), aliases=None,
                 **kw):
    riders = [r for r in riders if r is not None]
    r_in = [len(r.inputs) for r in riders]
    r_out = [len(r.out_shapes) for r in riders]
    r_scr = [len(r.scratch) for r in riders]
    al = dict(aliases or {})
    for k, r in enumerate(riders):
        al.update({n_in + sum(r_in[:k]) + i: n_out + sum(r_out[:k]) + o for i, o in r.aliases.items()})

    def full_body(*refs):
        ins, rest = refs[:n_in + sum(r_in)], refs[n_in + sum(r_in):]
        outs, scr = rest[:n_out + sum(r_out)], rest[n_out + sum(r_out):]
        hooks = [r.make(ins[n_in + sum(r_in[:k]):n_in + sum(r_in[:k + 1])], outs[n_out + sum(r_out[:k]):n_out + sum(r_out[:k + 1])],
                        scr[n_scratch + sum(r_scr[:k]):n_scratch + sum(r_scr[:k + 1])]) for k, r in enumerate(riders)]

        def start():
            for h in hooks:
                h[0]()

        def finish():
            for h in hooks:
                h[1]()

        finish.results = [outs[n_out + sum(r_out[:k]):n_out + sum(r_out[:k + 1])] for k in range(len(riders))]
        body(start, finish, *ins[:n_in], *outs[:n_out], *scr[:n_scratch])

    call = pl.pallas_call(
        full_body, in_specs=list(in_specs) + [_ANY] * sum(r_in), out_specs=list(out_specs) + [_ANY] * sum(r_out),
        out_shape=list(out_shape) + [s for r in riders for s in r.out_shapes],
        scratch_shapes=list(scratch_shapes) + [s for r in riders for s in r.scratch], input_output_aliases=al, **kw)

    def run(*args):
        res = call(*args, *[t for r in riders for t in r.inputs])
        return res[:n_out], [res[n_out + sum(r_out[:k]):n_out + sum(r_out[:k + 1])] for k in range(len(riders))]

    return run


def _matmul(a, b, dims, tn, out_dtype, name, rider=None, a_cols=None, b_rows=None, acc=None):
    a_shape = a.shape if a_cols is None else (a.shape[0], a_cols[0])
    a_index = 0 if a_cols is None else a_cols[1]
    m = a_shape[1] if dims == "tn" else a_shape[0]
    n = b.shape[0] if dims == "nt" else b.shape[1]
    kdim = b.shape[1] if dims == "nt" else b.shape[0] if b_rows is None else b_rows[0]
    b_index = 0 if b_rows is None else b_rows[1]
    tn = _tile(n, tn)
    steps = n // tn
    b_spec = pl.BlockSpec((tn, kdim), lambda j: (j, 0)) if dims == "nt" else pl.BlockSpec((kdim, tn), lambda j: (b_index, j))
    o_spec = pl.BlockSpec((m, tn), lambda j: (0, j))

    def body(start, finish, a_ref, b_ref, *refs):
        pl.when(pl.program_id(0) == 0)(start)
        prod = _dot(a_ref[...], b_ref[...], dims)
        refs[-1][...] = (prod if acc is None else prod + refs[0][...]).astype(out_dtype)
        pl.when(pl.program_id(0) == steps - 1)(finish)

    (out,), got = _hosted_call(
        body, [rider], 2 + (acc is not None), 1, 0, grid=(steps,),
        in_specs=[pl.BlockSpec(a_shape, lambda j: (0, a_index)), b_spec] + [o_spec] * (acc is not None),
        out_specs=[o_spec], out_shape=[jax.ShapeDtypeStruct((m, n), out_dtype)], aliases={2: 0} if acc is not None else None,
        name=name, compiler_params=_cp("parallel" if rider is None else "arbitrary"))(a, b, *([acc] if acc is not None else []))
    return out if rider is None else (out, got[0])


def _chip_rider(sums):
    def make(ins, outs, scratch):
        local, remote = _chip_exchange_copies(ins[0], outs[0], *scratch)

        def start():
            for cp in [local] + remote:
                cp.start()

        def finish():
            local.wait()
            for cp in remote:
                cp.wait_send()
                cp.wait_recv()

        return start, finish

    return _Rider([sums], [jax.ShapeDtypeStruct(sums.shape, sums.dtype)],
                  [pltpu.SemaphoreType.DMA((N_CHIP - 1,)), pltpu.SemaphoreType.DMA((N_CHIP - 1,)), pltpu.SemaphoreType.DMA((1,))],
                  {}, make)


_HBM = pl.BlockSpec(memory_space=pltpu.HBM)
_SEM = pl.BlockSpec(memory_space=pltpu.SEMAPHORE)


def _chip_exchange_copies(sums_ref, land_ref, send_sems, recv_sems, local_sem):
    x, y, core = lax.axis_index("x"), lax.axis_index("y"), lax.axis_index("c")
    mine = 2 * x + y
    local = pltpu.make_async_copy(sums_ref.at[mine], land_ref.at[mine], local_sem.at[0])
    remote = []
    for k in range(1, N_CHIP):
        px = 1 - x if k & 2 else x
        py = 1 - y if k & 1 else y
        remote.append(pltpu.make_async_remote_copy(
            src_ref=sums_ref.at[2 * px + py], dst_ref=land_ref.at[mine], send_sem=send_sems.at[k - 1],
            recv_sem=recv_sems.at[k - 1], device_id=(px, py, core), device_id_type=_MESH))
    return local, remote


def _chip_exchange_start(sums, name):
    def body(s_ref, send_sems, recv_sems, local_sem, s_thru, land_ref, token):
        local, remote = _chip_exchange_copies(s_ref, land_ref, send_sems, recv_sems, local_sem)
        for cp in [local] + remote:
            cp.start()
        token[...] = jnp.zeros_like(token)

    *flight, token = pl.pallas_call(
        body, name=name,
        out_shape=(pltpu.SemaphoreType.DMA((N_CHIP - 1,)), pltpu.SemaphoreType.DMA((N_CHIP - 1,)), pltpu.SemaphoreType.DMA((1,)),
                   pltpu.HBM(sums.shape, sums.dtype), pltpu.HBM(sums.shape, sums.dtype), jax.ShapeDtypeStruct((8, LANES), F32)),
        in_specs=(_HBM,), out_specs=(_SEM, _SEM, _SEM, _HBM, _HBM, pl.BlockSpec(memory_space=pltpu.VMEM)),
        input_output_aliases={0: 3},
        compiler_params=pltpu.CompilerParams(has_side_effects=pltpu.SideEffectType.DATAFLOW_SIDE_EFFECTING))(
            pltpu.with_memory_space_constraint(sums, pltpu.HBM))
    return flight, token


def _chip_exchange_wait(flights, after, name):
    n = len(flights)

    def body(*refs):
        for i in range(n):
            s_ref, land_ref = refs[2 * i:2 * i + 2]
            local, remote = _chip_exchange_copies(s_ref, land_ref, *refs[2 * n + 3 * i:2 * n + 3 * i + 3])
            local.wait()
            for cp in remote:
                cp.wait_send()
                cp.wait_recv()

    buffers = [b for f in flights for b in f[3:]]
    res = pl.pallas_call(
        body, name=name, out_shape=tuple(pltpu.HBM(b.shape, b.dtype) for b in buffers),
        in_specs=(_HBM,) * (2 * n) + (_SEM,) * (3 * n) + (_ANY,), out_specs=(_HBM,) * (2 * n),
        input_output_aliases={i: i for i in range(2 * n)},
        compiler_params=pltpu.CompilerParams(has_side_effects=pltpu.SideEffectType.DATAFLOW_SIDE_EFFECTING))(
            *buffers, *[s for f in flights for s in f[:3]], after)
    return res[1::2]


def _dxn(dproj, wfull, riders, tk, name):
    m, k = dproj.shape
    n = wfull.shape[0]
    tk = _tile(k, tk)
    steps = k // tk

    def body(start, finish, a_ref, b_ref, o_ref):
        j = pl.program_id(0)

        @pl.when(j == 0)
        def _():
            start()
            o_ref[...] = jnp.zeros_like(o_ref)
        o_ref[...] += _dot(a_ref[...], b_ref[...], "nt")
        pl.when(j == steps - 1)(finish)

    (dxn,), got = _hosted_call(
        body, riders, 2, 1, 0, grid=(steps,),
        in_specs=[pl.BlockSpec((m, tk), lambda j: (0, j)), pl.BlockSpec((n, tk), lambda j: (0, j))],
        out_specs=[pl.BlockSpec((m, n), lambda j: (0, 0))], out_shape=[jax.ShapeDtypeStruct((m, n), F32)],
        name=name, compiler_params=_cp("arbitrary"))(dproj, wfull)
    return dxn, got


def _conv_taps(x, w):
    c = x * w[CONV_WIDTH - 1:CONV_WIDTH, :]
    for j in range(CONV_WIDTH - 1):
        c = c + pltpu.roll(x, CONV_WIDTH - 1 - j, 0) * w[j:j + 1, :]
    return c


def _gdn_prep(proj, conv_wt, nh):
    lp = proj.shape[0]
    scale = HEAD_DIM ** -0.5

    def body(x_ref, w_ref, o_ref):
        which = pl.program_id(0) // nh
        c = _conv_taps(x_ref[...], w_ref[...])
        s = c * _sigmoid(c)
        r = lax.rsqrt(jnp.sum(s * s, axis=-1, keepdims=True) + EPS)
        f = jnp.where(which == 0, r * scale, jnp.where(which == 1, r, 1.0))
        o_ref[...] = jnp.where(_iota(s.shape, 0) >= PAD_ROWS, s * f, 0.0)

    return pl.pallas_call(
        body, grid=(3 * nh,),
        in_specs=[pl.BlockSpec((lp, LANES), lambda s: (0, s)), pl.BlockSpec((CONV_WIDTH, LANES), lambda s: (0, s))],
        out_specs=pl.BlockSpec((lp, LANES), lambda s: (0, s)),
        out_shape=jax.ShapeDtypeStruct((lp, 3 * nh * HEAD_DIM), F32), name="gdn_prep",
        compiler_params=_cp("parallel"))(proj, conv_wt)


def _gdn_prep_bwd(proj, conv_wt, dq, dk, dv, dproj, nh):
    lp = proj.shape[0]
    scale = HEAD_DIM ** -0.5
    part = lambda p: pl.BlockSpec((lp, LANES), lambda s: (0, jnp.clip(s - p * nh, 0, nh - 1)))

    def body(x_ref, w_ref, dq_ref, dk_ref, dv_ref, _, dx_ref, dw_ref):
        which = pl.program_id(0) // nh
        x = x_ref[...]
        w = w_ref[...]
        c = _conv_taps(x, w)
        sg = _sigmoid(c)
        s = c * sg
        r = lax.rsqrt(jnp.sum(s * s, axis=-1, keepdims=True) + EPS)
        dy = jnp.where(which == 0, dq_ref[...], jnp.where(which == 1, dk_ref[...], dv_ref[...]))
        dy = jnp.where(_iota(s.shape, 0) >= PAD_ROWS, dy, 0.0)
        y0 = s * r
        dy0 = dy * jnp.where(which == 0, scale, 1.0)
        ds_n = r * (dy0 - y0 * jnp.sum(dy0 * y0, axis=-1, keepdims=True))
        ds = jnp.where(which == 2, dy, ds_n)
        dc = ds * (sg * (1.0 + c * (1.0 - sg)))
        dx = dc * w[CONV_WIDTH - 1:CONV_WIDTH, :]
        rows = [jnp.sum(dc * x, axis=0, keepdims=True)]
        for j in range(CONV_WIDTH - 2, -1, -1):
            sh = CONV_WIDTH - 1 - j
            dx = dx + pltpu.roll(dc, lp - sh, 0) * w[j:j + 1, :]
            rows.insert(0, jnp.sum(dc * pltpu.roll(x, sh, 0), axis=0, keepdims=True))
        dx_ref[...] = dx.astype(BF16)
        dw_ref[...] = jnp.concatenate(rows, axis=0)

    strip = pl.BlockSpec((lp, LANES), lambda s: (0, s))
    taps = pl.BlockSpec((CONV_WIDTH, LANES), lambda s: (0, s))
    return pl.pallas_call(
        body, grid=(3 * nh,), in_specs=[strip, taps, part(0), part(1), part(2), _ANY], out_specs=[strip, taps],
        out_shape=[jax.ShapeDtypeStruct(dproj.shape, BF16), jax.ShapeDtypeStruct((CONV_WIDTH, 3 * nh * HEAD_DIM), F32)],
        input_output_aliases={5: 0}, name="gdn_prep_bwd", compiler_params=_cp("parallel"))(proj, conv_wt, dq, dk, dv, dproj)


def _gates(proj, bias_row, nega_row, nh):
    lp = proj.shape[0]
    nc = lp // CHUNK

    def body(p_ref, b_ref, a_ref, g_ref, gt3_ref, gtf_ref):
        lane = _iota((CHUNK, LANES), 1)
        tri = (_iota((CHUNK, CHUNK), 0) >= _iota((CHUNK, CHUNK), 1)).astype(F32)

        def step(n, carry):
            r0 = pl.multiple_of(n * CHUNK, CHUNK)
            z = p_ref[pl.ds(r0, CHUNK), :] + b_ref[...]
            base = jnp.where(lane < nh, _sigmoid(z),
                             jnp.where(lane < 2 * nh, a_ref[...] * _softplus(z),
                                       jnp.where(lane < 3 * nh, -_softplus(-z), 0.0)))
            base = jnp.where(r0 + _iota((CHUNK, LANES), 0) >= PAD_ROWS, base, 0.0)
            cs = _hdot(tri, base)
            run = jnp.where((lane >= 2 * nh) & (lane < 3 * nh), cs + carry, cs)
            sh = pltpu.roll(run, 2 * nh, 1)
            out = base + jnp.where((lane >= 3 * nh) & (lane < 5 * nh), sh, 0.0)
            g_ref[pl.ds(r0, CHUNK), :] = out
            gt3_ref[n] = out.T
            return carry + cs[CHUNK - 1:CHUNK, :]

        lax.fori_loop(0, nc, step, jnp.zeros((1, LANES), F32))
        gtf_ref[...] = g_ref[...].T

    vec = pl.BlockSpec((1, LANES), lambda i: (0, 0))
    return pl.pallas_call(
        body, grid=(1,), in_specs=[pl.BlockSpec((lp, LANES), lambda i: (0, 8 * nh)), vec, vec],
        out_specs=[pl.BlockSpec((lp, LANES), lambda i: (0, 0)), pl.BlockSpec((nc, LANES, CHUNK), lambda i: (0, 0, 0)),
                   pl.BlockSpec((LANES, lp), lambda i: (0, 0))],
        out_shape=[jax.ShapeDtypeStruct((lp, LANES), F32), jax.ShapeDtypeStruct((nc, LANES, CHUNK), F32),
                   jax.ShapeDtypeStruct((LANES, lp), F32)],
        name="gates", compiler_params=_cp("arbitrary"))(proj, bias_row, nega_row)


def _gates_bwd(proj, bias_row, nega_row, gates, dgate_gdn, dc_t, dproj, nh):
    lp = proj.shape[0]
    nc = lp // CHUNK

    def body(p_ref, b_ref, a_ref, g_ref, dg_ref, dc_ref, _, dz_ref, sm_ref, dct_scr):
        lane = _iota((CHUNK, LANES), 1)
        triu = (_iota((CHUNK, CHUNK), 0) <= _iota((CHUNK, CHUNK), 1)).astype(F32)
        dct_scr[...] = dc_ref[...].T
        sm_ref[...] = jnp.zeros_like(sm_ref)
        dz_ref[:, LANES:] = jnp.zeros((lp, NARROW - LANES), BF16)

        def step(i, carry):
            n = nc - 1 - i
            r0 = pl.multiple_of(n * CHUNK, CHUNK)
            z = p_ref[pl.ds(r0, CHUNK), :] + b_ref[...]
            gt = g_ref[pl.ds(r0, CHUNK), :]
            dgd = dg_ref[pl.ds(r0, CHUNK), :]
            dch = dct_scr[pl.ds(r0, CHUNK), :]
            rc = _hdot(triu, dch) + carry
            sg = _sigmoid(z)
            dz = jnp.where(lane < nh, dgd * sg * (1.0 - sg),
                           jnp.where(lane < 2 * nh, dgd * a_ref[...] * sg,
                                     jnp.where(lane < 3 * nh, rc * (1.0 - sg), 0.0)))
            dz = jnp.where(r0 + _iota((CHUNK, LANES), 0) >= PAD_ROWS, dz, 0.0)
            dz_ref[pl.ds(r0, CHUNK), 0:LANES] = dz.astype(BF16)
            sm_ref[0:1, :] += jnp.sum(dz, axis=0, keepdims=True)
            sm_ref[1:2, :] += jnp.sum(jnp.where((lane >= nh) & (lane < 2 * nh), dgd * gt, 0.0), axis=0, keepdims=True)
            return carry + jnp.sum(dch, axis=0, keepdims=True)

        lax.fori_loop(0, nc, step, jnp.zeros((1, LANES), F32))

    vec = pl.BlockSpec((1, LANES), lambda i: (0, 0))
    full = pl.BlockSpec((lp, LANES), lambda i: (0, 0))
    last = pl.BlockSpec((lp, LANES), lambda i: (0, 8 * nh))
    tail = pl.BlockSpec((lp, NARROW), lambda i: (0, 8 * nh * LANES // NARROW))
    return pl.pallas_call(
        body, grid=(1,), in_specs=[last, vec, vec, full, full, pl.BlockSpec((LANES, lp), lambda i: (0, 0)), _ANY],
        out_specs=[tail, pl.BlockSpec((8, LANES), lambda i: (0, 0))],
        out_shape=[jax.ShapeDtypeStruct(dproj.shape, BF16), jax.ShapeDtypeStruct((8, LANES), F32)],
        scratch_shapes=[pltpu.VMEM((lp, LANES), F32)], input_output_aliases={6: 0},
        name="gates_bwd", compiler_params=_cp("arbitrary"))(proj, bias_row, nega_row, gates, dgate_gdn, dc_t, dproj)


def _tri_inv(a):
    t = jnp.where(_iota(a.shape, 1) == _iota(a.shape, 2), 1.0, 0.0) - a
    p = a
    for _ in range(CHUNK.bit_length() - 2):
        ph, pw = _split(p)
        p = _dot3(ph, ph) + (_dot3(ph, pw) + _dot3(pw, ph))
        ph, pw = _split(p)
        th, tw = _split(t)
        t = t + (_dot3(th, ph) + (_dot3(th, pw) + _dot3(tw, ph)))
    return t


def _gdn_chunk(q, k, v, beta, gc, gr, t=None):
    ii, jj = _iota((1, CHUNK, CHUNK), 1), _iota((1, CHUNK, CHUNK), 2)
    causal, strict = ii >= jj, ii > jj
    dm = jnp.where(causal, jnp.exp(jnp.where(causal, gc - gr, 0.0)), 0.0)
    kk = _bdot3(k, k, "nt")
    a = jnp.where(strict, beta * kk * dm, 0.0)
    if t is None:
        t = _tri_inv(a)
    eg = jnp.exp(gc)
    glast = gc[:, CHUNK - 1:CHUNK, :]
    ekd = jnp.exp(glast - gc)
    bv = beta * v
    bk = (beta * eg) * k
    ub = _bdot3(t, jnp.concatenate([bv, bk], axis=2))
    qk = _bdot3(q, k, "nt")
    return dict(causal=causal, strict=strict, dm=dm, kk=kk, a=a, t=t, eg=eg, ekd=ekd, bv=bv, bk=bk,
                u=ub[:, :, :HEAD_DIM], w=ub[:, :, HEAD_DIM:], qk=qk, aqk=jnp.where(causal, qk * dm, 0.0),
                q_dec=q * eg, k_dec=k * ekd, decay=jnp.exp(glast))


def _heads(ref, nh):
    return jnp.stack([ref[:, h * HEAD_DIM:(h + 1) * HEAD_DIM] for h in range(nh)], axis=0)


def _gdn_chunk_inputs(q_ref, k_ref, v_ref, g, gt, nh):
    col = lambda o: jnp.stack([g[:, o + h:o + h + 1] for h in range(nh)], axis=0)
    gr = jnp.stack([gt[3 * nh + h:3 * nh + h + 1, :] for h in range(nh)], axis=0)
    return _heads(q_ref, nh), _heads(k_ref, nh), _heads(v_ref, nh), col(0), col(3 * nh), gr


def _gdn_fwd(qkv, gates, gt3, nh, rider=None):
    lp = qkv.shape[0]
    nc = lp // CHUNK
    w = nh * HEAD_DIM

    def body(start, finish, q_ref, k_ref, v_ref, g_ref, gt_ref, o_ref, sall_ref, tall_ref, s_scr):
        @pl.when(pl.program_id(0) == 0)
        def _():
            start()
            s_scr[...] = jnp.zeros_like(s_scr)
        c = _gdn_chunk(*_gdn_chunk_inputs(q_ref, k_ref, v_ref, g_ref[...], gt_ref[0], nh))
        s = s_scr[...]
        sall_ref[0] = s
        tall_ref[0] = c["t"]
        v_new = c["u"] - _bdot3(c["w"], s)
        o = _bdot3(c["q_dec"], s) + _bdot3(c["aqk"], v_new)
        s_scr[...] = s * c["decay"] + _bdot3(c["k_dec"], v_new, "tn")
        for h in range(nh):
            o_ref[:, h * HEAD_DIM:(h + 1) * HEAD_DIM] = o[h]
        pl.when(pl.program_id(0) == nc - 1)(finish)

    outs, got = _hosted_call(
        body, [rider], 5, 3, 1, grid=(nc,),
        in_specs=[pl.BlockSpec((CHUNK, w), lambda n: (n, 0)), pl.BlockSpec((CHUNK, w), lambda n: (n, 1)),
                  pl.BlockSpec((CHUNK, w), lambda n: (n, 2)), pl.BlockSpec((CHUNK, LANES), lambda n: (n, 0)),
                  pl.BlockSpec((1, LANES, CHUNK), lambda n: (n, 0, 0))],
        out_specs=[pl.BlockSpec((CHUNK, w), lambda n: (n, 0)),
                   pl.BlockSpec((1, nh, HEAD_DIM, HEAD_DIM), lambda n: (n, 0, 0, 0)),
                   pl.BlockSpec((1, nh, CHUNK, CHUNK), lambda n: (n, 0, 0, 0))],
        out_shape=[jax.ShapeDtypeStruct((lp, w), F32), jax.ShapeDtypeStruct((nc, nh, HEAD_DIM, HEAD_DIM), F32),
                   jax.ShapeDtypeStruct((nc, nh, CHUNK, CHUNK), F32)],
        scratch_shapes=[pltpu.VMEM((nh, HEAD_DIM, HEAD_DIM), F32)],
        name="gdn_fwd", compiler_params=_cp("arbitrary"))(qkv, qkv, qkv, gates, gt3)
    return outs, (got[0] if got else None)


def _gdn_bwd(qkv, gates, gt3, s_all, t_all, do, nh, rider=None):
    lp = qkv.shape[0]
    nc = lp // CHUNK
    w = nh * HEAD_DIM
    rev = lambda n: nc - 1 - n

    def body(start, finish, q_ref, k_ref, v_ref, g_ref, gt_ref, s_ref, t_ref, do_ref, dq_ref, dk_ref, dv_ref, dg_ref, ds_scr):
        @pl.when(pl.program_id(0) == 0)
        def _():
            start()
            ds_scr[...] = jnp.zeros_like(ds_scr)
        q, k, v, beta, gc, gr = _gdn_chunk_inputs(q_ref, k_ref, v_ref, g_ref[...], gt_ref[0], nh)
        c = _gdn_chunk(q, k, v, beta, gc, gr, t_ref[0])
        s = s_ref[0]
        dsn = ds_scr[...]
        dout = _heads(do_ref, nh)
        v_new = c["u"] - _bdot3(c["w"], s)
        dq_dec = _bdot3(dout, s, "nt")
        daqk = jnp.where(c["causal"], _bdot3(dout, v_new, "nt"), 0.0)
        dv_new = _bdot3(c["aqk"], dout, "tn") + _bdot3(c["k_dec"], dsn)
        dk_dec = _bdot3(v_new, dsn, "nt")
        ddecay = jnp.sum(jnp.sum(dsn * s, axis=2, keepdims=True), axis=1, keepdims=True)
        dw = -_bdot3(dv_new, s, "nt")
        ds_scr[...] = _bdot3(c["q_dec"], dout, "tn") + c["decay"] * dsn - _bdot3(c["w"], dv_new, "tn")
        duw = jnp.concatenate([dv_new, dw], axis=2)
        dt = _bdot3(duw, jnp.concatenate([c["bv"], c["bk"]], axis=2), "nt")
        dbvk = _bdot3(c["t"], duw, "tn")
        dbv, dbk = dbvk[:, :, :HEAD_DIM], dbvk[:, :, HEAD_DIM:]
        da = jnp.where(c["strict"], -_bdot3(_bdot3(c["t"], dt, "tn"), c["t"], "nt"), 0.0)
        dkk = da * beta * c["dm"]
        dqk = daqk * c["dm"]
        e = da * c["a"] + daqk * c["aqk"]
        dq = dq_dec * c["eg"] + _bdot3(dqk, k)
        dk = (dk_dec * c["ekd"] + _bdot3(dkk, k) + _bdot3(dkk, k, "tn") + _bdot3(dqk, q, "tn")
              + (beta * c["eg"]) * dbk)
        dv = beta * dbv
        rs = lambda x: jnp.sum(x, axis=2, keepdims=True)
        dbeta = rs(dbv * v) + c["eg"] * rs(dbk * k) + rs(da * c["kk"] * c["dm"])
        kd_term = rs(dk_dec * c["k_dec"])
        eh, ew = _split(e)
        ones = jnp.ones((nh, CHUNK, LANES), BF16)
        col_sums = (_dot3(eh, ones, "tn") + _dot3(ew, ones, "tn"))[:, :, 0:1]
        dg_cum = rs(dq_dec * c["q_dec"]) - kd_term + rs(dbk * c["bk"]) + rs(e) - col_sums
        last = jnp.sum(kd_term, axis=1, keepdims=True) + ddecay * c["decay"]
        dg_cum = dg_cum + jnp.where(_iota((1, CHUNK, 1), 1) == CHUNK - 1, last, 0.0)
        lane = _iota((CHUNK, LANES), 1)
        acc = jnp.zeros((CHUNK, LANES), F32)
        for h in range(nh):
            sl = slice(h * HEAD_DIM, (h + 1) * HEAD_DIM)
            dq_ref[:, sl] = dq[h]
            dk_ref[:, sl] = dk[h]
            dv_ref[:, sl] = dv[h]
            acc = acc + jnp.where(lane == h, dbeta[h], 0.0) + jnp.where(lane == nh + h, dg_cum[h], 0.0)
        triu = (_iota((CHUNK, CHUNK), 0) <= _iota((CHUNK, CHUNK), 1)).astype(F32)
        dg_ref[...] = jnp.where(lane < nh, acc, _hdot(triu, acc))
        pl.when(pl.program_id(0) == nc - 1)(finish)

    outs, got = _hosted_call(
        body, [rider], 8, 4, 1, grid=(nc,),
        in_specs=[pl.BlockSpec((CHUNK, w), lambda n: (rev(n), 0)), pl.BlockSpec((CHUNK, w), lambda n: (rev(n), 1)),
                  pl.BlockSpec((CHUNK, w), lambda n: (rev(n), 2)), pl.BlockSpec((CHUNK, LANES), lambda n: (rev(n), 0)),
                  pl.BlockSpec((1, LANES, CHUNK), lambda n: (rev(n), 0, 0)),
                  pl.BlockSpec((1, nh, HEAD_DIM, HEAD_DIM), lambda n: (rev(n), 0, 0, 0)),
                  pl.BlockSpec((1, nh, CHUNK, CHUNK), lambda n: (rev(n), 0, 0, 0)),
                  pl.BlockSpec((CHUNK, w), lambda n: (rev(n), 0))],
        out_specs=[pl.BlockSpec((CHUNK, w), lambda n: (rev(n), 0))] * 3 + [pl.BlockSpec((CHUNK, LANES), lambda n: (rev(n), 0))],
        out_shape=[jax.ShapeDtypeStruct((lp, w), F32)] * 3 + [jax.ShapeDtypeStruct((lp, LANES), F32)],
        scratch_shapes=[pltpu.VMEM((nh, HEAD_DIM, HEAD_DIM), F32)],
        name="gdn_bwd", compiler_params=_cp("arbitrary"))(qkv, qkv, qkv, gates, gt3, s_all, t_all, do)
    return outs, (got[0] if got else None)


def _merge_gdn(o_gdn, proj, norm_w, nh):
    lp = o_gdn.shape[0]

    def body(o_ref, z_ref, w_ref, m_ref):
        o = o_ref[...]
        z = z_ref[...]
        m_ref[...] = (o * _rms(o) * w_ref[...] * (z * _sigmoid(z))).astype(BF16)

    return pl.pallas_call(
        body, grid=(nh,),
        in_specs=[pl.BlockSpec((lp, LANES), lambda s: (0, s)), pl.BlockSpec((lp, LANES), lambda s: (0, 3 * nh + s)),
                  pl.BlockSpec((1, LANES), lambda s: (0, 0))],
        out_specs=pl.BlockSpec((lp, LANES), lambda s: (0, s)),
        out_shape=jax.ShapeDtypeStruct((lp, 2 * nh * HEAD_DIM), BF16), name="merge_gdn",
        compiler_params=_cp("parallel"))(o_gdn, proj, norm_w)


def _merge_gdn_bwd(o_gdn, proj, norm_w, dmerged, nh):
    lp = o_gdn.shape[0]

    def body(o_ref, z_ref, w_ref, dm_ref, do_ref, dz_ref, dw_ref):
        o = o_ref[...]
        r = _rms(o)
        xh = o * r
        silu, dsilu = _silu_and_grad(z_ref[...])
        dm = dm_ref[...]
        dn = dm * silu
        dz_ref[...] = (dm * (xh * w_ref[...]) * dsilu).astype(BF16)
        dnw = dn * w_ref[...]
        do_ref[...] = r * (dnw - xh * jnp.mean(dnw * xh, axis=-1, keepdims=True))

        @pl.when(pl.program_id(0) == 0)
        def _():
            dw_ref[...] = jnp.zeros_like(dw_ref)
        dw_ref[...] += jnp.sum(dn * xh, axis=0, keepdims=True)

    w = nh * HEAD_DIM
    return pl.pallas_call(
        body, grid=(nh,),
        in_specs=[pl.BlockSpec((lp, LANES), lambda s: (0, s)), pl.BlockSpec((lp, LANES), lambda s: (0, 3 * nh + s)),
                  pl.BlockSpec((1, LANES), lambda s: (0, 0)), pl.BlockSpec((lp, LANES), lambda s: (0, s))],
        out_specs=[pl.BlockSpec((lp, LANES), lambda s: (0, s)), pl.BlockSpec((lp, LANES), lambda s: (0, 3 * nh + s)),
                   pl.BlockSpec((1, LANES), lambda s: (0, 0))],
        out_shape=[jax.ShapeDtypeStruct((lp, w), F32), jax.ShapeDtypeStruct((lp, 8 * w + NARROW), BF16),
                   jax.ShapeDtypeStruct((1, LANES), F32)],
        name="merge_gdn_bwd", compiler_params=_cp("arbitrary"))(o_gdn, proj, norm_w, dmerged)


def _fox_prep(proj, qk_w, nh):
    lp = proj.shape[0]

    def body(x_ref, w_ref, o_ref):
        x = x_ref[...]
        o_ref[...] = x * _rms(x) * w_ref[0]

    return pl.pallas_call(
        body, grid=(2 * nh,),
        in_specs=[pl.BlockSpec((lp, LANES), lambda s: (0, 4 * nh + s)), pl.BlockSpec((1, 1, LANES), lambda s: (s // nh, 0, 0))],
        out_specs=pl.BlockSpec((lp, LANES), lambda s: (0, s)),
        out_shape=jax.ShapeDtypeStruct((lp, 2 * nh * HEAD_DIM), F32), name="fox_prep",
        compiler_params=_cp("parallel"))(proj, qk_w)


def _fox_prep_bwd(proj, qk_w, dq, dk, dproj, nh):
    lp = proj.shape[0]
    part = lambda p: pl.BlockSpec((lp, LANES), lambda s: (0, jnp.clip(s - p * nh, 0, nh - 1)))

    def body(x_ref, w_ref, dq_ref, dk_ref, _, dx_ref, dw_ref):
        x = x_ref[...]
        r = _rms(x)
        xh = x * r
        dy = jnp.where(pl.program_id(0) < nh, dq_ref[...], dk_ref[...])
        dyw = dy * w_ref[0]
        dx_ref[...] = (r * (dyw - xh * jnp.mean(dyw * xh, axis=-1, keepdims=True))).astype(BF16)

        @pl.when(pl.program_id(0) % nh == 0)
        def _():
            dw_ref[...] = jnp.zeros_like(dw_ref)
        dw_ref[0] += jnp.sum(dy * xh, axis=0, keepdims=True)

    strip = pl.BlockSpec((lp, LANES), lambda s: (0, 4 * nh + s))
    wsp = pl.BlockSpec((1, 1, LANES), lambda s: (s // nh, 0, 0))
    return pl.pallas_call(
        body, grid=(2 * nh,), in_specs=[strip, wsp, part(0), part(1), _ANY], out_specs=[strip, wsp],
        out_shape=[jax.ShapeDtypeStruct(dproj.shape, BF16), jax.ShapeDtypeStruct((2, 1, LANES), F32)],
        input_output_aliases={4: 0}, name="fox_prep_bwd", compiler_params=_cp("arbitrary"))(proj, qk_w, dq, dk, dproj)


def _fox_probs(q, k, gates, crow, h, i, nh, lse=None):
    kl = k.shape[0]
    lane = _iota((Q_BLOCK, LANES), 1)
    ct = jnp.sum(jnp.where(lane == 4 * nh + h, gates, 0.0), axis=1, keepdims=True)
    tq, kq = _iota((Q_BLOCK, Q_BLOCK), 0), _iota((Q_BLOCK, Q_BLOCK), 1)
    qs = q * (HEAD_DIM ** -0.5)
    if i == 0:
        s = _bdot(qs, k, "nt") + (ct - crow)
        s = jnp.where((kq <= tq) & ((kq >= PAD_ROWS) | (tq < PAD_ROWS)), s, NEG)
    else:
        crow = jnp.where(_iota((1, kl), 1) < PAD_ROWS, -NEG, crow)
        s = _bdot(qs, k, "nt") + (ct - crow)
        s = jnp.concatenate([s[:, :kl - Q_BLOCK], jnp.where(kq <= tq, s[:, kl - Q_BLOCK:], NEG)], axis=1)
    if lse is not None:
        return jnp.exp(s - lse)
    m = jnp.max(s, axis=1, keepdims=True)
    p = jnp.exp(s - m)
    tot = jnp.sum(p, axis=1, keepdims=True)
    return p / tot, m + jnp.log(tot)


FOX_HEADS_PER_STEP = 2


def _fox_specs(lp, nh):
    hw = FOX_HEADS_PER_STEP * LANES
    return [pl.BlockSpec((Q_BLOCK, hw), lambda g, i: (i, g)),
            pl.BlockSpec((lp, hw), lambda g, i: (0, nh // FOX_HEADS_PER_STEP + g)),
            pl.BlockSpec((lp, hw), lambda g, i: (0, 6 * nh // FOX_HEADS_PER_STEP + g)),
            pl.BlockSpec((Q_BLOCK, LANES), lambda g, i: (i, 0)),
            pl.BlockSpec((LANES, lp), lambda g, i: (0, 0))]


def _fox_fwd(qkn, proj, gates, gtf, nh):
    lp = qkn.shape[0]

    def body(q_ref, k_ref, v_ref, g_ref, gt_ref, o_ref, lse_ref):
        g, i = pl.program_id(0), pl.program_id(1)
        for j in range(lp // Q_BLOCK):
            @pl.when(i == j)
            def _(j=j):
                kl = (j + 1) * Q_BLOCK
                for hh in range(FOX_HEADS_PER_STEP):
                    h = FOX_HEADS_PER_STEP * g + hh
                    sl = slice(hh * LANES, (hh + 1) * LANES)
                    p, lse = _fox_probs(q_ref[:, sl], k_ref[0:kl, sl], g_ref[...], gt_ref[pl.ds(4 * nh + h, 1), :][:, 0:kl],
                                        h, j, nh)
                    o_ref[:, sl] = _bdot(p, v_ref[0:kl, sl])
                    lse_ref[:, sl] = jnp.broadcast_to(lse, (Q_BLOCK, LANES))

    blk = pl.BlockSpec((Q_BLOCK, FOX_HEADS_PER_STEP * LANES), lambda g, i: (i, g))
    return pl.pallas_call(
        body, grid=(nh // FOX_HEADS_PER_STEP, lp // Q_BLOCK), in_specs=_fox_specs(lp, nh), out_specs=[blk, blk],
        out_shape=[jax.ShapeDtypeStruct((lp, nh * HEAD_DIM), F32)] * 2, name="fox_fwd",
        compiler_params=_cp("parallel", "parallel"))(qkn, qkn, proj, gates, gtf)


def _fox_bwd(qkn, proj, gates, gtf, lse, do, dproj, nh):
    lp = qkn.shape[0]
    nq = lp // Q_BLOCK
    w = nh * HEAD_DIM
    scale = HEAD_DIM ** -0.5

    def body(q_ref, k_ref, v_ref, g_ref, gt_ref, lse_ref, do_ref, _, dq_ref, dk_ref, dc_ref, dv_ref, dv_scr):
        g, i = pl.program_id(0), pl.program_id(1)

        @pl.when(i == 0)
        def _():
            dk_ref[...] = jnp.zeros_like(dk_ref)
            dv_scr[...] = jnp.zeros_like(dv_scr)
            dc_ref[...] = jnp.zeros_like(dc_ref)
        for j in range(nq):
            @pl.when(i == j)
            def _(j=j):
                kl = (j + 1) * Q_BLOCK
                for hh in range(FOX_HEADS_PER_STEP):
                    h = FOX_HEADS_PER_STEP * g + hh
                    sl = slice(hh * LANES, (hh + 1) * LANES)
                    q, k = q_ref[:, sl], k_ref[0:kl, sl]
                    p = _fox_probs(q, k, g_ref[...], gt_ref[pl.ds(4 * nh + h, 1), :][:, 0:kl], h, j, nh,
                                   lse_ref[:, sl][:, 0:1])
                    dout = do_ref[:, sl]
                    dp = _bdot(dout, v_ref[0:kl, sl], "nt")
                    ds = p * (dp - jnp.sum(p * dp, axis=1, keepdims=True))
                    dq_ref[:, sl] = _bdot(ds, k) * scale
                    dk_ref[0:kl, sl] += _bdot(ds, q * scale, "tn")
                    dv_scr[0:kl, sl] += _bdot(p, dout, "tn")
                    dc_ref[hh, :, 0:kl] -= jnp.sum(ds, axis=0, keepdims=True)

        @pl.when(i == nq - 1)
        def _():
            dv_ref[...] = dv_scr[...].astype(BF16)

    hw = FOX_HEADS_PER_STEP * LANES
    blk = pl.BlockSpec((Q_BLOCK, hw), lambda g, i: (i, g))
    col = pl.BlockSpec((lp, hw), lambda g, i: (0, g))
    return pl.pallas_call(
        body, grid=(nh // FOX_HEADS_PER_STEP, nq), in_specs=_fox_specs(lp, nh) + [blk, blk, _ANY],
        out_specs=[blk, col, pl.BlockSpec((FOX_HEADS_PER_STEP, 1, lp), lambda g, i: (g, 0, 0)),
                   pl.BlockSpec((lp, hw), lambda g, i: (0, 6 * nh // FOX_HEADS_PER_STEP + g))],
        out_shape=[jax.ShapeDtypeStruct((lp, w), F32)] * 2 + [jax.ShapeDtypeStruct((nh, 1, lp), F32),
                                                             jax.ShapeDtypeStruct(dproj.shape, BF16)],
        scratch_shapes=[pltpu.VMEM((lp, hw), F32)], input_output_aliases={7: 3},
        name="fox_bwd", compiler_params=_cp("parallel", "arbitrary"))(qkn, qkn, proj, gates, gtf, lse, do, dproj)


def _merge_fox(o_fox, proj, merged, nh):
    lp = o_fox.shape[0]

    def body(o_ref, z_ref, _, m_ref):
        z = z_ref[...]
        m_ref[...] = (o_ref[...] * (z * _sigmoid(z))).astype(BF16)

    return pl.pallas_call(
        body, grid=(nh,),
        in_specs=[pl.BlockSpec((lp, LANES), lambda s: (0, s)), pl.BlockSpec((lp, LANES), lambda s: (0, 7 * nh + s)), _ANY],
        out_specs=pl.BlockSpec((lp, LANES), lambda s: (0, nh + s)),
        out_shape=jax.ShapeDtypeStruct(merged.shape, BF16), input_output_aliases={2: 0}, name="merge_fox",
        compiler_params=_cp("parallel"))(o_fox, proj, merged)


def _merge_fox_bwd(o_fox, proj, dmerged, dproj, nh):
    lp = o_fox.shape[0]

    def body(o_ref, z_ref, dm_ref, _, do_ref, dz_ref):
        silu, dsilu = _silu_and_grad(z_ref[...])
        dm = dm_ref[...]
        do_ref[...] = dm * silu
        dz_ref[...] = (dm * o_ref[...] * dsilu).astype(BF16)

    w = nh * HEAD_DIM
    return pl.pallas_call(
        body, grid=(nh,),
        in_specs=[pl.BlockSpec((lp, LANES), lambda s: (0, s)), pl.BlockSpec((lp, LANES), lambda s: (0, 7 * nh + s)),
                  pl.BlockSpec((lp, LANES), lambda s: (0, nh + s)), _ANY],
        out_specs=[pl.BlockSpec((lp, LANES), lambda s: (0, s)), pl.BlockSpec((lp, LANES), lambda s: (0, 7 * nh + s))],
        out_shape=[jax.ShapeDtypeStruct((lp, w), F32), jax.ShapeDtypeStruct(dproj.shape, BF16)],
        input_output_aliases={3: 1}, name="merge_fox_bwd", compiler_params=_cp("parallel"))(o_fox, proj, dmerged, dproj)


def _post(out, x, target, post_w):
    lp, d = out.shape

    def body(o_ref, x_ref, t_ref, w_ref, dy_ref, do_ref, loss_ref, dw_ref):
        i = pl.program_id(0)

        @pl.when(i == 0)
        def _():
            loss_ref[...] = jnp.zeros_like(loss_ref)
            dw_ref[...] = jnp.zeros_like(dw_ref)
        o = o_ref[...]
        r = _rms(o)
        nrm = o * r
        err = jnp.where(i > 0, x_ref[...] + nrm * w_ref[...] - t_ref[...], 0.0)
        loss_ref[0:1, :] += 0.5 * jnp.sum(jnp.sum(err * err, axis=1, keepdims=True), axis=0, keepdims=True) / d
        dy = err / d
        dy_ref[...] = dy
        dw_ref[...] += jnp.sum(dy * nrm, axis=0, keepdims=True)
        dyw = dy * w_ref[...]
        do_ref[...] = (r * (dyw - nrm * jnp.mean(dyw * nrm, axis=-1, keepdims=True))).astype(BF16)

    row = pl.BlockSpec((Q_BLOCK, d), lambda i: (i, 0))
    vec = pl.BlockSpec((1, d), lambda i: (0, 0))
    return pl.pallas_call(
        body, grid=(lp // Q_BLOCK,), in_specs=[row, _x_rows(d), _x_rows(d), vec],
        out_specs=[_x_rows(d), row, pl.BlockSpec((8, LANES), lambda i: (0, 0)), vec],
        out_shape=[jax.ShapeDtypeStruct(x.shape, F32), jax.ShapeDtypeStruct((lp, d), BF16),
                   jax.ShapeDtypeStruct((8, LANES), F32), jax.ShapeDtypeStruct((1, d), F32)],
        name="post", compiler_params=_cp("arbitrary"))(out, x, target, post_w)


def _prenorm_bwd(dxn, x, meta, w, dy, rider=None):
    seq, d = x.shape
    lp = seq + Q_BLOCK

    def body(start, finish, dx_ref, x_ref, m_ref, w_ref, dy_ref, gx_ref, gm_ref, dw_ref):
        i = pl.program_id(0)
        pl.when(i == 0)(start)
        h = _h_tile(i, x_ref, m_ref)
        r = _rms(h)
        xh = h * r
        dxn_ = dx_ref[...]
        dxw = dxn_ * w_ref[...]
        dh = jnp.where(i > 0, dy_ref[...], 0.0) + r * (dxw - xh * jnp.mean(dxw * xh, axis=-1, keepdims=True))
        gx_ref[...] = dh

        @pl.when(i == 0)
        def _():
            dw_ref[...] = jnp.zeros_like(dw_ref)
            gm_ref[...] = dh[PAD_ROWS:, :]
        dw_ref[...] += jnp.sum(dxn_ * xh, axis=0, keepdims=True)
        pl.when(i == lp // Q_BLOCK - 1)(finish)

    vec = pl.BlockSpec((1, d), lambda i: (0, 0))
    met = pl.BlockSpec((N_META, d), lambda i: (0, 0))
    outs, got = _hosted_call(
        body, [rider], 5, 3, 0, grid=(lp // Q_BLOCK,),
        in_specs=[pl.BlockSpec((Q_BLOCK, d), lambda i: (i, 0)), _x_rows(d), met, vec, _x_rows(d)],
        out_specs=[_x_rows(d), met, vec],
        out_shape=[jax.ShapeDtypeStruct((seq, d), F32), jax.ShapeDtypeStruct((N_META, d), F32),
                   jax.ShapeDtypeStruct((1, d), F32)],
        name="prenorm_bwd", compiler_params=_cp("arbitrary"))(dxn, x, meta, w, dy)
    return outs, (got[0] if got else None)


def _layer_grads(x, target, meta, pre_w, wfull, conv_wt, a_log, dt_bias, gdn_norm_w, fq_w, fk_w, f_bias, w_out, post_w,
                 late_weights=None, w_out_grads=None):
    nh = a_log.shape[1]
    zpad = jnp.zeros((1, LANES - 3 * nh), F32)
    bias_row = jnp.concatenate([jnp.zeros((1, nh), F32), dt_bias, f_bias, zpad], axis=1)
    nega_row = jnp.concatenate([jnp.zeros((1, nh), F32), -jnp.exp(a_log), jnp.zeros((1, nh), F32), zpad], axis=1)
    qk_w = jnp.stack([fq_w, fk_w])

    if isinstance(wfull, tuple):
        rider, meta_at, project = wfull
        xn, got = _prenorm_gathering(x, pre_w, rider, meta_at)
        proj, wfull, conv_wt, meta = project(xn, got)
    else:
        xn = _prenorm(x, meta, pre_w)
        proj = _matmul(xn, wfull, "nn", MM_TILE, F32, "proj")
    qkv = _gdn_prep(proj, conv_wt, nh)
    gates, gt3, gtf = _gates(proj, bias_row, nega_row, nh)
    (o_gdn, s_all, t_all), got = _gdn_fwd(qkv, gates, gt3, nh, None if late_weights is None else late_weights[0])
    if late_weights is not None:
        w_out = late_weights[1](got)
    qkn = _fox_prep(proj, qk_w, nh)
    o_fox, fox_lse = _fox_fwd(qkn, proj, gates, gtf, nh)
    merged = _merge_fox(o_fox, proj, _merge_gdn(o_gdn, proj, gdn_norm_w, nh), nh)
    out = _matmul(merged, w_out, "nn", 4 * LANES, F32, "out_proj")
    dy, dout, loss_blk, dpost_w = _post(out, x, target, post_w)

    dw_out = _matmul(merged, dout, "tn", 4 * LANES, BF16, "dw_out")
    if w_out_grads is None:
        dmerged, gdn_rider = _matmul(dout, w_out, "nt", 4 * LANES, F32, "dmerged"), None
    else:
        dmerged, got = _matmul(dout, w_out, "nt", 4 * LANES, F32, "dmerged", w_out_grads[0](dw_out))
        gdn_rider = w_out_grads[1](dw_out, got)
    do_gdn, dproj, dgdn_norm_w = _merge_gdn_bwd(o_gdn, proj, gdn_norm_w, dmerged, nh)
    do_fox, dproj = _merge_fox_bwd(o_fox, proj, dmerged, dproj, nh)
    dqn, dkn, dc_t, dproj = _fox_bwd(qkn, proj, gates, gtf, fox_lse, do_fox, dproj, nh)
    dproj, dqk_w = _fox_prep_bwd(proj, qk_w, dqn, dkn, dproj, nh)
    (dgq, dgk, dgv, dgate), w_out_parts = _gdn_bwd(qkv, gates, gt3, s_all, t_all, do_gdn, nh, gdn_rider)
    dproj, dconv_wt = _gdn_prep_bwd(proj, conv_wt, dgq, dgk, dgv, dproj, nh)
    dc_rows = jnp.pad(dc_t.reshape(nh, -1), ((2 * nh, LANES - 3 * nh), (0, 0)))
    dproj, gate_sums = _gates_bwd(proj, bias_row, nega_row, gates, dgate, dc_rows, dproj, nh)
    return dict(
        loss=loss_blk[0:1, 0:1], dy=dy, xn=xn, dproj=dproj, post_w=dpost_w,
        conv_wt=dconv_wt, a_log=gate_sums[1:2, nh:2 * nh], dt_bias=gate_sums[0:1, nh:2 * nh],
        gdn_norm_w=dgdn_norm_w, fq_w=dqk_w[0], fk_w=dqk_w[1], f_bias=gate_sums[0:1, 2 * nh:3 * nh], w_out=dw_out,
        w_out_parts=w_out_parts, wfull=wfull, meta=meta)


def _cast_bf16(a, tr, name):
    r, c = a.shape

    def body(a_ref, o_ref):
        o_ref[...] = a_ref[...].astype(BF16)

    return pl.pallas_call(
        body, grid=(r // tr,), in_specs=[pl.BlockSpec((tr, c), lambda i: (i, 0))],
        out_specs=pl.BlockSpec((tr, c), lambda i: (i, 0)), out_shape=jax.ShapeDtypeStruct((r, c), BF16),
        name=name, compiler_params=_cp("parallel"))(a)


def _column_major(a):
    return jnp.transpose(a, (2, 0, 1))


def _cast_bf16_column_major(a3, pieces, name):
    _, r, c = a3.shape
    rows = r // pieces

    def body(a_ref, *o_refs):
        t = a_ref[...].reshape(LANES, r).T.astype(BF16)
        for k, o_ref in enumerate(o_refs):
            o_ref[...] = t[k * rows:(k + 1) * rows]

    return pl.pallas_call(
        body, grid=(pl.cdiv(c, LANES),), in_specs=[pl.BlockSpec((LANES, 1, r), lambda i: (i, 0, 0))],
        out_specs=[pl.BlockSpec((rows, LANES), lambda i: (0, i))] * pieces,
        out_shape=[jax.ShapeDtypeStruct((rows, c), BF16)] * pieces, name=name, compiler_params=_cp("parallel"))(_column_major(a3))


def _adamw_column_major(w3, parts, m3, v3, name):
    _, r, c = w3.shape
    n_parts = parts[0].shape[0]

    def body(w_ref, *refs):
        p_refs, (m_ref, v_ref, g_ref, d_ref, nm_ref, nv_ref) = refs[:len(parts)], refs[len(parts):]
        sums = []
        for p_ref in p_refs:
            g = p_ref[0].astype(F32)
            for s in range(1, n_parts):
                g = g + p_ref[s].astype(F32)
            sums.append(g)
        g = jnp.concatenate(sums, axis=0).T
        flat = lambda ref: ref[...].reshape(LANES, r)
        m_new = ADAM_B1 * flat(m_ref) + (1.0 - ADAM_B1) * g
        v_new = ADAM_B2 * flat(v_ref) + (1.0 - ADAM_B2) * (g * g)
        m_hat = m_new / (1.0 - ADAM_B1 ** ADAM_STEP)
        v_hat = v_new / (1.0 - ADAM_B2 ** ADAM_STEP)
        delta = -ADAM_LR * (m_hat / (jnp.sqrt(v_hat) + ADAM_EPS) + ADAM_WD * flat(w_ref))
        for ref, val in ((g_ref, g), (d_ref, delta), (nm_ref, m_new), (nv_ref, v_new)):
            ref[...] = val.reshape(LANES, 1, r)

    blk = pl.BlockSpec((LANES, 1, r), lambda i: (i, 0, 0))
    outs = pl.pallas_call(
        body, grid=(pl.cdiv(c, LANES),),
        in_specs=[blk] + [pl.BlockSpec((n_parts, p.shape[1], LANES), lambda i: (0, 0, i)) for p in parts] + [blk, blk],
        out_specs=[blk] * 4, out_shape=[jax.ShapeDtypeStruct((c, 1, r), F32)] * 4, name=name,
        compiler_params=_cp("parallel"))(_column_major(w3), *parts, _column_major(m3), _column_major(v3))
    return [jnp.transpose(o, (1, 2, 0)) for o in outs]


def _gather_copies(ins, outs, send_sems, recv_sems, local_sems):
    n = len(ins)
    x, y, c = lax.axis_index("x"), lax.axis_index("y"), lax.axis_index("c")
    me, sibling = (x, y, c), (x, y, 1 - c)
    xn, yn, dg = (1 - x, y), (x, 1 - y), (1 - x, 1 - y)

    def copy(a, k, block, to, src=None):
        px, py, pc = block
        rows = outs[a].at[4 * px + 2 * py + pc]
        return pltpu.make_async_remote_copy(
            src_ref=rows if src is None else src, dst_ref=rows, send_sem=send_sems.at[a, k],
            recv_sem=recv_sems.at[a, k], device_id=to, device_id_type=_MESH)

    local = [pltpu.make_async_copy(ins[a], outs[a].at[4 * x + 2 * y + c], local_sems.at[a]) for a in range(n)]
    own = [cp for a in range(n) for cp in (copy(a, 0, me, sibling, src=ins[a]), copy(a, 1, me, (*xn, c), src=ins[a]),
                                           copy(a, 2, me, (*yn, c), src=ins[a]))]

    def start():
        for cp in local + own:
            cp.start()

    def finish():
        for a in range(n):
            @pl.when(c == 1)
            def _(a=a):
                copy(a, 1, (*xn, c), me).wait_recv()
                copy(a, 3, (*xn, c), (*yn, c)).start()

            @pl.when(c == 0)
            def _(a=a):
                copy(a, 2, (*yn, c), me).wait_recv()
                copy(a, 3, (*yn, c), (*xn, c)).start()
        for a in range(n):
            pl.when(c == 0)(copy(a, 1, (*xn, c), me).wait_recv)
            copy(a, 4, (*xn, c), sibling).start()
            pl.when(c == 1)(copy(a, 2, (*yn, c), me).wait_recv)
            copy(a, 5, (*yn, c), sibling).start()
        for a in range(n):
            copy(a, 3, (*dg, c), me).wait_recv()
            copy(a, 6, (*dg, c), sibling).start()
        for a in range(n):
            copy(a, 0, sibling, me).wait_recv()
            for k, chip in ((4, xn), (5, yn), (6, dg)):
                copy(a, k, (*chip, 1 - c), me).wait_recv()
                copy(a, k, (*chip, c), sibling).wait_send()
            copy(a, 3, (*xn, c), (*yn, c)).wait_send()
        for cp in own:
            cp.wait_send()
        for cp in local:
            cp.wait()

    return start, finish


def _gather_scratch(n):
    return [pltpu.SemaphoreType.DMA((n, N_DEV - 1)), pltpu.SemaphoreType.DMA((n, N_DEV - 1)), pltpu.SemaphoreType.DMA((n,))]


def _gather_rider(arrays):
    return _Rider(list(arrays), [jax.ShapeDtypeStruct((N_DEV,) + a.shape, a.dtype) for a in arrays],
                  _gather_scratch(len(arrays)), {}, lambda ins, outs, scratch: _gather_copies(ins, outs, *scratch))


def _all_gather(arrays, name):
    n = len(arrays)

    def body(*refs):
        start, finish = _gather_copies(refs[:n], refs[n:2 * n], *refs[2 * n:])
        start()
        finish()

    return pl.pallas_call(
        body, in_specs=[_ANY] * n, out_specs=[_ANY] * n,
        out_shape=[jax.ShapeDtypeStruct((N_DEV,) + a.shape, a.dtype) for a in arrays],
        scratch_shapes=_gather_scratch(n), name=name)(*arrays)


SLAB = 10 * LANES


def _slab_start(blk, nh, cols):
    in_second_half = blk >= N_DEV // 2
    shift = (2 * nh if in_second_half else 0) if isinstance(blk, int) else jnp.where(in_second_half, 2 * nh, 0)
    return (blk * cols - shift) // LANES * LANES


def _pair_rider(dw_rows=None, parts=None, nh=None, after=None):
    if dw_rows is not None:
        r, full = dw_rows.shape
        cols = (full - NARROW + 3 * nh) // N_DEV
        out_shapes = [jax.ShapeDtypeStruct((N_CHIP, r, SLAB), dw_rows.dtype), jax.ShapeDtypeStruct((r, NARROW), dw_rows.dtype)]
    else:
        out_shapes = [jax.ShapeDtypeStruct((N_CHIP,) + parts.shape[1:], parts.dtype)]

    def make(ins, outs, scratch):
        send_sems, recv_sems = scratch
        x, y, c = lax.axis_index("x"), lax.axis_index("y"), lax.axis_index("c")
        kw = lambda k: dict(send_sem=send_sems.at[k], recv_sem=recv_sems.at[k], device_id=(x, y, 1 - c), device_id_type=_MESH)
        copies = []
        for q in range(N_CHIP):
            if dw_rows is not None:
                first = pl.multiple_of(_slab_start(2 * q + 1 - c, nh, cols), LANES)
                copies.append(pltpu.make_async_remote_copy(src_ref=ins[0].at[:, pl.ds(first, SLAB)], dst_ref=outs[0].at[q], **kw(q)))
            else:
                copies.append(pltpu.make_async_remote_copy(src_ref=ins[0].at[2 * q + 1 - c], dst_ref=outs[0].at[q], **kw(q)))
        if dw_rows is not None:
            copies.append(pltpu.make_async_remote_copy(src_ref=ins[0].at[:, pl.ds(full - NARROW, NARROW)], dst_ref=outs[1],
                                                       **kw(N_CHIP)))

        def start():
            for cp in copies:
                cp.start()

        def finish():
            for cp in copies:
                cp.wait()

        return start, finish

    return _Rider([dw_rows if dw_rows is not None else parts] + ([] if after is None else [after]), out_shapes,
                  [pltpu.SemaphoreType.DMA((N_CHIP + 1,)), pltpu.SemaphoreType.DMA((N_CHIP + 1,))], {}, make)


def _relayout_pair_sum(dwfull, got_slabs, got_tail, core, nh, tr, name):
    d, full = dwfull.shape
    w = nh * HEAD_DIM
    cols = (8 * w + 3 * nh) // N_DEV
    segs = _native_segments(nh)

    def block(f_ref, s_ref, t_ref, q, blk):
        st = _slab_start(blk, nh, cols)
        wide = f_ref[:, st:st + SLAB].astype(F32) + s_ref[q].astype(F32)
        tail = f_ref[:, 8 * w:].astype(F32) + t_ref[...].astype(F32)
        pieces = []
        for s0, s1, t0 in segs:
            lo, hi = max(s0, blk * cols), min(s1, (blk + 1) * cols)
            if lo < hi:
                at = t0 + lo - s0
                pieces.append(tail[:, at - 8 * w:at - 8 * w + hi - lo] if at >= 8 * w else wide[:, at - st:at - st + hi - lo])
        return (pieces[0] if len(pieces) == 1 else jnp.concatenate(pieces, axis=1)).astype(dwfull.dtype)

    def body(core_ref, f_ref, s_ref, t_ref, o_ref):
        for parity in range(2):
            @pl.when(core_ref[0] == parity)
            def _(parity=parity):
                for q in range(N_CHIP):
                    o_ref[q] = block(f_ref, s_ref, t_ref, q, 2 * q + parity)

    return pl.pallas_call(
        body,
        grid_spec=pltpu.PrefetchScalarGridSpec(
            num_scalar_prefetch=1, grid=(d // tr,),
            in_specs=[pl.BlockSpec((tr, full), lambda i, c_ref: (i, 0)), pl.BlockSpec((N_CHIP, tr, SLAB), lambda i, c_ref: (0, i, 0)),
                      pl.BlockSpec((tr, NARROW), lambda i, c_ref: (i, 0))],
            out_specs=pl.BlockSpec((N_CHIP, tr, cols), lambda i, c_ref: (0, i, 0))),
        out_shape=jax.ShapeDtypeStruct((N_CHIP, d, cols), dwfull.dtype), name=name,
        compiler_params=_cp("parallel"))(core, dwfull, got_slabs, got_tail)


def _pair_sum(parts, got, core, tr, name):
    _, r, c = parts.shape

    def body(core_ref, p_ref, g_ref, o_ref):
        o_ref[...] = (p_ref[...].astype(F32) + g_ref[...].astype(F32)).astype(o_ref.dtype)

    return pl.pallas_call(
        body,
        grid_spec=pltpu.PrefetchScalarGridSpec(
            num_scalar_prefetch=1, grid=(N_CHIP, r // tr),
            in_specs=[pl.BlockSpec((1, tr, c), lambda q, i, core_ref: (2 * q + core_ref[0], i, 0)),
                      pl.BlockSpec((1, tr, c), lambda q, i, core_ref: (q, i, 0))],
            out_specs=pl.BlockSpec((1, tr, c), lambda q, i, core_ref: (q, i, 0))),
        out_shape=jax.ShapeDtypeStruct((N_CHIP, r, c), parts.dtype), name=name,
        compiler_params=_cp("parallel", "parallel"))(core, parts, got)


def _native_segments(nh):
    w = nh * HEAD_DIM
    return [(0, 4 * w, 0), (4 * w, 4 * w + 2 * nh, 8 * w), (4 * w + 2 * nh, 8 * w + 2 * nh, 4 * w),
            (8 * w + 2 * nh, 8 * w + 3 * nh, 8 * w + 2 * nh)]


def _relayout_w_in(wg, nh, tr, name, rows_total=None, into=None):
    _, d, cols = wg.shape
    w = nh * HEAD_DIM
    rows_total = into.shape[0] if into is not None else rows_total or d
    first = (rows_total - d) // tr if into is not None else 0

    def native(ref, j0, j1):
        out = []
        while j0 < j1:
            blk = j0 // cols
            end = min(j1, (blk + 1) * cols)
            out.append(ref[blk, :, pl.ds(j0 - blk * cols, end - j0)])
            j0 = end
        return out

    def body(g_ref, *refs):
        o_ref = refs[-1]
        for cidx in range(8 * w // LANES):
            j0 = cidx * LANES + (0 if cidx * LANES < 4 * w else 2 * nh)
            pieces = native(g_ref, j0, j0 + LANES)
            o_ref[:, cidx * LANES:(cidx + 1) * LANES] = pieces[0] if len(pieces) == 1 else jnp.concatenate(pieces, axis=1)
        pieces = (native(g_ref, 4 * w, 4 * w + 2 * nh) + native(g_ref, 8 * w + 2 * nh, 8 * w + 3 * nh)
                  + [jnp.zeros((tr, NARROW - 3 * nh), wg.dtype)])
        o_ref[:, 8 * w:] = jnp.concatenate(pieces, axis=1)

    return pl.pallas_call(
        body, grid=(d // tr,), in_specs=[pl.BlockSpec((N_DEV, tr, cols), lambda i: (0, i, 0))] + [_ANY] * (into is not None),
        out_specs=pl.BlockSpec((tr, 8 * w + NARROW), lambda i: (first + i, 0)),
        out_shape=jax.ShapeDtypeStruct((rows_total, 8 * w + NARROW), wg.dtype),
        input_output_aliases={1: 0} if into is not None else {},
        name=name, compiler_params=_cp("parallel"))(wg, *([into] if into is not None else []))


def _adamw(w, parts, m, v, tr, name):
    r, c = w.shape
    n_parts = parts.shape[0]

    def body(w_ref, p_ref, m_ref, v_ref, g_ref, d_ref, nm_ref, nv_ref):
        g = p_ref[0].astype(F32)
        for s in range(1, n_parts):
            g = g + p_ref[s].astype(F32)
        m_new = ADAM_B1 * m_ref[...] + (1.0 - ADAM_B1) * g
        v_new = ADAM_B2 * v_ref[...] + (1.0 - ADAM_B2) * (g * g)
        m_hat = m_new / (1.0 - ADAM_B1 ** ADAM_STEP)
        v_hat = v_new / (1.0 - ADAM_B2 ** ADAM_STEP)
        g_ref[...] = g
        d_ref[...] = -ADAM_LR * (m_hat / (jnp.sqrt(v_hat) + ADAM_EPS) + ADAM_WD * w_ref[...])
        nm_ref[...] = m_new
        nv_ref[...] = v_new

    blk = pl.BlockSpec((tr, c), lambda i: (i, 0))
    return pl.pallas_call(
        body, grid=(r // tr,), in_specs=[blk, pl.BlockSpec((n_parts, tr, c), lambda i: (0, i, 0)), blk, blk],
        out_specs=[blk] * 4, out_shape=[jax.ShapeDtypeStruct((r, c), F32)] * 4, name=name,
        compiler_params=_cp("parallel"))(w, parts, m, v)


def _pack_small(d, pre, post, a_log, dt_bias, f_bias, gdn_w, fq_w, fk_w, extra):
    row2 = jnp.concatenate([a_log, dt_bias, f_bias, gdn_w, fq_w, fk_w, extra], axis=1)
    row2 = jnp.pad(row2, ((0, 0), (0, d - row2.shape[1])))
    return jnp.concatenate([pre, post, row2, jnp.zeros((5, d), F32)], axis=0)


def _unpack_small(p, nh):
    o = 3 * nh
    return dict(pre=p[0:1], post=p[1:2], a_log=p[2:3, 0:nh], dt_bias=p[2:3, nh:2 * nh], f_bias=p[2:3, 2 * nh:o],
                gdn_w=p[2:3, o:o + HEAD_DIM], fq_w=p[2:3, o + HEAD_DIM:o + 2 * HEAD_DIM],
                fk_w=p[2:3, o + 2 * HEAD_DIM:o + 3 * HEAD_DIM], extra=p[2, o + 3 * HEAD_DIM])


def kernel(x, meta_tokens, pre_norm_w, w_in, conv_w, a_log, dt_bias, gdn_norm_w, fox_q_norm_w, fox_k_norm_w, fox_f_bias, w_out, post_norm_w, loss_target, m_meta_tokens, m_pre_norm_w, m_w_in, m_conv_w, m_a_log, m_dt_bias, m_gdn_norm_w, m_fox_q_norm_w, m_fox_k_norm_w, m_fox_f_bias, m_w_out, m_post_norm_w, v_meta_tokens, v_pre_norm_w, v_w_in, v_conv_w, v_a_log, v_dt_bias, v_gdn_norm_w, v_fox_q_norm_w, v_fox_k_norm_w, v_fox_f_bias, v_w_out, v_post_norm_w):
    nh = a_log.shape[1]
    d = x.shape[-1]
    w = nh * HEAD_DIM
    zero = jnp.zeros((1, 1), F32)

    w_in_a, w_in_b = _cast_bf16_column_major(w_in, 2, "cast_w_in")

    def project(xn, got):
        wg, cg, mg = got
        half = (d // 2, 0), (d // 2, 1)
        wfull = _relayout_w_in(wg, nh, 256, "relayout_w_in_a", rows_total=d)
        proj, got = _matmul(xn, wfull, "nn", MM_TILE, F32, "proj_a", _gather_rider([w_in_b]), a_cols=half[0], b_rows=half[0])
        wfull = _relayout_w_in(got[0], nh, 256, "relayout_w_in_b", into=wfull)
        proj = _matmul(xn, wfull, "nn", MM_TILE, F32, "proj_b", a_cols=half[1], b_rows=half[1], acc=proj)
        return proj, wfull, cg.transpose(1, 0, 2).reshape(CONV_WIDTH, 3 * w), mg.transpose(1, 0, 2).reshape(N_META, d)

    late_weights = (_gather_rider([_cast_bf16(w_out[0], 256, "cast_w_out")]), lambda got: got[0].reshape(2 * w, d))
    core = lax.axis_index("c")
    dev = 4 * lax.axis_index("x") + 2 * lax.axis_index("y") + core
    core_arr = jnp.reshape(core, (1,)).astype(jnp.int32)

    out_parts = lambda dw_out: dw_out.reshape(N_DEV, 2 * w // N_DEV, d)
    g = _layer_grads(
        x[0], loss_target[0], None, pre_norm_w, (_gather_rider([w_in_a, conv_w[0].T, meta_tokens]), 2, project), None,
        a_log, dt_bias, gdn_norm_w,
        fox_q_norm_w, fox_k_norm_w, fox_f_bias, None, post_norm_w, late_weights=late_weights,
        w_out_grads=(lambda dw_out: _pair_rider(parts=out_parts(dw_out)),
                     lambda dw_out, got: _chip_rider(_pair_sum(out_parts(dw_out), got[0], core_arr, 256, "pair_sum_w_out"))))
    p_out = g["w_out_parts"][0]
    xn, dproj, wfull, meta_full = g["xn"], g["dproj"], g["wfull"], g["meta"]
    flights, token, dw, rider = [], None, None, None

    def exchange(dw, got, i):
        sums = _relayout_pair_sum(dw, got[0], got[1], core_arr, nh, 128, f"relayout_pair_sum_{i}")
        flight, token = _chip_exchange_start(sums, f"chip_exchange_start_{i}")
        flights.append(flight)
        return token

    for i, (index, parts) in enumerate(DW_IN_PIECES):
        res = _matmul(xn, dproj, "tn", MM_TILE, BF16, f"dw_in_{i}", rider, a_cols=(d // parts, index))
        if i:
            token = exchange(dw, res[1], i - 1)
        dw = res[0] if i else res
        rider = _pair_rider(dw_rows=dw, nh=nh, after=token)
    dxn, (got,) = _dxn(dproj, wfull, [rider], MM_TILE, "dxn")
    token = exchange(dw, got, len(DW_IN_PIECES) - 1)
    (grad_x, dmeta, dpre_w), _ = _prenorm_bwd(dxn, x[0], meta_full, pre_norm_w + token[0:1, 0:1], g["dy"])
    small = _pack_small(d, dpre_w, g["post_w"], g["a_log"], g["dt_bias"], g["f_bias"], g["gdn_norm_w"], g["fq_w"],
                        g["fk_w"], g["loss"])
    a_conv, a_meta, p_small = _all_gather([g["conv_wt"], dmeta, small], "gather_small_grads")
    p_conv = lax.dynamic_slice_in_dim(a_conv, dev * conv_w.shape[1], conv_w.shape[1], axis=2).transpose(0, 2, 1)
    p_meta = lax.dynamic_slice_in_dim(a_meta, dev * meta_tokens.shape[1], meta_tokens.shape[1], axis=2)

    r_out = _adamw(w_out[0], p_out, m_w_out[0], v_w_out[0], 64, "adamw_w_out")
    r_conv = _adamw(conv_w[0], p_conv, m_conv_w[0], v_conv_w[0], conv_w.shape[1], "adamw_conv_w")
    r_meta = _adamw(meta_tokens, p_meta, m_meta_tokens, v_meta_tokens, N_META, "adamw_meta")
    pk = lambda pre, post, a, dt, gw, fq, fk, fb: _pack_small(d, pre, post, a, dt, fb, gw, fq, fk, zero)
    r_small = _adamw(
        pk(pre_norm_w, post_norm_w, a_log, dt_bias, gdn_norm_w, fox_q_norm_w, fox_k_norm_w, fox_f_bias), p_small,
        pk(m_pre_norm_w, m_post_norm_w, m_a_log, m_dt_bias, m_gdn_norm_w, m_fox_q_norm_w, m_fox_k_norm_w, m_fox_f_bias),
        pk(v_pre_norm_w, v_post_norm_w, v_a_log, v_dt_bias, v_gdn_norm_w, v_fox_q_norm_w, v_fox_k_norm_w, v_fox_f_bias),
        8, "adamw_small")
    p_in = _chip_exchange_wait(flights, r_small[0], "chip_exchange_wait")
    r_in = _adamw_column_major(w_in, p_in, m_w_in, v_w_in, "adamw_w_in")

    sm = [_unpack_small(r, nh) for r in r_small]
    outs = []
    for i in range(4):
        s = sm[i]
        outs += [r_meta[i], s["pre"], r_in[i], r_conv[i][None], s["a_log"], s["dt_bias"], s["gdn_w"], s["fq_w"],
                 s["fk_w"], s["f_bias"], r_out[i][None], s["post"]]
    return (sm[0]["extra"], grad_x[None], *outs)
```

```python
import jax
import jax.numpy as jnp
from jax import lax
from jax.experimental import pallas as pl
from jax.experimental.pallas import tpu as pltpu

F32, BF16 = jnp.float32, jnp.bfloat16
HEAD_DIM = 128
N_META = 16
CONV_WIDTH = 4
CHUNK = 128
Q_BLOCK = 128
LANES = 128
EPS = 1e-6
PAD_ROWS = Q_BLOCK - N_META
N_DEV = 8
N_CHIP = 4
VMEM_LIMIT = 56 * 1024 * 1024
NEG = -1e30
NARROW = 2 * LANES
MM_TILE = 6 * LANES
DW_IN_PIECES = ((0, 2), (2, 4), (3, 4))

ADAM_LR, ADAM_B1, ADAM_B2, ADAM_EPS, ADAM_WD, ADAM_STEP = 0.001, 0.9, 0.999, 1e-08, 0.01, 10

_DN = {"nn": (((1,), (0,)), ((), ())), "nt": (((1,), (1,)), ((), ())), "tn": (((0,), (0,)), ((), ()))}
_DN3 = {"nn": (((2,), (1,)), ((0,), (0,))), "nt": (((2,), (2,)), ((0,), (0,))), "tn": (((1,), (1,)), ((0,), (0,)))}
_ANY = pl.BlockSpec(memory_space=pl.ANY)
_MESH = pl.DeviceIdType.MESH


def _cp(*sem):
    return pltpu.CompilerParams(dimension_semantics=sem, vmem_limit_bytes=VMEM_LIMIT)


def _dot(a, b, dims="nn", prec=None):
    return lax.dot_general(a, b, _DN[dims], precision=prec, preferred_element_type=F32)


def _bdot(a, b, dims="nn"):
    return _dot(a.astype(BF16), b.astype(BF16), dims)


def _hdot(a, b, dims="nn"):
    return _dot(a, b, dims, prec=lax.Precision.HIGHEST)


def _dot3(a, b, dims="nn"):
    return lax.dot_general(a, b, _DN3[dims], preferred_element_type=F32)


def _bdot3(a, b, dims="nn"):
    return _dot3(a.astype(BF16), b.astype(BF16), dims)


def _split(a):
    hi = a.astype(BF16)
    return hi, (a - hi.astype(F32)).astype(BF16)


def _iota(shape, dim):
    return lax.broadcasted_iota(jnp.int32, shape, dim)


def _sigmoid(z):
    return 1.0 / (1.0 + jnp.exp(-z))


def _softplus(z):
    e = jnp.exp(-jnp.abs(z))
    u = 1.0 + e
    l1p = jnp.where(u == 1.0, e, jnp.log(u) * (e / jnp.where(u == 1.0, 1.0, u - 1.0)))
    return jnp.maximum(z, 0.0) + l1p


def _silu_and_grad(z):
    s = _sigmoid(z)
    return z * s, s * (1.0 + z * (1.0 - s))


def _rms(x):
    return lax.rsqrt(jnp.mean(x * x, axis=-1, keepdims=True) + EPS)


def _h_tile(i, x_ref, meta_ref):
    first = jnp.concatenate([jnp.zeros((PAD_ROWS, x_ref.shape[1]), F32), meta_ref[...]], axis=0)
    return jnp.where(i == 0, first, x_ref[...])


def _x_rows(d):
    return pl.BlockSpec((Q_BLOCK, d), lambda i: (jnp.maximum(i - 1, 0), 0))


def _prenorm(x, meta, w):
    seq, d = x.shape
    lp = seq + Q_BLOCK

    def body(x_ref, m_ref, w_ref, o_ref):
        h = _h_tile(pl.program_id(0), x_ref, m_ref)
        o_ref[...] = (h * _rms(h) * w_ref[...]).astype(BF16)

    return pl.pallas_call(
        body, grid=(lp // Q_BLOCK,),
        in_specs=[_x_rows(d), pl.BlockSpec((N_META, d), lambda i: (0, 0)), pl.BlockSpec((1, d), lambda i: (0, 0))],
        out_specs=pl.BlockSpec((Q_BLOCK, d), lambda i: (i, 0)),
        out_shape=jax.ShapeDtypeStruct((lp, d), BF16), name="prenorm", compiler_params=_cp("parallel"))(x, meta, w)


def _prenorm_gathering(x, w, rider, meta_at):
    seq, d = x.shape
    steps = seq // Q_BLOCK + 1

    def body(start, finish, x_ref, w_ref, o_ref, meta_buf, meta_sem):
        i = pl.program_id(0)
        pl.when(i == 0)(start)

        @pl.when(i < steps - 1)
        def _():
            h = x_ref[...]
            o_ref[...] = (h * _rms(h) * w_ref[...]).astype(BF16)

        @pl.when(i == steps - 1)
        def _():
            finish()
            cp = pltpu.make_async_copy(finish.results[0][meta_at], meta_buf, meta_sem)
            cp.start()
            cp.wait()
            h = jnp.concatenate([jnp.zeros((PAD_ROWS, d), F32), jnp.concatenate([meta_buf[s] for s in range(N_DEV)], axis=1)], axis=0)
            o_ref[...] = (h * _rms(h) * w_ref[...]).astype(BF16)

    (xn,), got = _hosted_call(
        body, [rider], 2, 1, 2, grid=(steps,),
        in_specs=[pl.BlockSpec((Q_BLOCK, d), lambda i: (jnp.minimum(i, steps - 2), 0)), pl.BlockSpec((1, d), lambda i: (0, 0))],
        out_specs=[pl.BlockSpec((Q_BLOCK, d), lambda i: ((i + 1) % steps, 0))],
        out_shape=[jax.ShapeDtypeStruct((seq + Q_BLOCK, d), BF16)],
        scratch_shapes=[pltpu.VMEM((N_DEV, N_META, d // N_DEV), F32), pltpu.SemaphoreType.DMA(())],
        name="prenorm", compiler_params=_cp("arbitrary"))(x, w)
    return xn, got[0]


def _tile(n, want):
    return max(t for t in range(LANES, want + 1, LANES) if n % t == 0)


class _Rider:
    def __init__(self, inputs, out_shapes, scratch, aliases, make):
        self.inputs, self.out_shapes, self.scratch, self.aliases, self.make = inputs, out_shapes, scratch, aliases, make


def _hosted_call(body, riders, n_in, n_out, n_scratch, *, in_specs, out_specs, out_shape, scratch_shapes=(), aliases=None,
                 **kw):
    riders = [r for r in riders if r is not None]
    r_in = [len(r.inputs) for r in riders]
    r_out = [len(r.out_shapes) for r in riders]
    r_scr = [len(r.scratch) for r in riders]
    al = dict(aliases or {})
    for k, r in enumerate(riders):
        al.update({n_in + sum(r_in[:k]) + i: n_out + sum(r_out[:k]) + o for i, o in r.aliases.items()})

    def full_body(*refs):
        ins, rest = refs[:n_in + sum(r_in)], refs[n_in + sum(r_in):]
        outs, scr = rest[:n_out + sum(r_out)], rest[n_out + sum(r_out):]
        hooks = [r.make(ins[n_in + sum(r_in[:k]):n_in + sum(r_in[:k + 1])], outs[n_out + sum(r_out[:k]):n_out + sum(r_out[:k + 1])],
                        scr[n_scratch + sum(r_scr[:k]):n_scratch + sum(r_scr[:k + 1])]) for k, r in enumerate(riders)]

        def start():
            for h in hooks:
                h[0]()

        def finish():
            for h in hooks:
                h[1]()

        finish.results = [outs[n_out + sum(r_out[:k]):n_out + sum(r_out[:k + 1])] for k in range(len(riders))]
        body(start, finish, *ins[:n_in], *outs[:n_out], *scr[:n_scratch])

    call = pl.pallas_call(
        full_body, in_specs=list(in_specs) + [_ANY] * sum(r_in), out_specs=list(out_specs) + [_ANY] * sum(r_out),
        out_shape=list(out_shape) + [s for r in riders for s in r.out_shapes],
        scratch_shapes=list(scratch_shapes) + [s for r in riders for s in r.scratch], input_output_aliases=al, **kw)

    def run(*args):
        res = call(*args, *[t for r in riders for t in r.inputs])
        return res[:n_out], [res[n_out + sum(r_out[:k]):n_out + sum(r_out[:k + 1])] for k in range(len(riders))]

    return run


def _matmul(a, b, dims, tn, out_dtype, name, rider=None, a_cols=None, b_rows=None, acc=None):
    a_shape = a.shape if a_cols is None else (a.shape[0], a_cols[0])
    a_index = 0 if a_cols is None else a_cols[1]
    m = a_shape[1] if dims == "tn" else a_shape[0]
    n = b.shape[0] if dims == "nt" else b.shape[1]
    kdim = b.shape[1] if dims == "nt" else b.shape[0] if b_rows is None else b_rows[0]
    b_index = 0 if b_rows is None else b_rows[1]
    tn = _tile(n, tn)
    steps = n // tn
    b_spec = pl.BlockSpec((tn, kdim), lambda j: (j, 0)) if dims == "nt" else pl.BlockSpec((kdim, tn), lambda j: (b_index, j))
    o_spec = pl.BlockSpec((m, tn), lambda j: (0, j))

    def body(start, finish, a_ref, b_ref, *refs):
        pl.when(pl.program_id(0) == 0)(start)
        prod = _dot(a_ref[...], b_ref[...], dims)
        refs[-1][...] = (prod if acc is None else prod + refs[0][...]).astype(out_dtype)
        pl.when(pl.program_id(0) == steps - 1)(finish)

    (out,), got = _hosted_call(
        body, [rider], 2 + (acc is not None), 1, 0, grid=(steps,),
        in_specs=[pl.BlockSpec(a_shape, lambda j: (0, a_index)), b_spec] + [o_spec] * (acc is not None),
        out_specs=[o_spec], out_shape=[jax.ShapeDtypeStruct((m, n), out_dtype)], aliases={2: 0} if acc is not None else None,
        name=name, compiler_params=_cp("parallel" if rider is None else "arbitrary"))(a, b, *([acc] if acc is not None else []))
    return out if rider is None else (out, got[0])


def _chip_rider(sums):
    def make(ins, outs, scratch):
        local, remote = _chip_exchange_copies(ins[0], outs[0], *scratch)

        def start():
            for cp in [local] + remote:
                cp.start()

        def finish():
            local.wait()
            for cp in remote:
                cp.wait_send()
                cp.wait_recv()

        return start, finish

    return _Rider([sums], [jax.ShapeDtypeStruct(sums.shape, sums.dtype)],
                  [pltpu.SemaphoreType.DMA((N_CHIP - 1,)), pltpu.SemaphoreType.DMA((N_CHIP - 1,)), pltpu.SemaphoreType.DMA((1,))],
                  {}, make)


_HBM = pl.BlockSpec(memory_space=pltpu.HBM)
_SEM = pl.BlockSpec(memory_space=pltpu.SEMAPHORE)


def _chip_exchange_copies(sums_ref, land_ref, send_sems, recv_sems, local_sem):
    x, y, core = lax.axis_index("x"), lax.axis_index("y"), lax.axis_index("c")
    mine = 2 * x + y
    local = pltpu.make_async_copy(sums_ref.at[mine], land_ref.at[mine], local_sem.at[0])
    remote = []
    for k in range(1, N_CHIP):
        px = 1 - x if k & 2 else x
        py = 1 - y if k & 1 else y
        remote.append(pltpu.make_async_remote_copy(
            src_ref=sums_ref.at[2 * px + py], dst_ref=land_ref.at[mine], send_sem=send_sems.at[k - 1],
            recv_sem=recv_sems.at[k - 1], device_id=(px, py, core), device_id_type=_MESH))
    return local, remote


def _chip_exchange_start(sums, name):
    def body(s_ref, send_sems, recv_sems, local_sem, s_thru, land_ref, token):
        local, remote = _chip_exchange_copies(s_ref, land_ref, send_sems, recv_sems, local_sem)
        for cp in [local] + remote:
            cp.start()
        token[...] = jnp.zeros_like(token)

    *flight, token = pl.pallas_call(
        body, name=name,
        out_shape=(pltpu.SemaphoreType.DMA((N_CHIP - 1,)), pltpu.SemaphoreType.DMA((N_CHIP - 1,)), pltpu.SemaphoreType.DMA((1,)),
                   pltpu.HBM(sums.shape, sums.dtype), pltpu.HBM(sums.shape, sums.dtype), jax.ShapeDtypeStruct((8, LANES), F32)),
        in_specs=(_HBM,), out_specs=(_SEM, _SEM, _SEM, _HBM, _HBM, pl.BlockSpec(memory_space=pltpu.VMEM)),
        input_output_aliases={0: 3},
        compiler_params=pltpu.CompilerParams(has_side_effects=pltpu.SideEffectType.DATAFLOW_SIDE_EFFECTING))(
            pltpu.with_memory_space_constraint(sums, pltpu.HBM))
    return flight, token


def _chip_exchange_wait(flights, after, name):
    n = len(flights)

    def body(*refs):
        for i in range(n):
            s_ref, land_ref = refs[2 * i:2 * i + 2]
            local, remote = _chip_exchange_copies(s_ref, land_ref, *refs[2 * n + 3 * i:2 * n + 3 * i + 3])
            local.wait()
            for cp in remote:
                cp.wait_send()
                cp.wait_recv()

    buffers = [b for f in flights for b in f[3:]]
    res = pl.pallas_call(
        body, name=name, out_shape=tuple(pltpu.HBM(b.shape, b.dtype) for b in buffers),
        in_specs=(_HBM,) * (2 * n) + (_SEM,) * (3 * n) + (_ANY,), out_specs=(_HBM,) * (2 * n),
        input_output_aliases={i: i for i in range(2 * n)},
        compiler_params=pltpu.CompilerParams(has_side_effects=pltpu.SideEffectType.DATAFLOW_SIDE_EFFECTING))(
            *buffers, *[s for f in flights for s in f[:3]], after)
    return res[1::2]


def _dxn(dproj, wfull, riders, tk, name):
    m, k = dproj.shape
    n = wfull.shape[0]
    tk = _tile(k, tk)
    steps = k // tk

    def body(start, finish, a_ref, b_ref, o_ref):
        j = pl.program_id(0)

        @pl.when(j == 0)
        def _():
            start()
            o_ref[...] = jnp.zeros_like(o_ref)
        o_ref[...] += _dot(a_ref[...], b_ref[...], "nt")
        pl.when(j == steps - 1)(finish)

    (dxn,), got = _hosted_call(
        body, riders, 2, 1, 0, grid=(steps,),
        in_specs=[pl.BlockSpec((m, tk), lambda j: (0, j)), pl.BlockSpec((n, tk), lambda j: (0, j))],
        out_specs=[pl.BlockSpec((m, n), lambda j: (0, 0))], out_shape=[jax.ShapeDtypeStruct((m, n), F32)],
        name=name, compiler_params=_cp("arbitrary"))(dproj, wfull)
    return dxn, got


def _conv_taps(x, w):
    c = x * w[CONV_WIDTH - 1:CONV_WIDTH, :]
    for j in range(CONV_WIDTH - 1):
        c = c + pltpu.roll(x, CONV_WIDTH - 1 - j, 0) * w[j:j + 1, :]
    return c


def _gdn_prep(proj, conv_wt, nh):
    lp = proj.shape[0]
    scale = HEAD_DIM ** -0.5

    def body(x_ref, w_ref, o_ref):
        which = pl.program_id(0) // nh
        c = _conv_taps(x_ref[...], w_ref[...])
        s = c * _sigmoid(c)
        r = lax.rsqrt(jnp.sum(s * s, axis=-1, keepdims=True) + EPS)
        f = jnp.where(which == 0, r * scale, jnp.where(which == 1, r, 1.0))
        o_ref[...] = jnp.where(_iota(s.shape, 0) >= PAD_ROWS, s * f, 0.0)

    return pl.pallas_call(
        body, grid=(3 * nh,),
        in_specs=[pl.BlockSpec((lp, LANES), lambda s: (0, s)), pl.BlockSpec((CONV_WIDTH, LANES), lambda s: (0, s))],
        out_specs=pl.BlockSpec((lp, LANES), lambda s: (0, s)),
        out_shape=jax.ShapeDtypeStruct((lp, 3 * nh * HEAD_DIM), F32), name="gdn_prep",
        compiler_params=_cp("parallel"))(proj, conv_wt)


def _gdn_prep_bwd(proj, conv_wt, dq, dk, dv, dproj, nh):
    lp = proj.shape[0]
    scale = HEAD_DIM ** -0.5
    part = lambda p: pl.BlockSpec((lp, LANES), lambda s: (0, jnp.clip(s - p * nh, 0, nh - 1)))

    def body(x_ref, w_ref, dq_ref, dk_ref, dv_ref, _, dx_ref, dw_ref):
        which = pl.program_id(0) // nh
        x = x_ref[...]
        w = w_ref[...]
        c = _conv_taps(x, w)
        sg = _sigmoid(c)
        s = c * sg
        r = lax.rsqrt(jnp.sum(s * s, axis=-1, keepdims=True) + EPS)
        dy = jnp.where(which == 0, dq_ref[...], jnp.where(which == 1, dk_ref[...], dv_ref[...]))
        dy = jnp.where(_iota(s.shape, 0) >= PAD_ROWS, dy, 0.0)
        y0 = s * r
        dy0 = dy * jnp.where(which == 0, scale, 1.0)
        ds_n = r * (dy0 - y0 * jnp.sum(dy0 * y0, axis=-1, keepdims=True))
        ds = jnp.where(which == 2, dy, ds_n)
        dc = ds * (sg * (1.0 + c * (1.0 - sg)))
        dx = dc * w[CONV_WIDTH - 1:CONV_WIDTH, :]
        rows = [jnp.sum(dc * x, axis=0, keepdims=True)]
        for j in range(CONV_WIDTH - 2, -1, -1):
            sh = CONV_WIDTH - 1 - j
            dx = dx + pltpu.roll(dc, lp - sh, 0) * w[j:j + 1, :]
            rows.insert(0, jnp.sum(dc * pltpu.roll(x, sh, 0), axis=0, keepdims=True))
        dx_ref[...] = dx.astype(BF16)
        dw_ref[...] = jnp.concatenate(rows, axis=0)

    strip = pl.BlockSpec((lp, LANES), lambda s: (0, s))
    taps = pl.BlockSpec((CONV_WIDTH, LANES), lambda s: (0, s))
    return pl.pallas_call(
        body, grid=(3 * nh,), in_specs=[strip, taps, part(0), part(1), part(2), _ANY], out_specs=[strip, taps],
        out_shape=[jax.ShapeDtypeStruct(dproj.shape, BF16), jax.ShapeDtypeStruct((CONV_WIDTH, 3 * nh * HEAD_DIM), F32)],
        input_output_aliases={5: 0}, name="gdn_prep_bwd", compiler_params=_cp("parallel"))(proj, conv_wt, dq, dk, dv, dproj)


def _gates(proj, bias_row, nega_row, nh):
    lp = proj.shape[0]
    nc = lp // CHUNK

    def body(p_ref, b_ref, a_ref, g_ref, gt3_ref, gtf_ref):
        lane = _iota((CHUNK, LANES), 1)
        tri = (_iota((CHUNK, CHUNK), 0) >= _iota((CHUNK, CHUNK), 1)).astype(F32)

        def step(n, carry):
            r0 = pl.multiple_of(n * CHUNK, CHUNK)
            z = p_ref[pl.ds(r0, CHUNK), :] + b_ref[...]
            base = jnp.where(lane < nh, _sigmoid(z),
                             jnp.where(lane < 2 * nh, a_ref[...] * _softplus(z),
                                       jnp.where(lane < 3 * nh, -_softplus(-z), 0.0)))
            base = jnp.where(r0 + _iota((CHUNK, LANES), 0) >= PAD_ROWS, base, 0.0)
            cs = _hdot(tri, base)
            run = jnp.where((lane >= 2 * nh) & (lane < 3 * nh), cs + carry, cs)
            sh = pltpu.roll(run, 2 * nh, 1)
            out = base + jnp.where((lane >= 3 * nh) & (lane < 5 * nh), sh, 0.0)
            g_ref[pl.ds(r0, CHUNK), :] = out
            gt3_ref[n] = out.T
            return carry + cs[CHUNK - 1:CHUNK, :]

        lax.fori_loop(0, nc, step, jnp.zeros((1, LANES), F32))
        gtf_ref[...] = g_ref[...].T

    vec = pl.BlockSpec((1, LANES), lambda i: (0, 0))
    return pl.pallas_call(
        body, grid=(1,), in_specs=[pl.BlockSpec((lp, LANES), lambda i: (0, 8 * nh)), vec, vec],
        out_specs=[pl.BlockSpec((lp, LANES), lambda i: (0, 0)), pl.BlockSpec((nc, LANES, CHUNK), lambda i: (0, 0, 0)),
                   pl.BlockSpec((LANES, lp), lambda i: (0, 0))],
        out_shape=[jax.ShapeDtypeStruct((lp, LANES), F32), jax.ShapeDtypeStruct((nc, LANES, CHUNK), F32),
                   jax.ShapeDtypeStruct((LANES, lp), F32)],
        name="gates", compiler_params=_cp("arbitrary"))(proj, bias_row, nega_row)


def _gates_bwd(proj, bias_row, nega_row, gates, dgate_gdn, dc_t, dproj, nh):
    lp = proj.shape[0]
    nc = lp // CHUNK

    def body(p_ref, b_ref, a_ref, g_ref, dg_ref, dc_ref, _, dz_ref, sm_ref, dct_scr):
        lane = _iota((CHUNK, LANES), 1)
        triu = (_iota((CHUNK, CHUNK), 0) <= _iota((CHUNK, CHUNK), 1)).astype(F32)
        dct_scr[...] = dc_ref[...].T
        sm_ref[...] = jnp.zeros_like(sm_ref)
        dz_ref[:, LANES:] = jnp.zeros((lp, NARROW - LANES), BF16)

        def step(i, carry):
            n = nc - 1 - i
            r0 = pl.multiple_of(n * CHUNK, CHUNK)
            z = p_ref[pl.ds(r0, CHUNK), :] + b_ref[...]
            gt = g_ref[pl.ds(r0, CHUNK), :]
            dgd = dg_ref[pl.ds(r0, CHUNK), :]
            dch = dct_scr[pl.ds(r0, CHUNK), :]
            rc = _hdot(triu, dch) + carry
            sg = _sigmoid(z)
            dz = jnp.where(lane < nh, dgd * sg * (1.0 - sg),
                           jnp.where(lane < 2 * nh, dgd * a_ref[...] * sg,
                                     jnp.where(lane < 3 * nh, rc * (1.0 - sg), 0.0)))
            dz = jnp.where(r0 + _iota((CHUNK, LANES), 0) >= PAD_ROWS, dz, 0.0)
            dz_ref[pl.ds(r0, CHUNK), 0:LANES] = dz.astype(BF16)
            sm_ref[0:1, :] += jnp.sum(dz, axis=0, keepdims=True)
            sm_ref[1:2, :] += jnp.sum(jnp.where((lane >= nh) & (lane < 2 * nh), dgd * gt, 0.0), axis=0, keepdims=True)
            return carry + jnp.sum(dch, axis=0, keepdims=True)

        lax.fori_loop(0, nc, step, jnp.zeros((1, LANES), F32))

    vec = pl.BlockSpec((1, LANES), lambda i: (0, 0))
    full = pl.BlockSpec((lp, LANES), lambda i: (0, 0))
    last = pl.BlockSpec((lp, LANES), lambda i: (0, 8 * nh))
    tail = pl.BlockSpec((lp, NARROW), lambda i: (0, 8 * nh * LANES // NARROW))
    return pl.pallas_call(
        body, grid=(1,), in_specs=[last, vec, vec, full, full, pl.BlockSpec((LANES, lp), lambda i: (0, 0)), _ANY],
        out_specs=[tail, pl.BlockSpec((8, LANES), lambda i: (0, 0))],
        out_shape=[jax.ShapeDtypeStruct(dproj.shape, BF16), jax.ShapeDtypeStruct((8, LANES), F32)],
        scratch_shapes=[pltpu.VMEM((lp, LANES), F32)], input_output_aliases={6: 0},
        name="gates_bwd", compiler_params=_cp("arbitrary"))(proj, bias_row, nega_row, gates, dgate_gdn, dc_t, dproj)


def _tri_inv(a):
    t = jnp.where(_iota(a.shape, 1) == _iota(a.shape, 2), 1.0, 0.0) - a
    p = a
    for _ in range(CHUNK.bit_length() - 2):
        ph, pw = _split(p)
        p = _dot3(ph, ph) + (_dot3(ph, pw) + _dot3(pw, ph))
        ph, pw = _split(p)
        th, tw = _split(t)
        t = t + (_dot3(th, ph) + (_dot3(th, pw) + _dot3(tw, ph)))
    return t


def _gdn_chunk(q, k, v, beta, gc, gr, t=None):
    ii, jj = _iota((1, CHUNK, CHUNK), 1), _iota((1, CHUNK, CHUNK), 2)
    causal, strict = ii >= jj, ii > jj
    dm = jnp.where(causal, jnp.exp(jnp.where(causal, gc - gr, 0.0)), 0.0)
    kk = _bdot3(k, k, "nt")
    a = jnp.where(strict, beta * kk * dm, 0.0)
    if t is None:
        t = _tri_inv(a)
    eg = jnp.exp(gc)
    glast = gc[:, CHUNK - 1:CHUNK, :]
    ekd = jnp.exp(glast - gc)
    bv = beta * v
    bk = (beta * eg) * k
    ub = _bdot3(t, jnp.concatenate([bv, bk], axis=2))
    qk = _bdot3(q, k, "nt")
    return dict(causal=causal, strict=strict, dm=dm, kk=kk, a=a, t=t, eg=eg, ekd=ekd, bv=bv, bk=bk,
                u=ub[:, :, :HEAD_DIM], w=ub[:, :, HEAD_DIM:], qk=qk, aqk=jnp.where(causal, qk * dm, 0.0),
                q_dec=q * eg, k_dec=k * ekd, decay=jnp.exp(glast))


def _heads(ref, nh):
    return jnp.stack([ref[:, h * HEAD_DIM:(h + 1) * HEAD_DIM] for h in range(nh)], axis=0)


def _gdn_chunk_inputs(q_ref, k_ref, v_ref, g, gt, nh):
    col = lambda o: jnp.stack([g[:, o + h:o + h + 1] for h in range(nh)], axis=0)
    gr = jnp.stack([gt[3 * nh + h:3 * nh + h + 1, :] for h in range(nh)], axis=0)
    return _heads(q_ref, nh), _heads(k_ref, nh), _heads(v_ref, nh), col(0), col(3 * nh), gr


def _gdn_fwd(qkv, gates, gt3, nh, rider=None):
    lp = qkv.shape[0]
    nc = lp // CHUNK
    w = nh * HEAD_DIM

    def body(start, finish, q_ref, k_ref, v_ref, g_ref, gt_ref, o_ref, sall_ref, tall_ref, s_scr):
        @pl.when(pl.program_id(0) == 0)
        def _():
            start()
            s_scr[...] = jnp.zeros_like(s_scr)
        c = _gdn_chunk(*_gdn_chunk_inputs(q_ref, k_ref, v_ref, g_ref[...], gt_ref[0], nh))
        s = s_scr[...]
        sall_ref[0] = s
        tall_ref[0] = c["t"]
        v_new = c["u"] - _bdot3(c["w"], s)
        o = _bdot3(c["q_dec"], s) + _bdot3(c["aqk"], v_new)
        s_scr[...] = s * c["decay"] + _bdot3(c["k_dec"], v_new, "tn")
        for h in range(nh):
            o_ref[:, h * HEAD_DIM:(h + 1) * HEAD_DIM] = o[h]
        pl.when(pl.program_id(0) == nc - 1)(finish)

    outs, got = _hosted_call(
        body, [rider], 5, 3, 1, grid=(nc,),
        in_specs=[pl.BlockSpec((CHUNK, w), lambda n: (n, 0)), pl.BlockSpec((CHUNK, w), lambda n: (n, 1)),
                  pl.BlockSpec((CHUNK, w), lambda n: (n, 2)), pl.BlockSpec((CHUNK, LANES), lambda n: (n, 0)),
                  pl.BlockSpec((1, LANES, CHUNK), lambda n: (n, 0, 0))],
        out_specs=[pl.BlockSpec((CHUNK, w), lambda n: (n, 0)),
                   pl.BlockSpec((1, nh, HEAD_DIM, HEAD_DIM), lambda n: (n, 0, 0, 0)),
                   pl.BlockSpec((1, nh, CHUNK, CHUNK), lambda n: (n, 0, 0, 0))],
        out_shape=[jax.ShapeDtypeStruct((lp, w), F32), jax.ShapeDtypeStruct((nc, nh, HEAD_DIM, HEAD_DIM), F32),
                   jax.ShapeDtypeStruct((nc, nh, CHUNK, CHUNK), F32)],
        scratch_shapes=[pltpu.VMEM((nh, HEAD_DIM, HEAD_DIM), F32)],
        name="gdn_fwd", compiler_params=_cp("arbitrary"))(qkv, qkv, qkv, gates, gt3)
    return outs, (got[0] if got else None)


def _gdn_bwd(qkv, gates, gt3, s_all, t_all, do, nh, rider=None):
    lp = qkv.shape[0]
    nc = lp // CHUNK
    w = nh * HEAD_DIM
    rev = lambda n: nc - 1 - n

    def body(start, finish, q_ref, k_ref, v_ref, g_ref, gt_ref, s_ref, t_ref, do_ref, dq_ref, dk_ref, dv_ref, dg_ref, ds_scr):
        @pl.when(pl.program_id(0) == 0)
        def _():
            start()
            ds_scr[...] = jnp.zeros_like(ds_scr)
        q, k, v, beta, gc, gr = _gdn_chunk_inputs(q_ref, k_ref, v_ref, g_ref[...], gt_ref[0], nh)
        c = _gdn_chunk(q, k, v, beta, gc, gr, t_ref[0])
        s = s_ref[0]
        dsn = ds_scr[...]
        dout = _heads(do_ref, nh)
        v_new = c["u"] - _bdot3(c["w"], s)
        dq_dec = _bdot3(dout, s, "nt")
        daqk = jnp.where(c["causal"], _bdot3(dout, v_new, "nt"), 0.0)
        dv_new = _bdot3(c["aqk"], dout, "tn") + _bdot3(c["k_dec"], dsn)
        dk_dec = _bdot3(v_new, dsn, "nt")
        ddecay = jnp.sum(jnp.sum(dsn * s, axis=2, keepdims=True), axis=1, keepdims=True)
        dw = -_bdot3(dv_new, s, "nt")
        ds_scr[...] = _bdot3(c["q_dec"], dout, "tn") + c["decay"] * dsn - _bdot3(c["w"], dv_new, "tn")
        duw = jnp.concatenate([dv_new, dw], axis=2)
        dt = _bdot3(duw, jnp.concatenate([c["bv"], c["bk"]], axis=2), "nt")
        dbvk = _bdot3(c["t"], duw, "tn")
        dbv, dbk = dbvk[:, :, :HEAD_DIM], dbvk[:, :, HEAD_DIM:]
        da = jnp.where(c["strict"], -_bdot3(_bdot3(c["t"], dt, "tn"), c["t"], "nt"), 0.0)
        dkk = da * beta * c["dm"]
        dqk = daqk * c["dm"]
        e = da * c["a"] + daqk * c["aqk"]
        dq = dq_dec * c["eg"] + _bdot3(dqk, k)
        dk = (dk_dec * c["ekd"] + _bdot3(dkk, k) + _bdot3(dkk, k, "tn") + _bdot3(dqk, q, "tn")
              + (beta * c["eg"]) * dbk)
        dv = beta * dbv
        rs = lambda x: jnp.sum(x, axis=2, keepdims=True)
        dbeta = rs(dbv * v) + c["eg"] * rs(dbk * k) + rs(da * c["kk"] * c["dm"])
        kd_term = rs(dk_dec * c["k_dec"])
        eh, ew = _split(e)
        ones = jnp.ones((nh, CHUNK, LANES), BF16)
        col_sums = (_dot3(eh, ones, "tn") + _dot3(ew, ones, "tn"))[:, :, 0:1]
        dg_cum = rs(dq_dec * c["q_dec"]) - kd_term + rs(dbk * c["bk"]) + rs(e) - col_sums
        last = jnp.sum(kd_term, axis=1, keepdims=True) + ddecay * c["decay"]
        dg_cum = dg_cum + jnp.where(_iota((1, CHUNK, 1), 1) == CHUNK - 1, last, 0.0)
        lane = _iota((CHUNK, LANES), 1)
        acc = jnp.zeros((CHUNK, LANES), F32)
        for h in range(nh):
            sl = slice(h * HEAD_DIM, (h + 1) * HEAD_DIM)
            dq_ref[:, sl] = dq[h]
            dk_ref[:, sl] = dk[h]
            dv_ref[:, sl] = dv[h]
            acc = acc + jnp.where(lane == h, dbeta[h], 0.0) + jnp.where(lane == nh + h, dg_cum[h], 0.0)
        triu = (_iota((CHUNK, CHUNK), 0) <= _iota((CHUNK, CHUNK), 1)).astype(F32)
        dg_ref[...] = jnp.where(lane < nh, acc, _hdot(triu, acc))
        pl.when(pl.program_id(0) == nc - 1)(finish)

    outs, got = _hosted_call(
        body, [rider], 8, 4, 1, grid=(nc,),
        in_specs=[pl.BlockSpec((CHUNK, w), lambda n: (rev(n), 0)), pl.BlockSpec((CHUNK, w), lambda n: (rev(n), 1)),
                  pl.BlockSpec((CHUNK, w), lambda n: (rev(n), 2)), pl.BlockSpec((CHUNK, LANES), lambda n: (rev(n), 0)),
                  pl.BlockSpec((1, LANES, CHUNK), lambda n: (rev(n), 0, 0)),
                  pl.BlockSpec((1, nh, HEAD_DIM, HEAD_DIM), lambda n: (rev(n), 0, 0, 0)),
                  pl.BlockSpec((1, nh, CHUNK, CHUNK), lambda n: (rev(n), 0, 0, 0)),
                  pl.BlockSpec((CHUNK, w), lambda n: (rev(n), 0))],
        out_specs=[pl.BlockSpec((CHUNK, w), lambda n: (rev(n), 0))] * 3 + [pl.BlockSpec((CHUNK, LANES), lambda n: (rev(n), 0))],
        out_shape=[jax.ShapeDtypeStruct((lp, w), F32)] * 3 + [jax.ShapeDtypeStruct((lp, LANES), F32)],
        scratch_shapes=[pltpu.VMEM((nh, HEAD_DIM, HEAD_DIM), F32)],
        name="gdn_bwd", compiler_params=_cp("arbitrary"))(qkv, qkv, qkv, gates, gt3, s_all, t_all, do)
    return outs, (got[0] if got else None)


def _merge_gdn(o_gdn, proj, norm_w, nh):
    lp = o_gdn.shape[0]

    def body(o_ref, z_ref, w_ref, m_ref):
        o = o_ref[...]
        z = z_ref[...]
        m_ref[...] = (o * _rms(o) * w_ref[...] * (z * _sigmoid(z))).astype(BF16)

    return pl.pallas_call(
        body, grid=(nh,),
        in_specs=[pl.BlockSpec((lp, LANES), lambda s: (0, s)), pl.BlockSpec((lp, LANES), lambda s: (0, 3 * nh + s)),
                  pl.BlockSpec((1, LANES), lambda s: (0, 0))],
        out_specs=pl.BlockSpec((lp, LANES), lambda s: (0, s)),
        out_shape=jax.ShapeDtypeStruct((lp, 2 * nh * HEAD_DIM), BF16), name="merge_gdn",
        compiler_params=_cp("parallel"))(o_gdn, proj, norm_w)


def _merge_gdn_bwd(o_gdn, proj, norm_w, dmerged, nh):
    lp = o_gdn.shape[0]

    def body(o_ref, z_ref, w_ref, dm_ref, do_ref, dz_ref, dw_ref):
        o = o_ref[...]
        r = _rms(o)
        xh = o * r
        silu, dsilu = _silu_and_grad(z_ref[...])
        dm = dm_ref[...]
        dn = dm * silu
        dz_ref[...] = (dm * (xh * w_ref[...]) * dsilu).astype(BF16)
        dnw = dn * w_ref[...]
        do_ref[...] = r * (dnw - xh * jnp.mean(dnw * xh, axis=-1, keepdims=True))

        @pl.when(pl.program_id(0) == 0)
        def _():
            dw_ref[...] = jnp.zeros_like(dw_ref)
        dw_ref[...] += jnp.sum(dn * xh, axis=0, keepdims=True)

    w = nh * HEAD_DIM
    return pl.pallas_call(
        body, grid=(nh,),
        in_specs=[pl.BlockSpec((lp, LANES), lambda s: (0, s)), pl.BlockSpec((lp, LANES), lambda s: (0, 3 * nh + s)),
                  pl.BlockSpec((1, LANES), lambda s: (0, 0)), pl.BlockSpec((lp, LANES), lambda s: (0, s))],
        out_specs=[pl.BlockSpec((lp, LANES), lambda s: (0, s)), pl.BlockSpec((lp, LANES), lambda s: (0, 3 * nh + s)),
                   pl.BlockSpec((1, LANES), lambda s: (0, 0))],
        out_shape=[jax.ShapeDtypeStruct((lp, w), F32), jax.ShapeDtypeStruct((lp, 8 * w + NARROW), BF16),
                   jax.ShapeDtypeStruct((1, LANES), F32)],
        name="merge_gdn_bwd", compiler_params=_cp("arbitrary"))(o_gdn, proj, norm_w, dmerged)


def _fox_prep(proj, qk_w, nh):
    lp = proj.shape[0]

    def body(x_ref, w_ref, o_ref):
        x = x_ref[...]
        o_ref[...] = x * _rms(x) * w_ref[0]

    return pl.pallas_call(
        body, grid=(2 * nh,),
        in_specs=[pl.BlockSpec((lp, LANES), lambda s: (0, 4 * nh + s)), pl.BlockSpec((1, 1, LANES), lambda s: (s // nh, 0, 0))],
        out_specs=pl.BlockSpec((lp, LANES), lambda s: (0, s)),
        out_shape=jax.ShapeDtypeStruct((lp, 2 * nh * HEAD_DIM), F32), name="fox_prep",
        compiler_params=_cp("parallel"))(proj, qk_w)


def _fox_prep_bwd(proj, qk_w, dq, dk, dproj, nh):
    lp = proj.shape[0]
    part = lambda p: pl.BlockSpec((lp, LANES), lambda s: (0, jnp.clip(s - p * nh, 0, nh - 1)))

    def body(x_ref, w_ref, dq_ref, dk_ref, _, dx_ref, dw_ref):
        x = x_ref[...]
        r = _rms(x)
        xh = x * r
        dy = jnp.where(pl.program_id(0) < nh, dq_ref[...], dk_ref[...])
        dyw = dy * w_ref[0]
        dx_ref[...] = (r * (dyw - xh * jnp.mean(dyw * xh, axis=-1, keepdims=True))).astype(BF16)

        @pl.when(pl.program_id(0) % nh == 0)
        def _():
            dw_ref[...] = jnp.zeros_like(dw_ref)
        dw_ref[0] += jnp.sum(dy * xh, axis=0, keepdims=True)

    strip = pl.BlockSpec((lp, LANES), lambda s: (0, 4 * nh + s))
    wsp = pl.BlockSpec((1, 1, LANES), lambda s: (s // nh, 0, 0))
    return pl.pallas_call(
        body, grid=(2 * nh,), in_specs=[strip, wsp, part(0), part(1), _ANY], out_specs=[strip, wsp],
        out_shape=[jax.ShapeDtypeStruct(dproj.shape, BF16), jax.ShapeDtypeStruct((2, 1, LANES), F32)],
        input_output_aliases={4: 0}, name="fox_prep_bwd", compiler_params=_cp("arbitrary"))(proj, qk_w, dq, dk, dproj)


def _fox_probs(q, k, gates, crow, h, i, nh, lse=None):
    kl = k.shape[0]
    lane = _iota((Q_BLOCK, LANES), 1)
    ct = jnp.sum(jnp.where(lane == 4 * nh + h, gates, 0.0), axis=1, keepdims=True)
    tq, kq = _iota((Q_BLOCK, Q_BLOCK), 0), _iota((Q_BLOCK, Q_BLOCK), 1)
    qs = q * (HEAD_DIM ** -0.5)
    if i == 0:
        s = _bdot(qs, k, "nt") + (ct - crow)
        s = jnp.where((kq <= tq) & ((kq >= PAD_ROWS) | (tq < PAD_ROWS)), s, NEG)
    else:
        crow = jnp.where(_iota((1, kl), 1) < PAD_ROWS, -NEG, crow)
        s = _bdot(qs, k, "nt") + (ct - crow)
        s = jnp.concatenate([s[:, :kl - Q_BLOCK], jnp.where(kq <= tq, s[:, kl - Q_BLOCK:], NEG)], axis=1)
    if lse is not None:
        return jnp.exp(s - lse)
    m = jnp.max(s, axis=1, keepdims=True)
    p = jnp.exp(s - m)
    tot = jnp.sum(p, axis=1, keepdims=True)
    return p / tot, m + jnp.log(tot)


FOX_HEADS_PER_STEP = 2


def _fox_specs(lp, nh):
    hw = FOX_HEADS_PER_STEP * LANES
    return [pl.BlockSpec((Q_BLOCK, hw), lambda g, i: (i, g)),
            pl.BlockSpec((lp, hw), lambda g, i: (0, nh // FOX_HEADS_PER_STEP + g)),
            pl.BlockSpec((lp, hw), lambda g, i: (0, 6 * nh // FOX_HEADS_PER_STEP + g)),
            pl.BlockSpec((Q_BLOCK, LANES), lambda g, i: (i, 0)),
            pl.BlockSpec((LANES, lp), lambda g, i: (0, 0))]


def _fox_fwd(qkn, proj, gates, gtf, nh):
    lp = qkn.shape[0]

    def body(q_ref, k_ref, v_ref, g_ref, gt_ref, o_ref, lse_ref):
        g, i = pl.program_id(0), pl.program_id(1)
        for j in range(lp // Q_BLOCK):
            @pl.when(i == j)
            def _(j=j):
                kl = (j + 1) * Q_BLOCK
                for hh in range(FOX_HEADS_PER_STEP):
                    h = FOX_HEADS_PER_STEP * g + hh
                    sl = slice(hh * LANES, (hh + 1) * LANES)
                    p, lse = _fox_probs(q_ref[:, sl], k_ref[0:kl, sl], g_ref[...], gt_ref[pl.ds(4 * nh + h, 1), :][:, 0:kl],
                                        h, j, nh)
                    o_ref[:, sl] = _bdot(p, v_ref[0:kl, sl])
                    lse_ref[:, sl] = jnp.broadcast_to(lse, (Q_BLOCK, LANES))

    blk = pl.BlockSpec((Q_BLOCK, FOX_HEADS_PER_STEP * LANES), lambda g, i: (i, g))
    return pl.pallas_call(
        body, grid=(nh // FOX_HEADS_PER_STEP, lp // Q_BLOCK), in_specs=_fox_specs(lp, nh), out_specs=[blk, blk],
        out_shape=[jax.ShapeDtypeStruct((lp, nh * HEAD_DIM), F32)] * 2, name="fox_fwd",
        compiler_params=_cp("parallel", "parallel"))(qkn, qkn, proj, gates, gtf)


def _fox_bwd(qkn, proj, gates, gtf, lse, do, dproj, nh):
    lp = qkn.shape[0]
    nq = lp // Q_BLOCK
    w = nh * HEAD_DIM
    scale = HEAD_DIM ** -0.5

    def body(q_ref, k_ref, v_ref, g_ref, gt_ref, lse_ref, do_ref, _, dq_ref, dk_ref, dc_ref, dv_ref, dv_scr):
        g, i = pl.program_id(0), pl.program_id(1)

        @pl.when(i == 0)
        def _():
            dk_ref[...] = jnp.zeros_like(dk_ref)
            dv_scr[...] = jnp.zeros_like(dv_scr)
            dc_ref[...] = jnp.zeros_like(dc_ref)
        for j in range(nq):
            @pl.when(i == j)
            def _(j=j):
                kl = (j + 1) * Q_BLOCK
                for hh in range(FOX_HEADS_PER_STEP):
                    h = FOX_HEADS_PER_STEP * g + hh
                    sl = slice(hh * LANES, (hh + 1) * LANES)
                    q, k = q_ref[:, sl], k_ref[0:kl, sl]
                    p = _fox_probs(q, k, g_ref[...], gt_ref[pl.ds(4 * nh + h, 1), :][:, 0:kl], h, j, nh,
                                   lse_ref[:, sl][:, 0:1])
                    dout = do_ref[:, sl]
                    dp = _bdot(dout, v_ref[0:kl, sl], "nt")
                    ds = p * (dp - jnp.sum(p * dp, axis=1, keepdims=True))
                    dq_ref[:, sl] = _bdot(ds, k) * scale
                    dk_ref[0:kl, sl] += _bdot(ds, q * scale, "tn")
                    dv_scr[0:kl, sl] += _bdot(p, dout, "tn")
                    dc_ref[hh, :, 0:kl] -= jnp.sum(ds, axis=0, keepdims=True)

        @pl.when(i == nq - 1)
        def _():
            dv_ref[...] = dv_scr[...].astype(BF16)

    hw = FOX_HEADS_PER_STEP * LANES
    blk = pl.BlockSpec((Q_BLOCK, hw), lambda g, i: (i, g))
    col = pl.BlockSpec((lp, hw), lambda g, i: (0, g))
    return pl.pallas_call(
        body, grid=(nh // FOX_HEADS_PER_STEP, nq), in_specs=_fox_specs(lp, nh) + [blk, blk, _ANY],
        out_specs=[blk, col, pl.BlockSpec((FOX_HEADS_PER_STEP, 1, lp), lambda g, i: (g, 0, 0)),
                   pl.BlockSpec((lp, hw), lambda g, i: (0, 6 * nh // FOX_HEADS_PER_STEP + g))],
        out_shape=[jax.ShapeDtypeStruct((lp, w), F32)] * 2 + [jax.ShapeDtypeStruct((nh, 1, lp), F32),
                                                             jax.ShapeDtypeStruct(dproj.shape, BF16)],
        scratch_shapes=[pltpu.VMEM((lp, hw), F32)], input_output_aliases={7: 3},
        name="fox_bwd", compiler_params=_cp("parallel", "arbitrary"))(qkn, qkn, proj, gates, gtf, lse, do, dproj)


def _merge_fox(o_fox, proj, merged, nh):
    lp = o_fox.shape[0]

    def body(o_ref, z_ref, _, m_ref):
        z = z_ref[...]
        m_ref[...] = (o_ref[...] * (z * _sigmoid(z))).astype(BF16)

    return pl.pallas_call(
        body, grid=(nh,),
        in_specs=[pl.BlockSpec((lp, LANES), lambda s: (0, s)), pl.BlockSpec((lp, LANES), lambda s: (0, 7 * nh + s)), _ANY],
        out_specs=pl.BlockSpec((lp, LANES), lambda s: (0, nh + s)),
        out_shape=jax.ShapeDtypeStruct(merged.shape, BF16), input_output_aliases={2: 0}, name="merge_fox",
        compiler_params=_cp("parallel"))(o_fox, proj, merged)


def _merge_fox_bwd(o_fox, proj, dmerged, dproj, nh):
    lp = o_fox.shape[0]

    def body(o_ref, z_ref, dm_ref, _, do_ref, dz_ref):
        silu, dsilu = _silu_and_grad(z_ref[...])
        dm = dm_ref[...]
        do_ref[...] = dm * silu
        dz_ref[...] = (dm * o_ref[...] * dsilu).astype(BF16)

    w = nh * HEAD_DIM
    return pl.pallas_call(
        body, grid=(nh,),
        in_specs=[pl.BlockSpec((lp, LANES), lambda s: (0, s)), pl.BlockSpec((lp, LANES), lambda s: (0, 7 * nh + s)),
                  pl.BlockSpec((lp, LANES), lambda s: (0, nh + s)), _ANY],
        out_specs=[pl.BlockSpec((lp, LANES), lambda s: (0, s)), pl.BlockSpec((lp, LANES), lambda s: (0, 7 * nh + s))],
        out_shape=[jax.ShapeDtypeStruct((lp, w), F32), jax.ShapeDtypeStruct(dproj.shape, BF16)],
        input_output_aliases={3: 1}, name="merge_fox_bwd", compiler_params=_cp("parallel"))(o_fox, proj, dmerged, dproj)


def _post(out, x, target, post_w):
    lp, d = out.shape

    def body(o_ref, x_ref, t_ref, w_ref, dy_ref, do_ref, loss_ref, dw_ref):
        i = pl.program_id(0)

        @pl.when(i == 0)
        def _():
            loss_ref[...] = jnp.zeros_like(loss_ref)
            dw_ref[...] = jnp.zeros_like(dw_ref)
        o = o_ref[...]
        r = _rms(o)
        nrm = o * r
        err = jnp.where(i > 0, x_ref[...] + nrm * w_ref[...] - t_ref[...], 0.0)
        loss_ref[0:1, :] += 0.5 * jnp.sum(jnp.sum(err * err, axis=1, keepdims=True), axis=0, keepdims=True) / d
        dy = err / d
        dy_ref[...] = dy
        dw_ref[...] += jnp.sum(dy * nrm, axis=0, keepdims=True)
        dyw = dy * w_ref[...]
        do_ref[...] = (r * (dyw - nrm * jnp.mean(dyw * nrm, axis=-1, keepdims=True))).astype(BF16)

    row = pl.BlockSpec((Q_BLOCK, d), lambda i: (i, 0))
    vec = pl.BlockSpec((1, d), lambda i: (0, 0))
    return pl.pallas_call(
        body, grid=(lp // Q_BLOCK,), in_specs=[row, _x_rows(d), _x_rows(d), vec],
        out_specs=[_x_rows(d), row, pl.BlockSpec((8, LANES), lambda i: (0, 0)), vec],
        out_shape=[jax.ShapeDtypeStruct(x.shape, F32), jax.ShapeDtypeStruct((lp, d), BF16),
                   jax.ShapeDtypeStruct((8, LANES), F32), jax.ShapeDtypeStruct((1, d), F32)],
        name="post", compiler_params=_cp("arbitrary"))(out, x, target, post_w)


def _prenorm_bwd(dxn, x, meta, w, dy, rider=None):
    seq, d = x.shape
    lp = seq + Q_BLOCK

    def body(start, finish, dx_ref, x_ref, m_ref, w_ref, dy_ref, gx_ref, gm_ref, dw_ref):
        i = pl.program_id(0)
        pl.when(i == 0)(start)
        h = _h_tile(i, x_ref, m_ref)
        r = _rms(h)
        xh = h * r
        dxn_ = dx_ref[...]
        dxw = dxn_ * w_ref[...]
        dh = jnp.where(i > 0, dy_ref[...], 0.0) + r * (dxw - xh * jnp.mean(dxw * xh, axis=-1, keepdims=True))
        gx_ref[...] = dh

        @pl.when(i == 0)
        def _():
            dw_ref[...] = jnp.zeros_like(dw_ref)
            gm_ref[...] = dh[PAD_ROWS:, :]
        dw_ref[...] += jnp.sum(dxn_ * xh, axis=0, keepdims=True)
        pl.when(i == lp // Q_BLOCK - 1)(finish)

    vec = pl.BlockSpec((1, d), lambda i: (0, 0))
    met = pl.BlockSpec((N_META, d), lambda i: (0, 0))
    outs, got = _hosted_call(
        body, [rider], 5, 3, 0, grid=(lp // Q_BLOCK,),
        in_specs=[pl.BlockSpec((Q_BLOCK, d), lambda i: (i, 0)), _x_rows(d), met, vec, _x_rows(d)],
        out_specs=[_x_rows(d), met, vec],
        out_shape=[jax.ShapeDtypeStruct((seq, d), F32), jax.ShapeDtypeStruct((N_META, d), F32),
                   jax.ShapeDtypeStruct((1, d), F32)],
        name="prenorm_bwd", compiler_params=_cp("arbitrary"))(dxn, x, meta, w, dy)
    return outs, (got[0] if got else None)


def _layer_grads(x, target, meta, pre_w, wfull, conv_wt, a_log, dt_bias, gdn_norm_w, fq_w, fk_w, f_bias, w_out, post_w,
                 late_weights=None, w_out_grads=None):
    nh = a_log.shape[1]
    zpad = jnp.zeros((1, LANES - 3 * nh), F32)
    bias_row = jnp.concatenate([jnp.zeros((1, nh), F32), dt_bias, f_bias, zpad], axis=1)
    nega_row = jnp.concatenate([jnp.zeros((1, nh), F32), -jnp.exp(a_log), jnp.zeros((1, nh), F32), zpad], axis=1)
    qk_w = jnp.stack([fq_w, fk_w])

    if isinstance(wfull, tuple):
        rider, meta_at, project = wfull
        xn, got = _prenorm_gathering(x, pre_w, rider, meta_at)
        proj, wfull, conv_wt, meta = project(xn, got)
    else:
        xn = _prenorm(x, meta, pre_w)
        proj = _matmul(xn, wfull, "nn", MM_TILE, F32, "proj")
    qkv = _gdn_prep(proj, conv_wt, nh)
    gates, gt3, gtf = _gates(proj, bias_row, nega_row, nh)
    (o_gdn, s_all, t_all), got = _gdn_fwd(qkv, gates, gt3, nh, None if late_weights is None else late_weights[0])
    if late_weights is not None:
        w_out = late_weights[1](got)
    qkn = _fox_prep(proj, qk_w, nh)
    o_fox, fox_lse = _fox_fwd(qkn, proj, gates, gtf, nh)
    merged = _merge_fox(o_fox, proj, _merge_gdn(o_gdn, proj, gdn_norm_w, nh), nh)
    out = _matmul(merged, w_out, "nn", 4 * LANES, F32, "out_proj")
    dy, dout, loss_blk, dpost_w = _post(out, x, target, post_w)

    dw_out = _matmul(merged, dout, "tn", 4 * LANES, BF16, "dw_out")
    if w_out_grads is None:
        dmerged, gdn_rider = _matmul(dout, w_out, "nt", 4 * LANES, F32, "dmerged"), None
    else:
        dmerged, got = _matmul(dout, w_out, "nt", 4 * LANES, F32, "dmerged", w_out_grads[0](dw_out))
        gdn_rider = w_out_grads[1](dw_out, got)
    do_gdn, dproj, dgdn_norm_w = _merge_gdn_bwd(o_gdn, proj, gdn_norm_w, dmerged, nh)
    do_fox, dproj = _merge_fox_bwd(o_fox, proj, dmerged, dproj, nh)
    dqn, dkn, dc_t, dproj = _fox_bwd(qkn, proj, gates, gtf, fox_lse, do_fox, dproj, nh)
    dproj, dqk_w = _fox_prep_bwd(proj, qk_w, dqn, dkn, dproj, nh)
    (dgq, dgk, dgv, dgate), w_out_parts = _gdn_bwd(qkv, gates, gt3, s_all, t_all, do_gdn, nh, gdn_rider)
    dproj, dconv_wt = _gdn_prep_bwd(proj, conv_wt, dgq, dgk, dgv, dproj, nh)
    dc_rows = jnp.pad(dc_t.reshape(nh, -1), ((2 * nh, LANES - 3 * nh), (0, 0)))
    dproj, gate_sums = _gates_bwd(proj, bias_row, nega_row, gates, dgate, dc_rows, dproj, nh)
    return dict(
        loss=loss_blk[0:1, 0:1], dy=dy, xn=xn, dproj=dproj, post_w=dpost_w,
        conv_wt=dconv_wt, a_log=gate_sums[1:2, nh:2 * nh], dt_bias=gate_sums[0:1, nh:2 * nh],
        gdn_norm_w=dgdn_norm_w, fq_w=dqk_w[0], fk_w=dqk_w[1], f_bias=gate_sums[0:1, 2 * nh:3 * nh], w_out=dw_out,
        w_out_parts=w_out_parts, wfull=wfull, meta=meta)


def _cast_bf16(a, tr, name):
    r, c = a.shape

    def body(a_ref, o_ref):
        o_ref[...] = a_ref[...].astype(BF16)

    return pl.pallas_call(
        body, grid=(r // tr,), in_specs=[pl.BlockSpec((tr, c), lambda i: (i, 0))],
        out_specs=pl.BlockSpec((tr, c), lambda i: (i, 0)), out_shape=jax.ShapeDtypeStruct((r, c), BF16),
        name=name, compiler_params=_cp("parallel"))(a)


def _column_major(a):
    return jnp.transpose(a, (2, 0, 1))


def _cast_bf16_column_major(a3, pieces, name):
    _, r, c = a3.shape
    rows = r // pieces

    def body(a_ref, *o_refs):
        t = a_ref[...].reshape(LANES, r).T.astype(BF16)
        for k, o_ref in enumerate(o_refs):
            o_ref[...] = t[k * rows:(k + 1) * rows]

    return pl.pallas_call(
        body, grid=(pl.cdiv(c, LANES),), in_specs=[pl.BlockSpec((LANES, 1, r), lambda i: (i, 0, 0))],
        out_specs=[pl.BlockSpec((rows, LANES), lambda i: (0, i))] * pieces,
        out_shape=[jax.ShapeDtypeStruct((rows, c), BF16)] * pieces, name=name, compiler_params=_cp("parallel"))(_column_major(a3))


def _adamw_column_major(w3, parts, m3, v3, name, rows=None, into=None, rider=None):
    _, r, c = w3.shape
    n_parts = parts[0].shape[0]
    first, last = rows or (0, r)
    rr = last - first
    steps = pl.cdiv(c, LANES)

    def body(start, finish, w_ref, *refs):
        p_refs, (m_ref, v_ref), (g_ref, d_ref, nm_ref, nv_ref) = refs[:len(parts)], refs[len(parts):len(parts) + 2], refs[-4:]
        pl.when(pl.program_id(0) == 0)(start)
        sums = []
        for p_ref in p_refs:
            g = p_ref[0].astype(F32)
            for s in range(1, n_parts):
                g = g + p_ref[s].astype(F32)
            sums.append(g)
        g = jnp.concatenate(sums, axis=0).T
        flat = lambda ref: ref[...].reshape(LANES, rr)
        m_new = ADAM_B1 * flat(m_ref) + (1.0 - ADAM_B1) * g
        v_new = ADAM_B2 * flat(v_ref) + (1.0 - ADAM_B2) * (g * g)
        m_hat = m_new / (1.0 - ADAM_B1 ** ADAM_STEP)
        v_hat = v_new / (1.0 - ADAM_B2 ** ADAM_STEP)
        delta = -ADAM_LR * (m_hat / (jnp.sqrt(v_hat) + ADAM_EPS) + ADAM_WD * flat(w_ref))
        for ref, val in ((g_ref, g), (d_ref, delta), (nm_ref, m_new), (nv_ref, v_new)):
            ref[...] = val.reshape(LANES, 1, rr)
        pl.when(pl.program_id(0) == steps - 1)(finish)

    blk = pl.BlockSpec((LANES, 1, rr), lambda i: (i, 0, first // rr))
    kept = list(into or [])
    n_in = 3 + len(parts) + len(kept)
    outs, got = _hosted_call(
        body, [rider], n_in, 4, 0, grid=(steps,),
        in_specs=[blk] + [pl.BlockSpec((n_parts, p.shape[1], LANES), lambda i: (0, 0, i)) for p in parts] + [blk, blk]
        + [_ANY] * len(kept),
        out_specs=[blk] * 4, out_shape=[jax.ShapeDtypeStruct((c, 1, r), F32)] * 4, name=name,
        aliases={n_in - len(kept) + k: k for k in range(len(kept))},
        compiler_params=_cp("parallel" if rider is None else "arbitrary"))(
            _column_major(w3), *parts, _column_major(m3), _column_major(v3), *kept)
    return outs if rider is None else (outs, got[0])


def _gather_copies(ins, outs, send_sems, recv_sems, local_sems):
    n = len(ins)
    x, y, c = lax.axis_index("x"), lax.axis_index("y"), lax.axis_index("c")
    me, sibling = (x, y, c), (x, y, 1 - c)
    xn, yn, dg = (1 - x, y), (x, 1 - y), (1 - x, 1 - y)

    def copy(a, k, block, to, src=None):
        px, py, pc = block
        rows = outs[a].at[4 * px + 2 * py + pc]
        return pltpu.make_async_remote_copy(
            src_ref=rows if src is None else src, dst_ref=rows, send_sem=send_sems.at[a, k],
            recv_sem=recv_sems.at[a, k], device_id=to, device_id_type=_MESH)

    local = [pltpu.make_async_copy(ins[a], outs[a].at[4 * x + 2 * y + c], local_sems.at[a]) for a in range(n)]
    own = [cp for a in range(n) for cp in (copy(a, 0, me, sibling, src=ins[a]), copy(a, 1, me, (*xn, c), src=ins[a]),
                                           copy(a, 2, me, (*yn, c), src=ins[a]))]

    def start():
        for cp in local + own:
            cp.start()

    def finish():
        for a in range(n):
            @pl.when(c == 1)
            def _(a=a):
                copy(a, 1, (*xn, c), me).wait_recv()
                copy(a, 3, (*xn, c), (*yn, c)).start()

            @pl.when(c == 0)
            def _(a=a):
                copy(a, 2, (*yn, c), me).wait_recv()
                copy(a, 3, (*yn, c), (*xn, c)).start()
        for a in range(n):
            pl.when(c == 0)(copy(a, 1, (*xn, c), me).wait_recv)
            copy(a, 4, (*xn, c), sibling).start()
            pl.when(c == 1)(copy(a, 2, (*yn, c), me).wait_recv)
            copy(a, 5, (*yn, c), sibling).start()
        for a in range(n):
            copy(a, 3, (*dg, c), me).wait_recv()
            copy(a, 6, (*dg, c), sibling).start()
        for a in range(n):
            copy(a, 0, sibling, me).wait_recv()
            for k, chip in ((4, xn), (5, yn), (6, dg)):
                copy(a, k, (*chip, 1 - c), me).wait_recv()
                copy(a, k, (*chip, c), sibling).wait_send()
            copy(a, 3, (*xn, c), (*yn, c)).wait_send()
        for cp in own:
            cp.wait_send()
        for cp in local:
            cp.wait()

    return start, finish


def _gather_scratch(n):
    return [pltpu.SemaphoreType.DMA((n, N_DEV - 1)), pltpu.SemaphoreType.DMA((n, N_DEV - 1)), pltpu.SemaphoreType.DMA((n,))]


def _gather_rider(arrays):
    return _Rider(list(arrays), [jax.ShapeDtypeStruct((N_DEV,) + a.shape, a.dtype) for a in arrays],
                  _gather_scratch(len(arrays)), {}, lambda ins, outs, scratch: _gather_copies(ins, outs, *scratch))


def _all_gather(arrays, name):
    n = len(arrays)

    def body(*refs):
        start, finish = _gather_copies(refs[:n], refs[n:2 * n], *refs[2 * n:])
        start()
        finish()

    return pl.pallas_call(
        body, in_specs=[_ANY] * n, out_specs=[_ANY] * n,
        out_shape=[jax.ShapeDtypeStruct((N_DEV,) + a.shape, a.dtype) for a in arrays],
        scratch_shapes=_gather_scratch(n), name=name)(*arrays)


SLAB = 10 * LANES


def _slab_start(blk, nh, cols):
    in_second_half = blk >= N_DEV // 2
    shift = (2 * nh if in_second_half else 0) if isinstance(blk, int) else jnp.where(in_second_half, 2 * nh, 0)
    return (blk * cols - shift) // LANES * LANES


def _pair_rider(dw_rows=None, parts=None, nh=None, after=None):
    if dw_rows is not None:
        r, full = dw_rows.shape
        cols = (full - NARROW + 3 * nh) // N_DEV
        out_shapes = [jax.ShapeDtypeStruct((N_CHIP, r, SLAB), dw_rows.dtype), jax.ShapeDtypeStruct((r, NARROW), dw_rows.dtype)]
    else:
        out_shapes = [jax.ShapeDtypeStruct((N_CHIP,) + parts.shape[1:], parts.dtype)]

    def make(ins, outs, scratch):
        send_sems, recv_sems = scratch
        x, y, c = lax.axis_index("x"), lax.axis_index("y"), lax.axis_index("c")
        kw = lambda k: dict(send_sem=send_sems.at[k], recv_sem=recv_sems.at[k], device_id=(x, y, 1 - c), device_id_type=_MESH)
        copies = []
        for q in range(N_CHIP):
            if dw_rows is not None:
                first = pl.multiple_of(_slab_start(2 * q + 1 - c, nh, cols), LANES)
                copies.append(pltpu.make_async_remote_copy(src_ref=ins[0].at[:, pl.ds(first, SLAB)], dst_ref=outs[0].at[q], **kw(q)))
            else:
                copies.append(pltpu.make_async_remote_copy(src_ref=ins[0].at[2 * q + 1 - c], dst_ref=outs[0].at[q], **kw(q)))
        if dw_rows is not None:
            copies.append(pltpu.make_async_remote_copy(src_ref=ins[0].at[:, pl.ds(full - NARROW, NARROW)], dst_ref=outs[1],
                                                       **kw(N_CHIP)))

        def start():
            for cp in copies:
                cp.start()

        def finish():
            for cp in copies:
                cp.wait()

        return start, finish

    return _Rider([dw_rows if dw_rows is not None else parts] + ([] if after is None else [after]), out_shapes,
                  [pltpu.SemaphoreType.DMA((N_CHIP + 1,)), pltpu.SemaphoreType.DMA((N_CHIP + 1,))], {}, make)


def _relayout_pair_sum(dwfull, got_slabs, got_tail, core, nh, tr, name):
    d, full = dwfull.shape
    w = nh * HEAD_DIM
    cols = (8 * w + 3 * nh) // N_DEV
    segs = _native_segments(nh)

    def block(f_ref, s_ref, t_ref, q, blk):
        st = _slab_start(blk, nh, cols)
        wide = f_ref[:, st:st + SLAB].astype(F32) + s_ref[q].astype(F32)
        tail = f_ref[:, 8 * w:].astype(F32) + t_ref[...].astype(F32)
        pieces = []
        for s0, s1, t0 in segs:
            lo, hi = max(s0, blk * cols), min(s1, (blk + 1) * cols)
            if lo < hi:
                at = t0 + lo - s0
                pieces.append(tail[:, at - 8 * w:at - 8 * w + hi - lo] if at >= 8 * w else wide[:, at - st:at - st + hi - lo])
        return (pieces[0] if len(pieces) == 1 else jnp.concatenate(pieces, axis=1)).astype(dwfull.dtype)

    def body(core_ref, f_ref, s_ref, t_ref, o_ref):
        for parity in range(2):
            @pl.when(core_ref[0] == parity)
            def _(parity=parity):
                for q in range(N_CHIP):
                    o_ref[q] = block(f_ref, s_ref, t_ref, q, 2 * q + parity)

    return pl.pallas_call(
        body,
        grid_spec=pltpu.PrefetchScalarGridSpec(
            num_scalar_prefetch=1, grid=(d // tr,),
            in_specs=[pl.BlockSpec((tr, full), lambda i, c_ref: (i, 0)), pl.BlockSpec((N_CHIP, tr, SLAB), lambda i, c_ref: (0, i, 0)),
                      pl.BlockSpec((tr, NARROW), lambda i, c_ref: (i, 0))],
            out_specs=pl.BlockSpec((N_CHIP, tr, cols), lambda i, c_ref: (0, i, 0))),
        out_shape=jax.ShapeDtypeStruct((N_CHIP, d, cols), dwfull.dtype), name=name,
        compiler_params=_cp("parallel"))(core, dwfull, got_slabs, got_tail)


def _pair_sum(parts, got, core, tr, name):
    _, r, c = parts.shape

    def body(core_ref, p_ref, g_ref, o_ref):
        o_ref[...] = (p_ref[...].astype(F32) + g_ref[...].astype(F32)).astype(o_ref.dtype)

    return pl.pallas_call(
        body,
        grid_spec=pltpu.PrefetchScalarGridSpec(
            num_scalar_prefetch=1, grid=(N_CHIP, r // tr),
            in_specs=[pl.BlockSpec((1, tr, c), lambda q, i, core_ref: (2 * q + core_ref[0], i, 0)),
                      pl.BlockSpec((1, tr, c), lambda q, i, core_ref: (q, i, 0))],
            out_specs=pl.BlockSpec((1, tr, c), lambda q, i, core_ref: (q, i, 0))),
        out_shape=jax.ShapeDtypeStruct((N_CHIP, r, c), parts.dtype), name=name,
        compiler_params=_cp("parallel", "parallel"))(core, parts, got)


def _native_segments(nh):
    w = nh * HEAD_DIM
    return [(0, 4 * w, 0), (4 * w, 4 * w + 2 * nh, 8 * w), (4 * w + 2 * nh, 8 * w + 2 * nh, 4 * w),
            (8 * w + 2 * nh, 8 * w + 3 * nh, 8 * w + 2 * nh)]


def _relayout_w_in(wg, nh, tr, name, rows_total=None, into=None):
    _, d, cols = wg.shape
    w = nh * HEAD_DIM
    rows_total = into.shape[0] if into is not None else rows_total or d
    first = (rows_total - d) // tr if into is not None else 0

    def native(ref, j0, j1):
        out = []
        while j0 < j1:
            blk = j0 // cols
            end = min(j1, (blk + 1) * cols)
            out.append(ref[blk, :, pl.ds(j0 - blk * cols, end - j0)])
            j0 = end
        return out

    def body(g_ref, *refs):
        o_ref = refs[-1]
        for cidx in range(8 * w // LANES):
            j0 = cidx * LANES + (0 if cidx * LANES < 4 * w else 2 * nh)
            pieces = native(g_ref, j0, j0 + LANES)
            o_ref[:, cidx * LANES:(cidx + 1) * LANES] = pieces[0] if len(pieces) == 1 else jnp.concatenate(pieces, axis=1)
        pieces = (native(g_ref, 4 * w, 4 * w + 2 * nh) + native(g_ref, 8 * w + 2 * nh, 8 * w + 3 * nh)
                  + [jnp.zeros((tr, NARROW - 3 * nh), wg.dtype)])
        o_ref[:, 8 * w:] = jnp.concatenate(pieces, axis=1)

    return pl.pallas_call(
        body, grid=(d // tr,), in_specs=[pl.BlockSpec((N_DEV, tr, cols), lambda i: (0, i, 0))] + [_ANY] * (into is not None),
        out_specs=pl.BlockSpec((tr, 8 * w + NARROW), lambda i: (first + i, 0)),
        out_shape=jax.ShapeDtypeStruct((rows_total, 8 * w + NARROW), wg.dtype),
        input_output_aliases={1: 0} if into is not None else {},
        name=name, compiler_params=_cp("parallel"))(wg, *([into] if into is not None else []))


def _adamw(w, parts, m, v, tr, name):
    r, c = w.shape
    n_parts = parts.shape[0]

    def body(w_ref, p_ref, m_ref, v_ref, g_ref, d_ref, nm_ref, nv_ref):
        g = p_ref[0].astype(F32)
        for s in range(1, n_parts):
            g = g + p_ref[s].astype(F32)
        m_new = ADAM_B1 * m_ref[...] + (1.0 - ADAM_B1) * g
        v_new = ADAM_B2 * v_ref[...] + (1.0 - ADAM_B2) * (g * g)
        m_hat = m_new / (1.0 - ADAM_B1 ** ADAM_STEP)
        v_hat = v_new / (1.0 - ADAM_B2 ** ADAM_STEP)
        g_ref[...] = g
        d_ref[...] = -ADAM_LR * (m_hat / (jnp.sqrt(v_hat) + ADAM_EPS) + ADAM_WD * w_ref[...])
        nm_ref[...] = m_new
        nv_ref[...] = v_new

    blk = pl.BlockSpec((tr, c), lambda i: (i, 0))
    return pl.pallas_call(
        body, grid=(r // tr,), in_specs=[blk, pl.BlockSpec((n_parts, tr, c), lambda i: (0, i, 0)), blk, blk],
        out_specs=[blk] * 4, out_shape=[jax.ShapeDtypeStruct((r, c), F32)] * 4, name=name,
        compiler_params=_cp("parallel"))(w, parts, m, v)


def _pack_small(d, pre, post, a_log, dt_bias, f_bias, gdn_w, fq_w, fk_w, extra):
    row2 = jnp.concatenate([a_log, dt_bias, f_bias, gdn_w, fq_w, fk_w, extra], axis=1)
    row2 = jnp.pad(row2, ((0, 0), (0, d - row2.shape[1])))
    return jnp.concatenate([pre, post, row2, jnp.zeros((5, d), F32)], axis=0)


def _unpack_small(p, nh):
    o = 3 * nh
    return dict(pre=p[0:1], post=p[1:2], a_log=p[2:3, 0:nh], dt_bias=p[2:3, nh:2 * nh], f_bias=p[2:3, 2 * nh:o],
                gdn_w=p[2:3, o:o + HEAD_DIM], fq_w=p[2:3, o + HEAD_DIM:o + 2 * HEAD_DIM],
                fk_w=p[2:3, o + 2 * HEAD_DIM:o + 3 * HEAD_DIM], extra=p[2, o + 3 * HEAD_DIM])


def kernel(x, meta_tokens, pre_norm_w, w_in, conv_w, a_log, dt_bias, gdn_norm_w, fox_q_norm_w, fox_k_norm_w, fox_f_bias, w_out, post_norm_w, loss_target, m_meta_tokens, m_pre_norm_w, m_w_in, m_conv_w, m_a_log, m_dt_bias, m_gdn_norm_w, m_fox_q_norm_w, m_fox_k_norm_w, m_fox_f_bias, m_w_out, m_post_norm_w, v_meta_tokens, v_pre_norm_w, v_w_in, v_conv_w, v_a_log, v_dt_bias, v_gdn_norm_w, v_fox_q_norm_w, v_fox_k_norm_w, v_fox_f_bias, v_w_out, v_post_norm_w):
    nh = a_log.shape[1]
    d = x.shape[-1]
    w = nh * HEAD_DIM
    zero = jnp.zeros((1, 1), F32)

    w_in_a, w_in_b = _cast_bf16_column_major(w_in, 2, "cast_w_in")

    def project(xn, got):
        wg, cg, mg = got
        half = (d // 2, 0), (d // 2, 1)
        wfull = _relayout_w_in(wg, nh, 256, "relayout_w_in_a", rows_total=d)
        proj, got = _matmul(xn, wfull, "nn", MM_TILE, F32, "proj_a", _gather_rider([w_in_b]), a_cols=half[0], b_rows=half[0])
        wfull = _relayout_w_in(got[0], nh, 256, "relayout_w_in_b", into=wfull)
        proj = _matmul(xn, wfull, "nn", MM_TILE, F32, "proj_b", a_cols=half[1], b_rows=half[1], acc=proj)
        return proj, wfull, cg.transpose(1, 0, 2).reshape(CONV_WIDTH, 3 * w), mg.transpose(1, 0, 2).reshape(N_META, d)

    late_weights = (_gather_rider([_cast_bf16(w_out[0], 256, "cast_w_out")]), lambda got: got[0].reshape(2 * w, d))
    core = lax.axis_index("c")
    dev = 4 * lax.axis_index("x") + 2 * lax.axis_index("y") + core
    core_arr = jnp.reshape(core, (1,)).astype(jnp.int32)

    out_parts = lambda dw_out: dw_out.reshape(N_DEV, 2 * w // N_DEV, d)
    g = _layer_grads(
        x[0], loss_target[0], None, pre_norm_w, (_gather_rider([w_in_a, conv_w[0].T, meta_tokens]), 2, project), None,
        a_log, dt_bias, gdn_norm_w,
        fox_q_norm_w, fox_k_norm_w, fox_f_bias, None, post_norm_w, late_weights=late_weights,
        w_out_grads=(lambda dw_out: _pair_rider(parts=out_parts(dw_out)),
                     lambda dw_out, got: _chip_rider(_pair_sum(out_parts(dw_out), got[0], core_arr, 256, "pair_sum_w_out"))))
    p_out = g["w_out_parts"][0]
    xn, dproj, wfull, meta_full = g["xn"], g["dproj"], g["wfull"], g["meta"]
    flights, token, dw, rider = [], None, None, None

    def exchange(dw, got, i):
        sums = _relayout_pair_sum(dw, got[0], got[1], core_arr, nh, 128, f"relayout_pair_sum_{i}")
        flight, token = _chip_exchange_start(sums, f"chip_exchange_start_{i}")
        flights.append(flight)
        return token

    for i, (index, parts) in enumerate(DW_IN_PIECES):
        res = _matmul(xn, dproj, "tn", MM_TILE, BF16, f"dw_in_{i}", rider, a_cols=(d // parts, index))
        if i:
            token = exchange(dw, res[1], i - 1)
        dw = res[0] if i else res
        rider = _pair_rider(dw_rows=dw, nh=nh, after=token)
    dxn, (got,) = _dxn(dproj, wfull, [rider], MM_TILE, "dxn")
    token = exchange(dw, got, len(DW_IN_PIECES) - 1)
    (grad_x, dmeta, dpre_w), _ = _prenorm_bwd(dxn, x[0], meta_full, pre_norm_w + token[0:1, 0:1], g["dy"])
    small = _pack_small(d, dpre_w, g["post_w"], g["a_log"], g["dt_bias"], g["f_bias"], g["gdn_norm_w"], g["fq_w"],
                        g["fk_w"], g["loss"])
    first_rows = d // DW_IN_PIECES[0][1]
    r_in, (a_conv, a_meta, p_small) = _adamw_column_major(
        w_in, _chip_exchange_wait(flights[:1], dpre_w, "chip_exchange_wait_0"), m_w_in, v_w_in, "adamw_w_in_0",
        rows=(0, first_rows), rider=_gather_rider([g["conv_wt"], dmeta, small]))
    p_conv = lax.dynamic_slice_in_dim(a_conv, dev * conv_w.shape[1], conv_w.shape[1], axis=2).transpose(0, 2, 1)
    p_meta = lax.dynamic_slice_in_dim(a_meta, dev * meta_tokens.shape[1], meta_tokens.shape[1], axis=2)

    r_out = _adamw(w_out[0], p_out, m_w_out[0], v_w_out[0], 64, "adamw_w_out")
    r_conv = _adamw(conv_w[0], p_conv, m_conv_w[0], v_conv_w[0], conv_w.shape[1], "adamw_conv_w")
    r_meta = _adamw(meta_tokens, p_meta, m_meta_tokens, v_meta_tokens, N_META, "adamw_meta")
    pk = lambda pre, post, a, dt, gw, fq, fk, fb: _pack_small(d, pre, post, a, dt, fb, gw, fq, fk, zero)
    r_small = _adamw(
        pk(pre_norm_w, post_norm_w, a_log, dt_bias, gdn_norm_w, fox_q_norm_w, fox_k_norm_w, fox_f_bias), p_small,
        pk(m_pre_norm_w, m_post_norm_w, m_a_log, m_dt_bias, m_gdn_norm_w, m_fox_q_norm_w, m_fox_k_norm_w, m_fox_f_bias),
        pk(v_pre_norm_w, v_post_norm_w, v_a_log, v_dt_bias, v_gdn_norm_w, v_fox_q_norm_w, v_fox_k_norm_w, v_fox_f_bias),
        8, "adamw_small")
    p_in = _chip_exchange_wait(flights[1:], r_small[0], "chip_exchange_wait")
    r_in = _adamw_column_major(w_in, p_in, m_w_in, v_w_in, "adamw_w_in_1", rows=(first_rows, d), into=r_in)
    r_in = [jnp.transpose(o, (1, 2, 0)) for o in r_in]

    sm = [_unpack_small(r, nh) for r in r_small]
    outs = []
    for i in range(4):
        s = sm[i]
        outs += [r_meta[i], s["pre"], r_in[i], r_conv[i][None], s["a_log"], s["dt_bias"], s["gdn_w"], s["fq_w"],
                 s["fk_w"], s["f_bias"], r_out[i][None], s["post"]]
    return (sm[0]["extra"], grad_x[None], *outs)
```

```python
import jax
import jax.numpy as jnp
from jax import lax
from jax.experimental import pallas as pl
from jax.experimental.pallas import tpu as pltpu

F32, BF16 = jnp.float32, jnp.bfloat16
HEAD_DIM = 128
N_META = 16
CONV_WIDTH = 4
CHUNK = 128
Q_BLOCK = 128
LANES = 128
EPS = 1e-6
PAD_ROWS = Q_BLOCK - N_META
N_DEV = 8
N_CHIP = 4
VMEM_LIMIT = 56 * 1024 * 1024
NEG = -1e30
NARROW = 2 * LANES
MM_TILE = 6 * LANES
DW_IN_PIECES = ((0, 2), (2, 4), (3, 4))

ADAM_LR, ADAM_B1, ADAM_B2, ADAM_EPS, ADAM_WD, ADAM_STEP = 0.001, 0.9, 0.999, 1e-08, 0.01, 10

_DN = {"nn": (((1,), (0,)), ((), ())), "nt": (((1,), (1,)), ((), ())), "tn": (((0,), (0,)), ((), ()))}
_DN3 = {"nn": (((2,), (1,)), ((0,), (0,))), "nt": (((2,), (2,)), ((0,), (0,))), "tn": (((1,), (1,)), ((0,), (0,)))}
_ANY = pl.BlockSpec(memory_space=pl.ANY)
_MESH = pl.DeviceIdType.MESH


def _cp(*sem):
    return pltpu.CompilerParams(dimension_semantics=sem, vmem_limit_bytes=VMEM_LIMIT)


def _dot(a, b, dims="nn", prec=None):
    return lax.dot_general(a, b, _DN[dims], precision=prec, preferred_element_type=F32)


def _bdot(a, b, dims="nn"):
    return _dot(a.astype(BF16), b.astype(BF16), dims)


def _hdot(a, b, dims="nn"):
    return _dot(a, b, dims, prec=lax.Precision.HIGHEST)


def _dot3(a, b, dims="nn"):
    return lax.dot_general(a, b, _DN3[dims], preferred_element_type=F32)


def _bdot3(a, b, dims="nn"):
    return _dot3(a.astype(BF16), b.astype(BF16), dims)


def _split(a):
    hi = a.astype(BF16)
    return hi, (a - hi.astype(F32)).astype(BF16)


def _iota(shape, dim):
    return lax.broadcasted_iota(jnp.int32, shape, dim)


def _sigmoid(z):
    return 1.0 / (1.0 + jnp.exp(-z))


def _softplus(z):
    e = jnp.exp(-jnp.abs(z))
    u = 1.0 + e
    l1p = jnp.where(u == 1.0, e, jnp.log(u) * (e / jnp.where(u == 1.0, 1.0, u - 1.0)))
    return jnp.maximum(z, 0.0) + l1p


def _silu_and_grad(z):
    s = _sigmoid(z)
    return z * s, s * (1.0 + z * (1.0 - s))


def _rms(x):
    return lax.rsqrt(jnp.mean(x * x, axis=-1, keepdims=True) + EPS)


def _h_tile(i, x_ref, meta_ref):
    first = jnp.concatenate([jnp.zeros((PAD_ROWS, x_ref.shape[1]), F32), meta_ref[...]], axis=0)
    return jnp.where(i == 0, first, x_ref[...])


def _x_rows(d):
    return pl.BlockSpec((Q_BLOCK, d), lambda i: (jnp.maximum(i - 1, 0), 0))


def _prenorm(x, meta, w):
    seq, d = x.shape
    lp = seq + Q_BLOCK

    def body(x_ref, m_ref, w_ref, o_ref):
        h = _h_tile(pl.program_id(0), x_ref, m_ref)
        o_ref[...] = (h * _rms(h) * w_ref[...]).astype(BF16)

    return pl.pallas_call(
        body, grid=(lp // Q_BLOCK,),
        in_specs=[_x_rows(d), pl.BlockSpec((N_META, d), lambda i: (0, 0)), pl.BlockSpec((1, d), lambda i: (0, 0))],
        out_specs=pl.BlockSpec((Q_BLOCK, d), lambda i: (i, 0)),
        out_shape=jax.ShapeDtypeStruct((lp, d), BF16), name="prenorm", compiler_params=_cp("parallel"))(x, meta, w)


def _prenorm_gathering(x, w, rider, meta_at):
    seq, d = x.shape
    steps = seq // Q_BLOCK + 1

    def body(start, finish, x_ref, w_ref, o_ref, meta_buf, meta_sem):
        i = pl.program_id(0)
        pl.when(i == 0)(start)

        @pl.when(i < steps - 1)
        def _():
            h = x_ref[...]
            o_ref[...] = (h * _rms(h) * w_ref[...]).astype(BF16)

        @pl.when(i == steps - 1)
        def _():
            finish()
            cp = pltpu.make_async_copy(finish.results[0][meta_at], meta_buf, meta_sem)
            cp.start()
            cp.wait()
            h = jnp.concatenate([jnp.zeros((PAD_ROWS, d), F32), jnp.concatenate([meta_buf[s] for s in range(N_DEV)], axis=1)], axis=0)
            o_ref[...] = (h * _rms(h) * w_ref[...]).astype(BF16)

    (xn,), got = _hosted_call(
        body, [rider], 2, 1, 2, grid=(steps,),
        in_specs=[pl.BlockSpec((Q_BLOCK, d), lambda i: (jnp.minimum(i, steps - 2), 0)), pl.BlockSpec((1, d), lambda i: (0, 0))],
        out_specs=[pl.BlockSpec((Q_BLOCK, d), lambda i: ((i + 1) % steps, 0))],
        out_shape=[jax.ShapeDtypeStruct((seq + Q_BLOCK, d), BF16)],
        scratch_shapes=[pltpu.VMEM((N_DEV, N_META, d // N_DEV), F32), pltpu.SemaphoreType.DMA(())],
        name="prenorm", compiler_params=_cp("arbitrary"))(x, w)
    return xn, got[0]


def _tile(n, want):
    return max(t for t in range(LANES, want + 1, LANES) if n % t == 0)


class _Rider:
    def __init__(self, inputs, out_shapes, scratch, aliases, make):
        self.inputs, self.out_shapes, self.scratch, self.aliases, self.make = inputs, out_shapes, scratch, aliases, make


def _hosted_call(body, riders, n_in, n_out, n_scratch, *, in_specs, out_specs, out_shape, scratch_shapes=(), aliases=None,
                 **kw):
    riders = [r for r in riders if r is not None]
    r_in = [len(r.inputs) for r in riders]
    r_out = [len(r.out_shapes) for r in riders]
    r_scr = [len(r.scratch) for r in riders]
    al = dict(aliases or {})
    for k, r in enumerate(riders):
        al.update({n_in + sum(r_in[:k]) + i: n_out + sum(r_out[:k]) + o for i, o in r.aliases.items()})

    def full_body(*refs):
        ins, rest = refs[:n_in + sum(r_in)], refs[n_in + sum(r_in):]
        outs, scr = rest[:n_out + sum(r_out)], rest[n_out + sum(r_out):]
        hooks = [r.make(ins[n_in + sum(r_in[:k]):n_in + sum(r_in[:k + 1])], outs[n_out + sum(r_out[:k]):n_out + sum(r_out[:k + 1])],
                        scr[n_scratch + sum(r_scr[:k]):n_scratch + sum(r_scr[:k + 1])]) for k, r in enumerate(riders)]

        def start():
            for h in hooks:
                h[0]()

        def finish():
            for h in hooks:
                h[1]()

        finish.results = [outs[n_out + sum(r_out[:k]):n_out + sum(r_out[:k + 1])] for k in range(len(riders))]
        body(start, finish, *ins[:n_in], *outs[:n_out], *scr[:n_scratch])

    call = pl.pallas_call(
        full_body, in_specs=list(in_specs) + [_ANY] * sum(r_in), out_specs=list(out_specs) + [_ANY] * sum(r_out),
        out_shape=list(out_shape) + [s for r in riders for s in r.out_shapes],
        scratch_shapes=list(scratch_shapes) + [s for r in riders for s in r.scratch], input_output_aliases=al, **kw)

    def run(*args):
        res = call(*args, *[t for r in riders for t in r.inputs])
        return res[:n_out], [res[n_out + sum(r_out[:k]):n_out + sum(r_out[:k + 1])] for k in range(len(riders))]

    return run


def _matmul(a, b, dims, tn, out_dtype, name, rider=None, a_cols=None, b_rows=None, acc=None):
    a_shape = a.shape if a_cols is None else (a.shape[0], a_cols[0])
    a_index = 0 if a_cols is None else a_cols[1]
    m = a_shape[1] if dims == "tn" else a_shape[0]
    n = b.shape[0] if dims == "nt" else b.shape[1]
    kdim = b.shape[1] if dims == "nt" else b.shape[0] if b_rows is None else b_rows[0]
    b_index = 0 if b_rows is None else b_rows[1]
    tn = _tile(n, tn)
    steps = n // tn
    b_spec = pl.BlockSpec((tn, kdim), lambda j: (j, 0)) if dims == "nt" else pl.BlockSpec((kdim, tn), lambda j: (b_index, j))
    o_spec = pl.BlockSpec((m, tn), lambda j: (0, j))

    def body(start, finish, a_ref, b_ref, *refs):
        pl.when(pl.program_id(0) == 0)(start)
        prod = _dot(a_ref[...], b_ref[...], dims)
        refs[-1][...] = (prod if acc is None else prod + refs[0][...]).astype(out_dtype)
        pl.when(pl.program_id(0) == steps - 1)(finish)

    (out,), got = _hosted_call(
        body, [rider], 2 + (acc is not None), 1, 0, grid=(steps,),
        in_specs=[pl.BlockSpec(a_shape, lambda j: (0, a_index)), b_spec] + [o_spec] * (acc is not None),
        out_specs=[o_spec], out_shape=[jax.ShapeDtypeStruct((m, n), out_dtype)], aliases={2: 0} if acc is not None else None,
        name=name, compiler_params=_cp("parallel" if rider is None else "arbitrary"))(a, b, *([acc] if acc is not None else []))
    return out if rider is None else (out, got[0])


def _chip_rider(sums):
    def make(ins, outs, scratch):
        local, remote = _chip_exchange_copies(ins[0], outs[0], *scratch)

        def start():
            for cp in [local] + remote:
                cp.start()

        def finish():
            local.wait()
            for cp in remote:
                cp.wait_send()
                cp.wait_recv()

        return start, finish

    return _Rider([sums], [jax.ShapeDtypeStruct(sums.shape, sums.dtype)],
                  [pltpu.SemaphoreType.DMA((N_CHIP - 1,)), pltpu.SemaphoreType.DMA((N_CHIP - 1,)), pltpu.SemaphoreType.DMA((1,))],
                  {}, make)


_HBM = pl.BlockSpec(memory_space=pltpu.HBM)
_SEM = pl.BlockSpec(memory_space=pltpu.SEMAPHORE)


def _chip_exchange_copies(sums_ref, land_ref, send_sems, recv_sems, local_sem):
    x, y, core = lax.axis_index("x"), lax.axis_index("y"), lax.axis_index("c")
    mine = 2 * x + y
    local = pltpu.make_async_copy(sums_ref.at[mine], land_ref.at[mine], local_sem.at[0])
    remote = []
    for k in range(1, N_CHIP):
        px = 1 - x if k & 2 else x
        py = 1 - y if k & 1 else y
        remote.append(pltpu.make_async_remote_copy(
            src_ref=sums_ref.at[2 * px + py], dst_ref=land_ref.at[mine], send_sem=send_sems.at[k - 1],
            recv_sem=recv_sems.at[k - 1], device_id=(px, py, core), device_id_type=_MESH))
    return local, remote


def _chip_exchange_start(sums, name):
    def body(s_ref, send_sems, recv_sems, local_sem, s_thru, land_ref, token):
        local, remote = _chip_exchange_copies(s_ref, land_ref, send_sems, recv_sems, local_sem)
        for cp in [local] + remote:
            cp.start()
        token[...] = jnp.zeros_like(token)

    *flight, token = pl.pallas_call(
        body, name=name,
        out_shape=(pltpu.SemaphoreType.DMA((N_CHIP - 1,)), pltpu.SemaphoreType.DMA((N_CHIP - 1,)), pltpu.SemaphoreType.DMA((1,)),
                   pltpu.HBM(sums.shape, sums.dtype), pltpu.HBM(sums.shape, sums.dtype), jax.ShapeDtypeStruct((8, LANES), F32)),
        in_specs=(_HBM,), out_specs=(_SEM, _SEM, _SEM, _HBM, _HBM, pl.BlockSpec(memory_space=pltpu.VMEM)),
        input_output_aliases={0: 3},
        compiler_params=pltpu.CompilerParams(has_side_effects=pltpu.SideEffectType.DATAFLOW_SIDE_EFFECTING))(
            pltpu.with_memory_space_constraint(sums, pltpu.HBM))
    return flight, token


def _chip_exchange_wait(flights, after, name):
    n = len(flights)

    def body(*refs):
        for i in range(n):
            s_ref, land_ref = refs[2 * i:2 * i + 2]
            local, remote = _chip_exchange_copies(s_ref, land_ref, *refs[2 * n + 3 * i:2 * n + 3 * i + 3])
            local.wait()
            for cp in remote:
                cp.wait_send()
                cp.wait_recv()

    buffers = [b for f in flights for b in f[3:]]
    res = pl.pallas_call(
        body, name=name, out_shape=tuple(pltpu.HBM(b.shape, b.dtype) for b in buffers),
        in_specs=(_HBM,) * (2 * n) + (_SEM,) * (3 * n) + (_ANY,), out_specs=(_HBM,) * (2 * n),
        input_output_aliases={i: i for i in range(2 * n)},
        compiler_params=pltpu.CompilerParams(has_side_effects=pltpu.SideEffectType.DATAFLOW_SIDE_EFFECTING))(
            *buffers, *[s for f in flights for s in f[:3]], after)
    return res[1::2]


def _dxn(dproj, wfull, riders, tk, name):
    m, k = dproj.shape
    n = wfull.shape[0]
    tk = _tile(k, tk)
    steps = k // tk

    def body(start, finish, a_ref, b_ref, o_ref):
        j = pl.program_id(0)

        @pl.when(j == 0)
        def _():
            start()
            o_ref[...] = jnp.zeros_like(o_ref)
        o_ref[...] += _dot(a_ref[...], b_ref[...], "nt")
        pl.when(j == steps - 1)(finish)

    (dxn,), got = _hosted_call(
        body, riders, 2, 1, 0, grid=(steps,),
        in_specs=[pl.BlockSpec((m, tk), lambda j: (0, j)), pl.BlockSpec((n, tk), lambda j: (0, j))],
        out_specs=[pl.BlockSpec((m, n), lambda j: (0, 0))], out_shape=[jax.ShapeDtypeStruct((m, n), F32)],
        name=name, compiler_params=_cp("arbitrary"))(dproj, wfull)
    return dxn, got


def _conv_taps(x, w):
    c = x * w[CONV_WIDTH - 1:CONV_WIDTH, :]
    for j in range(CONV_WIDTH - 1):
        c = c + pltpu.roll(x, CONV_WIDTH - 1 - j, 0) * w[j:j + 1, :]
    return c


def _gdn_prep(proj, conv_wt, nh):
    lp = proj.shape[0]
    scale = HEAD_DIM ** -0.5

    def body(x_ref, w_ref, o_ref):
        which = pl.program_id(0) // nh
        c = _conv_taps(x_ref[...], w_ref[...])
        s = c * _sigmoid(c)
        r = lax.rsqrt(jnp.sum(s * s, axis=-1, keepdims=True) + EPS)
        f = jnp.where(which == 0, r * scale, jnp.where(which == 1, r, 1.0))
        o_ref[...] = jnp.where(_iota(s.shape, 0) >= PAD_ROWS, s * f, 0.0)

    return pl.pallas_call(
        body, grid=(3 * nh,),
        in_specs=[pl.BlockSpec((lp, LANES), lambda s: (0, s)), pl.BlockSpec((CONV_WIDTH, LANES), lambda s: (0, s))],
        out_specs=pl.BlockSpec((lp, LANES), lambda s: (0, s)),
        out_shape=jax.ShapeDtypeStruct((lp, 3 * nh * HEAD_DIM), F32), name="gdn_prep",
        compiler_params=_cp("parallel"))(proj, conv_wt)


def _gdn_prep_bwd(proj, conv_wt, dq, dk, dv, dproj, nh):
    lp = proj.shape[0]
    scale = HEAD_DIM ** -0.5
    part = lambda p: pl.BlockSpec((lp, LANES), lambda s: (0, jnp.clip(s - p * nh, 0, nh - 1)))

    def body(x_ref, w_ref, dq_ref, dk_ref, dv_ref, _, dx_ref, dw_ref):
        which = pl.program_id(0) // nh
        x = x_ref[...]
        w = w_ref[...]
        c = _conv_taps(x, w)
        sg = _sigmoid(c)
        s = c * sg
        r = lax.rsqrt(jnp.sum(s * s, axis=-1, keepdims=True) + EPS)
        dy = jnp.where(which == 0, dq_ref[...], jnp.where(which == 1, dk_ref[...], dv_ref[...]))
        dy = jnp.where(_iota(s.shape, 0) >= PAD_ROWS, dy, 0.0)
        y0 = s * r
        dy0 = dy * jnp.where(which == 0, scale, 1.0)
        ds_n = r * (dy0 - y0 * jnp.sum(dy0 * y0, axis=-1, keepdims=True))
        ds = jnp.where(which == 2, dy, ds_n)
        dc = ds * (sg * (1.0 + c * (1.0 - sg)))
        dx = dc * w[CONV_WIDTH - 1:CONV_WIDTH, :]
        rows = [jnp.sum(dc * x, axis=0, keepdims=True)]
        for j in range(CONV_WIDTH - 2, -1, -1):
            sh = CONV_WIDTH - 1 - j
            dx = dx + pltpu.roll(dc, lp - sh, 0) * w[j:j + 1, :]
            rows.insert(0, jnp.sum(dc * pltpu.roll(x, sh, 0), axis=0, keepdims=True))
        dx_ref[...] = dx.astype(BF16)
        dw_ref[...] = jnp.concatenate(rows, axis=0)

    strip = pl.BlockSpec((lp, LANES), lambda s: (0, s))
    taps = pl.BlockSpec((CONV_WIDTH, LANES), lambda s: (0, s))
    return pl.pallas_call(
        body, grid=(3 * nh,), in_specs=[strip, taps, part(0), part(1), part(2), _ANY], out_specs=[strip, taps],
        out_shape=[jax.ShapeDtypeStruct(dproj.shape, BF16), jax.ShapeDtypeStruct((CONV_WIDTH, 3 * nh * HEAD_DIM), F32)],
        input_output_aliases={5: 0}, name="gdn_prep_bwd", compiler_params=_cp("parallel"))(proj, conv_wt, dq, dk, dv, dproj)


def _gates(proj, bias_row, nega_row, nh):
    lp = proj.shape[0]
    nc = lp // CHUNK

    def body(p_ref, b_ref, a_ref, g_ref, gt3_ref, gtf_ref):
        lane = _iota((CHUNK, LANES), 1)
        tri = (_iota((CHUNK, CHUNK), 0) >= _iota((CHUNK, CHUNK), 1)).astype(F32)

        def step(n, carry):
            r0 = pl.multiple_of(n * CHUNK, CHUNK)
            z = p_ref[pl.ds(r0, CHUNK), :] + b_ref[...]
            base = jnp.where(lane < nh, _sigmoid(z),
                             jnp.where(lane < 2 * nh, a_ref[...] * _softplus(z),
                                       jnp.where(lane < 3 * nh, -_softplus(-z), 0.0)))
            base = jnp.where(r0 + _iota((CHUNK, LANES), 0) >= PAD_ROWS, base, 0.0)
            cs = _hdot(tri, base)
            run = jnp.where((lane >= 2 * nh) & (lane < 3 * nh), cs + carry, cs)
            sh = pltpu.roll(run, 2 * nh, 1)
            out = base + jnp.where((lane >= 3 * nh) & (lane < 5 * nh), sh, 0.0)
            g_ref[pl.ds(r0, CHUNK), :] = out
            gt3_ref[n] = out.T
            return carry + cs[CHUNK - 1:CHUNK, :]

        lax.fori_loop(0, nc, step, jnp.zeros((1, LANES), F32))
        gtf_ref[...] = g_ref[...].T

    vec = pl.BlockSpec((1, LANES), lambda i: (0, 0))
    return pl.pallas_call(
        body, grid=(1,), in_specs=[pl.BlockSpec((lp, LANES), lambda i: (0, 8 * nh)), vec, vec],
        out_specs=[pl.BlockSpec((lp, LANES), lambda i: (0, 0)), pl.BlockSpec((nc, LANES, CHUNK), lambda i: (0, 0, 0)),
                   pl.BlockSpec((LANES, lp), lambda i: (0, 0))],
        out_shape=[jax.ShapeDtypeStruct((lp, LANES), F32), jax.ShapeDtypeStruct((nc, LANES, CHUNK), F32),
                   jax.ShapeDtypeStruct((LANES, lp), F32)],
        name="gates", compiler_params=_cp("arbitrary"))(proj, bias_row, nega_row)


def _gates_bwd(proj, bias_row, nega_row, gates, dgate_gdn, dc_t, dproj, nh):
    lp = proj.shape[0]
    nc = lp // CHUNK

    def body(p_ref, b_ref, a_ref, g_ref, dg_ref, dc_ref, _, dz_ref, sm_ref, dct_scr):
        lane = _iota((CHUNK, LANES), 1)
        triu = (_iota((CHUNK, CHUNK), 0) <= _iota((CHUNK, CHUNK), 1)).astype(F32)
        dct_scr[...] = dc_ref[...].T
        sm_ref[...] = jnp.zeros_like(sm_ref)
        dz_ref[:, LANES:] = jnp.zeros((lp, NARROW - LANES), BF16)

        def step(i, carry):
            n = nc - 1 - i
            r0 = pl.multiple_of(n * CHUNK, CHUNK)
            z = p_ref[pl.ds(r0, CHUNK), :] + b_ref[...]
            gt = g_ref[pl.ds(r0, CHUNK), :]
            dgd = dg_ref[pl.ds(r0, CHUNK), :]
            dch = dct_scr[pl.ds(r0, CHUNK), :]
            rc = _hdot(triu, dch) + carry
            sg = _sigmoid(z)
            dz = jnp.where(lane < nh, dgd * sg * (1.0 - sg),
                           jnp.where(lane < 2 * nh, dgd * a_ref[...] * sg,
                                     jnp.where(lane < 3 * nh, rc * (1.0 - sg), 0.0)))
            dz = jnp.where(r0 + _iota((CHUNK, LANES), 0) >= PAD_ROWS, dz, 0.0)
            dz_ref[pl.ds(r0, CHUNK), 0:LANES] = dz.astype(BF16)
            sm_ref[0:1, :] += jnp.sum(dz, axis=0, keepdims=True)
            sm_ref[1:2, :] += jnp.sum(jnp.where((lane >= nh) & (lane < 2 * nh), dgd * gt, 0.0), axis=0, keepdims=True)
            return carry + jnp.sum(dch, axis=0, keepdims=True)

        lax.fori_loop(0, nc, step, jnp.zeros((1, LANES), F32))

    vec = pl.BlockSpec((1, LANES), lambda i: (0, 0))
    full = pl.BlockSpec((lp, LANES), lambda i: (0, 0))
    last = pl.BlockSpec((lp, LANES), lambda i: (0, 8 * nh))
    tail = pl.BlockSpec((lp, NARROW), lambda i: (0, 8 * nh * LANES // NARROW))
    return pl.pallas_call(
        body, grid=(1,), in_specs=[last, vec, vec, full, full, pl.BlockSpec((LANES, lp), lambda i: (0, 0)), _ANY],
        out_specs=[tail, pl.BlockSpec((8, LANES), lambda i: (0, 0))],
        out_shape=[jax.ShapeDtypeStruct(dproj.shape, BF16), jax.ShapeDtypeStruct((8, LANES), F32)],
        scratch_shapes=[pltpu.VMEM((lp, LANES), F32)], input_output_aliases={6: 0},
        name="gates_bwd", compiler_params=_cp("arbitrary"))(proj, bias_row, nega_row, gates, dgate_gdn, dc_t, dproj)


def _tri_inv(a):
    t = jnp.where(_iota(a.shape, 1) == _iota(a.shape, 2), 1.0, 0.0) - a
    p = a
    for _ in range(CHUNK.bit_length() - 2):
        ph, pw = _split(p)
        p = _dot3(ph, ph) + (_dot3(ph, pw) + _dot3(pw, ph))
        ph, pw = _split(p)
        th, tw = _split(t)
        t = t + (_dot3(th, ph) + (_dot3(th, pw) + _dot3(tw, ph)))
    return t


def _gdn_chunk(q, k, v, beta, gc, gr, t=None):
    ii, jj = _iota((1, CHUNK, CHUNK), 1), _iota((1, CHUNK, CHUNK), 2)
    causal, strict = ii >= jj, ii > jj
    dm = jnp.where(causal, jnp.exp(jnp.where(causal, gc - gr, 0.0)), 0.0)
    kk = _bdot3(k, k, "nt")
    a = jnp.where(strict, beta * kk * dm, 0.0)
    if t is None:
        t = _tri_inv(a)
    eg = jnp.exp(gc)
    glast = gc[:, CHUNK - 1:CHUNK, :]
    ekd = jnp.exp(glast - gc)
    bv = beta * v
    bk = (beta * eg) * k
    ub = _bdot3(t, jnp.concatenate([bv, bk], axis=2))
    qk = _bdot3(q, k, "nt")
    return dict(causal=causal, strict=strict, dm=dm, kk=kk, a=a, t=t, eg=eg, ekd=ekd, bv=bv, bk=bk,
                u=ub[:, :, :HEAD_DIM], w=ub[:, :, HEAD_DIM:], qk=qk, aqk=jnp.where(causal, qk * dm, 0.0),
                q_dec=q * eg, k_dec=k * ekd, decay=jnp.exp(glast))


def _heads(ref, nh):
    return jnp.stack([ref[:, h * HEAD_DIM:(h + 1) * HEAD_DIM] for h in range(nh)], axis=0)


def _gdn_chunk_inputs(q_ref, k_ref, v_ref, g, gt, nh):
    col = lambda o: jnp.stack([g[:, o + h:o + h + 1] for h in range(nh)], axis=0)
    gr = jnp.stack([gt[3 * nh + h:3 * nh + h + 1, :] for h in range(nh)], axis=0)
    return _heads(q_ref, nh), _heads(k_ref, nh), _heads(v_ref, nh), col(0), col(3 * nh), gr


def _gdn_fwd(qkv, gates, gt3, nh, rider=None):
    lp = qkv.shape[0]
    nc = lp // CHUNK
    w = nh * HEAD_DIM

    def body(start, finish, q_ref, k_ref, v_ref, g_ref, gt_ref, o_ref, sall_ref, tall_ref, s_scr):
        @pl.when(pl.program_id(0) == 0)
        def _():
            start()
            s_scr[...] = jnp.zeros_like(s_scr)
        c = _gdn_chunk(*_gdn_chunk_inputs(q_ref, k_ref, v_ref, g_ref[...], gt_ref[0], nh))
        s = s_scr[...]
        sall_ref[0] = s
        tall_ref[0] = c["t"]
        v_new = c["u"] - _bdot3(c["w"], s)
        o = _bdot3(c["q_dec"], s) + _bdot3(c["aqk"], v_new)
        s_scr[...] = s * c["decay"] + _bdot3(c["k_dec"], v_new, "tn")
        for h in range(nh):
            o_ref[:, h * HEAD_DIM:(h + 1) * HEAD_DIM] = o[h]
        pl.when(pl.program_id(0) == nc - 1)(finish)

    outs, got = _hosted_call(
        body, [rider], 5, 3, 1, grid=(nc,),
        in_specs=[pl.BlockSpec((CHUNK, w), lambda n: (n, 0)), pl.BlockSpec((CHUNK, w), lambda n: (n, 1)),
                  pl.BlockSpec((CHUNK, w), lambda n: (n, 2)), pl.BlockSpec((CHUNK, LANES), lambda n: (n, 0)),
                  pl.BlockSpec((1, LANES, CHUNK), lambda n: (n, 0, 0))],
        out_specs=[pl.BlockSpec((CHUNK, w), lambda n: (n, 0)),
                   pl.BlockSpec((1, nh, HEAD_DIM, HEAD_DIM), lambda n: (n, 0, 0, 0)),
                   pl.BlockSpec((1, nh, CHUNK, CHUNK), lambda n: (n, 0, 0, 0))],
        out_shape=[jax.ShapeDtypeStruct((lp, w), F32), jax.ShapeDtypeStruct((nc, nh, HEAD_DIM, HEAD_DIM), F32),
                   jax.ShapeDtypeStruct((nc, nh, CHUNK, CHUNK), F32)],
        scratch_shapes=[pltpu.VMEM((nh, HEAD_DIM, HEAD_DIM), F32)],
        name="gdn_fwd", compiler_params=_cp("arbitrary"))(qkv, qkv, qkv, gates, gt3)
    return outs, (got[0] if got else None)


def _gdn_bwd(qkv, gates, gt3, s_all, t_all, do, nh, rider=None):
    lp = qkv.shape[0]
    nc = lp // CHUNK
    w = nh * HEAD_DIM
    rev = lambda n: nc - 1 - n

    def body(start, finish, q_ref, k_ref, v_ref, g_ref, gt_ref, s_ref, t_ref, do_ref, dq_ref, dk_ref, dv_ref, dg_ref, ds_scr):
        @pl.when(pl.program_id(0) == 0)
        def _():
            start()
            ds_scr[...] = jnp.zeros_like(ds_scr)
        q, k, v, beta, gc, gr = _gdn_chunk_inputs(q_ref, k_ref, v_ref, g_ref[...], gt_ref[0], nh)
        c = _gdn_chunk(q, k, v, beta, gc, gr, t_ref[0])
        s = s_ref[0]
        dsn = ds_scr[...]
        dout = _heads(do_ref, nh)
        v_new = c["u"] - _bdot3(c["w"], s)
        dq_dec = _bdot3(dout, s, "nt")
        daqk = jnp.where(c["causal"], _bdot3(dout, v_new, "nt"), 0.0)
        dv_new = _bdot3(c["aqk"], dout, "tn") + _bdot3(c["k_dec"], dsn)
        dk_dec = _bdot3(v_new, dsn, "nt")
        ddecay = jnp.sum(jnp.sum(dsn * s, axis=2, keepdims=True), axis=1, keepdims=True)
        dw = -_bdot3(dv_new, s, "nt")
        ds_scr[...] = _bdot3(c["q_dec"], dout, "tn") + c["decay"] * dsn - _bdot3(c["w"], dv_new, "tn")
        duw = jnp.concatenate([dv_new, dw], axis=2)
        dt = _bdot3(duw, jnp.concatenate([c["bv"], c["bk"]], axis=2), "nt")
        dbvk = _bdot3(c["t"], duw, "tn")
        dbv, dbk = dbvk[:, :, :HEAD_DIM], dbvk[:, :, HEAD_DIM:]
        da = jnp.where(c["strict"], -_bdot3(_bdot3(c["t"], dt, "tn"), c["t"], "nt"), 0.0)
        dkk = da * beta * c["dm"]
        dqk = daqk * c["dm"]
        e = da * c["a"] + daqk * c["aqk"]
        dq = dq_dec * c["eg"] + _bdot3(dqk, k)
        dk = (dk_dec * c["ekd"] + _bdot3(dkk, k) + _bdot3(dkk, k, "tn") + _bdot3(dqk, q, "tn")
              + (beta * c["eg"]) * dbk)
        dv = beta * dbv
        rs = lambda x: jnp.sum(x, axis=2, keepdims=True)
        dbeta = rs(dbv * v) + c["eg"] * rs(dbk * k) + rs(da * c["kk"] * c["dm"])
        kd_term = rs(dk_dec * c["k_dec"])
        eh, ew = _split(e)
        ones = jnp.ones((nh, CHUNK, LANES), BF16)
        col_sums = (_dot3(eh, ones, "tn") + _dot3(ew, ones, "tn"))[:, :, 0:1]
        dg_cum = rs(dq_dec * c["q_dec"]) - kd_term + rs(dbk * c["bk"]) + rs(e) - col_sums
        last = jnp.sum(kd_term, axis=1, keepdims=True) + ddecay * c["decay"]
        dg_cum = dg_cum + jnp.where(_iota((1, CHUNK, 1), 1) == CHUNK - 1, last, 0.0)
        lane = _iota((CHUNK, LANES), 1)
        acc = jnp.zeros((CHUNK, LANES), F32)
        for h in range(nh):
            sl = slice(h * HEAD_DIM, (h + 1) * HEAD_DIM)
            dq_ref[:, sl] = dq[h]
            dk_ref[:, sl] = dk[h]
            dv_ref[:, sl] = dv[h]
            acc = acc + jnp.where(lane == h, dbeta[h], 0.0) + jnp.where(lane == nh + h, dg_cum[h], 0.0)
        triu = (_iota((CHUNK, CHUNK), 0) <= _iota((CHUNK, CHUNK), 1)).astype(F32)
        dg_ref[...] = jnp.where(lane < nh, acc, _hdot(triu, acc))
        pl.when(pl.program_id(0) == nc - 1)(finish)

    outs, got = _hosted_call(
        body, [rider], 8, 4, 1, grid=(nc,),
        in_specs=[pl.BlockSpec((CHUNK, w), lambda n: (rev(n), 0)), pl.BlockSpec((CHUNK, w), lambda n: (rev(n), 1)),
                  pl.BlockSpec((CHUNK, w), lambda n: (rev(n), 2)), pl.BlockSpec((CHUNK, LANES), lambda n: (rev(n), 0)),
                  pl.BlockSpec((1, LANES, CHUNK), lambda n: (rev(n), 0, 0)),
                  pl.BlockSpec((1, nh, HEAD_DIM, HEAD_DIM), lambda n: (rev(n), 0, 0, 0)),
                  pl.BlockSpec((1, nh, CHUNK, CHUNK), lambda n: (rev(n), 0, 0, 0)),
                  pl.BlockSpec((CHUNK, w), lambda n: (rev(n), 0))],
        out_specs=[pl.BlockSpec((CHUNK, w), lambda n: (rev(n), 0))] * 3 + [pl.BlockSpec((CHUNK, LANES), lambda n: (rev(n), 0))],
        out_shape=[jax.ShapeDtypeStruct((lp, w), F32)] * 3 + [jax.ShapeDtypeStruct((lp, LANES), F32)],
        scratch_shapes=[pltpu.VMEM((nh, HEAD_DIM, HEAD_DIM), F32)],
        name="gdn_bwd", compiler_params=_cp("arbitrary"))(qkv, qkv, qkv, gates, gt3, s_all, t_all, do)
    return outs, (got[0] if got else None)


def _merge_gdn(o_gdn, proj, norm_w, nh):
    lp = o_gdn.shape[0]

    def body(o_ref, z_ref, w_ref, m_ref):
        o = o_ref[...]
        z = z_ref[...]
        m_ref[...] = (o * _rms(o) * w_ref[...] * (z * _sigmoid(z))).astype(BF16)

    return pl.pallas_call(
        body, grid=(nh,),
        in_specs=[pl.BlockSpec((lp, LANES), lambda s: (0, s)), pl.BlockSpec((lp, LANES), lambda s: (0, 3 * nh + s)),
                  pl.BlockSpec((1, LANES), lambda s: (0, 0))],
        out_specs=pl.BlockSpec((lp, LANES), lambda s: (0, s)),
        out_shape=jax.ShapeDtypeStruct((lp, 2 * nh * HEAD_DIM), BF16), name="merge_gdn",
        compiler_params=_cp("parallel"))(o_gdn, proj, norm_w)


def _merge_gdn_bwd(o_gdn, proj, norm_w, dmerged, nh):
    lp = o_gdn.shape[0]

    def body(o_ref, z_ref, w_ref, dm_ref, do_ref, dz_ref, dw_ref):
        o = o_ref[...]
        r = _rms(o)
        xh = o * r
        silu, dsilu = _silu_and_grad(z_ref[...])
        dm = dm_ref[...]
        dn = dm * silu
        dz_ref[...] = (dm * (xh * w_ref[...]) * dsilu).astype(BF16)
        dnw = dn * w_ref[...]
        do_ref[...] = r * (dnw - xh * jnp.mean(dnw * xh, axis=-1, keepdims=True))

        @pl.when(pl.program_id(0) == 0)
        def _():
            dw_ref[...] = jnp.zeros_like(dw_ref)
        dw_ref[...] += jnp.sum(dn * xh, axis=0, keepdims=True)

    w = nh * HEAD_DIM
    return pl.pallas_call(
        body, grid=(nh,),
        in_specs=[pl.BlockSpec((lp, LANES), lambda s: (0, s)), pl.BlockSpec((lp, LANES), lambda s: (0, 3 * nh + s)),
                  pl.BlockSpec((1, LANES), lambda s: (0, 0)), pl.BlockSpec((lp, LANES), lambda s: (0, s))],
        out_specs=[pl.BlockSpec((lp, LANES), lambda s: (0, s)), pl.BlockSpec((lp, LANES), lambda s: (0, 3 * nh + s)),
                   pl.BlockSpec((1, LANES), lambda s: (0, 0))],
        out_shape=[jax.ShapeDtypeStruct((lp, w), F32), jax.ShapeDtypeStruct((lp, 8 * w + NARROW), BF16),
                   jax.ShapeDtypeStruct((1, LANES), F32)],
        name="merge_gdn_bwd", compiler_params=_cp("arbitrary"))(o_gdn, proj, norm_w, dmerged)


def _fox_prep(proj, qk_w, nh):
    lp = proj.shape[0]

    def body(x_ref, w_ref, o_ref):
        x = x_ref[...]
        o_ref[...] = x * _rms(x) * w_ref[0]

    return pl.pallas_call(
        body, grid=(2 * nh,),
        in_specs=[pl.BlockSpec((lp, LANES), lambda s: (0, 4 * nh + s)), pl.BlockSpec((1, 1, LANES), lambda s: (s // nh, 0, 0))],
        out_specs=pl.BlockSpec((lp, LANES), lambda s: (0, s)),
        out_shape=jax.ShapeDtypeStruct((lp, 2 * nh * HEAD_DIM), F32), name="fox_prep",
        compiler_params=_cp("parallel"))(proj, qk_w)


def _fox_prep_bwd(proj, qk_w, dq, dk, dproj, nh):
    lp = proj.shape[0]
    part = lambda p: pl.BlockSpec((lp, LANES), lambda s: (0, jnp.clip(s - p * nh, 0, nh - 1)))

    def body(x_ref, w_ref, dq_ref, dk_ref, _, dx_ref, dw_ref):
        x = x_ref[...]
        r = _rms(x)
        xh = x * r
        dy = jnp.where(pl.program_id(0) < nh, dq_ref[...], dk_ref[...])
        dyw = dy * w_ref[0]
        dx_ref[...] = (r * (dyw - xh * jnp.mean(dyw * xh, axis=-1, keepdims=True))).astype(BF16)

        @pl.when(pl.program_id(0) % nh == 0)
        def _():
            dw_ref[...] = jnp.zeros_like(dw_ref)
        dw_ref[0] += jnp.sum(dy * xh, axis=0, keepdims=True)

    strip = pl.BlockSpec((lp, LANES), lambda s: (0, 4 * nh + s))
    wsp = pl.BlockSpec((1, 1, LANES), lambda s: (s // nh, 0, 0))
    return pl.pallas_call(
        body, grid=(2 * nh,), in_specs=[strip, wsp, part(0), part(1), _ANY], out_specs=[strip, wsp],
        out_shape=[jax.ShapeDtypeStruct(dproj.shape, BF16), jax.ShapeDtypeStruct((2, 1, LANES), F32)],
        input_output_aliases={4: 0}, name="fox_prep_bwd", compiler_params=_cp("arbitrary"))(proj, qk_w, dq, dk, dproj)


def _fox_probs(q, k, gates, crow, h, i, nh, lse=None):
    kl = k.shape[0]
    lane = _iota((Q_BLOCK, LANES), 1)
    ct = jnp.sum(jnp.where(lane == 4 * nh + h, gates, 0.0), axis=1, keepdims=True)
    tq, kq = _iota((Q_BLOCK, Q_BLOCK), 0), _iota((Q_BLOCK, Q_BLOCK), 1)
    qs = q * (HEAD_DIM ** -0.5)
    if i == 0:
        s = _bdot(qs, k, "nt") + (ct - crow)
        s = jnp.where((kq <= tq) & ((kq >= PAD_ROWS) | (tq < PAD_ROWS)), s, NEG)
    else:
        crow = jnp.where(_iota((1, kl), 1) < PAD_ROWS, -NEG, crow)
        s = _bdot(qs, k, "nt") + (ct - crow)
        s = jnp.concatenate([s[:, :kl - Q_BLOCK], jnp.where(kq <= tq, s[:, kl - Q_BLOCK:], NEG)], axis=1)
    if lse is not None:
        return jnp.exp(s - lse)
    m = jnp.max(s, axis=1, keepdims=True)
    p = jnp.exp(s - m)
    tot = jnp.sum(p, axis=1, keepdims=True)
    return p / tot, m + jnp.log(tot)


FOX_HEADS_PER_STEP = 2


def _fox_specs(lp, nh):
    hw = FOX_HEADS_PER_STEP * LANES
    return [pl.BlockSpec((Q_BLOCK, hw), lambda g, i: (i, g)),
            pl.BlockSpec((lp, hw), lambda g, i: (0, nh // FOX_HEADS_PER_STEP + g)),
            pl.BlockSpec((lp, hw), lambda g, i: (0, 6 * nh // FOX_HEADS_PER_STEP + g)),
            pl.BlockSpec((Q_BLOCK, LANES), lambda g, i: (i, 0)),
            pl.BlockSpec((LANES, lp), lambda g, i: (0, 0))]


def _fox_fwd(qkn, proj, gates, gtf, nh):
    lp = qkn.shape[0]

    def body(q_ref, k_ref, v_ref, g_ref, gt_ref, o_ref, lse_ref):
        g, i = pl.program_id(0), pl.program_id(1)
        for j in range(lp // Q_BLOCK):
            @pl.when(i == j)
            def _(j=j):
                kl = (j + 1) * Q_BLOCK
                for hh in range(FOX_HEADS_PER_STEP):
                    h = FOX_HEADS_PER_STEP * g + hh
                    sl = slice(hh * LANES, (hh + 1) * LANES)
                    p, lse = _fox_probs(q_ref[:, sl], k_ref[0:kl, sl], g_ref[...], gt_ref[pl.ds(4 * nh + h, 1), :][:, 0:kl],
                                        h, j, nh)
                    o_ref[:, sl] = _bdot(p, v_ref[0:kl, sl])
                    lse_ref[:, sl] = jnp.broadcast_to(lse, (Q_BLOCK, LANES))

    blk = pl.BlockSpec((Q_BLOCK, FOX_HEADS_PER_STEP * LANES), lambda g, i: (i, g))
    return pl.pallas_call(
        body, grid=(nh // FOX_HEADS_PER_STEP, lp // Q_BLOCK), in_specs=_fox_specs(lp, nh), out_specs=[blk, blk],
        out_shape=[jax.ShapeDtypeStruct((lp, nh * HEAD_DIM), F32)] * 2, name="fox_fwd",
        compiler_params=_cp("parallel", "parallel"))(qkn, qkn, proj, gates, gtf)


def _fox_bwd(qkn, proj, gates, gtf, lse, do, dproj, nh):
    lp = qkn.shape[0]
    nq = lp // Q_BLOCK
    w = nh * HEAD_DIM
    scale = HEAD_DIM ** -0.5

    def body(q_ref, k_ref, v_ref, g_ref, gt_ref, lse_ref, do_ref, _, dq_ref, dk_ref, dc_ref, dv_ref, dv_scr):
        g, i = pl.program_id(0), pl.program_id(1)

        @pl.when(i == 0)
        def _():
            dk_ref[...] = jnp.zeros_like(dk_ref)
            dv_scr[...] = jnp.zeros_like(dv_scr)
            dc_ref[...] = jnp.zeros_like(dc_ref)
        for j in range(nq):
            @pl.when(i == j)
            def _(j=j):
                kl = (j + 1) * Q_BLOCK
                for hh in range(FOX_HEADS_PER_STEP):
                    h = FOX_HEADS_PER_STEP * g + hh
                    sl = slice(hh * LANES, (hh + 1) * LANES)
                    q, k = q_ref[:, sl], k_ref[0:kl, sl]
                    p = _fox_probs(q, k, g_ref[...], gt_ref[pl.ds(4 * nh + h, 1), :][:, 0:kl], h, j, nh,
                                   lse_ref[:, sl][:, 0:1])
                    dout = do_ref[:, sl]
                    dp = _bdot(dout, v_ref[0:kl, sl], "nt")
                    ds = p * (dp - jnp.sum(p * dp, axis=1, keepdims=True))
                    dq_ref[:, sl] = _bdot(ds, k) * scale
                    dk_ref[0:kl, sl] += _bdot(ds, q * scale, "tn")
                    dv_scr[0:kl, sl] += _bdot(p, dout, "tn")
                    dc_ref[hh, :, 0:kl] -= jnp.sum(ds, axis=0, keepdims=True)

        @pl.when(i == nq - 1)
        def _():
            dv_ref[...] = dv_scr[...].astype(BF16)

    hw = FOX_HEADS_PER_STEP * LANES
    blk = pl.BlockSpec((Q_BLOCK, hw), lambda g, i: (i, g))
    col = pl.BlockSpec((lp, hw), lambda g, i: (0, g))
    return pl.pallas_call(
        body, grid=(nh // FOX_HEADS_PER_STEP, nq), in_specs=_fox_specs(lp, nh) + [blk, blk, _ANY],
        out_specs=[blk, col, pl.BlockSpec((FOX_HEADS_PER_STEP, 1, lp), lambda g, i: (g, 0, 0)),
                   pl.BlockSpec((lp, hw), lambda g, i: (0, 6 * nh // FOX_HEADS_PER_STEP + g))],
        out_shape=[jax.ShapeDtypeStruct((lp, w), F32)] * 2 + [jax.ShapeDtypeStruct((nh, 1, lp), F32),
                                                             jax.ShapeDtypeStruct(dproj.shape, BF16)],
        scratch_shapes=[pltpu.VMEM((lp, hw), F32)], input_output_aliases={7: 3},
        name="fox_bwd", compiler_params=_cp("parallel", "arbitrary"))(qkn, qkn, proj, gates, gtf, lse, do, dproj)


def _merge_fox(o_fox, proj, merged, nh):
    lp = o_fox.shape[0]

    def body(o_ref, z_ref, _, m_ref):
        z = z_ref[...]
        m_ref[...] = (o_ref[...] * (z * _sigmoid(z))).astype(BF16)

    return pl.pallas_call(
        body, grid=(nh,),
        in_specs=[pl.BlockSpec((lp, LANES), lambda s: (0, s)), pl.BlockSpec((lp, LANES), lambda s: (0, 7 * nh + s)), _ANY],
        out_specs=pl.BlockSpec((lp, LANES), lambda s: (0, nh + s)),
        out_shape=jax.ShapeDtypeStruct(merged.shape, BF16), input_output_aliases={2: 0}, name="merge_fox",
        compiler_params=_cp("parallel"))(o_fox, proj, merged)


def _merge_fox_bwd(o_fox, proj, dmerged, dproj, nh):
    lp = o_fox.shape[0]

    def body(o_ref, z_ref, dm_ref, _, do_ref, dz_ref):
        silu, dsilu = _silu_and_grad(z_ref[...])
        dm = dm_ref[...]
        do_ref[...] = dm * silu
        dz_ref[...] = (dm * o_ref[...] * dsilu).astype(BF16)

    w = nh * HEAD_DIM
    return pl.pallas_call(
        body, grid=(nh,),
        in_specs=[pl.BlockSpec((lp, LANES), lambda s: (0, s)), pl.BlockSpec((lp, LANES), lambda s: (0, 7 * nh + s)),
                  pl.BlockSpec((lp, LANES), lambda s: (0, nh + s)), _ANY],
        out_specs=[pl.BlockSpec((lp, LANES), lambda s: (0, s)), pl.BlockSpec((lp, LANES), lambda s: (0, 7 * nh + s))],
        out_shape=[jax.ShapeDtypeStruct((lp, w), F32), jax.ShapeDtypeStruct(dproj.shape, BF16)],
        input_output_aliases={3: 1}, name="merge_fox_bwd", compiler_params=_cp("parallel"))(o_fox, proj, dmerged, dproj)


def _post(out, x, target, post_w):
    lp, d = out.shape

    def body(o_ref, x_ref, t_ref, w_ref, dy_ref, do_ref, loss_ref, dw_ref):
        i = pl.program_id(0)

        @pl.when(i == 0)
        def _():
            loss_ref[...] = jnp.zeros_like(loss_ref)
            dw_ref[...] = jnp.zeros_like(dw_ref)
        o = o_ref[...]
        r = _rms(o)
        nrm = o * r
        err = jnp.where(i > 0, x_ref[...] + nrm * w_ref[...] - t_ref[...], 0.0)
        loss_ref[0:1, :] += 0.5 * jnp.sum(jnp.sum(err * err, axis=1, keepdims=True), axis=0, keepdims=True) / d
        dy = err / d
        dy_ref[...] = dy
        dw_ref[...] += jnp.sum(dy * nrm, axis=0, keepdims=True)
        dyw = dy * w_ref[...]
        do_ref[...] = (r * (dyw - nrm * jnp.mean(dyw * nrm, axis=-1, keepdims=True))).astype(BF16)

    row = pl.BlockSpec((Q_BLOCK, d), lambda i: (i, 0))
    vec = pl.BlockSpec((1, d), lambda i: (0, 0))
    return pl.pallas_call(
        body, grid=(lp // Q_BLOCK,), in_specs=[row, _x_rows(d), _x_rows(d), vec],
        out_specs=[_x_rows(d), row, pl.BlockSpec((8, LANES), lambda i: (0, 0)), vec],
        out_shape=[jax.ShapeDtypeStruct(x.shape, F32), jax.ShapeDtypeStruct((lp, d), BF16),
                   jax.ShapeDtypeStruct((8, LANES), F32), jax.ShapeDtypeStruct((1, d), F32)],
        name="post", compiler_params=_cp("arbitrary"))(out, x, target, post_w)


def _prenorm_bwd(dxn, x, meta, w, dy, rider=None):
    seq, d = x.shape
    lp = seq + Q_BLOCK

    def body(start, finish, dx_ref, x_ref, m_ref, w_ref, dy_ref, gx_ref, gm_ref, dw_ref):
        i = pl.program_id(0)
        pl.when(i == 0)(start)
        h = _h_tile(i, x_ref, m_ref)
        r = _rms(h)
        xh = h * r
        dxn_ = dx_ref[...]
        dxw = dxn_ * w_ref[...]
        dh = jnp.where(i > 0, dy_ref[...], 0.0) + r * (dxw - xh * jnp.mean(dxw * xh, axis=-1, keepdims=True))
        gx_ref[...] = dh

        @pl.when(i == 0)
        def _():
            dw_ref[...] = jnp.zeros_like(dw_ref)
            gm_ref[...] = dh[PAD_ROWS:, :]
        dw_ref[...] += jnp.sum(dxn_ * xh, axis=0, keepdims=True)
        pl.when(i == lp // Q_BLOCK - 1)(finish)

    vec = pl.BlockSpec((1, d), lambda i: (0, 0))
    met = pl.BlockSpec((N_META, d), lambda i: (0, 0))
    outs, got = _hosted_call(
        body, [rider], 5, 3, 0, grid=(lp // Q_BLOCK,),
        in_specs=[pl.BlockSpec((Q_BLOCK, d), lambda i: (i, 0)), _x_rows(d), met, vec, _x_rows(d)],
        out_specs=[_x_rows(d), met, vec],
        out_shape=[jax.ShapeDtypeStruct((seq, d), F32), jax.ShapeDtypeStruct((N_META, d), F32),
                   jax.ShapeDtypeStruct((1, d), F32)],
        name="prenorm_bwd", compiler_params=_cp("arbitrary"))(dxn, x, meta, w, dy)
    return outs, (got[0] if got else None)


def _layer_grads(x, target, meta, pre_w, wfull, conv_wt, a_log, dt_bias, gdn_norm_w, fq_w, fk_w, f_bias, w_out, post_w,
                 late_weights=None, w_out_grads=None):
    nh = a_log.shape[1]
    zpad = jnp.zeros((1, LANES - 3 * nh), F32)
    bias_row = jnp.concatenate([jnp.zeros((1, nh), F32), dt_bias, f_bias, zpad], axis=1)
    nega_row = jnp.concatenate([jnp.zeros((1, nh), F32), -jnp.exp(a_log), jnp.zeros((1, nh), F32), zpad], axis=1)
    qk_w = jnp.stack([fq_w, fk_w])

    if isinstance(wfull, tuple):
        rider, meta_at, project = wfull
        xn, got = _prenorm_gathering(x, pre_w, rider, meta_at)
        proj, wfull, conv_wt, meta = project(xn, got)
    else:
        xn = _prenorm(x, meta, pre_w)
        proj = _matmul(xn, wfull, "nn", MM_TILE, F32, "proj")
    qkv = _gdn_prep(proj, conv_wt, nh)
    gates, gt3, gtf = _gates(proj, bias_row, nega_row, nh)
    (o_gdn, s_all, t_all), got = _gdn_fwd(qkv, gates, gt3, nh, None if late_weights is None else late_weights[0])
    if late_weights is not None:
        w_out = late_weights[1](got)
    qkn = _fox_prep(proj, qk_w, nh)
    o_fox, fox_lse = _fox_fwd(qkn, proj, gates, gtf, nh)
    merged = _merge_fox(o_fox, proj, _merge_gdn(o_gdn, proj, gdn_norm_w, nh), nh)
    out = _matmul(merged, w_out, "nn", 4 * LANES, F32, "out_proj")
    dy, dout, loss_blk, dpost_w = _post(out, x, target, post_w)

    dw_out = _matmul(merged, dout, "tn", 4 * LANES, BF16, "dw_out")
    if w_out_grads is None:
        dmerged, gdn_rider = _matmul(dout, w_out, "nt", 4 * LANES, F32, "dmerged"), None
    else:
        dmerged, got = _matmul(dout, w_out, "nt", 4 * LANES, F32, "dmerged", w_out_grads[0](dw_out))
        gdn_rider = w_out_grads[1](dw_out, got)
    do_gdn, dproj, dgdn_norm_w = _merge_gdn_bwd(o_gdn, proj, gdn_norm_w, dmerged, nh)
    do_fox, dproj = _merge_fox_bwd(o_fox, proj, dmerged, dproj, nh)
    dqn, dkn, dc_t, dproj = _fox_bwd(qkn, proj, gates, gtf, fox_lse, do_fox, dproj, nh)
    dproj, dqk_w = _fox_prep_bwd(proj, qk_w, dqn, dkn, dproj, nh)
    (dgq, dgk, dgv, dgate), w_out_parts = _gdn_bwd(qkv, gates, gt3, s_all, t_all, do_gdn, nh, gdn_rider)
    dproj, dconv_wt = _gdn_prep_bwd(proj, conv_wt, dgq, dgk, dgv, dproj, nh)
    dc_rows = jnp.pad(dc_t.reshape(nh, -1), ((2 * nh, LANES - 3 * nh), (0, 0)))
    dproj, gate_sums = _gates_bwd(proj, bias_row, nega_row, gates, dgate, dc_rows, dproj, nh)
    return dict(
        loss=loss_blk[0:1, 0:1], dy=dy, xn=xn, dproj=dproj, post_w=dpost_w,
        conv_wt=dconv_wt, a_log=gate_sums[1:2, nh:2 * nh], dt_bias=gate_sums[0:1, nh:2 * nh],
        gdn_norm_w=dgdn_norm_w, fq_w=dqk_w[0], fk_w=dqk_w[1], f_bias=gate_sums[0:1, 2 * nh:3 * nh], w_out=dw_out,
        w_out_parts=w_out_parts, wfull=wfull, meta=meta)


def _cast_bf16(a, tr, name):
    r, c = a.shape

    def body(a_ref, o_ref):
        o_ref[...] = a_ref[...].astype(BF16)

    return pl.pallas_call(
        body, grid=(r // tr,), in_specs=[pl.BlockSpec((tr, c), lambda i: (i, 0))],
        out_specs=pl.BlockSpec((tr, c), lambda i: (i, 0)), out_shape=jax.ShapeDtypeStruct((r, c), BF16),
        name=name, compiler_params=_cp("parallel"))(a)


def _column_major(a):
    return jnp.transpose(a, (2, 0, 1))


def _cast_bf16_column_major(a3, pieces, name):
    _, r, c = a3.shape
    rows = r // pieces

    def body(a_ref, *o_refs):
        t = a_ref[...].reshape(LANES, r).T.astype(BF16)
        for k, o_ref in enumerate(o_refs):
            o_ref[...] = t[k * rows:(k + 1) * rows]

    return pl.pallas_call(
        body, grid=(pl.cdiv(c, LANES),), in_specs=[pl.BlockSpec((LANES, 1, r), lambda i: (i, 0, 0))],
        out_specs=[pl.BlockSpec((rows, LANES), lambda i: (0, i))] * pieces,
        out_shape=[jax.ShapeDtypeStruct((rows, c), BF16)] * pieces, name=name, compiler_params=_cp("parallel"))(_column_major(a3))


def _adamw_column_major(w3, parts, m3, v3, name):
    _, r, c = w3.shape
    n_parts = parts[0].shape[0]

    def body(w_ref, *refs):
        p_refs, (m_ref, v_ref, g_ref, d_ref, nm_ref, nv_ref) = refs[:len(parts)], refs[len(parts):]
        sums = []
        for p_ref in p_refs:
            g = p_ref[0].astype(F32)
            for s in range(1, n_parts):
                g = g + p_ref[s].astype(F32)
            sums.append(g)
        g = jnp.concatenate(sums, axis=0).T
        flat = lambda ref: ref[...].reshape(LANES, r)
        m_new = ADAM_B1 * flat(m_ref) + (1.0 - ADAM_B1) * g
        v_new = ADAM_B2 * flat(v_ref) + (1.0 - ADAM_B2) * (g * g)
        m_hat = m_new / (1.0 - ADAM_B1 ** ADAM_STEP)
        v_hat = v_new / (1.0 - ADAM_B2 ** ADAM_STEP)
        delta = -ADAM_LR * (m_hat / (jnp.sqrt(v_hat) + ADAM_EPS) + ADAM_WD * flat(w_ref))
        for ref, val in ((g_ref, g), (d_ref, delta), (nm_ref, m_new), (nv_ref, v_new)):
            ref[...] = val.reshape(LANES, 1, r)

    blk = pl.BlockSpec((LANES, 1, r), lambda i: (i, 0, 0))
    outs = pl.pallas_call(
        body, grid=(pl.cdiv(c, LANES),),
        in_specs=[blk] + [pl.BlockSpec((n_parts, p.shape[1], LANES), lambda i: (0, 0, i)) for p in parts] + [blk, blk],
        out_specs=[blk] * 4, out_shape=[jax.ShapeDtypeStruct((c, 1, r), F32)] * 4, name=name,
        compiler_params=_cp("parallel"))(_column_major(w3), *parts, _column_major(m3), _column_major(v3))
    return [jnp.transpose(o, (1, 2, 0)) for o in outs]


def _gather_copies(ins, outs, send_sems, recv_sems, local_sems):
    n = len(ins)
    x, y, c = lax.axis_index("x"), lax.axis_index("y"), lax.axis_index("c")
    me, sibling = (x, y, c), (x, y, 1 - c)
    xn, yn, dg = (1 - x, y), (x, 1 - y), (1 - x, 1 - y)

    def copy(a, k, block, to, src=None):
        px, py, pc = block
        rows = outs[a].at[4 * px + 2 * py + pc]
        return pltpu.make_async_remote_copy(
            src_ref=rows if src is None else src, dst_ref=rows, send_sem=send_sems.at[a, k],
            recv_sem=recv_sems.at[a, k], device_id=to, device_id_type=_MESH)

    local = [pltpu.make_async_copy(ins[a], outs[a].at[4 * x + 2 * y + c], local_sems.at[a]) for a in range(n)]
    own = [cp for a in range(n) for cp in (copy(a, 0, me, sibling, src=ins[a]), copy(a, 1, me, (*xn, c), src=ins[a]),
                                           copy(a, 2, me, (*yn, c), src=ins[a]))]

    def start():
        for cp in local + own:
            cp.start()

    def finish():
        for a in range(n):
            @pl.when(c == 1)
            def _(a=a):
                copy(a, 1, (*xn, c), me).wait_recv()
                copy(a, 3, (*xn, c), (*yn, c)).start()

            @pl.when(c == 0)
            def _(a=a):
                copy(a, 2, (*yn, c), me).wait_recv()
                copy(a, 3, (*yn, c), (*xn, c)).start()
        for a in range(n):
            pl.when(c == 0)(copy(a, 1, (*xn, c), me).wait_recv)
            copy(a, 4, (*xn, c), sibling).start()
            pl.when(c == 1)(copy(a, 2, (*yn, c), me).wait_recv)
            copy(a, 5, (*yn, c), sibling).start()
        for a in range(n):
            copy(a, 3, (*dg, c), me).wait_recv()
            copy(a, 6, (*dg, c), sibling).start()
        for a in range(n):
            copy(a, 0, sibling, me).wait_recv()
            for k, chip in ((4, xn), (5, yn), (6, dg)):
                copy(a, k, (*chip, 1 - c), me).wait_recv()
                copy(a, k, (*chip, c), sibling).wait_send()
            copy(a, 3, (*xn, c), (*yn, c)).wait_send()
        for cp in own:
            cp.wait_send()
        for cp in local:
            cp.wait()

    return start, finish


def _gather_scratch(n):
    return [pltpu.SemaphoreType.DMA((n, N_DEV - 1)), pltpu.SemaphoreType.DMA((n, N_DEV - 1)), pltpu.SemaphoreType.DMA((n,))]


def _gather_rider(arrays):
    return _Rider(list(arrays), [jax.ShapeDtypeStruct((N_DEV,) + a.shape, a.dtype) for a in arrays],
                  _gather_scratch(len(arrays)), {}, lambda ins, outs, scratch: _gather_copies(ins, outs, *scratch))


def _all_gather(arrays, name):
    n = len(arrays)

    def body(*refs):
        start, finish = _gather_copies(refs[:n], refs[n:2 * n], *refs[2 * n:])
        start()
        finish()

    return pl.pallas_call(
        body, in_specs=[_ANY] * n, out_specs=[_ANY] * n,
        out_shape=[jax.ShapeDtypeStruct((N_DEV,) + a.shape, a.dtype) for a in arrays],
        scratch_shapes=_gather_scratch(n), name=name)(*arrays)


SLAB = 10 * LANES


def _slab_start(blk, nh, cols):
    in_second_half = blk >= N_DEV // 2
    shift = (2 * nh if in_second_half else 0) if isinstance(blk, int) else jnp.where(in_second_half, 2 * nh, 0)
    return (blk * cols - shift) // LANES * LANES


def _pair_rider(dw_rows=None, parts=None, nh=None, after=None):
    if dw_rows is not None:
        r, full = dw_rows.shape
        cols = (full - NARROW + 3 * nh) // N_DEV
        out_shapes = [jax.ShapeDtypeStruct((N_CHIP, r, SLAB), dw_rows.dtype), jax.ShapeDtypeStruct((r, NARROW), dw_rows.dtype)]
    else:
        out_shapes = [jax.ShapeDtypeStruct((N_CHIP,) + parts.shape[1:], parts.dtype)]

    def make(ins, outs, scratch):
        send_sems, recv_sems = scratch
        x, y, c = lax.axis_index("x"), lax.axis_index("y"), lax.axis_index("c")
        kw = lambda k: dict(send_sem=send_sems.at[k], recv_sem=recv_sems.at[k], device_id=(x, y, 1 - c), device_id_type=_MESH)
        copies = []
        for q in range(N_CHIP):
            if dw_rows is not None:
                first = pl.multiple_of(_slab_start(2 * q + 1 - c, nh, cols), LANES)
                copies.append(pltpu.make_async_remote_copy(src_ref=ins[0].at[:, pl.ds(first, SLAB)], dst_ref=outs[0].at[q], **kw(q)))
            else:
                copies.append(pltpu.make_async_remote_copy(src_ref=ins[0].at[2 * q + 1 - c], dst_ref=outs[0].at[q], **kw(q)))
        if dw_rows is not None:
            copies.append(pltpu.make_async_remote_copy(src_ref=ins[0].at[:, pl.ds(full - NARROW, NARROW)], dst_ref=outs[1],
                                                       **kw(N_CHIP)))

        def start():
            for cp in copies:
                cp.start()

        def finish():
            for cp in copies:
                cp.wait()

        return start, finish

    return _Rider([dw_rows if dw_rows is not None else parts] + ([] if after is None else [after]), out_shapes,
                  [pltpu.SemaphoreType.DMA((N_CHIP + 1,)), pltpu.SemaphoreType.DMA((N_CHIP + 1,))], {}, make)


def _relayout_pair_sum(dwfull, got_slabs, got_tail, core, nh, tr, name):
    d, full = dwfull.shape
    w = nh * HEAD_DIM
    cols = (8 * w + 3 * nh) // N_DEV
    segs = _native_segments(nh)

    def block(f_ref, s_ref, t_ref, q, blk):
        st = _slab_start(blk, nh, cols)
        wide = f_ref[:, st:st + SLAB].astype(F32) + s_ref[q].astype(F32)
        tail = f_ref[:, 8 * w:].astype(F32) + t_ref[...].astype(F32)
        pieces = []
        for s0, s1, t0 in segs:
            lo, hi = max(s0, blk * cols), min(s1, (blk + 1) * cols)
            if lo < hi:
                at = t0 + lo - s0
                pieces.append(tail[:, at - 8 * w:at - 8 * w + hi - lo] if at >= 8 * w else wide[:, at - st:at - st + hi - lo])
        return (pieces[0] if len(pieces) == 1 else jnp.concatenate(pieces, axis=1)).astype(dwfull.dtype)

    def body(core_ref, f_ref, s_ref, t_ref, o_ref):
        for parity in range(2):
            @pl.when(core_ref[0] == parity)
            def _(parity=parity):
                for q in range(N_CHIP):
                    o_ref[q] = block(f_ref, s_ref, t_ref, q, 2 * q + parity)

    return pl.pallas_call(
        body,
        grid_spec=pltpu.PrefetchScalarGridSpec(
            num_scalar_prefetch=1, grid=(d // tr,),
            in_specs=[pl.BlockSpec((tr, full), lambda i, c_ref: (i, 0)), pl.BlockSpec((N_CHIP, tr, SLAB), lambda i, c_ref: (0, i, 0)),
                      pl.BlockSpec((tr, NARROW), lambda i, c_ref: (i, 0))],
            out_specs=pl.BlockSpec((N_CHIP, tr, cols), lambda i, c_ref: (0, i, 0))),
        out_shape=jax.ShapeDtypeStruct((N_CHIP, d, cols), dwfull.dtype), name=name,
        compiler_params=_cp("parallel"))(core, dwfull, got_slabs, got_tail)


def _pair_sum(parts, got, core, tr, name):
    _, r, c = parts.shape

    def body(core_ref, p_ref, g_ref, o_ref):
        o_ref[...] = (p_ref[...].astype(F32) + g_ref[...].astype(F32)).astype(o_ref.dtype)

    return pl.pallas_call(
        body,
        grid_spec=pltpu.PrefetchScalarGridSpec(
            num_scalar_prefetch=1, grid=(N_CHIP, r // tr),
            in_specs=[pl.BlockSpec((1, tr, c), lambda q, i, core_ref: (2 * q + core_ref[0], i, 0)),
                      pl.BlockSpec((1, tr, c), lambda q, i, core_ref: (q, i, 0))],
            out_specs=pl.BlockSpec((1, tr, c), lambda q, i, core_ref: (q, i, 0))),
        out_shape=jax.ShapeDtypeStruct((N_CHIP, r, c), parts.dtype), name=name,
        compiler_params=_cp("parallel", "parallel"))(core, parts, got)


def _native_segments(nh):
    w = nh * HEAD_DIM
    return [(0, 4 * w, 0), (4 * w, 4 * w + 2 * nh, 8 * w), (4 * w + 2 * nh, 8 * w + 2 * nh, 4 * w),
            (8 * w + 2 * nh, 8 * w + 3 * nh, 8 * w + 2 * nh)]


def _relayout_w_in(wg, nh, tr, name, rows_total=None, into=None):
    _, d, cols = wg.shape
    w = nh * HEAD_DIM
    rows_total = into.shape[0] if into is not None else rows_total or d
    first = (rows_total - d) // tr if into is not None else 0

    def native(ref, j0, j1):
        out = []
        while j0 < j1:
            blk = j0 // cols
            end = min(j1, (blk + 1) * cols)
            out.append(ref[blk, :, pl.ds(j0 - blk * cols, end - j0)])
            j0 = end
        return out

    def body(g_ref, *refs):
        o_ref = refs[-1]
        for cidx in range(8 * w // LANES):
            j0 = cidx * LANES + (0 if cidx * LANES < 4 * w else 2 * nh)
            pieces = native(g_ref, j0, j0 + LANES)
            o_ref[:, cidx * LANES:(cidx + 1) * LANES] = pieces[0] if len(pieces) == 1 else jnp.concatenate(pieces, axis=1)
        pieces = (native(g_ref, 4 * w, 4 * w + 2 * nh) + native(g_ref, 8 * w + 2 * nh, 8 * w + 3 * nh)
                  + [jnp.zeros((tr, NARROW - 3 * nh), wg.dtype)])
        o_ref[:, 8 * w:] = jnp.concatenate(pieces, axis=1)

    return pl.pallas_call(
        body, grid=(d // tr,), in_specs=[pl.BlockSpec((N_DEV, tr, cols), lambda i: (0, i, 0))] + [_ANY] * (into is not None),
        out_specs=pl.BlockSpec((tr, 8 * w + NARROW), lambda i: (first + i, 0)),
        out_shape=jax.ShapeDtypeStruct((rows_total, 8 * w + NARROW), wg.dtype),
        input_output_aliases={1: 0} if into is not None else {},
        name=name, compiler_params=_cp("parallel"))(wg, *([into] if into is not None else []))


def _adamw(w, parts, m, v, tr, name, after=None):
    r, c = w.shape
    n_parts = parts.shape[0]

    def body(w_ref, p_ref, m_ref, v_ref, *refs):
        g_ref, d_ref, nm_ref, nv_ref = refs[-5:-1] if after is not None else refs
        if after is not None:
            refs[-1][...] = jnp.zeros_like(refs[-1])
        g = p_ref[0].astype(F32)
        for s in range(1, n_parts):
            g = g + p_ref[s].astype(F32)
        m_new = ADAM_B1 * m_ref[...] + (1.0 - ADAM_B1) * g
        v_new = ADAM_B2 * v_ref[...] + (1.0 - ADAM_B2) * (g * g)
        m_hat = m_new / (1.0 - ADAM_B1 ** ADAM_STEP)
        v_hat = v_new / (1.0 - ADAM_B2 ** ADAM_STEP)
        g_ref[...] = g
        d_ref[...] = -ADAM_LR * (m_hat / (jnp.sqrt(v_hat) + ADAM_EPS) + ADAM_WD * w_ref[...])
        nm_ref[...] = m_new
        nv_ref[...] = v_new

    blk = pl.BlockSpec((tr, c), lambda i: (i, 0))
    extra = after is not None
    return pl.pallas_call(
        body, grid=(r // tr,),
        in_specs=[blk, pl.BlockSpec((n_parts, tr, c), lambda i: (0, i, 0)), blk, blk] + [_ANY] * extra,
        out_specs=[blk] * 4 + [pl.BlockSpec((8, LANES), lambda i: (0, 0))] * extra,
        out_shape=[jax.ShapeDtypeStruct((r, c), F32)] * 4 + [jax.ShapeDtypeStruct((8, LANES), F32)] * extra, name=name,
        compiler_params=_cp("arbitrary" if extra else "parallel"))(w, parts, m, v, *([after] if extra else []))


def _pack_small(d, pre, post, a_log, dt_bias, f_bias, gdn_w, fq_w, fk_w, extra):
    row2 = jnp.concatenate([a_log, dt_bias, f_bias, gdn_w, fq_w, fk_w, extra], axis=1)
    row2 = jnp.pad(row2, ((0, 0), (0, d - row2.shape[1])))
    return jnp.concatenate([pre, post, row2, jnp.zeros((5, d), F32)], axis=0)


def _unpack_small(p, nh):
    o = 3 * nh
    return dict(pre=p[0:1], post=p[1:2], a_log=p[2:3, 0:nh], dt_bias=p[2:3, nh:2 * nh], f_bias=p[2:3, 2 * nh:o],
                gdn_w=p[2:3, o:o + HEAD_DIM], fq_w=p[2:3, o + HEAD_DIM:o + 2 * HEAD_DIM],
                fk_w=p[2:3, o + 2 * HEAD_DIM:o + 3 * HEAD_DIM], extra=p[2, o + 3 * HEAD_DIM])


def kernel(x, meta_tokens, pre_norm_w, w_in, conv_w, a_log, dt_bias, gdn_norm_w, fox_q_norm_w, fox_k_norm_w, fox_f_bias, w_out, post_norm_w, loss_target, m_meta_tokens, m_pre_norm_w, m_w_in, m_conv_w, m_a_log, m_dt_bias, m_gdn_norm_w, m_fox_q_norm_w, m_fox_k_norm_w, m_fox_f_bias, m_w_out, m_post_norm_w, v_meta_tokens, v_pre_norm_w, v_w_in, v_conv_w, v_a_log, v_dt_bias, v_gdn_norm_w, v_fox_q_norm_w, v_fox_k_norm_w, v_fox_f_bias, v_w_out, v_post_norm_w):
    nh = a_log.shape[1]
    d = x.shape[-1]
    w = nh * HEAD_DIM
    zero = jnp.zeros((1, 1), F32)

    w_in_a, w_in_b = _cast_bf16_column_major(w_in, 2, "cast_w_in")

    def project(xn, got):
        wg, cg, mg = got
        half = (d // 2, 0), (d // 2, 1)
        wfull = _relayout_w_in(wg, nh, 256, "relayout_w_in_a", rows_total=d)
        proj, got = _matmul(xn, wfull, "nn", MM_TILE, F32, "proj_a", _gather_rider([w_in_b]), a_cols=half[0], b_rows=half[0])
        wfull = _relayout_w_in(got[0], nh, 256, "relayout_w_in_b", into=wfull)
        proj = _matmul(xn, wfull, "nn", MM_TILE, F32, "proj_b", a_cols=half[1], b_rows=half[1], acc=proj)
        return proj, wfull, cg.transpose(1, 0, 2).reshape(CONV_WIDTH, 3 * w), mg.transpose(1, 0, 2).reshape(N_META, d)

    late_weights = (_gather_rider([_cast_bf16(w_out[0], 256, "cast_w_out")]), lambda got: got[0].reshape(2 * w, d))
    core = lax.axis_index("c")
    dev = 4 * lax.axis_index("x") + 2 * lax.axis_index("y") + core
    core_arr = jnp.reshape(core, (1,)).astype(jnp.int32)

    out_parts = lambda dw_out: dw_out.reshape(N_DEV, 2 * w // N_DEV, d)
    g = _layer_grads(
        x[0], loss_target[0], None, pre_norm_w, (_gather_rider([w_in_a, conv_w[0].T, meta_tokens]), 2, project), None,
        a_log, dt_bias, gdn_norm_w,
        fox_q_norm_w, fox_k_norm_w, fox_f_bias, None, post_norm_w, late_weights=late_weights,
        w_out_grads=(lambda dw_out: _pair_rider(parts=out_parts(dw_out)),
                     lambda dw_out, got: _chip_rider(_pair_sum(out_parts(dw_out), got[0], core_arr, 256, "pair_sum_w_out"))))
    p_out = g["w_out_parts"][0]
    xn, dproj, wfull, meta_full = g["xn"], g["dproj"], g["wfull"], g["meta"]
    flights, token, dw, rider = [], None, None, None

    def exchange(dw, got, i):
        sums = _relayout_pair_sum(dw, got[0], got[1], core_arr, nh, 128, f"relayout_pair_sum_{i}")
        flight, token = _chip_exchange_start(sums, f"chip_exchange_start_{i}")
        flights.append(flight)
        return token

    for i, (index, parts) in enumerate(DW_IN_PIECES):
        res = _matmul(xn, dproj, "tn", MM_TILE, BF16, f"dw_in_{i}", rider, a_cols=(d // parts, index))
        if i:
            token = exchange(dw, res[1], i - 1)
        dw = res[0] if i else res
        rider = _pair_rider(dw_rows=dw, nh=nh, after=token)
    dxn, (got,) = _dxn(dproj, wfull, [rider], MM_TILE, "dxn")
    token = exchange(dw, got, len(DW_IN_PIECES) - 1)
    *r_out, token = _adamw(w_out[0], p_out, m_w_out[0], v_w_out[0], 64, "adamw_w_out", after=token)
    (grad_x, dmeta, dpre_w), _ = _prenorm_bwd(dxn, x[0], meta_full, pre_norm_w + token[0:1, 0:1], g["dy"])
    small = _pack_small(d, dpre_w, g["post_w"], g["a_log"], g["dt_bias"], g["f_bias"], g["gdn_norm_w"], g["fq_w"],
                        g["fk_w"], g["loss"])
    a_conv, a_meta, p_small = _all_gather([g["conv_wt"], dmeta, small], "gather_small_grads")
    p_conv = lax.dynamic_slice_in_dim(a_conv, dev * conv_w.shape[1], conv_w.shape[1], axis=2).transpose(0, 2, 1)
    p_meta = lax.dynamic_slice_in_dim(a_meta, dev * meta_tokens.shape[1], meta_tokens.shape[1], axis=2)

    r_conv = _adamw(conv_w[0], p_conv, m_conv_w[0], v_conv_w[0], conv_w.shape[1], "adamw_conv_w")
    r_meta = _adamw(meta_tokens, p_meta, m_meta_tokens, v_meta_tokens, N_META, "adamw_meta")
    pk = lambda pre, post, a, dt, gw, fq, fk, fb: _pack_small(d, pre, post, a, dt, fb, gw, fq, fk, zero)
    r_small = _adamw(
        pk(pre_norm_w, post_norm_w, a_log, dt_bias, gdn_norm_w, fox_q_norm_w, fox_k_norm_w, fox_f_bias), p_small,
        pk(m_pre_norm_w, m_post_norm_w, m_a_log, m_dt_bias, m_gdn_norm_w, m_fox_q_norm_w, m_fox_k_norm_w, m_fox_f_bias),
        pk(v_pre_norm_w, v_post_norm_w, v_a_log, v_dt_bias, v_gdn_norm_w, v_fox_q_norm_w, v_fox_k_norm_w, v_fox_f_bias),
        8, "adamw_small")
    p_in = _chip_exchange_wait(flights, r_small[0], "chip_exchange_wait")
    r_in = _adamw_column_major(w_in, p_in, m_w_in, v_w_in, "adamw_w_in")

    sm = [_unpack_small(r, nh) for r in r_small]
    outs = []
    for i in range(4):
        s = sm[i]
        outs += [r_meta[i], s["pre"], r_in[i], r_conv[i][None], s["a_log"], s["dt_bias"], s["gdn_w"], s["fq_w"],
                 s["fk_w"], s["f_bias"], r_out[i][None], s["post"]]
    return (sm[0]["extra"], grad_x[None], *outs)
```

```python
import jax
import jax.numpy as jnp
from jax import lax
from jax.experimental import pallas as pl
from jax.experimental.pallas import tpu as pltpu

F32, BF16 = jnp.float32, jnp.bfloat16
HEAD_DIM = 128
N_META = 16
CONV_WIDTH = 4
CHUNK = 128
Q_BLOCK = 128
LANES = 128
EPS = 1e-6
PAD_ROWS = Q_BLOCK - N_META
N_DEV = 8
N_CHIP = 4
VMEM_LIMIT = 56 * 1024 * 1024
NEG = -1e30
NARROW = 2 * LANES
MM_TILE = 6 * LANES
DW_IN_PIECES = ((0, 2), (2, 4), (3, 4))

ADAM_LR, ADAM_B1, ADAM_B2, ADAM_EPS, ADAM_WD, ADAM_STEP = 0.001, 0.9, 0.999, 1e-08, 0.01, 10

_DN = {"nn": (((1,), (0,)), ((), ())), "nt": (((1,), (1,)), ((), ())), "tn": (((0,), (0,)), ((), ()))}
_DN3 = {"nn": (((2,), (1,)), ((0,), (0,))), "nt": (((2,), (2,)), ((0,), (0,))), "tn": (((1,), (1,)), ((0,), (0,)))}
_ANY = pl.BlockSpec(memory_space=pl.ANY)
_MESH = pl.DeviceIdType.MESH


def _cp(*sem):
    return pltpu.CompilerParams(dimension_semantics=sem, vmem_limit_bytes=VMEM_LIMIT)


def _dot(a, b, dims="nn", prec=None):
    return lax.dot_general(a, b, _DN[dims], precision=prec, preferred_element_type=F32)


def _bdot(a, b, dims="nn"):
    return _dot(a.astype(BF16), b.astype(BF16), dims)


def _hdot(a, b, dims="nn"):
    return _dot(a, b, dims, prec=lax.Precision.HIGHEST)


def _dot3(a, b, dims="nn"):
    return lax.dot_general(a, b, _DN3[dims], preferred_element_type=F32)


def _bdot3(a, b, dims="nn"):
    return _dot3(a.astype(BF16), b.astype(BF16), dims)


def _split(a):
    hi = a.astype(BF16)
    return hi, (a - hi.astype(F32)).astype(BF16)


def _iota(shape, dim):
    return lax.broadcasted_iota(jnp.int32, shape, dim)


def _sigmoid(z):
    return 1.0 / (1.0 + jnp.exp(-z))


def _softplus(z):
    e = jnp.exp(-jnp.abs(z))
    u = 1.0 + e
    l1p = jnp.where(u == 1.0, e, jnp.log(u) * (e / jnp.where(u == 1.0, 1.0, u - 1.0)))
    return jnp.maximum(z, 0.0) + l1p


def _silu_and_grad(z):
    s = _sigmoid(z)
    return z * s, s * (1.0 + z * (1.0 - s))


def _rms(x):
    return lax.rsqrt(jnp.mean(x * x, axis=-1, keepdims=True) + EPS)


def _h_tile(i, x_ref, meta_ref):
    first = jnp.concatenate([jnp.zeros((PAD_ROWS, x_ref.shape[1]), F32), meta_ref[...]], axis=0)
    return jnp.where(i == 0, first, x_ref[...])


def _x_rows(d):
    return pl.BlockSpec((Q_BLOCK, d), lambda i: (jnp.maximum(i - 1, 0), 0))


def _prenorm(x, meta, w):
    seq, d = x.shape
    lp = seq + Q_BLOCK

    def body(x_ref, m_ref, w_ref, o_ref):
        h = _h_tile(pl.program_id(0), x_ref, m_ref)
        o_ref[...] = (h * _rms(h) * w_ref[...]).astype(BF16)

    return pl.pallas_call(
        body, grid=(lp // Q_BLOCK,),
        in_specs=[_x_rows(d), pl.BlockSpec((N_META, d), lambda i: (0, 0)), pl.BlockSpec((1, d), lambda i: (0, 0))],
        out_specs=pl.BlockSpec((Q_BLOCK, d), lambda i: (i, 0)),
        out_shape=jax.ShapeDtypeStruct((lp, d), BF16), name="prenorm", compiler_params=_cp("parallel"))(x, meta, w)


def _prenorm_gathering(x, w, rider, meta_at):
    seq, d = x.shape
    steps = seq // Q_BLOCK + 1

    def body(start, finish, x_ref, w_ref, o_ref, meta_buf, meta_sem):
        i = pl.program_id(0)
        pl.when(i == 0)(start)

        @pl.when(i < steps - 1)
        def _():
            h = x_ref[...]
            o_ref[...] = (h * _rms(h) * w_ref[...]).astype(BF16)

        @pl.when(i == steps - 1)
        def _():
            finish()
            cp = pltpu.make_async_copy(finish.results[0][meta_at], meta_buf, meta_sem)
            cp.start()
            cp.wait()
            h = jnp.concatenate([jnp.zeros((PAD_ROWS, d), F32), jnp.concatenate([meta_buf[s] for s in range(N_DEV)], axis=1)], axis=0)
            o_ref[...] = (h * _rms(h) * w_ref[...]).astype(BF16)

    (xn,), got = _hosted_call(
        body, [rider], 2, 1, 2, grid=(steps,),
        in_specs=[pl.BlockSpec((Q_BLOCK, d), lambda i: (jnp.minimum(i, steps - 2), 0)), pl.BlockSpec((1, d), lambda i: (0, 0))],
        out_specs=[pl.BlockSpec((Q_BLOCK, d), lambda i: ((i + 1) % steps, 0))],
        out_shape=[jax.ShapeDtypeStruct((seq + Q_BLOCK, d), BF16)],
        scratch_shapes=[pltpu.VMEM((N_DEV, N_META, d // N_DEV), F32), pltpu.SemaphoreType.DMA(())],
        name="prenorm", compiler_params=_cp("arbitrary"))(x, w)
    return xn, got[0]


def _tile(n, want):
    return max(t for t in range(LANES, want + 1, LANES) if n % t == 0)


class _Rider:
    def __init__(self, inputs, out_shapes, scratch, aliases, make):
        self.inputs, self.out_shapes, self.scratch, self.aliases, self.make = inputs, out_shapes, scratch, aliases, make


def _hosted_call(body, riders, n_in, n_out, n_scratch, *, in_specs, out_specs, out_shape, scratch_shapes=(), aliases=None,
                 **kw):
    riders = [r for r in riders if r is not None]
    r_in = [len(r.inputs) for r in riders]
    r_out = [len(r.out_shapes) for r in riders]
    r_scr = [len(r.scratch) for r in riders]
    al = dict(aliases or {})
    for k, r in enumerate(riders):
        al.update({n_in + sum(r_in[:k]) + i: n_out + sum(r_out[:k]) + o for i, o in r.aliases.items()})

    def full_body(*refs):
        ins, rest = refs[:n_in + sum(r_in)], refs[n_in + sum(r_in):]
        outs, scr = rest[:n_out + sum(r_out)], rest[n_out + sum(r_out):]
        hooks = [r.make(ins[n_in + sum(r_in[:k]):n_in + sum(r_in[:k + 1])], outs[n_out + sum(r_out[:k]):n_out + sum(r_out[:k + 1])],
                        scr[n_scratch + sum(r_scr[:k]):n_scratch + sum(r_scr[:k + 1])]) for k, r in enumerate(riders)]

        def start():
            for h in hooks:
                h[0]()

        def finish():
            for h in hooks:
                h[1]()

        finish.results = [outs[n_out + sum(r_out[:k]):n_out + sum(r_out[:k + 1])] for k in range(len(riders))]
        body(start, finish, *ins[:n_in], *outs[:n_out], *scr[:n_scratch])

    call = pl.pallas_call(
        full_body, in_specs=list(in_specs) + [_ANY] * sum(r_in), out_specs=list(out_specs) + [_ANY] * sum(r_out),
        out_shape=list(out_shape) + [s for r in riders for s in r.out_shapes],
        scratch_shapes=list(scratch_shapes) + [s for r in riders for s in r.scratch], input_output_aliases=al, **kw)

    def run(*args):
        res = call(*args, *[t for r in riders for t in r.inputs])
        return res[:n_out], [res[n_out + sum(r_out[:k]):n_out + sum(r_out[:k + 1])] for k in range(len(riders))]

    return run


def _matmul(a, b, dims, tn, out_dtype, name, rider=None, a_cols=None, b_rows=None, acc=None):
    a_shape = a.shape if a_cols is None else (a.shape[0], a_cols[0])
    a_index = 0 if a_cols is None else a_cols[1]
    m = a_shape[1] if dims == "tn" else a_shape[0]
    n = b.shape[0] if dims == "nt" else b.shape[1]
    kdim = b.shape[1] if dims == "nt" else b.shape[0] if b_rows is None else b_rows[0]
    b_index = 0 if b_rows is None else b_rows[1]
    tn = _tile(n, tn)
    steps = n // tn
    b_spec = pl.BlockSpec((tn, kdim), lambda j: (j, 0)) if dims == "nt" else pl.BlockSpec((kdim, tn), lambda j: (b_index, j))
    o_spec = pl.BlockSpec((m, tn), lambda j: (0, j))

    def body(start, finish, a_ref, b_ref, *refs):
        pl.when(pl.program_id(0) == 0)(start)
        prod = _dot(a_ref[...], b_ref[...], dims)
        refs[-1][...] = (prod if acc is None else prod + refs[0][...]).astype(out_dtype)
        pl.when(pl.program_id(0) == steps - 1)(finish)

    (out,), got = _hosted_call(
        body, [rider], 2 + (acc is not None), 1, 0, grid=(steps,),
        in_specs=[pl.BlockSpec(a_shape, lambda j: (0, a_index)), b_spec] + [o_spec] * (acc is not None),
        out_specs=[o_spec], out_shape=[jax.ShapeDtypeStruct((m, n), out_dtype)], aliases={2: 0} if acc is not None else None,
        name=name, compiler_params=_cp("parallel" if rider is None else "arbitrary"))(a, b, *([acc] if acc is not None else []))
    return out if rider is None else (out, got[0])


def _chip_rider(sums):
    def make(ins, outs, scratch):
        local, remote = _chip_exchange_copies(ins[0], outs[0], *scratch)

        def start():
            for cp in [local] + remote:
                cp.start()

        def finish():
            local.wait()
            for cp in remote:
                cp.wait_send()
                cp.wait_recv()

        return start, finish

    return _Rider([sums], [jax.ShapeDtypeStruct(sums.shape, sums.dtype)],
                  [pltpu.SemaphoreType.DMA((N_CHIP - 1,)), pltpu.SemaphoreType.DMA((N_CHIP - 1,)), pltpu.SemaphoreType.DMA((1,))],
                  {}, make)


_HBM = pl.BlockSpec(memory_space=pltpu.HBM)
_SEM = pl.BlockSpec(memory_space=pltpu.SEMAPHORE)


def _chip_exchange_copies(sums_ref, land_ref, send_sems, recv_sems, local_sem):
    x, y, core = lax.axis_index("x"), lax.axis_index("y"), lax.axis_index("c")
    mine = 2 * x + y
    local = pltpu.make_async_copy(sums_ref.at[mine], land_ref.at[mine], local_sem.at[0])
    remote = []
    for k in range(1, N_CHIP):
        px = 1 - x if k & 2 else x
        py = 1 - y if k & 1 else y
        remote.append(pltpu.make_async_remote_copy(
            src_ref=sums_ref.at[2 * px + py], dst_ref=land_ref.at[mine], send_sem=send_sems.at[k - 1],
            recv_sem=recv_sems.at[k - 1], device_id=(px, py, core), device_id_type=_MESH))
    return local, remote


def _chip_exchange_start(sums, name):
    def body(s_ref, send_sems, recv_sems, local_sem, s_thru, land_ref, token):
        local, remote = _chip_exchange_copies(s_ref, land_ref, send_sems, recv_sems, local_sem)
        for cp in [local] + remote:
            cp.start()
        token[...] = jnp.zeros_like(token)

    *flight, token = pl.pallas_call(
        body, name=name,
        out_shape=(pltpu.SemaphoreType.DMA((N_CHIP - 1,)), pltpu.SemaphoreType.DMA((N_CHIP - 1,)), pltpu.SemaphoreType.DMA((1,)),
                   pltpu.HBM(sums.shape, sums.dtype), pltpu.HBM(sums.shape, sums.dtype), jax.ShapeDtypeStruct((8, LANES), F32)),
        in_specs=(_HBM,), out_specs=(_SEM, _SEM, _SEM, _HBM, _HBM, pl.BlockSpec(memory_space=pltpu.VMEM)),
        input_output_aliases={0: 3},
        compiler_params=pltpu.CompilerParams(has_side_effects=pltpu.SideEffectType.DATAFLOW_SIDE_EFFECTING))(
            pltpu.with_memory_space_constraint(sums, pltpu.HBM))
    return flight, token


def _chip_exchange_wait(flights, after, name):
    n = len(flights)

    def body(*refs):
        for i in range(n):
            s_ref, land_ref = refs[2 * i:2 * i + 2]
            local, remote = _chip_exchange_copies(s_ref, land_ref, *refs[2 * n + 3 * i:2 * n + 3 * i + 3])
            local.wait()
            for cp in remote:
                cp.wait_send()
                cp.wait_recv()

    buffers = [b for f in flights for b in f[3:]]
    res = pl.pallas_call(
        body, name=name, out_shape=tuple(pltpu.HBM(b.shape, b.dtype) for b in buffers),
        in_specs=(_HBM,) * (2 * n) + (_SEM,) * (3 * n) + (_ANY,), out_specs=(_HBM,) * (2 * n),
        input_output_aliases={i: i for i in range(2 * n)},
        compiler_params=pltpu.CompilerParams(has_side_effects=pltpu.SideEffectType.DATAFLOW_SIDE_EFFECTING))(
            *buffers, *[s for f in flights for s in f[:3]], after)
    return res[1::2]


def _dxn(dproj, wfull, riders, tk, name):
    m, k = dproj.shape
    n = wfull.shape[0]
    tk = _tile(k, tk)
    steps = k // tk

    def body(start, finish, a_ref, b_ref, o_ref):
        j = pl.program_id(0)

        @pl.when(j == 0)
        def _():
            start()
            o_ref[...] = jnp.zeros_like(o_ref)
        o_ref[...] += _dot(a_ref[...], b_ref[...], "nt")
        pl.when(j == steps - 1)(finish)

    (dxn,), got = _hosted_call(
        body, riders, 2, 1, 0, grid=(steps,),
        in_specs=[pl.BlockSpec((m, tk), lambda j: (0, j)), pl.BlockSpec((n, tk), lambda j: (0, j))],
        out_specs=[pl.BlockSpec((m, n), lambda j: (0, 0))], out_shape=[jax.ShapeDtypeStruct((m, n), F32)],
        name=name, compiler_params=_cp("arbitrary"))(dproj, wfull)
    return dxn, got


def _conv_taps(x, w):
    c = x * w[CONV_WIDTH - 1:CONV_WIDTH, :]
    for j in range(CONV_WIDTH - 1):
        c = c + pltpu.roll(x, CONV_WIDTH - 1 - j, 0) * w[j:j + 1, :]
    return c


def _gdn_prep(proj, conv_wt, nh):
    lp = proj.shape[0]
    scale = HEAD_DIM ** -0.5

    def body(x_ref, w_ref, o_ref):
        which = pl.program_id(0) // nh
        c = _conv_taps(x_ref[...], w_ref[...])
        s = c * _sigmoid(c)
        r = lax.rsqrt(jnp.sum(s * s, axis=-1, keepdims=True) + EPS)
        f = jnp.where(which == 0, r * scale, jnp.where(which == 1, r, 1.0))
        o_ref[...] = jnp.where(_iota(s.shape, 0) >= PAD_ROWS, s * f, 0.0)

    return pl.pallas_call(
        body, grid=(3 * nh,),
        in_specs=[pl.BlockSpec((lp, LANES), lambda s: (0, s)), pl.BlockSpec((CONV_WIDTH, LANES), lambda s: (0, s))],
        out_specs=pl.BlockSpec((lp, LANES), lambda s: (0, s)),
        out_shape=jax.ShapeDtypeStruct((lp, 3 * nh * HEAD_DIM), F32), name="gdn_prep",
        compiler_params=_cp("parallel"))(proj, conv_wt)


def _gdn_prep_bwd(proj, conv_wt, dq, dk, dv, dproj, nh):
    lp = proj.shape[0]
    scale = HEAD_DIM ** -0.5
    part = lambda p: pl.BlockSpec((lp, LANES), lambda s: (0, jnp.clip(s - p * nh, 0, nh - 1)))

    def body(x_ref, w_ref, dq_ref, dk_ref, dv_ref, _, dx_ref, dw_ref):
        which = pl.program_id(0) // nh
        x = x_ref[...]
        w = w_ref[...]
        c = _conv_taps(x, w)
        sg = _sigmoid(c)
        s = c * sg
        r = lax.rsqrt(jnp.sum(s * s, axis=-1, keepdims=True) + EPS)
        dy = jnp.where(which == 0, dq_ref[...], jnp.where(which == 1, dk_ref[...], dv_ref[...]))
        dy = jnp.where(_iota(s.shape, 0) >= PAD_ROWS, dy, 0.0)
        y0 = s * r
        dy0 = dy * jnp.where(which == 0, scale, 1.0)
        ds_n = r * (dy0 - y0 * jnp.sum(dy0 * y0, axis=-1, keepdims=True))
        ds = jnp.where(which == 2, dy, ds_n)
        dc = ds * (sg * (1.0 + c * (1.0 - sg)))
        dx = dc * w[CONV_WIDTH - 1:CONV_WIDTH, :]
        rows = [jnp.sum(dc * x, axis=0, keepdims=True)]
        for j in range(CONV_WIDTH - 2, -1, -1):
            sh = CONV_WIDTH - 1 - j
            dx = dx + pltpu.roll(dc, lp - sh, 0) * w[j:j + 1, :]
            rows.insert(0, jnp.sum(dc * pltpu.roll(x, sh, 0), axis=0, keepdims=True))
        dx_ref[...] = dx.astype(BF16)
        dw_ref[...] = jnp.concatenate(rows, axis=0)

    strip = pl.BlockSpec((lp, LANES), lambda s: (0, s))
    taps = pl.BlockSpec((CONV_WIDTH, LANES), lambda s: (0, s))
    return pl.pallas_call(
        body, grid=(3 * nh,), in_specs=[strip, taps, part(0), part(1), part(2), _ANY], out_specs=[strip, taps],
        out_shape=[jax.ShapeDtypeStruct(dproj.shape, BF16), jax.ShapeDtypeStruct((CONV_WIDTH, 3 * nh * HEAD_DIM), F32)],
        input_output_aliases={5: 0}, name="gdn_prep_bwd", compiler_params=_cp("parallel"))(proj, conv_wt, dq, dk, dv, dproj)


def _gates(proj, bias_row, nega_row, nh):
    lp = proj.shape[0]
    nc = lp // CHUNK

    def body(p_ref, b_ref, a_ref, g_ref, gt3_ref, gtf_ref):
        lane = _iota((CHUNK, LANES), 1)
        tri = (_iota((CHUNK, CHUNK), 0) >= _iota((CHUNK, CHUNK), 1)).astype(F32)

        def step(n, carry):
            r0 = pl.multiple_of(n * CHUNK, CHUNK)
            z = p_ref[pl.ds(r0, CHUNK), :] + b_ref[...]
            base = jnp.where(lane < nh, _sigmoid(z),
                             jnp.where(lane < 2 * nh, a_ref[...] * _softplus(z),
                                       jnp.where(lane < 3 * nh, -_softplus(-z), 0.0)))
            base = jnp.where(r0 + _iota((CHUNK, LANES), 0) >= PAD_ROWS, base, 0.0)
            cs = _hdot(tri, base)
            run = jnp.where((lane >= 2 * nh) & (lane < 3 * nh), cs + carry, cs)
            sh = pltpu.roll(run, 2 * nh, 1)
            out = base + jnp.where((lane >= 3 * nh) & (lane < 5 * nh), sh, 0.0)
            g_ref[pl.ds(r0, CHUNK), :] = out
            gt3_ref[n] = out.T
            return carry + cs[CHUNK - 1:CHUNK, :]

        lax.fori_loop(0, nc, step, jnp.zeros((1, LANES), F32))
        gtf_ref[...] = g_ref[...].T

    vec = pl.BlockSpec((1, LANES), lambda i: (0, 0))
    return pl.pallas_call(
        body, grid=(1,), in_specs=[pl.BlockSpec((lp, LANES), lambda i: (0, 8 * nh)), vec, vec],
        out_specs=[pl.BlockSpec((lp, LANES), lambda i: (0, 0)), pl.BlockSpec((nc, LANES, CHUNK), lambda i: (0, 0, 0)),
                   pl.BlockSpec((LANES, lp), lambda i: (0, 0))],
        out_shape=[jax.ShapeDtypeStruct((lp, LANES), F32), jax.ShapeDtypeStruct((nc, LANES, CHUNK), F32),
                   jax.ShapeDtypeStruct((LANES, lp), F32)],
        name="gates", compiler_params=_cp("arbitrary"))(proj, bias_row, nega_row)


def _gates_bwd(proj, bias_row, nega_row, gates, dgate_gdn, dc_t, dproj, nh):
    lp = proj.shape[0]
    nc = lp // CHUNK

    def body(p_ref, b_ref, a_ref, g_ref, dg_ref, dc_ref, _, dz_ref, sm_ref, dct_scr):
        lane = _iota((CHUNK, LANES), 1)
        triu = (_iota((CHUNK, CHUNK), 0) <= _iota((CHUNK, CHUNK), 1)).astype(F32)
        dct_scr[...] = dc_ref[...].T
        sm_ref[...] = jnp.zeros_like(sm_ref)
        dz_ref[:, LANES:] = jnp.zeros((lp, NARROW - LANES), BF16)

        def step(i, carry):
            n = nc - 1 - i
            r0 = pl.multiple_of(n * CHUNK, CHUNK)
            z = p_ref[pl.ds(r0, CHUNK), :] + b_ref[...]
            gt = g_ref[pl.ds(r0, CHUNK), :]
            dgd = dg_ref[pl.ds(r0, CHUNK), :]
            dch = dct_scr[pl.ds(r0, CHUNK), :]
            rc = _hdot(triu, dch) + carry
            sg = _sigmoid(z)
            dz = jnp.where(lane < nh, dgd * sg * (1.0 - sg),
                           jnp.where(lane < 2 * nh, dgd * a_ref[...] * sg,
                                     jnp.where(lane < 3 * nh, rc * (1.0 - sg), 0.0)))
            dz = jnp.where(r0 + _iota((CHUNK, LANES), 0) >= PAD_ROWS, dz, 0.0)
            dz_ref[pl.ds(r0, CHUNK), 0:LANES] = dz.astype(BF16)
            sm_ref[0:1, :] += jnp.sum(dz, axis=0, keepdims=True)
            sm_ref[1:2, :] += jnp.sum(jnp.where((lane >= nh) & (lane < 2 * nh), dgd * gt, 0.0), axis=0, keepdims=True)
            return carry + jnp.sum(dch, axis=0, keepdims=True)

        lax.fori_loop(0, nc, step, jnp.zeros((1, LANES), F32))

    vec = pl.BlockSpec((1, LANES), lambda i: (0, 0))
    full = pl.BlockSpec((lp, LANES), lambda i: (0, 0))
    last = pl.BlockSpec((lp, LANES), lambda i: (0, 8 * nh))
    tail = pl.BlockSpec((lp, NARROW), lambda i: (0, 8 * nh * LANES // NARROW))
    return pl.pallas_call(
        body, grid=(1,), in_specs=[last, vec, vec, full, full, pl.BlockSpec((LANES, lp), lambda i: (0, 0)), _ANY],
        out_specs=[tail, pl.BlockSpec((8, LANES), lambda i: (0, 0))],
        out_shape=[jax.ShapeDtypeStruct(dproj.shape, BF16), jax.ShapeDtypeStruct((8, LANES), F32)],
        scratch_shapes=[pltpu.VMEM((lp, LANES), F32)], input_output_aliases={6: 0},
        name="gates_bwd", compiler_params=_cp("arbitrary"))(proj, bias_row, nega_row, gates, dgate_gdn, dc_t, dproj)


def _tri_inv(a):
    t = jnp.where(_iota(a.shape, 1) == _iota(a.shape, 2), 1.0, 0.0) - a
    p = a
    for _ in range(CHUNK.bit_length() - 2):
        ph, pw = _split(p)
        p = _dot3(ph, ph) + (_dot3(ph, pw) + _dot3(pw, ph))
        ph, pw = _split(p)
        th, tw = _split(t)
        t = t + (_dot3(th, ph) + (_dot3(th, pw) + _dot3(tw, ph)))
    return t


def _gdn_chunk(q, k, v, beta, gc, gr, t=None):
    ii, jj = _iota((1, CHUNK, CHUNK), 1), _iota((1, CHUNK, CHUNK), 2)
    causal, strict = ii >= jj, ii > jj
    dm = jnp.where(causal, jnp.exp(jnp.where(causal, gc - gr, 0.0)), 0.0)
    kk = _bdot3(k, k, "nt")
    a = jnp.where(strict, beta * kk * dm, 0.0)
    if t is None:
        t = _tri_inv(a)
    eg = jnp.exp(gc)
    glast = gc[:, CHUNK - 1:CHUNK, :]
    ekd = jnp.exp(glast - gc)
    bv = beta * v
    bk = (beta * eg) * k
    ub = _bdot3(t, jnp.concatenate([bv, bk], axis=2))
    qk = _bdot3(q, k, "nt")
    return dict(causal=causal, strict=strict, dm=dm, kk=kk, a=a, t=t, eg=eg, ekd=ekd, bv=bv, bk=bk,
                u=ub[:, :, :HEAD_DIM], w=ub[:, :, HEAD_DIM:], qk=qk, aqk=jnp.where(causal, qk * dm, 0.0),
                q_dec=q * eg, k_dec=k * ekd, decay=jnp.exp(glast))


def _heads(ref, nh):
    return jnp.stack([ref[:, h * HEAD_DIM:(h + 1) * HEAD_DIM] for h in range(nh)], axis=0)


def _gdn_chunk_inputs(q_ref, k_ref, v_ref, g, gt, nh):
    col = lambda o: jnp.stack([g[:, o + h:o + h + 1] for h in range(nh)], axis=0)
    gr = jnp.stack([gt[3 * nh + h:3 * nh + h + 1, :] for h in range(nh)], axis=0)
    return _heads(q_ref, nh), _heads(k_ref, nh), _heads(v_ref, nh), col(0), col(3 * nh), gr


def _gdn_fwd(qkv, gates, gt3, nh, rider=None):
    lp = qkv.shape[0]
    nc = lp // CHUNK
    w = nh * HEAD_DIM

    def body(start, finish, q_ref, k_ref, v_ref, g_ref, gt_ref, o_ref, sall_ref, tall_ref, s_scr):
        @pl.when(pl.program_id(0) == 0)
        def _():
            start()
            s_scr[...] = jnp.zeros_like(s_scr)
        c = _gdn_chunk(*_gdn_chunk_inputs(q_ref, k_ref, v_ref, g_ref[...], gt_ref[0], nh))
        s = s_scr[...]
        sall_ref[0] = s
        tall_ref[0] = c["t"]
        v_new = c["u"] - _bdot3(c["w"], s)
        o = _bdot3(c["q_dec"], s) + _bdot3(c["aqk"], v_new)
        s_scr[...] = s * c["decay"] + _bdot3(c["k_dec"], v_new, "tn")
        for h in range(nh):
            o_ref[:, h * HEAD_DIM:(h + 1) * HEAD_DIM] = o[h]
        pl.when(pl.program_id(0) == nc - 1)(finish)

    outs, got = _hosted_call(
        body, [rider], 5, 3, 1, grid=(nc,),
        in_specs=[pl.BlockSpec((CHUNK, w), lambda n: (n, 0)), pl.BlockSpec((CHUNK, w), lambda n: (n, 1)),
                  pl.BlockSpec((CHUNK, w), lambda n: (n, 2)), pl.BlockSpec((CHUNK, LANES), lambda n: (n, 0)),
                  pl.BlockSpec((1, LANES, CHUNK), lambda n: (n, 0, 0))],
        out_specs=[pl.BlockSpec((CHUNK, w), lambda n: (n, 0)),
                   pl.BlockSpec((1, nh, HEAD_DIM, HEAD_DIM), lambda n: (n, 0, 0, 0)),
                   pl.BlockSpec((1, nh, CHUNK, CHUNK), lambda n: (n, 0, 0, 0))],
        out_shape=[jax.ShapeDtypeStruct((lp, w), F32), jax.ShapeDtypeStruct((nc, nh, HEAD_DIM, HEAD_DIM), F32),
                   jax.ShapeDtypeStruct((nc, nh, CHUNK, CHUNK), F32)],
        scratch_shapes=[pltpu.VMEM((nh, HEAD_DIM, HEAD_DIM), F32)],
        name="gdn_fwd", compiler_params=_cp("arbitrary"))(qkv, qkv, qkv, gates, gt3)
    return outs, (got[0] if got else None)


def _gdn_bwd(qkv, gates, gt3, s_all, t_all, do, nh, rider=None):
    lp = qkv.shape[0]
    nc = lp // CHUNK
    w = nh * HEAD_DIM
    rev = lambda n: nc - 1 - n

    def body(start, finish, q_ref, k_ref, v_ref, g_ref, gt_ref, s_ref, t_ref, do_ref, dq_ref, dk_ref, dv_ref, dg_ref, ds_scr):
        @pl.when(pl.program_id(0) == 0)
        def _():
            start()
            ds_scr[...] = jnp.zeros_like(ds_scr)
        q, k, v, beta, gc, gr = _gdn_chunk_inputs(q_ref, k_ref, v_ref, g_ref[...], gt_ref[0], nh)
        c = _gdn_chunk(q, k, v, beta, gc, gr, t_ref[0])
        s = s_ref[0]
        dsn = ds_scr[...]
        dout = _heads(do_ref, nh)
        v_new = c["u"] - _bdot3(c["w"], s)
        dq_dec = _bdot3(dout, s, "nt")
        daqk = jnp.where(c["causal"], _bdot3(dout, v_new, "nt"), 0.0)
        dv_new = _bdot3(c["aqk"], dout, "tn") + _bdot3(c["k_dec"], dsn)
        dk_dec = _bdot3(v_new, dsn, "nt")
        ddecay = jnp.sum(jnp.sum(dsn * s, axis=2, keepdims=True), axis=1, keepdims=True)
        dw = -_bdot3(dv_new, s, "nt")
        ds_scr[...] = _bdot3(c["q_dec"], dout, "tn") + c["decay"] * dsn - _bdot3(c["w"], dv_new, "tn")
        duw = jnp.concatenate([dv_new, dw], axis=2)
        dt = _bdot3(duw, jnp.concatenate([c["bv"], c["bk"]], axis=2), "nt")
        dbvk = _bdot3(c["t"], duw, "tn")
        dbv, dbk = dbvk[:, :, :HEAD_DIM], dbvk[:, :, HEAD_DIM:]
        da = jnp.where(c["strict"], -_bdot3(_bdot3(c["t"], dt, "tn"), c["t"], "nt"), 0.0)
        dkk = da * beta * c["dm"]
        dqk = daqk * c["dm"]
        e = da * c["a"] + daqk * c["aqk"]
        dq = dq_dec * c["eg"] + _bdot3(dqk, k)
        dk = (dk_dec * c["ekd"] + _bdot3(dkk, k) + _bdot3(dkk, k, "tn") + _bdot3(dqk, q, "tn")
              + (beta * c["eg"]) * dbk)
        dv = beta * dbv
        rs = lambda x: jnp.sum(x, axis=2, keepdims=True)
        dbeta = rs(dbv * v) + c["eg"] * rs(dbk * k) + rs(da * c["kk"] * c["dm"])
        kd_term = rs(dk_dec * c["k_dec"])
        eh, ew = _split(e)
        ones = jnp.ones((nh, CHUNK, LANES), BF16)
        col_sums = (_dot3(eh, ones, "tn") + _dot3(ew, ones, "tn"))[:, :, 0:1]
        dg_cum = rs(dq_dec * c["q_dec"]) - kd_term + rs(dbk * c["bk"]) + rs(e) - col_sums
        last = jnp.sum(kd_term, axis=1, keepdims=True) + ddecay * c["decay"]
        dg_cum = dg_cum + jnp.where(_iota((1, CHUNK, 1), 1) == CHUNK - 1, last, 0.0)
        lane = _iota((CHUNK, LANES), 1)
        acc = jnp.zeros((CHUNK, LANES), F32)
        for h in range(nh):
            sl = slice(h * HEAD_DIM, (h + 1) * HEAD_DIM)
            dq_ref[:, sl] = dq[h]
            dk_ref[:, sl] = dk[h]
            dv_ref[:, sl] = dv[h]
            acc = acc + jnp.where(lane == h, dbeta[h], 0.0) + jnp.where(lane == nh + h, dg_cum[h], 0.0)
        triu = (_iota((CHUNK, CHUNK), 0) <= _iota((CHUNK, CHUNK), 1)).astype(F32)
        dg_ref[...] = jnp.where(lane < nh, acc, _hdot(triu, acc))
        pl.when(pl.program_id(0) == nc - 1)(finish)

    outs, got = _hosted_call(
        body, [rider], 8, 4, 1, grid=(nc,),
        in_specs=[pl.BlockSpec((CHUNK, w), lambda n: (rev(n), 0)), pl.BlockSpec((CHUNK, w), lambda n: (rev(n), 1)),
                  pl.BlockSpec((CHUNK, w), lambda n: (rev(n), 2)), pl.BlockSpec((CHUNK, LANES), lambda n: (rev(n), 0)),
                  pl.BlockSpec((1, LANES, CHUNK), lambda n: (rev(n), 0, 0)),
                  pl.BlockSpec((1, nh, HEAD_DIM, HEAD_DIM), lambda n: (rev(n), 0, 0, 0)),
                  pl.BlockSpec((1, nh, CHUNK, CHUNK), lambda n: (rev(n), 0, 0, 0)),
                  pl.BlockSpec((CHUNK, w), lambda n: (rev(n), 0))],
        out_specs=[pl.BlockSpec((CHUNK, w), lambda n: (rev(n), 0))] * 3 + [pl.BlockSpec((CHUNK, LANES), lambda n: (rev(n), 0))],
        out_shape=[jax.ShapeDtypeStruct((lp, w), F32)] * 3 + [jax.ShapeDtypeStruct((lp, LANES), F32)],
        scratch_shapes=[pltpu.VMEM((nh, HEAD_DIM, HEAD_DIM), F32)],
        name="gdn_bwd", compiler_params=_cp("arbitrary"))(qkv, qkv, qkv, gates, gt3, s_all, t_all, do)
    return outs, (got[0] if got else None)


def _merge_gdn(o_gdn, proj, norm_w, nh):
    lp = o_gdn.shape[0]

    def body(o_ref, z_ref, w_ref, m_ref):
        o = o_ref[...]
        z = z_ref[...]
        m_ref[...] = (o * _rms(o) * w_ref[...] * (z * _sigmoid(z))).astype(BF16)

    return pl.pallas_call(
        body, grid=(nh,),
        in_specs=[pl.BlockSpec((lp, LANES), lambda s: (0, s)), pl.BlockSpec((lp, LANES), lambda s: (0, 3 * nh + s)),
                  pl.BlockSpec((1, LANES), lambda s: (0, 0))],
        out_specs=pl.BlockSpec((lp, LANES), lambda s: (0, s)),
        out_shape=jax.ShapeDtypeStruct((lp, 2 * nh * HEAD_DIM), BF16), name="merge_gdn",
        compiler_params=_cp("parallel"))(o_gdn, proj, norm_w)


def _merge_gdn_bwd(o_gdn, proj, norm_w, dmerged, nh):
    lp = o_gdn.shape[0]

    def body(o_ref, z_ref, w_ref, dm_ref, do_ref, dz_ref, dw_ref):
        o = o_ref[...]
        r = _rms(o)
        xh = o * r
        silu, dsilu = _silu_and_grad(z_ref[...])
        dm = dm_ref[...]
        dn = dm * silu
        dz_ref[...] = (dm * (xh * w_ref[...]) * dsilu).astype(BF16)
        dnw = dn * w_ref[...]
        do_ref[...] = r * (dnw - xh * jnp.mean(dnw * xh, axis=-1, keepdims=True))

        @pl.when(pl.program_id(0) == 0)
        def _():
            dw_ref[...] = jnp.zeros_like(dw_ref)
        dw_ref[...] += jnp.sum(dn * xh, axis=0, keepdims=True)

    w = nh * HEAD_DIM
    return pl.pallas_call(
        body, grid=(nh,),
        in_specs=[pl.BlockSpec((lp, LANES), lambda s: (0, s)), pl.BlockSpec((lp, LANES), lambda s: (0, 3 * nh + s)),
                  pl.BlockSpec((1, LANES), lambda s: (0, 0)), pl.BlockSpec((lp, LANES), lambda s: (0, s))],
        out_specs=[pl.BlockSpec((lp, LANES), lambda s: (0, s)), pl.BlockSpec((lp, LANES), lambda s: (0, 3 * nh + s)),
                   pl.BlockSpec((1, LANES), lambda s: (0, 0))],
        out_shape=[jax.ShapeDtypeStruct((lp, w), F32), jax.ShapeDtypeStruct((lp, 8 * w + NARROW), BF16),
                   jax.ShapeDtypeStruct((1, LANES), F32)],
        name="merge_gdn_bwd", compiler_params=_cp("arbitrary"))(o_gdn, proj, norm_w, dmerged)


def _fox_prep(proj, qk_w, nh):
    lp = proj.shape[0]

    def body(x_ref, w_ref, o_ref):
        x = x_ref[...]
        o_ref[...] = x * _rms(x) * w_ref[0]

    return pl.pallas_call(
        body, grid=(2 * nh,),
        in_specs=[pl.BlockSpec((lp, LANES), lambda s: (0, 4 * nh + s)), pl.BlockSpec((1, 1, LANES), lambda s: (s // nh, 0, 0))],
        out_specs=pl.BlockSpec((lp, LANES), lambda s: (0, s)),
        out_shape=jax.ShapeDtypeStruct((lp, 2 * nh * HEAD_DIM), F32), name="fox_prep",
        compiler_params=_cp("parallel"))(proj, qk_w)


def _fox_prep_bwd(proj, qk_w, dq, dk, dproj, nh):
    lp = proj.shape[0]
    part = lambda p: pl.BlockSpec((lp, LANES), lambda s: (0, jnp.clip(s - p * nh, 0, nh - 1)))

    def body(x_ref, w_ref, dq_ref, dk_ref, _, dx_ref, dw_ref):
        x = x_ref[...]
        r = _rms(x)
        xh = x * r
        dy = jnp.where(pl.program_id(0) < nh, dq_ref[...], dk_ref[...])
        dyw = dy * w_ref[0]
        dx_ref[...] = (r * (dyw - xh * jnp.mean(dyw * xh, axis=-1, keepdims=True))).astype(BF16)

        @pl.when(pl.program_id(0) % nh == 0)
        def _():
            dw_ref[...] = jnp.zeros_like(dw_ref)
        dw_ref[0] += jnp.sum(dy * xh, axis=0, keepdims=True)

    strip = pl.BlockSpec((lp, LANES), lambda s: (0, 4 * nh + s))
    wsp = pl.BlockSpec((1, 1, LANES), lambda s: (s // nh, 0, 0))
    return pl.pallas_call(
        body, grid=(2 * nh,), in_specs=[strip, wsp, part(0), part(1), _ANY], out_specs=[strip, wsp],
        out_shape=[jax.ShapeDtypeStruct(dproj.shape, BF16), jax.ShapeDtypeStruct((2, 1, LANES), F32)],
        input_output_aliases={4: 0}, name="fox_prep_bwd", compiler_params=_cp("arbitrary"))(proj, qk_w, dq, dk, dproj)


def _fox_probs(q, k, gates, crow, h, i, nh, lse=None):
    kl = k.shape[0]
    lane = _iota((Q_BLOCK, LANES), 1)
    ct = jnp.sum(jnp.where(lane == 4 * nh + h, gates, 0.0), axis=1, keepdims=True)
    tq, kq = _iota((Q_BLOCK, Q_BLOCK), 0), _iota((Q_BLOCK, Q_BLOCK), 1)
    qs = q * (HEAD_DIM ** -0.5)
    if i == 0:
        s = _bdot(qs, k, "nt") + (ct - crow)
        s = jnp.where((kq <= tq) & ((kq >= PAD_ROWS) | (tq < PAD_ROWS)), s, NEG)
    else:
        crow = jnp.where(_iota((1, kl), 1) < PAD_ROWS, -NEG, crow)
        s = _bdot(qs, k, "nt") + (ct - crow)
        s = jnp.concatenate([s[:, :kl - Q_BLOCK], jnp.where(kq <= tq, s[:, kl - Q_BLOCK:], NEG)], axis=1)
    if lse is not None:
        return jnp.exp(s - lse)
    m = jnp.max(s, axis=1, keepdims=True)
    p = jnp.exp(s - m)
    tot = jnp.sum(p, axis=1, keepdims=True)
    return p / tot, m + jnp.log(tot)


FOX_HEADS_PER_STEP = 2


def _fox_specs(lp, nh):
    hw = FOX_HEADS_PER_STEP * LANES
    return [pl.BlockSpec((Q_BLOCK, hw), lambda g, i: (i, g)),
            pl.BlockSpec((lp, hw), lambda g, i: (0, nh // FOX_HEADS_PER_STEP + g)),
            pl.BlockSpec((lp, hw), lambda g, i: (0, 6 * nh // FOX_HEADS_PER_STEP + g)),
            pl.BlockSpec((Q_BLOCK, LANES), lambda g, i: (i, 0)),
            pl.BlockSpec((LANES, lp), lambda g, i: (0, 0))]


def _fox_fwd(qkn, proj, gates, gtf, nh):
    lp = qkn.shape[0]

    def body(q_ref, k_ref, v_ref, g_ref, gt_ref, o_ref, lse_ref):
        g, i = pl.program_id(0), pl.program_id(1)
        for j in range(lp // Q_BLOCK):
            @pl.when(i == j)
            def _(j=j):
                kl = (j + 1) * Q_BLOCK
                for hh in range(FOX_HEADS_PER_STEP):
                    h = FOX_HEADS_PER_STEP * g + hh
                    sl = slice(hh * LANES, (hh + 1) * LANES)
                    p, lse = _fox_probs(q_ref[:, sl], k_ref[0:kl, sl], g_ref[...], gt_ref[pl.ds(4 * nh + h, 1), :][:, 0:kl],
                                        h, j, nh)
                    o_ref[:, sl] = _bdot(p, v_ref[0:kl, sl])
                    lse_ref[:, sl] = jnp.broadcast_to(lse, (Q_BLOCK, LANES))

    blk = pl.BlockSpec((Q_BLOCK, FOX_HEADS_PER_STEP * LANES), lambda g, i: (i, g))
    return pl.pallas_call(
        body, grid=(nh // FOX_HEADS_PER_STEP, lp // Q_BLOCK), in_specs=_fox_specs(lp, nh), out_specs=[blk, blk],
        out_shape=[jax.ShapeDtypeStruct((lp, nh * HEAD_DIM), F32)] * 2, name="fox_fwd",
        compiler_params=_cp("parallel", "parallel"))(qkn, qkn, proj, gates, gtf)


def _fox_bwd(qkn, proj, gates, gtf, lse, do, dproj, nh):
    lp = qkn.shape[0]
    nq = lp // Q_BLOCK
    w = nh * HEAD_DIM
    scale = HEAD_DIM ** -0.5

    def body(q_ref, k_ref, v_ref, g_ref, gt_ref, lse_ref, do_ref, _, dq_ref, dk_ref, dc_ref, dv_ref, dv_scr):
        g, i = pl.program_id(0), pl.program_id(1)

        @pl.when(i == 0)
        def _():
            dk_ref[...] = jnp.zeros_like(dk_ref)
            dv_scr[...] = jnp.zeros_like(dv_scr)
            dc_ref[...] = jnp.zeros_like(dc_ref)
        for j in range(nq):
            @pl.when(i == j)
            def _(j=j):
                kl = (j + 1) * Q_BLOCK
                for hh in range(FOX_HEADS_PER_STEP):
                    h = FOX_HEADS_PER_STEP * g + hh
                    sl = slice(hh * LANES, (hh + 1) * LANES)
                    q, k = q_ref[:, sl], k_ref[0:kl, sl]
                    p = _fox_probs(q, k, g_ref[...], gt_ref[pl.ds(4 * nh + h, 1), :][:, 0:kl], h, j, nh,
                                   lse_ref[:, sl][:, 0:1])
                    dout = do_ref[:, sl]
                    dp = _bdot(dout, v_ref[0:kl, sl], "nt")
                    ds = p * (dp - jnp.sum(p * dp, axis=1, keepdims=True))
                    dq_ref[:, sl] = _bdot(ds, k) * scale
                    dk_ref[0:kl, sl] += _bdot(ds, q * scale, "tn")
                    dv_scr[0:kl, sl] += _bdot(p, dout, "tn")
                    dc_ref[hh, :, 0:kl] -= jnp.sum(ds, axis=0, keepdims=True)

        @pl.when(i == nq - 1)
        def _():
            dv_ref[...] = dv_scr[...].astype(BF16)

    hw = FOX_HEADS_PER_STEP * LANES
    blk = pl.BlockSpec((Q_BLOCK, hw), lambda g, i: (i, g))
    col = pl.BlockSpec((lp, hw), lambda g, i: (0, g))
    return pl.pallas_call(
        body, grid=(nh // FOX_HEADS_PER_STEP, nq), in_specs=_fox_specs(lp, nh) + [blk, blk, _ANY],
        out_specs=[blk, col, pl.BlockSpec((FOX_HEADS_PER_STEP, 1, lp), lambda g, i: (g, 0, 0)),
                   pl.BlockSpec((lp, hw), lambda g, i: (0, 6 * nh // FOX_HEADS_PER_STEP + g))],
        out_shape=[jax.ShapeDtypeStruct((lp, w), F32)] * 2 + [jax.ShapeDtypeStruct((nh, 1, lp), F32),
                                                             jax.ShapeDtypeStruct(dproj.shape, BF16)],
        scratch_shapes=[pltpu.VMEM((lp, hw), F32)], input_output_aliases={7: 3},
        name="fox_bwd", compiler_params=_cp("parallel", "arbitrary"))(qkn, qkn, proj, gates, gtf, lse, do, dproj)


def _merge_fox(o_fox, proj, merged, nh):
    lp = o_fox.shape[0]

    def body(o_ref, z_ref, _, m_ref):
        z = z_ref[...]
        m_ref[...] = (o_ref[...] * (z * _sigmoid(z))).astype(BF16)

    return pl.pallas_call(
        body, grid=(nh,),
        in_specs=[pl.BlockSpec((lp, LANES), lambda s: (0, s)), pl.BlockSpec((lp, LANES), lambda s: (0, 7 * nh + s)), _ANY],
        out_specs=pl.BlockSpec((lp, LANES), lambda s: (0, nh + s)),
        out_shape=jax.ShapeDtypeStruct(merged.shape, BF16), input_output_aliases={2: 0}, name="merge_fox",
        compiler_params=_cp("parallel"))(o_fox, proj, merged)


def _merge_fox_bwd(o_fox, proj, dmerged, dproj, nh):
    lp = o_fox.shape[0]

    def body(o_ref, z_ref, dm_ref, _, do_ref, dz_ref):
        silu, dsilu = _silu_and_grad(z_ref[...])
        dm = dm_ref[...]
        do_ref[...] = dm * silu
        dz_ref[...] = (dm * o_ref[...] * dsilu).astype(BF16)

    w = nh * HEAD_DIM
    return pl.pallas_call(
        body, grid=(nh,),
        in_specs=[pl.BlockSpec((lp, LANES), lambda s: (0, s)), pl.BlockSpec((lp, LANES), lambda s: (0, 7 * nh + s)),
                  pl.BlockSpec((lp, LANES), lambda s: (0, nh + s)), _ANY],
        out_specs=[pl.BlockSpec((lp, LANES), lambda s: (0, s)), pl.BlockSpec((lp, LANES), lambda s: (0, 7 * nh + s))],
        out_shape=[jax.ShapeDtypeStruct((lp, w), F32), jax.ShapeDtypeStruct(dproj.shape, BF16)],
        input_output_aliases={3: 1}, name="merge_fox_bwd", compiler_params=_cp("parallel"))(o_fox, proj, dmerged, dproj)


def _post(out, x, target, post_w):
    lp, d = out.shape

    def body(o_ref, x_ref, t_ref, w_ref, dy_ref, do_ref, loss_ref, dw_ref):
        i = pl.program_id(0)

        @pl.when(i == 0)
        def _():
            loss_ref[...] = jnp.zeros_like(loss_ref)
            dw_ref[...] = jnp.zeros_like(dw_ref)
        o = o_ref[...]
        r = _rms(o)
        nrm = o * r
        err = jnp.where(i > 0, x_ref[...] + nrm * w_ref[...] - t_ref[...], 0.0)
        loss_ref[0:1, :] += 0.5 * jnp.sum(jnp.sum(err * err, axis=1, keepdims=True), axis=0, keepdims=True) / d
        dy = err / d
        dy_ref[...] = dy
        dw_ref[...] += jnp.sum(dy * nrm, axis=0, keepdims=True)
        dyw = dy * w_ref[...]
        do_ref[...] = (r * (dyw - nrm * jnp.mean(dyw * nrm, axis=-1, keepdims=True))).astype(BF16)

    row = pl.BlockSpec((Q_BLOCK, d), lambda i: (i, 0))
    vec = pl.BlockSpec((1, d), lambda i: (0, 0))
    return pl.pallas_call(
        body, grid=(lp // Q_BLOCK,), in_specs=[row, _x_rows(d), _x_rows(d), vec],
        out_specs=[_x_rows(d), row, pl.BlockSpec((8, LANES), lambda i: (0, 0)), vec],
        out_shape=[jax.ShapeDtypeStruct(x.shape, F32), jax.ShapeDtypeStruct((lp, d), BF16),
                   jax.ShapeDtypeStruct((8, LANES), F32), jax.ShapeDtypeStruct((1, d), F32)],
        name="post", compiler_params=_cp("arbitrary"))(out, x, target, post_w)


def _prenorm_bwd(dxn, x, meta, w, dy, rider=None):
    seq, d = x.shape
    lp = seq + Q_BLOCK

    def body(start, finish, dx_ref, x_ref, m_ref, w_ref, dy_ref, gx_ref, gm_ref, dw_ref):
        i = pl.program_id(0)
        pl.when(i == 0)(start)
        h = _h_tile(i, x_ref, m_ref)
        r = _rms(h)
        xh = h * r
        dxn_ = dx_ref[...]
        dxw = dxn_ * w_ref[...]
        dh = jnp.where(i > 0, dy_ref[...], 0.0) + r * (dxw - xh * jnp.mean(dxw * xh, axis=-1, keepdims=True))
        gx_ref[...] = dh

        @pl.when(i == 0)
        def _():
            dw_ref[...] = jnp.zeros_like(dw_ref)
            gm_ref[...] = dh[PAD_ROWS:, :]
        dw_ref[...] += jnp.sum(dxn_ * xh, axis=0, keepdims=True)
        pl.when(i == lp // Q_BLOCK - 1)(finish)

    vec = pl.BlockSpec((1, d), lambda i: (0, 0))
    met = pl.BlockSpec((N_META, d), lambda i: (0, 0))
    outs, got = _hosted_call(
        body, [rider], 5, 3, 0, grid=(lp // Q_BLOCK,),
        in_specs=[pl.BlockSpec((Q_BLOCK, d), lambda i: (i, 0)), _x_rows(d), met, vec, _x_rows(d)],
        out_specs=[_x_rows(d), met, vec],
        out_shape=[jax.ShapeDtypeStruct((seq, d), F32), jax.ShapeDtypeStruct((N_META, d), F32),
                   jax.ShapeDtypeStruct((1, d), F32)],
        name="prenorm_bwd", compiler_params=_cp("arbitrary"))(dxn, x, meta, w, dy)
    return outs, (got[0] if got else None)


def _layer_grads(x, target, meta, pre_w, wfull, conv_wt, a_log, dt_bias, gdn_norm_w, fq_w, fk_w, f_bias, w_out, post_w,
                 late_weights=None, w_out_grads=None):
    nh = a_log.shape[1]
    zpad = jnp.zeros((1, LANES - 3 * nh), F32)
    bias_row = jnp.concatenate([jnp.zeros((1, nh), F32), dt_bias, f_bias, zpad], axis=1)
    nega_row = jnp.concatenate([jnp.zeros((1, nh), F32), -jnp.exp(a_log), jnp.zeros((1, nh), F32), zpad], axis=1)
    qk_w = jnp.stack([fq_w, fk_w])

    if isinstance(wfull, tuple):
        rider, meta_at, project = wfull
        xn, got = _prenorm_gathering(x, pre_w, rider, meta_at)
        proj, wfull, conv_wt, meta = project(xn, got)
    else:
        xn = _prenorm(x, meta, pre_w)
        proj = _matmul(xn, wfull, "nn", MM_TILE, F32, "proj")
    qkv = _gdn_prep(proj, conv_wt, nh)
    gates, gt3, gtf = _gates(proj, bias_row, nega_row, nh)
    (o_gdn, s_all, t_all), got = _gdn_fwd(qkv, gates, gt3, nh, None if late_weights is None else late_weights[0])
    if late_weights is not None:
        w_out = late_weights[1](got)
    qkn = _fox_prep(proj, qk_w, nh)
    o_fox, fox_lse = _fox_fwd(qkn, proj, gates, gtf, nh)
    merged = _merge_fox(o_fox, proj, _merge_gdn(o_gdn, proj, gdn_norm_w, nh), nh)
    out = _matmul(merged, w_out, "nn", 4 * LANES, F32, "out_proj")
    dy, dout, loss_blk, dpost_w = _post(out, x, target, post_w)

    dw_out = _matmul(merged, dout, "tn", 4 * LANES, BF16, "dw_out")
    if w_out_grads is None:
        dmerged, gdn_rider = _matmul(dout, w_out, "nt", 4 * LANES, F32, "dmerged"), None
    else:
        dmerged, got = _matmul(dout, w_out, "nt", 4 * LANES, F32, "dmerged", w_out_grads[0](dw_out))
        gdn_rider = w_out_grads[1](dw_out, got)
    do_gdn, dproj, dgdn_norm_w = _merge_gdn_bwd(o_gdn, proj, gdn_norm_w, dmerged, nh)
    do_fox, dproj = _merge_fox_bwd(o_fox, proj, dmerged, dproj, nh)
    dqn, dkn, dc_t, dproj = _fox_bwd(qkn, proj, gates, gtf, fox_lse, do_fox, dproj, nh)
    dproj, dqk_w = _fox_prep_bwd(proj, qk_w, dqn, dkn, dproj, nh)
    (dgq, dgk, dgv, dgate), w_out_parts = _gdn_bwd(qkv, gates, gt3, s_all, t_all, do_gdn, nh, gdn_rider)
    dproj, dconv_wt = _gdn_prep_bwd(proj, conv_wt, dgq, dgk, dgv, dproj, nh)
    dc_rows = jnp.pad(dc_t.reshape(nh, -1), ((2 * nh, LANES - 3 * nh), (0, 0)))
    dproj, gate_sums = _gates_bwd(proj, bias_row, nega_row, gates, dgate, dc_rows, dproj, nh)
    return dict(
        loss=loss_blk[0:1, 0:1], dy=dy, xn=xn, dproj=dproj, post_w=dpost_w,
        conv_wt=dconv_wt, a_log=gate_sums[1:2, nh:2 * nh], dt_bias=gate_sums[0:1, nh:2 * nh],
        gdn_norm_w=dgdn_norm_w, fq_w=dqk_w[0], fk_w=dqk_w[1], f_bias=gate_sums[0:1, 2 * nh:3 * nh], w_out=dw_out,
        w_out_parts=w_out_parts, wfull=wfull, meta=meta)


def _cast_bf16(a, tr, name):
    r, c = a.shape

    def body(a_ref, o_ref):
        o_ref[...] = a_ref[...].astype(BF16)

    return pl.pallas_call(
        body, grid=(r // tr,), in_specs=[pl.BlockSpec((tr, c), lambda i: (i, 0))],
        out_specs=pl.BlockSpec((tr, c), lambda i: (i, 0)), out_shape=jax.ShapeDtypeStruct((r, c), BF16),
        name=name, compiler_params=_cp("parallel"))(a)


def _column_major(a):
    return jnp.transpose(a, (2, 0, 1))


def _cast_bf16_column_major(a3, pieces, name):
    _, r, c = a3.shape
    rows = r // pieces

    def body(a_ref, *o_refs):
        t = a_ref[...].reshape(LANES, r).T.astype(BF16)
        for k, o_ref in enumerate(o_refs):
            o_ref[...] = t[k * rows:(k + 1) * rows]

    return pl.pallas_call(
        body, grid=(pl.cdiv(c, LANES),), in_specs=[pl.BlockSpec((LANES, 1, r), lambda i: (i, 0, 0))],
        out_specs=[pl.BlockSpec((rows, LANES), lambda i: (0, i))] * pieces,
        out_shape=[jax.ShapeDtypeStruct((rows, c), BF16)] * pieces, name=name, compiler_params=_cp("parallel"))(_column_major(a3))


def _adamw_column_major(w3, parts, m3, v3, name):
    _, r, c = w3.shape
    n_parts = parts[0].shape[0]

    def body(w_ref, *refs):
        p_refs, (m_ref, v_ref, g_ref, d_ref, nm_ref, nv_ref) = refs[:len(parts)], refs[len(parts):]
        sums = []
        for p_ref in p_refs:
            g = p_ref[0].astype(F32)
            for s in range(1, n_parts):
                g = g + p_ref[s].astype(F32)
            sums.append(g)
        g = jnp.concatenate(sums, axis=0).T
        flat = lambda ref: ref[...].reshape(LANES, r)
        m_new = ADAM_B1 * flat(m_ref) + (1.0 - ADAM_B1) * g
        v_new = ADAM_B2 * flat(v_ref) + (1.0 - ADAM_B2) * (g * g)
        m_hat = m_new / (1.0 - ADAM_B1 ** ADAM_STEP)
        v_hat = v_new / (1.0 - ADAM_B2 ** ADAM_STEP)
        delta = -ADAM_LR * (m_hat / (jnp.sqrt(v_hat) + ADAM_EPS) + ADAM_WD * flat(w_ref))
        for ref, val in ((g_ref, g), (d_ref, delta), (nm_ref, m_new), (nv_ref, v_new)):
            ref[...] = val.reshape(LANES, 1, r)

    blk = pl.BlockSpec((LANES, 1, r), lambda i: (i, 0, 0))
    outs = pl.pallas_call(
        body, grid=(pl.cdiv(c, LANES),),
        in_specs=[blk] + [pl.BlockSpec((n_parts, p.shape[1], LANES), lambda i: (0, 0, i)) for p in parts] + [blk, blk],
        out_specs=[blk] * 4, out_shape=[jax.ShapeDtypeStruct((c, 1, r), F32)] * 4, name=name,
        compiler_params=_cp("parallel"))(_column_major(w3), *parts, _column_major(m3), _column_major(v3))
    return [jnp.transpose(o, (1, 2, 0)) for o in outs]


def _gather_copies(ins, outs, send_sems, recv_sems, local_sems):
    n = len(ins)
    x, y, c = lax.axis_index("x"), lax.axis_index("y"), lax.axis_index("c")
    me, sibling = (x, y, c), (x, y, 1 - c)
    xn, yn, dg = (1 - x, y), (x, 1 - y), (1 - x, 1 - y)

    def copy(a, k, block, to, src=None):
        px, py, pc = block
        rows = outs[a].at[4 * px + 2 * py + pc]
        return pltpu.make_async_remote_copy(
            src_ref=rows if src is None else src, dst_ref=rows, send_sem=send_sems.at[a, k],
            recv_sem=recv_sems.at[a, k], device_id=to, device_id_type=_MESH)

    local = [pltpu.make_async_copy(ins[a], outs[a].at[4 * x + 2 * y + c], local_sems.at[a]) for a in range(n)]
    own = [cp for a in range(n) for cp in (copy(a, 0, me, sibling, src=ins[a]), copy(a, 1, me, (*xn, c), src=ins[a]),
                                           copy(a, 2, me, (*yn, c), src=ins[a]))]

    def start():
        for cp in local + own:
            cp.start()

    def finish():
        for a in range(n):
            @pl.when(c == 1)
            def _(a=a):
                copy(a, 1, (*xn, c), me).wait_recv()
                copy(a, 3, (*xn, c), (*yn, c)).start()

            @pl.when(c == 0)
            def _(a=a):
                copy(a, 2, (*yn, c), me).wait_recv()
                copy(a, 3, (*yn, c), (*xn, c)).start()
        for a in range(n):
            pl.when(c == 0)(copy(a, 1, (*xn, c), me).wait_recv)
            copy(a, 4, (*xn, c), sibling).start()
            pl.when(c == 1)(copy(a, 2, (*yn, c), me).wait_recv)
            copy(a, 5, (*yn, c), sibling).start()
        for a in range(n):
            copy(a, 3, (*dg, c), me).wait_recv()
            copy(a, 6, (*dg, c), sibling).start()
        for a in range(n):
            copy(a, 0, sibling, me).wait_recv()
            for k, chip in ((4, xn), (5, yn), (6, dg)):
                copy(a, k, (*chip, 1 - c), me).wait_recv()
                copy(a, k, (*chip, c), sibling).wait_send()
            copy(a, 3, (*xn, c), (*yn, c)).wait_send()
        for cp in own:
            cp.wait_send()
        for cp in local:
            cp.wait()

    return start, finish


def _gather_scratch(n):
    return [pltpu.SemaphoreType.DMA((n, N_DEV - 1)), pltpu.SemaphoreType.DMA((n, N_DEV - 1)), pltpu.SemaphoreType.DMA((n,))]


def _gather_rider(arrays):
    return _Rider(list(arrays), [jax.ShapeDtypeStruct((N_DEV,) + a.shape, a.dtype) for a in arrays],
                  _gather_scratch(len(arrays)), {}, lambda ins, outs, scratch: _gather_copies(ins, outs, *scratch))


def _all_gather(arrays, name):
    n = len(arrays)

    def body(*refs):
        start, finish = _gather_copies(refs[:n], refs[n:2 * n], *refs[2 * n:])
        start()
        finish()

    return pl.pallas_call(
        body, in_specs=[_ANY] * n, out_specs=[_ANY] * n,
        out_shape=[jax.ShapeDtypeStruct((N_DEV,) + a.shape, a.dtype) for a in arrays],
        scratch_shapes=_gather_scratch(n), name=name)(*arrays)


SLAB = 10 * LANES


def _slab_start(blk, nh, cols):
    in_second_half = blk >= N_DEV // 2
    shift = (2 * nh if in_second_half else 0) if isinstance(blk, int) else jnp.where(in_second_half, 2 * nh, 0)
    return (blk * cols - shift) // LANES * LANES


def _pair_rider(dw_rows=None, parts=None, nh=None, after=None):
    if dw_rows is not None:
        r, full = dw_rows.shape
        cols = (full - NARROW + 3 * nh) // N_DEV
        out_shapes = [jax.ShapeDtypeStruct((N_CHIP, r, SLAB), dw_rows.dtype), jax.ShapeDtypeStruct((r, NARROW), dw_rows.dtype)]
    else:
        out_shapes = [jax.ShapeDtypeStruct((N_CHIP,) + parts.shape[1:], parts.dtype)]

    def make(ins, outs, scratch):
        send_sems, recv_sems = scratch
        x, y, c = lax.axis_index("x"), lax.axis_index("y"), lax.axis_index("c")
        kw = lambda k: dict(send_sem=send_sems.at[k], recv_sem=recv_sems.at[k], device_id=(x, y, 1 - c), device_id_type=_MESH)
        copies = []
        for q in range(N_CHIP):
            if dw_rows is not None:
                first = pl.multiple_of(_slab_start(2 * q + 1 - c, nh, cols), LANES)
                copies.append(pltpu.make_async_remote_copy(src_ref=ins[0].at[:, pl.ds(first, SLAB)], dst_ref=outs[0].at[q], **kw(q)))
            else:
                copies.append(pltpu.make_async_remote_copy(src_ref=ins[0].at[2 * q + 1 - c], dst_ref=outs[0].at[q], **kw(q)))
        if dw_rows is not None:
            copies.append(pltpu.make_async_remote_copy(src_ref=ins[0].at[:, pl.ds(full - NARROW, NARROW)], dst_ref=outs[1],
                                                       **kw(N_CHIP)))

        def start():
            for cp in copies:
                cp.start()

        def finish():
            for cp in copies:
                cp.wait()

        return start, finish

    return _Rider([dw_rows if dw_rows is not None else parts] + ([] if after is None else [after]), out_shapes,
                  [pltpu.SemaphoreType.DMA((N_CHIP + 1,)), pltpu.SemaphoreType.DMA((N_CHIP + 1,))], {}, make)


def _relayout_pair_sum(dwfull, got_slabs, got_tail, core, nh, tr, name):
    d, full = dwfull.shape
    w = nh * HEAD_DIM
    cols = (8 * w + 3 * nh) // N_DEV
    segs = _native_segments(nh)

    def block(f_ref, s_ref, t_ref, q, blk):
        st = _slab_start(blk, nh, cols)
        wide = f_ref[:, st:st + SLAB].astype(F32) + s_ref[q].astype(F32)
        tail = f_ref[:, 8 * w:].astype(F32) + t_ref[...].astype(F32)
        pieces = []
        for s0, s1, t0 in segs:
            lo, hi = max(s0, blk * cols), min(s1, (blk + 1) * cols)
            if lo < hi:
                at = t0 + lo - s0
                pieces.append(tail[:, at - 8 * w:at - 8 * w + hi - lo] if at >= 8 * w else wide[:, at - st:at - st + hi - lo])
        return (pieces[0] if len(pieces) == 1 else jnp.concatenate(pieces, axis=1)).astype(dwfull.dtype)

    def body(core_ref, f_ref, s_ref, t_ref, o_ref):
        for parity in range(2):
            @pl.when(core_ref[0] == parity)
            def _(parity=parity):
                for q in range(N_CHIP):
                    o_ref[q] = block(f_ref, s_ref, t_ref, q, 2 * q + parity)

    return pl.pallas_call(
        body,
        grid_spec=pltpu.PrefetchScalarGridSpec(
            num_scalar_prefetch=1, grid=(d // tr,),
            in_specs=[pl.BlockSpec((tr, full), lambda i, c_ref: (i, 0)), pl.BlockSpec((N_CHIP, tr, SLAB), lambda i, c_ref: (0, i, 0)),
                      pl.BlockSpec((tr, NARROW), lambda i, c_ref: (i, 0))],
            out_specs=pl.BlockSpec((N_CHIP, tr, cols), lambda i, c_ref: (0, i, 0))),
        out_shape=jax.ShapeDtypeStruct((N_CHIP, d, cols), dwfull.dtype), name=name,
        compiler_params=_cp("parallel"))(core, dwfull, got_slabs, got_tail)


def _pair_sum(parts, got, core, tr, name):
    _, r, c = parts.shape

    def body(core_ref, p_ref, g_ref, o_ref):
        o_ref[...] = (p_ref[...].astype(F32) + g_ref[...].astype(F32)).astype(o_ref.dtype)

    return pl.pallas_call(
        body,
        grid_spec=pltpu.PrefetchScalarGridSpec(
            num_scalar_prefetch=1, grid=(N_CHIP, r // tr),
            in_specs=[pl.BlockSpec((1, tr, c), lambda q, i, core_ref: (2 * q + core_ref[0], i, 0)),
                      pl.BlockSpec((1, tr, c), lambda q, i, core_ref: (q, i, 0))],
            out_specs=pl.BlockSpec((1, tr, c), lambda q, i, core_ref: (q, i, 0))),
        out_shape=jax.ShapeDtypeStruct((N_CHIP, r, c), parts.dtype), name=name,
        compiler_params=_cp("parallel", "parallel"))(core, parts, got)


def _native_segments(nh):
    w = nh * HEAD_DIM
    return [(0, 4 * w, 0), (4 * w, 4 * w + 2 * nh, 8 * w), (4 * w + 2 * nh, 8 * w + 2 * nh, 4 * w),
            (8 * w + 2 * nh, 8 * w + 3 * nh, 8 * w + 2 * nh)]


def _relayout_w_in(wg, nh, tr, name, rows_total=None, into=None):
    _, d, cols = wg.shape
    w = nh * HEAD_DIM
    rows_total = into.shape[0] if into is not None else rows_total or d
    first = (rows_total - d) // tr if into is not None else 0

    def native(ref, j0, j1):
        out = []
        while j0 < j1:
            blk = j0 // cols
            end = min(j1, (blk + 1) * cols)
            out.append(ref[blk, :, pl.ds(j0 - blk * cols, end - j0)])
            j0 = end
        return out

    def body(g_ref, *refs):
        o_ref = refs[-1]
        for cidx in range(8 * w // LANES):
            j0 = cidx * LANES + (0 if cidx * LANES < 4 * w else 2 * nh)
            pieces = native(g_ref, j0, j0 + LANES)
            o_ref[:, cidx * LANES:(cidx + 1) * LANES] = pieces[0] if len(pieces) == 1 else jnp.concatenate(pieces, axis=1)
        pieces = (native(g_ref, 4 * w, 4 * w + 2 * nh) + native(g_ref, 8 * w + 2 * nh, 8 * w + 3 * nh)
                  + [jnp.zeros((tr, NARROW - 3 * nh), wg.dtype)])
        o_ref[:, 8 * w:] = jnp.concatenate(pieces, axis=1)

    return pl.pallas_call(
        body, grid=(d // tr,), in_specs=[pl.BlockSpec((N_DEV, tr, cols), lambda i: (0, i, 0))] + [_ANY] * (into is not None),
        out_specs=pl.BlockSpec((tr, 8 * w + NARROW), lambda i: (first + i, 0)),
        out_shape=jax.ShapeDtypeStruct((rows_total, 8 * w + NARROW), wg.dtype),
        input_output_aliases={1: 0} if into is not None else {},
        name=name, compiler_params=_cp("parallel"))(wg, *([into] if into is not None else []))


def _adamw(w, parts, m, v, tr, name, after=None):
    r, c = w.shape
    n_parts = parts.shape[0]

    def body(w_ref, p_ref, m_ref, v_ref, *refs):
        g_ref, d_ref, nm_ref, nv_ref = refs[-5:-1] if after is not None else refs
        if after is not None:
            refs[-1][...] = jnp.zeros_like(refs[-1])
        g = p_ref[0].astype(F32)
        for s in range(1, n_parts):
            g = g + p_ref[s].astype(F32)
        m_new = ADAM_B1 * m_ref[...] + (1.0 - ADAM_B1) * g
        v_new = ADAM_B2 * v_ref[...] + (1.0 - ADAM_B2) * (g * g)
        m_hat = m_new / (1.0 - ADAM_B1 ** ADAM_STEP)
        v_hat = v_new / (1.0 - ADAM_B2 ** ADAM_STEP)
        g_ref[...] = g
        d_ref[...] = -ADAM_LR * (m_hat / (jnp.sqrt(v_hat) + ADAM_EPS) + ADAM_WD * w_ref[...])
        nm_ref[...] = m_new
        nv_ref[...] = v_new

    blk = pl.BlockSpec((tr, c), lambda i: (i, 0))
    extra = after is not None
    return pl.pallas_call(
        body, grid=(r // tr,),
        in_specs=[blk, pl.BlockSpec((n_parts, tr, c), lambda i: (0, i, 0)), blk, blk] + [_ANY] * extra,
        out_specs=[blk] * 4 + [pl.BlockSpec((8, LANES), lambda i: (0, 0))] * extra,
        out_shape=[jax.ShapeDtypeStruct((r, c), F32)] * 4 + [jax.ShapeDtypeStruct((8, LANES), F32)] * extra, name=name,
        compiler_params=_cp("arbitrary" if extra else "parallel"))(w, parts, m, v, *([after] if extra else []))


def _adamw_conv(w, gathered, dev, m, v, name):
    r, c = w.shape
    n_parts = gathered.shape[0]

    def body(dev_ref, w_ref, p_ref, m_ref, v_ref, g_ref, d_ref, nm_ref, nv_ref):
        gt = p_ref[0].astype(F32)
        for s in range(1, n_parts):
            gt = gt + p_ref[s].astype(F32)
        gt = jnp.concatenate([gt, jnp.zeros((8 - c, r), F32)], axis=0)
        unit = (_iota((8, LANES), 0) == _iota((8, LANES), 1)).astype(F32)
        g = _hdot(gt, unit, "tn")[:, 0:c]
        m_new = ADAM_B1 * m_ref[...] + (1.0 - ADAM_B1) * g
        v_new = ADAM_B2 * v_ref[...] + (1.0 - ADAM_B2) * (g * g)
        m_hat = m_new / (1.0 - ADAM_B1 ** ADAM_STEP)
        v_hat = v_new / (1.0 - ADAM_B2 ** ADAM_STEP)
        g_ref[...] = g
        d_ref[...] = -ADAM_LR * (m_hat / (jnp.sqrt(v_hat) + ADAM_EPS) + ADAM_WD * w_ref[...])
        nm_ref[...] = m_new
        nv_ref[...] = v_new

    blk = pl.BlockSpec((r, c), lambda i, dev_ref: (0, 0))
    return pl.pallas_call(
        body,
        grid_spec=pltpu.PrefetchScalarGridSpec(
            num_scalar_prefetch=1, grid=(1,),
            in_specs=[blk, pl.BlockSpec((n_parts, c, r), lambda i, dev_ref: (0, 0, dev_ref[0])), blk, blk], out_specs=[blk] * 4),
        out_shape=[jax.ShapeDtypeStruct((r, c), F32)] * 4, name=name, compiler_params=_cp("arbitrary"))(dev, w, gathered, m, v)


def _pack_small(d, pre, post, a_log, dt_bias, f_bias, gdn_w, fq_w, fk_w, extra):
    row2 = jnp.concatenate([a_log, dt_bias, f_bias, gdn_w, fq_w, fk_w, extra], axis=1)
    row2 = jnp.pad(row2, ((0, 0), (0, d - row2.shape[1])))
    return jnp.concatenate([pre, post, row2, jnp.zeros((5, d), F32)], axis=0)


def _unpack_small(p, nh):
    o = 3 * nh
    return dict(pre=p[0:1], post=p[1:2], a_log=p[2:3, 0:nh], dt_bias=p[2:3, nh:2 * nh], f_bias=p[2:3, 2 * nh:o],
                gdn_w=p[2:3, o:o + HEAD_DIM], fq_w=p[2:3, o + HEAD_DIM:o + 2 * HEAD_DIM],
                fk_w=p[2:3, o + 2 * HEAD_DIM:o + 3 * HEAD_DIM], extra=p[2, o + 3 * HEAD_DIM])


def kernel(x, meta_tokens, pre_norm_w, w_in, conv_w, a_log, dt_bias, gdn_norm_w, fox_q_norm_w, fox_k_norm_w, fox_f_bias, w_out, post_norm_w, loss_target, m_meta_tokens, m_pre_norm_w, m_w_in, m_conv_w, m_a_log, m_dt_bias, m_gdn_norm_w, m_fox_q_norm_w, m_fox_k_norm_w, m_fox_f_bias, m_w_out, m_post_norm_w, v_meta_tokens, v_pre_norm_w, v_w_in, v_conv_w, v_a_log, v_dt_bias, v_gdn_norm_w, v_fox_q_norm_w, v_fox_k_norm_w, v_fox_f_bias, v_w_out, v_post_norm_w):
    nh = a_log.shape[1]
    d = x.shape[-1]
    w = nh * HEAD_DIM
    zero = jnp.zeros((1, 1), F32)

    w_in_a, w_in_b = _cast_bf16_column_major(w_in, 2, "cast_w_in")

    def project(xn, got):
        wg, cg, mg = got
        half = (d // 2, 0), (d // 2, 1)
        wfull = _relayout_w_in(wg, nh, 256, "relayout_w_in_a", rows_total=d)
        proj, got = _matmul(xn, wfull, "nn", MM_TILE, F32, "proj_a", _gather_rider([w_in_b]), a_cols=half[0], b_rows=half[0])
        wfull = _relayout_w_in(got[0], nh, 256, "relayout_w_in_b", into=wfull)
        proj = _matmul(xn, wfull, "nn", MM_TILE, F32, "proj_b", a_cols=half[1], b_rows=half[1], acc=proj)
        return proj, wfull, cg.transpose(1, 0, 2).reshape(CONV_WIDTH, 3 * w), mg.transpose(1, 0, 2).reshape(N_META, d)

    late_weights = (_gather_rider([_cast_bf16(w_out[0], 256, "cast_w_out")]), lambda got: got[0].reshape(2 * w, d))
    core = lax.axis_index("c")
    dev = 4 * lax.axis_index("x") + 2 * lax.axis_index("y") + core
    core_arr = jnp.reshape(core, (1,)).astype(jnp.int32)

    out_parts = lambda dw_out: dw_out.reshape(N_DEV, 2 * w // N_DEV, d)
    g = _layer_grads(
        x[0], loss_target[0], None, pre_norm_w, (_gather_rider([w_in_a, conv_w[0].T, meta_tokens]), 2, project), None,
        a_log, dt_bias, gdn_norm_w,
        fox_q_norm_w, fox_k_norm_w, fox_f_bias, None, post_norm_w, late_weights=late_weights,
        w_out_grads=(lambda dw_out: _pair_rider(parts=out_parts(dw_out)),
                     lambda dw_out, got: _chip_rider(_pair_sum(out_parts(dw_out), got[0], core_arr, 256, "pair_sum_w_out"))))
    p_out = g["w_out_parts"][0]
    xn, dproj, wfull, meta_full = g["xn"], g["dproj"], g["wfull"], g["meta"]
    flights, token, dw, rider = [], None, None, None

    def exchange(dw, got, i):
        sums = _relayout_pair_sum(dw, got[0], got[1], core_arr, nh, 128, f"relayout_pair_sum_{i}")
        flight, token = _chip_exchange_start(sums, f"chip_exchange_start_{i}")
        flights.append(flight)
        return token

    for i, (index, parts) in enumerate(DW_IN_PIECES):
        res = _matmul(xn, dproj, "tn", MM_TILE, BF16, f"dw_in_{i}", rider, a_cols=(d // parts, index))
        if i:
            token = exchange(dw, res[1], i - 1)
        dw = res[0] if i else res
        rider = _pair_rider(dw_rows=dw, nh=nh, after=token)
    dxn, (got,) = _dxn(dproj, wfull, [rider], MM_TILE, "dxn")
    token = exchange(dw, got, len(DW_IN_PIECES) - 1)
    *r_out, token = _adamw(w_out[0], p_out, m_w_out[0], v_w_out[0], 64, "adamw_w_out", after=token)
    (grad_x, dmeta, dpre_w), _ = _prenorm_bwd(dxn, x[0], meta_full, pre_norm_w + token[0:1, 0:1], g["dy"])
    small = _pack_small(d, dpre_w, g["post_w"], g["a_log"], g["dt_bias"], g["f_bias"], g["gdn_norm_w"], g["fq_w"],
                        g["fk_w"], g["loss"])
    a_conv, a_meta, p_small = _all_gather([g["conv_wt"], dmeta, small], "gather_small_grads")
    p_meta = lax.dynamic_slice_in_dim(a_meta, dev * meta_tokens.shape[1], meta_tokens.shape[1], axis=2)

    r_conv = _adamw_conv(conv_w[0], a_conv, jnp.reshape(dev, (1,)).astype(jnp.int32), m_conv_w[0], v_conv_w[0], "adamw_conv_w")
    r_meta = _adamw(meta_tokens, p_meta, m_meta_tokens, v_meta_tokens, N_META, "adamw_meta")
    pk = lambda pre, post, a, dt, gw, fq, fk, fb: _pack_small(d, pre, post, a, dt, fb, gw, fq, fk, zero)
    r_small = _adamw(
        pk(pre_norm_w, post_norm_w, a_log, dt_bias, gdn_norm_w, fox_q_norm_w, fox_k_norm_w, fox_f_bias), p_small,
        pk(m_pre_norm_w, m_post_norm_w, m_a_log, m_dt_bias, m_gdn_norm_w, m_fox_q_norm_w, m_fox_k_norm_w, m_fox_f_bias),
        pk(v_pre_norm_w, v_post_norm_w, v_a_log, v_dt_bias, v_gdn_norm_w, v_fox_q_norm_w, v_fox_k_norm_w, v_fox_f_bias),
        8, "adamw_small")
    p_in = _chip_exchange_wait(flights, r_small[0], "chip_exchange_wait")
    r_in = _adamw_column_major(w_in, p_in, m_w_in, v_w_in, "adamw_w_in")

    sm = [_unpack_small(r, nh) for r in r_small]
    outs = []
    for i in range(4):
        s = sm[i]
        outs += [r_meta[i], s["pre"], r_in[i], r_conv[i][None], s["a_log"], s["dt_bias"], s["gdn_w"], s["fq_w"],
                 s["fk_w"], s["f_bias"], r_out[i][None], s["post"]]
    return (sm[0]["extra"], grad_x[None], *outs)
```

```python
import jax
import jax.numpy as jnp
from jax import lax
from jax.experimental import pallas as pl
from jax.experimental.pallas import tpu as pltpu

F32, BF16 = jnp.float32, jnp.bfloat16
HEAD_DIM = 128
N_META = 16
CONV_WIDTH = 4
CHUNK = 128
Q_BLOCK = 128
LANES = 128
EPS = 1e-6
PAD_ROWS = Q_BLOCK - N_META
N_DEV = 8
N_CHIP = 4
VMEM_LIMIT = 56 * 1024 * 1024
NEG = -1e30
NARROW = 2 * LANES
MM_TILE = 6 * LANES
DW_IN_PIECES = ((0, 2), (2, 4), (3, 4))

ADAM_LR, ADAM_B1, ADAM_B2, ADAM_EPS, ADAM_WD, ADAM_STEP = 0.001, 0.9, 0.999, 1e-08, 0.01, 10

_DN = {"nn": (((1,), (0,)), ((), ())), "nt": (((1,), (1,)), ((), ())), "tn": (((0,), (0,)), ((), ()))}
_DN3 = {"nn": (((2,), (1,)), ((0,), (0,))), "nt": (((2,), (2,)), ((0,), (0,))), "tn": (((1,), (1,)), ((0,), (0,)))}
_ANY = pl.BlockSpec(memory_space=pl.ANY)
_MESH = pl.DeviceIdType.MESH


def _cp(*sem):
    return pltpu.CompilerParams(dimension_semantics=sem, vmem_limit_bytes=VMEM_LIMIT)


def _dot(a, b, dims="nn", prec=None):
    return lax.dot_general(a, b, _DN[dims], precision=prec, preferred_element_type=F32)


def _bdot(a, b, dims="nn"):
    return _dot(a.astype(BF16), b.astype(BF16), dims)


def _hdot(a, b, dims="nn"):
    return _dot(a, b, dims, prec=lax.Precision.HIGHEST)


def _dot3(a, b, dims="nn"):
    return lax.dot_general(a, b, _DN3[dims], preferred_element_type=F32)


def _bdot3(a, b, dims="nn"):
    return _dot3(a.astype(BF16), b.astype(BF16), dims)


def _split(a):
    hi = a.astype(BF16)
    return hi, (a - hi.astype(F32)).astype(BF16)


def _iota(shape, dim):
    return lax.broadcasted_iota(jnp.int32, shape, dim)


def _sigmoid(z):
    return 1.0 / (1.0 + jnp.exp(-z))


def _softplus(z):
    e = jnp.exp(-jnp.abs(z))
    u = 1.0 + e
    l1p = jnp.where(u == 1.0, e, jnp.log(u) * (e / jnp.where(u == 1.0, 1.0, u - 1.0)))
    return jnp.maximum(z, 0.0) + l1p


def _silu_and_grad(z):
    s = _sigmoid(z)
    return z * s, s * (1.0 + z * (1.0 - s))


def _rms(x):
    return lax.rsqrt(jnp.mean(x * x, axis=-1, keepdims=True) + EPS)


def _h_tile(i, x_ref, meta_ref):
    first = jnp.concatenate([jnp.zeros((PAD_ROWS, x_ref.shape[1]), F32), meta_ref[...]], axis=0)
    return jnp.where(i == 0, first, x_ref[...])


def _x_rows(d):
    return pl.BlockSpec((Q_BLOCK, d), lambda i: (jnp.maximum(i - 1, 0), 0))


def _prenorm(x, meta, w):
    seq, d = x.shape
    lp = seq + Q_BLOCK

    def body(x_ref, m_ref, w_ref, o_ref):
        h = _h_tile(pl.program_id(0), x_ref, m_ref)
        o_ref[...] = (h * _rms(h) * w_ref[...]).astype(BF16)

    return pl.pallas_call(
        body, grid=(lp // Q_BLOCK,),
        in_specs=[_x_rows(d), pl.BlockSpec((N_META, d), lambda i: (0, 0)), pl.BlockSpec((1, d), lambda i: (0, 0))],
        out_specs=pl.BlockSpec((Q_BLOCK, d), lambda i: (i, 0)),
        out_shape=jax.ShapeDtypeStruct((lp, d), BF16), name="prenorm", compiler_params=_cp("parallel"))(x, meta, w)


def _prenorm_gathering(x, w, rider, meta_at):
    seq, d = x.shape
    steps = seq // Q_BLOCK + 1

    def body(start, finish, x_ref, w_ref, o_ref, meta_buf, meta_sem):
        i = pl.program_id(0)
        pl.when(i == 0)(start)

        @pl.when(i < steps - 1)
        def _():
            h = x_ref[...]
            o_ref[...] = (h * _rms(h) * w_ref[...]).astype(BF16)

        @pl.when(i == steps - 1)
        def _():
            finish()
            cp = pltpu.make_async_copy(finish.results[0][meta_at], meta_buf, meta_sem)
            cp.start()
            cp.wait()
            h = jnp.concatenate([jnp.zeros((PAD_ROWS, d), F32), jnp.concatenate([meta_buf[s] for s in range(N_DEV)], axis=1)], axis=0)
            o_ref[...] = (h * _rms(h) * w_ref[...]).astype(BF16)

    (xn,), got = _hosted_call(
        body, [rider], 2, 1, 2, grid=(steps,),
        in_specs=[pl.BlockSpec((Q_BLOCK, d), lambda i: (jnp.minimum(i, steps - 2), 0)), pl.BlockSpec((1, d), lambda i: (0, 0))],
        out_specs=[pl.BlockSpec((Q_BLOCK, d), lambda i: ((i + 1) % steps, 0))],
        out_shape=[jax.ShapeDtypeStruct((seq + Q_BLOCK, d), BF16)],
        scratch_shapes=[pltpu.VMEM((N_DEV, N_META, d // N_DEV), F32), pltpu.SemaphoreType.DMA(())],
        name="prenorm", compiler_params=_cp("arbitrary"))(x, w)
    return xn, got[0]


def _tile(n, want):
    return max(t for t in range(LANES, want + 1, LANES) if n % t == 0)


class _Rider:
    def __init__(self, inputs, out_shapes, scratch, aliases, make):
        self.inputs, self.out_shapes, self.scratch, self.aliases, self.make = inputs, out_shapes, scratch, aliases, make


def _hosted_call(body, riders, n_in, n_out, n_scratch, *, in_specs, out_specs, out_shape, scratch_shapes=(), aliases=None,
                 **kw):
    riders = [r for r in riders if r is not None]
    r_in = [len(r.inputs) for r in riders]
    r_out = [len(r.out_shapes) for r in riders]
    r_scr = [len(r.scratch) for r in riders]
    al = dict(aliases or {})
    for k, r in enumerate(riders):
        al.update({n_in + sum(r_in[:k]) + i: n_out + sum(r_out[:k]) + o for i, o in r.aliases.items()})

    def full_body(*refs):
        ins, rest = refs[:n_in + sum(r_in)], refs[n_in + sum(r_in):]
        outs, scr = rest[:n_out + sum(r_out)], rest[n_out + sum(r_out):]
        hooks = [r.make(ins[n_in + sum(r_in[:k]):n_in + sum(r_in[:k + 1])], outs[n_out + sum(r_out[:k]):n_out + sum(r_out[:k + 1])],
                        scr[n_scratch + sum(r_scr[:k]):n_scratch + sum(r_scr[:k + 1])]) for k, r in enumerate(riders)]

        def start():
            for h in hooks:
                h[0]()

        def finish():
            for h in hooks:
                h[1]()

        finish.results = [outs[n_out + sum(r_out[:k]):n_out + sum(r_out[:k + 1])] for k in range(len(riders))]
        body(start, finish, *ins[:n_in], *outs[:n_out], *scr[:n_scratch])

    call = pl.pallas_call(
        full_body, in_specs=list(in_specs) + [_ANY] * sum(r_in), out_specs=list(out_specs) + [_ANY] * sum(r_out),
        out_shape=list(out_shape) + [s for r in riders for s in r.out_shapes],
        scratch_shapes=list(scratch_shapes) + [s for r in riders for s in r.scratch], input_output_aliases=al, **kw)

    def run(*args):
        res = call(*args, *[t for r in riders for t in r.inputs])
        return res[:n_out], [res[n_out + sum(r_out[:k]):n_out + sum(r_out[:k + 1])] for k in range(len(riders))]

    return run


def _matmul(a, b, dims, tn, out_dtype, name, rider=None, a_cols=None, b_rows=None, acc=None):
    a_shape = a.shape if a_cols is None else (a.shape[0], a_cols[0])
    a_index = 0 if a_cols is None else a_cols[1]
    m = a_shape[1] if dims == "tn" else a_shape[0]
    n = b.shape[0] if dims == "nt" else b.shape[1]
    kdim = b.shape[1] if dims == "nt" else b.shape[0] if b_rows is None else b_rows[0]
    b_index = 0 if b_rows is None else b_rows[1]
    tn = _tile(n, tn)
    steps = n // tn
    b_spec = pl.BlockSpec((tn, kdim), lambda j: (j, 0)) if dims == "nt" else pl.BlockSpec((kdim, tn), lambda j: (b_index, j))
    o_spec = pl.BlockSpec((m, tn), lambda j: (0, j))

    def body(start, finish, a_ref, b_ref, *refs):
        pl.when(pl.program_id(0) == 0)(start)
        prod = _dot(a_ref[...], b_ref[...], dims)
        refs[-1][...] = (prod if acc is None else prod + refs[0][...]).astype(out_dtype)
        pl.when(pl.program_id(0) == steps - 1)(finish)

    (out,), got = _hosted_call(
        body, [rider], 2 + (acc is not None), 1, 0, grid=(steps,),
        in_specs=[pl.BlockSpec(a_shape, lambda j: (0, a_index)), b_spec] + [o_spec] * (acc is not None),
        out_specs=[o_spec], out_shape=[jax.ShapeDtypeStruct((m, n), out_dtype)], aliases={2: 0} if acc is not None else None,
        name=name, compiler_params=_cp("parallel" if rider is None else "arbitrary"))(a, b, *([acc] if acc is not None else []))
    return out if rider is None else (out, got[0])


def _chip_rider(sums):
    def make(ins, outs, scratch):
        local, remote = _chip_exchange_copies(ins[0], outs[0], *scratch)

        def start():
            for cp in [local] + remote:
                cp.start()

        def finish():
            local.wait()
            for cp in remote:
                cp.wait_send()
                cp.wait_recv()

        return start, finish

    return _Rider([sums], [jax.ShapeDtypeStruct(sums.shape, sums.dtype)],
                  [pltpu.SemaphoreType.DMA((N_CHIP - 1,)), pltpu.SemaphoreType.DMA((N_CHIP - 1,)), pltpu.SemaphoreType.DMA((1,))],
                  {}, make)


_HBM = pl.BlockSpec(memory_space=pltpu.HBM)
_SEM = pl.BlockSpec(memory_space=pltpu.SEMAPHORE)


def _chip_exchange_copies(sums_ref, land_ref, send_sems, recv_sems, local_sem):
    x, y, core = lax.axis_index("x"), lax.axis_index("y"), lax.axis_index("c")
    mine = 2 * x + y
    local = pltpu.make_async_copy(sums_ref.at[mine], land_ref.at[mine], local_sem.at[0])
    remote = []
    for k in range(1, N_CHIP):
        px = 1 - x if k & 2 else x
        py = 1 - y if k & 1 else y
        remote.append(pltpu.make_async_remote_copy(
            src_ref=sums_ref.at[2 * px + py], dst_ref=land_ref.at[mine], send_sem=send_sems.at[k - 1],
            recv_sem=recv_sems.at[k - 1], device_id=(px, py, core), device_id_type=_MESH))
    return local, remote


def _chip_exchange_start(sums, name):
    def body(s_ref, send_sems, recv_sems, local_sem, s_thru, land_ref, token):
        local, remote = _chip_exchange_copies(s_ref, land_ref, send_sems, recv_sems, local_sem)
        for cp in [local] + remote:
            cp.start()
        token[...] = jnp.zeros_like(token)

    *flight, token = pl.pallas_call(
        body, name=name,
        out_shape=(pltpu.SemaphoreType.DMA((N_CHIP - 1,)), pltpu.SemaphoreType.DMA((N_CHIP - 1,)), pltpu.SemaphoreType.DMA((1,)),
                   pltpu.HBM(sums.shape, sums.dtype), pltpu.HBM(sums.shape, sums.dtype), jax.ShapeDtypeStruct((8, LANES), F32)),
        in_specs=(_HBM,), out_specs=(_SEM, _SEM, _SEM, _HBM, _HBM, pl.BlockSpec(memory_space=pltpu.VMEM)),
        input_output_aliases={0: 3},
        compiler_params=pltpu.CompilerParams(has_side_effects=pltpu.SideEffectType.DATAFLOW_SIDE_EFFECTING))(
            pltpu.with_memory_space_constraint(sums, pltpu.HBM))
    return flight, token


def _chip_exchange_wait(flights, after, name):
    n = len(flights)

    def body(*refs):
        for i in range(n):
            s_ref, land_ref = refs[2 * i:2 * i + 2]
            local, remote = _chip_exchange_copies(s_ref, land_ref, *refs[2 * n + 3 * i:2 * n + 3 * i + 3])
            local.wait()
            for cp in remote:
                cp.wait_send()
                cp.wait_recv()

    buffers = [b for f in flights for b in f[3:]]
    res = pl.pallas_call(
        body, name=name, out_shape=tuple(pltpu.HBM(b.shape, b.dtype) for b in buffers),
        in_specs=(_HBM,) * (2 * n) + (_SEM,) * (3 * n) + (_ANY,), out_specs=(_HBM,) * (2 * n),
        input_output_aliases={i: i for i in range(2 * n)},
        compiler_params=pltpu.CompilerParams(has_side_effects=pltpu.SideEffectType.DATAFLOW_SIDE_EFFECTING))(
            *buffers, *[s for f in flights for s in f[:3]], after)
    return res[1::2]


def _dxn(dproj, wfull, riders, tk, name):
    m, k = dproj.shape
    n = wfull.shape[0]
    tk = _tile(k, tk)
    steps = k // tk

    def body(start, finish, a_ref, b_ref, o_ref):
        j = pl.program_id(0)

        @pl.when(j == 0)
        def _():
            start()
            o_ref[...] = jnp.zeros_like(o_ref)
        o_ref[...] += _dot(a_ref[...], b_ref[...], "nt")
        pl.when(j == steps - 1)(finish)

    (dxn,), got = _hosted_call(
        body, riders, 2, 1, 0, grid=(steps,),
        in_specs=[pl.BlockSpec((m, tk), lambda j: (0, j)), pl.BlockSpec((n, tk), lambda j: (0, j))],
        out_specs=[pl.BlockSpec((m, n), lambda j: (0, 0))], out_shape=[jax.ShapeDtypeStruct((m, n), F32)],
        name=name, compiler_params=_cp("arbitrary"))(dproj, wfull)
    return dxn, got


def _conv_taps(x, w):
    c = x * w[CONV_WIDTH - 1:CONV_WIDTH, :]
    for j in range(CONV_WIDTH - 1):
        c = c + pltpu.roll(x, CONV_WIDTH - 1 - j, 0) * w[j:j + 1, :]
    return c


def _gdn_prep(proj, conv_wt, nh):
    lp = proj.shape[0]
    scale = HEAD_DIM ** -0.5

    def body(x_ref, w_ref, o_ref):
        which = pl.program_id(0) // nh
        c = _conv_taps(x_ref[...], w_ref[...])
        s = c * _sigmoid(c)
        r = lax.rsqrt(jnp.sum(s * s, axis=-1, keepdims=True) + EPS)
        f = jnp.where(which == 0, r * scale, jnp.where(which == 1, r, 1.0))
        o_ref[...] = jnp.where(_iota(s.shape, 0) >= PAD_ROWS, s * f, 0.0)

    return pl.pallas_call(
        body, grid=(3 * nh,),
        in_specs=[pl.BlockSpec((lp, LANES), lambda s: (0, s)), pl.BlockSpec((CONV_WIDTH, LANES), lambda s: (0, s))],
        out_specs=pl.BlockSpec((lp, LANES), lambda s: (0, s)),
        out_shape=jax.ShapeDtypeStruct((lp, 3 * nh * HEAD_DIM), F32), name="gdn_prep",
        compiler_params=_cp("parallel"))(proj, conv_wt)


def _gdn_prep_bwd(proj, conv_wt, dq, dk, dv, dproj, nh):
    lp = proj.shape[0]
    scale = HEAD_DIM ** -0.5
    part = lambda p: pl.BlockSpec((lp, LANES), lambda s: (0, jnp.clip(s - p * nh, 0, nh - 1)))

    def body(x_ref, w_ref, dq_ref, dk_ref, dv_ref, _, dx_ref, dw_ref):
        which = pl.program_id(0) // nh
        x = x_ref[...]
        w = w_ref[...]
        c = _conv_taps(x, w)
        sg = _sigmoid(c)
        s = c * sg
        r = lax.rsqrt(jnp.sum(s * s, axis=-1, keepdims=True) + EPS)
        dy = jnp.where(which == 0, dq_ref[...], jnp.where(which == 1, dk_ref[...], dv_ref[...]))
        dy = jnp.where(_iota(s.shape, 0) >= PAD_ROWS, dy, 0.0)
        y0 = s * r
        dy0 = dy * jnp.where(which == 0, scale, 1.0)
        ds_n = r * (dy0 - y0 * jnp.sum(dy0 * y0, axis=-1, keepdims=True))
        ds = jnp.where(which == 2, dy, ds_n)
        dc = ds * (sg * (1.0 + c * (1.0 - sg)))
        dx = dc * w[CONV_WIDTH - 1:CONV_WIDTH, :]
        rows = [jnp.sum(dc * x, axis=0, keepdims=True)]
        for j in range(CONV_WIDTH - 2, -1, -1):
            sh = CONV_WIDTH - 1 - j
            dx = dx + pltpu.roll(dc, lp - sh, 0) * w[j:j + 1, :]
            rows.insert(0, jnp.sum(dc * pltpu.roll(x, sh, 0), axis=0, keepdims=True))
        dx_ref[...] = dx.astype(BF16)
        dw_ref[...] = jnp.concatenate(rows, axis=0)

    strip = pl.BlockSpec((lp, LANES), lambda s: (0, s))
    taps = pl.BlockSpec((CONV_WIDTH, LANES), lambda s: (0, s))
    return pl.pallas_call(
        body, grid=(3 * nh,), in_specs=[strip, taps, part(0), part(1), part(2), _ANY], out_specs=[strip, taps],
        out_shape=[jax.ShapeDtypeStruct(dproj.shape, BF16), jax.ShapeDtypeStruct((CONV_WIDTH, 3 * nh * HEAD_DIM), F32)],
        input_output_aliases={5: 0}, name="gdn_prep_bwd", compiler_params=_cp("parallel"))(proj, conv_wt, dq, dk, dv, dproj)


def _gates(proj, bias_row, nega_row, nh):
    lp = proj.shape[0]
    nc = lp // CHUNK

    def body(p_ref, b_ref, a_ref, g_ref, gt3_ref, gtf_ref):
        lane = _iota((CHUNK, LANES), 1)
        tri = (_iota((CHUNK, CHUNK), 0) >= _iota((CHUNK, CHUNK), 1)).astype(F32)

        def step(n, carry):
            r0 = pl.multiple_of(n * CHUNK, CHUNK)
            z = p_ref[pl.ds(r0, CHUNK), :] + b_ref[...]
            base = jnp.where(lane < nh, _sigmoid(z),
                             jnp.where(lane < 2 * nh, a_ref[...] * _softplus(z),
                                       jnp.where(lane < 3 * nh, -_softplus(-z), 0.0)))
            base = jnp.where(r0 + _iota((CHUNK, LANES), 0) >= PAD_ROWS, base, 0.0)
            cs = _hdot(tri, base)
            run = jnp.where((lane >= 2 * nh) & (lane < 3 * nh), cs + carry, cs)
            sh = pltpu.roll(run, 2 * nh, 1)
            out = base + jnp.where((lane >= 3 * nh) & (lane < 5 * nh), sh, 0.0)
            g_ref[pl.ds(r0, CHUNK), :] = out
            gt3_ref[n] = out.T
            return carry + cs[CHUNK - 1:CHUNK, :]

        lax.fori_loop(0, nc, step, jnp.zeros((1, LANES), F32))
        gtf_ref[...] = g_ref[...].T

    vec = pl.BlockSpec((1, LANES), lambda i: (0, 0))
    return pl.pallas_call(
        body, grid=(1,), in_specs=[pl.BlockSpec((lp, LANES), lambda i: (0, 8 * nh)), vec, vec],
        out_specs=[pl.BlockSpec((lp, LANES), lambda i: (0, 0)), pl.BlockSpec((nc, LANES, CHUNK), lambda i: (0, 0, 0)),
                   pl.BlockSpec((LANES, lp), lambda i: (0, 0))],
        out_shape=[jax.ShapeDtypeStruct((lp, LANES), F32), jax.ShapeDtypeStruct((nc, LANES, CHUNK), F32),
                   jax.ShapeDtypeStruct((LANES, lp), F32)],
        name="gates", compiler_params=_cp("arbitrary"))(proj, bias_row, nega_row)


def _gates_bwd(proj, bias_row, nega_row, gates, dgate_gdn, dc_t, dproj, nh):
    lp = proj.shape[0]
    nc = lp // CHUNK

    def body(p_ref, b_ref, a_ref, g_ref, dg_ref, dc_ref, _, dz_ref, sm_ref, dct_scr):
        lane = _iota((CHUNK, LANES), 1)
        triu = (_iota((CHUNK, CHUNK), 0) <= _iota((CHUNK, CHUNK), 1)).astype(F32)
        dct_scr[...] = dc_ref[...].T
        sm_ref[...] = jnp.zeros_like(sm_ref)
        dz_ref[:, LANES:] = jnp.zeros((lp, NARROW - LANES), BF16)

        def step(i, carry):
            n = nc - 1 - i
            r0 = pl.multiple_of(n * CHUNK, CHUNK)
            z = p_ref[pl.ds(r0, CHUNK), :] + b_ref[...]
            gt = g_ref[pl.ds(r0, CHUNK), :]
            dgd = dg_ref[pl.ds(r0, CHUNK), :]
            dch = dct_scr[pl.ds(r0, CHUNK), :]
            rc = _hdot(triu, dch) + carry
            sg = _sigmoid(z)
            dz = jnp.where(lane < nh, dgd * sg * (1.0 - sg),
                           jnp.where(lane < 2 * nh, dgd * a_ref[...] * sg,
                                     jnp.where(lane < 3 * nh, rc * (1.0 - sg), 0.0)))
            dz = jnp.where(r0 + _iota((CHUNK, LANES), 0) >= PAD_ROWS, dz, 0.0)
            dz_ref[pl.ds(r0, CHUNK), 0:LANES] = dz.astype(BF16)
            sm_ref[0:1, :] += jnp.sum(dz, axis=0, keepdims=True)
            sm_ref[1:2, :] += jnp.sum(jnp.where((lane >= nh) & (lane < 2 * nh), dgd * gt, 0.0), axis=0, keepdims=True)
            return carry + jnp.sum(dch, axis=0, keepdims=True)

        lax.fori_loop(0, nc, step, jnp.zeros((1, LANES), F32))

    vec = pl.BlockSpec((1, LANES), lambda i: (0, 0))
    full = pl.BlockSpec((lp, LANES), lambda i: (0, 0))
    last = pl.BlockSpec((lp, LANES), lambda i: (0, 8 * nh))
    tail = pl.BlockSpec((lp, NARROW), lambda i: (0, 8 * nh * LANES // NARROW))
    return pl.pallas_call(
        body, grid=(1,), in_specs=[last, vec, vec, full, full, pl.BlockSpec((LANES, lp), lambda i: (0, 0)), _ANY],
        out_specs=[tail, pl.BlockSpec((8, LANES), lambda i: (0, 0))],
        out_shape=[jax.ShapeDtypeStruct(dproj.shape, BF16), jax.ShapeDtypeStruct((8, LANES), F32)],
        scratch_shapes=[pltpu.VMEM((lp, LANES), F32)], input_output_aliases={6: 0},
        name="gates_bwd", compiler_params=_cp("arbitrary"))(proj, bias_row, nega_row, gates, dgate_gdn, dc_t, dproj)


def _tri_inv(a):
    t = jnp.where(_iota(a.shape, 1) == _iota(a.shape, 2), 1.0, 0.0) - a
    p = a
    for _ in range(CHUNK.bit_length() - 2):
        ph, pw = _split(p)
        p = _dot3(ph, ph) + (_dot3(ph, pw) + _dot3(pw, ph))
        ph, pw = _split(p)
        th, tw = _split(t)
        t = t + (_dot3(th, ph) + (_dot3(th, pw) + _dot3(tw, ph)))
    return t


def _gdn_chunk(q, k, v, beta, gc, gr, t=None):
    ii, jj = _iota((1, CHUNK, CHUNK), 1), _iota((1, CHUNK, CHUNK), 2)
    causal, strict = ii >= jj, ii > jj
    dm = jnp.where(causal, jnp.exp(jnp.where(causal, gc - gr, 0.0)), 0.0)
    kk = _bdot3(k, k, "nt")
    a = jnp.where(strict, beta * kk * dm, 0.0)
    if t is None:
        t = _tri_inv(a)
    eg = jnp.exp(gc)
    glast = gc[:, CHUNK - 1:CHUNK, :]
    ekd = jnp.exp(glast - gc)
    bv = beta * v
    bk = (beta * eg) * k
    ub = _bdot3(t, jnp.concatenate([bv, bk], axis=2))
    qk = _bdot3(q, k, "nt")
    return dict(causal=causal, strict=strict, dm=dm, kk=kk, a=a, t=t, eg=eg, ekd=ekd, bv=bv, bk=bk,
                u=ub[:, :, :HEAD_DIM], w=ub[:, :, HEAD_DIM:], qk=qk, aqk=jnp.where(causal, qk * dm, 0.0),
                q_dec=q * eg, k_dec=k * ekd, decay=jnp.exp(glast))


def _heads(ref, nh):
    return jnp.stack([ref[:, h * HEAD_DIM:(h + 1) * HEAD_DIM] for h in range(nh)], axis=0)


def _gdn_chunk_inputs(q_ref, k_ref, v_ref, g, gt, nh):
    col = lambda o: jnp.stack([g[:, o + h:o + h + 1] for h in range(nh)], axis=0)
    gr = jnp.stack([gt[3 * nh + h:3 * nh + h + 1, :] for h in range(nh)], axis=0)
    return _heads(q_ref, nh), _heads(k_ref, nh), _heads(v_ref, nh), col(0), col(3 * nh), gr


def _gdn_fwd(qkv, gates, gt3, nh, rider=None):
    lp = qkv.shape[0]
    nc = lp // CHUNK
    w = nh * HEAD_DIM

    def body(start, finish, q_ref, k_ref, v_ref, g_ref, gt_ref, o_ref, sall_ref, tall_ref, s_scr):
        @pl.when(pl.program_id(0) == 0)
        def _():
            start()
            s_scr[...] = jnp.zeros_like(s_scr)
        c = _gdn_chunk(*_gdn_chunk_inputs(q_ref, k_ref, v_ref, g_ref[...], gt_ref[0], nh))
        s = s_scr[...]
        sall_ref[0] = s
        tall_ref[0] = c["t"]
        v_new = c["u"] - _bdot3(c["w"], s)
        o = _bdot3(c["q_dec"], s) + _bdot3(c["aqk"], v_new)
        s_scr[...] = s * c["decay"] + _bdot3(c["k_dec"], v_new, "tn")
        for h in range(nh):
            o_ref[:, h * HEAD_DIM:(h + 1) * HEAD_DIM] = o[h]
        pl.when(pl.program_id(0) == nc - 1)(finish)

    outs, got = _hosted_call(
        body, [rider], 5, 3, 1, grid=(nc,),
        in_specs=[pl.BlockSpec((CHUNK, w), lambda n: (n, 0)), pl.BlockSpec((CHUNK, w), lambda n: (n, 1)),
                  pl.BlockSpec((CHUNK, w), lambda n: (n, 2)), pl.BlockSpec((CHUNK, LANES), lambda n: (n, 0)),
                  pl.BlockSpec((1, LANES, CHUNK), lambda n: (n, 0, 0))],
        out_specs=[pl.BlockSpec((CHUNK, w), lambda n: (n, 0)),
                   pl.BlockSpec((1, nh, HEAD_DIM, HEAD_DIM), lambda n: (n, 0, 0, 0)),
                   pl.BlockSpec((1, nh, CHUNK, CHUNK), lambda n: (n, 0, 0, 0))],
        out_shape=[jax.ShapeDtypeStruct((lp, w), F32), jax.ShapeDtypeStruct((nc, nh, HEAD_DIM, HEAD_DIM), F32),
                   jax.ShapeDtypeStruct((nc, nh, CHUNK, CHUNK), F32)],
        scratch_shapes=[pltpu.VMEM((nh, HEAD_DIM, HEAD_DIM), F32)],
        name="gdn_fwd", compiler_params=_cp("arbitrary"))(qkv, qkv, qkv, gates, gt3)
    return outs, (got[0] if got else None)


def _gdn_bwd(qkv, gates, gt3, s_all, t_all, do, nh, rider=None):
    lp = qkv.shape[0]
    nc = lp // CHUNK
    w = nh * HEAD_DIM
    rev = lambda n: nc - 1 - n

    def body(start, finish, q_ref, k_ref, v_ref, g_ref, gt_ref, s_ref, t_ref, do_ref, dq_ref, dk_ref, dv_ref, dg_ref, ds_scr):
        @pl.when(pl.program_id(0) == 0)
        def _():
            start()
            ds_scr[...] = jnp.zeros_like(ds_scr)
        q, k, v, beta, gc, gr = _gdn_chunk_inputs(q_ref, k_ref, v_ref, g_ref[...], gt_ref[0], nh)
        c = _gdn_chunk(q, k, v, beta, gc, gr, t_ref[0])
        s = s_ref[0]
        dsn = ds_scr[...]
        dout = _heads(do_ref, nh)
        v_new = c["u"] - _bdot3(c["w"], s)
        dq_dec = _bdot3(dout, s, "nt")
        daqk = jnp.where(c["causal"], _bdot3(dout, v_new, "nt"), 0.0)
        dv_new = _bdot3(c["aqk"], dout, "tn") + _bdot3(c["k_dec"], dsn)
        dk_dec = _bdot3(v_new, dsn, "nt")
        ddecay = jnp.sum(jnp.sum(dsn * s, axis=2, keepdims=True), axis=1, keepdims=True)
        dw = -_bdot3(dv_new, s, "nt")
        ds_scr[...] = _bdot3(c["q_dec"], dout, "tn") + c["decay"] * dsn - _bdot3(c["w"], dv_new, "tn")
        duw = jnp.concatenate([dv_new, dw], axis=2)
        dt = _bdot3(duw, jnp.concatenate([c["bv"], c["bk"]], axis=2), "nt")
        dbvk = _bdot3(c["t"], duw, "tn")
        dbv, dbk = dbvk[:, :, :HEAD_DIM], dbvk[:, :, HEAD_DIM:]
        da = jnp.where(c["strict"], -_bdot3(_bdot3(c["t"], dt, "tn"), c["t"], "nt"), 0.0)
        dkk = da * beta * c["dm"]
        dqk = daqk * c["dm"]
        e = da * c["a"] + daqk * c["aqk"]
        dq = dq_dec * c["eg"] + _bdot3(dqk, k)
        dk = (dk_dec * c["ekd"] + _bdot3(dkk, k) + _bdot3(dkk, k, "tn") + _bdot3(dqk, q, "tn")
              + (beta * c["eg"]) * dbk)
        dv = beta * dbv
        rs = lambda x: jnp.sum(x, axis=2, keepdims=True)
        dbeta = rs(dbv * v) + c["eg"] * rs(dbk * k) + rs(da * c["kk"] * c["dm"])
        kd_term = rs(dk_dec * c["k_dec"])
        eh, ew = _split(e)
        ones = jnp.ones((nh, CHUNK, LANES), BF16)
        col_sums = (_dot3(eh, ones, "tn") + _dot3(ew, ones, "tn"))[:, :, 0:1]
        dg_cum = rs(dq_dec * c["q_dec"]) - kd_term + rs(dbk * c["bk"]) + rs(e) - col_sums
        last = jnp.sum(kd_term, axis=1, keepdims=True) + ddecay * c["decay"]
        dg_cum = dg_cum + jnp.where(_iota((1, CHUNK, 1), 1) == CHUNK - 1, last, 0.0)
        lane = _iota((CHUNK, LANES), 1)
        acc = jnp.zeros((CHUNK, LANES), F32)
        for h in range(nh):
            sl = slice(h * HEAD_DIM, (h + 1) * HEAD_DIM)
            dq_ref[:, sl] = dq[h]
            dk_ref[:, sl] = dk[h]
            dv_ref[:, sl] = dv[h]
            acc = acc + jnp.where(lane == h, dbeta[h], 0.0) + jnp.where(lane == nh + h, dg_cum[h], 0.0)
        triu = (_iota((CHUNK, CHUNK), 0) <= _iota((CHUNK, CHUNK), 1)).astype(F32)
        dg_ref[...] = jnp.where(lane < nh, acc, _hdot(triu, acc))
        pl.when(pl.program_id(0) == nc - 1)(finish)

    outs, got = _hosted_call(
        body, [rider], 8, 4, 1, grid=(nc,),
        in_specs=[pl.BlockSpec((CHUNK, w), lambda n: (rev(n), 0)), pl.BlockSpec((CHUNK, w), lambda n: (rev(n), 1)),
                  pl.BlockSpec((CHUNK, w), lambda n: (rev(n), 2)), pl.BlockSpec((CHUNK, LANES), lambda n: (rev(n), 0)),
                  pl.BlockSpec((1, LANES, CHUNK), lambda n: (rev(n), 0, 0)),
                  pl.BlockSpec((1, nh, HEAD_DIM, HEAD_DIM), lambda n: (rev(n), 0, 0, 0)),
                  pl.BlockSpec((1, nh, CHUNK, CHUNK), lambda n: (rev(n), 0, 0, 0)),
                  pl.BlockSpec((CHUNK, w), lambda n: (rev(n), 0))],
        out_specs=[pl.BlockSpec((CHUNK, w), lambda n: (rev(n), 0))] * 3 + [pl.BlockSpec((CHUNK, LANES), lambda n: (rev(n), 0))],
        out_shape=[jax.ShapeDtypeStruct((lp, w), F32)] * 3 + [jax.ShapeDtypeStruct((lp, LANES), F32)],
        scratch_shapes=[pltpu.VMEM((nh, HEAD_DIM, HEAD_DIM), F32)],
        name="gdn_bwd", compiler_params=_cp("arbitrary"))(qkv, qkv, qkv, gates, gt3, s_all, t_all, do)
    return outs, (got[0] if got else None)


def _merge_gdn(o_gdn, proj, norm_w, nh):
    lp = o_gdn.shape[0]

    def body(o_ref, z_ref, w_ref, m_ref):
        o = o_ref[...]
        z = z_ref[...]
        m_ref[...] = (o * _rms(o) * w_ref[...] * (z * _sigmoid(z))).astype(BF16)

    return pl.pallas_call(
        body, grid=(nh,),
        in_specs=[pl.BlockSpec((lp, LANES), lambda s: (0, s)), pl.BlockSpec((lp, LANES), lambda s: (0, 3 * nh + s)),
                  pl.BlockSpec((1, LANES), lambda s: (0, 0))],
        out_specs=pl.BlockSpec((lp, LANES), lambda s: (0, s)),
        out_shape=jax.ShapeDtypeStruct((lp, 2 * nh * HEAD_DIM), BF16), name="merge_gdn",
        compiler_params=_cp("parallel"))(o_gdn, proj, norm_w)


def _merge_gdn_bwd(o_gdn, proj, norm_w, dmerged, nh):
    lp = o_gdn.shape[0]

    def body(o_ref, z_ref, w_ref, dm_ref, do_ref, dz_ref, dw_ref):
        o = o_ref[...]
        r = _rms(o)
        xh = o * r
        silu, dsilu = _silu_and_grad(z_ref[...])
        dm = dm_ref[...]
        dn = dm * silu
        dz_ref[...] = (dm * (xh * w_ref[...]) * dsilu).astype(BF16)
        dnw = dn * w_ref[...]
        do_ref[...] = r * (dnw - xh * jnp.mean(dnw * xh, axis=-1, keepdims=True))

        @pl.when(pl.program_id(0) == 0)
        def _():
            dw_ref[...] = jnp.zeros_like(dw_ref)
        dw_ref[...] += jnp.sum(dn * xh, axis=0, keepdims=True)

    w = nh * HEAD_DIM
    return pl.pallas_call(
        body, grid=(nh,),
        in_specs=[pl.BlockSpec((lp, LANES), lambda s: (0, s)), pl.BlockSpec((lp, LANES), lambda s: (0, 3 * nh + s)),
                  pl.BlockSpec((1, LANES), lambda s: (0, 0)), pl.BlockSpec((lp, LANES), lambda s: (0, s))],
        out_specs=[pl.BlockSpec((lp, LANES), lambda s: (0, s)), pl.BlockSpec((lp, LANES), lambda s: (0, 3 * nh + s)),
                   pl.BlockSpec((1, LANES), lambda s: (0, 0))],
        out_shape=[jax.ShapeDtypeStruct((lp, w), F32), jax.ShapeDtypeStruct((lp, 8 * w + NARROW), BF16),
                   jax.ShapeDtypeStruct((1, LANES), F32)],
        name="merge_gdn_bwd", compiler_params=_cp("arbitrary"))(o_gdn, proj, norm_w, dmerged)


def _fox_prep(proj, qk_w, nh):
    lp = proj.shape[0]

    def body(x_ref, w_ref, o_ref):
        x = x_ref[...]
        o_ref[...] = x * _rms(x) * w_ref[0]

    return pl.pallas_call(
        body, grid=(2 * nh,),
        in_specs=[pl.BlockSpec((lp, LANES), lambda s: (0, 4 * nh + s)), pl.BlockSpec((1, 1, LANES), lambda s: (s // nh, 0, 0))],
        out_specs=pl.BlockSpec((lp, LANES), lambda s: (0, s)),
        out_shape=jax.ShapeDtypeStruct((lp, 2 * nh * HEAD_DIM), F32), name="fox_prep",
        compiler_params=_cp("parallel"))(proj, qk_w)


def _fox_prep_bwd(proj, qk_w, dq, dk, dproj, nh):
    lp = proj.shape[0]
    part = lambda p: pl.BlockSpec((lp, LANES), lambda s: (0, jnp.clip(s - p * nh, 0, nh - 1)))

    def body(x_ref, w_ref, dq_ref, dk_ref, _, dx_ref, dw_ref):
        x = x_ref[...]
        r = _rms(x)
        xh = x * r
        dy = jnp.where(pl.program_id(0) < nh, dq_ref[...], dk_ref[...])
        dyw = dy * w_ref[0]
        dx_ref[...] = (r * (dyw - xh * jnp.mean(dyw * xh, axis=-1, keepdims=True))).astype(BF16)

        @pl.when(pl.program_id(0) % nh == 0)
        def _():
            dw_ref[...] = jnp.zeros_like(dw_ref)
        dw_ref[0] += jnp.sum(dy * xh, axis=0, keepdims=True)

    strip = pl.BlockSpec((lp, LANES), lambda s: (0, 4 * nh + s))
    wsp = pl.BlockSpec((1, 1, LANES), lambda s: (s // nh, 0, 0))
    return pl.pallas_call(
        body, grid=(2 * nh,), in_specs=[strip, wsp, part(0), part(1), _ANY], out_specs=[strip, wsp],
        out_shape=[jax.ShapeDtypeStruct(dproj.shape, BF16), jax.ShapeDtypeStruct((2, 1, LANES), F32)],
        input_output_aliases={4: 0}, name="fox_prep_bwd", compiler_params=_cp("arbitrary"))(proj, qk_w, dq, dk, dproj)


def _fox_probs(q, k, gates, crow, h, i, nh, lse=None):
    kl = k.shape[0]
    lane = _iota((Q_BLOCK, LANES), 1)
    ct = jnp.sum(jnp.where(lane == 4 * nh + h, gates, 0.0), axis=1, keepdims=True)
    tq, kq = _iota((Q_BLOCK, Q_BLOCK), 0), _iota((Q_BLOCK, Q_BLOCK), 1)
    qs = q * (HEAD_DIM ** -0.5)
    if i == 0:
        s = _bdot(qs, k, "nt") + (ct - crow)
        s = jnp.where((kq <= tq) & ((kq >= PAD_ROWS) | (tq < PAD_ROWS)), s, NEG)
    else:
        crow = jnp.where(_iota((1, kl), 1) < PAD_ROWS, -NEG, crow)
        s = _bdot(qs, k, "nt") + (ct - crow)
        s = jnp.concatenate([s[:, :kl - Q_BLOCK], jnp.where(kq <= tq, s[:, kl - Q_BLOCK:], NEG)], axis=1)
    if lse is not None:
        return jnp.exp(s - lse)
    m = jnp.max(s, axis=1, keepdims=True)
    p = jnp.exp(s - m)
    tot = jnp.sum(p, axis=1, keepdims=True)
    return p / tot, m + jnp.log(tot)


FOX_HEADS_PER_STEP = 2


def _fox_specs(lp, nh):
    hw = FOX_HEADS_PER_STEP * LANES
    return [pl.BlockSpec((Q_BLOCK, hw), lambda g, i: (i, g)),
            pl.BlockSpec((lp, hw), lambda g, i: (0, nh // FOX_HEADS_PER_STEP + g)),
            pl.BlockSpec((lp, hw), lambda g, i: (0, 6 * nh // FOX_HEADS_PER_STEP + g)),
            pl.BlockSpec((Q_BLOCK, LANES), lambda g, i: (i, 0)),
            pl.BlockSpec((LANES, lp), lambda g, i: (0, 0))]


def _fox_fwd(qkn, proj, gates, gtf, nh):
    lp = qkn.shape[0]

    def body(q_ref, k_ref, v_ref, g_ref, gt_ref, o_ref, lse_ref):
        g, i = pl.program_id(0), pl.program_id(1)
        for j in range(lp // Q_BLOCK):
            @pl.when(i == j)
            def _(j=j):
                kl = (j + 1) * Q_BLOCK
                for hh in range(FOX_HEADS_PER_STEP):
                    h = FOX_HEADS_PER_STEP * g + hh
                    sl = slice(hh * LANES, (hh + 1) * LANES)
                    p, lse = _fox_probs(q_ref[:, sl], k_ref[0:kl, sl], g_ref[...], gt_ref[pl.ds(4 * nh + h, 1), :][:, 0:kl],
                                        h, j, nh)
                    o_ref[:, sl] = _bdot(p, v_ref[0:kl, sl])
                    lse_ref[:, sl] = jnp.broadcast_to(lse, (Q_BLOCK, LANES))

    blk = pl.BlockSpec((Q_BLOCK, FOX_HEADS_PER_STEP * LANES), lambda g, i: (i, g))
    return pl.pallas_call(
        body, grid=(nh // FOX_HEADS_PER_STEP, lp // Q_BLOCK), in_specs=_fox_specs(lp, nh), out_specs=[blk, blk],
        out_shape=[jax.ShapeDtypeStruct((lp, nh * HEAD_DIM), F32)] * 2, name="fox_fwd",
        compiler_params=_cp("parallel", "parallel"))(qkn, qkn, proj, gates, gtf)


def _fox_bwd(qkn, proj, gates, gtf, lse, do, dproj, nh):
    lp = qkn.shape[0]
    nq = lp // Q_BLOCK
    w = nh * HEAD_DIM
    scale = HEAD_DIM ** -0.5

    def body(q_ref, k_ref, v_ref, g_ref, gt_ref, lse_ref, do_ref, _, dq_ref, dk_ref, dc_ref, dv_ref, dv_scr):
        g, i = pl.program_id(0), pl.program_id(1)

        @pl.when(i == 0)
        def _():
            dk_ref[...] = jnp.zeros_like(dk_ref)
            dv_scr[...] = jnp.zeros_like(dv_scr)
            dc_ref[...] = jnp.zeros_like(dc_ref)
        for j in range(nq):
            @pl.when(i == j)
            def _(j=j):
                kl = (j + 1) * Q_BLOCK
                for hh in range(FOX_HEADS_PER_STEP):
                    h = FOX_HEADS_PER_STEP * g + hh
                    sl = slice(hh * LANES, (hh + 1) * LANES)
                    q, k = q_ref[:, sl], k_ref[0:kl, sl]
                    p = _fox_probs(q, k, g_ref[...], gt_ref[pl.ds(4 * nh + h, 1), :][:, 0:kl], h, j, nh,
                                   lse_ref[:, sl][:, 0:1])
                    dout = do_ref[:, sl]
                    dp = _bdot(dout, v_ref[0:kl, sl], "nt")
                    ds = p * (dp - jnp.sum(p * dp, axis=1, keepdims=True))
                    dq_ref[:, sl] = _bdot(ds, k) * scale
                    dk_ref[0:kl, sl] += _bdot(ds, q * scale, "tn")
                    dv_scr[0:kl, sl] += _bdot(p, dout, "tn")
                    dc_ref[hh, :, 0:kl] -= jnp.sum(ds, axis=0, keepdims=True)

        @pl.when(i == nq - 1)
        def _():
            dv_ref[...] = dv_scr[...].astype(BF16)

    hw = FOX_HEADS_PER_STEP * LANES
    blk = pl.BlockSpec((Q_BLOCK, hw), lambda g, i: (i, g))
    col = pl.BlockSpec((lp, hw), lambda g, i: (0, g))
    return pl.pallas_call(
        body, grid=(nh // FOX_HEADS_PER_STEP, nq), in_specs=_fox_specs(lp, nh) + [blk, blk, _ANY],
        out_specs=[blk, col, pl.BlockSpec((FOX_HEADS_PER_STEP, 1, lp), lambda g, i: (g, 0, 0)),
                   pl.BlockSpec((lp, hw), lambda g, i: (0, 6 * nh // FOX_HEADS_PER_STEP + g))],
        out_shape=[jax.ShapeDtypeStruct((lp, w), F32)] * 2 + [jax.ShapeDtypeStruct((nh, 1, lp), F32),
                                                             jax.ShapeDtypeStruct(dproj.shape, BF16)],
        scratch_shapes=[pltpu.VMEM((lp, hw), F32)], input_output_aliases={7: 3},
        name="fox_bwd", compiler_params=_cp("parallel", "arbitrary"))(qkn, qkn, proj, gates, gtf, lse, do, dproj)


def _merge_fox(o_fox, proj, merged, nh):
    lp = o_fox.shape[0]

    def body(o_ref, z_ref, _, m_ref):
        z = z_ref[...]
        m_ref[...] = (o_ref[...] * (z * _sigmoid(z))).astype(BF16)

    return pl.pallas_call(
        body, grid=(nh,),
        in_specs=[pl.BlockSpec((lp, LANES), lambda s: (0, s)), pl.BlockSpec((lp, LANES), lambda s: (0, 7 * nh + s)), _ANY],
        out_specs=pl.BlockSpec((lp, LANES), lambda s: (0, nh + s)),
        out_shape=jax.ShapeDtypeStruct(merged.shape, BF16), input_output_aliases={2: 0}, name="merge_fox",
        compiler_params=_cp("parallel"))(o_fox, proj, merged)


def _merge_fox_bwd(o_fox, proj, dmerged, dproj, nh):
    lp = o_fox.shape[0]

    def body(o_ref, z_ref, dm_ref, _, do_ref, dz_ref):
        silu, dsilu = _silu_and_grad(z_ref[...])
        dm = dm_ref[...]
        do_ref[...] = dm * silu
        dz_ref[...] = (dm * o_ref[...] * dsilu).astype(BF16)

    w = nh * HEAD_DIM
    return pl.pallas_call(
        body, grid=(nh,),
        in_specs=[pl.BlockSpec((lp, LANES), lambda s: (0, s)), pl.BlockSpec((lp, LANES), lambda s: (0, 7 * nh + s)),
                  pl.BlockSpec((lp, LANES), lambda s: (0, nh + s)), _ANY],
        out_specs=[pl.BlockSpec((lp, LANES), lambda s: (0, s)), pl.BlockSpec((lp, LANES), lambda s: (0, 7 * nh + s))],
        out_shape=[jax.ShapeDtypeStruct((lp, w), F32), jax.ShapeDtypeStruct(dproj.shape, BF16)],
        input_output_aliases={3: 1}, name="merge_fox_bwd", compiler_params=_cp("parallel"))(o_fox, proj, dmerged, dproj)


def _post(out, x, target, post_w):
    lp, d = out.shape

    def body(o_ref, x_ref, t_ref, w_ref, dy_ref, do_ref, loss_ref, dw_ref):
        i = pl.program_id(0)

        @pl.when(i == 0)
        def _():
            loss_ref[...] = jnp.zeros_like(loss_ref)
            dw_ref[...] = jnp.zeros_like(dw_ref)
        o = o_ref[...]
        r = _rms(o)
        nrm = o * r
        err = jnp.where(i > 0, x_ref[...] + nrm * w_ref[...] - t_ref[...], 0.0)
        loss_ref[0:1, :] += 0.5 * jnp.sum(jnp.sum(err * err, axis=1, keepdims=True), axis=0, keepdims=True) / d
        dy = err / d
        dy_ref[...] = dy
        dw_ref[...] += jnp.sum(dy * nrm, axis=0, keepdims=True)
        dyw = dy * w_ref[...]
        do_ref[...] = (r * (dyw - nrm * jnp.mean(dyw * nrm, axis=-1, keepdims=True))).astype(BF16)

    row = pl.BlockSpec((Q_BLOCK, d), lambda i: (i, 0))
    vec = pl.BlockSpec((1, d), lambda i: (0, 0))
    return pl.pallas_call(
        body, grid=(lp // Q_BLOCK,), in_specs=[row, _x_rows(d), _x_rows(d), vec],
        out_specs=[_x_rows(d), row, pl.BlockSpec((8, LANES), lambda i: (0, 0)), vec],
        out_shape=[jax.ShapeDtypeStruct(x.shape, F32), jax.ShapeDtypeStruct((lp, d), BF16),
                   jax.ShapeDtypeStruct((8, LANES), F32), jax.ShapeDtypeStruct((1, d), F32)],
        name="post", compiler_params=_cp("arbitrary"))(out, x, target, post_w)


def _prenorm_bwd(dxn, x, meta, w, dy, rider=None):
    seq, d = x.shape
    lp = seq + Q_BLOCK

    def body(start, finish, dx_ref, x_ref, m_ref, w_ref, dy_ref, gx_ref, gm_ref, dw_ref):
        i = pl.program_id(0)
        pl.when(i == 0)(start)
        h = _h_tile(i, x_ref, m_ref)
        r = _rms(h)
        xh = h * r
        dxn_ = dx_ref[...]
        dxw = dxn_ * w_ref[...]
        dh = jnp.where(i > 0, dy_ref[...], 0.0) + r * (dxw - xh * jnp.mean(dxw * xh, axis=-1, keepdims=True))
        gx_ref[...] = dh

        @pl.when(i == 0)
        def _():
            dw_ref[...] = jnp.zeros_like(dw_ref)
            gm_ref[...] = dh[PAD_ROWS:, :]
        dw_ref[...] += jnp.sum(dxn_ * xh, axis=0, keepdims=True)
        pl.when(i == lp // Q_BLOCK - 1)(finish)

    vec = pl.BlockSpec((1, d), lambda i: (0, 0))
    met = pl.BlockSpec((N_META, d), lambda i: (0, 0))
    outs, got = _hosted_call(
        body, [rider], 5, 3, 0, grid=(lp // Q_BLOCK,),
        in_specs=[pl.BlockSpec((Q_BLOCK, d), lambda i: (i, 0)), _x_rows(d), met, vec, _x_rows(d)],
        out_specs=[_x_rows(d), met, vec],
        out_shape=[jax.ShapeDtypeStruct((seq, d), F32), jax.ShapeDtypeStruct((N_META, d), F32),
                   jax.ShapeDtypeStruct((1, d), F32)],
        name="prenorm_bwd", compiler_params=_cp("arbitrary"))(dxn, x, meta, w, dy)
    return outs, (got[0] if got else None)


def _layer_grads(x, target, meta, pre_w, wfull, conv_wt, a_log, dt_bias, gdn_norm_w, fq_w, fk_w, f_bias, w_out, post_w,
                 late_weights=None, w_out_grads=None):
    nh = a_log.shape[1]
    zpad = jnp.zeros((1, LANES - 3 * nh), F32)
    bias_row = jnp.concatenate([jnp.zeros((1, nh), F32), dt_bias, f_bias, zpad], axis=1)
    nega_row = jnp.concatenate([jnp.zeros((1, nh), F32), -jnp.exp(a_log), jnp.zeros((1, nh), F32), zpad], axis=1)
    qk_w = jnp.stack([fq_w, fk_w])

    if isinstance(wfull, tuple):
        rider, meta_at, project = wfull
        xn, got = _prenorm_gathering(x, pre_w, rider, meta_at)
        proj, wfull, conv_wt, meta = project(xn, got)
    else:
        xn = _prenorm(x, meta, pre_w)
        proj = _matmul(xn, wfull, "nn", MM_TILE, F32, "proj")
    qkv = _gdn_prep(proj, conv_wt, nh)
    gates, gt3, gtf = _gates(proj, bias_row, nega_row, nh)
    (o_gdn, s_all, t_all), got = _gdn_fwd(qkv, gates, gt3, nh, None if late_weights is None else late_weights[0])
    if late_weights is not None:
        w_out = late_weights[1](got)
    qkn = _fox_prep(proj, qk_w, nh)
    o_fox, fox_lse = _fox_fwd(qkn, proj, gates, gtf, nh)
    merged = _merge_fox(o_fox, proj, _merge_gdn(o_gdn, proj, gdn_norm_w, nh), nh)
    out = _matmul(merged, w_out, "nn", 4 * LANES, F32, "out_proj")
    dy, dout, loss_blk, dpost_w = _post(out, x, target, post_w)

    dw_out = _matmul(merged, dout, "tn", 4 * LANES, BF16, "dw_out")
    if w_out_grads is None:
        dmerged, gdn_rider = _matmul(dout, w_out, "nt", 4 * LANES, F32, "dmerged"), None
    else:
        dmerged, got = _matmul(dout, w_out, "nt", 4 * LANES, F32, "dmerged", w_out_grads[0](dw_out))
        gdn_rider = w_out_grads[1](dw_out, got)
    do_gdn, dproj, dgdn_norm_w = _merge_gdn_bwd(o_gdn, proj, gdn_norm_w, dmerged, nh)
    do_fox, dproj = _merge_fox_bwd(o_fox, proj, dmerged, dproj, nh)
    dqn, dkn, dc_t, dproj = _fox_bwd(qkn, proj, gates, gtf, fox_lse, do_fox, dproj, nh)
    dproj, dqk_w = _fox_prep_bwd(proj, qk_w, dqn, dkn, dproj, nh)
    (dgq, dgk, dgv, dgate), w_out_parts = _gdn_bwd(qkv, gates, gt3, s_all, t_all, do_gdn, nh, gdn_rider)
    dproj, dconv_wt = _gdn_prep_bwd(proj, conv_wt, dgq, dgk, dgv, dproj, nh)
    dc_rows = jnp.pad(dc_t.reshape(nh, -1), ((2 * nh, LANES - 3 * nh), (0, 0)))
    dproj, gate_sums = _gates_bwd(proj, bias_row, nega_row, gates, dgate, dc_rows, dproj, nh)
    return dict(
        loss=loss_blk[0:1, 0:1], dy=dy, xn=xn, dproj=dproj, post_w=dpost_w,
        conv_wt=dconv_wt, a_log=gate_sums[1:2, nh:2 * nh], dt_bias=gate_sums[0:1, nh:2 * nh],
        gdn_norm_w=dgdn_norm_w, fq_w=dqk_w[0], fk_w=dqk_w[1], f_bias=gate_sums[0:1, 2 * nh:3 * nh], w_out=dw_out,
        w_out_parts=w_out_parts, wfull=wfull, meta=meta)


def _cast_bf16(a, tr, name):
    r, c = a.shape

    def body(a_ref, o_ref):
        o_ref[...] = a_ref[...].astype(BF16)

    return pl.pallas_call(
        body, grid=(r // tr,), in_specs=[pl.BlockSpec((tr, c), lambda i: (i, 0))],
        out_specs=pl.BlockSpec((tr, c), lambda i: (i, 0)), out_shape=jax.ShapeDtypeStruct((r, c), BF16),
        name=name, compiler_params=_cp("parallel"))(a)


def _column_major(a):
    return jnp.transpose(a, (2, 0, 1))


def _cast_bf16_column_major(a3, pieces, name):
    _, r, c = a3.shape
    rows = r // pieces

    def body(a_ref, *o_refs):
        t = a_ref[...].reshape(LANES, r).T.astype(BF16)
        for k, o_ref in enumerate(o_refs):
            o_ref[...] = t[k * rows:(k + 1) * rows]

    return pl.pallas_call(
        body, grid=(pl.cdiv(c, LANES),), in_specs=[pl.BlockSpec((LANES, 1, r), lambda i: (i, 0, 0))],
        out_specs=[pl.BlockSpec((rows, LANES), lambda i: (0, i))] * pieces,
        out_shape=[jax.ShapeDtypeStruct((rows, c), BF16)] * pieces, name=name, compiler_params=_cp("parallel"))(_column_major(a3))


def _adamw_column_major(w3, parts, m3, v3, name):
    _, r, c = w3.shape
    n_parts = parts[0].shape[0]

    def body(w_ref, *refs):
        p_refs, (m_ref, v_ref, g_ref, d_ref, nm_ref, nv_ref) = refs[:len(parts)], refs[len(parts):]
        sums = []
        for p_ref in p_refs:
            g = p_ref[0].astype(F32)
            for s in range(1, n_parts):
                g = g + p_ref[s].astype(F32)
            sums.append(g)
        g = jnp.concatenate(sums, axis=0).T
        flat = lambda ref: ref[...].reshape(LANES, r)
        m_new = ADAM_B1 * flat(m_ref) + (1.0 - ADAM_B1) * g
        v_new = ADAM_B2 * flat(v_ref) + (1.0 - ADAM_B2) * (g * g)
        m_hat = m_new / (1.0 - ADAM_B1 ** ADAM_STEP)
        v_hat = v_new / (1.0 - ADAM_B2 ** ADAM_STEP)
        delta = -ADAM_LR * (m_hat / (jnp.sqrt(v_hat) + ADAM_EPS) + ADAM_WD * flat(w_ref))
        for ref, val in ((g_ref, g), (d_ref, delta), (nm_ref, m_new), (nv_ref, v_new)):
            ref[...] = val.reshape(LANES, 1, r)

    blk = pl.BlockSpec((LANES, 1, r), lambda i: (i, 0, 0))
    outs = pl.pallas_call(
        body, grid=(pl.cdiv(c, LANES),),
        in_specs=[blk] + [pl.BlockSpec((n_parts, p.shape[1], LANES), lambda i: (0, 0, i)) for p in parts] + [blk, blk],
        out_specs=[blk] * 4, out_shape=[jax.ShapeDtypeStruct((c, 1, r), F32)] * 4, name=name,
        compiler_params=_cp("parallel"))(_column_major(w3), *parts, _column_major(m3), _column_major(v3))
    return [jnp.transpose(o, (1, 2, 0)) for o in outs]


def _gather_copies(ins, outs, send_sems, recv_sems, local_sems):
    n = len(ins)
    x, y, c = lax.axis_index("x"), lax.axis_index("y"), lax.axis_index("c")
    me, sibling = (x, y, c), (x, y, 1 - c)
    xn, yn, dg = (1 - x, y), (x, 1 - y), (1 - x, 1 - y)

    def copy(a, k, block, to, src=None):
        px, py, pc = block
        rows = outs[a].at[4 * px + 2 * py + pc]
        return pltpu.make_async_remote_copy(
            src_ref=rows if src is None else src, dst_ref=rows, send_sem=send_sems.at[a, k],
            recv_sem=recv_sems.at[a, k], device_id=to, device_id_type=_MESH)

    local = [pltpu.make_async_copy(ins[a], outs[a].at[4 * x + 2 * y + c], local_sems.at[a]) for a in range(n)]
    own = [cp for a in range(n) for cp in (copy(a, 0, me, sibling, src=ins[a]), copy(a, 1, me, (*xn, c), src=ins[a]),
                                           copy(a, 2, me, (*yn, c), src=ins[a]))]

    def start():
        for cp in local + own:
            cp.start()

    def finish():
        for a in range(n):
            @pl.when(c == 1)
            def _(a=a):
                copy(a, 1, (*xn, c), me).wait_recv()
                copy(a, 3, (*xn, c), (*yn, c)).start()

            @pl.when(c == 0)
            def _(a=a):
                copy(a, 2, (*yn, c), me).wait_recv()
                copy(a, 3, (*yn, c), (*xn, c)).start()
        for a in range(n):
            pl.when(c == 0)(copy(a, 1, (*xn, c), me).wait_recv)
            copy(a, 4, (*xn, c), sibling).start()
            pl.when(c == 1)(copy(a, 2, (*yn, c), me).wait_recv)
            copy(a, 5, (*yn, c), sibling).start()
        for a in range(n):
            copy(a, 3, (*dg, c), me).wait_recv()
            copy(a, 6, (*dg, c), sibling).start()
        for a in range(n):
            copy(a, 0, sibling, me).wait_recv()
            for k, chip in ((4, xn), (5, yn), (6, dg)):
                copy(a, k, (*chip, 1 - c), me).wait_recv()
                copy(a, k, (*chip, c), sibling).wait_send()
            copy(a, 3, (*xn, c), (*yn, c)).wait_send()
        for cp in own:
            cp.wait_send()
        for cp in local:
            cp.wait()

    return start, finish


def _gather_scratch(n):
    return [pltpu.SemaphoreType.DMA((n, N_DEV - 1)), pltpu.SemaphoreType.DMA((n, N_DEV - 1)), pltpu.SemaphoreType.DMA((n,))]


def _gather_rider(arrays):
    return _Rider(list(arrays), [jax.ShapeDtypeStruct((N_DEV,) + a.shape, a.dtype) for a in arrays],
                  _gather_scratch(len(arrays)), {}, lambda ins, outs, scratch: _gather_copies(ins, outs, *scratch))


def _all_gather(arrays, name):
    n = len(arrays)

    def body(*refs):
        start, finish = _gather_copies(refs[:n], refs[n:2 * n], *refs[2 * n:])
        start()
        finish()

    return pl.pallas_call(
        body, in_specs=[_ANY] * n, out_specs=[_ANY] * n,
        out_shape=[jax.ShapeDtypeStruct((N_DEV,) + a.shape, a.dtype) for a in arrays],
        scratch_shapes=_gather_scratch(n), name=name)(*arrays)


SLAB = 10 * LANES


def _slab_start(blk, nh, cols):
    in_second_half = blk >= N_DEV // 2
    shift = (2 * nh if in_second_half else 0) if isinstance(blk, int) else jnp.where(in_second_half, 2 * nh, 0)
    return (blk * cols - shift) // LANES * LANES


def _pair_rider(dw_rows=None, parts=None, nh=None, after=None):
    if dw_rows is not None:
        r, full = dw_rows.shape
        cols = (full - NARROW + 3 * nh) // N_DEV
        out_shapes = [jax.ShapeDtypeStruct((N_CHIP, r, SLAB), dw_rows.dtype), jax.ShapeDtypeStruct((r, NARROW), dw_rows.dtype)]
    else:
        out_shapes = [jax.ShapeDtypeStruct((N_CHIP,) + parts.shape[1:], parts.dtype)]

    def make(ins, outs, scratch):
        send_sems, recv_sems = scratch
        x, y, c = lax.axis_index("x"), lax.axis_index("y"), lax.axis_index("c")
        kw = lambda k: dict(send_sem=send_sems.at[k], recv_sem=recv_sems.at[k], device_id=(x, y, 1 - c), device_id_type=_MESH)
        copies = []
        for q in range(N_CHIP):
            if dw_rows is not None:
                first = pl.multiple_of(_slab_start(2 * q + 1 - c, nh, cols), LANES)
                copies.append(pltpu.make_async_remote_copy(src_ref=ins[0].at[:, pl.ds(first, SLAB)], dst_ref=outs[0].at[q], **kw(q)))
            else:
                copies.append(pltpu.make_async_remote_copy(src_ref=ins[0].at[2 * q + 1 - c], dst_ref=outs[0].at[q], **kw(q)))
        if dw_rows is not None:
            copies.append(pltpu.make_async_remote_copy(src_ref=ins[0].at[:, pl.ds(full - NARROW, NARROW)], dst_ref=outs[1],
                                                       **kw(N_CHIP)))

        def start():
            for cp in copies:
                cp.start()

        def finish():
            for cp in copies:
                cp.wait()

        return start, finish

    return _Rider([dw_rows if dw_rows is not None else parts] + ([] if after is None else [after]), out_shapes,
                  [pltpu.SemaphoreType.DMA((N_CHIP + 1,)), pltpu.SemaphoreType.DMA((N_CHIP + 1,))], {}, make)


def _relayout_pair_sum(dwfull, got_slabs, got_tail, core, nh, tr, name):
    d, full = dwfull.shape
    w = nh * HEAD_DIM
    cols = (8 * w + 3 * nh) // N_DEV
    segs = _native_segments(nh)

    def block(f_ref, s_ref, t_ref, q, blk):
        st = _slab_start(blk, nh, cols)
        wide = f_ref[:, st:st + SLAB].astype(F32) + s_ref[q].astype(F32)
        tail = f_ref[:, 8 * w:].astype(F32) + t_ref[...].astype(F32)
        pieces = []
        for s0, s1, t0 in segs:
            lo, hi = max(s0, blk * cols), min(s1, (blk + 1) * cols)
            if lo < hi:
                at = t0 + lo - s0
                pieces.append(tail[:, at - 8 * w:at - 8 * w + hi - lo] if at >= 8 * w else wide[:, at - st:at - st + hi - lo])
        return (pieces[0] if len(pieces) == 1 else jnp.concatenate(pieces, axis=1)).astype(dwfull.dtype)

    def body(core_ref, f_ref, s_ref, t_ref, o_ref):
        for parity in range(2):
            @pl.when(core_ref[0] == parity)
            def _(parity=parity):
                for q in range(N_CHIP):
                    o_ref[q] = block(f_ref, s_ref, t_ref, q, 2 * q + parity)

    return pl.pallas_call(
        body,
        grid_spec=pltpu.PrefetchScalarGridSpec(
            num_scalar_prefetch=1, grid=(d // tr,),
            in_specs=[pl.BlockSpec((tr, full), lambda i, c_ref: (i, 0)), pl.BlockSpec((N_CHIP, tr, SLAB), lambda i, c_ref: (0, i, 0)),
                      pl.BlockSpec((tr, NARROW), lambda i, c_ref: (i, 0))],
            out_specs=pl.BlockSpec((N_CHIP, tr, cols), lambda i, c_ref: (0, i, 0))),
        out_shape=jax.ShapeDtypeStruct((N_CHIP, d, cols), dwfull.dtype), name=name,
        compiler_params=_cp("parallel"))(core, dwfull, got_slabs, got_tail)


def _pair_sum(parts, got, core, tr, name):
    _, r, c = parts.shape

    def body(core_ref, p_ref, g_ref, o_ref):
        o_ref[...] = (p_ref[...].astype(F32) + g_ref[...].astype(F32)).astype(o_ref.dtype)

    return pl.pallas_call(
        body,
        grid_spec=pltpu.PrefetchScalarGridSpec(
            num_scalar_prefetch=1, grid=(N_CHIP, r // tr),
            in_specs=[pl.BlockSpec((1, tr, c), lambda q, i, core_ref: (2 * q + core_ref[0], i, 0)),
                      pl.BlockSpec((1, tr, c), lambda q, i, core_ref: (q, i, 0))],
            out_specs=pl.BlockSpec((1, tr, c), lambda q, i, core_ref: (q, i, 0))),
        out_shape=jax.ShapeDtypeStruct((N_CHIP, r, c), parts.dtype), name=name,
        compiler_params=_cp("parallel", "parallel"))(core, parts, got)


def _native_segments(nh):
    w = nh * HEAD_DIM
    return [(0, 4 * w, 0), (4 * w, 4 * w + 2 * nh, 8 * w), (4 * w + 2 * nh, 8 * w + 2 * nh, 4 * w),
            (8 * w + 2 * nh, 8 * w + 3 * nh, 8 * w + 2 * nh)]


def _relayout_w_in(wg, nh, tr, name, rows_total=None, into=None):
    _, d, cols = wg.shape
    w = nh * HEAD_DIM
    rows_total = into.shape[0] if into is not None else rows_total or d
    first = (rows_total - d) // tr if into is not None else 0

    def native(ref, j0, j1):
        out = []
        while j0 < j1:
            blk = j0 // cols
            end = min(j1, (blk + 1) * cols)
            out.append(ref[blk, :, pl.ds(j0 - blk * cols, end - j0)])
            j0 = end
        return out

    def body(g_ref, *refs):
        o_ref = refs[-1]
        for cidx in range(8 * w // LANES):
            j0 = cidx * LANES + (0 if cidx * LANES < 4 * w else 2 * nh)
            pieces = native(g_ref, j0, j0 + LANES)
            o_ref[:, cidx * LANES:(cidx + 1) * LANES] = pieces[0] if len(pieces) == 1 else jnp.concatenate(pieces, axis=1)
        pieces = (native(g_ref, 4 * w, 4 * w + 2 * nh) + native(g_ref, 8 * w + 2 * nh, 8 * w + 3 * nh)
                  + [jnp.zeros((tr, NARROW - 3 * nh), wg.dtype)])
        o_ref[:, 8 * w:] = jnp.concatenate(pieces, axis=1)

    return pl.pallas_call(
        body, grid=(d // tr,), in_specs=[pl.BlockSpec((N_DEV, tr, cols), lambda i: (0, i, 0))] + [_ANY] * (into is not None),
        out_specs=pl.BlockSpec((tr, 8 * w + NARROW), lambda i: (first + i, 0)),
        out_shape=jax.ShapeDtypeStruct((rows_total, 8 * w + NARROW), wg.dtype),
        input_output_aliases={1: 0} if into is not None else {},
        name=name, compiler_params=_cp("parallel"))(wg, *([into] if into is not None else []))


def _adamw(w, parts, m, v, tr, name, after=None):
    r, c = w.shape
    n_parts = parts.shape[0]

    def body(w_ref, p_ref, m_ref, v_ref, *refs):
        g_ref, d_ref, nm_ref, nv_ref = refs[-5:-1] if after is not None else refs
        if after is not None:
            refs[-1][...] = jnp.zeros_like(refs[-1])
        g = p_ref[0].astype(F32)
        for s in range(1, n_parts):
            g = g + p_ref[s].astype(F32)
        m_new = ADAM_B1 * m_ref[...] + (1.0 - ADAM_B1) * g
        v_new = ADAM_B2 * v_ref[...] + (1.0 - ADAM_B2) * (g * g)
        m_hat = m_new / (1.0 - ADAM_B1 ** ADAM_STEP)
        v_hat = v_new / (1.0 - ADAM_B2 ** ADAM_STEP)
        g_ref[...] = g
        d_ref[...] = -ADAM_LR * (m_hat / (jnp.sqrt(v_hat) + ADAM_EPS) + ADAM_WD * w_ref[...])
        nm_ref[...] = m_new
        nv_ref[...] = v_new

    blk = pl.BlockSpec((tr, c), lambda i: (i, 0))
    extra = after is not None
    return pl.pallas_call(
        body, grid=(r // tr,),
        in_specs=[blk, pl.BlockSpec((n_parts, tr, c), lambda i: (0, i, 0)), blk, blk] + [_ANY] * extra,
        out_specs=[blk] * 4 + [pl.BlockSpec((8, LANES), lambda i: (0, 0))] * extra,
        out_shape=[jax.ShapeDtypeStruct((r, c), F32)] * 4 + [jax.ShapeDtypeStruct((8, LANES), F32)] * extra, name=name,
        compiler_params=_cp("arbitrary" if extra else "parallel"))(w, parts, m, v, *([after] if extra else []))


def _adamw_conv(w, gathered, dev, m, v, name):
    c, r = w.shape
    n_parts = gathered.shape[0]

    def body(dev_ref, w_ref, p_ref, m_ref, v_ref, g_ref, d_ref, nm_ref, nv_ref):
        g = p_ref[0].astype(F32)
        for s in range(1, n_parts):
            g = g + p_ref[s].astype(F32)
        m_new = ADAM_B1 * m_ref[...] + (1.0 - ADAM_B1) * g
        v_new = ADAM_B2 * v_ref[...] + (1.0 - ADAM_B2) * (g * g)
        m_hat = m_new / (1.0 - ADAM_B1 ** ADAM_STEP)
        v_hat = v_new / (1.0 - ADAM_B2 ** ADAM_STEP)
        g_ref[...] = g
        d_ref[...] = -ADAM_LR * (m_hat / (jnp.sqrt(v_hat) + ADAM_EPS) + ADAM_WD * w_ref[...])
        nm_ref[...] = m_new
        nv_ref[...] = v_new

    blk = pl.BlockSpec((c, r), lambda i, dev_ref: (0, 0))
    return pl.pallas_call(
        body,
        grid_spec=pltpu.PrefetchScalarGridSpec(
            num_scalar_prefetch=1, grid=(1,),
            in_specs=[blk, pl.BlockSpec((n_parts, c, r), lambda i, dev_ref: (0, 0, dev_ref[0])), blk, blk], out_specs=[blk] * 4),
        out_shape=[jax.ShapeDtypeStruct((c, r), F32)] * 4, name=name, compiler_params=_cp("arbitrary"))(dev, w, gathered, m, v)


def _pack_small(d, pre, post, a_log, dt_bias, f_bias, gdn_w, fq_w, fk_w, extra):
    row2 = jnp.concatenate([a_log, dt_bias, f_bias, gdn_w, fq_w, fk_w, extra], axis=1)
    row2 = jnp.pad(row2, ((0, 0), (0, d - row2.shape[1])))
    return jnp.concatenate([pre, post, row2, jnp.zeros((5, d), F32)], axis=0)


def _unpack_small(p, nh):
    o = 3 * nh
    return dict(pre=p[0:1], post=p[1:2], a_log=p[2:3, 0:nh], dt_bias=p[2:3, nh:2 * nh], f_bias=p[2:3, 2 * nh:o],
                gdn_w=p[2:3, o:o + HEAD_DIM], fq_w=p[2:3, o + HEAD_DIM:o + 2 * HEAD_DIM],
                fk_w=p[2:3, o + 2 * HEAD_DIM:o + 3 * HEAD_DIM], extra=p[2, o + 3 * HEAD_DIM])


def kernel(x, meta_tokens, pre_norm_w, w_in, conv_w, a_log, dt_bias, gdn_norm_w, fox_q_norm_w, fox_k_norm_w, fox_f_bias, w_out, post_norm_w, loss_target, m_meta_tokens, m_pre_norm_w, m_w_in, m_conv_w, m_a_log, m_dt_bias, m_gdn_norm_w, m_fox_q_norm_w, m_fox_k_norm_w, m_fox_f_bias, m_w_out, m_post_norm_w, v_meta_tokens, v_pre_norm_w, v_w_in, v_conv_w, v_a_log, v_dt_bias, v_gdn_norm_w, v_fox_q_norm_w, v_fox_k_norm_w, v_fox_f_bias, v_w_out, v_post_norm_w):
    nh = a_log.shape[1]
    d = x.shape[-1]
    w = nh * HEAD_DIM
    zero = jnp.zeros((1, 1), F32)

    w_in_a, w_in_b = _cast_bf16_column_major(w_in, 2, "cast_w_in")

    def project(xn, got):
        wg, cg, mg = got
        half = (d // 2, 0), (d // 2, 1)
        wfull = _relayout_w_in(wg, nh, 256, "relayout_w_in_a", rows_total=d)
        proj, got = _matmul(xn, wfull, "nn", MM_TILE, F32, "proj_a", _gather_rider([w_in_b]), a_cols=half[0], b_rows=half[0])
        wfull = _relayout_w_in(got[0], nh, 256, "relayout_w_in_b", into=wfull)
        proj = _matmul(xn, wfull, "nn", MM_TILE, F32, "proj_b", a_cols=half[1], b_rows=half[1], acc=proj)
        return proj, wfull, cg.transpose(1, 0, 2).reshape(CONV_WIDTH, 3 * w), mg.transpose(1, 0, 2).reshape(N_META, d)

    late_weights = (_gather_rider([_cast_bf16(w_out[0], 256, "cast_w_out")]), lambda got: got[0].reshape(2 * w, d))
    core = lax.axis_index("c")
    dev = 4 * lax.axis_index("x") + 2 * lax.axis_index("y") + core
    core_arr = jnp.reshape(core, (1,)).astype(jnp.int32)

    out_parts = lambda dw_out: dw_out.reshape(N_DEV, 2 * w // N_DEV, d)
    g = _layer_grads(
        x[0], loss_target[0], None, pre_norm_w, (_gather_rider([w_in_a, conv_w[0].T, meta_tokens]), 2, project), None,
        a_log, dt_bias, gdn_norm_w,
        fox_q_norm_w, fox_k_norm_w, fox_f_bias, None, post_norm_w, late_weights=late_weights,
        w_out_grads=(lambda dw_out: _pair_rider(parts=out_parts(dw_out)),
                     lambda dw_out, got: _chip_rider(_pair_sum(out_parts(dw_out), got[0], core_arr, 256, "pair_sum_w_out"))))
    p_out = g["w_out_parts"][0]
    xn, dproj, wfull, meta_full = g["xn"], g["dproj"], g["wfull"], g["meta"]
    flights, token, dw, rider = [], None, None, None

    def exchange(dw, got, i):
        sums = _relayout_pair_sum(dw, got[0], got[1], core_arr, nh, 128, f"relayout_pair_sum_{i}")
        flight, token = _chip_exchange_start(sums, f"chip_exchange_start_{i}")
        flights.append(flight)
        return token

    for i, (index, parts) in enumerate(DW_IN_PIECES):
        res = _matmul(xn, dproj, "tn", MM_TILE, BF16, f"dw_in_{i}", rider, a_cols=(d // parts, index))
        if i:
            token = exchange(dw, res[1], i - 1)
        dw = res[0] if i else res
        rider = _pair_rider(dw_rows=dw, nh=nh, after=token)
    dxn, (got,) = _dxn(dproj, wfull, [rider], MM_TILE, "dxn")
    token = exchange(dw, got, len(DW_IN_PIECES) - 1)
    *r_out, token = _adamw(w_out[0], p_out, m_w_out[0], v_w_out[0], 64, "adamw_w_out", after=token)
    (grad_x, dmeta, dpre_w), _ = _prenorm_bwd(dxn, x[0], meta_full, pre_norm_w + token[0:1, 0:1], g["dy"])
    small = _pack_small(d, dpre_w, g["post_w"], g["a_log"], g["dt_bias"], g["f_bias"], g["gdn_norm_w"], g["fq_w"],
                        g["fk_w"], g["loss"])
    a_conv, a_meta, p_small = _all_gather([g["conv_wt"], dmeta, small], "gather_small_grads")
    p_meta = lax.dynamic_slice_in_dim(a_meta, dev * meta_tokens.shape[1], meta_tokens.shape[1], axis=2)

    r_conv = _adamw_conv(conv_w[0].T, a_conv, jnp.reshape(dev, (1,)).astype(jnp.int32), m_conv_w[0].T, v_conv_w[0].T,
                         "adamw_conv_w")
    r_conv = [o.T for o in r_conv]
    r_meta = _adamw(meta_tokens, p_meta, m_meta_tokens, v_meta_tokens, N_META, "adamw_meta")
    pk = lambda pre, post, a, dt, gw, fq, fk, fb: _pack_small(d, pre, post, a, dt, fb, gw, fq, fk, zero)
    r_small = _adamw(
        pk(pre_norm_w, post_norm_w, a_log, dt_bias, gdn_norm_w, fox_q_norm_w, fox_k_norm_w, fox_f_bias), p_small,
        pk(m_pre_norm_w, m_post_norm_w, m_a_log, m_dt_bias, m_gdn_norm_w, m_fox_q_norm_w, m_fox_k_norm_w, m_fox_f_bias),
        pk(v_pre_norm_w, v_post_norm_w, v_a_log, v_dt_bias, v_gdn_norm_w, v_fox_q_norm_w, v_fox_k_norm_w, v_fox_f_bias),
        8, "adamw_small")
    p_in = _chip_exchange_wait(flights, r_small[0], "chip_exchange_wait")
    r_in = _adamw_column_major(w_in, p_in, m_w_in, v_w_in, "adamw_w_in")

    sm = [_unpack_small(r, nh) for r in r_small]
    outs = []
    for i in range(4):
        s = sm[i]
        outs += [r_meta[i], s["pre"], r_in[i], r_conv[i][None], s["a_log"], s["dt_bias"], s["gdn_w"], s["fq_w"],
                 s["fk_w"], s["f_bias"], r_out[i][None], s["post"]]
    return (sm[0]["extra"], grad_x[None], *outs)
```

```python
import jax
import jax.numpy as jnp
from jax import lax
from jax.experimental import pallas as pl
from jax.experimental.pallas import tpu as pltpu

F32, BF16 = jnp.float32, jnp.bfloat16
HEAD_DIM = 128
N_META = 16
CONV_WIDTH = 4
CHUNK = 128
Q_BLOCK = 128
LANES = 128
EPS = 1e-6
PAD_ROWS = Q_BLOCK - N_META
N_DEV = 8
N_CHIP = 4
VMEM_LIMIT = 56 * 1024 * 1024
NEG = -1e30
NARROW = 2 * LANES
MM_TILE = 6 * LANES
DW_IN_PIECES = ((0, 2), (2, 4), (3, 4))

ADAM_LR, ADAM_B1, ADAM_B2, ADAM_EPS, ADAM_WD, ADAM_STEP = 0.001, 0.9, 0.999, 1e-08, 0.01, 10

_DN = {"nn": (((1,), (0,)), ((), ())), "nt": (((1,), (1,)), ((), ())), "tn": (((0,), (0,)), ((), ()))}
_DN3 = {"nn": (((2,), (1,)), ((0,), (0,))), "nt": (((2,), (2,)), ((0,), (0,))), "tn": (((1,), (1,)), ((0,), (0,)))}
_ANY = pl.BlockSpec(memory_space=pl.ANY)
_MESH = pl.DeviceIdType.MESH


def _cp(*sem):
    return pltpu.CompilerParams(dimension_semantics=sem, vmem_limit_bytes=VMEM_LIMIT)


def _dot(a, b, dims="nn", prec=None):
    return lax.dot_general(a, b, _DN[dims], precision=prec, preferred_element_type=F32)


def _bdot(a, b, dims="nn"):
    return _dot(a.astype(BF16), b.astype(BF16), dims)


def _hdot(a, b, dims="nn"):
    return _dot(a, b, dims, prec=lax.Precision.HIGHEST)


def _dot3(a, b, dims="nn"):
    return lax.dot_general(a, b, _DN3[dims], preferred_element_type=F32)


def _bdot3(a, b, dims="nn"):
    return _dot3(a.astype(BF16), b.astype(BF16), dims)


def _split(a):
    hi = a.astype(BF16)
    return hi, (a - hi.astype(F32)).astype(BF16)


def _iota(shape, dim):
    return lax.broadcasted_iota(jnp.int32, shape, dim)


def _sigmoid(z):
    return 1.0 / (1.0 + jnp.exp(-z))


def _softplus(z):
    e = jnp.exp(-jnp.abs(z))
    u = 1.0 + e
    l1p = jnp.where(u == 1.0, e, jnp.log(u) * (e / jnp.where(u == 1.0, 1.0, u - 1.0)))
    return jnp.maximum(z, 0.0) + l1p


def _silu_and_grad(z):
    s = _sigmoid(z)
    return z * s, s * (1.0 + z * (1.0 - s))


def _rms(x):
    return lax.rsqrt(jnp.mean(x * x, axis=-1, keepdims=True) + EPS)


def _h_tile(i, x_ref, meta_ref):
    first = jnp.concatenate([jnp.zeros((PAD_ROWS, x_ref.shape[1]), F32), meta_ref[...]], axis=0)
    return jnp.where(i == 0, first, x_ref[...])


def _x_rows(d):
    return pl.BlockSpec((Q_BLOCK, d), lambda i: (jnp.maximum(i - 1, 0), 0))


def _prenorm(x, meta, w):
    seq, d = x.shape
    lp = seq + Q_BLOCK

    def body(x_ref, m_ref, w_ref, o_ref):
        h = _h_tile(pl.program_id(0), x_ref, m_ref)
        o_ref[...] = (h * _rms(h) * w_ref[...]).astype(BF16)

    return pl.pallas_call(
        body, grid=(lp // Q_BLOCK,),
        in_specs=[_x_rows(d), pl.BlockSpec((N_META, d), lambda i: (0, 0)), pl.BlockSpec((1, d), lambda i: (0, 0))],
        out_specs=pl.BlockSpec((Q_BLOCK, d), lambda i: (i, 0)),
        out_shape=jax.ShapeDtypeStruct((lp, d), BF16), name="prenorm", compiler_params=_cp("parallel"))(x, meta, w)


def _prenorm_gathering(x, w, rider, meta_at):
    seq, d = x.shape
    steps = seq // Q_BLOCK + 1

    def body(start, finish, x_ref, w_ref, o_ref, meta_buf, meta_sem):
        i = pl.program_id(0)
        pl.when(i == 0)(start)

        @pl.when(i < steps - 1)
        def _():
            h = x_ref[...]
            o_ref[...] = (h * _rms(h) * w_ref[...]).astype(BF16)

        @pl.when(i == steps - 1)
        def _():
            finish()
            cp = pltpu.make_async_copy(finish.results[0][meta_at], meta_buf, meta_sem)
            cp.start()
            cp.wait()
            h = jnp.concatenate([jnp.zeros((PAD_ROWS, d), F32), jnp.concatenate([meta_buf[s] for s in range(N_DEV)], axis=1)], axis=0)
            o_ref[...] = (h * _rms(h) * w_ref[...]).astype(BF16)

    (xn,), got = _hosted_call(
        body, [rider], 2, 1, 2, grid=(steps,),
        in_specs=[pl.BlockSpec((Q_BLOCK, d), lambda i: (jnp.minimum(i, steps - 2), 0)), pl.BlockSpec((1, d), lambda i: (0, 0))],
        out_specs=[pl.BlockSpec((Q_BLOCK, d), lambda i: ((i + 1) % steps, 0))],
        out_shape=[jax.ShapeDtypeStruct((seq + Q_BLOCK, d), BF16)],
        scratch_shapes=[pltpu.VMEM((N_DEV, N_META, d // N_DEV), F32), pltpu.SemaphoreType.DMA(())],
        name="prenorm", compiler_params=_cp("arbitrary"))(x, w)
    return xn, got[0]


def _tile(n, want):
    return max(t for t in range(LANES, want + 1, LANES) if n % t == 0)


class _Rider:
    def __init__(self, inputs, out_shapes, scratch, aliases, make):
        self.inputs, self.out_shapes, self.scratch, self.aliases, self.make = inputs, out_shapes, scratch, aliases, make


def _hosted_call(body, riders, n_in, n_out, n_scratch, *, in_specs, out_specs, out_shape, scratch_shapes=(), aliases=None,
                 **kw):
    riders = [r for r in riders if r is not None]
    r_in = [len(r.inputs) for r in riders]
    r_out = [len(r.out_shapes) for r in riders]
    r_scr = [len(r.scratch) for r in riders]
    al = dict(aliases or {})
    for k, r in enumerate(riders):
        al.update({n_in + sum(r_in[:k]) + i: n_out + sum(r_out[:k]) + o for i, o in r.aliases.items()})

    def full_body(*refs):
        ins, rest = refs[:n_in + sum(r_in)], refs[n_in + sum(r_in):]
        outs, scr = rest[:n_out + sum(r_out)], rest[n_out + sum(r_out):]
        hooks = [r.make(ins[n_in + sum(r_in[:k]):n_in + sum(r_in[:k + 1])], outs[n_out + sum(r_out[:k]):n_out + sum(r_out[:k + 1])],
                        scr[n_scratch + sum(r_scr[:k]):n_scratch + sum(r_scr[:k + 1])]) for k, r in enumerate(riders)]

        def start():
            for h in hooks:
                h[0]()

        def finish():
            for h in hooks:
                h[1]()

        finish.results = [outs[n_out + sum(r_out[:k]):n_out + sum(r_out[:k + 1])] for k in range(len(riders))]
        body(start, finish, *ins[:n_in], *outs[:n_out], *scr[:n_scratch])

    call = pl.pallas_call(
        full_body, in_specs=list(in_specs) + [_ANY] * sum(r_in), out_specs=list(out_specs) + [_ANY] * sum(r_out),
        out_shape=list(out_shape) + [s for r in riders for s in r.out_shapes],
        scratch_shapes=list(scratch_shapes) + [s for r in riders for s in r.scratch], input_output_aliases=al, **kw)

    def run(*args):
        res = call(*args, *[t for r in riders for t in r.inputs])
        return res[:n_out], [res[n_out + sum(r_out[:k]):n_out + sum(r_out[:k + 1])] for k in range(len(riders))]

    return run


def _matmul(a, b, dims, tn, out_dtype, name, rider=None, a_cols=None, b_rows=None, acc=None):
    a_shape = a.shape if a_cols is None else (a.shape[0], a_cols[0])
    a_index = 0 if a_cols is None else a_cols[1]
    m = a_shape[1] if dims == "tn" else a_shape[0]
    n = b.shape[0] if dims == "nt" else b.shape[1]
    kdim = b.shape[1] if dims == "nt" else b.shape[0] if b_rows is None else b_rows[0]
    b_index = 0 if b_rows is None else b_rows[1]
    tn = _tile(n, tn)
    steps = n // tn
    b_spec = pl.BlockSpec((tn, kdim), lambda j: (j, 0)) if dims == "nt" else pl.BlockSpec((kdim, tn), lambda j: (b_index, j))
    o_spec = pl.BlockSpec((m, tn), lambda j: (0, j))

    def body(start, finish, a_ref, b_ref, *refs):
        pl.when(pl.program_id(0) == 0)(start)
        prod = _dot(a_ref[...], b_ref[...], dims)
        refs[-1][...] = (prod if acc is None else prod + refs[0][...]).astype(out_dtype)
        pl.when(pl.program_id(0) == steps - 1)(finish)

    (out,), got = _hosted_call(
        body, [rider], 2 + (acc is not None), 1, 0, grid=(steps,),
        in_specs=[pl.BlockSpec(a_shape, lambda j: (0, a_index)), b_spec] + [o_spec] * (acc is not None),
        out_specs=[o_spec], out_shape=[jax.ShapeDtypeStruct((m, n), out_dtype)], aliases={2: 0} if acc is not None else None,
        name=name, compiler_params=_cp("parallel" if rider is None else "arbitrary"))(a, b, *([acc] if acc is not None else []))
    return out if rider is None else (out, got[0])


def _chip_rider(sums):
    def make(ins, outs, scratch):
        local, remote = _chip_exchange_copies(ins[0], outs[0], *scratch)

        def start():
            for cp in [local] + remote:
                cp.start()

        def finish():
            local.wait()
            for cp in remote:
                cp.wait_send()
                cp.wait_recv()

        return start, finish

    return _Rider([sums], [jax.ShapeDtypeStruct(sums.shape, sums.dtype)],
                  [pltpu.SemaphoreType.DMA((N_CHIP - 1,)), pltpu.SemaphoreType.DMA((N_CHIP - 1,)), pltpu.SemaphoreType.DMA((1,))],
                  {}, make)


_HBM = pl.BlockSpec(memory_space=pltpu.HBM)
_SEM = pl.BlockSpec(memory_space=pltpu.SEMAPHORE)


def _chip_exchange_copies(sums_ref, land_ref, send_sems, recv_sems, local_sem):
    x, y, core = lax.axis_index("x"), lax.axis_index("y"), lax.axis_index("c")
    mine = 2 * x + y
    local = pltpu.make_async_copy(sums_ref.at[mine], land_ref.at[mine], local_sem.at[0])
    remote = []
    for k in range(1, N_CHIP):
        px = 1 - x if k & 2 else x
        py = 1 - y if k & 1 else y
        remote.append(pltpu.make_async_remote_copy(
            src_ref=sums_ref.at[2 * px + py], dst_ref=land_ref.at[mine], send_sem=send_sems.at[k - 1],
            recv_sem=recv_sems.at[k - 1], device_id=(px, py, core), device_id_type=_MESH))
    return local, remote


def _chip_exchange_start(sums, name):
    def body(s_ref, send_sems, recv_sems, local_sem, s_thru, land_ref, token):
        local, remote = _chip_exchange_copies(s_ref, land_ref, send_sems, recv_sems, local_sem)
        for cp in [local] + remote:
            cp.start()
        token[...] = jnp.zeros_like(token)

    *flight, token = pl.pallas_call(
        body, name=name,
        out_shape=(pltpu.SemaphoreType.DMA((N_CHIP - 1,)), pltpu.SemaphoreType.DMA((N_CHIP - 1,)), pltpu.SemaphoreType.DMA((1,)),
                   pltpu.HBM(sums.shape, sums.dtype), pltpu.HBM(sums.shape, sums.dtype), jax.ShapeDtypeStruct((8, LANES), F32)),
        in_specs=(_HBM,), out_specs=(_SEM, _SEM, _SEM, _HBM, _HBM, pl.BlockSpec(memory_space=pltpu.VMEM)),
        input_output_aliases={0: 3},
        compiler_params=pltpu.CompilerParams(has_side_effects=pltpu.SideEffectType.DATAFLOW_SIDE_EFFECTING))(
            pltpu.with_memory_space_constraint(sums, pltpu.HBM))
    return flight, token


def _chip_exchange_wait(flights, after, name):
    n = len(flights)

    def body(*refs):
        for i in range(n):
            s_ref, land_ref = refs[2 * i:2 * i + 2]
            local, remote = _chip_exchange_copies(s_ref, land_ref, *refs[2 * n + 3 * i:2 * n + 3 * i + 3])
            local.wait()
            for cp in remote:
                cp.wait_send()
                cp.wait_recv()

    buffers = [b for f in flights for b in f[3:]]
    res = pl.pallas_call(
        body, name=name, out_shape=tuple(pltpu.HBM(b.shape, b.dtype) for b in buffers),
        in_specs=(_HBM,) * (2 * n) + (_SEM,) * (3 * n) + (_ANY,), out_specs=(_HBM,) * (2 * n),
        input_output_aliases={i: i for i in range(2 * n)},
        compiler_params=pltpu.CompilerParams(has_side_effects=pltpu.SideEffectType.DATAFLOW_SIDE_EFFECTING))(
            *buffers, *[s for f in flights for s in f[:3]], after)
    return res[1::2]


def _dxn(dproj, wfull, riders, tk, name):
    m, k = dproj.shape
    n = wfull.shape[0]
    tk = _tile(k, tk)
    steps = k // tk

    def body(start, finish, a_ref, b_ref, o_ref):
        j = pl.program_id(0)

        @pl.when(j == 0)
        def _():
            start()
            o_ref[...] = jnp.zeros_like(o_ref)
        o_ref[...] += _dot(a_ref[...], b_ref[...], "nt")
        pl.when(j == steps - 1)(finish)

    (dxn,), got = _hosted_call(
        body, riders, 2, 1, 0, grid=(steps,),
        in_specs=[pl.BlockSpec((m, tk), lambda j: (0, j)), pl.BlockSpec((n, tk), lambda j: (0, j))],
        out_specs=[pl.BlockSpec((m, n), lambda j: (0, 0))], out_shape=[jax.ShapeDtypeStruct((m, n), F32)],
        name=name, compiler_params=_cp("arbitrary"))(dproj, wfull)
    return dxn, got


def _conv_taps(x, w):
    c = x * w[CONV_WIDTH - 1:CONV_WIDTH, :]
    for j in range(CONV_WIDTH - 1):
        c = c + pltpu.roll(x, CONV_WIDTH - 1 - j, 0) * w[j:j + 1, :]
    return c


def _gdn_prep(proj, conv_wt, nh):
    lp = proj.shape[0]
    scale = HEAD_DIM ** -0.5

    def body(x_ref, w_ref, o_ref):
        which = pl.program_id(0) // nh
        c = _conv_taps(x_ref[...], w_ref[...])
        s = c * _sigmoid(c)
        r = lax.rsqrt(jnp.sum(s * s, axis=-1, keepdims=True) + EPS)
        f = jnp.where(which == 0, r * scale, jnp.where(which == 1, r, 1.0))
        o_ref[...] = jnp.where(_iota(s.shape, 0) >= PAD_ROWS, s * f, 0.0)

    return pl.pallas_call(
        body, grid=(3 * nh,),
        in_specs=[pl.BlockSpec((lp, LANES), lambda s: (0, s)), pl.BlockSpec((CONV_WIDTH, LANES), lambda s: (0, s))],
        out_specs=pl.BlockSpec((lp, LANES), lambda s: (0, s)),
        out_shape=jax.ShapeDtypeStruct((lp, 3 * nh * HEAD_DIM), F32), name="gdn_prep",
        compiler_params=_cp("parallel"))(proj, conv_wt)


def _gdn_prep_bwd(proj, conv_wt, dq, dk, dv, dproj, nh):
    lp = proj.shape[0]
    scale = HEAD_DIM ** -0.5
    part = lambda p: pl.BlockSpec((lp, LANES), lambda s: (0, jnp.clip(s - p * nh, 0, nh - 1)))

    def body(x_ref, w_ref, dq_ref, dk_ref, dv_ref, _, dx_ref, dw_ref):
        which = pl.program_id(0) // nh
        x = x_ref[...]
        w = w_ref[...]
        c = _conv_taps(x, w)
        sg = _sigmoid(c)
        s = c * sg
        r = lax.rsqrt(jnp.sum(s * s, axis=-1, keepdims=True) + EPS)
        dy = jnp.where(which == 0, dq_ref[...], jnp.where(which == 1, dk_ref[...], dv_ref[...]))
        dy = jnp.where(_iota(s.shape, 0) >= PAD_ROWS, dy, 0.0)
        y0 = s * r
        dy0 = dy * jnp.where(which == 0, scale, 1.0)
        ds_n = r * (dy0 - y0 * jnp.sum(dy0 * y0, axis=-1, keepdims=True))
        ds = jnp.where(which == 2, dy, ds_n)
        dc = ds * (sg * (1.0 + c * (1.0 - sg)))
        dx = dc * w[CONV_WIDTH - 1:CONV_WIDTH, :]
        rows = [jnp.sum(dc * x, axis=0, keepdims=True)]
        for j in range(CONV_WIDTH - 2, -1, -1):
            sh = CONV_WIDTH - 1 - j
            dx = dx + pltpu.roll(dc, lp - sh, 0) * w[j:j + 1, :]
            rows.insert(0, jnp.sum(dc * pltpu.roll(x, sh, 0), axis=0, keepdims=True))
        dx_ref[...] = dx.astype(BF16)
        dw_ref[...] = jnp.concatenate(rows, axis=0)

    strip = pl.BlockSpec((lp, LANES), lambda s: (0, s))
    taps = pl.BlockSpec((CONV_WIDTH, LANES), lambda s: (0, s))
    return pl.pallas_call(
        body, grid=(3 * nh,), in_specs=[strip, taps, part(0), part(1), part(2), _ANY], out_specs=[strip, taps],
        out_shape=[jax.ShapeDtypeStruct(dproj.shape, BF16), jax.ShapeDtypeStruct((CONV_WIDTH, 3 * nh * HEAD_DIM), F32)],
        input_output_aliases={5: 0}, name="gdn_prep_bwd", compiler_params=_cp("parallel"))(proj, conv_wt, dq, dk, dv, dproj)


def _gates(proj, bias_row, nega_row, nh):
    lp = proj.shape[0]
    nc = lp // CHUNK

    def body(p_ref, b_ref, a_ref, g_ref, gt3_ref, gtf_ref):
        lane = _iota((CHUNK, LANES), 1)
        tri = (_iota((CHUNK, CHUNK), 0) >= _iota((CHUNK, CHUNK), 1)).astype(F32)

        def step(n, carry):
            r0 = pl.multiple_of(n * CHUNK, CHUNK)
            z = p_ref[pl.ds(r0, CHUNK), :] + b_ref[...]
            base = jnp.where(lane < nh, _sigmoid(z),
                             jnp.where(lane < 2 * nh, a_ref[...] * _softplus(z),
                                       jnp.where(lane < 3 * nh, -_softplus(-z), 0.0)))
            base = jnp.where(r0 + _iota((CHUNK, LANES), 0) >= PAD_ROWS, base, 0.0)
            cs = _hdot(tri, base)
            run = jnp.where((lane >= 2 * nh) & (lane < 3 * nh), cs + carry, cs)
            sh = pltpu.roll(run, 2 * nh, 1)
            out = base + jnp.where((lane >= 3 * nh) & (lane < 5 * nh), sh, 0.0)
            g_ref[pl.ds(r0, CHUNK), :] = out
            gt3_ref[n] = out.T
            return carry + cs[CHUNK - 1:CHUNK, :]

        lax.fori_loop(0, nc, step, jnp.zeros((1, LANES), F32))
        gtf_ref[...] = g_ref[...].T

    vec = pl.BlockSpec((1, LANES), lambda i: (0, 0))
    return pl.pallas_call(
        body, grid=(1,), in_specs=[pl.BlockSpec((lp, LANES), lambda i: (0, 8 * nh)), vec, vec],
        out_specs=[pl.BlockSpec((lp, LANES), lambda i: (0, 0)), pl.BlockSpec((nc, LANES, CHUNK), lambda i: (0, 0, 0)),
                   pl.BlockSpec((LANES, lp), lambda i: (0, 0))],
        out_shape=[jax.ShapeDtypeStruct((lp, LANES), F32), jax.ShapeDtypeStruct((nc, LANES, CHUNK), F32),
                   jax.ShapeDtypeStruct((LANES, lp), F32)],
        name="gates", compiler_params=_cp("arbitrary"))(proj, bias_row, nega_row)


def _gates_bwd(proj, bias_row, nega_row, gates, dgate_gdn, dc_t, dproj, nh):
    lp = proj.shape[0]
    nc = lp // CHUNK

    def body(p_ref, b_ref, a_ref, g_ref, dg_ref, dc_ref, _, dz_ref, sm_ref, dct_scr):
        lane = _iota((CHUNK, LANES), 1)
        triu = (_iota((CHUNK, CHUNK), 0) <= _iota((CHUNK, CHUNK), 1)).astype(F32)
        dct_scr[...] = dc_ref[...].T
        sm_ref[...] = jnp.zeros_like(sm_ref)
        dz_ref[:, LANES:] = jnp.zeros((lp, NARROW - LANES), BF16)

        def step(i, carry):
            n = nc - 1 - i
            r0 = pl.multiple_of(n * CHUNK, CHUNK)
            z = p_ref[pl.ds(r0, CHUNK), :] + b_ref[...]
            gt = g_ref[pl.ds(r0, CHUNK), :]
            dgd = dg_ref[pl.ds(r0, CHUNK), :]
            dch = dct_scr[pl.ds(r0, CHUNK), :]
            rc = _hdot(triu, dch) + carry
            sg = _sigmoid(z)
            dz = jnp.where(lane < nh, dgd * sg * (1.0 - sg),
                           jnp.where(lane < 2 * nh, dgd * a_ref[...] * sg,
                                     jnp.where(lane < 3 * nh, rc * (1.0 - sg), 0.0)))
            dz = jnp.where(r0 + _iota((CHUNK, LANES), 0) >= PAD_ROWS, dz, 0.0)
            dz_ref[pl.ds(r0, CHUNK), 0:LANES] = dz.astype(BF16)
            sm_ref[0:1, :] += jnp.sum(dz, axis=0, keepdims=True)
            sm_ref[1:2, :] += jnp.sum(jnp.where((lane >= nh) & (lane < 2 * nh), dgd * gt, 0.0), axis=0, keepdims=True)
            return carry + jnp.sum(dch, axis=0, keepdims=True)

        lax.fori_loop(0, nc, step, jnp.zeros((1, LANES), F32))

    vec = pl.BlockSpec((1, LANES), lambda i: (0, 0))
    full = pl.BlockSpec((lp, LANES), lambda i: (0, 0))
    last = pl.BlockSpec((lp, LANES), lambda i: (0, 8 * nh))
    tail = pl.BlockSpec((lp, NARROW), lambda i: (0, 8 * nh * LANES // NARROW))
    return pl.pallas_call(
        body, grid=(1,), in_specs=[last, vec, vec, full, full, pl.BlockSpec((LANES, lp), lambda i: (0, 0)), _ANY],
        out_specs=[tail, pl.BlockSpec((8, LANES), lambda i: (0, 0))],
        out_shape=[jax.ShapeDtypeStruct(dproj.shape, BF16), jax.ShapeDtypeStruct((8, LANES), F32)],
        scratch_shapes=[pltpu.VMEM((lp, LANES), F32)], input_output_aliases={6: 0},
        name="gates_bwd", compiler_params=_cp("arbitrary"))(proj, bias_row, nega_row, gates, dgate_gdn, dc_t, dproj)


def _tri_inv(a):
    t = jnp.where(_iota(a.shape, 1) == _iota(a.shape, 2), 1.0, 0.0) - a
    p = a
    for _ in range(CHUNK.bit_length() - 2):
        ph, pw = _split(p)
        p = _dot3(ph, ph) + (_dot3(ph, pw) + _dot3(pw, ph))
        ph, pw = _split(p)
        th, tw = _split(t)
        t = t + (_dot3(th, ph) + (_dot3(th, pw) + _dot3(tw, ph)))
    return t


def _gdn_chunk(q, k, v, beta, gc, gr, t=None):
    ii, jj = _iota((1, CHUNK, CHUNK), 1), _iota((1, CHUNK, CHUNK), 2)
    causal, strict = ii >= jj, ii > jj
    dm = jnp.where(causal, jnp.exp(jnp.where(causal, gc - gr, 0.0)), 0.0)
    kk = _bdot3(k, k, "nt")
    a = jnp.where(strict, beta * kk * dm, 0.0)
    if t is None:
        t = _tri_inv(a)
    eg = jnp.exp(gc)
    glast = gc[:, CHUNK - 1:CHUNK, :]
    ekd = jnp.exp(glast - gc)
    bv = beta * v
    bk = (beta * eg) * k
    ub = _bdot3(t, jnp.concatenate([bv, bk], axis=2))
    qk = _bdot3(q, k, "nt")
    return dict(causal=causal, strict=strict, dm=dm, kk=kk, a=a, t=t, eg=eg, ekd=ekd, bv=bv, bk=bk,
                u=ub[:, :, :HEAD_DIM], w=ub[:, :, HEAD_DIM:], qk=qk, aqk=jnp.where(causal, qk * dm, 0.0),
                q_dec=q * eg, k_dec=k * ekd, decay=jnp.exp(glast))


def _heads(ref, nh):
    return jnp.stack([ref[:, h * HEAD_DIM:(h + 1) * HEAD_DIM] for h in range(nh)], axis=0)


def _gdn_chunk_inputs(q_ref, k_ref, v_ref, g, gt, nh):
    col = lambda o: jnp.stack([g[:, o + h:o + h + 1] for h in range(nh)], axis=0)
    gr = jnp.stack([gt[3 * nh + h:3 * nh + h + 1, :] for h in range(nh)], axis=0)
    return _heads(q_ref, nh), _heads(k_ref, nh), _heads(v_ref, nh), col(0), col(3 * nh), gr


def _gdn_fwd(qkv, gates, gt3, nh, rider=None):
    lp = qkv.shape[0]
    nc = lp // CHUNK
    w = nh * HEAD_DIM

    def body(start, finish, q_ref, k_ref, v_ref, g_ref, gt_ref, o_ref, sall_ref, tall_ref, s_scr):
        @pl.when(pl.program_id(0) == 0)
        def _():
            start()
            s_scr[...] = jnp.zeros_like(s_scr)
        c = _gdn_chunk(*_gdn_chunk_inputs(q_ref, k_ref, v_ref, g_ref[...], gt_ref[0], nh))
        s = s_scr[...]
        sall_ref[0] = s
        tall_ref[0] = c["t"]
        v_new = c["u"] - _bdot3(c["w"], s)
        o = _bdot3(c["q_dec"], s) + _bdot3(c["aqk"], v_new)
        s_scr[...] = s * c["decay"] + _bdot3(c["k_dec"], v_new, "tn")
        for h in range(nh):
            o_ref[:, h * HEAD_DIM:(h + 1) * HEAD_DIM] = o[h]
        pl.when(pl.program_id(0) == nc - 1)(finish)

    outs, got = _hosted_call(
        body, [rider], 5, 3, 1, grid=(nc,),
        in_specs=[pl.BlockSpec((CHUNK, w), lambda n: (n, 0)), pl.BlockSpec((CHUNK, w), lambda n: (n, 1)),
                  pl.BlockSpec((CHUNK, w), lambda n: (n, 2)), pl.BlockSpec((CHUNK, LANES), lambda n: (n, 0)),
                  pl.BlockSpec((1, LANES, CHUNK), lambda n: (n, 0, 0))],
        out_specs=[pl.BlockSpec((CHUNK, w), lambda n: (n, 0)),
                   pl.BlockSpec((1, nh, HEAD_DIM, HEAD_DIM), lambda n: (n, 0, 0, 0)),
                   pl.BlockSpec((1, nh, CHUNK, CHUNK), lambda n: (n, 0, 0, 0))],
        out_shape=[jax.ShapeDtypeStruct((lp, w), F32), jax.ShapeDtypeStruct((nc, nh, HEAD_DIM, HEAD_DIM), F32),
                   jax.ShapeDtypeStruct((nc, nh, CHUNK, CHUNK), F32)],
        scratch_shapes=[pltpu.VMEM((nh, HEAD_DIM, HEAD_DIM), F32)],
        name="gdn_fwd", compiler_params=_cp("arbitrary"))(qkv, qkv, qkv, gates, gt3)
    return outs, (got[0] if got else None)


def _gdn_bwd(qkv, gates, gt3, s_all, t_all, do, nh, rider=None):
    lp = qkv.shape[0]
    nc = lp // CHUNK
    w = nh * HEAD_DIM
    rev = lambda n: nc - 1 - n

    def body(start, finish, q_ref, k_ref, v_ref, g_ref, gt_ref, s_ref, t_ref, do_ref, dq_ref, dk_ref, dv_ref, dg_ref, ds_scr):
        @pl.when(pl.program_id(0) == 0)
        def _():
            start()
            ds_scr[...] = jnp.zeros_like(ds_scr)
        q, k, v, beta, gc, gr = _gdn_chunk_inputs(q_ref, k_ref, v_ref, g_ref[...], gt_ref[0], nh)
        c = _gdn_chunk(q, k, v, beta, gc, gr, t_ref[0])
        s = s_ref[0]
        dsn = ds_scr[...]
        dout = _heads(do_ref, nh)
        v_new = c["u"] - _bdot3(c["w"], s)
        dq_dec = _bdot3(dout, s, "nt")
        daqk = jnp.where(c["causal"], _bdot3(dout, v_new, "nt"), 0.0)
        dv_new = _bdot3(c["aqk"], dout, "tn") + _bdot3(c["k_dec"], dsn)
        dk_dec = _bdot3(v_new, dsn, "nt")
        ddecay = jnp.sum(jnp.sum(dsn * s, axis=2, keepdims=True), axis=1, keepdims=True)
        dw = -_bdot3(dv_new, s, "nt")
        ds_scr[...] = _bdot3(c["q_dec"], dout, "tn") + c["decay"] * dsn - _bdot3(c["w"], dv_new, "tn")
        duw = jnp.concatenate([dv_new, dw], axis=2)
        dt = _bdot3(duw, jnp.concatenate([c["bv"], c["bk"]], axis=2), "nt")
        dbvk = _bdot3(c["t"], duw, "tn")
        dbv, dbk = dbvk[:, :, :HEAD_DIM], dbvk[:, :, HEAD_DIM:]
        da = jnp.where(c["strict"], -_bdot3(_bdot3(c["t"], dt, "tn"), c["t"], "nt"), 0.0)
        dkk = da * beta * c["dm"]
        dqk = daqk * c["dm"]
        e = da * c["a"] + daqk * c["aqk"]
        dq = dq_dec * c["eg"] + _bdot3(dqk, k)
        dk = (dk_dec * c["ekd"] + _bdot3(dkk, k) + _bdot3(dkk, k, "tn") + _bdot3(dqk, q, "tn")
              + (beta * c["eg"]) * dbk)
        dv = beta * dbv
        rs = lambda x: jnp.sum(x, axis=2, keepdims=True)
        dbeta = rs(dbv * v) + c["eg"] * rs(dbk * k) + rs(da * c["kk"] * c["dm"])
        kd_term = rs(dk_dec * c["k_dec"])
        eh, ew = _split(e)
        ones = jnp.ones((nh, CHUNK, LANES), BF16)
        col_sums = (_dot3(eh, ones, "tn") + _dot3(ew, ones, "tn"))[:, :, 0:1]
        dg_cum = rs(dq_dec * c["q_dec"]) - kd_term + rs(dbk * c["bk"]) + rs(e) - col_sums
        last = jnp.sum(kd_term, axis=1, keepdims=True) + ddecay * c["decay"]
        dg_cum = dg_cum + jnp.where(_iota((1, CHUNK, 1), 1) == CHUNK - 1, last, 0.0)
        lane = _iota((CHUNK, LANES), 1)
        acc = jnp.zeros((CHUNK, LANES), F32)
        for h in range(nh):
            sl = slice(h * HEAD_DIM, (h + 1) * HEAD_DIM)
            dq_ref[:, sl] = dq[h]
            dk_ref[:, sl] = dk[h]
            dv_ref[:, sl] = dv[h]
            acc = acc + jnp.where(lane == h, dbeta[h], 0.0) + jnp.where(lane == nh + h, dg_cum[h], 0.0)
        triu = (_iota((CHUNK, CHUNK), 0) <= _iota((CHUNK, CHUNK), 1)).astype(F32)
        dg_ref[...] = jnp.where(lane < nh, acc, _hdot(triu, acc))
        pl.when(pl.program_id(0) == nc - 1)(finish)

    outs, got = _hosted_call(
        body, [rider], 8, 4, 1, grid=(nc,),
        in_specs=[pl.BlockSpec((CHUNK, w), lambda n: (rev(n), 0)), pl.BlockSpec((CHUNK, w), lambda n: (rev(n), 1)),
                  pl.BlockSpec((CHUNK, w), lambda n: (rev(n), 2)), pl.BlockSpec((CHUNK, LANES), lambda n: (rev(n), 0)),
                  pl.BlockSpec((1, LANES, CHUNK), lambda n: (rev(n), 0, 0)),
                  pl.BlockSpec((1, nh, HEAD_DIM, HEAD_DIM), lambda n: (rev(n), 0, 0, 0)),
                  pl.BlockSpec((1, nh, CHUNK, CHUNK), lambda n: (rev(n), 0, 0, 0)),
                  pl.BlockSpec((CHUNK, w), lambda n: (rev(n), 0))],
        out_specs=[pl.BlockSpec((CHUNK, w), lambda n: (rev(n), 0))] * 3 + [pl.BlockSpec((CHUNK, LANES), lambda n: (rev(n), 0))],
        out_shape=[jax.ShapeDtypeStruct((lp, w), F32)] * 3 + [jax.ShapeDtypeStruct((lp, LANES), F32)],
        scratch_shapes=[pltpu.VMEM((nh, HEAD_DIM, HEAD_DIM), F32)],
        name="gdn_bwd", compiler_params=_cp("arbitrary"))(qkv, qkv, qkv, gates, gt3, s_all, t_all, do)
    return outs, (got[0] if got else None)


def _merge_gdn(o_gdn, proj, norm_w, nh):
    lp = o_gdn.shape[0]

    def body(o_ref, z_ref, w_ref, m_ref):
        o = o_ref[...]
        z = z_ref[...]
        m_ref[...] = (o * _rms(o) * w_ref[...] * (z * _sigmoid(z))).astype(BF16)

    return pl.pallas_call(
        body, grid=(nh,),
        in_specs=[pl.BlockSpec((lp, LANES), lambda s: (0, s)), pl.BlockSpec((lp, LANES), lambda s: (0, 3 * nh + s)),
                  pl.BlockSpec((1, LANES), lambda s: (0, 0))],
        out_specs=pl.BlockSpec((lp, LANES), lambda s: (0, s)),
        out_shape=jax.ShapeDtypeStruct((lp, 2 * nh * HEAD_DIM), BF16), name="merge_gdn",
        compiler_params=_cp("parallel"))(o_gdn, proj, norm_w)


def _merge_gdn_bwd(o_gdn, proj, norm_w, dmerged, nh):
    lp = o_gdn.shape[0]

    def body(o_ref, z_ref, w_ref, dm_ref, do_ref, dz_ref, dw_ref):
        o = o_ref[...]
        r = _rms(o)
        xh = o * r
        silu, dsilu = _silu_and_grad(z_ref[...])
        dm = dm_ref[...]
        dn = dm * silu
        dz_ref[...] = (dm * (xh * w_ref[...]) * dsilu).astype(BF16)
        dnw = dn * w_ref[...]
        do_ref[...] = r * (dnw - xh * jnp.mean(dnw * xh, axis=-1, keepdims=True))

        @pl.when(pl.program_id(0) == 0)
        def _():
            dw_ref[...] = jnp.zeros_like(dw_ref)
        dw_ref[...] += jnp.sum(dn * xh, axis=0, keepdims=True)

    w = nh * HEAD_DIM
    return pl.pallas_call(
        body, grid=(nh,),
        in_specs=[pl.BlockSpec((lp, LANES), lambda s: (0, s)), pl.BlockSpec((lp, LANES), lambda s: (0, 3 * nh + s)),
                  pl.BlockSpec((1, LANES), lambda s: (0, 0)), pl.BlockSpec((lp, LANES), lambda s: (0, s))],
        out_specs=[pl.BlockSpec((lp, LANES), lambda s: (0, s)), pl.BlockSpec((lp, LANES), lambda s: (0, 3 * nh + s)),
                   pl.BlockSpec((1, LANES), lambda s: (0, 0))],
        out_shape=[jax.ShapeDtypeStruct((lp, w), F32), jax.ShapeDtypeStruct((lp, 8 * w + NARROW), BF16),
                   jax.ShapeDtypeStruct((1, LANES), F32)],
        name="merge_gdn_bwd", compiler_params=_cp("arbitrary"))(o_gdn, proj, norm_w, dmerged)


def _fox_prep(proj, qk_w, nh):
    lp = proj.shape[0]

    def body(x_ref, w_ref, o_ref):
        x = x_ref[...]
        o_ref[...] = x * _rms(x) * w_ref[0]

    return pl.pallas_call(
        body, grid=(2 * nh,),
        in_specs=[pl.BlockSpec((lp, LANES), lambda s: (0, 4 * nh + s)), pl.BlockSpec((1, 1, LANES), lambda s: (s // nh, 0, 0))],
        out_specs=pl.BlockSpec((lp, LANES), lambda s: (0, s)),
        out_shape=jax.ShapeDtypeStruct((lp, 2 * nh * HEAD_DIM), F32), name="fox_prep",
        compiler_params=_cp("parallel"))(proj, qk_w)


def _fox_prep_bwd(proj, qk_w, dq, dk, dproj, nh):
    lp = proj.shape[0]
    part = lambda p: pl.BlockSpec((lp, LANES), lambda s: (0, jnp.clip(s - p * nh, 0, nh - 1)))

    def body(x_ref, w_ref, dq_ref, dk_ref, _, dx_ref, dw_ref):
        x = x_ref[...]
        r = _rms(x)
        xh = x * r
        dy = jnp.where(pl.program_id(0) < nh, dq_ref[...], dk_ref[...])
        dyw = dy * w_ref[0]
        dx_ref[...] = (r * (dyw - xh * jnp.mean(dyw * xh, axis=-1, keepdims=True))).astype(BF16)

        @pl.when(pl.program_id(0) % nh == 0)
        def _():
            dw_ref[...] = jnp.zeros_like(dw_ref)
        dw_ref[0] += jnp.sum(dy * xh, axis=0, keepdims=True)

    strip = pl.BlockSpec((lp, LANES), lambda s: (0, 4 * nh + s))
    wsp = pl.BlockSpec((1, 1, LANES), lambda s: (s // nh, 0, 0))
    return pl.pallas_call(
        body, grid=(2 * nh,), in_specs=[strip, wsp, part(0), part(1), _ANY], out_specs=[strip, wsp],
        out_shape=[jax.ShapeDtypeStruct(dproj.shape, BF16), jax.ShapeDtypeStruct((2, 1, LANES), F32)],
        input_output_aliases={4: 0}, name="fox_prep_bwd", compiler_params=_cp("arbitrary"))(proj, qk_w, dq, dk, dproj)


def _fox_probs(q, k, gates, crow, h, i, nh, lse=None):
    kl = k.shape[0]
    lane = _iota((Q_BLOCK, LANES), 1)
    ct = jnp.sum(jnp.where(lane == 4 * nh + h, gates, 0.0), axis=1, keepdims=True)
    tq, kq = _iota((Q_BLOCK, Q_BLOCK), 0), _iota((Q_BLOCK, Q_BLOCK), 1)
    qs = q * (HEAD_DIM ** -0.5)
    if i == 0:
        s = _bdot(qs, k, "nt") + (ct - crow)
        s = jnp.where((kq <= tq) & ((kq >= PAD_ROWS) | (tq < PAD_ROWS)), s, NEG)
    else:
        crow = jnp.where(_iota((1, kl), 1) < PAD_ROWS, -NEG, crow)
        s = _bdot(qs, k, "nt") + (ct - crow)
        s = jnp.concatenate([s[:, :kl - Q_BLOCK], jnp.where(kq <= tq, s[:, kl - Q_BLOCK:], NEG)], axis=1)
    if lse is not None:
        return jnp.exp(s - lse)
    m = jnp.max(s, axis=1, keepdims=True)
    p = jnp.exp(s - m)
    tot = jnp.sum(p, axis=1, keepdims=True)
    return p / tot, m + jnp.log(tot)


FOX_HEADS_PER_STEP = 2


def _fox_specs(lp, nh):
    hw = FOX_HEADS_PER_STEP * LANES
    return [pl.BlockSpec((Q_BLOCK, hw), lambda g, i: (i, g)),
            pl.BlockSpec((lp, hw), lambda g, i: (0, nh // FOX_HEADS_PER_STEP + g)),
            pl.BlockSpec((lp, hw), lambda g, i: (0, 6 * nh // FOX_HEADS_PER_STEP + g)),
            pl.BlockSpec((Q_BLOCK, LANES), lambda g, i: (i, 0)),
            pl.BlockSpec((LANES, lp), lambda g, i: (0, 0))]


def _fox_fwd(qkn, proj, gates, gtf, nh):
    lp = qkn.shape[0]

    def body(q_ref, k_ref, v_ref, g_ref, gt_ref, o_ref, lse_ref):
        g, i = pl.program_id(0), pl.program_id(1)
        for j in range(lp // Q_BLOCK):
            @pl.when(i == j)
            def _(j=j):
                kl = (j + 1) * Q_BLOCK
                for hh in range(FOX_HEADS_PER_STEP):
                    h = FOX_HEADS_PER_STEP * g + hh
                    sl = slice(hh * LANES, (hh + 1) * LANES)
                    p, lse = _fox_probs(q_ref[:, sl], k_ref[0:kl, sl], g_ref[...], gt_ref[pl.ds(4 * nh + h, 1), :][:, 0:kl],
                                        h, j, nh)
                    o_ref[:, sl] = _bdot(p, v_ref[0:kl, sl])
                    lse_ref[:, sl] = jnp.broadcast_to(lse, (Q_BLOCK, LANES))

    blk = pl.BlockSpec((Q_BLOCK, FOX_HEADS_PER_STEP * LANES), lambda g, i: (i, g))
    return pl.pallas_call(
        body, grid=(nh // FOX_HEADS_PER_STEP, lp // Q_BLOCK), in_specs=_fox_specs(lp, nh), out_specs=[blk, blk],
        out_shape=[jax.ShapeDtypeStruct((lp, nh * HEAD_DIM), F32)] * 2, name="fox_fwd",
        compiler_params=_cp("parallel", "parallel"))(qkn, qkn, proj, gates, gtf)


def _fox_bwd(qkn, proj, gates, gtf, lse, do, dproj, nh):
    lp = qkn.shape[0]
    nq = lp // Q_BLOCK
    w = nh * HEAD_DIM
    scale = HEAD_DIM ** -0.5

    def body(q_ref, k_ref, v_ref, g_ref, gt_ref, lse_ref, do_ref, _, dq_ref, dk_ref, dc_ref, dv_ref, dv_scr):
        g, i = pl.program_id(0), pl.program_id(1)

        @pl.when(i == 0)
        def _():
            dk_ref[...] = jnp.zeros_like(dk_ref)
            dv_scr[...] = jnp.zeros_like(dv_scr)
            dc_ref[...] = jnp.zeros_like(dc_ref)
        for j in range(nq):
            @pl.when(i == j)
            def _(j=j):
                kl = (j + 1) * Q_BLOCK
                for hh in range(FOX_HEADS_PER_STEP):
                    h = FOX_HEADS_PER_STEP * g + hh
                    sl = slice(hh * LANES, (hh + 1) * LANES)
                    q, k = q_ref[:, sl], k_ref[0:kl, sl]
                    p = _fox_probs(q, k, g_ref[...], gt_ref[pl.ds(4 * nh + h, 1), :][:, 0:kl], h, j, nh,
                                   lse_ref[:, sl][:, 0:1])
                    dout = do_ref[:, sl]
                    dp = _bdot(dout, v_ref[0:kl, sl], "nt")
                    ds = p * (dp - jnp.sum(p * dp, axis=1, keepdims=True))
                    dq_ref[:, sl] = _bdot(ds, k) * scale
                    dk_ref[0:kl, sl] += _bdot(ds, q * scale, "tn")
                    dv_scr[0:kl, sl] += _bdot(p, dout, "tn")
                    dc_ref[hh, :, 0:kl] -= jnp.sum(ds, axis=0, keepdims=True)

        @pl.when(i == nq - 1)
        def _():
            dv_ref[...] = dv_scr[...].astype(BF16)

    hw = FOX_HEADS_PER_STEP * LANES
    blk = pl.BlockSpec((Q_BLOCK, hw), lambda g, i: (i, g))
    col = pl.BlockSpec((lp, hw), lambda g, i: (0, g))
    return pl.pallas_call(
        body, grid=(nh // FOX_HEADS_PER_STEP, nq), in_specs=_fox_specs(lp, nh) + [blk, blk, _ANY],
        out_specs=[blk, col, pl.BlockSpec((FOX_HEADS_PER_STEP, 1, lp), lambda g, i: (g, 0, 0)),
                   pl.BlockSpec((lp, hw), lambda g, i: (0, 6 * nh // FOX_HEADS_PER_STEP + g))],
        out_shape=[jax.ShapeDtypeStruct((lp, w), F32)] * 2 + [jax.ShapeDtypeStruct((nh, 1, lp), F32),
                                                             jax.ShapeDtypeStruct(dproj.shape, BF16)],
        scratch_shapes=[pltpu.VMEM((lp, hw), F32)], input_output_aliases={7: 3},
        name="fox_bwd", compiler_params=_cp("parallel", "arbitrary"))(qkn, qkn, proj, gates, gtf, lse, do, dproj)


def _merge_fox(o_fox, proj, merged, nh):
    lp = o_fox.shape[0]

    def body(o_ref, z_ref, _, m_ref):
        z = z_ref[...]
        m_ref[...] = (o_ref[...] * (z * _sigmoid(z))).astype(BF16)

    return pl.pallas_call(
        body, grid=(nh,),
        in_specs=[pl.BlockSpec((lp, LANES), lambda s: (0, s)), pl.BlockSpec((lp, LANES), lambda s: (0, 7 * nh + s)), _ANY],
        out_specs=pl.BlockSpec((lp, LANES), lambda s: (0, nh + s)),
        out_shape=jax.ShapeDtypeStruct(merged.shape, BF16), input_output_aliases={2: 0}, name="merge_fox",
        compiler_params=_cp("parallel"))(o_fox, proj, merged)


def _merge_fox_bwd(o_fox, proj, dmerged, dproj, nh):
    lp = o_fox.shape[0]

    def body(o_ref, z_ref, dm_ref, _, do_ref, dz_ref):
        silu, dsilu = _silu_and_grad(z_ref[...])
        dm = dm_ref[...]
        do_ref[...] = dm * silu
        dz_ref[...] = (dm * o_ref[...] * dsilu).astype(BF16)

    w = nh * HEAD_DIM
    return pl.pallas_call(
        body, grid=(nh,),
        in_specs=[pl.BlockSpec((lp, LANES), lambda s: (0, s)), pl.BlockSpec((lp, LANES), lambda s: (0, 7 * nh + s)),
                  pl.BlockSpec((lp, LANES), lambda s: (0, nh + s)), _ANY],
        out_specs=[pl.BlockSpec((lp, LANES), lambda s: (0, s)), pl.BlockSpec((lp, LANES), lambda s: (0, 7 * nh + s))],
        out_shape=[jax.ShapeDtypeStruct((lp, w), F32), jax.ShapeDtypeStruct(dproj.shape, BF16)],
        input_output_aliases={3: 1}, name="merge_fox_bwd", compiler_params=_cp("parallel"))(o_fox, proj, dmerged, dproj)


def _post(out, x, target, post_w):
    lp, d = out.shape

    def body(o_ref, x_ref, t_ref, w_ref, dy_ref, do_ref, loss_ref, dw_ref):
        i = pl.program_id(0)

        @pl.when(i == 0)
        def _():
            loss_ref[...] = jnp.zeros_like(loss_ref)
            dw_ref[...] = jnp.zeros_like(dw_ref)
        o = o_ref[...]
        r = _rms(o)
        nrm = o * r
        err = jnp.where(i > 0, x_ref[...] + nrm * w_ref[...] - t_ref[...], 0.0)
        loss_ref[0:1, :] += 0.5 * jnp.sum(jnp.sum(err * err, axis=1, keepdims=True), axis=0, keepdims=True) / d
        dy = err / d
        dy_ref[...] = dy
        dw_ref[...] += jnp.sum(dy * nrm, axis=0, keepdims=True)
        dyw = dy * w_ref[...]
        do_ref[...] = (r * (dyw - nrm * jnp.mean(dyw * nrm, axis=-1, keepdims=True))).astype(BF16)

    row = pl.BlockSpec((Q_BLOCK, d), lambda i: (i, 0))
    vec = pl.BlockSpec((1, d), lambda i: (0, 0))
    return pl.pallas_call(
        body, grid=(lp // Q_BLOCK,), in_specs=[row, _x_rows(d), _x_rows(d), vec],
        out_specs=[_x_rows(d), row, pl.BlockSpec((8, LANES), lambda i: (0, 0)), vec],
        out_shape=[jax.ShapeDtypeStruct(x.shape, F32), jax.ShapeDtypeStruct((lp, d), BF16),
                   jax.ShapeDtypeStruct((8, LANES), F32), jax.ShapeDtypeStruct((1, d), F32)],
        name="post", compiler_params=_cp("arbitrary"))(out, x, target, post_w)


def _prenorm_bwd(dxn, x, meta, w, dy, rider=None):
    seq, d = x.shape
    lp = seq + Q_BLOCK

    def body(start, finish, dx_ref, x_ref, m_ref, w_ref, dy_ref, gx_ref, gm_ref, dw_ref):
        i = pl.program_id(0)
        pl.when(i == 0)(start)
        h = _h_tile(i, x_ref, m_ref)
        r = _rms(h)
        xh = h * r
        dxn_ = dx_ref[...]
        dxw = dxn_ * w_ref[...]
        dh = jnp.where(i > 0, dy_ref[...], 0.0) + r * (dxw - xh * jnp.mean(dxw * xh, axis=-1, keepdims=True))
        gx_ref[...] = dh

        @pl.when(i == 0)
        def _():
            dw_ref[...] = jnp.zeros_like(dw_ref)
            gm_ref[...] = dh[PAD_ROWS:, :]
        dw_ref[...] += jnp.sum(dxn_ * xh, axis=0, keepdims=True)
        pl.when(i == lp // Q_BLOCK - 1)(finish)

    vec = pl.BlockSpec((1, d), lambda i: (0, 0))
    met = pl.BlockSpec((N_META, d), lambda i: (0, 0))
    outs, got = _hosted_call(
        body, [rider], 5, 3, 0, grid=(lp // Q_BLOCK,),
        in_specs=[pl.BlockSpec((Q_BLOCK, d), lambda i: (i, 0)), _x_rows(d), met, vec, _x_rows(d)],
        out_specs=[_x_rows(d), met, vec],
        out_shape=[jax.ShapeDtypeStruct((seq, d), F32), jax.ShapeDtypeStruct((N_META, d), F32),
                   jax.ShapeDtypeStruct((1, d), F32)],
        name="prenorm_bwd", compiler_params=_cp("arbitrary"))(dxn, x, meta, w, dy)
    return outs, (got[0] if got else None)


def _layer_grads(x, target, meta, pre_w, wfull, conv_wt, a_log, dt_bias, gdn_norm_w, fq_w, fk_w, f_bias, w_out, post_w,
                 late_weights=None, w_out_grads=None):
    nh = a_log.shape[1]
    zpad = jnp.zeros((1, LANES - 3 * nh), F32)
    bias_row = jnp.concatenate([jnp.zeros((1, nh), F32), dt_bias, f_bias, zpad], axis=1)
    nega_row = jnp.concatenate([jnp.zeros((1, nh), F32), -jnp.exp(a_log), jnp.zeros((1, nh), F32), zpad], axis=1)
    qk_w = jnp.stack([fq_w, fk_w])

    if isinstance(wfull, tuple):
        rider, meta_at, project = wfull
        xn, got = _prenorm_gathering(x, pre_w, rider, meta_at)
        proj, wfull, conv_wt, meta = project(xn, got)
    else:
        xn = _prenorm(x, meta, pre_w)
        proj = _matmul(xn, wfull, "nn", MM_TILE, F32, "proj")
    qkv = _gdn_prep(proj, conv_wt, nh)
    gates, gt3, gtf = _gates(proj, bias_row, nega_row, nh)
    (o_gdn, s_all, t_all), got = _gdn_fwd(qkv, gates, gt3, nh, None if late_weights is None else late_weights[0])
    if late_weights is not None:
        w_out = late_weights[1](got)
    qkn = _fox_prep(proj, qk_w, nh)
    o_fox, fox_lse = _fox_fwd(qkn, proj, gates, gtf, nh)
    merged = _merge_fox(o_fox, proj, _merge_gdn(o_gdn, proj, gdn_norm_w, nh), nh)
    out = _matmul(merged, w_out, "nn", 4 * LANES, F32, "out_proj")
    dy, dout, loss_blk, dpost_w = _post(out, x, target, post_w)

    dw_out = _matmul(merged, dout, "tn", 4 * LANES, BF16, "dw_out")
    if w_out_grads is None:
        dmerged, gdn_rider = _matmul(dout, w_out, "nt", 4 * LANES, F32, "dmerged"), None
    else:
        dmerged, got = _matmul(dout, w_out, "nt", 4 * LANES, F32, "dmerged", w_out_grads[0](dw_out))
        gdn_rider = w_out_grads[1](dw_out, got)
    do_gdn, dproj, dgdn_norm_w = _merge_gdn_bwd(o_gdn, proj, gdn_norm_w, dmerged, nh)
    do_fox, dproj = _merge_fox_bwd(o_fox, proj, dmerged, dproj, nh)
    dqn, dkn, dc_t, dproj = _fox_bwd(qkn, proj, gates, gtf, fox_lse, do_fox, dproj, nh)
    dproj, dqk_w = _fox_prep_bwd(proj, qk_w, dqn, dkn, dproj, nh)
    (dgq, dgk, dgv, dgate), w_out_parts = _gdn_bwd(qkv, gates, gt3, s_all, t_all, do_gdn, nh, gdn_rider)
    dproj, dconv_wt = _gdn_prep_bwd(proj, conv_wt, dgq, dgk, dgv, dproj, nh)
    dc_rows = jnp.pad(dc_t.reshape(nh, -1), ((2 * nh, LANES - 3 * nh), (0, 0)))
    dproj, gate_sums = _gates_bwd(proj, bias_row, nega_row, gates, dgate, dc_rows, dproj, nh)
    return dict(
        loss=loss_blk[0:1, 0:1], dy=dy, xn=xn, dproj=dproj, post_w=dpost_w,
        conv_wt=dconv_wt, a_log=gate_sums[1:2, nh:2 * nh], dt_bias=gate_sums[0:1, nh:2 * nh],
        gdn_norm_w=dgdn_norm_w, fq_w=dqk_w[0], fk_w=dqk_w[1], f_bias=gate_sums[0:1, 2 * nh:3 * nh], w_out=dw_out,
        w_out_parts=w_out_parts, wfull=wfull, meta=meta)


def _cast_bf16(a, tr, name):
    r, c = a.shape

    def body(a_ref, o_ref):
        o_ref[...] = a_ref[...].astype(BF16)

    return pl.pallas_call(
        body, grid=(r // tr,), in_specs=[pl.BlockSpec((tr, c), lambda i: (i, 0))],
        out_specs=pl.BlockSpec((tr, c), lambda i: (i, 0)), out_shape=jax.ShapeDtypeStruct((r, c), BF16),
        name=name, compiler_params=_cp("parallel"))(a)


def _column_major(a):
    return jnp.transpose(a, (2, 0, 1))


def _cast_bf16_column_major(a3, pieces, name):
    _, r, c = a3.shape
    rows = r // pieces

    def body(a_ref, *o_refs):
        t = a_ref[...].reshape(LANES, r).T.astype(BF16)
        for k, o_ref in enumerate(o_refs):
            o_ref[...] = t[k * rows:(k + 1) * rows]

    return pl.pallas_call(
        body, grid=(pl.cdiv(c, LANES),), in_specs=[pl.BlockSpec((LANES, 1, r), lambda i: (i, 0, 0))],
        out_specs=[pl.BlockSpec((rows, LANES), lambda i: (0, i))] * pieces,
        out_shape=[jax.ShapeDtypeStruct((rows, c), BF16)] * pieces, name=name, compiler_params=_cp("parallel"))(_column_major(a3))


def _adamw_column_major(w3, parts, m3, v3, name):
    _, r, c = w3.shape
    n_parts = parts[0].shape[0]

    def body(w_ref, *refs):
        p_refs, (m_ref, v_ref, g_ref, d_ref, nm_ref, nv_ref) = refs[:len(parts)], refs[len(parts):]
        sums = []
        for p_ref in p_refs:
            g = p_ref[0].astype(F32)
            for s in range(1, n_parts):
                g = g + p_ref[s].astype(F32)
            sums.append(g)
        g = jnp.concatenate(sums, axis=0).T
        flat = lambda ref: ref[...].reshape(LANES, r)
        m_new = ADAM_B1 * flat(m_ref) + (1.0 - ADAM_B1) * g
        v_new = ADAM_B2 * flat(v_ref) + (1.0 - ADAM_B2) * (g * g)
        m_hat = m_new / (1.0 - ADAM_B1 ** ADAM_STEP)
        v_hat = v_new / (1.0 - ADAM_B2 ** ADAM_STEP)
        delta = -ADAM_LR * (m_hat / (jnp.sqrt(v_hat) + ADAM_EPS) + ADAM_WD * flat(w_ref))
        for ref, val in ((g_ref, g), (d_ref, delta), (nm_ref, m_new), (nv_ref, v_new)):
            ref[...] = val.reshape(LANES, 1, r)

    blk = pl.BlockSpec((LANES, 1, r), lambda i: (i, 0, 0))
    outs = pl.pallas_call(
        body, grid=(pl.cdiv(c, LANES),),
        in_specs=[blk] + [pl.BlockSpec((n_parts, p.shape[1], LANES), lambda i: (0, 0, i)) for p in parts] + [blk, blk],
        out_specs=[blk] * 4, out_shape=[jax.ShapeDtypeStruct((c, 1, r), F32)] * 4, name=name,
        compiler_params=_cp("parallel"))(_column_major(w3), *parts, _column_major(m3), _column_major(v3))
    return [jnp.transpose(o, (1, 2, 0)) for o in outs]


def _gather_copies(ins, outs, send_sems, recv_sems, local_sems):
    n = len(ins)
    x, y, c = lax.axis_index("x"), lax.axis_index("y"), lax.axis_index("c")
    me, sibling = (x, y, c), (x, y, 1 - c)
    xn, yn, dg = (1 - x, y), (x, 1 - y), (1 - x, 1 - y)

    def copy(a, k, block, to, src=None):
        px, py, pc = block
        rows = outs[a].at[4 * px + 2 * py + pc]
        return pltpu.make_async_remote_copy(
            src_ref=rows if src is None else src, dst_ref=rows, send_sem=send_sems.at[a, k],
            recv_sem=recv_sems.at[a, k], device_id=to, device_id_type=_MESH)

    local = [pltpu.make_async_copy(ins[a], outs[a].at[4 * x + 2 * y + c], local_sems.at[a]) for a in range(n)]
    own = [cp for a in range(n) for cp in (copy(a, 0, me, sibling, src=ins[a]), copy(a, 1, me, (*xn, c), src=ins[a]),
                                           copy(a, 2, me, (*yn, c), src=ins[a]))]

    def start():
        for cp in local + own:
            cp.start()

    def finish():
        for a in range(n):
            @pl.when(c == 1)
            def _(a=a):
                copy(a, 1, (*xn, c), me).wait_recv()
                copy(a, 3, (*xn, c), (*yn, c)).start()

            @pl.when(c == 0)
            def _(a=a):
                copy(a, 2, (*yn, c), me).wait_recv()
                copy(a, 3, (*yn, c), (*xn, c)).start()
        for a in range(n):
            pl.when(c == 0)(copy(a, 1, (*xn, c), me).wait_recv)
            copy(a, 4, (*xn, c), sibling).start()
            pl.when(c == 1)(copy(a, 2, (*yn, c), me).wait_recv)
            copy(a, 5, (*yn, c), sibling).start()
        for a in range(n):
            copy(a, 3, (*dg, c), me).wait_recv()
            copy(a, 6, (*dg, c), sibling).start()
        for a in range(n):
            copy(a, 0, sibling, me).wait_recv()
            for k, chip in ((4, xn), (5, yn), (6, dg)):
                copy(a, k, (*chip, 1 - c), me).wait_recv()
                copy(a, k, (*chip, c), sibling).wait_send()
            copy(a, 3, (*xn, c), (*yn, c)).wait_send()
        for cp in own:
            cp.wait_send()
        for cp in local:
            cp.wait()

    return start, finish


def _gather_scratch(n):
    return [pltpu.SemaphoreType.DMA((n, N_DEV - 1)), pltpu.SemaphoreType.DMA((n, N_DEV - 1)), pltpu.SemaphoreType.DMA((n,))]


def _gather_rider(arrays):
    return _Rider(list(arrays), [jax.ShapeDtypeStruct((N_DEV,) + a.shape, a.dtype) for a in arrays],
                  _gather_scratch(len(arrays)), {}, lambda ins, outs, scratch: _gather_copies(ins, outs, *scratch))


def _all_gather(arrays, name):
    n = len(arrays)

    def body(*refs):
        start, finish = _gather_copies(refs[:n], refs[n:2 * n], *refs[2 * n:])
        start()
        finish()

    return pl.pallas_call(
        body, in_specs=[_ANY] * n, out_specs=[_ANY] * n,
        out_shape=[jax.ShapeDtypeStruct((N_DEV,) + a.shape, a.dtype) for a in arrays],
        scratch_shapes=_gather_scratch(n), name=name)(*arrays)


SLAB = 10 * LANES


def _slab_start(blk, nh, cols):
    in_second_half = blk >= N_DEV // 2
    shift = (2 * nh if in_second_half else 0) if isinstance(blk, int) else jnp.where(in_second_half, 2 * nh, 0)
    return (blk * cols - shift) // LANES * LANES


def _pair_rider(dw_rows=None, parts=None, nh=None, after=None):
    if dw_rows is not None:
        r, full = dw_rows.shape
        cols = (full - NARROW + 3 * nh) // N_DEV
        out_shapes = [jax.ShapeDtypeStruct((N_CHIP, r, SLAB), dw_rows.dtype), jax.ShapeDtypeStruct((r, NARROW), dw_rows.dtype)]
    else:
        out_shapes = [jax.ShapeDtypeStruct((N_CHIP,) + parts.shape[1:], parts.dtype)]

    def make(ins, outs, scratch):
        send_sems, recv_sems = scratch
        x, y, c = lax.axis_index("x"), lax.axis_index("y"), lax.axis_index("c")
        kw = lambda k: dict(send_sem=send_sems.at[k], recv_sem=recv_sems.at[k], device_id=(x, y, 1 - c), device_id_type=_MESH)
        copies = []
        for q in range(N_CHIP):
            if dw_rows is not None:
                first = pl.multiple_of(_slab_start(2 * q + 1 - c, nh, cols), LANES)
                copies.append(pltpu.make_async_remote_copy(src_ref=ins[0].at[:, pl.ds(first, SLAB)], dst_ref=outs[0].at[q], **kw(q)))
            else:
                copies.append(pltpu.make_async_remote_copy(src_ref=ins[0].at[2 * q + 1 - c], dst_ref=outs[0].at[q], **kw(q)))
        if dw_rows is not None:
            copies.append(pltpu.make_async_remote_copy(src_ref=ins[0].at[:, pl.ds(full - NARROW, NARROW)], dst_ref=outs[1],
                                                       **kw(N_CHIP)))

        def start():
            for cp in copies:
                cp.start()

        def finish():
            for cp in copies:
                cp.wait()

        return start, finish

    return _Rider([dw_rows if dw_rows is not None else parts] + ([] if after is None else [after]), out_shapes,
                  [pltpu.SemaphoreType.DMA((N_CHIP + 1,)), pltpu.SemaphoreType.DMA((N_CHIP + 1,))], {}, make)


def _relayout_pair_sum(dwfull, got_slabs, got_tail, core, nh, tr, name):
    d, full = dwfull.shape
    w = nh * HEAD_DIM
    cols = (8 * w + 3 * nh) // N_DEV
    segs = _native_segments(nh)

    def block(f_ref, s_ref, t_ref, q, blk):
        st = _slab_start(blk, nh, cols)
        wide = f_ref[:, st:st + SLAB].astype(F32) + s_ref[q].astype(F32)
        tail = f_ref[:, 8 * w:].astype(F32) + t_ref[...].astype(F32)
        pieces = []
        for s0, s1, t0 in segs:
            lo, hi = max(s0, blk * cols), min(s1, (blk + 1) * cols)
            if lo < hi:
                at = t0 + lo - s0
                pieces.append(tail[:, at - 8 * w:at - 8 * w + hi - lo] if at >= 8 * w else wide[:, at - st:at - st + hi - lo])
        return (pieces[0] if len(pieces) == 1 else jnp.concatenate(pieces, axis=1)).astype(dwfull.dtype)

    def body(core_ref, f_ref, s_ref, t_ref, o_ref):
        for parity in range(2):
            @pl.when(core_ref[0] == parity)
            def _(parity=parity):
                for q in range(N_CHIP):
                    o_ref[q] = block(f_ref, s_ref, t_ref, q, 2 * q + parity)

    return pl.pallas_call(
        body,
        grid_spec=pltpu.PrefetchScalarGridSpec(
            num_scalar_prefetch=1, grid=(d // tr,),
            in_specs=[pl.BlockSpec((tr, full), lambda i, c_ref: (i, 0)), pl.BlockSpec((N_CHIP, tr, SLAB), lambda i, c_ref: (0, i, 0)),
                      pl.BlockSpec((tr, NARROW), lambda i, c_ref: (i, 0))],
            out_specs=pl.BlockSpec((N_CHIP, tr, cols), lambda i, c_ref: (0, i, 0))),
        out_shape=jax.ShapeDtypeStruct((N_CHIP, d, cols), dwfull.dtype), name=name,
        compiler_params=_cp("parallel"))(core, dwfull, got_slabs, got_tail)


def _pair_sum(parts, got, core, tr, name):
    _, r, c = parts.shape

    def body(core_ref, p_ref, g_ref, o_ref):
        o_ref[...] = (p_ref[...].astype(F32) + g_ref[...].astype(F32)).astype(o_ref.dtype)

    return pl.pallas_call(
        body,
        grid_spec=pltpu.PrefetchScalarGridSpec(
            num_scalar_prefetch=1, grid=(N_CHIP, r // tr),
            in_specs=[pl.BlockSpec((1, tr, c), lambda q, i, core_ref: (2 * q + core_ref[0], i, 0)),
                      pl.BlockSpec((1, tr, c), lambda q, i, core_ref: (q, i, 0))],
            out_specs=pl.BlockSpec((1, tr, c), lambda q, i, core_ref: (q, i, 0))),
        out_shape=jax.ShapeDtypeStruct((N_CHIP, r, c), parts.dtype), name=name,
        compiler_params=_cp("parallel", "parallel"))(core, parts, got)


def _native_segments(nh):
    w = nh * HEAD_DIM
    return [(0, 4 * w, 0), (4 * w, 4 * w + 2 * nh, 8 * w), (4 * w + 2 * nh, 8 * w + 2 * nh, 4 * w),
            (8 * w + 2 * nh, 8 * w + 3 * nh, 8 * w + 2 * nh)]


def _relayout_w_in(wg, nh, tr, name, rows_total=None, into=None):
    _, d, cols = wg.shape
    w = nh * HEAD_DIM
    rows_total = into.shape[0] if into is not None else rows_total or d
    first = (rows_total - d) // tr if into is not None else 0

    def native(ref, j0, j1):
        out = []
        while j0 < j1:
            blk = j0 // cols
            end = min(j1, (blk + 1) * cols)
            out.append(ref[blk, :, pl.ds(j0 - blk * cols, end - j0)])
            j0 = end
        return out

    def body(g_ref, *refs):
        o_ref = refs[-1]
        for cidx in range(8 * w // LANES):
            j0 = cidx * LANES + (0 if cidx * LANES < 4 * w else 2 * nh)
            pieces = native(g_ref, j0, j0 + LANES)
            o_ref[:, cidx * LANES:(cidx + 1) * LANES] = pieces[0] if len(pieces) == 1 else jnp.concatenate(pieces, axis=1)
        pieces = (native(g_ref, 4 * w, 4 * w + 2 * nh) + native(g_ref, 8 * w + 2 * nh, 8 * w + 3 * nh)
                  + [jnp.zeros((tr, NARROW - 3 * nh), wg.dtype)])
        o_ref[:, 8 * w:] = jnp.concatenate(pieces, axis=1)

    return pl.pallas_call(
        body, grid=(d // tr,), in_specs=[pl.BlockSpec((N_DEV, tr, cols), lambda i: (0, i, 0))] + [_ANY] * (into is not None),
        out_specs=pl.BlockSpec((tr, 8 * w + NARROW), lambda i: (first + i, 0)),
        out_shape=jax.ShapeDtypeStruct((rows_total, 8 * w + NARROW), wg.dtype),
        input_output_aliases={1: 0} if into is not None else {},
        name=name, compiler_params=_cp("parallel"))(wg, *([into] if into is not None else []))


def _adamw(w, parts, m, v, tr, name, after=None):
    r, c = w.shape
    n_parts = parts.shape[0]

    def body(w_ref, p_ref, m_ref, v_ref, *refs):
        g_ref, d_ref, nm_ref, nv_ref = refs[-5:-1] if after is not None else refs
        if after is not None:
            refs[-1][...] = jnp.zeros_like(refs[-1])
        g = p_ref[0].astype(F32)
        for s in range(1, n_parts):
            g = g + p_ref[s].astype(F32)
        m_new = ADAM_B1 * m_ref[...] + (1.0 - ADAM_B1) * g
        v_new = ADAM_B2 * v_ref[...] + (1.0 - ADAM_B2) * (g * g)
        m_hat = m_new / (1.0 - ADAM_B1 ** ADAM_STEP)
        v_hat = v_new / (1.0 - ADAM_B2 ** ADAM_STEP)
        g_ref[...] = g
        d_ref[...] = -ADAM_LR * (m_hat / (jnp.sqrt(v_hat) + ADAM_EPS) + ADAM_WD * w_ref[...])
        nm_ref[...] = m_new
        nv_ref[...] = v_new

    blk = pl.BlockSpec((tr, c), lambda i: (i, 0))
    extra = after is not None
    return pl.pallas_call(
        body, grid=(r // tr,),
        in_specs=[blk, pl.BlockSpec((n_parts, tr, c), lambda i: (0, i, 0)), blk, blk] + [_ANY] * extra,
        out_specs=[blk] * 4 + [pl.BlockSpec((8, LANES), lambda i: (0, 0))] * extra,
        out_shape=[jax.ShapeDtypeStruct((r, c), F32)] * 4 + [jax.ShapeDtypeStruct((8, LANES), F32)] * extra, name=name,
        compiler_params=_cp("arbitrary" if extra else "parallel"))(w, parts, m, v, *([after] if extra else []))


def _adamw_conv(w, gathered, dev, m, v, name):
    c, r = w.shape
    n_parts = gathered.shape[0]

    def body(dev_ref, w_ref, p_ref, m_ref, v_ref, g_ref, d_ref, nm_ref, nv_ref):
        g = p_ref[0].astype(F32)
        for s in range(1, n_parts):
            g = g + p_ref[s].astype(F32)
        m_new = ADAM_B1 * m_ref[...] + (1.0 - ADAM_B1) * g
        v_new = ADAM_B2 * v_ref[...] + (1.0 - ADAM_B2) * (g * g)
        m_hat = m_new / (1.0 - ADAM_B1 ** ADAM_STEP)
        v_hat = v_new / (1.0 - ADAM_B2 ** ADAM_STEP)
        g_ref[...] = g
        d_ref[...] = -ADAM_LR * (m_hat / (jnp.sqrt(v_hat) + ADAM_EPS) + ADAM_WD * w_ref[...])
        nm_ref[...] = m_new
        nv_ref[...] = v_new

    blk = pl.BlockSpec((c, r), lambda i, dev_ref: (0, 0))
    return pl.pallas_call(
        body,
        grid_spec=pltpu.PrefetchScalarGridSpec(
            num_scalar_prefetch=1, grid=(1,),
            in_specs=[blk, pl.BlockSpec((n_parts, c, r), lambda i, dev_ref: (0, 0, dev_ref[0])), blk, blk], out_specs=[blk] * 4),
        out_shape=[jax.ShapeDtypeStruct((c, r), F32)] * 4, name=name, compiler_params=_cp("arbitrary"))(dev, w, gathered, m, v)


def _pack_small(d, pre, post, a_log, dt_bias, f_bias, gdn_w, fq_w, fk_w, extra):
    row2 = jnp.concatenate([a_log, dt_bias, f_bias, gdn_w, fq_w, fk_w, extra], axis=1)
    row2 = jnp.pad(row2, ((0, 0), (0, d - row2.shape[1])))
    return jnp.concatenate([pre, post, row2, jnp.zeros((5, d), F32)], axis=0)


def _adamw_small(w, parts, m, v, nh, name):
    d = w.shape[1]
    n_parts = parts.shape[0]
    shapes = dict(pre=d, post=d, a_log=nh, dt_bias=nh, f_bias=nh, gdn_w=HEAD_DIM, fq_w=HEAD_DIM, fk_w=HEAD_DIM, extra=1)

    def body(w_ref, p_ref, m_ref, v_ref, *o_refs):
        g = p_ref[0]
        for s in range(1, n_parts):
            g = g + p_ref[s]
        m_new = ADAM_B1 * m_ref[...] + (1.0 - ADAM_B1) * g
        v_new = ADAM_B2 * v_ref[...] + (1.0 - ADAM_B2) * (g * g)
        m_hat = m_new / (1.0 - ADAM_B1 ** ADAM_STEP)
        v_hat = v_new / (1.0 - ADAM_B2 ** ADAM_STEP)
        delta = -ADAM_LR * (m_hat / (jnp.sqrt(v_hat) + ADAM_EPS) + ADAM_WD * w_ref[...])
        for k, val in enumerate((g, delta, m_new, v_new)):
            vectors = _unpack_small(val, nh)
            for j, key in enumerate(shapes):
                o_refs[k * len(shapes) + j][...] = vectors[key]

    full = lambda shape: pl.BlockSpec(shape, lambda i: (0,) * len(shape))
    outs = pl.pallas_call(
        body, grid=(1,), in_specs=[full(w.shape), full(parts.shape), full(w.shape), full(w.shape)],
        out_specs=[full((1, n)) for n in shapes.values()] * 4,
        out_shape=[jax.ShapeDtypeStruct((1, n), F32) for n in shapes.values()] * 4, name=name,
        compiler_params=_cp("arbitrary"))(w, parts, m, v)
    return [dict(zip(shapes, outs[k * len(shapes):(k + 1) * len(shapes)])) for k in range(4)]


def _unpack_small(p, nh):
    o = 3 * nh
    return dict(pre=p[0:1], post=p[1:2], a_log=p[2:3, 0:nh], dt_bias=p[2:3, nh:2 * nh], f_bias=p[2:3, 2 * nh:o],
                gdn_w=p[2:3, o:o + HEAD_DIM], fq_w=p[2:3, o + HEAD_DIM:o + 2 * HEAD_DIM],
                fk_w=p[2:3, o + 2 * HEAD_DIM:o + 3 * HEAD_DIM], extra=p[2:3, o + 3 * HEAD_DIM:o + 3 * HEAD_DIM + 1])


def kernel(x, meta_tokens, pre_norm_w, w_in, conv_w, a_log, dt_bias, gdn_norm_w, fox_q_norm_w, fox_k_norm_w, fox_f_bias, w_out, post_norm_w, loss_target, m_meta_tokens, m_pre_norm_w, m_w_in, m_conv_w, m_a_log, m_dt_bias, m_gdn_norm_w, m_fox_q_norm_w, m_fox_k_norm_w, m_fox_f_bias, m_w_out, m_post_norm_w, v_meta_tokens, v_pre_norm_w, v_w_in, v_conv_w, v_a_log, v_dt_bias, v_gdn_norm_w, v_fox_q_norm_w, v_fox_k_norm_w, v_fox_f_bias, v_w_out, v_post_norm_w):
    nh = a_log.shape[1]
    d = x.shape[-1]
    w = nh * HEAD_DIM
    zero = jnp.zeros((1, 1), F32)

    w_in_a, w_in_b = _cast_bf16_column_major(w_in, 2, "cast_w_in")

    def project(xn, got):
        wg, cg, mg = got
        half = (d // 2, 0), (d // 2, 1)
        wfull = _relayout_w_in(wg, nh, 256, "relayout_w_in_a", rows_total=d)
        proj, got = _matmul(xn, wfull, "nn", MM_TILE, F32, "proj_a", _gather_rider([w_in_b]), a_cols=half[0], b_rows=half[0])
        wfull = _relayout_w_in(got[0], nh, 256, "relayout_w_in_b", into=wfull)
        proj = _matmul(xn, wfull, "nn", MM_TILE, F32, "proj_b", a_cols=half[1], b_rows=half[1], acc=proj)
        return proj, wfull, cg.transpose(1, 0, 2).reshape(CONV_WIDTH, 3 * w), mg.transpose(1, 0, 2).reshape(N_META, d)

    late_weights = (_gather_rider([_cast_bf16(w_out[0], 256, "cast_w_out")]), lambda got: got[0].reshape(2 * w, d))
    core = lax.axis_index("c")
    dev = 4 * lax.axis_index("x") + 2 * lax.axis_index("y") + core
    core_arr = jnp.reshape(core, (1,)).astype(jnp.int32)

    out_parts = lambda dw_out: dw_out.reshape(N_DEV, 2 * w // N_DEV, d)
    g = _layer_grads(
        x[0], loss_target[0], None, pre_norm_w, (_gather_rider([w_in_a, conv_w[0].T, meta_tokens]), 2, project), None,
        a_log, dt_bias, gdn_norm_w,
        fox_q_norm_w, fox_k_norm_w, fox_f_bias, None, post_norm_w, late_weights=late_weights,
        w_out_grads=(lambda dw_out: _pair_rider(parts=out_parts(dw_out)),
                     lambda dw_out, got: _chip_rider(_pair_sum(out_parts(dw_out), got[0], core_arr, 256, "pair_sum_w_out"))))
    p_out = g["w_out_parts"][0]
    xn, dproj, wfull, meta_full = g["xn"], g["dproj"], g["wfull"], g["meta"]
    flights, token, dw, rider = [], None, None, None

    def exchange(dw, got, i):
        sums = _relayout_pair_sum(dw, got[0], got[1], core_arr, nh, 128, f"relayout_pair_sum_{i}")
        flight, token = _chip_exchange_start(sums, f"chip_exchange_start_{i}")
        flights.append(flight)
        return token

    for i, (index, parts) in enumerate(DW_IN_PIECES):
        res = _matmul(xn, dproj, "tn", MM_TILE, BF16, f"dw_in_{i}", rider, a_cols=(d // parts, index))
        if i:
            token = exchange(dw, res[1], i - 1)
        dw = res[0] if i else res
        rider = _pair_rider(dw_rows=dw, nh=nh, after=token)
    dxn, (got,) = _dxn(dproj, wfull, [rider], MM_TILE, "dxn")
    token = exchange(dw, got, len(DW_IN_PIECES) - 1)
    *r_out, token = _adamw(w_out[0], p_out, m_w_out[0], v_w_out[0], 64, "adamw_w_out", after=token)
    (grad_x, dmeta, dpre_w), _ = _prenorm_bwd(dxn, x[0], meta_full, pre_norm_w + token[0:1, 0:1], g["dy"])
    small = _pack_small(d, dpre_w, g["post_w"], g["a_log"], g["dt_bias"], g["f_bias"], g["gdn_norm_w"], g["fq_w"],
                        g["fk_w"], g["loss"])
    a_conv, a_meta, p_small = _all_gather([g["conv_wt"], dmeta, small], "gather_small_grads")
    p_meta = lax.dynamic_slice_in_dim(a_meta, dev * meta_tokens.shape[1], meta_tokens.shape[1], axis=2)

    r_conv = _adamw_conv(conv_w[0].T, a_conv, jnp.reshape(dev, (1,)).astype(jnp.int32), m_conv_w[0].T, v_conv_w[0].T,
                         "adamw_conv_w")
    r_conv = [o.T for o in r_conv]
    r_meta = _adamw(meta_tokens, p_meta, m_meta_tokens, v_meta_tokens, N_META, "adamw_meta")
    pk = lambda pre, post, a, dt, gw, fq, fk, fb: _pack_small(d, pre, post, a, dt, fb, gw, fq, fk, zero)
    sm = _adamw_small(
        pk(pre_norm_w, post_norm_w, a_log, dt_bias, gdn_norm_w, fox_q_norm_w, fox_k_norm_w, fox_f_bias), p_small,
        pk(m_pre_norm_w, m_post_norm_w, m_a_log, m_dt_bias, m_gdn_norm_w, m_fox_q_norm_w, m_fox_k_norm_w, m_fox_f_bias),
        pk(v_pre_norm_w, v_post_norm_w, v_a_log, v_dt_bias, v_gdn_norm_w, v_fox_q_norm_w, v_fox_k_norm_w, v_fox_f_bias),
        nh, "adamw_small")
    p_in = _chip_exchange_wait(flights, sm[0]["pre"], "chip_exchange_wait")
    r_in = _adamw_column_major(w_in, p_in, m_w_in, v_w_in, "adamw_w_in")

    outs = []
    for i in range(4):
        s = sm[i]
        outs += [r_meta[i], s["pre"], r_in[i], r_conv[i][None], s["a_log"], s["dt_bias"], s["gdn_w"], s["fq_w"],
                 s["fk_w"], s["f_bias"], r_out[i][None], s["post"]]
    return (sm[0]["extra"].reshape(()), grad_x[None], *outs)
```

```python
import jax
import jax.numpy as jnp
from jax import lax
from jax.experimental import pallas as pl
from jax.experimental.pallas import tpu as pltpu

F32, BF16 = jnp.float32, jnp.bfloat16
HEAD_DIM = 128
N_META = 16
CONV_WIDTH = 4
CHUNK = 128
Q_BLOCK = 128
LANES = 128
EPS = 1e-6
PAD_ROWS = Q_BLOCK - N_META
N_DEV = 8
N_CHIP = 4
VMEM_LIMIT = 56 * 1024 * 1024
NEG = -1e30
NARROW = 2 * LANES
MM_TILE = 6 * LANES
DW_IN_PIECES = ((0, 2), (2, 4), (3, 4))

ADAM_LR, ADAM_B1, ADAM_B2, ADAM_EPS, ADAM_WD, ADAM_STEP = 0.001, 0.9, 0.999, 1e-08, 0.01, 10

_DN = {"nn": (((1,), (0,)), ((), ())), "nt": (((1,), (1,)), ((), ())), "tn": (((0,), (0,)), ((), ()))}
_DN3 = {"nn": (((2,), (1,)), ((0,), (0,))), "nt": (((2,), (2,)), ((0,), (0,))), "tn": (((1,), (1,)), ((0,), (0,)))}
_ANY = pl.BlockSpec(memory_space=pl.ANY)
_MESH = pl.DeviceIdType.MESH


def _cp(*sem):
    return pltpu.CompilerParams(dimension_semantics=sem, vmem_limit_bytes=VMEM_LIMIT)


def _dot(a, b, dims="nn", prec=None):
    return lax.dot_general(a, b, _DN[dims], precision=prec, preferred_element_type=F32)


def _bdot(a, b, dims="nn"):
    return _dot(a.astype(BF16), b.astype(BF16), dims)


def _hdot(a, b, dims="nn"):
    return _dot(a, b, dims, prec=lax.Precision.HIGHEST)


def _dot3(a, b, dims="nn"):
    return lax.dot_general(a, b, _DN3[dims], preferred_element_type=F32)


def _bdot3(a, b, dims="nn"):
    return _dot3(a.astype(BF16), b.astype(BF16), dims)


def _split(a):
    hi = a.astype(BF16)
    return hi, (a - hi.astype(F32)).astype(BF16)


def _iota(shape, dim):
    return lax.broadcasted_iota(jnp.int32, shape, dim)


def _sigmoid(z):
    return 1.0 / (1.0 + jnp.exp(-z))


def _softplus(z):
    e = jnp.exp(-jnp.abs(z))
    u = 1.0 + e
    l1p = jnp.where(u == 1.0, e, jnp.log(u) * (e / jnp.where(u == 1.0, 1.0, u - 1.0)))
    return jnp.maximum(z, 0.0) + l1p


def _silu_and_grad(z):
    s = _sigmoid(z)
    return z * s, s * (1.0 + z * (1.0 - s))


def _rms(x):
    return lax.rsqrt(jnp.mean(x * x, axis=-1, keepdims=True) + EPS)


def _h_tile(i, x_ref, meta_ref):
    first = jnp.concatenate([jnp.zeros((PAD_ROWS, x_ref.shape[1]), F32), meta_ref[...]], axis=0)
    return jnp.where(i == 0, first, x_ref[...])


def _x_rows(d):
    return pl.BlockSpec((Q_BLOCK, d), lambda i: (jnp.maximum(i - 1, 0), 0))


def _prenorm(x, meta, w):
    seq, d = x.shape
    lp = seq + Q_BLOCK

    def body(x_ref, m_ref, w_ref, o_ref):
        h = _h_tile(pl.program_id(0), x_ref, m_ref)
        o_ref[...] = (h * _rms(h) * w_ref[...]).astype(BF16)

    return pl.pallas_call(
        body, grid=(lp // Q_BLOCK,),
        in_specs=[_x_rows(d), pl.BlockSpec((N_META, d), lambda i: (0, 0)), pl.BlockSpec((1, d), lambda i: (0, 0))],
        out_specs=pl.BlockSpec((Q_BLOCK, d), lambda i: (i, 0)),
        out_shape=jax.ShapeDtypeStruct((lp, d), BF16), name="prenorm", compiler_params=_cp("parallel"))(x, meta, w)


def _prenorm_gathering(x, w, rider, meta_at):
    seq, d = x.shape
    steps = seq // Q_BLOCK + 1

    def body(start, finish, x_ref, w_ref, o_ref, meta_buf, meta_sem):
        i = pl.program_id(0)
        pl.when(i == 0)(start)

        @pl.when(i < steps - 1)
        def _():
            h = x_ref[...]
            o_ref[...] = (h * _rms(h) * w_ref[...]).astype(BF16)

        @pl.when(i == steps - 1)
        def _():
            finish()
            cp = pltpu.make_async_copy(finish.results[0][meta_at], meta_buf, meta_sem)
            cp.start()
            cp.wait()
            h = jnp.concatenate([jnp.zeros((PAD_ROWS, d), F32), jnp.concatenate([meta_buf[s] for s in range(N_DEV)], axis=1)], axis=0)
            o_ref[...] = (h * _rms(h) * w_ref[...]).astype(BF16)

    (xn,), got = _hosted_call(
        body, [rider], 2, 1, 2, grid=(steps,),
        in_specs=[pl.BlockSpec((Q_BLOCK, d), lambda i: (jnp.minimum(i, steps - 2), 0)), pl.BlockSpec((1, d), lambda i: (0, 0))],
        out_specs=[pl.BlockSpec((Q_BLOCK, d), lambda i: ((i + 1) % steps, 0))],
        out_shape=[jax.ShapeDtypeStruct((seq + Q_BLOCK, d), BF16)],
        scratch_shapes=[pltpu.VMEM((N_DEV, N_META, d // N_DEV), F32), pltpu.SemaphoreType.DMA(())],
        name="prenorm", compiler_params=_cp("arbitrary"))(x, w)
    return xn, got[0]


def _tile(n, want):
    return max(t for t in range(LANES, want + 1, LANES) if n % t == 0)


class _Rider:
    def __init__(self, inputs, out_shapes, scratch, aliases, make):
        self.inputs, self.out_shapes, self.scratch, self.aliases, self.make = inputs, out_shapes, scratch, aliases, make


def _hosted_call(body, riders, n_in, n_out, n_scratch, *, in_specs, out_specs, out_shape, scratch_shapes=(), aliases=None,
                 **kw):
    riders = [r for r in riders if r is not None]
    r_in = [len(r.inputs) for r in riders]
    r_out = [len(r.out_shapes) for r in riders]
    r_scr = [len(r.scratch) for r in riders]
    al = dict(aliases or {})
    for k, r in enumerate(riders):
        al.update({n_in + sum(r_in[:k]) + i: n_out + sum(r_out[:k]) + o for i, o in r.aliases.items()})

    def full_body(*refs):
        ins, rest = refs[:n_in + sum(r_in)], refs[n_in + sum(r_in):]
        outs, scr = rest[:n_out + sum(r_out)], rest[n_out + sum(r_out):]
        hooks = [r.make(ins[n_in + sum(r_in[:k]):n_in + sum(r_in[:k + 1])], outs[n_out + sum(r_out[:k]):n_out + sum(r_out[:k + 1])],
                        scr[n_scratch + sum(r_scr[:k]):n_scratch + sum(r_scr[:k + 1])]) for k, r in enumerate(riders)]

        def start():
            for h in hooks:
                h[0]()

        def finish():
            for h in hooks:
                h[1]()

        finish.results = [outs[n_out + sum(r_out[:k]):n_out + sum(r_out[:k + 1])] for k in range(len(riders))]
        body(start, finish, *ins[:n_in], *outs[:n_out], *scr[:n_scratch])

    call = pl.pallas_call(
        full_body, in_specs=list(in_specs) + [_ANY] * sum(r_in), out_specs=list(out_specs) + [_ANY] * sum(r_out),
        out_shape=list(out_shape) + [s for r in riders for s in r.out_shapes],
        scratch_shapes=list(scratch_shapes) + [s for r in riders for s in r.scratch], input_output_aliases=al, **kw)

    def run(*args):
        res = call(*args, *[t for r in riders for t in r.inputs])
        return res[:n_out], [res[n_out + sum(r_out[:k]):n_out + sum(r_out[:k + 1])] for k in range(len(riders))]

    return run


def _matmul(a, b, dims, tn, out_dtype, name, rider=None, a_cols=None, b_rows=None, acc=None):
    a_shape = a.shape if a_cols is None else (a.shape[0], a_cols[0])
    a_index = 0 if a_cols is None else a_cols[1]
    m = a_shape[1] if dims == "tn" else a_shape[0]
    n = b.shape[0] if dims == "nt" else b.shape[1]
    kdim = b.shape[1] if dims == "nt" else b.shape[0] if b_rows is None else b_rows[0]
    b_index = 0 if b_rows is None else b_rows[1]
    tn = _tile(n, tn)
    steps = n // tn
    b_spec = pl.BlockSpec((tn, kdim), lambda j: (j, 0)) if dims == "nt" else pl.BlockSpec((kdim, tn), lambda j: (b_index, j))
    o_spec = pl.BlockSpec((m, tn), lambda j: (0, j))

    def body(start, finish, a_ref, b_ref, *refs):
        pl.when(pl.program_id(0) == 0)(start)
        prod = _dot(a_ref[...], b_ref[...], dims)
        refs[-1][...] = (prod if acc is None else prod + refs[0][...]).astype(out_dtype)
        pl.when(pl.program_id(0) == steps - 1)(finish)

    (out,), got = _hosted_call(
        body, [rider], 2 + (acc is not None), 1, 0, grid=(steps,),
        in_specs=[pl.BlockSpec(a_shape, lambda j: (0, a_index)), b_spec] + [o_spec] * (acc is not None),
        out_specs=[o_spec], out_shape=[jax.ShapeDtypeStruct((m, n), out_dtype)], aliases={2: 0} if acc is not None else None,
        name=name, compiler_params=_cp("parallel" if rider is None else "arbitrary"))(a, b, *([acc] if acc is not None else []))
    return out if rider is None else (out, got[0])


def _chip_rider(sums):
    def make(ins, outs, scratch):
        local, remote = _chip_exchange_copies(ins[0], outs[0], *scratch)

        def start():
            for cp in [local] + remote:
                cp.start()

        def finish():
            local.wait()
            for cp in remote:
                cp.wait_send()
                cp.wait_recv()

        return start, finish

    return _Rider([sums], [jax.ShapeDtypeStruct(sums.shape, sums.dtype)],
                  [pltpu.SemaphoreType.DMA((N_CHIP - 1,)), pltpu.SemaphoreType.DMA((N_CHIP - 1,)), pltpu.SemaphoreType.DMA((1,))],
                  {}, make)


_HBM = pl.BlockSpec(memory_space=pltpu.HBM)
_SEM = pl.BlockSpec(memory_space=pltpu.SEMAPHORE)


def _chip_exchange_copies(sums_ref, land_ref, send_sems, recv_sems, local_sem):
    x, y, core = lax.axis_index("x"), lax.axis_index("y"), lax.axis_index("c")
    mine = 2 * x + y
    local = pltpu.make_async_copy(sums_ref.at[mine], land_ref.at[mine], local_sem.at[0])
    remote = []
    for k in range(1, N_CHIP):
        px = 1 - x if k & 2 else x
        py = 1 - y if k & 1 else y
        remote.append(pltpu.make_async_remote_copy(
            src_ref=sums_ref.at[2 * px + py], dst_ref=land_ref.at[mine], send_sem=send_sems.at[k - 1],
            recv_sem=recv_sems.at[k - 1], device_id=(px, py, core), device_id_type=_MESH))
    return local, remote


def _chip_exchange_start(sums, name):
    def body(s_ref, send_sems, recv_sems, local_sem, s_thru, land_ref, token):
        local, remote = _chip_exchange_copies(s_ref, land_ref, send_sems, recv_sems, local_sem)
        for cp in [local] + remote:
            cp.start()
        token[...] = jnp.zeros_like(token)

    *flight, token = pl.pallas_call(
        body, name=name,
        out_shape=(pltpu.SemaphoreType.DMA((N_CHIP - 1,)), pltpu.SemaphoreType.DMA((N_CHIP - 1,)), pltpu.SemaphoreType.DMA((1,)),
                   pltpu.HBM(sums.shape, sums.dtype), pltpu.HBM(sums.shape, sums.dtype), jax.ShapeDtypeStruct((8, LANES), F32)),
        in_specs=(_HBM,), out_specs=(_SEM, _SEM, _SEM, _HBM, _HBM, pl.BlockSpec(memory_space=pltpu.VMEM)),
        input_output_aliases={0: 3},
        compiler_params=pltpu.CompilerParams(has_side_effects=pltpu.SideEffectType.DATAFLOW_SIDE_EFFECTING))(
            pltpu.with_memory_space_constraint(sums, pltpu.HBM))
    return flight, token


def _chip_exchange_wait(flights, after, name):
    n = len(flights)

    def body(*refs):
        for i in range(n):
            s_ref, land_ref = refs[2 * i:2 * i + 2]
            local, remote = _chip_exchange_copies(s_ref, land_ref, *refs[2 * n + 3 * i:2 * n + 3 * i + 3])
            local.wait()
            for cp in remote:
                cp.wait_send()
                cp.wait_recv()

    buffers = [b for f in flights for b in f[3:]]
    res = pl.pallas_call(
        body, name=name, out_shape=tuple(pltpu.HBM(b.shape, b.dtype) for b in buffers),
        in_specs=(_HBM,) * (2 * n) + (_SEM,) * (3 * n) + (_ANY,), out_specs=(_HBM,) * (2 * n),
        input_output_aliases={i: i for i in range(2 * n)},
        compiler_params=pltpu.CompilerParams(has_side_effects=pltpu.SideEffectType.DATAFLOW_SIDE_EFFECTING))(
            *buffers, *[s for f in flights for s in f[:3]], after)
    return res[1::2]


def _dxn(dproj, wfull, riders, tk, name):
    m, k = dproj.shape
    n = wfull.shape[0]
    tk = _tile(k, tk)
    steps = k // tk

    def body(start, finish, a_ref, b_ref, o_ref):
        j = pl.program_id(0)

        @pl.when(j == 0)
        def _():
            start()
            o_ref[...] = jnp.zeros_like(o_ref)
        o_ref[...] += _dot(a_ref[...], b_ref[...], "nt")
        pl.when(j == steps - 1)(finish)

    (dxn,), got = _hosted_call(
        body, riders, 2, 1, 0, grid=(steps,),
        in_specs=[pl.BlockSpec((m, tk), lambda j: (0, j)), pl.BlockSpec((n, tk), lambda j: (0, j))],
        out_specs=[pl.BlockSpec((m, n), lambda j: (0, 0))], out_shape=[jax.ShapeDtypeStruct((m, n), F32)],
        name=name, compiler_params=_cp("arbitrary"))(dproj, wfull)
    return dxn, got


def _conv_taps(x, w):
    c = x * w[CONV_WIDTH - 1:CONV_WIDTH, :]
    for j in range(CONV_WIDTH - 1):
        c = c + pltpu.roll(x, CONV_WIDTH - 1 - j, 0) * w[j:j + 1, :]
    return c


def _gdn_prep(proj, conv_wt, nh):
    lp = proj.shape[0]
    scale = HEAD_DIM ** -0.5

    def body(x_ref, w_ref, o_ref):
        which = pl.program_id(0) // nh
        c = _conv_taps(x_ref[...], w_ref[...])
        s = c * _sigmoid(c)
        r = lax.rsqrt(jnp.sum(s * s, axis=-1, keepdims=True) + EPS)
        f = jnp.where(which == 0, r * scale, jnp.where(which == 1, r, 1.0))
        o_ref[...] = jnp.where(_iota(s.shape, 0) >= PAD_ROWS, s * f, 0.0)

    return pl.pallas_call(
        body, grid=(3 * nh,),
        in_specs=[pl.BlockSpec((lp, LANES), lambda s: (0, s)), pl.BlockSpec((CONV_WIDTH, LANES), lambda s: (0, s))],
        out_specs=pl.BlockSpec((lp, LANES), lambda s: (0, s)),
        out_shape=jax.ShapeDtypeStruct((lp, 3 * nh * HEAD_DIM), F32), name="gdn_prep",
        compiler_params=_cp("parallel"))(proj, conv_wt)


def _gdn_prep_bwd(proj, conv_wt, dq, dk, dv, dproj, nh):
    lp = proj.shape[0]
    scale = HEAD_DIM ** -0.5
    part = lambda p: pl.BlockSpec((lp, LANES), lambda s: (0, jnp.clip(s - p * nh, 0, nh - 1)))

    def body(x_ref, w_ref, dq_ref, dk_ref, dv_ref, _, dx_ref, dw_ref):
        which = pl.program_id(0) // nh
        x = x_ref[...]
        w = w_ref[...]
        c = _conv_taps(x, w)
        sg = _sigmoid(c)
        s = c * sg
        r = lax.rsqrt(jnp.sum(s * s, axis=-1, keepdims=True) + EPS)
        dy = jnp.where(which == 0, dq_ref[...], jnp.where(which == 1, dk_ref[...], dv_ref[...]))
        dy = jnp.where(_iota(s.shape, 0) >= PAD_ROWS, dy, 0.0)
        y0 = s * r
        dy0 = dy * jnp.where(which == 0, scale, 1.0)
        ds_n = r * (dy0 - y0 * jnp.sum(dy0 * y0, axis=-1, keepdims=True))
        ds = jnp.where(which == 2, dy, ds_n)
        dc = ds * (sg * (1.0 + c * (1.0 - sg)))
        dx = dc * w[CONV_WIDTH - 1:CONV_WIDTH, :]
        rows = [jnp.sum(dc * x, axis=0, keepdims=True)]
        for j in range(CONV_WIDTH - 2, -1, -1):
            sh = CONV_WIDTH - 1 - j
            dx = dx + pltpu.roll(dc, lp - sh, 0) * w[j:j + 1, :]
            rows.insert(0, jnp.sum(dc * pltpu.roll(x, sh, 0), axis=0, keepdims=True))
        dx_ref[...] = dx.astype(BF16)
        dw_ref[...] = jnp.concatenate(rows, axis=0)

    strip = pl.BlockSpec((lp, LANES), lambda s: (0, s))
    taps = pl.BlockSpec((CONV_WIDTH, LANES), lambda s: (0, s))
    return pl.pallas_call(
        body, grid=(3 * nh,), in_specs=[strip, taps, part(0), part(1), part(2), _ANY], out_specs=[strip, taps],
        out_shape=[jax.ShapeDtypeStruct(dproj.shape, BF16), jax.ShapeDtypeStruct((CONV_WIDTH, 3 * nh * HEAD_DIM), F32)],
        input_output_aliases={5: 0}, name="gdn_prep_bwd", compiler_params=_cp("parallel"))(proj, conv_wt, dq, dk, dv, dproj)


def _gates(proj, bias_row, nega_row, nh):
    lp = proj.shape[0]
    nc = lp // CHUNK

    def body(p_ref, b_ref, a_ref, g_ref, gt3_ref, gtf_ref):
        lane = _iota((CHUNK, LANES), 1)
        tri = (_iota((CHUNK, CHUNK), 0) >= _iota((CHUNK, CHUNK), 1)).astype(F32)

        def step(n, carry):
            r0 = pl.multiple_of(n * CHUNK, CHUNK)
            z = p_ref[pl.ds(r0, CHUNK), :] + b_ref[...]
            base = jnp.where(lane < nh, _sigmoid(z),
                             jnp.where(lane < 2 * nh, a_ref[...] * _softplus(z),
                                       jnp.where(lane < 3 * nh, -_softplus(-z), 0.0)))
            base = jnp.where(r0 + _iota((CHUNK, LANES), 0) >= PAD_ROWS, base, 0.0)
            cs = _hdot(tri, base)
            run = jnp.where((lane >= 2 * nh) & (lane < 3 * nh), cs + carry, cs)
            sh = pltpu.roll(run, 2 * nh, 1)
            out = base + jnp.where((lane >= 3 * nh) & (lane < 5 * nh), sh, 0.0)
            g_ref[pl.ds(r0, CHUNK), :] = out
            gt3_ref[n] = out.T
            return carry + cs[CHUNK - 1:CHUNK, :]

        lax.fori_loop(0, nc, step, jnp.zeros((1, LANES), F32))
        gtf_ref[...] = g_ref[...].T

    vec = pl.BlockSpec((1, LANES), lambda i: (0, 0))
    return pl.pallas_call(
        body, grid=(1,), in_specs=[pl.BlockSpec((lp, LANES), lambda i: (0, 8 * nh)), vec, vec],
        out_specs=[pl.BlockSpec((lp, LANES), lambda i: (0, 0)), pl.BlockSpec((nc, LANES, CHUNK), lambda i: (0, 0, 0)),
                   pl.BlockSpec((LANES, lp), lambda i: (0, 0))],
        out_shape=[jax.ShapeDtypeStruct((lp, LANES), F32), jax.ShapeDtypeStruct((nc, LANES, CHUNK), F32),
                   jax.ShapeDtypeStruct((LANES, lp), F32)],
        name="gates", compiler_params=_cp("arbitrary"))(proj, bias_row, nega_row)


def _gates_bwd(proj, bias_row, nega_row, gates, dgate_gdn, dc_t, dproj, nh):
    lp = proj.shape[0]
    nc = lp // CHUNK

    def body(p_ref, b_ref, a_ref, g_ref, dg_ref, dc_ref, _, dz_ref, sm_ref, dct_scr):
        lane = _iota((CHUNK, LANES), 1)
        triu = (_iota((CHUNK, CHUNK), 0) <= _iota((CHUNK, CHUNK), 1)).astype(F32)
        dct_scr[...] = dc_ref[...].T
        sm_ref[...] = jnp.zeros_like(sm_ref)
        dz_ref[:, LANES:] = jnp.zeros((lp, NARROW - LANES), BF16)

        def step(i, carry):
            n = nc - 1 - i
            r0 = pl.multiple_of(n * CHUNK, CHUNK)
            z = p_ref[pl.ds(r0, CHUNK), :] + b_ref[...]
            gt = g_ref[pl.ds(r0, CHUNK), :]
            dgd = dg_ref[pl.ds(r0, CHUNK), :]
            dch = dct_scr[pl.ds(r0, CHUNK), :]
            rc = _hdot(triu, dch) + carry
            sg = _sigmoid(z)
            dz = jnp.where(lane < nh, dgd * sg * (1.0 - sg),
                           jnp.where(lane < 2 * nh, dgd * a_ref[...] * sg,
                                     jnp.where(lane < 3 * nh, rc * (1.0 - sg), 0.0)))
            dz = jnp.where(r0 + _iota((CHUNK, LANES), 0) >= PAD_ROWS, dz, 0.0)
            dz_ref[pl.ds(r0, CHUNK), 0:LANES] = dz.astype(BF16)
            sm_ref[0:1, :] += jnp.sum(dz, axis=0, keepdims=True)
            sm_ref[1:2, :] += jnp.sum(jnp.where((lane >= nh) & (lane < 2 * nh), dgd * gt, 0.0), axis=0, keepdims=True)
            return carry + jnp.sum(dch, axis=0, keepdims=True)

        lax.fori_loop(0, nc, step, jnp.zeros((1, LANES), F32))

    vec = pl.BlockSpec((1, LANES), lambda i: (0, 0))
    full = pl.BlockSpec((lp, LANES), lambda i: (0, 0))
    last = pl.BlockSpec((lp, LANES), lambda i: (0, 8 * nh))
    tail = pl.BlockSpec((lp, NARROW), lambda i: (0, 8 * nh * LANES // NARROW))
    return pl.pallas_call(
        body, grid=(1,), in_specs=[last, vec, vec, full, full, pl.BlockSpec((LANES, lp), lambda i: (0, 0)), _ANY],
        out_specs=[tail, pl.BlockSpec((8, LANES), lambda i: (0, 0))],
        out_shape=[jax.ShapeDtypeStruct(dproj.shape, BF16), jax.ShapeDtypeStruct((8, LANES), F32)],
        scratch_shapes=[pltpu.VMEM((lp, LANES), F32)], input_output_aliases={6: 0},
        name="gates_bwd", compiler_params=_cp("arbitrary"))(proj, bias_row, nega_row, gates, dgate_gdn, dc_t, dproj)


def _tri_inv(a):
    t = jnp.where(_iota(a.shape, 1) == _iota(a.shape, 2), 1.0, 0.0) - a
    p = a
    for _ in range(CHUNK.bit_length() - 2):
        ph, pw = _split(p)
        p = _dot3(ph, ph) + (_dot3(ph, pw) + _dot3(pw, ph))
        ph, pw = _split(p)
        th, tw = _split(t)
        t = t + (_dot3(th, ph) + (_dot3(th, pw) + _dot3(tw, ph)))
    return t


def _gdn_chunk(q, k, v, beta, gc, gr, t=None):
    ii, jj = _iota((1, CHUNK, CHUNK), 1), _iota((1, CHUNK, CHUNK), 2)
    causal, strict = ii >= jj, ii > jj
    dm = jnp.where(causal, jnp.exp(jnp.where(causal, gc - gr, 0.0)), 0.0)
    kk = _bdot3(k, k, "nt")
    a = jnp.where(strict, beta * kk * dm, 0.0)
    if t is None:
        t = _tri_inv(a)
    eg = jnp.exp(gc)
    glast = gc[:, CHUNK - 1:CHUNK, :]
    ekd = jnp.exp(glast - gc)
    bv = beta * v
    bk = (beta * eg) * k
    ub = _bdot3(t, jnp.concatenate([bv, bk], axis=2))
    qk = _bdot3(q, k, "nt")
    return dict(causal=causal, strict=strict, dm=dm, kk=kk, a=a, t=t, eg=eg, ekd=ekd, bv=bv, bk=bk,
                u=ub[:, :, :HEAD_DIM], w=ub[:, :, HEAD_DIM:], qk=qk, aqk=jnp.where(causal, qk * dm, 0.0),
                q_dec=q * eg, k_dec=k * ekd, decay=jnp.exp(glast))


def _heads(ref, nh):
    return jnp.stack([ref[:, h * HEAD_DIM:(h + 1) * HEAD_DIM] for h in range(nh)], axis=0)


def _gdn_chunk_inputs(q_ref, k_ref, v_ref, g, gt, nh):
    col = lambda o: jnp.stack([g[:, o + h:o + h + 1] for h in range(nh)], axis=0)
    gr = jnp.stack([gt[3 * nh + h:3 * nh + h + 1, :] for h in range(nh)], axis=0)
    return _heads(q_ref, nh), _heads(k_ref, nh), _heads(v_ref, nh), col(0), col(3 * nh), gr


def _gdn_fwd(qkv, gates, gt3, nh, rider=None):
    lp = qkv.shape[0]
    nc = lp // CHUNK
    w = nh * HEAD_DIM

    def body(start, finish, q_ref, k_ref, v_ref, g_ref, gt_ref, o_ref, sall_ref, tall_ref, s_scr):
        @pl.when(pl.program_id(0) == 0)
        def _():
            start()
            s_scr[...] = jnp.zeros_like(s_scr)
        c = _gdn_chunk(*_gdn_chunk_inputs(q_ref, k_ref, v_ref, g_ref[...], gt_ref[0], nh))
        s = s_scr[...]
        sall_ref[0] = s
        tall_ref[0] = c["t"]
        v_new = c["u"] - _bdot3(c["w"], s)
        o = _bdot3(c["q_dec"], s) + _bdot3(c["aqk"], v_new)
        s_scr[...] = s * c["decay"] + _bdot3(c["k_dec"], v_new, "tn")
        for h in range(nh):
            o_ref[:, h * HEAD_DIM:(h + 1) * HEAD_DIM] = o[h]
        pl.when(pl.program_id(0) == nc - 1)(finish)

    outs, got = _hosted_call(
        body, [rider], 5, 3, 1, grid=(nc,),
        in_specs=[pl.BlockSpec((CHUNK, w), lambda n: (n, 0)), pl.BlockSpec((CHUNK, w), lambda n: (n, 1)),
                  pl.BlockSpec((CHUNK, w), lambda n: (n, 2)), pl.BlockSpec((CHUNK, LANES), lambda n: (n, 0)),
                  pl.BlockSpec((1, LANES, CHUNK), lambda n: (n, 0, 0))],
        out_specs=[pl.BlockSpec((CHUNK, w), lambda n: (n, 0)),
                   pl.BlockSpec((1, nh, HEAD_DIM, HEAD_DIM), lambda n: (n, 0, 0, 0)),
                   pl.BlockSpec((1, nh, CHUNK, CHUNK), lambda n: (n, 0, 0, 0))],
        out_shape=[jax.ShapeDtypeStruct((lp, w), F32), jax.ShapeDtypeStruct((nc, nh, HEAD_DIM, HEAD_DIM), F32),
                   jax.ShapeDtypeStruct((nc, nh, CHUNK, CHUNK), F32)],
        scratch_shapes=[pltpu.VMEM((nh, HEAD_DIM, HEAD_DIM), F32)],
        name="gdn_fwd", compiler_params=_cp("arbitrary"))(qkv, qkv, qkv, gates, gt3)
    return outs, (got[0] if got else None)


def _gdn_bwd(qkv, gates, gt3, s_all, t_all, do, nh, rider=None):
    lp = qkv.shape[0]
    nc = lp // CHUNK
    w = nh * HEAD_DIM
    rev = lambda n: nc - 1 - n

    def body(start, finish, q_ref, k_ref, v_ref, g_ref, gt_ref, s_ref, t_ref, do_ref, dq_ref, dk_ref, dv_ref, dg_ref, ds_scr):
        @pl.when(pl.program_id(0) == 0)
        def _():
            start()
            ds_scr[...] = jnp.zeros_like(ds_scr)
        q, k, v, beta, gc, gr = _gdn_chunk_inputs(q_ref, k_ref, v_ref, g_ref[...], gt_ref[0], nh)
        c = _gdn_chunk(q, k, v, beta, gc, gr, t_ref[0])
        s = s_ref[0]
        dsn = ds_scr[...]
        dout = _heads(do_ref, nh)
        v_new = c["u"] - _bdot3(c["w"], s)
        dq_dec = _bdot3(dout, s, "nt")
        daqk = jnp.where(c["causal"], _bdot3(dout, v_new, "nt"), 0.0)
        dv_new = _bdot3(c["aqk"], dout, "tn") + _bdot3(c["k_dec"], dsn)
        dk_dec = _bdot3(v_new, dsn, "nt")
        ddecay = jnp.sum(jnp.sum(dsn * s, axis=2, keepdims=True), axis=1, keepdims=True)
        dw = -_bdot3(dv_new, s, "nt")
        ds_scr[...] = _bdot3(c["q_dec"], dout, "tn") + c["decay"] * dsn - _bdot3(c["w"], dv_new, "tn")
        duw = jnp.concatenate([dv_new, dw], axis=2)
        dt = _bdot3(duw, jnp.concatenate([c["bv"], c["bk"]], axis=2), "nt")
        dbvk = _bdot3(c["t"], duw, "tn")
        dbv, dbk = dbvk[:, :, :HEAD_DIM], dbvk[:, :, HEAD_DIM:]
        da = jnp.where(c["strict"], -_bdot3(_bdot3(c["t"], dt, "tn"), c["t"], "nt"), 0.0)
        dkk = da * beta * c["dm"]
        dqk = daqk * c["dm"]
        e = da * c["a"] + daqk * c["aqk"]
        dq = dq_dec * c["eg"] + _bdot3(dqk, k)
        dk = (dk_dec * c["ekd"] + _bdot3(dkk, k) + _bdot3(dkk, k, "tn") + _bdot3(dqk, q, "tn")
              + (beta * c["eg"]) * dbk)
        dv = beta * dbv
        rs = lambda x: jnp.sum(x, axis=2, keepdims=True)
        dbeta = rs(dbv * v) + c["eg"] * rs(dbk * k) + rs(da * c["kk"] * c["dm"])
        kd_term = rs(dk_dec * c["k_dec"])
        eh, ew = _split(e)
        ones = jnp.ones((nh, CHUNK, LANES), BF16)
        col_sums = (_dot3(eh, ones, "tn") + _dot3(ew, ones, "tn"))[:, :, 0:1]
        dg_cum = rs(dq_dec * c["q_dec"]) - kd_term + rs(dbk * c["bk"]) + rs(e) - col_sums
        last = jnp.sum(kd_term, axis=1, keepdims=True) + ddecay * c["decay"]
        dg_cum = dg_cum + jnp.where(_iota((1, CHUNK, 1), 1) == CHUNK - 1, last, 0.0)
        lane = _iota((CHUNK, LANES), 1)
        acc = jnp.zeros((CHUNK, LANES), F32)
        for h in range(nh):
            sl = slice(h * HEAD_DIM, (h + 1) * HEAD_DIM)
            dq_ref[:, sl] = dq[h]
            dk_ref[:, sl] = dk[h]
            dv_ref[:, sl] = dv[h]
            acc = acc + jnp.where(lane == h, dbeta[h], 0.0) + jnp.where(lane == nh + h, dg_cum[h], 0.0)
        triu = (_iota((CHUNK, CHUNK), 0) <= _iota((CHUNK, CHUNK), 1)).astype(F32)
        dg_ref[...] = jnp.where(lane < nh, acc, _hdot(triu, acc))
        pl.when(pl.program_id(0) == nc - 1)(finish)

    outs, got = _hosted_call(
        body, [rider], 8, 4, 1, grid=(nc,),
        in_specs=[pl.BlockSpec((CHUNK, w), lambda n: (rev(n), 0)), pl.BlockSpec((CHUNK, w), lambda n: (rev(n), 1)),
                  pl.BlockSpec((CHUNK, w), lambda n: (rev(n), 2)), pl.BlockSpec((CHUNK, LANES), lambda n: (rev(n), 0)),
                  pl.BlockSpec((1, LANES, CHUNK), lambda n: (rev(n), 0, 0)),
                  pl.BlockSpec((1, nh, HEAD_DIM, HEAD_DIM), lambda n: (rev(n), 0, 0, 0)),
                  pl.BlockSpec((1, nh, CHUNK, CHUNK), lambda n: (rev(n), 0, 0, 0)),
                  pl.BlockSpec((CHUNK, w), lambda n: (rev(n), 0))],
        out_specs=[pl.BlockSpec((CHUNK, w), lambda n: (rev(n), 0))] * 3 + [pl.BlockSpec((CHUNK, LANES), lambda n: (rev(n), 0))],
        out_shape=[jax.ShapeDtypeStruct((lp, w), F32)] * 3 + [jax.ShapeDtypeStruct((lp, LANES), F32)],
        scratch_shapes=[pltpu.VMEM((nh, HEAD_DIM, HEAD_DIM), F32)],
        name="gdn_bwd", compiler_params=_cp("arbitrary"))(qkv, qkv, qkv, gates, gt3, s_all, t_all, do)
    return outs, (got[0] if got else None)


def _merge_gdn(o_gdn, proj, norm_w, nh):
    lp = o_gdn.shape[0]

    def body(o_ref, z_ref, w_ref, m_ref):
        o = o_ref[...]
        z = z_ref[...]
        m_ref[...] = (o * _rms(o) * w_ref[...] * (z * _sigmoid(z))).astype(BF16)

    return pl.pallas_call(
        body, grid=(nh,),
        in_specs=[pl.BlockSpec((lp, LANES), lambda s: (0, s)), pl.BlockSpec((lp, LANES), lambda s: (0, 3 * nh + s)),
                  pl.BlockSpec((1, LANES), lambda s: (0, 0))],
        out_specs=pl.BlockSpec((lp, LANES), lambda s: (0, s)),
        out_shape=jax.ShapeDtypeStruct((lp, 2 * nh * HEAD_DIM), BF16), name="merge_gdn",
        compiler_params=_cp("parallel"))(o_gdn, proj, norm_w)


def _merge_gdn_bwd(o_gdn, proj, norm_w, dmerged, nh):
    lp = o_gdn.shape[0]

    def body(o_ref, z_ref, w_ref, dm_ref, do_ref, dz_ref, dw_ref):
        o = o_ref[...]
        r = _rms(o)
        xh = o * r
        silu, dsilu = _silu_and_grad(z_ref[...])
        dm = dm_ref[...]
        dn = dm * silu
        dz_ref[...] = (dm * (xh * w_ref[...]) * dsilu).astype(BF16)
        dnw = dn * w_ref[...]
        do_ref[...] = r * (dnw - xh * jnp.mean(dnw * xh, axis=-1, keepdims=True))

        @pl.when(pl.program_id(0) == 0)
        def _():
            dw_ref[...] = jnp.zeros_like(dw_ref)
        dw_ref[...] += jnp.sum(dn * xh, axis=0, keepdims=True)

    w = nh * HEAD_DIM
    return pl.pallas_call(
        body, grid=(nh,),
        in_specs=[pl.BlockSpec((lp, LANES), lambda s: (0, s)), pl.BlockSpec((lp, LANES), lambda s: (0, 3 * nh + s)),
                  pl.BlockSpec((1, LANES), lambda s: (0, 0)), pl.BlockSpec((lp, LANES), lambda s: (0, s))],
        out_specs=[pl.BlockSpec((lp, LANES), lambda s: (0, s)), pl.BlockSpec((lp, LANES), lambda s: (0, 3 * nh + s)),
                   pl.BlockSpec((1, LANES), lambda s: (0, 0))],
        out_shape=[jax.ShapeDtypeStruct((lp, w), F32), jax.ShapeDtypeStruct((lp, 8 * w + NARROW), BF16),
                   jax.ShapeDtypeStruct((1, LANES), F32)],
        name="merge_gdn_bwd", compiler_params=_cp("arbitrary"))(o_gdn, proj, norm_w, dmerged)


def _fox_prep(proj, qk_w, nh):
    lp = proj.shape[0]

    def body(x_ref, w_ref, o_ref):
        x = x_ref[...]
        o_ref[...] = x * _rms(x) * w_ref[0]

    return pl.pallas_call(
        body, grid=(2 * nh,),
        in_specs=[pl.BlockSpec((lp, LANES), lambda s: (0, 4 * nh + s)), pl.BlockSpec((1, 1, LANES), lambda s: (s // nh, 0, 0))],
        out_specs=pl.BlockSpec((lp, LANES), lambda s: (0, s)),
        out_shape=jax.ShapeDtypeStruct((lp, 2 * nh * HEAD_DIM), F32), name="fox_prep",
        compiler_params=_cp("parallel"))(proj, qk_w)


def _fox_prep_bwd(proj, qk_w, dq, dk, dproj, nh):
    lp = proj.shape[0]
    part = lambda p: pl.BlockSpec((lp, LANES), lambda s: (0, jnp.clip(s - p * nh, 0, nh - 1)))

    def body(x_ref, w_ref, dq_ref, dk_ref, _, dx_ref, dw_ref):
        x = x_ref[...]
        r = _rms(x)
        xh = x * r
        dy = jnp.where(pl.program_id(0) < nh, dq_ref[...], dk_ref[...])
        dyw = dy * w_ref[0]
        dx_ref[...] = (r * (dyw - xh * jnp.mean(dyw * xh, axis=-1, keepdims=True))).astype(BF16)

        @pl.when(pl.program_id(0) % nh == 0)
        def _():
            dw_ref[...] = jnp.zeros_like(dw_ref)
        dw_ref[0] += jnp.sum(dy * xh, axis=0, keepdims=True)

    strip = pl.BlockSpec((lp, LANES), lambda s: (0, 4 * nh + s))
    wsp = pl.BlockSpec((1, 1, LANES), lambda s: (s // nh, 0, 0))
    return pl.pallas_call(
        body, grid=(2 * nh,), in_specs=[strip, wsp, part(0), part(1), _ANY], out_specs=[strip, wsp],
        out_shape=[jax.ShapeDtypeStruct(dproj.shape, BF16), jax.ShapeDtypeStruct((2, 1, LANES), F32)],
        input_output_aliases={4: 0}, name="fox_prep_bwd", compiler_params=_cp("arbitrary"))(proj, qk_w, dq, dk, dproj)


def _fox_probs(q, k, gates, crow, h, i, nh, lse=None):
    kl = k.shape[0]
    lane = _iota((Q_BLOCK, LANES), 1)
    ct = jnp.sum(jnp.where(lane == 4 * nh + h, gates, 0.0), axis=1, keepdims=True)
    tq, kq = _iota((Q_BLOCK, Q_BLOCK), 0), _iota((Q_BLOCK, Q_BLOCK), 1)
    qs = q * (HEAD_DIM ** -0.5)
    if i == 0:
        s = _bdot(qs, k, "nt") + (ct - crow)
        s = jnp.where((kq <= tq) & ((kq >= PAD_ROWS) | (tq < PAD_ROWS)), s, NEG)
    else:
        crow = jnp.where(_iota((1, kl), 1) < PAD_ROWS, -NEG, crow)
        s = _bdot(qs, k, "nt") + (ct - crow)
        s = jnp.concatenate([s[:, :kl - Q_BLOCK], jnp.where(kq <= tq, s[:, kl - Q_BLOCK:], NEG)], axis=1)
    if lse is not None:
        return jnp.exp(s - lse)
    m = jnp.max(s, axis=1, keepdims=True)
    p = jnp.exp(s - m)
    tot = jnp.sum(p, axis=1, keepdims=True)
    return p / tot, m + jnp.log(tot)


FOX_HEADS_PER_STEP = 2


def _fox_specs(lp, nh):
    hw = FOX_HEADS_PER_STEP * LANES
    return [pl.BlockSpec((Q_BLOCK, hw), lambda g, i: (i, g)),
            pl.BlockSpec((lp, hw), lambda g, i: (0, nh // FOX_HEADS_PER_STEP + g)),
            pl.BlockSpec((lp, hw), lambda g, i: (0, 6 * nh // FOX_HEADS_PER_STEP + g)),
            pl.BlockSpec((Q_BLOCK, LANES), lambda g, i: (i, 0)),
            pl.BlockSpec((LANES, lp), lambda g, i: (0, 0))]


def _fox_fwd(qkn, proj, gates, gtf, nh):
    lp = qkn.shape[0]

    def body(q_ref, k_ref, v_ref, g_ref, gt_ref, o_ref, lse_ref):
        g, i = pl.program_id(0), pl.program_id(1)
        for j in range(lp // Q_BLOCK):
            @pl.when(i == j)
            def _(j=j):
                kl = (j + 1) * Q_BLOCK
                for hh in range(FOX_HEADS_PER_STEP):
                    h = FOX_HEADS_PER_STEP * g + hh
                    sl = slice(hh * LANES, (hh + 1) * LANES)
                    p, lse = _fox_probs(q_ref[:, sl], k_ref[0:kl, sl], g_ref[...], gt_ref[pl.ds(4 * nh + h, 1), :][:, 0:kl],
                                        h, j, nh)
                    o_ref[:, sl] = _bdot(p, v_ref[0:kl, sl])
                    lse_ref[:, sl] = jnp.broadcast_to(lse, (Q_BLOCK, LANES))

    blk = pl.BlockSpec((Q_BLOCK, FOX_HEADS_PER_STEP * LANES), lambda g, i: (i, g))
    return pl.pallas_call(
        body, grid=(nh // FOX_HEADS_PER_STEP, lp // Q_BLOCK), in_specs=_fox_specs(lp, nh), out_specs=[blk, blk],
        out_shape=[jax.ShapeDtypeStruct((lp, nh * HEAD_DIM), F32)] * 2, name="fox_fwd",
        compiler_params=_cp("parallel", "parallel"))(qkn, qkn, proj, gates, gtf)


def _fox_bwd(qkn, proj, gates, gtf, lse, do, dproj, nh):
    lp = qkn.shape[0]
    nq = lp // Q_BLOCK
    w = nh * HEAD_DIM
    scale = HEAD_DIM ** -0.5

    def body(q_ref, k_ref, v_ref, g_ref, gt_ref, lse_ref, do_ref, _, dq_ref, dk_ref, dc_ref, dv_ref, dv_scr):
        g, i = pl.program_id(0), pl.program_id(1)

        @pl.when(i == 0)
        def _():
            dk_ref[...] = jnp.zeros_like(dk_ref)
            dv_scr[...] = jnp.zeros_like(dv_scr)
            dc_ref[...] = jnp.zeros_like(dc_ref)
        for j in range(nq):
            @pl.when(i == j)
            def _(j=j):
                kl = (j + 1) * Q_BLOCK
                for hh in range(FOX_HEADS_PER_STEP):
                    h = FOX_HEADS_PER_STEP * g + hh
                    sl = slice(hh * LANES, (hh + 1) * LANES)
                    q, k = q_ref[:, sl], k_ref[0:kl, sl]
                    p = _fox_probs(q, k, g_ref[...], gt_ref[pl.ds(4 * nh + h, 1), :][:, 0:kl], h, j, nh,
                                   lse_ref[:, sl][:, 0:1])
                    dout = do_ref[:, sl]
                    dp = _bdot(dout, v_ref[0:kl, sl], "nt")
                    ds = p * (dp - jnp.sum(p * dp, axis=1, keepdims=True))
                    dq_ref[:, sl] = _bdot(ds, k) * scale
                    dk_ref[0:kl, sl] += _bdot(ds, q * scale, "tn")
                    dv_scr[0:kl, sl] += _bdot(p, dout, "tn")
                    dc_ref[hh, :, 0:kl] -= jnp.sum(ds, axis=0, keepdims=True)

        @pl.when(i == nq - 1)
        def _():
            dv_ref[...] = dv_scr[...].astype(BF16)

    hw = FOX_HEADS_PER_STEP * LANES
    blk = pl.BlockSpec((Q_BLOCK, hw), lambda g, i: (i, g))
    col = pl.BlockSpec((lp, hw), lambda g, i: (0, g))
    return pl.pallas_call(
        body, grid=(nh // FOX_HEADS_PER_STEP, nq), in_specs=_fox_specs(lp, nh) + [blk, blk, _ANY],
        out_specs=[blk, col, pl.BlockSpec((FOX_HEADS_PER_STEP, 1, lp), lambda g, i: (g, 0, 0)),
                   pl.BlockSpec((lp, hw), lambda g, i: (0, 6 * nh // FOX_HEADS_PER_STEP + g))],
        out_shape=[jax.ShapeDtypeStruct((lp, w), F32)] * 2 + [jax.ShapeDtypeStruct((nh, 1, lp), F32),
                                                             jax.ShapeDtypeStruct(dproj.shape, BF16)],
        scratch_shapes=[pltpu.VMEM((lp, hw), F32)], input_output_aliases={7: 3},
        name="fox_bwd", compiler_params=_cp("parallel", "arbitrary"))(qkn, qkn, proj, gates, gtf, lse, do, dproj)


def _merge_fox(o_fox, proj, merged, nh):
    lp = o_fox.shape[0]

    def body(o_ref, z_ref, _, m_ref):
        z = z_ref[...]
        m_ref[...] = (o_ref[...] * (z * _sigmoid(z))).astype(BF16)

    return pl.pallas_call(
        body, grid=(nh,),
        in_specs=[pl.BlockSpec((lp, LANES), lambda s: (0, s)), pl.BlockSpec((lp, LANES), lambda s: (0, 7 * nh + s)), _ANY],
        out_specs=pl.BlockSpec((lp, LANES), lambda s: (0, nh + s)),
        out_shape=jax.ShapeDtypeStruct(merged.shape, BF16), input_output_aliases={2: 0}, name="merge_fox",
        compiler_params=_cp("parallel"))(o_fox, proj, merged)


def _merge_fox_bwd(o_fox, proj, dmerged, dproj, nh):
    lp = o_fox.shape[0]

    def body(o_ref, z_ref, dm_ref, _, do_ref, dz_ref):
        silu, dsilu = _silu_and_grad(z_ref[...])
        dm = dm_ref[...]
        do_ref[...] = dm * silu
        dz_ref[...] = (dm * o_ref[...] * dsilu).astype(BF16)

    w = nh * HEAD_DIM
    return pl.pallas_call(
        body, grid=(nh,),
        in_specs=[pl.BlockSpec((lp, LANES), lambda s: (0, s)), pl.BlockSpec((lp, LANES), lambda s: (0, 7 * nh + s)),
                  pl.BlockSpec((lp, LANES), lambda s: (0, nh + s)), _ANY],
        out_specs=[pl.BlockSpec((lp, LANES), lambda s: (0, s)), pl.BlockSpec((lp, LANES), lambda s: (0, 7 * nh + s))],
        out_shape=[jax.ShapeDtypeStruct((lp, w), F32), jax.ShapeDtypeStruct(dproj.shape, BF16)],
        input_output_aliases={3: 1}, name="merge_fox_bwd", compiler_params=_cp("parallel"))(o_fox, proj, dmerged, dproj)


def _post(out, x, target, post_w):
    lp, d = out.shape

    def body(o_ref, x_ref, t_ref, w_ref, dy_ref, do_ref, loss_ref, dw_ref):
        i = pl.program_id(0)

        @pl.when(i == 0)
        def _():
            loss_ref[...] = jnp.zeros_like(loss_ref)
            dw_ref[...] = jnp.zeros_like(dw_ref)
        o = o_ref[...]
        r = _rms(o)
        nrm = o * r
        err = jnp.where(i > 0, x_ref[...] + nrm * w_ref[...] - t_ref[...], 0.0)
        loss_ref[0:1, :] += 0.5 * jnp.sum(jnp.sum(err * err, axis=1, keepdims=True), axis=0, keepdims=True) / d
        dy = err / d
        dy_ref[...] = dy
        dw_ref[...] += jnp.sum(dy * nrm, axis=0, keepdims=True)
        dyw = dy * w_ref[...]
        do_ref[...] = (r * (dyw - nrm * jnp.mean(dyw * nrm, axis=-1, keepdims=True))).astype(BF16)

    row = pl.BlockSpec((Q_BLOCK, d), lambda i: (i, 0))
    vec = pl.BlockSpec((1, d), lambda i: (0, 0))
    return pl.pallas_call(
        body, grid=(lp // Q_BLOCK,), in_specs=[row, _x_rows(d), _x_rows(d), vec],
        out_specs=[_x_rows(d), row, pl.BlockSpec((8, LANES), lambda i: (0, 0)), vec],
        out_shape=[jax.ShapeDtypeStruct(x.shape, F32), jax.ShapeDtypeStruct((lp, d), BF16),
                   jax.ShapeDtypeStruct((8, LANES), F32), jax.ShapeDtypeStruct((1, d), F32)],
        name="post", compiler_params=_cp("arbitrary"))(out, x, target, post_w)


def _prenorm_bwd(dxn, x, meta, w, dy, rider=None):
    seq, d = x.shape
    lp = seq + Q_BLOCK

    def body(start, finish, dx_ref, x_ref, m_ref, w_ref, dy_ref, gx_ref, gm_ref, dw_ref):
        i = pl.program_id(0)
        pl.when(i == 0)(start)
        h = _h_tile(i, x_ref, m_ref)
        r = _rms(h)
        xh = h * r
        dxn_ = dx_ref[...]
        dxw = dxn_ * w_ref[...]
        dh = jnp.where(i > 0, dy_ref[...], 0.0) + r * (dxw - xh * jnp.mean(dxw * xh, axis=-1, keepdims=True))
        gx_ref[...] = dh

        @pl.when(i == 0)
        def _():
            dw_ref[...] = jnp.zeros_like(dw_ref)
            gm_ref[...] = dh[PAD_ROWS:, :]
        dw_ref[...] += jnp.sum(dxn_ * xh, axis=0, keepdims=True)
        pl.when(i == lp // Q_BLOCK - 1)(finish)

    vec = pl.BlockSpec((1, d), lambda i: (0, 0))
    met = pl.BlockSpec((N_META, d), lambda i: (0, 0))
    outs, got = _hosted_call(
        body, [rider], 5, 3, 0, grid=(lp // Q_BLOCK,),
        in_specs=[pl.BlockSpec((Q_BLOCK, d), lambda i: (i, 0)), _x_rows(d), met, vec, _x_rows(d)],
        out_specs=[_x_rows(d), met, vec],
        out_shape=[jax.ShapeDtypeStruct((seq, d), F32), jax.ShapeDtypeStruct((N_META, d), F32),
                   jax.ShapeDtypeStruct((1, d), F32)],
        name="prenorm_bwd", compiler_params=_cp("arbitrary"))(dxn, x, meta, w, dy)
    return outs, (got[0] if got else None)


def _layer_grads(x, target, meta, pre_w, wfull, conv_wt, a_log, dt_bias, gdn_norm_w, fq_w, fk_w, f_bias, w_out, post_w,
                 late_weights=None, w_out_grads=None):
    nh = a_log.shape[1]
    zpad = jnp.zeros((1, LANES - 3 * nh), F32)
    bias_row = jnp.concatenate([jnp.zeros((1, nh), F32), dt_bias, f_bias, zpad], axis=1)
    nega_row = jnp.concatenate([jnp.zeros((1, nh), F32), -jnp.exp(a_log), jnp.zeros((1, nh), F32), zpad], axis=1)
    qk_w = jnp.stack([fq_w, fk_w])

    if isinstance(wfull, tuple):
        rider, meta_at, project = wfull
        xn, got = _prenorm_gathering(x, pre_w, rider, meta_at)
        proj, wfull, conv_wt, meta = project(xn, got)
    else:
        xn = _prenorm(x, meta, pre_w)
        proj = _matmul(xn, wfull, "nn", MM_TILE, F32, "proj")
    qkv = _gdn_prep(proj, conv_wt, nh)
    gates, gt3, gtf = _gates(proj, bias_row, nega_row, nh)
    (o_gdn, s_all, t_all), got = _gdn_fwd(qkv, gates, gt3, nh, None if late_weights is None else late_weights[0])
    if late_weights is not None:
        w_out = late_weights[1](got)
    qkn = _fox_prep(proj, qk_w, nh)
    o_fox, fox_lse = _fox_fwd(qkn, proj, gates, gtf, nh)
    merged = _merge_fox(o_fox, proj, _merge_gdn(o_gdn, proj, gdn_norm_w, nh), nh)
    out = _matmul(merged, w_out, "nn", 4 * LANES, F32, "out_proj")
    dy, dout, loss_blk, dpost_w = _post(out, x, target, post_w)

    dw_out = _matmul(merged, dout, "tn", 4 * LANES, BF16, "dw_out")
    if w_out_grads is None:
        dmerged, gdn_rider = _matmul(dout, w_out, "nt", 4 * LANES, F32, "dmerged"), None
    else:
        dmerged, got = _matmul(dout, w_out, "nt", 4 * LANES, F32, "dmerged", w_out_grads[0](dw_out))
        gdn_rider = w_out_grads[1](dw_out, got)
    do_gdn, dproj, dgdn_norm_w = _merge_gdn_bwd(o_gdn, proj, gdn_norm_w, dmerged, nh)
    do_fox, dproj = _merge_fox_bwd(o_fox, proj, dmerged, dproj, nh)
    dqn, dkn, dc_t, dproj = _fox_bwd(qkn, proj, gates, gtf, fox_lse, do_fox, dproj, nh)
    dproj, dqk_w = _fox_prep_bwd(proj, qk_w, dqn, dkn, dproj, nh)
    (dgq, dgk, dgv, dgate), w_out_parts = _gdn_bwd(qkv, gates, gt3, s_all, t_all, do_gdn, nh, gdn_rider)
    dproj, dconv_wt = _gdn_prep_bwd(proj, conv_wt, dgq, dgk, dgv, dproj, nh)
    dc_rows = jnp.pad(dc_t.reshape(nh, -1), ((2 * nh, LANES - 3 * nh), (0, 0)))
    dproj, gate_sums = _gates_bwd(proj, bias_row, nega_row, gates, dgate, dc_rows, dproj, nh)
    return dict(
        loss=loss_blk[0:1, 0:1], dy=dy, xn=xn, dproj=dproj, post_w=dpost_w,
        conv_wt=dconv_wt, a_log=gate_sums[1:2, nh:2 * nh], dt_bias=gate_sums[0:1, nh:2 * nh],
        gdn_norm_w=dgdn_norm_w, fq_w=dqk_w[0], fk_w=dqk_w[1], f_bias=gate_sums[0:1, 2 * nh:3 * nh], w_out=dw_out,
        w_out_parts=w_out_parts, wfull=wfull, meta=meta)


def _cast_bf16(a, tr, name):
    r, c = a.shape

    def body(a_ref, o_ref):
        o_ref[...] = a_ref[...].astype(BF16)

    return pl.pallas_call(
        body, grid=(r // tr,), in_specs=[pl.BlockSpec((tr, c), lambda i: (i, 0))],
        out_specs=pl.BlockSpec((tr, c), lambda i: (i, 0)), out_shape=jax.ShapeDtypeStruct((r, c), BF16),
        name=name, compiler_params=_cp("parallel"))(a)


def _column_major(a):
    return jnp.transpose(a, (2, 0, 1))


def _cast_bf16_column_major(a3, pieces, name):
    _, r, c = a3.shape
    rows = r // pieces

    def body(a_ref, *o_refs):
        t = a_ref[...].reshape(LANES, r).T.astype(BF16)
        for k, o_ref in enumerate(o_refs):
            o_ref[...] = t[k * rows:(k + 1) * rows]

    return pl.pallas_call(
        body, grid=(pl.cdiv(c, LANES),), in_specs=[pl.BlockSpec((LANES, 1, r), lambda i: (i, 0, 0))],
        out_specs=[pl.BlockSpec((rows, LANES), lambda i: (0, i))] * pieces,
        out_shape=[jax.ShapeDtypeStruct((rows, c), BF16)] * pieces, name=name, compiler_params=_cp("parallel"))(_column_major(a3))


def _adamw_column_major(w3, parts, m3, v3, name):
    _, r, c = w3.shape
    n_parts = parts[0].shape[0]

    def body(w_ref, *refs):
        p_refs, (m_ref, v_ref, g_ref, d_ref, nm_ref, nv_ref) = refs[:len(parts)], refs[len(parts):]
        sums = []
        for p_ref in p_refs:
            g = p_ref[0].astype(F32)
            for s in range(1, n_parts):
                g = g + p_ref[s].astype(F32)
            sums.append(g)
        g = jnp.concatenate(sums, axis=0).T
        flat = lambda ref: ref[...].reshape(LANES, r)
        m_new = ADAM_B1 * flat(m_ref) + (1.0 - ADAM_B1) * g
        v_new = ADAM_B2 * flat(v_ref) + (1.0 - ADAM_B2) * (g * g)
        m_hat = m_new / (1.0 - ADAM_B1 ** ADAM_STEP)
        v_hat = v_new / (1.0 - ADAM_B2 ** ADAM_STEP)
        delta = -ADAM_LR * (m_hat / (jnp.sqrt(v_hat) + ADAM_EPS) + ADAM_WD * flat(w_ref))
        for ref, val in ((g_ref, g), (d_ref, delta), (nm_ref, m_new), (nv_ref, v_new)):
            ref[...] = val.reshape(LANES, 1, r)

    blk = pl.BlockSpec((LANES, 1, r), lambda i: (i, 0, 0))
    outs = pl.pallas_call(
        body, grid=(pl.cdiv(c, LANES),),
        in_specs=[blk] + [pl.BlockSpec((n_parts, p.shape[1], LANES), lambda i: (0, 0, i)) for p in parts] + [blk, blk],
        out_specs=[blk] * 4, out_shape=[jax.ShapeDtypeStruct((c, 1, r), F32)] * 4, name=name,
        compiler_params=_cp("parallel"))(_column_major(w3), *parts, _column_major(m3), _column_major(v3))
    return [jnp.transpose(o, (1, 2, 0)) for o in outs]


def _gather_copies(ins, outs, send_sems, recv_sems, local_sems):
    n = len(ins)
    x, y, c = lax.axis_index("x"), lax.axis_index("y"), lax.axis_index("c")
    me, sibling = (x, y, c), (x, y, 1 - c)
    xn, yn, dg = (1 - x, y), (x, 1 - y), (1 - x, 1 - y)

    def copy(a, k, block, to, src=None):
        px, py, pc = block
        rows = outs[a].at[4 * px + 2 * py + pc]
        return pltpu.make_async_remote_copy(
            src_ref=rows if src is None else src, dst_ref=rows, send_sem=send_sems.at[a, k],
            recv_sem=recv_sems.at[a, k], device_id=to, device_id_type=_MESH)

    local = [pltpu.make_async_copy(ins[a], outs[a].at[4 * x + 2 * y + c], local_sems.at[a]) for a in range(n)]
    own = [cp for a in range(n) for cp in (copy(a, 0, me, sibling, src=ins[a]), copy(a, 1, me, (*xn, c), src=ins[a]),
                                           copy(a, 2, me, (*yn, c), src=ins[a]))]

    def start():
        for cp in local + own:
            cp.start()

    def finish():
        for a in range(n):
            @pl.when(c == 1)
            def _(a=a):
                copy(a, 1, (*xn, c), me).wait_recv()
                copy(a, 3, (*xn, c), (*yn, c)).start()

            @pl.when(c == 0)
            def _(a=a):
                copy(a, 2, (*yn, c), me).wait_recv()
                copy(a, 3, (*yn, c), (*xn, c)).start()
        for a in range(n):
            pl.when(c == 0)(copy(a, 1, (*xn, c), me).wait_recv)
            copy(a, 4, (*xn, c), sibling).start()
            pl.when(c == 1)(copy(a, 2, (*yn, c), me).wait_recv)
            copy(a, 5, (*yn, c), sibling).start()
        for a in range(n):
            copy(a, 3, (*dg, c), me).wait_recv()
            copy(a, 6, (*dg, c), sibling).start()
        for a in range(n):
            copy(a, 0, sibling, me).wait_recv()
            for k, chip in ((4, xn), (5, yn), (6, dg)):
                copy(a, k, (*chip, 1 - c), me).wait_recv()
                copy(a, k, (*chip, c), sibling).wait_send()
            copy(a, 3, (*xn, c), (*yn, c)).wait_send()
        for cp in own:
            cp.wait_send()
        for cp in local:
            cp.wait()

    return start, finish


def _gather_scratch(n):
    return [pltpu.SemaphoreType.DMA((n, N_DEV - 1)), pltpu.SemaphoreType.DMA((n, N_DEV - 1)), pltpu.SemaphoreType.DMA((n,))]


def _gather_rider(arrays):
    return _Rider(list(arrays), [jax.ShapeDtypeStruct((N_DEV,) + a.shape, a.dtype) for a in arrays],
                  _gather_scratch(len(arrays)), {}, lambda ins, outs, scratch: _gather_copies(ins, outs, *scratch))


def _all_gather(arrays, name):
    n = len(arrays)

    def body(*refs):
        ins, outs, (send_sems, recv_sems, local_sems) = refs[:n], refs[n:2 * n], refs[2 * n:]
        x, y, c = lax.axis_index("x"), lax.axis_index("y"), lax.axis_index("c")
        mine = 4 * x + 2 * y + c
        copies = [pltpu.make_async_copy(ins[a], outs[a].at[mine], local_sems.at[a]) for a in range(n)]
        for a in range(n):
            for k in range(N_DEV - 1):
                peer = (1 - x if (k + 1) & 4 else x, 1 - y if (k + 1) & 2 else y, 1 - c if (k + 1) & 1 else c)
                copies.append(pltpu.make_async_remote_copy(
                    src_ref=ins[a], dst_ref=outs[a].at[mine], send_sem=send_sems.at[a, k], recv_sem=recv_sems.at[a, k],
                    device_id=peer, device_id_type=_MESH))
        for cp in copies:
            cp.start()
        for cp in copies:
            cp.wait()

    return pl.pallas_call(
        body, in_specs=[_ANY] * n, out_specs=[_ANY] * n,
        out_shape=[jax.ShapeDtypeStruct((N_DEV,) + a.shape, a.dtype) for a in arrays],
        scratch_shapes=_gather_scratch(n), name=name)(*arrays)


SLAB = 10 * LANES


def _slab_start(blk, nh, cols):
    in_second_half = blk >= N_DEV // 2
    shift = (2 * nh if in_second_half else 0) if isinstance(blk, int) else jnp.where(in_second_half, 2 * nh, 0)
    return (blk * cols - shift) // LANES * LANES


def _pair_rider(dw_rows=None, parts=None, nh=None, after=None):
    if dw_rows is not None:
        r, full = dw_rows.shape
        cols = (full - NARROW + 3 * nh) // N_DEV
        out_shapes = [jax.ShapeDtypeStruct((N_CHIP, r, SLAB), dw_rows.dtype), jax.ShapeDtypeStruct((r, NARROW), dw_rows.dtype)]
    else:
        out_shapes = [jax.ShapeDtypeStruct((N_CHIP,) + parts.shape[1:], parts.dtype)]

    def make(ins, outs, scratch):
        send_sems, recv_sems = scratch
        x, y, c = lax.axis_index("x"), lax.axis_index("y"), lax.axis_index("c")
        kw = lambda k: dict(send_sem=send_sems.at[k], recv_sem=recv_sems.at[k], device_id=(x, y, 1 - c), device_id_type=_MESH)
        copies = []
        for q in range(N_CHIP):
            if dw_rows is not None:
                first = pl.multiple_of(_slab_start(2 * q + 1 - c, nh, cols), LANES)
                copies.append(pltpu.make_async_remote_copy(src_ref=ins[0].at[:, pl.ds(first, SLAB)], dst_ref=outs[0].at[q], **kw(q)))
            else:
                copies.append(pltpu.make_async_remote_copy(src_ref=ins[0].at[2 * q + 1 - c], dst_ref=outs[0].at[q], **kw(q)))
        if dw_rows is not None:
            copies.append(pltpu.make_async_remote_copy(src_ref=ins[0].at[:, pl.ds(full - NARROW, NARROW)], dst_ref=outs[1],
                                                       **kw(N_CHIP)))

        def start():
            for cp in copies:
                cp.start()

        def finish():
            for cp in copies:
                cp.wait()

        return start, finish

    return _Rider([dw_rows if dw_rows is not None else parts] + ([] if after is None else [after]), out_shapes,
                  [pltpu.SemaphoreType.DMA((N_CHIP + 1,)), pltpu.SemaphoreType.DMA((N_CHIP + 1,))], {}, make)


def _relayout_pair_sum(dwfull, got_slabs, got_tail, core, nh, tr, name):
    d, full = dwfull.shape
    w = nh * HEAD_DIM
    cols = (8 * w + 3 * nh) // N_DEV
    segs = _native_segments(nh)

    def block(f_ref, s_ref, t_ref, q, blk):
        st = _slab_start(blk, nh, cols)
        wide = f_ref[:, st:st + SLAB].astype(F32) + s_ref[q].astype(F32)
        tail = f_ref[:, 8 * w:].astype(F32) + t_ref[...].astype(F32)
        pieces = []
        for s0, s1, t0 in segs:
            lo, hi = max(s0, blk * cols), min(s1, (blk + 1) * cols)
            if lo < hi:
                at = t0 + lo - s0
                pieces.append(tail[:, at - 8 * w:at - 8 * w + hi - lo] if at >= 8 * w else wide[:, at - st:at - st + hi - lo])
        return (pieces[0] if len(pieces) == 1 else jnp.concatenate(pieces, axis=1)).astype(dwfull.dtype)

    def body(core_ref, f_ref, s_ref, t_ref, o_ref):
        for parity in range(2):
            @pl.when(core_ref[0] == parity)
            def _(parity=parity):
                for q in range(N_CHIP):
                    o_ref[q] = block(f_ref, s_ref, t_ref, q, 2 * q + parity)

    return pl.pallas_call(
        body,
        grid_spec=pltpu.PrefetchScalarGridSpec(
            num_scalar_prefetch=1, grid=(d // tr,),
            in_specs=[pl.BlockSpec((tr, full), lambda i, c_ref: (i, 0)), pl.BlockSpec((N_CHIP, tr, SLAB), lambda i, c_ref: (0, i, 0)),
                      pl.BlockSpec((tr, NARROW), lambda i, c_ref: (i, 0))],
            out_specs=pl.BlockSpec((N_CHIP, tr, cols), lambda i, c_ref: (0, i, 0))),
        out_shape=jax.ShapeDtypeStruct((N_CHIP, d, cols), dwfull.dtype), name=name,
        compiler_params=_cp("parallel"))(core, dwfull, got_slabs, got_tail)


def _pair_sum(parts, got, core, tr, name):
    _, r, c = parts.shape

    def body(core_ref, p_ref, g_ref, o_ref):
        o_ref[...] = (p_ref[...].astype(F32) + g_ref[...].astype(F32)).astype(o_ref.dtype)

    return pl.pallas_call(
        body,
        grid_spec=pltpu.PrefetchScalarGridSpec(
            num_scalar_prefetch=1, grid=(N_CHIP, r // tr),
            in_specs=[pl.BlockSpec((1, tr, c), lambda q, i, core_ref: (2 * q + core_ref[0], i, 0)),
                      pl.BlockSpec((1, tr, c), lambda q, i, core_ref: (q, i, 0))],
            out_specs=pl.BlockSpec((1, tr, c), lambda q, i, core_ref: (q, i, 0))),
        out_shape=jax.ShapeDtypeStruct((N_CHIP, r, c), parts.dtype), name=name,
        compiler_params=_cp("parallel", "parallel"))(core, parts, got)


def _native_segments(nh):
    w = nh * HEAD_DIM
    return [(0, 4 * w, 0), (4 * w, 4 * w + 2 * nh, 8 * w), (4 * w + 2 * nh, 8 * w + 2 * nh, 4 * w),
            (8 * w + 2 * nh, 8 * w + 3 * nh, 8 * w + 2 * nh)]


def _relayout_w_in(wg, nh, tr, name, rows_total=None, into=None):
    _, d, cols = wg.shape
    w = nh * HEAD_DIM
    rows_total = into.shape[0] if into is not None else rows_total or d
    first = (rows_total - d) // tr if into is not None else 0

    def native(ref, j0, j1):
        out = []
        while j0 < j1:
            blk = j0 // cols
            end = min(j1, (blk + 1) * cols)
            out.append(ref[blk, :, pl.ds(j0 - blk * cols, end - j0)])
            j0 = end
        return out

    def body(g_ref, *refs):
        o_ref = refs[-1]
        for cidx in range(8 * w // LANES):
            j0 = cidx * LANES + (0 if cidx * LANES < 4 * w else 2 * nh)
            pieces = native(g_ref, j0, j0 + LANES)
            o_ref[:, cidx * LANES:(cidx + 1) * LANES] = pieces[0] if len(pieces) == 1 else jnp.concatenate(pieces, axis=1)
        pieces = (native(g_ref, 4 * w, 4 * w + 2 * nh) + native(g_ref, 8 * w + 2 * nh, 8 * w + 3 * nh)
                  + [jnp.zeros((tr, NARROW - 3 * nh), wg.dtype)])
        o_ref[:, 8 * w:] = jnp.concatenate(pieces, axis=1)

    return pl.pallas_call(
        body, grid=(d // tr,), in_specs=[pl.BlockSpec((N_DEV, tr, cols), lambda i: (0, i, 0))] + [_ANY] * (into is not None),
        out_specs=pl.BlockSpec((tr, 8 * w + NARROW), lambda i: (first + i, 0)),
        out_shape=jax.ShapeDtypeStruct((rows_total, 8 * w + NARROW), wg.dtype),
        input_output_aliases={1: 0} if into is not None else {},
        name=name, compiler_params=_cp("parallel"))(wg, *([into] if into is not None else []))


def _adamw(w, parts, m, v, tr, name, after=None):
    r, c = w.shape
    n_parts = parts.shape[0]

    def body(w_ref, p_ref, m_ref, v_ref, *refs):
        g_ref, d_ref, nm_ref, nv_ref = refs[-5:-1] if after is not None else refs
        if after is not None:
            refs[-1][...] = jnp.zeros_like(refs[-1])
        g = p_ref[0].astype(F32)
        for s in range(1, n_parts):
            g = g + p_ref[s].astype(F32)
        m_new = ADAM_B1 * m_ref[...] + (1.0 - ADAM_B1) * g
        v_new = ADAM_B2 * v_ref[...] + (1.0 - ADAM_B2) * (g * g)
        m_hat = m_new / (1.0 - ADAM_B1 ** ADAM_STEP)
        v_hat = v_new / (1.0 - ADAM_B2 ** ADAM_STEP)
        g_ref[...] = g
        d_ref[...] = -ADAM_LR * (m_hat / (jnp.sqrt(v_hat) + ADAM_EPS) + ADAM_WD * w_ref[...])
        nm_ref[...] = m_new
        nv_ref[...] = v_new

    blk = pl.BlockSpec((tr, c), lambda i: (i, 0))
    extra = after is not None
    return pl.pallas_call(
        body, grid=(r // tr,),
        in_specs=[blk, pl.BlockSpec((n_parts, tr, c), lambda i: (0, i, 0)), blk, blk] + [_ANY] * extra,
        out_specs=[blk] * 4 + [pl.BlockSpec((8, LANES), lambda i: (0, 0))] * extra,
        out_shape=[jax.ShapeDtypeStruct((r, c), F32)] * 4 + [jax.ShapeDtypeStruct((8, LANES), F32)] * extra, name=name,
        compiler_params=_cp("arbitrary" if extra else "parallel"))(w, parts, m, v, *([after] if extra else []))


def _adamw_conv(w, gathered, dev, m, v, name):
    c, r = w.shape
    n_parts = gathered.shape[0]

    def body(dev_ref, w_ref, p_ref, m_ref, v_ref, g_ref, d_ref, nm_ref, nv_ref):
        g = p_ref[0].astype(F32)
        for s in range(1, n_parts):
            g = g + p_ref[s].astype(F32)
        m_new = ADAM_B1 * m_ref[...] + (1.0 - ADAM_B1) * g
        v_new = ADAM_B2 * v_ref[...] + (1.0 - ADAM_B2) * (g * g)
        m_hat = m_new / (1.0 - ADAM_B1 ** ADAM_STEP)
        v_hat = v_new / (1.0 - ADAM_B2 ** ADAM_STEP)
        g_ref[...] = g
        d_ref[...] = -ADAM_LR * (m_hat / (jnp.sqrt(v_hat) + ADAM_EPS) + ADAM_WD * w_ref[...])
        nm_ref[...] = m_new
        nv_ref[...] = v_new

    blk = pl.BlockSpec((c, r), lambda i, dev_ref: (0, 0))
    return pl.pallas_call(
        body,
        grid_spec=pltpu.PrefetchScalarGridSpec(
            num_scalar_prefetch=1, grid=(1,),
            in_specs=[blk, pl.BlockSpec((n_parts, c, r), lambda i, dev_ref: (0, 0, dev_ref[0])), blk, blk], out_specs=[blk] * 4),
        out_shape=[jax.ShapeDtypeStruct((c, r), F32)] * 4, name=name, compiler_params=_cp("arbitrary"))(dev, w, gathered, m, v)


def _pack_small(d, pre, post, a_log, dt_bias, f_bias, gdn_w, fq_w, fk_w, extra):
    row2 = jnp.concatenate([a_log, dt_bias, f_bias, gdn_w, fq_w, fk_w, extra], axis=1)
    row2 = jnp.pad(row2, ((0, 0), (0, d - row2.shape[1])))
    return jnp.concatenate([pre, post, row2, jnp.zeros((5, d), F32)], axis=0)


def _adamw_small(w, parts, m, v, nh, name):
    d = w.shape[1]
    n_parts = parts.shape[0]
    shapes = dict(pre=d, post=d, a_log=nh, dt_bias=nh, f_bias=nh, gdn_w=HEAD_DIM, fq_w=HEAD_DIM, fk_w=HEAD_DIM, extra=1)

    def body(w_ref, p_ref, m_ref, v_ref, *o_refs):
        g = p_ref[0]
        for s in range(1, n_parts):
            g = g + p_ref[s]
        m_new = ADAM_B1 * m_ref[...] + (1.0 - ADAM_B1) * g
        v_new = ADAM_B2 * v_ref[...] + (1.0 - ADAM_B2) * (g * g)
        m_hat = m_new / (1.0 - ADAM_B1 ** ADAM_STEP)
        v_hat = v_new / (1.0 - ADAM_B2 ** ADAM_STEP)
        delta = -ADAM_LR * (m_hat / (jnp.sqrt(v_hat) + ADAM_EPS) + ADAM_WD * w_ref[...])
        for k, val in enumerate((g, delta, m_new, v_new)):
            vectors = _unpack_small(val, nh)
            for j, key in enumerate(shapes):
                o_refs[k * len(shapes) + j][...] = vectors[key]

    full = lambda shape: pl.BlockSpec(shape, lambda i: (0,) * len(shape))
    outs = pl.pallas_call(
        body, grid=(1,), in_specs=[full(w.shape), full(parts.shape), full(w.shape), full(w.shape)],
        out_specs=[full((1, n)) for n in shapes.values()] * 4,
        out_shape=[jax.ShapeDtypeStruct((1, n), F32) for n in shapes.values()] * 4, name=name,
        compiler_params=_cp("arbitrary"))(w, parts, m, v)
    return [dict(zip(shapes, outs[k * len(shapes):(k + 1) * len(shapes)])) for k in range(4)]


def _unpack_small(p, nh):
    o = 3 * nh
    return dict(pre=p[0:1], post=p[1:2], a_log=p[2:3, 0:nh], dt_bias=p[2:3, nh:2 * nh], f_bias=p[2:3, 2 * nh:o],
                gdn_w=p[2:3, o:o + HEAD_DIM], fq_w=p[2:3, o + HEAD_DIM:o + 2 * HEAD_DIM],
                fk_w=p[2:3, o + 2 * HEAD_DIM:o + 3 * HEAD_DIM], extra=p[2:3, o + 3 * HEAD_DIM:o + 3 * HEAD_DIM + 1])


def kernel(x, meta_tokens, pre_norm_w, w_in, conv_w, a_log, dt_bias, gdn_norm_w, fox_q_norm_w, fox_k_norm_w, fox_f_bias, w_out, post_norm_w, loss_target, m_meta_tokens, m_pre_norm_w, m_w_in, m_conv_w, m_a_log, m_dt_bias, m_gdn_norm_w, m_fox_q_norm_w, m_fox_k_norm_w, m_fox_f_bias, m_w_out, m_post_norm_w, v_meta_tokens, v_pre_norm_w, v_w_in, v_conv_w, v_a_log, v_dt_bias, v_gdn_norm_w, v_fox_q_norm_w, v_fox_k_norm_w, v_fox_f_bias, v_w_out, v_post_norm_w):
    nh = a_log.shape[1]
    d = x.shape[-1]
    w = nh * HEAD_DIM
    zero = jnp.zeros((1, 1), F32)

    w_in_a, w_in_b = _cast_bf16_column_major(w_in, 2, "cast_w_in")

    def project(xn, got):
        wg, cg, mg = got
        half = (d // 2, 0), (d // 2, 1)
        wfull = _relayout_w_in(wg, nh, 256, "relayout_w_in_a", rows_total=d)
        proj, got = _matmul(xn, wfull, "nn", MM_TILE, F32, "proj_a", _gather_rider([w_in_b]), a_cols=half[0], b_rows=half[0])
        wfull = _relayout_w_in(got[0], nh, 256, "relayout_w_in_b", into=wfull)
        proj = _matmul(xn, wfull, "nn", MM_TILE, F32, "proj_b", a_cols=half[1], b_rows=half[1], acc=proj)
        return proj, wfull, cg.transpose(1, 0, 2).reshape(CONV_WIDTH, 3 * w), mg.transpose(1, 0, 2).reshape(N_META, d)

    late_weights = (_gather_rider([_cast_bf16(w_out[0], 256, "cast_w_out")]), lambda got: got[0].reshape(2 * w, d))
    core = lax.axis_index("c")
    dev = 4 * lax.axis_index("x") + 2 * lax.axis_index("y") + core
    core_arr = jnp.reshape(core, (1,)).astype(jnp.int32)

    out_parts = lambda dw_out: dw_out.reshape(N_DEV, 2 * w // N_DEV, d)
    g = _layer_grads(
        x[0], loss_target[0], None, pre_norm_w, (_gather_rider([w_in_a, conv_w[0].T, meta_tokens]), 2, project), None,
        a_log, dt_bias, gdn_norm_w,
        fox_q_norm_w, fox_k_norm_w, fox_f_bias, None, post_norm_w, late_weights=late_weights,
        w_out_grads=(lambda dw_out: _pair_rider(parts=out_parts(dw_out)),
                     lambda dw_out, got: _chip_rider(_pair_sum(out_parts(dw_out), got[0], core_arr, 256, "pair_sum_w_out"))))
    p_out = g["w_out_parts"][0]
    xn, dproj, wfull, meta_full = g["xn"], g["dproj"], g["wfull"], g["meta"]
    flights, token, dw, rider = [], None, None, None

    def exchange(dw, got, i):
        sums = _relayout_pair_sum(dw, got[0], got[1], core_arr, nh, 128, f"relayout_pair_sum_{i}")
        flight, token = _chip_exchange_start(sums, f"chip_exchange_start_{i}")
        flights.append(flight)
        return token

    for i, (index, parts) in enumerate(DW_IN_PIECES):
        res = _matmul(xn, dproj, "tn", MM_TILE, BF16, f"dw_in_{i}", rider, a_cols=(d // parts, index))
        if i:
            token = exchange(dw, res[1], i - 1)
        dw = res[0] if i else res
        rider = _pair_rider(dw_rows=dw, nh=nh, after=token)
    dxn, (got,) = _dxn(dproj, wfull, [rider], MM_TILE, "dxn")
    token = exchange(dw, got, len(DW_IN_PIECES) - 1)
    *r_out, token = _adamw(w_out[0], p_out, m_w_out[0], v_w_out[0], 64, "adamw_w_out", after=token)
    (grad_x, dmeta, dpre_w), _ = _prenorm_bwd(dxn, x[0], meta_full, pre_norm_w + token[0:1, 0:1], g["dy"])
    small = _pack_small(d, dpre_w, g["post_w"], g["a_log"], g["dt_bias"], g["f_bias"], g["gdn_norm_w"], g["fq_w"],
                        g["fk_w"], g["loss"])
    a_conv, a_meta, p_small = _all_gather([g["conv_wt"], dmeta, small], "gather_small_grads")
    p_meta = lax.dynamic_slice_in_dim(a_meta, dev * meta_tokens.shape[1], meta_tokens.shape[1], axis=2)

    r_conv = _adamw_conv(conv_w[0].T, a_conv, jnp.reshape(dev, (1,)).astype(jnp.int32), m_conv_w[0].T, v_conv_w[0].T,
                         "adamw_conv_w")
    r_conv = [o.T for o in r_conv]
    r_meta = _adamw(meta_tokens, p_meta, m_meta_tokens, v_meta_tokens, N_META, "adamw_meta")
    pk = lambda pre, post, a, dt, gw, fq, fk, fb: _pack_small(d, pre, post, a, dt, fb, gw, fq, fk, zero)
    sm = _adamw_small(
        pk(pre_norm_w, post_norm_w, a_log, dt_bias, gdn_norm_w, fox_q_norm_w, fox_k_norm_w, fox_f_bias), p_small,
        pk(m_pre_norm_w, m_post_norm_w, m_a_log, m_dt_bias, m_gdn_norm_w, m_fox_q_norm_w, m_fox_k_norm_w, m_fox_f_bias),
        pk(v_pre_norm_w, v_post_norm_w, v_a_log, v_dt_bias, v_gdn_norm_w, v_fox_q_norm_w, v_fox_k_norm_w, v_fox_f_bias),
        nh, "adamw_small")
    p_in = _chip_exchange_wait(flights, sm[0]["pre"], "chip_exchange_wait")
    r_in = _adamw_column_major(w_in, p_in, m_w_in, v_w_in, "adamw_w_in")

    outs = []
    for i in range(4):
        s = sm[i]
        outs += [r_meta[i], s["pre"], r_in[i], r_conv[i][None], s["a_log"], s["dt_bias"], s["gdn_w"], s["fq_w"],
                 s["fk_w"], s["f_bias"], r_out[i][None], s["post"]]
    return (sm[0]["extra"].reshape(()), grad_x[None], *outs)
```

```python
import jax
import jax.numpy as jnp
from jax import lax
from jax.experimental import pallas as pl
from jax.experimental.pallas import tpu as pltpu

F32, BF16 = jnp.float32, jnp.bfloat16
HEAD_DIM = 128
N_META = 16
CONV_WIDTH = 4
CHUNK = 128
Q_BLOCK = 128
LANES = 128
EPS = 1e-6
PAD_ROWS = Q_BLOCK - N_META
N_DEV = 8
N_CHIP = 4
VMEM_LIMIT = 56 * 1024 * 1024
NEG = -1e30
NARROW = 2 * LANES
MM_TILE = 6 * LANES
DW_IN_PIECES = ((0, 2), (2, 4), (3, 4))

ADAM_LR, ADAM_B1, ADAM_B2, ADAM_EPS, ADAM_WD, ADAM_STEP = 0.001, 0.9, 0.999, 1e-08, 0.01, 10

_DN = {"nn": (((1,), (0,)), ((), ())), "nt": (((1,), (1,)), ((), ())), "tn": (((0,), (0,)), ((), ()))}
_DN3 = {"nn": (((2,), (1,)), ((0,), (0,))), "nt": (((2,), (2,)), ((0,), (0,))), "tn": (((1,), (1,)), ((0,), (0,)))}
_ANY = pl.BlockSpec(memory_space=pl.ANY)
_MESH = pl.DeviceIdType.MESH


def _cp(*sem):
    return pltpu.CompilerParams(dimension_semantics=sem, vmem_limit_bytes=VMEM_LIMIT)


def _dot(a, b, dims="nn", prec=None):
    return lax.dot_general(a, b, _DN[dims], precision=prec, preferred_element_type=F32)


def _bdot(a, b, dims="nn"):
    return _dot(a.astype(BF16), b.astype(BF16), dims)


def _hdot(a, b, dims="nn"):
    return _dot(a, b, dims, prec=lax.Precision.HIGHEST)


def _dot3(a, b, dims="nn"):
    return lax.dot_general(a, b, _DN3[dims], preferred_element_type=F32)


def _bdot3(a, b, dims="nn"):
    return _dot3(a.astype(BF16), b.astype(BF16), dims)


def _split(a):
    hi = a.astype(BF16)
    return hi, (a - hi.astype(F32)).astype(BF16)


def _iota(shape, dim):
    return lax.broadcasted_iota(jnp.int32, shape, dim)


def _sigmoid(z):
    return 1.0 / (1.0 + jnp.exp(-z))


def _softplus(z):
    e = jnp.exp(-jnp.abs(z))
    u = 1.0 + e
    l1p = jnp.where(u == 1.0, e, jnp.log(u) * (e / jnp.where(u == 1.0, 1.0, u - 1.0)))
    return jnp.maximum(z, 0.0) + l1p


def _silu_and_grad(z):
    s = _sigmoid(z)
    return z * s, s * (1.0 + z * (1.0 - s))


def _rms(x):
    return lax.rsqrt(jnp.mean(x * x, axis=-1, keepdims=True) + EPS)


def _h_tile(i, x_ref, meta_ref):
    first = jnp.concatenate([jnp.zeros((PAD_ROWS, x_ref.shape[1]), F32), meta_ref[...]], axis=0)
    return jnp.where(i == 0, first, x_ref[...])


def _x_rows(d):
    return pl.BlockSpec((Q_BLOCK, d), lambda i: (jnp.maximum(i - 1, 0), 0))


def _prenorm(x, meta, w):
    seq, d = x.shape
    lp = seq + Q_BLOCK

    def body(x_ref, m_ref, w_ref, o_ref):
        h = _h_tile(pl.program_id(0), x_ref, m_ref)
        o_ref[...] = (h * _rms(h) * w_ref[...]).astype(BF16)

    return pl.pallas_call(
        body, grid=(lp // Q_BLOCK,),
        in_specs=[_x_rows(d), pl.BlockSpec((N_META, d), lambda i: (0, 0)), pl.BlockSpec((1, d), lambda i: (0, 0))],
        out_specs=pl.BlockSpec((Q_BLOCK, d), lambda i: (i, 0)),
        out_shape=jax.ShapeDtypeStruct((lp, d), BF16), name="prenorm", compiler_params=_cp("parallel"))(x, meta, w)


def _prenorm_gathering(x, w, rider, meta_at):
    seq, d = x.shape
    steps = seq // Q_BLOCK + 1

    def body(start, finish, x_ref, w_ref, o_ref, meta_buf, meta_sem):
        i = pl.program_id(0)
        pl.when(i == 0)(start)

        @pl.when(i < steps - 1)
        def _():
            h = x_ref[...]
            o_ref[...] = (h * _rms(h) * w_ref[...]).astype(BF16)

        @pl.when(i == steps - 1)
        def _():
            finish()
            cp = pltpu.make_async_copy(finish.results[0][meta_at], meta_buf, meta_sem)
            cp.start()
            cp.wait()
            h = jnp.concatenate([jnp.zeros((PAD_ROWS, d), F32), jnp.concatenate([meta_buf[s] for s in range(N_DEV)], axis=1)], axis=0)
            o_ref[...] = (h * _rms(h) * w_ref[...]).astype(BF16)

    (xn,), got = _hosted_call(
        body, [rider], 2, 1, 2, grid=(steps,),
        in_specs=[pl.BlockSpec((Q_BLOCK, d), lambda i: (jnp.minimum(i, steps - 2), 0)), pl.BlockSpec((1, d), lambda i: (0, 0))],
        out_specs=[pl.BlockSpec((Q_BLOCK, d), lambda i: ((i + 1) % steps, 0))],
        out_shape=[jax.ShapeDtypeStruct((seq + Q_BLOCK, d), BF16)],
        scratch_shapes=[pltpu.VMEM((N_DEV, N_META, d // N_DEV), F32), pltpu.SemaphoreType.DMA(())],
        name="prenorm", compiler_params=_cp("arbitrary"))(x, w)
    return xn, got[0]


def _tile(n, want):
    return max(t for t in range(LANES, want + 1, LANES) if n % t == 0)


class _Rider:
    def __init__(self, inputs, out_shapes, scratch, aliases, make):
        self.inputs, self.out_shapes, self.scratch, self.aliases, self.make = inputs, out_shapes, scratch, aliases, make


def _hosted_call(body, riders, n_in, n_out, n_scratch, *, in_specs, out_specs, out_shape, scratch_shapes=(), aliases=None,
                 **kw):
    riders = [r for r in riders if r is not None]
    r_in = [len(r.inputs) for r in riders]
    r_out = [len(r.out_shapes) for r in riders]
    r_scr = [len(r.scratch) for r in riders]
    al = dict(aliases or {})
    for k, r in enumerate(riders):
        al.update({n_in + sum(r_in[:k]) + i: n_out + sum(r_out[:k]) + o for i, o in r.aliases.items()})

    def full_body(*refs):
        ins, rest = refs[:n_in + sum(r_in)], refs[n_in + sum(r_in):]
        outs, scr = rest[:n_out + sum(r_out)], rest[n_out + sum(r_out):]
        hooks = [r.make(ins[n_in + sum(r_in[:k]):n_in + sum(r_in[:k + 1])], outs[n_out + sum(r_out[:k]):n_out + sum(r_out[:k + 1])],
                        scr[n_scratch + sum(r_scr[:k]):n_scratch + sum(r_scr[:k + 1])]) for k, r in enumerate(riders)]

        def start():
            for h in hooks:
                h[0]()

        def finish():
            for h in hooks:
                h[1]()

        finish.results = [outs[n_out + sum(r_out[:k]):n_out + sum(r_out[:k + 1])] for k in range(len(riders))]
        body(start, finish, *ins[:n_in], *outs[:n_out], *scr[:n_scratch])

    call = pl.pallas_call(
        full_body, in_specs=list(in_specs) + [_ANY] * sum(r_in), out_specs=list(out_specs) + [_ANY] * sum(r_out),
        out_shape=list(out_shape) + [s for r in riders for s in r.out_shapes],
        scratch_shapes=list(scratch_shapes) + [s for r in riders for s in r.scratch], input_output_aliases=al, **kw)

    def run(*args):
        res = call(*args, *[t for r in riders for t in r.inputs])
        return res[:n_out], [res[n_out + sum(r_out[:k]):n_out + sum(r_out[:k + 1])] for k in range(len(riders))]

    return run


def _matmul(a, b, dims, tn, out_dtype, name, rider=None, a_cols=None, b_rows=None, acc=None):
    a_shape = a.shape if a_cols is None else (a.shape[0], a_cols[0])
    a_index = 0 if a_cols is None else a_cols[1]
    m = a_shape[1] if dims == "tn" else a_shape[0]
    n = b.shape[0] if dims == "nt" else b.shape[1]
    kdim = b.shape[1] if dims == "nt" else b.shape[0] if b_rows is None else b_rows[0]
    b_index = 0 if b_rows is None else b_rows[1]
    tn = _tile(n, tn)
    steps = n // tn
    b_spec = pl.BlockSpec((tn, kdim), lambda j: (j, 0)) if dims == "nt" else pl.BlockSpec((kdim, tn), lambda j: (b_index, j))
    o_spec = pl.BlockSpec((m, tn), lambda j: (0, j))

    def body(start, finish, a_ref, b_ref, *refs):
        pl.when(pl.program_id(0) == 0)(start)
        prod = _dot(a_ref[...], b_ref[...], dims)
        refs[-1][...] = (prod if acc is None else prod + refs[0][...]).astype(out_dtype)
        pl.when(pl.program_id(0) == steps - 1)(finish)

    (out,), got = _hosted_call(
        body, [rider], 2 + (acc is not None), 1, 0, grid=(steps,),
        in_specs=[pl.BlockSpec(a_shape, lambda j: (0, a_index)), b_spec] + [o_spec] * (acc is not None),
        out_specs=[o_spec], out_shape=[jax.ShapeDtypeStruct((m, n), out_dtype)], aliases={2: 0} if acc is not None else None,
        name=name, compiler_params=_cp("parallel" if rider is None else "arbitrary"))(a, b, *([acc] if acc is not None else []))
    return out if rider is None else (out, got[0])


def _chip_rider(sums):
    def make(ins, outs, scratch):
        local, remote = _chip_exchange_copies(ins[0], outs[0], *scratch)

        def start():
            for cp in [local] + remote:
                cp.start()

        def finish():
            local.wait()
            for cp in remote:
                cp.wait_send()
                cp.wait_recv()

        return start, finish

    return _Rider([sums], [jax.ShapeDtypeStruct(sums.shape, sums.dtype)],
                  [pltpu.SemaphoreType.DMA((N_CHIP - 1,)), pltpu.SemaphoreType.DMA((N_CHIP - 1,)), pltpu.SemaphoreType.DMA((1,))],
                  {}, make)


_HBM = pl.BlockSpec(memory_space=pltpu.HBM)
_SEM = pl.BlockSpec(memory_space=pltpu.SEMAPHORE)


def _chip_exchange_copies(sums_ref, land_ref, send_sems, recv_sems, local_sem):
    x, y, core = lax.axis_index("x"), lax.axis_index("y"), lax.axis_index("c")
    mine = 2 * x + y
    local = pltpu.make_async_copy(sums_ref.at[mine], land_ref.at[mine], local_sem.at[0])
    remote = []
    for k in range(1, N_CHIP):
        px = 1 - x if k & 2 else x
        py = 1 - y if k & 1 else y
        remote.append(pltpu.make_async_remote_copy(
            src_ref=sums_ref.at[2 * px + py], dst_ref=land_ref.at[mine], send_sem=send_sems.at[k - 1],
            recv_sem=recv_sems.at[k - 1], device_id=(px, py, core), device_id_type=_MESH))
    return local, remote


def _chip_exchange_start(sums, name):
    def body(s_ref, send_sems, recv_sems, local_sem, s_thru, land_ref, token):
        local, remote = _chip_exchange_copies(s_ref, land_ref, send_sems, recv_sems, local_sem)
        for cp in [local] + remote:
            cp.start()
        token[...] = jnp.zeros_like(token)

    *flight, token = pl.pallas_call(
        body, name=name,
        out_shape=(pltpu.SemaphoreType.DMA((N_CHIP - 1,)), pltpu.SemaphoreType.DMA((N_CHIP - 1,)), pltpu.SemaphoreType.DMA((1,)),
                   pltpu.HBM(sums.shape, sums.dtype), pltpu.HBM(sums.shape, sums.dtype), jax.ShapeDtypeStruct((8, LANES), F32)),
        in_specs=(_HBM,), out_specs=(_SEM, _SEM, _SEM, _HBM, _HBM, pl.BlockSpec(memory_space=pltpu.VMEM)),
        input_output_aliases={0: 3},
        compiler_params=pltpu.CompilerParams(has_side_effects=pltpu.SideEffectType.DATAFLOW_SIDE_EFFECTING))(
            pltpu.with_memory_space_constraint(sums, pltpu.HBM))
    return flight, token


def _chip_exchange_wait(flights, after, name):
    n = len(flights)

    def body(*refs):
        for i in range(n):
            s_ref, land_ref = refs[2 * i:2 * i + 2]
            local, remote = _chip_exchange_copies(s_ref, land_ref, *refs[2 * n + 3 * i:2 * n + 3 * i + 3])
            local.wait()
            for cp in remote:
                cp.wait_send()
                cp.wait_recv()

    buffers = [b for f in flights for b in f[3:]]
    res = pl.pallas_call(
        body, name=name, out_shape=tuple(pltpu.HBM(b.shape, b.dtype) for b in buffers),
        in_specs=(_HBM,) * (2 * n) + (_SEM,) * (3 * n) + (_ANY,), out_specs=(_HBM,) * (2 * n),
        input_output_aliases={i: i for i in range(2 * n)},
        compiler_params=pltpu.CompilerParams(has_side_effects=pltpu.SideEffectType.DATAFLOW_SIDE_EFFECTING))(
            *buffers, *[s for f in flights for s in f[:3]], after)
    return res[1::2]


def _dxn(dproj, wfull, riders, tk, name):
    m, k = dproj.shape
    n = wfull.shape[0]
    tk = _tile(k, tk)
    steps = k // tk

    def body(start, finish, a_ref, b_ref, o_ref):
        j = pl.program_id(0)

        @pl.when(j == 0)
        def _():
            start()
            o_ref[...] = jnp.zeros_like(o_ref)
        o_ref[...] += _dot(a_ref[...], b_ref[...], "nt")
        pl.when(j == steps - 1)(finish)

    (dxn,), got = _hosted_call(
        body, riders, 2, 1, 0, grid=(steps,),
        in_specs=[pl.BlockSpec((m, tk), lambda j: (0, j)), pl.BlockSpec((n, tk), lambda j: (0, j))],
        out_specs=[pl.BlockSpec((m, n), lambda j: (0, 0))], out_shape=[jax.ShapeDtypeStruct((m, n), F32)],
        name=name, compiler_params=_cp("arbitrary"))(dproj, wfull)
    return dxn, got


def _conv_taps(x, w):
    c = x * w[CONV_WIDTH - 1:CONV_WIDTH, :]
    for j in range(CONV_WIDTH - 1):
        c = c + pltpu.roll(x, CONV_WIDTH - 1 - j, 0) * w[j:j + 1, :]
    return c


def _gdn_prep(proj, conv_wt, nh):
    lp = proj.shape[0]
    scale = HEAD_DIM ** -0.5

    def body(x_ref, w_ref, o_ref):
        which = pl.program_id(0) // nh
        c = _conv_taps(x_ref[...], w_ref[...])
        s = c * _sigmoid(c)
        r = lax.rsqrt(jnp.sum(s * s, axis=-1, keepdims=True) + EPS)
        f = jnp.where(which == 0, r * scale, jnp.where(which == 1, r, 1.0))
        o_ref[...] = jnp.where(_iota(s.shape, 0) >= PAD_ROWS, s * f, 0.0)

    return pl.pallas_call(
        body, grid=(3 * nh,),
        in_specs=[pl.BlockSpec((lp, LANES), lambda s: (0, s)), pl.BlockSpec((CONV_WIDTH, LANES), lambda s: (0, s))],
        out_specs=pl.BlockSpec((lp, LANES), lambda s: (0, s)),
        out_shape=jax.ShapeDtypeStruct((lp, 3 * nh * HEAD_DIM), F32), name="gdn_prep",
        compiler_params=_cp("parallel"))(proj, conv_wt)


def _gdn_prep_bwd(proj, conv_wt, dq, dk, dv, dproj, nh):
    lp = proj.shape[0]
    scale = HEAD_DIM ** -0.5
    part = lambda p: pl.BlockSpec((lp, LANES), lambda s: (0, jnp.clip(s - p * nh, 0, nh - 1)))

    def body(x_ref, w_ref, dq_ref, dk_ref, dv_ref, _, dx_ref, dw_ref):
        which = pl.program_id(0) // nh
        x = x_ref[...]
        w = w_ref[...]
        c = _conv_taps(x, w)
        sg = _sigmoid(c)
        s = c * sg
        r = lax.rsqrt(jnp.sum(s * s, axis=-1, keepdims=True) + EPS)
        dy = jnp.where(which == 0, dq_ref[...], jnp.where(which == 1, dk_ref[...], dv_ref[...]))
        dy = jnp.where(_iota(s.shape, 0) >= PAD_ROWS, dy, 0.0)
        y0 = s * r
        dy0 = dy * jnp.where(which == 0, scale, 1.0)
        ds_n = r * (dy0 - y0 * jnp.sum(dy0 * y0, axis=-1, keepdims=True))
        ds = jnp.where(which == 2, dy, ds_n)
        dc = ds * (sg * (1.0 + c * (1.0 - sg)))
        dx = dc * w[CONV_WIDTH - 1:CONV_WIDTH, :]
        rows = [jnp.sum(dc * x, axis=0, keepdims=True)]
        for j in range(CONV_WIDTH - 2, -1, -1):
            sh = CONV_WIDTH - 1 - j
            dx = dx + pltpu.roll(dc, lp - sh, 0) * w[j:j + 1, :]
            rows.insert(0, jnp.sum(dc * pltpu.roll(x, sh, 0), axis=0, keepdims=True))
        dx_ref[...] = dx.astype(BF16)
        dw_ref[...] = jnp.concatenate(rows, axis=0)

    strip = pl.BlockSpec((lp, LANES), lambda s: (0, s))
    taps = pl.BlockSpec((CONV_WIDTH, LANES), lambda s: (0, s))
    return pl.pallas_call(
        body, grid=(3 * nh,), in_specs=[strip, taps, part(0), part(1), part(2), _ANY], out_specs=[strip, taps],
        out_shape=[jax.ShapeDtypeStruct(dproj.shape, BF16), jax.ShapeDtypeStruct((CONV_WIDTH, 3 * nh * HEAD_DIM), F32)],
        input_output_aliases={5: 0}, name="gdn_prep_bwd", compiler_params=_cp("parallel"))(proj, conv_wt, dq, dk, dv, dproj)


def _gates(proj, bias_row, nega_row, nh):
    lp = proj.shape[0]
    nc = lp // CHUNK

    def body(p_ref, b_ref, a_ref, g_ref, gt3_ref, gtf_ref):
        lane = _iota((CHUNK, LANES), 1)
        tri = (_iota((CHUNK, CHUNK), 0) >= _iota((CHUNK, CHUNK), 1)).astype(F32)

        def step(n, carry):
            r0 = pl.multiple_of(n * CHUNK, CHUNK)
            z = p_ref[pl.ds(r0, CHUNK), :] + b_ref[...]
            base = jnp.where(lane < nh, _sigmoid(z),
                             jnp.where(lane < 2 * nh, a_ref[...] * _softplus(z),
                                       jnp.where(lane < 3 * nh, -_softplus(-z), 0.0)))
            base = jnp.where(r0 + _iota((CHUNK, LANES), 0) >= PAD_ROWS, base, 0.0)
            cs = _hdot(tri, base)
            run = jnp.where((lane >= 2 * nh) & (lane < 3 * nh), cs + carry, cs)
            sh = pltpu.roll(run, 2 * nh, 1)
            out = base + jnp.where((lane >= 3 * nh) & (lane < 5 * nh), sh, 0.0)
            g_ref[pl.ds(r0, CHUNK), :] = out
            gt3_ref[n] = out.T
            return carry + cs[CHUNK - 1:CHUNK, :]

        lax.fori_loop(0, nc, step, jnp.zeros((1, LANES), F32))
        gtf_ref[...] = g_ref[...].T

    vec = pl.BlockSpec((1, LANES), lambda i: (0, 0))
    return pl.pallas_call(
        body, grid=(1,), in_specs=[pl.BlockSpec((lp, LANES), lambda i: (0, 8 * nh)), vec, vec],
        out_specs=[pl.BlockSpec((lp, LANES), lambda i: (0, 0)), pl.BlockSpec((nc, LANES, CHUNK), lambda i: (0, 0, 0)),
                   pl.BlockSpec((LANES, lp), lambda i: (0, 0))],
        out_shape=[jax.ShapeDtypeStruct((lp, LANES), F32), jax.ShapeDtypeStruct((nc, LANES, CHUNK), F32),
                   jax.ShapeDtypeStruct((LANES, lp), F32)],
        name="gates", compiler_params=_cp("arbitrary"))(proj, bias_row, nega_row)


def _gates_bwd(proj, bias_row, nega_row, gates, dgate_gdn, dc_t, dproj, nh):
    lp = proj.shape[0]
    nc = lp // CHUNK

    def body(p_ref, b_ref, a_ref, g_ref, dg_ref, dc_ref, _, dz_ref, sm_ref, dct_scr):
        lane = _iota((CHUNK, LANES), 1)
        triu = (_iota((CHUNK, CHUNK), 0) <= _iota((CHUNK, CHUNK), 1)).astype(F32)
        dct_scr[...] = dc_ref[...].T
        sm_ref[...] = jnp.zeros_like(sm_ref)
        dz_ref[:, LANES:] = jnp.zeros((lp, NARROW - LANES), BF16)

        def step(i, carry):
            n = nc - 1 - i
            r0 = pl.multiple_of(n * CHUNK, CHUNK)
            z = p_ref[pl.ds(r0, CHUNK), :] + b_ref[...]
            gt = g_ref[pl.ds(r0, CHUNK), :]
            dgd = dg_ref[pl.ds(r0, CHUNK), :]
            dch = dct_scr[pl.ds(r0, CHUNK), :]
            rc = _hdot(triu, dch) + carry
            sg = _sigmoid(z)
            dz = jnp.where(lane < nh, dgd * sg * (1.0 - sg),
                           jnp.where(lane < 2 * nh, dgd * a_ref[...] * sg,
                                     jnp.where(lane < 3 * nh, rc * (1.0 - sg), 0.0)))
            dz = jnp.where(r0 + _iota((CHUNK, LANES), 0) >= PAD_ROWS, dz, 0.0)
            dz_ref[pl.ds(r0, CHUNK), 0:LANES] = dz.astype(BF16)
            sm_ref[0:1, :] += jnp.sum(dz, axis=0, keepdims=True)
            sm_ref[1:2, :] += jnp.sum(jnp.where((lane >= nh) & (lane < 2 * nh), dgd * gt, 0.0), axis=0, keepdims=True)
            return carry + jnp.sum(dch, axis=0, keepdims=True)

        lax.fori_loop(0, nc, step, jnp.zeros((1, LANES), F32))

    vec = pl.BlockSpec((1, LANES), lambda i: (0, 0))
    full = pl.BlockSpec((lp, LANES), lambda i: (0, 0))
    last = pl.BlockSpec((lp, LANES), lambda i: (0, 8 * nh))
    tail = pl.BlockSpec((lp, NARROW), lambda i: (0, 8 * nh * LANES // NARROW))
    return pl.pallas_call(
        body, grid=(1,), in_specs=[last, vec, vec, full, full, pl.BlockSpec((LANES, lp), lambda i: (0, 0)), _ANY],
        out_specs=[tail, pl.BlockSpec((8, LANES), lambda i: (0, 0))],
        out_shape=[jax.ShapeDtypeStruct(dproj.shape, BF16), jax.ShapeDtypeStruct((8, LANES), F32)],
        scratch_shapes=[pltpu.VMEM((lp, LANES), F32)], input_output_aliases={6: 0},
        name="gates_bwd", compiler_params=_cp("arbitrary"))(proj, bias_row, nega_row, gates, dgate_gdn, dc_t, dproj)


def _tri_inv(a):
    t = jnp.where(_iota(a.shape, 1) == _iota(a.shape, 2), 1.0, 0.0) - a
    p = a
    for _ in range(CHUNK.bit_length() - 2):
        ph, pw = _split(p)
        p = _dot3(ph, ph) + (_dot3(ph, pw) + _dot3(pw, ph))
        ph, pw = _split(p)
        th, tw = _split(t)
        t = t + (_dot3(th, ph) + (_dot3(th, pw) + _dot3(tw, ph)))
    return t


def _gdn_chunk(q, k, v, beta, gc, gr, t=None):
    ii, jj = _iota((1, CHUNK, CHUNK), 1), _iota((1, CHUNK, CHUNK), 2)
    causal, strict = ii >= jj, ii > jj
    dm = jnp.where(causal, jnp.exp(jnp.where(causal, gc - gr, 0.0)), 0.0)
    kk = _bdot3(k, k, "nt")
    a = jnp.where(strict, beta * kk * dm, 0.0)
    if t is None:
        t = _tri_inv(a)
    eg = jnp.exp(gc)
    glast = gc[:, CHUNK - 1:CHUNK, :]
    ekd = jnp.exp(glast - gc)
    bv = beta * v
    bk = (beta * eg) * k
    ub = _bdot3(t, jnp.concatenate([bv, bk], axis=2))
    qk = _bdot3(q, k, "nt")
    return dict(causal=causal, strict=strict, dm=dm, kk=kk, a=a, t=t, eg=eg, ekd=ekd, bv=bv, bk=bk,
                u=ub[:, :, :HEAD_DIM], w=ub[:, :, HEAD_DIM:], qk=qk, aqk=jnp.where(causal, qk * dm, 0.0),
                q_dec=q * eg, k_dec=k * ekd, decay=jnp.exp(glast))


def _heads(ref, nh):
    return jnp.stack([ref[:, h * HEAD_DIM:(h + 1) * HEAD_DIM] for h in range(nh)], axis=0)


def _gdn_chunk_inputs(q_ref, k_ref, v_ref, g, gt, nh):
    col = lambda o: jnp.stack([g[:, o + h:o + h + 1] for h in range(nh)], axis=0)
    gr = jnp.stack([gt[3 * nh + h:3 * nh + h + 1, :] for h in range(nh)], axis=0)
    return _heads(q_ref, nh), _heads(k_ref, nh), _heads(v_ref, nh), col(0), col(3 * nh), gr


def _gdn_fwd(qkv, gates, gt3, nh, rider=None):
    lp = qkv.shape[0]
    nc = lp // CHUNK
    w = nh * HEAD_DIM

    def body(start, finish, q_ref, k_ref, v_ref, g_ref, gt_ref, o_ref, sall_ref, tall_ref, s_scr):
        @pl.when(pl.program_id(0) == 0)
        def _():
            start()
            s_scr[...] = jnp.zeros_like(s_scr)
        c = _gdn_chunk(*_gdn_chunk_inputs(q_ref, k_ref, v_ref, g_ref[...], gt_ref[0], nh))
        s = s_scr[...]
        sall_ref[0] = s
        tall_ref[0] = c["t"]
        v_new = c["u"] - _bdot3(c["w"], s)
        o = _bdot3(c["q_dec"], s) + _bdot3(c["aqk"], v_new)
        s_scr[...] = s * c["decay"] + _bdot3(c["k_dec"], v_new, "tn")
        for h in range(nh):
            o_ref[:, h * HEAD_DIM:(h + 1) * HEAD_DIM] = o[h]
        pl.when(pl.program_id(0) == nc - 1)(finish)

    outs, got = _hosted_call(
        body, [rider], 5, 3, 1, grid=(nc,),
        in_specs=[pl.BlockSpec((CHUNK, w), lambda n: (n, 0)), pl.BlockSpec((CHUNK, w), lambda n: (n, 1)),
                  pl.BlockSpec((CHUNK, w), lambda n: (n, 2)), pl.BlockSpec((CHUNK, LANES), lambda n: (n, 0)),
                  pl.BlockSpec((1, LANES, CHUNK), lambda n: (n, 0, 0))],
        out_specs=[pl.BlockSpec((CHUNK, w), lambda n: (n, 0)),
                   pl.BlockSpec((1, nh, HEAD_DIM, HEAD_DIM), lambda n: (n, 0, 0, 0)),
                   pl.BlockSpec((1, nh, CHUNK, CHUNK), lambda n: (n, 0, 0, 0))],
        out_shape=[jax.ShapeDtypeStruct((lp, w), F32), jax.ShapeDtypeStruct((nc, nh, HEAD_DIM, HEAD_DIM), F32),
                   jax.ShapeDtypeStruct((nc, nh, CHUNK, CHUNK), F32)],
        scratch_shapes=[pltpu.VMEM((nh, HEAD_DIM, HEAD_DIM), F32)],
        name="gdn_fwd", compiler_params=_cp("arbitrary"))(qkv, qkv, qkv, gates, gt3)
    return outs, (got[0] if got else None)


def _gdn_bwd(qkv, gates, gt3, s_all, t_all, do, nh, rider=None):
    lp = qkv.shape[0]
    nc = lp // CHUNK
    w = nh * HEAD_DIM
    rev = lambda n: nc - 1 - n

    def body(start, finish, q_ref, k_ref, v_ref, g_ref, gt_ref, s_ref, t_ref, do_ref, dq_ref, dk_ref, dv_ref, dg_ref, ds_scr):
        @pl.when(pl.program_id(0) == 0)
        def _():
            start()
            ds_scr[...] = jnp.zeros_like(ds_scr)
        q, k, v, beta, gc, gr = _gdn_chunk_inputs(q_ref, k_ref, v_ref, g_ref[...], gt_ref[0], nh)
        c = _gdn_chunk(q, k, v, beta, gc, gr, t_ref[0])
        s = s_ref[0]
        dsn = ds_scr[...]
        dout = _heads(do_ref, nh)
        v_new = c["u"] - _bdot3(c["w"], s)
        dq_dec = _bdot3(dout, s, "nt")
        daqk = jnp.where(c["causal"], _bdot3(dout, v_new, "nt"), 0.0)
        dv_new = _bdot3(c["aqk"], dout, "tn") + _bdot3(c["k_dec"], dsn)
        dk_dec = _bdot3(v_new, dsn, "nt")
        ddecay = jnp.sum(jnp.sum(dsn * s, axis=2, keepdims=True), axis=1, keepdims=True)
        dw = -_bdot3(dv_new, s, "nt")
        ds_scr[...] = _bdot3(c["q_dec"], dout, "tn") + c["decay"] * dsn - _bdot3(c["w"], dv_new, "tn")
        duw = jnp.concatenate([dv_new, dw], axis=2)
        dt = _bdot3(duw, jnp.concatenate([c["bv"], c["bk"]], axis=2), "nt")
        dbvk = _bdot3(c["t"], duw, "tn")
        dbv, dbk = dbvk[:, :, :HEAD_DIM], dbvk[:, :, HEAD_DIM:]
        da = jnp.where(c["strict"], -_bdot3(_bdot3(c["t"], dt, "tn"), c["t"], "nt"), 0.0)
        dkk = da * beta * c["dm"]
        dqk = daqk * c["dm"]
        e = da * c["a"] + daqk * c["aqk"]
        dq = dq_dec * c["eg"] + _bdot3(dqk, k)
        dk = (dk_dec * c["ekd"] + _bdot3(dkk, k) + _bdot3(dkk, k, "tn") + _bdot3(dqk, q, "tn")
              + (beta * c["eg"]) * dbk)
        dv = beta * dbv
        rs = lambda x: jnp.sum(x, axis=2, keepdims=True)
        dbeta = rs(dbv * v) + c["eg"] * rs(dbk * k) + rs(da * c["kk"] * c["dm"])
        kd_term = rs(dk_dec * c["k_dec"])
        eh, ew = _split(e)
        ones = jnp.ones((nh, CHUNK, LANES), BF16)
        col_sums = (_dot3(eh, ones, "tn") + _dot3(ew, ones, "tn"))[:, :, 0:1]
        dg_cum = rs(dq_dec * c["q_dec"]) - kd_term + rs(dbk * c["bk"]) + rs(e) - col_sums
        last = jnp.sum(kd_term, axis=1, keepdims=True) + ddecay * c["decay"]
        dg_cum = dg_cum + jnp.where(_iota((1, CHUNK, 1), 1) == CHUNK - 1, last, 0.0)
        lane = _iota((CHUNK, LANES), 1)
        acc = jnp.zeros((CHUNK, LANES), F32)
        for h in range(nh):
            sl = slice(h * HEAD_DIM, (h + 1) * HEAD_DIM)
            dq_ref[:, sl] = dq[h]
            dk_ref[:, sl] = dk[h]
            dv_ref[:, sl] = dv[h]
            acc = acc + jnp.where(lane == h, dbeta[h], 0.0) + jnp.where(lane == nh + h, dg_cum[h], 0.0)
        triu = (_iota((CHUNK, CHUNK), 0) <= _iota((CHUNK, CHUNK), 1)).astype(F32)
        dg_ref[...] = jnp.where(lane < nh, acc, _hdot(triu, acc))
        pl.when(pl.program_id(0) == nc - 1)(finish)

    outs, got = _hosted_call(
        body, [rider], 8, 4, 1, grid=(nc,),
        in_specs=[pl.BlockSpec((CHUNK, w), lambda n: (rev(n), 0)), pl.BlockSpec((CHUNK, w), lambda n: (rev(n), 1)),
                  pl.BlockSpec((CHUNK, w), lambda n: (rev(n), 2)), pl.BlockSpec((CHUNK, LANES), lambda n: (rev(n), 0)),
                  pl.BlockSpec((1, LANES, CHUNK), lambda n: (rev(n), 0, 0)),
                  pl.BlockSpec((1, nh, HEAD_DIM, HEAD_DIM), lambda n: (rev(n), 0, 0, 0)),
                  pl.BlockSpec((1, nh, CHUNK, CHUNK), lambda n: (rev(n), 0, 0, 0)),
                  pl.BlockSpec((CHUNK, w), lambda n: (rev(n), 0))],
        out_specs=[pl.BlockSpec((CHUNK, w), lambda n: (rev(n), 0))] * 3 + [pl.BlockSpec((CHUNK, LANES), lambda n: (rev(n), 0))],
        out_shape=[jax.ShapeDtypeStruct((lp, w), F32)] * 3 + [jax.ShapeDtypeStruct((lp, LANES), F32)],
        scratch_shapes=[pltpu.VMEM((nh, HEAD_DIM, HEAD_DIM), F32)],
        name="gdn_bwd", compiler_params=_cp("arbitrary"))(qkv, qkv, qkv, gates, gt3, s_all, t_all, do)
    return outs, (got[0] if got else None)


def _merge_gdn(o_gdn, proj, norm_w, nh):
    lp = o_gdn.shape[0]

    def body(o_ref, z_ref, w_ref, m_ref):
        o = o_ref[...]
        z = z_ref[...]
        m_ref[...] = (o * _rms(o) * w_ref[...] * (z * _sigmoid(z))).astype(BF16)

    return pl.pallas_call(
        body, grid=(nh,),
        in_specs=[pl.BlockSpec((lp, LANES), lambda s: (0, s)), pl.BlockSpec((lp, LANES), lambda s: (0, 3 * nh + s)),
                  pl.BlockSpec((1, LANES), lambda s: (0, 0))],
        out_specs=pl.BlockSpec((lp, LANES), lambda s: (0, s)),
        out_shape=jax.ShapeDtypeStruct((lp, 2 * nh * HEAD_DIM), BF16), name="merge_gdn",
        compiler_params=_cp("parallel"))(o_gdn, proj, norm_w)


def _merge_gdn_bwd(o_gdn, proj, norm_w, dmerged, nh):
    lp = o_gdn.shape[0]

    def body(o_ref, z_ref, w_ref, dm_ref, do_ref, dz_ref, dw_ref):
        o = o_ref[...]
        r = _rms(o)
        xh = o * r
        silu, dsilu = _silu_and_grad(z_ref[...])
        dm = dm_ref[...]
        dn = dm * silu
        dz_ref[...] = (dm * (xh * w_ref[...]) * dsilu).astype(BF16)
        dnw = dn * w_ref[...]
        do_ref[...] = r * (dnw - xh * jnp.mean(dnw * xh, axis=-1, keepdims=True))

        @pl.when(pl.program_id(0) == 0)
        def _():
            dw_ref[...] = jnp.zeros_like(dw_ref)
        dw_ref[...] += jnp.sum(dn * xh, axis=0, keepdims=True)

    w = nh * HEAD_DIM
    return pl.pallas_call(
        body, grid=(nh,),
        in_specs=[pl.BlockSpec((lp, LANES), lambda s: (0, s)), pl.BlockSpec((lp, LANES), lambda s: (0, 3 * nh + s)),
                  pl.BlockSpec((1, LANES), lambda s: (0, 0)), pl.BlockSpec((lp, LANES), lambda s: (0, s))],
        out_specs=[pl.BlockSpec((lp, LANES), lambda s: (0, s)), pl.BlockSpec((lp, LANES), lambda s: (0, 3 * nh + s)),
                   pl.BlockSpec((1, LANES), lambda s: (0, 0))],
        out_shape=[jax.ShapeDtypeStruct((lp, w), F32), jax.ShapeDtypeStruct((lp, 8 * w + NARROW), BF16),
                   jax.ShapeDtypeStruct((1, LANES), F32)],
        name="merge_gdn_bwd", compiler_params=_cp("arbitrary"))(o_gdn, proj, norm_w, dmerged)


def _fox_prep(proj, qk_w, nh):
    lp = proj.shape[0]

    def body(x_ref, w_ref, o_ref):
        x = x_ref[...]
        o_ref[...] = x * _rms(x) * w_ref[0]

    return pl.pallas_call(
        body, grid=(2 * nh,),
        in_specs=[pl.BlockSpec((lp, LANES), lambda s: (0, 4 * nh + s)), pl.BlockSpec((1, 1, LANES), lambda s: (s // nh, 0, 0))],
        out_specs=pl.BlockSpec((lp, LANES), lambda s: (0, s)),
        out_shape=jax.ShapeDtypeStruct((lp, 2 * nh * HEAD_DIM), F32), name="fox_prep",
        compiler_params=_cp("parallel"))(proj, qk_w)


def _fox_prep_bwd(proj, qk_w, dq, dk, dproj, nh):
    lp = proj.shape[0]
    part = lambda p: pl.BlockSpec((lp, LANES), lambda s: (0, jnp.clip(s - p * nh, 0, nh - 1)))

    def body(x_ref, w_ref, dq_ref, dk_ref, _, dx_ref, dw_ref):
        x = x_ref[...]
        r = _rms(x)
        xh = x * r
        dy = jnp.where(pl.program_id(0) < nh, dq_ref[...], dk_ref[...])
        dyw = dy * w_ref[0]
        dx_ref[...] = (r * (dyw - xh * jnp.mean(dyw * xh, axis=-1, keepdims=True))).astype(BF16)

        @pl.when(pl.program_id(0) % nh == 0)
        def _():
            dw_ref[...] = jnp.zeros_like(dw_ref)
        dw_ref[0] += jnp.sum(dy * xh, axis=0, keepdims=True)

    strip = pl.BlockSpec((lp, LANES), lambda s: (0, 4 * nh + s))
    wsp = pl.BlockSpec((1, 1, LANES), lambda s: (s // nh, 0, 0))
    return pl.pallas_call(
        body, grid=(2 * nh,), in_specs=[strip, wsp, part(0), part(1), _ANY], out_specs=[strip, wsp],
        out_shape=[jax.ShapeDtypeStruct(dproj.shape, BF16), jax.ShapeDtypeStruct((2, 1, LANES), F32)],
        input_output_aliases={4: 0}, name="fox_prep_bwd", compiler_params=_cp("arbitrary"))(proj, qk_w, dq, dk, dproj)


def _fox_probs(q, k, gates, crow, h, i, nh, lse=None):
    kl = k.shape[0]
    lane = _iota((Q_BLOCK, LANES), 1)
    ct = jnp.sum(jnp.where(lane == 4 * nh + h, gates, 0.0), axis=1, keepdims=True)
    tq, kq = _iota((Q_BLOCK, Q_BLOCK), 0), _iota((Q_BLOCK, Q_BLOCK), 1)
    qs = q * (HEAD_DIM ** -0.5)
    if i == 0:
        s = _bdot(qs, k, "nt") + (ct - crow)
        s = jnp.where((kq <= tq) & ((kq >= PAD_ROWS) | (tq < PAD_ROWS)), s, NEG)
    else:
        crow = jnp.where(_iota((1, kl), 1) < PAD_ROWS, -NEG, crow)
        s = _bdot(qs, k, "nt") + (ct - crow)
        s = jnp.concatenate([s[:, :kl - Q_BLOCK], jnp.where(kq <= tq, s[:, kl - Q_BLOCK:], NEG)], axis=1)
    if lse is not None:
        return jnp.exp(s - lse)
    m = jnp.max(s, axis=1, keepdims=True)
    p = jnp.exp(s - m)
    tot = jnp.sum(p, axis=1, keepdims=True)
    return p / tot, m + jnp.log(tot)


FOX_HEADS_PER_STEP = 2


def _fox_specs(lp, nh):
    hw = FOX_HEADS_PER_STEP * LANES
    return [pl.BlockSpec((Q_BLOCK, hw), lambda g, i: (i, g)),
            pl.BlockSpec((lp, hw), lambda g, i: (0, nh // FOX_HEADS_PER_STEP + g)),
            pl.BlockSpec((lp, hw), lambda g, i: (0, 6 * nh // FOX_HEADS_PER_STEP + g)),
            pl.BlockSpec((Q_BLOCK, LANES), lambda g, i: (i, 0)),
            pl.BlockSpec((LANES, lp), lambda g, i: (0, 0))]


def _fox_fwd(qkn, proj, gates, gtf, nh):
    lp = qkn.shape[0]

    def body(q_ref, k_ref, v_ref, g_ref, gt_ref, o_ref, lse_ref):
        g, i = pl.program_id(0), pl.program_id(1)
        for j in range(lp // Q_BLOCK):
            @pl.when(i == j)
            def _(j=j):
                kl = (j + 1) * Q_BLOCK
                for hh in range(FOX_HEADS_PER_STEP):
                    h = FOX_HEADS_PER_STEP * g + hh
                    sl = slice(hh * LANES, (hh + 1) * LANES)
                    p, lse = _fox_probs(q_ref[:, sl], k_ref[0:kl, sl], g_ref[...], gt_ref[pl.ds(4 * nh + h, 1), :][:, 0:kl],
                                        h, j, nh)
                    o_ref[:, sl] = _bdot(p, v_ref[0:kl, sl])
                    lse_ref[:, sl] = jnp.broadcast_to(lse, (Q_BLOCK, LANES))

    blk = pl.BlockSpec((Q_BLOCK, FOX_HEADS_PER_STEP * LANES), lambda g, i: (i, g))
    return pl.pallas_call(
        body, grid=(nh // FOX_HEADS_PER_STEP, lp // Q_BLOCK), in_specs=_fox_specs(lp, nh), out_specs=[blk, blk],
        out_shape=[jax.ShapeDtypeStruct((lp, nh * HEAD_DIM), F32)] * 2, name="fox_fwd",
        compiler_params=_cp("parallel", "parallel"))(qkn, qkn, proj, gates, gtf)


def _fox_bwd(qkn, proj, gates, gtf, lse, do, dproj, nh):
    lp = qkn.shape[0]
    nq = lp // Q_BLOCK
    w = nh * HEAD_DIM
    scale = HEAD_DIM ** -0.5

    def body(q_ref, k_ref, v_ref, g_ref, gt_ref, lse_ref, do_ref, _, dq_ref, dk_ref, dc_ref, dv_ref, dv_scr):
        g, i = pl.program_id(0), pl.program_id(1)

        @pl.when(i == 0)
        def _():
            dk_ref[...] = jnp.zeros_like(dk_ref)
            dv_scr[...] = jnp.zeros_like(dv_scr)
            dc_ref[...] = jnp.zeros_like(dc_ref)
        for j in range(nq):
            @pl.when(i == j)
            def _(j=j):
                kl = (j + 1) * Q_BLOCK
                for hh in range(FOX_HEADS_PER_STEP):
                    h = FOX_HEADS_PER_STEP * g + hh
                    sl = slice(hh * LANES, (hh + 1) * LANES)
                    q, k = q_ref[:, sl], k_ref[0:kl, sl]
                    p = _fox_probs(q, k, g_ref[...], gt_ref[pl.ds(4 * nh + h, 1), :][:, 0:kl], h, j, nh,
                                   lse_ref[:, sl][:, 0:1])
                    dout = do_ref[:, sl]
                    dp = _bdot(dout, v_ref[0:kl, sl], "nt")
                    ds = p * (dp - jnp.sum(p * dp, axis=1, keepdims=True))
                    dq_ref[:, sl] = _bdot(ds, k) * scale
                    dk_ref[0:kl, sl] += _bdot(ds, q * scale, "tn")
                    dv_scr[0:kl, sl] += _bdot(p, dout, "tn")
                    dc_ref[hh, :, 0:kl] -= jnp.sum(ds, axis=0, keepdims=True)

        @pl.when(i == nq - 1)
        def _():
            dv_ref[...] = dv_scr[...].astype(BF16)

    hw = FOX_HEADS_PER_STEP * LANES
    blk = pl.BlockSpec((Q_BLOCK, hw), lambda g, i: (i, g))
    col = pl.BlockSpec((lp, hw), lambda g, i: (0, g))
    return pl.pallas_call(
        body, grid=(nh // FOX_HEADS_PER_STEP, nq), in_specs=_fox_specs(lp, nh) + [blk, blk, _ANY],
        out_specs=[blk, col, pl.BlockSpec((FOX_HEADS_PER_STEP, 1, lp), lambda g, i: (g, 0, 0)),
                   pl.BlockSpec((lp, hw), lambda g, i: (0, 6 * nh // FOX_HEADS_PER_STEP + g))],
        out_shape=[jax.ShapeDtypeStruct((lp, w), F32)] * 2 + [jax.ShapeDtypeStruct((nh, 1, lp), F32),
                                                             jax.ShapeDtypeStruct(dproj.shape, BF16)],
        scratch_shapes=[pltpu.VMEM((lp, hw), F32)], input_output_aliases={7: 3},
        name="fox_bwd", compiler_params=_cp("parallel", "arbitrary"))(qkn, qkn, proj, gates, gtf, lse, do, dproj)


def _merge_fox(o_fox, proj, merged, nh):
    lp = o_fox.shape[0]

    def body(o_ref, z_ref, _, m_ref):
        z = z_ref[...]
        m_ref[...] = (o_ref[...] * (z * _sigmoid(z))).astype(BF16)

    return pl.pallas_call(
        body, grid=(nh,),
        in_specs=[pl.BlockSpec((lp, LANES), lambda s: (0, s)), pl.BlockSpec((lp, LANES), lambda s: (0, 7 * nh + s)), _ANY],
        out_specs=pl.BlockSpec((lp, LANES), lambda s: (0, nh + s)),
        out_shape=jax.ShapeDtypeStruct(merged.shape, BF16), input_output_aliases={2: 0}, name="merge_fox",
        compiler_params=_cp("parallel"))(o_fox, proj, merged)


def _merge_fox_bwd(o_fox, proj, dmerged, dproj, nh):
    lp = o_fox.shape[0]

    def body(o_ref, z_ref, dm_ref, _, do_ref, dz_ref):
        silu, dsilu = _silu_and_grad(z_ref[...])
        dm = dm_ref[...]
        do_ref[...] = dm * silu
        dz_ref[...] = (dm * o_ref[...] * dsilu).astype(BF16)

    w = nh * HEAD_DIM
    return pl.pallas_call(
        body, grid=(nh,),
        in_specs=[pl.BlockSpec((lp, LANES), lambda s: (0, s)), pl.BlockSpec((lp, LANES), lambda s: (0, 7 * nh + s)),
                  pl.BlockSpec((lp, LANES), lambda s: (0, nh + s)), _ANY],
        out_specs=[pl.BlockSpec((lp, LANES), lambda s: (0, s)), pl.BlockSpec((lp, LANES), lambda s: (0, 7 * nh + s))],
        out_shape=[jax.ShapeDtypeStruct((lp, w), F32), jax.ShapeDtypeStruct(dproj.shape, BF16)],
        input_output_aliases={3: 1}, name="merge_fox_bwd", compiler_params=_cp("parallel"))(o_fox, proj, dmerged, dproj)


def _post(out, x, target, post_w):
    lp, d = out.shape

    def body(o_ref, x_ref, t_ref, w_ref, dy_ref, do_ref, loss_ref, dw_ref):
        i = pl.program_id(0)

        @pl.when(i == 0)
        def _():
            loss_ref[...] = jnp.zeros_like(loss_ref)
            dw_ref[...] = jnp.zeros_like(dw_ref)
        o = o_ref[...]
        r = _rms(o)
        nrm = o * r
        err = jnp.where(i > 0, x_ref[...] + nrm * w_ref[...] - t_ref[...], 0.0)
        loss_ref[0:1, :] += 0.5 * jnp.sum(jnp.sum(err * err, axis=1, keepdims=True), axis=0, keepdims=True) / d
        dy = err / d
        dy_ref[...] = dy
        dw_ref[...] += jnp.sum(dy * nrm, axis=0, keepdims=True)
        dyw = dy * w_ref[...]
        do_ref[...] = (r * (dyw - nrm * jnp.mean(dyw * nrm, axis=-1, keepdims=True))).astype(BF16)

    row = pl.BlockSpec((Q_BLOCK, d), lambda i: (i, 0))
    vec = pl.BlockSpec((1, d), lambda i: (0, 0))
    return pl.pallas_call(
        body, grid=(lp // Q_BLOCK,), in_specs=[row, _x_rows(d), _x_rows(d), vec],
        out_specs=[_x_rows(d), row, pl.BlockSpec((8, LANES), lambda i: (0, 0)), vec],
        out_shape=[jax.ShapeDtypeStruct(x.shape, F32), jax.ShapeDtypeStruct((lp, d), BF16),
                   jax.ShapeDtypeStruct((8, LANES), F32), jax.ShapeDtypeStruct((1, d), F32)],
        name="post", compiler_params=_cp("arbitrary"))(out, x, target, post_w)


def _prenorm_bwd(dxn, x, meta, w, dy, rider=None):
    seq, d = x.shape
    lp = seq + Q_BLOCK

    def body(start, finish, dx_ref, x_ref, m_ref, w_ref, dy_ref, gx_ref, gm_ref, dw_ref):
        i = pl.program_id(0)
        pl.when(i == 0)(start)
        h = _h_tile(i, x_ref, m_ref)
        r = _rms(h)
        xh = h * r
        dxn_ = dx_ref[...]
        dxw = dxn_ * w_ref[...]
        dh = jnp.where(i > 0, dy_ref[...], 0.0) + r * (dxw - xh * jnp.mean(dxw * xh, axis=-1, keepdims=True))
        gx_ref[...] = dh

        @pl.when(i == 0)
        def _():
            dw_ref[...] = jnp.zeros_like(dw_ref)
            gm_ref[...] = dh[PAD_ROWS:, :]
        dw_ref[...] += jnp.sum(dxn_ * xh, axis=0, keepdims=True)
        pl.when(i == lp // Q_BLOCK - 1)(finish)

    vec = pl.BlockSpec((1, d), lambda i: (0, 0))
    met = pl.BlockSpec((N_META, d), lambda i: (0, 0))
    outs, got = _hosted_call(
        body, [rider], 5, 3, 0, grid=(lp // Q_BLOCK,),
        in_specs=[pl.BlockSpec((Q_BLOCK, d), lambda i: (i, 0)), _x_rows(d), met, vec, _x_rows(d)],
        out_specs=[_x_rows(d), met, vec],
        out_shape=[jax.ShapeDtypeStruct((seq, d), F32), jax.ShapeDtypeStruct((N_META, d), F32),
                   jax.ShapeDtypeStruct((1, d), F32)],
        name="prenorm_bwd", compiler_params=_cp("arbitrary"))(dxn, x, meta, w, dy)
    return outs, (got[0] if got else None)


def _layer_grads(x, target, meta, pre_w, wfull, conv_wt, a_log, dt_bias, gdn_norm_w, fq_w, fk_w, f_bias, w_out, post_w,
                 w_out_grads=None):
    nh = a_log.shape[1]
    zpad = jnp.zeros((1, LANES - 3 * nh), F32)
    bias_row = jnp.concatenate([jnp.zeros((1, nh), F32), dt_bias, f_bias, zpad], axis=1)
    nega_row = jnp.concatenate([jnp.zeros((1, nh), F32), -jnp.exp(a_log), jnp.zeros((1, nh), F32), zpad], axis=1)
    qk_w = jnp.stack([fq_w, fk_w])

    if isinstance(wfull, tuple):
        rider, meta_at, project = wfull
        xn, got = _prenorm_gathering(x, pre_w, rider, meta_at)
        proj, wfull, conv_wt, meta, w_out = project(xn, got)
    else:
        xn = _prenorm(x, meta, pre_w)
        proj = _matmul(xn, wfull, "nn", MM_TILE, F32, "proj")
    qkv = _gdn_prep(proj, conv_wt, nh)
    gates, gt3, gtf = _gates(proj, bias_row, nega_row, nh)
    (o_gdn, s_all, t_all), _ = _gdn_fwd(qkv, gates, gt3, nh, None)
    qkn = _fox_prep(proj, qk_w, nh)
    o_fox, fox_lse = _fox_fwd(qkn, proj, gates, gtf, nh)
    merged = _merge_fox(o_fox, proj, _merge_gdn(o_gdn, proj, gdn_norm_w, nh), nh)
    out = _matmul(merged, w_out, "nn", 4 * LANES, F32, "out_proj")
    dy, dout, loss_blk, dpost_w = _post(out, x, target, post_w)

    dw_out = _matmul(merged, dout, "tn", 4 * LANES, BF16, "dw_out")
    if w_out_grads is None:
        dmerged, gdn_rider = _matmul(dout, w_out, "nt", 4 * LANES, F32, "dmerged"), None
    else:
        dmerged, got = _matmul(dout, w_out, "nt", 4 * LANES, F32, "dmerged", w_out_grads[0](dw_out))
        gdn_rider = w_out_grads[1](dw_out, got)
    do_gdn, dproj, dgdn_norm_w = _merge_gdn_bwd(o_gdn, proj, gdn_norm_w, dmerged, nh)
    do_fox, dproj = _merge_fox_bwd(o_fox, proj, dmerged, dproj, nh)
    dqn, dkn, dc_t, dproj = _fox_bwd(qkn, proj, gates, gtf, fox_lse, do_fox, dproj, nh)
    dproj, dqk_w = _fox_prep_bwd(proj, qk_w, dqn, dkn, dproj, nh)
    (dgq, dgk, dgv, dgate), w_out_parts = _gdn_bwd(qkv, gates, gt3, s_all, t_all, do_gdn, nh, gdn_rider)
    dproj, dconv_wt = _gdn_prep_bwd(proj, conv_wt, dgq, dgk, dgv, dproj, nh)
    dc_rows = jnp.pad(dc_t.reshape(nh, -1), ((2 * nh, LANES - 3 * nh), (0, 0)))
    dproj, gate_sums = _gates_bwd(proj, bias_row, nega_row, gates, dgate, dc_rows, dproj, nh)
    return dict(
        loss=loss_blk[0:1, 0:1], dy=dy, xn=xn, dproj=dproj, post_w=dpost_w,
        conv_wt=dconv_wt, a_log=gate_sums[1:2, nh:2 * nh], dt_bias=gate_sums[0:1, nh:2 * nh],
        gdn_norm_w=dgdn_norm_w, fq_w=dqk_w[0], fk_w=dqk_w[1], f_bias=gate_sums[0:1, 2 * nh:3 * nh], w_out=dw_out,
        w_out_parts=w_out_parts, wfull=wfull, meta=meta)


def _cast_bf16(a, tr, name):
    r, c = a.shape

    def body(a_ref, o_ref):
        o_ref[...] = a_ref[...].astype(BF16)

    return pl.pallas_call(
        body, grid=(r // tr,), in_specs=[pl.BlockSpec((tr, c), lambda i: (i, 0))],
        out_specs=pl.BlockSpec((tr, c), lambda i: (i, 0)), out_shape=jax.ShapeDtypeStruct((r, c), BF16),
        name=name, compiler_params=_cp("parallel"))(a)


def _column_major(a):
    return jnp.transpose(a, (2, 0, 1))


def _cast_bf16_column_major(a3, pieces, name):
    _, r, c = a3.shape
    rows = r // pieces

    def body(a_ref, *o_refs):
        t = a_ref[...].reshape(LANES, r).T.astype(BF16)
        for k, o_ref in enumerate(o_refs):
            o_ref[...] = t[k * rows:(k + 1) * rows]

    return pl.pallas_call(
        body, grid=(pl.cdiv(c, LANES),), in_specs=[pl.BlockSpec((LANES, 1, r), lambda i: (i, 0, 0))],
        out_specs=[pl.BlockSpec((rows, LANES), lambda i: (0, i))] * pieces,
        out_shape=[jax.ShapeDtypeStruct((rows, c), BF16)] * pieces, name=name, compiler_params=_cp("parallel"))(_column_major(a3))


def _adamw_column_major(w3, parts, m3, v3, name):
    _, r, c = w3.shape
    n_parts = parts[0].shape[0]

    def body(w_ref, *refs):
        p_refs, (m_ref, v_ref, g_ref, d_ref, nm_ref, nv_ref) = refs[:len(parts)], refs[len(parts):]
        sums = []
        for p_ref in p_refs:
            g = p_ref[0].astype(F32)
            for s in range(1, n_parts):
                g = g + p_ref[s].astype(F32)
            sums.append(g)
        g = jnp.concatenate(sums, axis=0).T
        flat = lambda ref: ref[...].reshape(LANES, r)
        m_new = ADAM_B1 * flat(m_ref) + (1.0 - ADAM_B1) * g
        v_new = ADAM_B2 * flat(v_ref) + (1.0 - ADAM_B2) * (g * g)
        m_hat = m_new / (1.0 - ADAM_B1 ** ADAM_STEP)
        v_hat = v_new / (1.0 - ADAM_B2 ** ADAM_STEP)
        delta = -ADAM_LR * (m_hat / (jnp.sqrt(v_hat) + ADAM_EPS) + ADAM_WD * flat(w_ref))
        for ref, val in ((g_ref, g), (d_ref, delta), (nm_ref, m_new), (nv_ref, v_new)):
            ref[...] = val.reshape(LANES, 1, r)

    blk = pl.BlockSpec((LANES, 1, r), lambda i: (i, 0, 0))
    outs = pl.pallas_call(
        body, grid=(pl.cdiv(c, LANES),),
        in_specs=[blk] + [pl.BlockSpec((n_parts, p.shape[1], LANES), lambda i: (0, 0, i)) for p in parts] + [blk, blk],
        out_specs=[blk] * 4, out_shape=[jax.ShapeDtypeStruct((c, 1, r), F32)] * 4, name=name,
        compiler_params=_cp("parallel"))(_column_major(w3), *parts, _column_major(m3), _column_major(v3))
    return [jnp.transpose(o, (1, 2, 0)) for o in outs]


def _gather_copies(ins, outs, send_sems, recv_sems, local_sems):
    n = len(ins)
    x, y, c = lax.axis_index("x"), lax.axis_index("y"), lax.axis_index("c")
    me, sibling = (x, y, c), (x, y, 1 - c)
    xn, yn, dg = (1 - x, y), (x, 1 - y), (1 - x, 1 - y)

    def copy(a, k, block, to, src=None):
        px, py, pc = block
        rows = outs[a].at[4 * px + 2 * py + pc]
        return pltpu.make_async_remote_copy(
            src_ref=rows if src is None else src, dst_ref=rows, send_sem=send_sems.at[a, k],
            recv_sem=recv_sems.at[a, k], device_id=to, device_id_type=_MESH)

    local = [pltpu.make_async_copy(ins[a], outs[a].at[4 * x + 2 * y + c], local_sems.at[a]) for a in range(n)]
    own = [cp for a in range(n) for cp in (copy(a, 0, me, sibling, src=ins[a]), copy(a, 1, me, (*xn, c), src=ins[a]),
                                           copy(a, 2, me, (*yn, c), src=ins[a]))]

    def start():
        for cp in local + own:
            cp.start()

    def finish():
        for a in range(n):
            @pl.when(c == 1)
            def _(a=a):
                copy(a, 1, (*xn, c), me).wait_recv()
                copy(a, 3, (*xn, c), (*yn, c)).start()

            @pl.when(c == 0)
            def _(a=a):
                copy(a, 2, (*yn, c), me).wait_recv()
                copy(a, 3, (*yn, c), (*xn, c)).start()
        for a in range(n):
            pl.when(c == 0)(copy(a, 1, (*xn, c), me).wait_recv)
            copy(a, 4, (*xn, c), sibling).start()
            pl.when(c == 1)(copy(a, 2, (*yn, c), me).wait_recv)
            copy(a, 5, (*yn, c), sibling).start()
        for a in range(n):
            copy(a, 3, (*dg, c), me).wait_recv()
            copy(a, 6, (*dg, c), sibling).start()
        for a in range(n):
            copy(a, 0, sibling, me).wait_recv()
            for k, chip in ((4, xn), (5, yn), (6, dg)):
                copy(a, k, (*chip, 1 - c), me).wait_recv()
                copy(a, k, (*chip, c), sibling).wait_send()
            copy(a, 3, (*xn, c), (*yn, c)).wait_send()
        for cp in own:
            cp.wait_send()
        for cp in local:
            cp.wait()

    return start, finish


def _gather_scratch(n):
    return [pltpu.SemaphoreType.DMA((n, N_DEV - 1)), pltpu.SemaphoreType.DMA((n, N_DEV - 1)), pltpu.SemaphoreType.DMA((n,))]


def _gather_rider(arrays):
    return _Rider(list(arrays), [jax.ShapeDtypeStruct((N_DEV,) + a.shape, a.dtype) for a in arrays],
                  _gather_scratch(len(arrays)), {}, lambda ins, outs, scratch: _gather_copies(ins, outs, *scratch))


def _all_gather(arrays, name):
    n = len(arrays)

    def body(*refs):
        start, finish = _gather_copies(refs[:n], refs[n:2 * n], *refs[2 * n:])
        start()
        finish()

    return pl.pallas_call(
        body, in_specs=[_ANY] * n, out_specs=[_ANY] * n,
        out_shape=[jax.ShapeDtypeStruct((N_DEV,) + a.shape, a.dtype) for a in arrays],
        scratch_shapes=_gather_scratch(n), name=name)(*arrays)


SLAB = 10 * LANES


def _slab_start(blk, nh, cols):
    in_second_half = blk >= N_DEV // 2
    shift = (2 * nh if in_second_half else 0) if isinstance(blk, int) else jnp.where(in_second_half, 2 * nh, 0)
    return (blk * cols - shift) // LANES * LANES


def _pair_rider(dw_rows=None, parts=None, nh=None, after=None):
    if dw_rows is not None:
        r, full = dw_rows.shape
        cols = (full - NARROW + 3 * nh) // N_DEV
        out_shapes = [jax.ShapeDtypeStruct((N_CHIP, r, SLAB), dw_rows.dtype), jax.ShapeDtypeStruct((r, NARROW), dw_rows.dtype)]
    else:
        out_shapes = [jax.ShapeDtypeStruct((N_CHIP,) + parts.shape[1:], parts.dtype)]

    def make(ins, outs, scratch):
        send_sems, recv_sems = scratch
        x, y, c = lax.axis_index("x"), lax.axis_index("y"), lax.axis_index("c")
        kw = lambda k: dict(send_sem=send_sems.at[k], recv_sem=recv_sems.at[k], device_id=(x, y, 1 - c), device_id_type=_MESH)
        copies = []
        for q in range(N_CHIP):
            if dw_rows is not None:
                first = pl.multiple_of(_slab_start(2 * q + 1 - c, nh, cols), LANES)
                copies.append(pltpu.make_async_remote_copy(src_ref=ins[0].at[:, pl.ds(first, SLAB)], dst_ref=outs[0].at[q], **kw(q)))
            else:
                copies.append(pltpu.make_async_remote_copy(src_ref=ins[0].at[2 * q + 1 - c], dst_ref=outs[0].at[q], **kw(q)))
        if dw_rows is not None:
            copies.append(pltpu.make_async_remote_copy(src_ref=ins[0].at[:, pl.ds(full - NARROW, NARROW)], dst_ref=outs[1],
                                                       **kw(N_CHIP)))

        def start():
            for cp in copies:
                cp.start()

        def finish():
            for cp in copies:
                cp.wait()

        return start, finish

    return _Rider([dw_rows if dw_rows is not None else parts] + ([] if after is None else [after]), out_shapes,
                  [pltpu.SemaphoreType.DMA((N_CHIP + 1,)), pltpu.SemaphoreType.DMA((N_CHIP + 1,))], {}, make)


def _relayout_pair_sum(dwfull, got_slabs, got_tail, core, nh, tr, name):
    d, full = dwfull.shape
    w = nh * HEAD_DIM
    cols = (8 * w + 3 * nh) // N_DEV
    segs = _native_segments(nh)

    def block(f_ref, s_ref, t_ref, q, blk):
        st = _slab_start(blk, nh, cols)
        wide = f_ref[:, st:st + SLAB].astype(F32) + s_ref[q].astype(F32)
        tail = f_ref[:, 8 * w:].astype(F32) + t_ref[...].astype(F32)
        pieces = []
        for s0, s1, t0 in segs:
            lo, hi = max(s0, blk * cols), min(s1, (blk + 1) * cols)
            if lo < hi:
                at = t0 + lo - s0
                pieces.append(tail[:, at - 8 * w:at - 8 * w + hi - lo] if at >= 8 * w else wide[:, at - st:at - st + hi - lo])
        return (pieces[0] if len(pieces) == 1 else jnp.concatenate(pieces, axis=1)).astype(dwfull.dtype)

    def body(core_ref, f_ref, s_ref, t_ref, o_ref):
        for parity in range(2):
            @pl.when(core_ref[0] == parity)
            def _(parity=parity):
                for q in range(N_CHIP):
                    o_ref[q] = block(f_ref, s_ref, t_ref, q, 2 * q + parity)

    return pl.pallas_call(
        body,
        grid_spec=pltpu.PrefetchScalarGridSpec(
            num_scalar_prefetch=1, grid=(d // tr,),
            in_specs=[pl.BlockSpec((tr, full), lambda i, c_ref: (i, 0)), pl.BlockSpec((N_CHIP, tr, SLAB), lambda i, c_ref: (0, i, 0)),
                      pl.BlockSpec((tr, NARROW), lambda i, c_ref: (i, 0))],
            out_specs=pl.BlockSpec((N_CHIP, tr, cols), lambda i, c_ref: (0, i, 0))),
        out_shape=jax.ShapeDtypeStruct((N_CHIP, d, cols), dwfull.dtype), name=name,
        compiler_params=_cp("parallel"))(core, dwfull, got_slabs, got_tail)


def _pair_sum(parts, got, core, tr, name):
    _, r, c = parts.shape

    def body(core_ref, p_ref, g_ref, o_ref):
        o_ref[...] = (p_ref[...].astype(F32) + g_ref[...].astype(F32)).astype(o_ref.dtype)

    return pl.pallas_call(
        body,
        grid_spec=pltpu.PrefetchScalarGridSpec(
            num_scalar_prefetch=1, grid=(N_CHIP, r // tr),
            in_specs=[pl.BlockSpec((1, tr, c), lambda q, i, core_ref: (2 * q + core_ref[0], i, 0)),
                      pl.BlockSpec((1, tr, c), lambda q, i, core_ref: (q, i, 0))],
            out_specs=pl.BlockSpec((1, tr, c), lambda q, i, core_ref: (q, i, 0))),
        out_shape=jax.ShapeDtypeStruct((N_CHIP, r, c), parts.dtype), name=name,
        compiler_params=_cp("parallel", "parallel"))(core, parts, got)


def _native_segments(nh):
    w = nh * HEAD_DIM
    return [(0, 4 * w, 0), (4 * w, 4 * w + 2 * nh, 8 * w), (4 * w + 2 * nh, 8 * w + 2 * nh, 4 * w),
            (8 * w + 2 * nh, 8 * w + 3 * nh, 8 * w + 2 * nh)]


def _relayout_w_in(wg, nh, tr, name, rows_total=None, into=None):
    _, d, cols = wg.shape
    w = nh * HEAD_DIM
    rows_total = into.shape[0] if into is not None else rows_total or d
    first = (rows_total - d) // tr if into is not None else 0

    def native(ref, j0, j1):
        out = []
        while j0 < j1:
            blk = j0 // cols
            end = min(j1, (blk + 1) * cols)
            out.append(ref[blk, :, pl.ds(j0 - blk * cols, end - j0)])
            j0 = end
        return out

    def body(g_ref, *refs):
        o_ref = refs[-1]
        for cidx in range(8 * w // LANES):
            j0 = cidx * LANES + (0 if cidx * LANES < 4 * w else 2 * nh)
            pieces = native(g_ref, j0, j0 + LANES)
            o_ref[:, cidx * LANES:(cidx + 1) * LANES] = pieces[0] if len(pieces) == 1 else jnp.concatenate(pieces, axis=1)
        pieces = (native(g_ref, 4 * w, 4 * w + 2 * nh) + native(g_ref, 8 * w + 2 * nh, 8 * w + 3 * nh)
                  + [jnp.zeros((tr, NARROW - 3 * nh), wg.dtype)])
        o_ref[:, 8 * w:] = jnp.concatenate(pieces, axis=1)

    return pl.pallas_call(
        body, grid=(d // tr,), in_specs=[pl.BlockSpec((N_DEV, tr, cols), lambda i: (0, i, 0))] + [_ANY] * (into is not None),
        out_specs=pl.BlockSpec((tr, 8 * w + NARROW), lambda i: (first + i, 0)),
        out_shape=jax.ShapeDtypeStruct((rows_total, 8 * w + NARROW), wg.dtype),
        input_output_aliases={1: 0} if into is not None else {},
        name=name, compiler_params=_cp("parallel"))(wg, *([into] if into is not None else []))


def _adamw(w, parts, m, v, tr, name, after=None):
    r, c = w.shape
    n_parts = parts.shape[0]

    def body(w_ref, p_ref, m_ref, v_ref, *refs):
        g_ref, d_ref, nm_ref, nv_ref = refs[-5:-1] if after is not None else refs
        if after is not None:
            refs[-1][...] = jnp.zeros_like(refs[-1])
        g = p_ref[0].astype(F32)
        for s in range(1, n_parts):
            g = g + p_ref[s].astype(F32)
        m_new = ADAM_B1 * m_ref[...] + (1.0 - ADAM_B1) * g
        v_new = ADAM_B2 * v_ref[...] + (1.0 - ADAM_B2) * (g * g)
        m_hat = m_new / (1.0 - ADAM_B1 ** ADAM_STEP)
        v_hat = v_new / (1.0 - ADAM_B2 ** ADAM_STEP)
        g_ref[...] = g
        d_ref[...] = -ADAM_LR * (m_hat / (jnp.sqrt(v_hat) + ADAM_EPS) + ADAM_WD * w_ref[...])
        nm_ref[...] = m_new
        nv_ref[...] = v_new

    blk = pl.BlockSpec((tr, c), lambda i: (i, 0))
    extra = after is not None
    return pl.pallas_call(
        body, grid=(r // tr,),
        in_specs=[blk, pl.BlockSpec((n_parts, tr, c), lambda i: (0, i, 0)), blk, blk] + [_ANY] * extra,
        out_specs=[blk] * 4 + [pl.BlockSpec((8, LANES), lambda i: (0, 0))] * extra,
        out_shape=[jax.ShapeDtypeStruct((r, c), F32)] * 4 + [jax.ShapeDtypeStruct((8, LANES), F32)] * extra, name=name,
        compiler_params=_cp("arbitrary" if extra else "parallel"))(w, parts, m, v, *([after] if extra else []))


def _adamw_conv(w, gathered, dev, m, v, name):
    c, r = w.shape
    n_parts = gathered.shape[0]

    def body(dev_ref, w_ref, p_ref, m_ref, v_ref, g_ref, d_ref, nm_ref, nv_ref):
        g = p_ref[0].astype(F32)
        for s in range(1, n_parts):
            g = g + p_ref[s].astype(F32)
        m_new = ADAM_B1 * m_ref[...] + (1.0 - ADAM_B1) * g
        v_new = ADAM_B2 * v_ref[...] + (1.0 - ADAM_B2) * (g * g)
        m_hat = m_new / (1.0 - ADAM_B1 ** ADAM_STEP)
        v_hat = v_new / (1.0 - ADAM_B2 ** ADAM_STEP)
        g_ref[...] = g
        d_ref[...] = -ADAM_LR * (m_hat / (jnp.sqrt(v_hat) + ADAM_EPS) + ADAM_WD * w_ref[...])
        nm_ref[...] = m_new
        nv_ref[...] = v_new

    blk = pl.BlockSpec((c, r), lambda i, dev_ref: (0, 0))
    return pl.pallas_call(
        body,
        grid_spec=pltpu.PrefetchScalarGridSpec(
            num_scalar_prefetch=1, grid=(1,),
            in_specs=[blk, pl.BlockSpec((n_parts, c, r), lambda i, dev_ref: (0, 0, dev_ref[0])), blk, blk], out_specs=[blk] * 4),
        out_shape=[jax.ShapeDtypeStruct((c, r), F32)] * 4, name=name, compiler_params=_cp("arbitrary"))(dev, w, gathered, m, v)


def _pack_small(d, pre, post, a_log, dt_bias, f_bias, gdn_w, fq_w, fk_w, extra):
    row2 = jnp.concatenate([a_log, dt_bias, f_bias, gdn_w, fq_w, fk_w, extra], axis=1)
    row2 = jnp.pad(row2, ((0, 0), (0, d - row2.shape[1])))
    return jnp.concatenate([pre, post, row2, jnp.zeros((5, d), F32)], axis=0)


def _adamw_small(w, parts, m, v, nh, name):
    d = w.shape[1]
    n_parts = parts.shape[0]
    shapes = dict(pre=d, post=d, a_log=nh, dt_bias=nh, f_bias=nh, gdn_w=HEAD_DIM, fq_w=HEAD_DIM, fk_w=HEAD_DIM, extra=1)

    def body(w_ref, p_ref, m_ref, v_ref, *o_refs):
        g = p_ref[0]
        for s in range(1, n_parts):
            g = g + p_ref[s]
        m_new = ADAM_B1 * m_ref[...] + (1.0 - ADAM_B1) * g
        v_new = ADAM_B2 * v_ref[...] + (1.0 - ADAM_B2) * (g * g)
        m_hat = m_new / (1.0 - ADAM_B1 ** ADAM_STEP)
        v_hat = v_new / (1.0 - ADAM_B2 ** ADAM_STEP)
        delta = -ADAM_LR * (m_hat / (jnp.sqrt(v_hat) + ADAM_EPS) + ADAM_WD * w_ref[...])
        for k, val in enumerate((g, delta, m_new, v_new)):
            vectors = _unpack_small(val, nh)
            for j, key in enumerate(shapes):
                o_refs[k * len(shapes) + j][...] = vectors[key]

    full = lambda shape: pl.BlockSpec(shape, lambda i: (0,) * len(shape))
    outs = pl.pallas_call(
        body, grid=(1,), in_specs=[full(w.shape), full(parts.shape), full(w.shape), full(w.shape)],
        out_specs=[full((1, n)) for n in shapes.values()] * 4,
        out_shape=[jax.ShapeDtypeStruct((1, n), F32) for n in shapes.values()] * 4, name=name,
        compiler_params=_cp("arbitrary"))(w, parts, m, v)
    return [dict(zip(shapes, outs[k * len(shapes):(k + 1) * len(shapes)])) for k in range(4)]


def _unpack_small(p, nh):
    o = 3 * nh
    return dict(pre=p[0:1], post=p[1:2], a_log=p[2:3, 0:nh], dt_bias=p[2:3, nh:2 * nh], f_bias=p[2:3, 2 * nh:o],
                gdn_w=p[2:3, o:o + HEAD_DIM], fq_w=p[2:3, o + HEAD_DIM:o + 2 * HEAD_DIM],
                fk_w=p[2:3, o + 2 * HEAD_DIM:o + 3 * HEAD_DIM], extra=p[2:3, o + 3 * HEAD_DIM:o + 3 * HEAD_DIM + 1])


def kernel(x, meta_tokens, pre_norm_w, w_in, conv_w, a_log, dt_bias, gdn_norm_w, fox_q_norm_w, fox_k_norm_w, fox_f_bias, w_out, post_norm_w, loss_target, m_meta_tokens, m_pre_norm_w, m_w_in, m_conv_w, m_a_log, m_dt_bias, m_gdn_norm_w, m_fox_q_norm_w, m_fox_k_norm_w, m_fox_f_bias, m_w_out, m_post_norm_w, v_meta_tokens, v_pre_norm_w, v_w_in, v_conv_w, v_a_log, v_dt_bias, v_gdn_norm_w, v_fox_q_norm_w, v_fox_k_norm_w, v_fox_f_bias, v_w_out, v_post_norm_w):
    nh = a_log.shape[1]
    d = x.shape[-1]
    w = nh * HEAD_DIM
    zero = jnp.zeros((1, 1), F32)

    w_in_a, w_in_b = _cast_bf16_column_major(w_in, 2, "cast_w_in")

    def project(xn, got):
        wg, cg, mg = got
        half = (d // 2, 0), (d // 2, 1)
        wfull = _relayout_w_in(wg, nh, 256, "relayout_w_in_a", rows_total=d)
        proj, got = _matmul(xn, wfull, "nn", MM_TILE, F32, "proj_a", _gather_rider([w_in_b]), a_cols=half[0], b_rows=half[0])
        wfull = _relayout_w_in(got[0], nh, 256, "relayout_w_in_b", into=wfull)
        proj, got = _matmul(xn, wfull, "nn", MM_TILE, F32, "proj_b", _gather_rider([_cast_bf16(w_out[0], 256, "cast_w_out")]),
                            a_cols=half[1], b_rows=half[1], acc=proj)
        return (proj, wfull, cg.transpose(1, 0, 2).reshape(CONV_WIDTH, 3 * w), mg.transpose(1, 0, 2).reshape(N_META, d),
                got[0].reshape(2 * w, d))

    core = lax.axis_index("c")
    dev = 4 * lax.axis_index("x") + 2 * lax.axis_index("y") + core
    core_arr = jnp.reshape(core, (1,)).astype(jnp.int32)

    out_parts = lambda dw_out: dw_out.reshape(N_DEV, 2 * w // N_DEV, d)
    g = _layer_grads(
        x[0], loss_target[0], None, pre_norm_w, (_gather_rider([w_in_a, conv_w[0].T, meta_tokens]), 2, project), None,
        a_log, dt_bias, gdn_norm_w,
        fox_q_norm_w, fox_k_norm_w, fox_f_bias, None, post_norm_w,
        w_out_grads=(lambda dw_out: _pair_rider(parts=out_parts(dw_out)),
                     lambda dw_out, got: _chip_rider(_pair_sum(out_parts(dw_out), got[0], core_arr, 256, "pair_sum_w_out"))))
    p_out = g["w_out_parts"][0]
    xn, dproj, wfull, meta_full = g["xn"], g["dproj"], g["wfull"], g["meta"]
    flights, token, dw, rider = [], None, None, None

    def exchange(dw, got, i):
        sums = _relayout_pair_sum(dw, got[0], got[1], core_arr, nh, 128, f"relayout_pair_sum_{i}")
        flight, token = _chip_exchange_start(sums, f"chip_exchange_start_{i}")
        flights.append(flight)
        return token

    for i, (index, parts) in enumerate(DW_IN_PIECES):
        res = _matmul(xn, dproj, "tn", MM_TILE, BF16, f"dw_in_{i}", rider, a_cols=(d // parts, index))
        if i:
            token = exchange(dw, res[1], i - 1)
        dw = res[0] if i else res
        rider = _pair_rider(dw_rows=dw, nh=nh, after=token)
    dxn, (got,) = _dxn(dproj, wfull, [rider], MM_TILE, "dxn")
    token = exchange(dw, got, len(DW_IN_PIECES) - 1)
    *r_out, token = _adamw(w_out[0], p_out, m_w_out[0], v_w_out[0], 64, "adamw_w_out", after=token)
    (grad_x, dmeta, dpre_w), _ = _prenorm_bwd(dxn, x[0], meta_full, pre_norm_w + token[0:1, 0:1], g["dy"])
    small = _pack_small(d, dpre_w, g["post_w"], g["a_log"], g["dt_bias"], g["f_bias"], g["gdn_norm_w"], g["fq_w"],
                        g["fk_w"], g["loss"])
    a_conv, a_meta, p_small = _all_gather([g["conv_wt"], dmeta, small], "gather_small_grads")
    p_meta = lax.dynamic_slice_in_dim(a_meta, dev * meta_tokens.shape[1], meta_tokens.shape[1], axis=2)

    r_conv = _adamw_conv(conv_w[0].T, a_conv, jnp.reshape(dev, (1,)).astype(jnp.int32), m_conv_w[0].T, v_conv_w[0].T,
                         "adamw_conv_w")
    r_conv = [o.T for o in r_conv]
    r_meta = _adamw(meta_tokens, p_meta, m_meta_tokens, v_meta_tokens, N_META, "adamw_meta")
    pk = lambda pre, post, a, dt, gw, fq, fk, fb: _pack_small(d, pre, post, a, dt, fb, gw, fq, fk, zero)
    sm = _adamw_small(
        pk(pre_norm_w, post_norm_w, a_log, dt_bias, gdn_norm_w, fox_q_norm_w, fox_k_norm_w, fox_f_bias), p_small,
        pk(m_pre_norm_w, m_post_norm_w, m_a_log, m_dt_bias, m_gdn_norm_w, m_fox_q_norm_w, m_fox_k_norm_w, m_fox_f_bias),
        pk(v_pre_norm_w, v_post_norm_w, v_a_log, v_dt_bias, v_gdn_norm_w, v_fox_q_norm_w, v_fox_k_norm_w, v_fox_f_bias),
        nh, "adamw_small")
    p_in = _chip_exchange_wait(flights, sm[0]["pre"], "chip_exchange_wait")
    r_in = _adamw_column_major(w_in, p_in, m_w_in, v_w_in, "adamw_w_in")

    outs = []
    for i in range(4):
        s = sm[i]
        outs += [r_meta[i], s["pre"], r_in[i], r_conv[i][None], s["a_log"], s["dt_bias"], s["gdn_w"], s["fq_w"],
                 s["fk_w"], s["f_bias"], r_out[i][None], s["post"]]
    return (sm[0]["extra"].reshape(()), grad_x[None], *outs)
```

```python
import jax
import jax.numpy as jnp
from jax import lax
from jax.experimental import pallas as pl
from jax.experimental.pallas import tpu as pltpu

F32, BF16 = jnp.float32, jnp.bfloat16
HEAD_DIM = 128
N_META = 16
CONV_WIDTH = 4
CHUNK = 128
Q_BLOCK = 128
LANES = 128
EPS = 1e-6
PAD_ROWS = Q_BLOCK - N_META
N_DEV = 8
N_CHIP = 4
VMEM_LIMIT = 56 * 1024 * 1024
NEG = -1e30
NARROW = 2 * LANES
MM_TILE = 6 * LANES
DW_IN_PIECES = ((0, 2), (2, 4), (3, 4))

ADAM_LR, ADAM_B1, ADAM_B2, ADAM_EPS, ADAM_WD, ADAM_STEP = 0.001, 0.9, 0.999, 1e-08, 0.01, 10

_DN = {"nn": (((1,), (0,)), ((), ())), "nt": (((1,), (1,)), ((), ())), "tn": (((0,), (0,)), ((), ()))}
_DN3 = {"nn": (((2,), (1,)), ((0,), (0,))), "nt": (((2,), (2,)), ((0,), (0,))), "tn": (((1,), (1,)), ((0,), (0,)))}
_ANY = pl.BlockSpec(memory_space=pl.ANY)
_MESH = pl.DeviceIdType.MESH


def _cp(*sem):
    return pltpu.CompilerParams(dimension_semantics=sem, vmem_limit_bytes=VMEM_LIMIT)


def _dot(a, b, dims="nn", prec=None):
    return lax.dot_general(a, b, _DN[dims], precision=prec, preferred_element_type=F32)


def _bdot(a, b, dims="nn"):
    return _dot(a.astype(BF16), b.astype(BF16), dims)


def _hdot(a, b, dims="nn"):
    return _dot(a, b, dims, prec=lax.Precision.HIGHEST)


def _dot3(a, b, dims="nn"):
    return lax.dot_general(a, b, _DN3[dims], preferred_element_type=F32)


def _bdot3(a, b, dims="nn"):
    return _dot3(a.astype(BF16), b.astype(BF16), dims)


def _split(a):
    hi = a.astype(BF16)
    return hi, (a - hi.astype(F32)).astype(BF16)


def _iota(shape, dim):
    return lax.broadcasted_iota(jnp.int32, shape, dim)


def _sigmoid(z):
    return 1.0 / (1.0 + jnp.exp(-z))


def _softplus(z):
    e = jnp.exp(-jnp.abs(z))
    u = 1.0 + e
    l1p = jnp.where(u == 1.0, e, jnp.log(u) * (e / jnp.where(u == 1.0, 1.0, u - 1.0)))
    return jnp.maximum(z, 0.0) + l1p


def _silu_and_grad(z):
    s = _sigmoid(z)
    return z * s, s * (1.0 + z * (1.0 - s))


def _rms(x):
    return lax.rsqrt(jnp.mean(x * x, axis=-1, keepdims=True) + EPS)


def _h_tile(i, x_ref, meta_ref):
    first = jnp.concatenate([jnp.zeros((PAD_ROWS, x_ref.shape[1]), F32), meta_ref[...]], axis=0)
    return jnp.where(i == 0, first, x_ref[...])


def _x_rows(d):
    return pl.BlockSpec((Q_BLOCK, d), lambda i: (jnp.maximum(i - 1, 0), 0))


def _prenorm(x, meta, w):
    seq, d = x.shape
    lp = seq + Q_BLOCK

    def body(x_ref, m_ref, w_ref, o_ref):
        h = _h_tile(pl.program_id(0), x_ref, m_ref)
        o_ref[...] = (h * _rms(h) * w_ref[...]).astype(BF16)

    return pl.pallas_call(
        body, grid=(lp // Q_BLOCK,),
        in_specs=[_x_rows(d), pl.BlockSpec((N_META, d), lambda i: (0, 0)), pl.BlockSpec((1, d), lambda i: (0, 0))],
        out_specs=pl.BlockSpec((Q_BLOCK, d), lambda i: (i, 0)),
        out_shape=jax.ShapeDtypeStruct((lp, d), BF16), name="prenorm", compiler_params=_cp("parallel"))(x, meta, w)


def _prenorm_gathering(x, w, rider, meta_at):
    seq, d = x.shape
    steps = seq // Q_BLOCK + 1

    def body(start, finish, x_ref, w_ref, o_ref, meta_buf, meta_sem):
        i = pl.program_id(0)
        pl.when(i == 0)(start)

        @pl.when(i < steps - 1)
        def _():
            h = x_ref[...]
            o_ref[...] = (h * _rms(h) * w_ref[...]).astype(BF16)

        @pl.when(i == steps - 1)
        def _():
            finish()
            cp = pltpu.make_async_copy(finish.results[0][meta_at], meta_buf, meta_sem)
            cp.start()
            cp.wait()
            h = jnp.concatenate([jnp.zeros((PAD_ROWS, d), F32), jnp.concatenate([meta_buf[s] for s in range(N_DEV)], axis=1)], axis=0)
            o_ref[...] = (h * _rms(h) * w_ref[...]).astype(BF16)

    (xn,), got = _hosted_call(
        body, [rider], 2, 1, 2, grid=(steps,),
        in_specs=[pl.BlockSpec((Q_BLOCK, d), lambda i: (jnp.minimum(i, steps - 2), 0)), pl.BlockSpec((1, d), lambda i: (0, 0))],
        out_specs=[pl.BlockSpec((Q_BLOCK, d), lambda i: ((i + 1) % steps, 0))],
        out_shape=[jax.ShapeDtypeStruct((seq + Q_BLOCK, d), BF16)],
        scratch_shapes=[pltpu.VMEM((N_DEV, N_META, d // N_DEV), F32), pltpu.SemaphoreType.DMA(())],
        name="prenorm", compiler_params=_cp("arbitrary"))(x, w)
    return xn, got[0]


def _tile(n, want):
    return max(t for t in range(LANES, want + 1, LANES) if n % t == 0)


class _Rider:
    def __init__(self, inputs, out_shapes, scratch, aliases, make):
        self.inputs, self.out_shapes, self.scratch, self.aliases, self.make = inputs, out_shapes, scratch, aliases, make


def _hosted_call(body, riders, n_in, n_out, n_scratch, *, in_specs, out_specs, out_shape, scratch_shapes=(), aliases=None,
                 **kw):
    riders = [r for r in riders if r is not None]
    r_in = [len(r.inputs) for r in riders]
    r_out = [len(r.out_shapes) for r in riders]
    r_scr = [len(r.scratch) for r in riders]
    al = dict(aliases or {})
    for k, r in enumerate(riders):
        al.update({n_in + sum(r_in[:k]) + i: n_out + sum(r_out[:k]) + o for i, o in r.aliases.items()})

    def full_body(*refs):
        ins, rest = refs[:n_in + sum(r_in)], refs[n_in + sum(r_in):]
        outs, scr = rest[:n_out + sum(r_out)], rest[n_out + sum(r_out):]
        hooks = [r.make(ins[n_in + sum(r_in[:k]):n_in + sum(r_in[:k + 1])], outs[n_out + sum(r_out[:k]):n_out + sum(r_out[:k + 1])],
                        scr[n_scratch + sum(r_scr[:k]):n_scratch + sum(r_scr[:k + 1])]) for k, r in enumerate(riders)]

        def start():
            for h in hooks:
                h[0]()

        def finish():
            for h in hooks:
                h[1]()

        finish.results = [outs[n_out + sum(r_out[:k]):n_out + sum(r_out[:k + 1])] for k in range(len(riders))]
        body(start, finish, *ins[:n_in], *outs[:n_out], *scr[:n_scratch])

    call = pl.pallas_call(
        full_body, in_specs=list(in_specs) + [_ANY] * sum(r_in), out_specs=list(out_specs) + [_ANY] * sum(r_out),
        out_shape=list(out_shape) + [s for r in riders for s in r.out_shapes],
        scratch_shapes=list(scratch_shapes) + [s for r in riders for s in r.scratch], input_output_aliases=al, **kw)

    def run(*args):
        res = call(*args, *[t for r in riders for t in r.inputs])
        return res[:n_out], [res[n_out + sum(r_out[:k]):n_out + sum(r_out[:k + 1])] for k in range(len(riders))]

    return run


def _matmul(a, b, dims, tn, out_dtype, name, rider=None, a_cols=None, b_rows=None, acc=None):
    a_shape = a.shape if a_cols is None else (a.shape[0], a_cols[0])
    a_index = 0 if a_cols is None else a_cols[1]
    m = a_shape[1] if dims == "tn" else a_shape[0]
    n = b.shape[0] if dims == "nt" else b.shape[1]
    kdim = b.shape[1] if dims == "nt" else b.shape[0] if b_rows is None else b_rows[0]
    b_index = 0 if b_rows is None else b_rows[1]
    tn = _tile(n, tn)
    steps = n // tn
    b_spec = pl.BlockSpec((tn, kdim), lambda j: (j, 0)) if dims == "nt" else pl.BlockSpec((kdim, tn), lambda j: (b_index, j))
    o_spec = pl.BlockSpec((m, tn), lambda j: (0, j))

    def body(start, finish, a_ref, b_ref, *refs):
        pl.when(pl.program_id(0) == 0)(start)
        prod = _dot(a_ref[...], b_ref[...], dims)
        refs[-1][...] = (prod if acc is None else prod + refs[0][...]).astype(out_dtype)
        pl.when(pl.program_id(0) == steps - 1)(finish)

    (out,), got = _hosted_call(
        body, [rider], 2 + (acc is not None), 1, 0, grid=(steps,),
        in_specs=[pl.BlockSpec(a_shape, lambda j: (0, a_index)), b_spec] + [o_spec] * (acc is not None),
        out_specs=[o_spec], out_shape=[jax.ShapeDtypeStruct((m, n), out_dtype)], aliases={2: 0} if acc is not None else None,
        name=name, compiler_params=_cp("parallel" if rider is None else "arbitrary"))(a, b, *([acc] if acc is not None else []))
    return out if rider is None else (out, got[0])


def _chip_rider(sums):
    def make(ins, outs, scratch):
        local, remote = _chip_exchange_copies(ins[0], outs[0], *scratch)

        def start():
            for cp in [local] + remote:
                cp.start()

        def finish():
            local.wait()
            for cp in remote:
                cp.wait_send()
                cp.wait_recv()

        return start, finish

    return _Rider([sums], [jax.ShapeDtypeStruct(sums.shape, sums.dtype)],
                  [pltpu.SemaphoreType.DMA((N_CHIP - 1,)), pltpu.SemaphoreType.DMA((N_CHIP - 1,)), pltpu.SemaphoreType.DMA((1,))],
                  {}, make)


_HBM = pl.BlockSpec(memory_space=pltpu.HBM)
_SEM = pl.BlockSpec(memory_space=pltpu.SEMAPHORE)


def _chip_exchange_copies(sums_ref, land_ref, send_sems, recv_sems, local_sem):
    x, y, core = lax.axis_index("x"), lax.axis_index("y"), lax.axis_index("c")
    mine = 2 * x + y
    local = pltpu.make_async_copy(sums_ref.at[mine], land_ref.at[mine], local_sem.at[0])
    remote = []
    for k in range(1, N_CHIP):
        px = 1 - x if k & 2 else x
        py = 1 - y if k & 1 else y
        remote.append(pltpu.make_async_remote_copy(
            src_ref=sums_ref.at[2 * px + py], dst_ref=land_ref.at[mine], send_sem=send_sems.at[k - 1],
            recv_sem=recv_sems.at[k - 1], device_id=(px, py, core), device_id_type=_MESH))
    return local, remote


def _chip_exchange_start(sums, name):
    def body(s_ref, send_sems, recv_sems, local_sem, s_thru, land_ref, token):
        local, remote = _chip_exchange_copies(s_ref, land_ref, send_sems, recv_sems, local_sem)
        for cp in [local] + remote:
            cp.start()
        token[...] = jnp.zeros_like(token)

    *flight, token = pl.pallas_call(
        body, name=name,
        out_shape=(pltpu.SemaphoreType.DMA((N_CHIP - 1,)), pltpu.SemaphoreType.DMA((N_CHIP - 1,)), pltpu.SemaphoreType.DMA((1,)),
                   pltpu.HBM(sums.shape, sums.dtype), pltpu.HBM(sums.shape, sums.dtype), jax.ShapeDtypeStruct((8, LANES), F32)),
        in_specs=(_HBM,), out_specs=(_SEM, _SEM, _SEM, _HBM, _HBM, pl.BlockSpec(memory_space=pltpu.VMEM)),
        input_output_aliases={0: 3},
        compiler_params=pltpu.CompilerParams(has_side_effects=pltpu.SideEffectType.DATAFLOW_SIDE_EFFECTING))(
            pltpu.with_memory_space_constraint(sums, pltpu.HBM))
    return flight, token


def _chip_exchange_wait(flights, after, name):
    n = len(flights)

    def body(*refs):
        for i in range(n):
            s_ref, land_ref = refs[2 * i:2 * i + 2]
            local, remote = _chip_exchange_copies(s_ref, land_ref, *refs[2 * n + 3 * i:2 * n + 3 * i + 3])
            local.wait()
            for cp in remote:
                cp.wait_send()
                cp.wait_recv()

    buffers = [b for f in flights for b in f[3:]]
    res = pl.pallas_call(
        body, name=name, out_shape=tuple(pltpu.HBM(b.shape, b.dtype) for b in buffers),
        in_specs=(_HBM,) * (2 * n) + (_SEM,) * (3 * n) + (_ANY,), out_specs=(_HBM,) * (2 * n),
        input_output_aliases={i: i for i in range(2 * n)},
        compiler_params=pltpu.CompilerParams(has_side_effects=pltpu.SideEffectType.DATAFLOW_SIDE_EFFECTING))(
            *buffers, *[s for f in flights for s in f[:3]], after)
    return res[1::2]


def _dxn(dproj, wfull, riders, tk, name):
    m, k = dproj.shape
    n = wfull.shape[0]
    tk = _tile(k, tk)
    steps = k // tk

    def body(start, finish, a_ref, b_ref, o_ref):
        j = pl.program_id(0)

        @pl.when(j == 0)
        def _():
            start()
            o_ref[...] = jnp.zeros_like(o_ref)
        o_ref[...] += _dot(a_ref[...], b_ref[...], "nt")
        pl.when(j == steps - 1)(finish)

    (dxn,), got = _hosted_call(
        body, riders, 2, 1, 0, grid=(steps,),
        in_specs=[pl.BlockSpec((m, tk), lambda j: (0, j)), pl.BlockSpec((n, tk), lambda j: (0, j))],
        out_specs=[pl.BlockSpec((m, n), lambda j: (0, 0))], out_shape=[jax.ShapeDtypeStruct((m, n), F32)],
        name=name, compiler_params=_cp("arbitrary"))(dproj, wfull)
    return dxn, got


def _conv_taps(x, w):
    c = x * w[CONV_WIDTH - 1:CONV_WIDTH, :]
    for j in range(CONV_WIDTH - 1):
        c = c + pltpu.roll(x, CONV_WIDTH - 1 - j, 0) * w[j:j + 1, :]
    return c


def _gdn_prep(proj, conv_wt, nh):
    lp = proj.shape[0]
    scale = HEAD_DIM ** -0.5

    def body(x_ref, w_ref, o_ref):
        which = pl.program_id(0) // nh
        c = _conv_taps(x_ref[...], w_ref[...])
        s = c * _sigmoid(c)
        r = lax.rsqrt(jnp.sum(s * s, axis=-1, keepdims=True) + EPS)
        f = jnp.where(which == 0, r * scale, jnp.where(which == 1, r, 1.0))
        o_ref[...] = jnp.where(_iota(s.shape, 0) >= PAD_ROWS, s * f, 0.0)

    return pl.pallas_call(
        body, grid=(3 * nh,),
        in_specs=[pl.BlockSpec((lp, LANES), lambda s: (0, s)), pl.BlockSpec((CONV_WIDTH, LANES), lambda s: (0, s))],
        out_specs=pl.BlockSpec((lp, LANES), lambda s: (0, s)),
        out_shape=jax.ShapeDtypeStruct((lp, 3 * nh * HEAD_DIM), F32), name="gdn_prep",
        compiler_params=_cp("parallel"))(proj, conv_wt)


def _gdn_prep_bwd(proj, conv_wt, dq, dk, dv, dproj, nh):
    lp = proj.shape[0]
    scale = HEAD_DIM ** -0.5
    part = lambda p: pl.BlockSpec((lp, LANES), lambda s: (0, jnp.clip(s - p * nh, 0, nh - 1)))

    def body(x_ref, w_ref, dq_ref, dk_ref, dv_ref, _, dx_ref, dw_ref):
        which = pl.program_id(0) // nh
        x = x_ref[...]
        w = w_ref[...]
        c = _conv_taps(x, w)
        sg = _sigmoid(c)
        s = c * sg
        r = lax.rsqrt(jnp.sum(s * s, axis=-1, keepdims=True) + EPS)
        dy = jnp.where(which == 0, dq_ref[...], jnp.where(which == 1, dk_ref[...], dv_ref[...]))
        dy = jnp.where(_iota(s.shape, 0) >= PAD_ROWS, dy, 0.0)
        y0 = s * r
        dy0 = dy * jnp.where(which == 0, scale, 1.0)
        ds_n = r * (dy0 - y0 * jnp.sum(dy0 * y0, axis=-1, keepdims=True))
        ds = jnp.where(which == 2, dy, ds_n)
        dc = ds * (sg * (1.0 + c * (1.0 - sg)))
        dx = dc * w[CONV_WIDTH - 1:CONV_WIDTH, :]
        rows = [jnp.sum(dc * x, axis=0, keepdims=True)]
        for j in range(CONV_WIDTH - 2, -1, -1):
            sh = CONV_WIDTH - 1 - j
            dx = dx + pltpu.roll(dc, lp - sh, 0) * w[j:j + 1, :]
            rows.insert(0, jnp.sum(dc * pltpu.roll(x, sh, 0), axis=0, keepdims=True))
        dx_ref[...] = dx.astype(BF16)
        dw_ref[...] = jnp.concatenate(rows, axis=0)

    strip = pl.BlockSpec((lp, LANES), lambda s: (0, s))
    taps = pl.BlockSpec((CONV_WIDTH, LANES), lambda s: (0, s))
    return pl.pallas_call(
        body, grid=(3 * nh,), in_specs=[strip, taps, part(0), part(1), part(2), _ANY], out_specs=[strip, taps],
        out_shape=[jax.ShapeDtypeStruct(dproj.shape, BF16), jax.ShapeDtypeStruct((CONV_WIDTH, 3 * nh * HEAD_DIM), F32)],
        input_output_aliases={5: 0}, name="gdn_prep_bwd", compiler_params=_cp("parallel"))(proj, conv_wt, dq, dk, dv, dproj)


def _gates(proj, bias_row, nega_row, nh):
    lp = proj.shape[0]
    nc = lp // CHUNK

    def body(p_ref, b_ref, a_ref, g_ref, gt3_ref, gtf_ref):
        lane = _iota((CHUNK, LANES), 1)
        tri = (_iota((CHUNK, CHUNK), 0) >= _iota((CHUNK, CHUNK), 1)).astype(F32)

        def step(n, carry):
            r0 = pl.multiple_of(n * CHUNK, CHUNK)
            z = p_ref[pl.ds(r0, CHUNK), :] + b_ref[...]
            base = jnp.where(lane < nh, _sigmoid(z),
                             jnp.where(lane < 2 * nh, a_ref[...] * _softplus(z),
                                       jnp.where(lane < 3 * nh, -_softplus(-z), 0.0)))
            base = jnp.where(r0 + _iota((CHUNK, LANES), 0) >= PAD_ROWS, base, 0.0)
            cs = _hdot(tri, base)
            run = jnp.where((lane >= 2 * nh) & (lane < 3 * nh), cs + carry, cs)
            sh = pltpu.roll(run, 2 * nh, 1)
            out = base + jnp.where((lane >= 3 * nh) & (lane < 5 * nh), sh, 0.0)
            g_ref[pl.ds(r0, CHUNK), :] = out
            gt3_ref[n] = out.T
            return carry + cs[CHUNK - 1:CHUNK, :]

        lax.fori_loop(0, nc, step, jnp.zeros((1, LANES), F32))
        gtf_ref[...] = g_ref[...].T

    vec = pl.BlockSpec((1, LANES), lambda i: (0, 0))
    return pl.pallas_call(
        body, grid=(1,), in_specs=[pl.BlockSpec((lp, LANES), lambda i: (0, 8 * nh)), vec, vec],
        out_specs=[pl.BlockSpec((lp, LANES), lambda i: (0, 0)), pl.BlockSpec((nc, LANES, CHUNK), lambda i: (0, 0, 0)),
                   pl.BlockSpec((LANES, lp), lambda i: (0, 0))],
        out_shape=[jax.ShapeDtypeStruct((lp, LANES), F32), jax.ShapeDtypeStruct((nc, LANES, CHUNK), F32),
                   jax.ShapeDtypeStruct((LANES, lp), F32)],
        name="gates", compiler_params=_cp("arbitrary"))(proj, bias_row, nega_row)


def _gates_bwd(proj, bias_row, nega_row, gates, dgate_gdn, dc_t, dproj, nh):
    lp = proj.shape[0]
    nc = lp // CHUNK

    def body(p_ref, b_ref, a_ref, g_ref, dg_ref, dc_ref, _, dz_ref, sm_ref, dct_scr):
        lane = _iota((CHUNK, LANES), 1)
        triu = (_iota((CHUNK, CHUNK), 0) <= _iota((CHUNK, CHUNK), 1)).astype(F32)
        dct_scr[...] = dc_ref[...].T
        sm_ref[...] = jnp.zeros_like(sm_ref)
        dz_ref[:, LANES:] = jnp.zeros((lp, NARROW - LANES), BF16)

        def step(i, carry):
            n = nc - 1 - i
            r0 = pl.multiple_of(n * CHUNK, CHUNK)
            z = p_ref[pl.ds(r0, CHUNK), :] + b_ref[...]
            gt = g_ref[pl.ds(r0, CHUNK), :]
            dgd = dg_ref[pl.ds(r0, CHUNK), :]
            dch = dct_scr[pl.ds(r0, CHUNK), :]
            rc = _hdot(triu, dch) + carry
            sg = _sigmoid(z)
            dz = jnp.where(lane < nh, dgd * sg * (1.0 - sg),
                           jnp.where(lane < 2 * nh, dgd * a_ref[...] * sg,
                                     jnp.where(lane < 3 * nh, rc * (1.0 - sg), 0.0)))
            dz = jnp.where(r0 + _iota((CHUNK, LANES), 0) >= PAD_ROWS, dz, 0.0)
            dz_ref[pl.ds(r0, CHUNK), 0:LANES] = dz.astype(BF16)
            sm_ref[0:1, :] += jnp.sum(dz, axis=0, keepdims=True)
            sm_ref[1:2, :] += jnp.sum(jnp.where((lane >= nh) & (lane < 2 * nh), dgd * gt, 0.0), axis=0, keepdims=True)
            return carry + jnp.sum(dch, axis=0, keepdims=True)

        lax.fori_loop(0, nc, step, jnp.zeros((1, LANES), F32))

    vec = pl.BlockSpec((1, LANES), lambda i: (0, 0))
    full = pl.BlockSpec((lp, LANES), lambda i: (0, 0))
    last = pl.BlockSpec((lp, LANES), lambda i: (0, 8 * nh))
    tail = pl.BlockSpec((lp, NARROW), lambda i: (0, 8 * nh * LANES // NARROW))
    return pl.pallas_call(
        body, grid=(1,), in_specs=[last, vec, vec, full, full, pl.BlockSpec((LANES, lp), lambda i: (0, 0)), _ANY],
        out_specs=[tail, pl.BlockSpec((8, LANES), lambda i: (0, 0))],
        out_shape=[jax.ShapeDtypeStruct(dproj.shape, BF16), jax.ShapeDtypeStruct((8, LANES), F32)],
        scratch_shapes=[pltpu.VMEM((lp, LANES), F32)], input_output_aliases={6: 0},
        name="gates_bwd", compiler_params=_cp("arbitrary"))(proj, bias_row, nega_row, gates, dgate_gdn, dc_t, dproj)


def _tri_inv(a):
    t = jnp.where(_iota(a.shape, 1) == _iota(a.shape, 2), 1.0, 0.0) - a
    p = a
    for _ in range(CHUNK.bit_length() - 2):
        ph, pw = _split(p)
        p = _dot3(ph, ph) + (_dot3(ph, pw) + _dot3(pw, ph))
        ph, pw = _split(p)
        th, tw = _split(t)
        t = t + (_dot3(th, ph) + (_dot3(th, pw) + _dot3(tw, ph)))
    return t


def _gdn_chunk(q, k, v, beta, gc, gr, t=None):
    ii, jj = _iota((1, CHUNK, CHUNK), 1), _iota((1, CHUNK, CHUNK), 2)
    causal, strict = ii >= jj, ii > jj
    dm = jnp.where(causal, jnp.exp(jnp.where(causal, gc - gr, 0.0)), 0.0)
    kk = _bdot3(k, k, "nt")
    a = jnp.where(strict, beta * kk * dm, 0.0)
    if t is None:
        t = _tri_inv(a)
    eg = jnp.exp(gc)
    glast = gc[:, CHUNK - 1:CHUNK, :]
    ekd = jnp.exp(glast - gc)
    bv = beta * v
    bk = (beta * eg) * k
    ub = _bdot3(t, jnp.concatenate([bv, bk], axis=2))
    qk = _bdot3(q, k, "nt")
    return dict(causal=causal, strict=strict, dm=dm, kk=kk, a=a, t=t, eg=eg, ekd=ekd, bv=bv, bk=bk,
                u=ub[:, :, :HEAD_DIM], w=ub[:, :, HEAD_DIM:], qk=qk, aqk=jnp.where(causal, qk * dm, 0.0),
                q_dec=q * eg, k_dec=k * ekd, decay=jnp.exp(glast))


def _heads(ref, nh):
    return jnp.stack([ref[:, h * HEAD_DIM:(h + 1) * HEAD_DIM] for h in range(nh)], axis=0)


def _gdn_chunk_inputs(q_ref, k_ref, v_ref, g, gt, nh):
    col = lambda o: jnp.stack([g[:, o + h:o + h + 1] for h in range(nh)], axis=0)
    gr = jnp.stack([gt[3 * nh + h:3 * nh + h + 1, :] for h in range(nh)], axis=0)
    return _heads(q_ref, nh), _heads(k_ref, nh), _heads(v_ref, nh), col(0), col(3 * nh), gr


def _gdn_fwd(qkv, gates, gt3, nh, rider=None):
    lp = qkv.shape[0]
    nc = lp // CHUNK
    w = nh * HEAD_DIM

    def body(start, finish, q_ref, k_ref, v_ref, g_ref, gt_ref, o_ref, sall_ref, tall_ref, s_scr):
        @pl.when(pl.program_id(0) == 0)
        def _():
            start()
            s_scr[...] = jnp.zeros_like(s_scr)
        c = _gdn_chunk(*_gdn_chunk_inputs(q_ref, k_ref, v_ref, g_ref[...], gt_ref[0], nh))
        s = s_scr[...]
        sall_ref[0] = s
        tall_ref[0] = c["t"]
        v_new = c["u"] - _bdot3(c["w"], s)
        o = _bdot3(c["q_dec"], s) + _bdot3(c["aqk"], v_new)
        s_scr[...] = s * c["decay"] + _bdot3(c["k_dec"], v_new, "tn")
        for h in range(nh):
            o_ref[:, h * HEAD_DIM:(h + 1) * HEAD_DIM] = o[h]
        pl.when(pl.program_id(0) == nc - 1)(finish)

    outs, got = _hosted_call(
        body, [rider], 5, 3, 1, grid=(nc,),
        in_specs=[pl.BlockSpec((CHUNK, w), lambda n: (n, 0)), pl.BlockSpec((CHUNK, w), lambda n: (n, 1)),
                  pl.BlockSpec((CHUNK, w), lambda n: (n, 2)), pl.BlockSpec((CHUNK, LANES), lambda n: (n, 0)),
                  pl.BlockSpec((1, LANES, CHUNK), lambda n: (n, 0, 0))],
        out_specs=[pl.BlockSpec((CHUNK, w), lambda n: (n, 0)),
                   pl.BlockSpec((1, nh, HEAD_DIM, HEAD_DIM), lambda n: (n, 0, 0, 0)),
                   pl.BlockSpec((1, nh, CHUNK, CHUNK), lambda n: (n, 0, 0, 0))],
        out_shape=[jax.ShapeDtypeStruct((lp, w), F32), jax.ShapeDtypeStruct((nc, nh, HEAD_DIM, HEAD_DIM), F32),
                   jax.ShapeDtypeStruct((nc, nh, CHUNK, CHUNK), F32)],
        scratch_shapes=[pltpu.VMEM((nh, HEAD_DIM, HEAD_DIM), F32)],
        name="gdn_fwd", compiler_params=_cp("arbitrary"))(qkv, qkv, qkv, gates, gt3)
    return outs, (got[0] if got else None)


def _gdn_bwd(qkv, gates, gt3, s_all, t_all, do, nh, rider=None):
    lp = qkv.shape[0]
    nc = lp // CHUNK
    w = nh * HEAD_DIM
    rev = lambda n: nc - 1 - n

    def body(start, finish, q_ref, k_ref, v_ref, g_ref, gt_ref, s_ref, t_ref, do_ref, dq_ref, dk_ref, dv_ref, dg_ref, ds_scr):
        @pl.when(pl.program_id(0) == 0)
        def _():
            start()
            ds_scr[...] = jnp.zeros_like(ds_scr)
        q, k, v, beta, gc, gr = _gdn_chunk_inputs(q_ref, k_ref, v_ref, g_ref[...], gt_ref[0], nh)
        c = _gdn_chunk(q, k, v, beta, gc, gr, t_ref[0])
        s = s_ref[0]
        dsn = ds_scr[...]
        dout = _heads(do_ref, nh)
        v_new = c["u"] - _bdot3(c["w"], s)
        dq_dec = _bdot3(dout, s, "nt")
        daqk = jnp.where(c["causal"], _bdot3(dout, v_new, "nt"), 0.0)
        dv_new = _bdot3(c["aqk"], dout, "tn") + _bdot3(c["k_dec"], dsn)
        dk_dec = _bdot3(v_new, dsn, "nt")
        ddecay = jnp.sum(jnp.sum(dsn * s, axis=2, keepdims=True), axis=1, keepdims=True)
        dw = -_bdot3(dv_new, s, "nt")
        ds_scr[...] = _bdot3(c["q_dec"], dout, "tn") + c["decay"] * dsn - _bdot3(c["w"], dv_new, "tn")
        duw = jnp.concatenate([dv_new, dw], axis=2)
        dt = _bdot3(duw, jnp.concatenate([c["bv"], c["bk"]], axis=2), "nt")
        dbvk = _bdot3(c["t"], duw, "tn")
        dbv, dbk = dbvk[:, :, :HEAD_DIM], dbvk[:, :, HEAD_DIM:]
        da = jnp.where(c["strict"], -_bdot3(_bdot3(c["t"], dt, "tn"), c["t"], "nt"), 0.0)
        dkk = da * beta * c["dm"]
        dqk = daqk * c["dm"]
        e = da * c["a"] + daqk * c["aqk"]
        dq = dq_dec * c["eg"] + _bdot3(dqk, k)
        dk = (dk_dec * c["ekd"] + _bdot3(dkk, k) + _bdot3(dkk, k, "tn") + _bdot3(dqk, q, "tn")
              + (beta * c["eg"]) * dbk)
        dv = beta * dbv
        rs = lambda x: jnp.sum(x, axis=2, keepdims=True)
        dbeta = rs(dbv * v) + c["eg"] * rs(dbk * k) + rs(da * c["kk"] * c["dm"])
        kd_term = rs(dk_dec * c["k_dec"])
        eh, ew = _split(e)
        ones = jnp.ones((nh, CHUNK, LANES), BF16)
        col_sums = (_dot3(eh, ones, "tn") + _dot3(ew, ones, "tn"))[:, :, 0:1]
        dg_cum = rs(dq_dec * c["q_dec"]) - kd_term + rs(dbk * c["bk"]) + rs(e) - col_sums
        last = jnp.sum(kd_term, axis=1, keepdims=True) + ddecay * c["decay"]
        dg_cum = dg_cum + jnp.where(_iota((1, CHUNK, 1), 1) == CHUNK - 1, last, 0.0)
        lane = _iota((CHUNK, LANES), 1)
        acc = jnp.zeros((CHUNK, LANES), F32)
        for h in range(nh):
            sl = slice(h * HEAD_DIM, (h + 1) * HEAD_DIM)
            dq_ref[:, sl] = dq[h]
            dk_ref[:, sl] = dk[h]
            dv_ref[:, sl] = dv[h]
            acc = acc + jnp.where(lane == h, dbeta[h], 0.0) + jnp.where(lane == nh + h, dg_cum[h], 0.0)
        triu = (_iota((CHUNK, CHUNK), 0) <= _iota((CHUNK, CHUNK), 1)).astype(F32)
        dg_ref[...] = jnp.where(lane < nh, acc, _hdot(triu, acc))
        pl.when(pl.program_id(0) == nc - 1)(finish)

    outs, got = _hosted_call(
        body, [rider], 8, 4, 1, grid=(nc,),
        in_specs=[pl.BlockSpec((CHUNK, w), lambda n: (rev(n), 0)), pl.BlockSpec((CHUNK, w), lambda n: (rev(n), 1)),
                  pl.BlockSpec((CHUNK, w), lambda n: (rev(n), 2)), pl.BlockSpec((CHUNK, LANES), lambda n: (rev(n), 0)),
                  pl.BlockSpec((1, LANES, CHUNK), lambda n: (rev(n), 0, 0)),
                  pl.BlockSpec((1, nh, HEAD_DIM, HEAD_DIM), lambda n: (rev(n), 0, 0, 0)),
                  pl.BlockSpec((1, nh, CHUNK, CHUNK), lambda n: (rev(n), 0, 0, 0)),
                  pl.BlockSpec((CHUNK, w), lambda n: (rev(n), 0))],
        out_specs=[pl.BlockSpec((CHUNK, w), lambda n: (rev(n), 0))] * 3 + [pl.BlockSpec((CHUNK, LANES), lambda n: (rev(n), 0))],
        out_shape=[jax.ShapeDtypeStruct((lp, w), F32)] * 3 + [jax.ShapeDtypeStruct((lp, LANES), F32)],
        scratch_shapes=[pltpu.VMEM((nh, HEAD_DIM, HEAD_DIM), F32)],
        name="gdn_bwd", compiler_params=_cp("arbitrary"))(qkv, qkv, qkv, gates, gt3, s_all, t_all, do)
    return outs, (got[0] if got else None)


def _merge_gdn(o_gdn, proj, norm_w, nh):
    lp = o_gdn.shape[0]

    def body(o_ref, z_ref, w_ref, m_ref):
        o = o_ref[...]
        z = z_ref[...]
        m_ref[...] = (o * _rms(o) * w_ref[...] * (z * _sigmoid(z))).astype(BF16)

    return pl.pallas_call(
        body, grid=(nh,),
        in_specs=[pl.BlockSpec((lp, LANES), lambda s: (0, s)), pl.BlockSpec((lp, LANES), lambda s: (0, 3 * nh + s)),
                  pl.BlockSpec((1, LANES), lambda s: (0, 0))],
        out_specs=pl.BlockSpec((lp, LANES), lambda s: (0, s)),
        out_shape=jax.ShapeDtypeStruct((lp, 2 * nh * HEAD_DIM), BF16), name="merge_gdn",
        compiler_params=_cp("parallel"))(o_gdn, proj, norm_w)


def _merge_gdn_bwd(o_gdn, proj, norm_w, dmerged, nh):
    lp = o_gdn.shape[0]

    def body(o_ref, z_ref, w_ref, dm_ref, do_ref, dz_ref, dw_ref):
        o = o_ref[...]
        r = _rms(o)
        xh = o * r
        silu, dsilu = _silu_and_grad(z_ref[...])
        dm = dm_ref[...]
        dn = dm * silu
        dz_ref[...] = (dm * (xh * w_ref[...]) * dsilu).astype(BF16)
        dnw = dn * w_ref[...]
        do_ref[...] = r * (dnw - xh * jnp.mean(dnw * xh, axis=-1, keepdims=True))

        @pl.when(pl.program_id(0) == 0)
        def _():
            dw_ref[...] = jnp.zeros_like(dw_ref)
        dw_ref[...] += jnp.sum(dn * xh, axis=0, keepdims=True)

    w = nh * HEAD_DIM
    return pl.pallas_call(
        body, grid=(nh,),
        in_specs=[pl.BlockSpec((lp, LANES), lambda s: (0, s)), pl.BlockSpec((lp, LANES), lambda s: (0, 3 * nh + s)),
                  pl.BlockSpec((1, LANES), lambda s: (0, 0)), pl.BlockSpec((lp, LANES), lambda s: (0, s))],
        out_specs=[pl.BlockSpec((lp, LANES), lambda s: (0, s)), pl.BlockSpec((lp, LANES), lambda s: (0, 3 * nh + s)),
                   pl.BlockSpec((1, LANES), lambda s: (0, 0))],
        out_shape=[jax.ShapeDtypeStruct((lp, w), F32), jax.ShapeDtypeStruct((lp, 8 * w + NARROW), BF16),
                   jax.ShapeDtypeStruct((1, LANES), F32)],
        name="merge_gdn_bwd", compiler_params=_cp("arbitrary"))(o_gdn, proj, norm_w, dmerged)


def _fox_prep(proj, qk_w, nh):
    lp = proj.shape[0]

    def body(x_ref, w_ref, o_ref):
        x = x_ref[...]
        o_ref[...] = x * _rms(x) * w_ref[0]

    return pl.pallas_call(
        body, grid=(2 * nh,),
        in_specs=[pl.BlockSpec((lp, LANES), lambda s: (0, 4 * nh + s)), pl.BlockSpec((1, 1, LANES), lambda s: (s // nh, 0, 0))],
        out_specs=pl.BlockSpec((lp, LANES), lambda s: (0, s)),
        out_shape=jax.ShapeDtypeStruct((lp, 2 * nh * HEAD_DIM), F32), name="fox_prep",
        compiler_params=_cp("parallel"))(proj, qk_w)


def _fox_prep_bwd(proj, qk_w, dq, dk, dproj, nh):
    lp = proj.shape[0]
    part = lambda p: pl.BlockSpec((lp, LANES), lambda s: (0, jnp.clip(s - p * nh, 0, nh - 1)))

    def body(x_ref, w_ref, dq_ref, dk_ref, _, dx_ref, dw_ref):
        x = x_ref[...]
        r = _rms(x)
        xh = x * r
        dy = jnp.where(pl.program_id(0) < nh, dq_ref[...], dk_ref[...])
        dyw = dy * w_ref[0]
        dx_ref[...] = (r * (dyw - xh * jnp.mean(dyw * xh, axis=-1, keepdims=True))).astype(BF16)

        @pl.when(pl.program_id(0) % nh == 0)
        def _():
            dw_ref[...] = jnp.zeros_like(dw_ref)
        dw_ref[0] += jnp.sum(dy * xh, axis=0, keepdims=True)

    strip = pl.BlockSpec((lp, LANES), lambda s: (0, 4 * nh + s))
    wsp = pl.BlockSpec((1, 1, LANES), lambda s: (s // nh, 0, 0))
    return pl.pallas_call(
        body, grid=(2 * nh,), in_specs=[strip, wsp, part(0), part(1), _ANY], out_specs=[strip, wsp],
        out_shape=[jax.ShapeDtypeStruct(dproj.shape, BF16), jax.ShapeDtypeStruct((2, 1, LANES), F32)],
        input_output_aliases={4: 0}, name="fox_prep_bwd", compiler_params=_cp("arbitrary"))(proj, qk_w, dq, dk, dproj)


def _fox_probs(q, k, gates, crow, h, i, nh, lse=None):
    kl = k.shape[0]
    lane = _iota((Q_BLOCK, LANES), 1)
    ct = jnp.sum(jnp.where(lane == 4 * nh + h, gates, 0.0), axis=1, keepdims=True)
    tq, kq = _iota((Q_BLOCK, Q_BLOCK), 0), _iota((Q_BLOCK, Q_BLOCK), 1)
    qs = q * (HEAD_DIM ** -0.5)
    if i == 0:
        s = _bdot(qs, k, "nt") + (ct - crow)
        s = jnp.where((kq <= tq) & ((kq >= PAD_ROWS) | (tq < PAD_ROWS)), s, NEG)
    else:
        crow = jnp.where(_iota((1, kl), 1) < PAD_ROWS, -NEG, crow)
        s = _bdot(qs, k, "nt") + (ct - crow)
        s = jnp.concatenate([s[:, :kl - Q_BLOCK], jnp.where(kq <= tq, s[:, kl - Q_BLOCK:], NEG)], axis=1)
    if lse is not None:
        return jnp.exp(s - lse)
    m = jnp.max(s, axis=1, keepdims=True)
    p = jnp.exp(s - m)
    tot = jnp.sum(p, axis=1, keepdims=True)
    return p / tot, m + jnp.log(tot)


FOX_HEADS_PER_STEP = 2


def _fox_specs(lp, nh):
    hw = FOX_HEADS_PER_STEP * LANES
    return [pl.BlockSpec((Q_BLOCK, hw), lambda g, i: (i, g)),
            pl.BlockSpec((lp, hw), lambda g, i: (0, nh // FOX_HEADS_PER_STEP + g)),
            pl.BlockSpec((lp, hw), lambda g, i: (0, 6 * nh // FOX_HEADS_PER_STEP + g)),
            pl.BlockSpec((Q_BLOCK, LANES), lambda g, i: (i, 0)),
            pl.BlockSpec((LANES, lp), lambda g, i: (0, 0))]


def _fox_fwd(qkn, proj, gates, gtf, nh, rider=None):
    lp = qkn.shape[0]
    steps = nh // FOX_HEADS_PER_STEP, lp // Q_BLOCK

    def body(start, finish, q_ref, k_ref, v_ref, g_ref, gt_ref, o_ref, lse_ref):
        g, i = pl.program_id(0), pl.program_id(1)
        pl.when((g == 0) & (i == 0))(start)
        for j in range(lp // Q_BLOCK):
            @pl.when(i == j)
            def _(j=j):
                kl = (j + 1) * Q_BLOCK
                for hh in range(FOX_HEADS_PER_STEP):
                    h = FOX_HEADS_PER_STEP * g + hh
                    sl = slice(hh * LANES, (hh + 1) * LANES)
                    p, lse = _fox_probs(q_ref[:, sl], k_ref[0:kl, sl], g_ref[...], gt_ref[pl.ds(4 * nh + h, 1), :][:, 0:kl],
                                        h, j, nh)
                    o_ref[:, sl] = _bdot(p, v_ref[0:kl, sl])
                    lse_ref[:, sl] = jnp.broadcast_to(lse, (Q_BLOCK, LANES))
        pl.when((g == steps[0] - 1) & (i == steps[1] - 1))(finish)

    blk = pl.BlockSpec((Q_BLOCK, FOX_HEADS_PER_STEP * LANES), lambda g, i: (i, g))
    outs, got = _hosted_call(
        body, [rider], 5, 2, 0, grid=steps, in_specs=_fox_specs(lp, nh), out_specs=[blk, blk],
        out_shape=[jax.ShapeDtypeStruct((lp, nh * HEAD_DIM), F32)] * 2, name="fox_fwd",
        compiler_params=_cp(*(("parallel", "parallel") if rider is None else ("arbitrary", "arbitrary"))))(
            qkn, qkn, proj, gates, gtf)
    return outs, (got[0] if got else None)


def _fox_bwd(qkn, proj, gates, gtf, lse, do, dproj, nh):
    lp = qkn.shape[0]
    nq = lp // Q_BLOCK
    w = nh * HEAD_DIM
    scale = HEAD_DIM ** -0.5

    def body(q_ref, k_ref, v_ref, g_ref, gt_ref, lse_ref, do_ref, _, dq_ref, dk_ref, dc_ref, dv_ref, dv_scr):
        g, i = pl.program_id(0), pl.program_id(1)

        @pl.when(i == 0)
        def _():
            dk_ref[...] = jnp.zeros_like(dk_ref)
            dv_scr[...] = jnp.zeros_like(dv_scr)
            dc_ref[...] = jnp.zeros_like(dc_ref)
        for j in range(nq):
            @pl.when(i == j)
            def _(j=j):
                kl = (j + 1) * Q_BLOCK
                for hh in range(FOX_HEADS_PER_STEP):
                    h = FOX_HEADS_PER_STEP * g + hh
                    sl = slice(hh * LANES, (hh + 1) * LANES)
                    q, k = q_ref[:, sl], k_ref[0:kl, sl]
                    p = _fox_probs(q, k, g_ref[...], gt_ref[pl.ds(4 * nh + h, 1), :][:, 0:kl], h, j, nh,
                                   lse_ref[:, sl][:, 0:1])
                    dout = do_ref[:, sl]
                    dp = _bdot(dout, v_ref[0:kl, sl], "nt")
                    ds = p * (dp - jnp.sum(p * dp, axis=1, keepdims=True))
                    dq_ref[:, sl] = _bdot(ds, k) * scale
                    dk_ref[0:kl, sl] += _bdot(ds, q * scale, "tn")
                    dv_scr[0:kl, sl] += _bdot(p, dout, "tn")
                    dc_ref[hh, :, 0:kl] -= jnp.sum(ds, axis=0, keepdims=True)

        @pl.when(i == nq - 1)
        def _():
            dv_ref[...] = dv_scr[...].astype(BF16)

    hw = FOX_HEADS_PER_STEP * LANES
    blk = pl.BlockSpec((Q_BLOCK, hw), lambda g, i: (i, g))
    col = pl.BlockSpec((lp, hw), lambda g, i: (0, g))
    return pl.pallas_call(
        body, grid=(nh // FOX_HEADS_PER_STEP, nq), in_specs=_fox_specs(lp, nh) + [blk, blk, _ANY],
        out_specs=[blk, col, pl.BlockSpec((FOX_HEADS_PER_STEP, 1, lp), lambda g, i: (g, 0, 0)),
                   pl.BlockSpec((lp, hw), lambda g, i: (0, 6 * nh // FOX_HEADS_PER_STEP + g))],
        out_shape=[jax.ShapeDtypeStruct((lp, w), F32)] * 2 + [jax.ShapeDtypeStruct((nh, 1, lp), F32),
                                                             jax.ShapeDtypeStruct(dproj.shape, BF16)],
        scratch_shapes=[pltpu.VMEM((lp, hw), F32)], input_output_aliases={7: 3},
        name="fox_bwd", compiler_params=_cp("parallel", "arbitrary"))(qkn, qkn, proj, gates, gtf, lse, do, dproj)


def _merge_fox(o_fox, proj, merged, nh):
    lp = o_fox.shape[0]

    def body(o_ref, z_ref, _, m_ref):
        z = z_ref[...]
        m_ref[...] = (o_ref[...] * (z * _sigmoid(z))).astype(BF16)

    return pl.pallas_call(
        body, grid=(nh,),
        in_specs=[pl.BlockSpec((lp, LANES), lambda s: (0, s)), pl.BlockSpec((lp, LANES), lambda s: (0, 7 * nh + s)), _ANY],
        out_specs=pl.BlockSpec((lp, LANES), lambda s: (0, nh + s)),
        out_shape=jax.ShapeDtypeStruct(merged.shape, BF16), input_output_aliases={2: 0}, name="merge_fox",
        compiler_params=_cp("parallel"))(o_fox, proj, merged)


def _merge_fox_bwd(o_fox, proj, dmerged, dproj, nh):
    lp = o_fox.shape[0]

    def body(o_ref, z_ref, dm_ref, _, do_ref, dz_ref):
        silu, dsilu = _silu_and_grad(z_ref[...])
        dm = dm_ref[...]
        do_ref[...] = dm * silu
        dz_ref[...] = (dm * o_ref[...] * dsilu).astype(BF16)

    w = nh * HEAD_DIM
    return pl.pallas_call(
        body, grid=(nh,),
        in_specs=[pl.BlockSpec((lp, LANES), lambda s: (0, s)), pl.BlockSpec((lp, LANES), lambda s: (0, 7 * nh + s)),
                  pl.BlockSpec((lp, LANES), lambda s: (0, nh + s)), _ANY],
        out_specs=[pl.BlockSpec((lp, LANES), lambda s: (0, s)), pl.BlockSpec((lp, LANES), lambda s: (0, 7 * nh + s))],
        out_shape=[jax.ShapeDtypeStruct((lp, w), F32), jax.ShapeDtypeStruct(dproj.shape, BF16)],
        input_output_aliases={3: 1}, name="merge_fox_bwd", compiler_params=_cp("parallel"))(o_fox, proj, dmerged, dproj)


def _post(out, x, target, post_w):
    lp, d = out.shape

    def body(o_ref, x_ref, t_ref, w_ref, dy_ref, do_ref, loss_ref, dw_ref):
        i = pl.program_id(0)

        @pl.when(i == 0)
        def _():
            loss_ref[...] = jnp.zeros_like(loss_ref)
            dw_ref[...] = jnp.zeros_like(dw_ref)
        o = o_ref[...]
        r = _rms(o)
        nrm = o * r
        err = jnp.where(i > 0, x_ref[...] + nrm * w_ref[...] - t_ref[...], 0.0)
        loss_ref[0:1, :] += 0.5 * jnp.sum(jnp.sum(err * err, axis=1, keepdims=True), axis=0, keepdims=True) / d
        dy = err / d
        dy_ref[...] = dy
        dw_ref[...] += jnp.sum(dy * nrm, axis=0, keepdims=True)
        dyw = dy * w_ref[...]
        do_ref[...] = (r * (dyw - nrm * jnp.mean(dyw * nrm, axis=-1, keepdims=True))).astype(BF16)

    row = pl.BlockSpec((Q_BLOCK, d), lambda i: (i, 0))
    vec = pl.BlockSpec((1, d), lambda i: (0, 0))
    return pl.pallas_call(
        body, grid=(lp // Q_BLOCK,), in_specs=[row, _x_rows(d), _x_rows(d), vec],
        out_specs=[_x_rows(d), row, pl.BlockSpec((8, LANES), lambda i: (0, 0)), vec],
        out_shape=[jax.ShapeDtypeStruct(x.shape, F32), jax.ShapeDtypeStruct((lp, d), BF16),
                   jax.ShapeDtypeStruct((8, LANES), F32), jax.ShapeDtypeStruct((1, d), F32)],
        name="post", compiler_params=_cp("arbitrary"))(out, x, target, post_w)


def _prenorm_bwd(dxn, x, meta, w, dy, rider=None):
    seq, d = x.shape
    lp = seq + Q_BLOCK

    def body(start, finish, dx_ref, x_ref, m_ref, w_ref, dy_ref, gx_ref, gm_ref, dw_ref):
        i = pl.program_id(0)
        pl.when(i == 0)(start)
        h = _h_tile(i, x_ref, m_ref)
        r = _rms(h)
        xh = h * r
        dxn_ = dx_ref[...]
        dxw = dxn_ * w_ref[...]
        dh = jnp.where(i > 0, dy_ref[...], 0.0) + r * (dxw - xh * jnp.mean(dxw * xh, axis=-1, keepdims=True))
        gx_ref[...] = dh

        @pl.when(i == 0)
        def _():
            dw_ref[...] = jnp.zeros_like(dw_ref)
            gm_ref[...] = dh[PAD_ROWS:, :]
        dw_ref[...] += jnp.sum(dxn_ * xh, axis=0, keepdims=True)
        pl.when(i == lp // Q_BLOCK - 1)(finish)

    vec = pl.BlockSpec((1, d), lambda i: (0, 0))
    met = pl.BlockSpec((N_META, d), lambda i: (0, 0))
    outs, got = _hosted_call(
        body, [rider], 5, 3, 0, grid=(lp // Q_BLOCK,),
        in_specs=[pl.BlockSpec((Q_BLOCK, d), lambda i: (i, 0)), _x_rows(d), met, vec, _x_rows(d)],
        out_specs=[_x_rows(d), met, vec],
        out_shape=[jax.ShapeDtypeStruct((seq, d), F32), jax.ShapeDtypeStruct((N_META, d), F32),
                   jax.ShapeDtypeStruct((1, d), F32)],
        name="prenorm_bwd", compiler_params=_cp("arbitrary"))(dxn, x, meta, w, dy)
    return outs, (got[0] if got else None)


def _layer_grads(x, target, meta, pre_w, wfull, conv_wt, a_log, dt_bias, gdn_norm_w, fq_w, fk_w, f_bias, w_out, post_w,
                 late_weights=None, w_out_grads=None):
    nh = a_log.shape[1]
    zpad = jnp.zeros((1, LANES - 3 * nh), F32)
    bias_row = jnp.concatenate([jnp.zeros((1, nh), F32), dt_bias, f_bias, zpad], axis=1)
    nega_row = jnp.concatenate([jnp.zeros((1, nh), F32), -jnp.exp(a_log), jnp.zeros((1, nh), F32), zpad], axis=1)
    qk_w = jnp.stack([fq_w, fk_w])

    if isinstance(wfull, tuple):
        rider, meta_at, project = wfull
        xn, got = _prenorm_gathering(x, pre_w, rider, meta_at)
        proj, wfull, conv_wt, meta = project(xn, got)
    else:
        xn = _prenorm(x, meta, pre_w)
        proj = _matmul(xn, wfull, "nn", MM_TILE, F32, "proj")
    qkv = _gdn_prep(proj, conv_wt, nh)
    gates, gt3, gtf = _gates(proj, bias_row, nega_row, nh)
    (o_gdn, s_all, t_all), _ = _gdn_fwd(qkv, gates, gt3, nh, None)
    qkn = _fox_prep(proj, qk_w, nh)
    (o_fox, fox_lse), got = _fox_fwd(qkn, proj, gates, gtf, nh, None if late_weights is None else late_weights[0])
    if late_weights is not None:
        w_out = late_weights[1](got)
    merged = _merge_fox(o_fox, proj, _merge_gdn(o_gdn, proj, gdn_norm_w, nh), nh)
    out = _matmul(merged, w_out, "nn", 4 * LANES, F32, "out_proj")
    dy, dout, loss_blk, dpost_w = _post(out, x, target, post_w)

    dw_out = _matmul(merged, dout, "tn", 4 * LANES, BF16, "dw_out")
    if w_out_grads is None:
        dmerged, gdn_rider = _matmul(dout, w_out, "nt", 4 * LANES, F32, "dmerged"), None
    else:
        dmerged, got = _matmul(dout, w_out, "nt", 4 * LANES, F32, "dmerged", w_out_grads[0](dw_out))
        gdn_rider = w_out_grads[1](dw_out, got)
    do_gdn, dproj, dgdn_norm_w = _merge_gdn_bwd(o_gdn, proj, gdn_norm_w, dmerged, nh)
    do_fox, dproj = _merge_fox_bwd(o_fox, proj, dmerged, dproj, nh)
    dqn, dkn, dc_t, dproj = _fox_bwd(qkn, proj, gates, gtf, fox_lse, do_fox, dproj, nh)
    dproj, dqk_w = _fox_prep_bwd(proj, qk_w, dqn, dkn, dproj, nh)
    (dgq, dgk, dgv, dgate), w_out_parts = _gdn_bwd(qkv, gates, gt3, s_all, t_all, do_gdn, nh, gdn_rider)
    dproj, dconv_wt = _gdn_prep_bwd(proj, conv_wt, dgq, dgk, dgv, dproj, nh)
    dc_rows = jnp.pad(dc_t.reshape(nh, -1), ((2 * nh, LANES - 3 * nh), (0, 0)))
    dproj, gate_sums = _gates_bwd(proj, bias_row, nega_row, gates, dgate, dc_rows, dproj, nh)
    return dict(
        loss=loss_blk[0:1, 0:1], dy=dy, xn=xn, dproj=dproj, post_w=dpost_w,
        conv_wt=dconv_wt, a_log=gate_sums[1:2, nh:2 * nh], dt_bias=gate_sums[0:1, nh:2 * nh],
        gdn_norm_w=dgdn_norm_w, fq_w=dqk_w[0], fk_w=dqk_w[1], f_bias=gate_sums[0:1, 2 * nh:3 * nh], w_out=dw_out,
        w_out_parts=w_out_parts, wfull=wfull, meta=meta)


def _cast_bf16(a, tr, name):
    r, c = a.shape

    def body(a_ref, o_ref):
        o_ref[...] = a_ref[...].astype(BF16)

    return pl.pallas_call(
        body, grid=(r // tr,), in_specs=[pl.BlockSpec((tr, c), lambda i: (i, 0))],
        out_specs=pl.BlockSpec((tr, c), lambda i: (i, 0)), out_shape=jax.ShapeDtypeStruct((r, c), BF16),
        name=name, compiler_params=_cp("parallel"))(a)


def _column_major(a):
    return jnp.transpose(a, (2, 0, 1))


def _cast_bf16_column_major(a3, pieces, name):
    _, r, c = a3.shape
    rows = r // pieces

    def body(a_ref, *o_refs):
        t = a_ref[...].reshape(LANES, r).T.astype(BF16)
        for k, o_ref in enumerate(o_refs):
            o_ref[...] = t[k * rows:(k + 1) * rows]

    return pl.pallas_call(
        body, grid=(pl.cdiv(c, LANES),), in_specs=[pl.BlockSpec((LANES, 1, r), lambda i: (i, 0, 0))],
        out_specs=[pl.BlockSpec((rows, LANES), lambda i: (0, i))] * pieces,
        out_shape=[jax.ShapeDtypeStruct((rows, c), BF16)] * pieces, name=name, compiler_params=_cp("parallel"))(_column_major(a3))


def _adamw_column_major(w3, parts, m3, v3, name):
    _, r, c = w3.shape
    n_parts = parts[0].shape[0]

    def body(w_ref, *refs):
        p_refs, (m_ref, v_ref, g_ref, d_ref, nm_ref, nv_ref) = refs[:len(parts)], refs[len(parts):]
        sums = []
        for p_ref in p_refs:
            g = p_ref[0].astype(F32)
            for s in range(1, n_parts):
                g = g + p_ref[s].astype(F32)
            sums.append(g)
        g = jnp.concatenate(sums, axis=0).T
        flat = lambda ref: ref[...].reshape(LANES, r)
        m_new = ADAM_B1 * flat(m_ref) + (1.0 - ADAM_B1) * g
        v_new = ADAM_B2 * flat(v_ref) + (1.0 - ADAM_B2) * (g * g)
        m_hat = m_new / (1.0 - ADAM_B1 ** ADAM_STEP)
        v_hat = v_new / (1.0 - ADAM_B2 ** ADAM_STEP)
        delta = -ADAM_LR * (m_hat / (jnp.sqrt(v_hat) + ADAM_EPS) + ADAM_WD * flat(w_ref))
        for ref, val in ((g_ref, g), (d_ref, delta), (nm_ref, m_new), (nv_ref, v_new)):
            ref[...] = val.reshape(LANES, 1, r)

    blk = pl.BlockSpec((LANES, 1, r), lambda i: (i, 0, 0))
    outs = pl.pallas_call(
        body, grid=(pl.cdiv(c, LANES),),
        in_specs=[blk] + [pl.BlockSpec((n_parts, p.shape[1], LANES), lambda i: (0, 0, i)) for p in parts] + [blk, blk],
        out_specs=[blk] * 4, out_shape=[jax.ShapeDtypeStruct((c, 1, r), F32)] * 4, name=name,
        compiler_params=_cp("parallel"))(_column_major(w3), *parts, _column_major(m3), _column_major(v3))
    return [jnp.transpose(o, (1, 2, 0)) for o in outs]


def _gather_copies(ins, outs, send_sems, recv_sems, local_sems):
    n = len(ins)
    x, y, c = lax.axis_index("x"), lax.axis_index("y"), lax.axis_index("c")
    me, sibling = (x, y, c), (x, y, 1 - c)
    xn, yn, dg = (1 - x, y), (x, 1 - y), (1 - x, 1 - y)

    def copy(a, k, block, to, src=None):
        px, py, pc = block
        rows = outs[a].at[4 * px + 2 * py + pc]
        return pltpu.make_async_remote_copy(
            src_ref=rows if src is None else src, dst_ref=rows, send_sem=send_sems.at[a, k],
            recv_sem=recv_sems.at[a, k], device_id=to, device_id_type=_MESH)

    local = [pltpu.make_async_copy(ins[a], outs[a].at[4 * x + 2 * y + c], local_sems.at[a]) for a in range(n)]
    own = [cp for a in range(n) for cp in (copy(a, 0, me, sibling, src=ins[a]), copy(a, 1, me, (*xn, c), src=ins[a]),
                                           copy(a, 2, me, (*yn, c), src=ins[a]))]

    def start():
        for cp in local + own:
            cp.start()

    def finish():
        for a in range(n):
            @pl.when(c == 1)
            def _(a=a):
                copy(a, 1, (*xn, c), me).wait_recv()
                copy(a, 3, (*xn, c), (*yn, c)).start()

            @pl.when(c == 0)
            def _(a=a):
                copy(a, 2, (*yn, c), me).wait_recv()
                copy(a, 3, (*yn, c), (*xn, c)).start()
        for a in range(n):
            pl.when(c == 0)(copy(a, 1, (*xn, c), me).wait_recv)
            copy(a, 4, (*xn, c), sibling).start()
            pl.when(c == 1)(copy(a, 2, (*yn, c), me).wait_recv)
            copy(a, 5, (*yn, c), sibling).start()
        for a in range(n):
            copy(a, 3, (*dg, c), me).wait_recv()
            copy(a, 6, (*dg, c), sibling).start()
        for a in range(n):
            copy(a, 0, sibling, me).wait_recv()
            for k, chip in ((4, xn), (5, yn), (6, dg)):
                copy(a, k, (*chip, 1 - c), me).wait_recv()
                copy(a, k, (*chip, c), sibling).wait_send()
            copy(a, 3, (*xn, c), (*yn, c)).wait_send()
        for cp in own:
            cp.wait_send()
        for cp in local:
            cp.wait()

    return start, finish


def _gather_scratch(n):
    return [pltpu.SemaphoreType.DMA((n, N_DEV - 1)), pltpu.SemaphoreType.DMA((n, N_DEV - 1)), pltpu.SemaphoreType.DMA((n,))]


def _gather_rider(arrays):
    return _Rider(list(arrays), [jax.ShapeDtypeStruct((N_DEV,) + a.shape, a.dtype) for a in arrays],
                  _gather_scratch(len(arrays)), {}, lambda ins, outs, scratch: _gather_copies(ins, outs, *scratch))


def _all_gather(arrays, name):
    n = len(arrays)

    def body(*refs):
        start, finish = _gather_copies(refs[:n], refs[n:2 * n], *refs[2 * n:])
        start()
        finish()

    return pl.pallas_call(
        body, in_specs=[_ANY] * n, out_specs=[_ANY] * n,
        out_shape=[jax.ShapeDtypeStruct((N_DEV,) + a.shape, a.dtype) for a in arrays],
        scratch_shapes=_gather_scratch(n), name=name)(*arrays)


SLAB = 10 * LANES


def _slab_start(blk, nh, cols):
    in_second_half = blk >= N_DEV // 2
    shift = (2 * nh if in_second_half else 0) if isinstance(blk, int) else jnp.where(in_second_half, 2 * nh, 0)
    return (blk * cols - shift) // LANES * LANES


def _pair_rider(dw_rows=None, parts=None, nh=None, after=None):
    if dw_rows is not None:
        r, full = dw_rows.shape
        cols = (full - NARROW + 3 * nh) // N_DEV
        out_shapes = [jax.ShapeDtypeStruct((N_CHIP, r, SLAB), dw_rows.dtype), jax.ShapeDtypeStruct((r, NARROW), dw_rows.dtype)]
    else:
        out_shapes = [jax.ShapeDtypeStruct((N_CHIP,) + parts.shape[1:], parts.dtype)]

    def make(ins, outs, scratch):
        send_sems, recv_sems = scratch
        x, y, c = lax.axis_index("x"), lax.axis_index("y"), lax.axis_index("c")
        kw = lambda k: dict(send_sem=send_sems.at[k], recv_sem=recv_sems.at[k], device_id=(x, y, 1 - c), device_id_type=_MESH)
        copies = []
        for q in range(N_CHIP):
            if dw_rows is not None:
                first = pl.multiple_of(_slab_start(2 * q + 1 - c, nh, cols), LANES)
                copies.append(pltpu.make_async_remote_copy(src_ref=ins[0].at[:, pl.ds(first, SLAB)], dst_ref=outs[0].at[q], **kw(q)))
            else:
                copies.append(pltpu.make_async_remote_copy(src_ref=ins[0].at[2 * q + 1 - c], dst_ref=outs[0].at[q], **kw(q)))
        if dw_rows is not None:
            copies.append(pltpu.make_async_remote_copy(src_ref=ins[0].at[:, pl.ds(full - NARROW, NARROW)], dst_ref=outs[1],
                                                       **kw(N_CHIP)))

        def start():
            for cp in copies:
                cp.start()

        def finish():
            for cp in copies:
                cp.wait()

        return start, finish

    return _Rider([dw_rows if dw_rows is not None else parts] + ([] if after is None else [after]), out_shapes,
                  [pltpu.SemaphoreType.DMA((N_CHIP + 1,)), pltpu.SemaphoreType.DMA((N_CHIP + 1,))], {}, make)


def _relayout_pair_sum(dwfull, got_slabs, got_tail, core, nh, tr, name):
    d, full = dwfull.shape
    w = nh * HEAD_DIM
    cols = (8 * w + 3 * nh) // N_DEV
    segs = _native_segments(nh)

    def block(f_ref, s_ref, t_ref, q, blk):
        st = _slab_start(blk, nh, cols)
        wide = f_ref[:, st:st + SLAB].astype(F32) + s_ref[q].astype(F32)
        tail = f_ref[:, 8 * w:].astype(F32) + t_ref[...].astype(F32)
        pieces = []
        for s0, s1, t0 in segs:
            lo, hi = max(s0, blk * cols), min(s1, (blk + 1) * cols)
            if lo < hi:
                at = t0 + lo - s0
                pieces.append(tail[:, at - 8 * w:at - 8 * w + hi - lo] if at >= 8 * w else wide[:, at - st:at - st + hi - lo])
        return (pieces[0] if len(pieces) == 1 else jnp.concatenate(pieces, axis=1)).astype(dwfull.dtype)

    def body(core_ref, f_ref, s_ref, t_ref, o_ref):
        for parity in range(2):
            @pl.when(core_ref[0] == parity)
            def _(parity=parity):
                for q in range(N_CHIP):
                    o_ref[q] = block(f_ref, s_ref, t_ref, q, 2 * q + parity)

    return pl.pallas_call(
        body,
        grid_spec=pltpu.PrefetchScalarGridSpec(
            num_scalar_prefetch=1, grid=(d // tr,),
            in_specs=[pl.BlockSpec((tr, full), lambda i, c_ref: (i, 0)), pl.BlockSpec((N_CHIP, tr, SLAB), lambda i, c_ref: (0, i, 0)),
                      pl.BlockSpec((tr, NARROW), lambda i, c_ref: (i, 0))],
            out_specs=pl.BlockSpec((N_CHIP, tr, cols), lambda i, c_ref: (0, i, 0))),
        out_shape=jax.ShapeDtypeStruct((N_CHIP, d, cols), dwfull.dtype), name=name,
        compiler_params=_cp("parallel"))(core, dwfull, got_slabs, got_tail)


def _pair_sum(parts, got, core, tr, name):
    _, r, c = parts.shape

    def body(core_ref, p_ref, g_ref, o_ref):
        o_ref[...] = (p_ref[...].astype(F32) + g_ref[...].astype(F32)).astype(o_ref.dtype)

    return pl.pallas_call(
        body,
        grid_spec=pltpu.PrefetchScalarGridSpec(
            num_scalar_prefetch=1, grid=(N_CHIP, r // tr),
            in_specs=[pl.BlockSpec((1, tr, c), lambda q, i, core_ref: (2 * q + core_ref[0], i, 0)),
                      pl.BlockSpec((1, tr, c), lambda q, i, core_ref: (q, i, 0))],
            out_specs=pl.BlockSpec((1, tr, c), lambda q, i, core_ref: (q, i, 0))),
        out_shape=jax.ShapeDtypeStruct((N_CHIP, r, c), parts.dtype), name=name,
        compiler_params=_cp("parallel", "parallel"))(core, parts, got)


def _native_segments(nh):
    w = nh * HEAD_DIM
    return [(0, 4 * w, 0), (4 * w, 4 * w + 2 * nh, 8 * w), (4 * w + 2 * nh, 8 * w + 2 * nh, 4 * w),
            (8 * w + 2 * nh, 8 * w + 3 * nh, 8 * w + 2 * nh)]


def _relayout_w_in(wg, nh, tr, name, rows_total=None, into=None):
    _, d, cols = wg.shape
    w = nh * HEAD_DIM
    rows_total = into.shape[0] if into is not None else rows_total or d
    first = (rows_total - d) // tr if into is not None else 0

    def native(ref, j0, j1):
        out = []
        while j0 < j1:
            blk = j0 // cols
            end = min(j1, (blk + 1) * cols)
            out.append(ref[blk, :, pl.ds(j0 - blk * cols, end - j0)])
            j0 = end
        return out

    def body(g_ref, *refs):
        o_ref = refs[-1]
        for cidx in range(8 * w // LANES):
            j0 = cidx * LANES + (0 if cidx * LANES < 4 * w else 2 * nh)
            pieces = native(g_ref, j0, j0 + LANES)
            o_ref[:, cidx * LANES:(cidx + 1) * LANES] = pieces[0] if len(pieces) == 1 else jnp.concatenate(pieces, axis=1)
        pieces = (native(g_ref, 4 * w, 4 * w + 2 * nh) + native(g_ref, 8 * w + 2 * nh, 8 * w + 3 * nh)
                  + [jnp.zeros((tr, NARROW - 3 * nh), wg.dtype)])
        o_ref[:, 8 * w:] = jnp.concatenate(pieces, axis=1)

    return pl.pallas_call(
        body, grid=(d // tr,), in_specs=[pl.BlockSpec((N_DEV, tr, cols), lambda i: (0, i, 0))] + [_ANY] * (into is not None),
        out_specs=pl.BlockSpec((tr, 8 * w + NARROW), lambda i: (first + i, 0)),
        out_shape=jax.ShapeDtypeStruct((rows_total, 8 * w + NARROW), wg.dtype),
        input_output_aliases={1: 0} if into is not None else {},
        name=name, compiler_params=_cp("parallel"))(wg, *([into] if into is not None else []))


def _adamw(w, parts, m, v, tr, name, after=None):
    r, c = w.shape
    n_parts = parts.shape[0]

    def body(w_ref, p_ref, m_ref, v_ref, *refs):
        g_ref, d_ref, nm_ref, nv_ref = refs[-5:-1] if after is not None else refs
        if after is not None:
            refs[-1][...] = jnp.zeros_like(refs[-1])
        g = p_ref[0].astype(F32)
        for s in range(1, n_parts):
            g = g + p_ref[s].astype(F32)
        m_new = ADAM_B1 * m_ref[...] + (1.0 - ADAM_B1) * g
        v_new = ADAM_B2 * v_ref[...] + (1.0 - ADAM_B2) * (g * g)
        m_hat = m_new / (1.0 - ADAM_B1 ** ADAM_STEP)
        v_hat = v_new / (1.0 - ADAM_B2 ** ADAM_STEP)
        g_ref[...] = g
        d_ref[...] = -ADAM_LR * (m_hat / (jnp.sqrt(v_hat) + ADAM_EPS) + ADAM_WD * w_ref[...])
        nm_ref[...] = m_new
        nv_ref[...] = v_new

    blk = pl.BlockSpec((tr, c), lambda i: (i, 0))
    extra = after is not None
    return pl.pallas_call(
        body, grid=(r // tr,),
        in_specs=[blk, pl.BlockSpec((n_parts, tr, c), lambda i: (0, i, 0)), blk, blk] + [_ANY] * extra,
        out_specs=[blk] * 4 + [pl.BlockSpec((8, LANES), lambda i: (0, 0))] * extra,
        out_shape=[jax.ShapeDtypeStruct((r, c), F32)] * 4 + [jax.ShapeDtypeStruct((8, LANES), F32)] * extra, name=name,
        compiler_params=_cp("arbitrary" if extra else "parallel"))(w, parts, m, v, *([after] if extra else []))


def _adamw_conv(w, gathered, dev, m, v, name):
    c, r = w.shape
    n_parts = gathered.shape[0]

    def body(dev_ref, w_ref, p_ref, m_ref, v_ref, g_ref, d_ref, nm_ref, nv_ref):
        g = p_ref[0].astype(F32)
        for s in range(1, n_parts):
            g = g + p_ref[s].astype(F32)
        m_new = ADAM_B1 * m_ref[...] + (1.0 - ADAM_B1) * g
        v_new = ADAM_B2 * v_ref[...] + (1.0 - ADAM_B2) * (g * g)
        m_hat = m_new / (1.0 - ADAM_B1 ** ADAM_STEP)
        v_hat = v_new / (1.0 - ADAM_B2 ** ADAM_STEP)
        g_ref[...] = g
        d_ref[...] = -ADAM_LR * (m_hat / (jnp.sqrt(v_hat) + ADAM_EPS) + ADAM_WD * w_ref[...])
        nm_ref[...] = m_new
        nv_ref[...] = v_new

    blk = pl.BlockSpec((c, r), lambda i, dev_ref: (0, 0))
    return pl.pallas_call(
        body,
        grid_spec=pltpu.PrefetchScalarGridSpec(
            num_scalar_prefetch=1, grid=(1,),
            in_specs=[blk, pl.BlockSpec((n_parts, c, r), lambda i, dev_ref: (0, 0, dev_ref[0])), blk, blk], out_specs=[blk] * 4),
        out_shape=[jax.ShapeDtypeStruct((c, r), F32)] * 4, name=name, compiler_params=_cp("arbitrary"))(dev, w, gathered, m, v)


def _pack_small(d, pre, post, a_log, dt_bias, f_bias, gdn_w, fq_w, fk_w, extra):
    row2 = jnp.concatenate([a_log, dt_bias, f_bias, gdn_w, fq_w, fk_w, extra], axis=1)
    row2 = jnp.pad(row2, ((0, 0), (0, d - row2.shape[1])))
    return jnp.concatenate([pre, post, row2, jnp.zeros((5, d), F32)], axis=0)


def _adamw_small(w, parts, m, v, nh, name):
    d = w.shape[1]
    n_parts = parts.shape[0]
    shapes = dict(pre=d, post=d, a_log=nh, dt_bias=nh, f_bias=nh, gdn_w=HEAD_DIM, fq_w=HEAD_DIM, fk_w=HEAD_DIM, extra=1)

    def body(w_ref, p_ref, m_ref, v_ref, *o_refs):
        g = p_ref[0]
        for s in range(1, n_parts):
            g = g + p_ref[s]
        m_new = ADAM_B1 * m_ref[...] + (1.0 - ADAM_B1) * g
        v_new = ADAM_B2 * v_ref[...] + (1.0 - ADAM_B2) * (g * g)
        m_hat = m_new / (1.0 - ADAM_B1 ** ADAM_STEP)
        v_hat = v_new / (1.0 - ADAM_B2 ** ADAM_STEP)
        delta = -ADAM_LR * (m_hat / (jnp.sqrt(v_hat) + ADAM_EPS) + ADAM_WD * w_ref[...])
        for k, val in enumerate((g, delta, m_new, v_new)):
            vectors = _unpack_small(val, nh)
            for j, key in enumerate(shapes):
                o_refs[k * len(shapes) + j][...] = vectors[key]

    full = lambda shape: pl.BlockSpec(shape, lambda i: (0,) * len(shape))
    outs = pl.pallas_call(
        body, grid=(1,), in_specs=[full(w.shape), full(parts.shape), full(w.shape), full(w.shape)],
        out_specs=[full((1, n)) for n in shapes.values()] * 4,
        out_shape=[jax.ShapeDtypeStruct((1, n), F32) for n in shapes.values()] * 4, name=name,
        compiler_params=_cp("arbitrary"))(w, parts, m, v)
    return [dict(zip(shapes, outs[k * len(shapes):(k + 1) * len(shapes)])) for k in range(4)]


def _unpack_small(p, nh):
    o = 3 * nh
    return dict(pre=p[0:1], post=p[1:2], a_log=p[2:3, 0:nh], dt_bias=p[2:3, nh:2 * nh], f_bias=p[2:3, 2 * nh:o],
                gdn_w=p[2:3, o:o + HEAD_DIM], fq_w=p[2:3, o + HEAD_DIM:o + 2 * HEAD_DIM],
                fk_w=p[2:3, o + 2 * HEAD_DIM:o + 3 * HEAD_DIM], extra=p[2:3, o + 3 * HEAD_DIM:o + 3 * HEAD_DIM + 1])


def kernel(x, meta_tokens, pre_norm_w, w_in, conv_w, a_log, dt_bias, gdn_norm_w, fox_q_norm_w, fox_k_norm_w, fox_f_bias, w_out, post_norm_w, loss_target, m_meta_tokens, m_pre_norm_w, m_w_in, m_conv_w, m_a_log, m_dt_bias, m_gdn_norm_w, m_fox_q_norm_w, m_fox_k_norm_w, m_fox_f_bias, m_w_out, m_post_norm_w, v_meta_tokens, v_pre_norm_w, v_w_in, v_conv_w, v_a_log, v_dt_bias, v_gdn_norm_w, v_fox_q_norm_w, v_fox_k_norm_w, v_fox_f_bias, v_w_out, v_post_norm_w):
    nh = a_log.shape[1]
    d = x.shape[-1]
    w = nh * HEAD_DIM
    zero = jnp.zeros((1, 1), F32)

    w_in_a, w_in_b = _cast_bf16_column_major(w_in, 2, "cast_w_in")

    def project(xn, got):
        wg, cg, mg = got
        half = (d // 2, 0), (d // 2, 1)
        wfull = _relayout_w_in(wg, nh, 256, "relayout_w_in_a", rows_total=d)
        proj, got = _matmul(xn, wfull, "nn", MM_TILE, F32, "proj_a", _gather_rider([w_in_b]), a_cols=half[0], b_rows=half[0])
        wfull = _relayout_w_in(got[0], nh, 256, "relayout_w_in_b", into=wfull)
        proj = _matmul(xn, wfull, "nn", MM_TILE, F32, "proj_b", a_cols=half[1], b_rows=half[1], acc=proj)
        return proj, wfull, cg.transpose(1, 0, 2).reshape(CONV_WIDTH, 3 * w), mg.transpose(1, 0, 2).reshape(N_META, d)

    late_weights = (_gather_rider([_cast_bf16(w_out[0], 256, "cast_w_out")]), lambda got: got[0].reshape(2 * w, d))
    core = lax.axis_index("c")
    dev = 4 * lax.axis_index("x") + 2 * lax.axis_index("y") + core
    core_arr = jnp.reshape(core, (1,)).astype(jnp.int32)

    out_parts = lambda dw_out: dw_out.reshape(N_DEV, 2 * w // N_DEV, d)
    g = _layer_grads(
        x[0], loss_target[0], None, pre_norm_w, (_gather_rider([w_in_a, conv_w[0].T, meta_tokens]), 2, project), None,
        a_log, dt_bias, gdn_norm_w,
        fox_q_norm_w, fox_k_norm_w, fox_f_bias, None, post_norm_w, late_weights=late_weights,
        w_out_grads=(lambda dw_out: _pair_rider(parts=out_parts(dw_out)),
                     lambda dw_out, got: _chip_rider(_pair_sum(out_parts(dw_out), got[0], core_arr, 256, "pair_sum_w_out"))))
    p_out = g["w_out_parts"][0]
    xn, dproj, wfull, meta_full = g["xn"], g["dproj"], g["wfull"], g["meta"]
    flights, token, dw, rider = [], None, None, None

    def exchange(dw, got, i):
        sums = _relayout_pair_sum(dw, got[0], got[1], core_arr, nh, 128, f"relayout_pair_sum_{i}")
        flight, token = _chip_exchange_start(sums, f"chip_exchange_start_{i}")
        flights.append(flight)
        return token

    for i, (index, parts) in enumerate(DW_IN_PIECES):
        res = _matmul(xn, dproj, "tn", MM_TILE, BF16, f"dw_in_{i}", rider, a_cols=(d // parts, index))
        if i:
            token = exchange(dw, res[1], i - 1)
        dw = res[0] if i else res
        rider = _pair_rider(dw_rows=dw, nh=nh, after=token)
    dxn, (got,) = _dxn(dproj, wfull, [rider], MM_TILE, "dxn")
    token = exchange(dw, got, len(DW_IN_PIECES) - 1)
    *r_out, token = _adamw(w_out[0], p_out, m_w_out[0], v_w_out[0], 64, "adamw_w_out", after=token)
    (grad_x, dmeta, dpre_w), _ = _prenorm_bwd(dxn, x[0], meta_full, pre_norm_w + token[0:1, 0:1], g["dy"])
    small = _pack_small(d, dpre_w, g["post_w"], g["a_log"], g["dt_bias"], g["f_bias"], g["gdn_norm_w"], g["fq_w"],
                        g["fk_w"], g["loss"])
    a_conv, a_meta, p_small = _all_gather([g["conv_wt"], dmeta, small], "gather_small_grads")
    p_meta = lax.dynamic_slice_in_dim(a_meta, dev * meta_tokens.shape[1], meta_tokens.shape[1], axis=2)

    r_conv = _adamw_conv(conv_w[0].T, a_conv, jnp.reshape(dev, (1,)).astype(jnp.int32), m_conv_w[0].T, v_conv_w[0].T,
                         "adamw_conv_w")
    r_conv = [o.T for o in r_conv]
    r_meta = _adamw(meta_tokens, p_meta, m_meta_tokens, v_meta_tokens, N_META, "adamw_meta")
    pk = lambda pre, post, a, dt, gw, fq, fk, fb: _pack_small(d, pre, post, a, dt, fb, gw, fq, fk, zero)
    sm = _adamw_small(
        pk(pre_norm_w, post_norm_w, a_log, dt_bias, gdn_norm_w, fox_q_norm_w, fox_k_norm_w, fox_f_bias), p_small,
        pk(m_pre_norm_w, m_post_norm_w, m_a_log, m_dt_bias, m_gdn_norm_w, m_fox_q_norm_w, m_fox_k_norm_w, m_fox_f_bias),
        pk(v_pre_norm_w, v_post_norm_w, v_a_log, v_dt_bias, v_gdn_norm_w, v_fox_q_norm_w, v_fox_k_norm_w, v_fox_f_bias),
        nh, "adamw_small")
    p_in = _chip_exchange_wait(flights, sm[0]["pre"], "chip_exchange_wait")
    r_in = _adamw_column_major(w_in, p_in, m_w_in, v_w_in, "adamw_w_in")

    outs = []
    for i in range(4):
        s = sm[i]
        outs += [r_meta[i], s["pre"], r_in[i], r_conv[i][None], s["a_log"], s["dt_bias"], s["gdn_w"], s["fq_w"],
                 s["fk_w"], s["f_bias"], r_out[i][None], s["post"]]
    return (sm[0]["extra"].reshape(()), grad_x[None], *outs)
```

```python
import jax
import jax.numpy as jnp
from jax import lax
from jax.experimental import pallas as pl
from jax.experimental.pallas import tpu as pltpu

F32, BF16 = jnp.float32, jnp.bfloat16
HEAD_DIM = 128
N_META = 16
CONV_WIDTH = 4
CHUNK = 128
Q_BLOCK = 128
LANES = 128
EPS = 1e-6
PAD_ROWS = Q_BLOCK - N_META
N_DEV = 8
N_CHIP = 4
VMEM_LIMIT = 56 * 1024 * 1024
NEG = -1e30
NARROW = 2 * LANES
MM_TILE = 6 * LANES
DW_IN_PIECES = ((0, 2), (2, 4), (3, 4))

ADAM_LR, ADAM_B1, ADAM_B2, ADAM_EPS, ADAM_WD, ADAM_STEP = 0.001, 0.9, 0.999, 1e-08, 0.01, 10

_DN = {"nn": (((1,), (0,)), ((), ())), "nt": (((1,), (1,)), ((), ())), "tn": (((0,), (0,)), ((), ()))}
_DN3 = {"nn": (((2,), (1,)), ((0,), (0,))), "nt": (((2,), (2,)), ((0,), (0,))), "tn": (((1,), (1,)), ((0,), (0,)))}
_ANY = pl.BlockSpec(memory_space=pl.ANY)
_MESH = pl.DeviceIdType.MESH


def _cp(*sem):
    return pltpu.CompilerParams(dimension_semantics=sem, vmem_limit_bytes=VMEM_LIMIT)


def _dot(a, b, dims="nn", prec=None):
    return lax.dot_general(a, b, _DN[dims], precision=prec, preferred_element_type=F32)


def _bdot(a, b, dims="nn"):
    return _dot(a.astype(BF16), b.astype(BF16), dims)


def _hdot(a, b, dims="nn"):
    return _dot(a, b, dims, prec=lax.Precision.HIGHEST)


def _dot3(a, b, dims="nn"):
    return lax.dot_general(a, b, _DN3[dims], preferred_element_type=F32)


def _bdot3(a, b, dims="nn"):
    return _dot3(a.astype(BF16), b.astype(BF16), dims)


def _split(a):
    hi = a.astype(BF16)
    return hi, (a - hi.astype(F32)).astype(BF16)


def _iota(shape, dim):
    return lax.broadcasted_iota(jnp.int32, shape, dim)


def _sigmoid(z):
    return 1.0 / (1.0 + jnp.exp(-z))


def _softplus(z):
    e = jnp.exp(-jnp.abs(z))
    u = 1.0 + e
    l1p = jnp.where(u == 1.0, e, jnp.log(u) * (e / jnp.where(u == 1.0, 1.0, u - 1.0)))
    return jnp.maximum(z, 0.0) + l1p


def _silu_and_grad(z):
    s = _sigmoid(z)
    return z * s, s * (1.0 + z * (1.0 - s))


def _rms(x):
    return lax.rsqrt(jnp.mean(x * x, axis=-1, keepdims=True) + EPS)


def _h_tile(i, x_ref, meta_ref):
    first = jnp.concatenate([jnp.zeros((PAD_ROWS, x_ref.shape[1]), F32), meta_ref[...]], axis=0)
    return jnp.where(i == 0, first, x_ref[...])


def _x_rows(d):
    return pl.BlockSpec((Q_BLOCK, d), lambda i: (jnp.maximum(i - 1, 0), 0))


def _prenorm(x, meta, w):
    seq, d = x.shape
    lp = seq + Q_BLOCK

    def body(x_ref, m_ref, w_ref, o_ref):
        h = _h_tile(pl.program_id(0), x_ref, m_ref)
        o_ref[...] = (h * _rms(h) * w_ref[...]).astype(BF16)

    return pl.pallas_call(
        body, grid=(lp // Q_BLOCK,),
        in_specs=[_x_rows(d), pl.BlockSpec((N_META, d), lambda i: (0, 0)), pl.BlockSpec((1, d), lambda i: (0, 0))],
        out_specs=pl.BlockSpec((Q_BLOCK, d), lambda i: (i, 0)),
        out_shape=jax.ShapeDtypeStruct((lp, d), BF16), name="prenorm", compiler_params=_cp("parallel"))(x, meta, w)


def _prenorm_gathering(x, w, rider, meta_at):
    seq, d = x.shape
    steps = seq // Q_BLOCK + 1

    def body(start, finish, x_ref, w_ref, o_ref, meta_buf, meta_sem):
        i = pl.program_id(0)
        pl.when(i == 0)(start)

        @pl.when(i < steps - 1)
        def _():
            h = x_ref[...]
            o_ref[...] = (h * _rms(h) * w_ref[...]).astype(BF16)

        @pl.when(i == steps - 1)
        def _():
            finish()
            cp = pltpu.make_async_copy(finish.results[0][meta_at], meta_buf, meta_sem)
            cp.start()
            cp.wait()
            h = jnp.concatenate([jnp.zeros((PAD_ROWS, d), F32), jnp.concatenate([meta_buf[s] for s in range(N_DEV)], axis=1)], axis=0)
            o_ref[...] = (h * _rms(h) * w_ref[...]).astype(BF16)

    (xn,), got = _hosted_call(
        body, [rider], 2, 1, 2, grid=(steps,),
        in_specs=[pl.BlockSpec((Q_BLOCK, d), lambda i: (jnp.minimum(i, steps - 2), 0)), pl.BlockSpec((1, d), lambda i: (0, 0))],
        out_specs=[pl.BlockSpec((Q_BLOCK, d), lambda i: ((i + 1) % steps, 0))],
        out_shape=[jax.ShapeDtypeStruct((seq + Q_BLOCK, d), BF16)],
        scratch_shapes=[pltpu.VMEM((N_DEV, N_META, d // N_DEV), F32), pltpu.SemaphoreType.DMA(())],
        name="prenorm", compiler_params=_cp("arbitrary"))(x, w)
    return xn, got[0]


def _tile(n, want):
    return max(t for t in range(LANES, want + 1, LANES) if n % t == 0)


class _Rider:
    def __init__(self, inputs, out_shapes, scratch, aliases, make):
        self.inputs, self.out_shapes, self.scratch, self.aliases, self.make = inputs, out_shapes, scratch, aliases, make


def _hosted_call(body, riders, n_in, n_out, n_scratch, *, in_specs, out_specs, out_shape, scratch_shapes=(), aliases=None,
                 **kw):
    riders = [r for r in riders if r is not None]
    r_in = [len(r.inputs) for r in riders]
    r_out = [len(r.out_shapes) for r in riders]
    r_scr = [len(r.scratch) for r in riders]
    al = dict(aliases or {})
    for k, r in enumerate(riders):
        al.update({n_in + sum(r_in[:k]) + i: n_out + sum(r_out[:k]) + o for i, o in r.aliases.items()})

    def full_body(*refs):
        ins, rest = refs[:n_in + sum(r_in)], refs[n_in + sum(r_in):]
        outs, scr = rest[:n_out + sum(r_out)], rest[n_out + sum(r_out):]
        hooks = [r.make(ins[n_in + sum(r_in[:k]):n_in + sum(r_in[:k + 1])], outs[n_out + sum(r_out[:k]):n_out + sum(r_out[:k + 1])],
                        scr[n_scratch + sum(r_scr[:k]):n_scratch + sum(r_scr[:k + 1])]) for k, r in enumerate(riders)]

        def start():
            for h in hooks:
                h[0]()

        ran_middle = []

        def middle():
            ran_middle.append(True)
            for h in hooks:
                if len(h) == 3:
                    h[1]()

        def finish():
            if not ran_middle:
                middle()
            for h in hooks:
                h[-1]()

        finish.middle = middle
        finish.results = [outs[n_out + sum(r_out[:k]):n_out + sum(r_out[:k + 1])] for k in range(len(riders))]
        body(start, finish, *ins[:n_in], *outs[:n_out], *scr[:n_scratch])

    call = pl.pallas_call(
        full_body, in_specs=list(in_specs) + [_ANY] * sum(r_in), out_specs=list(out_specs) + [_ANY] * sum(r_out),
        out_shape=list(out_shape) + [s for r in riders for s in r.out_shapes],
        scratch_shapes=list(scratch_shapes) + [s for r in riders for s in r.scratch], input_output_aliases=al, **kw)

    def run(*args):
        res = call(*args, *[t for r in riders for t in r.inputs])
        return res[:n_out], [res[n_out + sum(r_out[:k]):n_out + sum(r_out[:k + 1])] for k in range(len(riders))]

    return run


def _matmul(a, b, dims, tn, out_dtype, name, rider=None, a_cols=None, b_rows=None, acc=None):
    a_shape = a.shape if a_cols is None else (a.shape[0], a_cols[0])
    a_index = 0 if a_cols is None else a_cols[1]
    m = a_shape[1] if dims == "tn" else a_shape[0]
    n = b.shape[0] if dims == "nt" else b.shape[1]
    kdim = b.shape[1] if dims == "nt" else b.shape[0] if b_rows is None else b_rows[0]
    b_index = 0 if b_rows is None else b_rows[1]
    tn = _tile(n, tn)
    steps = n // tn
    b_spec = pl.BlockSpec((tn, kdim), lambda j: (j, 0)) if dims == "nt" else pl.BlockSpec((kdim, tn), lambda j: (b_index, j))
    o_spec = pl.BlockSpec((m, tn), lambda j: (0, j))

    def body(start, finish, a_ref, b_ref, *refs):
        pl.when(pl.program_id(0) == 0)(start)
        prod = _dot(a_ref[...], b_ref[...], dims)
        refs[-1][...] = (prod if acc is None else prod + refs[0][...]).astype(out_dtype)
        pl.when(pl.program_id(0) == steps // 2)(finish.middle)
        pl.when(pl.program_id(0) == steps - 1)(finish)

    (out,), got = _hosted_call(
        body, [rider], 2 + (acc is not None), 1, 0, grid=(steps,),
        in_specs=[pl.BlockSpec(a_shape, lambda j: (0, a_index)), b_spec] + [o_spec] * (acc is not None),
        out_specs=[o_spec], out_shape=[jax.ShapeDtypeStruct((m, n), out_dtype)], aliases={2: 0} if acc is not None else None,
        name=name, compiler_params=_cp("parallel" if rider is None else "arbitrary"))(a, b, *([acc] if acc is not None else []))
    return out if rider is None else (out, got[0])


def _chip_rider(sums):
    def make(ins, outs, scratch):
        local, remote = _chip_exchange_copies(ins[0], outs[0], *scratch)

        def start():
            for cp in [local] + remote:
                cp.start()

        def finish():
            local.wait()
            for cp in remote:
                cp.wait_send()
                cp.wait_recv()

        return start, finish

    return _Rider([sums], [jax.ShapeDtypeStruct(sums.shape, sums.dtype)],
                  [pltpu.SemaphoreType.DMA((N_CHIP - 1,)), pltpu.SemaphoreType.DMA((N_CHIP - 1,)), pltpu.SemaphoreType.DMA((1,))],
                  {}, make)


_HBM = pl.BlockSpec(memory_space=pltpu.HBM)
_SEM = pl.BlockSpec(memory_space=pltpu.SEMAPHORE)


def _chip_exchange_copies(sums_ref, land_ref, send_sems, recv_sems, local_sem):
    x, y, core = lax.axis_index("x"), lax.axis_index("y"), lax.axis_index("c")
    mine = 2 * x + y
    local = pltpu.make_async_copy(sums_ref.at[mine], land_ref.at[mine], local_sem.at[0])
    remote = []
    for k in range(1, N_CHIP):
        px = 1 - x if k & 2 else x
        py = 1 - y if k & 1 else y
        remote.append(pltpu.make_async_remote_copy(
            src_ref=sums_ref.at[2 * px + py], dst_ref=land_ref.at[mine], send_sem=send_sems.at[k - 1],
            recv_sem=recv_sems.at[k - 1], device_id=(px, py, core), device_id_type=_MESH))
    return local, remote


def _chip_exchange_start(sums, name):
    def body(s_ref, send_sems, recv_sems, local_sem, s_thru, land_ref, token):
        local, remote = _chip_exchange_copies(s_ref, land_ref, send_sems, recv_sems, local_sem)
        for cp in [local] + remote:
            cp.start()
        token[...] = jnp.zeros_like(token)

    *flight, token = pl.pallas_call(
        body, name=name,
        out_shape=(pltpu.SemaphoreType.DMA((N_CHIP - 1,)), pltpu.SemaphoreType.DMA((N_CHIP - 1,)), pltpu.SemaphoreType.DMA((1,)),
                   pltpu.HBM(sums.shape, sums.dtype), pltpu.HBM(sums.shape, sums.dtype), jax.ShapeDtypeStruct((8, LANES), F32)),
        in_specs=(_HBM,), out_specs=(_SEM, _SEM, _SEM, _HBM, _HBM, pl.BlockSpec(memory_space=pltpu.VMEM)),
        input_output_aliases={0: 3},
        compiler_params=pltpu.CompilerParams(has_side_effects=pltpu.SideEffectType.DATAFLOW_SIDE_EFFECTING))(
            pltpu.with_memory_space_constraint(sums, pltpu.HBM))
    return flight, token


def _chip_exchange_wait(flights, after, name):
    n = len(flights)

    def body(*refs):
        for i in range(n):
            s_ref, land_ref = refs[2 * i:2 * i + 2]
            local, remote = _chip_exchange_copies(s_ref, land_ref, *refs[2 * n + 3 * i:2 * n + 3 * i + 3])
            local.wait()
            for cp in remote:
                cp.wait_send()
                cp.wait_recv()

    buffers = [b for f in flights for b in f[3:]]
    res = pl.pallas_call(
        body, name=name, out_shape=tuple(pltpu.HBM(b.shape, b.dtype) for b in buffers),
        in_specs=(_HBM,) * (2 * n) + (_SEM,) * (3 * n) + (_ANY,), out_specs=(_HBM,) * (2 * n),
        input_output_aliases={i: i for i in range(2 * n)},
        compiler_params=pltpu.CompilerParams(has_side_effects=pltpu.SideEffectType.DATAFLOW_SIDE_EFFECTING))(
            *buffers, *[s for f in flights for s in f[:3]], after)
    return res[1::2]


def _dxn(dproj, wfull, riders, tk, name):
    m, k = dproj.shape
    n = wfull.shape[0]
    tk = _tile(k, tk)
    steps = k // tk

    def body(start, finish, a_ref, b_ref, o_ref):
        j = pl.program_id(0)

        @pl.when(j == 0)
        def _():
            start()
            o_ref[...] = jnp.zeros_like(o_ref)
        o_ref[...] += _dot(a_ref[...], b_ref[...], "nt")
        pl.when(j == steps - 1)(finish)

    (dxn,), got = _hosted_call(
        body, riders, 2, 1, 0, grid=(steps,),
        in_specs=[pl.BlockSpec((m, tk), lambda j: (0, j)), pl.BlockSpec((n, tk), lambda j: (0, j))],
        out_specs=[pl.BlockSpec((m, n), lambda j: (0, 0))], out_shape=[jax.ShapeDtypeStruct((m, n), F32)],
        name=name, compiler_params=_cp("arbitrary"))(dproj, wfull)
    return dxn, got


def _conv_taps(x, w):
    c = x * w[CONV_WIDTH - 1:CONV_WIDTH, :]
    for j in range(CONV_WIDTH - 1):
        c = c + pltpu.roll(x, CONV_WIDTH - 1 - j, 0) * w[j:j + 1, :]
    return c


def _gdn_prep(proj, conv_wt, nh):
    lp = proj.shape[0]
    scale = HEAD_DIM ** -0.5

    def body(x_ref, w_ref, o_ref):
        which = pl.program_id(0) // nh
        c = _conv_taps(x_ref[...], w_ref[...])
        s = c * _sigmoid(c)
        r = lax.rsqrt(jnp.sum(s * s, axis=-1, keepdims=True) + EPS)
        f = jnp.where(which == 0, r * scale, jnp.where(which == 1, r, 1.0))
        o_ref[...] = jnp.where(_iota(s.shape, 0) >= PAD_ROWS, s * f, 0.0)

    return pl.pallas_call(
        body, grid=(3 * nh,),
        in_specs=[pl.BlockSpec((lp, LANES), lambda s: (0, s)), pl.BlockSpec((CONV_WIDTH, LANES), lambda s: (0, s))],
        out_specs=pl.BlockSpec((lp, LANES), lambda s: (0, s)),
        out_shape=jax.ShapeDtypeStruct((lp, 3 * nh * HEAD_DIM), F32), name="gdn_prep",
        compiler_params=_cp("parallel"))(proj, conv_wt)


def _gdn_prep_bwd(proj, conv_wt, dq, dk, dv, dproj, nh):
    lp = proj.shape[0]
    scale = HEAD_DIM ** -0.5
    part = lambda p: pl.BlockSpec((lp, LANES), lambda s: (0, jnp.clip(s - p * nh, 0, nh - 1)))

    def body(x_ref, w_ref, dq_ref, dk_ref, dv_ref, _, dx_ref, dw_ref):
        which = pl.program_id(0) // nh
        x = x_ref[...]
        w = w_ref[...]
        c = _conv_taps(x, w)
        sg = _sigmoid(c)
        s = c * sg
        r = lax.rsqrt(jnp.sum(s * s, axis=-1, keepdims=True) + EPS)
        dy = jnp.where(which == 0, dq_ref[...], jnp.where(which == 1, dk_ref[...], dv_ref[...]))
        dy = jnp.where(_iota(s.shape, 0) >= PAD_ROWS, dy, 0.0)
        y0 = s * r
        dy0 = dy * jnp.where(which == 0, scale, 1.0)
        ds_n = r * (dy0 - y0 * jnp.sum(dy0 * y0, axis=-1, keepdims=True))
        ds = jnp.where(which == 2, dy, ds_n)
        dc = ds * (sg * (1.0 + c * (1.0 - sg)))
        dx = dc * w[CONV_WIDTH - 1:CONV_WIDTH, :]
        rows = [jnp.sum(dc * x, axis=0, keepdims=True)]
        for j in range(CONV_WIDTH - 2, -1, -1):
            sh = CONV_WIDTH - 1 - j
            dx = dx + pltpu.roll(dc, lp - sh, 0) * w[j:j + 1, :]
            rows.insert(0, jnp.sum(dc * pltpu.roll(x, sh, 0), axis=0, keepdims=True))
        dx_ref[...] = dx.astype(BF16)
        dw_ref[...] = jnp.concatenate(rows, axis=0)

    strip = pl.BlockSpec((lp, LANES), lambda s: (0, s))
    taps = pl.BlockSpec((CONV_WIDTH, LANES), lambda s: (0, s))
    return pl.pallas_call(
        body, grid=(3 * nh,), in_specs=[strip, taps, part(0), part(1), part(2), _ANY], out_specs=[strip, taps],
        out_shape=[jax.ShapeDtypeStruct(dproj.shape, BF16), jax.ShapeDtypeStruct((CONV_WIDTH, 3 * nh * HEAD_DIM), F32)],
        input_output_aliases={5: 0}, name="gdn_prep_bwd", compiler_params=_cp("parallel"))(proj, conv_wt, dq, dk, dv, dproj)


def _gates(proj, bias_row, nega_row, nh):
    lp = proj.shape[0]
    nc = lp // CHUNK

    def body(p_ref, b_ref, a_ref, g_ref, gt3_ref, gtf_ref):
        lane = _iota((CHUNK, LANES), 1)
        tri = (_iota((CHUNK, CHUNK), 0) >= _iota((CHUNK, CHUNK), 1)).astype(F32)

        def step(n, carry):
            r0 = pl.multiple_of(n * CHUNK, CHUNK)
            z = p_ref[pl.ds(r0, CHUNK), :] + b_ref[...]
            base = jnp.where(lane < nh, _sigmoid(z),
                             jnp.where(lane < 2 * nh, a_ref[...] * _softplus(z),
                                       jnp.where(lane < 3 * nh, -_softplus(-z), 0.0)))
            base = jnp.where(r0 + _iota((CHUNK, LANES), 0) >= PAD_ROWS, base, 0.0)
            cs = _hdot(tri, base)
            run = jnp.where((lane >= 2 * nh) & (lane < 3 * nh), cs + carry, cs)
            sh = pltpu.roll(run, 2 * nh, 1)
            out = base + jnp.where((lane >= 3 * nh) & (lane < 5 * nh), sh, 0.0)
            g_ref[pl.ds(r0, CHUNK), :] = out
            gt3_ref[n] = out.T
            return carry + cs[CHUNK - 1:CHUNK, :]

        lax.fori_loop(0, nc, step, jnp.zeros((1, LANES), F32))
        gtf_ref[...] = g_ref[...].T

    vec = pl.BlockSpec((1, LANES), lambda i: (0, 0))
    return pl.pallas_call(
        body, grid=(1,), in_specs=[pl.BlockSpec((lp, LANES), lambda i: (0, 8 * nh)), vec, vec],
        out_specs=[pl.BlockSpec((lp, LANES), lambda i: (0, 0)), pl.BlockSpec((nc, LANES, CHUNK), lambda i: (0, 0, 0)),
                   pl.BlockSpec((LANES, lp), lambda i: (0, 0))],
        out_shape=[jax.ShapeDtypeStruct((lp, LANES), F32), jax.ShapeDtypeStruct((nc, LANES, CHUNK), F32),
                   jax.ShapeDtypeStruct((LANES, lp), F32)],
        name="gates", compiler_params=_cp("arbitrary"))(proj, bias_row, nega_row)


def _gates_bwd(proj, bias_row, nega_row, gates, dgate_gdn, dc_t, dproj, nh):
    lp = proj.shape[0]
    nc = lp // CHUNK

    def body(p_ref, b_ref, a_ref, g_ref, dg_ref, dc_ref, _, dz_ref, sm_ref, dct_scr):
        lane = _iota((CHUNK, LANES), 1)
        triu = (_iota((CHUNK, CHUNK), 0) <= _iota((CHUNK, CHUNK), 1)).astype(F32)
        dct_scr[...] = dc_ref[...].T
        sm_ref[...] = jnp.zeros_like(sm_ref)
        dz_ref[:, LANES:] = jnp.zeros((lp, NARROW - LANES), BF16)

        def step(i, carry):
            n = nc - 1 - i
            r0 = pl.multiple_of(n * CHUNK, CHUNK)
            z = p_ref[pl.ds(r0, CHUNK), :] + b_ref[...]
            gt = g_ref[pl.ds(r0, CHUNK), :]
            dgd = dg_ref[pl.ds(r0, CHUNK), :]
            dch = dct_scr[pl.ds(r0, CHUNK), :]
            rc = _hdot(triu, dch) + carry
            sg = _sigmoid(z)
            dz = jnp.where(lane < nh, dgd * sg * (1.0 - sg),
                           jnp.where(lane < 2 * nh, dgd * a_ref[...] * sg,
                                     jnp.where(lane < 3 * nh, rc * (1.0 - sg), 0.0)))
            dz = jnp.where(r0 + _iota((CHUNK, LANES), 0) >= PAD_ROWS, dz, 0.0)
            dz_ref[pl.ds(r0, CHUNK), 0:LANES] = dz.astype(BF16)
            sm_ref[0:1, :] += jnp.sum(dz, axis=0, keepdims=True)
            sm_ref[1:2, :] += jnp.sum(jnp.where((lane >= nh) & (lane < 2 * nh), dgd * gt, 0.0), axis=0, keepdims=True)
            return carry + jnp.sum(dch, axis=0, keepdims=True)

        lax.fori_loop(0, nc, step, jnp.zeros((1, LANES), F32))

    vec = pl.BlockSpec((1, LANES), lambda i: (0, 0))
    full = pl.BlockSpec((lp, LANES), lambda i: (0, 0))
    last = pl.BlockSpec((lp, LANES), lambda i: (0, 8 * nh))
    tail = pl.BlockSpec((lp, NARROW), lambda i: (0, 8 * nh * LANES // NARROW))
    return pl.pallas_call(
        body, grid=(1,), in_specs=[last, vec, vec, full, full, pl.BlockSpec((LANES, lp), lambda i: (0, 0)), _ANY],
        out_specs=[tail, pl.BlockSpec((8, LANES), lambda i: (0, 0))],
        out_shape=[jax.ShapeDtypeStruct(dproj.shape, BF16), jax.ShapeDtypeStruct((8, LANES), F32)],
        scratch_shapes=[pltpu.VMEM((lp, LANES), F32)], input_output_aliases={6: 0},
        name="gates_bwd", compiler_params=_cp("arbitrary"))(proj, bias_row, nega_row, gates, dgate_gdn, dc_t, dproj)


def _tri_inv(a):
    t = jnp.where(_iota(a.shape, 1) == _iota(a.shape, 2), 1.0, 0.0) - a
    p = a
    for _ in range(CHUNK.bit_length() - 2):
        ph, pw = _split(p)
        p = _dot3(ph, ph) + (_dot3(ph, pw) + _dot3(pw, ph))
        ph, pw = _split(p)
        th, tw = _split(t)
        t = t + (_dot3(th, ph) + (_dot3(th, pw) + _dot3(tw, ph)))
    return t


def _gdn_chunk(q, k, v, beta, gc, gr, t=None):
    ii, jj = _iota((1, CHUNK, CHUNK), 1), _iota((1, CHUNK, CHUNK), 2)
    causal, strict = ii >= jj, ii > jj
    dm = jnp.where(causal, jnp.exp(jnp.where(causal, gc - gr, 0.0)), 0.0)
    kk = _bdot3(k, k, "nt")
    a = jnp.where(strict, beta * kk * dm, 0.0)
    if t is None:
        t = _tri_inv(a)
    eg = jnp.exp(gc)
    glast = gc[:, CHUNK - 1:CHUNK, :]
    ekd = jnp.exp(glast - gc)
    bv = beta * v
    bk = (beta * eg) * k
    ub = _bdot3(t, jnp.concatenate([bv, bk], axis=2))
    qk = _bdot3(q, k, "nt")
    return dict(causal=causal, strict=strict, dm=dm, kk=kk, a=a, t=t, eg=eg, ekd=ekd, bv=bv, bk=bk,
                u=ub[:, :, :HEAD_DIM], w=ub[:, :, HEAD_DIM:], qk=qk, aqk=jnp.where(causal, qk * dm, 0.0),
                q_dec=q * eg, k_dec=k * ekd, decay=jnp.exp(glast))


def _heads(ref, nh):
    return jnp.stack([ref[:, h * HEAD_DIM:(h + 1) * HEAD_DIM] for h in range(nh)], axis=0)


def _gdn_chunk_inputs(q_ref, k_ref, v_ref, g, gt, nh):
    col = lambda o: jnp.stack([g[:, o + h:o + h + 1] for h in range(nh)], axis=0)
    gr = jnp.stack([gt[3 * nh + h:3 * nh + h + 1, :] for h in range(nh)], axis=0)
    return _heads(q_ref, nh), _heads(k_ref, nh), _heads(v_ref, nh), col(0), col(3 * nh), gr


def _gdn_fwd(qkv, gates, gt3, nh, rider=None):
    lp = qkv.shape[0]
    nc = lp // CHUNK
    w = nh * HEAD_DIM

    def body(start, finish, q_ref, k_ref, v_ref, g_ref, gt_ref, o_ref, sall_ref, tall_ref, s_scr):
        @pl.when(pl.program_id(0) == 0)
        def _():
            start()
            s_scr[...] = jnp.zeros_like(s_scr)
        c = _gdn_chunk(*_gdn_chunk_inputs(q_ref, k_ref, v_ref, g_ref[...], gt_ref[0], nh))
        s = s_scr[...]
        sall_ref[0] = s
        tall_ref[0] = c["t"]
        v_new = c["u"] - _bdot3(c["w"], s)
        o = _bdot3(c["q_dec"], s) + _bdot3(c["aqk"], v_new)
        s_scr[...] = s * c["decay"] + _bdot3(c["k_dec"], v_new, "tn")
        for h in range(nh):
            o_ref[:, h * HEAD_DIM:(h + 1) * HEAD_DIM] = o[h]
        pl.when(pl.program_id(0) == nc - 1)(finish)

    outs, got = _hosted_call(
        body, [rider], 5, 3, 1, grid=(nc,),
        in_specs=[pl.BlockSpec((CHUNK, w), lambda n: (n, 0)), pl.BlockSpec((CHUNK, w), lambda n: (n, 1)),
                  pl.BlockSpec((CHUNK, w), lambda n: (n, 2)), pl.BlockSpec((CHUNK, LANES), lambda n: (n, 0)),
                  pl.BlockSpec((1, LANES, CHUNK), lambda n: (n, 0, 0))],
        out_specs=[pl.BlockSpec((CHUNK, w), lambda n: (n, 0)),
                   pl.BlockSpec((1, nh, HEAD_DIM, HEAD_DIM), lambda n: (n, 0, 0, 0)),
                   pl.BlockSpec((1, nh, CHUNK, CHUNK), lambda n: (n, 0, 0, 0))],
        out_shape=[jax.ShapeDtypeStruct((lp, w), F32), jax.ShapeDtypeStruct((nc, nh, HEAD_DIM, HEAD_DIM), F32),
                   jax.ShapeDtypeStruct((nc, nh, CHUNK, CHUNK), F32)],
        scratch_shapes=[pltpu.VMEM((nh, HEAD_DIM, HEAD_DIM), F32)],
        name="gdn_fwd", compiler_params=_cp("arbitrary"))(qkv, qkv, qkv, gates, gt3)
    return outs, (got[0] if got else None)


def _gdn_bwd(qkv, gates, gt3, s_all, t_all, do, nh, rider=None):
    lp = qkv.shape[0]
    nc = lp // CHUNK
    w = nh * HEAD_DIM
    rev = lambda n: nc - 1 - n

    def body(start, finish, q_ref, k_ref, v_ref, g_ref, gt_ref, s_ref, t_ref, do_ref, dq_ref, dk_ref, dv_ref, dg_ref, ds_scr):
        @pl.when(pl.program_id(0) == 0)
        def _():
            start()
            ds_scr[...] = jnp.zeros_like(ds_scr)
        q, k, v, beta, gc, gr = _gdn_chunk_inputs(q_ref, k_ref, v_ref, g_ref[...], gt_ref[0], nh)
        c = _gdn_chunk(q, k, v, beta, gc, gr, t_ref[0])
        s = s_ref[0]
        dsn = ds_scr[...]
        dout = _heads(do_ref, nh)
        v_new = c["u"] - _bdot3(c["w"], s)
        dq_dec = _bdot3(dout, s, "nt")
        daqk = jnp.where(c["causal"], _bdot3(dout, v_new, "nt"), 0.0)
        dv_new = _bdot3(c["aqk"], dout, "tn") + _bdot3(c["k_dec"], dsn)
        dk_dec = _bdot3(v_new, dsn, "nt")
        ddecay = jnp.sum(jnp.sum(dsn * s, axis=2, keepdims=True), axis=1, keepdims=True)
        dw = -_bdot3(dv_new, s, "nt")
        ds_scr[...] = _bdot3(c["q_dec"], dout, "tn") + c["decay"] * dsn - _bdot3(c["w"], dv_new, "tn")
        duw = jnp.concatenate([dv_new, dw], axis=2)
        dt = _bdot3(duw, jnp.concatenate([c["bv"], c["bk"]], axis=2), "nt")
        dbvk = _bdot3(c["t"], duw, "tn")
        dbv, dbk = dbvk[:, :, :HEAD_DIM], dbvk[:, :, HEAD_DIM:]
        da = jnp.where(c["strict"], -_bdot3(_bdot3(c["t"], dt, "tn"), c["t"], "nt"), 0.0)
        dkk = da * beta * c["dm"]
        dqk = daqk * c["dm"]
        e = da * c["a"] + daqk * c["aqk"]
        dq = dq_dec * c["eg"] + _bdot3(dqk, k)
        dk = (dk_dec * c["ekd"] + _bdot3(dkk, k) + _bdot3(dkk, k, "tn") + _bdot3(dqk, q, "tn")
              + (beta * c["eg"]) * dbk)
        dv = beta * dbv
        rs = lambda x: jnp.sum(x, axis=2, keepdims=True)
        dbeta = rs(dbv * v) + c["eg"] * rs(dbk * k) + rs(da * c["kk"] * c["dm"])
        kd_term = rs(dk_dec * c["k_dec"])
        eh, ew = _split(e)
        ones = jnp.ones((nh, CHUNK, LANES), BF16)
        col_sums = (_dot3(eh, ones, "tn") + _dot3(ew, ones, "tn"))[:, :, 0:1]
        dg_cum = rs(dq_dec * c["q_dec"]) - kd_term + rs(dbk * c["bk"]) + rs(e) - col_sums
        last = jnp.sum(kd_term, axis=1, keepdims=True) + ddecay * c["decay"]
        dg_cum = dg_cum + jnp.where(_iota((1, CHUNK, 1), 1) == CHUNK - 1, last, 0.0)
        lane = _iota((CHUNK, LANES), 1)
        acc = jnp.zeros((CHUNK, LANES), F32)
        for h in range(nh):
            sl = slice(h * HEAD_DIM, (h + 1) * HEAD_DIM)
            dq_ref[:, sl] = dq[h]
            dk_ref[:, sl] = dk[h]
            dv_ref[:, sl] = dv[h]
            acc = acc + jnp.where(lane == h, dbeta[h], 0.0) + jnp.where(lane == nh + h, dg_cum[h], 0.0)
        triu = (_iota((CHUNK, CHUNK), 0) <= _iota((CHUNK, CHUNK), 1)).astype(F32)
        dg_ref[...] = jnp.where(lane < nh, acc, _hdot(triu, acc))
        pl.when(pl.program_id(0) == nc - 1)(finish)

    outs, got = _hosted_call(
        body, [rider], 8, 4, 1, grid=(nc,),
        in_specs=[pl.BlockSpec((CHUNK, w), lambda n: (rev(n), 0)), pl.BlockSpec((CHUNK, w), lambda n: (rev(n), 1)),
                  pl.BlockSpec((CHUNK, w), lambda n: (rev(n), 2)), pl.BlockSpec((CHUNK, LANES), lambda n: (rev(n), 0)),
                  pl.BlockSpec((1, LANES, CHUNK), lambda n: (rev(n), 0, 0)),
                  pl.BlockSpec((1, nh, HEAD_DIM, HEAD_DIM), lambda n: (rev(n), 0, 0, 0)),
                  pl.BlockSpec((1, nh, CHUNK, CHUNK), lambda n: (rev(n), 0, 0, 0)),
                  pl.BlockSpec((CHUNK, w), lambda n: (rev(n), 0))],
        out_specs=[pl.BlockSpec((CHUNK, w), lambda n: (rev(n), 0))] * 3 + [pl.BlockSpec((CHUNK, LANES), lambda n: (rev(n), 0))],
        out_shape=[jax.ShapeDtypeStruct((lp, w), F32)] * 3 + [jax.ShapeDtypeStruct((lp, LANES), F32)],
        scratch_shapes=[pltpu.VMEM((nh, HEAD_DIM, HEAD_DIM), F32)],
        name="gdn_bwd", compiler_params=_cp("arbitrary"))(qkv, qkv, qkv, gates, gt3, s_all, t_all, do)
    return outs, (got[0] if got else None)


def _merge_gdn(o_gdn, proj, norm_w, nh):
    lp = o_gdn.shape[0]

    def body(o_ref, z_ref, w_ref, m_ref):
        o = o_ref[...]
        z = z_ref[...]
        m_ref[...] = (o * _rms(o) * w_ref[...] * (z * _sigmoid(z))).astype(BF16)

    return pl.pallas_call(
        body, grid=(nh,),
        in_specs=[pl.BlockSpec((lp, LANES), lambda s: (0, s)), pl.BlockSpec((lp, LANES), lambda s: (0, 3 * nh + s)),
                  pl.BlockSpec((1, LANES), lambda s: (0, 0))],
        out_specs=pl.BlockSpec((lp, LANES), lambda s: (0, s)),
        out_shape=jax.ShapeDtypeStruct((lp, 2 * nh * HEAD_DIM), BF16), name="merge_gdn",
        compiler_params=_cp("parallel"))(o_gdn, proj, norm_w)


def _merge_gdn_bwd(o_gdn, proj, norm_w, dmerged, nh):
    lp = o_gdn.shape[0]

    def body(o_ref, z_ref, w_ref, dm_ref, do_ref, dz_ref, dw_ref):
        o = o_ref[...]
        r = _rms(o)
        xh = o * r
        silu, dsilu = _silu_and_grad(z_ref[...])
        dm = dm_ref[...]
        dn = dm * silu
        dz_ref[...] = (dm * (xh * w_ref[...]) * dsilu).astype(BF16)
        dnw = dn * w_ref[...]
        do_ref[...] = r * (dnw - xh * jnp.mean(dnw * xh, axis=-1, keepdims=True))

        @pl.when(pl.program_id(0) == 0)
        def _():
            dw_ref[...] = jnp.zeros_like(dw_ref)
        dw_ref[...] += jnp.sum(dn * xh, axis=0, keepdims=True)

    w = nh * HEAD_DIM
    return pl.pallas_call(
        body, grid=(nh,),
        in_specs=[pl.BlockSpec((lp, LANES), lambda s: (0, s)), pl.BlockSpec((lp, LANES), lambda s: (0, 3 * nh + s)),
                  pl.BlockSpec((1, LANES), lambda s: (0, 0)), pl.BlockSpec((lp, LANES), lambda s: (0, s))],
        out_specs=[pl.BlockSpec((lp, LANES), lambda s: (0, s)), pl.BlockSpec((lp, LANES), lambda s: (0, 3 * nh + s)),
                   pl.BlockSpec((1, LANES), lambda s: (0, 0))],
        out_shape=[jax.ShapeDtypeStruct((lp, w), F32), jax.ShapeDtypeStruct((lp, 8 * w + NARROW), BF16),
                   jax.ShapeDtypeStruct((1, LANES), F32)],
        name="merge_gdn_bwd", compiler_params=_cp("arbitrary"))(o_gdn, proj, norm_w, dmerged)


def _fox_prep(proj, qk_w, nh):
    lp = proj.shape[0]

    def body(x_ref, w_ref, o_ref):
        x = x_ref[...]
        o_ref[...] = x * _rms(x) * w_ref[0]

    return pl.pallas_call(
        body, grid=(2 * nh,),
        in_specs=[pl.BlockSpec((lp, LANES), lambda s: (0, 4 * nh + s)), pl.BlockSpec((1, 1, LANES), lambda s: (s // nh, 0, 0))],
        out_specs=pl.BlockSpec((lp, LANES), lambda s: (0, s)),
        out_shape=jax.ShapeDtypeStruct((lp, 2 * nh * HEAD_DIM), F32), name="fox_prep",
        compiler_params=_cp("parallel"))(proj, qk_w)


def _fox_prep_bwd(proj, qk_w, dq, dk, dproj, nh):
    lp = proj.shape[0]
    part = lambda p: pl.BlockSpec((lp, LANES), lambda s: (0, jnp.clip(s - p * nh, 0, nh - 1)))

    def body(x_ref, w_ref, dq_ref, dk_ref, _, dx_ref, dw_ref):
        x = x_ref[...]
        r = _rms(x)
        xh = x * r
        dy = jnp.where(pl.program_id(0) < nh, dq_ref[...], dk_ref[...])
        dyw = dy * w_ref[0]
        dx_ref[...] = (r * (dyw - xh * jnp.mean(dyw * xh, axis=-1, keepdims=True))).astype(BF16)

        @pl.when(pl.program_id(0) % nh == 0)
        def _():
            dw_ref[...] = jnp.zeros_like(dw_ref)
        dw_ref[0] += jnp.sum(dy * xh, axis=0, keepdims=True)

    strip = pl.BlockSpec((lp, LANES), lambda s: (0, 4 * nh + s))
    wsp = pl.BlockSpec((1, 1, LANES), lambda s: (s // nh, 0, 0))
    return pl.pallas_call(
        body, grid=(2 * nh,), in_specs=[strip, wsp, part(0), part(1), _ANY], out_specs=[strip, wsp],
        out_shape=[jax.ShapeDtypeStruct(dproj.shape, BF16), jax.ShapeDtypeStruct((2, 1, LANES), F32)],
        input_output_aliases={4: 0}, name="fox_prep_bwd", compiler_params=_cp("arbitrary"))(proj, qk_w, dq, dk, dproj)


def _fox_probs(q, k, gates, crow, h, i, nh, lse=None):
    kl = k.shape[0]
    lane = _iota((Q_BLOCK, LANES), 1)
    ct = jnp.sum(jnp.where(lane == 4 * nh + h, gates, 0.0), axis=1, keepdims=True)
    tq, kq = _iota((Q_BLOCK, Q_BLOCK), 0), _iota((Q_BLOCK, Q_BLOCK), 1)
    qs = q * (HEAD_DIM ** -0.5)
    if i == 0:
        s = _bdot(qs, k, "nt") + (ct - crow)
        s = jnp.where((kq <= tq) & ((kq >= PAD_ROWS) | (tq < PAD_ROWS)), s, NEG)
    else:
        crow = jnp.where(_iota((1, kl), 1) < PAD_ROWS, -NEG, crow)
        s = _bdot(qs, k, "nt") + (ct - crow)
        s = jnp.concatenate([s[:, :kl - Q_BLOCK], jnp.where(kq <= tq, s[:, kl - Q_BLOCK:], NEG)], axis=1)
    if lse is not None:
        return jnp.exp(s - lse)
    m = jnp.max(s, axis=1, keepdims=True)
    p = jnp.exp(s - m)
    tot = jnp.sum(p, axis=1, keepdims=True)
    return p / tot, m + jnp.log(tot)


FOX_HEADS_PER_STEP = 2


def _fox_specs(lp, nh):
    hw = FOX_HEADS_PER_STEP * LANES
    return [pl.BlockSpec((Q_BLOCK, hw), lambda g, i: (i, g)),
            pl.BlockSpec((lp, hw), lambda g, i: (0, nh // FOX_HEADS_PER_STEP + g)),
            pl.BlockSpec((lp, hw), lambda g, i: (0, 6 * nh // FOX_HEADS_PER_STEP + g)),
            pl.BlockSpec((Q_BLOCK, LANES), lambda g, i: (i, 0)),
            pl.BlockSpec((LANES, lp), lambda g, i: (0, 0))]


def _fox_fwd(qkn, proj, gates, gtf, nh):
    lp = qkn.shape[0]

    def body(q_ref, k_ref, v_ref, g_ref, gt_ref, o_ref, lse_ref):
        g, i = pl.program_id(0), pl.program_id(1)
        for j in range(lp // Q_BLOCK):
            @pl.when(i == j)
            def _(j=j):
                kl = (j + 1) * Q_BLOCK
                for hh in range(FOX_HEADS_PER_STEP):
                    h = FOX_HEADS_PER_STEP * g + hh
                    sl = slice(hh * LANES, (hh + 1) * LANES)
                    p, lse = _fox_probs(q_ref[:, sl], k_ref[0:kl, sl], g_ref[...], gt_ref[pl.ds(4 * nh + h, 1), :][:, 0:kl],
                                        h, j, nh)
                    o_ref[:, sl] = _bdot(p, v_ref[0:kl, sl])
                    lse_ref[:, sl] = jnp.broadcast_to(lse, (Q_BLOCK, LANES))

    blk = pl.BlockSpec((Q_BLOCK, FOX_HEADS_PER_STEP * LANES), lambda g, i: (i, g))
    return pl.pallas_call(
        body, grid=(nh // FOX_HEADS_PER_STEP, lp // Q_BLOCK), in_specs=_fox_specs(lp, nh), out_specs=[blk, blk],
        out_shape=[jax.ShapeDtypeStruct((lp, nh * HEAD_DIM), F32)] * 2, name="fox_fwd",
        compiler_params=_cp("parallel", "parallel"))(qkn, qkn, proj, gates, gtf)


def _fox_bwd(qkn, proj, gates, gtf, lse, do, dproj, nh):
    lp = qkn.shape[0]
    nq = lp // Q_BLOCK
    w = nh * HEAD_DIM
    scale = HEAD_DIM ** -0.5

    def body(q_ref, k_ref, v_ref, g_ref, gt_ref, lse_ref, do_ref, _, dq_ref, dk_ref, dc_ref, dv_ref, dv_scr):
        g, i = pl.program_id(0), pl.program_id(1)

        @pl.when(i == 0)
        def _():
            dk_ref[...] = jnp.zeros_like(dk_ref)
            dv_scr[...] = jnp.zeros_like(dv_scr)
            dc_ref[...] = jnp.zeros_like(dc_ref)
        for j in range(nq):
            @pl.when(i == j)
            def _(j=j):
                kl = (j + 1) * Q_BLOCK
                for hh in range(FOX_HEADS_PER_STEP):
                    h = FOX_HEADS_PER_STEP * g + hh
                    sl = slice(hh * LANES, (hh + 1) * LANES)
                    q, k = q_ref[:, sl], k_ref[0:kl, sl]
                    p = _fox_probs(q, k, g_ref[...], gt_ref[pl.ds(4 * nh + h, 1), :][:, 0:kl], h, j, nh,
                                   lse_ref[:, sl][:, 0:1])
                    dout = do_ref[:, sl]
                    dp = _bdot(dout, v_ref[0:kl, sl], "nt")
                    ds = p * (dp - jnp.sum(p * dp, axis=1, keepdims=True))
                    dq_ref[:, sl] = _bdot(ds, k) * scale
                    dk_ref[0:kl, sl] += _bdot(ds, q * scale, "tn")
                    dv_scr[0:kl, sl] += _bdot(p, dout, "tn")
                    dc_ref[hh, :, 0:kl] -= jnp.sum(ds, axis=0, keepdims=True)

        @pl.when(i == nq - 1)
        def _():
            dv_ref[...] = dv_scr[...].astype(BF16)

    hw = FOX_HEADS_PER_STEP * LANES
    blk = pl.BlockSpec((Q_BLOCK, hw), lambda g, i: (i, g))
    col = pl.BlockSpec((lp, hw), lambda g, i: (0, g))
    return pl.pallas_call(
        body, grid=(nh // FOX_HEADS_PER_STEP, nq), in_specs=_fox_specs(lp, nh) + [blk, blk, _ANY],
        out_specs=[blk, col, pl.BlockSpec((FOX_HEADS_PER_STEP, 1, lp), lambda g, i: (g, 0, 0)),
                   pl.BlockSpec((lp, hw), lambda g, i: (0, 6 * nh // FOX_HEADS_PER_STEP + g))],
        out_shape=[jax.ShapeDtypeStruct((lp, w), F32)] * 2 + [jax.ShapeDtypeStruct((nh, 1, lp), F32),
                                                             jax.ShapeDtypeStruct(dproj.shape, BF16)],
        scratch_shapes=[pltpu.VMEM((lp, hw), F32)], input_output_aliases={7: 3},
        name="fox_bwd", compiler_params=_cp("parallel", "arbitrary"))(qkn, qkn, proj, gates, gtf, lse, do, dproj)


def _merge_fox(o_fox, proj, merged, nh):
    lp = o_fox.shape[0]

    def body(o_ref, z_ref, _, m_ref):
        z = z_ref[...]
        m_ref[...] = (o_ref[...] * (z * _sigmoid(z))).astype(BF16)

    return pl.pallas_call(
        body, grid=(nh,),
        in_specs=[pl.BlockSpec((lp, LANES), lambda s: (0, s)), pl.BlockSpec((lp, LANES), lambda s: (0, 7 * nh + s)), _ANY],
        out_specs=pl.BlockSpec((lp, LANES), lambda s: (0, nh + s)),
        out_shape=jax.ShapeDtypeStruct(merged.shape, BF16), input_output_aliases={2: 0}, name="merge_fox",
        compiler_params=_cp("parallel"))(o_fox, proj, merged)


def _merge_fox_bwd(o_fox, proj, dmerged, dproj, nh):
    lp = o_fox.shape[0]

    def body(o_ref, z_ref, dm_ref, _, do_ref, dz_ref):
        silu, dsilu = _silu_and_grad(z_ref[...])
        dm = dm_ref[...]
        do_ref[...] = dm * silu
        dz_ref[...] = (dm * o_ref[...] * dsilu).astype(BF16)

    w = nh * HEAD_DIM
    return pl.pallas_call(
        body, grid=(nh,),
        in_specs=[pl.BlockSpec((lp, LANES), lambda s: (0, s)), pl.BlockSpec((lp, LANES), lambda s: (0, 7 * nh + s)),
                  pl.BlockSpec((lp, LANES), lambda s: (0, nh + s)), _ANY],
        out_specs=[pl.BlockSpec((lp, LANES), lambda s: (0, s)), pl.BlockSpec((lp, LANES), lambda s: (0, 7 * nh + s))],
        out_shape=[jax.ShapeDtypeStruct((lp, w), F32), jax.ShapeDtypeStruct(dproj.shape, BF16)],
        input_output_aliases={3: 1}, name="merge_fox_bwd", compiler_params=_cp("parallel"))(o_fox, proj, dmerged, dproj)


def _post(out, x, target, post_w):
    lp, d = out.shape

    def body(o_ref, x_ref, t_ref, w_ref, dy_ref, do_ref, loss_ref, dw_ref):
        i = pl.program_id(0)

        @pl.when(i == 0)
        def _():
            loss_ref[...] = jnp.zeros_like(loss_ref)
            dw_ref[...] = jnp.zeros_like(dw_ref)
        o = o_ref[...]
        r = _rms(o)
        nrm = o * r
        err = jnp.where(i > 0, x_ref[...] + nrm * w_ref[...] - t_ref[...], 0.0)
        loss_ref[0:1, :] += 0.5 * jnp.sum(jnp.sum(err * err, axis=1, keepdims=True), axis=0, keepdims=True) / d
        dy = err / d
        dy_ref[...] = dy
        dw_ref[...] += jnp.sum(dy * nrm, axis=0, keepdims=True)
        dyw = dy * w_ref[...]
        do_ref[...] = (r * (dyw - nrm * jnp.mean(dyw * nrm, axis=-1, keepdims=True))).astype(BF16)

    row = pl.BlockSpec((Q_BLOCK, d), lambda i: (i, 0))
    vec = pl.BlockSpec((1, d), lambda i: (0, 0))
    return pl.pallas_call(
        body, grid=(lp // Q_BLOCK,), in_specs=[row, _x_rows(d), _x_rows(d), vec],
        out_specs=[_x_rows(d), row, pl.BlockSpec((8, LANES), lambda i: (0, 0)), vec],
        out_shape=[jax.ShapeDtypeStruct(x.shape, F32), jax.ShapeDtypeStruct((lp, d), BF16),
                   jax.ShapeDtypeStruct((8, LANES), F32), jax.ShapeDtypeStruct((1, d), F32)],
        name="post", compiler_params=_cp("arbitrary"))(out, x, target, post_w)


def _prenorm_bwd(dxn, x, meta, w, dy, rider=None):
    seq, d = x.shape
    lp = seq + Q_BLOCK

    def body(start, finish, dx_ref, x_ref, m_ref, w_ref, dy_ref, gx_ref, gm_ref, dw_ref):
        i = pl.program_id(0)
        pl.when(i == 0)(start)
        h = _h_tile(i, x_ref, m_ref)
        r = _rms(h)
        xh = h * r
        dxn_ = dx_ref[...]
        dxw = dxn_ * w_ref[...]
        dh = jnp.where(i > 0, dy_ref[...], 0.0) + r * (dxw - xh * jnp.mean(dxw * xh, axis=-1, keepdims=True))
        gx_ref[...] = dh

        @pl.when(i == 0)
        def _():
            dw_ref[...] = jnp.zeros_like(dw_ref)
            gm_ref[...] = dh[PAD_ROWS:, :]
        dw_ref[...] += jnp.sum(dxn_ * xh, axis=0, keepdims=True)
        pl.when(i == lp // Q_BLOCK - 1)(finish)

    vec = pl.BlockSpec((1, d), lambda i: (0, 0))
    met = pl.BlockSpec((N_META, d), lambda i: (0, 0))
    outs, got = _hosted_call(
        body, [rider], 5, 3, 0, grid=(lp // Q_BLOCK,),
        in_specs=[pl.BlockSpec((Q_BLOCK, d), lambda i: (i, 0)), _x_rows(d), met, vec, _x_rows(d)],
        out_specs=[_x_rows(d), met, vec],
        out_shape=[jax.ShapeDtypeStruct((seq, d), F32), jax.ShapeDtypeStruct((N_META, d), F32),
                   jax.ShapeDtypeStruct((1, d), F32)],
        name="prenorm_bwd", compiler_params=_cp("arbitrary"))(dxn, x, meta, w, dy)
    return outs, (got[0] if got else None)


def _layer_grads(x, target, meta, pre_w, wfull, conv_wt, a_log, dt_bias, gdn_norm_w, fq_w, fk_w, f_bias, w_out, post_w,
                 w_out_grads=None):
    nh = a_log.shape[1]
    zpad = jnp.zeros((1, LANES - 3 * nh), F32)
    bias_row = jnp.concatenate([jnp.zeros((1, nh), F32), dt_bias, f_bias, zpad], axis=1)
    nega_row = jnp.concatenate([jnp.zeros((1, nh), F32), -jnp.exp(a_log), jnp.zeros((1, nh), F32), zpad], axis=1)
    qk_w = jnp.stack([fq_w, fk_w])

    if isinstance(wfull, tuple):
        rider, meta_at, project = wfull
        xn, got = _prenorm_gathering(x, pre_w, rider, meta_at)
        proj, wfull, conv_wt, meta, w_out = project(xn, got)
    else:
        xn = _prenorm(x, meta, pre_w)
        proj = _matmul(xn, wfull, "nn", MM_TILE, F32, "proj")
    qkv = _gdn_prep(proj, conv_wt, nh)
    gates, gt3, gtf = _gates(proj, bias_row, nega_row, nh)
    (o_gdn, s_all, t_all), _ = _gdn_fwd(qkv, gates, gt3, nh, None)
    qkn = _fox_prep(proj, qk_w, nh)
    o_fox, fox_lse = _fox_fwd(qkn, proj, gates, gtf, nh)
    merged = _merge_fox(o_fox, proj, _merge_gdn(o_gdn, proj, gdn_norm_w, nh), nh)
    out = _matmul(merged, w_out, "nn", 4 * LANES, F32, "out_proj")
    dy, dout, loss_blk, dpost_w = _post(out, x, target, post_w)

    dw_out = _matmul(merged, dout, "tn", 4 * LANES, BF16, "dw_out")
    if w_out_grads is None:
        dmerged, gdn_rider = _matmul(dout, w_out, "nt", 4 * LANES, F32, "dmerged"), None
    else:
        dmerged, got = _matmul(dout, w_out, "nt", 4 * LANES, F32, "dmerged", w_out_grads[0](dw_out))
        gdn_rider = w_out_grads[1](dw_out, got)
    do_gdn, dproj, dgdn_norm_w = _merge_gdn_bwd(o_gdn, proj, gdn_norm_w, dmerged, nh)
    do_fox, dproj = _merge_fox_bwd(o_fox, proj, dmerged, dproj, nh)
    dqn, dkn, dc_t, dproj = _fox_bwd(qkn, proj, gates, gtf, fox_lse, do_fox, dproj, nh)
    dproj, dqk_w = _fox_prep_bwd(proj, qk_w, dqn, dkn, dproj, nh)
    (dgq, dgk, dgv, dgate), w_out_parts = _gdn_bwd(qkv, gates, gt3, s_all, t_all, do_gdn, nh, gdn_rider)
    dproj, dconv_wt = _gdn_prep_bwd(proj, conv_wt, dgq, dgk, dgv, dproj, nh)
    dc_rows = jnp.pad(dc_t.reshape(nh, -1), ((2 * nh, LANES - 3 * nh), (0, 0)))
    dproj, gate_sums = _gates_bwd(proj, bias_row, nega_row, gates, dgate, dc_rows, dproj, nh)
    return dict(
        loss=loss_blk[0:1, 0:1], dy=dy, xn=xn, dproj=dproj, post_w=dpost_w,
        conv_wt=dconv_wt, a_log=gate_sums[1:2, nh:2 * nh], dt_bias=gate_sums[0:1, nh:2 * nh],
        gdn_norm_w=dgdn_norm_w, fq_w=dqk_w[0], fk_w=dqk_w[1], f_bias=gate_sums[0:1, 2 * nh:3 * nh], w_out=dw_out,
        w_out_parts=w_out_parts, wfull=wfull, meta=meta)


def _cast_bf16(a, tr, name):
    r, c = a.shape

    def body(a_ref, o_ref):
        o_ref[...] = a_ref[...].astype(BF16)

    return pl.pallas_call(
        body, grid=(r // tr,), in_specs=[pl.BlockSpec((tr, c), lambda i: (i, 0))],
        out_specs=pl.BlockSpec((tr, c), lambda i: (i, 0)), out_shape=jax.ShapeDtypeStruct((r, c), BF16),
        name=name, compiler_params=_cp("parallel"))(a)


def _column_major(a):
    return jnp.transpose(a, (2, 0, 1))


def _cast_bf16_column_major(a3, pieces, name):
    _, r, c = a3.shape
    rows = r // pieces

    def body(a_ref, *o_refs):
        t = a_ref[...].reshape(LANES, r).T.astype(BF16)
        for k, o_ref in enumerate(o_refs):
            o_ref[...] = t[k * rows:(k + 1) * rows]

    return pl.pallas_call(
        body, grid=(pl.cdiv(c, LANES),), in_specs=[pl.BlockSpec((LANES, 1, r), lambda i: (i, 0, 0))],
        out_specs=[pl.BlockSpec((rows, LANES), lambda i: (0, i))] * pieces,
        out_shape=[jax.ShapeDtypeStruct((rows, c), BF16)] * pieces, name=name, compiler_params=_cp("parallel"))(_column_major(a3))


def _adamw_column_major(w3, parts, m3, v3, name):
    _, r, c = w3.shape
    n_parts = parts[0].shape[0]

    def body(w_ref, *refs):
        p_refs, (m_ref, v_ref, g_ref, d_ref, nm_ref, nv_ref) = refs[:len(parts)], refs[len(parts):]
        sums = []
        for p_ref in p_refs:
            g = p_ref[0].astype(F32)
            for s in range(1, n_parts):
                g = g + p_ref[s].astype(F32)
            sums.append(g)
        g = jnp.concatenate(sums, axis=0).T
        flat = lambda ref: ref[...].reshape(LANES, r)
        m_new = ADAM_B1 * flat(m_ref) + (1.0 - ADAM_B1) * g
        v_new = ADAM_B2 * flat(v_ref) + (1.0 - ADAM_B2) * (g * g)
        m_hat = m_new / (1.0 - ADAM_B1 ** ADAM_STEP)
        v_hat = v_new / (1.0 - ADAM_B2 ** ADAM_STEP)
        delta = -ADAM_LR * (m_hat / (jnp.sqrt(v_hat) + ADAM_EPS) + ADAM_WD * flat(w_ref))
        for ref, val in ((g_ref, g), (d_ref, delta), (nm_ref, m_new), (nv_ref, v_new)):
            ref[...] = val.reshape(LANES, 1, r)

    blk = pl.BlockSpec((LANES, 1, r), lambda i: (i, 0, 0))
    outs = pl.pallas_call(
        body, grid=(pl.cdiv(c, LANES),),
        in_specs=[blk] + [pl.BlockSpec((n_parts, p.shape[1], LANES), lambda i: (0, 0, i)) for p in parts] + [blk, blk],
        out_specs=[blk] * 4, out_shape=[jax.ShapeDtypeStruct((c, 1, r), F32)] * 4, name=name,
        compiler_params=_cp("parallel"))(_column_major(w3), *parts, _column_major(m3), _column_major(v3))
    return [jnp.transpose(o, (1, 2, 0)) for o in outs]


def _gather_copies(ins, outs, send_sems, recv_sems, local_sems):
    n = len(ins)
    x, y, c = lax.axis_index("x"), lax.axis_index("y"), lax.axis_index("c")
    me, sibling = (x, y, c), (x, y, 1 - c)
    xn, yn, dg = (1 - x, y), (x, 1 - y), (1 - x, 1 - y)

    def copy(a, k, block, to, src=None):
        px, py, pc = block
        rows = outs[a].at[4 * px + 2 * py + pc]
        return pltpu.make_async_remote_copy(
            src_ref=rows if src is None else src, dst_ref=rows, send_sem=send_sems.at[a, k],
            recv_sem=recv_sems.at[a, k], device_id=to, device_id_type=_MESH)

    local = [pltpu.make_async_copy(ins[a], outs[a].at[4 * x + 2 * y + c], local_sems.at[a]) for a in range(n)]
    own = [cp for a in range(n) for cp in (copy(a, 0, me, sibling, src=ins[a]), copy(a, 1, me, (*xn, c), src=ins[a]),
                                           copy(a, 2, me, (*yn, c), src=ins[a]))]

    def start():
        for cp in local + own:
            cp.start()

    def middle():
        for a in range(n):
            @pl.when(c == 1)
            def _(a=a):
                copy(a, 1, (*xn, c), me).wait_recv()
                copy(a, 3, (*xn, c), (*yn, c)).start()

            @pl.when(c == 0)
            def _(a=a):
                copy(a, 2, (*yn, c), me).wait_recv()
                copy(a, 3, (*yn, c), (*xn, c)).start()
        for a in range(n):
            pl.when(c == 0)(copy(a, 1, (*xn, c), me).wait_recv)
            copy(a, 4, (*xn, c), sibling).start()
            pl.when(c == 1)(copy(a, 2, (*yn, c), me).wait_recv)
            copy(a, 5, (*yn, c), sibling).start()

    def finish():
        for a in range(n):
            copy(a, 3, (*dg, c), me).wait_recv()
            copy(a, 6, (*dg, c), sibling).start()
        for a in range(n):
            copy(a, 0, sibling, me).wait_recv()
            for k, chip in ((4, xn), (5, yn), (6, dg)):
                copy(a, k, (*chip, 1 - c), me).wait_recv()
                copy(a, k, (*chip, c), sibling).wait_send()
            copy(a, 3, (*xn, c), (*yn, c)).wait_send()
        for cp in own:
            cp.wait_send()
        for cp in local:
            cp.wait()

    return start, middle, finish


def _gather_scratch(n):
    return [pltpu.SemaphoreType.DMA((n, N_DEV - 1)), pltpu.SemaphoreType.DMA((n, N_DEV - 1)), pltpu.SemaphoreType.DMA((n,))]


def _gather_rider(arrays):
    return _Rider(list(arrays), [jax.ShapeDtypeStruct((N_DEV,) + a.shape, a.dtype) for a in arrays],
                  _gather_scratch(len(arrays)), {}, lambda ins, outs, scratch: _gather_copies(ins, outs, *scratch))


def _all_gather(arrays, name):
    n = len(arrays)

    def body(*refs):
        start, middle, finish = _gather_copies(refs[:n], refs[n:2 * n], *refs[2 * n:])
        start()
        middle()
        finish()

    return pl.pallas_call(
        body, in_specs=[_ANY] * n, out_specs=[_ANY] * n,
        out_shape=[jax.ShapeDtypeStruct((N_DEV,) + a.shape, a.dtype) for a in arrays],
        scratch_shapes=_gather_scratch(n), name=name)(*arrays)


SLAB = 10 * LANES


def _slab_start(blk, nh, cols):
    in_second_half = blk >= N_DEV // 2
    shift = (2 * nh if in_second_half else 0) if isinstance(blk, int) else jnp.where(in_second_half, 2 * nh, 0)
    return (blk * cols - shift) // LANES * LANES


def _pair_rider(dw_rows=None, parts=None, nh=None, after=None):
    if dw_rows is not None:
        r, full = dw_rows.shape
        cols = (full - NARROW + 3 * nh) // N_DEV
        out_shapes = [jax.ShapeDtypeStruct((N_CHIP, r, SLAB), dw_rows.dtype), jax.ShapeDtypeStruct((r, NARROW), dw_rows.dtype)]
    else:
        out_shapes = [jax.ShapeDtypeStruct((N_CHIP,) + parts.shape[1:], parts.dtype)]

    def make(ins, outs, scratch):
        send_sems, recv_sems = scratch
        x, y, c = lax.axis_index("x"), lax.axis_index("y"), lax.axis_index("c")
        kw = lambda k: dict(send_sem=send_sems.at[k], recv_sem=recv_sems.at[k], device_id=(x, y, 1 - c), device_id_type=_MESH)
        copies = []
        for q in range(N_CHIP):
            if dw_rows is not None:
                first = pl.multiple_of(_slab_start(2 * q + 1 - c, nh, cols), LANES)
                copies.append(pltpu.make_async_remote_copy(src_ref=ins[0].at[:, pl.ds(first, SLAB)], dst_ref=outs[0].at[q], **kw(q)))
            else:
                copies.append(pltpu.make_async_remote_copy(src_ref=ins[0].at[2 * q + 1 - c], dst_ref=outs[0].at[q], **kw(q)))
        if dw_rows is not None:
            copies.append(pltpu.make_async_remote_copy(src_ref=ins[0].at[:, pl.ds(full - NARROW, NARROW)], dst_ref=outs[1],
                                                       **kw(N_CHIP)))

        def start():
            for cp in copies:
                cp.start()

        def finish():
            for cp in copies:
                cp.wait()

        return start, finish

    return _Rider([dw_rows if dw_rows is not None else parts] + ([] if after is None else [after]), out_shapes,
                  [pltpu.SemaphoreType.DMA((N_CHIP + 1,)), pltpu.SemaphoreType.DMA((N_CHIP + 1,))], {}, make)


def _relayout_pair_sum(dwfull, got_slabs, got_tail, core, nh, tr, name):
    d, full = dwfull.shape
    w = nh * HEAD_DIM
    cols = (8 * w + 3 * nh) // N_DEV
    segs = _native_segments(nh)

    def block(f_ref, s_ref, t_ref, q, blk):
        st = _slab_start(blk, nh, cols)
        wide = f_ref[:, st:st + SLAB].astype(F32) + s_ref[q].astype(F32)
        tail = f_ref[:, 8 * w:].astype(F32) + t_ref[...].astype(F32)
        pieces = []
        for s0, s1, t0 in segs:
            lo, hi = max(s0, blk * cols), min(s1, (blk + 1) * cols)
            if lo < hi:
                at = t0 + lo - s0
                pieces.append(tail[:, at - 8 * w:at - 8 * w + hi - lo] if at >= 8 * w else wide[:, at - st:at - st + hi - lo])
        return (pieces[0] if len(pieces) == 1 else jnp.concatenate(pieces, axis=1)).astype(dwfull.dtype)

    def body(core_ref, f_ref, s_ref, t_ref, o_ref):
        for parity in range(2):
            @pl.when(core_ref[0] == parity)
            def _(parity=parity):
                for q in range(N_CHIP):
                    o_ref[q] = block(f_ref, s_ref, t_ref, q, 2 * q + parity)

    return pl.pallas_call(
        body,
        grid_spec=pltpu.PrefetchScalarGridSpec(
            num_scalar_prefetch=1, grid=(d // tr,),
            in_specs=[pl.BlockSpec((tr, full), lambda i, c_ref: (i, 0)), pl.BlockSpec((N_CHIP, tr, SLAB), lambda i, c_ref: (0, i, 0)),
                      pl.BlockSpec((tr, NARROW), lambda i, c_ref: (i, 0))],
            out_specs=pl.BlockSpec((N_CHIP, tr, cols), lambda i, c_ref: (0, i, 0))),
        out_shape=jax.ShapeDtypeStruct((N_CHIP, d, cols), dwfull.dtype), name=name,
        compiler_params=_cp("parallel"))(core, dwfull, got_slabs, got_tail)


def _pair_sum(parts, got, core, tr, name):
    _, r, c = parts.shape

    def body(core_ref, p_ref, g_ref, o_ref):
        o_ref[...] = (p_ref[...].astype(F32) + g_ref[...].astype(F32)).astype(o_ref.dtype)

    return pl.pallas_call(
        body,
        grid_spec=pltpu.PrefetchScalarGridSpec(
            num_scalar_prefetch=1, grid=(N_CHIP, r // tr),
            in_specs=[pl.BlockSpec((1, tr, c), lambda q, i, core_ref: (2 * q + core_ref[0], i, 0)),
                      pl.BlockSpec((1, tr, c), lambda q, i, core_ref: (q, i, 0))],
            out_specs=pl.BlockSpec((1, tr, c), lambda q, i, core_ref: (q, i, 0))),
        out_shape=jax.ShapeDtypeStruct((N_CHIP, r, c), parts.dtype), name=name,
        compiler_params=_cp("parallel", "parallel"))(core, parts, got)


def _native_segments(nh):
    w = nh * HEAD_DIM
    return [(0, 4 * w, 0), (4 * w, 4 * w + 2 * nh, 8 * w), (4 * w + 2 * nh, 8 * w + 2 * nh, 4 * w),
            (8 * w + 2 * nh, 8 * w + 3 * nh, 8 * w + 2 * nh)]


def _relayout_w_in(wg, nh, tr, name, rows_total=None, into=None):
    _, d, cols = wg.shape
    w = nh * HEAD_DIM
    rows_total = into.shape[0] if into is not None else rows_total or d
    first = (rows_total - d) // tr if into is not None else 0

    def native(ref, j0, j1):
        out = []
        while j0 < j1:
            blk = j0 // cols
            end = min(j1, (blk + 1) * cols)
            out.append(ref[blk, :, pl.ds(j0 - blk * cols, end - j0)])
            j0 = end
        return out

    def body(g_ref, *refs):
        o_ref = refs[-1]
        for cidx in range(8 * w // LANES):
            j0 = cidx * LANES + (0 if cidx * LANES < 4 * w else 2 * nh)
            pieces = native(g_ref, j0, j0 + LANES)
            o_ref[:, cidx * LANES:(cidx + 1) * LANES] = pieces[0] if len(pieces) == 1 else jnp.concatenate(pieces, axis=1)
        pieces = (native(g_ref, 4 * w, 4 * w + 2 * nh) + native(g_ref, 8 * w + 2 * nh, 8 * w + 3 * nh)
                  + [jnp.zeros((tr, NARROW - 3 * nh), wg.dtype)])
        o_ref[:, 8 * w:] = jnp.concatenate(pieces, axis=1)

    return pl.pallas_call(
        body, grid=(d // tr,), in_specs=[pl.BlockSpec((N_DEV, tr, cols), lambda i: (0, i, 0))] + [_ANY] * (into is not None),
        out_specs=pl.BlockSpec((tr, 8 * w + NARROW), lambda i: (first + i, 0)),
        out_shape=jax.ShapeDtypeStruct((rows_total, 8 * w + NARROW), wg.dtype),
        input_output_aliases={1: 0} if into is not None else {},
        name=name, compiler_params=_cp("parallel"))(wg, *([into] if into is not None else []))


def _adamw(w, parts, m, v, tr, name, after=None):
    r, c = w.shape
    n_parts = parts.shape[0]

    def body(w_ref, p_ref, m_ref, v_ref, *refs):
        g_ref, d_ref, nm_ref, nv_ref = refs[-5:-1] if after is not None else refs
        if after is not None:
            refs[-1][...] = jnp.zeros_like(refs[-1])
        g = p_ref[0].astype(F32)
        for s in range(1, n_parts):
            g = g + p_ref[s].astype(F32)
        m_new = ADAM_B1 * m_ref[...] + (1.0 - ADAM_B1) * g
        v_new = ADAM_B2 * v_ref[...] + (1.0 - ADAM_B2) * (g * g)
        m_hat = m_new / (1.0 - ADAM_B1 ** ADAM_STEP)
        v_hat = v_new / (1.0 - ADAM_B2 ** ADAM_STEP)
        g_ref[...] = g
        d_ref[...] = -ADAM_LR * (m_hat / (jnp.sqrt(v_hat) + ADAM_EPS) + ADAM_WD * w_ref[...])
        nm_ref[...] = m_new
        nv_ref[...] = v_new

    blk = pl.BlockSpec((tr, c), lambda i: (i, 0))
    extra = after is not None
    return pl.pallas_call(
        body, grid=(r // tr,),
        in_specs=[blk, pl.BlockSpec((n_parts, tr, c), lambda i: (0, i, 0)), blk, blk] + [_ANY] * extra,
        out_specs=[blk] * 4 + [pl.BlockSpec((8, LANES), lambda i: (0, 0))] * extra,
        out_shape=[jax.ShapeDtypeStruct((r, c), F32)] * 4 + [jax.ShapeDtypeStruct((8, LANES), F32)] * extra, name=name,
        compiler_params=_cp("arbitrary" if extra else "parallel"))(w, parts, m, v, *([after] if extra else []))


def _adamw_conv(w, gathered, dev, m, v, name):
    c, r = w.shape
    n_parts = gathered.shape[0]

    def body(dev_ref, w_ref, p_ref, m_ref, v_ref, g_ref, d_ref, nm_ref, nv_ref):
        g = p_ref[0].astype(F32)
        for s in range(1, n_parts):
            g = g + p_ref[s].astype(F32)
        m_new = ADAM_B1 * m_ref[...] + (1.0 - ADAM_B1) * g
        v_new = ADAM_B2 * v_ref[...] + (1.0 - ADAM_B2) * (g * g)
        m_hat = m_new / (1.0 - ADAM_B1 ** ADAM_STEP)
        v_hat = v_new / (1.0 - ADAM_B2 ** ADAM_STEP)
        g_ref[...] = g
        d_ref[...] = -ADAM_LR * (m_hat / (jnp.sqrt(v_hat) + ADAM_EPS) + ADAM_WD * w_ref[...])
        nm_ref[...] = m_new
        nv_ref[...] = v_new

    blk = pl.BlockSpec((c, r), lambda i, dev_ref: (0, 0))
    return pl.pallas_call(
        body,
        grid_spec=pltpu.PrefetchScalarGridSpec(
            num_scalar_prefetch=1, grid=(1,),
            in_specs=[blk, pl.BlockSpec((n_parts, c, r), lambda i, dev_ref: (0, 0, dev_ref[0])), blk, blk], out_specs=[blk] * 4),
        out_shape=[jax.ShapeDtypeStruct((c, r), F32)] * 4, name=name, compiler_params=_cp("arbitrary"))(dev, w, gathered, m, v)


def _pack_small(d, pre, post, a_log, dt_bias, f_bias, gdn_w, fq_w, fk_w, extra):
    row2 = jnp.concatenate([a_log, dt_bias, f_bias, gdn_w, fq_w, fk_w, extra], axis=1)
    row2 = jnp.pad(row2, ((0, 0), (0, d - row2.shape[1])))
    return jnp.concatenate([pre, post, row2, jnp.zeros((5, d), F32)], axis=0)


def _adamw_small(w, parts, m, v, nh, name):
    d = w.shape[1]
    n_parts = parts.shape[0]
    shapes = dict(pre=d, post=d, a_log=nh, dt_bias=nh, f_bias=nh, gdn_w=HEAD_DIM, fq_w=HEAD_DIM, fk_w=HEAD_DIM, extra=1)

    def body(w_ref, p_ref, m_ref, v_ref, *o_refs):
        g = p_ref[0]
        for s in range(1, n_parts):
            g = g + p_ref[s]
        m_new = ADAM_B1 * m_ref[...] + (1.0 - ADAM_B1) * g
        v_new = ADAM_B2 * v_ref[...] + (1.0 - ADAM_B2) * (g * g)
        m_hat = m_new / (1.0 - ADAM_B1 ** ADAM_STEP)
        v_hat = v_new / (1.0 - ADAM_B2 ** ADAM_STEP)
        delta = -ADAM_LR * (m_hat / (jnp.sqrt(v_hat) + ADAM_EPS) + ADAM_WD * w_ref[...])
        for k, val in enumerate((g, delta, m_new, v_new)):
            vectors = _unpack_small(val, nh)
            for j, key in enumerate(shapes):
                o_refs[k * len(shapes) + j][...] = vectors[key]

    full = lambda shape: pl.BlockSpec(shape, lambda i: (0,) * len(shape))
    outs = pl.pallas_call(
        body, grid=(1,), in_specs=[full(w.shape), full(parts.shape), full(w.shape), full(w.shape)],
        out_specs=[full((1, n)) for n in shapes.values()] * 4,
        out_shape=[jax.ShapeDtypeStruct((1, n), F32) for n in shapes.values()] * 4, name=name,
        compiler_params=_cp("arbitrary"))(w, parts, m, v)
    return [dict(zip(shapes, outs[k * len(shapes):(k + 1) * len(shapes)])) for k in range(4)]


def _unpack_small(p, nh):
    o = 3 * nh
    return dict(pre=p[0:1], post=p[1:2], a_log=p[2:3, 0:nh], dt_bias=p[2:3, nh:2 * nh], f_bias=p[2:3, 2 * nh:o],
                gdn_w=p[2:3, o:o + HEAD_DIM], fq_w=p[2:3, o + HEAD_DIM:o + 2 * HEAD_DIM],
                fk_w=p[2:3, o + 2 * HEAD_DIM:o + 3 * HEAD_DIM], extra=p[2:3, o + 3 * HEAD_DIM:o + 3 * HEAD_DIM + 1])


def kernel(x, meta_tokens, pre_norm_w, w_in, conv_w, a_log, dt_bias, gdn_norm_w, fox_q_norm_w, fox_k_norm_w, fox_f_bias, w_out, post_norm_w, loss_target, m_meta_tokens, m_pre_norm_w, m_w_in, m_conv_w, m_a_log, m_dt_bias, m_gdn_norm_w, m_fox_q_norm_w, m_fox_k_norm_w, m_fox_f_bias, m_w_out, m_post_norm_w, v_meta_tokens, v_pre_norm_w, v_w_in, v_conv_w, v_a_log, v_dt_bias, v_gdn_norm_w, v_fox_q_norm_w, v_fox_k_norm_w, v_fox_f_bias, v_w_out, v_post_norm_w):
    nh = a_log.shape[1]
    d = x.shape[-1]
    w = nh * HEAD_DIM
    zero = jnp.zeros((1, 1), F32)

    w_in_a, w_in_b = _cast_bf16_column_major(w_in, 2, "cast_w_in")

    def project(xn, got):
        wg, cg, mg = got
        half = (d // 2, 0), (d // 2, 1)
        wfull = _relayout_w_in(wg, nh, 256, "relayout_w_in_a", rows_total=d)
        proj, got = _matmul(xn, wfull, "nn", MM_TILE, F32, "proj_a", _gather_rider([w_in_b]), a_cols=half[0], b_rows=half[0])
        wfull = _relayout_w_in(got[0], nh, 256, "relayout_w_in_b", into=wfull)
        proj, got = _matmul(xn, wfull, "nn", MM_TILE, F32, "proj_b", _gather_rider([_cast_bf16(w_out[0], 256, "cast_w_out")]),
                            a_cols=half[1], b_rows=half[1], acc=proj)
        return (proj, wfull, cg.transpose(1, 0, 2).reshape(CONV_WIDTH, 3 * w), mg.transpose(1, 0, 2).reshape(N_META, d),
                got[0].reshape(2 * w, d))

    core = lax.axis_index("c")
    dev = 4 * lax.axis_index("x") + 2 * lax.axis_index("y") + core
    core_arr = jnp.reshape(core, (1,)).astype(jnp.int32)

    out_parts = lambda dw_out: dw_out.reshape(N_DEV, 2 * w // N_DEV, d)
    g = _layer_grads(
        x[0], loss_target[0], None, pre_norm_w, (_gather_rider([w_in_a, conv_w[0].T, meta_tokens]), 2, project), None,
        a_log, dt_bias, gdn_norm_w,
        fox_q_norm_w, fox_k_norm_w, fox_f_bias, None, post_norm_w,
        w_out_grads=(lambda dw_out: _pair_rider(parts=out_parts(dw_out)),
                     lambda dw_out, got: _chip_rider(_pair_sum(out_parts(dw_out), got[0], core_arr, 256, "pair_sum_w_out"))))
    p_out = g["w_out_parts"][0]
    xn, dproj, wfull, meta_full = g["xn"], g["dproj"], g["wfull"], g["meta"]
    flights, token, dw, rider = [], None, None, None

    def exchange(dw, got, i):
        sums = _relayout_pair_sum(dw, got[0], got[1], core_arr, nh, 128, f"relayout_pair_sum_{i}")
        flight, token = _chip_exchange_start(sums, f"chip_exchange_start_{i}")
        flights.append(flight)
        return token

    for i, (index, parts) in enumerate(DW_IN_PIECES):
        res = _matmul(xn, dproj, "tn", MM_TILE, BF16, f"dw_in_{i}", rider, a_cols=(d // parts, index))
        if i:
            token = exchange(dw, res[1], i - 1)
        dw = res[0] if i else res
        rider = _pair_rider(dw_rows=dw, nh=nh, after=token)
    dxn, (got,) = _dxn(dproj, wfull, [rider], MM_TILE, "dxn")
    token = exchange(dw, got, len(DW_IN_PIECES) - 1)
    *r_out, token = _adamw(w_out[0], p_out, m_w_out[0], v_w_out[0], 64, "adamw_w_out", after=token)
    (grad_x, dmeta, dpre_w), _ = _prenorm_bwd(dxn, x[0], meta_full, pre_norm_w + token[0:1, 0:1], g["dy"])
    small = _pack_small(d, dpre_w, g["post_w"], g["a_log"], g["dt_bias"], g["f_bias"], g["gdn_norm_w"], g["fq_w"],
                        g["fk_w"], g["loss"])
    a_conv, a_meta, p_small = _all_gather([g["conv_wt"], dmeta, small], "gather_small_grads")
    p_meta = lax.dynamic_slice_in_dim(a_meta, dev * meta_tokens.shape[1], meta_tokens.shape[1], axis=2)

    r_conv = _adamw_conv(conv_w[0].T, a_conv, jnp.reshape(dev, (1,)).astype(jnp.int32), m_conv_w[0].T, v_conv_w[0].T,
                         "adamw_conv_w")
    r_conv = [o.T for o in r_conv]
    r_meta = _adamw(meta_tokens, p_meta, m_meta_tokens, v_meta_tokens, N_META, "adamw_meta")
    pk = lambda pre, post, a, dt, gw, fq, fk, fb: _pack_small(d, pre, post, a, dt, fb, gw, fq, fk, zero)
    sm = _adamw_small(
        pk(pre_norm_w, post_norm_w, a_log, dt_bias, gdn_norm_w, fox_q_norm_w, fox_k_norm_w, fox_f_bias), p_small,
        pk(m_pre_norm_w, m_post_norm_w, m_a_log, m_dt_bias, m_gdn_norm_w, m_fox_q_norm_w, m_fox_k_norm_w, m_fox_f_bias),
        pk(v_pre_norm_w, v_post_norm_w, v_a_log, v_dt_bias, v_gdn_norm_w, v_fox_q_norm_w, v_fox_k_norm_w, v_fox_f_bias),
        nh, "adamw_small")
    p_in = _chip_exchange_wait(flights, sm[0]["pre"], "chip_exchange_wait")
    r_in = _adamw_column_major(w_in, p_in, m_w_in, v_w_in, "adamw_w_in")

    outs = []
    for i in range(4):
        s = sm[i]
        outs += [r_meta[i], s["pre"], r_in[i], r_conv[i][None], s["a_log"], s["dt_bias"], s["gdn_w"], s["fq_w"],
                 s["fk_w"], s["f_bias"], r_out[i][None], s["post"]]
    return (sm[0]["extra"].reshape(()), grad_x[None], *outs)
```

```python
import jax
import jax.numpy as jnp
from jax import lax
from jax.experimental import pallas as pl
from jax.experimental.pallas import tpu as pltpu

F32, BF16 = jnp.float32, jnp.bfloat16
HEAD_DIM = 128
N_META = 16
CONV_WIDTH = 4
CHUNK = 128
Q_BLOCK = 128
LANES = 128
EPS = 1e-6
PAD_ROWS = Q_BLOCK - N_META
N_DEV = 8
N_CHIP = 4
VMEM_LIMIT = 56 * 1024 * 1024
NEG = -1e30
NARROW = 2 * LANES
MM_TILE = 6 * LANES
LOCAL_DMA_PRIORITY = 1
DW_IN_PIECES = ((0, 2), (2, 4), (3, 4))

ADAM_LR, ADAM_B1, ADAM_B2, ADAM_EPS, ADAM_WD, ADAM_STEP = 0.001, 0.9, 0.999, 1e-08, 0.01, 10

_DN = {"nn": (((1,), (0,)), ((), ())), "nt": (((1,), (1,)), ((), ())), "tn": (((0,), (0,)), ((), ()))}
_DN3 = {"nn": (((2,), (1,)), ((0,), (0,))), "nt": (((2,), (2,)), ((0,), (0,))), "tn": (((1,), (1,)), ((0,), (0,)))}
_ANY = pl.BlockSpec(memory_space=pl.ANY)
_MESH = pl.DeviceIdType.MESH


def _cp(*sem):
    return pltpu.CompilerParams(dimension_semantics=sem, vmem_limit_bytes=VMEM_LIMIT)


def _dot(a, b, dims="nn", prec=None):
    return lax.dot_general(a, b, _DN[dims], precision=prec, preferred_element_type=F32)


def _bdot(a, b, dims="nn"):
    return _dot(a.astype(BF16), b.astype(BF16), dims)


def _hdot(a, b, dims="nn"):
    return _dot(a, b, dims, prec=lax.Precision.HIGHEST)


def _dot3(a, b, dims="nn"):
    return lax.dot_general(a, b, _DN3[dims], preferred_element_type=F32)


def _bdot3(a, b, dims="nn"):
    return _dot3(a.astype(BF16), b.astype(BF16), dims)


def _split(a):
    hi = a.astype(BF16)
    return hi, (a - hi.astype(F32)).astype(BF16)


def _iota(shape, dim):
    return lax.broadcasted_iota(jnp.int32, shape, dim)


def _sigmoid(z):
    return 1.0 / (1.0 + jnp.exp(-z))


def _softplus(z):
    e = jnp.exp(-jnp.abs(z))
    u = 1.0 + e
    l1p = jnp.where(u == 1.0, e, jnp.log(u) * (e / jnp.where(u == 1.0, 1.0, u - 1.0)))
    return jnp.maximum(z, 0.0) + l1p


def _silu_and_grad(z):
    s = _sigmoid(z)
    return z * s, s * (1.0 + z * (1.0 - s))


def _rms(x):
    return lax.rsqrt(jnp.mean(x * x, axis=-1, keepdims=True) + EPS)


def _h_tile(i, x_ref, meta_ref):
    first = jnp.concatenate([jnp.zeros((PAD_ROWS, x_ref.shape[1]), F32), meta_ref[...]], axis=0)
    return jnp.where(i == 0, first, x_ref[...])


def _x_rows(d):
    return pl.BlockSpec((Q_BLOCK, d), lambda i: (jnp.maximum(i - 1, 0), 0))


def _prenorm(x, meta, w):
    seq, d = x.shape
    lp = seq + Q_BLOCK

    def body(x_ref, m_ref, w_ref, o_ref):
        h = _h_tile(pl.program_id(0), x_ref, m_ref)
        o_ref[...] = (h * _rms(h) * w_ref[...]).astype(BF16)

    return pl.pallas_call(
        body, grid=(lp // Q_BLOCK,),
        in_specs=[_x_rows(d), pl.BlockSpec((N_META, d), lambda i: (0, 0)), pl.BlockSpec((1, d), lambda i: (0, 0))],
        out_specs=pl.BlockSpec((Q_BLOCK, d), lambda i: (i, 0)),
        out_shape=jax.ShapeDtypeStruct((lp, d), BF16), name="prenorm", compiler_params=_cp("parallel"))(x, meta, w)


def _prenorm_gathering(x, w, rider, meta_at):
    seq, d = x.shape
    steps = seq // Q_BLOCK + 1

    def body(start, finish, x_ref, w_ref, o_ref, meta_buf, meta_sem):
        i = pl.program_id(0)
        pl.when(i == 0)(start)

        @pl.when(i < steps - 1)
        def _():
            h = x_ref[...]
            o_ref[...] = (h * _rms(h) * w_ref[...]).astype(BF16)

        @pl.when(i == steps - 1)
        def _():
            finish()
            cp = pltpu.make_async_copy(finish.results[0][meta_at], meta_buf, meta_sem)
            cp.start()
            cp.wait()
            h = jnp.concatenate([jnp.zeros((PAD_ROWS, d), F32), jnp.concatenate([meta_buf[s] for s in range(N_DEV)], axis=1)], axis=0)
            o_ref[...] = (h * _rms(h) * w_ref[...]).astype(BF16)

    (xn,), got = _hosted_call(
        body, [rider], 2, 1, 2, grid=(steps,),
        in_specs=[pl.BlockSpec((Q_BLOCK, d), lambda i: (jnp.minimum(i, steps - 2), 0)), pl.BlockSpec((1, d), lambda i: (0, 0))],
        out_specs=[pl.BlockSpec((Q_BLOCK, d), lambda i: ((i + 1) % steps, 0))],
        out_shape=[jax.ShapeDtypeStruct((seq + Q_BLOCK, d), BF16)],
        scratch_shapes=[pltpu.VMEM((N_DEV, N_META, d // N_DEV), F32), pltpu.SemaphoreType.DMA(())],
        name="prenorm", compiler_params=_cp("arbitrary"))(x, w)
    return xn, got[0]


def _tile(n, want):
    return max(t for t in range(LANES, want + 1, LANES) if n % t == 0)


class _Rider:
    def __init__(self, inputs, out_shapes, scratch, aliases, make):
        self.inputs, self.out_shapes, self.scratch, self.aliases, self.make = inputs, out_shapes, scratch, aliases, make


def _hosted_call(body, riders, n_in, n_out, n_scratch, *, in_specs, out_specs, out_shape, scratch_shapes=(), aliases=None,
                 **kw):
    riders = [r for r in riders if r is not None]
    r_in = [len(r.inputs) for r in riders]
    r_out = [len(r.out_shapes) for r in riders]
    r_scr = [len(r.scratch) for r in riders]
    al = dict(aliases or {})
    for k, r in enumerate(riders):
        al.update({n_in + sum(r_in[:k]) + i: n_out + sum(r_out[:k]) + o for i, o in r.aliases.items()})

    def full_body(*refs):
        ins, rest = refs[:n_in + sum(r_in)], refs[n_in + sum(r_in):]
        outs, scr = rest[:n_out + sum(r_out)], rest[n_out + sum(r_out):]
        hooks = [r.make(ins[n_in + sum(r_in[:k]):n_in + sum(r_in[:k + 1])], outs[n_out + sum(r_out[:k]):n_out + sum(r_out[:k + 1])],
                        scr[n_scratch + sum(r_scr[:k]):n_scratch + sum(r_scr[:k + 1])]) for k, r in enumerate(riders)]

        def start():
            for h in hooks:
                h[0]()

        ran_middle = []

        def middle():
            ran_middle.append(True)
            for h in hooks:
                if len(h) == 3:
                    h[1]()

        def finish():
            if not ran_middle:
                middle()
            for h in hooks:
                h[-1]()

        finish.middle = middle
        finish.results = [outs[n_out + sum(r_out[:k]):n_out + sum(r_out[:k + 1])] for k in range(len(riders))]
        body(start, finish, *ins[:n_in], *outs[:n_out], *scr[:n_scratch])

    call = pl.pallas_call(
        full_body, in_specs=list(in_specs) + [_ANY] * sum(r_in), out_specs=list(out_specs) + [_ANY] * sum(r_out),
        out_shape=list(out_shape) + [s for r in riders for s in r.out_shapes],
        scratch_shapes=list(scratch_shapes) + [s for r in riders for s in r.scratch], input_output_aliases=al, **kw)

    def run(*args):
        res = call(*args, *[t for r in riders for t in r.inputs])
        return res[:n_out], [res[n_out + sum(r_out[:k]):n_out + sum(r_out[:k + 1])] for k in range(len(riders))]

    return run


def _matmul(a, b, dims, tn, out_dtype, name, rider=None, a_cols=None, b_rows=None, acc=None):
    a_shape = a.shape if a_cols is None else (a.shape[0], a_cols[0])
    a_index = 0 if a_cols is None else a_cols[1]
    m = a_shape[1] if dims == "tn" else a_shape[0]
    n = b.shape[0] if dims == "nt" else b.shape[1]
    kdim = b.shape[1] if dims == "nt" else b.shape[0] if b_rows is None else b_rows[0]
    b_index = 0 if b_rows is None else b_rows[1]
    tn = _tile(n, tn)
    steps = n // tn
    b_spec = pl.BlockSpec((tn, kdim), lambda j: (j, 0)) if dims == "nt" else pl.BlockSpec((kdim, tn), lambda j: (b_index, j))
    o_spec = pl.BlockSpec((m, tn), lambda j: (0, j))

    def body(start, finish, a_ref, b_ref, *refs):
        pl.when(pl.program_id(0) == 0)(start)
        prod = _dot(a_ref[...], b_ref[...], dims)
        refs[-1][...] = (prod if acc is None else prod + refs[0][...]).astype(out_dtype)
        pl.when(pl.program_id(0) == steps // 2)(finish.middle)
        pl.when(pl.program_id(0) == steps - 1)(finish)

    (out,), got = _hosted_call(
        body, [rider], 2 + (acc is not None), 1, 0, grid=(steps,),
        in_specs=[pl.BlockSpec(a_shape, lambda j: (0, a_index)), b_spec] + [o_spec] * (acc is not None),
        out_specs=[o_spec], out_shape=[jax.ShapeDtypeStruct((m, n), out_dtype)], aliases={2: 0} if acc is not None else None,
        name=name, compiler_params=_cp("parallel" if rider is None else "arbitrary"))(a, b, *([acc] if acc is not None else []))
    return out if rider is None else (out, got[0])


def _chip_rider(sums):
    def make(ins, outs, scratch):
        local, remote = _chip_exchange_copies(ins[0], outs[0], *scratch)

        def start():
            local.start(LOCAL_DMA_PRIORITY)
            for cp in remote:
                cp.start()

        def finish():
            local.wait()
            for cp in remote:
                cp.wait_send()
                cp.wait_recv()

        return start, finish

    return _Rider([sums], [jax.ShapeDtypeStruct(sums.shape, sums.dtype)],
                  [pltpu.SemaphoreType.DMA((N_CHIP - 1,)), pltpu.SemaphoreType.DMA((N_CHIP - 1,)), pltpu.SemaphoreType.DMA((1,))],
                  {}, make)


_HBM = pl.BlockSpec(memory_space=pltpu.HBM)
_SEM = pl.BlockSpec(memory_space=pltpu.SEMAPHORE)


def _chip_exchange_copies(sums_ref, land_ref, send_sems, recv_sems, local_sem):
    x, y, core = lax.axis_index("x"), lax.axis_index("y"), lax.axis_index("c")
    mine = 2 * x + y
    local = pltpu.make_async_copy(sums_ref.at[mine], land_ref.at[mine], local_sem.at[0])
    remote = []
    for k in range(1, N_CHIP):
        px = 1 - x if k & 2 else x
        py = 1 - y if k & 1 else y
        remote.append(pltpu.make_async_remote_copy(
            src_ref=sums_ref.at[2 * px + py], dst_ref=land_ref.at[mine], send_sem=send_sems.at[k - 1],
            recv_sem=recv_sems.at[k - 1], device_id=(px, py, core), device_id_type=_MESH))
    return local, remote


def _chip_exchange_start(sums, name):
    def body(s_ref, send_sems, recv_sems, local_sem, s_thru, land_ref, token):
        local, remote = _chip_exchange_copies(s_ref, land_ref, send_sems, recv_sems, local_sem)
        local.start(LOCAL_DMA_PRIORITY)
        for cp in remote:
            cp.start()
        token[...] = jnp.zeros_like(token)

    *flight, token = pl.pallas_call(
        body, name=name,
        out_shape=(pltpu.SemaphoreType.DMA((N_CHIP - 1,)), pltpu.SemaphoreType.DMA((N_CHIP - 1,)), pltpu.SemaphoreType.DMA((1,)),
                   pltpu.HBM(sums.shape, sums.dtype), pltpu.HBM(sums.shape, sums.dtype), jax.ShapeDtypeStruct((8, LANES), F32)),
        in_specs=(_HBM,), out_specs=(_SEM, _SEM, _SEM, _HBM, _HBM, pl.BlockSpec(memory_space=pltpu.VMEM)),
        input_output_aliases={0: 3},
        compiler_params=pltpu.CompilerParams(has_side_effects=pltpu.SideEffectType.DATAFLOW_SIDE_EFFECTING))(
            pltpu.with_memory_space_constraint(sums, pltpu.HBM))
    return flight, token


def _chip_exchange_wait(flights, after, name):
    n = len(flights)

    def body(*refs):
        for i in range(n):
            s_ref, land_ref = refs[2 * i:2 * i + 2]
            local, remote = _chip_exchange_copies(s_ref, land_ref, *refs[2 * n + 3 * i:2 * n + 3 * i + 3])
            local.wait()
            for cp in remote:
                cp.wait_send()
                cp.wait_recv()

    buffers = [b for f in flights for b in f[3:]]
    res = pl.pallas_call(
        body, name=name, out_shape=tuple(pltpu.HBM(b.shape, b.dtype) for b in buffers),
        in_specs=(_HBM,) * (2 * n) + (_SEM,) * (3 * n) + (_ANY,), out_specs=(_HBM,) * (2 * n),
        input_output_aliases={i: i for i in range(2 * n)},
        compiler_params=pltpu.CompilerParams(has_side_effects=pltpu.SideEffectType.DATAFLOW_SIDE_EFFECTING))(
            *buffers, *[s for f in flights for s in f[:3]], after)
    return res[1::2]


def _dxn(dproj, wfull, riders, tk, name):
    m, k = dproj.shape
    n = wfull.shape[0]
    tk = _tile(k, tk)
    steps = k // tk

    def body(start, finish, a_ref, b_ref, o_ref):
        j = pl.program_id(0)

        @pl.when(j == 0)
        def _():
            start()
            o_ref[...] = jnp.zeros_like(o_ref)
        o_ref[...] += _dot(a_ref[...], b_ref[...], "nt")
        pl.when(j == steps - 1)(finish)

    (dxn,), got = _hosted_call(
        body, riders, 2, 1, 0, grid=(steps,),
        in_specs=[pl.BlockSpec((m, tk), lambda j: (0, j)), pl.BlockSpec((n, tk), lambda j: (0, j))],
        out_specs=[pl.BlockSpec((m, n), lambda j: (0, 0))], out_shape=[jax.ShapeDtypeStruct((m, n), F32)],
        name=name, compiler_params=_cp("arbitrary"))(dproj, wfull)
    return dxn, got


def _conv_taps(x, w):
    c = x * w[CONV_WIDTH - 1:CONV_WIDTH, :]
    for j in range(CONV_WIDTH - 1):
        c = c + pltpu.roll(x, CONV_WIDTH - 1 - j, 0) * w[j:j + 1, :]
    return c


def _gdn_prep(proj, conv_wt, nh):
    lp = proj.shape[0]
    scale = HEAD_DIM ** -0.5

    def body(x_ref, w_ref, o_ref):
        which = pl.program_id(0) // nh
        c = _conv_taps(x_ref[...], w_ref[...])
        s = c * _sigmoid(c)
        r = lax.rsqrt(jnp.sum(s * s, axis=-1, keepdims=True) + EPS)
        f = jnp.where(which == 0, r * scale, jnp.where(which == 1, r, 1.0))
        o_ref[...] = jnp.where(_iota(s.shape, 0) >= PAD_ROWS, s * f, 0.0)

    return pl.pallas_call(
        body, grid=(3 * nh,),
        in_specs=[pl.BlockSpec((lp, LANES), lambda s: (0, s)), pl.BlockSpec((CONV_WIDTH, LANES), lambda s: (0, s))],
        out_specs=pl.BlockSpec((lp, LANES), lambda s: (0, s)),
        out_shape=jax.ShapeDtypeStruct((lp, 3 * nh * HEAD_DIM), F32), name="gdn_prep",
        compiler_params=_cp("parallel"))(proj, conv_wt)


def _gdn_prep_bwd(proj, conv_wt, dq, dk, dv, dproj, nh):
    lp = proj.shape[0]
    scale = HEAD_DIM ** -0.5
    part = lambda p: pl.BlockSpec((lp, LANES), lambda s: (0, jnp.clip(s - p * nh, 0, nh - 1)))

    def body(x_ref, w_ref, dq_ref, dk_ref, dv_ref, _, dx_ref, dw_ref):
        which = pl.program_id(0) // nh
        x = x_ref[...]
        w = w_ref[...]
        c = _conv_taps(x, w)
        sg = _sigmoid(c)
        s = c * sg
        r = lax.rsqrt(jnp.sum(s * s, axis=-1, keepdims=True) + EPS)
        dy = jnp.where(which == 0, dq_ref[...], jnp.where(which == 1, dk_ref[...], dv_ref[...]))
        dy = jnp.where(_iota(s.shape, 0) >= PAD_ROWS, dy, 0.0)
        y0 = s * r
        dy0 = dy * jnp.where(which == 0, scale, 1.0)
        ds_n = r * (dy0 - y0 * jnp.sum(dy0 * y0, axis=-1, keepdims=True))
        ds = jnp.where(which == 2, dy, ds_n)
        dc = ds * (sg * (1.0 + c * (1.0 - sg)))
        dx = dc * w[CONV_WIDTH - 1:CONV_WIDTH, :]
        rows = [jnp.sum(dc * x, axis=0, keepdims=True)]
        for j in range(CONV_WIDTH - 2, -1, -1):
            sh = CONV_WIDTH - 1 - j
            dx = dx + pltpu.roll(dc, lp - sh, 0) * w[j:j + 1, :]
            rows.insert(0, jnp.sum(dc * pltpu.roll(x, sh, 0), axis=0, keepdims=True))
        dx_ref[...] = dx.astype(BF16)
        dw_ref[...] = jnp.concatenate(rows, axis=0)

    strip = pl.BlockSpec((lp, LANES), lambda s: (0, s))
    taps = pl.BlockSpec((CONV_WIDTH, LANES), lambda s: (0, s))
    return pl.pallas_call(
        body, grid=(3 * nh,), in_specs=[strip, taps, part(0), part(1), part(2), _ANY], out_specs=[strip, taps],
        out_shape=[jax.ShapeDtypeStruct(dproj.shape, BF16), jax.ShapeDtypeStruct((CONV_WIDTH, 3 * nh * HEAD_DIM), F32)],
        input_output_aliases={5: 0}, name="gdn_prep_bwd", compiler_params=_cp("parallel"))(proj, conv_wt, dq, dk, dv, dproj)


def _gates(proj, bias_row, nega_row, nh):
    lp = proj.shape[0]
    nc = lp // CHUNK

    def body(p_ref, b_ref, a_ref, g_ref, gt3_ref, gtf_ref):
        lane = _iota((CHUNK, LANES), 1)
        tri = (_iota((CHUNK, CHUNK), 0) >= _iota((CHUNK, CHUNK), 1)).astype(F32)

        def step(n, carry):
            r0 = pl.multiple_of(n * CHUNK, CHUNK)
            z = p_ref[pl.ds(r0, CHUNK), :] + b_ref[...]
            base = jnp.where(lane < nh, _sigmoid(z),
                             jnp.where(lane < 2 * nh, a_ref[...] * _softplus(z),
                                       jnp.where(lane < 3 * nh, -_softplus(-z), 0.0)))
            base = jnp.where(r0 + _iota((CHUNK, LANES), 0) >= PAD_ROWS, base, 0.0)
            cs = _hdot(tri, base)
            run = jnp.where((lane >= 2 * nh) & (lane < 3 * nh), cs + carry, cs)
            sh = pltpu.roll(run, 2 * nh, 1)
            out = base + jnp.where((lane >= 3 * nh) & (lane < 5 * nh), sh, 0.0)
            g_ref[pl.ds(r0, CHUNK), :] = out
            gt3_ref[n] = out.T
            return carry + cs[CHUNK - 1:CHUNK, :]

        lax.fori_loop(0, nc, step, jnp.zeros((1, LANES), F32))
        gtf_ref[...] = g_ref[...].T

    vec = pl.BlockSpec((1, LANES), lambda i: (0, 0))
    return pl.pallas_call(
        body, grid=(1,), in_specs=[pl.BlockSpec((lp, LANES), lambda i: (0, 8 * nh)), vec, vec],
        out_specs=[pl.BlockSpec((lp, LANES), lambda i: (0, 0)), pl.BlockSpec((nc, LANES, CHUNK), lambda i: (0, 0, 0)),
                   pl.BlockSpec((LANES, lp), lambda i: (0, 0))],
        out_shape=[jax.ShapeDtypeStruct((lp, LANES), F32), jax.ShapeDtypeStruct((nc, LANES, CHUNK), F32),
                   jax.ShapeDtypeStruct((LANES, lp), F32)],
        name="gates", compiler_params=_cp("arbitrary"))(proj, bias_row, nega_row)


def _gates_bwd(proj, bias_row, nega_row, gates, dgate_gdn, dc_t, dproj, nh):
    lp = proj.shape[0]
    nc = lp // CHUNK

    def body(p_ref, b_ref, a_ref, g_ref, dg_ref, dc_ref, _, dz_ref, sm_ref, dct_scr):
        lane = _iota((CHUNK, LANES), 1)
        triu = (_iota((CHUNK, CHUNK), 0) <= _iota((CHUNK, CHUNK), 1)).astype(F32)
        dct_scr[...] = dc_ref[...].T
        sm_ref[...] = jnp.zeros_like(sm_ref)
        dz_ref[:, LANES:] = jnp.zeros((lp, NARROW - LANES), BF16)

        def step(i, carry):
            n = nc - 1 - i
            r0 = pl.multiple_of(n * CHUNK, CHUNK)
            z = p_ref[pl.ds(r0, CHUNK), :] + b_ref[...]
            gt = g_ref[pl.ds(r0, CHUNK), :]
            dgd = dg_ref[pl.ds(r0, CHUNK), :]
            dch = dct_scr[pl.ds(r0, CHUNK), :]
            rc = _hdot(triu, dch) + carry
            sg = _sigmoid(z)
            dz = jnp.where(lane < nh, dgd * sg * (1.0 - sg),
                           jnp.where(lane < 2 * nh, dgd * a_ref[...] * sg,
                                     jnp.where(lane < 3 * nh, rc * (1.0 - sg), 0.0)))
            dz = jnp.where(r0 + _iota((CHUNK, LANES), 0) >= PAD_ROWS, dz, 0.0)
            dz_ref[pl.ds(r0, CHUNK), 0:LANES] = dz.astype(BF16)
            sm_ref[0:1, :] += jnp.sum(dz, axis=0, keepdims=True)
            sm_ref[1:2, :] += jnp.sum(jnp.where((lane >= nh) & (lane < 2 * nh), dgd * gt, 0.0), axis=0, keepdims=True)
            return carry + jnp.sum(dch, axis=0, keepdims=True)

        lax.fori_loop(0, nc, step, jnp.zeros((1, LANES), F32))

    vec = pl.BlockSpec((1, LANES), lambda i: (0, 0))
    full = pl.BlockSpec((lp, LANES), lambda i: (0, 0))
    last = pl.BlockSpec((lp, LANES), lambda i: (0, 8 * nh))
    tail = pl.BlockSpec((lp, NARROW), lambda i: (0, 8 * nh * LANES // NARROW))
    return pl.pallas_call(
        body, grid=(1,), in_specs=[last, vec, vec, full, full, pl.BlockSpec((LANES, lp), lambda i: (0, 0)), _ANY],
        out_specs=[tail, pl.BlockSpec((8, LANES), lambda i: (0, 0))],
        out_shape=[jax.ShapeDtypeStruct(dproj.shape, BF16), jax.ShapeDtypeStruct((8, LANES), F32)],
        scratch_shapes=[pltpu.VMEM((lp, LANES), F32)], input_output_aliases={6: 0},
        name="gates_bwd", compiler_params=_cp("arbitrary"))(proj, bias_row, nega_row, gates, dgate_gdn, dc_t, dproj)


def _tri_inv(a):
    t = jnp.where(_iota(a.shape, 1) == _iota(a.shape, 2), 1.0, 0.0) - a
    p = a
    for _ in range(CHUNK.bit_length() - 2):
        ph, pw = _split(p)
        p = _dot3(ph, ph) + (_dot3(ph, pw) + _dot3(pw, ph))
        ph, pw = _split(p)
        th, tw = _split(t)
        t = t + (_dot3(th, ph) + (_dot3(th, pw) + _dot3(tw, ph)))
    return t


def _gdn_chunk(q, k, v, beta, gc, gr, t=None):
    ii, jj = _iota((1, CHUNK, CHUNK), 1), _iota((1, CHUNK, CHUNK), 2)
    causal, strict = ii >= jj, ii > jj
    dm = jnp.where(causal, jnp.exp(jnp.where(causal, gc - gr, 0.0)), 0.0)
    kk = _bdot3(k, k, "nt")
    a = jnp.where(strict, beta * kk * dm, 0.0)
    if t is None:
        t = _tri_inv(a)
    eg = jnp.exp(gc)
    glast = gc[:, CHUNK - 1:CHUNK, :]
    ekd = jnp.exp(glast - gc)
    bv = beta * v
    bk = (beta * eg) * k
    ub = _bdot3(t, jnp.concatenate([bv, bk], axis=2))
    qk = _bdot3(q, k, "nt")
    return dict(causal=causal, strict=strict, dm=dm, kk=kk, a=a, t=t, eg=eg, ekd=ekd, bv=bv, bk=bk,
                u=ub[:, :, :HEAD_DIM], w=ub[:, :, HEAD_DIM:], qk=qk, aqk=jnp.where(causal, qk * dm, 0.0),
                q_dec=q * eg, k_dec=k * ekd, decay=jnp.exp(glast))


def _heads(ref, nh):
    return jnp.stack([ref[:, h * HEAD_DIM:(h + 1) * HEAD_DIM] for h in range(nh)], axis=0)


def _gdn_chunk_inputs(q_ref, k_ref, v_ref, g, gt, nh):
    col = lambda o: jnp.stack([g[:, o + h:o + h + 1] for h in range(nh)], axis=0)
    gr = jnp.stack([gt[3 * nh + h:3 * nh + h + 1, :] for h in range(nh)], axis=0)
    return _heads(q_ref, nh), _heads(k_ref, nh), _heads(v_ref, nh), col(0), col(3 * nh), gr


def _gdn_fwd(qkv, gates, gt3, nh, rider=None):
    lp = qkv.shape[0]
    nc = lp // CHUNK
    w = nh * HEAD_DIM

    def body(start, finish, q_ref, k_ref, v_ref, g_ref, gt_ref, o_ref, sall_ref, tall_ref, s_scr):
        @pl.when(pl.program_id(0) == 0)
        def _():
            start()
            s_scr[...] = jnp.zeros_like(s_scr)
        c = _gdn_chunk(*_gdn_chunk_inputs(q_ref, k_ref, v_ref, g_ref[...], gt_ref[0], nh))
        s = s_scr[...]
        sall_ref[0] = s
        tall_ref[0] = c["t"]
        v_new = c["u"] - _bdot3(c["w"], s)
        o = _bdot3(c["q_dec"], s) + _bdot3(c["aqk"], v_new)
        s_scr[...] = s * c["decay"] + _bdot3(c["k_dec"], v_new, "tn")
        for h in range(nh):
            o_ref[:, h * HEAD_DIM:(h + 1) * HEAD_DIM] = o[h]
        pl.when(pl.program_id(0) == nc - 1)(finish)

    outs, got = _hosted_call(
        body, [rider], 5, 3, 1, grid=(nc,),
        in_specs=[pl.BlockSpec((CHUNK, w), lambda n: (n, 0)), pl.BlockSpec((CHUNK, w), lambda n: (n, 1)),
                  pl.BlockSpec((CHUNK, w), lambda n: (n, 2)), pl.BlockSpec((CHUNK, LANES), lambda n: (n, 0)),
                  pl.BlockSpec((1, LANES, CHUNK), lambda n: (n, 0, 0))],
        out_specs=[pl.BlockSpec((CHUNK, w), lambda n: (n, 0)),
                   pl.BlockSpec((1, nh, HEAD_DIM, HEAD_DIM), lambda n: (n, 0, 0, 0)),
                   pl.BlockSpec((1, nh, CHUNK, CHUNK), lambda n: (n, 0, 0, 0))],
        out_shape=[jax.ShapeDtypeStruct((lp, w), F32), jax.ShapeDtypeStruct((nc, nh, HEAD_DIM, HEAD_DIM), F32),
                   jax.ShapeDtypeStruct((nc, nh, CHUNK, CHUNK), F32)],
        scratch_shapes=[pltpu.VMEM((nh, HEAD_DIM, HEAD_DIM), F32)],
        name="gdn_fwd", compiler_params=_cp("arbitrary"))(qkv, qkv, qkv, gates, gt3)
    return outs, (got[0] if got else None)


def _gdn_bwd(qkv, gates, gt3, s_all, t_all, do, nh, rider=None):
    lp = qkv.shape[0]
    nc = lp // CHUNK
    w = nh * HEAD_DIM
    rev = lambda n: nc - 1 - n

    def body(start, finish, q_ref, k_ref, v_ref, g_ref, gt_ref, s_ref, t_ref, do_ref, dq_ref, dk_ref, dv_ref, dg_ref, ds_scr):
        @pl.when(pl.program_id(0) == 0)
        def _():
            start()
            ds_scr[...] = jnp.zeros_like(ds_scr)
        q, k, v, beta, gc, gr = _gdn_chunk_inputs(q_ref, k_ref, v_ref, g_ref[...], gt_ref[0], nh)
        c = _gdn_chunk(q, k, v, beta, gc, gr, t_ref[0])
        s = s_ref[0]
        dsn = ds_scr[...]
        dout = _heads(do_ref, nh)
        v_new = c["u"] - _bdot3(c["w"], s)
        dq_dec = _bdot3(dout, s, "nt")
        daqk = jnp.where(c["causal"], _bdot3(dout, v_new, "nt"), 0.0)
        dv_new = _bdot3(c["aqk"], dout, "tn") + _bdot3(c["k_dec"], dsn)
        dk_dec = _bdot3(v_new, dsn, "nt")
        ddecay = jnp.sum(jnp.sum(dsn * s, axis=2, keepdims=True), axis=1, keepdims=True)
        dw = -_bdot3(dv_new, s, "nt")
        ds_scr[...] = _bdot3(c["q_dec"], dout, "tn") + c["decay"] * dsn - _bdot3(c["w"], dv_new, "tn")
        duw = jnp.concatenate([dv_new, dw], axis=2)
        dt = _bdot3(duw, jnp.concatenate([c["bv"], c["bk"]], axis=2), "nt")
        dbvk = _bdot3(c["t"], duw, "tn")
        dbv, dbk = dbvk[:, :, :HEAD_DIM], dbvk[:, :, HEAD_DIM:]
        da = jnp.where(c["strict"], -_bdot3(_bdot3(c["t"], dt, "tn"), c["t"], "nt"), 0.0)
        dkk = da * beta * c["dm"]
        dqk = daqk * c["dm"]
        e = da * c["a"] + daqk * c["aqk"]
        dq = dq_dec * c["eg"] + _bdot3(dqk, k)
        dk = (dk_dec * c["ekd"] + _bdot3(dkk, k) + _bdot3(dkk, k, "tn") + _bdot3(dqk, q, "tn")
              + (beta * c["eg"]) * dbk)
        dv = beta * dbv
        rs = lambda x: jnp.sum(x, axis=2, keepdims=True)
        dbeta = rs(dbv * v) + c["eg"] * rs(dbk * k) + rs(da * c["kk"] * c["dm"])
        kd_term = rs(dk_dec * c["k_dec"])
        eh, ew = _split(e)
        ones = jnp.ones((nh, CHUNK, LANES), BF16)
        col_sums = (_dot3(eh, ones, "tn") + _dot3(ew, ones, "tn"))[:, :, 0:1]
        dg_cum = rs(dq_dec * c["q_dec"]) - kd_term + rs(dbk * c["bk"]) + rs(e) - col_sums
        last = jnp.sum(kd_term, axis=1, keepdims=True) + ddecay * c["decay"]
        dg_cum = dg_cum + jnp.where(_iota((1, CHUNK, 1), 1) == CHUNK - 1, last, 0.0)
        lane = _iota((CHUNK, LANES), 1)
        acc = jnp.zeros((CHUNK, LANES), F32)
        for h in range(nh):
            sl = slice(h * HEAD_DIM, (h + 1) * HEAD_DIM)
            dq_ref[:, sl] = dq[h]
            dk_ref[:, sl] = dk[h]
            dv_ref[:, sl] = dv[h]
            acc = acc + jnp.where(lane == h, dbeta[h], 0.0) + jnp.where(lane == nh + h, dg_cum[h], 0.0)
        triu = (_iota((CHUNK, CHUNK), 0) <= _iota((CHUNK, CHUNK), 1)).astype(F32)
        dg_ref[...] = jnp.where(lane < nh, acc, _hdot(triu, acc))
        pl.when(pl.program_id(0) == nc - 1)(finish)

    outs, got = _hosted_call(
        body, [rider], 8, 4, 1, grid=(nc,),
        in_specs=[pl.BlockSpec((CHUNK, w), lambda n: (rev(n), 0)), pl.BlockSpec((CHUNK, w), lambda n: (rev(n), 1)),
                  pl.BlockSpec((CHUNK, w), lambda n: (rev(n), 2)), pl.BlockSpec((CHUNK, LANES), lambda n: (rev(n), 0)),
                  pl.BlockSpec((1, LANES, CHUNK), lambda n: (rev(n), 0, 0)),
                  pl.BlockSpec((1, nh, HEAD_DIM, HEAD_DIM), lambda n: (rev(n), 0, 0, 0)),
                  pl.BlockSpec((1, nh, CHUNK, CHUNK), lambda n: (rev(n), 0, 0, 0)),
                  pl.BlockSpec((CHUNK, w), lambda n: (rev(n), 0))],
        out_specs=[pl.BlockSpec((CHUNK, w), lambda n: (rev(n), 0))] * 3 + [pl.BlockSpec((CHUNK, LANES), lambda n: (rev(n), 0))],
        out_shape=[jax.ShapeDtypeStruct((lp, w), F32)] * 3 + [jax.ShapeDtypeStruct((lp, LANES), F32)],
        scratch_shapes=[pltpu.VMEM((nh, HEAD_DIM, HEAD_DIM), F32)],
        name="gdn_bwd", compiler_params=_cp("arbitrary"))(qkv, qkv, qkv, gates, gt3, s_all, t_all, do)
    return outs, (got[0] if got else None)


def _merge_gdn(o_gdn, proj, norm_w, nh):
    lp = o_gdn.shape[0]

    def body(o_ref, z_ref, w_ref, m_ref):
        o = o_ref[...]
        z = z_ref[...]
        m_ref[...] = (o * _rms(o) * w_ref[...] * (z * _sigmoid(z))).astype(BF16)

    return pl.pallas_call(
        body, grid=(nh,),
        in_specs=[pl.BlockSpec((lp, LANES), lambda s: (0, s)), pl.BlockSpec((lp, LANES), lambda s: (0, 3 * nh + s)),
                  pl.BlockSpec((1, LANES), lambda s: (0, 0))],
        out_specs=pl.BlockSpec((lp, LANES), lambda s: (0, s)),
        out_shape=jax.ShapeDtypeStruct((lp, 2 * nh * HEAD_DIM), BF16), name="merge_gdn",
        compiler_params=_cp("parallel"))(o_gdn, proj, norm_w)


def _merge_gdn_bwd(o_gdn, proj, norm_w, dmerged, nh):
    lp = o_gdn.shape[0]

    def body(o_ref, z_ref, w_ref, dm_ref, do_ref, dz_ref, dw_ref):
        o = o_ref[...]
        r = _rms(o)
        xh = o * r
        silu, dsilu = _silu_and_grad(z_ref[...])
        dm = dm_ref[...]
        dn = dm * silu
        dz_ref[...] = (dm * (xh * w_ref[...]) * dsilu).astype(BF16)
        dnw = dn * w_ref[...]
        do_ref[...] = r * (dnw - xh * jnp.mean(dnw * xh, axis=-1, keepdims=True))

        @pl.when(pl.program_id(0) == 0)
        def _():
            dw_ref[...] = jnp.zeros_like(dw_ref)
        dw_ref[...] += jnp.sum(dn * xh, axis=0, keepdims=True)

    w = nh * HEAD_DIM
    return pl.pallas_call(
        body, grid=(nh,),
        in_specs=[pl.BlockSpec((lp, LANES), lambda s: (0, s)), pl.BlockSpec((lp, LANES), lambda s: (0, 3 * nh + s)),
                  pl.BlockSpec((1, LANES), lambda s: (0, 0)), pl.BlockSpec((lp, LANES), lambda s: (0, s))],
        out_specs=[pl.BlockSpec((lp, LANES), lambda s: (0, s)), pl.BlockSpec((lp, LANES), lambda s: (0, 3 * nh + s)),
                   pl.BlockSpec((1, LANES), lambda s: (0, 0))],
        out_shape=[jax.ShapeDtypeStruct((lp, w), F32), jax.ShapeDtypeStruct((lp, 8 * w + NARROW), BF16),
                   jax.ShapeDtypeStruct((1, LANES), F32)],
        name="merge_gdn_bwd", compiler_params=_cp("arbitrary"))(o_gdn, proj, norm_w, dmerged)


def _fox_prep(proj, qk_w, nh):
    lp = proj.shape[0]

    def body(x_ref, w_ref, o_ref):
        x = x_ref[...]
        o_ref[...] = x * _rms(x) * w_ref[0]

    return pl.pallas_call(
        body, grid=(2 * nh,),
        in_specs=[pl.BlockSpec((lp, LANES), lambda s: (0, 4 * nh + s)), pl.BlockSpec((1, 1, LANES), lambda s: (s // nh, 0, 0))],
        out_specs=pl.BlockSpec((lp, LANES), lambda s: (0, s)),
        out_shape=jax.ShapeDtypeStruct((lp, 2 * nh * HEAD_DIM), F32), name="fox_prep",
        compiler_params=_cp("parallel"))(proj, qk_w)


def _fox_prep_bwd(proj, qk_w, dq, dk, dproj, nh):
    lp = proj.shape[0]
    part = lambda p: pl.BlockSpec((lp, LANES), lambda s: (0, jnp.clip(s - p * nh, 0, nh - 1)))

    def body(x_ref, w_ref, dq_ref, dk_ref, _, dx_ref, dw_ref):
        x = x_ref[...]
        r = _rms(x)
        xh = x * r
        dy = jnp.where(pl.program_id(0) < nh, dq_ref[...], dk_ref[...])
        dyw = dy * w_ref[0]
        dx_ref[...] = (r * (dyw - xh * jnp.mean(dyw * xh, axis=-1, keepdims=True))).astype(BF16)

        @pl.when(pl.program_id(0) % nh == 0)
        def _():
            dw_ref[...] = jnp.zeros_like(dw_ref)
        dw_ref[0] += jnp.sum(dy * xh, axis=0, keepdims=True)

    strip = pl.BlockSpec((lp, LANES), lambda s: (0, 4 * nh + s))
    wsp = pl.BlockSpec((1, 1, LANES), lambda s: (s // nh, 0, 0))
    return pl.pallas_call(
        body, grid=(2 * nh,), in_specs=[strip, wsp, part(0), part(1), _ANY], out_specs=[strip, wsp],
        out_shape=[jax.ShapeDtypeStruct(dproj.shape, BF16), jax.ShapeDtypeStruct((2, 1, LANES), F32)],
        input_output_aliases={4: 0}, name="fox_prep_bwd", compiler_params=_cp("arbitrary"))(proj, qk_w, dq, dk, dproj)


def _fox_probs(q, k, gates, crow, h, i, nh, lse=None):
    kl = k.shape[0]
    lane = _iota((Q_BLOCK, LANES), 1)
    ct = jnp.sum(jnp.where(lane == 4 * nh + h, gates, 0.0), axis=1, keepdims=True)
    tq, kq = _iota((Q_BLOCK, Q_BLOCK), 0), _iota((Q_BLOCK, Q_BLOCK), 1)
    qs = q * (HEAD_DIM ** -0.5)
    if i == 0:
        s = _bdot(qs, k, "nt") + (ct - crow)
        s = jnp.where((kq <= tq) & ((kq >= PAD_ROWS) | (tq < PAD_ROWS)), s, NEG)
    else:
        crow = jnp.where(_iota((1, kl), 1) < PAD_ROWS, -NEG, crow)
        s = _bdot(qs, k, "nt") + (ct - crow)
        s = jnp.concatenate([s[:, :kl - Q_BLOCK], jnp.where(kq <= tq, s[:, kl - Q_BLOCK:], NEG)], axis=1)
    if lse is not None:
        return jnp.exp(s - lse)
    m = jnp.max(s, axis=1, keepdims=True)
    p = jnp.exp(s - m)
    tot = jnp.sum(p, axis=1, keepdims=True)
    return p / tot, m + jnp.log(tot)


FOX_HEADS_PER_STEP = 2


def _fox_specs(lp, nh):
    hw = FOX_HEADS_PER_STEP * LANES
    return [pl.BlockSpec((Q_BLOCK, hw), lambda g, i: (i, g)),
            pl.BlockSpec((lp, hw), lambda g, i: (0, nh // FOX_HEADS_PER_STEP + g)),
            pl.BlockSpec((lp, hw), lambda g, i: (0, 6 * nh // FOX_HEADS_PER_STEP + g)),
            pl.BlockSpec((Q_BLOCK, LANES), lambda g, i: (i, 0)),
            pl.BlockSpec((LANES, lp), lambda g, i: (0, 0))]


def _fox_fwd(qkn, proj, gates, gtf, nh):
    lp = qkn.shape[0]

    def body(q_ref, k_ref, v_ref, g_ref, gt_ref, o_ref, lse_ref):
        g, i = pl.program_id(0), pl.program_id(1)
        for j in range(lp // Q_BLOCK):
            @pl.when(i == j)
            def _(j=j):
                kl = (j + 1) * Q_BLOCK
                for hh in range(FOX_HEADS_PER_STEP):
                    h = FOX_HEADS_PER_STEP * g + hh
                    sl = slice(hh * LANES, (hh + 1) * LANES)
                    p, lse = _fox_probs(q_ref[:, sl], k_ref[0:kl, sl], g_ref[...], gt_ref[pl.ds(4 * nh + h, 1), :][:, 0:kl],
                                        h, j, nh)
                    o_ref[:, sl] = _bdot(p, v_ref[0:kl, sl])
                    lse_ref[:, sl] = jnp.broadcast_to(lse, (Q_BLOCK, LANES))

    blk = pl.BlockSpec((Q_BLOCK, FOX_HEADS_PER_STEP * LANES), lambda g, i: (i, g))
    return pl.pallas_call(
        body, grid=(nh // FOX_HEADS_PER_STEP, lp // Q_BLOCK), in_specs=_fox_specs(lp, nh), out_specs=[blk, blk],
        out_shape=[jax.ShapeDtypeStruct((lp, nh * HEAD_DIM), F32)] * 2, name="fox_fwd",
        compiler_params=_cp("parallel", "parallel"))(qkn, qkn, proj, gates, gtf)


def _fox_bwd(qkn, proj, gates, gtf, lse, do, dproj, nh):
    lp = qkn.shape[0]
    nq = lp // Q_BLOCK
    w = nh * HEAD_DIM
    scale = HEAD_DIM ** -0.5

    def body(q_ref, k_ref, v_ref, g_ref, gt_ref, lse_ref, do_ref, _, dq_ref, dk_ref, dc_ref, dv_ref, dv_scr):
        g, i = pl.program_id(0), pl.program_id(1)

        @pl.when(i == 0)
        def _():
            dk_ref[...] = jnp.zeros_like(dk_ref)
            dv_scr[...] = jnp.zeros_like(dv_scr)
            dc_ref[...] = jnp.zeros_like(dc_ref)
        for j in range(nq):
            @pl.when(i == j)
            def _(j=j):
                kl = (j + 1) * Q_BLOCK
                for hh in range(FOX_HEADS_PER_STEP):
                    h = FOX_HEADS_PER_STEP * g + hh
                    sl = slice(hh * LANES, (hh + 1) * LANES)
                    q, k = q_ref[:, sl], k_ref[0:kl, sl]
                    p = _fox_probs(q, k, g_ref[...], gt_ref[pl.ds(4 * nh + h, 1), :][:, 0:kl], h, j, nh,
                                   lse_ref[:, sl][:, 0:1])
                    dout = do_ref[:, sl]
                    dp = _bdot(dout, v_ref[0:kl, sl], "nt")
                    ds = p * (dp - jnp.sum(p * dp, axis=1, keepdims=True))
                    dq_ref[:, sl] = _bdot(ds, k) * scale
                    dk_ref[0:kl, sl] += _bdot(ds, q * scale, "tn")
                    dv_scr[0:kl, sl] += _bdot(p, dout, "tn")
                    dc_ref[hh, :, 0:kl] -= jnp.sum(ds, axis=0, keepdims=True)

        @pl.when(i == nq - 1)
        def _():
            dv_ref[...] = dv_scr[...].astype(BF16)

    hw = FOX_HEADS_PER_STEP * LANES
    blk = pl.BlockSpec((Q_BLOCK, hw), lambda g, i: (i, g))
    col = pl.BlockSpec((lp, hw), lambda g, i: (0, g))
    return pl.pallas_call(
        body, grid=(nh // FOX_HEADS_PER_STEP, nq), in_specs=_fox_specs(lp, nh) + [blk, blk, _ANY],
        out_specs=[blk, col, pl.BlockSpec((FOX_HEADS_PER_STEP, 1, lp), lambda g, i: (g, 0, 0)),
                   pl.BlockSpec((lp, hw), lambda g, i: (0, 6 * nh // FOX_HEADS_PER_STEP + g))],
        out_shape=[jax.ShapeDtypeStruct((lp, w), F32)] * 2 + [jax.ShapeDtypeStruct((nh, 1, lp), F32),
                                                             jax.ShapeDtypeStruct(dproj.shape, BF16)],
        scratch_shapes=[pltpu.VMEM((lp, hw), F32)], input_output_aliases={7: 3},
        name="fox_bwd", compiler_params=_cp("parallel", "arbitrary"))(qkn, qkn, proj, gates, gtf, lse, do, dproj)


def _merge_fox(o_fox, proj, merged, nh):
    lp = o_fox.shape[0]

    def body(o_ref, z_ref, _, m_ref):
        z = z_ref[...]
        m_ref[...] = (o_ref[...] * (z * _sigmoid(z))).astype(BF16)

    return pl.pallas_call(
        body, grid=(nh,),
        in_specs=[pl.BlockSpec((lp, LANES), lambda s: (0, s)), pl.BlockSpec((lp, LANES), lambda s: (0, 7 * nh + s)), _ANY],
        out_specs=pl.BlockSpec((lp, LANES), lambda s: (0, nh + s)),
        out_shape=jax.ShapeDtypeStruct(merged.shape, BF16), input_output_aliases={2: 0}, name="merge_fox",
        compiler_params=_cp("parallel"))(o_fox, proj, merged)


def _merge_fox_bwd(o_fox, proj, dmerged, dproj, nh):
    lp = o_fox.shape[0]

    def body(o_ref, z_ref, dm_ref, _, do_ref, dz_ref):
        silu, dsilu = _silu_and_grad(z_ref[...])
        dm = dm_ref[...]
        do_ref[...] = dm * silu
        dz_ref[...] = (dm * o_ref[...] * dsilu).astype(BF16)

    w = nh * HEAD_DIM
    return pl.pallas_call(
        body, grid=(nh,),
        in_specs=[pl.BlockSpec((lp, LANES), lambda s: (0, s)), pl.BlockSpec((lp, LANES), lambda s: (0, 7 * nh + s)),
                  pl.BlockSpec((lp, LANES), lambda s: (0, nh + s)), _ANY],
        out_specs=[pl.BlockSpec((lp, LANES), lambda s: (0, s)), pl.BlockSpec((lp, LANES), lambda s: (0, 7 * nh + s))],
        out_shape=[jax.ShapeDtypeStruct((lp, w), F32), jax.ShapeDtypeStruct(dproj.shape, BF16)],
        input_output_aliases={3: 1}, name="merge_fox_bwd", compiler_params=_cp("parallel"))(o_fox, proj, dmerged, dproj)


def _post(out, x, target, post_w):
    lp, d = out.shape

    def body(o_ref, x_ref, t_ref, w_ref, dy_ref, do_ref, loss_ref, dw_ref):
        i = pl.program_id(0)

        @pl.when(i == 0)
        def _():
            loss_ref[...] = jnp.zeros_like(loss_ref)
            dw_ref[...] = jnp.zeros_like(dw_ref)
        o = o_ref[...]
        r = _rms(o)
        nrm = o * r
        err = jnp.where(i > 0, x_ref[...] + nrm * w_ref[...] - t_ref[...], 0.0)
        loss_ref[0:1, :] += 0.5 * jnp.sum(jnp.sum(err * err, axis=1, keepdims=True), axis=0, keepdims=True) / d
        dy = err / d
        dy_ref[...] = dy
        dw_ref[...] += jnp.sum(dy * nrm, axis=0, keepdims=True)
        dyw = dy * w_ref[...]
        do_ref[...] = (r * (dyw - nrm * jnp.mean(dyw * nrm, axis=-1, keepdims=True))).astype(BF16)

    row = pl.BlockSpec((Q_BLOCK, d), lambda i: (i, 0))
    vec = pl.BlockSpec((1, d), lambda i: (0, 0))
    return pl.pallas_call(
        body, grid=(lp // Q_BLOCK,), in_specs=[row, _x_rows(d), _x_rows(d), vec],
        out_specs=[_x_rows(d), row, pl.BlockSpec((8, LANES), lambda i: (0, 0)), vec],
        out_shape=[jax.ShapeDtypeStruct(x.shape, F32), jax.ShapeDtypeStruct((lp, d), BF16),
                   jax.ShapeDtypeStruct((8, LANES), F32), jax.ShapeDtypeStruct((1, d), F32)],
        name="post", compiler_params=_cp("arbitrary"))(out, x, target, post_w)


def _prenorm_bwd(dxn, x, meta, w, dy, rider=None):
    seq, d = x.shape
    lp = seq + Q_BLOCK

    def body(start, finish, dx_ref, x_ref, m_ref, w_ref, dy_ref, gx_ref, gm_ref, dw_ref):
        i = pl.program_id(0)
        pl.when(i == 0)(start)
        h = _h_tile(i, x_ref, m_ref)
        r = _rms(h)
        xh = h * r
        dxn_ = dx_ref[...]
        dxw = dxn_ * w_ref[...]
        dh = jnp.where(i > 0, dy_ref[...], 0.0) + r * (dxw - xh * jnp.mean(dxw * xh, axis=-1, keepdims=True))
        gx_ref[...] = dh

        @pl.when(i == 0)
        def _():
            dw_ref[...] = jnp.zeros_like(dw_ref)
            gm_ref[...] = dh[PAD_ROWS:, :]
        dw_ref[...] += jnp.sum(dxn_ * xh, axis=0, keepdims=True)
        pl.when(i == lp // Q_BLOCK - 1)(finish)

    vec = pl.BlockSpec((1, d), lambda i: (0, 0))
    met = pl.BlockSpec((N_META, d), lambda i: (0, 0))
    outs, got = _hosted_call(
        body, [rider], 5, 3, 0, grid=(lp // Q_BLOCK,),
        in_specs=[pl.BlockSpec((Q_BLOCK, d), lambda i: (i, 0)), _x_rows(d), met, vec, _x_rows(d)],
        out_specs=[_x_rows(d), met, vec],
        out_shape=[jax.ShapeDtypeStruct((seq, d), F32), jax.ShapeDtypeStruct((N_META, d), F32),
                   jax.ShapeDtypeStruct((1, d), F32)],
        name="prenorm_bwd", compiler_params=_cp("arbitrary"))(dxn, x, meta, w, dy)
    return outs, (got[0] if got else None)


def _layer_grads(x, target, meta, pre_w, wfull, conv_wt, a_log, dt_bias, gdn_norm_w, fq_w, fk_w, f_bias, w_out, post_w,
                 w_out_grads=None):
    nh = a_log.shape[1]
    zpad = jnp.zeros((1, LANES - 3 * nh), F32)
    bias_row = jnp.concatenate([jnp.zeros((1, nh), F32), dt_bias, f_bias, zpad], axis=1)
    nega_row = jnp.concatenate([jnp.zeros((1, nh), F32), -jnp.exp(a_log), jnp.zeros((1, nh), F32), zpad], axis=1)
    qk_w = jnp.stack([fq_w, fk_w])

    if isinstance(wfull, tuple):
        rider, meta_at, project = wfull
        xn, got = _prenorm_gathering(x, pre_w, rider, meta_at)
        proj, wfull, conv_wt, meta, w_out = project(xn, got)
    else:
        xn = _prenorm(x, meta, pre_w)
        proj = _matmul(xn, wfull, "nn", MM_TILE, F32, "proj")
    qkv = _gdn_prep(proj, conv_wt, nh)
    gates, gt3, gtf = _gates(proj, bias_row, nega_row, nh)
    (o_gdn, s_all, t_all), _ = _gdn_fwd(qkv, gates, gt3, nh, None)
    qkn = _fox_prep(proj, qk_w, nh)
    o_fox, fox_lse = _fox_fwd(qkn, proj, gates, gtf, nh)
    merged = _merge_fox(o_fox, proj, _merge_gdn(o_gdn, proj, gdn_norm_w, nh), nh)
    out = _matmul(merged, w_out, "nn", 4 * LANES, F32, "out_proj")
    dy, dout, loss_blk, dpost_w = _post(out, x, target, post_w)

    dw_out = _matmul(merged, dout, "tn", 4 * LANES, BF16, "dw_out")
    if w_out_grads is None:
        dmerged, gdn_rider = _matmul(dout, w_out, "nt", 4 * LANES, F32, "dmerged"), None
    else:
        dmerged, got = _matmul(dout, w_out, "nt", 4 * LANES, F32, "dmerged", w_out_grads[0](dw_out))
        gdn_rider = w_out_grads[1](dw_out, got)
    do_gdn, dproj, dgdn_norm_w = _merge_gdn_bwd(o_gdn, proj, gdn_norm_w, dmerged, nh)
    do_fox, dproj = _merge_fox_bwd(o_fox, proj, dmerged, dproj, nh)
    dqn, dkn, dc_t, dproj = _fox_bwd(qkn, proj, gates, gtf, fox_lse, do_fox, dproj, nh)
    dproj, dqk_w = _fox_prep_bwd(proj, qk_w, dqn, dkn, dproj, nh)
    (dgq, dgk, dgv, dgate), w_out_parts = _gdn_bwd(qkv, gates, gt3, s_all, t_all, do_gdn, nh, gdn_rider)
    dproj, dconv_wt = _gdn_prep_bwd(proj, conv_wt, dgq, dgk, dgv, dproj, nh)
    dc_rows = jnp.pad(dc_t.reshape(nh, -1), ((2 * nh, LANES - 3 * nh), (0, 0)))
    dproj, gate_sums = _gates_bwd(proj, bias_row, nega_row, gates, dgate, dc_rows, dproj, nh)
    return dict(
        loss=loss_blk[0:1, 0:1], dy=dy, xn=xn, dproj=dproj, post_w=dpost_w,
        conv_wt=dconv_wt, a_log=gate_sums[1:2, nh:2 * nh], dt_bias=gate_sums[0:1, nh:2 * nh],
        gdn_norm_w=dgdn_norm_w, fq_w=dqk_w[0], fk_w=dqk_w[1], f_bias=gate_sums[0:1, 2 * nh:3 * nh], w_out=dw_out,
        w_out_parts=w_out_parts, wfull=wfull, meta=meta)


def _cast_bf16(a, tr, name):
    r, c = a.shape

    def body(a_ref, o_ref):
        o_ref[...] = a_ref[...].astype(BF16)

    return pl.pallas_call(
        body, grid=(r // tr,), in_specs=[pl.BlockSpec((tr, c), lambda i: (i, 0))],
        out_specs=pl.BlockSpec((tr, c), lambda i: (i, 0)), out_shape=jax.ShapeDtypeStruct((r, c), BF16),
        name=name, compiler_params=_cp("parallel"))(a)


def _column_major(a):
    return jnp.transpose(a, (2, 0, 1))


def _cast_bf16_column_major(a3, pieces, name):
    _, r, c = a3.shape
    rows = r // pieces

    def body(a_ref, *o_refs):
        t = a_ref[...].reshape(LANES, r).T.astype(BF16)
        for k, o_ref in enumerate(o_refs):
            o_ref[...] = t[k * rows:(k + 1) * rows]

    return pl.pallas_call(
        body, grid=(pl.cdiv(c, LANES),), in_specs=[pl.BlockSpec((LANES, 1, r), lambda i: (i, 0, 0))],
        out_specs=[pl.BlockSpec((rows, LANES), lambda i: (0, i))] * pieces,
        out_shape=[jax.ShapeDtypeStruct((rows, c), BF16)] * pieces, name=name, compiler_params=_cp("parallel"))(_column_major(a3))


def _adamw_column_major(w3, parts, m3, v3, name):
    _, r, c = w3.shape
    n_parts = parts[0].shape[0]

    def body(w_ref, *refs):
        p_refs, (m_ref, v_ref, g_ref, d_ref, nm_ref, nv_ref) = refs[:len(parts)], refs[len(parts):]
        sums = []
        for p_ref in p_refs:
            g = p_ref[0].astype(F32)
            for s in range(1, n_parts):
                g = g + p_ref[s].astype(F32)
            sums.append(g)
        g = jnp.concatenate(sums, axis=0).T
        flat = lambda ref: ref[...].reshape(LANES, r)
        m_new = ADAM_B1 * flat(m_ref) + (1.0 - ADAM_B1) * g
        v_new = ADAM_B2 * flat(v_ref) + (1.0 - ADAM_B2) * (g * g)
        m_hat = m_new / (1.0 - ADAM_B1 ** ADAM_STEP)
        v_hat = v_new / (1.0 - ADAM_B2 ** ADAM_STEP)
        delta = -ADAM_LR * (m_hat / (jnp.sqrt(v_hat) + ADAM_EPS) + ADAM_WD * flat(w_ref))
        for ref, val in ((g_ref, g), (d_ref, delta), (nm_ref, m_new), (nv_ref, v_new)):
            ref[...] = val.reshape(LANES, 1, r)

    blk = pl.BlockSpec((LANES, 1, r), lambda i: (i, 0, 0))
    outs = pl.pallas_call(
        body, grid=(pl.cdiv(c, LANES),),
        in_specs=[blk] + [pl.BlockSpec((n_parts, p.shape[1], LANES), lambda i: (0, 0, i)) for p in parts] + [blk, blk],
        out_specs=[blk] * 4, out_shape=[jax.ShapeDtypeStruct((c, 1, r), F32)] * 4, name=name,
        compiler_params=_cp("parallel"))(_column_major(w3), *parts, _column_major(m3), _column_major(v3))
    return [jnp.transpose(o, (1, 2, 0)) for o in outs]


def _gather_copies(ins, outs, send_sems, recv_sems, local_sems):
    n = len(ins)
    x, y, c = lax.axis_index("x"), lax.axis_index("y"), lax.axis_index("c")
    me, sibling = (x, y, c), (x, y, 1 - c)
    xn, yn, dg = (1 - x, y), (x, 1 - y), (1 - x, 1 - y)

    def copy(a, k, block, to, src=None):
        px, py, pc = block
        rows = outs[a].at[4 * px + 2 * py + pc]
        return pltpu.make_async_remote_copy(
            src_ref=rows if src is None else src, dst_ref=rows, send_sem=send_sems.at[a, k],
            recv_sem=recv_sems.at[a, k], device_id=to, device_id_type=_MESH)

    local = [pltpu.make_async_copy(ins[a], outs[a].at[4 * x + 2 * y + c], local_sems.at[a]) for a in range(n)]
    own = [cp for a in range(n) for cp in (copy(a, 0, me, sibling, src=ins[a]), copy(a, 1, me, (*xn, c), src=ins[a]),
                                           copy(a, 2, me, (*yn, c), src=ins[a]))]

    def start():
        for cp in local:
            cp.start(LOCAL_DMA_PRIORITY)
        for cp in own:
            cp.start()

    def middle():
        for a in range(n):
            @pl.when(c == 1)
            def _(a=a):
                copy(a, 1, (*xn, c), me).wait_recv()
                copy(a, 3, (*xn, c), (*yn, c)).start()

            @pl.when(c == 0)
            def _(a=a):
                copy(a, 2, (*yn, c), me).wait_recv()
                copy(a, 3, (*yn, c), (*xn, c)).start()
        for a in range(n):
            pl.when(c == 0)(copy(a, 1, (*xn, c), me).wait_recv)
            copy(a, 4, (*xn, c), sibling).start()
            pl.when(c == 1)(copy(a, 2, (*yn, c), me).wait_recv)
            copy(a, 5, (*yn, c), sibling).start()

    def finish():
        for a in range(n):
            copy(a, 3, (*dg, c), me).wait_recv()
            copy(a, 6, (*dg, c), sibling).start()
        for a in range(n):
            copy(a, 0, sibling, me).wait_recv()
            for k, chip in ((4, xn), (5, yn), (6, dg)):
                copy(a, k, (*chip, 1 - c), me).wait_recv()
                copy(a, k, (*chip, c), sibling).wait_send()
            copy(a, 3, (*xn, c), (*yn, c)).wait_send()
        for cp in own:
            cp.wait_send()
        for cp in local:
            cp.wait()

    return start, middle, finish


def _gather_scratch(n):
    return [pltpu.SemaphoreType.DMA((n, N_DEV - 1)), pltpu.SemaphoreType.DMA((n, N_DEV - 1)), pltpu.SemaphoreType.DMA((n,))]


def _gather_rider(arrays):
    return _Rider(list(arrays), [jax.ShapeDtypeStruct((N_DEV,) + a.shape, a.dtype) for a in arrays],
                  _gather_scratch(len(arrays)), {}, lambda ins, outs, scratch: _gather_copies(ins, outs, *scratch))


def _all_gather(arrays, name):
    n = len(arrays)

    def body(*refs):
        start, middle, finish = _gather_copies(refs[:n], refs[n:2 * n], *refs[2 * n:])
        start()
        middle()
        finish()

    return pl.pallas_call(
        body, in_specs=[_ANY] * n, out_specs=[_ANY] * n,
        out_shape=[jax.ShapeDtypeStruct((N_DEV,) + a.shape, a.dtype) for a in arrays],
        scratch_shapes=_gather_scratch(n), name=name)(*arrays)


SLAB = 10 * LANES


def _slab_start(blk, nh, cols):
    in_second_half = blk >= N_DEV // 2
    shift = (2 * nh if in_second_half else 0) if isinstance(blk, int) else jnp.where(in_second_half, 2 * nh, 0)
    return (blk * cols - shift) // LANES * LANES


def _pair_rider(dw_rows=None, parts=None, nh=None, after=None):
    if dw_rows is not None:
        r, full = dw_rows.shape
        cols = (full - NARROW + 3 * nh) // N_DEV
        out_shapes = [jax.ShapeDtypeStruct((N_CHIP, r, SLAB), dw_rows.dtype), jax.ShapeDtypeStruct((r, NARROW), dw_rows.dtype)]
    else:
        out_shapes = [jax.ShapeDtypeStruct((N_CHIP,) + parts.shape[1:], parts.dtype)]

    def make(ins, outs, scratch):
        send_sems, recv_sems = scratch
        x, y, c = lax.axis_index("x"), lax.axis_index("y"), lax.axis_index("c")
        kw = lambda k: dict(send_sem=send_sems.at[k], recv_sem=recv_sems.at[k], device_id=(x, y, 1 - c), device_id_type=_MESH)
        copies = []
        for q in range(N_CHIP):
            if dw_rows is not None:
                first = pl.multiple_of(_slab_start(2 * q + 1 - c, nh, cols), LANES)
                copies.append(pltpu.make_async_remote_copy(src_ref=ins[0].at[:, pl.ds(first, SLAB)], dst_ref=outs[0].at[q], **kw(q)))
            else:
                copies.append(pltpu.make_async_remote_copy(src_ref=ins[0].at[2 * q + 1 - c], dst_ref=outs[0].at[q], **kw(q)))
        if dw_rows is not None:
            copies.append(pltpu.make_async_remote_copy(src_ref=ins[0].at[:, pl.ds(full - NARROW, NARROW)], dst_ref=outs[1],
                                                       **kw(N_CHIP)))

        def start():
            for cp in copies:
                cp.start()

        def finish():
            for cp in copies:
                cp.wait()

        return start, finish

    return _Rider([dw_rows if dw_rows is not None else parts] + ([] if after is None else [after]), out_shapes,
                  [pltpu.SemaphoreType.DMA((N_CHIP + 1,)), pltpu.SemaphoreType.DMA((N_CHIP + 1,))], {}, make)


def _relayout_pair_sum(dwfull, got_slabs, got_tail, core, nh, tr, name):
    d, full = dwfull.shape
    w = nh * HEAD_DIM
    cols = (8 * w + 3 * nh) // N_DEV
    segs = _native_segments(nh)

    def block(f_ref, s_ref, t_ref, q, blk):
        st = _slab_start(blk, nh, cols)
        wide = f_ref[:, st:st + SLAB].astype(F32) + s_ref[q].astype(F32)
        tail = f_ref[:, 8 * w:].astype(F32) + t_ref[...].astype(F32)
        pieces = []
        for s0, s1, t0 in segs:
            lo, hi = max(s0, blk * cols), min(s1, (blk + 1) * cols)
            if lo < hi:
                at = t0 + lo - s0
                pieces.append(tail[:, at - 8 * w:at - 8 * w + hi - lo] if at >= 8 * w else wide[:, at - st:at - st + hi - lo])
        return (pieces[0] if len(pieces) == 1 else jnp.concatenate(pieces, axis=1)).astype(dwfull.dtype)

    def body(core_ref, f_ref, s_ref, t_ref, o_ref):
        for parity in range(2):
            @pl.when(core_ref[0] == parity)
            def _(parity=parity):
                for q in range(N_CHIP):
                    o_ref[q] = block(f_ref, s_ref, t_ref, q, 2 * q + parity)

    return pl.pallas_call(
        body,
        grid_spec=pltpu.PrefetchScalarGridSpec(
            num_scalar_prefetch=1, grid=(d // tr,),
            in_specs=[pl.BlockSpec((tr, full), lambda i, c_ref: (i, 0)), pl.BlockSpec((N_CHIP, tr, SLAB), lambda i, c_ref: (0, i, 0)),
                      pl.BlockSpec((tr, NARROW), lambda i, c_ref: (i, 0))],
            out_specs=pl.BlockSpec((N_CHIP, tr, cols), lambda i, c_ref: (0, i, 0))),
        out_shape=jax.ShapeDtypeStruct((N_CHIP, d, cols), dwfull.dtype), name=name,
        compiler_params=_cp("parallel"))(core, dwfull, got_slabs, got_tail)


def _pair_sum(parts, got, core, tr, name):
    _, r, c = parts.shape

    def body(core_ref, p_ref, g_ref, o_ref):
        o_ref[...] = (p_ref[...].astype(F32) + g_ref[...].astype(F32)).astype(o_ref.dtype)

    return pl.pallas_call(
        body,
        grid_spec=pltpu.PrefetchScalarGridSpec(
            num_scalar_prefetch=1, grid=(N_CHIP, r // tr),
            in_specs=[pl.BlockSpec((1, tr, c), lambda q, i, core_ref: (2 * q + core_ref[0], i, 0)),
                      pl.BlockSpec((1, tr, c), lambda q, i, core_ref: (q, i, 0))],
            out_specs=pl.BlockSpec((1, tr, c), lambda q, i, core_ref: (q, i, 0))),
        out_shape=jax.ShapeDtypeStruct((N_CHIP, r, c), parts.dtype), name=name,
        compiler_params=_cp("parallel", "parallel"))(core, parts, got)


def _native_segments(nh):
    w = nh * HEAD_DIM
    return [(0, 4 * w, 0), (4 * w, 4 * w + 2 * nh, 8 * w), (4 * w + 2 * nh, 8 * w + 2 * nh, 4 * w),
            (8 * w + 2 * nh, 8 * w + 3 * nh, 8 * w + 2 * nh)]


def _relayout_w_in(wg, nh, tr, name, rows_total=None, into=None):
    _, d, cols = wg.shape
    w = nh * HEAD_DIM
    rows_total = into.shape[0] if into is not None else rows_total or d
    first = (rows_total - d) // tr if into is not None else 0

    def native(ref, j0, j1):
        out = []
        while j0 < j1:
            blk = j0 // cols
            end = min(j1, (blk + 1) * cols)
            out.append(ref[blk, :, pl.ds(j0 - blk * cols, end - j0)])
            j0 = end
        return out

    def body(g_ref, *refs):
        o_ref = refs[-1]
        for cidx in range(8 * w // LANES):
            j0 = cidx * LANES + (0 if cidx * LANES < 4 * w else 2 * nh)
            pieces = native(g_ref, j0, j0 + LANES)
            o_ref[:, cidx * LANES:(cidx + 1) * LANES] = pieces[0] if len(pieces) == 1 else jnp.concatenate(pieces, axis=1)
        pieces = (native(g_ref, 4 * w, 4 * w + 2 * nh) + native(g_ref, 8 * w + 2 * nh, 8 * w + 3 * nh)
                  + [jnp.zeros((tr, NARROW - 3 * nh), wg.dtype)])
        o_ref[:, 8 * w:] = jnp.concatenate(pieces, axis=1)

    return pl.pallas_call(
        body, grid=(d // tr,), in_specs=[pl.BlockSpec((N_DEV, tr, cols), lambda i: (0, i, 0))] + [_ANY] * (into is not None),
        out_specs=pl.BlockSpec((tr, 8 * w + NARROW), lambda i: (first + i, 0)),
        out_shape=jax.ShapeDtypeStruct((rows_total, 8 * w + NARROW), wg.dtype),
        input_output_aliases={1: 0} if into is not None else {},
        name=name, compiler_params=_cp("parallel"))(wg, *([into] if into is not None else []))


def _adamw(w, parts, m, v, tr, name, after=None):
    r, c = w.shape
    n_parts = parts.shape[0]

    def body(w_ref, p_ref, m_ref, v_ref, *refs):
        g_ref, d_ref, nm_ref, nv_ref = refs[-5:-1] if after is not None else refs
        if after is not None:
            refs[-1][...] = jnp.zeros_like(refs[-1])
        g = p_ref[0].astype(F32)
        for s in range(1, n_parts):
            g = g + p_ref[s].astype(F32)
        m_new = ADAM_B1 * m_ref[...] + (1.0 - ADAM_B1) * g
        v_new = ADAM_B2 * v_ref[...] + (1.0 - ADAM_B2) * (g * g)
        m_hat = m_new / (1.0 - ADAM_B1 ** ADAM_STEP)
        v_hat = v_new / (1.0 - ADAM_B2 ** ADAM_STEP)
        g_ref[...] = g
        d_ref[...] = -ADAM_LR * (m_hat / (jnp.sqrt(v_hat) + ADAM_EPS) + ADAM_WD * w_ref[...])
        nm_ref[...] = m_new
        nv_ref[...] = v_new

    blk = pl.BlockSpec((tr, c), lambda i: (i, 0))
    extra = after is not None
    return pl.pallas_call(
        body, grid=(r // tr,),
        in_specs=[blk, pl.BlockSpec((n_parts, tr, c), lambda i: (0, i, 0)), blk, blk] + [_ANY] * extra,
        out_specs=[blk] * 4 + [pl.BlockSpec((8, LANES), lambda i: (0, 0))] * extra,
        out_shape=[jax.ShapeDtypeStruct((r, c), F32)] * 4 + [jax.ShapeDtypeStruct((8, LANES), F32)] * extra, name=name,
        compiler_params=_cp("arbitrary" if extra else "parallel"))(w, parts, m, v, *([after] if extra else []))


def _adamw_conv(w, gathered, dev, m, v, name):
    c, r = w.shape
    n_parts = gathered.shape[0]

    def body(dev_ref, w_ref, p_ref, m_ref, v_ref, g_ref, d_ref, nm_ref, nv_ref):
        g = p_ref[0].astype(F32)
        for s in range(1, n_parts):
            g = g + p_ref[s].astype(F32)
        m_new = ADAM_B1 * m_ref[...] + (1.0 - ADAM_B1) * g
        v_new = ADAM_B2 * v_ref[...] + (1.0 - ADAM_B2) * (g * g)
        m_hat = m_new / (1.0 - ADAM_B1 ** ADAM_STEP)
        v_hat = v_new / (1.0 - ADAM_B2 ** ADAM_STEP)
        g_ref[...] = g
        d_ref[...] = -ADAM_LR * (m_hat / (jnp.sqrt(v_hat) + ADAM_EPS) + ADAM_WD * w_ref[...])
        nm_ref[...] = m_new
        nv_ref[...] = v_new

    blk = pl.BlockSpec((c, r), lambda i, dev_ref: (0, 0))
    return pl.pallas_call(
        body,
        grid_spec=pltpu.PrefetchScalarGridSpec(
            num_scalar_prefetch=1, grid=(1,),
            in_specs=[blk, pl.BlockSpec((n_parts, c, r), lambda i, dev_ref: (0, 0, dev_ref[0])), blk, blk], out_specs=[blk] * 4),
        out_shape=[jax.ShapeDtypeStruct((c, r), F32)] * 4, name=name, compiler_params=_cp("arbitrary"))(dev, w, gathered, m, v)


def _pack_small(d, pre, post, a_log, dt_bias, f_bias, gdn_w, fq_w, fk_w, extra):
    row2 = jnp.concatenate([a_log, dt_bias, f_bias, gdn_w, fq_w, fk_w, extra], axis=1)
    row2 = jnp.pad(row2, ((0, 0), (0, d - row2.shape[1])))
    return jnp.concatenate([pre, post, row2, jnp.zeros((5, d), F32)], axis=0)


def _adamw_small(w, parts, m, v, nh, name):
    d = w.shape[1]
    n_parts = parts.shape[0]
    shapes = dict(pre=d, post=d, a_log=nh, dt_bias=nh, f_bias=nh, gdn_w=HEAD_DIM, fq_w=HEAD_DIM, fk_w=HEAD_DIM, extra=1)

    def body(w_ref, p_ref, m_ref, v_ref, *o_refs):
        g = p_ref[0]
        for s in range(1, n_parts):
            g = g + p_ref[s]
        m_new = ADAM_B1 * m_ref[...] + (1.0 - ADAM_B1) * g
        v_new = ADAM_B2 * v_ref[...] + (1.0 - ADAM_B2) * (g * g)
        m_hat = m_new / (1.0 - ADAM_B1 ** ADAM_STEP)
        v_hat = v_new / (1.0 - ADAM_B2 ** ADAM_STEP)
        delta = -ADAM_LR * (m_hat / (jnp.sqrt(v_hat) + ADAM_EPS) + ADAM_WD * w_ref[...])
        for k, val in enumerate((g, delta, m_new, v_new)):
            vectors = _unpack_small(val, nh)
            for j, key in enumerate(shapes):
                o_refs[k * len(shapes) + j][...] = vectors[key]

    full = lambda shape: pl.BlockSpec(shape, lambda i: (0,) * len(shape))
    outs = pl.pallas_call(
        body, grid=(1,), in_specs=[full(w.shape), full(parts.shape), full(w.shape), full(w.shape)],
        out_specs=[full((1, n)) for n in shapes.values()] * 4,
        out_shape=[jax.ShapeDtypeStruct((1, n), F32) for n in shapes.values()] * 4, name=name,
        compiler_params=_cp("arbitrary"))(w, parts, m, v)
    return [dict(zip(shapes, outs[k * len(shapes):(k + 1) * len(shapes)])) for k in range(4)]


def _unpack_small(p, nh):
    o = 3 * nh
    return dict(pre=p[0:1], post=p[1:2], a_log=p[2:3, 0:nh], dt_bias=p[2:3, nh:2 * nh], f_bias=p[2:3, 2 * nh:o],
                gdn_w=p[2:3, o:o + HEAD_DIM], fq_w=p[2:3, o + HEAD_DIM:o + 2 * HEAD_DIM],
                fk_w=p[2:3, o + 2 * HEAD_DIM:o + 3 * HEAD_DIM], extra=p[2:3, o + 3 * HEAD_DIM:o + 3 * HEAD_DIM + 1])


def kernel(x, meta_tokens, pre_norm_w, w_in, conv_w, a_log, dt_bias, gdn_norm_w, fox_q_norm_w, fox_k_norm_w, fox_f_bias, w_out, post_norm_w, loss_target, m_meta_tokens, m_pre_norm_w, m_w_in, m_conv_w, m_a_log, m_dt_bias, m_gdn_norm_w, m_fox_q_norm_w, m_fox_k_norm_w, m_fox_f_bias, m_w_out, m_post_norm_w, v_meta_tokens, v_pre_norm_w, v_w_in, v_conv_w, v_a_log, v_dt_bias, v_gdn_norm_w, v_fox_q_norm_w, v_fox_k_norm_w, v_fox_f_bias, v_w_out, v_post_norm_w):
    nh = a_log.shape[1]
    d = x.shape[-1]
    w = nh * HEAD_DIM
    zero = jnp.zeros((1, 1), F32)

    w_in_a, w_in_b = _cast_bf16_column_major(w_in, 2, "cast_w_in")

    def project(xn, got):
        wg, cg, mg = got
        half = (d // 2, 0), (d // 2, 1)
        wfull = _relayout_w_in(wg, nh, 256, "relayout_w_in_a", rows_total=d)
        proj, got = _matmul(xn, wfull, "nn", MM_TILE, F32, "proj_a", _gather_rider([w_in_b]), a_cols=half[0], b_rows=half[0])
        wfull = _relayout_w_in(got[0], nh, 256, "relayout_w_in_b", into=wfull)
        proj, got = _matmul(xn, wfull, "nn", MM_TILE, F32, "proj_b", _gather_rider([_cast_bf16(w_out[0], 256, "cast_w_out")]),
                            a_cols=half[1], b_rows=half[1], acc=proj)
        return (proj, wfull, cg.transpose(1, 0, 2).reshape(CONV_WIDTH, 3 * w), mg.transpose(1, 0, 2).reshape(N_META, d),
                got[0].reshape(2 * w, d))

    core = lax.axis_index("c")
    dev = 4 * lax.axis_index("x") + 2 * lax.axis_index("y") + core
    core_arr = jnp.reshape(core, (1,)).astype(jnp.int32)

    out_parts = lambda dw_out: dw_out.reshape(N_DEV, 2 * w // N_DEV, d)
    g = _layer_grads(
        x[0], loss_target[0], None, pre_norm_w, (_gather_rider([w_in_a, conv_w[0].T, meta_tokens]), 2, project), None,
        a_log, dt_bias, gdn_norm_w,
        fox_q_norm_w, fox_k_norm_w, fox_f_bias, None, post_norm_w,
        w_out_grads=(lambda dw_out: _pair_rider(parts=out_parts(dw_out)),
                     lambda dw_out, got: _chip_rider(_pair_sum(out_parts(dw_out), got[0], core_arr, 256, "pair_sum_w_out"))))
    p_out = g["w_out_parts"][0]
    xn, dproj, wfull, meta_full = g["xn"], g["dproj"], g["wfull"], g["meta"]
    flights, token, dw, rider = [], None, None, None

    def exchange(dw, got, i):
        sums = _relayout_pair_sum(dw, got[0], got[1], core_arr, nh, 128, f"relayout_pair_sum_{i}")
        flight, token = _chip_exchange_start(sums, f"chip_exchange_start_{i}")
        flights.append(flight)
        return token

    for i, (index, parts) in enumerate(DW_IN_PIECES):
        res = _matmul(xn, dproj, "tn", MM_TILE, BF16, f"dw_in_{i}", rider, a_cols=(d // parts, index))
        if i:
            token = exchange(dw, res[1], i - 1)
        dw = res[0] if i else res
        rider = _pair_rider(dw_rows=dw, nh=nh, after=token)
    dxn, (got,) = _dxn(dproj, wfull, [rider], MM_TILE, "dxn")
    token = exchange(dw, got, len(DW_IN_PIECES) - 1)
    *r_out, token = _adamw(w_out[0], p_out, m_w_out[0], v_w_out[0], 64, "adamw_w_out", after=token)
    (grad_x, dmeta, dpre_w), _ = _prenorm_bwd(dxn, x[0], meta_full, pre_norm_w + token[0:1, 0:1], g["dy"])
    small = _pack_small(d, dpre_w, g["post_w"], g["a_log"], g["dt_bias"], g["f_bias"], g["gdn_norm_w"], g["fq_w"],
                        g["fk_w"], g["loss"])
    a_conv, a_meta, p_small = _all_gather([g["conv_wt"], dmeta, small], "gather_small_grads")
    p_meta = lax.dynamic_slice_in_dim(a_meta, dev * meta_tokens.shape[1], meta_tokens.shape[1], axis=2)

    r_conv = _adamw_conv(conv_w[0].T, a_conv, jnp.reshape(dev, (1,)).astype(jnp.int32), m_conv_w[0].T, v_conv_w[0].T,
                         "adamw_conv_w")
    r_conv = [o.T for o in r_conv]
    r_meta = _adamw(meta_tokens, p_meta, m_meta_tokens, v_meta_tokens, N_META, "adamw_meta")
    pk = lambda pre, post, a, dt, gw, fq, fk, fb: _pack_small(d, pre, post, a, dt, fb, gw, fq, fk, zero)
    sm = _adamw_small(
        pk(pre_norm_w, post_norm_w, a_log, dt_bias, gdn_norm_w, fox_q_norm_w, fox_k_norm_w, fox_f_bias), p_small,
        pk(m_pre_norm_w, m_post_norm_w, m_a_log, m_dt_bias, m_gdn_norm_w, m_fox_q_norm_w, m_fox_k_norm_w, m_fox_f_bias),
        pk(v_pre_norm_w, v_post_norm_w, v_a_log, v_dt_bias, v_gdn_norm_w, v_fox_q_norm_w, v_fox_k_norm_w, v_fox_f_bias),
        nh, "adamw_small")
    p_in = _chip_exchange_wait(flights, sm[0]["pre"], "chip_exchange_wait")
    r_in = _adamw_column_major(w_in, p_in, m_w_in, v_w_in, "adamw_w_in")

    outs = []
    for i in range(4):
        s = sm[i]
        outs += [r_meta[i], s["pre"], r_in[i], r_conv[i][None], s["a_log"], s["dt_bias"], s["gdn_w"], s["fq_w"],
                 s["fk_w"], s["f_bias"], r_out[i][None], s["post"]]
    return (sm[0]["extra"].reshape(()), grad_x[None], *outs)
```
